```python
import math
import jax, jax.numpy as jnp
from jax import lax
import numpy as np

D_MODEL = 1024
BATCH = 8
SEQ = 4096
DEPTH = 1

ATTN_HEAD_DIM = 64
ATTN_Q_HEADS = 16
ATTN_KV_HEADS = 2
ATTN_GROUP = ATTN_Q_HEADS // ATTN_KV_HEADS
ATTN_WIDTH = ATTN_Q_HEADS * ATTN_HEAD_DIM
ATTN_KV_WIDTH = ATTN_KV_HEADS * ATTN_HEAD_DIM
WINDOW = 128
ATTN_BLOCK = 128
HGRN_EXPAND = 128
HGRN_HEADS = D_MODEL // HGRN_EXPAND
HGRN_K = HGRN_EXPAND
HGRN_V = D_MODEL // HGRN_HEADS
HGRN_KEY_WIDTH = HGRN_HEADS * HGRN_K
HGRN_WIDTH = HGRN_HEADS * HGRN_V
CHUNK = 64
FFN_HIDDEN = -(-(8 * D_MODEL) // (3 * 256)) * 256
EPS = 1e-6
NEG_INF = -1e30
IN_SPLITS = (ATTN_WIDTH, ATTN_KV_WIDTH, ATTN_KV_WIDTH,
             HGRN_KEY_WIDTH, HGRN_KEY_WIDTH, HGRN_WIDTH, HGRN_WIDTH,
             D_MODEL, D_MODEL)
IN_WIDTH = sum(IN_SPLITS)

kernel_name = "hybrid_swa_sink_hgrn2_gated_block"


def rmsnorm(x, g):
    xf = x.astype(jnp.float32)
    y = xf * lax.rsqrt(jnp.mean(xf * xf, axis=-1, keepdims=True) + EPS)
    return (y * g.astype(jnp.float32)).astype(x.dtype)


def split_columns(p):
    idx = np.cumsum(np.array(IN_SPLITS))[:-1].tolist()
    return jnp.split(p, idx, axis=-1)


def sliding_window_sink_attention(q, k, v, sinks):
    B, S = q.shape[0], q.shape[1]
    nb = S // ATTN_BLOCK
    qb = q.reshape(B, nb, ATTN_BLOCK, ATTN_KV_HEADS, ATTN_GROUP, ATTN_HEAD_DIM)
    kb = k.reshape(B, nb, ATTN_BLOCK, ATTN_KV_HEADS, ATTN_HEAD_DIM)
    vb = v.reshape(B, nb, ATTN_BLOCK, ATTN_KV_HEADS, ATTN_HEAD_DIM)

    def with_prev(t):
        prev = jnp.concatenate([jnp.zeros_like(t[:, :1]), t[:, :-1]], axis=1)
        return jnp.concatenate([prev, t], axis=2)

    kw, vw = with_prev(kb), with_prev(vb)
    scale = 1.0 / math.sqrt(ATTN_HEAD_DIM)
    scores = jnp.einsum('bnqhgd,bnkhd->bnhgqk', qb, kw).astype(jnp.float32) * scale
    qi = jnp.arange(ATTN_BLOCK)[:, None]
    kj = jnp.arange(2 * ATTN_BLOCK)[None, :]
    rel = qi + ATTN_BLOCK - kj
    key_pos = jnp.arange(nb)[:, None, None] * ATTN_BLOCK - ATTN_BLOCK + kj[None]
    valid = (rel >= 0)[None] & (rel < WINDOW)[None] & (key_pos >= 0)
    scores = jnp.where(valid[None, :, None, None], scores, NEG_INF)
    sink = jnp.broadcast_to(
        sinks.astype(jnp.float32).reshape(1, 1, ATTN_KV_HEADS, ATTN_GROUP, 1, 1),
        scores.shape[:-1] + (1,))
    probs = jax.nn.softmax(jnp.concatenate([scores, sink], axis=-1), axis=-1)[..., :-1]
    out = jnp.einsum('bnhgqk,bnkhd->bnqhgd', probs.astype(v.dtype), vw)
    return out.reshape(B, S, ATTN_WIDTH)


def hgrn2_chunkwise(q, k, v, log_f):
    B, S, H, K = q.shape
    V = v.shape[-1]
    n = S // CHUNK

    def to_chunks(t):
        return t.reshape(B, n, CHUNK, H, t.shape[-1]).transpose(1, 0, 3, 2, 4)

    causal = jnp.tril(jnp.ones((CHUNK, CHUNK), dtype=bool))

    def step(state, inp):
        qc, kc, vc, gc = inp
        b = jnp.cumsum(gc, axis=2)
        b_mid = b[:, :, CHUNK // 2 - 1:CHUNK // 2]
        b_last = b[:, :, -1:]
        a = jnp.einsum('bhck,bhsk->bhcs', qc * jnp.exp(b - b_mid), kc * jnp.exp(b_mid - b))
        a = jnp.where(causal, a, 0.0)
        o = (jnp.einsum('bhcs,bhsv->bhcv', a, vc)
             + jnp.einsum('bhck,bhkv->bhcv', qc * jnp.exp(b), state))
        state = (jnp.exp(b_last)[:, :, 0, :, None] * state
                 + jnp.einsum('bhsk,bhsv->bhkv', kc * jnp.exp(b_last - b), vc))
        return state, o

    s0 = jnp.zeros((B, H, K, V), jnp.float32)
    _, o = lax.scan(step, s0, (to_chunks(q), to_chunks(k), to_chunks(v), to_chunks(log_f)))
    return o.transpose(1, 0, 3, 2, 4).reshape(B, S, H, V)


def hgrn2_branch(hq, hf, hi, hg, lb, norm_g):
    B, S = hq.shape[0], hq.shape[1]
    fp = hf.astype(jnp.float32)
    log_f = jnp.log(lb + (1.0 - lb) * jax.nn.sigmoid(fp))
    k = (1.0 - lb) * jax.nn.sigmoid(-fp)
    q = jax.nn.silu(hq.astype(jnp.float32))
    shp_k = (B, S, HGRN_HEADS, HGRN_K)
    o = hgrn2_chunkwise(q.reshape(shp_k), k.reshape(shp_k),
                        hi.astype(jnp.float32).reshape(B, S, HGRN_HEADS, HGRN_V),
                        log_f.reshape(shp_k))
    o = o * lax.rsqrt(jnp.mean(o * o, axis=-1, keepdims=True) + EPS)
    o = o.reshape(B, S, HGRN_WIDTH) * norm_g.astype(jnp.float32)
    o = o * jax.nn.sigmoid(hg.astype(jnp.float32))
    return o.astype(hq.dtype)


def _fwd_setup_inputs(seed: int = 0) -> dict:
    key = jax.random.key(seed)
    ks = jax.random.split(key, 20)
    D, F = D_MODEL, FFN_HIDDEN
    nrm = lambda k, shape, fan_in: jax.random.normal(k, shape, jnp.float32) * fan_in ** -0.5
    gain = lambda k, shape: 1.0 + 0.05 * jax.random.normal(k, shape, jnp.float32)
    return {
        "x": jax.random.normal(ks[0], (BATCH, SEQ, D), jnp.float32),
        "norm_mix_g": gain(ks[1], (DEPTH, D)),
        "w_in": nrm(ks[2], (DEPTH, D, IN_WIDTH), D),
        "b_in": 0.01 * jax.random.normal(ks[3], (DEPTH, IN_WIDTH), jnp.float32),
        "attn_sinks": 0.5 * jax.random.normal(ks[4], (DEPTH, ATTN_Q_HEADS), jnp.float32),
        "hgrn_lb_logits": gain(ks[5], (DEPTH + 1, HGRN_KEY_WIDTH)),
        "hgrn_norm_g": gain(ks[6], (DEPTH, HGRN_WIDTH)),
        "w_branch_attn": nrm(ks[7], (DEPTH, ATTN_WIDTH, D), ATTN_WIDTH),
        "w_branch_hgrn": nrm(ks[8], (DEPTH, HGRN_WIDTH, D), HGRN_WIDTH),
        "w_out": nrm(ks[9], (DEPTH, D, D), D),
        "norm_ffn_g": gain(ks[10], (DEPTH, D)),
        "w_ffn_gate": nrm(ks[11], (DEPTH, D, F), D),
        "w_ffn_up": nrm(ks[12], (DEPTH, D, F), D),
        "w_ffn_down": nrm(ks[13], (DEPTH, F, D), F),
        "norm_final_g": gain(ks[14], (D,)),
    }


def _fwd_reference(x, norm_mix_g, w_in, b_in, attn_sinks, hgrn_lb_logits, hgrn_norm_g,
              w_branch_attn, w_branch_hgrn, w_out, norm_ffn_g, w_ffn_gate, w_ffn_up,
              w_ffn_down, norm_final_g):
    lb_all = jnp.cumsum(jax.nn.softmax(hgrn_lb_logits.astype(jnp.float32), axis=0), axis=0)
    h = x
    for layer in range(DEPTH):
        u = rmsnorm(h, norm_mix_g[layer])
        p = jnp.einsum('bsd,de->bse', u, w_in[layer]) + b_in[layer]
        aq, ak, av, hq, hf, hi, hg, gate_a, gate_b = split_columns(p)
        y_attn = sliding_window_sink_attention(aq, ak, av, attn_sinks[layer])
        y_hgrn = hgrn2_branch(hq, hf, hi, hg, lb_all[layer], hgrn_norm_g[layer])
        ya = jnp.einsum('bse,ed->bsd', y_attn, w_branch_attn[layer])
        yb = jnp.einsum('bse,ed->bsd', y_hgrn, w_branch_hgrn[layer])
        merged = jax.nn.sigmoid(gate_a) * ya + jax.nn.sigmoid(gate_b) * yb
        h = h + jnp.einsum('bsd,de->bse', merged, w_out[layer])
        u = rmsnorm(h, norm_ffn_g[layer])
        z = (jax.nn.silu(jnp.einsum('bsd,df->bsf', u, w_ffn_gate[layer]))
             * jnp.einsum('bsd,df->bsf', u, w_ffn_up[layer]))
        h = h + jnp.einsum('bsf,fd->bsd', z, w_ffn_down[layer])
    return rmsnorm(h, norm_final_g)


import jax as _jax
import jax.numpy as _jnp

TWIN_FORMAT = 'train_step'
FWD_PARAMS = ['x', 'norm_mix_g', 'w_in', 'b_in', 'attn_sinks', 'hgrn_lb_logits', 'hgrn_norm_g', 'w_branch_attn', 'w_branch_hgrn', 'w_out', 'norm_ffn_g', 'w_ffn_gate', 'w_ffn_up', 'w_ffn_down', 'norm_final_g']
TWIN_WEIGHTS = ['norm_mix_g', 'w_in', 'b_in', 'attn_sinks', 'hgrn_lb_logits', 'hgrn_norm_g', 'w_branch_attn', 'w_branch_hgrn', 'w_out', 'norm_ffn_g', 'w_ffn_gate', 'w_ffn_up', 'w_ffn_down', 'norm_final_g']
TWIN_DIFF_INPUT = 'x'
TWIN_INPUTS = ['x', 'norm_mix_g', 'w_in', 'b_in', 'attn_sinks', 'hgrn_lb_logits', 'hgrn_norm_g', 'w_branch_attn', 'w_branch_hgrn', 'w_out', 'norm_ffn_g', 'w_ffn_gate', 'w_ffn_up', 'w_ffn_down', 'norm_final_g', 'loss_target', 'm_norm_mix_g', 'm_w_in', 'm_b_in', 'm_attn_sinks', 'm_hgrn_lb_logits', 'm_hgrn_norm_g', 'm_w_branch_attn', 'm_w_branch_hgrn', 'm_w_out', 'm_norm_ffn_g', 'm_w_ffn_gate', 'm_w_ffn_up', 'm_w_ffn_down', 'm_norm_final_g', 'v_norm_mix_g', 'v_w_in', 'v_b_in', 'v_attn_sinks', 'v_hgrn_lb_logits', 'v_hgrn_norm_g', 'v_w_branch_attn', 'v_w_branch_hgrn', 'v_w_out', 'v_norm_ffn_g', 'v_w_ffn_gate', 'v_w_ffn_up', 'v_w_ffn_down', 'v_norm_final_g']
TWIN_OUTPUTS = ['loss', 'grad_x', 'grad_norm_mix_g', 'grad_w_in', 'grad_b_in', 'grad_attn_sinks', 'grad_hgrn_lb_logits', 'grad_hgrn_norm_g', 'grad_w_branch_attn', 'grad_w_branch_hgrn', 'grad_w_out', 'grad_norm_ffn_g', 'grad_w_ffn_gate', 'grad_w_ffn_up', 'grad_w_ffn_down', 'grad_norm_final_g', 'delta_norm_mix_g', 'delta_w_in', 'delta_b_in', 'delta_attn_sinks', 'delta_hgrn_lb_logits', 'delta_hgrn_norm_g', 'delta_w_branch_attn', 'delta_w_branch_hgrn', 'delta_w_out', 'delta_norm_ffn_g', 'delta_w_ffn_gate', 'delta_w_ffn_up', 'delta_w_ffn_down', 'delta_norm_final_g', 'new_m_norm_mix_g', 'new_m_w_in', 'new_m_b_in', 'new_m_attn_sinks', 'new_m_hgrn_lb_logits', 'new_m_hgrn_norm_g', 'new_m_w_branch_attn', 'new_m_w_branch_hgrn', 'new_m_w_out', 'new_m_norm_ffn_g', 'new_m_w_ffn_gate', 'new_m_w_ffn_up', 'new_m_w_ffn_down', 'new_m_norm_final_g', 'new_v_norm_mix_g', 'new_v_w_in', 'new_v_b_in', 'new_v_attn_sinks', 'new_v_hgrn_lb_logits', 'new_v_hgrn_norm_g', 'new_v_w_branch_attn', 'new_v_w_branch_hgrn', 'new_v_w_out', 'new_v_norm_ffn_g', 'new_v_w_ffn_gate', 'new_v_w_ffn_up', 'new_v_w_ffn_down', 'new_v_norm_final_g']
TWIN_LEAF_KINDS = {'loss': 'loss', 'grad_x': 'grad_x', 'grad_norm_mix_g': 'grad_w', 'grad_w_in': 'grad_w', 'grad_b_in': 'grad_w', 'grad_attn_sinks': 'grad_w', 'grad_hgrn_lb_logits': 'grad_w', 'grad_hgrn_norm_g': 'grad_w', 'grad_w_branch_attn': 'grad_w', 'grad_w_branch_hgrn': 'grad_w', 'grad_w_out': 'grad_w', 'grad_norm_ffn_g': 'grad_w', 'grad_w_ffn_gate': 'grad_w', 'grad_w_ffn_up': 'grad_w', 'grad_w_ffn_down': 'grad_w', 'grad_norm_final_g': 'grad_w', 'delta_norm_mix_g': 'delta_w', 'delta_w_in': 'delta_w', 'delta_b_in': 'delta_w', 'delta_attn_sinks': 'delta_w', 'delta_hgrn_lb_logits': 'delta_w', 'delta_hgrn_norm_g': 'delta_w', 'delta_w_branch_attn': 'delta_w', 'delta_w_branch_hgrn': 'delta_w', 'delta_w_out': 'delta_w', 'delta_norm_ffn_g': 'delta_w', 'delta_w_ffn_gate': 'delta_w', 'delta_w_ffn_up': 'delta_w', 'delta_w_ffn_down': 'delta_w', 'delta_norm_final_g': 'delta_w', 'new_m_norm_mix_g': 'new_m', 'new_m_w_in': 'new_m', 'new_m_b_in': 'new_m', 'new_m_attn_sinks': 'new_m', 'new_m_hgrn_lb_logits': 'new_m', 'new_m_hgrn_norm_g': 'new_m', 'new_m_w_branch_attn': 'new_m', 'new_m_w_branch_hgrn': 'new_m', 'new_m_w_out': 'new_m', 'new_m_norm_ffn_g': 'new_m', 'new_m_w_ffn_gate': 'new_m', 'new_m_w_ffn_up': 'new_m', 'new_m_w_ffn_down': 'new_m', 'new_m_norm_final_g': 'new_m', 'new_v_norm_mix_g': 'new_v', 'new_v_w_in': 'new_v', 'new_v_b_in': 'new_v', 'new_v_attn_sinks': 'new_v', 'new_v_hgrn_lb_logits': 'new_v', 'new_v_hgrn_norm_g': 'new_v', 'new_v_w_branch_attn': 'new_v', 'new_v_w_branch_hgrn': 'new_v', 'new_v_w_out': 'new_v', 'new_v_norm_ffn_g': 'new_v', 'new_v_w_ffn_gate': 'new_v', 'new_v_w_ffn_up': 'new_v', 'new_v_w_ffn_down': 'new_v', 'new_v_norm_final_g': 'new_v'}


def _forward(args):
    return _fwd_reference(*[args[k] for k in FWD_PARAMS])


def _output_shape():
    def fwd():
        inp = _fwd_setup_inputs(0)
        return _fwd_reference(*[inp[k] for k in FWD_PARAMS])
    out = _jax.eval_shape(fwd)
    return out.shape, out.dtype

N_MICROBATCH = 1
ADAM_LR = 0.001
ADAM_B1 = 0.9
ADAM_B2 = 0.999
ADAM_EPS = 1e-08
ADAM_WD = 0.01
ADAM_STEP = 10
PER_EXAMPLE_BATCH_AXIS = {'x': 0, 'loss_target': 0}
SHARED_INPUTS = []
_WEIGHT_DTYPES = {'norm_mix_g': _jnp.float32, 'w_in': _jnp.float32, 'b_in': _jnp.float32, 'attn_sinks': _jnp.float32, 'hgrn_lb_logits': _jnp.float32, 'hgrn_norm_g': _jnp.float32, 'w_branch_attn': _jnp.float32, 'w_branch_hgrn': _jnp.float32, 'w_out': _jnp.float32, 'norm_ffn_g': _jnp.float32, 'w_ffn_gate': _jnp.float32, 'w_ffn_up': _jnp.float32, 'w_ffn_down': _jnp.float32, 'norm_final_g': _jnp.float32}
MOMENT_SCALE = {'norm_mix_g': 7.240073e-02, 'w_in': 2.742747e-02, 'b_in': 8.548568e-02, 'attn_sinks': 1.687718e-02, 'hgrn_lb_logits': 5.504632e-03, 'hgrn_norm_g': 5.577721e-02, 'w_branch_attn': 2.108969e-02, 'w_branch_hgrn': 5.727486e-02, 'w_out': 5.999332e-02, 'norm_ffn_g': 1.305579e-01, 'w_ffn_gate': 5.520579e-02, 'w_ffn_up': 5.380992e-02, 'w_ffn_down': 8.976343e-02, 'norm_final_g': 3.207146e+01}


def _to_microbatches(a, axis):
    t = _jnp.moveaxis(a, axis, 0)
    t = t.reshape((N_MICROBATCH, t.shape[0] // N_MICROBATCH) + t.shape[1:])
    return _jnp.moveaxis(t, 1, axis + 1)


def setup_inputs(seed: int = 0) -> dict:
    inp = _fwd_setup_inputs(seed)
    key = _jax.random.fold_in(_jax.random.key(seed), 7919)
    shape, _ = _output_shape()
    out = dict(inp)
    out["loss_target"] = _jax.random.normal(_jax.random.fold_in(key, 0), shape, _jnp.float32)
    for i, name in enumerate(TWIN_WEIGHTS):
        w = inp[name].astype(_jnp.float32)
        if MOMENT_SCALE is None:
            s = _jnp.sqrt(_jnp.mean(_jnp.square(w)) + 1e-30)
        else:
            s = MOMENT_SCALE[name]
        km, kv = _jax.random.split(_jax.random.fold_in(key, i + 1))
        out[name] = w
        out["m_" + name] = s * _jax.random.normal(km, w.shape, _jnp.float32)
        out["v_" + name] = (s * s) * _jax.random.uniform(kv, w.shape, _jnp.float32, 0.5, 1.5)
    if N_MICROBATCH > 1:
        for name, axis in PER_EXAMPLE_BATCH_AXIS.items():
            out[name] = _to_microbatches(out[name], axis)
    return {'x': out['x'], 'norm_mix_g': out['norm_mix_g'], 'w_in': out['w_in'], 'b_in': out['b_in'], 'attn_sinks': out['attn_sinks'], 'hgrn_lb_logits': out['hgrn_lb_logits'], 'hgrn_norm_g': out['hgrn_norm_g'], 'w_branch_attn': out['w_branch_attn'], 'w_branch_hgrn': out['w_branch_hgrn'], 'w_out': out['w_out'], 'norm_ffn_g': out['norm_ffn_g'], 'w_ffn_gate': out['w_ffn_gate'], 'w_ffn_up': out['w_ffn_up'], 'w_ffn_down': out['w_ffn_down'], 'norm_final_g': out['norm_final_g'], 'loss_target': out['loss_target'], 'm_norm_mix_g': out['m_norm_mix_g'], 'm_w_in': out['m_w_in'], 'm_b_in': out['m_b_in'], 'm_attn_sinks': out['m_attn_sinks'], 'm_hgrn_lb_logits': out['m_hgrn_lb_logits'], 'm_hgrn_norm_g': out['m_hgrn_norm_g'], 'm_w_branch_attn': out['m_w_branch_attn'], 'm_w_branch_hgrn': out['m_w_branch_hgrn'], 'm_w_out': out['m_w_out'], 'm_norm_ffn_g': out['m_norm_ffn_g'], 'm_w_ffn_gate': out['m_w_ffn_gate'], 'm_w_ffn_up': out['m_w_ffn_up'], 'm_w_ffn_down': out['m_w_ffn_down'], 'm_norm_final_g': out['m_norm_final_g'], 'v_norm_mix_g': out['v_norm_mix_g'], 'v_w_in': out['v_w_in'], 'v_b_in': out['v_b_in'], 'v_attn_sinks': out['v_attn_sinks'], 'v_hgrn_lb_logits': out['v_hgrn_lb_logits'], 'v_hgrn_norm_g': out['v_hgrn_norm_g'], 'v_w_branch_attn': out['v_w_branch_attn'], 'v_w_branch_hgrn': out['v_w_branch_hgrn'], 'v_w_out': out['v_w_out'], 'v_norm_ffn_g': out['v_norm_ffn_g'], 'v_w_ffn_gate': out['v_w_ffn_gate'], 'v_w_ffn_up': out['v_w_ffn_up'], 'v_w_ffn_down': out['v_w_ffn_down'], 'v_norm_final_g': out['v_norm_final_g']}


def _loss(weights, diff, rest, loss_target):
    with _jax.named_scope("forward"):
        args = {**rest, TWIN_DIFF_INPUT: diff, **{k: w.astype(_WEIGHT_DTYPES[k]) for k, w in weights.items()}}
        y = _forward(args)
    with _jax.named_scope("loss_head"):
        err = _jnp.square(y.astype(_jnp.float32) - loss_target)
        return 0.5 * _jnp.sum(_jnp.mean(err, axis=-1)) if err.ndim else 0.5 * err


def _adamw(w, g, m, v):
    m = ADAM_B1 * m + (1.0 - ADAM_B1) * g
    v = ADAM_B2 * v + (1.0 - ADAM_B2) * _jnp.square(g)
    m_hat = m / (1.0 - ADAM_B1 ** ADAM_STEP)
    v_hat = v / (1.0 - ADAM_B2 ** ADAM_STEP)
    delta = -ADAM_LR * (m_hat / (_jnp.sqrt(v_hat) + ADAM_EPS) + ADAM_WD * w)
    return delta, m, v


def reference(x, norm_mix_g, w_in, b_in, attn_sinks, hgrn_lb_logits, hgrn_norm_g, w_branch_attn, w_branch_hgrn, w_out, norm_ffn_g, w_ffn_gate, w_ffn_up, w_ffn_down, norm_final_g, loss_target, m_norm_mix_g, m_w_in, m_b_in, m_attn_sinks, m_hgrn_lb_logits, m_hgrn_norm_g, m_w_branch_attn, m_w_branch_hgrn, m_w_out, m_norm_ffn_g, m_w_ffn_gate, m_w_ffn_up, m_w_ffn_down, m_norm_final_g, v_norm_mix_g, v_w_in, v_b_in, v_attn_sinks, v_hgrn_lb_logits, v_hgrn_norm_g, v_w_branch_attn, v_w_branch_hgrn, v_w_out, v_norm_ffn_g, v_w_ffn_gate, v_w_ffn_up, v_w_ffn_down, v_norm_final_g):
    given = dict(x=x, norm_mix_g=norm_mix_g, w_in=w_in, b_in=b_in, attn_sinks=attn_sinks, hgrn_lb_logits=hgrn_lb_logits, hgrn_norm_g=hgrn_norm_g, w_branch_attn=w_branch_attn, w_branch_hgrn=w_branch_hgrn, w_out=w_out, norm_ffn_g=norm_ffn_g, w_ffn_gate=w_ffn_gate, w_ffn_up=w_ffn_up, w_ffn_down=w_ffn_down, norm_final_g=norm_final_g, loss_target=loss_target, m_norm_mix_g=m_norm_mix_g, m_w_in=m_w_in, m_b_in=m_b_in, m_attn_sinks=m_attn_sinks, m_hgrn_lb_logits=m_hgrn_lb_logits, m_hgrn_norm_g=m_hgrn_norm_g, m_w_branch_attn=m_w_branch_attn, m_w_branch_hgrn=m_w_branch_hgrn, m_w_out=m_w_out, m_norm_ffn_g=m_norm_ffn_g, m_w_ffn_gate=m_w_ffn_gate, m_w_ffn_up=m_w_ffn_up, m_w_ffn_down=m_w_ffn_down, m_norm_final_g=m_norm_final_g, v_norm_mix_g=v_norm_mix_g, v_w_in=v_w_in, v_b_in=v_b_in, v_attn_sinks=v_attn_sinks, v_hgrn_lb_logits=v_hgrn_lb_logits, v_hgrn_norm_g=v_hgrn_norm_g, v_w_branch_attn=v_w_branch_attn, v_w_branch_hgrn=v_w_branch_hgrn, v_w_out=v_w_out, v_norm_ffn_g=v_norm_ffn_g, v_w_ffn_gate=v_w_ffn_gate, v_w_ffn_up=v_w_ffn_up, v_w_ffn_down=v_w_ffn_down, v_norm_final_g=v_norm_final_g)
    weights = {n: given[n] for n in TWIN_WEIGHTS}
    shared = {n: given[n] for n in SHARED_INPUTS}
    per_example = {n: given[n] for n in ['x']}
    grad_fn = _jax.value_and_grad(_loss, argnums=(0, 1))

    def one_microbatch(ex, loss_target):
        ex = dict(ex)
        diff = ex.pop(TWIN_DIFF_INPUT)
        return grad_fn(weights, diff, {**shared, **ex}, loss_target)

    if N_MICROBATCH == 1:
        loss, (grad_w, grad_x) = one_microbatch(per_example, given["loss_target"])
    else:
        def body(carry, xs):
            loss_sum, grad_sum = carry
            l_k, (gw_k, gx_k) = one_microbatch(xs[0], xs[1])
            with _jax.named_scope("update"):
                return (loss_sum + l_k, _jax.tree.map(_jnp.add, grad_sum, gw_k)), gx_k

        init = (_jnp.zeros((), _jnp.float32), _jax.tree.map(_jnp.zeros_like, weights))
        (loss, grad_w), grad_x = _jax.lax.scan(body, init, (per_example, given["loss_target"]))
    with _jax.named_scope("update"):
        delta_w, new_m, new_v = {}, {}, {}
        for n in TWIN_WEIGHTS:
            delta_w[n], new_m[n], new_v[n] = _adamw(weights[n], grad_w[n], given["m_" + n], given["v_" + n])
    return (loss, grad_x, *[grad_w[n] for n in TWIN_WEIGHTS], *[delta_w[n] for n in TWIN_WEIGHTS],
            *[new_m[n] for n in TWIN_WEIGHTS], *[new_v[n] for n in TWIN_WEIGHTS])
```

```python
import math

import jax
import jax.numpy as jnp
from jax import lax
from jax.experimental import pallas as pl
from jax.experimental.pallas import tpu as pltpu

F32 = jnp.float32
BF = jnp.bfloat16
MESH = pl.DeviceIdType.MESH

D = 1024
HEAD = 64
N_PAIR = 8
BLK = 128
CH = 64
HG_HEADS = 8
HG_K = 128
FFN = 2816
IN_W = 7424
N_DEV = 8
N_CHIP = 4
EPS = 1e-6
NEG = -1e30
SCALE = 1.0 / math.sqrt(HEAD)
VMEM_LIMIT = 56 * 1024 * 1024
WT = 256

ADAM_LR, ADAM_B1, ADAM_B2, ADAM_EPS, ADAM_WD, ADAM_STEP = 0.001, 0.9, 0.999, 1e-08, 0.01, 10

SLAB_R = (IN_W // N_DEV, FFN // N_DEV, FFN // N_DEV, FFN // N_DEV, D // N_DEV, D // N_DEV, D // N_DEV)
SLAB_ROWS = sum(SLAB_R)
SLAB_OFF = tuple(sum(SLAB_R[:i]) for i in range(len(SLAB_R)))
N_W = len(SLAB_R)
GRP_OFF = (0, D // WT, (D + 256) // WT, (5 * D + 256) // WT)
GRP_N = (D // WT, 256 // WT, 4 * D // WT, 2 * D // WT)
SMALL_ROWS = 16


def _pcall(body, **kw):
    return pl.pallas_call(body, **kw)


def _cp(sem=None, **kw):
    return pltpu.CompilerParams(dimension_semantics=sem, vmem_limit_bytes=VMEM_LIMIT, **kw)


def _sig(v):
    return 1.0 / (1.0 + jnp.exp(-v))


def _accum(ref, val, first):
    @pl.when(first)
    def _():
        ref[...] = val

    @pl.when(jnp.logical_not(first))
    def _():
        ref[...] += val


def _mm(a, b, *, m, n, k, tm, tn, tk, ta=False, tb=False, out_dtype=F32, resid=None, name):
    tm, tn, tk = min(tm, m), min(tn, n), min(tk, k)
    gm, gn, gk = m // tm, n // tn, k // tk
    assert gm * tm == m and gn * tn == n and gk * tk == k, (name, m, n, k, tm, tn, tk)
    a_spec = (pl.BlockSpec((tk, tm), lambda i, j, l: (l, i)) if ta
              else pl.BlockSpec((tm, tk), lambda i, j, l: (i, l)))
    b_spec = (pl.BlockSpec((tn, tk), lambda i, j, l: (j, l)) if tb
              else pl.BlockSpec((tk, tn), lambda i, j, l: (l, j)))
    dims = (((0 if ta else 1,), (1 if tb else 0,)), ((), ()))
    ins, in_specs = [a, b], [a_spec, b_spec]
    if resid is not None:
        ins.append(resid)
        in_specs.append(pl.BlockSpec((tm, tn), lambda i, j, l: (i, j)))
    scratch = [pltpu.VMEM((tm, tn), F32)] if gk > 1 else []

    def body(*refs):
        it = iter(refs)
        a_ref, b_ref = next(it), next(it)
        resid_ref = next(it) if resid is not None else None
        o_ref = next(it)
        acc_ref = next(it) if gk > 1 else None
        l = pl.program_id(2)
        part = lax.dot_general(a_ref[...].astype(BF), b_ref[...].astype(BF), dims,
                               preferred_element_type=F32)

        def finish(acc):
            if resid_ref is not None:
                acc = acc + resid_ref[...].astype(F32)
            o_ref[...] = acc.astype(out_dtype)

        if gk == 1:
            finish(part)
        else:
            _accum(acc_ref, part, l == 0)

            @pl.when(l == gk - 1)
            def _():
                finish(acc_ref[...])

    return _pcall(body, name=name, grid=(gm, gn, gk), in_specs=in_specs,
                  out_specs=pl.BlockSpec((tm, tn), lambda i, j, l: (i, j)),
                  out_shape=jax.ShapeDtypeStruct((m, n), out_dtype), scratch_shapes=scratch,
                  compiler_params=_cp(("parallel", "parallel", "arbitrary")))(*ins)


def _grp_of(i):
    return [jnp.logical_and(i >= GRP_OFF[g], i < GRP_OFF[g] + GRP_N[g]) for g in range(4)]


def _grp_idx(i, g):
    return jnp.clip(i - GRP_OFF[g], 0, GRP_N[g] - 1)


def _inproj_fwd(u, win_t, b_in, *, t):
    n_tiles = IN_W // WT
    dims = (((1,), (1,)), ((), ()))
    dtypes = (BF, BF, F32, F32)

    def body(u_ref, w_ref, b_ref, *o_refs):
        i = pl.program_id(0)
        p = lax.dot_general(u_ref[...], w_ref[...], dims, preferred_element_type=F32) + b_ref[...]
        for g, pred in enumerate(_grp_of(i)):
            @pl.when(pred)
            def _(g=g):
                o_refs[g][...] = p.astype(dtypes[g])

    return _pcall(body, name="inproj_fwd", grid=(n_tiles,),
                  in_specs=[pl.BlockSpec((t, D), lambda i: (0, 0)),
                            pl.BlockSpec((WT, D), lambda i: (i, 0)),
                            pl.BlockSpec((1, WT), lambda i: (0, i))],
                  out_specs=[pl.BlockSpec((t, WT), lambda i, g=g: (0, _grp_idx(i, g))) for g in range(4)],
                  out_shape=[jax.ShapeDtypeStruct((t, GRP_N[g] * WT), dtypes[g]) for g in range(4)],
                  compiler_params=_cp(("arbitrary",)))(u, win_t, b_in)


def _inproj_bwd_x(dps, win_t, *, t, tm):
    n_tiles = IN_W // WT
    tm = min(tm, t)

    def body(d0, d1, d2, d3, w_ref, o_ref, acc_ref):
        l = pl.program_id(1)
        w = w_ref[...]
        for g, (pred, d_ref) in enumerate(zip(_grp_of(l), (d0, d1, d2, d3))):
            @pl.when(pred)
            def _(d_ref=d_ref):
                part = jnp.dot(d_ref[...], w, preferred_element_type=F32)
                _accum(acc_ref, part, l == 0)

        @pl.when(l == n_tiles - 1)
        def _():
            o_ref[...] = acc_ref[...]

    return _pcall(body, name="inproj_bwd_x", grid=(t // tm, n_tiles),
                  in_specs=[pl.BlockSpec((tm, WT), lambda i, l, g=g: (i, _grp_idx(l, g))) for g in range(4)]
                  + [pl.BlockSpec((WT, D), lambda i, l: (l, 0))],
                  out_specs=pl.BlockSpec((tm, D), lambda i, l: (i, 0)),
                  out_shape=jax.ShapeDtypeStruct((t, D), F32),
                  scratch_shapes=[pltpu.VMEM((tm, D), F32)],
                  compiler_params=_cp(("parallel", "arbitrary")))(*dps, win_t)


def _inproj_bwd_w(dps, u, *, t):
    n_tiles = IN_W // WT
    dims = (((0,), (0,)), ((), ()))

    def body(d0, d1, d2, d3, u_ref, o_ref, db_ref):
        i = pl.program_id(0)
        uv = u_ref[...]
        for g, (pred, d_ref) in enumerate(zip(_grp_of(i), (d0, d1, d2, d3))):
            @pl.when(pred)
            def _(d_ref=d_ref):
                dv = d_ref[...]
                o_ref[...] = lax.dot_general(dv, uv, dims, preferred_element_type=F32).astype(BF)
                db_ref[...] = jnp.sum(dv.astype(F32), axis=0, keepdims=True)

    return _pcall(body, name="inproj_bwd_w", grid=(n_tiles,),
                  in_specs=[pl.BlockSpec((t, WT), lambda i, g=g: (0, _grp_idx(i, g))) for g in range(4)]
                  + [pl.BlockSpec((t, D), lambda i: (0, 0))],
                  out_specs=[pl.BlockSpec((WT, D), lambda i: (i, 0)),
                             pl.BlockSpec((1, WT), lambda i: (0, i))],
                  out_shape=[jax.ShapeDtypeStruct((IN_W, D), BF), jax.ShapeDtypeStruct((1, IN_W), F32)],
                  compiler_params=_cp(("arbitrary",)))(*dps, u)


def _row_spec(tm, width, col=0):
    return pl.BlockSpec((tm, width), lambda i: (i, col))


def _vec_spec(width):
    return pl.BlockSpec((1, width), lambda i: (0, 0))


def _rms_fwd(x, g, *, tm, name):
    t = x.shape[0]
    tm = min(tm, t)

    def body(x_ref, g_ref, u_ref):
        xv = x_ref[...]
        r = lax.rsqrt(jnp.mean(xv * xv, axis=-1, keepdims=True) + EPS)
        u_ref[...] = (xv * r * g_ref[...]).astype(BF)

    return _pcall(body, name=name, grid=(t // tm,), in_specs=[_row_spec(tm, D), _vec_spec(D)],
                  out_specs=_row_spec(tm, D), out_shape=jax.ShapeDtypeStruct((t, D), BF),
                  compiler_params=_cp(("parallel",)))(x, g)


def _rms_bwd(du, x, g, resid, *, tm, name):
    t = x.shape[0]
    tm = min(tm, t)

    def body(du_ref, x_ref, g_ref, r_ref, dx_ref, dxb_ref, dg_ref):
        xv = x_ref[...]
        r = lax.rsqrt(jnp.mean(xv * xv, axis=-1, keepdims=True) + EPS)
        xh = xv * r
        duv = du_ref[...]
        dxh = duv * g_ref[...]
        dx = r_ref[...] + r * (dxh - xh * jnp.mean(dxh * xh, axis=-1, keepdims=True))
        dx_ref[...] = dx
        dxb_ref[...] = dx.astype(BF)
        _accum(dg_ref, jnp.sum(duv * xh, axis=0, keepdims=True), pl.program_id(0) == 0)

    return _pcall(body, name=name, grid=(t // tm,),
                  in_specs=[_row_spec(tm, D), _row_spec(tm, D), _vec_spec(D), _row_spec(tm, D)],
                  out_specs=[_row_spec(tm, D), _row_spec(tm, D), _vec_spec(D)],
                  out_shape=[jax.ShapeDtypeStruct((t, D), F32), jax.ShapeDtypeStruct((t, D), BF),
                             jax.ShapeDtypeStruct((1, D), F32)],
                  compiler_params=_cp(("arbitrary",)))(du, x, g, resid)


def _loss_head(h2, tgt, g, *, tm):
    t = h2.shape[0]
    tm = min(tm, t)

    def body(h_ref, t_ref, g_ref, dh_ref, dhb_ref, dg_ref, loss_ref):
        hv = h_ref[...]
        gv = g_ref[...]
        r = lax.rsqrt(jnp.mean(hv * hv, axis=-1, keepdims=True) + EPS)
        xh = hv * r
        err = xh * gv - t_ref[...]
        lp = jnp.sum(jnp.sum(err * err, axis=1, keepdims=True), axis=0, keepdims=True) * (0.5 / D)
        dy = err * (1.0 / D)
        dxh = dy * gv
        dh = r * (dxh - xh * jnp.mean(dxh * xh, axis=-1, keepdims=True))
        dh_ref[...] = dh
        dhb_ref[...] = dh.astype(BF)
        first = pl.program_id(0) == 0
        _accum(dg_ref, jnp.sum(dy * xh, axis=0, keepdims=True), first)
        _accum(loss_ref, jnp.broadcast_to(lp, (1, 128)), first)

    return _pcall(body, name="loss_head", grid=(t // tm,),
                  in_specs=[_row_spec(tm, D), _row_spec(tm, D), _vec_spec(D)],
                  out_specs=[_row_spec(tm, D), _row_spec(tm, D), _vec_spec(D), _vec_spec(128)],
                  out_shape=[jax.ShapeDtypeStruct((t, D), F32), jax.ShapeDtypeStruct((t, D), BF),
                             jax.ShapeDtypeStruct((1, D), F32), jax.ShapeDtypeStruct((1, 128), F32)],
                  compiler_params=_cp(("arbitrary",)))(h2, tgt, g)


def _merge_fwd(gates, ya, yb, *, tm):
    t = ya.shape[0]
    tm = min(tm, t)

    def body(ga_ref, gb_ref, ya_ref, yb_ref, o_ref):
        o_ref[...] = (_sig(ga_ref[...]) * ya_ref[...] + _sig(gb_ref[...]) * yb_ref[...]).astype(BF)

    return _pcall(body, name="merge_fwd", grid=(t // tm,),
                  in_specs=[_row_spec(tm, D, 0), _row_spec(tm, D, 1), _row_spec(tm, D), _row_spec(tm, D)],
                  out_specs=_row_spec(tm, D), out_shape=jax.ShapeDtypeStruct((t, D), BF),
                  compiler_params=_cp(("parallel",)))(gates, gates, ya, yb)


def _merge_bwd(dm, gates, ya, yb, *, tm):
    t = ya.shape[0]
    tm = min(tm, t)

    def body(dm_ref, ga_ref, gb_ref, ya_ref, yb_ref, dya_ref, dyb_ref, dg_ref):
        dmv = dm_ref[...]
        sa, sb = _sig(ga_ref[...]), _sig(gb_ref[...])
        dya_ref[...] = (dmv * sa).astype(BF)
        dyb_ref[...] = (dmv * sb).astype(BF)
        dg_ref[:, 0:D] = (dmv * ya_ref[...] * sa * (1.0 - sa)).astype(BF)
        dg_ref[:, D:2 * D] = (dmv * yb_ref[...] * sb * (1.0 - sb)).astype(BF)

    return _pcall(body, name="merge_bwd", grid=(t // tm,),
                  in_specs=[_row_spec(tm, D), _row_spec(tm, D, 0), _row_spec(tm, D, 1),
                            _row_spec(tm, D), _row_spec(tm, D)],
                  out_specs=[_row_spec(tm, D), _row_spec(tm, D), _row_spec(tm, 2 * D)],
                  out_shape=[jax.ShapeDtypeStruct((t, D), BF), jax.ShapeDtypeStruct((t, D), BF),
                             jax.ShapeDtypeStruct((t, 2 * D), BF)],
                  compiler_params=_cp(("parallel",)))(dm, gates, gates, ya, yb)


def _swiglu_fwd(gt, up, *, tm):
    t = gt.shape[0]
    tm = min(tm, t)

    def body(g_ref, u_ref, z_ref):
        gv = g_ref[...]
        z_ref[...] = (gv * _sig(gv) * u_ref[...]).astype(BF)

    return _pcall(body, name="swiglu_fwd", grid=(t // tm,),
                  in_specs=[_row_spec(tm, FFN), _row_spec(tm, FFN)],
                  out_specs=_row_spec(tm, FFN), out_shape=jax.ShapeDtypeStruct((t, FFN), BF),
                  compiler_params=_cp(("parallel",)))(gt, up)


def _swiglu_bwd(dz, gt, up, *, tm):
    t = gt.shape[0]
    tm = min(tm, t)

    def body(dz_ref, g_ref, u_ref, dg_ref, du_ref):
        gv, dzv = g_ref[...], dz_ref[...]
        s = _sig(gv)
        dg_ref[...] = (dzv * u_ref[...] * s * (1.0 + gv * (1.0 - s))).astype(BF)
        du_ref[...] = (dzv * gv * s).astype(BF)

    return _pcall(body, name="swiglu_bwd", grid=(t // tm,),
                  in_specs=[_row_spec(tm, FFN)] * 3,
                  out_specs=[_row_spec(tm, FFN)] * 2,
                  out_shape=[jax.ShapeDtypeStruct((t, FFN), BF)] * 2,
                  compiler_params=_cp(("parallel",)))(dz, gt, up)


def _attn_kv_tiles(kprev, kcur):
    kv = jnp.concatenate([kprev, kcur], axis=0).astype(F32)
    lo = lax.broadcasted_iota(jnp.int32, (2 * BLK, 128), 1) < HEAD
    tiles = []
    for part in (kv[:, 0:128], kv[:, 128:256]):
        rolled = pltpu.roll(part, HEAD, 1)
        z = jnp.zeros_like(part)
        tiles.append(((jnp.where(lo, part, z).astype(BF), jnp.where(lo, z, rolled).astype(BF)),
                      (jnp.where(lo, rolled, z).astype(BF), jnp.where(lo, z, part).astype(BF))))
    k_t, v_t = tiles
    return [(k_t[h][0], k_t[h][1], v_t[h][0], v_t[h][1]) for h in range(2)]


def _attn_mask(i):
    qi = lax.broadcasted_iota(jnp.int32, (BLK, 2 * BLK), 0)
    kj = lax.broadcasted_iota(jnp.int32, (BLK, 2 * BLK), 1)
    first_key = jnp.where(i == 0, BLK, 0)
    in_prev = jnp.logical_and(jnp.logical_and(kj < BLK, kj > qi), kj >= first_key)
    in_cur = jnp.logical_and(kj >= BLK, kj - BLK <= qi)
    return jnp.logical_or(in_prev, in_cur)


def _attn_probs(q2, kt, sink, valid):
    s = lax.dot_general(q2, kt, (((1,), (1,)), ((), ())), preferred_element_type=F32) * SCALE
    s = jnp.where(valid, s, NEG)
    mx = jnp.maximum(jnp.max(s, axis=-1, keepdims=True), sink)
    e = jnp.exp(s - mx)
    es = jnp.exp(sink - mx)
    inv = 1.0 / (jnp.sum(e, axis=-1, keepdims=True) + es)
    return e * inv, es * inv


def _attn_fwd(q, kv, sinks, *, t):
    nb = t // BLK

    def body(sink_ref, q_ref, kp_ref, kc_ref, o_ref):
        i = pl.program_id(0)
        valid = _attn_mask(i)
        tiles = _attn_kv_tiles(kp_ref[...], kc_ref[...])
        for j in range(N_PAIR):
            ke, ko, ve, vo = tiles[j // 4]
            q2 = q_ref[:, j * 128:(j + 1) * 128]
            pe, _ = _attn_probs(q2, ke, sink_ref[0, 2 * j], valid)
            po, _ = _attn_probs(q2, ko, sink_ref[0, 2 * j + 1], valid)
            o2 = (jnp.dot(pe.astype(BF), ve, preferred_element_type=F32)
                  + jnp.dot(po.astype(BF), vo, preferred_element_type=F32))
            o_ref[:, j * 128:(j + 1) * 128] = o2.astype(BF)

    return _pcall(body, name="attn_fwd", grid=(nb,),
                  in_specs=[pl.BlockSpec(memory_space=pltpu.SMEM),
                            pl.BlockSpec((BLK, D), lambda i: (i, 0)),
                            pl.BlockSpec((BLK, 256), lambda i: (jnp.maximum(i - 1, 0), 0)),
                            pl.BlockSpec((BLK, 256), lambda i: (i, 0))],
                  out_specs=pl.BlockSpec((BLK, D), lambda i: (i, 0)),
                  out_shape=jax.ShapeDtypeStruct((t, D), BF),
                  compiler_params=_cp(("parallel",)))(sinks, q, kv, kv)


def _attn_bwd(q, kv, sinks, do, *, t):
    nb = t // BLK
    last = nb - 1
    tn_dims = (((0,), (0,)), ((), ()))
    nt_dims = (((1,), (1,)), ((), ()))

    def body(sink_ref, q_ref, kp_ref, kc_ref, do_ref, dq_ref, dkv_ref, ds_ref, carry_ref):
        i = pl.program_id(0)

        @pl.when(i == 0)
        def _():
            ds_ref[...] = jnp.zeros_like(ds_ref)
            carry_ref[...] = jnp.zeros_like(carry_ref)

        @pl.when(i < nb)
        def _():
            valid = _attn_mask(i)
            tiles = _attn_kv_tiles(kp_ref[...], kc_ref[...])
            lane1 = lax.broadcasted_iota(jnp.int32, (1, 128), 1)
            dsink = jnp.zeros((1, 128), F32)
            gk = [[None, None], [None, None]]
            gv = [[None, None], [None, None]]
            for j in range(N_PAIR):
                h = j // 4
                ke, ko, ve, vo = tiles[h]
                q2 = q_ref[:, j * 128:(j + 1) * 128]
                do2 = do_ref[:, j * 128:(j + 1) * 128]
                dq2 = jnp.zeros((BLK, 128), F32)
                for par, (kt, vt) in enumerate(((ke, ve), (ko, vo))):
                    p, ps = _attn_probs(q2, kt, sink_ref[0, 2 * j + par], valid)
                    dp = lax.dot_general(do2, vt, nt_dims, preferred_element_type=F32)
                    dd = jnp.sum(p * dp, axis=-1, keepdims=True)
                    dsc = (p * (dp - dd)).astype(BF)
                    dsink = dsink + jnp.where(lane1 == 2 * j + par,
                                              -jnp.sum(ps * dd, axis=0, keepdims=True), 0.0)
                    dq2 = dq2 + jnp.dot(dsc, kt, preferred_element_type=F32)
                    gk_c = lax.dot_general(dsc, q2, tn_dims, preferred_element_type=F32)
                    gv_c = lax.dot_general(p.astype(BF), do2, tn_dims, preferred_element_type=F32)
                    gk[h][par] = gk_c if gk[h][par] is None else gk[h][par] + gk_c
                    gv[h][par] = gv_c if gv[h][par] is None else gv[h][par] + gv_c
                dq_ref[:, j * 128:(j + 1) * 128] = (dq2 * SCALE).astype(BF)
            ds_ref[...] += dsink
            lo = lax.broadcasted_iota(jnp.int32, (2 * BLK, 128), 1) < HEAD
            zero = jnp.zeros((2 * BLK, 128), F32)

            def unpad(g):
                return (jnp.where(lo, g[0][0] + pltpu.roll(g[0][1], HEAD, 1), zero)
                        + jnp.where(lo, zero, pltpu.roll(g[1][0], HEAD, 1) + g[1][1]))

            dk = unpad(gk) * SCALE
            dv = unpad(gv)
            dkv_ref[:, 0:128] = (carry_ref[:, 0:128] + dk[0:BLK]).astype(BF)
            dkv_ref[:, 128:256] = (carry_ref[:, 128:256] + dv[0:BLK]).astype(BF)
            carry_ref[:, 0:128] = dk[BLK:2 * BLK]
            carry_ref[:, 128:256] = dv[BLK:2 * BLK]

        @pl.when(i == nb)
        def _():
            dkv_ref[...] = carry_ref[...].astype(BF)

    return _pcall(body, name="attn_bwd", grid=(nb + 1,),
                  in_specs=[pl.BlockSpec(memory_space=pltpu.SMEM),
                            pl.BlockSpec((BLK, D), lambda i: (jnp.minimum(i, last), 0)),
                            pl.BlockSpec((BLK, 256), lambda i: (jnp.clip(i - 1, 0, last), 0)),
                            pl.BlockSpec((BLK, 256), lambda i: (jnp.minimum(i, last), 0)),
                            pl.BlockSpec((BLK, D), lambda i: (jnp.minimum(i, last), 0))],
                  out_specs=[pl.BlockSpec((BLK, D), lambda i: (jnp.minimum(i, last), 0)),
                             pl.BlockSpec((BLK, 256), lambda i: (jnp.maximum(i - 1, 0), 0)),
                             pl.BlockSpec((1, 128), lambda i: (0, 0))],
                  out_shape=[jax.ShapeDtypeStruct((t, D), BF), jax.ShapeDtypeStruct((t, 256), BF),
                             jax.ShapeDtypeStruct((1, 128), F32)],
                  scratch_shapes=[pltpu.VMEM((BLK, 256), F32)],
                  compiler_params=_cp(("arbitrary",)))(sinks, q, kv, kv, do)


def _split3(v):
    h = v.astype(BF)
    r = v - h.astype(F32)
    m = r.astype(BF)
    lo = (r - m.astype(F32)).astype(BF)
    return jnp.concatenate([h, m, lo], axis=1)


def _apply01(mat, v):
    n = v.shape[1]
    r = jnp.dot(mat, _split3(v), preferred_element_type=F32)
    return r[:, 0:n] + r[:, n:2 * n] + r[:, 2 * n:3 * n]


def _cum_mats():
    r = lax.broadcasted_iota(jnp.int32, (4 * CH, CH), 0)
    c = lax.broadcasted_iota(jnp.int32, (4 * CH, CH), 1)
    limit = jnp.where(r < CH, r, jnp.where(r < 2 * CH, CH // 2 - 1, CH))
    return jnp.where(c <= limit, 1.0, 0.0).astype(BF)


def _hgrn_decays(mats, g):
    cum = _apply01(mats, g)
    return cum[0:CH], cum[CH:2 * CH], cum[2 * CH:3 * CH], cum[2 * CH:4 * CH]


def _hgrn_gates(hq, hf, lb):
    sq = _sig(hq)
    sg = _sig(hf)
    f = lb + (1.0 - lb) * sg
    return hq * sq, (1.0 - lb) * (1.0 - sg), jnp.log(f), sq, sg, f


def _tri(upper):
    r = lax.broadcasted_iota(jnp.int32, (CH, CH), 0)
    c = lax.broadcasted_iota(jnp.int32, (CH, CH), 1)
    return (c >= r) if upper else (c <= r)


def _lb_from_logits(lg_ref):
    return 1.0 / (1.0 + jnp.exp(lg_ref[1:2, :] - lg_ref[0:1, :]))


def _hgrn_fwd(h4, logits, norm_g, *, t):
    nc = t // CH
    nt_dims = (((1,), (1,)), ((), ()))
    tn_dims = (((0,), (0,)), ((), ()))

    def body(h_ref, lg_ref, ng_ref, y_ref, o_ref, st_ref, s_scr):
        @pl.when(pl.program_id(0) == 0)
        def _():
            s_scr[...] = jnp.zeros_like(s_scr)

        lb = _lb_from_logits(lg_ref)
        mats = _cum_mats()
        causal = _tri(False)
        st_ref[0] = s_scr[...]
        for h in range(HG_HEADS):
            sl = slice(h * HG_K, (h + 1) * HG_K)
            col = lambda part: slice(part * D + h * HG_K, part * D + (h + 1) * HG_K)
            q, k, g, _, _, _ = _hgrn_gates(h_ref[:, col(0)], h_ref[:, col(1)], lb[:, sl])
            vb = h_ref[:, col(2)].astype(BF)
            b, b_mid, b_last, b_last2 = _hgrn_decays(mats, g)
            qa = (q * jnp.exp(b - b_mid)).astype(BF)
            ka = (k * jnp.exp(b_mid - b)).astype(BF)
            qb = (q * jnp.exp(b)).astype(BF)
            kb = (k * jnp.exp(b_last - b)).astype(BF)
            st = s_scr[h]
            a = lax.dot_general(qa, ka, nt_dims, preferred_element_type=F32)
            a = jnp.where(causal, a, 0.0).astype(BF)
            o = (jnp.dot(a, vb, preferred_element_type=F32)
                 + lax.dot_general(qb, st.astype(BF), nt_dims, preferred_element_type=F32))
            s_scr[h] = jnp.exp(b_last2) * st + lax.dot_general(vb, kb, tn_dims, preferred_element_type=F32)
            o_ref[:, sl] = o
            on = o * lax.rsqrt(jnp.mean(o * o, axis=-1, keepdims=True) + EPS)
            y_ref[:, sl] = (on * ng_ref[:, sl] * _sig(h_ref[:, col(3)])).astype(BF)

    return _pcall(body, name="hgrn_fwd", grid=(nc,),
                  in_specs=[pl.BlockSpec((CH, 4 * D), lambda n: (n, 0)),
                            pl.BlockSpec((2, D), lambda n: (0, 0)),
                            pl.BlockSpec((1, D), lambda n: (0, 0))],
                  out_specs=[pl.BlockSpec((CH, D), lambda n: (n, 0)),
                             pl.BlockSpec((CH, D), lambda n: (n, 0)),
                             pl.BlockSpec((1, HG_HEADS, HG_K, HG_K), lambda n: (n, 0, 0, 0))],
                  out_shape=[jax.ShapeDtypeStruct((t, D), BF), jax.ShapeDtypeStruct((t, D), F32),
                             jax.ShapeDtypeStruct((nc, HG_HEADS, HG_K, HG_K), F32)],
                  scratch_shapes=[pltpu.VMEM((HG_HEADS, HG_K, HG_K), F32)],
                  compiler_params=_cp(("arbitrary",)))(h4, logits, norm_g)


def _hgrn_bwd(h4, logits, norm_g, o_pre, states, dy, *, t):
    nc = t // CH
    nt_dims = (((1,), (1,)), ((), ()))
    tn_dims = (((0,), (0,)), ((), ()))

    def body(h_ref, lg_ref, ng_ref, o_ref, st_ref, dy_ref, dh_ref, dlg_ref, dng_ref, ds_scr, dlb_scr):
        n = pl.program_id(0)

        @pl.when(n == 0)
        def _():
            ds_scr[...] = jnp.zeros_like(ds_scr)
            dlb_scr[...] = jnp.zeros_like(dlb_scr)
            dng_ref[...] = jnp.zeros_like(dng_ref)

        lb = _lb_from_logits(lg_ref)
        mats = _cum_mats()
        causal = _tri(False)
        suffix = jnp.where(_tri(True), 1.0, 0.0).astype(BF)
        last_row = lax.broadcasted_iota(jnp.int32, (CH, HG_K), 0) == CH - 1
        for h in range(HG_HEADS):
            sl = slice(h * HG_K, (h + 1) * HG_K)
            col = lambda part: slice(part * D + h * HG_K, part * D + (h + 1) * HG_K)
            hq = h_ref[:, col(0)]
            lbh = lb[:, sl]
            q, k, g, sq, sg, f = _hgrn_gates(hq, h_ref[:, col(1)], lbh)
            vb = h_ref[:, col(2)].astype(BF)
            b, b_mid, b_last, b_last2 = _hgrn_decays(mats, g)
            e_qa, e_ka, e_qb, e_kb = jnp.exp(b - b_mid), jnp.exp(b_mid - b), jnp.exp(b), jnp.exp(b_last - b)
            qa_f, ka_f, qb_f, kb_f = q * e_qa, k * e_ka, q * e_qb, k * e_kb
            qa, ka, qb, kb = qa_f.astype(BF), ka_f.astype(BF), qb_f.astype(BF), kb_f.astype(BF)
            ngh = ng_ref[:, sl]
            sgate = _sig(h_ref[:, col(3)])
            o = o_ref[:, sl]
            r = lax.rsqrt(jnp.mean(o * o, axis=-1, keepdims=True) + EPS)
            on = o * r
            dyh = dy_ref[:, sl]
            dh_ref[:, col(3)] = (dyh * on * ngh * sgate * (1.0 - sgate)).astype(BF)
            dng_ref[:, sl] += jnp.sum(dyh * on * sgate, axis=0, keepdims=True)
            don = dyh * ngh * sgate
            dob = (r * (don - on * jnp.mean(don * on, axis=-1, keepdims=True))).astype(BF)
            st = st_ref[0, h]
            dsn = ds_scr[h]
            stb, dsb = st.astype(BF), dsn.astype(BF)
            a = lax.dot_general(qa, ka, nt_dims, preferred_element_type=F32)
            a = jnp.where(causal, a, 0.0).astype(BF)
            da = lax.dot_general(dob, vb, nt_dims, preferred_element_type=F32)
            da = jnp.where(causal, da, 0.0).astype(BF)
            dv = (lax.dot_general(a, dob, tn_dims, preferred_element_type=F32)
                  + lax.dot_general(kb, dsb, nt_dims, preferred_element_type=F32))
            dqa = jnp.dot(da, ka, preferred_element_type=F32)
            dka = lax.dot_general(da, qa, tn_dims, preferred_element_type=F32)
            dqb = jnp.dot(dob, stb, preferred_element_type=F32)
            dkb = jnp.dot(vb, dsb, preferred_element_type=F32)
            dec2 = jnp.exp(b_last2)
            ds_scr[h] = lax.dot_general(dob, qb, tn_dims, preferred_element_type=F32) + dec2 * dsn
            dkb_kb = dkb * kb_f
            db_last = (jnp.sum(dkb_kb, axis=0, keepdims=True)
                       + jnp.sum(dec2 * st * dsn, axis=0, keepdims=True))
            db = dqa * qa_f - dka * ka_f + dqb * qb_f - dkb_kb + jnp.where(last_row, db_last, 0.0)
            dg = _apply01(suffix, db)
            dq = dqa * e_qa + dqb * e_qb
            dk = dka * e_ka + dkb * e_kb
            dh_ref[:, col(0)] = (dq * sq * (1.0 + hq * (1.0 - sq))).astype(BF)
            dh_ref[:, col(2)] = dv.astype(BF)
            dfk = dg / f - dk
            dh_ref[:, col(1)] = ((1.0 - lbh) * dfk * sg * (1.0 - sg)).astype(BF)
            dlb_scr[:, sl] += jnp.sum((1.0 - sg) * dfk, axis=0, keepdims=True)

        @pl.when(n == nc - 1)
        def _():
            dl0 = dlb_scr[...] * lb * (1.0 - lb)
            dlg_ref[0:1, :] = dl0
            dlg_ref[1:2, :] = -dl0

    rev = lambda n: (nc - 1 - n, 0)
    return _pcall(body, name="hgrn_bwd", grid=(nc,),
                  in_specs=[pl.BlockSpec((CH, 4 * D), rev),
                            pl.BlockSpec((2, D), lambda n: (0, 0)),
                            pl.BlockSpec((1, D), lambda n: (0, 0)),
                            pl.BlockSpec((CH, D), rev),
                            pl.BlockSpec((1, HG_HEADS, HG_K, HG_K), lambda n: (nc - 1 - n, 0, 0, 0)),
                            pl.BlockSpec((CH, D), rev)],
                  out_specs=[pl.BlockSpec((CH, 4 * D), rev),
                             pl.BlockSpec((2, D), lambda n: (0, 0)),
                             pl.BlockSpec((1, D), lambda n: (0, 0))],
                  out_shape=[jax.ShapeDtypeStruct((t, 4 * D), BF), jax.ShapeDtypeStruct((2, D), F32),
                             jax.ShapeDtypeStruct((1, D), F32)],
                  scratch_shapes=[pltpu.VMEM((HG_HEADS, HG_K, HG_K), F32), pltpu.VMEM((1, D), F32)],
                  compiler_params=_cp(("arbitrary",)))(h4, logits, norm_g, o_pre, states, dy)


def _place():
    x, y, c = lax.axis_index("x"), lax.axis_index("y"), lax.axis_index("c")
    return x, y, c, [(1 - x, y), (x, 1 - y), (1 - x, 1 - y)]


def _hbm_spec():
    return pl.BlockSpec(memory_space=pl.ANY)


def _all_gather_weights(shards):
    def body(*refs):
        ins, outs = refs[:N_W], refs[N_W:2 * N_W]
        send_sems, recv_sems, local_sems = refs[2 * N_W:]
        x, y, c, chips = _place()
        me, sib = (x, y, c), (x, y, 1 - c)

        def rows(w, dev):
            return outs[w].at[pl.ds((4 * dev[0] + 2 * dev[1] + dev[2]) * SLAB_R[w], SLAB_R[w]), :]

        def copy(kind, w, block, to, src=None):
            return pltpu.make_async_remote_copy(
                src_ref=rows(w, block) if src is None else src, dst_ref=rows(w, block),
                send_sem=send_sems.at[kind], recv_sem=recv_sems.at[kind], device_id=to, device_id_type=MESH)

        def all_of(kind):
            whole = outs[0].at[pl.ds(0, SLAB_ROWS), :]
            return pltpu.make_async_remote_copy(
                src_ref=whole, dst_ref=whole, send_sem=send_sems.at[kind], recv_sem=recv_sems.at[kind],
                device_id=me, device_id_type=MESH)

        mine = [pltpu.make_async_copy(ins[w], rows(w, me), local_sems.at[w]) for w in range(N_W)]
        for cp in mine:
            cp.start()
        for w in range(N_W):
            copy(0, w, me, sib, src=ins[w]).start()
            for j, chip in enumerate(chips):
                copy(1 + j, w, me, (*chip, c), src=ins[w]).start()
        for j, chip in enumerate(chips):
            all_of(1 + j).wait_recv()
            for w in range(N_W):
                copy(4 + j, w, (*chip, c), sib).start()
        all_of(0).wait_recv()
        for j in range(3):
            all_of(4 + j).wait_recv()
        for kind in range(7):
            all_of(kind).wait_send()
        for cp in mine:
            cp.wait()

    total = (IN_W, FFN, FFN, FFN, D, D, D)
    return _pcall(body, name="all_gather_weights",
                  in_specs=[_hbm_spec()] * N_W, out_specs=[_hbm_spec()] * N_W,
                  out_shape=[jax.ShapeDtypeStruct((r, D), BF) for r in total],
                  scratch_shapes=[pltpu.SemaphoreType.DMA((7,)), pltpu.SemaphoreType.DMA((7,)),
                                  pltpu.SemaphoreType.DMA((N_W,))],
                  compiler_params=pltpu.CompilerParams(has_side_effects=True))(*shards)


def _rs_pair_exchange(grads):
    def body(*refs):
        ins = refs[:N_W]
        kept, got, send_sem, recv_sem, local_sem = refs[N_W:]
        x, y, c, _ = _place()
        for a in range(N_CHIP):
            for w in range(N_W):
                dst = pl.ds(SLAB_OFF[w], SLAB_R[w])
                pltpu.make_async_remote_copy(
                    src_ref=ins[w].at[pl.ds((2 * a + 1 - c) * SLAB_R[w], SLAB_R[w]), :],
                    dst_ref=got.at[a, dst, :], send_sem=send_sem, recv_sem=recv_sem,
                    device_id=(x, y, 1 - c), device_id_type=MESH).start()
                pltpu.make_async_copy(ins[w].at[pl.ds((2 * a + c) * SLAB_R[w], SLAB_R[w]), :],
                                      kept.at[a, dst, :], local_sem).start()
        pltpu.make_async_remote_copy(src_ref=got, dst_ref=got, send_sem=send_sem, recv_sem=recv_sem,
                                     device_id=(x, y, c), device_id_type=MESH).wait()
        pltpu.make_async_copy(kept, kept, local_sem).wait()

    shape = jax.ShapeDtypeStruct((N_CHIP, SLAB_ROWS, D), BF)
    return _pcall(body, name="rs_pair_exchange", in_specs=[_hbm_spec()] * N_W,
                  out_specs=[_hbm_spec()] * 2, out_shape=[shape, shape],
                  scratch_shapes=[pltpu.SemaphoreType.DMA, pltpu.SemaphoreType.DMA, pltpu.SemaphoreType.DMA],
                  compiler_params=pltpu.CompilerParams(has_side_effects=True))(*grads)


def _add_bf16(a, b, *, tr):
    rows = a.shape[0]

    def body(a_ref, b_ref, o_ref):
        o_ref[...] = (a_ref[...].astype(F32) + b_ref[...].astype(F32)).astype(BF)

    return _pcall(body, name="rs_pair_add", grid=(rows // tr,), in_specs=[_row_spec(tr, D)] * 2,
                  out_specs=_row_spec(tr, D), out_shape=jax.ShapeDtypeStruct((rows, D), BF),
                  compiler_params=_cp(("parallel",)))(a, b)


def _rs_chip_exchange(pair_sums):
    def body(ps, out, send_sems, recv_sems, local_sem):
        x, y, c, chips = _place()
        my_chip = 2 * x + y
        own = pltpu.make_async_copy(ps.at[my_chip], out.at[my_chip], local_sem)
        own.start()
        cps = [pltpu.make_async_remote_copy(
            src_ref=ps.at[2 * chip[0] + chip[1]], dst_ref=out.at[my_chip], send_sem=send_sems.at[j],
            recv_sem=recv_sems.at[j], device_id=(*chip, c), device_id_type=MESH)
            for j, chip in enumerate(chips)]
        for cp in cps:
            cp.start()
        for cp in cps:
            cp.wait()
        own.wait()

    return _pcall(body, name="rs_chip_exchange", in_specs=[_hbm_spec()], out_specs=_hbm_spec(),
                  out_shape=jax.ShapeDtypeStruct((N_CHIP, SLAB_ROWS, D), BF),
                  scratch_shapes=[pltpu.SemaphoreType.DMA((3,)), pltpu.SemaphoreType.DMA((3,)),
                                  pltpu.SemaphoreType.DMA],
                  compiler_params=pltpu.CompilerParams(has_side_effects=True))(pair_sums)


def _sum_chips(parts, *, tr):
    def body(p_ref, o_ref):
        acc = p_ref[0].astype(F32)
        for a in range(1, N_CHIP):
            acc = acc + p_ref[a].astype(F32)
        o_ref[...] = acc

    return _pcall(body, name="rs_sum_chips", grid=(SLAB_ROWS // tr,),
                  in_specs=[pl.BlockSpec((N_CHIP, tr, D), lambda i: (0, i, 0))],
                  out_specs=_row_spec(tr, D), out_shape=jax.ShapeDtypeStruct((SLAB_ROWS, D), F32),
                  compiler_params=_cp(("parallel",)))(parts)


def _adam_math(w, g, m, v):
    m = ADAM_B1 * m + (1.0 - ADAM_B1) * g
    v = ADAM_B2 * v + (1.0 - ADAM_B2) * (g * g)
    m_hat = m / (1.0 - ADAM_B1 ** ADAM_STEP)
    v_hat = v / (1.0 - ADAM_B2 ** ADAM_STEP)
    delta = -ADAM_LR * (m_hat / (jnp.sqrt(v_hat) + ADAM_EPS) + ADAM_WD * w)
    return delta, m, v


def _small_allreduce_adam(gpart, w, m, v):
    def body(g_ref, w_ref, m_ref, v_ref, gs_ref, d_ref, mo_ref, vo_ref, gath, send_sems, recv_sems):
        x, y, c, _ = _place()
        me = 4 * x + 2 * y + c
        gath[me] = g_ref[...]
        cps = []
        for d in range(1, N_DEV):
            peer = (x ^ (d >> 2), y ^ ((d >> 1) & 1), c ^ (d & 1))
            cps.append(pltpu.make_async_remote_copy(
                src_ref=g_ref, dst_ref=gath.at[me], send_sem=send_sems.at[d - 1],
                recv_sem=recv_sems.at[d - 1], device_id=peer, device_id_type=MESH))
        for cp in cps:
            cp.start()
        for cp in cps:
            cp.wait()
        g = gath[0]
        for k in range(1, N_DEV):
            g = g + gath[k]
        gs_ref[...] = g
        d_ref[...], mo_ref[...], vo_ref[...] = _adam_math(w_ref[...], g, m_ref[...], v_ref[...])

    shape = jax.ShapeDtypeStruct((SMALL_ROWS, D), F32)
    vm = pl.BlockSpec(memory_space=pltpu.VMEM)
    return _pcall(body, name="small_allreduce_adam", in_specs=[vm] * 4, out_specs=[vm] * 4,
                  out_shape=[shape] * 4,
                  scratch_shapes=[pltpu.VMEM((N_DEV, SMALL_ROWS, D), F32),
                                  pltpu.SemaphoreType.DMA((N_DEV - 1,)), pltpu.SemaphoreType.DMA((N_DEV - 1,))],
                  compiler_params=pltpu.CompilerParams(has_side_effects=True))(gpart, w, m, v)


def _adam(w, g, m, v, *, name):
    rows, cols = w.shape
    tr = rows if rows <= 512 else 256

    def body(w_ref, g_ref, m_ref, v_ref, d_ref, mo_ref, vo_ref):
        d_ref[...], mo_ref[...], vo_ref[...] = _adam_math(w_ref[...], g_ref[...], m_ref[...], v_ref[...])

    spec = pl.BlockSpec((tr, cols), lambda i: (i, 0))
    return _pcall(body, name=name, grid=(rows // tr,), in_specs=[spec] * 4, out_specs=[spec] * 3,
                  out_shape=[jax.ShapeDtypeStruct((rows, cols), F32)] * 3,
                  compiler_params=_cp(("parallel",)))(w, g, m, v)


def _local_step(x, tgt, weights, norm_mix_g, b_in, sinks, logits, hgrn_norm_g, norm_ffn_g, norm_final_g):
    win_t, wg_t, wu_t, wd, wba, wbh, wout = weights
    t = x.shape[0]
    big = dict(tm=1024, tn=1024, tk=4096)

    u1 = _rms_fwd(x, norm_mix_g, tm=512, name="rms_mix")
    q, kv, h4, gates = _inproj_fwd(u1, win_t, b_in, t=t)
    y_attn = _attn_fwd(q, kv, sinks, t=t)
    y_hgrn, o_pre, states = _hgrn_fwd(h4, logits, hgrn_norm_g, t=t)
    ya = _mm(y_attn, wba, m=t, n=D, k=D, name="branch_attn", **big)
    yb = _mm(y_hgrn, wbh, m=t, n=D, k=D, name="branch_hgrn", **big)
    merged = _merge_fwd(gates, ya, yb, tm=512)
    h1 = _mm(merged, wout, m=t, n=D, k=D, resid=x, name="out_proj", **big)
    u2 = _rms_fwd(h1, norm_ffn_g, tm=512, name="rms_ffn")
    gt = _mm(u2, wg_t, m=t, n=FFN, k=D, tb=True, tm=1024, tn=FFN // 2, tk=D, name="ffn_gate")
    up = _mm(u2, wu_t, m=t, n=FFN, k=D, tb=True, tm=1024, tn=FFN // 2, tk=D, name="ffn_up")
    z = _swiglu_fwd(gt, up, tm=256)
    h2 = _mm(z, wd, m=t, n=D, k=FFN, resid=h1, name="ffn_down", **big)
    dh2, dh2_b, d_norm_final, loss_row = _loss_head(h2, tgt, norm_final_g, tm=512)

    dz = _mm(dh2_b, wd, m=t, n=FFN, k=D, tb=True, tm=1024, tn=FFN // 2, tk=D, name="d_z")
    d_wd = _mm(z, dh2_b, m=FFN, n=D, k=t, ta=True, tm=256, tn=D, tk=4096, out_dtype=BF, name="d_w_down")
    dgt, dup = _swiglu_bwd(dz, gt, up, tm=256)
    du2 = _mm(dgt, wg_t, m=t, n=D, k=FFN, name="d_u2_gate", **big)
    du2 = _mm(dup, wu_t, m=t, n=D, k=FFN, resid=du2, name="d_u2_up", **big)
    d_wg = _mm(dgt, u2, m=FFN, n=D, k=t, ta=True, tm=256, tn=D, tk=4096, out_dtype=BF, name="d_w_gate")
    d_wu = _mm(dup, u2, m=FFN, n=D, k=t, ta=True, tm=256, tn=D, tk=4096, out_dtype=BF, name="d_w_up")
    dh1, dh1_b, d_norm_ffn = _rms_bwd(du2, h1, norm_ffn_g, dh2, tm=512, name="rms_ffn_bwd")
    dmerged = _mm(dh1_b, wout, m=t, n=D, k=D, tb=True, name="d_merged", **big)
    d_wout = _mm(merged, dh1_b, m=D, n=D, k=t, ta=True, tm=256, tn=D, tk=4096, out_dtype=BF, name="d_w_out")
    dya, dyb, dgates = _merge_bwd(dmerged, gates, ya, yb, tm=512)
    dy_attn = _mm(dya, wba, m=t, n=D, k=D, tb=True, out_dtype=BF, name="d_y_attn", **big)
    dy_hgrn = _mm(dyb, wbh, m=t, n=D, k=D, tb=True, name="d_y_hgrn", **big)
    d_wba = _mm(y_attn, dya, m=D, n=D, k=t, ta=True, tm=256, tn=D, tk=4096, out_dtype=BF, name="d_w_ba")
    d_wbh = _mm(y_hgrn, dyb, m=D, n=D, k=t, ta=True, tm=256, tn=D, tk=4096, out_dtype=BF, name="d_w_bh")
    dq, dkv, d_sinks = _attn_bwd(q, kv, sinks, dy_attn, t=t)
    dh4, d_logits, d_hgrn_norm = _hgrn_bwd(h4, logits, hgrn_norm_g, o_pre, states, dy_hgrn, t=t)
    dps = (dq, dkv, dh4, dgates)
    du1 = _inproj_bwd_x(dps, win_t, t=t, tm=2048)
    d_win_t, d_b_in = _inproj_bwd_w(dps, u1, t=t)
    grad_x, _, d_norm_mix = _rms_bwd(du1, x, norm_mix_g, dh1, tm=512, name="rms_mix_bwd")

    big_grads = (d_win_t, d_wg, d_wu, d_wd, d_wba, d_wbh, d_wout)
    small_grads = (d_norm_mix, d_b_in, d_sinks, d_logits, d_hgrn_norm, d_norm_ffn, d_norm_final)
    return loss_row, grad_x, big_grads, small_grads


def _pack_small(norm_mix, b_in, sinks, logits, hgrn_norm, norm_ffn, norm_final, extra=None):
    pad = lambda a, n: jnp.pad(a.reshape(1, -1), ((0, 0), (0, n - a.size)))
    rows = [norm_mix.reshape(1, D), hgrn_norm.reshape(1, D), norm_ffn.reshape(1, D), norm_final.reshape(1, D),
            logits.reshape(2, D), pad(sinks.reshape(-1)[:16], D),
            jnp.zeros((1, D), F32) if extra is None else pad(extra, D),
            pad(b_in, 8 * D).reshape(8, D)]
    return jnp.concatenate(rows, axis=0).astype(F32)


def _unpack_small(p):
    return dict(norm_mix_g=p[0:1], hgrn_norm_g=p[1:2], norm_ffn_g=p[2:3], norm_final_g=p[3],
                hgrn_lb_logits=p[4:6], attn_sinks=p[6:7, 0:16], extra=p[7],
                b_in=p[8:16].reshape(1, 8 * D)[:, :IN_W])


def kernel(x, norm_mix_g, w_in, b_in, attn_sinks, hgrn_lb_logits, hgrn_norm_g, w_branch_attn, w_branch_hgrn, w_out, norm_ffn_g, w_ffn_gate, w_ffn_up, w_ffn_down, norm_final_g, loss_target, m_norm_mix_g, m_w_in, m_b_in, m_attn_sinks, m_hgrn_lb_logits, m_hgrn_norm_g, m_w_branch_attn, m_w_branch_hgrn, m_w_out, m_norm_ffn_g, m_w_ffn_gate, m_w_ffn_up, m_w_ffn_down, m_norm_final_g, v_norm_mix_g, v_w_in, v_b_in, v_attn_sinks, v_hgrn_lb_logits, v_hgrn_norm_g, v_w_branch_attn, v_w_branch_hgrn, v_w_out, v_norm_ffn_g, v_w_ffn_gate, v_w_ffn_up, v_w_ffn_down, v_norm_final_g):
    shards = [w_in[0].T.astype(BF), w_ffn_gate[0].T.astype(BF), w_ffn_up[0].T.astype(BF),
              w_ffn_down[0].astype(BF), w_branch_attn[0].astype(BF), w_branch_hgrn[0].astype(BF),
              w_out[0].astype(BF)]
    weights = _all_gather_weights(shards)

    loss_row, grad_x, big_grads, small_grads = _local_step(
        x[0], loss_target[0], weights, norm_mix_g, b_in, attn_sinks, hgrn_lb_logits, hgrn_norm_g,
        norm_ffn_g, norm_final_g.reshape(1, D))

    kept, got = _rs_pair_exchange(big_grads)
    pair = _add_bf16(kept.reshape(N_CHIP * SLAB_ROWS, D), got.reshape(N_CHIP * SLAB_ROWS, D), tr=256)
    parts = _rs_chip_exchange(pair.reshape(N_CHIP, SLAB_ROWS, D))
    g_slab = _sum_chips(parts, tr=592)

    d_norm_mix, d_b_in, d_sinks, d_logits, d_hgrn_norm, d_norm_ffn, d_norm_final = small_grads
    g_small = _pack_small(d_norm_mix, d_b_in, d_sinks[:, :16], d_logits, d_hgrn_norm, d_norm_ffn,
                          d_norm_final, extra=loss_row[0, 0:1])
    w_small = _pack_small(norm_mix_g, b_in, attn_sinks, hgrn_lb_logits, hgrn_norm_g, norm_ffn_g, norm_final_g)
    m_small = _pack_small(m_norm_mix_g, m_b_in, m_attn_sinks, m_hgrn_lb_logits, m_hgrn_norm_g, m_norm_ffn_g,
                          m_norm_final_g)
    v_small = _pack_small(v_norm_mix_g, v_b_in, v_attn_sinks, v_hgrn_lb_logits, v_hgrn_norm_g, v_norm_ffn_g,
                          v_norm_final_g)
    small = [_unpack_small(p) for p in _small_allreduce_adam(g_small, w_small, m_small, v_small)]
    loss = small[0]["extra"][0]

    names = ["w_in", "w_ffn_gate", "w_ffn_up", "w_ffn_down", "w_branch_attn", "w_branch_hgrn", "w_out"]
    w_full = dict(w_in=(w_in, m_w_in, v_w_in), w_ffn_gate=(w_ffn_gate, m_w_ffn_gate, v_w_ffn_gate),
                  w_ffn_up=(w_ffn_up, m_w_ffn_up, v_w_ffn_up), w_ffn_down=(w_ffn_down, m_w_ffn_down, v_w_ffn_down),
                  w_branch_attn=(w_branch_attn, m_w_branch_attn, v_w_branch_attn),
                  w_branch_hgrn=(w_branch_hgrn, m_w_branch_hgrn, v_w_branch_hgrn),
                  w_out=(w_out, m_w_out, v_w_out))
    big = {}
    for i, name in enumerate(names):
        g = g_slab[SLAB_OFF[i]:SLAB_OFF[i] + SLAB_R[i]]
        if i < 3:
            g = g.T
        wv, mv, vv = w_full[name]
        delta, new_m, new_v = _adam(wv[0], g, mv[0], vv[0], name="adam_" + name)
        big[name] = [a[None] for a in (g, delta, new_m, new_v)]

    order = ["norm_mix_g", "w_in", "b_in", "attn_sinks", "hgrn_lb_logits", "hgrn_norm_g", "w_branch_attn",
             "w_branch_hgrn", "w_out", "norm_ffn_g", "w_ffn_gate", "w_ffn_up", "w_ffn_down", "norm_final_g"]
    outs = [loss, grad_x[None]]
    for kind in range(4):
        for name in order:
            outs.append(big[name][kind] if name in big else small[kind][name])
    return tuple(outs)
```

```python
import math

import jax
import jax.numpy as jnp
from jax import lax
from jax.experimental import pallas as pl
from jax.experimental.pallas import tpu as pltpu

F32 = jnp.float32
BF = jnp.bfloat16
MESH = pl.DeviceIdType.MESH

D = 1024
HEAD = 64
N_PAIR = 8
BLK = 128
CH = 64
HG_HEADS = 8
HG_K = 128
FFN = 2816
IN_W = 7424
N_DEV = 8
N_CHIP = 4
EPS = 1e-6
NEG = -1e30
SCALE = 1.0 / math.sqrt(HEAD)
VMEM_LIMIT = 56 * 1024 * 1024
WT = 256

ADAM_LR, ADAM_B1, ADAM_B2, ADAM_EPS, ADAM_WD, ADAM_STEP = 0.001, 0.9, 0.999, 1e-08, 0.01, 10

SLAB_R = (IN_W // N_DEV, FFN // N_DEV, FFN // N_DEV, FFN // N_DEV, D // N_DEV, D // N_DEV, D // N_DEV)
SLAB_ROWS = sum(SLAB_R)
SLAB_OFF = tuple(sum(SLAB_R[:i]) for i in range(len(SLAB_R)))
N_W = len(SLAB_R)
GRP_OFF = (0, D // WT, (D + 256) // WT, (5 * D + 256) // WT)
GRP_N = (D // WT, 256 // WT, 4 * D // WT, 2 * D // WT)
SMALL_ROWS = 16


def _pcall(body, **kw):
    return pl.pallas_call(body, **kw)


def _cp(sem=None, **kw):
    return pltpu.CompilerParams(dimension_semantics=sem, vmem_limit_bytes=VMEM_LIMIT, **kw)


def _sig(v):
    return 1.0 / (1.0 + jnp.exp(-v))


def _accum(ref, val, first):
    @pl.when(first)
    def _():
        ref[...] = val

    @pl.when(jnp.logical_not(first))
    def _():
        ref[...] += val


class _Comm:
    def __init__(self, ins, out_shapes, sem_shapes, phases):
        self.ins, self.out_shapes, self.sem_shapes, self.phases = list(ins), list(out_shapes), list(sem_shapes), phases


def _host(body, comm, n_in, n_out, n_scr, nsteps, step_fn):
    if comm is None:
        return body
    ci, co = len(comm.ins), len(comm.out_shapes)

    def wrapped(*refs):
        p = 0
        ins, p = refs[p:p + n_in], p + n_in
        cins, p = refs[p:p + ci], p + ci
        outs, p = refs[p:p + n_out], p + n_out
        couts, p = refs[p:p + co], p + co
        scr, p = refs[p:p + n_scr], p + n_scr
        csems = refs[p:]
        step = step_fn()
        for frac, fn in comm.phases:
            if frac < 1.0:
                @pl.when(step == int(round(frac * (nsteps - 1))))
                def _(fn=fn):
                    fn(cins, couts, csems)
        body(*ins, *outs, *scr)
        for frac, fn in comm.phases:
            if frac >= 1.0:
                @pl.when(step == nsteps - 1)
                def _(fn=fn):
                    fn(cins, couts, csems)

    return wrapped


def _hosted_call(body, comm, args, *, name, grid, in_specs, out_specs, out_shape, scratch_shapes, sem,
                 nsteps, step_fn, aliases=None):
    n_in, n_out, n_scr = len(in_specs), len(out_specs), len(scratch_shapes)
    args = list(args)
    extra = {}
    if comm is not None:
        in_specs = list(in_specs) + [_hbm_spec()] * len(comm.ins)
        out_specs = list(out_specs) + [_hbm_spec()] * len(comm.out_shapes)
        out_shape = list(out_shape) + comm.out_shapes
        scratch_shapes = list(scratch_shapes) + comm.sem_shapes
        args += comm.ins
        extra = dict(has_side_effects=True)
    outs = _pcall(_host(body, comm, n_in, n_out, n_scr, nsteps, step_fn), name=name, grid=grid,
                  in_specs=in_specs, out_specs=out_specs, out_shape=out_shape, scratch_shapes=scratch_shapes,
                  input_output_aliases=aliases or {}, compiler_params=_cp(sem, **extra))(*args)
    return list(outs[:n_out]), list(outs[n_out:])


def _run_comm(comm, *, name):
    ci, co = len(comm.ins), len(comm.out_shapes)

    def body(*refs):
        for _, fn in comm.phases:
            fn(refs[:ci], refs[ci:ci + co], refs[ci + co:])

    return _pcall(body, name=name, in_specs=[_hbm_spec()] * ci, out_specs=[_hbm_spec()] * co,
                  out_shape=comm.out_shapes, scratch_shapes=comm.sem_shapes,
                  compiler_params=pltpu.CompilerParams(has_side_effects=True))(*comm.ins)


def _hbm_spec():
    return pl.BlockSpec(memory_space=pl.ANY)


def _mm(a, b, *, m, n, k, tm, tn, tk, ta=False, tb=False, out_dtype=F32, resid=None, name):
    tm, tn, tk = min(tm, m), min(tn, n), min(tk, k)
    gm, gn, gk = m // tm, n // tn, k // tk
    assert gm * tm == m and gn * tn == n and gk * tk == k, (name, m, n, k, tm, tn, tk)
    a_spec = (pl.BlockSpec((tk, tm), lambda i, j, l: (l, i)) if ta
              else pl.BlockSpec((tm, tk), lambda i, j, l: (i, l)))
    b_spec = (pl.BlockSpec((tn, tk), lambda i, j, l: (j, l)) if tb
              else pl.BlockSpec((tk, tn), lambda i, j, l: (l, j)))
    dims = (((0 if ta else 1,), (1 if tb else 0,)), ((), ()))
    ins, in_specs = [a, b], [a_spec, b_spec]
    if resid is not None:
        ins.append(resid)
        in_specs.append(pl.BlockSpec((tm, tn), lambda i, j, l: (i, j)))
    scratch = [pltpu.VMEM((tm, tn), F32)] if gk > 1 else []

    def body(*refs):
        it = iter(refs)
        a_ref, b_ref = next(it), next(it)
        resid_ref = next(it) if resid is not None else None
        o_ref = next(it)
        acc_ref = next(it) if gk > 1 else None
        l = pl.program_id(2)
        part = lax.dot_general(a_ref[...].astype(BF), b_ref[...].astype(BF), dims,
                               preferred_element_type=F32)

        def finish(acc):
            if resid_ref is not None:
                acc = acc + resid_ref[...].astype(F32)
            o_ref[...] = acc.astype(out_dtype)

        if gk == 1:
            finish(part)
        else:
            _accum(acc_ref, part, l == 0)

            @pl.when(l == gk - 1)
            def _():
                finish(acc_ref[...])

    return _pcall(body, name=name, grid=(gm, gn, gk), in_specs=in_specs,
                  out_specs=pl.BlockSpec((tm, tn), lambda i, j, l: (i, j)),
                  out_shape=jax.ShapeDtypeStruct((m, n), out_dtype), scratch_shapes=scratch,
                  compiler_params=_cp(("parallel", "parallel", "arbitrary")))(*ins)


def _grp_of(i):
    return [jnp.logical_and(i >= GRP_OFF[g], i < GRP_OFF[g] + GRP_N[g]) for g in range(4)]


def _grp_idx(i, g):
    return jnp.clip(i - GRP_OFF[g], 0, GRP_N[g] - 1)


def _inproj_fwd(u, win_t, b_in, *, t, comm=None):
    n_tiles = IN_W // WT
    dims = (((1,), (1,)), ((), ()))
    dtypes = (BF, BF, F32, F32)

    def body(u_ref, w_ref, b_ref, *o_refs):
        i = pl.program_id(0)
        p = lax.dot_general(u_ref[...], w_ref[...], dims, preferred_element_type=F32) + b_ref[...]
        for g, pred in enumerate(_grp_of(i)):
            @pl.when(pred)
            def _(g=g):
                o_refs[g][...] = p.astype(dtypes[g])

    return _hosted_call(
        body, comm, (u, win_t, b_in), name="inproj_fwd", grid=(n_tiles,),
        in_specs=[pl.BlockSpec((t, D), lambda i: (0, 0)),
                  pl.BlockSpec((WT, D), lambda i: (i, 0)),
                  pl.BlockSpec((1, WT), lambda i: (0, i))],
        out_specs=[pl.BlockSpec((t, WT), lambda i, g=g: (0, _grp_idx(i, g))) for g in range(4)],
        out_shape=[jax.ShapeDtypeStruct((t, GRP_N[g] * WT), dtypes[g]) for g in range(4)],
        scratch_shapes=[], sem=("arbitrary",), nsteps=n_tiles, step_fn=lambda: pl.program_id(0))


def _inproj_bwd_x(dps, win_t, *, t, part, into=None, comm=None):
    n_tiles = IN_W // WT
    tm = t // 2

    def body(d0, d1, d2, d3, w_ref, *rest):
        o_ref, acc_ref = rest[-2], rest[-1]
        l = pl.program_id(0)
        w = w_ref[...]
        for g, (pred, d_ref) in enumerate(zip(_grp_of(l), (d0, d1, d2, d3))):
            @pl.when(pred)
            def _(d_ref=d_ref):
                _accum(acc_ref, jnp.dot(d_ref[...], w, preferred_element_type=F32), l == 0)

        @pl.when(l == n_tiles - 1)
        def _():
            o_ref[...] = acc_ref[...]

    in_specs = ([pl.BlockSpec((tm, WT), lambda l, g=g: (part, _grp_idx(l, g))) for g in range(4)]
                + [pl.BlockSpec((WT, D), lambda l: (l, 0))])
    args = list(dps) + [win_t]
    aliases = None
    if into is not None:
        in_specs.append(_hbm_spec())
        args.append(into)
        aliases = {5: 0}
    return _hosted_call(
        body, comm, args, name="inproj_bwd_x%d" % part, grid=(n_tiles,), in_specs=in_specs,
        out_specs=[pl.BlockSpec((tm, D), lambda l: (part, 0))],
        out_shape=[jax.ShapeDtypeStruct((t, D), F32)],
        scratch_shapes=[pltpu.VMEM((tm, D), F32)], sem=("arbitrary",), nsteps=n_tiles,
        step_fn=lambda: pl.program_id(0), aliases=aliases)


def _inproj_bwd_w(dps, u, *, t):
    n_tiles = IN_W // WT
    dims = (((0,), (0,)), ((), ()))

    def body(d0, d1, d2, d3, u_ref, o_ref, db_ref):
        i = pl.program_id(0)
        uv = u_ref[...]
        for g, (pred, d_ref) in enumerate(zip(_grp_of(i), (d0, d1, d2, d3))):
            @pl.when(pred)
            def _(d_ref=d_ref):
                dv = d_ref[...]
                o_ref[...] = lax.dot_general(dv, uv, dims, preferred_element_type=F32).astype(BF)
                db_ref[...] = jnp.sum(dv.astype(F32), axis=0, keepdims=True)

    return _pcall(body, name="inproj_bwd_w", grid=(n_tiles,),
                  in_specs=[pl.BlockSpec((t, WT), lambda i, g=g: (0, _grp_idx(i, g))) for g in range(4)]
                  + [pl.BlockSpec((t, D), lambda i: (0, 0))],
                  out_specs=[pl.BlockSpec((WT, D), lambda i: (i, 0)),
                             pl.BlockSpec((1, WT), lambda i: (0, i))],
                  out_shape=[jax.ShapeDtypeStruct((IN_W, D), BF), jax.ShapeDtypeStruct((1, IN_W), F32)],
                  compiler_params=_cp(("arbitrary",)))(*dps, u)


def _row_spec(tm, width, col=0):
    return pl.BlockSpec((tm, width), lambda i: (i, col))


def _vec_spec(width):
    return pl.BlockSpec((1, width), lambda i: (0, 0))


def _rms_fwd(x, g, *, tm, name):
    t = x.shape[0]
    tm = min(tm, t)

    def body(x_ref, g_ref, u_ref):
        xv = x_ref[...]
        r = lax.rsqrt(jnp.mean(xv * xv, axis=-1, keepdims=True) + EPS)
        u_ref[...] = (xv * r * g_ref[...]).astype(BF)

    return _pcall(body, name=name, grid=(t // tm,), in_specs=[_row_spec(tm, D), _vec_spec(D)],
                  out_specs=_row_spec(tm, D), out_shape=jax.ShapeDtypeStruct((t, D), BF),
                  compiler_params=_cp(("parallel",)))(x, g)


def _rms_bwd(du, x, g, resid, *, tm, name):
    t = x.shape[0]
    tm = min(tm, t)

    def body(du_ref, x_ref, g_ref, r_ref, dx_ref, dxb_ref, dg_ref):
        xv = x_ref[...]
        r = lax.rsqrt(jnp.mean(xv * xv, axis=-1, keepdims=True) + EPS)
        xh = xv * r
        duv = du_ref[...]
        dxh = duv * g_ref[...]
        dx = r_ref[...] + r * (dxh - xh * jnp.mean(dxh * xh, axis=-1, keepdims=True))
        dx_ref[...] = dx
        dxb_ref[...] = dx.astype(BF)
        _accum(dg_ref, jnp.sum(duv * xh, axis=0, keepdims=True), pl.program_id(0) == 0)

    return _pcall(body, name=name, grid=(t // tm,),
                  in_specs=[_row_spec(tm, D), _row_spec(tm, D), _vec_spec(D), _row_spec(tm, D)],
                  out_specs=[_row_spec(tm, D), _row_spec(tm, D), _vec_spec(D)],
                  out_shape=[jax.ShapeDtypeStruct((t, D), F32), jax.ShapeDtypeStruct((t, D), BF),
                             jax.ShapeDtypeStruct((1, D), F32)],
                  compiler_params=_cp(("arbitrary",)))(du, x, g, resid)


def _loss_head(h2, tgt, g, *, tm):
    t = h2.shape[0]
    tm = min(tm, t)

    def body(h_ref, t_ref, g_ref, dh_ref, dhb_ref, dg_ref, loss_ref):
        hv = h_ref[...]
        gv = g_ref[...]
        r = lax.rsqrt(jnp.mean(hv * hv, axis=-1, keepdims=True) + EPS)
        xh = hv * r
        err = xh * gv - t_ref[...]
        lp = jnp.sum(jnp.sum(err * err, axis=1, keepdims=True), axis=0, keepdims=True) * (0.5 / D)
        dy = err * (1.0 / D)
        dxh = dy * gv
        dh = r * (dxh - xh * jnp.mean(dxh * xh, axis=-1, keepdims=True))
        dh_ref[...] = dh
        dhb_ref[...] = dh.astype(BF)
        first = pl.program_id(0) == 0
        _accum(dg_ref, jnp.sum(dy * xh, axis=0, keepdims=True), first)
        _accum(loss_ref, jnp.broadcast_to(lp, (1, 128)), first)

    return _pcall(body, name="loss_head", grid=(t // tm,),
                  in_specs=[_row_spec(tm, D), _row_spec(tm, D), _vec_spec(D)],
                  out_specs=[_row_spec(tm, D), _row_spec(tm, D), _vec_spec(D), _vec_spec(128)],
                  out_shape=[jax.ShapeDtypeStruct((t, D), F32), jax.ShapeDtypeStruct((t, D), BF),
                             jax.ShapeDtypeStruct((1, D), F32), jax.ShapeDtypeStruct((1, 128), F32)],
                  compiler_params=_cp(("arbitrary",)))(h2, tgt, g)


def _merge_fwd(gates, ya, yb, *, tm):
    t = ya.shape[0]
    tm = min(tm, t)

    def body(ga_ref, gb_ref, ya_ref, yb_ref, o_ref):
        o_ref[...] = (_sig(ga_ref[...]) * ya_ref[...] + _sig(gb_ref[...]) * yb_ref[...]).astype(BF)

    return _pcall(body, name="merge_fwd", grid=(t // tm,),
                  in_specs=[_row_spec(tm, D, 0), _row_spec(tm, D, 1), _row_spec(tm, D), _row_spec(tm, D)],
                  out_specs=_row_spec(tm, D), out_shape=jax.ShapeDtypeStruct((t, D), BF),
                  compiler_params=_cp(("parallel",)))(gates, gates, ya, yb)


def _merge_bwd(dm, gates, ya, yb, *, tm):
    t = ya.shape[0]
    tm = min(tm, t)

    def body(dm_ref, ga_ref, gb_ref, ya_ref, yb_ref, dya_ref, dyb_ref, dg_ref):
        dmv = dm_ref[...]
        sa, sb = _sig(ga_ref[...]), _sig(gb_ref[...])
        dya_ref[...] = (dmv * sa).astype(BF)
        dyb_ref[...] = (dmv * sb).astype(BF)
        dg_ref[:, 0:D] = (dmv * ya_ref[...] * sa * (1.0 - sa)).astype(BF)
        dg_ref[:, D:2 * D] = (dmv * yb_ref[...] * sb * (1.0 - sb)).astype(BF)

    return _pcall(body, name="merge_bwd", grid=(t // tm,),
                  in_specs=[_row_spec(tm, D), _row_spec(tm, D, 0), _row_spec(tm, D, 1),
                            _row_spec(tm, D), _row_spec(tm, D)],
                  out_specs=[_row_spec(tm, D), _row_spec(tm, D), _row_spec(tm, 2 * D)],
                  out_shape=[jax.ShapeDtypeStruct((t, D), BF), jax.ShapeDtypeStruct((t, D), BF),
                             jax.ShapeDtypeStruct((t, 2 * D), BF)],
                  compiler_params=_cp(("parallel",)))(dm, gates, gates, ya, yb)


def _swiglu_fwd(gt, up, *, tm):
    t = gt.shape[0]
    tm = min(tm, t)

    def body(g_ref, u_ref, z_ref):
        gv = g_ref[...]
        z_ref[...] = (gv * _sig(gv) * u_ref[...]).astype(BF)

    return _pcall(body, name="swiglu_fwd", grid=(t // tm,),
                  in_specs=[_row_spec(tm, FFN), _row_spec(tm, FFN)],
                  out_specs=_row_spec(tm, FFN), out_shape=jax.ShapeDtypeStruct((t, FFN), BF),
                  compiler_params=_cp(("parallel",)))(gt, up)


def _swiglu_bwd(dz, gt, up, *, tm):
    t = gt.shape[0]
    tm = min(tm, t)

    def body(dz_ref, g_ref, u_ref, dg_ref, du_ref):
        gv, dzv = g_ref[...], dz_ref[...]
        s = _sig(gv)
        dg_ref[...] = (dzv * u_ref[...] * s * (1.0 + gv * (1.0 - s))).astype(BF)
        du_ref[...] = (dzv * gv * s).astype(BF)

    return _pcall(body, name="swiglu_bwd", grid=(t // tm,),
                  in_specs=[_row_spec(tm, FFN)] * 3,
                  out_specs=[_row_spec(tm, FFN)] * 2,
                  out_shape=[jax.ShapeDtypeStruct((t, FFN), BF)] * 2,
                  compiler_params=_cp(("parallel",)))(dz, gt, up)


def _attn_kv_tiles(kprev, kcur):
    kv = jnp.concatenate([kprev, kcur], axis=0).astype(F32)
    lo = lax.broadcasted_iota(jnp.int32, (2 * BLK, 128), 1) < HEAD
    tiles = []
    for part in (kv[:, 0:128], kv[:, 128:256]):
        rolled = pltpu.roll(part, HEAD, 1)
        z = jnp.zeros_like(part)
        tiles.append(((jnp.where(lo, part, z).astype(BF), jnp.where(lo, z, rolled).astype(BF)),
                      (jnp.where(lo, rolled, z).astype(BF), jnp.where(lo, z, part).astype(BF))))
    k_t, v_t = tiles
    return [(k_t[h][0], k_t[h][1], v_t[h][0], v_t[h][1]) for h in range(2)]


def _attn_mask(i):
    qi = lax.broadcasted_iota(jnp.int32, (BLK, 2 * BLK), 0)
    kj = lax.broadcasted_iota(jnp.int32, (BLK, 2 * BLK), 1)
    first_key = jnp.where(i == 0, BLK, 0)
    in_prev = jnp.logical_and(jnp.logical_and(kj < BLK, kj > qi), kj >= first_key)
    in_cur = jnp.logical_and(kj >= BLK, kj - BLK <= qi)
    return jnp.logical_or(in_prev, in_cur)


def _attn_probs(q2, kt, sink, valid):
    s = lax.dot_general(q2, kt, (((1,), (1,)), ((), ())), preferred_element_type=F32) * SCALE
    s = jnp.where(valid, s, NEG)
    mx = jnp.maximum(jnp.max(s, axis=-1, keepdims=True), sink)
    e = jnp.exp(s - mx)
    es = jnp.exp(sink - mx)
    inv = 1.0 / (jnp.sum(e, axis=-1, keepdims=True) + es)
    return e * inv, es * inv


def _attn_fwd(q, kv, sinks, *, t, comm=None):
    nb = t // BLK

    def body(sink_ref, q_ref, kp_ref, kc_ref, o_ref):
        i = pl.program_id(0)
        valid = _attn_mask(i)
        tiles = _attn_kv_tiles(kp_ref[...], kc_ref[...])
        for j in range(N_PAIR):
            ke, ko, ve, vo = tiles[j // 4]
            q2 = q_ref[:, j * 128:(j + 1) * 128]
            pe, _ = _attn_probs(q2, ke, sink_ref[0, 2 * j], valid)
            po, _ = _attn_probs(q2, ko, sink_ref[0, 2 * j + 1], valid)
            o2 = (jnp.dot(pe.astype(BF), ve, preferred_element_type=F32)
                  + jnp.dot(po.astype(BF), vo, preferred_element_type=F32))
            o_ref[:, j * 128:(j + 1) * 128] = o2.astype(BF)

    return _hosted_call(
        body, comm, (sinks, q, kv, kv), name="attn_fwd", grid=(nb,),
        in_specs=[pl.BlockSpec(memory_space=pltpu.SMEM),
                  pl.BlockSpec((BLK, D), lambda i: (i, 0)),
                  pl.BlockSpec((BLK, 256), lambda i: (jnp.maximum(i - 1, 0), 0)),
                  pl.BlockSpec((BLK, 256), lambda i: (i, 0))],
        out_specs=[pl.BlockSpec((BLK, D), lambda i: (i, 0))],
        out_shape=[jax.ShapeDtypeStruct((t, D), BF)],
        scratch_shapes=[], sem=("arbitrary",), nsteps=nb, step_fn=lambda: pl.program_id(0))


def _attn_bwd(q, kv, sinks, do, *, t, comm=None):
    nb = t // BLK
    last = nb - 1
    tn_dims = (((0,), (0,)), ((), ()))
    nt_dims = (((1,), (1,)), ((), ()))

    def body(sink_ref, q_ref, kp_ref, kc_ref, do_ref, dq_ref, dkv_ref, ds_ref, carry_ref):
        i = pl.program_id(0)

        @pl.when(i == 0)
        def _():
            ds_ref[...] = jnp.zeros_like(ds_ref)
            carry_ref[...] = jnp.zeros_like(carry_ref)

        @pl.when(i < nb)
        def _():
            valid = _attn_mask(i)
            tiles = _attn_kv_tiles(kp_ref[...], kc_ref[...])
            lane1 = lax.broadcasted_iota(jnp.int32, (1, 128), 1)
            dsink = jnp.zeros((1, 128), F32)
            gk = [[None, None], [None, None]]
            gv = [[None, None], [None, None]]
            for j in range(N_PAIR):
                h = j // 4
                ke, ko, ve, vo = tiles[h]
                q2 = q_ref[:, j * 128:(j + 1) * 128]
                do2 = do_ref[:, j * 128:(j + 1) * 128]
                dq2 = jnp.zeros((BLK, 128), F32)
                for par, (kt, vt) in enumerate(((ke, ve), (ko, vo))):
                    p, ps = _attn_probs(q2, kt, sink_ref[0, 2 * j + par], valid)
                    dp = lax.dot_general(do2, vt, nt_dims, preferred_element_type=F32)
                    dd = jnp.sum(p * dp, axis=-1, keepdims=True)
                    dsc = (p * (dp - dd)).astype(BF)
                    dsink = dsink + jnp.where(lane1 == 2 * j + par,
                                              -jnp.sum(ps * dd, axis=0, keepdims=True), 0.0)
                    dq2 = dq2 + jnp.dot(dsc, kt, preferred_element_type=F32)
                    gk_c = lax.dot_general(dsc, q2, tn_dims, preferred_element_type=F32)
                    gv_c = lax.dot_general(p.astype(BF), do2, tn_dims, preferred_element_type=F32)
                    gk[h][par] = gk_c if gk[h][par] is None else gk[h][par] + gk_c
                    gv[h][par] = gv_c if gv[h][par] is None else gv[h][par] + gv_c
                dq_ref[:, j * 128:(j + 1) * 128] = (dq2 * SCALE).astype(BF)
            ds_ref[...] += dsink
            lo = lax.broadcasted_iota(jnp.int32, (2 * BLK, 128), 1) < HEAD
            zero = jnp.zeros((2 * BLK, 128), F32)

            def unpad(g):
                return (jnp.where(lo, g[0][0] + pltpu.roll(g[0][1], HEAD, 1), zero)
                        + jnp.where(lo, zero, pltpu.roll(g[1][0], HEAD, 1) + g[1][1]))

            dk = unpad(gk) * SCALE
            dv = unpad(gv)
            dkv_ref[:, 0:128] = (carry_ref[:, 0:128] + dk[0:BLK]).astype(BF)
            dkv_ref[:, 128:256] = (carry_ref[:, 128:256] + dv[0:BLK]).astype(BF)
            carry_ref[:, 0:128] = dk[BLK:2 * BLK]
            carry_ref[:, 128:256] = dv[BLK:2 * BLK]

        @pl.when(i == nb)
        def _():
            dkv_ref[...] = carry_ref[...].astype(BF)

    return _hosted_call(
        body, comm, (sinks, q, kv, kv, do), name="attn_bwd", grid=(nb + 1,),
        in_specs=[pl.BlockSpec(memory_space=pltpu.SMEM),
                  pl.BlockSpec((BLK, D), lambda i: (jnp.minimum(i, last), 0)),
                  pl.BlockSpec((BLK, 256), lambda i: (jnp.clip(i - 1, 0, last), 0)),
                  pl.BlockSpec((BLK, 256), lambda i: (jnp.minimum(i, last), 0)),
                  pl.BlockSpec((BLK, D), lambda i: (jnp.minimum(i, last), 0))],
        out_specs=[pl.BlockSpec((BLK, D), lambda i: (jnp.minimum(i, last), 0)),
                   pl.BlockSpec((BLK, 256), lambda i: (jnp.maximum(i - 1, 0), 0)),
                   pl.BlockSpec((1, 128), lambda i: (0, 0))],
        out_shape=[jax.ShapeDtypeStruct((t, D), BF), jax.ShapeDtypeStruct((t, 256), BF),
                   jax.ShapeDtypeStruct((1, 128), F32)],
        scratch_shapes=[pltpu.VMEM((BLK, 256), F32)], sem=("arbitrary",), nsteps=nb + 1,
        step_fn=lambda: pl.program_id(0))


def _split3(v):
    h = v.astype(BF)
    r = v - h.astype(F32)
    m = r.astype(BF)
    lo = (r - m.astype(F32)).astype(BF)
    return jnp.concatenate([h, m, lo], axis=1)


def _apply01(mat, v):
    n = v.shape[1]
    r = jnp.dot(mat, _split3(v), preferred_element_type=F32)
    return r[:, 0:n] + r[:, n:2 * n] + r[:, 2 * n:3 * n]


def _cum_mats():
    r = lax.broadcasted_iota(jnp.int32, (4 * CH, CH), 0)
    c = lax.broadcasted_iota(jnp.int32, (4 * CH, CH), 1)
    limit = jnp.where(r < CH, r, jnp.where(r < 2 * CH, CH // 2 - 1, CH))
    return jnp.where(c <= limit, 1.0, 0.0).astype(BF)


def _hgrn_decays(mats, g):
    cum = _apply01(mats, g)
    return cum[0:CH], cum[CH:2 * CH], cum[2 * CH:3 * CH], cum[2 * CH:4 * CH]


def _hgrn_gates(hq, hf, lb):
    sq = _sig(hq)
    sg = _sig(hf)
    f = lb + (1.0 - lb) * sg
    return hq * sq, (1.0 - lb) * (1.0 - sg), jnp.log(f), sq, sg, f


def _tri(upper):
    r = lax.broadcasted_iota(jnp.int32, (CH, CH), 0)
    c = lax.broadcasted_iota(jnp.int32, (CH, CH), 1)
    return (c >= r) if upper else (c <= r)


def _lb_from_logits(lg_ref):
    return 1.0 / (1.0 + jnp.exp(lg_ref[1:2, :] - lg_ref[0:1, :]))


def _hgrn_fwd(h4, logits, norm_g, *, t):
    nc = t // CH
    nt_dims = (((1,), (1,)), ((), ()))
    tn_dims = (((0,), (0,)), ((), ()))

    def body(h_ref, lg_ref, ng_ref, y_ref, o_ref, st_ref, s_scr):
        @pl.when(pl.program_id(0) == 0)
        def _():
            s_scr[...] = jnp.zeros_like(s_scr)

        lb = _lb_from_logits(lg_ref)
        mats = _cum_mats()
        causal = _tri(False)
        st_ref[0] = s_scr[...]
        for h in range(HG_HEADS):
            sl = slice(h * HG_K, (h + 1) * HG_K)
            col = lambda part: slice(part * D + h * HG_K, part * D + (h + 1) * HG_K)
            q, k, g, _, _, _ = _hgrn_gates(h_ref[:, col(0)], h_ref[:, col(1)], lb[:, sl])
            vb = h_ref[:, col(2)].astype(BF)
            b, b_mid, b_last, b_last2 = _hgrn_decays(mats, g)
            qa = (q * jnp.exp(b - b_mid)).astype(BF)
            ka = (k * jnp.exp(b_mid - b)).astype(BF)
            qb = (q * jnp.exp(b)).astype(BF)
            kb = (k * jnp.exp(b_last - b)).astype(BF)
            st = s_scr[h]
            a = lax.dot_general(qa, ka, nt_dims, preferred_element_type=F32)
            a = jnp.where(causal, a, 0.0).astype(BF)
            o = (jnp.dot(a, vb, preferred_element_type=F32)
                 + lax.dot_general(qb, st.astype(BF), nt_dims, preferred_element_type=F32))
            s_scr[h] = jnp.exp(b_last2) * st + lax.dot_general(vb, kb, tn_dims, preferred_element_type=F32)
            o_ref[:, sl] = o
            on = o * lax.rsqrt(jnp.mean(o * o, axis=-1, keepdims=True) + EPS)
            y_ref[:, sl] = (on * ng_ref[:, sl] * _sig(h_ref[:, col(3)])).astype(BF)

    return _pcall(body, name="hgrn_fwd", grid=(nc,),
                  in_specs=[pl.BlockSpec((CH, 4 * D), lambda n: (n, 0)),
                            pl.BlockSpec((2, D), lambda n: (0, 0)),
                            pl.BlockSpec((1, D), lambda n: (0, 0))],
                  out_specs=[pl.BlockSpec((CH, D), lambda n: (n, 0)),
                             pl.BlockSpec((CH, D), lambda n: (n, 0)),
                             pl.BlockSpec((1, HG_HEADS, HG_K, HG_K), lambda n: (n, 0, 0, 0))],
                  out_shape=[jax.ShapeDtypeStruct((t, D), BF), jax.ShapeDtypeStruct((t, D), F32),
                             jax.ShapeDtypeStruct((nc, HG_HEADS, HG_K, HG_K), F32)],
                  scratch_shapes=[pltpu.VMEM((HG_HEADS, HG_K, HG_K), F32)],
                  compiler_params=_cp(("arbitrary",)))(h4, logits, norm_g)


def _hgrn_bwd(h4, logits, norm_g, o_pre, states, dy, *, t, comm=None):
    nc = t // CH
    nt_dims = (((1,), (1,)), ((), ()))
    tn_dims = (((0,), (0,)), ((), ()))

    def body(h_ref, lg_ref, ng_ref, o_ref, st_ref, dy_ref, dh_ref, dlg_ref, dng_ref, ds_scr, dlb_scr):
        n = pl.program_id(0)

        @pl.when(n == 0)
        def _():
            ds_scr[...] = jnp.zeros_like(ds_scr)
            dlb_scr[...] = jnp.zeros_like(dlb_scr)
            dng_ref[...] = jnp.zeros_like(dng_ref)

        lb = _lb_from_logits(lg_ref)
        mats = _cum_mats()
        causal = _tri(False)
        suffix = jnp.where(_tri(True), 1.0, 0.0).astype(BF)
        last_row = lax.broadcasted_iota(jnp.int32, (CH, HG_K), 0) == CH - 1
        for h in range(HG_HEADS):
            sl = slice(h * HG_K, (h + 1) * HG_K)
            col = lambda part: slice(part * D + h * HG_K, part * D + (h + 1) * HG_K)
            hq = h_ref[:, col(0)]
            lbh = lb[:, sl]
            q, k, g, sq, sg, f = _hgrn_gates(hq, h_ref[:, col(1)], lbh)
            vb = h_ref[:, col(2)].astype(BF)
            b, b_mid, b_last, b_last2 = _hgrn_decays(mats, g)
            e_qa, e_ka, e_qb, e_kb = jnp.exp(b - b_mid), jnp.exp(b_mid - b), jnp.exp(b), jnp.exp(b_last - b)
            qa_f, ka_f, qb_f, kb_f = q * e_qa, k * e_ka, q * e_qb, k * e_kb
            qa, ka, qb, kb = qa_f.astype(BF), ka_f.astype(BF), qb_f.astype(BF), kb_f.astype(BF)
            ngh = ng_ref[:, sl]
            sgate = _sig(h_ref[:, col(3)])
            o = o_ref[:, sl]
            r = lax.rsqrt(jnp.mean(o * o, axis=-1, keepdims=True) + EPS)
            on = o * r
            dyh = dy_ref[:, sl]
            dh_ref[:, col(3)] = (dyh * on * ngh * sgate * (1.0 - sgate)).astype(BF)
            dng_ref[:, sl] += jnp.sum(dyh * on * sgate, axis=0, keepdims=True)
            don = dyh * ngh * sgate
            dob = (r * (don - on * jnp.mean(don * on, axis=-1, keepdims=True))).astype(BF)
            st = st_ref[0, h]
            dsn = ds_scr[h]
            stb, dsb = st.astype(BF), dsn.astype(BF)
            a = lax.dot_general(qa, ka, nt_dims, preferred_element_type=F32)
            a = jnp.where(causal, a, 0.0).astype(BF)
            da = lax.dot_general(dob, vb, nt_dims, preferred_element_type=F32)
            da = jnp.where(causal, da, 0.0).astype(BF)
            dv = (lax.dot_general(a, dob, tn_dims, preferred_element_type=F32)
                  + lax.dot_general(kb, dsb, nt_dims, preferred_element_type=F32))
            dqa = jnp.dot(da, ka, preferred_element_type=F32)
            dka = lax.dot_general(da, qa, tn_dims, preferred_element_type=F32)
            dqb = jnp.dot(dob, stb, preferred_element_type=F32)
            dkb = jnp.dot(vb, dsb, preferred_element_type=F32)
            dec2 = jnp.exp(b_last2)
            ds_scr[h] = lax.dot_general(dob, qb, tn_dims, preferred_element_type=F32) + dec2 * dsn
            dkb_kb = dkb * kb_f
            db_last = (jnp.sum(dkb_kb, axis=0, keepdims=True)
                       + jnp.sum(dec2 * st * dsn, axis=0, keepdims=True))
            db = dqa * qa_f - dka * ka_f + dqb * qb_f - dkb_kb + jnp.where(last_row, db_last, 0.0)
            dg = _apply01(suffix, db)
            dq = dqa * e_qa + dqb * e_qb
            dk = dka * e_ka + dkb * e_kb
            dh_ref[:, col(0)] = (dq * sq * (1.0 + hq * (1.0 - sq))).astype(BF)
            dh_ref[:, col(2)] = dv.astype(BF)
            dfk = dg / f - dk
            dh_ref[:, col(1)] = ((1.0 - lbh) * dfk * sg * (1.0 - sg)).astype(BF)
            dlb_scr[:, sl] += jnp.sum((1.0 - sg) * dfk, axis=0, keepdims=True)

        @pl.when(n == nc - 1)
        def _():
            dl0 = dlb_scr[...] * lb * (1.0 - lb)
            dlg_ref[0:1, :] = dl0
            dlg_ref[1:2, :] = -dl0

    rev = lambda n: (nc - 1 - n, 0)
    return _hosted_call(
        body, comm, (h4, logits, norm_g, o_pre, states, dy), name="hgrn_bwd", grid=(nc,),
        in_specs=[pl.BlockSpec((CH, 4 * D), rev),
                  pl.BlockSpec((2, D), lambda n: (0, 0)),
                  pl.BlockSpec((1, D), lambda n: (0, 0)),
                  pl.BlockSpec((CH, D), rev),
                  pl.BlockSpec((1, HG_HEADS, HG_K, HG_K), lambda n: (nc - 1 - n, 0, 0, 0)),
                  pl.BlockSpec((CH, D), rev)],
        out_specs=[pl.BlockSpec((CH, 4 * D), rev),
                   pl.BlockSpec((2, D), lambda n: (0, 0)),
                   pl.BlockSpec((1, D), lambda n: (0, 0))],
        out_shape=[jax.ShapeDtypeStruct((t, 4 * D), BF), jax.ShapeDtypeStruct((2, D), F32),
                   jax.ShapeDtypeStruct((1, D), F32)],
        scratch_shapes=[pltpu.VMEM((HG_HEADS, HG_K, HG_K), F32), pltpu.VMEM((1, D), F32)],
        sem=("arbitrary",), nsteps=nc, step_fn=lambda: pl.program_id(0))


def _place():
    x, y, c = lax.axis_index("x"), lax.axis_index("y"), lax.axis_index("c")
    return x, y, c, [(1 - x, y), (x, 1 - y), (1 - x, 1 - y)]


def _gather_comm(shards, mid):
    n = len(shards)
    r = [s.shape[0] for s in shards]

    def tools(ins, outs, sems):
        send_sems, recv_sems, local_sems = sems
        x, y, c, chips = _place()
        me, sib = (x, y, c), (x, y, 1 - c)

        def rows(w, dev):
            return outs[w].at[pl.ds((4 * dev[0] + 2 * dev[1] + dev[2]) * r[w], r[w]), :]

        def copy(kind, w, block, to, src=None):
            return pltpu.make_async_remote_copy(
                src_ref=rows(w, block) if src is None else src, dst_ref=rows(w, block),
                send_sem=send_sems.at[kind], recv_sem=recv_sems.at[kind], device_id=to, device_id_type=MESH)

        def all_of(kind):
            whole = outs[0].at[pl.ds(0, sum(r)), :]
            return pltpu.make_async_remote_copy(
                src_ref=whole, dst_ref=whole, send_sem=send_sems.at[kind], recv_sem=recv_sems.at[kind],
                device_id=me, device_id_type=MESH)

        mine = [pltpu.make_async_copy(ins[w], rows(w, me), local_sems.at[w]) for w in range(n)]
        return c, chips, me, sib, copy, all_of, mine

    def start(ins, outs, sems):
        c, chips, me, sib, copy, _, mine = tools(ins, outs, sems)
        for cp in mine:
            cp.start()
        for w in range(n):
            copy(0, w, me, sib, src=ins[w]).start()
            for j, chip in enumerate(chips):
                copy(1 + j, w, me, (*chip, c), src=ins[w]).start()

    def pass_on(ins, outs, sems):
        c, chips, _, sib, copy, all_of, _ = tools(ins, outs, sems)
        for j, chip in enumerate(chips):
            all_of(1 + j).wait_recv()
            for w in range(n):
                copy(4 + j, w, (*chip, c), sib).start()

    def finish(ins, outs, sems):
        _, _, _, _, _, all_of, mine = tools(ins, outs, sems)
        all_of(0).wait_recv()
        for j in range(3):
            all_of(4 + j).wait_recv()
        for kind in range(7):
            all_of(kind).wait_send()
        for cp in mine:
            cp.wait()

    return _Comm(shards, [jax.ShapeDtypeStruct((N_DEV * rw, D), BF) for rw in r],
                 [pltpu.SemaphoreType.DMA((7,)), pltpu.SemaphoreType.DMA((7,)), pltpu.SemaphoreType.DMA((n,))],
                 [(0.0, start), (mid, pass_on), (1.0, finish)])


def _pair_comm(grads):
    n = len(grads)
    r = [g.shape[0] // N_DEV for g in grads]

    def start(ins, outs, sems):
        send_sems, recv_sems = sems
        x, y, c, _ = _place()
        for w in range(n):
            for a in range(N_CHIP):
                pltpu.make_async_remote_copy(
                    src_ref=ins[w].at[pl.ds((2 * a + 1 - c) * r[w], r[w]), :], dst_ref=outs[w].at[a],
                    send_sem=send_sems.at[w], recv_sem=recv_sems.at[w],
                    device_id=(x, y, 1 - c), device_id_type=MESH).start()

    def finish(ins, outs, sems):
        send_sems, recv_sems = sems
        x, y, c, _ = _place()
        for w in range(n):
            pltpu.make_async_remote_copy(
                src_ref=outs[w], dst_ref=outs[w], send_sem=send_sems.at[w], recv_sem=recv_sems.at[w],
                device_id=(x, y, c), device_id_type=MESH).wait()

    return _Comm(grads, [jax.ShapeDtypeStruct((N_CHIP, rw, D), BF) for rw in r],
                 [pltpu.SemaphoreType.DMA((n,)), pltpu.SemaphoreType.DMA((n,))],
                 [(0.0, start), (1.0, finish)])


def _pair_add(grad, got, core, *, name):
    r = got.shape[1]

    def body(c_ref, g_ref, got_ref, o_ref):
        o_ref[0] = (g_ref[...].astype(F32) + got_ref[0].astype(F32)).astype(BF)

    grid_spec = pltpu.PrefetchScalarGridSpec(
        num_scalar_prefetch=1, grid=(N_CHIP,),
        in_specs=[pl.BlockSpec((r, D), lambda a, c_ref: (2 * a + c_ref[0], 0)),
                  pl.BlockSpec((1, r, D), lambda a, c_ref: (a, 0, 0))],
        out_specs=pl.BlockSpec((1, r, D), lambda a, c_ref: (a, 0, 0)))
    return _pcall(body, name=name, grid_spec=grid_spec,
                  out_shape=jax.ShapeDtypeStruct((N_CHIP, r, D), BF),
                  compiler_params=_cp(("parallel",)))(core, grad, got)


def _chip_comm(pair_sums):
    n = len(pair_sums)
    r = [p.shape[1] for p in pair_sums]
    off = [sum(r[:w]) for w in range(n)]

    def tools(ins, outs, sems):
        send_sems, recv_sems, local_sems = sems
        x, y, c, chips = _place()
        my_chip = 2 * x + y

        def slot(w):
            return outs[0].at[my_chip, pl.ds(off[w], r[w]), :]

        own = [pltpu.make_async_copy(ins[w].at[my_chip], slot(w), local_sems.at[w]) for w in range(n)]
        return x, y, c, chips, my_chip, slot, own, send_sems, recv_sems

    def start(ins, outs, sems):
        x, y, c, chips, my_chip, slot, own, send_sems, recv_sems = tools(ins, outs, sems)
        for cp in own:
            cp.start()
        for j, chip in enumerate(chips):
            for w in range(n):
                pltpu.make_async_remote_copy(
                    src_ref=ins[w].at[2 * chip[0] + chip[1]], dst_ref=slot(w), send_sem=send_sems.at[j],
                    recv_sem=recv_sems.at[j], device_id=(*chip, c), device_id_type=MESH).start()

    def finish(ins, outs, sems):
        x, y, c, chips, my_chip, slot, own, send_sems, recv_sems = tools(ins, outs, sems)
        whole = outs[0].at[my_chip]
        for j in range(3):
            pltpu.make_async_remote_copy(
                src_ref=whole, dst_ref=whole, send_sem=send_sems.at[j], recv_sem=recv_sems.at[j],
                device_id=(x, y, c), device_id_type=MESH).wait()
        for cp in own:
            cp.wait()

    return _Comm(pair_sums, [jax.ShapeDtypeStruct((N_CHIP, sum(r), D), BF)],
                 [pltpu.SemaphoreType.DMA((3,)), pltpu.SemaphoreType.DMA((3,)), pltpu.SemaphoreType.DMA((n,))],
                 [(0.0, start), (1.0, finish)])


def _sum_chips(parts, *, tr, name):
    rows = parts.shape[1]

    def body(p_ref, o_ref):
        acc = p_ref[0].astype(F32)
        for a in range(1, N_CHIP):
            acc = acc + p_ref[a].astype(F32)
        o_ref[...] = acc

    return _pcall(body, name=name, grid=(rows // tr,),
                  in_specs=[pl.BlockSpec((N_CHIP, tr, D), lambda i: (0, i, 0))],
                  out_specs=_row_spec(tr, D), out_shape=jax.ShapeDtypeStruct((rows, D), F32),
                  compiler_params=_cp(("parallel",)))(parts)


def _adam_math(w, g, m, v):
    m = ADAM_B1 * m + (1.0 - ADAM_B1) * g
    v = ADAM_B2 * v + (1.0 - ADAM_B2) * (g * g)
    m_hat = m / (1.0 - ADAM_B1 ** ADAM_STEP)
    v_hat = v / (1.0 - ADAM_B2 ** ADAM_STEP)
    delta = -ADAM_LR * (m_hat / (jnp.sqrt(v_hat) + ADAM_EPS) + ADAM_WD * w)
    return delta, m, v


def _small_allreduce_adam(gpart, w, m, v):
    def body(g_ref, w_ref, m_ref, v_ref, gs_ref, d_ref, mo_ref, vo_ref, gath, send_sems, recv_sems):
        x, y, c, _ = _place()
        me = 4 * x + 2 * y + c
        gath[me] = g_ref[...]
        cps = []
        for d in range(1, N_DEV):
            peer = (x ^ (d >> 2), y ^ ((d >> 1) & 1), c ^ (d & 1))
            cps.append(pltpu.make_async_remote_copy(
                src_ref=g_ref, dst_ref=gath.at[me], send_sem=send_sems.at[d - 1],
                recv_sem=recv_sems.at[d - 1], device_id=peer, device_id_type=MESH))
        for cp in cps:
            cp.start()
        for cp in cps:
            cp.wait()
        g = gath[0]
        for k in range(1, N_DEV):
            g = g + gath[k]
        gs_ref[...] = g
        d_ref[...], mo_ref[...], vo_ref[...] = _adam_math(w_ref[...], g, m_ref[...], v_ref[...])

    shape = jax.ShapeDtypeStruct((SMALL_ROWS, D), F32)
    vm = pl.BlockSpec(memory_space=pltpu.VMEM)
    return _pcall(body, name="small_allreduce_adam", in_specs=[vm] * 4, out_specs=[vm] * 4,
                  out_shape=[shape] * 4,
                  scratch_shapes=[pltpu.VMEM((N_DEV, SMALL_ROWS, D), F32),
                                  pltpu.SemaphoreType.DMA((N_DEV - 1,)), pltpu.SemaphoreType.DMA((N_DEV - 1,))],
                  compiler_params=pltpu.CompilerParams(has_side_effects=True))(gpart, w, m, v)


def _adam(w, g, m, v, *, name):
    rows, cols = w.shape
    tr = rows if rows <= 512 else 256

    def body(w_ref, g_ref, m_ref, v_ref, d_ref, mo_ref, vo_ref):
        d_ref[...], mo_ref[...], vo_ref[...] = _adam_math(w_ref[...], g_ref[...], m_ref[...], v_ref[...])

    spec = pl.BlockSpec((tr, cols), lambda i: (i, 0))
    return _pcall(body, name=name, grid=(rows // tr,), in_specs=[spec] * 4, out_specs=[spec] * 3,
                  out_shape=[jax.ShapeDtypeStruct((rows, cols), F32)] * 3,
                  compiler_params=_cp(("parallel",)))(w, g, m, v)


def _step(x, tgt, shards, norm_mix_g, b_in, sinks, logits, hgrn_norm_g, norm_ffn_g, norm_final_g):
    t = x.shape[0]
    big = dict(tm=1024, tn=1024, tk=4096)
    core = lax.axis_index("c").astype(jnp.int32).reshape(1)

    (win_t,) = _run_comm(_gather_comm(shards[0:1], 0.0), name="gather_w_in")
    u1 = _rms_fwd(x, norm_mix_g, tm=512, name="rms_mix")
    (q, kv, h4, gates), (wg_t, wu_t, wd) = _inproj_fwd(u1, win_t, b_in, t=t,
                                                       comm=_gather_comm(shards[1:4], 0.8))
    (y_attn,), (wba, wbh, wout) = _attn_fwd(q, kv, sinks, t=t, comm=_gather_comm(shards[4:7], 0.7))
    y_hgrn, o_pre, states = _hgrn_fwd(h4, logits, hgrn_norm_g, t=t)
    ya = _mm(y_attn, wba, m=t, n=D, k=D, name="branch_attn", **big)
    yb = _mm(y_hgrn, wbh, m=t, n=D, k=D, name="branch_hgrn", **big)
    merged = _merge_fwd(gates, ya, yb, tm=512)
    h1 = _mm(merged, wout, m=t, n=D, k=D, resid=x, name="out_proj", **big)
    u2 = _rms_fwd(h1, norm_ffn_g, tm=512, name="rms_ffn")
    gt = _mm(u2, wg_t, m=t, n=FFN, k=D, tb=True, tm=1024, tn=FFN // 2, tk=D, name="ffn_gate")
    up = _mm(u2, wu_t, m=t, n=FFN, k=D, tb=True, tm=1024, tn=FFN // 2, tk=D, name="ffn_up")
    z = _swiglu_fwd(gt, up, tm=256)
    h2 = _mm(z, wd, m=t, n=D, k=FFN, resid=h1, name="ffn_down", **big)
    dh2, dh2_b, d_norm_final, loss_row = _loss_head(h2, tgt, norm_final_g, tm=512)

    dz = _mm(dh2_b, wd, m=t, n=FFN, k=D, tb=True, tm=1024, tn=FFN // 2, tk=D, name="d_z")
    d_wd = _mm(z, dh2_b, m=FFN, n=D, k=t, ta=True, tm=256, tn=D, tk=4096, out_dtype=BF, name="d_w_down")
    dgt, dup = _swiglu_bwd(dz, gt, up, tm=256)
    du2 = _mm(dgt, wg_t, m=t, n=D, k=FFN, name="d_u2_gate", **big)
    du2 = _mm(dup, wu_t, m=t, n=D, k=FFN, resid=du2, name="d_u2_up", **big)
    d_wg = _mm(dgt, u2, m=FFN, n=D, k=t, ta=True, tm=256, tn=D, tk=4096, out_dtype=BF, name="d_w_gate")
    d_wu = _mm(dup, u2, m=FFN, n=D, k=t, ta=True, tm=256, tn=D, tk=4096, out_dtype=BF, name="d_w_up")
    dh1, dh1_b, d_norm_ffn = _rms_bwd(du2, h1, norm_ffn_g, dh2, tm=512, name="rms_ffn_bwd")
    dmerged = _mm(dh1_b, wout, m=t, n=D, k=D, tb=True, name="d_merged", **big)
    d_wout = _mm(merged, dh1_b, m=D, n=D, k=t, ta=True, tm=256, tn=D, tk=4096, out_dtype=BF, name="d_w_out")
    dya, dyb, dgates = _merge_bwd(dmerged, gates, ya, yb, tm=512)
    dy_attn = _mm(dya, wba, m=t, n=D, k=D, tb=True, out_dtype=BF, name="d_y_attn", **big)
    dy_hgrn = _mm(dyb, wbh, m=t, n=D, k=D, tb=True, name="d_y_hgrn", **big)
    d_wba = _mm(y_attn, dya, m=D, n=D, k=t, ta=True, tm=256, tn=D, tk=4096, out_dtype=BF, name="d_w_ba")
    d_wbh = _mm(y_hgrn, dyb, m=D, n=D, k=t, ta=True, tm=256, tn=D, tk=4096, out_dtype=BF, name="d_w_bh")
    rest = (d_wg, d_wu, d_wd, d_wba, d_wbh, d_wout)
    (dq, dkv, d_sinks), got = _attn_bwd(q, kv, sinks, dy_attn, t=t, comm=_pair_comm(rest))
    pair = [_pair_add(g, r, core, name="pair_add_%d" % i) for i, (g, r) in enumerate(zip(rest, got))]
    (dh4, d_logits, d_hgrn_norm), (parts_rest,) = _hgrn_bwd(h4, logits, hgrn_norm_g, o_pre, states, dy_hgrn,
                                                             t=t, comm=_chip_comm(pair))
    g_rest = _sum_chips(parts_rest, tr=parts_rest.shape[1] // 2, name="sum_chips_rest")
    dps = (dq, dkv, dh4, dgates)
    d_win_t, d_b_in = _inproj_bwd_w(dps, u1, t=t)
    (du1,), got_in = _inproj_bwd_x(dps, win_t, t=t, part=0, comm=_pair_comm([d_win_t]))
    pair_in = _pair_add(d_win_t, got_in[0], core, name="pair_add_w_in")
    (du1,), (parts_in,) = _inproj_bwd_x(dps, win_t, t=t, part=1, into=du1, comm=_chip_comm([pair_in]))
    g_in = _sum_chips(parts_in, tr=parts_in.shape[1] // 2, name="sum_chips_w_in")
    grad_x, _, d_norm_mix = _rms_bwd(du1, x, norm_mix_g, dh1, tm=512, name="rms_mix_bwd")

    small_grads = (d_norm_mix, d_b_in, d_sinks, d_logits, d_hgrn_norm, d_norm_ffn, d_norm_final)
    return loss_row, grad_x, g_in, g_rest, small_grads


def _pack_small(norm_mix, b_in, sinks, logits, hgrn_norm, norm_ffn, norm_final, extra=None):
    pad = lambda a, n: jnp.pad(a.reshape(1, -1), ((0, 0), (0, n - a.size)))
    rows = [norm_mix.reshape(1, D), hgrn_norm.reshape(1, D), norm_ffn.reshape(1, D), norm_final.reshape(1, D),
            logits.reshape(2, D), pad(sinks.reshape(-1)[:16], D),
            jnp.zeros((1, D), F32) if extra is None else pad(extra, D),
            pad(b_in, 8 * D).reshape(8, D)]
    return jnp.concatenate(rows, axis=0).astype(F32)


def _unpack_small(p):
    return dict(norm_mix_g=p[0:1], hgrn_norm_g=p[1:2], norm_ffn_g=p[2:3], norm_final_g=p[3],
                hgrn_lb_logits=p[4:6], attn_sinks=p[6:7, 0:16], extra=p[7],
                b_in=p[8:16].reshape(1, 8 * D)[:, :IN_W])


def kernel(x, norm_mix_g, w_in, b_in, attn_sinks, hgrn_lb_logits, hgrn_norm_g, w_branch_attn, w_branch_hgrn, w_out, norm_ffn_g, w_ffn_gate, w_ffn_up, w_ffn_down, norm_final_g, loss_target, m_norm_mix_g, m_w_in, m_b_in, m_attn_sinks, m_hgrn_lb_logits, m_hgrn_norm_g, m_w_branch_attn, m_w_branch_hgrn, m_w_out, m_norm_ffn_g, m_w_ffn_gate, m_w_ffn_up, m_w_ffn_down, m_norm_final_g, v_norm_mix_g, v_w_in, v_b_in, v_attn_sinks, v_hgrn_lb_logits, v_hgrn_norm_g, v_w_branch_attn, v_w_branch_hgrn, v_w_out, v_norm_ffn_g, v_w_ffn_gate, v_w_ffn_up, v_w_ffn_down, v_norm_final_g):
    shards = [w_in[0].T.astype(BF), w_ffn_gate[0].T.astype(BF), w_ffn_up[0].T.astype(BF),
              w_ffn_down[0].astype(BF), w_branch_attn[0].astype(BF), w_branch_hgrn[0].astype(BF),
              w_out[0].astype(BF)]
    loss_row, grad_x, g_in, g_rest, small_grads = _step(
        x[0], loss_target[0], shards, norm_mix_g, b_in, attn_sinks, hgrn_lb_logits, hgrn_norm_g,
        norm_ffn_g, norm_final_g.reshape(1, D))

    d_norm_mix, d_b_in, d_sinks, d_logits, d_hgrn_norm, d_norm_ffn, d_norm_final = small_grads
    g_small = _pack_small(d_norm_mix, d_b_in, d_sinks[:, :16], d_logits, d_hgrn_norm, d_norm_ffn,
                          d_norm_final, extra=loss_row[0, 0:1])
    w_small = _pack_small(norm_mix_g, b_in, attn_sinks, hgrn_lb_logits, hgrn_norm_g, norm_ffn_g, norm_final_g)
    m_small = _pack_small(m_norm_mix_g, m_b_in, m_attn_sinks, m_hgrn_lb_logits, m_hgrn_norm_g, m_norm_ffn_g,
                          m_norm_final_g)
    v_small = _pack_small(v_norm_mix_g, v_b_in, v_attn_sinks, v_hgrn_lb_logits, v_hgrn_norm_g, v_norm_ffn_g,
                          v_norm_final_g)
    small = [_unpack_small(p) for p in _small_allreduce_adam(g_small, w_small, m_small, v_small)]
    loss = small[0]["extra"][0]

    names = ["w_in", "w_ffn_gate", "w_ffn_up", "w_ffn_down", "w_branch_attn", "w_branch_hgrn", "w_out"]
    w_full = dict(w_in=(w_in, m_w_in, v_w_in), w_ffn_gate=(w_ffn_gate, m_w_ffn_gate, v_w_ffn_gate),
                  w_ffn_up=(w_ffn_up, m_w_ffn_up, v_w_ffn_up), w_ffn_down=(w_ffn_down, m_w_ffn_down, v_w_ffn_down),
                  w_branch_attn=(w_branch_attn, m_w_branch_attn, v_w_branch_attn),
                  w_branch_hgrn=(w_branch_hgrn, m_w_branch_hgrn, v_w_branch_hgrn),
                  w_out=(w_out, m_w_out, v_w_out))
    big = {}
    for i, name in enumerate(names):
        g = g_in if i == 0 else g_rest[SLAB_OFF[i] - SLAB_R[0]:SLAB_OFF[i] - SLAB_R[0] + SLAB_R[i]]
        if i < 3:
            g = g.T
        wv, mv, vv = w_full[name]
        delta, new_m, new_v = _adam(wv[0], g, mv[0], vv[0], name="adam_" + name)
        big[name] = [a[None] for a in (g, delta, new_m, new_v)]

    order = ["norm_mix_g", "w_in", "b_in", "attn_sinks", "hgrn_lb_logits", "hgrn_norm_g", "w_branch_attn",
             "w_branch_hgrn", "w_out", "norm_ffn_g", "w_ffn_gate", "w_ffn_up", "w_ffn_down", "norm_final_g"]
    outs = [loss, grad_x[None]]
    for kind in range(4):
        for name in order:
            outs.append(big[name][kind] if name in big else small[kind][name])
    return tuple(outs)
```

```python
import math

import jax
import jax.numpy as jnp
from jax import lax
from jax.experimental import pallas as pl
from jax.experimental.pallas import tpu as pltpu

F32 = jnp.float32
BF = jnp.bfloat16
MESH = pl.DeviceIdType.MESH

D = 1024
HEAD = 64
N_PAIR = 8
BLK = 128
CH = 64
HG_HEADS = 8
HG_K = 128
FFN = 2816
IN_W = 7424
N_DEV = 8
N_CHIP = 4
EPS = 1e-6
NEG = -1e30
SCALE = 1.0 / math.sqrt(HEAD)
VMEM_LIMIT = 56 * 1024 * 1024
WT = 256

ADAM_LR, ADAM_B1, ADAM_B2, ADAM_EPS, ADAM_WD, ADAM_STEP = 0.001, 0.9, 0.999, 1e-08, 0.01, 10

SLAB_R = (IN_W // N_DEV, FFN // N_DEV, FFN // N_DEV, FFN // N_DEV, D // N_DEV, D // N_DEV, D // N_DEV)
SLAB_ROWS = sum(SLAB_R)
SLAB_OFF = tuple(sum(SLAB_R[:i]) for i in range(len(SLAB_R)))
N_W = len(SLAB_R)
GRP_OFF = (0, D // WT, (D + 256) // WT, (5 * D + 256) // WT)
GRP_N = (D // WT, 256 // WT, 4 * D // WT, 2 * D // WT)
SMALL_ROWS = 16


def _pcall(body, **kw):
    return pl.pallas_call(body, **kw)


def _cp(sem=None, **kw):
    return pltpu.CompilerParams(dimension_semantics=sem, vmem_limit_bytes=VMEM_LIMIT, **kw)


def _sig(v):
    return 1.0 / (1.0 + jnp.exp(-v))


def _accum(ref, val, first):
    @pl.when(first)
    def _():
        ref[...] = val

    @pl.when(jnp.logical_not(first))
    def _():
        ref[...] += val


class _Comm:
    def __init__(self, ins, out_shapes, sem_shapes, phases):
        self.ins, self.out_shapes, self.sem_shapes, self.phases = list(ins), list(out_shapes), list(sem_shapes), phases


def _host(body, comm, n_in, n_out, n_scr, nsteps, step_fn):
    if comm is None:
        return body
    ci, co = len(comm.ins), len(comm.out_shapes)

    def wrapped(*refs):
        p = 0
        ins, p = refs[p:p + n_in], p + n_in
        cins, p = refs[p:p + ci], p + ci
        outs, p = refs[p:p + n_out], p + n_out
        couts, p = refs[p:p + co], p + co
        scr, p = refs[p:p + n_scr], p + n_scr
        csems = refs[p:]
        step = step_fn()
        for frac, fn in comm.phases:
            if frac < 1.0:
                @pl.when(step == int(round(frac * (nsteps - 1))))
                def _(fn=fn):
                    fn(cins, couts, csems)
        body(*ins, *outs, *scr)
        for frac, fn in comm.phases:
            if frac >= 1.0:
                @pl.when(step == nsteps - 1)
                def _(fn=fn):
                    fn(cins, couts, csems)

    return wrapped


def _hosted_call(body, comm, args, *, name, grid, in_specs, out_specs, out_shape, scratch_shapes, sem,
                 nsteps, step_fn, aliases=None):
    n_in, n_out, n_scr = len(in_specs), len(out_specs), len(scratch_shapes)
    args = list(args)
    extra = {}
    if comm is not None:
        in_specs = list(in_specs) + [_hbm_spec()] * len(comm.ins)
        out_specs = list(out_specs) + [_hbm_spec()] * len(comm.out_shapes)
        out_shape = list(out_shape) + comm.out_shapes
        scratch_shapes = list(scratch_shapes) + comm.sem_shapes
        args += comm.ins
        extra = dict(has_side_effects=True)
    outs = _pcall(_host(body, comm, n_in, n_out, n_scr, nsteps, step_fn), name=name, grid=grid,
                  in_specs=in_specs, out_specs=out_specs, out_shape=out_shape, scratch_shapes=scratch_shapes,
                  input_output_aliases=aliases or {}, compiler_params=_cp(sem, **extra))(*args)
    return list(outs[:n_out]), list(outs[n_out:])


def _run_comm(comm, *, name):
    ci, co = len(comm.ins), len(comm.out_shapes)

    def body(*refs):
        for _, fn in comm.phases:
            fn(refs[:ci], refs[ci:ci + co], refs[ci + co:])

    return _pcall(body, name=name, in_specs=[_hbm_spec()] * ci, out_specs=[_hbm_spec()] * co,
                  out_shape=comm.out_shapes, scratch_shapes=comm.sem_shapes,
                  compiler_params=pltpu.CompilerParams(has_side_effects=True))(*comm.ins)


def _hbm_spec():
    return pl.BlockSpec(memory_space=pl.ANY)


def _mm(a, b, *, m, n, k, tm, tn, tk, ta=False, tb=False, out_dtype=F32, resid=None, name):
    tm, tn, tk = min(tm, m), min(tn, n), min(tk, k)
    gm, gn, gk = m // tm, n // tn, k // tk
    assert gm * tm == m and gn * tn == n and gk * tk == k, (name, m, n, k, tm, tn, tk)
    a_spec = (pl.BlockSpec((tk, tm), lambda i, j, l: (l, i)) if ta
              else pl.BlockSpec((tm, tk), lambda i, j, l: (i, l)))
    b_spec = (pl.BlockSpec((tn, tk), lambda i, j, l: (j, l)) if tb
              else pl.BlockSpec((tk, tn), lambda i, j, l: (l, j)))
    dims = (((0 if ta else 1,), (1 if tb else 0,)), ((), ()))
    ins, in_specs = [a, b], [a_spec, b_spec]
    if resid is not None:
        ins.append(resid)
        in_specs.append(pl.BlockSpec((tm, tn), lambda i, j, l: (i, j)))
    scratch = [pltpu.VMEM((tm, tn), F32)] if gk > 1 else []

    def body(*refs):
        it = iter(refs)
        a_ref, b_ref = next(it), next(it)
        resid_ref = next(it) if resid is not None else None
        o_ref = next(it)
        acc_ref = next(it) if gk > 1 else None
        l = pl.program_id(2)
        part = lax.dot_general(a_ref[...].astype(BF), b_ref[...].astype(BF), dims,
                               preferred_element_type=F32)

        def finish(acc):
            if resid_ref is not None:
                acc = acc + resid_ref[...].astype(F32)
            o_ref[...] = acc.astype(out_dtype)

        if gk == 1:
            finish(part)
        else:
            _accum(acc_ref, part, l == 0)

            @pl.when(l == gk - 1)
            def _():
                finish(acc_ref[...])

    return _pcall(body, name=name, grid=(gm, gn, gk), in_specs=in_specs,
                  out_specs=pl.BlockSpec((tm, tn), lambda i, j, l: (i, j)),
                  out_shape=jax.ShapeDtypeStruct((m, n), out_dtype), scratch_shapes=scratch,
                  compiler_params=_cp(("parallel", "parallel", "arbitrary")))(*ins)


def _grp_of(i):
    return [jnp.logical_and(i >= GRP_OFF[g], i < GRP_OFF[g] + GRP_N[g]) for g in range(4)]


def _grp_idx(i, g):
    return jnp.clip(i - GRP_OFF[g], 0, GRP_N[g] - 1)


def _inproj_fwd(u, win_t, b_in, *, t, comm=None):
    n_tiles = IN_W // WT
    dims = (((1,), (1,)), ((), ()))
    dtypes = (BF, BF, F32, F32)

    def body(u_ref, w_ref, b_ref, *o_refs):
        i = pl.program_id(0)
        p = lax.dot_general(u_ref[...], w_ref[...], dims, preferred_element_type=F32) + b_ref[...]
        for g, pred in enumerate(_grp_of(i)):
            @pl.when(pred)
            def _(g=g):
                o_refs[g][...] = p.astype(dtypes[g])

    return _hosted_call(
        body, comm, (u, win_t, b_in), name="inproj_fwd", grid=(n_tiles,),
        in_specs=[pl.BlockSpec((t, D), lambda i: (0, 0)),
                  pl.BlockSpec((WT, D), lambda i: (i, 0)),
                  pl.BlockSpec((1, WT), lambda i: (0, i))],
        out_specs=[pl.BlockSpec((t, WT), lambda i, g=g: (0, _grp_idx(i, g))) for g in range(4)],
        out_shape=[jax.ShapeDtypeStruct((t, GRP_N[g] * WT), dtypes[g]) for g in range(4)],
        scratch_shapes=[], sem=("arbitrary",), nsteps=n_tiles, step_fn=lambda: pl.program_id(0))


def _inproj_bwd_x(dps, win_t, *, t, part, into=None, comm=None):
    n_tiles = IN_W // WT
    tm = t // 2

    def body(d0, d1, d2, d3, w_ref, *rest):
        o_ref, acc_ref = rest[-2], rest[-1]
        l = pl.program_id(0)
        w = w_ref[...]
        for g, (pred, d_ref) in enumerate(zip(_grp_of(l), (d0, d1, d2, d3))):
            @pl.when(pred)
            def _(d_ref=d_ref):
                _accum(acc_ref, jnp.dot(d_ref[...], w, preferred_element_type=F32), l == 0)

        @pl.when(l == n_tiles - 1)
        def _():
            o_ref[...] = acc_ref[...]

    in_specs = ([pl.BlockSpec((tm, WT), lambda l, g=g: (part, _grp_idx(l, g))) for g in range(4)]
                + [pl.BlockSpec((WT, D), lambda l: (l, 0))])
    args = list(dps) + [win_t]
    aliases = None
    if into is not None:
        in_specs.append(_hbm_spec())
        args.append(into)
        aliases = {5: 0}
    return _hosted_call(
        body, comm, args, name="inproj_bwd_x%d" % part, grid=(n_tiles,), in_specs=in_specs,
        out_specs=[pl.BlockSpec((tm, D), lambda l: (part, 0))],
        out_shape=[jax.ShapeDtypeStruct((t, D), F32)],
        scratch_shapes=[pltpu.VMEM((tm, D), F32)], sem=("arbitrary",), nsteps=n_tiles,
        step_fn=lambda: pl.program_id(0), aliases=aliases)


def _inproj_bwd_w(dps, u, *, t):
    n_tiles = IN_W // WT
    dims = (((0,), (0,)), ((), ()))

    def body(d0, d1, d2, d3, u_ref, o_ref, db_ref):
        i = pl.program_id(0)
        uv = u_ref[...]
        for g, (pred, d_ref) in enumerate(zip(_grp_of(i), (d0, d1, d2, d3))):
            @pl.when(pred)
            def _(d_ref=d_ref):
                dv = d_ref[...]
                o_ref[...] = lax.dot_general(dv, uv, dims, preferred_element_type=F32).astype(BF)
                db_ref[...] = jnp.sum(dv.astype(F32), axis=0, keepdims=True)

    return _pcall(body, name="inproj_bwd_w", grid=(n_tiles,),
                  in_specs=[pl.BlockSpec((t, WT), lambda i, g=g: (0, _grp_idx(i, g))) for g in range(4)]
                  + [pl.BlockSpec((t, D), lambda i: (0, 0))],
                  out_specs=[pl.BlockSpec((WT, D), lambda i: (i, 0)),
                             pl.BlockSpec((1, WT), lambda i: (0, i))],
                  out_shape=[jax.ShapeDtypeStruct((IN_W, D), BF), jax.ShapeDtypeStruct((1, IN_W), F32)],
                  compiler_params=_cp(("arbitrary",)))(*dps, u)


def _row_spec(tm, width, col=0):
    return pl.BlockSpec((tm, width), lambda i: (i, col))


def _vec_spec(width):
    return pl.BlockSpec((1, width), lambda i: (0, 0))


def _rms_fwd(x, g, *, tm, name):
    t = x.shape[0]
    tm = min(tm, t)

    def body(x_ref, g_ref, u_ref):
        xv = x_ref[...]
        r = lax.rsqrt(jnp.mean(xv * xv, axis=-1, keepdims=True) + EPS)
        u_ref[...] = (xv * r * g_ref[...]).astype(BF)

    return _pcall(body, name=name, grid=(t // tm,), in_specs=[_row_spec(tm, D), _vec_spec(D)],
                  out_specs=_row_spec(tm, D), out_shape=jax.ShapeDtypeStruct((t, D), BF),
                  compiler_params=_cp(("parallel",)))(x, g)


def _rms_bwd(du, x, g, resid, *, tm, name):
    t = x.shape[0]
    tm = min(tm, t)

    def body(du_ref, x_ref, g_ref, r_ref, dx_ref, dxb_ref, dg_ref):
        xv = x_ref[...]
        r = lax.rsqrt(jnp.mean(xv * xv, axis=-1, keepdims=True) + EPS)
        xh = xv * r
        duv = du_ref[...]
        dxh = duv * g_ref[...]
        dx = r_ref[...] + r * (dxh - xh * jnp.mean(dxh * xh, axis=-1, keepdims=True))
        dx_ref[...] = dx
        dxb_ref[...] = dx.astype(BF)
        _accum(dg_ref, jnp.sum(duv * xh, axis=0, keepdims=True), pl.program_id(0) == 0)

    return _pcall(body, name=name, grid=(t // tm,),
                  in_specs=[_row_spec(tm, D), _row_spec(tm, D), _vec_spec(D), _row_spec(tm, D)],
                  out_specs=[_row_spec(tm, D), _row_spec(tm, D), _vec_spec(D)],
                  out_shape=[jax.ShapeDtypeStruct((t, D), F32), jax.ShapeDtypeStruct((t, D), BF),
                             jax.ShapeDtypeStruct((1, D), F32)],
                  compiler_params=_cp(("arbitrary",)))(du, x, g, resid)


def _loss_head(h2, tgt, g, *, tm):
    t = h2.shape[0]
    tm = min(tm, t)

    def body(h_ref, t_ref, g_ref, dh_ref, dhb_ref, dg_ref, loss_ref):
        hv = h_ref[...]
        gv = g_ref[...]
        r = lax.rsqrt(jnp.mean(hv * hv, axis=-1, keepdims=True) + EPS)
        xh = hv * r
        err = xh * gv - t_ref[...]
        lp = jnp.sum(jnp.sum(err * err, axis=1, keepdims=True), axis=0, keepdims=True) * (0.5 / D)
        dy = err * (1.0 / D)
        dxh = dy * gv
        dh = r * (dxh - xh * jnp.mean(dxh * xh, axis=-1, keepdims=True))
        dh_ref[...] = dh
        dhb_ref[...] = dh.astype(BF)
        first = pl.program_id(0) == 0
        _accum(dg_ref, jnp.sum(dy * xh, axis=0, keepdims=True), first)
        _accum(loss_ref, jnp.broadcast_to(lp, (1, 128)), first)

    return _pcall(body, name="loss_head", grid=(t // tm,),
                  in_specs=[_row_spec(tm, D), _row_spec(tm, D), _vec_spec(D)],
                  out_specs=[_row_spec(tm, D), _row_spec(tm, D), _vec_spec(D), _vec_spec(128)],
                  out_shape=[jax.ShapeDtypeStruct((t, D), F32), jax.ShapeDtypeStruct((t, D), BF),
                             jax.ShapeDtypeStruct((1, D), F32), jax.ShapeDtypeStruct((1, 128), F32)],
                  compiler_params=_cp(("arbitrary",)))(h2, tgt, g)


def _merge_fwd(gates, ya, yb, *, tm):
    t = ya.shape[0]
    tm = min(tm, t)

    def body(ga_ref, gb_ref, ya_ref, yb_ref, o_ref):
        o_ref[...] = (_sig(ga_ref[...]) * ya_ref[...] + _sig(gb_ref[...]) * yb_ref[...]).astype(BF)

    return _pcall(body, name="merge_fwd", grid=(t // tm,),
                  in_specs=[_row_spec(tm, D, 0), _row_spec(tm, D, 1), _row_spec(tm, D), _row_spec(tm, D)],
                  out_specs=_row_spec(tm, D), out_shape=jax.ShapeDtypeStruct((t, D), BF),
                  compiler_params=_cp(("parallel",)))(gates, gates, ya, yb)


def _merge_bwd(dm, gates, ya, yb, *, tm):
    t = ya.shape[0]
    tm = min(tm, t)

    def body(dm_ref, ga_ref, gb_ref, ya_ref, yb_ref, dya_ref, dyb_ref, dg_ref):
        dmv = dm_ref[...]
        sa, sb = _sig(ga_ref[...]), _sig(gb_ref[...])
        dya_ref[...] = (dmv * sa).astype(BF)
        dyb_ref[...] = (dmv * sb).astype(BF)
        dg_ref[:, 0:D] = (dmv * ya_ref[...] * sa * (1.0 - sa)).astype(BF)
        dg_ref[:, D:2 * D] = (dmv * yb_ref[...] * sb * (1.0 - sb)).astype(BF)

    return _pcall(body, name="merge_bwd", grid=(t // tm,),
                  in_specs=[_row_spec(tm, D), _row_spec(tm, D, 0), _row_spec(tm, D, 1),
                            _row_spec(tm, D), _row_spec(tm, D)],
                  out_specs=[_row_spec(tm, D), _row_spec(tm, D), _row_spec(tm, 2 * D)],
                  out_shape=[jax.ShapeDtypeStruct((t, D), BF), jax.ShapeDtypeStruct((t, D), BF),
                             jax.ShapeDtypeStruct((t, 2 * D), BF)],
                  compiler_params=_cp(("parallel",)))(dm, gates, gates, ya, yb)


def _swiglu_fwd(gt, up, *, tm):
    t = gt.shape[0]
    tm = min(tm, t)

    def body(g_ref, u_ref, z_ref):
        gv = g_ref[...]
        z_ref[...] = (gv * _sig(gv) * u_ref[...]).astype(BF)

    return _pcall(body, name="swiglu_fwd", grid=(t // tm,),
                  in_specs=[_row_spec(tm, FFN), _row_spec(tm, FFN)],
                  out_specs=_row_spec(tm, FFN), out_shape=jax.ShapeDtypeStruct((t, FFN), BF),
                  compiler_params=_cp(("parallel",)))(gt, up)


def _swiglu_bwd(dz, gt, up, *, tm):
    t = gt.shape[0]
    tm = min(tm, t)

    def body(dz_ref, g_ref, u_ref, dg_ref, du_ref):
        gv, dzv = g_ref[...], dz_ref[...]
        s = _sig(gv)
        dg_ref[...] = (dzv * u_ref[...] * s * (1.0 + gv * (1.0 - s))).astype(BF)
        du_ref[...] = (dzv * gv * s).astype(BF)

    return _pcall(body, name="swiglu_bwd", grid=(t // tm,),
                  in_specs=[_row_spec(tm, FFN)] * 3,
                  out_specs=[_row_spec(tm, FFN)] * 2,
                  out_shape=[jax.ShapeDtypeStruct((t, FFN), BF)] * 2,
                  compiler_params=_cp(("parallel",)))(dz, gt, up)


def _attn_kv_tiles(kprev, kcur):
    kv = jnp.concatenate([kprev, kcur], axis=0).astype(F32)
    lo = lax.broadcasted_iota(jnp.int32, (2 * BLK, 128), 1) < HEAD
    tiles = []
    for part in (kv[:, 0:128], kv[:, 128:256]):
        rolled = pltpu.roll(part, HEAD, 1)
        z = jnp.zeros_like(part)
        tiles.append(((jnp.where(lo, part, z).astype(BF), jnp.where(lo, z, rolled).astype(BF)),
                      (jnp.where(lo, rolled, z).astype(BF), jnp.where(lo, z, part).astype(BF))))
    k_t, v_t = tiles
    return [(jnp.concatenate(k_t[h], axis=0), jnp.concatenate(v_t[h], axis=0)) for h in range(2)]


def _attn_mask(i):
    qi = lax.broadcasted_iota(jnp.int32, (BLK, 2 * BLK), 0)
    kj = lax.broadcasted_iota(jnp.int32, (BLK, 2 * BLK), 1)
    first_key = jnp.where(i == 0, BLK, 0)
    in_prev = jnp.logical_and(jnp.logical_and(kj < BLK, kj > qi), kj >= first_key)
    in_cur = jnp.logical_and(kj >= BLK, kj - BLK <= qi)
    return jnp.logical_or(in_prev, in_cur)


def _attn_probs(s, sink, valid):
    s = jnp.where(valid, s * SCALE, NEG)
    mx = jnp.maximum(jnp.max(s, axis=-1, keepdims=True), sink)
    e = jnp.exp(s - mx)
    es = jnp.exp(sink - mx)
    inv = 1.0 / (jnp.sum(e, axis=-1, keepdims=True) + es)
    return e * inv, es * inv


_NT = (((1,), (1,)), ((), ()))
_TN = (((0,), (0,)), ((), ()))
_KEYS = 2 * BLK


def _pair(ref, j):
    return ref[:, j * 128:(j + 1) * 128]


def _attn_fwd(q, kv, sinks, *, t, comm=None):
    nb = t // BLK

    def body(sink_ref, q_ref, kp_ref, kc_ref, o_ref):
        valid = _attn_mask(pl.program_id(0))
        tiles = _attn_kv_tiles(kp_ref[...], kc_ref[...])
        s = [lax.dot_general(_pair(q_ref, j), tiles[j // 4][0], _NT, preferred_element_type=F32)
             for j in range(N_PAIR)]
        p = []
        for j in range(N_PAIR):
            pe, _ = _attn_probs(s[j][:, 0:_KEYS], sink_ref[0, 2 * j], valid)
            po, _ = _attn_probs(s[j][:, _KEYS:2 * _KEYS], sink_ref[0, 2 * j + 1], valid)
            p.append(jnp.concatenate([pe.astype(BF), po.astype(BF)], axis=1))
        for j in range(N_PAIR):
            o_ref[:, j * 128:(j + 1) * 128] = jnp.dot(p[j], tiles[j // 4][1],
                                                      preferred_element_type=F32).astype(BF)

    return _hosted_call(
        body, comm, (sinks, q, kv, kv), name="attn_fwd", grid=(nb,),
        in_specs=[pl.BlockSpec(memory_space=pltpu.SMEM),
                  pl.BlockSpec((BLK, D), lambda i: (i, 0)),
                  pl.BlockSpec((BLK, 256), lambda i: (jnp.maximum(i - 1, 0), 0)),
                  pl.BlockSpec((BLK, 256), lambda i: (i, 0))],
        out_specs=[pl.BlockSpec((BLK, D), lambda i: (i, 0))],
        out_shape=[jax.ShapeDtypeStruct((t, D), BF)],
        scratch_shapes=[], sem=("arbitrary",), nsteps=nb, step_fn=lambda: pl.program_id(0))


def _attn_bwd(q, kv, sinks, do, *, t, comm=None):
    nb = t // BLK
    last = nb - 1

    def body(sink_ref, q_ref, kp_ref, kc_ref, do_ref, dq_ref, dkv_ref, ds_ref, carry_ref):
        i = pl.program_id(0)

        @pl.when(i == 0)
        def _():
            ds_ref[...] = jnp.zeros_like(ds_ref)
            carry_ref[...] = jnp.zeros_like(carry_ref)

        @pl.when(i < nb)
        def _():
            valid = _attn_mask(i)
            tiles = _attn_kv_tiles(kp_ref[...], kc_ref[...])
            lane1 = lax.broadcasted_iota(jnp.int32, (1, 128), 1)
            dsink = jnp.zeros((1, 128), F32)
            s = [lax.dot_general(_pair(q_ref, j), tiles[j // 4][0], _NT, preferred_element_type=F32)
                 for j in range(N_PAIR)]
            dp = [lax.dot_general(_pair(do_ref, j), tiles[j // 4][1], _NT, preferred_element_type=F32)
                  for j in range(N_PAIR)]
            p_all, ds_all = [], []
            for j in range(N_PAIR):
                halves = []
                for par in range(2):
                    cols = slice(par * _KEYS, (par + 1) * _KEYS)
                    p, ps = _attn_probs(s[j][:, cols], sink_ref[0, 2 * j + par], valid)
                    dpj = dp[j][:, cols]
                    dd = jnp.sum(p * dpj, axis=-1, keepdims=True)
                    dsink = dsink + jnp.where(lane1 == 2 * j + par,
                                              -jnp.sum(ps * dd, axis=0, keepdims=True), 0.0)
                    halves.append((p.astype(BF), (p * (dpj - dd)).astype(BF)))
                p_all.append(jnp.concatenate([halves[0][0], halves[1][0]], axis=1))
                ds_all.append(jnp.concatenate([halves[0][1], halves[1][1]], axis=1))
            for j in range(N_PAIR):
                dq_ref[:, j * 128:(j + 1) * 128] = (
                    jnp.dot(ds_all[j], tiles[j // 4][0], preferred_element_type=F32) * SCALE).astype(BF)
            ds_ref[...] += dsink
            gk, gv = [], []
            for h in range(2):
                grp = range(4 * h, 4 * h + 4)
                q_rows = jnp.concatenate([_pair(q_ref, j) for j in grp], axis=0)
                do_rows = jnp.concatenate([_pair(do_ref, j) for j in grp], axis=0)
                g_k = lax.dot_general(jnp.concatenate([ds_all[j] for j in grp], axis=0), q_rows, _TN,
                                      preferred_element_type=F32)
                g_v = lax.dot_general(jnp.concatenate([p_all[j] for j in grp], axis=0), do_rows, _TN,
                                      preferred_element_type=F32)
                gk.append((g_k[0:_KEYS], g_k[_KEYS:2 * _KEYS]))
                gv.append((g_v[0:_KEYS], g_v[_KEYS:2 * _KEYS]))
            lo = lax.broadcasted_iota(jnp.int32, (2 * BLK, 128), 1) < HEAD
            zero = jnp.zeros((2 * BLK, 128), F32)

            def unpad(g):
                return (jnp.where(lo, g[0][0] + pltpu.roll(g[0][1], HEAD, 1), zero)
                        + jnp.where(lo, zero, pltpu.roll(g[1][0], HEAD, 1) + g[1][1]))

            dk = unpad(gk) * SCALE
            dv = unpad(gv)
            dkv_ref[:, 0:128] = (carry_ref[:, 0:128] + dk[0:BLK]).astype(BF)
            dkv_ref[:, 128:256] = (carry_ref[:, 128:256] + dv[0:BLK]).astype(BF)
            carry_ref[:, 0:128] = dk[BLK:2 * BLK]
            carry_ref[:, 128:256] = dv[BLK:2 * BLK]

        @pl.when(i == nb)
        def _():
            dkv_ref[...] = carry_ref[...].astype(BF)

    return _hosted_call(
        body, comm, (sinks, q, kv, kv, do), name="attn_bwd", grid=(nb + 1,),
        in_specs=[pl.BlockSpec(memory_space=pltpu.SMEM),
                  pl.BlockSpec((BLK, D), lambda i: (jnp.minimum(i, last), 0)),
                  pl.BlockSpec((BLK, 256), lambda i: (jnp.clip(i - 1, 0, last), 0)),
                  pl.BlockSpec((BLK, 256), lambda i: (jnp.minimum(i, last), 0)),
                  pl.BlockSpec((BLK, D), lambda i: (jnp.minimum(i, last), 0))],
        out_specs=[pl.BlockSpec((BLK, D), lambda i: (jnp.minimum(i, last), 0)),
                   pl.BlockSpec((BLK, 256), lambda i: (jnp.maximum(i - 1, 0), 0)),
                   pl.BlockSpec((1, 128), lambda i: (0, 0))],
        out_shape=[jax.ShapeDtypeStruct((t, D), BF), jax.ShapeDtypeStruct((t, 256), BF),
                   jax.ShapeDtypeStruct((1, 128), F32)],
        scratch_shapes=[pltpu.VMEM((BLK, 256), F32)], sem=("arbitrary",), nsteps=nb + 1,
        step_fn=lambda: pl.program_id(0))


def _split3(v):
    h = v.astype(BF)
    r = v - h.astype(F32)
    m = r.astype(BF)
    lo = (r - m.astype(F32)).astype(BF)
    return jnp.concatenate([h, m, lo], axis=1)


def _apply01(mat, v):
    n = v.shape[1]
    r = jnp.dot(mat, _split3(v), preferred_element_type=F32)
    return r[:, 0:n] + r[:, n:2 * n] + r[:, 2 * n:3 * n]


def _hgrn_gates(hq, hf, lb):
    sq = _sig(hq)
    sg = _sig(hf)
    f = lb + (1.0 - lb) * sg
    return hq * sq, (1.0 - lb) * (1.0 - sg), jnp.log(f), sq, sg, f


def _tri(upper):
    r = lax.broadcasted_iota(jnp.int32, (CH, CH), 0)
    c = lax.broadcasted_iota(jnp.int32, (CH, CH), 1)
    return (c >= r) if upper else (c <= r)


def _lb_from_logits(lg_ref):
    return 1.0 / (1.0 + jnp.exp(lg_ref[1:2, :] - lg_ref[0:1, :]))


def _hgrn_fwd(h4, logits, norm_g, *, t):
    nc = t // CH
    nt_dims = (((1,), (1,)), ((), ()))
    tn_dims = (((0,), (0,)), ((), ()))

    def body(h_ref, lg_ref, ng_ref, y_ref, o_ref, st_ref, s_scr, b_scr, qa_s, ka_s, qb_s, kb_s, v_s):
        @pl.when(pl.program_id(0) == 0)
        def _():
            s_scr[...] = jnp.zeros_like(s_scr)

        heads = [slice(h * HG_K, (h + 1) * HG_K) for h in range(HG_HEADS)]
        causal = _tri(False)
        q, k, g, _, _, _ = _hgrn_gates(h_ref[:, 0:D], h_ref[:, D:2 * D], _lb_from_logits(lg_ref))
        b_scr[...] = _apply01(jnp.where(causal, 1.0, 0.0).astype(BF), g)
        b = b_scr[...]
        b_mid = b_scr[CH // 2 - 1:CH // 2, :]
        b_last = b_scr[CH - 1:CH, :]
        qa_s[...] = (q * jnp.exp(b - b_mid)).astype(BF)
        ka_s[...] = (k * jnp.exp(b_mid - b)).astype(BF)
        qb_s[...] = (q * jnp.exp(b)).astype(BF)
        kb_s[...] = (k * jnp.exp(b_last - b)).astype(BF)
        v_s[...] = h_ref[:, 2 * D:3 * D].astype(BF)
        dec = jnp.exp(b_last)
        st_ref[0] = s_scr[...]
        a = [jnp.where(causal, lax.dot_general(qa_s[:, sl], ka_s[:, sl], nt_dims, preferred_element_type=F32),
                       0.0).astype(BF) for sl in heads]
        for h, sl in enumerate(heads):
            o_ref[:, sl] = (jnp.dot(a[h], v_s[:, sl], preferred_element_type=F32)
                            + lax.dot_general(qb_s[:, sl], s_scr[h].astype(BF), nt_dims,
                                              preferred_element_type=F32))
        for h, sl in enumerate(heads):
            s_scr[h] = dec[:, sl] * s_scr[h] + lax.dot_general(v_s[:, sl], kb_s[:, sl], tn_dims,
                                                               preferred_element_type=F32)
        for h, sl in enumerate(heads):
            o = o_ref[:, sl]
            on = o * lax.rsqrt(jnp.mean(o * o, axis=-1, keepdims=True) + EPS)
            y_ref[:, sl] = (on * ng_ref[:, sl] * _sig(h_ref[:, 3 * D + h * HG_K:3 * D + (h + 1) * HG_K])).astype(BF)

    half = lambda: pltpu.VMEM((CH, D), BF)
    return _pcall(body, name="hgrn_fwd", grid=(nc,),
                  in_specs=[pl.BlockSpec((CH, 4 * D), lambda n: (n, 0)),
                            pl.BlockSpec((2, D), lambda n: (0, 0)),
                            pl.BlockSpec((1, D), lambda n: (0, 0))],
                  out_specs=[pl.BlockSpec((CH, D), lambda n: (n, 0)),
                             pl.BlockSpec((CH, D), lambda n: (n, 0)),
                             pl.BlockSpec((1, HG_HEADS, HG_K, HG_K), lambda n: (n, 0, 0, 0))],
                  out_shape=[jax.ShapeDtypeStruct((t, D), BF), jax.ShapeDtypeStruct((t, D), F32),
                             jax.ShapeDtypeStruct((nc, HG_HEADS, HG_K, HG_K), F32)],
                  scratch_shapes=[pltpu.VMEM((HG_HEADS, HG_K, HG_K), F32), pltpu.VMEM((CH, D), F32),
                                  half(), half(), half(), half(), half()],
                  compiler_params=_cp(("arbitrary",)))(h4, logits, norm_g)


def _hgrn_bwd(h4, logits, norm_g, o_pre, states, dy, *, t, comm=None):
    nc = t // CH
    nt_dims = (((1,), (1,)), ((), ()))
    tn_dims = (((0,), (0,)), ((), ()))

    def body(h_ref, lg_ref, ng_ref, o_ref, st_ref, dy_ref, dh_ref, dlg_ref, dng_ref, ds_scr, dlb_scr,
             b_scr, tail_s, e_qa, e_ka, e_qb, e_kb, q_s, k_s, dqa_s, dka_s, dqb_s, dkb_s,
             qa_s, ka_s, qb_s, kb_s, v_s, do_s):
        n = pl.program_id(0)

        @pl.when(n == 0)
        def _():
            ds_scr[...] = jnp.zeros_like(ds_scr)
            dlb_scr[...] = jnp.zeros_like(dlb_scr)
            dng_ref[...] = jnp.zeros_like(dng_ref)

        heads = [slice(h * HG_K, (h + 1) * HG_K) for h in range(HG_HEADS)]
        lb = _lb_from_logits(lg_ref)
        causal = _tri(False)
        q, k, g, _, _, _ = _hgrn_gates(h_ref[:, 0:D], h_ref[:, D:2 * D], lb)
        b_scr[...] = _apply01(jnp.where(causal, 1.0, 0.0).astype(BF), g)
        b = b_scr[...]
        b_mid = b_scr[CH // 2 - 1:CH // 2, :]
        b_last = b_scr[CH - 1:CH, :]
        q_s[...] = q
        k_s[...] = k
        for e_ref, s_ref, base, expo in ((e_qa, qa_s, q, b - b_mid), (e_ka, ka_s, k, b_mid - b),
                                         (e_qb, qb_s, q, b), (e_kb, kb_s, k, b_last - b)):
            e = jnp.exp(expo)
            e_ref[...] = e
            s_ref[...] = (base * e).astype(BF)
        v_s[...] = h_ref[:, 2 * D:3 * D].astype(BF)
        dec = jnp.exp(b_last)
        for h, sl in enumerate(heads):
            gcol = slice(3 * D + h * HG_K, 3 * D + (h + 1) * HG_K)
            ngh = ng_ref[:, sl]
            sgate = _sig(h_ref[:, gcol])
            o = o_ref[:, sl]
            r = lax.rsqrt(jnp.mean(o * o, axis=-1, keepdims=True) + EPS)
            on = o * r
            dyh = dy_ref[:, sl]
            dh_ref[:, gcol] = (dyh * on * ngh * sgate * (1.0 - sgate)).astype(BF)
            dng_ref[:, sl] += jnp.sum(dyh * on * sgate, axis=0, keepdims=True)
            don = dyh * ngh * sgate
            do_s[:, sl] = (r * (don - on * jnp.mean(don * on, axis=-1, keepdims=True))).astype(BF)
        a = [jnp.where(causal, lax.dot_general(qa_s[:, sl], ka_s[:, sl], nt_dims, preferred_element_type=F32),
                       0.0).astype(BF) for sl in heads]
        da = [jnp.where(causal, lax.dot_general(do_s[:, sl], v_s[:, sl], nt_dims, preferred_element_type=F32),
                        0.0).astype(BF) for sl in heads]
        for h, sl in enumerate(heads):
            dh_ref[:, 2 * D + h * HG_K:2 * D + (h + 1) * HG_K] = (
                lax.dot_general(a[h], do_s[:, sl], tn_dims, preferred_element_type=F32)
                + lax.dot_general(kb_s[:, sl], ds_scr[h].astype(BF), nt_dims, preferred_element_type=F32)
            ).astype(BF)
        for h, sl in enumerate(heads):
            dqa_s[:, sl] = jnp.dot(da[h], ka_s[:, sl], preferred_element_type=F32)
        for h, sl in enumerate(heads):
            dka_s[:, sl] = lax.dot_general(da[h], qa_s[:, sl], tn_dims, preferred_element_type=F32)
        for h, sl in enumerate(heads):
            dqb_s[:, sl] = jnp.dot(do_s[:, sl], st_ref[0, h].astype(BF), preferred_element_type=F32)
        for h, sl in enumerate(heads):
            dkb_s[:, sl] = jnp.dot(v_s[:, sl], ds_scr[h].astype(BF), preferred_element_type=F32)
        for h, sl in enumerate(heads):
            tail_s[:, sl] = jnp.sum(dec[:, sl] * st_ref[0, h] * ds_scr[h], axis=0, keepdims=True)
        for h, sl in enumerate(heads):
            ds_scr[h] = (lax.dot_general(do_s[:, sl], qb_s[:, sl], tn_dims, preferred_element_type=F32)
                         + dec[:, sl] * ds_scr[h])
        qv, kv = q_s[...], k_s[...]
        dqa, dka, dqb, dkb = dqa_s[...], dka_s[...], dqb_s[...], dkb_s[...]
        eqa, eka, eqb, ekb = e_qa[...], e_ka[...], e_qb[...], e_kb[...]
        dkb_kb = dkb * (kv * ekb)
        db_last = jnp.sum(dkb_kb, axis=0, keepdims=True) + tail_s[...]
        last_row = lax.broadcasted_iota(jnp.int32, (CH, D), 0) == CH - 1
        db = (dqa * (qv * eqa) - dka * (kv * eka) + dqb * (qv * eqb) - dkb_kb
              + jnp.where(last_row, db_last, 0.0))
        dg = _apply01(jnp.where(_tri(True), 1.0, 0.0).astype(BF), db)
        dq = dqa * eqa + dqb * eqb
        dk = dka * eka + dkb * ekb
        hq = h_ref[:, 0:D]
        _, _, _, sq, sg, f = _hgrn_gates(hq, h_ref[:, D:2 * D], lb)
        dh_ref[:, 0:D] = (dq * sq * (1.0 + hq * (1.0 - sq))).astype(BF)
        dfk = dg / f - dk
        dh_ref[:, D:2 * D] = ((1.0 - lb) * dfk * sg * (1.0 - sg)).astype(BF)
        dlb_scr[...] += jnp.sum((1.0 - sg) * dfk, axis=0, keepdims=True)

        @pl.when(n == nc - 1)
        def _():
            dl0 = dlb_scr[...] * lb * (1.0 - lb)
            dlg_ref[0:1, :] = dl0
            dlg_ref[1:2, :] = -dl0

    rev = lambda n: (nc - 1 - n, 0)
    return _hosted_call(
        body, comm, (h4, logits, norm_g, o_pre, states, dy), name="hgrn_bwd", grid=(nc,),
        in_specs=[pl.BlockSpec((CH, 4 * D), rev),
                  pl.BlockSpec((2, D), lambda n: (0, 0)),
                  pl.BlockSpec((1, D), lambda n: (0, 0)),
                  pl.BlockSpec((CH, D), rev),
                  pl.BlockSpec((1, HG_HEADS, HG_K, HG_K), lambda n: (nc - 1 - n, 0, 0, 0)),
                  pl.BlockSpec((CH, D), rev)],
        out_specs=[pl.BlockSpec((CH, 4 * D), rev),
                   pl.BlockSpec((2, D), lambda n: (0, 0)),
                   pl.BlockSpec((1, D), lambda n: (0, 0))],
        out_shape=[jax.ShapeDtypeStruct((t, 4 * D), BF), jax.ShapeDtypeStruct((2, D), F32),
                   jax.ShapeDtypeStruct((1, D), F32)],
        scratch_shapes=([pltpu.VMEM((HG_HEADS, HG_K, HG_K), F32), pltpu.VMEM((1, D), F32),
                         pltpu.VMEM((CH, D), F32), pltpu.VMEM((1, D), F32)]
                        + [pltpu.VMEM((CH, D), F32)] * 10 + [pltpu.VMEM((CH, D), BF)] * 6),
        sem=("arbitrary",), nsteps=nc, step_fn=lambda: pl.program_id(0))


def _place():
    x, y, c = lax.axis_index("x"), lax.axis_index("y"), lax.axis_index("c")
    return x, y, c, [(1 - x, y), (x, 1 - y), (1 - x, 1 - y)]


def _gather_comm(shards, mid):
    n = len(shards)
    r = [s.shape[0] for s in shards]

    def tools(ins, outs, sems):
        send_sems, recv_sems, local_sems = sems
        x, y, c, chips = _place()
        me, sib = (x, y, c), (x, y, 1 - c)

        def rows(w, dev):
            return outs[w].at[pl.ds((4 * dev[0] + 2 * dev[1] + dev[2]) * r[w], r[w]), :]

        def copy(kind, w, block, to, src=None):
            return pltpu.make_async_remote_copy(
                src_ref=rows(w, block) if src is None else src, dst_ref=rows(w, block),
                send_sem=send_sems.at[kind], recv_sem=recv_sems.at[kind], device_id=to, device_id_type=MESH)

        def all_of(kind):
            whole = outs[0].at[pl.ds(0, sum(r)), :]
            return pltpu.make_async_remote_copy(
                src_ref=whole, dst_ref=whole, send_sem=send_sems.at[kind], recv_sem=recv_sems.at[kind],
                device_id=me, device_id_type=MESH)

        mine = [pltpu.make_async_copy(ins[w], rows(w, me), local_sems.at[w]) for w in range(n)]
        return c, chips, me, sib, copy, all_of, mine

    def start(ins, outs, sems):
        c, chips, me, sib, copy, _, mine = tools(ins, outs, sems)
        for cp in mine:
            cp.start()
        for w in range(n):
            copy(0, w, me, sib, src=ins[w]).start()
            for j, chip in enumerate(chips):
                copy(1 + j, w, me, (*chip, c), src=ins[w]).start()

    def pass_on(ins, outs, sems):
        c, chips, _, sib, copy, all_of, _ = tools(ins, outs, sems)
        for j, chip in enumerate(chips):
            all_of(1 + j).wait_recv()
            for w in range(n):
                copy(4 + j, w, (*chip, c), sib).start()

    def finish(ins, outs, sems):
        _, _, _, _, _, all_of, mine = tools(ins, outs, sems)
        all_of(0).wait_recv()
        for j in range(3):
            all_of(4 + j).wait_recv()
        for kind in range(7):
            all_of(kind).wait_send()
        for cp in mine:
            cp.wait()

    return _Comm(shards, [jax.ShapeDtypeStruct((N_DEV * rw, D), BF) for rw in r],
                 [pltpu.SemaphoreType.DMA((7,)), pltpu.SemaphoreType.DMA((7,)), pltpu.SemaphoreType.DMA((n,))],
                 [(0.0, start), (mid, pass_on), (1.0, finish)])


def _pair_comm(grads):
    n = len(grads)
    r = [g.shape[0] // N_DEV for g in grads]

    def start(ins, outs, sems):
        send_sems, recv_sems = sems
        x, y, c, _ = _place()
        for w in range(n):
            for a in range(N_CHIP):
                pltpu.make_async_remote_copy(
                    src_ref=ins[w].at[pl.ds((2 * a + 1 - c) * r[w], r[w]), :], dst_ref=outs[w].at[a],
                    send_sem=send_sems.at[w], recv_sem=recv_sems.at[w],
                    device_id=(x, y, 1 - c), device_id_type=MESH).start()

    def finish(ins, outs, sems):
        send_sems, recv_sems = sems
        x, y, c, _ = _place()
        for w in range(n):
            pltpu.make_async_remote_copy(
                src_ref=outs[w], dst_ref=outs[w], send_sem=send_sems.at[w], recv_sem=recv_sems.at[w],
                device_id=(x, y, c), device_id_type=MESH).wait()

    return _Comm(grads, [jax.ShapeDtypeStruct((N_CHIP, rw, D), BF) for rw in r],
                 [pltpu.SemaphoreType.DMA((n,)), pltpu.SemaphoreType.DMA((n,))],
                 [(0.0, start), (1.0, finish)])


def _pair_add(grad, got, core, *, name):
    r = got.shape[1]

    def body(c_ref, g_ref, got_ref, o_ref):
        o_ref[0] = (g_ref[...].astype(F32) + got_ref[0].astype(F32)).astype(BF)

    grid_spec = pltpu.PrefetchScalarGridSpec(
        num_scalar_prefetch=1, grid=(N_CHIP,),
        in_specs=[pl.BlockSpec((r, D), lambda a, c_ref: (2 * a + c_ref[0], 0)),
                  pl.BlockSpec((1, r, D), lambda a, c_ref: (a, 0, 0))],
        out_specs=pl.BlockSpec((1, r, D), lambda a, c_ref: (a, 0, 0)))
    return _pcall(body, name=name, grid_spec=grid_spec,
                  out_shape=jax.ShapeDtypeStruct((N_CHIP, r, D), BF),
                  compiler_params=_cp(("parallel",)))(core, grad, got)


def _chip_comm(pair_sums):
    n = len(pair_sums)
    r = [p.shape[1] for p in pair_sums]
    off = [sum(r[:w]) for w in range(n)]

    def tools(ins, outs, sems):
        send_sems, recv_sems, local_sems = sems
        x, y, c, chips = _place()
        my_chip = 2 * x + y

        def slot(w):
            return outs[0].at[my_chip, pl.ds(off[w], r[w]), :]

        own = [pltpu.make_async_copy(ins[w].at[my_chip], slot(w), local_sems.at[w]) for w in range(n)]
        return x, y, c, chips, my_chip, slot, own, send_sems, recv_sems

    def start(ins, outs, sems):
        x, y, c, chips, my_chip, slot, own, send_sems, recv_sems = tools(ins, outs, sems)
        for cp in own:
            cp.start()
        for j, chip in enumerate(chips):
            for w in range(n):
                pltpu.make_async_remote_copy(
                    src_ref=ins[w].at[2 * chip[0] + chip[1]], dst_ref=slot(w), send_sem=send_sems.at[j],
                    recv_sem=recv_sems.at[j], device_id=(*chip, c), device_id_type=MESH).start()

    def finish(ins, outs, sems):
        x, y, c, chips, my_chip, slot, own, send_sems, recv_sems = tools(ins, outs, sems)
        whole = outs[0].at[my_chip]
        for j in range(3):
            pltpu.make_async_remote_copy(
                src_ref=whole, dst_ref=whole, send_sem=send_sems.at[j], recv_sem=recv_sems.at[j],
                device_id=(x, y, c), device_id_type=MESH).wait()
        for cp in own:
            cp.wait()

    return _Comm(pair_sums, [jax.ShapeDtypeStruct((N_CHIP, sum(r), D), BF)],
                 [pltpu.SemaphoreType.DMA((3,)), pltpu.SemaphoreType.DMA((3,)), pltpu.SemaphoreType.DMA((n,))],
                 [(0.0, start), (1.0, finish)])


def _sum_chips(parts, *, tr, name):
    rows = parts.shape[1]

    def body(p_ref, o_ref):
        acc = p_ref[0].astype(F32)
        for a in range(1, N_CHIP):
            acc = acc + p_ref[a].astype(F32)
        o_ref[...] = acc

    return _pcall(body, name=name, grid=(rows // tr,),
                  in_specs=[pl.BlockSpec((N_CHIP, tr, D), lambda i: (0, i, 0))],
                  out_specs=_row_spec(tr, D), out_shape=jax.ShapeDtypeStruct((rows, D), F32),
                  compiler_params=_cp(("parallel",)))(parts)


def _adam_math(w, g, m, v):
    m = ADAM_B1 * m + (1.0 - ADAM_B1) * g
    v = ADAM_B2 * v + (1.0 - ADAM_B2) * (g * g)
    m_hat = m / (1.0 - ADAM_B1 ** ADAM_STEP)
    v_hat = v / (1.0 - ADAM_B2 ** ADAM_STEP)
    delta = -ADAM_LR * (m_hat / (jnp.sqrt(v_hat) + ADAM_EPS) + ADAM_WD * w)
    return delta, m, v


def _small_allreduce_adam(gpart, w, m, v):
    def body(g_ref, w_ref, m_ref, v_ref, gs_ref, d_ref, mo_ref, vo_ref, gath, send_sems, recv_sems):
        x, y, c, _ = _place()
        me = 4 * x + 2 * y + c
        gath[me] = g_ref[...]
        cps = []
        for d in range(1, N_DEV):
            peer = (x ^ (d >> 2), y ^ ((d >> 1) & 1), c ^ (d & 1))
            cps.append(pltpu.make_async_remote_copy(
                src_ref=g_ref, dst_ref=gath.at[me], send_sem=send_sems.at[d - 1],
                recv_sem=recv_sems.at[d - 1], device_id=peer, device_id_type=MESH))
        for cp in cps:
            cp.start()
        for cp in cps:
            cp.wait()
        g = gath[0]
        for k in range(1, N_DEV):
            g = g + gath[k]
        gs_ref[...] = g
        d_ref[...], mo_ref[...], vo_ref[...] = _adam_math(w_ref[...], g, m_ref[...], v_ref[...])

    shape = jax.ShapeDtypeStruct((SMALL_ROWS, D), F32)
    vm = pl.BlockSpec(memory_space=pltpu.VMEM)
    return _pcall(body, name="small_allreduce_adam", in_specs=[vm] * 4, out_specs=[vm] * 4,
                  out_shape=[shape] * 4,
                  scratch_shapes=[pltpu.VMEM((N_DEV, SMALL_ROWS, D), F32),
                                  pltpu.SemaphoreType.DMA((N_DEV - 1,)), pltpu.SemaphoreType.DMA((N_DEV - 1,))],
                  compiler_params=pltpu.CompilerParams(has_side_effects=True))(gpart, w, m, v)


def _adam(w, g, m, v, *, name):
    rows, cols = w.shape
    tr = rows if rows <= 512 else 256

    def body(w_ref, g_ref, m_ref, v_ref, d_ref, mo_ref, vo_ref):
        d_ref[...], mo_ref[...], vo_ref[...] = _adam_math(w_ref[...], g_ref[...], m_ref[...], v_ref[...])

    spec = pl.BlockSpec((tr, cols), lambda i: (i, 0))
    return _pcall(body, name=name, grid=(rows // tr,), in_specs=[spec] * 4, out_specs=[spec] * 3,
                  out_shape=[jax.ShapeDtypeStruct((rows, cols), F32)] * 3,
                  compiler_params=_cp(("parallel",)))(w, g, m, v)


def _step(x, tgt, shards, norm_mix_g, b_in, sinks, logits, hgrn_norm_g, norm_ffn_g, norm_final_g):
    t = x.shape[0]
    big = dict(tm=1024, tn=1024, tk=4096)
    core = lax.axis_index("c").astype(jnp.int32).reshape(1)

    (win_t,) = _run_comm(_gather_comm(shards[0:1], 0.0), name="gather_w_in")
    u1 = _rms_fwd(x, norm_mix_g, tm=512, name="rms_mix")
    (q, kv, h4, gates), (wg_t, wu_t, wd) = _inproj_fwd(u1, win_t, b_in, t=t,
                                                       comm=_gather_comm(shards[1:4], 0.8))
    (y_attn,), (wba, wbh, wout) = _attn_fwd(q, kv, sinks, t=t, comm=_gather_comm(shards[4:7], 0.7))
    y_hgrn, o_pre, states = _hgrn_fwd(h4, logits, hgrn_norm_g, t=t)
    ya = _mm(y_attn, wba, m=t, n=D, k=D, name="branch_attn", **big)
    yb = _mm(y_hgrn, wbh, m=t, n=D, k=D, name="branch_hgrn", **big)
    merged = _merge_fwd(gates, ya, yb, tm=512)
    h1 = _mm(merged, wout, m=t, n=D, k=D, resid=x, name="out_proj", **big)
    u2 = _rms_fwd(h1, norm_ffn_g, tm=512, name="rms_ffn")
    gt = _mm(u2, wg_t, m=t, n=FFN, k=D, tb=True, tm=1024, tn=FFN // 2, tk=D, name="ffn_gate")
    up = _mm(u2, wu_t, m=t, n=FFN, k=D, tb=True, tm=1024, tn=FFN // 2, tk=D, name="ffn_up")
    z = _swiglu_fwd(gt, up, tm=256)
    h2 = _mm(z, wd, m=t, n=D, k=FFN, resid=h1, name="ffn_down", **big)
    dh2, dh2_b, d_norm_final, loss_row = _loss_head(h2, tgt, norm_final_g, tm=512)

    dz = _mm(dh2_b, wd, m=t, n=FFN, k=D, tb=True, tm=1024, tn=FFN // 2, tk=D, name="d_z")
    d_wd = _mm(z, dh2_b, m=FFN, n=D, k=t, ta=True, tm=256, tn=D, tk=4096, out_dtype=BF, name="d_w_down")
    dgt, dup = _swiglu_bwd(dz, gt, up, tm=256)
    du2 = _mm(dgt, wg_t, m=t, n=D, k=FFN, name="d_u2_gate", **big)
    du2 = _mm(dup, wu_t, m=t, n=D, k=FFN, resid=du2, name="d_u2_up", **big)
    d_wg = _mm(dgt, u2, m=FFN, n=D, k=t, ta=True, tm=256, tn=D, tk=4096, out_dtype=BF, name="d_w_gate")
    d_wu = _mm(dup, u2, m=FFN, n=D, k=t, ta=True, tm=256, tn=D, tk=4096, out_dtype=BF, name="d_w_up")
    dh1, dh1_b, d_norm_ffn = _rms_bwd(du2, h1, norm_ffn_g, dh2, tm=512, name="rms_ffn_bwd")
    dmerged = _mm(dh1_b, wout, m=t, n=D, k=D, tb=True, name="d_merged", **big)
    d_wout = _mm(merged, dh1_b, m=D, n=D, k=t, ta=True, tm=256, tn=D, tk=4096, out_dtype=BF, name="d_w_out")
    dya, dyb, dgates = _merge_bwd(dmerged, gates, ya, yb, tm=512)
    dy_attn = _mm(dya, wba, m=t, n=D, k=D, tb=True, out_dtype=BF, name="d_y_attn", **big)
    dy_hgrn = _mm(dyb, wbh, m=t, n=D, k=D, tb=True, name="d_y_hgrn", **big)
    d_wba = _mm(y_attn, dya, m=D, n=D, k=t, ta=True, tm=256, tn=D, tk=4096, out_dtype=BF, name="d_w_ba")
    d_wbh = _mm(y_hgrn, dyb, m=D, n=D, k=t, ta=True, tm=256, tn=D, tk=4096, out_dtype=BF, name="d_w_bh")
    rest = (d_wg, d_wu, d_wd, d_wba, d_wbh, d_wout)
    (dq, dkv, d_sinks), got = _attn_bwd(q, kv, sinks, dy_attn, t=t, comm=_pair_comm(rest))
    pair = [_pair_add(g, r, core, name="pair_add_%d" % i) for i, (g, r) in enumerate(zip(rest, got))]
    (dh4, d_logits, d_hgrn_norm), (parts_rest,) = _hgrn_bwd(h4, logits, hgrn_norm_g, o_pre, states, dy_hgrn,
                                                             t=t, comm=_chip_comm(pair))
    g_rest = _sum_chips(parts_rest, tr=parts_rest.shape[1] // 2, name="sum_chips_rest")
    dps = (dq, dkv, dh4, dgates)
    d_win_t, d_b_in = _inproj_bwd_w(dps, u1, t=t)
    (du1,), got_in = _inproj_bwd_x(dps, win_t, t=t, part=0, comm=_pair_comm([d_win_t]))
    pair_in = _pair_add(d_win_t, got_in[0], core, name="pair_add_w_in")
    (du1,), (parts_in,) = _inproj_bwd_x(dps, win_t, t=t, part=1, into=du1, comm=_chip_comm([pair_in]))
    g_in = _sum_chips(parts_in, tr=parts_in.shape[1] // 2, name="sum_chips_w_in")
    grad_x, _, d_norm_mix = _rms_bwd(du1, x, norm_mix_g, dh1, tm=512, name="rms_mix_bwd")

    small_grads = (d_norm_mix, d_b_in, d_sinks, d_logits, d_hgrn_norm, d_norm_ffn, d_norm_final)
    return loss_row, grad_x, g_in, g_rest, small_grads


def _pack_small(norm_mix, b_in, sinks, logits, hgrn_norm, norm_ffn, norm_final, extra=None):
    pad = lambda a, n: jnp.pad(a.reshape(1, -1), ((0, 0), (0, n - a.size)))
    rows = [norm_mix.reshape(1, D), hgrn_norm.reshape(1, D), norm_ffn.reshape(1, D), norm_final.reshape(1, D),
            logits.reshape(2, D), pad(sinks.reshape(-1)[:16], D),
            jnp.zeros((1, D), F32) if extra is None else pad(extra, D),
            pad(b_in, 8 * D).reshape(8, D)]
    return jnp.concatenate(rows, axis=0).astype(F32)


def _unpack_small(p):
    return dict(norm_mix_g=p[0:1], hgrn_norm_g=p[1:2], norm_ffn_g=p[2:3], norm_final_g=p[3],
                hgrn_lb_logits=p[4:6], attn_sinks=p[6:7, 0:16], extra=p[7],
                b_in=p[8:16].reshape(1, 8 * D)[:, :IN_W])


def kernel(x, norm_mix_g, w_in, b_in, attn_sinks, hgrn_lb_logits, hgrn_norm_g, w_branch_attn, w_branch_hgrn, w_out, norm_ffn_g, w_ffn_gate, w_ffn_up, w_ffn_down, norm_final_g, loss_target, m_norm_mix_g, m_w_in, m_b_in, m_attn_sinks, m_hgrn_lb_logits, m_hgrn_norm_g, m_w_branch_attn, m_w_branch_hgrn, m_w_out, m_norm_ffn_g, m_w_ffn_gate, m_w_ffn_up, m_w_ffn_down, m_norm_final_g, v_norm_mix_g, v_w_in, v_b_in, v_attn_sinks, v_hgrn_lb_logits, v_hgrn_norm_g, v_w_branch_attn, v_w_branch_hgrn, v_w_out, v_norm_ffn_g, v_w_ffn_gate, v_w_ffn_up, v_w_ffn_down, v_norm_final_g):
    shards = [w_in[0].T.astype(BF), w_ffn_gate[0].T.astype(BF), w_ffn_up[0].T.astype(BF),
              w_ffn_down[0].astype(BF), w_branch_attn[0].astype(BF), w_branch_hgrn[0].astype(BF),
              w_out[0].astype(BF)]
    loss_row, grad_x, g_in, g_rest, small_grads = _step(
        x[0], loss_target[0], shards, norm_mix_g, b_in, attn_sinks, hgrn_lb_logits, hgrn_norm_g,
        norm_ffn_g, norm_final_g.reshape(1, D))

    d_norm_mix, d_b_in, d_sinks, d_logits, d_hgrn_norm, d_norm_ffn, d_norm_final = small_grads
    g_small = _pack_small(d_norm_mix, d_b_in, d_sinks[:, :16], d_logits, d_hgrn_norm, d_norm_ffn,
                          d_norm_final, extra=loss_row[0, 0:1])
    w_small = _pack_small(norm_mix_g, b_in, attn_sinks, hgrn_lb_logits, hgrn_norm_g, norm_ffn_g, norm_final_g)
    m_small = _pack_small(m_norm_mix_g, m_b_in, m_attn_sinks, m_hgrn_lb_logits, m_hgrn_norm_g, m_norm_ffn_g,
                          m_norm_final_g)
    v_small = _pack_small(v_norm_mix_g, v_b_in, v_attn_sinks, v_hgrn_lb_logits, v_hgrn_norm_g, v_norm_ffn_g,
                          v_norm_final_g)
    small = [_unpack_small(p) for p in _small_allreduce_adam(g_small, w_small, m_small, v_small)]
    loss = small[0]["extra"][0]

    names = ["w_in", "w_ffn_gate", "w_ffn_up", "w_ffn_down", "w_branch_attn", "w_branch_hgrn", "w_out"]
    w_full = dict(w_in=(w_in, m_w_in, v_w_in), w_ffn_gate=(w_ffn_gate, m_w_ffn_gate, v_w_ffn_gate),
                  w_ffn_up=(w_ffn_up, m_w_ffn_up, v_w_ffn_up), w_ffn_down=(w_ffn_down, m_w_ffn_down, v_w_ffn_down),
                  w_branch_attn=(w_branch_attn, m_w_branch_attn, v_w_branch_attn),
                  w_branch_hgrn=(w_branch_hgrn, m_w_branch_hgrn, v_w_branch_hgrn),
                  w_out=(w_out, m_w_out, v_w_out))
    big = {}
    for i, name in enumerate(names):
        g = g_in if i == 0 else g_rest[SLAB_OFF[i] - SLAB_R[0]:SLAB_OFF[i] - SLAB_R[0] + SLAB_R[i]]
        if i < 3:
            g = g.T
        wv, mv, vv = w_full[name]
        delta, new_m, new_v = _adam(wv[0], g, mv[0], vv[0], name="adam_" + name)
        big[name] = [a[None] for a in (g, delta, new_m, new_v)]

    order = ["norm_mix_g", "w_in", "b_in", "attn_sinks", "hgrn_lb_logits", "hgrn_norm_g", "w_branch_attn",
             "w_branch_hgrn", "w_out", "norm_ffn_g", "w_ffn_gate", "w_ffn_up", "w_ffn_down", "norm_final_g"]
    outs = [loss, grad_x[None]]
    for kind in range(4):
        for name in order:
            outs.append(big[name][kind] if name in big else small[kind][name])
    return tuple(outs)
```

```python
import math

import jax
import jax.numpy as jnp
from jax import lax
from jax.experimental import pallas as pl
from jax.experimental.pallas import tpu as pltpu

F32 = jnp.float32
BF = jnp.bfloat16
MESH = pl.DeviceIdType.MESH

D = 1024
HEAD = 64
N_PAIR = 8
BLK = 128
CH = 64
HG_HEADS = 8
HG_K = 128
FFN = 2816
IN_W = 7424
N_DEV = 8
N_CHIP = 4
EPS = 1e-6
NEG = -1e30
SCALE = 1.0 / math.sqrt(HEAD)
VMEM_LIMIT = 56 * 1024 * 1024
WT = 256

ADAM_LR, ADAM_B1, ADAM_B2, ADAM_EPS, ADAM_WD, ADAM_STEP = 0.001, 0.9, 0.999, 1e-08, 0.01, 10

SLAB_R = (IN_W // N_DEV, FFN // N_DEV, FFN // N_DEV, FFN // N_DEV, D // N_DEV, D // N_DEV, D // N_DEV)
SLAB_ROWS = sum(SLAB_R)
SLAB_OFF = tuple(sum(SLAB_R[:i]) for i in range(len(SLAB_R)))
N_W = len(SLAB_R)
GRP_OFF = (0, D // WT, (D + 256) // WT, (5 * D + 256) // WT)
GRP_N = (D // WT, 256 // WT, 4 * D // WT, 2 * D // WT)
SMALL_ROWS = 16


_NN = (((1,), (0,)), ((), ()))
_NT = (((1,), (1,)), ((), ()))
_TN = (((0,), (0,)), ((), ()))


def _pcall(body, **kw):
    return pl.pallas_call(body, **kw)


def _cp(sem=None, **kw):
    return pltpu.CompilerParams(dimension_semantics=sem, vmem_limit_bytes=VMEM_LIMIT, **kw)


def _sig(v):
    return 1.0 / (1.0 + jnp.exp(-v))


def _accum(ref, val, first):
    @pl.when(first)
    def _():
        ref[...] = val

    @pl.when(jnp.logical_not(first))
    def _():
        ref[...] += val


class _Comm:
    def __init__(self, ins, out_shapes, sem_shapes, phases):
        self.ins, self.out_shapes, self.sem_shapes, self.phases = list(ins), list(out_shapes), list(sem_shapes), phases


def _host(body, comm, n_in, n_out, n_scr, nsteps, step_fn):
    if comm is None:
        return body
    ci, co = len(comm.ins), len(comm.out_shapes)

    def wrapped(*refs):
        p = 0
        ins, p = refs[p:p + n_in], p + n_in
        cins, p = refs[p:p + ci], p + ci
        outs, p = refs[p:p + n_out], p + n_out
        couts, p = refs[p:p + co], p + co
        scr, p = refs[p:p + n_scr], p + n_scr
        csems = refs[p:]
        step = step_fn()
        for frac, fn in comm.phases:
            if frac < 1.0:
                @pl.when(step == int(round(frac * (nsteps - 1))))
                def _(fn=fn):
                    fn(cins, couts, csems)
        body(*ins, *outs, *scr)
        for frac, fn in comm.phases:
            if frac >= 1.0:
                @pl.when(step == nsteps - 1)
                def _(fn=fn):
                    fn(cins, couts, csems)

    return wrapped


def _hosted_call(body, comm, args, *, name, grid, in_specs, out_specs, out_shape, scratch_shapes, sem,
                 nsteps, step_fn, aliases=None):
    n_in, n_out, n_scr = len(in_specs), len(out_specs), len(scratch_shapes)
    args = list(args)
    extra = {}
    if comm is not None:
        in_specs = list(in_specs) + [_hbm_spec()] * len(comm.ins)
        out_specs = list(out_specs) + [_hbm_spec()] * len(comm.out_shapes)
        out_shape = list(out_shape) + comm.out_shapes
        scratch_shapes = list(scratch_shapes) + comm.sem_shapes
        args += comm.ins
        extra = dict(has_side_effects=True)
    outs = _pcall(_host(body, comm, n_in, n_out, n_scr, nsteps, step_fn), name=name, grid=grid,
                  in_specs=in_specs, out_specs=out_specs, out_shape=out_shape, scratch_shapes=scratch_shapes,
                  input_output_aliases=aliases or {}, compiler_params=_cp(sem, **extra))(*args)
    return list(outs[:n_out]), list(outs[n_out:])


def _run_comm(comm, *, name):
    ci, co = len(comm.ins), len(comm.out_shapes)

    def body(*refs):
        for _, fn in comm.phases:
            fn(refs[:ci], refs[ci:ci + co], refs[ci + co:])

    return _pcall(body, name=name, in_specs=[_hbm_spec()] * ci, out_specs=[_hbm_spec()] * co,
                  out_shape=comm.out_shapes, scratch_shapes=comm.sem_shapes,
                  compiler_params=pltpu.CompilerParams(has_side_effects=True))(*comm.ins)


def _hbm_spec():
    return pl.BlockSpec(memory_space=pl.ANY)


def _mm(a, b, *, m, n, k, tm, tn, tk, ta=False, tb=False, out_dtype=F32, resid=None, name):
    tm, tn, tk = min(tm, m), min(tn, n), min(tk, k)
    gm, gn, gk = m // tm, n // tn, k // tk
    assert gm * tm == m and gn * tn == n and gk * tk == k, (name, m, n, k, tm, tn, tk)
    a_spec = (pl.BlockSpec((tk, tm), lambda i, j, l: (l, i)) if ta
              else pl.BlockSpec((tm, tk), lambda i, j, l: (i, l)))
    b_spec = (pl.BlockSpec((tn, tk), lambda i, j, l: (j, l)) if tb
              else pl.BlockSpec((tk, tn), lambda i, j, l: (l, j)))
    dims = (((0 if ta else 1,), (1 if tb else 0,)), ((), ()))
    ins, in_specs = [a, b], [a_spec, b_spec]
    if resid is not None:
        ins.append(resid)
        in_specs.append(pl.BlockSpec((tm, tn), lambda i, j, l: (i, j)))
    scratch = [pltpu.VMEM((tm, tn), F32)] if gk > 1 else []

    def body(*refs):
        it = iter(refs)
        a_ref, b_ref = next(it), next(it)
        resid_ref = next(it) if resid is not None else None
        o_ref = next(it)
        acc_ref = next(it) if gk > 1 else None
        l = pl.program_id(2)
        part = lax.dot_general(a_ref[...].astype(BF), b_ref[...].astype(BF), dims,
                               preferred_element_type=F32)

        def finish(acc):
            if resid_ref is not None:
                acc = acc + resid_ref[...].astype(F32)
            o_ref[...] = acc.astype(out_dtype)

        if gk == 1:
            finish(part)
        else:
            _accum(acc_ref, part, l == 0)

            @pl.when(l == gk - 1)
            def _():
                finish(acc_ref[...])

    return _pcall(body, name=name, grid=(gm, gn, gk), in_specs=in_specs,
                  out_specs=pl.BlockSpec((tm, tn), lambda i, j, l: (i, j)),
                  out_shape=jax.ShapeDtypeStruct((m, n), out_dtype), scratch_shapes=scratch,
                  compiler_params=_cp(("parallel", "parallel", "arbitrary")))(*ins)


def _fmm(lhs, rhs, extras, epilogue, outs, *, m, n, tm, tn, name):
    tm, tn = min(tm, m), min(tn, n)
    assert m % tm == 0 and n % tn == 0, (name, m, n, tm, tn)
    in_specs, args = [], []
    for a in lhs:
        in_specs.append(pl.BlockSpec((tm, a.shape[1]), lambda i, j: (i, 0)))
        args.append(a)
    for li, b, tb in rhs:
        k = lhs[li].shape[1]
        in_specs.append(pl.BlockSpec((tn, k), lambda i, j: (j, 0)) if tb
                        else pl.BlockSpec((k, tn), lambda i, j: (0, j)))
        args.append(b)
    for arr, w, col in extras:
        in_specs.append(pl.BlockSpec((tm, w), lambda i, j, col=col: (i, col(j))))
        args.append(arr)
    out_specs = [pl.BlockSpec((tm, w), lambda i, j, col=col: (i, col(j))) for _, _, w, col in outs]
    out_shape = [jax.ShapeDtypeStruct((m, total), dt) for dt, total, _, _ in outs]
    nl, nr, ne = len(lhs), len(rhs), len(extras)

    def body(*refs):
        prods = []
        for r, (li, _, tb) in enumerate(rhs):
            prods.append(lax.dot_general(refs[li][...], refs[nl + r][...], _NT if tb else _NN,
                                         preferred_element_type=F32))
        vals = epilogue(prods, [ref[...] for ref in refs[nl + nr:nl + nr + ne]])
        for o_ref, v in zip(refs[nl + nr + ne:], vals):
            o_ref[...] = v.astype(o_ref.dtype)

    return _pcall(body, name=name, grid=(m // tm, n // tn), in_specs=in_specs, out_specs=out_specs,
                  out_shape=out_shape, compiler_params=_cp(("parallel", "parallel")))(*args)


def _grp_of(i):
    return [jnp.logical_and(i >= GRP_OFF[g], i < GRP_OFF[g] + GRP_N[g]) for g in range(4)]


def _grp_idx(i, g):
    return jnp.clip(i - GRP_OFF[g], 0, GRP_N[g] - 1)


def _inproj_fwd(u, win_t, b_in, *, t, comm=None):
    n_tiles = IN_W // WT
    dims = (((1,), (1,)), ((), ()))
    dtypes = (BF, BF, F32, F32)

    def body(u_ref, w_ref, b_ref, *o_refs):
        i = pl.program_id(0)
        p = lax.dot_general(u_ref[...], w_ref[...], dims, preferred_element_type=F32) + b_ref[...]
        for g, pred in enumerate(_grp_of(i)):
            @pl.when(pred)
            def _(g=g):
                o_refs[g][...] = p.astype(dtypes[g])

    return _hosted_call(
        body, comm, (u, win_t, b_in), name="inproj_fwd", grid=(n_tiles,),
        in_specs=[pl.BlockSpec((t, D), lambda i: (0, 0)),
                  pl.BlockSpec((WT, D), lambda i: (i, 0)),
                  pl.BlockSpec((1, WT), lambda i: (0, i))],
        out_specs=[pl.BlockSpec((t, WT), lambda i, g=g: (0, _grp_idx(i, g))) for g in range(4)],
        out_shape=[jax.ShapeDtypeStruct((t, GRP_N[g] * WT), dtypes[g]) for g in range(4)],
        scratch_shapes=[], sem=("arbitrary",), nsteps=n_tiles, step_fn=lambda: pl.program_id(0))


def _inproj_bwd_x(dps, win_t, *, t, part, into=None, comm=None):
    n_tiles = IN_W // WT
    tm = t // 2

    def body(d0, d1, d2, d3, w_ref, *rest):
        o_ref, acc_ref = rest[-2], rest[-1]
        l = pl.program_id(0)
        w = w_ref[...]
        for g, (pred, d_ref) in enumerate(zip(_grp_of(l), (d0, d1, d2, d3))):
            @pl.when(pred)
            def _(d_ref=d_ref):
                _accum(acc_ref, jnp.dot(d_ref[...], w, preferred_element_type=F32), l == 0)

        @pl.when(l == n_tiles - 1)
        def _():
            o_ref[...] = acc_ref[...]

    in_specs = ([pl.BlockSpec((tm, WT), lambda l, g=g: (part, _grp_idx(l, g))) for g in range(4)]
                + [pl.BlockSpec((WT, D), lambda l: (l, 0))])
    args = list(dps) + [win_t]
    aliases = None
    if into is not None:
        in_specs.append(_hbm_spec())
        args.append(into)
        aliases = {5: 0}
    return _hosted_call(
        body, comm, args, name="inproj_bwd_x%d" % part, grid=(n_tiles,), in_specs=in_specs,
        out_specs=[pl.BlockSpec((tm, D), lambda l: (part, 0))],
        out_shape=[jax.ShapeDtypeStruct((t, D), F32)],
        scratch_shapes=[pltpu.VMEM((tm, D), F32)], sem=("arbitrary",), nsteps=n_tiles,
        step_fn=lambda: pl.program_id(0), aliases=aliases)


def _inproj_bwd_w(dps, u, *, t):
    n_tiles = IN_W // WT
    dims = (((0,), (0,)), ((), ()))

    def body(d0, d1, d2, d3, u_ref, o_ref, db_ref):
        i = pl.program_id(0)
        uv = u_ref[...]
        for g, (pred, d_ref) in enumerate(zip(_grp_of(i), (d0, d1, d2, d3))):
            @pl.when(pred)
            def _(d_ref=d_ref):
                dv = d_ref[...]
                o_ref[...] = lax.dot_general(dv, uv, dims, preferred_element_type=F32).astype(BF)
                db_ref[...] = jnp.sum(dv.astype(F32), axis=0, keepdims=True)

    return _pcall(body, name="inproj_bwd_w", grid=(n_tiles,),
                  in_specs=[pl.BlockSpec((t, WT), lambda i, g=g: (0, _grp_idx(i, g))) for g in range(4)]
                  + [pl.BlockSpec((t, D), lambda i: (0, 0))],
                  out_specs=[pl.BlockSpec((WT, D), lambda i: (i, 0)),
                             pl.BlockSpec((1, WT), lambda i: (0, i))],
                  out_shape=[jax.ShapeDtypeStruct((IN_W, D), BF), jax.ShapeDtypeStruct((1, IN_W), F32)],
                  compiler_params=_cp(("arbitrary",)))(*dps, u)


def _row_spec(tm, width, col=0):
    return pl.BlockSpec((tm, width), lambda i: (i, col))


def _vec_spec(width):
    return pl.BlockSpec((1, width), lambda i: (0, 0))


def _rms_fwd(x, g, *, tm, name):
    t = x.shape[0]
    tm = min(tm, t)

    def body(x_ref, g_ref, u_ref):
        xv = x_ref[...]
        r = lax.rsqrt(jnp.mean(xv * xv, axis=-1, keepdims=True) + EPS)
        u_ref[...] = (xv * r * g_ref[...]).astype(BF)

    return _pcall(body, name=name, grid=(t // tm,), in_specs=[_row_spec(tm, D), _vec_spec(D)],
                  out_specs=_row_spec(tm, D), out_shape=jax.ShapeDtypeStruct((t, D), BF),
                  compiler_params=_cp(("parallel",)))(x, g)


def _rms_bwd(du, x, g, resid, *, tm, name):
    t = x.shape[0]
    tm = min(tm, t)

    def body(du_ref, x_ref, g_ref, r_ref, dx_ref, dxb_ref, dg_ref):
        xv = x_ref[...]
        r = lax.rsqrt(jnp.mean(xv * xv, axis=-1, keepdims=True) + EPS)
        xh = xv * r
        duv = du_ref[...]
        dxh = duv * g_ref[...]
        dx = r_ref[...] + r * (dxh - xh * jnp.mean(dxh * xh, axis=-1, keepdims=True))
        dx_ref[...] = dx
        dxb_ref[...] = dx.astype(BF)
        _accum(dg_ref, jnp.sum(duv * xh, axis=0, keepdims=True), pl.program_id(0) == 0)

    return _pcall(body, name=name, grid=(t // tm,),
                  in_specs=[_row_spec(tm, D), _row_spec(tm, D), _vec_spec(D), _row_spec(tm, D)],
                  out_specs=[_row_spec(tm, D), _row_spec(tm, D), _vec_spec(D)],
                  out_shape=[jax.ShapeDtypeStruct((t, D), F32), jax.ShapeDtypeStruct((t, D), BF),
                             jax.ShapeDtypeStruct((1, D), F32)],
                  compiler_params=_cp(("arbitrary",)))(du, x, g, resid)


def _loss_head(h2, tgt, g, *, tm):
    t = h2.shape[0]
    tm = min(tm, t)

    def body(h_ref, t_ref, g_ref, dh_ref, dhb_ref, dg_ref, loss_ref):
        hv = h_ref[...]
        gv = g_ref[...]
        r = lax.rsqrt(jnp.mean(hv * hv, axis=-1, keepdims=True) + EPS)
        xh = hv * r
        err = xh * gv - t_ref[...]
        lp = jnp.sum(jnp.sum(err * err, axis=1, keepdims=True), axis=0, keepdims=True) * (0.5 / D)
        dy = err * (1.0 / D)
        dxh = dy * gv
        dh = r * (dxh - xh * jnp.mean(dxh * xh, axis=-1, keepdims=True))
        dh_ref[...] = dh
        dhb_ref[...] = dh.astype(BF)
        first = pl.program_id(0) == 0
        _accum(dg_ref, jnp.sum(dy * xh, axis=0, keepdims=True), first)
        _accum(loss_ref, jnp.broadcast_to(lp, (1, 128)), first)

    return _pcall(body, name="loss_head", grid=(t // tm,),
                  in_specs=[_row_spec(tm, D), _row_spec(tm, D), _vec_spec(D)],
                  out_specs=[_row_spec(tm, D), _row_spec(tm, D), _vec_spec(D), _vec_spec(128)],
                  out_shape=[jax.ShapeDtypeStruct((t, D), F32), jax.ShapeDtypeStruct((t, D), BF),
                             jax.ShapeDtypeStruct((1, D), F32), jax.ShapeDtypeStruct((1, 128), F32)],
                  compiler_params=_cp(("arbitrary",)))(h2, tgt, g)


def _attn_kv_tiles(kprev, kcur):
    kv = jnp.concatenate([kprev, kcur], axis=0).astype(F32)
    lo = lax.broadcasted_iota(jnp.int32, (2 * BLK, 128), 1) < HEAD
    tiles = []
    for part in (kv[:, 0:128], kv[:, 128:256]):
        rolled = pltpu.roll(part, HEAD, 1)
        z = jnp.zeros_like(part)
        tiles.append(((jnp.where(lo, part, z).astype(BF), jnp.where(lo, z, rolled).astype(BF)),
                      (jnp.where(lo, rolled, z).astype(BF), jnp.where(lo, z, part).astype(BF))))
    k_t, v_t = tiles
    return [(jnp.concatenate(k_t[h], axis=0), jnp.concatenate(v_t[h], axis=0)) for h in range(2)]


def _attn_mask(i):
    qi = lax.broadcasted_iota(jnp.int32, (BLK, 2 * BLK), 0)
    kj = lax.broadcasted_iota(jnp.int32, (BLK, 2 * BLK), 1)
    first_key = jnp.where(i == 0, BLK, 0)
    in_prev = jnp.logical_and(jnp.logical_and(kj < BLK, kj > qi), kj >= first_key)
    in_cur = jnp.logical_and(kj >= BLK, kj - BLK <= qi)
    return jnp.logical_or(in_prev, in_cur)


def _attn_probs(s, sink, valid):
    s = jnp.where(valid, s * SCALE, NEG)
    mx = jnp.maximum(jnp.max(s, axis=-1, keepdims=True), sink)
    e = jnp.exp(s - mx)
    es = jnp.exp(sink - mx)
    inv = 1.0 / (jnp.sum(e, axis=-1, keepdims=True) + es)
    return e * inv, es * inv


_KEYS = 2 * BLK


def _pair(ref, j):
    return ref[:, j * 128:(j + 1) * 128]


def _attn_fwd(q, kv, sinks, *, t, comm=None):
    nb = t // BLK

    def body(sink_ref, q_ref, kp_ref, kc_ref, o_ref):
        valid = _attn_mask(pl.program_id(0))
        tiles = _attn_kv_tiles(kp_ref[...], kc_ref[...])
        s = [lax.dot_general(_pair(q_ref, j), tiles[j // 4][0], _NT, preferred_element_type=F32)
             for j in range(N_PAIR)]
        p = []
        for j in range(N_PAIR):
            pe, _ = _attn_probs(s[j][:, 0:_KEYS], sink_ref[0, 2 * j], valid)
            po, _ = _attn_probs(s[j][:, _KEYS:2 * _KEYS], sink_ref[0, 2 * j + 1], valid)
            p.append(jnp.concatenate([pe.astype(BF), po.astype(BF)], axis=1))
        for j in range(N_PAIR):
            o_ref[:, j * 128:(j + 1) * 128] = jnp.dot(p[j], tiles[j // 4][1],
                                                      preferred_element_type=F32).astype(BF)

    return _hosted_call(
        body, comm, (sinks, q, kv, kv), name="attn_fwd", grid=(nb,),
        in_specs=[pl.BlockSpec(memory_space=pltpu.SMEM),
                  pl.BlockSpec((BLK, D), lambda i: (i, 0)),
                  pl.BlockSpec((BLK, 256), lambda i: (jnp.maximum(i - 1, 0), 0)),
                  pl.BlockSpec((BLK, 256), lambda i: (i, 0))],
        out_specs=[pl.BlockSpec((BLK, D), lambda i: (i, 0))],
        out_shape=[jax.ShapeDtypeStruct((t, D), BF)],
        scratch_shapes=[], sem=("arbitrary",), nsteps=nb, step_fn=lambda: pl.program_id(0))


def _attn_bwd(q, kv, sinks, do, *, t, comm=None):
    nb = t // BLK
    last = nb - 1

    def body(sink_ref, q_ref, kp_ref, kc_ref, do_ref, dq_ref, dkv_ref, ds_ref, carry_ref):
        i = pl.program_id(0)

        @pl.when(i == 0)
        def _():
            ds_ref[...] = jnp.zeros_like(ds_ref)
            carry_ref[...] = jnp.zeros_like(carry_ref)

        @pl.when(i < nb)
        def _():
            valid = _attn_mask(i)
            tiles = _attn_kv_tiles(kp_ref[...], kc_ref[...])
            lane1 = lax.broadcasted_iota(jnp.int32, (1, 128), 1)
            dsink = jnp.zeros((1, 128), F32)
            s = [lax.dot_general(_pair(q_ref, j), tiles[j // 4][0], _NT, preferred_element_type=F32)
                 for j in range(N_PAIR)]
            dp = [lax.dot_general(_pair(do_ref, j), tiles[j // 4][1], _NT, preferred_element_type=F32)
                  for j in range(N_PAIR)]
            p_all, ds_all = [], []
            for j in range(N_PAIR):
                halves = []
                for par in range(2):
                    cols = slice(par * _KEYS, (par + 1) * _KEYS)
                    p, ps = _attn_probs(s[j][:, cols], sink_ref[0, 2 * j + par], valid)
                    dpj = dp[j][:, cols]
                    dd = jnp.sum(p * dpj, axis=-1, keepdims=True)
                    dsink = dsink + jnp.where(lane1 == 2 * j + par,
                                              -jnp.sum(ps * dd, axis=0, keepdims=True), 0.0)
                    halves.append((p.astype(BF), (p * (dpj - dd)).astype(BF)))
                p_all.append(jnp.concatenate([halves[0][0], halves[1][0]], axis=1))
                ds_all.append(jnp.concatenate([halves[0][1], halves[1][1]], axis=1))
            for j in range(N_PAIR):
                dq_ref[:, j * 128:(j + 1) * 128] = (
                    jnp.dot(ds_all[j], tiles[j // 4][0], preferred_element_type=F32) * SCALE).astype(BF)
            ds_ref[...] += dsink
            gk, gv = [], []
            for h in range(2):
                grp = range(4 * h, 4 * h + 4)
                q_rows = jnp.concatenate([_pair(q_ref, j) for j in grp], axis=0)
                do_rows = jnp.concatenate([_pair(do_ref, j) for j in grp], axis=0)
                g_k = lax.dot_general(jnp.concatenate([ds_all[j] for j in grp], axis=0), q_rows, _TN,
                                      preferred_element_type=F32)
                g_v = lax.dot_general(jnp.concatenate([p_all[j] for j in grp], axis=0), do_rows, _TN,
                                      preferred_element_type=F32)
                gk.append((g_k[0:_KEYS], g_k[_KEYS:2 * _KEYS]))
                gv.append((g_v[0:_KEYS], g_v[_KEYS:2 * _KEYS]))
            lo = lax.broadcasted_iota(jnp.int32, (2 * BLK, 128), 1) < HEAD
            zero = jnp.zeros((2 * BLK, 128), F32)

            def unpad(g):
                return (jnp.where(lo, g[0][0] + pltpu.roll(g[0][1], HEAD, 1), zero)
                        + jnp.where(lo, zero, pltpu.roll(g[1][0], HEAD, 1) + g[1][1]))

            dk = unpad(gk) * SCALE
            dv = unpad(gv)
            dkv_ref[:, 0:128] = (carry_ref[:, 0:128] + dk[0:BLK]).astype(BF)
            dkv_ref[:, 128:256] = (carry_ref[:, 128:256] + dv[0:BLK]).astype(BF)
            carry_ref[:, 0:128] = dk[BLK:2 * BLK]
            carry_ref[:, 128:256] = dv[BLK:2 * BLK]

        @pl.when(i == nb)
        def _():
            dkv_ref[...] = carry_ref[...].astype(BF)

    return _hosted_call(
        body, comm, (sinks, q, kv, kv, do), name="attn_bwd", grid=(nb + 1,),
        in_specs=[pl.BlockSpec(memory_space=pltpu.SMEM),
                  pl.BlockSpec((BLK, D), lambda i: (jnp.minimum(i, last), 0)),
                  pl.BlockSpec((BLK, 256), lambda i: (jnp.clip(i - 1, 0, last), 0)),
                  pl.BlockSpec((BLK, 256), lambda i: (jnp.minimum(i, last), 0)),
                  pl.BlockSpec((BLK, D), lambda i: (jnp.minimum(i, last), 0))],
        out_specs=[pl.BlockSpec((BLK, D), lambda i: (jnp.minimum(i, last), 0)),
                   pl.BlockSpec((BLK, 256), lambda i: (jnp.maximum(i - 1, 0), 0)),
                   pl.BlockSpec((1, 128), lambda i: (0, 0))],
        out_shape=[jax.ShapeDtypeStruct((t, D), BF), jax.ShapeDtypeStruct((t, 256), BF),
                   jax.ShapeDtypeStruct((1, 128), F32)],
        scratch_shapes=[pltpu.VMEM((BLK, 256), F32)], sem=("arbitrary",), nsteps=nb + 1,
        step_fn=lambda: pl.program_id(0))


def _split3(v):
    h = v.astype(BF)
    r = v - h.astype(F32)
    m = r.astype(BF)
    lo = (r - m.astype(F32)).astype(BF)
    return jnp.concatenate([h, m, lo], axis=1)


def _apply01(mat, v):
    n = v.shape[1]
    r = jnp.dot(mat, _split3(v), preferred_element_type=F32)
    return r[:, 0:n] + r[:, n:2 * n] + r[:, 2 * n:3 * n]


def _hgrn_gates(hq, hf, lb):
    sq = _sig(hq)
    sg = _sig(hf)
    f = lb + (1.0 - lb) * sg
    return hq * sq, (1.0 - lb) * (1.0 - sg), jnp.log(f), sq, sg, f


def _tri(upper):
    r = lax.broadcasted_iota(jnp.int32, (CH, CH), 0)
    c = lax.broadcasted_iota(jnp.int32, (CH, CH), 1)
    return (c >= r) if upper else (c <= r)


def _lb_from_logits(lg_ref):
    return 1.0 / (1.0 + jnp.exp(lg_ref[1:2, :] - lg_ref[0:1, :]))


def _hgrn_fwd(h4, logits, norm_g, *, t):
    nc = t // CH
    nt_dims = (((1,), (1,)), ((), ()))
    tn_dims = (((0,), (0,)), ((), ()))

    def body(h_ref, lg_ref, ng_ref, y_ref, o_ref, st_ref, s_scr, b_scr, qa_s, ka_s, qb_s, kb_s, v_s):
        @pl.when(pl.program_id(0) == 0)
        def _():
            s_scr[...] = jnp.zeros_like(s_scr)

        heads = [slice(h * HG_K, (h + 1) * HG_K) for h in range(HG_HEADS)]
        causal = _tri(False)
        q, k, g, _, _, _ = _hgrn_gates(h_ref[:, 0:D], h_ref[:, D:2 * D], _lb_from_logits(lg_ref))
        b_scr[...] = _apply01(jnp.where(causal, 1.0, 0.0).astype(BF), g)
        b = b_scr[...]
        b_mid = b_scr[CH // 2 - 1:CH // 2, :]
        b_last = b_scr[CH - 1:CH, :]
        qa_s[...] = (q * jnp.exp(b - b_mid)).astype(BF)
        ka_s[...] = (k * jnp.exp(b_mid - b)).astype(BF)
        qb_s[...] = (q * jnp.exp(b)).astype(BF)
        kb_s[...] = (k * jnp.exp(b_last - b)).astype(BF)
        v_s[...] = h_ref[:, 2 * D:3 * D].astype(BF)
        dec = jnp.exp(b_last)
        st_ref[0] = s_scr[...]
        a = [jnp.where(causal, lax.dot_general(qa_s[:, sl], ka_s[:, sl], nt_dims, preferred_element_type=F32),
                       0.0).astype(BF) for sl in heads]
        for h, sl in enumerate(heads):
            o_ref[:, sl] = (jnp.dot(a[h], v_s[:, sl], preferred_element_type=F32)
                            + lax.dot_general(qb_s[:, sl], s_scr[h].astype(BF), nt_dims,
                                              preferred_element_type=F32))
        for h, sl in enumerate(heads):
            s_scr[h] = dec[:, sl] * s_scr[h] + lax.dot_general(v_s[:, sl], kb_s[:, sl], tn_dims,
                                                               preferred_element_type=F32)
        for h, sl in enumerate(heads):
            o = o_ref[:, sl]
            on = o * lax.rsqrt(jnp.mean(o * o, axis=-1, keepdims=True) + EPS)
            y_ref[:, sl] = (on * ng_ref[:, sl] * _sig(h_ref[:, 3 * D + h * HG_K:3 * D + (h + 1) * HG_K])).astype(BF)

    half = lambda: pltpu.VMEM((CH, D), BF)
    return _pcall(body, name="hgrn_fwd", grid=(nc,),
                  in_specs=[pl.BlockSpec((CH, 4 * D), lambda n: (n, 0)),
                            pl.BlockSpec((2, D), lambda n: (0, 0)),
                            pl.BlockSpec((1, D), lambda n: (0, 0))],
                  out_specs=[pl.BlockSpec((CH, D), lambda n: (n, 0)),
                             pl.BlockSpec((CH, D), lambda n: (n, 0)),
                             pl.BlockSpec((1, HG_HEADS, HG_K, HG_K), lambda n: (n, 0, 0, 0))],
                  out_shape=[jax.ShapeDtypeStruct((t, D), BF), jax.ShapeDtypeStruct((t, D), F32),
                             jax.ShapeDtypeStruct((nc, HG_HEADS, HG_K, HG_K), F32)],
                  scratch_shapes=[pltpu.VMEM((HG_HEADS, HG_K, HG_K), F32), pltpu.VMEM((CH, D), F32),
                                  half(), half(), half(), half(), half()],
                  compiler_params=_cp(("arbitrary",)))(h4, logits, norm_g)


def _hgrn_bwd(h4, logits, norm_g, o_pre, states, dy, *, t, comm=None):
    nc = t // CH
    nt_dims = (((1,), (1,)), ((), ()))
    tn_dims = (((0,), (0,)), ((), ()))

    def body(h_ref, lg_ref, ng_ref, o_ref, st_ref, dy_ref, dh_ref, dlg_ref, dng_ref, ds_scr, dlb_scr,
             b_scr, tail_s, e_qa, e_ka, e_qb, e_kb, q_s, k_s, dqa_s, dka_s, dqb_s, dkb_s,
             qa_s, ka_s, qb_s, kb_s, v_s, do_s):
        n = pl.program_id(0)

        @pl.when(n == 0)
        def _():
            ds_scr[...] = jnp.zeros_like(ds_scr)
            dlb_scr[...] = jnp.zeros_like(dlb_scr)
            dng_ref[...] = jnp.zeros_like(dng_ref)

        heads = [slice(h * HG_K, (h + 1) * HG_K) for h in range(HG_HEADS)]
        lb = _lb_from_logits(lg_ref)
        causal = _tri(False)
        q, k, g, _, _, _ = _hgrn_gates(h_ref[:, 0:D], h_ref[:, D:2 * D], lb)
        b_scr[...] = _apply01(jnp.where(causal, 1.0, 0.0).astype(BF), g)
        b = b_scr[...]
        b_mid = b_scr[CH // 2 - 1:CH // 2, :]
        b_last = b_scr[CH - 1:CH, :]
        q_s[...] = q
        k_s[...] = k
        for e_ref, s_ref, base, expo in ((e_qa, qa_s, q, b - b_mid), (e_ka, ka_s, k, b_mid - b),
                                         (e_qb, qb_s, q, b), (e_kb, kb_s, k, b_last - b)):
            e = jnp.exp(expo)
            e_ref[...] = e
            s_ref[...] = (base * e).astype(BF)
        v_s[...] = h_ref[:, 2 * D:3 * D].astype(BF)
        dec = jnp.exp(b_last)
        for h, sl in enumerate(heads):
            gcol = slice(3 * D + h * HG_K, 3 * D + (h + 1) * HG_K)
            ngh = ng_ref[:, sl]
            sgate = _sig(h_ref[:, gcol])
            o = o_ref[:, sl]
            r = lax.rsqrt(jnp.mean(o * o, axis=-1, keepdims=True) + EPS)
            on = o * r
            dyh = dy_ref[:, sl]
            dh_ref[:, gcol] = (dyh * on * ngh * sgate * (1.0 - sgate)).astype(BF)
            dng_ref[:, sl] += jnp.sum(dyh * on * sgate, axis=0, keepdims=True)
            don = dyh * ngh * sgate
            do_s[:, sl] = (r * (don - on * jnp.mean(don * on, axis=-1, keepdims=True))).astype(BF)
        a = [jnp.where(causal, lax.dot_general(qa_s[:, sl], ka_s[:, sl], nt_dims, preferred_element_type=F32),
                       0.0).astype(BF) for sl in heads]
        da = [jnp.where(causal, lax.dot_general(do_s[:, sl], v_s[:, sl], nt_dims, preferred_element_type=F32),
                        0.0).astype(BF) for sl in heads]
        for h, sl in enumerate(heads):
            dh_ref[:, 2 * D + h * HG_K:2 * D + (h + 1) * HG_K] = (
                lax.dot_general(a[h], do_s[:, sl], tn_dims, preferred_element_type=F32)
                + lax.dot_general(kb_s[:, sl], ds_scr[h].astype(BF), nt_dims, preferred_element_type=F32)
            ).astype(BF)
        for h, sl in enumerate(heads):
            dqa_s[:, sl] = jnp.dot(da[h], ka_s[:, sl], preferred_element_type=F32)
        for h, sl in enumerate(heads):
            dka_s[:, sl] = lax.dot_general(da[h], qa_s[:, sl], tn_dims, preferred_element_type=F32)
        for h, sl in enumerate(heads):
            dqb_s[:, sl] = jnp.dot(do_s[:, sl], st_ref[0, h].astype(BF), preferred_element_type=F32)
        for h, sl in enumerate(heads):
            dkb_s[:, sl] = jnp.dot(v_s[:, sl], ds_scr[h].astype(BF), preferred_element_type=F32)
        for h, sl in enumerate(heads):
            tail_s[:, sl] = jnp.sum(dec[:, sl] * st_ref[0, h] * ds_scr[h], axis=0, keepdims=True)
        for h, sl in enumerate(heads):
            ds_scr[h] = (lax.dot_general(do_s[:, sl], qb_s[:, sl], tn_dims, preferred_element_type=F32)
                         + dec[:, sl] * ds_scr[h])
        qv, kv = q_s[...], k_s[...]
        dqa, dka, dqb, dkb = dqa_s[...], dka_s[...], dqb_s[...], dkb_s[...]
        eqa, eka, eqb, ekb = e_qa[...], e_ka[...], e_qb[...], e_kb[...]
        dkb_kb = dkb * (kv * ekb)
        db_last = jnp.sum(dkb_kb, axis=0, keepdims=True) + tail_s[...]
        last_row = lax.broadcasted_iota(jnp.int32, (CH, D), 0) == CH - 1
        db = (dqa * (qv * eqa) - dka * (kv * eka) + dqb * (qv * eqb) - dkb_kb
              + jnp.where(last_row, db_last, 0.0))
        dg = _apply01(jnp.where(_tri(True), 1.0, 0.0).astype(BF), db)
        dq = dqa * eqa + dqb * eqb
        dk = dka * eka + dkb * ekb
        hq = h_ref[:, 0:D]
        _, _, _, sq, sg, f = _hgrn_gates(hq, h_ref[:, D:2 * D], lb)
        dh_ref[:, 0:D] = (dq * sq * (1.0 + hq * (1.0 - sq))).astype(BF)
        dfk = dg / f - dk
        dh_ref[:, D:2 * D] = ((1.0 - lb) * dfk * sg * (1.0 - sg)).astype(BF)
        dlb_scr[...] += jnp.sum((1.0 - sg) * dfk, axis=0, keepdims=True)

        @pl.when(n == nc - 1)
        def _():
            dl0 = dlb_scr[...] * lb * (1.0 - lb)
            dlg_ref[0:1, :] = dl0
            dlg_ref[1:2, :] = -dl0

    rev = lambda n: (nc - 1 - n, 0)
    return _hosted_call(
        body, comm, (h4, logits, norm_g, o_pre, states, dy), name="hgrn_bwd", grid=(nc,),
        in_specs=[pl.BlockSpec((CH, 4 * D), rev),
                  pl.BlockSpec((2, D), lambda n: (0, 0)),
                  pl.BlockSpec((1, D), lambda n: (0, 0)),
                  pl.BlockSpec((CH, D), rev),
                  pl.BlockSpec((1, HG_HEADS, HG_K, HG_K), lambda n: (nc - 1 - n, 0, 0, 0)),
                  pl.BlockSpec((CH, D), rev)],
        out_specs=[pl.BlockSpec((CH, 4 * D), rev),
                   pl.BlockSpec((2, D), lambda n: (0, 0)),
                   pl.BlockSpec((1, D), lambda n: (0, 0))],
        out_shape=[jax.ShapeDtypeStruct((t, 4 * D), BF), jax.ShapeDtypeStruct((2, D), F32),
                   jax.ShapeDtypeStruct((1, D), F32)],
        scratch_shapes=([pltpu.VMEM((HG_HEADS, HG_K, HG_K), F32), pltpu.VMEM((1, D), F32),
                         pltpu.VMEM((CH, D), F32), pltpu.VMEM((1, D), F32)]
                        + [pltpu.VMEM((CH, D), F32)] * 10 + [pltpu.VMEM((CH, D), BF)] * 6),
        sem=("arbitrary",), nsteps=nc, step_fn=lambda: pl.program_id(0))


def _place():
    x, y, c = lax.axis_index("x"), lax.axis_index("y"), lax.axis_index("c")
    return x, y, c, [(1 - x, y), (x, 1 - y), (1 - x, 1 - y)]


def _gather_comm(shards, mid):
    n = len(shards)
    r = [s.shape[0] for s in shards]

    def tools(ins, outs, sems):
        send_sems, recv_sems, local_sems = sems
        x, y, c, chips = _place()
        me, sib = (x, y, c), (x, y, 1 - c)

        def rows(w, dev):
            return outs[w].at[pl.ds((4 * dev[0] + 2 * dev[1] + dev[2]) * r[w], r[w]), :]

        def copy(kind, w, block, to, src=None):
            return pltpu.make_async_remote_copy(
                src_ref=rows(w, block) if src is None else src, dst_ref=rows(w, block),
                send_sem=send_sems.at[kind], recv_sem=recv_sems.at[kind], device_id=to, device_id_type=MESH)

        def all_of(kind):
            whole = outs[0].at[pl.ds(0, sum(r)), :]
            return pltpu.make_async_remote_copy(
                src_ref=whole, dst_ref=whole, send_sem=send_sems.at[kind], recv_sem=recv_sems.at[kind],
                device_id=me, device_id_type=MESH)

        mine = [pltpu.make_async_copy(ins[w], rows(w, me), local_sems.at[w]) for w in range(n)]
        return c, chips, me, sib, copy, all_of, mine

    def start(ins, outs, sems):
        c, chips, me, sib, copy, _, mine = tools(ins, outs, sems)
        for cp in mine:
            cp.start()
        for w in range(n):
            copy(0, w, me, sib, src=ins[w]).start()
            for j, chip in enumerate(chips):
                copy(1 + j, w, me, (*chip, c), src=ins[w]).start()

    def pass_on(ins, outs, sems):
        c, chips, _, sib, copy, all_of, _ = tools(ins, outs, sems)
        for j, chip in enumerate(chips):
            all_of(1 + j).wait_recv()
            for w in range(n):
                copy(4 + j, w, (*chip, c), sib).start()

    def finish(ins, outs, sems):
        _, _, _, _, _, all_of, mine = tools(ins, outs, sems)
        all_of(0).wait_recv()
        for j in range(3):
            all_of(4 + j).wait_recv()
        for kind in range(7):
            all_of(kind).wait_send()
        for cp in mine:
            cp.wait()

    return _Comm(shards, [jax.ShapeDtypeStruct((N_DEV * rw, D), BF) for rw in r],
                 [pltpu.SemaphoreType.DMA((7,)), pltpu.SemaphoreType.DMA((7,)), pltpu.SemaphoreType.DMA((n,))],
                 [(0.0, start), (mid, pass_on), (1.0, finish)])


def _pair_comm(grads):
    n = len(grads)
    r = [g.shape[0] // N_DEV for g in grads]

    def start(ins, outs, sems):
        send_sems, recv_sems = sems
        x, y, c, _ = _place()
        for w in range(n):
            for a in range(N_CHIP):
                pltpu.make_async_remote_copy(
                    src_ref=ins[w].at[pl.ds((2 * a + 1 - c) * r[w], r[w]), :], dst_ref=outs[w].at[a],
                    send_sem=send_sems.at[w], recv_sem=recv_sems.at[w],
                    device_id=(x, y, 1 - c), device_id_type=MESH).start()

    def finish(ins, outs, sems):
        send_sems, recv_sems = sems
        x, y, c, _ = _place()
        for w in range(n):
            pltpu.make_async_remote_copy(
                src_ref=outs[w], dst_ref=outs[w], send_sem=send_sems.at[w], recv_sem=recv_sems.at[w],
                device_id=(x, y, c), device_id_type=MESH).wait()

    return _Comm(grads, [jax.ShapeDtypeStruct((N_CHIP, rw, D), BF) for rw in r],
                 [pltpu.SemaphoreType.DMA((n,)), pltpu.SemaphoreType.DMA((n,))],
                 [(0.0, start), (1.0, finish)])


def _pair_add(grad, got, core, *, name):
    r = got.shape[1]

    def body(c_ref, g_ref, got_ref, o_ref):
        o_ref[0] = (g_ref[...].astype(F32) + got_ref[0].astype(F32)).astype(BF)

    grid_spec = pltpu.PrefetchScalarGridSpec(
        num_scalar_prefetch=1, grid=(N_CHIP,),
        in_specs=[pl.BlockSpec((r, D), lambda a, c_ref: (2 * a + c_ref[0], 0)),
                  pl.BlockSpec((1, r, D), lambda a, c_ref: (a, 0, 0))],
        out_specs=pl.BlockSpec((1, r, D), lambda a, c_ref: (a, 0, 0)))
    return _pcall(body, name=name, grid_spec=grid_spec,
                  out_shape=jax.ShapeDtypeStruct((N_CHIP, r, D), BF),
                  compiler_params=_cp(("parallel",)))(core, grad, got)


def _chip_comm(pair_sums):
    n = len(pair_sums)
    r = [p.shape[1] for p in pair_sums]
    off = [sum(r[:w]) for w in range(n)]

    def tools(ins, outs, sems):
        send_sems, recv_sems, local_sems = sems
        x, y, c, chips = _place()
        my_chip = 2 * x + y

        def slot(w):
            return outs[0].at[my_chip, pl.ds(off[w], r[w]), :]

        own = [pltpu.make_async_copy(ins[w].at[my_chip], slot(w), local_sems.at[w]) for w in range(n)]
        return x, y, c, chips, my_chip, slot, own, send_sems, recv_sems

    def start(ins, outs, sems):
        x, y, c, chips, my_chip, slot, own, send_sems, recv_sems = tools(ins, outs, sems)
        for cp in own:
            cp.start()
        for j, chip in enumerate(chips):
            for w in range(n):
                pltpu.make_async_remote_copy(
                    src_ref=ins[w].at[2 * chip[0] + chip[1]], dst_ref=slot(w), send_sem=send_sems.at[j],
                    recv_sem=recv_sems.at[j], device_id=(*chip, c), device_id_type=MESH).start()

    def finish(ins, outs, sems):
        x, y, c, chips, my_chip, slot, own, send_sems, recv_sems = tools(ins, outs, sems)
        whole = outs[0].at[my_chip]
        for j in range(3):
            pltpu.make_async_remote_copy(
                src_ref=whole, dst_ref=whole, send_sem=send_sems.at[j], recv_sem=recv_sems.at[j],
                device_id=(x, y, c), device_id_type=MESH).wait()
        for cp in own:
            cp.wait()

    return _Comm(pair_sums, [jax.ShapeDtypeStruct((N_CHIP, sum(r), D), BF)],
                 [pltpu.SemaphoreType.DMA((3,)), pltpu.SemaphoreType.DMA((3,)), pltpu.SemaphoreType.DMA((n,))],
                 [(0.0, start), (1.0, finish)])


def _sum_chips(parts, *, tr, name):
    rows = parts.shape[1]

    def body(p_ref, o_ref):
        acc = p_ref[0].astype(F32)
        for a in range(1, N_CHIP):
            acc = acc + p_ref[a].astype(F32)
        o_ref[...] = acc

    return _pcall(body, name=name, grid=(rows // tr,),
                  in_specs=[pl.BlockSpec((N_CHIP, tr, D), lambda i: (0, i, 0))],
                  out_specs=_row_spec(tr, D), out_shape=jax.ShapeDtypeStruct((rows, D), F32),
                  compiler_params=_cp(("parallel",)))(parts)


def _adam_math(w, g, m, v):
    m = ADAM_B1 * m + (1.0 - ADAM_B1) * g
    v = ADAM_B2 * v + (1.0 - ADAM_B2) * (g * g)
    m_hat = m / (1.0 - ADAM_B1 ** ADAM_STEP)
    v_hat = v / (1.0 - ADAM_B2 ** ADAM_STEP)
    delta = -ADAM_LR * (m_hat / (jnp.sqrt(v_hat) + ADAM_EPS) + ADAM_WD * w)
    return delta, m, v


def _small_allreduce_adam(gpart, w, m, v):
    def body(g_ref, w_ref, m_ref, v_ref, gs_ref, d_ref, mo_ref, vo_ref, gath, send_sems, recv_sems):
        x, y, c, _ = _place()
        me = 4 * x + 2 * y + c
        gath[me] = g_ref[...]
        cps = []
        for d in range(1, N_DEV):
            peer = (x ^ (d >> 2), y ^ ((d >> 1) & 1), c ^ (d & 1))
            cps.append(pltpu.make_async_remote_copy(
                src_ref=g_ref, dst_ref=gath.at[me], send_sem=send_sems.at[d - 1],
                recv_sem=recv_sems.at[d - 1], device_id=peer, device_id_type=MESH))
        for cp in cps:
            cp.start()
        for cp in cps:
            cp.wait()
        g = gath[0]
        for k in range(1, N_DEV):
            g = g + gath[k]
        gs_ref[...] = g
        d_ref[...], mo_ref[...], vo_ref[...] = _adam_math(w_ref[...], g, m_ref[...], v_ref[...])

    shape = jax.ShapeDtypeStruct((SMALL_ROWS, D), F32)
    vm = pl.BlockSpec(memory_space=pltpu.VMEM)
    return _pcall(body, name="small_allreduce_adam", in_specs=[vm] * 4, out_specs=[vm] * 4,
                  out_shape=[shape] * 4,
                  scratch_shapes=[pltpu.VMEM((N_DEV, SMALL_ROWS, D), F32),
                                  pltpu.SemaphoreType.DMA((N_DEV - 1,)), pltpu.SemaphoreType.DMA((N_DEV - 1,))],
                  compiler_params=pltpu.CompilerParams(has_side_effects=True))(gpart, w, m, v)


def _adam(w, g, m, v, *, name):
    rows, cols = w.shape
    tr = rows if rows <= 512 else 256

    def body(w_ref, g_ref, m_ref, v_ref, d_ref, mo_ref, vo_ref):
        d_ref[...], mo_ref[...], vo_ref[...] = _adam_math(w_ref[...], g_ref[...], m_ref[...], v_ref[...])

    spec = pl.BlockSpec((tr, cols), lambda i: (i, 0))
    return _pcall(body, name=name, grid=(rows // tr,), in_specs=[spec] * 4, out_specs=[spec] * 3,
                  out_shape=[jax.ShapeDtypeStruct((rows, cols), F32)] * 3,
                  compiler_params=_cp(("parallel",)))(w, g, m, v)


def _step(x, tgt, shards, norm_mix_g, b_in, sinks, logits, hgrn_norm_g, norm_ffn_g, norm_final_g):
    t = x.shape[0]
    big = dict(tm=1024, tn=1024, tk=4096)
    core = lax.axis_index("c").astype(jnp.int32).reshape(1)

    (win_t,) = _run_comm(_gather_comm(shards[0:1], 0.0), name="gather_w_in")
    u1 = _rms_fwd(x, norm_mix_g, tm=512, name="rms_mix")
    (q, kv, h4, gates), (wg_t, wu_t, wd) = _inproj_fwd(u1, win_t, b_in, t=t,
                                                       comm=_gather_comm(shards[1:4], 0.8))
    (y_attn,), (wba, wbh, wout) = _attn_fwd(q, kv, sinks, t=t, comm=_gather_comm(shards[4:7], 0.7))
    y_hgrn, o_pre, states = _hgrn_fwd(h4, logits, hgrn_norm_g, t=t)
    col = lambda j: j
    first, second = (lambda j: 0), (lambda j: 1)
    gate_tiles = [(gates, D, first), (gates, D, second)]

    def merge(prods, ex):
        (ya_, yb_), (ga, gb) = prods, ex
        return ya_, yb_, _sig(ga) * ya_ + _sig(gb) * yb_

    ya, yb, merged = _fmm([y_attn, y_hgrn], [(0, wba, False), (1, wbh, False)], gate_tiles, merge,
                          [(BF, D, D, first)] * 3, m=t, n=D, tm=512, tn=D, name="branch_merge")
    h1 = _mm(merged, wout, m=t, n=D, k=D, resid=x, name="out_proj", **big)
    u2 = _rms_fwd(h1, norm_ffn_g, tm=512, name="rms_ffn")

    def swiglu(prods, ex):
        g_, u_ = prods
        return g_, u_, g_ * _sig(g_) * u_

    gt, up, z = _fmm([u2], [(0, wg_t, True), (0, wu_t, True)], [], swiglu, [(BF, FFN, FFN // 2, col)] * 3,
                     m=t, n=FFN, tm=512, tn=FFN // 2, name="ffn_gate_up")
    h2 = _mm(z, wd, m=t, n=D, k=FFN, resid=h1, name="ffn_down", **big)
    dh2, dh2_b, d_norm_final, loss_row = _loss_head(h2, tgt, norm_final_g, tm=512)

    def swiglu_bwd(prods, ex):
        (dz,), (g_, u_) = prods, ex
        g_ = g_.astype(F32)
        s = _sig(g_)
        return dz * u_.astype(F32) * s * (1.0 + g_ * (1.0 - s)), dz * g_ * s

    ffn_tiles = [(gt, FFN // 2, col), (up, FFN // 2, col)]
    dgt, dup = _fmm([dh2_b], [(0, wd, True)], ffn_tiles, swiglu_bwd, [(BF, FFN, FFN // 2, col)] * 2,
                    m=t, n=FFN, tm=512, tn=FFN // 2, name="d_gate_up")
    d_wd = _mm(z, dh2_b, m=FFN, n=D, k=t, ta=True, tm=256, tn=D, tk=4096, out_dtype=BF, name="d_w_down")
    (du2,) = _fmm([dgt, dup], [(0, wg_t, False), (1, wu_t, False)], [], lambda prods, ex: (prods[0] + prods[1],),
                  [(F32, D, 512, col)], m=t, n=D, tm=512, tn=512, name="d_u2")
    d_wg = _mm(dgt, u2, m=FFN, n=D, k=t, ta=True, tm=256, tn=D, tk=4096, out_dtype=BF, name="d_w_gate")
    d_wu = _mm(dup, u2, m=FFN, n=D, k=t, ta=True, tm=256, tn=D, tk=4096, out_dtype=BF, name="d_w_up")
    dh1, dh1_b, d_norm_ffn = _rms_bwd(du2, h1, norm_ffn_g, dh2, tm=512, name="rms_ffn_bwd")
    d_wout = _mm(merged, dh1_b, m=D, n=D, k=t, ta=True, tm=256, tn=D, tk=4096, out_dtype=BF, name="d_w_out")

    def merge_bwd(prods, ex):
        (dm,), (ga, gb, ya_, yb_) = prods, ex
        sa, sb = _sig(ga), _sig(gb)
        dgate = jnp.concatenate([dm * ya_.astype(F32) * sa * (1.0 - sa),
                                 dm * yb_.astype(F32) * sb * (1.0 - sb)], axis=1)
        return dm * sa, dm * sb, dgate

    dya, dyb, dgates = _fmm([dh1_b], [(0, wout, True)], gate_tiles + [(ya, D, first), (yb, D, first)], merge_bwd,
                            [(BF, D, D, first), (BF, D, D, first), (BF, 2 * D, 2 * D, first)],
                            m=t, n=D, tm=512, tn=D, name="d_merge")
    dy_attn = _mm(dya, wba, m=t, n=D, k=D, tb=True, out_dtype=BF, name="d_y_attn", **big)
    dy_hgrn = _mm(dyb, wbh, m=t, n=D, k=D, tb=True, name="d_y_hgrn", **big)
    d_wba = _mm(y_attn, dya, m=D, n=D, k=t, ta=True, tm=256, tn=D, tk=4096, out_dtype=BF, name="d_w_ba")
    d_wbh = _mm(y_hgrn, dyb, m=D, n=D, k=t, ta=True, tm=256, tn=D, tk=4096, out_dtype=BF, name="d_w_bh")
    rest = (d_wg, d_wu, d_wd, d_wba, d_wbh, d_wout)
    (dq, dkv, d_sinks), got = _attn_bwd(q, kv, sinks, dy_attn, t=t, comm=_pair_comm(rest))
    pair = [_pair_add(g, r, core, name="pair_add_%d" % i) for i, (g, r) in enumerate(zip(rest, got))]
    (dh4, d_logits, d_hgrn_norm), (parts_rest,) = _hgrn_bwd(h4, logits, hgrn_norm_g, o_pre, states, dy_hgrn,
                                                             t=t, comm=_chip_comm(pair))
    g_rest = _sum_chips(parts_rest, tr=parts_rest.shape[1] // 2, name="sum_chips_rest")
    dps = (dq, dkv, dh4, dgates)
    d_win_t, d_b_in = _inproj_bwd_w(dps, u1, t=t)
    (du1,), got_in = _inproj_bwd_x(dps, win_t, t=t, part=0, comm=_pair_comm([d_win_t]))
    pair_in = _pair_add(d_win_t, got_in[0], core, name="pair_add_w_in")
    (du1,), (parts_in,) = _inproj_bwd_x(dps, win_t, t=t, part=1, into=du1, comm=_chip_comm([pair_in]))
    g_in = _sum_chips(parts_in, tr=parts_in.shape[1] // 2, name="sum_chips_w_in")
    grad_x, _, d_norm_mix = _rms_bwd(du1, x, norm_mix_g, dh1, tm=512, name="rms_mix_bwd")

    small_grads = (d_norm_mix, d_b_in, d_sinks, d_logits, d_hgrn_norm, d_norm_ffn, d_norm_final)
    return loss_row, grad_x, g_in, g_rest, small_grads


def _pack_small(norm_mix, b_in, sinks, logits, hgrn_norm, norm_ffn, norm_final, extra=None):
    pad = lambda a, n: jnp.pad(a.reshape(1, -1), ((0, 0), (0, n - a.size)))
    rows = [norm_mix.reshape(1, D), hgrn_norm.reshape(1, D), norm_ffn.reshape(1, D), norm_final.reshape(1, D),
            logits.reshape(2, D), pad(sinks.reshape(-1)[:16], D),
            jnp.zeros((1, D), F32) if extra is None else pad(extra, D),
            pad(b_in, 8 * D).reshape(8, D)]
    return jnp.concatenate(rows, axis=0).astype(F32)


def _unpack_small(p):
    return dict(norm_mix_g=p[0:1], hgrn_norm_g=p[1:2], norm_ffn_g=p[2:3], norm_final_g=p[3],
                hgrn_lb_logits=p[4:6], attn_sinks=p[6:7, 0:16], extra=p[7],
                b_in=p[8:16].reshape(1, 8 * D)[:, :IN_W])


def kernel(x, norm_mix_g, w_in, b_in, attn_sinks, hgrn_lb_logits, hgrn_norm_g, w_branch_attn, w_branch_hgrn, w_out, norm_ffn_g, w_ffn_gate, w_ffn_up, w_ffn_down, norm_final_g, loss_target, m_norm_mix_g, m_w_in, m_b_in, m_attn_sinks, m_hgrn_lb_logits, m_hgrn_norm_g, m_w_branch_attn, m_w_branch_hgrn, m_w_out, m_norm_ffn_g, m_w_ffn_gate, m_w_ffn_up, m_w_ffn_down, m_norm_final_g, v_norm_mix_g, v_w_in, v_b_in, v_attn_sinks, v_hgrn_lb_logits, v_hgrn_norm_g, v_w_branch_attn, v_w_branch_hgrn, v_w_out, v_norm_ffn_g, v_w_ffn_gate, v_w_ffn_up, v_w_ffn_down, v_norm_final_g):
    shards = [w_in[0].T.astype(BF), w_ffn_gate[0].T.astype(BF), w_ffn_up[0].T.astype(BF),
              w_ffn_down[0].astype(BF), w_branch_attn[0].astype(BF), w_branch_hgrn[0].astype(BF),
              w_out[0].astype(BF)]
    loss_row, grad_x, g_in, g_rest, small_grads = _step(
        x[0], loss_target[0], shards, norm_mix_g, b_in, attn_sinks, hgrn_lb_logits, hgrn_norm_g,
        norm_ffn_g, norm_final_g.reshape(1, D))

    d_norm_mix, d_b_in, d_sinks, d_logits, d_hgrn_norm, d_norm_ffn, d_norm_final = small_grads
    g_small = _pack_small(d_norm_mix, d_b_in, d_sinks[:, :16], d_logits, d_hgrn_norm, d_norm_ffn,
                          d_norm_final, extra=loss_row[0, 0:1])
    w_small = _pack_small(norm_mix_g, b_in, attn_sinks, hgrn_lb_logits, hgrn_norm_g, norm_ffn_g, norm_final_g)
    m_small = _pack_small(m_norm_mix_g, m_b_in, m_attn_sinks, m_hgrn_lb_logits, m_hgrn_norm_g, m_norm_ffn_g,
                          m_norm_final_g)
    v_small = _pack_small(v_norm_mix_g, v_b_in, v_attn_sinks, v_hgrn_lb_logits, v_hgrn_norm_g, v_norm_ffn_g,
                          v_norm_final_g)
    small = [_unpack_small(p) for p in _small_allreduce_adam(g_small, w_small, m_small, v_small)]
    loss = small[0]["extra"][0]

    names = ["w_in", "w_ffn_gate", "w_ffn_up", "w_ffn_down", "w_branch_attn", "w_branch_hgrn", "w_out"]
    w_full = dict(w_in=(w_in, m_w_in, v_w_in), w_ffn_gate=(w_ffn_gate, m_w_ffn_gate, v_w_ffn_gate),
                  w_ffn_up=(w_ffn_up, m_w_ffn_up, v_w_ffn_up), w_ffn_down=(w_ffn_down, m_w_ffn_down, v_w_ffn_down),
                  w_branch_attn=(w_branch_attn, m_w_branch_attn, v_w_branch_attn),
                  w_branch_hgrn=(w_branch_hgrn, m_w_branch_hgrn, v_w_branch_hgrn),
                  w_out=(w_out, m_w_out, v_w_out))
    big = {}
    for i, name in enumerate(names):
        g = g_in if i == 0 else g_rest[SLAB_OFF[i] - SLAB_R[0]:SLAB_OFF[i] - SLAB_R[0] + SLAB_R[i]]
        if i < 3:
            g = g.T
        wv, mv, vv = w_full[name]
        delta, new_m, new_v = _adam(wv[0], g, mv[0], vv[0], name="adam_" + name)
        big[name] = [a[None] for a in (g, delta, new_m, new_v)]

    order = ["norm_mix_g", "w_in", "b_in", "attn_sinks", "hgrn_lb_logits", "hgrn_norm_g", "w_branch_attn",
             "w_branch_hgrn", "w_out", "norm_ffn_g", "w_ffn_gate", "w_ffn_up", "w_ffn_down", "norm_final_g"]
    outs = [loss, grad_x[None]]
    for kind in range(4):
        for name in order:
            outs.append(big[name][kind] if name in big else small[kind][name])
    return tuple(outs)
```

```python
import math

import jax
import jax.numpy as jnp
from jax import lax
from jax.experimental import pallas as pl
from jax.experimental.pallas import tpu as pltpu

F32 = jnp.float32
BF = jnp.bfloat16
MESH = pl.DeviceIdType.MESH

D = 1024
HEAD = 64
N_PAIR = 8
BLK = 128
CH = 64
HG_HEADS = 8
HG_K = 128
FFN = 2816
IN_W = 7424
N_DEV = 8
N_CHIP = 4
EPS = 1e-6
NEG = -1e30
SCALE = 1.0 / math.sqrt(HEAD)
VMEM_LIMIT = 56 * 1024 * 1024
WT = 256

ADAM_LR, ADAM_B1, ADAM_B2, ADAM_EPS, ADAM_WD, ADAM_STEP = 0.001, 0.9, 0.999, 1e-08, 0.01, 10

SLAB_R = (IN_W // N_DEV, FFN // N_DEV, FFN // N_DEV, FFN // N_DEV, D // N_DEV, D // N_DEV, D // N_DEV)
SLAB_ROWS = sum(SLAB_R)
SLAB_OFF = tuple(sum(SLAB_R[:i]) for i in range(len(SLAB_R)))
N_W = len(SLAB_R)
GRP_OFF = (0, D // WT, (D + 256) // WT, (5 * D + 256) // WT)
GRP_N = (D // WT, 256 // WT, 4 * D // WT, 2 * D // WT)
SMALL_ROWS = 16


_NN = (((1,), (0,)), ((), ()))
_NT = (((1,), (1,)), ((), ()))
_TN = (((0,), (0,)), ((), ()))


def _pcall(body, **kw):
    return pl.pallas_call(body, **kw)


def _cp(sem=None, **kw):
    return pltpu.CompilerParams(dimension_semantics=sem, vmem_limit_bytes=VMEM_LIMIT, **kw)


def _sig(v):
    return 1.0 / (1.0 + jnp.exp(-v))


def _accum(ref, val, first):
    @pl.when(first)
    def _():
        ref[...] = val

    @pl.when(jnp.logical_not(first))
    def _():
        ref[...] += val


class _Comm:
    def __init__(self, ins, out_shapes, sem_shapes, phases):
        self.ins, self.out_shapes, self.sem_shapes, self.phases = list(ins), list(out_shapes), list(sem_shapes), phases


def _both(a, b):
    ni, no, ns = len(a.ins), len(a.out_shapes), len(a.sem_shapes)

    def of_a(fn):
        return lambda ins, outs, sems: fn(ins[:ni], outs[:no], sems[:ns])

    def of_b(fn):
        return lambda ins, outs, sems: fn(ins[ni:], outs[no:], sems[ns:])

    return _Comm(a.ins + b.ins, a.out_shapes + b.out_shapes, a.sem_shapes + b.sem_shapes,
                 [(f, of_a(fn)) for f, fn in a.phases] + [(f, of_b(fn)) for f, fn in b.phases])


def _host(body, comm, n_in, n_out, n_scr, nsteps, step_fn):
    if comm is None:
        return body
    ci, co = len(comm.ins), len(comm.out_shapes)

    def wrapped(*refs):
        p = 0
        ins, p = refs[p:p + n_in], p + n_in
        cins, p = refs[p:p + ci], p + ci
        outs, p = refs[p:p + n_out], p + n_out
        couts, p = refs[p:p + co], p + co
        scr, p = refs[p:p + n_scr], p + n_scr
        csems = refs[p:]
        step = step_fn()
        for frac, fn in comm.phases:
            if frac < 1.0:
                @pl.when(step == int(round(frac * (nsteps - 1))))
                def _(fn=fn):
                    fn(cins, couts, csems)
        body(*ins, *outs, *scr)
        for frac, fn in comm.phases:
            if frac >= 1.0:
                @pl.when(step == nsteps - 1)
                def _(fn=fn):
                    fn(cins, couts, csems)

    return wrapped


def _hosted_call(body, comm, args, *, name, grid, in_specs, out_specs, out_shape, scratch_shapes, sem,
                 nsteps, step_fn, aliases=None):
    n_in, n_out, n_scr = len(in_specs), len(out_specs), len(scratch_shapes)
    args = list(args)
    extra = {}
    if comm is not None:
        in_specs = list(in_specs) + [_hbm_spec()] * len(comm.ins)
        out_specs = list(out_specs) + [_hbm_spec()] * len(comm.out_shapes)
        out_shape = list(out_shape) + comm.out_shapes
        scratch_shapes = list(scratch_shapes) + comm.sem_shapes
        args += comm.ins
        extra = dict(has_side_effects=True)
    outs = _pcall(_host(body, comm, n_in, n_out, n_scr, nsteps, step_fn), name=name, grid=grid,
                  in_specs=in_specs, out_specs=out_specs, out_shape=out_shape, scratch_shapes=scratch_shapes,
                  input_output_aliases=aliases or {}, compiler_params=_cp(sem, **extra))(*args)
    return list(outs[:n_out]), list(outs[n_out:])


def _run_comm(comm, *, name):
    ci, co = len(comm.ins), len(comm.out_shapes)

    def body(*refs):
        for _, fn in comm.phases:
            fn(refs[:ci], refs[ci:ci + co], refs[ci + co:])

    return _pcall(body, name=name, in_specs=[_hbm_spec()] * ci, out_specs=[_hbm_spec()] * co,
                  out_shape=comm.out_shapes, scratch_shapes=comm.sem_shapes,
                  compiler_params=pltpu.CompilerParams(has_side_effects=True))(*comm.ins)


def _hbm_spec():
    return pl.BlockSpec(memory_space=pl.ANY)


def _mm(a, b, *, m, n, k, tm, tn, tk, ta=False, tb=False, out_dtype=F32, resid=None, name):
    tm, tn, tk = min(tm, m), min(tn, n), min(tk, k)
    gm, gn, gk = m // tm, n // tn, k // tk
    assert gm * tm == m and gn * tn == n and gk * tk == k, (name, m, n, k, tm, tn, tk)
    a_spec = (pl.BlockSpec((tk, tm), lambda i, j, l: (l, i)) if ta
              else pl.BlockSpec((tm, tk), lambda i, j, l: (i, l)))
    b_spec = (pl.BlockSpec((tn, tk), lambda i, j, l: (j, l)) if tb
              else pl.BlockSpec((tk, tn), lambda i, j, l: (l, j)))
    dims = (((0 if ta else 1,), (1 if tb else 0,)), ((), ()))
    ins, in_specs = [a, b], [a_spec, b_spec]
    if resid is not None:
        ins.append(resid)
        in_specs.append(pl.BlockSpec((tm, tn), lambda i, j, l: (i, j)))
    scratch = [pltpu.VMEM((tm, tn), F32)] if gk > 1 else []

    def body(*refs):
        it = iter(refs)
        a_ref, b_ref = next(it), next(it)
        resid_ref = next(it) if resid is not None else None
        o_ref = next(it)
        acc_ref = next(it) if gk > 1 else None
        l = pl.program_id(2)
        part = lax.dot_general(a_ref[...].astype(BF), b_ref[...].astype(BF), dims,
                               preferred_element_type=F32)

        def finish(acc):
            if resid_ref is not None:
                acc = acc + resid_ref[...].astype(F32)
            o_ref[...] = acc.astype(out_dtype)

        if gk == 1:
            finish(part)
        else:
            _accum(acc_ref, part, l == 0)

            @pl.when(l == gk - 1)
            def _():
                finish(acc_ref[...])

    return _pcall(body, name=name, grid=(gm, gn, gk), in_specs=in_specs,
                  out_specs=pl.BlockSpec((tm, tn), lambda i, j, l: (i, j)),
                  out_shape=jax.ShapeDtypeStruct((m, n), out_dtype), scratch_shapes=scratch,
                  compiler_params=_cp(("parallel", "parallel", "arbitrary")))(*ins)


def _fmm(lhs, rhs, extras, epilogue, outs, *, m, n, tm, tn, name, comm=None):
    tm, tn = min(tm, m), min(tn, n)
    assert m % tm == 0 and n % tn == 0, (name, m, n, tm, tn)
    in_specs, args = [], []
    for a in lhs:
        in_specs.append(pl.BlockSpec((tm, a.shape[1]), lambda i, j: (i, 0)))
        args.append(a)
    for li, b, tb in rhs:
        k = lhs[li].shape[1]
        in_specs.append(pl.BlockSpec((tn, k), lambda i, j: (j, 0)) if tb
                        else pl.BlockSpec((k, tn), lambda i, j: (0, j)))
        args.append(b)
    for arr, w, col in extras:
        in_specs.append(pl.BlockSpec((tm, w), lambda i, j, col=col: (i, col(j))))
        args.append(arr)
    out_specs = [pl.BlockSpec((tm, w), lambda i, j, col=col: (i, col(j))) for _, _, w, col in outs]
    out_shape = [jax.ShapeDtypeStruct((m, total), dt) for dt, total, _, _ in outs]
    nl, nr, ne = len(lhs), len(rhs), len(extras)

    def body(*refs):
        prods = []
        for r, (li, _, tb) in enumerate(rhs):
            prods.append(lax.dot_general(refs[li][...], refs[nl + r][...], _NT if tb else _NN,
                                         preferred_element_type=F32))
        vals = epilogue(prods, [ref[...] for ref in refs[nl + nr:nl + nr + ne]])
        for o_ref, v in zip(refs[nl + nr + ne:], vals):
            o_ref[...] = v.astype(o_ref.dtype)

    gm, gn = m // tm, n // tn
    res, comm_res = _hosted_call(
        body, comm, args, name=name, grid=(gm, gn), in_specs=in_specs, out_specs=out_specs,
        out_shape=out_shape, scratch_shapes=[], sem=("arbitrary", "arbitrary"), nsteps=gm * gn,
        step_fn=lambda: pl.program_id(0) * gn + pl.program_id(1))
    return res if comm is None else (res, comm_res)


def _grp_of(i):
    return [jnp.logical_and(i >= GRP_OFF[g], i < GRP_OFF[g] + GRP_N[g]) for g in range(4)]


def _grp_idx(i, g):
    return jnp.clip(i - GRP_OFF[g], 0, GRP_N[g] - 1)


def _inproj_fwd(u, win_t, b_in, *, t, comm=None):
    n_tiles = IN_W // WT
    dims = (((1,), (1,)), ((), ()))
    dtypes = (BF, BF, F32, F32)

    def body(u_ref, w_ref, b_ref, *o_refs):
        i = pl.program_id(0)
        p = lax.dot_general(u_ref[...], w_ref[...], dims, preferred_element_type=F32) + b_ref[...]
        for g, pred in enumerate(_grp_of(i)):
            @pl.when(pred)
            def _(g=g):
                o_refs[g][...] = p.astype(dtypes[g])

    return _hosted_call(
        body, comm, (u, win_t, b_in), name="inproj_fwd", grid=(n_tiles,),
        in_specs=[pl.BlockSpec((t, D), lambda i: (0, 0)),
                  pl.BlockSpec((WT, D), lambda i: (i, 0)),
                  pl.BlockSpec((1, WT), lambda i: (0, i))],
        out_specs=[pl.BlockSpec((t, WT), lambda i, g=g: (0, _grp_idx(i, g))) for g in range(4)],
        out_shape=[jax.ShapeDtypeStruct((t, GRP_N[g] * WT), dtypes[g]) for g in range(4)],
        scratch_shapes=[], sem=("arbitrary",), nsteps=n_tiles, step_fn=lambda: pl.program_id(0))


def _inproj_bwd_x(dps, win_t, x, g, resid, *, t, part, prev=None, comm=None):
    n_tiles = IN_W // WT
    per = 2 if t >= 2048 else 1
    tm = t // (2 * per)
    row = lambda i: part * per + i

    def body(d0, d1, d2, d3, w_ref, x_ref, g_ref, r_ref, *rest):
        dg_prev = rest[0] if prev is not None else None
        o_ref, dg_ref, acc_ref = rest[-3], rest[-2], rest[-1]
        i, l = pl.program_id(0), pl.program_id(1)
        w = w_ref[...]
        for pred, d_ref in zip(_grp_of(l), (d0, d1, d2, d3)):
            @pl.when(pred)
            def _(d_ref=d_ref):
                _accum(acc_ref, jnp.dot(d_ref[...], w, preferred_element_type=F32), l == 0)

        @pl.when(l == n_tiles - 1)
        def _():
            xv = x_ref[...]
            r = lax.rsqrt(jnp.mean(xv * xv, axis=-1, keepdims=True) + EPS)
            xh = xv * r
            du = acc_ref[...]
            dxh = du * g_ref[...]
            o_ref[...] = r_ref[...] + r * (dxh - xh * jnp.mean(dxh * xh, axis=-1, keepdims=True))
            dg = jnp.sum(du * xh, axis=0, keepdims=True)
            if dg_prev is not None:
                dg = dg + jnp.where(i == 0, 1.0, 0.0) * dg_prev[...]
            _accum(dg_ref, dg, i == 0)

    rows = lambda w: pl.BlockSpec((tm, w), lambda i, l: (row(i), 0))
    in_specs = ([pl.BlockSpec((tm, WT), lambda i, l, g=g: (row(i), _grp_idx(l, g))) for g in range(4)]
                + [pl.BlockSpec((WT, D), lambda i, l: (l, 0)), rows(D),
                   pl.BlockSpec((1, D), lambda i, l: (0, 0)), rows(D)])
    args = list(dps) + [win_t, x, g, resid]
    aliases = None
    if prev is not None:
        in_specs += [pl.BlockSpec((1, D), lambda i, l: (0, 0)), _hbm_spec()]
        args += [prev[1], prev[0]]
        aliases = {len(args) - 1: 0}
    return _hosted_call(
        body, comm, args, name="inproj_bwd_x%d" % part, grid=(per, n_tiles), in_specs=in_specs,
        out_specs=[rows(D), pl.BlockSpec((1, D), lambda i, l: (0, 0))],
        out_shape=[jax.ShapeDtypeStruct((t, D), F32), jax.ShapeDtypeStruct((1, D), F32)],
        scratch_shapes=[pltpu.VMEM((tm, D), F32)], sem=("arbitrary", "arbitrary"), nsteps=per * n_tiles,
        step_fn=lambda: pl.program_id(0) * n_tiles + pl.program_id(1), aliases=aliases)


def _inproj_bwd_w(dps, u, *, t):
    n_tiles = IN_W // WT
    dims = (((0,), (0,)), ((), ()))

    def body(d0, d1, d2, d3, u_ref, o_ref, db_ref):
        i = pl.program_id(0)
        uv = u_ref[...]
        for g, (pred, d_ref) in enumerate(zip(_grp_of(i), (d0, d1, d2, d3))):
            @pl.when(pred)
            def _(d_ref=d_ref):
                dv = d_ref[...]
                o_ref[...] = lax.dot_general(dv, uv, dims, preferred_element_type=F32).astype(BF)
                db_ref[...] = jnp.sum(dv.astype(F32), axis=0, keepdims=True)

    return _pcall(body, name="inproj_bwd_w", grid=(n_tiles,),
                  in_specs=[pl.BlockSpec((t, WT), lambda i, g=g: (0, _grp_idx(i, g))) for g in range(4)]
                  + [pl.BlockSpec((t, D), lambda i: (0, 0))],
                  out_specs=[pl.BlockSpec((WT, D), lambda i: (i, 0)),
                             pl.BlockSpec((1, WT), lambda i: (0, i))],
                  out_shape=[jax.ShapeDtypeStruct((IN_W, D), BF), jax.ShapeDtypeStruct((1, IN_W), F32)],
                  compiler_params=_cp(("arbitrary",)))(*dps, u)


def _row_spec(tm, width, col=0):
    return pl.BlockSpec((tm, width), lambda i: (i, col))


def _vec_spec(width):
    return pl.BlockSpec((1, width), lambda i: (0, 0))


def _rms_fwd(x, g, *, tm, name):
    t = x.shape[0]
    tm = min(tm, t)

    def body(x_ref, g_ref, u_ref):
        xv = x_ref[...]
        r = lax.rsqrt(jnp.mean(xv * xv, axis=-1, keepdims=True) + EPS)
        u_ref[...] = (xv * r * g_ref[...]).astype(BF)

    return _pcall(body, name=name, grid=(t // tm,), in_specs=[_row_spec(tm, D), _vec_spec(D)],
                  out_specs=_row_spec(tm, D), out_shape=jax.ShapeDtypeStruct((t, D), BF),
                  compiler_params=_cp(("parallel",)))(x, g)


def _rms_bwd(du, x, g, resid, *, tm, name):
    t = x.shape[0]
    tm = min(tm, t)

    def body(du_ref, x_ref, g_ref, r_ref, dx_ref, dxb_ref, dg_ref):
        xv = x_ref[...]
        r = lax.rsqrt(jnp.mean(xv * xv, axis=-1, keepdims=True) + EPS)
        xh = xv * r
        duv = du_ref[...]
        dxh = duv * g_ref[...]
        dx = r_ref[...] + r * (dxh - xh * jnp.mean(dxh * xh, axis=-1, keepdims=True))
        dx_ref[...] = dx
        dxb_ref[...] = dx.astype(BF)
        _accum(dg_ref, jnp.sum(duv * xh, axis=0, keepdims=True), pl.program_id(0) == 0)

    return _pcall(body, name=name, grid=(t // tm,),
                  in_specs=[_row_spec(tm, D), _row_spec(tm, D), _vec_spec(D), _row_spec(tm, D)],
                  out_specs=[_row_spec(tm, D), _row_spec(tm, D), _vec_spec(D)],
                  out_shape=[jax.ShapeDtypeStruct((t, D), F32), jax.ShapeDtypeStruct((t, D), BF),
                             jax.ShapeDtypeStruct((1, D), F32)],
                  compiler_params=_cp(("arbitrary",)))(du, x, g, resid)


def _loss_head(h2, tgt, g, *, tm):
    t = h2.shape[0]
    tm = min(tm, t)

    def body(h_ref, t_ref, g_ref, dh_ref, dhb_ref, dg_ref, loss_ref):
        hv = h_ref[...]
        gv = g_ref[...]
        r = lax.rsqrt(jnp.mean(hv * hv, axis=-1, keepdims=True) + EPS)
        xh = hv * r
        err = xh * gv - t_ref[...]
        lp = jnp.sum(jnp.sum(err * err, axis=1, keepdims=True), axis=0, keepdims=True) * (0.5 / D)
        dy = err * (1.0 / D)
        dxh = dy * gv
        dh = r * (dxh - xh * jnp.mean(dxh * xh, axis=-1, keepdims=True))
        dh_ref[...] = dh
        dhb_ref[...] = dh.astype(BF)
        first = pl.program_id(0) == 0
        _accum(dg_ref, jnp.sum(dy * xh, axis=0, keepdims=True), first)
        _accum(loss_ref, jnp.broadcast_to(lp, (1, 128)), first)

    return _pcall(body, name="loss_head", grid=(t // tm,),
                  in_specs=[_row_spec(tm, D), _row_spec(tm, D), _vec_spec(D)],
                  out_specs=[_row_spec(tm, D), _row_spec(tm, D), _vec_spec(D), _vec_spec(128)],
                  out_shape=[jax.ShapeDtypeStruct((t, D), F32), jax.ShapeDtypeStruct((t, D), BF),
                             jax.ShapeDtypeStruct((1, D), F32), jax.ShapeDtypeStruct((1, 128), F32)],
                  compiler_params=_cp(("arbitrary",)))(h2, tgt, g)


def _attn_kv_tiles(kprev, kcur):
    kv = jnp.concatenate([kprev, kcur], axis=0).astype(F32)
    lo = lax.broadcasted_iota(jnp.int32, (2 * BLK, 128), 1) < HEAD
    tiles = []
    for part in (kv[:, 0:128], kv[:, 128:256]):
        rolled = pltpu.roll(part, HEAD, 1)
        z = jnp.zeros_like(part)
        tiles.append(((jnp.where(lo, part, z).astype(BF), jnp.where(lo, z, rolled).astype(BF)),
                      (jnp.where(lo, rolled, z).astype(BF), jnp.where(lo, z, part).astype(BF))))
    k_t, v_t = tiles
    return [(jnp.concatenate(k_t[h], axis=0), jnp.concatenate(v_t[h], axis=0)) for h in range(2)]


def _attn_mask(i):
    qi = lax.broadcasted_iota(jnp.int32, (BLK, 2 * BLK), 0)
    kj = lax.broadcasted_iota(jnp.int32, (BLK, 2 * BLK), 1)
    first_key = jnp.where(i == 0, BLK, 0)
    in_prev = jnp.logical_and(jnp.logical_and(kj < BLK, kj > qi), kj >= first_key)
    in_cur = jnp.logical_and(kj >= BLK, kj - BLK <= qi)
    return jnp.logical_or(in_prev, in_cur)


def _attn_probs(s, sink, valid):
    s = jnp.where(valid, s * SCALE, NEG)
    mx = jnp.maximum(jnp.max(s, axis=-1, keepdims=True), sink)
    e = jnp.exp(s - mx)
    es = jnp.exp(sink - mx)
    inv = 1.0 / (jnp.sum(e, axis=-1, keepdims=True) + es)
    return e * inv, es * inv


_KEYS = 2 * BLK


def _pair(ref, j):
    return ref[:, j * 128:(j + 1) * 128]


def _attn_fwd(q, kv, sinks, *, t, comm=None):
    nb = t // BLK

    def body(sink_ref, q_ref, kp_ref, kc_ref, o_ref):
        valid = _attn_mask(pl.program_id(0))
        tiles = _attn_kv_tiles(kp_ref[...], kc_ref[...])
        s = [lax.dot_general(_pair(q_ref, j), tiles[j // 4][0], _NT, preferred_element_type=F32)
             for j in range(N_PAIR)]
        p = []
        for j in range(N_PAIR):
            pe, _ = _attn_probs(s[j][:, 0:_KEYS], sink_ref[0, 2 * j], valid)
            po, _ = _attn_probs(s[j][:, _KEYS:2 * _KEYS], sink_ref[0, 2 * j + 1], valid)
            p.append(jnp.concatenate([pe.astype(BF), po.astype(BF)], axis=1))
        for j in range(N_PAIR):
            o_ref[:, j * 128:(j + 1) * 128] = jnp.dot(p[j], tiles[j // 4][1],
                                                      preferred_element_type=F32).astype(BF)

    return _hosted_call(
        body, comm, (sinks, q, kv, kv), name="attn_fwd", grid=(nb,),
        in_specs=[pl.BlockSpec(memory_space=pltpu.SMEM),
                  pl.BlockSpec((BLK, D), lambda i: (i, 0)),
                  pl.BlockSpec((BLK, 256), lambda i: (jnp.maximum(i - 1, 0), 0)),
                  pl.BlockSpec((BLK, 256), lambda i: (i, 0))],
        out_specs=[pl.BlockSpec((BLK, D), lambda i: (i, 0))],
        out_shape=[jax.ShapeDtypeStruct((t, D), BF)],
        scratch_shapes=[], sem=("arbitrary",), nsteps=nb, step_fn=lambda: pl.program_id(0))


def _attn_bwd(q, kv, sinks, do, *, t, comm=None):
    nb = t // BLK
    last = nb - 1

    def body(sink_ref, q_ref, kp_ref, kc_ref, do_ref, dq_ref, dkv_ref, ds_ref, carry_ref):
        i = pl.program_id(0)

        @pl.when(i == 0)
        def _():
            ds_ref[...] = jnp.zeros_like(ds_ref)
            carry_ref[...] = jnp.zeros_like(carry_ref)

        @pl.when(i < nb)
        def _():
            valid = _attn_mask(i)
            tiles = _attn_kv_tiles(kp_ref[...], kc_ref[...])
            lane1 = lax.broadcasted_iota(jnp.int32, (1, 128), 1)
            dsink = jnp.zeros((1, 128), F32)
            s = [lax.dot_general(_pair(q_ref, j), tiles[j // 4][0], _NT, preferred_element_type=F32)
                 for j in range(N_PAIR)]
            dp = [lax.dot_general(_pair(do_ref, j), tiles[j // 4][1], _NT, preferred_element_type=F32)
                  for j in range(N_PAIR)]
            p_all, ds_all = [], []
            for j in range(N_PAIR):
                halves = []
                for par in range(2):
                    cols = slice(par * _KEYS, (par + 1) * _KEYS)
                    p, ps = _attn_probs(s[j][:, cols], sink_ref[0, 2 * j + par], valid)
                    dpj = dp[j][:, cols]
                    dd = jnp.sum(p * dpj, axis=-1, keepdims=True)
                    dsink = dsink + jnp.where(lane1 == 2 * j + par,
                                              -jnp.sum(ps * dd, axis=0, keepdims=True), 0.0)
                    halves.append((p.astype(BF), (p * (dpj - dd)).astype(BF)))
                p_all.append(jnp.concatenate([halves[0][0], halves[1][0]], axis=1))
                ds_all.append(jnp.concatenate([halves[0][1], halves[1][1]], axis=1))
            for j in range(N_PAIR):
                dq_ref[:, j * 128:(j + 1) * 128] = (
                    jnp.dot(ds_all[j], tiles[j // 4][0], preferred_element_type=F32) * SCALE).astype(BF)
            ds_ref[...] += dsink
            gk, gv = [], []
            for h in range(2):
                grp = range(4 * h, 4 * h + 4)
                q_rows = jnp.concatenate([_pair(q_ref, j) for j in grp], axis=0)
                do_rows = jnp.concatenate([_pair(do_ref, j) for j in grp], axis=0)
                g_k = lax.dot_general(jnp.concatenate([ds_all[j] for j in grp], axis=0), q_rows, _TN,
                                      preferred_element_type=F32)
                g_v = lax.dot_general(jnp.concatenate([p_all[j] for j in grp], axis=0), do_rows, _TN,
                                      preferred_element_type=F32)
                gk.append((g_k[0:_KEYS], g_k[_KEYS:2 * _KEYS]))
                gv.append((g_v[0:_KEYS], g_v[_KEYS:2 * _KEYS]))
            lo = lax.broadcasted_iota(jnp.int32, (2 * BLK, 128), 1) < HEAD
            zero = jnp.zeros((2 * BLK, 128), F32)

            def unpad(g):
                return (jnp.where(lo, g[0][0] + pltpu.roll(g[0][1], HEAD, 1), zero)
                        + jnp.where(lo, zero, pltpu.roll(g[1][0], HEAD, 1) + g[1][1]))

            dk = unpad(gk) * SCALE
            dv = unpad(gv)
            dkv_ref[:, 0:128] = (carry_ref[:, 0:128] + dk[0:BLK]).astype(BF)
            dkv_ref[:, 128:256] = (carry_ref[:, 128:256] + dv[0:BLK]).astype(BF)
            carry_ref[:, 0:128] = dk[BLK:2 * BLK]
            carry_ref[:, 128:256] = dv[BLK:2 * BLK]

        @pl.when(i == nb)
        def _():
            dkv_ref[...] = carry_ref[...].astype(BF)

    return _hosted_call(
        body, comm, (sinks, q, kv, kv, do), name="attn_bwd", grid=(nb + 1,),
        in_specs=[pl.BlockSpec(memory_space=pltpu.SMEM),
                  pl.BlockSpec((BLK, D), lambda i: (jnp.minimum(i, last), 0)),
                  pl.BlockSpec((BLK, 256), lambda i: (jnp.clip(i - 1, 0, last), 0)),
                  pl.BlockSpec((BLK, 256), lambda i: (jnp.minimum(i, last), 0)),
                  pl.BlockSpec((BLK, D), lambda i: (jnp.minimum(i, last), 0))],
        out_specs=[pl.BlockSpec((BLK, D), lambda i: (jnp.minimum(i, last), 0)),
                   pl.BlockSpec((BLK, 256), lambda i: (jnp.maximum(i - 1, 0), 0)),
                   pl.BlockSpec((1, 128), lambda i: (0, 0))],
        out_shape=[jax.ShapeDtypeStruct((t, D), BF), jax.ShapeDtypeStruct((t, 256), BF),
                   jax.ShapeDtypeStruct((1, 128), F32)],
        scratch_shapes=[pltpu.VMEM((BLK, 256), F32)], sem=("arbitrary",), nsteps=nb + 1,
        step_fn=lambda: pl.program_id(0))


def _split3(v):
    h = v.astype(BF)
    r = v - h.astype(F32)
    m = r.astype(BF)
    lo = (r - m.astype(F32)).astype(BF)
    return jnp.concatenate([h, m, lo], axis=1)


def _apply01(mat, v):
    n = v.shape[1]
    r = jnp.dot(mat, _split3(v), preferred_element_type=F32)
    return r[:, 0:n] + r[:, n:2 * n] + r[:, 2 * n:3 * n]


def _hgrn_gates(hq, hf, lb):
    sq = _sig(hq)
    sg = _sig(hf)
    f = lb + (1.0 - lb) * sg
    return hq * sq, (1.0 - lb) * (1.0 - sg), jnp.log(f), sq, sg, f


def _tri(upper):
    r = lax.broadcasted_iota(jnp.int32, (CH, CH), 0)
    c = lax.broadcasted_iota(jnp.int32, (CH, CH), 1)
    return (c >= r) if upper else (c <= r)


def _lb_from_logits(lg_ref):
    return 1.0 / (1.0 + jnp.exp(lg_ref[1:2, :] - lg_ref[0:1, :]))


def _hgrn_fwd(h4, logits, norm_g, *, t, comm=None):
    nc = t // CH
    nt_dims = (((1,), (1,)), ((), ()))
    tn_dims = (((0,), (0,)), ((), ()))

    def body(h_ref, lg_ref, ng_ref, y_ref, o_ref, st_ref, s_scr, b_scr, qa_s, ka_s, qb_s, kb_s, v_s):
        @pl.when(pl.program_id(0) == 0)
        def _():
            s_scr[...] = jnp.zeros_like(s_scr)

        heads = [slice(h * HG_K, (h + 1) * HG_K) for h in range(HG_HEADS)]
        causal = _tri(False)
        q, k, g, _, _, _ = _hgrn_gates(h_ref[:, 0:D], h_ref[:, D:2 * D], _lb_from_logits(lg_ref))
        b_scr[...] = _apply01(jnp.where(causal, 1.0, 0.0).astype(BF), g)
        b = b_scr[...]
        b_mid = b_scr[CH // 2 - 1:CH // 2, :]
        b_last = b_scr[CH - 1:CH, :]
        qa_s[...] = (q * jnp.exp(b - b_mid)).astype(BF)
        ka_s[...] = (k * jnp.exp(b_mid - b)).astype(BF)
        qb_s[...] = (q * jnp.exp(b)).astype(BF)
        kb_s[...] = (k * jnp.exp(b_last - b)).astype(BF)
        v_s[...] = h_ref[:, 2 * D:3 * D].astype(BF)
        dec = jnp.exp(b_last)
        st_ref[0] = s_scr[...]
        a = [jnp.where(causal, lax.dot_general(qa_s[:, sl], ka_s[:, sl], nt_dims, preferred_element_type=F32),
                       0.0).astype(BF) for sl in heads]
        for h, sl in enumerate(heads):
            o_ref[:, sl] = (jnp.dot(a[h], v_s[:, sl], preferred_element_type=F32)
                            + lax.dot_general(qb_s[:, sl], s_scr[h].astype(BF), nt_dims,
                                              preferred_element_type=F32))
        for h, sl in enumerate(heads):
            s_scr[h] = dec[:, sl] * s_scr[h] + lax.dot_general(v_s[:, sl], kb_s[:, sl], tn_dims,
                                                               preferred_element_type=F32)
        for h, sl in enumerate(heads):
            o = o_ref[:, sl]
            on = o * lax.rsqrt(jnp.mean(o * o, axis=-1, keepdims=True) + EPS)
            y_ref[:, sl] = (on * ng_ref[:, sl] * _sig(h_ref[:, 3 * D + h * HG_K:3 * D + (h + 1) * HG_K])).astype(BF)

    half = lambda: pltpu.VMEM((CH, D), BF)
    return _hosted_call(
        body, comm, (h4, logits, norm_g), name="hgrn_fwd", grid=(nc,),
        in_specs=[pl.BlockSpec((CH, 4 * D), lambda n: (n, 0)),
                  pl.BlockSpec((2, D), lambda n: (0, 0)),
                  pl.BlockSpec((1, D), lambda n: (0, 0))],
        out_specs=[pl.BlockSpec((CH, D), lambda n: (n, 0)),
                   pl.BlockSpec((CH, D), lambda n: (n, 0)),
                   pl.BlockSpec((1, HG_HEADS, HG_K, HG_K), lambda n: (n, 0, 0, 0))],
        out_shape=[jax.ShapeDtypeStruct((t, D), BF), jax.ShapeDtypeStruct((t, D), F32),
                   jax.ShapeDtypeStruct((nc, HG_HEADS, HG_K, HG_K), F32)],
        scratch_shapes=[pltpu.VMEM((HG_HEADS, HG_K, HG_K), F32), pltpu.VMEM((CH, D), F32),
                        half(), half(), half(), half(), half()],
        sem=("arbitrary",), nsteps=nc, step_fn=lambda: pl.program_id(0))


def _hgrn_bwd(h4, logits, norm_g, o_pre, states, dy, *, t, comm=None):
    nc = t // CH
    nt_dims = (((1,), (1,)), ((), ()))
    tn_dims = (((0,), (0,)), ((), ()))

    def body(h_ref, lg_ref, ng_ref, o_ref, st_ref, dy_ref, dh_ref, dlg_ref, dng_ref, ds_scr, dlb_scr,
             b_scr, tail_s, e_qa, e_ka, e_qb, e_kb, q_s, k_s, dqa_s, dka_s, dqb_s, dkb_s,
             qa_s, ka_s, qb_s, kb_s, v_s, do_s):
        n = pl.program_id(0)

        @pl.when(n == 0)
        def _():
            ds_scr[...] = jnp.zeros_like(ds_scr)
            dlb_scr[...] = jnp.zeros_like(dlb_scr)
            dng_ref[...] = jnp.zeros_like(dng_ref)

        heads = [slice(h * HG_K, (h + 1) * HG_K) for h in range(HG_HEADS)]
        lb = _lb_from_logits(lg_ref)
        causal = _tri(False)
        q, k, g, _, _, _ = _hgrn_gates(h_ref[:, 0:D], h_ref[:, D:2 * D], lb)
        b_scr[...] = _apply01(jnp.where(causal, 1.0, 0.0).astype(BF), g)
        b = b_scr[...]
        b_mid = b_scr[CH // 2 - 1:CH // 2, :]
        b_last = b_scr[CH - 1:CH, :]
        q_s[...] = q
        k_s[...] = k
        for e_ref, s_ref, base, expo in ((e_qa, qa_s, q, b - b_mid), (e_ka, ka_s, k, b_mid - b),
                                         (e_qb, qb_s, q, b), (e_kb, kb_s, k, b_last - b)):
            e = jnp.exp(expo)
            e_ref[...] = e
            s_ref[...] = (base * e).astype(BF)
        v_s[...] = h_ref[:, 2 * D:3 * D].astype(BF)
        dec = jnp.exp(b_last)
        for h, sl in enumerate(heads):
            gcol = slice(3 * D + h * HG_K, 3 * D + (h + 1) * HG_K)
            ngh = ng_ref[:, sl]
            sgate = _sig(h_ref[:, gcol])
            o = o_ref[:, sl]
            r = lax.rsqrt(jnp.mean(o * o, axis=-1, keepdims=True) + EPS)
            on = o * r
            dyh = dy_ref[:, sl]
            dh_ref[:, gcol] = (dyh * on * ngh * sgate * (1.0 - sgate)).astype(BF)
            dng_ref[:, sl] += jnp.sum(dyh * on * sgate, axis=0, keepdims=True)
            don = dyh * ngh * sgate
            do_s[:, sl] = (r * (don - on * jnp.mean(don * on, axis=-1, keepdims=True))).astype(BF)
        a = [jnp.where(causal, lax.dot_general(qa_s[:, sl], ka_s[:, sl], nt_dims, preferred_element_type=F32),
                       0.0).astype(BF) for sl in heads]
        da = [jnp.where(causal, lax.dot_general(do_s[:, sl], v_s[:, sl], nt_dims, preferred_element_type=F32),
                        0.0).astype(BF) for sl in heads]
        for h, sl in enumerate(heads):
            dh_ref[:, 2 * D + h * HG_K:2 * D + (h + 1) * HG_K] = (
                lax.dot_general(a[h], do_s[:, sl], tn_dims, preferred_element_type=F32)
                + lax.dot_general(kb_s[:, sl], ds_scr[h].astype(BF), nt_dims, preferred_element_type=F32)
            ).astype(BF)
        for h, sl in enumerate(heads):
            dqa_s[:, sl] = jnp.dot(da[h], ka_s[:, sl], preferred_element_type=F32)
        for h, sl in enumerate(heads):
            dka_s[:, sl] = lax.dot_general(da[h], qa_s[:, sl], tn_dims, preferred_element_type=F32)
        for h, sl in enumerate(heads):
            dqb_s[:, sl] = jnp.dot(do_s[:, sl], st_ref[0, h].astype(BF), preferred_element_type=F32)
        for h, sl in enumerate(heads):
            dkb_s[:, sl] = jnp.dot(v_s[:, sl], ds_scr[h].astype(BF), preferred_element_type=F32)
        for h, sl in enumerate(heads):
            tail_s[:, sl] = jnp.sum(dec[:, sl] * st_ref[0, h] * ds_scr[h], axis=0, keepdims=True)
        for h, sl in enumerate(heads):
            ds_scr[h] = (lax.dot_general(do_s[:, sl], qb_s[:, sl], tn_dims, preferred_element_type=F32)
                         + dec[:, sl] * ds_scr[h])
        qv, kv = q_s[...], k_s[...]
        dqa, dka, dqb, dkb = dqa_s[...], dka_s[...], dqb_s[...], dkb_s[...]
        eqa, eka, eqb, ekb = e_qa[...], e_ka[...], e_qb[...], e_kb[...]
        dkb_kb = dkb * (kv * ekb)
        db_last = jnp.sum(dkb_kb, axis=0, keepdims=True) + tail_s[...]
        last_row = lax.broadcasted_iota(jnp.int32, (CH, D), 0) == CH - 1
        db = (dqa * (qv * eqa) - dka * (kv * eka) + dqb * (qv * eqb) - dkb_kb
              + jnp.where(last_row, db_last, 0.0))
        dg = _apply01(jnp.where(_tri(True), 1.0, 0.0).astype(BF), db)
        dq = dqa * eqa + dqb * eqb
        dk = dka * eka + dkb * ekb
        hq = h_ref[:, 0:D]
        _, _, _, sq, sg, f = _hgrn_gates(hq, h_ref[:, D:2 * D], lb)
        dh_ref[:, 0:D] = (dq * sq * (1.0 + hq * (1.0 - sq))).astype(BF)
        dfk = dg / f - dk
        dh_ref[:, D:2 * D] = ((1.0 - lb) * dfk * sg * (1.0 - sg)).astype(BF)
        dlb_scr[...] += jnp.sum((1.0 - sg) * dfk, axis=0, keepdims=True)

        @pl.when(n == nc - 1)
        def _():
            dl0 = dlb_scr[...] * lb * (1.0 - lb)
            dlg_ref[0:1, :] = dl0
            dlg_ref[1:2, :] = -dl0

    rev = lambda n: (nc - 1 - n, 0)
    return _hosted_call(
        body, comm, (h4, logits, norm_g, o_pre, states, dy), name="hgrn_bwd", grid=(nc,),
        in_specs=[pl.BlockSpec((CH, 4 * D), rev),
                  pl.BlockSpec((2, D), lambda n: (0, 0)),
                  pl.BlockSpec((1, D), lambda n: (0, 0)),
                  pl.BlockSpec((CH, D), rev),
                  pl.BlockSpec((1, HG_HEADS, HG_K, HG_K), lambda n: (nc - 1 - n, 0, 0, 0)),
                  pl.BlockSpec((CH, D), rev)],
        out_specs=[pl.BlockSpec((CH, 4 * D), rev),
                   pl.BlockSpec((2, D), lambda n: (0, 0)),
                   pl.BlockSpec((1, D), lambda n: (0, 0))],
        out_shape=[jax.ShapeDtypeStruct((t, 4 * D), BF), jax.ShapeDtypeStruct((2, D), F32),
                   jax.ShapeDtypeStruct((1, D), F32)],
        scratch_shapes=([pltpu.VMEM((HG_HEADS, HG_K, HG_K), F32), pltpu.VMEM((1, D), F32),
                         pltpu.VMEM((CH, D), F32), pltpu.VMEM((1, D), F32)]
                        + [pltpu.VMEM((CH, D), F32)] * 10 + [pltpu.VMEM((CH, D), BF)] * 6),
        sem=("arbitrary",), nsteps=nc, step_fn=lambda: pl.program_id(0))


def _place():
    x, y, c = lax.axis_index("x"), lax.axis_index("y"), lax.axis_index("c")
    return x, y, c, [(1 - x, y), (x, 1 - y), (1 - x, 1 - y)]


def _gather_comm(shards, mid):
    n = len(shards)
    r = [s.shape[0] for s in shards]

    def tools(ins, outs, sems):
        send_sems, recv_sems, local_sems = sems
        x, y, c, chips = _place()
        me, sib = (x, y, c), (x, y, 1 - c)

        def rows(w, dev):
            return outs[w].at[pl.ds((4 * dev[0] + 2 * dev[1] + dev[2]) * r[w], r[w]), :]

        def copy(kind, w, block, to, src=None):
            return pltpu.make_async_remote_copy(
                src_ref=rows(w, block) if src is None else src, dst_ref=rows(w, block),
                send_sem=send_sems.at[kind], recv_sem=recv_sems.at[kind], device_id=to, device_id_type=MESH)

        def all_of(kind):
            whole = outs[0].at[pl.ds(0, sum(r)), :]
            return pltpu.make_async_remote_copy(
                src_ref=whole, dst_ref=whole, send_sem=send_sems.at[kind], recv_sem=recv_sems.at[kind],
                device_id=me, device_id_type=MESH)

        mine = [pltpu.make_async_copy(ins[w], rows(w, me), local_sems.at[w]) for w in range(n)]
        return c, chips, me, sib, copy, all_of, mine

    def start(ins, outs, sems):
        c, chips, me, sib, copy, _, mine = tools(ins, outs, sems)
        for cp in mine:
            cp.start()
        for w in range(n):
            copy(0, w, me, sib, src=ins[w]).start()
            for j, chip in enumerate(chips):
                copy(1 + j, w, me, (*chip, c), src=ins[w]).start()

    def pass_on(ins, outs, sems):
        c, chips, _, sib, copy, all_of, _ = tools(ins, outs, sems)
        for j, chip in enumerate(chips):
            all_of(1 + j).wait_recv()
            for w in range(n):
                copy(4 + j, w, (*chip, c), sib).start()

    def finish(ins, outs, sems):
        _, _, _, _, _, all_of, mine = tools(ins, outs, sems)
        all_of(0).wait_recv()
        for j in range(3):
            all_of(4 + j).wait_recv()
        for kind in range(7):
            all_of(kind).wait_send()
        for cp in mine:
            cp.wait()

    return _Comm(shards, [jax.ShapeDtypeStruct((N_DEV * rw, D), BF) for rw in r],
                 [pltpu.SemaphoreType.DMA((7,)), pltpu.SemaphoreType.DMA((7,)), pltpu.SemaphoreType.DMA((n,))],
                 [(0.0, start), (mid, pass_on), (1.0, finish)])


def _pair_comm(grads):
    n = len(grads)
    r = [g.shape[0] // N_DEV for g in grads]

    def start(ins, outs, sems):
        send_sems, recv_sems = sems
        x, y, c, _ = _place()
        for w in range(n):
            for a in range(N_CHIP):
                pltpu.make_async_remote_copy(
                    src_ref=ins[w].at[pl.ds((2 * a + 1 - c) * r[w], r[w]), :], dst_ref=outs[w].at[a],
                    send_sem=send_sems.at[w], recv_sem=recv_sems.at[w],
                    device_id=(x, y, 1 - c), device_id_type=MESH).start()

    def finish(ins, outs, sems):
        send_sems, recv_sems = sems
        x, y, c, _ = _place()
        for w in range(n):
            pltpu.make_async_remote_copy(
                src_ref=outs[w], dst_ref=outs[w], send_sem=send_sems.at[w], recv_sem=recv_sems.at[w],
                device_id=(x, y, c), device_id_type=MESH).wait()

    return _Comm(grads, [jax.ShapeDtypeStruct((N_CHIP, rw, D), BF) for rw in r],
                 [pltpu.SemaphoreType.DMA((n,)), pltpu.SemaphoreType.DMA((n,))],
                 [(0.0, start), (1.0, finish)])


def _pair_add(grad, got, core, *, name):
    r = got.shape[1]

    def body(c_ref, g_ref, got_ref, o_ref):
        o_ref[0] = (g_ref[...].astype(F32) + got_ref[0].astype(F32)).astype(BF)

    grid_spec = pltpu.PrefetchScalarGridSpec(
        num_scalar_prefetch=1, grid=(N_CHIP,),
        in_specs=[pl.BlockSpec((r, D), lambda a, c_ref: (2 * a + c_ref[0], 0)),
                  pl.BlockSpec((1, r, D), lambda a, c_ref: (a, 0, 0))],
        out_specs=pl.BlockSpec((1, r, D), lambda a, c_ref: (a, 0, 0)))
    return _pcall(body, name=name, grid_spec=grid_spec,
                  out_shape=jax.ShapeDtypeStruct((N_CHIP, r, D), BF),
                  compiler_params=_cp(("parallel",)))(core, grad, got)


def _chip_comm(pair_sums):
    n = len(pair_sums)
    r = [p.shape[1] for p in pair_sums]
    off = [sum(r[:w]) for w in range(n)]

    def tools(ins, outs, sems):
        send_sems, recv_sems, local_sems = sems
        x, y, c, chips = _place()
        my_chip = 2 * x + y

        def slot(w):
            return outs[0].at[my_chip, pl.ds(off[w], r[w]), :]

        own = [pltpu.make_async_copy(ins[w].at[my_chip], slot(w), local_sems.at[w]) for w in range(n)]
        return x, y, c, chips, my_chip, slot, own, send_sems, recv_sems

    def start(ins, outs, sems):
        x, y, c, chips, my_chip, slot, own, send_sems, recv_sems = tools(ins, outs, sems)
        for cp in own:
            cp.start()
        for j, chip in enumerate(chips):
            for w in range(n):
                pltpu.make_async_remote_copy(
                    src_ref=ins[w].at[2 * chip[0] + chip[1]], dst_ref=slot(w), send_sem=send_sems.at[j],
                    recv_sem=recv_sems.at[j], device_id=(*chip, c), device_id_type=MESH).start()

    def finish(ins, outs, sems):
        x, y, c, chips, my_chip, slot, own, send_sems, recv_sems = tools(ins, outs, sems)
        whole = outs[0].at[my_chip]
        for j in range(3):
            pltpu.make_async_remote_copy(
                src_ref=whole, dst_ref=whole, send_sem=send_sems.at[j], recv_sem=recv_sems.at[j],
                device_id=(x, y, c), device_id_type=MESH).wait()
        for cp in own:
            cp.wait()

    return _Comm(pair_sums, [jax.ShapeDtypeStruct((N_CHIP, sum(r), D), BF)],
                 [pltpu.SemaphoreType.DMA((3,)), pltpu.SemaphoreType.DMA((3,)), pltpu.SemaphoreType.DMA((n,))],
                 [(0.0, start), (1.0, finish)])


def _sum_chips(parts, *, tr, name):
    rows = parts.shape[1]

    def body(p_ref, o_ref):
        acc = p_ref[0].astype(F32)
        for a in range(1, N_CHIP):
            acc = acc + p_ref[a].astype(F32)
        o_ref[...] = acc

    return _pcall(body, name=name, grid=(rows // tr,),
                  in_specs=[pl.BlockSpec((N_CHIP, tr, D), lambda i: (0, i, 0))],
                  out_specs=_row_spec(tr, D), out_shape=jax.ShapeDtypeStruct((rows, D), F32),
                  compiler_params=_cp(("parallel",)))(parts)


def _adam_math(w, g, m, v):
    m = ADAM_B1 * m + (1.0 - ADAM_B1) * g
    v = ADAM_B2 * v + (1.0 - ADAM_B2) * (g * g)
    m_hat = m / (1.0 - ADAM_B1 ** ADAM_STEP)
    v_hat = v / (1.0 - ADAM_B2 ** ADAM_STEP)
    delta = -ADAM_LR * (m_hat / (jnp.sqrt(v_hat) + ADAM_EPS) + ADAM_WD * w)
    return delta, m, v


def _small_allreduce_adam(gpart, w, m, v):
    def body(g_ref, w_ref, m_ref, v_ref, gs_ref, d_ref, mo_ref, vo_ref, gath, send_sems, recv_sems):
        x, y, c, _ = _place()
        me = 4 * x + 2 * y + c
        gath[me] = g_ref[...]
        cps = []
        for d in range(1, N_DEV):
            peer = (x ^ (d >> 2), y ^ ((d >> 1) & 1), c ^ (d & 1))
            cps.append(pltpu.make_async_remote_copy(
                src_ref=g_ref, dst_ref=gath.at[me], send_sem=send_sems.at[d - 1],
                recv_sem=recv_sems.at[d - 1], device_id=peer, device_id_type=MESH))
        for cp in cps:
            cp.start()
        for cp in cps:
            cp.wait()
        g = gath[0]
        for k in range(1, N_DEV):
            g = g + gath[k]
        gs_ref[...] = g
        d_ref[...], mo_ref[...], vo_ref[...] = _adam_math(w_ref[...], g, m_ref[...], v_ref[...])

    shape = jax.ShapeDtypeStruct((SMALL_ROWS, D), F32)
    vm = pl.BlockSpec(memory_space=pltpu.VMEM)
    return _pcall(body, name="small_allreduce_adam", in_specs=[vm] * 4, out_specs=[vm] * 4,
                  out_shape=[shape] * 4,
                  scratch_shapes=[pltpu.VMEM((N_DEV, SMALL_ROWS, D), F32),
                                  pltpu.SemaphoreType.DMA((N_DEV - 1,)), pltpu.SemaphoreType.DMA((N_DEV - 1,))],
                  compiler_params=pltpu.CompilerParams(has_side_effects=True))(gpart, w, m, v)


def _adam(w, g, m, v, *, name):
    rows, cols = w.shape
    tr = rows if rows <= 512 else rows // 2

    def body(w_ref, g_ref, m_ref, v_ref, d_ref, mo_ref, vo_ref):
        d_ref[...], mo_ref[...], vo_ref[...] = _adam_math(w_ref[...], g_ref[...], m_ref[...], v_ref[...])

    spec = pl.BlockSpec((tr, cols), lambda i: (i, 0))
    return _pcall(body, name=name, grid=(rows // tr,), in_specs=[spec] * 4, out_specs=[spec] * 3,
                  out_shape=[jax.ShapeDtypeStruct((rows, cols), F32)] * 3,
                  compiler_params=_cp(("parallel",)))(w, g, m, v)


def _step(x, tgt, shards, norm_mix_g, b_in, sinks, logits, hgrn_norm_g, norm_ffn_g, norm_final_g):
    t = x.shape[0]
    big = dict(tm=1024, tn=1024, tk=4096)
    core = lax.axis_index("c").astype(jnp.int32).reshape(1)

    (win_t,) = _run_comm(_gather_comm(shards[0:1], 0.0), name="gather_w_in")
    u1 = _rms_fwd(x, norm_mix_g, tm=512, name="rms_mix")
    (q, kv, h4, gates), (wg_t, wba, wbh, wout) = _inproj_fwd(
        u1, win_t, b_in, t=t, comm=_gather_comm([shards[1]] + shards[4:7], 0.8))
    (y_attn,), _ = _attn_fwd(q, kv, sinks, t=t)
    (y_hgrn, o_pre, states), (wu_t, wd) = _hgrn_fwd(h4, logits, hgrn_norm_g, t=t,
                                                    comm=_gather_comm(shards[2:4], 0.8))
    col = lambda j: j
    first, second = (lambda j: 0), (lambda j: 1)
    gate_tiles = [(gates, D, first), (gates, D, second)]

    def merge(prods, ex):
        (ya_, yb_), (ga, gb) = prods, ex
        return ya_, yb_, _sig(ga) * ya_ + _sig(gb) * yb_

    ya, yb, merged = _fmm([y_attn, y_hgrn], [(0, wba, False), (1, wbh, False)], gate_tiles, merge,
                          [(BF, D, D, first)] * 3, m=t, n=D, tm=512, tn=D, name="branch_merge")
    h1 = _mm(merged, wout, m=t, n=D, k=D, resid=x, name="out_proj", **big)
    u2 = _rms_fwd(h1, norm_ffn_g, tm=512, name="rms_ffn")

    def swiglu(prods, ex):
        g_, u_ = prods
        return g_, u_, g_ * _sig(g_) * u_

    gt, up, z = _fmm([u2], [(0, wg_t, True), (0, wu_t, True)], [], swiglu, [(BF, FFN, FFN // 2, col)] * 3,
                     m=t, n=FFN, tm=512, tn=FFN // 2, name="ffn_gate_up")
    h2 = _mm(z, wd, m=t, n=D, k=FFN, resid=h1, name="ffn_down", **big)
    dh2, dh2_b, d_norm_final, loss_row = _loss_head(h2, tgt, norm_final_g, tm=512)

    def swiglu_bwd(prods, ex):
        (dz,), (g_, u_) = prods, ex
        g_ = g_.astype(F32)
        s = _sig(g_)
        return dz * u_.astype(F32) * s * (1.0 + g_ * (1.0 - s)), dz * g_ * s

    ffn_tiles = [(gt, FFN // 2, col), (up, FFN // 2, col)]
    dgt, dup = _fmm([dh2_b], [(0, wd, True)], ffn_tiles, swiglu_bwd, [(BF, FFN, FFN // 2, col)] * 2,
                    m=t, n=FFN, tm=512, tn=FFN // 2, name="d_gate_up")
    d_wd = _mm(z, dh2_b, m=FFN, n=D, k=t, ta=True, tm=256, tn=D, tk=4096, out_dtype=BF, name="d_w_down")
    (du2,) = _fmm([dgt, dup], [(0, wg_t, False), (1, wu_t, False)], [], lambda prods, ex: (prods[0] + prods[1],),
                  [(F32, D, 512, col)], m=t, n=D, tm=512, tn=512, name="d_u2")
    d_wg = _mm(dgt, u2, m=FFN, n=D, k=t, ta=True, tm=256, tn=D, tk=4096, out_dtype=BF, name="d_w_gate")
    d_wu = _mm(dup, u2, m=FFN, n=D, k=t, ta=True, tm=256, tn=D, tk=4096, out_dtype=BF, name="d_w_up")
    dh1, dh1_b, d_norm_ffn = _rms_bwd(du2, h1, norm_ffn_g, dh2, tm=512, name="rms_ffn_bwd")
    d_wout = _mm(merged, dh1_b, m=D, n=D, k=t, ta=True, tm=256, tn=D, tk=4096, out_dtype=BF, name="d_w_out")

    def merge_bwd(prods, ex):
        (dm,), (ga, gb, ya_, yb_) = prods, ex
        sa, sb = _sig(ga), _sig(gb)
        dgate = jnp.concatenate([dm * ya_.astype(F32) * sa * (1.0 - sa),
                                 dm * yb_.astype(F32) * sb * (1.0 - sb)], axis=1)
        return dm * sa, dm * sb, dgate

    ffn_grads = (d_wg, d_wu, d_wd)
    (dya, dyb, dgates), got = _fmm(
        [dh1_b], [(0, wout, True)], gate_tiles + [(ya, D, first), (yb, D, first)], merge_bwd,
        [(BF, D, D, first), (BF, D, D, first), (BF, 2 * D, 2 * D, first)],
        m=t, n=D, tm=512, tn=D, name="d_merge", comm=_pair_comm(ffn_grads))
    pair_ffn = [_pair_add(g, r, core, name="pair_add_ffn%d" % i) for i, (g, r) in enumerate(zip(ffn_grads, got))]
    dy_attn = _mm(dya, wba, m=t, n=D, k=D, tb=True, out_dtype=BF, name="d_y_attn", **big)
    dy_hgrn = _mm(dyb, wbh, m=t, n=D, k=D, tb=True, name="d_y_hgrn", **big)
    d_wba = _mm(y_attn, dya, m=D, n=D, k=t, ta=True, tm=256, tn=D, tk=4096, out_dtype=BF, name="d_w_ba")
    d_wbh = _mm(y_hgrn, dyb, m=D, n=D, k=t, ta=True, tm=256, tn=D, tk=4096, out_dtype=BF, name="d_w_bh")
    sq_grads = (d_wba, d_wbh, d_wout)
    (dq, dkv, d_sinks), (parts_ffn, *got) = _attn_bwd(
        q, kv, sinks, dy_attn, t=t, comm=_both(_chip_comm(pair_ffn), _pair_comm(sq_grads)))
    pair_sq = [_pair_add(g, r, core, name="pair_add_sq%d" % i) for i, (g, r) in enumerate(zip(sq_grads, got))]
    (dh4, d_logits, d_hgrn_norm), (parts_sq,) = _hgrn_bwd(h4, logits, hgrn_norm_g, o_pre, states, dy_hgrn,
                                                           t=t, comm=_chip_comm(pair_sq))
    g_ffn = _sum_chips(parts_ffn, tr=parts_ffn.shape[1] // 2, name="sum_chips_ffn")
    g_sq = _sum_chips(parts_sq, tr=parts_sq.shape[1] // 2, name="sum_chips_sq")
    dps = (dq, dkv, dh4, dgates)
    d_win_t, d_b_in = _inproj_bwd_w(dps, u1, t=t)
    half0, got_in = _inproj_bwd_x(dps, win_t, x, norm_mix_g, dh1, t=t, part=0, comm=_pair_comm([d_win_t]))
    pair_in = _pair_add(d_win_t, got_in[0], core, name="pair_add_w_in")
    (grad_x, d_norm_mix), (parts_in,) = _inproj_bwd_x(dps, win_t, x, norm_mix_g, dh1, t=t, part=1, prev=half0,
                                                      comm=_chip_comm([pair_in]))
    g_in = _sum_chips(parts_in, tr=parts_in.shape[1] // 2, name="sum_chips_w_in")

    small_grads = (d_norm_mix, d_b_in, d_sinks, d_logits, d_hgrn_norm, d_norm_ffn, d_norm_final)
    return loss_row, grad_x, (g_in, g_ffn, g_sq), small_grads


def _pack_small(norm_mix, b_in, sinks, logits, hgrn_norm, norm_ffn, norm_final, extra=None):
    pad = lambda a, n: jnp.pad(a.reshape(1, -1), ((0, 0), (0, n - a.size)))
    rows = [norm_mix.reshape(1, D), hgrn_norm.reshape(1, D), norm_ffn.reshape(1, D), norm_final.reshape(1, D),
            logits.reshape(2, D), pad(sinks.reshape(-1)[:16], D),
            jnp.zeros((1, D), F32) if extra is None else pad(extra, D),
            pad(b_in, 8 * D).reshape(8, D)]
    return jnp.concatenate(rows, axis=0).astype(F32)


def _unpack_small(p):
    return dict(norm_mix_g=p[0:1], hgrn_norm_g=p[1:2], norm_ffn_g=p[2:3], norm_final_g=p[3],
                hgrn_lb_logits=p[4:6], attn_sinks=p[6:7, 0:16], extra=p[7],
                b_in=p[8:16].reshape(1, 8 * D)[:, :IN_W])


def kernel(x, norm_mix_g, w_in, b_in, attn_sinks, hgrn_lb_logits, hgrn_norm_g, w_branch_attn, w_branch_hgrn, w_out, norm_ffn_g, w_ffn_gate, w_ffn_up, w_ffn_down, norm_final_g, loss_target, m_norm_mix_g, m_w_in, m_b_in, m_attn_sinks, m_hgrn_lb_logits, m_hgrn_norm_g, m_w_branch_attn, m_w_branch_hgrn, m_w_out, m_norm_ffn_g, m_w_ffn_gate, m_w_ffn_up, m_w_ffn_down, m_norm_final_g, v_norm_mix_g, v_w_in, v_b_in, v_attn_sinks, v_hgrn_lb_logits, v_hgrn_norm_g, v_w_branch_attn, v_w_branch_hgrn, v_w_out, v_norm_ffn_g, v_w_ffn_gate, v_w_ffn_up, v_w_ffn_down, v_norm_final_g):
    shards = [w_in[0].T.astype(BF), w_ffn_gate[0].T.astype(BF), w_ffn_up[0].T.astype(BF),
              w_ffn_down[0].astype(BF), w_branch_attn[0].astype(BF), w_branch_hgrn[0].astype(BF),
              w_out[0].astype(BF)]
    loss_row, grad_x, g_slabs, small_grads = _step(
        x[0], loss_target[0], shards, norm_mix_g, b_in, attn_sinks, hgrn_lb_logits, hgrn_norm_g,
        norm_ffn_g, norm_final_g.reshape(1, D))

    d_norm_mix, d_b_in, d_sinks, d_logits, d_hgrn_norm, d_norm_ffn, d_norm_final = small_grads
    g_small = _pack_small(d_norm_mix, d_b_in, d_sinks[:, :16], d_logits, d_hgrn_norm, d_norm_ffn,
                          d_norm_final, extra=loss_row[0, 0:1])
    w_small = _pack_small(norm_mix_g, b_in, attn_sinks, hgrn_lb_logits, hgrn_norm_g, norm_ffn_g, norm_final_g)
    m_small = _pack_small(m_norm_mix_g, m_b_in, m_attn_sinks, m_hgrn_lb_logits, m_hgrn_norm_g, m_norm_ffn_g,
                          m_norm_final_g)
    v_small = _pack_small(v_norm_mix_g, v_b_in, v_attn_sinks, v_hgrn_lb_logits, v_hgrn_norm_g, v_norm_ffn_g,
                          v_norm_final_g)
    small = [_unpack_small(p) for p in _small_allreduce_adam(g_small, w_small, m_small, v_small)]
    loss = small[0]["extra"][0]

    names = ["w_in", "w_ffn_gate", "w_ffn_up", "w_ffn_down", "w_branch_attn", "w_branch_hgrn", "w_out"]
    w_full = dict(w_in=(w_in, m_w_in, v_w_in), w_ffn_gate=(w_ffn_gate, m_w_ffn_gate, v_w_ffn_gate),
                  w_ffn_up=(w_ffn_up, m_w_ffn_up, v_w_ffn_up), w_ffn_down=(w_ffn_down, m_w_ffn_down, v_w_ffn_down),
                  w_branch_attn=(w_branch_attn, m_w_branch_attn, v_w_branch_attn),
                  w_branch_hgrn=(w_branch_hgrn, m_w_branch_hgrn, v_w_branch_hgrn),
                  w_out=(w_out, m_w_out, v_w_out))
    g_in, g_ffn, g_sq = g_slabs
    rf, rs = FFN // N_DEV, D // N_DEV
    grads = [g_in, g_ffn[0:rf], g_ffn[rf:2 * rf], g_ffn[2 * rf:3 * rf], g_sq[0:rs], g_sq[rs:2 * rs], g_sq[2 * rs:3 * rs]]
    big = {}
    for i, name in enumerate(names):
        view = (lambda a: a[0].T) if i < 3 else (lambda a: a[0])
        back = (lambda a: a.T[None]) if i < 3 else (lambda a: a[None])
        wv, mv, vv = w_full[name]
        delta, new_m, new_v = _adam(view(wv), grads[i], view(mv), view(vv), name="adam_" + name)
        big[name] = [back(a) for a in (grads[i], delta, new_m, new_v)]

    order = ["norm_mix_g", "w_in", "b_in", "attn_sinks", "hgrn_lb_logits", "hgrn_norm_g", "w_branch_attn",
             "w_branch_hgrn", "w_out", "norm_ffn_g", "w_ffn_gate", "w_ffn_up", "w_ffn_down", "norm_final_g"]
    outs = [loss, grad_x[None]]
    for kind in range(4):
        for name in order:
            outs.append(big[name][kind] if name in big else small[kind][name])
    return tuple(outs)
```

```python
import math

import jax
import jax.numpy as jnp
from jax import lax
from jax.experimental import pallas as pl
from jax.experimental.pallas import tpu as pltpu

F32 = jnp.float32
BF = jnp.bfloat16
MESH = pl.DeviceIdType.MESH

D = 1024
HEAD = 64
N_PAIR = 8
BLK = 128
CH = 64
HG_HEADS = 8
HG_K = 128
FFN = 2816
IN_W = 7424
N_DEV = 8
N_CHIP = 4
EPS = 1e-6
NEG = -1e30
SCALE = 1.0 / math.sqrt(HEAD)
VMEM_LIMIT = 56 * 1024 * 1024
WT = 256

ADAM_LR, ADAM_B1, ADAM_B2, ADAM_EPS, ADAM_WD, ADAM_STEP = 0.001, 0.9, 0.999, 1e-08, 0.01, 10

SLAB_R = (IN_W // N_DEV, FFN // N_DEV, FFN // N_DEV, FFN // N_DEV, D // N_DEV, D // N_DEV, D // N_DEV)
SLAB_ROWS = sum(SLAB_R)
SLAB_OFF = tuple(sum(SLAB_R[:i]) for i in range(len(SLAB_R)))
N_W = len(SLAB_R)
GRP_OFF = (0, D // WT, (D + 256) // WT, (5 * D + 256) // WT)
GRP_N = (D // WT, 256 // WT, 4 * D // WT, 2 * D // WT)
SMALL_ROWS = 16


_NN = (((1,), (0,)), ((), ()))
_NT = (((1,), (1,)), ((), ()))
_TN = (((0,), (0,)), ((), ()))


def _pcall(body, **kw):
    return pl.pallas_call(body, **kw)


def _cp(sem=None, **kw):
    return pltpu.CompilerParams(dimension_semantics=sem, vmem_limit_bytes=VMEM_LIMIT, **kw)


def _sig(v):
    return 0.5 * jnp.tanh(0.5 * v) + 0.5


def _accum(ref, val, first):
    @pl.when(first)
    def _():
        ref[...] = val

    @pl.when(jnp.logical_not(first))
    def _():
        ref[...] += val


class _Comm:
    def __init__(self, ins, out_shapes, sem_shapes, phases):
        self.ins, self.out_shapes, self.sem_shapes, self.phases = list(ins), list(out_shapes), list(sem_shapes), phases


def _both(a, b):
    ni, no, ns = len(a.ins), len(a.out_shapes), len(a.sem_shapes)

    def of_a(fn):
        return lambda ins, outs, sems: fn(ins[:ni], outs[:no], sems[:ns])

    def of_b(fn):
        return lambda ins, outs, sems: fn(ins[ni:], outs[no:], sems[ns:])

    return _Comm(a.ins + b.ins, a.out_shapes + b.out_shapes, a.sem_shapes + b.sem_shapes,
                 [(f, of_a(fn)) for f, fn in a.phases] + [(f, of_b(fn)) for f, fn in b.phases])


def _host(body, comm, n_in, n_out, n_scr, nsteps, step_fn):
    if comm is None:
        return body
    ci, co = len(comm.ins), len(comm.out_shapes)

    def wrapped(*refs):
        p = 0
        ins, p = refs[p:p + n_in], p + n_in
        cins, p = refs[p:p + ci], p + ci
        outs, p = refs[p:p + n_out], p + n_out
        couts, p = refs[p:p + co], p + co
        scr, p = refs[p:p + n_scr], p + n_scr
        csems = refs[p:]
        step = step_fn()
        for frac, fn in comm.phases:
            if frac < 1.0:
                @pl.when(step == int(round(frac * (nsteps - 1))))
                def _(fn=fn):
                    fn(cins, couts, csems)
        body(*ins, *outs, *scr)
        for frac, fn in comm.phases:
            if frac >= 1.0:
                @pl.when(step == nsteps - 1)
                def _(fn=fn):
                    fn(cins, couts, csems)

    return wrapped


def _hosted_call(body, comm, args, *, name, grid, in_specs, out_specs, out_shape, scratch_shapes, sem,
                 nsteps, step_fn, aliases=None):
    n_in, n_out, n_scr = len(in_specs), len(out_specs), len(scratch_shapes)
    args = list(args)
    extra = {}
    if comm is not None:
        in_specs = list(in_specs) + [_hbm_spec()] * len(comm.ins)
        out_specs = list(out_specs) + [_hbm_spec()] * len(comm.out_shapes)
        out_shape = list(out_shape) + comm.out_shapes
        scratch_shapes = list(scratch_shapes) + comm.sem_shapes
        args += comm.ins
        extra = dict(has_side_effects=True)
    outs = _pcall(_host(body, comm, n_in, n_out, n_scr, nsteps, step_fn), name=name, grid=grid,
                  in_specs=in_specs, out_specs=out_specs, out_shape=out_shape, scratch_shapes=scratch_shapes,
                  input_output_aliases=aliases or {}, compiler_params=_cp(sem, **extra))(*args)
    return list(outs[:n_out]), list(outs[n_out:])


def _run_comm(comm, *, name):
    ci, co = len(comm.ins), len(comm.out_shapes)

    def body(*refs):
        for _, fn in comm.phases:
            fn(refs[:ci], refs[ci:ci + co], refs[ci + co:])

    return _pcall(body, name=name, in_specs=[_hbm_spec()] * ci, out_specs=[_hbm_spec()] * co,
                  out_shape=comm.out_shapes, scratch_shapes=comm.sem_shapes,
                  compiler_params=pltpu.CompilerParams(has_side_effects=True))(*comm.ins)


def _hbm_spec():
    return pl.BlockSpec(memory_space=pl.ANY)


def _mm(a, b, *, m, n, k, tm, tn, tk, ta=False, tb=False, out_dtype=F32, resid=None, name):
    tm, tn, tk = min(tm, m), min(tn, n), min(tk, k)
    gm, gn, gk = m // tm, n // tn, k // tk
    assert gm * tm == m and gn * tn == n and gk * tk == k, (name, m, n, k, tm, tn, tk)
    a_spec = (pl.BlockSpec((tk, tm), lambda i, j, l: (l, i)) if ta
              else pl.BlockSpec((tm, tk), lambda i, j, l: (i, l)))
    b_spec = (pl.BlockSpec((tn, tk), lambda i, j, l: (j, l)) if tb
              else pl.BlockSpec((tk, tn), lambda i, j, l: (l, j)))
    dims = (((0 if ta else 1,), (1 if tb else 0,)), ((), ()))
    ins, in_specs = [a, b], [a_spec, b_spec]
    if resid is not None:
        ins.append(resid)
        in_specs.append(pl.BlockSpec((tm, tn), lambda i, j, l: (i, j)))
    scratch = [pltpu.VMEM((tm, tn), F32)] if gk > 1 else []

    def body(*refs):
        it = iter(refs)
        a_ref, b_ref = next(it), next(it)
        resid_ref = next(it) if resid is not None else None
        o_ref = next(it)
        acc_ref = next(it) if gk > 1 else None
        l = pl.program_id(2)
        part = lax.dot_general(a_ref[...].astype(BF), b_ref[...].astype(BF), dims,
                               preferred_element_type=F32)

        def finish(acc):
            if resid_ref is not None:
                acc = acc + resid_ref[...].astype(F32)
            o_ref[...] = acc.astype(out_dtype)

        if gk == 1:
            finish(part)
        else:
            _accum(acc_ref, part, l == 0)

            @pl.when(l == gk - 1)
            def _():
                finish(acc_ref[...])

    return _pcall(body, name=name, grid=(gm, gn, gk), in_specs=in_specs,
                  out_specs=pl.BlockSpec((tm, tn), lambda i, j, l: (i, j)),
                  out_shape=jax.ShapeDtypeStruct((m, n), out_dtype), scratch_shapes=scratch,
                  compiler_params=_cp(("parallel", "parallel", "arbitrary")))(*ins)


def _fmm(lhs, rhs, extras, epilogue, outs, *, m, n, tm, tn, name, comm=None):
    tm, tn = min(tm, m), min(tn, n)
    assert m % tm == 0 and n % tn == 0, (name, m, n, tm, tn)
    in_specs, args = [], []
    for a in lhs:
        in_specs.append(pl.BlockSpec((tm, a.shape[1]), lambda i, j: (i, 0)))
        args.append(a)
    for li, b, tb in rhs:
        k = lhs[li].shape[1]
        in_specs.append(pl.BlockSpec((tn, k), lambda i, j: (j, 0)) if tb
                        else pl.BlockSpec((k, tn), lambda i, j: (0, j)))
        args.append(b)
    for arr, w, col in extras:
        in_specs.append(pl.BlockSpec((tm, w), lambda i, j, col=col: (i, col(j))))
        args.append(arr)
    out_specs = [pl.BlockSpec((tm, w), lambda i, j, col=col: (i, col(j))) for _, _, w, col in outs]
    out_shape = [jax.ShapeDtypeStruct((m, total), dt) for dt, total, _, _ in outs]
    nl, nr, ne = len(lhs), len(rhs), len(extras)

    def body(*refs):
        prods = []
        for r, (li, _, tb) in enumerate(rhs):
            prods.append(lax.dot_general(refs[li][...], refs[nl + r][...], _NT if tb else _NN,
                                         preferred_element_type=F32))
        vals = epilogue(prods, [ref[...] for ref in refs[nl + nr:nl + nr + ne]])
        for o_ref, v in zip(refs[nl + nr + ne:], vals):
            o_ref[...] = v.astype(o_ref.dtype)

    gm, gn = m // tm, n // tn
    res, comm_res = _hosted_call(
        body, comm, args, name=name, grid=(gm, gn), in_specs=in_specs, out_specs=out_specs,
        out_shape=out_shape, scratch_shapes=[], sem=("arbitrary", "arbitrary"), nsteps=gm * gn,
        step_fn=lambda: pl.program_id(0) * gn + pl.program_id(1))
    return res if comm is None else (res, comm_res)


def _grp_of(i):
    return [jnp.logical_and(i >= GRP_OFF[g], i < GRP_OFF[g] + GRP_N[g]) for g in range(4)]


def _grp_idx(i, g):
    return jnp.clip(i - GRP_OFF[g], 0, GRP_N[g] - 1)


def _inproj_fwd(u, win_t, b_in, *, t, comm=None):
    n_tiles = IN_W // WT
    dims = (((1,), (1,)), ((), ()))
    dtypes = (BF, BF, F32, F32)

    def body(u_ref, w_ref, b_ref, *o_refs):
        i = pl.program_id(0)
        p = lax.dot_general(u_ref[...], w_ref[...], dims, preferred_element_type=F32) + b_ref[...]
        for g, pred in enumerate(_grp_of(i)):
            @pl.when(pred)
            def _(g=g):
                o_refs[g][...] = p.astype(dtypes[g])

    return _hosted_call(
        body, comm, (u, win_t, b_in), name="inproj_fwd", grid=(n_tiles,),
        in_specs=[pl.BlockSpec((t, D), lambda i: (0, 0)),
                  pl.BlockSpec((WT, D), lambda i: (i, 0)),
                  pl.BlockSpec((1, WT), lambda i: (0, i))],
        out_specs=[pl.BlockSpec((t, WT), lambda i, g=g: (0, _grp_idx(i, g))) for g in range(4)],
        out_shape=[jax.ShapeDtypeStruct((t, GRP_N[g] * WT), dtypes[g]) for g in range(4)],
        scratch_shapes=[], sem=("arbitrary",), nsteps=n_tiles, step_fn=lambda: pl.program_id(0))


def _inproj_bwd_x(dps, win_t, x, g, resid, *, t, part, prev=None, comm=None):
    n_tiles = IN_W // WT
    per = 2 if t >= 2048 else 1
    tm = t // (2 * per)
    row = lambda i: part * per + i

    n_chunks = 8
    h_first, g_first = 2, 6
    sub = D // WT

    def w_block(l):
        return jnp.where(l == 0, GRP_OFF[0], jnp.where(l == 1, GRP_OFF[1], GRP_OFF[2] + sub * (l - h_first)))

    def body(d0, d1, d2, d3, w0, w1, w2, w3, x_ref, g_ref, r_ref, *rest):
        dg_prev = rest[0] if prev is not None else None
        o_ref, dg_ref, acc_ref = rest[-3], rest[-2], rest[-1]
        i, l = pl.program_id(0), pl.program_id(1)

        @pl.when(l == 1)
        def _():
            acc_ref[...] += jnp.dot(d1[...], w0[...], preferred_element_type=F32)

        w = jnp.concatenate([w0[...], w1[...], w2[...], w3[...]], axis=0)
        for pred, d_ref in ((l == 0, d0), (jnp.logical_and(l >= h_first, l < g_first), d2), (l >= g_first, d3)):
            @pl.when(pred)
            def _(d_ref=d_ref):
                _accum(acc_ref, jnp.dot(d_ref[...], w, preferred_element_type=F32), l == 0)

        @pl.when(l == n_chunks - 1)
        def _():
            xv = x_ref[...]
            r = lax.rsqrt(jnp.mean(xv * xv, axis=-1, keepdims=True) + EPS)
            xh = xv * r
            du = acc_ref[...]
            dxh = du * g_ref[...]
            o_ref[...] = r_ref[...] + r * (dxh - xh * jnp.mean(dxh * xh, axis=-1, keepdims=True))
            dg = jnp.sum(du * xh, axis=0, keepdims=True)
            if dg_prev is not None:
                dg = dg + jnp.where(i == 0, 1.0, 0.0) * dg_prev[...]
            _accum(dg_ref, dg, i == 0)

    rows = lambda w: pl.BlockSpec((tm, w), lambda i, l: (row(i), 0))
    in_specs = ([rows(D), rows(256),
                 pl.BlockSpec((tm, D), lambda i, l: (row(i), jnp.clip(l - h_first, 0, 3))),
                 pl.BlockSpec((tm, D), lambda i, l: (row(i), jnp.clip(l - g_first, 0, 1)))]
                + [pl.BlockSpec((WT, D), lambda i, l, o=o: (w_block(l) + o, 0)) for o in range(sub)]
                + [rows(D), pl.BlockSpec((1, D), lambda i, l: (0, 0)), rows(D)])
    args = list(dps) + [win_t] * sub + [x, g, resid]
    aliases = None
    if prev is not None:
        in_specs += [pl.BlockSpec((1, D), lambda i, l: (0, 0)), _hbm_spec()]
        args += [prev[1], prev[0]]
        aliases = {len(args) - 1: 0}
    return _hosted_call(
        body, comm, args, name="inproj_bwd_x%d" % part, grid=(per, n_chunks), in_specs=in_specs,
        out_specs=[rows(D), pl.BlockSpec((1, D), lambda i, l: (0, 0))],
        out_shape=[jax.ShapeDtypeStruct((t, D), F32), jax.ShapeDtypeStruct((1, D), F32)],
        scratch_shapes=[pltpu.VMEM((tm, D), F32)], sem=("arbitrary", "arbitrary"), nsteps=per * n_chunks,
        step_fn=lambda: pl.program_id(0) * n_chunks + pl.program_id(1), aliases=aliases)


def _inproj_bwd_w(dps, u, *, t):
    n_tiles = IN_W // WT
    dims = (((0,), (0,)), ((), ()))

    def body(d0, d1, d2, d3, u_ref, o_ref, db_ref):
        i = pl.program_id(0)
        uv = u_ref[...]
        for g, (pred, d_ref) in enumerate(zip(_grp_of(i), (d0, d1, d2, d3))):
            @pl.when(pred)
            def _(d_ref=d_ref):
                dv = d_ref[...]
                o_ref[...] = lax.dot_general(dv, uv, dims, preferred_element_type=F32).astype(BF)
                db_ref[...] = jnp.sum(dv.astype(F32), axis=0, keepdims=True)

    return _pcall(body, name="inproj_bwd_w", grid=(n_tiles,),
                  in_specs=[pl.BlockSpec((t, WT), lambda i, g=g: (0, _grp_idx(i, g))) for g in range(4)]
                  + [pl.BlockSpec((t, D), lambda i: (0, 0))],
                  out_specs=[pl.BlockSpec((WT, D), lambda i: (i, 0)),
                             pl.BlockSpec((1, WT), lambda i: (0, i))],
                  out_shape=[jax.ShapeDtypeStruct((IN_W, D), BF), jax.ShapeDtypeStruct((1, IN_W), F32)],
                  compiler_params=_cp(("arbitrary",)))(*dps, u)


def _row_spec(tm, width, col=0):
    return pl.BlockSpec((tm, width), lambda i: (i, col))


def _vec_spec(width):
    return pl.BlockSpec((1, width), lambda i: (0, 0))


def _rms_fwd(x, g, *, tm, name):
    t = x.shape[0]
    tm = min(tm, t)

    def body(x_ref, g_ref, u_ref):
        xv = x_ref[...]
        r = lax.rsqrt(jnp.mean(xv * xv, axis=-1, keepdims=True) + EPS)
        u_ref[...] = (xv * r * g_ref[...]).astype(BF)

    return _pcall(body, name=name, grid=(t // tm,), in_specs=[_row_spec(tm, D), _vec_spec(D)],
                  out_specs=_row_spec(tm, D), out_shape=jax.ShapeDtypeStruct((t, D), BF),
                  compiler_params=_cp(("parallel",)))(x, g)


def _rms_bwd(du, x, g, resid, *, tm, name):
    t = x.shape[0]
    tm = min(tm, t)

    def body(du_ref, x_ref, g_ref, r_ref, dx_ref, dxb_ref, dg_ref):
        xv = x_ref[...]
        r = lax.rsqrt(jnp.mean(xv * xv, axis=-1, keepdims=True) + EPS)
        xh = xv * r
        duv = du_ref[...]
        dxh = duv * g_ref[...]
        dx = r_ref[...] + r * (dxh - xh * jnp.mean(dxh * xh, axis=-1, keepdims=True))
        dx_ref[...] = dx
        dxb_ref[...] = dx.astype(BF)
        _accum(dg_ref, jnp.sum(duv * xh, axis=0, keepdims=True), pl.program_id(0) == 0)

    return _pcall(body, name=name, grid=(t // tm,),
                  in_specs=[_row_spec(tm, D), _row_spec(tm, D), _vec_spec(D), _row_spec(tm, D)],
                  out_specs=[_row_spec(tm, D), _row_spec(tm, D), _vec_spec(D)],
                  out_shape=[jax.ShapeDtypeStruct((t, D), F32), jax.ShapeDtypeStruct((t, D), BF),
                             jax.ShapeDtypeStruct((1, D), F32)],
                  compiler_params=_cp(("arbitrary",)))(du, x, g, resid)


def _loss_head(h2, tgt, g, *, tm):
    t = h2.shape[0]
    tm = min(tm, t)

    def body(h_ref, t_ref, g_ref, dh_ref, dhb_ref, dg_ref, loss_ref):
        hv = h_ref[...]
        gv = g_ref[...]
        r = lax.rsqrt(jnp.mean(hv * hv, axis=-1, keepdims=True) + EPS)
        xh = hv * r
        err = xh * gv - t_ref[...]
        lp = jnp.sum(jnp.sum(err * err, axis=1, keepdims=True), axis=0, keepdims=True) * (0.5 / D)
        dy = err * (1.0 / D)
        dxh = dy * gv
        dh = r * (dxh - xh * jnp.mean(dxh * xh, axis=-1, keepdims=True))
        dh_ref[...] = dh
        dhb_ref[...] = dh.astype(BF)
        first = pl.program_id(0) == 0
        _accum(dg_ref, jnp.sum(dy * xh, axis=0, keepdims=True), first)
        _accum(loss_ref, jnp.broadcast_to(lp, (1, 128)), first)

    return _pcall(body, name="loss_head", grid=(t // tm,),
                  in_specs=[_row_spec(tm, D), _row_spec(tm, D), _vec_spec(D)],
                  out_specs=[_row_spec(tm, D), _row_spec(tm, D), _vec_spec(D), _vec_spec(128)],
                  out_shape=[jax.ShapeDtypeStruct((t, D), F32), jax.ShapeDtypeStruct((t, D), BF),
                             jax.ShapeDtypeStruct((1, D), F32), jax.ShapeDtypeStruct((1, 128), F32)],
                  compiler_params=_cp(("arbitrary",)))(h2, tgt, g)


def _attn_kv_tiles(kprev, kcur):
    kv = jnp.concatenate([kprev, kcur], axis=0).astype(F32)
    lo = lax.broadcasted_iota(jnp.int32, (2 * BLK, 128), 1) < HEAD
    tiles = []
    for part in (kv[:, 0:128], kv[:, 128:256]):
        rolled = pltpu.roll(part, HEAD, 1)
        z = jnp.zeros_like(part)
        tiles.append(((jnp.where(lo, part, z).astype(BF), jnp.where(lo, z, rolled).astype(BF)),
                      (jnp.where(lo, rolled, z).astype(BF), jnp.where(lo, z, part).astype(BF))))
    k_t, v_t = tiles
    return [(jnp.concatenate(k_t[h], axis=0), jnp.concatenate(v_t[h], axis=0)) for h in range(2)]


def _attn_mask(i):
    qi = lax.broadcasted_iota(jnp.int32, (BLK, 2 * BLK), 0)
    kj = lax.broadcasted_iota(jnp.int32, (BLK, 2 * BLK), 1)
    first_key = jnp.where(i == 0, BLK, 0)
    in_prev = jnp.logical_and(jnp.logical_and(kj < BLK, kj > qi), kj >= first_key)
    in_cur = jnp.logical_and(kj >= BLK, kj - BLK <= qi)
    return jnp.logical_or(in_prev, in_cur)


def _attn_probs(s, sink, valid):
    s = jnp.where(valid, s * SCALE, NEG)
    mx = jnp.maximum(jnp.max(s, axis=-1, keepdims=True), sink)
    e = jnp.exp(s - mx)
    es = jnp.exp(sink - mx)
    inv = 1.0 / (jnp.sum(e, axis=-1, keepdims=True) + es)
    return e * inv, es * inv


_KEYS = 2 * BLK


def _pair(ref, j):
    return ref[:, j * 128:(j + 1) * 128]


def _attn_fwd(q, kv, sinks, *, t, comm=None):
    nb = t // BLK

    def body(sink_ref, q_ref, kp_ref, kc_ref, o_ref):
        valid = _attn_mask(pl.program_id(0))
        tiles = _attn_kv_tiles(kp_ref[...], kc_ref[...])
        s = [lax.dot_general(_pair(q_ref, j), tiles[j // 4][0], _NT, preferred_element_type=F32)
             for j in range(N_PAIR)]
        p = []
        for j in range(N_PAIR):
            pe, _ = _attn_probs(s[j][:, 0:_KEYS], sink_ref[0, 2 * j], valid)
            po, _ = _attn_probs(s[j][:, _KEYS:2 * _KEYS], sink_ref[0, 2 * j + 1], valid)
            p.append(jnp.concatenate([pe.astype(BF), po.astype(BF)], axis=1))
        for j in range(N_PAIR):
            o_ref[:, j * 128:(j + 1) * 128] = jnp.dot(p[j], tiles[j // 4][1],
                                                      preferred_element_type=F32).astype(BF)

    return _hosted_call(
        body, comm, (sinks, q, kv, kv), name="attn_fwd", grid=(nb,),
        in_specs=[pl.BlockSpec(memory_space=pltpu.SMEM),
                  pl.BlockSpec((BLK, D), lambda i: (i, 0)),
                  pl.BlockSpec((BLK, 256), lambda i: (jnp.maximum(i - 1, 0), 0)),
                  pl.BlockSpec((BLK, 256), lambda i: (i, 0))],
        out_specs=[pl.BlockSpec((BLK, D), lambda i: (i, 0))],
        out_shape=[jax.ShapeDtypeStruct((t, D), BF)],
        scratch_shapes=[], sem=("arbitrary",), nsteps=nb, step_fn=lambda: pl.program_id(0))


def _attn_bwd(q, kv, sinks, do, *, t, comm=None):
    nb = t // BLK
    last = nb - 1

    def body(sink_ref, q_ref, kp_ref, kc_ref, do_ref, dq_ref, dkv_ref, ds_ref, carry_ref):
        i = pl.program_id(0)

        @pl.when(i == 0)
        def _():
            ds_ref[...] = jnp.zeros_like(ds_ref)
            carry_ref[...] = jnp.zeros_like(carry_ref)

        @pl.when(i < nb)
        def _():
            valid = _attn_mask(i)
            tiles = _attn_kv_tiles(kp_ref[...], kc_ref[...])
            lane1 = lax.broadcasted_iota(jnp.int32, (1, 128), 1)
            dsink = jnp.zeros((1, 128), F32)
            s = [lax.dot_general(_pair(q_ref, j), tiles[j // 4][0], _NT, preferred_element_type=F32)
                 for j in range(N_PAIR)]
            dp = [lax.dot_general(_pair(do_ref, j), tiles[j // 4][1], _NT, preferred_element_type=F32)
                  for j in range(N_PAIR)]
            p_all, ds_all = [], []
            for j in range(N_PAIR):
                halves = []
                for par in range(2):
                    cols = slice(par * _KEYS, (par + 1) * _KEYS)
                    p, ps = _attn_probs(s[j][:, cols], sink_ref[0, 2 * j + par], valid)
                    dpj = dp[j][:, cols]
                    dd = jnp.sum(p * dpj, axis=-1, keepdims=True)
                    dsink = dsink + jnp.where(lane1 == 2 * j + par,
                                              -jnp.sum(ps * dd, axis=0, keepdims=True), 0.0)
                    halves.append((p.astype(BF), (p * (dpj - dd)).astype(BF)))
                p_all.append(jnp.concatenate([halves[0][0], halves[1][0]], axis=1))
                ds_all.append(jnp.concatenate([halves[0][1], halves[1][1]], axis=1))
            for j in range(N_PAIR):
                dq_ref[:, j * 128:(j + 1) * 128] = (
                    jnp.dot(ds_all[j], tiles[j // 4][0], preferred_element_type=F32) * SCALE).astype(BF)
            ds_ref[...] += dsink
            gk, gv = [], []
            for h in range(2):
                grp = range(4 * h, 4 * h + 4)
                q_rows = jnp.concatenate([_pair(q_ref, j) for j in grp], axis=0)
                do_rows = jnp.concatenate([_pair(do_ref, j) for j in grp], axis=0)
                g_k = lax.dot_general(jnp.concatenate([ds_all[j] for j in grp], axis=0), q_rows, _TN,
                                      preferred_element_type=F32)
                g_v = lax.dot_general(jnp.concatenate([p_all[j] for j in grp], axis=0), do_rows, _TN,
                                      preferred_element_type=F32)
                gk.append((g_k[0:_KEYS], g_k[_KEYS:2 * _KEYS]))
                gv.append((g_v[0:_KEYS], g_v[_KEYS:2 * _KEYS]))
            lo = lax.broadcasted_iota(jnp.int32, (2 * BLK, 128), 1) < HEAD
            zero = jnp.zeros((2 * BLK, 128), F32)

            def unpad(g):
                return (jnp.where(lo, g[0][0] + pltpu.roll(g[0][1], HEAD, 1), zero)
                        + jnp.where(lo, zero, pltpu.roll(g[1][0], HEAD, 1) + g[1][1]))

            dk = unpad(gk) * SCALE
            dv = unpad(gv)
            dkv_ref[:, 0:128] = (carry_ref[:, 0:128] + dk[0:BLK]).astype(BF)
            dkv_ref[:, 128:256] = (carry_ref[:, 128:256] + dv[0:BLK]).astype(BF)
            carry_ref[:, 0:128] = dk[BLK:2 * BLK]
            carry_ref[:, 128:256] = dv[BLK:2 * BLK]

        @pl.when(i == nb)
        def _():
            dkv_ref[...] = carry_ref[...].astype(BF)

    return _hosted_call(
        body, comm, (sinks, q, kv, kv, do), name="attn_bwd", grid=(nb + 1,),
        in_specs=[pl.BlockSpec(memory_space=pltpu.SMEM),
                  pl.BlockSpec((BLK, D), lambda i: (jnp.minimum(i, last), 0)),
                  pl.BlockSpec((BLK, 256), lambda i: (jnp.clip(i - 1, 0, last), 0)),
                  pl.BlockSpec((BLK, 256), lambda i: (jnp.minimum(i, last), 0)),
                  pl.BlockSpec((BLK, D), lambda i: (jnp.minimum(i, last), 0))],
        out_specs=[pl.BlockSpec((BLK, D), lambda i: (jnp.minimum(i, last), 0)),
                   pl.BlockSpec((BLK, 256), lambda i: (jnp.maximum(i - 1, 0), 0)),
                   pl.BlockSpec((1, 128), lambda i: (0, 0))],
        out_shape=[jax.ShapeDtypeStruct((t, D), BF), jax.ShapeDtypeStruct((t, 256), BF),
                   jax.ShapeDtypeStruct((1, 128), F32)],
        scratch_shapes=[pltpu.VMEM((BLK, 256), F32)], sem=("arbitrary",), nsteps=nb + 1,
        step_fn=lambda: pl.program_id(0))


def _split3(v):
    h = v.astype(BF)
    r = v - h.astype(F32)
    m = r.astype(BF)
    lo = (r - m.astype(F32)).astype(BF)
    return jnp.concatenate([h, m, lo], axis=1)


def _apply01(mat, v):
    n = v.shape[1]
    r = jnp.dot(mat, _split3(v), preferred_element_type=F32)
    return r[:, 0:n] + r[:, n:2 * n] + r[:, 2 * n:3 * n]


def _hgrn_gates(hq, hf, lb):
    sq = _sig(hq)
    sg = _sig(hf)
    f = lb + (1.0 - lb) * sg
    return hq * sq, (1.0 - lb) * (1.0 - sg), jnp.log(f), sq, sg, f


def _tri(upper):
    r = lax.broadcasted_iota(jnp.int32, (CH, CH), 0)
    c = lax.broadcasted_iota(jnp.int32, (CH, CH), 1)
    return (c >= r) if upper else (c <= r)


def _lb_from_logits(lg_ref):
    return 1.0 / (1.0 + jnp.exp(lg_ref[1:2, :] - lg_ref[0:1, :]))


def _hgrn_fwd(h4, logits, norm_g, *, t, comm=None):
    nc = t // CH
    nt_dims = (((1,), (1,)), ((), ()))
    tn_dims = (((0,), (0,)), ((), ()))

    def body(h_ref, lg_ref, ng_ref, y_ref, o_ref, st_ref, s_scr, b_scr, qa_s, ka_s, qb_s, kb_s, v_s):
        @pl.when(pl.program_id(0) == 0)
        def _():
            s_scr[...] = jnp.zeros_like(s_scr)

        heads = [slice(h * HG_K, (h + 1) * HG_K) for h in range(HG_HEADS)]
        causal = _tri(False)
        q, k, g, _, _, _ = _hgrn_gates(h_ref[:, 0:D], h_ref[:, D:2 * D], _lb_from_logits(lg_ref))
        b_scr[...] = _apply01(jnp.where(causal, 1.0, 0.0).astype(BF), g)
        b = b_scr[...]
        b_mid = b_scr[CH // 2 - 1:CH // 2, :]
        b_last = b_scr[CH - 1:CH, :]
        qa_s[...] = (q * jnp.exp(b - b_mid)).astype(BF)
        ka_s[...] = (k * jnp.exp(b_mid - b)).astype(BF)
        qb_s[...] = (q * jnp.exp(b)).astype(BF)
        kb_s[...] = (k * jnp.exp(b_last - b)).astype(BF)
        v_s[...] = h_ref[:, 2 * D:3 * D].astype(BF)
        dec = jnp.exp(b_last)
        st_ref[0] = s_scr[...]
        a = [jnp.where(causal, lax.dot_general(qa_s[:, sl], ka_s[:, sl], nt_dims, preferred_element_type=F32),
                       0.0).astype(BF) for sl in heads]
        for h, sl in enumerate(heads):
            o_ref[:, sl] = (jnp.dot(a[h], v_s[:, sl], preferred_element_type=F32)
                            + lax.dot_general(qb_s[:, sl], s_scr[h].astype(BF), nt_dims,
                                              preferred_element_type=F32))
        for h, sl in enumerate(heads):
            s_scr[h] = dec[:, sl] * s_scr[h] + lax.dot_general(v_s[:, sl], kb_s[:, sl], tn_dims,
                                                               preferred_element_type=F32)
        for h, sl in enumerate(heads):
            o = o_ref[:, sl]
            on = o * lax.rsqrt(jnp.mean(o * o, axis=-1, keepdims=True) + EPS)
            y_ref[:, sl] = (on * ng_ref[:, sl] * _sig(h_ref[:, 3 * D + h * HG_K:3 * D + (h + 1) * HG_K])).astype(BF)

    half = lambda: pltpu.VMEM((CH, D), BF)
    return _hosted_call(
        body, comm, (h4, logits, norm_g), name="hgrn_fwd", grid=(nc,),
        in_specs=[pl.BlockSpec((CH, 4 * D), lambda n: (n, 0)),
                  pl.BlockSpec((2, D), lambda n: (0, 0)),
                  pl.BlockSpec((1, D), lambda n: (0, 0))],
        out_specs=[pl.BlockSpec((CH, D), lambda n: (n, 0)),
                   pl.BlockSpec((CH, D), lambda n: (n, 0)),
                   pl.BlockSpec((1, HG_HEADS, HG_K, HG_K), lambda n: (n, 0, 0, 0))],
        out_shape=[jax.ShapeDtypeStruct((t, D), BF), jax.ShapeDtypeStruct((t, D), F32),
                   jax.ShapeDtypeStruct((nc, HG_HEADS, HG_K, HG_K), F32)],
        scratch_shapes=[pltpu.VMEM((HG_HEADS, HG_K, HG_K), F32), pltpu.VMEM((CH, D), F32),
                        half(), half(), half(), half(), half()],
        sem=("arbitrary",), nsteps=nc, step_fn=lambda: pl.program_id(0))


def _hgrn_bwd(h4, logits, norm_g, o_pre, states, dy, *, t, comm=None):
    nc = t // CH
    nt_dims = (((1,), (1,)), ((), ()))
    tn_dims = (((0,), (0,)), ((), ()))

    def body(h_ref, lg_ref, ng_ref, o_ref, st_ref, dy_ref, dh_ref, dlg_ref, dng_ref, ds_scr, dlb_scr,
             b_scr, tail_s, e_qa, e_ka, e_qb, e_kb, q_s, k_s, dqa_s, dka_s, dqb_s, dkb_s,
             qa_s, ka_s, qb_s, kb_s, v_s, do_s):
        n = pl.program_id(0)

        @pl.when(n == 0)
        def _():
            ds_scr[...] = jnp.zeros_like(ds_scr)
            dlb_scr[...] = jnp.zeros_like(dlb_scr)
            dng_ref[...] = jnp.zeros_like(dng_ref)

        heads = [slice(h * HG_K, (h + 1) * HG_K) for h in range(HG_HEADS)]
        lb = _lb_from_logits(lg_ref)
        causal = _tri(False)
        q, k, g, _, _, _ = _hgrn_gates(h_ref[:, 0:D], h_ref[:, D:2 * D], lb)
        b_scr[...] = _apply01(jnp.where(causal, 1.0, 0.0).astype(BF), g)
        b = b_scr[...]
        b_mid = b_scr[CH // 2 - 1:CH // 2, :]
        b_last = b_scr[CH - 1:CH, :]
        q_s[...] = q
        k_s[...] = k
        for e_ref, s_ref, base, expo in ((e_qa, qa_s, q, b - b_mid), (e_ka, ka_s, k, b_mid - b),
                                         (e_qb, qb_s, q, b), (e_kb, kb_s, k, b_last - b)):
            e = jnp.exp(expo)
            e_ref[...] = e
            s_ref[...] = (base * e).astype(BF)
        v_s[...] = h_ref[:, 2 * D:3 * D].astype(BF)
        dec = jnp.exp(b_last)
        for h, sl in enumerate(heads):
            gcol = slice(3 * D + h * HG_K, 3 * D + (h + 1) * HG_K)
            ngh = ng_ref[:, sl]
            sgate = _sig(h_ref[:, gcol])
            o = o_ref[:, sl]
            r = lax.rsqrt(jnp.mean(o * o, axis=-1, keepdims=True) + EPS)
            on = o * r
            dyh = dy_ref[:, sl]
            dh_ref[:, gcol] = (dyh * on * ngh * sgate * (1.0 - sgate)).astype(BF)
            dng_ref[:, sl] += jnp.sum(dyh * on * sgate, axis=0, keepdims=True)
            don = dyh * ngh * sgate
            do_s[:, sl] = (r * (don - on * jnp.mean(don * on, axis=-1, keepdims=True))).astype(BF)
        a = [jnp.where(causal, lax.dot_general(qa_s[:, sl], ka_s[:, sl], nt_dims, preferred_element_type=F32),
                       0.0).astype(BF) for sl in heads]
        da = [jnp.where(causal, lax.dot_general(do_s[:, sl], v_s[:, sl], nt_dims, preferred_element_type=F32),
                        0.0).astype(BF) for sl in heads]
        for h, sl in enumerate(heads):
            dh_ref[:, 2 * D + h * HG_K:2 * D + (h + 1) * HG_K] = (
                lax.dot_general(a[h], do_s[:, sl], tn_dims, preferred_element_type=F32)
                + lax.dot_general(kb_s[:, sl], ds_scr[h].astype(BF), nt_dims, preferred_element_type=F32)
            ).astype(BF)
        for h, sl in enumerate(heads):
            dqa_s[:, sl] = jnp.dot(da[h], ka_s[:, sl], preferred_element_type=F32)
        for h, sl in enumerate(heads):
            dka_s[:, sl] = lax.dot_general(da[h], qa_s[:, sl], tn_dims, preferred_element_type=F32)
        for h, sl in enumerate(heads):
            dqb_s[:, sl] = jnp.dot(do_s[:, sl], st_ref[0, h].astype(BF), preferred_element_type=F32)
        for h, sl in enumerate(heads):
            dkb_s[:, sl] = jnp.dot(v_s[:, sl], ds_scr[h].astype(BF), preferred_element_type=F32)
        for h, sl in enumerate(heads):
            tail_s[:, sl] = jnp.sum(dec[:, sl] * st_ref[0, h] * ds_scr[h], axis=0, keepdims=True)
        for h, sl in enumerate(heads):
            ds_scr[h] = (lax.dot_general(do_s[:, sl], qb_s[:, sl], tn_dims, preferred_element_type=F32)
                         + dec[:, sl] * ds_scr[h])
        qv, kv = q_s[...], k_s[...]
        dqa, dka, dqb, dkb = dqa_s[...], dka_s[...], dqb_s[...], dkb_s[...]
        eqa, eka, eqb, ekb = e_qa[...], e_ka[...], e_qb[...], e_kb[...]
        dkb_kb = dkb * (kv * ekb)
        db_last = jnp.sum(dkb_kb, axis=0, keepdims=True) + tail_s[...]
        last_row = lax.broadcasted_iota(jnp.int32, (CH, D), 0) == CH - 1
        db = (dqa * (qv * eqa) - dka * (kv * eka) + dqb * (qv * eqb) - dkb_kb
              + jnp.where(last_row, db_last, 0.0))
        dg = _apply01(jnp.where(_tri(True), 1.0, 0.0).astype(BF), db)
        dq = dqa * eqa + dqb * eqb
        dk = dka * eka + dkb * ekb
        hq = h_ref[:, 0:D]
        _, _, _, sq, sg, f = _hgrn_gates(hq, h_ref[:, D:2 * D], lb)
        dh_ref[:, 0:D] = (dq * sq * (1.0 + hq * (1.0 - sq))).astype(BF)
        dfk = dg / f - dk
        dh_ref[:, D:2 * D] = ((1.0 - lb) * dfk * sg * (1.0 - sg)).astype(BF)
        dlb_scr[...] += jnp.sum((1.0 - sg) * dfk, axis=0, keepdims=True)

        @pl.when(n == nc - 1)
        def _():
            dl0 = dlb_scr[...] * lb * (1.0 - lb)
            dlg_ref[0:1, :] = dl0
            dlg_ref[1:2, :] = -dl0

    rev = lambda n: (nc - 1 - n, 0)
    return _hosted_call(
        body, comm, (h4, logits, norm_g, o_pre, states, dy), name="hgrn_bwd", grid=(nc,),
        in_specs=[pl.BlockSpec((CH, 4 * D), rev),
                  pl.BlockSpec((2, D), lambda n: (0, 0)),
                  pl.BlockSpec((1, D), lambda n: (0, 0)),
                  pl.BlockSpec((CH, D), rev),
                  pl.BlockSpec((1, HG_HEADS, HG_K, HG_K), lambda n: (nc - 1 - n, 0, 0, 0)),
                  pl.BlockSpec((CH, D), rev)],
        out_specs=[pl.BlockSpec((CH, 4 * D), rev),
                   pl.BlockSpec((2, D), lambda n: (0, 0)),
                   pl.BlockSpec((1, D), lambda n: (0, 0))],
        out_shape=[jax.ShapeDtypeStruct((t, 4 * D), BF), jax.ShapeDtypeStruct((2, D), F32),
                   jax.ShapeDtypeStruct((1, D), F32)],
        scratch_shapes=([pltpu.VMEM((HG_HEADS, HG_K, HG_K), F32), pltpu.VMEM((1, D), F32),
                         pltpu.VMEM((CH, D), F32), pltpu.VMEM((1, D), F32)]
                        + [pltpu.VMEM((CH, D), F32)] * 10 + [pltpu.VMEM((CH, D), BF)] * 6),
        sem=("arbitrary",), nsteps=nc, step_fn=lambda: pl.program_id(0))


def _place():
    x, y, c = lax.axis_index("x"), lax.axis_index("y"), lax.axis_index("c")
    return x, y, c, [(1 - x, y), (x, 1 - y), (1 - x, 1 - y)]


def _gather_comm(shards, mid):
    n = len(shards)
    r = [s.shape[0] for s in shards]

    def tools(ins, outs, sems):
        send_sems, recv_sems, local_sems = sems
        x, y, c, chips = _place()
        me, sib = (x, y, c), (x, y, 1 - c)

        def rows(w, dev):
            return outs[w].at[pl.ds((4 * dev[0] + 2 * dev[1] + dev[2]) * r[w], r[w]), :]

        def copy(kind, w, block, to, src=None):
            return pltpu.make_async_remote_copy(
                src_ref=rows(w, block) if src is None else src, dst_ref=rows(w, block),
                send_sem=send_sems.at[kind], recv_sem=recv_sems.at[kind], device_id=to, device_id_type=MESH)

        def all_of(kind):
            whole = outs[0].at[pl.ds(0, sum(r)), :]
            return pltpu.make_async_remote_copy(
                src_ref=whole, dst_ref=whole, send_sem=send_sems.at[kind], recv_sem=recv_sems.at[kind],
                device_id=me, device_id_type=MESH)

        mine = [pltpu.make_async_copy(ins[w], rows(w, me), local_sems.at[w]) for w in range(n)]
        return c, chips, me, sib, copy, all_of, mine

    def start(ins, outs, sems):
        c, chips, me, sib, copy, _, mine = tools(ins, outs, sems)
        for cp in mine:
            cp.start()
        for w in range(n):
            copy(0, w, me, sib, src=ins[w]).start()
            for j, chip in enumerate(chips):
                copy(1 + j, w, me, (*chip, c), src=ins[w]).start()

    def pass_on(ins, outs, sems):
        c, chips, _, sib, copy, all_of, _ = tools(ins, outs, sems)
        for j, chip in enumerate(chips):
            all_of(1 + j).wait_recv()
            for w in range(n):
                copy(4 + j, w, (*chip, c), sib).start()

    def finish(ins, outs, sems):
        _, _, _, _, _, all_of, mine = tools(ins, outs, sems)
        all_of(0).wait_recv()
        for j in range(3):
            all_of(4 + j).wait_recv()
        for kind in range(7):
            all_of(kind).wait_send()
        for cp in mine:
            cp.wait()

    return _Comm(shards, [jax.ShapeDtypeStruct((N_DEV * rw, D), BF) for rw in r],
                 [pltpu.SemaphoreType.DMA((7,)), pltpu.SemaphoreType.DMA((7,)), pltpu.SemaphoreType.DMA((n,))],
                 [(0.0, start), (mid, pass_on), (1.0, finish)])


def _pair_comm(grads):
    n = len(grads)
    r = [g.shape[0] // N_DEV for g in grads]

    def start(ins, outs, sems):
        send_sems, recv_sems = sems
        x, y, c, _ = _place()
        for w in range(n):
            for a in range(N_CHIP):
                pltpu.make_async_remote_copy(
                    src_ref=ins[w].at[pl.ds((2 * a + 1 - c) * r[w], r[w]), :], dst_ref=outs[w].at[a],
                    send_sem=send_sems.at[w], recv_sem=recv_sems.at[w],
                    device_id=(x, y, 1 - c), device_id_type=MESH).start()

    def finish(ins, outs, sems):
        send_sems, recv_sems = sems
        x, y, c, _ = _place()
        for w in range(n):
            pltpu.make_async_remote_copy(
                src_ref=outs[w], dst_ref=outs[w], send_sem=send_sems.at[w], recv_sem=recv_sems.at[w],
                device_id=(x, y, c), device_id_type=MESH).wait()

    return _Comm(grads, [jax.ShapeDtypeStruct((N_CHIP, rw, D), BF) for rw in r],
                 [pltpu.SemaphoreType.DMA((n,)), pltpu.SemaphoreType.DMA((n,))],
                 [(0.0, start), (1.0, finish)])


def _pair_add(grad, got, core, *, name):
    r = got.shape[1]

    def body(c_ref, g_ref, got_ref, o_ref):
        o_ref[0] = (g_ref[...].astype(F32) + got_ref[0].astype(F32)).astype(BF)

    grid_spec = pltpu.PrefetchScalarGridSpec(
        num_scalar_prefetch=1, grid=(N_CHIP,),
        in_specs=[pl.BlockSpec((r, D), lambda a, c_ref: (2 * a + c_ref[0], 0)),
                  pl.BlockSpec((1, r, D), lambda a, c_ref: (a, 0, 0))],
        out_specs=pl.BlockSpec((1, r, D), lambda a, c_ref: (a, 0, 0)))
    return _pcall(body, name=name, grid_spec=grid_spec,
                  out_shape=jax.ShapeDtypeStruct((N_CHIP, r, D), BF),
                  compiler_params=_cp(("parallel",)))(core, grad, got)


def _chip_comm(pair_sums):
    n = len(pair_sums)
    r = [p.shape[1] for p in pair_sums]
    off = [sum(r[:w]) for w in range(n)]

    def tools(ins, outs, sems):
        send_sems, recv_sems, local_sems = sems
        x, y, c, chips = _place()
        my_chip = 2 * x + y

        def slot(w):
            return outs[0].at[my_chip, pl.ds(off[w], r[w]), :]

        own = [pltpu.make_async_copy(ins[w].at[my_chip], slot(w), local_sems.at[w]) for w in range(n)]
        return x, y, c, chips, my_chip, slot, own, send_sems, recv_sems

    def start(ins, outs, sems):
        x, y, c, chips, my_chip, slot, own, send_sems, recv_sems = tools(ins, outs, sems)
        for cp in own:
            cp.start()
        for j, chip in enumerate(chips):
            for w in range(n):
                pltpu.make_async_remote_copy(
                    src_ref=ins[w].at[2 * chip[0] + chip[1]], dst_ref=slot(w), send_sem=send_sems.at[j],
                    recv_sem=recv_sems.at[j], device_id=(*chip, c), device_id_type=MESH).start()

    def finish(ins, outs, sems):
        x, y, c, chips, my_chip, slot, own, send_sems, recv_sems = tools(ins, outs, sems)
        whole = outs[0].at[my_chip]
        for j in range(3):
            pltpu.make_async_remote_copy(
                src_ref=whole, dst_ref=whole, send_sem=send_sems.at[j], recv_sem=recv_sems.at[j],
                device_id=(x, y, c), device_id_type=MESH).wait()
        for cp in own:
            cp.wait()

    return _Comm(pair_sums, [jax.ShapeDtypeStruct((N_CHIP, sum(r), D), BF)],
                 [pltpu.SemaphoreType.DMA((3,)), pltpu.SemaphoreType.DMA((3,)), pltpu.SemaphoreType.DMA((n,))],
                 [(0.0, start), (1.0, finish)])


def _sum_chips(parts, *, tr, name):
    rows = parts.shape[1]

    def body(p_ref, o_ref):
        acc = p_ref[0].astype(F32)
        for a in range(1, N_CHIP):
            acc = acc + p_ref[a].astype(F32)
        o_ref[...] = acc

    return _pcall(body, name=name, grid=(rows // tr,),
                  in_specs=[pl.BlockSpec((N_CHIP, tr, D), lambda i: (0, i, 0))],
                  out_specs=_row_spec(tr, D), out_shape=jax.ShapeDtypeStruct((rows, D), F32),
                  compiler_params=_cp(("parallel",)))(parts)


def _adam_math(w, g, m, v):
    m = ADAM_B1 * m + (1.0 - ADAM_B1) * g
    v = ADAM_B2 * v + (1.0 - ADAM_B2) * (g * g)
    m_hat = m / (1.0 - ADAM_B1 ** ADAM_STEP)
    v_hat = v / (1.0 - ADAM_B2 ** ADAM_STEP)
    delta = -ADAM_LR * (m_hat / (jnp.sqrt(v_hat) + ADAM_EPS) + ADAM_WD * w)
    return delta, m, v


def _small_allreduce_adam(gpart, w, m, v):
    def body(g_ref, w_ref, m_ref, v_ref, gs_ref, d_ref, mo_ref, vo_ref, gath, send_sems, recv_sems):
        x, y, c, _ = _place()
        me = 4 * x + 2 * y + c
        gath[me] = g_ref[...]
        cps = []
        for d in range(1, N_DEV):
            peer = (x ^ (d >> 2), y ^ ((d >> 1) & 1), c ^ (d & 1))
            cps.append(pltpu.make_async_remote_copy(
                src_ref=g_ref, dst_ref=gath.at[me], send_sem=send_sems.at[d - 1],
                recv_sem=recv_sems.at[d - 1], device_id=peer, device_id_type=MESH))
        for cp in cps:
            cp.start()
        for cp in cps:
            cp.wait()
        g = gath[0]
        for k in range(1, N_DEV):
            g = g + gath[k]
        gs_ref[...] = g
        d_ref[...], mo_ref[...], vo_ref[...] = _adam_math(w_ref[...], g, m_ref[...], v_ref[...])

    shape = jax.ShapeDtypeStruct((SMALL_ROWS, D), F32)
    vm = pl.BlockSpec(memory_space=pltpu.VMEM)
    return _pcall(body, name="small_allreduce_adam", in_specs=[vm] * 4, out_specs=[vm] * 4,
                  out_shape=[shape] * 4,
                  scratch_shapes=[pltpu.VMEM((N_DEV, SMALL_ROWS, D), F32),
                                  pltpu.SemaphoreType.DMA((N_DEV - 1,)), pltpu.SemaphoreType.DMA((N_DEV - 1,))],
                  compiler_params=pltpu.CompilerParams(has_side_effects=True))(gpart, w, m, v)


def _adam(w, g, m, v, *, name):
    rows, cols = w.shape
    tr = rows if rows <= 512 else rows // 2

    def body(w_ref, g_ref, m_ref, v_ref, d_ref, mo_ref, vo_ref):
        d_ref[...], mo_ref[...], vo_ref[...] = _adam_math(w_ref[...], g_ref[...], m_ref[...], v_ref[...])

    spec = pl.BlockSpec((tr, cols), lambda i: (i, 0))
    return _pcall(body, name=name, grid=(rows // tr,), in_specs=[spec] * 4, out_specs=[spec] * 3,
                  out_shape=[jax.ShapeDtypeStruct((rows, cols), F32)] * 3,
                  compiler_params=_cp(("parallel",)))(w, g, m, v)


def _step(x, tgt, shards, norm_mix_g, b_in, sinks, logits, hgrn_norm_g, norm_ffn_g, norm_final_g):
    t = x.shape[0]
    big = dict(tm=1024, tn=1024, tk=4096)
    core = lax.axis_index("c").astype(jnp.int32).reshape(1)

    (win_t,) = _run_comm(_gather_comm(shards[0:1], 0.0), name="gather_w_in")
    u1 = _rms_fwd(x, norm_mix_g, tm=512, name="rms_mix")
    (q, kv, h4, gates), (wg_t, wba, wbh, wout) = _inproj_fwd(
        u1, win_t, b_in, t=t, comm=_gather_comm([shards[1]] + shards[4:7], 0.8))
    (y_attn,), _ = _attn_fwd(q, kv, sinks, t=t)
    (y_hgrn, o_pre, states), (wu_t, wd) = _hgrn_fwd(h4, logits, hgrn_norm_g, t=t,
                                                    comm=_gather_comm(shards[2:4], 0.8))
    col = lambda j: j
    first, second = (lambda j: 0), (lambda j: 1)
    gate_tiles = [(gates, D, first), (gates, D, second)]

    def merge(prods, ex):
        (ya_, yb_), (ga, gb) = prods, ex
        return ya_, yb_, _sig(ga) * ya_ + _sig(gb) * yb_

    ya, yb, merged = _fmm([y_attn, y_hgrn], [(0, wba, False), (1, wbh, False)], gate_tiles, merge,
                          [(BF, D, D, first)] * 3, m=t, n=D, tm=512, tn=D, name="branch_merge")
    h1 = _mm(merged, wout, m=t, n=D, k=D, resid=x, name="out_proj", **big)
    u2 = _rms_fwd(h1, norm_ffn_g, tm=512, name="rms_ffn")

    def swiglu(prods, ex):
        g_, u_ = prods
        return g_, u_, g_ * _sig(g_) * u_

    gt, up, z = _fmm([u2], [(0, wg_t, True), (0, wu_t, True)], [], swiglu, [(BF, FFN, FFN // 2, col)] * 3,
                     m=t, n=FFN, tm=512, tn=FFN // 2, name="ffn_gate_up")
    h2 = _mm(z, wd, m=t, n=D, k=FFN, resid=h1, name="ffn_down", **big)
    dh2, dh2_b, d_norm_final, loss_row = _loss_head(h2, tgt, norm_final_g, tm=512)

    def swiglu_bwd(prods, ex):
        (dz,), (g_, u_) = prods, ex
        g_ = g_.astype(F32)
        s = _sig(g_)
        return dz * u_.astype(F32) * s * (1.0 + g_ * (1.0 - s)), dz * g_ * s

    ffn_tiles = [(gt, FFN // 2, col), (up, FFN // 2, col)]
    dgt, dup = _fmm([dh2_b], [(0, wd, True)], ffn_tiles, swiglu_bwd, [(BF, FFN, FFN // 2, col)] * 2,
                    m=t, n=FFN, tm=512, tn=FFN // 2, name="d_gate_up")
    d_wd = _mm(z, dh2_b, m=FFN, n=D, k=t, ta=True, tm=256, tn=D, tk=4096, out_dtype=BF, name="d_w_down")
    (du2,) = _fmm([dgt, dup], [(0, wg_t, False), (1, wu_t, False)], [], lambda prods, ex: (prods[0] + prods[1],),
                  [(F32, D, 512, col)], m=t, n=D, tm=512, tn=512, name="d_u2")
    d_wg = _mm(dgt, u2, m=FFN, n=D, k=t, ta=True, tm=256, tn=D, tk=4096, out_dtype=BF, name="d_w_gate")
    d_wu = _mm(dup, u2, m=FFN, n=D, k=t, ta=True, tm=256, tn=D, tk=4096, out_dtype=BF, name="d_w_up")
    dh1, dh1_b, d_norm_ffn = _rms_bwd(du2, h1, norm_ffn_g, dh2, tm=512, name="rms_ffn_bwd")
    d_wout = _mm(merged, dh1_b, m=D, n=D, k=t, ta=True, tm=256, tn=D, tk=4096, out_dtype=BF, name="d_w_out")

    def merge_bwd(prods, ex):
        (dm,), (ga, gb, ya_, yb_) = prods, ex
        sa, sb = _sig(ga), _sig(gb)
        dgate = jnp.concatenate([dm * ya_.astype(F32) * sa * (1.0 - sa),
                                 dm * yb_.astype(F32) * sb * (1.0 - sb)], axis=1)
        return dm * sa, dm * sb, dgate

    ffn_grads = (d_wg, d_wu, d_wd)
    (dya, dyb, dgates), got = _fmm(
        [dh1_b], [(0, wout, True)], gate_tiles + [(ya, D, first), (yb, D, first)], merge_bwd,
        [(BF, D, D, first), (BF, D, D, first), (BF, 2 * D, 2 * D, first)],
        m=t, n=D, tm=512, tn=D, name="d_merge", comm=_pair_comm(ffn_grads))
    pair_ffn = [_pair_add(g, r, core, name="pair_add_ffn%d" % i) for i, (g, r) in enumerate(zip(ffn_grads, got))]
    dy_attn = _mm(dya, wba, m=t, n=D, k=D, tb=True, out_dtype=BF, name="d_y_attn", **big)
    dy_hgrn = _mm(dyb, wbh, m=t, n=D, k=D, tb=True, name="d_y_hgrn", **big)
    d_wba = _mm(y_attn, dya, m=D, n=D, k=t, ta=True, tm=256, tn=D, tk=4096, out_dtype=BF, name="d_w_ba")
    d_wbh = _mm(y_hgrn, dyb, m=D, n=D, k=t, ta=True, tm=256, tn=D, tk=4096, out_dtype=BF, name="d_w_bh")
    sq_grads = (d_wba, d_wbh, d_wout)
    (dq, dkv, d_sinks), (parts_ffn, *got) = _attn_bwd(
        q, kv, sinks, dy_attn, t=t, comm=_both(_chip_comm(pair_ffn), _pair_comm(sq_grads)))
    pair_sq = [_pair_add(g, r, core, name="pair_add_sq%d" % i) for i, (g, r) in enumerate(zip(sq_grads, got))]
    (dh4, d_logits, d_hgrn_norm), (parts_sq,) = _hgrn_bwd(h4, logits, hgrn_norm_g, o_pre, states, dy_hgrn,
                                                           t=t, comm=_chip_comm(pair_sq))
    g_ffn = _sum_chips(parts_ffn, tr=parts_ffn.shape[1] // 2, name="sum_chips_ffn")
    g_sq = _sum_chips(parts_sq, tr=parts_sq.shape[1] // 2, name="sum_chips_sq")
    dps = (dq, dkv, dh4, dgates)
    d_win_t, d_b_in = _inproj_bwd_w(dps, u1, t=t)
    half0, got_in = _inproj_bwd_x(dps, win_t, x, norm_mix_g, dh1, t=t, part=0, comm=_pair_comm([d_win_t]))
    pair_in = _pair_add(d_win_t, got_in[0], core, name="pair_add_w_in")
    (grad_x, d_norm_mix), (parts_in,) = _inproj_bwd_x(dps, win_t, x, norm_mix_g, dh1, t=t, part=1, prev=half0,
                                                      comm=_chip_comm([pair_in]))
    g_in = _sum_chips(parts_in, tr=parts_in.shape[1] // 2, name="sum_chips_w_in")

    small_grads = (d_norm_mix, d_b_in, d_sinks, d_logits, d_hgrn_norm, d_norm_ffn, d_norm_final)
    return loss_row, grad_x, (g_in, g_ffn, g_sq), small_grads


def _pack_small(norm_mix, b_in, sinks, logits, hgrn_norm, norm_ffn, norm_final, extra=None):
    pad = lambda a, n: jnp.pad(a.reshape(1, -1), ((0, 0), (0, n - a.size)))
    rows = [norm_mix.reshape(1, D), hgrn_norm.reshape(1, D), norm_ffn.reshape(1, D), norm_final.reshape(1, D),
            logits.reshape(2, D), pad(sinks.reshape(-1)[:16], D),
            jnp.zeros((1, D), F32) if extra is None else pad(extra, D),
            pad(b_in, 8 * D).reshape(8, D)]
    return jnp.concatenate(rows, axis=0).astype(F32)


def _unpack_small(p):
    return dict(norm_mix_g=p[0:1], hgrn_norm_g=p[1:2], norm_ffn_g=p[2:3], norm_final_g=p[3],
                hgrn_lb_logits=p[4:6], attn_sinks=p[6:7, 0:16], extra=p[7],
                b_in=p[8:16].reshape(1, 8 * D)[:, :IN_W])


def kernel(x, norm_mix_g, w_in, b_in, attn_sinks, hgrn_lb_logits, hgrn_norm_g, w_branch_attn, w_branch_hgrn, w_out, norm_ffn_g, w_ffn_gate, w_ffn_up, w_ffn_down, norm_final_g, loss_target, m_norm_mix_g, m_w_in, m_b_in, m_attn_sinks, m_hgrn_lb_logits, m_hgrn_norm_g, m_w_branch_attn, m_w_branch_hgrn, m_w_out, m_norm_ffn_g, m_w_ffn_gate, m_w_ffn_up, m_w_ffn_down, m_norm_final_g, v_norm_mix_g, v_w_in, v_b_in, v_attn_sinks, v_hgrn_lb_logits, v_hgrn_norm_g, v_w_branch_attn, v_w_branch_hgrn, v_w_out, v_norm_ffn_g, v_w_ffn_gate, v_w_ffn_up, v_w_ffn_down, v_norm_final_g):
    shards = [w_in[0].T.astype(BF), w_ffn_gate[0].T.astype(BF), w_ffn_up[0].T.astype(BF),
              w_ffn_down[0].astype(BF), w_branch_attn[0].astype(BF), w_branch_hgrn[0].astype(BF),
              w_out[0].astype(BF)]
    loss_row, grad_x, g_slabs, small_grads = _step(
        x[0], loss_target[0], shards, norm_mix_g, b_in, attn_sinks, hgrn_lb_logits, hgrn_norm_g,
        norm_ffn_g, norm_final_g.reshape(1, D))

    d_norm_mix, d_b_in, d_sinks, d_logits, d_hgrn_norm, d_norm_ffn, d_norm_final = small_grads
    g_small = _pack_small(d_norm_mix, d_b_in, d_sinks[:, :16], d_logits, d_hgrn_norm, d_norm_ffn,
                          d_norm_final, extra=loss_row[0, 0:1])
    w_small = _pack_small(norm_mix_g, b_in, attn_sinks, hgrn_lb_logits, hgrn_norm_g, norm_ffn_g, norm_final_g)
    m_small = _pack_small(m_norm_mix_g, m_b_in, m_attn_sinks, m_hgrn_lb_logits, m_hgrn_norm_g, m_norm_ffn_g,
                          m_norm_final_g)
    v_small = _pack_small(v_norm_mix_g, v_b_in, v_attn_sinks, v_hgrn_lb_logits, v_hgrn_norm_g, v_norm_ffn_g,
                          v_norm_final_g)
    small = [_unpack_small(p) for p in _small_allreduce_adam(g_small, w_small, m_small, v_small)]
    loss = small[0]["extra"][0]

    names = ["w_in", "w_ffn_gate", "w_ffn_up", "w_ffn_down", "w_branch_attn", "w_branch_hgrn", "w_out"]
    w_full = dict(w_in=(w_in, m_w_in, v_w_in), w_ffn_gate=(w_ffn_gate, m_w_ffn_gate, v_w_ffn_gate),
                  w_ffn_up=(w_ffn_up, m_w_ffn_up, v_w_ffn_up), w_ffn_down=(w_ffn_down, m_w_ffn_down, v_w_ffn_down),
                  w_branch_attn=(w_branch_attn, m_w_branch_attn, v_w_branch_attn),
                  w_branch_hgrn=(w_branch_hgrn, m_w_branch_hgrn, v_w_branch_hgrn),
                  w_out=(w_out, m_w_out, v_w_out))
    g_in, g_ffn, g_sq = g_slabs
    rf, rs = FFN // N_DEV, D // N_DEV
    grads = [g_in, g_ffn[0:rf], g_ffn[rf:2 * rf], g_ffn[2 * rf:3 * rf], g_sq[0:rs], g_sq[rs:2 * rs], g_sq[2 * rs:3 * rs]]
    big = {}
    for i, name in enumerate(names):
        view = (lambda a: a[0].T) if i < 3 else (lambda a: a[0])
        back = (lambda a: a.T[None]) if i < 3 else (lambda a: a[None])
        wv, mv, vv = w_full[name]
        delta, new_m, new_v = _adam(view(wv), grads[i], view(mv), view(vv), name="adam_" + name)
        big[name] = [back(a) for a in (grads[i], delta, new_m, new_v)]

    order = ["norm_mix_g", "w_in", "b_in", "attn_sinks", "hgrn_lb_logits", "hgrn_norm_g", "w_branch_attn",
             "w_branch_hgrn", "w_out", "norm_ffn_g", "w_ffn_gate", "w_ffn_up", "w_ffn_down", "norm_final_g"]
    outs = [loss, grad_x[None]]
    for kind in range(4):
        for name in order:
            outs.append(big[name][kind] if name in big else small[kind][name])
    return tuple(outs)
```

```python
import math

import jax
import jax.numpy as jnp
from jax import lax
from jax.experimental import pallas as pl
from jax.experimental.pallas import tpu as pltpu

F32 = jnp.float32
BF = jnp.bfloat16
MESH = pl.DeviceIdType.MESH

D = 1024
HEAD = 64
N_PAIR = 8
BLK = 128
CH = 64
HG_HEADS = 8
HG_K = 128
FFN = 2816
IN_W = 7424
N_DEV = 8
N_CHIP = 4
EPS = 1e-6
NEG = -1e30
SCALE = 1.0 / math.sqrt(HEAD)
VMEM_LIMIT = 56 * 1024 * 1024
WT = 256

ADAM_LR, ADAM_B1, ADAM_B2, ADAM_EPS, ADAM_WD, ADAM_STEP = 0.001, 0.9, 0.999, 1e-08, 0.01, 10

SLAB_R = (IN_W // N_DEV, FFN // N_DEV, FFN // N_DEV, FFN // N_DEV, D // N_DEV, D // N_DEV, D // N_DEV)
SLAB_ROWS = sum(SLAB_R)
SLAB_OFF = tuple(sum(SLAB_R[:i]) for i in range(len(SLAB_R)))
N_W = len(SLAB_R)
GRP_OFF = (0, D // WT, (D + 256) // WT, (5 * D + 256) // WT)
GRP_N = (D // WT, 256 // WT, 4 * D // WT, 2 * D // WT)
SMALL_ROWS = 16


_NN = (((1,), (0,)), ((), ()))
_NT = (((1,), (1,)), ((), ()))
_TN = (((0,), (0,)), ((), ()))


def _pcall(body, **kw):
    return pl.pallas_call(body, **kw)


def _cp(sem=None, **kw):
    return pltpu.CompilerParams(dimension_semantics=sem, vmem_limit_bytes=VMEM_LIMIT, **kw)


def _sig(v):
    return 0.5 * jnp.tanh(0.5 * v) + 0.5


def _accum(ref, val, first):
    @pl.when(first)
    def _():
        ref[...] = val

    @pl.when(jnp.logical_not(first))
    def _():
        ref[...] += val


class _Comm:
    def __init__(self, ins, out_shapes, sem_shapes, phases):
        self.ins, self.out_shapes, self.sem_shapes, self.phases = list(ins), list(out_shapes), list(sem_shapes), phases


def _both(a, b):
    ni, no, ns = len(a.ins), len(a.out_shapes), len(a.sem_shapes)

    def of_a(fn):
        return lambda ins, outs, sems: fn(ins[:ni], outs[:no], sems[:ns])

    def of_b(fn):
        return lambda ins, outs, sems: fn(ins[ni:], outs[no:], sems[ns:])

    return _Comm(a.ins + b.ins, a.out_shapes + b.out_shapes, a.sem_shapes + b.sem_shapes,
                 [(f, of_a(fn)) for f, fn in a.phases] + [(f, of_b(fn)) for f, fn in b.phases])


def _host(body, comm, n_in, n_out, n_scr, nsteps, step_fn):
    if comm is None:
        return body
    ci, co = len(comm.ins), len(comm.out_shapes)

    def wrapped(*refs):
        p = 0
        ins, p = refs[p:p + n_in], p + n_in
        cins, p = refs[p:p + ci], p + ci
        outs, p = refs[p:p + n_out], p + n_out
        couts, p = refs[p:p + co], p + co
        scr, p = refs[p:p + n_scr], p + n_scr
        csems = refs[p:]
        step = step_fn()
        for frac, fn in comm.phases:
            if frac < 1.0:
                @pl.when(step == int(round(frac * (nsteps - 1))))
                def _(fn=fn):
                    fn(cins, couts, csems)
        body(*ins, *outs, *scr)
        for frac, fn in comm.phases:
            if frac >= 1.0:
                @pl.when(step == nsteps - 1)
                def _(fn=fn):
                    fn(cins, couts, csems)

    return wrapped


def _hosted_call(body, comm, args, *, name, grid, in_specs, out_specs, out_shape, scratch_shapes, sem,
                 nsteps, step_fn, aliases=None):
    n_in, n_out, n_scr = len(in_specs), len(out_specs), len(scratch_shapes)
    args = list(args)
    extra = {}
    if comm is not None:
        in_specs = list(in_specs) + [_hbm_spec()] * len(comm.ins)
        out_specs = list(out_specs) + [_hbm_spec()] * len(comm.out_shapes)
        out_shape = list(out_shape) + comm.out_shapes
        scratch_shapes = list(scratch_shapes) + comm.sem_shapes
        args += comm.ins
        extra = dict(has_side_effects=True)
    outs = _pcall(_host(body, comm, n_in, n_out, n_scr, nsteps, step_fn), name=name, grid=grid,
                  in_specs=in_specs, out_specs=out_specs, out_shape=out_shape, scratch_shapes=scratch_shapes,
                  input_output_aliases=aliases or {}, compiler_params=_cp(sem, **extra))(*args)
    return list(outs[:n_out]), list(outs[n_out:])


def _hbm_spec():
    return pl.BlockSpec(memory_space=pl.ANY)


def _mm(a, b, *, m, n, k, tm, tn, tk, ta=False, tb=False, out_dtype=F32, resid=None, name):
    tm, tn, tk = min(tm, m), min(tn, n), min(tk, k)
    gm, gn, gk = m // tm, n // tn, k // tk
    assert gm * tm == m and gn * tn == n and gk * tk == k, (name, m, n, k, tm, tn, tk)
    a_spec = (pl.BlockSpec((tk, tm), lambda i, j, l: (l, i)) if ta
              else pl.BlockSpec((tm, tk), lambda i, j, l: (i, l)))
    b_spec = (pl.BlockSpec((tn, tk), lambda i, j, l: (j, l)) if tb
              else pl.BlockSpec((tk, tn), lambda i, j, l: (l, j)))
    dims = (((0 if ta else 1,), (1 if tb else 0,)), ((), ()))
    ins, in_specs = [a, b], [a_spec, b_spec]
    if resid is not None:
        ins.append(resid)
        in_specs.append(pl.BlockSpec((tm, tn), lambda i, j, l: (i, j)))
    scratch = [pltpu.VMEM((tm, tn), F32)] if gk > 1 else []

    def body(*refs):
        it = iter(refs)
        a_ref, b_ref = next(it), next(it)
        resid_ref = next(it) if resid is not None else None
        o_ref = next(it)
        acc_ref = next(it) if gk > 1 else None
        l = pl.program_id(2)
        part = lax.dot_general(a_ref[...].astype(BF), b_ref[...].astype(BF), dims,
                               preferred_element_type=F32)

        def finish(acc):
            if resid_ref is not None:
                acc = acc + resid_ref[...].astype(F32)
            o_ref[...] = acc.astype(out_dtype)

        if gk == 1:
            finish(part)
        else:
            _accum(acc_ref, part, l == 0)

            @pl.when(l == gk - 1)
            def _():
                finish(acc_ref[...])

    return _pcall(body, name=name, grid=(gm, gn, gk), in_specs=in_specs,
                  out_specs=pl.BlockSpec((tm, tn), lambda i, j, l: (i, j)),
                  out_shape=jax.ShapeDtypeStruct((m, n), out_dtype), scratch_shapes=scratch,
                  compiler_params=_cp(("parallel", "parallel", "arbitrary")))(*ins)


def _fmm(lhs, rhs, extras, epilogue, outs, *, m, n, tm, tn, name, comm=None):
    tm, tn = min(tm, m), min(tn, n)
    assert m % tm == 0 and n % tn == 0, (name, m, n, tm, tn)
    in_specs, args = [], []
    for a in lhs:
        in_specs.append(pl.BlockSpec((tm, a.shape[1]), lambda i, j: (i, 0)))
        args.append(a)
    for li, b, tb in rhs:
        k = lhs[li].shape[1]
        in_specs.append(pl.BlockSpec((tn, k), lambda i, j: (j, 0)) if tb
                        else pl.BlockSpec((k, tn), lambda i, j: (0, j)))
        args.append(b)
    for arr, w, col in extras:
        in_specs.append(pl.BlockSpec((tm, w), lambda i, j, col=col: (i, col(j))))
        args.append(arr)
    out_specs = [pl.BlockSpec((tm, w), lambda i, j, col=col: (i, col(j))) for _, _, w, col in outs]
    out_shape = [jax.ShapeDtypeStruct((m, total), dt) for dt, total, _, _ in outs]
    nl, nr, ne = len(lhs), len(rhs), len(extras)

    def body(*refs):
        prods = []
        for r, (li, _, tb) in enumerate(rhs):
            prods.append(lax.dot_general(refs[li][...], refs[nl + r][...], _NT if tb else _NN,
                                         preferred_element_type=F32))
        vals = epilogue(prods, [ref[...] for ref in refs[nl + nr:nl + nr + ne]])
        for o_ref, v in zip(refs[nl + nr + ne:], vals):
            o_ref[...] = v.astype(o_ref.dtype)

    gm, gn = m // tm, n // tn
    res, comm_res = _hosted_call(
        body, comm, args, name=name, grid=(gm, gn), in_specs=in_specs, out_specs=out_specs,
        out_shape=out_shape, scratch_shapes=[], sem=("arbitrary", "arbitrary"), nsteps=gm * gn,
        step_fn=lambda: pl.program_id(0) * gn + pl.program_id(1))
    return res if comm is None else (res, comm_res)


def _grp_of(i):
    return [jnp.logical_and(i >= GRP_OFF[g], i < GRP_OFF[g] + GRP_N[g]) for g in range(4)]


def _grp_idx(i, g):
    return jnp.clip(i - GRP_OFF[g], 0, GRP_N[g] - 1)


def _inproj_fwd(u, win_t, b_in, *, t, comm=None):
    n_tiles = IN_W // WT
    dims = (((1,), (1,)), ((), ()))
    dtypes = (BF, BF, F32, F32)

    def body(u_ref, w_ref, b_ref, *o_refs):
        i = pl.program_id(0)
        p = lax.dot_general(u_ref[...], w_ref[...], dims, preferred_element_type=F32) + b_ref[...]
        for g, pred in enumerate(_grp_of(i)):
            @pl.when(pred)
            def _(g=g):
                o_refs[g][...] = p.astype(dtypes[g])

    return _hosted_call(
        body, comm, (u, win_t, b_in), name="inproj_fwd", grid=(n_tiles,),
        in_specs=[pl.BlockSpec((t, D), lambda i: (0, 0)),
                  pl.BlockSpec((WT, D), lambda i: (i, 0)),
                  pl.BlockSpec((1, WT), lambda i: (0, i))],
        out_specs=[pl.BlockSpec((t, WT), lambda i, g=g: (0, _grp_idx(i, g))) for g in range(4)],
        out_shape=[jax.ShapeDtypeStruct((t, GRP_N[g] * WT), dtypes[g]) for g in range(4)],
        scratch_shapes=[], sem=("arbitrary",), nsteps=n_tiles, step_fn=lambda: pl.program_id(0))


def _inproj_bwd_x(dps, win_t, x, g, resid, *, t, part, prev=None, comm=None):
    n_tiles = IN_W // WT
    n_row = 4 if t >= 2048 else 2
    tm = t // n_row
    per = 1 if part == 0 else n_row - 1
    row = lambda i: part + i

    n_chunks = 8
    h_first, g_first = 2, 6
    sub = D // WT

    def w_block(l):
        return jnp.where(l == 0, GRP_OFF[0], jnp.where(l == 1, GRP_OFF[1], GRP_OFF[2] + sub * (l - h_first)))

    def body(d0, d1, d2, d3, w0, w1, w2, w3, x_ref, g_ref, r_ref, *rest):
        dg_prev = rest[0] if prev is not None else None
        o_ref, dg_ref, acc_ref = rest[-3], rest[-2], rest[-1]
        i, l = pl.program_id(0), pl.program_id(1)

        @pl.when(l == 1)
        def _():
            acc_ref[...] += jnp.dot(d1[...], w0[...], preferred_element_type=F32)

        w = jnp.concatenate([w0[...], w1[...], w2[...], w3[...]], axis=0)
        for pred, d_ref in ((l == 0, d0), (jnp.logical_and(l >= h_first, l < g_first), d2), (l >= g_first, d3)):
            @pl.when(pred)
            def _(d_ref=d_ref):
                _accum(acc_ref, jnp.dot(d_ref[...], w, preferred_element_type=F32), l == 0)

        @pl.when(l == n_chunks - 1)
        def _():
            xv = x_ref[...]
            r = lax.rsqrt(jnp.mean(xv * xv, axis=-1, keepdims=True) + EPS)
            xh = xv * r
            du = acc_ref[...]
            dxh = du * g_ref[...]
            o_ref[...] = r_ref[...] + r * (dxh - xh * jnp.mean(dxh * xh, axis=-1, keepdims=True))
            dg = jnp.sum(du * xh, axis=0, keepdims=True)
            if dg_prev is not None:
                dg = dg + jnp.where(i == 0, 1.0, 0.0) * dg_prev[...]
            _accum(dg_ref, dg, i == 0)

    rows = lambda w: pl.BlockSpec((tm, w), lambda i, l: (row(i), 0))
    in_specs = ([rows(D), rows(256),
                 pl.BlockSpec((tm, D), lambda i, l: (row(i), jnp.clip(l - h_first, 0, 3))),
                 pl.BlockSpec((tm, D), lambda i, l: (row(i), jnp.clip(l - g_first, 0, 1)))]
                + [pl.BlockSpec((WT, D), lambda i, l, o=o: (w_block(l) + o, 0)) for o in range(sub)]
                + [rows(D), pl.BlockSpec((1, D), lambda i, l: (0, 0)), rows(D)])
    args = list(dps) + [win_t] * sub + [x, g, resid]
    aliases = None
    if prev is not None:
        in_specs += [pl.BlockSpec((1, D), lambda i, l: (0, 0)), _hbm_spec()]
        args += [prev[1], prev[0]]
        aliases = {len(args) - 1: 0}
    return _hosted_call(
        body, comm, args, name="inproj_bwd_x%d" % part, grid=(per, n_chunks), in_specs=in_specs,
        out_specs=[rows(D), pl.BlockSpec((1, D), lambda i, l: (0, 0))],
        out_shape=[jax.ShapeDtypeStruct((t, D), F32), jax.ShapeDtypeStruct((1, D), F32)],
        scratch_shapes=[pltpu.VMEM((tm, D), F32)], sem=("arbitrary", "arbitrary"), nsteps=per * n_chunks,
        step_fn=lambda: pl.program_id(0) * n_chunks + pl.program_id(1), aliases=aliases)


def _inproj_bwd_w(dps, u, *, t):
    n_tiles = IN_W // WT
    dims = (((0,), (0,)), ((), ()))

    def body(d0, d1, d2, d3, u_ref, o_ref, db_ref):
        i = pl.program_id(0)
        uv = u_ref[...]
        for g, (pred, d_ref) in enumerate(zip(_grp_of(i), (d0, d1, d2, d3))):
            @pl.when(pred)
            def _(d_ref=d_ref):
                dv = d_ref[...]
                o_ref[...] = lax.dot_general(dv, uv, dims, preferred_element_type=F32).astype(BF)
                db_ref[...] = jnp.sum(dv.astype(F32), axis=0, keepdims=True)

    return _pcall(body, name="inproj_bwd_w", grid=(n_tiles,),
                  in_specs=[pl.BlockSpec((t, WT), lambda i, g=g: (0, _grp_idx(i, g))) for g in range(4)]
                  + [pl.BlockSpec((t, D), lambda i: (0, 0))],
                  out_specs=[pl.BlockSpec((WT, D), lambda i: (i, 0)),
                             pl.BlockSpec((1, WT), lambda i: (0, i))],
                  out_shape=[jax.ShapeDtypeStruct((IN_W, D), BF), jax.ShapeDtypeStruct((1, IN_W), F32)],
                  compiler_params=_cp(("arbitrary",)))(*dps, u)


def _row_spec(tm, width, col=0):
    return pl.BlockSpec((tm, width), lambda i: (i, col))


def _vec_spec(width):
    return pl.BlockSpec((1, width), lambda i: (0, 0))


def _rms_fwd(x, g, *, tm, name, comm=None):
    t = x.shape[0]
    tm = min(tm, t)

    def body(x_ref, g_ref, u_ref):
        xv = x_ref[...]
        r = lax.rsqrt(jnp.mean(xv * xv, axis=-1, keepdims=True) + EPS)
        u_ref[...] = (xv * r * g_ref[...]).astype(BF)

    (u,), comm_res = _hosted_call(
        body, comm, (x, g), name=name, grid=(t // tm,), in_specs=[_row_spec(tm, D), _vec_spec(D)],
        out_specs=[_row_spec(tm, D)], out_shape=[jax.ShapeDtypeStruct((t, D), BF)], scratch_shapes=[],
        sem=("arbitrary",), nsteps=t // tm, step_fn=lambda: pl.program_id(0))
    return u if comm is None else (u, comm_res)


def _rms_bwd(du, x, g, resid, *, tm, name):
    t = x.shape[0]
    tm = min(tm, t)

    def body(du_ref, x_ref, g_ref, r_ref, dx_ref, dxb_ref, dg_ref):
        xv = x_ref[...]
        r = lax.rsqrt(jnp.mean(xv * xv, axis=-1, keepdims=True) + EPS)
        xh = xv * r
        duv = du_ref[...]
        dxh = duv * g_ref[...]
        dx = r_ref[...] + r * (dxh - xh * jnp.mean(dxh * xh, axis=-1, keepdims=True))
        dx_ref[...] = dx
        dxb_ref[...] = dx.astype(BF)
        _accum(dg_ref, jnp.sum(duv * xh, axis=0, keepdims=True), pl.program_id(0) == 0)

    return _pcall(body, name=name, grid=(t // tm,),
                  in_specs=[_row_spec(tm, D), _row_spec(tm, D), _vec_spec(D), _row_spec(tm, D)],
                  out_specs=[_row_spec(tm, D), _row_spec(tm, D), _vec_spec(D)],
                  out_shape=[jax.ShapeDtypeStruct((t, D), F32), jax.ShapeDtypeStruct((t, D), BF),
                             jax.ShapeDtypeStruct((1, D), F32)],
                  compiler_params=_cp(("arbitrary",)))(du, x, g, resid)


def _loss_head(h2, tgt, g, *, tm):
    t = h2.shape[0]
    tm = min(tm, t)

    def body(h_ref, t_ref, g_ref, dh_ref, dhb_ref, dg_ref, loss_ref):
        hv = h_ref[...]
        gv = g_ref[...]
        r = lax.rsqrt(jnp.mean(hv * hv, axis=-1, keepdims=True) + EPS)
        xh = hv * r
        err = xh * gv - t_ref[...]
        lp = jnp.sum(jnp.sum(err * err, axis=1, keepdims=True), axis=0, keepdims=True) * (0.5 / D)
        dy = err * (1.0 / D)
        dxh = dy * gv
        dh = r * (dxh - xh * jnp.mean(dxh * xh, axis=-1, keepdims=True))
        dh_ref[...] = dh
        dhb_ref[...] = dh.astype(BF)
        first = pl.program_id(0) == 0
        _accum(dg_ref, jnp.sum(dy * xh, axis=0, keepdims=True), first)
        _accum(loss_ref, jnp.broadcast_to(lp, (1, 128)), first)

    return _pcall(body, name="loss_head", grid=(t // tm,),
                  in_specs=[_row_spec(tm, D), _row_spec(tm, D), _vec_spec(D)],
                  out_specs=[_row_spec(tm, D), _row_spec(tm, D), _vec_spec(D), _vec_spec(128)],
                  out_shape=[jax.ShapeDtypeStruct((t, D), F32), jax.ShapeDtypeStruct((t, D), BF),
                             jax.ShapeDtypeStruct((1, D), F32), jax.ShapeDtypeStruct((1, 128), F32)],
                  compiler_params=_cp(("arbitrary",)))(h2, tgt, g)


def _attn_kv_tiles(kprev, kcur):
    kv = jnp.concatenate([kprev, kcur], axis=0).astype(F32)
    lo = lax.broadcasted_iota(jnp.int32, (2 * BLK, 128), 1) < HEAD
    tiles = []
    for part in (kv[:, 0:128], kv[:, 128:256]):
        rolled = pltpu.roll(part, HEAD, 1)
        z = jnp.zeros_like(part)
        tiles.append(((jnp.where(lo, part, z).astype(BF), jnp.where(lo, z, rolled).astype(BF)),
                      (jnp.where(lo, rolled, z).astype(BF), jnp.where(lo, z, part).astype(BF))))
    k_t, v_t = tiles
    return [(jnp.concatenate(k_t[h], axis=0), jnp.concatenate(v_t[h], axis=0)) for h in range(2)]


def _attn_mask(i):
    qi = lax.broadcasted_iota(jnp.int32, (BLK, 2 * BLK), 0)
    kj = lax.broadcasted_iota(jnp.int32, (BLK, 2 * BLK), 1)
    first_key = jnp.where(i == 0, BLK, 0)
    in_prev = jnp.logical_and(jnp.logical_and(kj < BLK, kj > qi), kj >= first_key)
    in_cur = jnp.logical_and(kj >= BLK, kj - BLK <= qi)
    return jnp.logical_or(in_prev, in_cur)


def _attn_probs(s, sink, valid):
    s = jnp.where(valid, s * SCALE, NEG)
    mx = jnp.maximum(jnp.max(s, axis=-1, keepdims=True), sink)
    e = jnp.exp(s - mx)
    es = jnp.exp(sink - mx)
    inv = 1.0 / (jnp.sum(e, axis=-1, keepdims=True) + es)
    return e * inv, es * inv


_KEYS = 2 * BLK


def _pair(ref, j):
    return ref[:, j * 128:(j + 1) * 128]


def _attn_fwd(q, kv, sinks, *, t, comm=None):
    nb = t // BLK

    def body(sink_ref, q_ref, kp_ref, kc_ref, o_ref):
        valid = _attn_mask(pl.program_id(0))
        tiles = _attn_kv_tiles(kp_ref[...], kc_ref[...])
        s = [lax.dot_general(_pair(q_ref, j), tiles[j // 4][0], _NT, preferred_element_type=F32)
             for j in range(N_PAIR)]
        p = []
        for j in range(N_PAIR):
            pe, _ = _attn_probs(s[j][:, 0:_KEYS], sink_ref[0, 2 * j], valid)
            po, _ = _attn_probs(s[j][:, _KEYS:2 * _KEYS], sink_ref[0, 2 * j + 1], valid)
            p.append(jnp.concatenate([pe.astype(BF), po.astype(BF)], axis=1))
        for j in range(N_PAIR):
            o_ref[:, j * 128:(j + 1) * 128] = jnp.dot(p[j], tiles[j // 4][1],
                                                      preferred_element_type=F32).astype(BF)

    return _hosted_call(
        body, comm, (sinks, q, kv, kv), name="attn_fwd", grid=(nb,),
        in_specs=[pl.BlockSpec(memory_space=pltpu.SMEM),
                  pl.BlockSpec((BLK, D), lambda i: (i, 0)),
                  pl.BlockSpec((BLK, 256), lambda i: (jnp.maximum(i - 1, 0), 0)),
                  pl.BlockSpec((BLK, 256), lambda i: (i, 0))],
        out_specs=[pl.BlockSpec((BLK, D), lambda i: (i, 0))],
        out_shape=[jax.ShapeDtypeStruct((t, D), BF)],
        scratch_shapes=[], sem=("arbitrary",), nsteps=nb, step_fn=lambda: pl.program_id(0))


def _attn_bwd(q, kv, sinks, do, *, t, comm=None):
    nb = t // BLK
    last = nb - 1

    def body(sink_ref, q_ref, kp_ref, kc_ref, do_ref, dq_ref, dkv_ref, ds_ref, carry_ref):
        i = pl.program_id(0)

        @pl.when(i == 0)
        def _():
            ds_ref[...] = jnp.zeros_like(ds_ref)
            carry_ref[...] = jnp.zeros_like(carry_ref)

        @pl.when(i < nb)
        def _():
            valid = _attn_mask(i)
            tiles = _attn_kv_tiles(kp_ref[...], kc_ref[...])
            lane1 = lax.broadcasted_iota(jnp.int32, (1, 128), 1)
            dsink = jnp.zeros((1, 128), F32)
            s = [lax.dot_general(_pair(q_ref, j), tiles[j // 4][0], _NT, preferred_element_type=F32)
                 for j in range(N_PAIR)]
            dp = [lax.dot_general(_pair(do_ref, j), tiles[j // 4][1], _NT, preferred_element_type=F32)
                  for j in range(N_PAIR)]
            p_all, ds_all = [], []
            for j in range(N_PAIR):
                halves = []
                for par in range(2):
                    cols = slice(par * _KEYS, (par + 1) * _KEYS)
                    p, ps = _attn_probs(s[j][:, cols], sink_ref[0, 2 * j + par], valid)
                    dpj = dp[j][:, cols]
                    dd = jnp.sum(p * dpj, axis=-1, keepdims=True)
                    dsink = dsink + jnp.where(lane1 == 2 * j + par,
                                              -jnp.sum(ps * dd, axis=0, keepdims=True), 0.0)
                    halves.append((p.astype(BF), (p * (dpj - dd)).astype(BF)))
                p_all.append(jnp.concatenate([halves[0][0], halves[1][0]], axis=1))
                ds_all.append(jnp.concatenate([halves[0][1], halves[1][1]], axis=1))
            for j in range(N_PAIR):
                dq_ref[:, j * 128:(j + 1) * 128] = (
                    jnp.dot(ds_all[j], tiles[j // 4][0], preferred_element_type=F32) * SCALE).astype(BF)
            ds_ref[...] += dsink
            gk, gv = [], []
            for h in range(2):
                grp = range(4 * h, 4 * h + 4)
                q_rows = jnp.concatenate([_pair(q_ref, j) for j in grp], axis=0)
                do_rows = jnp.concatenate([_pair(do_ref, j) for j in grp], axis=0)
                g_k = lax.dot_general(jnp.concatenate([ds_all[j] for j in grp], axis=0), q_rows, _TN,
                                      preferred_element_type=F32)
                g_v = lax.dot_general(jnp.concatenate([p_all[j] for j in grp], axis=0), do_rows, _TN,
                                      preferred_element_type=F32)
                gk.append((g_k[0:_KEYS], g_k[_KEYS:2 * _KEYS]))
                gv.append((g_v[0:_KEYS], g_v[_KEYS:2 * _KEYS]))
            lo = lax.broadcasted_iota(jnp.int32, (2 * BLK, 128), 1) < HEAD
            zero = jnp.zeros((2 * BLK, 128), F32)

            def unpad(g):
                return (jnp.where(lo, g[0][0] + pltpu.roll(g[0][1], HEAD, 1), zero)
                        + jnp.where(lo, zero, pltpu.roll(g[1][0], HEAD, 1) + g[1][1]))

            dk = unpad(gk) * SCALE
            dv = unpad(gv)
            dkv_ref[:, 0:128] = (carry_ref[:, 0:128] + dk[0:BLK]).astype(BF)
            dkv_ref[:, 128:256] = (carry_ref[:, 128:256] + dv[0:BLK]).astype(BF)
            carry_ref[:, 0:128] = dk[BLK:2 * BLK]
            carry_ref[:, 128:256] = dv[BLK:2 * BLK]

        @pl.when(i == nb)
        def _():
            dkv_ref[...] = carry_ref[...].astype(BF)

    return _hosted_call(
        body, comm, (sinks, q, kv, kv, do), name="attn_bwd", grid=(nb + 1,),
        in_specs=[pl.BlockSpec(memory_space=pltpu.SMEM),
                  pl.BlockSpec((BLK, D), lambda i: (jnp.minimum(i, last), 0)),
                  pl.BlockSpec((BLK, 256), lambda i: (jnp.clip(i - 1, 0, last), 0)),
                  pl.BlockSpec((BLK, 256), lambda i: (jnp.minimum(i, last), 0)),
                  pl.BlockSpec((BLK, D), lambda i: (jnp.minimum(i, last), 0))],
        out_specs=[pl.BlockSpec((BLK, D), lambda i: (jnp.minimum(i, last), 0)),
                   pl.BlockSpec((BLK, 256), lambda i: (jnp.maximum(i - 1, 0), 0)),
                   pl.BlockSpec((1, 128), lambda i: (0, 0))],
        out_shape=[jax.ShapeDtypeStruct((t, D), BF), jax.ShapeDtypeStruct((t, 256), BF),
                   jax.ShapeDtypeStruct((1, 128), F32)],
        scratch_shapes=[pltpu.VMEM((BLK, 256), F32)], sem=("arbitrary",), nsteps=nb + 1,
        step_fn=lambda: pl.program_id(0))


def _split3(v):
    h = v.astype(BF)
    r = v - h.astype(F32)
    m = r.astype(BF)
    lo = (r - m.astype(F32)).astype(BF)
    return jnp.concatenate([h, m, lo], axis=1)


def _apply01(mat, v):
    n = v.shape[1]
    r = jnp.dot(mat, _split3(v), preferred_element_type=F32)
    return r[:, 0:n] + r[:, n:2 * n] + r[:, 2 * n:3 * n]


def _hgrn_gates(hq, hf, lb):
    sq = _sig(hq)
    sg = _sig(hf)
    f = lb + (1.0 - lb) * sg
    return hq * sq, (1.0 - lb) * (1.0 - sg), jnp.log(f), sq, sg, f


def _tri(upper):
    r = lax.broadcasted_iota(jnp.int32, (CH, CH), 0)
    c = lax.broadcasted_iota(jnp.int32, (CH, CH), 1)
    return (c >= r) if upper else (c <= r)


def _lb_from_logits(lg_ref):
    return 1.0 / (1.0 + jnp.exp(lg_ref[1:2, :] - lg_ref[0:1, :]))


def _hgrn_fwd(h4, logits, norm_g, *, t, comm=None):
    nc = t // CH
    nt_dims = (((1,), (1,)), ((), ()))
    tn_dims = (((0,), (0,)), ((), ()))

    def body(h_ref, lg_ref, ng_ref, y_ref, o_ref, st_ref, s_scr, b_scr, qa_s, ka_s, qb_s, kb_s, v_s):
        @pl.when(pl.program_id(0) == 0)
        def _():
            s_scr[...] = jnp.zeros_like(s_scr)

        heads = [slice(h * HG_K, (h + 1) * HG_K) for h in range(HG_HEADS)]
        causal = _tri(False)
        q, k, g, _, _, _ = _hgrn_gates(h_ref[:, 0:D], h_ref[:, D:2 * D], _lb_from_logits(lg_ref))
        b_scr[...] = _apply01(jnp.where(causal, 1.0, 0.0).astype(BF), g)
        b = b_scr[...]
        b_mid = b_scr[CH // 2 - 1:CH // 2, :]
        b_last = b_scr[CH - 1:CH, :]
        qa_s[...] = (q * jnp.exp(b - b_mid)).astype(BF)
        ka_s[...] = (k * jnp.exp(b_mid - b)).astype(BF)
        qb_s[...] = (q * jnp.exp(b)).astype(BF)
        kb_s[...] = (k * jnp.exp(b_last - b)).astype(BF)
        v_s[...] = h_ref[:, 2 * D:3 * D].astype(BF)
        dec = jnp.exp(b_last)
        st_ref[0] = s_scr[...].astype(BF)
        a = [jnp.where(causal, lax.dot_general(qa_s[:, sl], ka_s[:, sl], nt_dims, preferred_element_type=F32),
                       0.0).astype(BF) for sl in heads]
        for h, sl in enumerate(heads):
            o_ref[:, sl] = (jnp.dot(a[h], v_s[:, sl], preferred_element_type=F32)
                            + lax.dot_general(qb_s[:, sl], s_scr[h].astype(BF), nt_dims,
                                              preferred_element_type=F32))
        for h, sl in enumerate(heads):
            s_scr[h] = dec[:, sl] * s_scr[h] + lax.dot_general(v_s[:, sl], kb_s[:, sl], tn_dims,
                                                               preferred_element_type=F32)
        for h, sl in enumerate(heads):
            o = o_ref[:, sl]
            on = o * lax.rsqrt(jnp.mean(o * o, axis=-1, keepdims=True) + EPS)
            y_ref[:, sl] = (on * ng_ref[:, sl] * _sig(h_ref[:, 3 * D + h * HG_K:3 * D + (h + 1) * HG_K])).astype(BF)

    half = lambda: pltpu.VMEM((CH, D), BF)
    return _hosted_call(
        body, comm, (h4, logits, norm_g), name="hgrn_fwd", grid=(nc,),
        in_specs=[pl.BlockSpec((CH, 4 * D), lambda n: (n, 0)),
                  pl.BlockSpec((2, D), lambda n: (0, 0)),
                  pl.BlockSpec((1, D), lambda n: (0, 0))],
        out_specs=[pl.BlockSpec((CH, D), lambda n: (n, 0)),
                   pl.BlockSpec((CH, D), lambda n: (n, 0)),
                   pl.BlockSpec((1, HG_HEADS, HG_K, HG_K), lambda n: (n, 0, 0, 0))],
        out_shape=[jax.ShapeDtypeStruct((t, D), BF), jax.ShapeDtypeStruct((t, D), F32),
                   jax.ShapeDtypeStruct((nc, HG_HEADS, HG_K, HG_K), BF)],
        scratch_shapes=[pltpu.VMEM((HG_HEADS, HG_K, HG_K), F32), pltpu.VMEM((CH, D), F32),
                        half(), half(), half(), half(), half()],
        sem=("arbitrary",), nsteps=nc, step_fn=lambda: pl.program_id(0))


def _hgrn_bwd(h4, logits, norm_g, o_pre, states, dy, *, t, comm=None):
    nc = t // CH
    nt_dims = (((1,), (1,)), ((), ()))
    tn_dims = (((0,), (0,)), ((), ()))

    def body(h_ref, lg_ref, ng_ref, o_ref, st_ref, dy_ref, dh_ref, dlg_ref, dng_ref, ds_scr, dlb_scr,
             b_scr, tail_s, e_qa, e_ka, e_qb, e_kb, q_s, k_s, dqa_s, dka_s, dqb_s, dkb_s,
             qa_s, ka_s, qb_s, kb_s, v_s, do_s):
        n = pl.program_id(0)

        @pl.when(n == 0)
        def _():
            ds_scr[...] = jnp.zeros_like(ds_scr)
            dlb_scr[...] = jnp.zeros_like(dlb_scr)
            dng_ref[...] = jnp.zeros_like(dng_ref)

        heads = [slice(h * HG_K, (h + 1) * HG_K) for h in range(HG_HEADS)]
        lb = _lb_from_logits(lg_ref)
        causal = _tri(False)
        q, k, g, _, _, _ = _hgrn_gates(h_ref[:, 0:D], h_ref[:, D:2 * D], lb)
        b_scr[...] = _apply01(jnp.where(causal, 1.0, 0.0).astype(BF), g)
        b = b_scr[...]
        b_mid = b_scr[CH // 2 - 1:CH // 2, :]
        b_last = b_scr[CH - 1:CH, :]
        q_s[...] = q
        k_s[...] = k
        for e_ref, s_ref, base, expo in ((e_qa, qa_s, q, b - b_mid), (e_ka, ka_s, k, b_mid - b),
                                         (e_qb, qb_s, q, b), (e_kb, kb_s, k, b_last - b)):
            e = jnp.exp(expo)
            e_ref[...] = e
            s_ref[...] = (base * e).astype(BF)
        v_s[...] = h_ref[:, 2 * D:3 * D].astype(BF)
        dec = jnp.exp(b_last)
        for h, sl in enumerate(heads):
            gcol = slice(3 * D + h * HG_K, 3 * D + (h + 1) * HG_K)
            ngh = ng_ref[:, sl]
            sgate = _sig(h_ref[:, gcol])
            o = o_ref[:, sl]
            r = lax.rsqrt(jnp.mean(o * o, axis=-1, keepdims=True) + EPS)
            on = o * r
            dyh = dy_ref[:, sl]
            dh_ref[:, gcol] = (dyh * on * ngh * sgate * (1.0 - sgate)).astype(BF)
            dng_ref[:, sl] += jnp.sum(dyh * on * sgate, axis=0, keepdims=True)
            don = dyh * ngh * sgate
            do_s[:, sl] = (r * (don - on * jnp.mean(don * on, axis=-1, keepdims=True))).astype(BF)
        a = [jnp.where(causal, lax.dot_general(qa_s[:, sl], ka_s[:, sl], nt_dims, preferred_element_type=F32),
                       0.0).astype(BF) for sl in heads]
        da = [jnp.where(causal, lax.dot_general(do_s[:, sl], v_s[:, sl], nt_dims, preferred_element_type=F32),
                        0.0).astype(BF) for sl in heads]
        for h, sl in enumerate(heads):
            dh_ref[:, 2 * D + h * HG_K:2 * D + (h + 1) * HG_K] = (
                lax.dot_general(a[h], do_s[:, sl], tn_dims, preferred_element_type=F32)
                + lax.dot_general(kb_s[:, sl], ds_scr[h].astype(BF), nt_dims, preferred_element_type=F32)
            ).astype(BF)
        for h, sl in enumerate(heads):
            dqa_s[:, sl] = jnp.dot(da[h], ka_s[:, sl], preferred_element_type=F32)
        for h, sl in enumerate(heads):
            dka_s[:, sl] = lax.dot_general(da[h], qa_s[:, sl], tn_dims, preferred_element_type=F32)
        for h, sl in enumerate(heads):
            dqb_s[:, sl] = jnp.dot(do_s[:, sl], st_ref[0, h], preferred_element_type=F32)
        for h, sl in enumerate(heads):
            dkb_s[:, sl] = jnp.dot(v_s[:, sl], ds_scr[h].astype(BF), preferred_element_type=F32)
        for h, sl in enumerate(heads):
            tail_s[:, sl] = jnp.sum(dec[:, sl] * st_ref[0, h].astype(F32) * ds_scr[h], axis=0, keepdims=True)
        for h, sl in enumerate(heads):
            ds_scr[h] = (lax.dot_general(do_s[:, sl], qb_s[:, sl], tn_dims, preferred_element_type=F32)
                         + dec[:, sl] * ds_scr[h])
        qv, kv = q_s[...], k_s[...]
        dqa, dka, dqb, dkb = dqa_s[...], dka_s[...], dqb_s[...], dkb_s[...]
        eqa, eka, eqb, ekb = e_qa[...], e_ka[...], e_qb[...], e_kb[...]
        dkb_kb = dkb * (kv * ekb)
        db_last = jnp.sum(dkb_kb, axis=0, keepdims=True) + tail_s[...]
        last_row = lax.broadcasted_iota(jnp.int32, (CH, D), 0) == CH - 1
        db = (dqa * (qv * eqa) - dka * (kv * eka) + dqb * (qv * eqb) - dkb_kb
              + jnp.where(last_row, db_last, 0.0))
        dg = _apply01(jnp.where(_tri(True), 1.0, 0.0).astype(BF), db)
        dq = dqa * eqa + dqb * eqb
        dk = dka * eka + dkb * ekb
        hq = h_ref[:, 0:D]
        _, _, _, sq, sg, f = _hgrn_gates(hq, h_ref[:, D:2 * D], lb)
        dh_ref[:, 0:D] = (dq * sq * (1.0 + hq * (1.0 - sq))).astype(BF)
        dfk = dg / f - dk
        dh_ref[:, D:2 * D] = ((1.0 - lb) * dfk * sg * (1.0 - sg)).astype(BF)
        dlb_scr[...] += jnp.sum((1.0 - sg) * dfk, axis=0, keepdims=True)

        @pl.when(n == nc - 1)
        def _():
            dl0 = dlb_scr[...] * lb * (1.0 - lb)
            dlg_ref[0:1, :] = dl0
            dlg_ref[1:2, :] = -dl0

    rev = lambda n: (nc - 1 - n, 0)
    return _hosted_call(
        body, comm, (h4, logits, norm_g, o_pre, states, dy), name="hgrn_bwd", grid=(nc,),
        in_specs=[pl.BlockSpec((CH, 4 * D), rev),
                  pl.BlockSpec((2, D), lambda n: (0, 0)),
                  pl.BlockSpec((1, D), lambda n: (0, 0)),
                  pl.BlockSpec((CH, D), rev),
                  pl.BlockSpec((1, HG_HEADS, HG_K, HG_K), lambda n: (nc - 1 - n, 0, 0, 0)),
                  pl.BlockSpec((CH, D), rev)],
        out_specs=[pl.BlockSpec((CH, 4 * D), rev),
                   pl.BlockSpec((2, D), lambda n: (0, 0)),
                   pl.BlockSpec((1, D), lambda n: (0, 0))],
        out_shape=[jax.ShapeDtypeStruct((t, 4 * D), BF), jax.ShapeDtypeStruct((2, D), F32),
                   jax.ShapeDtypeStruct((1, D), F32)],
        scratch_shapes=([pltpu.VMEM((HG_HEADS, HG_K, HG_K), F32), pltpu.VMEM((1, D), F32),
                         pltpu.VMEM((CH, D), F32), pltpu.VMEM((1, D), F32)]
                        + [pltpu.VMEM((CH, D), F32)] * 10 + [pltpu.VMEM((CH, D), BF)] * 6),
        sem=("arbitrary",), nsteps=nc, step_fn=lambda: pl.program_id(0))


def _place():
    x, y, c = lax.axis_index("x"), lax.axis_index("y"), lax.axis_index("c")
    return x, y, c, [(1 - x, y), (x, 1 - y), (1 - x, 1 - y)]


def _gather_comm(shards, mid):
    n = len(shards)
    r = [s.shape[0] for s in shards]

    def tools(ins, outs, sems):
        send_sems, recv_sems, local_sems = sems
        x, y, c, chips = _place()
        me, sib = (x, y, c), (x, y, 1 - c)

        def rows(w, dev):
            return outs[w].at[pl.ds((4 * dev[0] + 2 * dev[1] + dev[2]) * r[w], r[w]), :]

        def copy(kind, w, block, to, src=None):
            return pltpu.make_async_remote_copy(
                src_ref=rows(w, block) if src is None else src, dst_ref=rows(w, block),
                send_sem=send_sems.at[kind], recv_sem=recv_sems.at[kind], device_id=to, device_id_type=MESH)

        def all_of(kind):
            whole = outs[0].at[pl.ds(0, sum(r)), :]
            return pltpu.make_async_remote_copy(
                src_ref=whole, dst_ref=whole, send_sem=send_sems.at[kind], recv_sem=recv_sems.at[kind],
                device_id=me, device_id_type=MESH)

        mine = [pltpu.make_async_copy(ins[w], rows(w, me), local_sems.at[w]) for w in range(n)]
        return c, chips, me, sib, copy, all_of, mine

    def start(ins, outs, sems):
        c, chips, me, sib, copy, _, mine = tools(ins, outs, sems)
        for cp in mine:
            cp.start()
        for w in range(n):
            copy(0, w, me, sib, src=ins[w]).start()
            for j, chip in enumerate(chips):
                copy(1 + j, w, me, (*chip, c), src=ins[w]).start()

    def pass_on(ins, outs, sems):
        c, chips, _, sib, copy, all_of, _ = tools(ins, outs, sems)
        for j, chip in enumerate(chips):
            all_of(1 + j).wait_recv()
            for w in range(n):
                copy(4 + j, w, (*chip, c), sib).start()

    def finish(ins, outs, sems):
        _, _, _, _, _, all_of, mine = tools(ins, outs, sems)
        all_of(0).wait_recv()
        for j in range(3):
            all_of(4 + j).wait_recv()
        for kind in range(7):
            all_of(kind).wait_send()
        for cp in mine:
            cp.wait()

    return _Comm(shards, [jax.ShapeDtypeStruct((N_DEV * rw, D), BF) for rw in r],
                 [pltpu.SemaphoreType.DMA((7,)), pltpu.SemaphoreType.DMA((7,)), pltpu.SemaphoreType.DMA((n,))],
                 [(0.0, start), (mid, pass_on), (1.0, finish)])


def _pair_comm(grads):
    n = len(grads)
    r = [g.shape[0] // N_DEV for g in grads]

    def start(ins, outs, sems):
        send_sems, recv_sems = sems
        x, y, c, _ = _place()
        for w in range(n):
            for a in range(N_CHIP):
                pltpu.make_async_remote_copy(
                    src_ref=ins[w].at[pl.ds((2 * a + 1 - c) * r[w], r[w]), :], dst_ref=outs[w].at[a],
                    send_sem=send_sems.at[w], recv_sem=recv_sems.at[w],
                    device_id=(x, y, 1 - c), device_id_type=MESH).start()

    def finish(ins, outs, sems):
        send_sems, recv_sems = sems
        x, y, c, _ = _place()
        for w in range(n):
            pltpu.make_async_remote_copy(
                src_ref=outs[w], dst_ref=outs[w], send_sem=send_sems.at[w], recv_sem=recv_sems.at[w],
                device_id=(x, y, c), device_id_type=MESH).wait()

    return _Comm(grads, [jax.ShapeDtypeStruct((N_CHIP, rw, D), BF) for rw in r],
                 [pltpu.SemaphoreType.DMA((n,)), pltpu.SemaphoreType.DMA((n,))],
                 [(0.0, start), (1.0, finish)])


def _pair_add(grad, got, core, *, name):
    r = got.shape[1]

    def body(c_ref, g_ref, got_ref, o_ref):
        o_ref[0] = (g_ref[...].astype(F32) + got_ref[0].astype(F32)).astype(BF)

    grid_spec = pltpu.PrefetchScalarGridSpec(
        num_scalar_prefetch=1, grid=(N_CHIP,),
        in_specs=[pl.BlockSpec((r, D), lambda a, c_ref: (2 * a + c_ref[0], 0)),
                  pl.BlockSpec((1, r, D), lambda a, c_ref: (a, 0, 0))],
        out_specs=pl.BlockSpec((1, r, D), lambda a, c_ref: (a, 0, 0)))
    return _pcall(body, name=name, grid_spec=grid_spec,
                  out_shape=jax.ShapeDtypeStruct((N_CHIP, r, D), BF),
                  compiler_params=_cp(("parallel",)))(core, grad, got)


def _chip_comm(pair_sums):
    n = len(pair_sums)
    r = [p.shape[1] for p in pair_sums]
    off = [sum(r[:w]) for w in range(n)]

    def tools(ins, outs, sems):
        send_sems, recv_sems, local_sems = sems
        x, y, c, chips = _place()
        my_chip = 2 * x + y

        def slot(w):
            return outs[0].at[my_chip, pl.ds(off[w], r[w]), :]

        own = [pltpu.make_async_copy(ins[w].at[my_chip], slot(w), local_sems.at[w]) for w in range(n)]
        return x, y, c, chips, my_chip, slot, own, send_sems, recv_sems

    def start(ins, outs, sems):
        x, y, c, chips, my_chip, slot, own, send_sems, recv_sems = tools(ins, outs, sems)
        for cp in own:
            cp.start()
        for j, chip in enumerate(chips):
            for w in range(n):
                pltpu.make_async_remote_copy(
                    src_ref=ins[w].at[2 * chip[0] + chip[1]], dst_ref=slot(w), send_sem=send_sems.at[j],
                    recv_sem=recv_sems.at[j], device_id=(*chip, c), device_id_type=MESH).start()

    def finish(ins, outs, sems):
        x, y, c, chips, my_chip, slot, own, send_sems, recv_sems = tools(ins, outs, sems)
        whole = outs[0].at[my_chip]
        for j in range(3):
            pltpu.make_async_remote_copy(
                src_ref=whole, dst_ref=whole, send_sem=send_sems.at[j], recv_sem=recv_sems.at[j],
                device_id=(x, y, c), device_id_type=MESH).wait()
        for cp in own:
            cp.wait()

    return _Comm(pair_sums, [jax.ShapeDtypeStruct((N_CHIP, sum(r), D), BF)],
                 [pltpu.SemaphoreType.DMA((3,)), pltpu.SemaphoreType.DMA((3,)), pltpu.SemaphoreType.DMA((n,))],
                 [(0.0, start), (1.0, finish)])


def _adam_math(w, g, m, v):
    m = ADAM_B1 * m + (1.0 - ADAM_B1) * g
    v = ADAM_B2 * v + (1.0 - ADAM_B2) * (g * g)
    m_hat = m / (1.0 - ADAM_B1 ** ADAM_STEP)
    v_hat = v / (1.0 - ADAM_B2 ** ADAM_STEP)
    delta = -ADAM_LR * (m_hat / (jnp.sqrt(v_hat) + ADAM_EPS) + ADAM_WD * w)
    return delta, m, v


def _small_allreduce_adam(gpart, w, m, v):
    def body(g_ref, w_ref, m_ref, v_ref, gs_ref, d_ref, mo_ref, vo_ref, gath, send_sems, recv_sems):
        x, y, c, _ = _place()
        me = 4 * x + 2 * y + c
        gath[me] = g_ref[...]
        cps = []
        for d in range(1, N_DEV):
            peer = (x ^ (d >> 2), y ^ ((d >> 1) & 1), c ^ (d & 1))
            cps.append(pltpu.make_async_remote_copy(
                src_ref=g_ref, dst_ref=gath.at[me], send_sem=send_sems.at[d - 1],
                recv_sem=recv_sems.at[d - 1], device_id=peer, device_id_type=MESH))
        for cp in cps:
            cp.start()
        for cp in cps:
            cp.wait()
        g = gath[0]
        for k in range(1, N_DEV):
            g = g + gath[k]
        gs_ref[...] = g
        d_ref[...], mo_ref[...], vo_ref[...] = _adam_math(w_ref[...], g, m_ref[...], v_ref[...])

    shape = jax.ShapeDtypeStruct((SMALL_ROWS, D), F32)
    vm = pl.BlockSpec(memory_space=pltpu.VMEM)
    return _pcall(body, name="small_allreduce_adam", in_specs=[vm] * 4, out_specs=[vm] * 4,
                  out_shape=[shape] * 4,
                  scratch_shapes=[pltpu.VMEM((N_DEV, SMALL_ROWS, D), F32),
                                  pltpu.SemaphoreType.DMA((N_DEV - 1,)), pltpu.SemaphoreType.DMA((N_DEV - 1,))],
                  compiler_params=pltpu.CompilerParams(has_side_effects=True))(gpart, w, m, v)


def _adam(w, parts, index, m, v, *, name):
    rows = w.shape[0]
    tr = rows if rows <= 512 else rows // 2
    steps = rows // tr

    def body(w_ref, p_ref, m_ref, v_ref, g_ref, d_ref, mo_ref, vo_ref):
        g = p_ref[0].astype(F32)
        for a in range(1, N_CHIP):
            g = g + p_ref[a].astype(F32)
        g_ref[...] = g
        d_ref[...], mo_ref[...], vo_ref[...] = _adam_math(w_ref[...], g, m_ref[...], v_ref[...])

    spec = pl.BlockSpec((tr, D), lambda i: (i, 0))
    return _pcall(body, name=name, grid=(steps,),
                  in_specs=[spec, pl.BlockSpec((N_CHIP, tr, D), lambda i: (0, index * steps + i, 0)), spec, spec],
                  out_specs=[spec] * 4, out_shape=[jax.ShapeDtypeStruct((rows, D), F32)] * 4,
                  compiler_params=_cp(("parallel",)))(w, parts, m, v)


def _step(x, tgt, shards, norm_mix_g, b_in, sinks, logits, hgrn_norm_g, norm_ffn_g, norm_final_g):
    t = x.shape[0]
    big = dict(tm=1024, tn=1024, tk=4096)
    core = lax.axis_index("c").astype(jnp.int32).reshape(1)

    u1, (win_t,) = _rms_fwd(x, norm_mix_g, tm=512, name="rms_mix", comm=_gather_comm(shards[0:1], 0.5))
    (q, kv, h4, gates), (wg_t, wba, wbh, wout) = _inproj_fwd(
        u1, win_t, b_in, t=t, comm=_gather_comm([shards[1]] + shards[4:7], 0.8))
    (y_attn,), _ = _attn_fwd(q, kv, sinks, t=t)
    (y_hgrn, o_pre, states), (wu_t, wd) = _hgrn_fwd(h4, logits, hgrn_norm_g, t=t,
                                                    comm=_gather_comm(shards[2:4], 0.8))
    col = lambda j: j
    first, second = (lambda j: 0), (lambda j: 1)
    gate_tiles = [(gates, D, first), (gates, D, second)]

    def merge(prods, ex):
        (ya_, yb_), (ga, gb) = prods, ex
        return ya_, yb_, _sig(ga) * ya_ + _sig(gb) * yb_

    ya, yb, merged = _fmm([y_attn, y_hgrn], [(0, wba, False), (1, wbh, False)], gate_tiles, merge,
                          [(BF, D, D, first)] * 3, m=t, n=D, tm=512, tn=D, name="branch_merge")
    h1 = _mm(merged, wout, m=t, n=D, k=D, resid=x, name="out_proj", **big)
    u2 = _rms_fwd(h1, norm_ffn_g, tm=512, name="rms_ffn")

    def swiglu(prods, ex):
        g_, u_ = prods
        return g_, u_, g_ * _sig(g_) * u_

    gt, up, z = _fmm([u2], [(0, wg_t, True), (0, wu_t, True)], [], swiglu, [(BF, FFN, FFN // 2, col)] * 3,
                     m=t, n=FFN, tm=1024, tn=FFN // 2, name="ffn_gate_up")
    h2 = _mm(z, wd, m=t, n=D, k=FFN, resid=h1, name="ffn_down", **big)
    dh2, dh2_b, d_norm_final, loss_row = _loss_head(h2, tgt, norm_final_g, tm=512)

    def swiglu_bwd(prods, ex):
        (dz,), (g_, u_) = prods, ex
        g_ = g_.astype(F32)
        s = _sig(g_)
        return dz * u_.astype(F32) * s * (1.0 + g_ * (1.0 - s)), dz * g_ * s

    ffn_tiles = [(gt, FFN // 2, col), (up, FFN // 2, col)]
    dgt, dup = _fmm([dh2_b], [(0, wd, True)], ffn_tiles, swiglu_bwd, [(BF, FFN, FFN // 2, col)] * 2,
                    m=t, n=FFN, tm=1024, tn=FFN // 2, name="d_gate_up")
    d_wd = _mm(z, dh2_b, m=FFN, n=D, k=t, ta=True, tm=256, tn=D, tk=4096, out_dtype=BF, name="d_w_down")
    (du2,) = _fmm([dgt, dup], [(0, wg_t, False), (1, wu_t, False)], [], lambda prods, ex: (prods[0] + prods[1],),
                  [(F32, D, 512, col)], m=t, n=D, tm=1024, tn=512, name="d_u2")
    d_wg = _mm(dgt, u2, m=FFN, n=D, k=t, ta=True, tm=256, tn=D, tk=4096, out_dtype=BF, name="d_w_gate")
    d_wu = _mm(dup, u2, m=FFN, n=D, k=t, ta=True, tm=256, tn=D, tk=4096, out_dtype=BF, name="d_w_up")
    dh1, dh1_b, d_norm_ffn = _rms_bwd(du2, h1, norm_ffn_g, dh2, tm=512, name="rms_ffn_bwd")
    d_wout = _mm(merged, dh1_b, m=D, n=D, k=t, ta=True, tm=256, tn=D, tk=4096, out_dtype=BF, name="d_w_out")

    def merge_bwd(prods, ex):
        (dm,), (ga, gb, ya_, yb_) = prods, ex
        sa, sb = _sig(ga), _sig(gb)
        dgate = jnp.concatenate([dm * ya_.astype(F32) * sa * (1.0 - sa),
                                 dm * yb_.astype(F32) * sb * (1.0 - sb)], axis=1)
        return dm * sa, dm * sb, dgate

    ffn_grads = (d_wg, d_wu, d_wd)
    (dya, dyb, dgates), got = _fmm(
        [dh1_b], [(0, wout, True)], gate_tiles + [(ya, D, first), (yb, D, first)], merge_bwd,
        [(BF, D, D, first), (BF, D, D, first), (BF, 2 * D, 2 * D, first)],
        m=t, n=D, tm=512, tn=D, name="d_merge", comm=_pair_comm(ffn_grads))
    pair_ffn = [_pair_add(g, r, core, name="pair_add_ffn%d" % i) for i, (g, r) in enumerate(zip(ffn_grads, got))]
    dy_attn = _mm(dya, wba, m=t, n=D, k=D, tb=True, out_dtype=BF, name="d_y_attn", **big)
    dy_hgrn = _mm(dyb, wbh, m=t, n=D, k=D, tb=True, name="d_y_hgrn", **big)
    d_wba = _mm(y_attn, dya, m=D, n=D, k=t, ta=True, tm=256, tn=D, tk=4096, out_dtype=BF, name="d_w_ba")
    d_wbh = _mm(y_hgrn, dyb, m=D, n=D, k=t, ta=True, tm=256, tn=D, tk=4096, out_dtype=BF, name="d_w_bh")
    sq_grads = (d_wba, d_wbh, d_wout)
    (dq, dkv, d_sinks), (parts_ffn, *got) = _attn_bwd(
        q, kv, sinks, dy_attn, t=t, comm=_both(_chip_comm(pair_ffn), _pair_comm(sq_grads)))
    pair_sq = [_pair_add(g, r, core, name="pair_add_sq%d" % i) for i, (g, r) in enumerate(zip(sq_grads, got))]
    (dh4, d_logits, d_hgrn_norm), (parts_sq,) = _hgrn_bwd(h4, logits, hgrn_norm_g, o_pre, states, dy_hgrn,
                                                           t=t, comm=_chip_comm(pair_sq))
    dps = (dq, dkv, dh4, dgates)
    d_win_t, d_b_in = _inproj_bwd_w(dps, u1, t=t)
    half0, got_in = _inproj_bwd_x(dps, win_t, x, norm_mix_g, dh1, t=t, part=0, comm=_pair_comm([d_win_t]))
    pair_in = _pair_add(d_win_t, got_in[0], core, name="pair_add_w_in")
    (grad_x, d_norm_mix), (parts_in,) = _inproj_bwd_x(dps, win_t, x, norm_mix_g, dh1, t=t, part=1, prev=half0,
                                                      comm=_chip_comm([pair_in]))

    small_grads = (d_norm_mix, d_b_in, d_sinks, d_logits, d_hgrn_norm, d_norm_ffn, d_norm_final)
    return loss_row, grad_x, (parts_in, parts_ffn, parts_sq), small_grads


def _pack_small(norm_mix, b_in, sinks, logits, hgrn_norm, norm_ffn, norm_final, extra=None):
    pad = lambda a, n: jnp.pad(a.reshape(1, -1), ((0, 0), (0, n - a.size)))
    rows = [norm_mix.reshape(1, D), hgrn_norm.reshape(1, D), norm_ffn.reshape(1, D), norm_final.reshape(1, D),
            logits.reshape(2, D), pad(sinks.reshape(-1)[:16], D),
            jnp.zeros((1, D), F32) if extra is None else pad(extra, D),
            pad(b_in, 8 * D).reshape(8, D)]
    return jnp.concatenate(rows, axis=0).astype(F32)


def _unpack_small(p):
    return dict(norm_mix_g=p[0:1], hgrn_norm_g=p[1:2], norm_ffn_g=p[2:3], norm_final_g=p[3],
                hgrn_lb_logits=p[4:6], attn_sinks=p[6:7, 0:16], extra=p[7],
                b_in=p[8:16].reshape(1, 8 * D)[:, :IN_W])


def kernel(x, norm_mix_g, w_in, b_in, attn_sinks, hgrn_lb_logits, hgrn_norm_g, w_branch_attn, w_branch_hgrn, w_out, norm_ffn_g, w_ffn_gate, w_ffn_up, w_ffn_down, norm_final_g, loss_target, m_norm_mix_g, m_w_in, m_b_in, m_attn_sinks, m_hgrn_lb_logits, m_hgrn_norm_g, m_w_branch_attn, m_w_branch_hgrn, m_w_out, m_norm_ffn_g, m_w_ffn_gate, m_w_ffn_up, m_w_ffn_down, m_norm_final_g, v_norm_mix_g, v_w_in, v_b_in, v_attn_sinks, v_hgrn_lb_logits, v_hgrn_norm_g, v_w_branch_attn, v_w_branch_hgrn, v_w_out, v_norm_ffn_g, v_w_ffn_gate, v_w_ffn_up, v_w_ffn_down, v_norm_final_g):
    shards = [w_in[0].T.astype(BF), w_ffn_gate[0].T.astype(BF), w_ffn_up[0].T.astype(BF),
              w_ffn_down[0].astype(BF), w_branch_attn[0].astype(BF), w_branch_hgrn[0].astype(BF),
              w_out[0].astype(BF)]
    loss_row, grad_x, grad_parts, small_grads = _step(
        x[0], loss_target[0], shards, norm_mix_g, b_in, attn_sinks, hgrn_lb_logits, hgrn_norm_g,
        norm_ffn_g, norm_final_g.reshape(1, D))

    d_norm_mix, d_b_in, d_sinks, d_logits, d_hgrn_norm, d_norm_ffn, d_norm_final = small_grads
    g_small = _pack_small(d_norm_mix, d_b_in, d_sinks[:, :16], d_logits, d_hgrn_norm, d_norm_ffn,
                          d_norm_final, extra=loss_row[0, 0:1])
    w_small = _pack_small(norm_mix_g, b_in, attn_sinks, hgrn_lb_logits, hgrn_norm_g, norm_ffn_g, norm_final_g)
    m_small = _pack_small(m_norm_mix_g, m_b_in, m_attn_sinks, m_hgrn_lb_logits, m_hgrn_norm_g, m_norm_ffn_g,
                          m_norm_final_g)
    v_small = _pack_small(v_norm_mix_g, v_b_in, v_attn_sinks, v_hgrn_lb_logits, v_hgrn_norm_g, v_norm_ffn_g,
                          v_norm_final_g)
    small = [_unpack_small(p) for p in _small_allreduce_adam(g_small, w_small, m_small, v_small)]
    loss = small[0]["extra"][0]

    names = ["w_in", "w_ffn_gate", "w_ffn_up", "w_ffn_down", "w_branch_attn", "w_branch_hgrn", "w_out"]
    w_full = dict(w_in=(w_in, m_w_in, v_w_in), w_ffn_gate=(w_ffn_gate, m_w_ffn_gate, v_w_ffn_gate),
                  w_ffn_up=(w_ffn_up, m_w_ffn_up, v_w_ffn_up), w_ffn_down=(w_ffn_down, m_w_ffn_down, v_w_ffn_down),
                  w_branch_attn=(w_branch_attn, m_w_branch_attn, v_w_branch_attn),
                  w_branch_hgrn=(w_branch_hgrn, m_w_branch_hgrn, v_w_branch_hgrn),
                  w_out=(w_out, m_w_out, v_w_out))
    parts_in, parts_ffn, parts_sq = grad_parts
    where = [(parts_in, 0), (parts_ffn, 0), (parts_ffn, 1), (parts_ffn, 2), (parts_sq, 0), (parts_sq, 1), (parts_sq, 2)]
    big = {}
    for i, name in enumerate(names):
        view = (lambda a: a[0].T) if i < 3 else (lambda a: a[0])
        back = (lambda a: a.T[None]) if i < 3 else (lambda a: a[None])
        wv, mv, vv = w_full[name]
        res = _adam(view(wv), where[i][0], where[i][1], view(mv), view(vv), name="adam_" + name)
        big[name] = [back(a) for a in res]

    order = ["norm_mix_g", "w_in", "b_in", "attn_sinks", "hgrn_lb_logits", "hgrn_norm_g", "w_branch_attn",
             "w_branch_hgrn", "w_out", "norm_ffn_g", "w_ffn_gate", "w_ffn_up", "w_ffn_down", "norm_final_g"]
    outs = [loss, grad_x[None]]
    for kind in range(4):
        for name in order:
            outs.append(big[name][kind] if name in big else small[kind][name])
    return tuple(outs)
```

```python
import math

import jax
import jax.numpy as jnp
from jax import lax
from jax.experimental import pallas as pl
from jax.experimental.pallas import tpu as pltpu

F32 = jnp.float32
BF = jnp.bfloat16
MESH = pl.DeviceIdType.MESH

D = 1024
HEAD = 64
N_PAIR = 8
BLK = 128
CH = 64
HG_SUB = 2
HG_HEADS = 8
HG_K = 128
FFN = 2816
IN_W = 7424
N_DEV = 8
N_CHIP = 4
EPS = 1e-6
NEG = -1e30
SCALE = 1.0 / math.sqrt(HEAD)
VMEM_LIMIT = 56 * 1024 * 1024
WT = 256

ADAM_LR, ADAM_B1, ADAM_B2, ADAM_EPS, ADAM_WD, ADAM_STEP = 0.001, 0.9, 0.999, 1e-08, 0.01, 10

SLAB_R = (IN_W // N_DEV, FFN // N_DEV, FFN // N_DEV, FFN // N_DEV, D // N_DEV, D // N_DEV, D // N_DEV)
SLAB_ROWS = sum(SLAB_R)
SLAB_OFF = tuple(sum(SLAB_R[:i]) for i in range(len(SLAB_R)))
N_W = len(SLAB_R)
GRP_OFF = (0, D // WT, (D + 256) // WT, (5 * D + 256) // WT)
GRP_N = (D // WT, 256 // WT, 4 * D // WT, 2 * D // WT)
SMALL_ROWS = 16


_NN = (((1,), (0,)), ((), ()))
_NT = (((1,), (1,)), ((), ()))
_TN = (((0,), (0,)), ((), ()))


def _pcall(body, **kw):
    return pl.pallas_call(body, **kw)


def _cp(sem=None, **kw):
    return pltpu.CompilerParams(dimension_semantics=sem, vmem_limit_bytes=VMEM_LIMIT, **kw)


def _sig(v):
    return 0.5 * jnp.tanh(0.5 * v) + 0.5


def _accum(ref, val, first):
    @pl.when(first)
    def _():
        ref[...] = val

    @pl.when(jnp.logical_not(first))
    def _():
        ref[...] += val


class _Comm:
    def __init__(self, ins, out_shapes, sem_shapes, phases):
        self.ins, self.out_shapes, self.sem_shapes, self.phases = list(ins), list(out_shapes), list(sem_shapes), phases


def _both(a, b):
    ni, no, ns = len(a.ins), len(a.out_shapes), len(a.sem_shapes)

    def of_a(fn):
        return lambda ins, outs, sems: fn(ins[:ni], outs[:no], sems[:ns])

    def of_b(fn):
        return lambda ins, outs, sems: fn(ins[ni:], outs[no:], sems[ns:])

    return _Comm(a.ins + b.ins, a.out_shapes + b.out_shapes, a.sem_shapes + b.sem_shapes,
                 [(f, of_a(fn)) for f, fn in a.phases] + [(f, of_b(fn)) for f, fn in b.phases])


def _host(body, comm, n_in, n_out, n_scr, nsteps, step_fn):
    if comm is None:
        return body
    ci, co = len(comm.ins), len(comm.out_shapes)

    def wrapped(*refs):
        p = 0
        ins, p = refs[p:p + n_in], p + n_in
        cins, p = refs[p:p + ci], p + ci
        outs, p = refs[p:p + n_out], p + n_out
        couts, p = refs[p:p + co], p + co
        scr, p = refs[p:p + n_scr], p + n_scr
        csems = refs[p:]
        step = step_fn()
        for frac, fn in comm.phases:
            if frac < 1.0:
                @pl.when(step == int(round(frac * (nsteps - 1))))
                def _(fn=fn):
                    fn(cins, couts, csems)
        body(*ins, *outs, *scr)
        for frac, fn in comm.phases:
            if frac >= 1.0:
                @pl.when(step == nsteps - 1)
                def _(fn=fn):
                    fn(cins, couts, csems)

    return wrapped


def _hosted_call(body, comm, args, *, name, grid, in_specs, out_specs, out_shape, scratch_shapes, sem,
                 nsteps, step_fn, aliases=None):
    n_in, n_out, n_scr = len(in_specs), len(out_specs), len(scratch_shapes)
    args = list(args)
    extra = {}
    if comm is not None:
        in_specs = list(in_specs) + [_hbm_spec()] * len(comm.ins)
        out_specs = list(out_specs) + [_hbm_spec()] * len(comm.out_shapes)
        out_shape = list(out_shape) + comm.out_shapes
        scratch_shapes = list(scratch_shapes) + comm.sem_shapes
        args += comm.ins
        extra = dict(has_side_effects=True)
    outs = _pcall(_host(body, comm, n_in, n_out, n_scr, nsteps, step_fn), name=name, grid=grid,
                  in_specs=in_specs, out_specs=out_specs, out_shape=out_shape, scratch_shapes=scratch_shapes,
                  input_output_aliases=aliases or {}, compiler_params=_cp(sem, **extra))(*args)
    return list(outs[:n_out]), list(outs[n_out:])


def _hbm_spec():
    return pl.BlockSpec(memory_space=pl.ANY)


def _mm(a, b, *, m, n, k, tm, tn, tk, ta=False, tb=False, out_dtype=F32, resid=None, name):
    tm, tn, tk = min(tm, m), min(tn, n), min(tk, k)
    gm, gn, gk = m // tm, n // tn, k // tk
    assert gm * tm == m and gn * tn == n and gk * tk == k, (name, m, n, k, tm, tn, tk)
    a_spec = (pl.BlockSpec((tk, tm), lambda i, j, l: (l, i)) if ta
              else pl.BlockSpec((tm, tk), lambda i, j, l: (i, l)))
    b_spec = (pl.BlockSpec((tn, tk), lambda i, j, l: (j, l)) if tb
              else pl.BlockSpec((tk, tn), lambda i, j, l: (l, j)))
    dims = (((0 if ta else 1,), (1 if tb else 0,)), ((), ()))
    ins, in_specs = [a, b], [a_spec, b_spec]
    if resid is not None:
        ins.append(resid)
        in_specs.append(pl.BlockSpec((tm, tn), lambda i, j, l: (i, j)))
    scratch = [pltpu.VMEM((tm, tn), F32)] if gk > 1 else []

    def body(*refs):
        it = iter(refs)
        a_ref, b_ref = next(it), next(it)
        resid_ref = next(it) if resid is not None else None
        o_ref = next(it)
        acc_ref = next(it) if gk > 1 else None
        l = pl.program_id(2)
        part = lax.dot_general(a_ref[...].astype(BF), b_ref[...].astype(BF), dims,
                               preferred_element_type=F32)

        def finish(acc):
            if resid_ref is not None:
                acc = acc + resid_ref[...].astype(F32)
            o_ref[...] = acc.astype(out_dtype)

        if gk == 1:
            finish(part)
        else:
            _accum(acc_ref, part, l == 0)

            @pl.when(l == gk - 1)
            def _():
                finish(acc_ref[...])

    return _pcall(body, name=name, grid=(gm, gn, gk), in_specs=in_specs,
                  out_specs=pl.BlockSpec((tm, tn), lambda i, j, l: (i, j)),
                  out_shape=jax.ShapeDtypeStruct((m, n), out_dtype), scratch_shapes=scratch,
                  compiler_params=_cp(("parallel", "parallel", "arbitrary")))(*ins)


def _fmm(lhs, rhs, extras, epilogue, outs, *, m, n, tm, tn, name, comm=None, vecs=(), sums=()):
    tm, tn = min(tm, m), min(tn, n)
    assert m % tm == 0 and n % tn == 0 and (not sums or tn == n), (name, m, n, tm, tn)
    in_specs, args = [], []
    for a in lhs:
        in_specs.append(pl.BlockSpec((tm, a.shape[1]), lambda i, j: (i, 0)))
        args.append(a)
    for li, b, tb in rhs:
        k = lhs[li].shape[1]
        in_specs.append(pl.BlockSpec((tn, k), lambda i, j: (j, 0)) if tb
                        else pl.BlockSpec((k, tn), lambda i, j: (0, j)))
        args.append(b)
    for arr, w, col in extras:
        in_specs.append(pl.BlockSpec((tm, w), lambda i, j, col=col: (i, col(j))))
        args.append(arr)
    for vec in vecs:
        in_specs.append(pl.BlockSpec((1, tn), lambda i, j: (0, j)))
        args.append(vec)
    out_specs = [pl.BlockSpec((tm, w), lambda i, j, col=col: (i, col(j))) for _, _, w, col in outs]
    out_shape = [jax.ShapeDtypeStruct((m, total), dt) for dt, total, _, _ in outs]
    for w in sums:
        out_specs.append(pl.BlockSpec((1, w), lambda i, j: (0, 0)))
        out_shape.append(jax.ShapeDtypeStruct((1, w), F32))
    nl, nr, ne, no = len(lhs), len(rhs), len(extras) + len(vecs), len(outs)

    def body(*refs):
        prods = []
        for r, (li, _, tb) in enumerate(rhs):
            prods.append(lax.dot_general(refs[li][...], refs[nl + r][...], _NT if tb else _NN,
                                         preferred_element_type=F32))
        vals = epilogue(prods, [ref[...] for ref in refs[nl + nr:nl + nr + ne]])
        o_refs = refs[nl + nr + ne:]
        for o_ref, v in zip(o_refs[:no], vals[:no]):
            o_ref[...] = v.astype(o_ref.dtype)
        for s_ref, v in zip(o_refs[no:], vals[no:]):
            _accum(s_ref, v, pl.program_id(0) == 0)

    gm, gn = m // tm, n // tn
    res, comm_res = _hosted_call(
        body, comm, args, name=name, grid=(gm, gn), in_specs=in_specs, out_specs=out_specs,
        out_shape=out_shape, scratch_shapes=[], sem=("arbitrary", "arbitrary"), nsteps=gm * gn,
        step_fn=lambda: pl.program_id(0) * gn + pl.program_id(1))
    return res if comm is None else (res, comm_res)


def _grp_of(i):
    return [jnp.logical_and(i >= GRP_OFF[g], i < GRP_OFF[g] + GRP_N[g]) for g in range(4)]


def _grp_idx(i, g):
    return jnp.clip(i - GRP_OFF[g], 0, GRP_N[g] - 1)


def _inproj_fwd(u, win_t, b_in, *, t, comm=None):
    n_tiles = IN_W // WT
    dims = (((1,), (1,)), ((), ()))
    dtypes = (BF, BF, F32, F32)

    def body(u_ref, w_ref, b_ref, *o_refs):
        i = pl.program_id(0)
        p = lax.dot_general(u_ref[...], w_ref[...], dims, preferred_element_type=F32) + b_ref[...]
        for g, pred in enumerate(_grp_of(i)):
            @pl.when(pred)
            def _(g=g):
                o_refs[g][...] = p.astype(dtypes[g])

    return _hosted_call(
        body, comm, (u, win_t, b_in), name="inproj_fwd", grid=(n_tiles,),
        in_specs=[pl.BlockSpec((t, D), lambda i: (0, 0)),
                  pl.BlockSpec((WT, D), lambda i: (i, 0)),
                  pl.BlockSpec((1, WT), lambda i: (0, i))],
        out_specs=[pl.BlockSpec((t, WT), lambda i, g=g: (0, _grp_idx(i, g))) for g in range(4)],
        out_shape=[jax.ShapeDtypeStruct((t, GRP_N[g] * WT), dtypes[g]) for g in range(4)],
        scratch_shapes=[], sem=("arbitrary",), nsteps=n_tiles, step_fn=lambda: pl.program_id(0))


def _inproj_bwd_x(dps, win_t, x, g, resid, *, t, part, prev=None, comm=None):
    n_tiles = IN_W // WT
    n_row = 4 if t >= 2048 else 2
    tm = t // n_row
    per = 1 if part == 0 else n_row - 1
    row = lambda i: part + i

    n_chunks = 8
    h_first, g_first = 2, 6
    sub = D // WT

    def w_block(l):
        return jnp.where(l == 0, GRP_OFF[0], jnp.where(l == 1, GRP_OFF[1], GRP_OFF[2] + sub * (l - h_first)))

    def body(d0, d1, d2, d3, w0, w1, w2, w3, x_ref, g_ref, r_ref, *rest):
        dg_prev = rest[0] if prev is not None else None
        o_ref, dg_ref, acc_ref = rest[-3], rest[-2], rest[-1]
        i, l = pl.program_id(0), pl.program_id(1)

        @pl.when(l == 1)
        def _():
            acc_ref[...] += jnp.dot(d1[...], w0[...], preferred_element_type=F32)

        w = jnp.concatenate([w0[...], w1[...], w2[...], w3[...]], axis=0)
        for pred, d_ref in ((l == 0, d0), (jnp.logical_and(l >= h_first, l < g_first), d2), (l >= g_first, d3)):
            @pl.when(pred)
            def _(d_ref=d_ref):
                _accum(acc_ref, jnp.dot(d_ref[...], w, preferred_element_type=F32), l == 0)

        @pl.when(l == n_chunks - 1)
        def _():
            xv = x_ref[...]
            r = lax.rsqrt(jnp.mean(xv * xv, axis=-1, keepdims=True) + EPS)
            xh = xv * r
            du = acc_ref[...]
            dxh = du * g_ref[...]
            o_ref[...] = r_ref[...] + r * (dxh - xh * jnp.mean(dxh * xh, axis=-1, keepdims=True))
            dg = jnp.sum(du * xh, axis=0, keepdims=True)
            if dg_prev is not None:
                dg = dg + jnp.where(i == 0, 1.0, 0.0) * dg_prev[...]
            _accum(dg_ref, dg, i == 0)

    rows = lambda w: pl.BlockSpec((tm, w), lambda i, l: (row(i), 0))
    in_specs = ([rows(D), rows(256),
                 pl.BlockSpec((tm, D), lambda i, l: (row(i), jnp.clip(l - h_first, 0, 3))),
                 pl.BlockSpec((tm, D), lambda i, l: (row(i), jnp.clip(l - g_first, 0, 1)))]
                + [pl.BlockSpec((WT, D), lambda i, l, o=o: (w_block(l) + o, 0)) for o in range(sub)]
                + [rows(D), pl.BlockSpec((1, D), lambda i, l: (0, 0)), rows(D)])
    args = list(dps) + [win_t] * sub + [x, g, resid]
    aliases = None
    if prev is not None:
        in_specs += [pl.BlockSpec((1, D), lambda i, l: (0, 0)), _hbm_spec()]
        args += [prev[1], prev[0]]
        aliases = {len(args) - 1: 0}
    return _hosted_call(
        body, comm, args, name="inproj_bwd_x%d" % part, grid=(per, n_chunks), in_specs=in_specs,
        out_specs=[rows(D), pl.BlockSpec((1, D), lambda i, l: (0, 0))],
        out_shape=[jax.ShapeDtypeStruct((t, D), F32), jax.ShapeDtypeStruct((1, D), F32)],
        scratch_shapes=[pltpu.VMEM((tm, D), F32)], sem=("arbitrary", "arbitrary"), nsteps=per * n_chunks,
        step_fn=lambda: pl.program_id(0) * n_chunks + pl.program_id(1), aliases=aliases)


def _inproj_bwd_w(dps, u, *, t):
    n_tiles = IN_W // WT
    dims = (((0,), (0,)), ((), ()))

    def body(d0, d1, d2, d3, u_ref, o_ref, db_ref):
        i = pl.program_id(0)
        uv = u_ref[...]
        for g, (pred, d_ref) in enumerate(zip(_grp_of(i), (d0, d1, d2, d3))):
            @pl.when(pred)
            def _(d_ref=d_ref):
                dv = d_ref[...]
                o_ref[...] = lax.dot_general(dv, uv, dims, preferred_element_type=F32).astype(BF)
                db_ref[...] = jnp.sum(dv.astype(F32), axis=0, keepdims=True)

    return _pcall(body, name="inproj_bwd_w", grid=(n_tiles,),
                  in_specs=[pl.BlockSpec((t, WT), lambda i, g=g: (0, _grp_idx(i, g))) for g in range(4)]
                  + [pl.BlockSpec((t, D), lambda i: (0, 0))],
                  out_specs=[pl.BlockSpec((WT, D), lambda i: (i, 0)),
                             pl.BlockSpec((1, WT), lambda i: (0, i))],
                  out_shape=[jax.ShapeDtypeStruct((IN_W, D), BF), jax.ShapeDtypeStruct((1, IN_W), F32)],
                  compiler_params=_cp(("arbitrary",)))(*dps, u)


def _row_spec(tm, width, col=0):
    return pl.BlockSpec((tm, width), lambda i: (i, col))


def _vec_spec(width):
    return pl.BlockSpec((1, width), lambda i: (0, 0))


def _rms_fwd(x, g, *, tm, name, comm=None):
    t = x.shape[0]
    tm = min(tm, t)

    def body(x_ref, g_ref, u_ref):
        xv = x_ref[...]
        r = lax.rsqrt(jnp.mean(xv * xv, axis=-1, keepdims=True) + EPS)
        u_ref[...] = (xv * r * g_ref[...]).astype(BF)

    (u,), comm_res = _hosted_call(
        body, comm, (x, g), name=name, grid=(t // tm,), in_specs=[_row_spec(tm, D), _vec_spec(D)],
        out_specs=[_row_spec(tm, D)], out_shape=[jax.ShapeDtypeStruct((t, D), BF)], scratch_shapes=[],
        sem=("arbitrary",), nsteps=t // tm, step_fn=lambda: pl.program_id(0))
    return u if comm is None else (u, comm_res)


def _rms_bwd(du, x, g, resid, *, tm, name):
    t = x.shape[0]
    tm = min(tm, t)

    def body(du_ref, x_ref, g_ref, r_ref, dx_ref, dxb_ref, dg_ref):
        xv = x_ref[...]
        r = lax.rsqrt(jnp.mean(xv * xv, axis=-1, keepdims=True) + EPS)
        xh = xv * r
        duv = du_ref[...]
        dxh = duv * g_ref[...]
        dx = r_ref[...] + r * (dxh - xh * jnp.mean(dxh * xh, axis=-1, keepdims=True))
        dx_ref[...] = dx
        dxb_ref[...] = dx.astype(BF)
        _accum(dg_ref, jnp.sum(duv * xh, axis=0, keepdims=True), pl.program_id(0) == 0)

    return _pcall(body, name=name, grid=(t // tm,),
                  in_specs=[_row_spec(tm, D), _row_spec(tm, D), _vec_spec(D), _row_spec(tm, D)],
                  out_specs=[_row_spec(tm, D), _row_spec(tm, D), _vec_spec(D)],
                  out_shape=[jax.ShapeDtypeStruct((t, D), F32), jax.ShapeDtypeStruct((t, D), BF),
                             jax.ShapeDtypeStruct((1, D), F32)],
                  compiler_params=_cp(("arbitrary",)))(du, x, g, resid)


def _attn_kv_tiles(kprev, kcur):
    kv = jnp.concatenate([kprev, kcur], axis=0).astype(F32)
    lo = lax.broadcasted_iota(jnp.int32, (2 * BLK, 128), 1) < HEAD
    tiles = []
    for part in (kv[:, 0:128], kv[:, 128:256]):
        rolled = pltpu.roll(part, HEAD, 1)
        z = jnp.zeros_like(part)
        tiles.append(((jnp.where(lo, part, z).astype(BF), jnp.where(lo, z, rolled).astype(BF)),
                      (jnp.where(lo, rolled, z).astype(BF), jnp.where(lo, z, part).astype(BF))))
    k_t, v_t = tiles
    return [(jnp.concatenate(k_t[h], axis=0), jnp.concatenate(v_t[h], axis=0)) for h in range(2)]


def _attn_mask(i):
    qi = lax.broadcasted_iota(jnp.int32, (BLK, 2 * BLK), 0)
    kj = lax.broadcasted_iota(jnp.int32, (BLK, 2 * BLK), 1)
    first_key = jnp.where(i == 0, BLK, 0)
    in_prev = jnp.logical_and(jnp.logical_and(kj < BLK, kj > qi), kj >= first_key)
    in_cur = jnp.logical_and(kj >= BLK, kj - BLK <= qi)
    return jnp.logical_or(in_prev, in_cur)


def _attn_probs(s, sink, valid):
    s = jnp.where(valid, s * SCALE, NEG)
    mx = jnp.maximum(jnp.max(s, axis=-1, keepdims=True), sink)
    e = jnp.exp(s - mx)
    es = jnp.exp(sink - mx)
    inv = 1.0 / (jnp.sum(e, axis=-1, keepdims=True) + es)
    return e * inv, es * inv


_KEYS = 2 * BLK


def _pair(ref, j):
    return ref[:, j * 128:(j + 1) * 128]


def _attn_fwd(q, kv, sinks, *, t, comm=None):
    nb = t // BLK

    def body(sink_ref, q_ref, kp_ref, kc_ref, o_ref):
        valid = _attn_mask(pl.program_id(0))
        tiles = _attn_kv_tiles(kp_ref[...], kc_ref[...])
        s = [lax.dot_general(_pair(q_ref, j), tiles[j // 4][0], _NT, preferred_element_type=F32)
             for j in range(N_PAIR)]
        p = []
        for j in range(N_PAIR):
            pe, _ = _attn_probs(s[j][:, 0:_KEYS], sink_ref[0, 2 * j], valid)
            po, _ = _attn_probs(s[j][:, _KEYS:2 * _KEYS], sink_ref[0, 2 * j + 1], valid)
            p.append(jnp.concatenate([pe.astype(BF), po.astype(BF)], axis=1))
        for j in range(N_PAIR):
            o_ref[:, j * 128:(j + 1) * 128] = jnp.dot(p[j], tiles[j // 4][1],
                                                      preferred_element_type=F32).astype(BF)

    return _hosted_call(
        body, comm, (sinks, q, kv, kv), name="attn_fwd", grid=(nb,),
        in_specs=[pl.BlockSpec(memory_space=pltpu.SMEM),
                  pl.BlockSpec((BLK, D), lambda i: (i, 0)),
                  pl.BlockSpec((BLK, 256), lambda i: (jnp.maximum(i - 1, 0), 0)),
                  pl.BlockSpec((BLK, 256), lambda i: (i, 0))],
        out_specs=[pl.BlockSpec((BLK, D), lambda i: (i, 0))],
        out_shape=[jax.ShapeDtypeStruct((t, D), BF)],
        scratch_shapes=[], sem=("arbitrary",), nsteps=nb, step_fn=lambda: pl.program_id(0))


def _attn_bwd(q, kv, sinks, do, *, t, comm=None):
    nb = t // BLK
    last = nb - 1

    def body(sink_ref, q_ref, kp_ref, kc_ref, do_ref, dq_ref, dkv_ref, ds_ref, carry_ref):
        i = pl.program_id(0)

        @pl.when(i == 0)
        def _():
            ds_ref[...] = jnp.zeros_like(ds_ref)
            carry_ref[...] = jnp.zeros_like(carry_ref)

        @pl.when(i < nb)
        def _():
            valid = _attn_mask(i)
            tiles = _attn_kv_tiles(kp_ref[...], kc_ref[...])
            lane1 = lax.broadcasted_iota(jnp.int32, (1, 128), 1)
            dsink = jnp.zeros((1, 128), F32)
            s = [lax.dot_general(_pair(q_ref, j), tiles[j // 4][0], _NT, preferred_element_type=F32)
                 for j in range(N_PAIR)]
            dp = [lax.dot_general(_pair(do_ref, j), tiles[j // 4][1], _NT, preferred_element_type=F32)
                  for j in range(N_PAIR)]
            p_all, ds_all = [], []
            for j in range(N_PAIR):
                halves = []
                for par in range(2):
                    cols = slice(par * _KEYS, (par + 1) * _KEYS)
                    p, ps = _attn_probs(s[j][:, cols], sink_ref[0, 2 * j + par], valid)
                    dpj = dp[j][:, cols]
                    dd = jnp.sum(p * dpj, axis=-1, keepdims=True)
                    dsink = dsink + jnp.where(lane1 == 2 * j + par,
                                              -jnp.sum(ps * dd, axis=0, keepdims=True), 0.0)
                    halves.append((p.astype(BF), (p * (dpj - dd)).astype(BF)))
                p_all.append(jnp.concatenate([halves[0][0], halves[1][0]], axis=1))
                ds_all.append(jnp.concatenate([halves[0][1], halves[1][1]], axis=1))
            for j in range(N_PAIR):
                dq_ref[:, j * 128:(j + 1) * 128] = (
                    jnp.dot(ds_all[j], tiles[j // 4][0], preferred_element_type=F32) * SCALE).astype(BF)
            ds_ref[...] += dsink
            gk, gv = [], []
            for h in range(2):
                grp = range(4 * h, 4 * h + 4)
                q_rows = jnp.concatenate([_pair(q_ref, j) for j in grp], axis=0)
                do_rows = jnp.concatenate([_pair(do_ref, j) for j in grp], axis=0)
                g_k = lax.dot_general(jnp.concatenate([ds_all[j] for j in grp], axis=0), q_rows, _TN,
                                      preferred_element_type=F32)
                g_v = lax.dot_general(jnp.concatenate([p_all[j] for j in grp], axis=0), do_rows, _TN,
                                      preferred_element_type=F32)
                gk.append((g_k[0:_KEYS], g_k[_KEYS:2 * _KEYS]))
                gv.append((g_v[0:_KEYS], g_v[_KEYS:2 * _KEYS]))
            lo = lax.broadcasted_iota(jnp.int32, (2 * BLK, 128), 1) < HEAD
            zero = jnp.zeros((2 * BLK, 128), F32)

            def unpad(g):
                return (jnp.where(lo, g[0][0] + pltpu.roll(g[0][1], HEAD, 1), zero)
                        + jnp.where(lo, zero, pltpu.roll(g[1][0], HEAD, 1) + g[1][1]))

            dk = unpad(gk) * SCALE
            dv = unpad(gv)
            dkv_ref[:, 0:128] = (carry_ref[:, 0:128] + dk[0:BLK]).astype(BF)
            dkv_ref[:, 128:256] = (carry_ref[:, 128:256] + dv[0:BLK]).astype(BF)
            carry_ref[:, 0:128] = dk[BLK:2 * BLK]
            carry_ref[:, 128:256] = dv[BLK:2 * BLK]

        @pl.when(i == nb)
        def _():
            dkv_ref[...] = carry_ref[...].astype(BF)

    return _hosted_call(
        body, comm, (sinks, q, kv, kv, do), name="attn_bwd", grid=(nb + 1,),
        in_specs=[pl.BlockSpec(memory_space=pltpu.SMEM),
                  pl.BlockSpec((BLK, D), lambda i: (jnp.minimum(i, last), 0)),
                  pl.BlockSpec((BLK, 256), lambda i: (jnp.clip(i - 1, 0, last), 0)),
                  pl.BlockSpec((BLK, 256), lambda i: (jnp.minimum(i, last), 0)),
                  pl.BlockSpec((BLK, D), lambda i: (jnp.minimum(i, last), 0))],
        out_specs=[pl.BlockSpec((BLK, D), lambda i: (jnp.minimum(i, last), 0)),
                   pl.BlockSpec((BLK, 256), lambda i: (jnp.maximum(i - 1, 0), 0)),
                   pl.BlockSpec((1, 128), lambda i: (0, 0))],
        out_shape=[jax.ShapeDtypeStruct((t, D), BF), jax.ShapeDtypeStruct((t, 256), BF),
                   jax.ShapeDtypeStruct((1, 128), F32)],
        scratch_shapes=[pltpu.VMEM((BLK, 256), F32)], sem=("arbitrary",), nsteps=nb + 1,
        step_fn=lambda: pl.program_id(0))


def _split3(v):
    h = v.astype(BF)
    r = v - h.astype(F32)
    m = r.astype(BF)
    lo = (r - m.astype(F32)).astype(BF)
    return jnp.concatenate([h, m, lo], axis=1)


def _apply01(mat, v):
    n = v.shape[1]
    r = jnp.dot(mat, _split3(v), preferred_element_type=F32)
    return r[:, 0:n] + r[:, n:2 * n] + r[:, 2 * n:3 * n]


def _hgrn_gates(hq, hf, lb):
    sq = _sig(hq)
    sg = _sig(hf)
    f = lb + (1.0 - lb) * sg
    return hq * sq, (1.0 - lb) * (1.0 - sg), jnp.log(f), sq, sg, f


def _tri(upper):
    r = lax.broadcasted_iota(jnp.int32, (CH, CH), 0)
    c = lax.broadcasted_iota(jnp.int32, (CH, CH), 1)
    return (c >= r) if upper else (c <= r)


def _lb_from_logits(lg_ref):
    return 1.0 / (1.0 + jnp.exp(lg_ref[1:2, :] - lg_ref[0:1, :]))


def _hgrn_fwd(h4, logits, norm_g, *, t, comm=None):
    nc = t // CH
    nt_dims = (((1,), (1,)), ((), ()))
    tn_dims = (((0,), (0,)), ((), ()))

    def body(h_ref, lg_ref, ng_ref, y_ref, o_ref, st_ref, s_scr, b_scr, qa_s, ka_s, qb_s, kb_s, v_s):
        @pl.when(pl.program_id(0) == 0)
        def _():
            s_scr[...] = jnp.zeros_like(s_scr)

        heads = [slice(h * HG_K, (h + 1) * HG_K) for h in range(HG_HEADS)]
        causal = _tri(False)
        lb = _lb_from_logits(lg_ref)
        for c in range(HG_SUB):
            rows = slice(c * CH, (c + 1) * CH)
            q, k, g, _, _, _ = _hgrn_gates(h_ref[rows, 0:D], h_ref[rows, D:2 * D], lb)
            b_scr[...] = _apply01(jnp.where(causal, 1.0, 0.0).astype(BF), g)
            b = b_scr[...]
            b_mid = b_scr[CH // 2 - 1:CH // 2, :]
            b_last = b_scr[CH - 1:CH, :]
            qa_s[...] = (q * jnp.exp(b - b_mid)).astype(BF)
            ka_s[...] = (k * jnp.exp(b_mid - b)).astype(BF)
            qb_s[...] = (q * jnp.exp(b)).astype(BF)
            kb_s[...] = (k * jnp.exp(b_last - b)).astype(BF)
            v_s[...] = h_ref[rows, 2 * D:3 * D].astype(BF)
            dec = jnp.exp(b_last)
            st_ref[c] = s_scr[...].astype(BF)
            a = [jnp.where(causal, lax.dot_general(qa_s[:, sl], ka_s[:, sl], nt_dims, preferred_element_type=F32),
                           0.0).astype(BF) for sl in heads]
            for h, sl in enumerate(heads):
                o_ref[rows, sl] = (jnp.dot(a[h], v_s[:, sl], preferred_element_type=F32)
                                   + lax.dot_general(qb_s[:, sl], s_scr[h].astype(BF), nt_dims,
                                                     preferred_element_type=F32))
            for h, sl in enumerate(heads):
                s_scr[h] = dec[:, sl] * s_scr[h] + lax.dot_general(v_s[:, sl], kb_s[:, sl], tn_dims,
                                                                   preferred_element_type=F32)
            for h, sl in enumerate(heads):
                o = o_ref[rows, sl]
                on = o * lax.rsqrt(jnp.mean(o * o, axis=-1, keepdims=True) + EPS)
                gate = _sig(h_ref[rows, 3 * D + h * HG_K:3 * D + (h + 1) * HG_K])
                y_ref[rows, sl] = (on * ng_ref[:, sl] * gate).astype(BF)

    half = lambda: pltpu.VMEM((CH, D), BF)
    blk = HG_SUB * CH
    return _hosted_call(
        body, comm, (h4, logits, norm_g), name="hgrn_fwd", grid=(nc // HG_SUB,),
        in_specs=[pl.BlockSpec((blk, 4 * D), lambda n: (n, 0)),
                  pl.BlockSpec((2, D), lambda n: (0, 0)),
                  pl.BlockSpec((1, D), lambda n: (0, 0))],
        out_specs=[pl.BlockSpec((blk, D), lambda n: (n, 0)),
                   pl.BlockSpec((blk, D), lambda n: (n, 0)),
                   pl.BlockSpec((HG_SUB, HG_HEADS, HG_K, HG_K), lambda n: (n, 0, 0, 0))],
        out_shape=[jax.ShapeDtypeStruct((t, D), BF), jax.ShapeDtypeStruct((t, D), F32),
                   jax.ShapeDtypeStruct((nc, HG_HEADS, HG_K, HG_K), BF)],
        scratch_shapes=[pltpu.VMEM((HG_HEADS, HG_K, HG_K), F32), pltpu.VMEM((CH, D), F32),
                        half(), half(), half(), half(), half()],
        sem=("arbitrary",), nsteps=nc // HG_SUB, step_fn=lambda: pl.program_id(0))


def _hgrn_bwd(h4, logits, norm_g, o_pre, states, dy, *, t, comm=None):
    nc = t // CH
    nt_dims = (((1,), (1,)), ((), ()))
    tn_dims = (((0,), (0,)), ((), ()))

    def body(h_ref, lg_ref, ng_ref, o_ref, st_ref, dy_ref, dh_ref, dlg_ref, dng_ref, ds_scr, dlb_scr,
             b_scr, tail_s, e_qa, e_ka, e_qb, e_kb, q_s, k_s, dqa_s, dka_s, dqb_s, dkb_s,
             qa_s, ka_s, qb_s, kb_s, v_s, do_s):
        n = pl.program_id(0)

        @pl.when(n == 0)
        def _():
            ds_scr[...] = jnp.zeros_like(ds_scr)
            dlb_scr[...] = jnp.zeros_like(dlb_scr)
            dng_ref[...] = jnp.zeros_like(dng_ref)

        heads = [slice(h * HG_K, (h + 1) * HG_K) for h in range(HG_HEADS)]
        lb = _lb_from_logits(lg_ref)
        causal = _tri(False)

        def chunk(c):
            rows = slice(c * CH, (c + 1) * CH)
            q, k, g, _, _, _ = _hgrn_gates(h_ref[rows, 0:D], h_ref[rows, D:2 * D], lb)
            b_scr[...] = _apply01(jnp.where(causal, 1.0, 0.0).astype(BF), g)
            b = b_scr[...]
            b_mid = b_scr[CH // 2 - 1:CH // 2, :]
            b_last = b_scr[CH - 1:CH, :]
            q_s[...] = q
            k_s[...] = k
            for e_ref, s_ref, base, expo in ((e_qa, qa_s, q, b - b_mid), (e_ka, ka_s, k, b_mid - b),
                                             (e_qb, qb_s, q, b), (e_kb, kb_s, k, b_last - b)):
                e = jnp.exp(expo)
                e_ref[...] = e
                s_ref[...] = (base * e).astype(BF)
            v_s[...] = h_ref[rows, 2 * D:3 * D].astype(BF)
            dec = jnp.exp(b_last)
            for h, sl in enumerate(heads):
                gcol = slice(3 * D + h * HG_K, 3 * D + (h + 1) * HG_K)
                ngh = ng_ref[:, sl]
                sgate = _sig(h_ref[rows, gcol])
                o = o_ref[rows, sl]
                r = lax.rsqrt(jnp.mean(o * o, axis=-1, keepdims=True) + EPS)
                on = o * r
                dyh = dy_ref[rows, sl]
                dh_ref[rows, gcol] = (dyh * on * ngh * sgate * (1.0 - sgate)).astype(BF)
                dng_ref[:, sl] += jnp.sum(dyh * on * sgate, axis=0, keepdims=True)
                don = dyh * ngh * sgate
                do_s[:, sl] = (r * (don - on * jnp.mean(don * on, axis=-1, keepdims=True))).astype(BF)
            a = [jnp.where(causal, lax.dot_general(qa_s[:, sl], ka_s[:, sl], nt_dims, preferred_element_type=F32),
                           0.0).astype(BF) for sl in heads]
            da = [jnp.where(causal, lax.dot_general(do_s[:, sl], v_s[:, sl], nt_dims, preferred_element_type=F32),
                            0.0).astype(BF) for sl in heads]
            for h, sl in enumerate(heads):
                dh_ref[rows, 2 * D + h * HG_K:2 * D + (h + 1) * HG_K] = (
                    lax.dot_general(a[h], do_s[:, sl], tn_dims, preferred_element_type=F32)
                    + lax.dot_general(kb_s[:, sl], ds_scr[h].astype(BF), nt_dims, preferred_element_type=F32)
                ).astype(BF)
            for h, sl in enumerate(heads):
                dqa_s[:, sl] = jnp.dot(da[h], ka_s[:, sl], preferred_element_type=F32)
            for h, sl in enumerate(heads):
                dka_s[:, sl] = lax.dot_general(da[h], qa_s[:, sl], tn_dims, preferred_element_type=F32)
            for h, sl in enumerate(heads):
                dqb_s[:, sl] = jnp.dot(do_s[:, sl], st_ref[c, h], preferred_element_type=F32)
            for h, sl in enumerate(heads):
                dkb_s[:, sl] = jnp.dot(v_s[:, sl], ds_scr[h].astype(BF), preferred_element_type=F32)
            for h, sl in enumerate(heads):
                tail_s[:, sl] = jnp.sum(dec[:, sl] * st_ref[c, h].astype(F32) * ds_scr[h], axis=0, keepdims=True)
            for h, sl in enumerate(heads):
                ds_scr[h] = (lax.dot_general(do_s[:, sl], qb_s[:, sl], tn_dims, preferred_element_type=F32)
                             + dec[:, sl] * ds_scr[h])
            qv, kv = q_s[...], k_s[...]
            dqa, dka, dqb, dkb = dqa_s[...], dka_s[...], dqb_s[...], dkb_s[...]
            eqa, eka, eqb, ekb = e_qa[...], e_ka[...], e_qb[...], e_kb[...]
            dkb_kb = dkb * (kv * ekb)
            db_last = jnp.sum(dkb_kb, axis=0, keepdims=True) + tail_s[...]
            last_row = lax.broadcasted_iota(jnp.int32, (CH, D), 0) == CH - 1
            db = (dqa * (qv * eqa) - dka * (kv * eka) + dqb * (qv * eqb) - dkb_kb
                  + jnp.where(last_row, db_last, 0.0))
            dg = _apply01(jnp.where(_tri(True), 1.0, 0.0).astype(BF), db)
            dq = dqa * eqa + dqb * eqb
            dk = dka * eka + dkb * ekb
            hq = h_ref[rows, 0:D]
            _, _, _, sq, sg, f = _hgrn_gates(hq, h_ref[rows, D:2 * D], lb)
            dh_ref[rows, 0:D] = (dq * sq * (1.0 + hq * (1.0 - sq))).astype(BF)
            dfk = dg / f - dk
            dh_ref[rows, D:2 * D] = ((1.0 - lb) * dfk * sg * (1.0 - sg)).astype(BF)
            dlb_scr[...] += jnp.sum((1.0 - sg) * dfk, axis=0, keepdims=True)

        for c in reversed(range(HG_SUB)):
            chunk(c)

        @pl.when(n == nc // HG_SUB - 1)
        def _():
            dl0 = dlb_scr[...] * lb * (1.0 - lb)
            dlg_ref[0:1, :] = dl0
            dlg_ref[1:2, :] = -dl0

    steps = nc // HG_SUB
    blk = HG_SUB * CH
    rev = lambda n: (steps - 1 - n, 0)
    return _hosted_call(
        body, comm, (h4, logits, norm_g, o_pre, states, dy), name="hgrn_bwd", grid=(steps,),
        in_specs=[pl.BlockSpec((blk, 4 * D), rev),
                  pl.BlockSpec((2, D), lambda n: (0, 0)),
                  pl.BlockSpec((1, D), lambda n: (0, 0)),
                  pl.BlockSpec((blk, D), rev),
                  pl.BlockSpec((HG_SUB, HG_HEADS, HG_K, HG_K), lambda n: (steps - 1 - n, 0, 0, 0)),
                  pl.BlockSpec((blk, D), rev)],
        out_specs=[pl.BlockSpec((blk, 4 * D), rev),
                   pl.BlockSpec((2, D), lambda n: (0, 0)),
                   pl.BlockSpec((1, D), lambda n: (0, 0))],
        out_shape=[jax.ShapeDtypeStruct((t, 4 * D), BF), jax.ShapeDtypeStruct((2, D), F32),
                   jax.ShapeDtypeStruct((1, D), F32)],
        scratch_shapes=([pltpu.VMEM((HG_HEADS, HG_K, HG_K), F32), pltpu.VMEM((1, D), F32),
                         pltpu.VMEM((CH, D), F32), pltpu.VMEM((1, D), F32)]
                        + [pltpu.VMEM((CH, D), F32)] * 10 + [pltpu.VMEM((CH, D), BF)] * 6),
        sem=("arbitrary",), nsteps=steps, step_fn=lambda: pl.program_id(0))


def _place():
    x, y, c = lax.axis_index("x"), lax.axis_index("y"), lax.axis_index("c")
    return x, y, c, [(1 - x, y), (x, 1 - y), (1 - x, 1 - y)]


def _gather_comm(shards, mid):
    n = len(shards)
    r = [s.shape[0] for s in shards]

    def tools(ins, outs, sems):
        send_sems, recv_sems, local_sems = sems
        x, y, c, chips = _place()
        me, sib = (x, y, c), (x, y, 1 - c)

        def rows(w, dev):
            return outs[w].at[pl.ds((4 * dev[0] + 2 * dev[1] + dev[2]) * r[w], r[w]), :]

        def copy(kind, w, block, to, src=None):
            return pltpu.make_async_remote_copy(
                src_ref=rows(w, block) if src is None else src, dst_ref=rows(w, block),
                send_sem=send_sems.at[kind], recv_sem=recv_sems.at[kind], device_id=to, device_id_type=MESH)

        def all_of(kind):
            whole = outs[0].at[pl.ds(0, sum(r)), :]
            return pltpu.make_async_remote_copy(
                src_ref=whole, dst_ref=whole, send_sem=send_sems.at[kind], recv_sem=recv_sems.at[kind],
                device_id=me, device_id_type=MESH)

        mine = [pltpu.make_async_copy(ins[w], rows(w, me), local_sems.at[w]) for w in range(n)]
        return c, chips, me, sib, copy, all_of, mine

    def start(ins, outs, sems):
        c, chips, me, sib, copy, _, mine = tools(ins, outs, sems)
        for cp in mine:
            cp.start()
        for w in range(n):
            copy(0, w, me, sib, src=ins[w]).start()
            for j, chip in enumerate(chips):
                copy(1 + j, w, me, (*chip, c), src=ins[w]).start()

    def pass_on(ins, outs, sems):
        c, chips, _, sib, copy, all_of, _ = tools(ins, outs, sems)
        for j, chip in enumerate(chips):
            all_of(1 + j).wait_recv()
            for w in range(n):
                copy(4 + j, w, (*chip, c), sib).start()

    def finish(ins, outs, sems):
        _, _, _, _, _, all_of, mine = tools(ins, outs, sems)
        all_of(0).wait_recv()
        for j in range(3):
            all_of(4 + j).wait_recv()
        for kind in range(7):
            all_of(kind).wait_send()
        for cp in mine:
            cp.wait()

    return _Comm(shards, [jax.ShapeDtypeStruct((N_DEV * rw, D), BF) for rw in r],
                 [pltpu.SemaphoreType.DMA((7,)), pltpu.SemaphoreType.DMA((7,)), pltpu.SemaphoreType.DMA((n,))],
                 [(0.0, start), (mid, pass_on), (1.0, finish)])


def _pair_comm(grads):
    n = len(grads)
    r = [g.shape[0] // N_DEV for g in grads]

    def start(ins, outs, sems):
        send_sems, recv_sems = sems
        x, y, c, _ = _place()
        for w in range(n):
            for a in range(N_CHIP):
                pltpu.make_async_remote_copy(
                    src_ref=ins[w].at[pl.ds((2 * a + 1 - c) * r[w], r[w]), :], dst_ref=outs[w].at[a],
                    send_sem=send_sems.at[w], recv_sem=recv_sems.at[w],
                    device_id=(x, y, 1 - c), device_id_type=MESH).start()

    def finish(ins, outs, sems):
        send_sems, recv_sems = sems
        x, y, c, _ = _place()
        for w in range(n):
            pltpu.make_async_remote_copy(
                src_ref=outs[w], dst_ref=outs[w], send_sem=send_sems.at[w], recv_sem=recv_sems.at[w],
                device_id=(x, y, c), device_id_type=MESH).wait()

    return _Comm(grads, [jax.ShapeDtypeStruct((N_CHIP, rw, D), BF) for rw in r],
                 [pltpu.SemaphoreType.DMA((n,)), pltpu.SemaphoreType.DMA((n,))],
                 [(0.0, start), (1.0, finish)])


def _pair_add(grad, got, core, *, name):
    r = got.shape[1]

    def body(c_ref, g_ref, got_ref, o_ref):
        o_ref[0] = (g_ref[...].astype(F32) + got_ref[0].astype(F32)).astype(BF)

    grid_spec = pltpu.PrefetchScalarGridSpec(
        num_scalar_prefetch=1, grid=(N_CHIP,),
        in_specs=[pl.BlockSpec((r, D), lambda a, c_ref: (2 * a + c_ref[0], 0)),
                  pl.BlockSpec((1, r, D), lambda a, c_ref: (a, 0, 0))],
        out_specs=pl.BlockSpec((1, r, D), lambda a, c_ref: (a, 0, 0)))
    return _pcall(body, name=name, grid_spec=grid_spec,
                  out_shape=jax.ShapeDtypeStruct((N_CHIP, r, D), BF),
                  compiler_params=_cp(("parallel",)))(core, grad, got)


def _chip_comm(pair_sums):
    n = len(pair_sums)
    r = [p.shape[1] for p in pair_sums]
    off = [sum(r[:w]) for w in range(n)]

    def tools(ins, outs, sems):
        send_sems, recv_sems, local_sems = sems
        x, y, c, chips = _place()
        my_chip = 2 * x + y

        def slot(w):
            return outs[0].at[my_chip, pl.ds(off[w], r[w]), :]

        own = [pltpu.make_async_copy(ins[w].at[my_chip], slot(w), local_sems.at[w]) for w in range(n)]
        return x, y, c, chips, my_chip, slot, own, send_sems, recv_sems

    def start(ins, outs, sems):
        x, y, c, chips, my_chip, slot, own, send_sems, recv_sems = tools(ins, outs, sems)
        for cp in own:
            cp.start()
        for j, chip in enumerate(chips):
            for w in range(n):
                pltpu.make_async_remote_copy(
                    src_ref=ins[w].at[2 * chip[0] + chip[1]], dst_ref=slot(w), send_sem=send_sems.at[j],
                    recv_sem=recv_sems.at[j], device_id=(*chip, c), device_id_type=MESH).start()

    def finish(ins, outs, sems):
        x, y, c, chips, my_chip, slot, own, send_sems, recv_sems = tools(ins, outs, sems)
        whole = outs[0].at[my_chip]
        for j in range(3):
            pltpu.make_async_remote_copy(
                src_ref=whole, dst_ref=whole, send_sem=send_sems.at[j], recv_sem=recv_sems.at[j],
                device_id=(x, y, c), device_id_type=MESH).wait()
        for cp in own:
            cp.wait()

    return _Comm(pair_sums, [jax.ShapeDtypeStruct((N_CHIP, sum(r), D), BF)],
                 [pltpu.SemaphoreType.DMA((3,)), pltpu.SemaphoreType.DMA((3,)), pltpu.SemaphoreType.DMA((n,))],
                 [(0.0, start), (1.0, finish)])


def _adam_math(w, g, m, v):
    m = ADAM_B1 * m + (1.0 - ADAM_B1) * g
    v = ADAM_B2 * v + (1.0 - ADAM_B2) * (g * g)
    m_hat = m / (1.0 - ADAM_B1 ** ADAM_STEP)
    v_hat = v / (1.0 - ADAM_B2 ** ADAM_STEP)
    delta = -ADAM_LR * (m_hat / (jnp.sqrt(v_hat) + ADAM_EPS) + ADAM_WD * w)
    return delta, m, v


def _small_allreduce_adam(gpart, w, m, v):
    def body(g_ref, w_ref, m_ref, v_ref, gs_ref, d_ref, mo_ref, vo_ref, gath, send_sems, recv_sems):
        x, y, c, _ = _place()
        me = 4 * x + 2 * y + c
        gath[me] = g_ref[...]
        cps = []
        for d in range(1, N_DEV):
            peer = (x ^ (d >> 2), y ^ ((d >> 1) & 1), c ^ (d & 1))
            cps.append(pltpu.make_async_remote_copy(
                src_ref=g_ref, dst_ref=gath.at[me], send_sem=send_sems.at[d - 1],
                recv_sem=recv_sems.at[d - 1], device_id=peer, device_id_type=MESH))
        for cp in cps:
            cp.start()
        for cp in cps:
            cp.wait()
        g = gath[0]
        for k in range(1, N_DEV):
            g = g + gath[k]
        gs_ref[...] = g
        d_ref[...], mo_ref[...], vo_ref[...] = _adam_math(w_ref[...], g, m_ref[...], v_ref[...])

    shape = jax.ShapeDtypeStruct((SMALL_ROWS, D), F32)
    vm = pl.BlockSpec(memory_space=pltpu.VMEM)
    return _pcall(body, name="small_allreduce_adam", in_specs=[vm] * 4, out_specs=[vm] * 4,
                  out_shape=[shape] * 4,
                  scratch_shapes=[pltpu.VMEM((N_DEV, SMALL_ROWS, D), F32),
                                  pltpu.SemaphoreType.DMA((N_DEV - 1,)), pltpu.SemaphoreType.DMA((N_DEV - 1,))],
                  compiler_params=pltpu.CompilerParams(has_side_effects=True))(gpart, w, m, v)


def _adam(w, parts, index, m, v, *, name):
    rows = w.shape[0]
    tr = rows if rows <= 512 else rows // 2
    steps = rows // tr

    def body(w_ref, p_ref, m_ref, v_ref, g_ref, d_ref, mo_ref, vo_ref):
        g = p_ref[0].astype(F32)
        for a in range(1, N_CHIP):
            g = g + p_ref[a].astype(F32)
        g_ref[...] = g
        d_ref[...], mo_ref[...], vo_ref[...] = _adam_math(w_ref[...], g, m_ref[...], v_ref[...])

    spec = pl.BlockSpec((tr, D), lambda i: (i, 0))
    return _pcall(body, name=name, grid=(steps,),
                  in_specs=[spec, pl.BlockSpec((N_CHIP, tr, D), lambda i: (0, index * steps + i, 0)), spec, spec],
                  out_specs=[spec] * 4, out_shape=[jax.ShapeDtypeStruct((rows, D), F32)] * 4,
                  compiler_params=_cp(("parallel",)))(w, parts, m, v)


def _step(x, tgt, shards, norm_mix_g, b_in, sinks, logits, hgrn_norm_g, norm_ffn_g, norm_final_g):
    t = x.shape[0]
    big = dict(tm=1024, tn=1024, tk=4096)
    core = lax.axis_index("c").astype(jnp.int32).reshape(1)

    u1, (win_t,) = _rms_fwd(x, norm_mix_g, tm=512, name="rms_mix", comm=_gather_comm(shards[0:1], 0.5))
    (q, kv, h4, gates), (wg_t, wba, wbh, wout) = _inproj_fwd(
        u1, win_t, b_in, t=t, comm=_gather_comm([shards[1]] + shards[4:7], 0.8))
    (y_attn,), _ = _attn_fwd(q, kv, sinks, t=t)
    (y_hgrn, o_pre, states), (wu_t, wd) = _hgrn_fwd(h4, logits, hgrn_norm_g, t=t,
                                                    comm=_gather_comm(shards[2:4], 0.8))
    col = lambda j: j
    first, second = (lambda j: 0), (lambda j: 1)
    gate_tiles = [(gates, D, first), (gates, D, second)]

    def merge(prods, ex):
        (ya_, yb_), (ga, gb) = prods, ex
        return ya_, yb_, _sig(ga) * ya_ + _sig(gb) * yb_

    ya, yb, merged = _fmm([y_attn, y_hgrn], [(0, wba, False), (1, wbh, False)], gate_tiles, merge,
                          [(BF, D, D, first)] * 3, m=t, n=D, tm=512, tn=D, name="branch_merge")
    def resid_norm(prods, ex):
        (p,), (xv, gv) = prods, ex
        hv = xv + p
        return hv, hv * lax.rsqrt(jnp.mean(hv * hv, axis=-1, keepdims=True) + EPS) * gv

    h1, u2 = _fmm([merged], [(0, wout, False)], [(x, D, first)], resid_norm, [(F32, D, D, first), (BF, D, D, first)],
                  m=t, n=D, tm=1024, tn=D, name="out_proj", vecs=[norm_ffn_g])

    def swiglu(prods, ex):
        g_, u_ = prods
        return g_, u_, g_ * _sig(g_) * u_

    gt, up, z = _fmm([u2], [(0, wg_t, True), (0, wu_t, True)], [], swiglu, [(BF, FFN, FFN // 2, col)] * 3,
                     m=t, n=FFN, tm=1024, tn=FFN // 2, name="ffn_gate_up")
    def loss_head(prods, ex):
        (p,), (hv, tv, gv) = prods, ex
        hv = hv + p
        r = lax.rsqrt(jnp.mean(hv * hv, axis=-1, keepdims=True) + EPS)
        xh = hv * r
        err = xh * gv - tv
        lp = jnp.sum(jnp.sum(err * err, axis=1, keepdims=True), axis=0, keepdims=True) * (0.5 / D)
        dy = err * (1.0 / D)
        dxh = dy * gv
        dh = r * (dxh - xh * jnp.mean(dxh * xh, axis=-1, keepdims=True))
        return dh, dh, jnp.sum(dy * xh, axis=0, keepdims=True), jnp.broadcast_to(lp, (1, 128))

    dh2, dh2_b, d_norm_final, loss_row = _fmm(
        [z], [(0, wd, False)], [(h1, D, first), (tgt, D, first)], loss_head, [(F32, D, D, first), (BF, D, D, first)],
        m=t, n=D, tm=512, tn=D, name="ffn_down_loss", vecs=[norm_final_g], sums=[D, 128])

    def swiglu_bwd(prods, ex):
        (dz,), (g_, u_) = prods, ex
        g_ = g_.astype(F32)
        s = _sig(g_)
        return dz * u_.astype(F32) * s * (1.0 + g_ * (1.0 - s)), dz * g_ * s

    ffn_tiles = [(gt, FFN // 2, col), (up, FFN // 2, col)]
    dgt, dup = _fmm([dh2_b], [(0, wd, True)], ffn_tiles, swiglu_bwd, [(BF, FFN, FFN // 2, col)] * 2,
                    m=t, n=FFN, tm=1024, tn=FFN // 2, name="d_gate_up")
    d_wd = _mm(z, dh2_b, m=FFN, n=D, k=t, ta=True, tm=256, tn=D, tk=4096, out_dtype=BF, name="d_w_down")
    (du2,) = _fmm([dgt, dup], [(0, wg_t, False), (1, wu_t, False)], [], lambda prods, ex: (prods[0] + prods[1],),
                  [(F32, D, 512, col)], m=t, n=D, tm=1024, tn=512, name="d_u2")
    d_wg = _mm(dgt, u2, m=FFN, n=D, k=t, ta=True, tm=256, tn=D, tk=4096, out_dtype=BF, name="d_w_gate")
    d_wu = _mm(dup, u2, m=FFN, n=D, k=t, ta=True, tm=256, tn=D, tk=4096, out_dtype=BF, name="d_w_up")
    dh1, dh1_b, d_norm_ffn = _rms_bwd(du2, h1, norm_ffn_g, dh2, tm=512, name="rms_ffn_bwd")
    d_wout = _mm(merged, dh1_b, m=D, n=D, k=t, ta=True, tm=256, tn=D, tk=4096, out_dtype=BF, name="d_w_out")

    def merge_bwd(prods, ex):
        (dm,), (ga, gb, ya_, yb_) = prods, ex
        sa, sb = _sig(ga), _sig(gb)
        dgate = jnp.concatenate([dm * ya_.astype(F32) * sa * (1.0 - sa),
                                 dm * yb_.astype(F32) * sb * (1.0 - sb)], axis=1)
        return dm * sa, dm * sb, dgate

    ffn_grads = (d_wg, d_wu, d_wd)
    (dya, dyb, dgates), got = _fmm(
        [dh1_b], [(0, wout, True)], gate_tiles + [(ya, D, first), (yb, D, first)], merge_bwd,
        [(BF, D, D, first), (BF, D, D, first), (BF, 2 * D, 2 * D, first)],
        m=t, n=D, tm=512, tn=D, name="d_merge", comm=_pair_comm(ffn_grads))
    pair_ffn = [_pair_add(g, r, core, name="pair_add_ffn%d" % i) for i, (g, r) in enumerate(zip(ffn_grads, got))]
    dy_attn = _mm(dya, wba, m=t, n=D, k=D, tb=True, out_dtype=BF, name="d_y_attn", **big)
    dy_hgrn = _mm(dyb, wbh, m=t, n=D, k=D, tb=True, name="d_y_hgrn", **big)
    d_wba = _mm(y_attn, dya, m=D, n=D, k=t, ta=True, tm=256, tn=D, tk=4096, out_dtype=BF, name="d_w_ba")
    d_wbh = _mm(y_hgrn, dyb, m=D, n=D, k=t, ta=True, tm=256, tn=D, tk=4096, out_dtype=BF, name="d_w_bh")
    sq_grads = (d_wba, d_wbh, d_wout)
    (dq, dkv, d_sinks), (parts_ffn, *got) = _attn_bwd(
        q, kv, sinks, dy_attn, t=t, comm=_both(_chip_comm(pair_ffn), _pair_comm(sq_grads)))
    pair_sq = [_pair_add(g, r, core, name="pair_add_sq%d" % i) for i, (g, r) in enumerate(zip(sq_grads, got))]
    (dh4, d_logits, d_hgrn_norm), (parts_sq,) = _hgrn_bwd(h4, logits, hgrn_norm_g, o_pre, states, dy_hgrn,
                                                           t=t, comm=_chip_comm(pair_sq))
    dps = (dq, dkv, dh4, dgates)
    d_win_t, d_b_in = _inproj_bwd_w(dps, u1, t=t)
    half0, got_in = _inproj_bwd_x(dps, win_t, x, norm_mix_g, dh1, t=t, part=0, comm=_pair_comm([d_win_t]))
    pair_in = _pair_add(d_win_t, got_in[0], core, name="pair_add_w_in")
    (grad_x, d_norm_mix), (parts_in,) = _inproj_bwd_x(dps, win_t, x, norm_mix_g, dh1, t=t, part=1, prev=half0,
                                                      comm=_chip_comm([pair_in]))

    small_grads = (d_norm_mix, d_b_in, d_sinks, d_logits, d_hgrn_norm, d_norm_ffn, d_norm_final)
    return loss_row, grad_x, (parts_in, parts_ffn, parts_sq), small_grads


def _pack_small(norm_mix, b_in, sinks, logits, hgrn_norm, norm_ffn, norm_final, extra=None):
    pad = lambda a, n: jnp.pad(a.reshape(1, -1), ((0, 0), (0, n - a.size)))
    rows = [norm_mix.reshape(1, D), hgrn_norm.reshape(1, D), norm_ffn.reshape(1, D), norm_final.reshape(1, D),
            logits.reshape(2, D), pad(sinks.reshape(-1)[:16], D),
            jnp.zeros((1, D), F32) if extra is None else pad(extra, D),
            pad(b_in, 8 * D).reshape(8, D)]
    return jnp.concatenate(rows, axis=0).astype(F32)


def _unpack_small(p):
    return dict(norm_mix_g=p[0:1], hgrn_norm_g=p[1:2], norm_ffn_g=p[2:3], norm_final_g=p[3],
                hgrn_lb_logits=p[4:6], attn_sinks=p[6:7, 0:16], extra=p[7],
                b_in=p[8:16].reshape(1, 8 * D)[:, :IN_W])


def kernel(x, norm_mix_g, w_in, b_in, attn_sinks, hgrn_lb_logits, hgrn_norm_g, w_branch_attn, w_branch_hgrn, w_out, norm_ffn_g, w_ffn_gate, w_ffn_up, w_ffn_down, norm_final_g, loss_target, m_norm_mix_g, m_w_in, m_b_in, m_attn_sinks, m_hgrn_lb_logits, m_hgrn_norm_g, m_w_branch_attn, m_w_branch_hgrn, m_w_out, m_norm_ffn_g, m_w_ffn_gate, m_w_ffn_up, m_w_ffn_down, m_norm_final_g, v_norm_mix_g, v_w_in, v_b_in, v_attn_sinks, v_hgrn_lb_logits, v_hgrn_norm_g, v_w_branch_attn, v_w_branch_hgrn, v_w_out, v_norm_ffn_g, v_w_ffn_gate, v_w_ffn_up, v_w_ffn_down, v_norm_final_g):
    shards = [w_in[0].T.astype(BF), w_ffn_gate[0].T.astype(BF), w_ffn_up[0].T.astype(BF),
              w_ffn_down[0].astype(BF), w_branch_attn[0].astype(BF), w_branch_hgrn[0].astype(BF),
              w_out[0].astype(BF)]
    loss_row, grad_x, grad_parts, small_grads = _step(
        x[0], loss_target[0], shards, norm_mix_g, b_in, attn_sinks, hgrn_lb_logits, hgrn_norm_g,
        norm_ffn_g, norm_final_g.reshape(1, D))

    d_norm_mix, d_b_in, d_sinks, d_logits, d_hgrn_norm, d_norm_ffn, d_norm_final = small_grads
    g_small = _pack_small(d_norm_mix, d_b_in, d_sinks[:, :16], d_logits, d_hgrn_norm, d_norm_ffn,
                          d_norm_final, extra=loss_row[0, 0:1])
    w_small = _pack_small(norm_mix_g, b_in, attn_sinks, hgrn_lb_logits, hgrn_norm_g, norm_ffn_g, norm_final_g)
    m_small = _pack_small(m_norm_mix_g, m_b_in, m_attn_sinks, m_hgrn_lb_logits, m_hgrn_norm_g, m_norm_ffn_g,
                          m_norm_final_g)
    v_small = _pack_small(v_norm_mix_g, v_b_in, v_attn_sinks, v_hgrn_lb_logits, v_hgrn_norm_g, v_norm_ffn_g,
                          v_norm_final_g)
    small = [_unpack_small(p) for p in _small_allreduce_adam(g_small, w_small, m_small, v_small)]
    loss = small[0]["extra"][0]

    names = ["w_in", "w_ffn_gate", "w_ffn_up", "w_ffn_down", "w_branch_attn", "w_branch_hgrn", "w_out"]
    w_full = dict(w_in=(w_in, m_w_in, v_w_in), w_ffn_gate=(w_ffn_gate, m_w_ffn_gate, v_w_ffn_gate),
                  w_ffn_up=(w_ffn_up, m_w_ffn_up, v_w_ffn_up), w_ffn_down=(w_ffn_down, m_w_ffn_down, v_w_ffn_down),
                  w_branch_attn=(w_branch_attn, m_w_branch_attn, v_w_branch_attn),
                  w_branch_hgrn=(w_branch_hgrn, m_w_branch_hgrn, v_w_branch_hgrn),
                  w_out=(w_out, m_w_out, v_w_out))
    parts_in, parts_ffn, parts_sq = grad_parts
    where = [(parts_in, 0), (parts_ffn, 0), (parts_ffn, 1), (parts_ffn, 2), (parts_sq, 0), (parts_sq, 1), (parts_sq, 2)]
    big = {}
    for i, name in enumerate(names):
        view = (lambda a: a[0].T) if i < 3 else (lambda a: a[0])
        back = (lambda a: a.T[None]) if i < 3 else (lambda a: a[None])
        wv, mv, vv = w_full[name]
        res = _adam(view(wv), where[i][0], where[i][1], view(mv), view(vv), name="adam_" + name)
        big[name] = [back(a) for a in res]

    order = ["norm_mix_g", "w_in", "b_in", "attn_sinks", "hgrn_lb_logits", "hgrn_norm_g", "w_branch_attn",
             "w_branch_hgrn", "w_out", "norm_ffn_g", "w_ffn_gate", "w_ffn_up", "w_ffn_down", "norm_final_g"]
    outs = [loss, grad_x[None]]
    for kind in range(4):
        for name in order:
            outs.append(big[name][kind] if name in big else small[kind][name])
    return tuple(outs)
```

```python
import math

import jax
import jax.numpy as jnp
from jax import lax
from jax.experimental import pallas as pl
from jax.experimental.pallas import tpu as pltpu

F32 = jnp.float32
BF = jnp.bfloat16
MESH = pl.DeviceIdType.MESH

D = 1024
HEAD = 64
N_PAIR = 8
BLK = 128
CH = 64
HG_SUB = 2
HG_HEADS = 8
HG_K = 128
FFN = 2816
IN_W = 7424
N_DEV = 8
N_CHIP = 4
EPS = 1e-6
NEG = -1e30
SCALE = 1.0 / math.sqrt(HEAD)
VMEM_LIMIT = 56 * 1024 * 1024
WT = 256

ADAM_LR, ADAM_B1, ADAM_B2, ADAM_EPS, ADAM_WD, ADAM_STEP = 0.001, 0.9, 0.999, 1e-08, 0.01, 10

SLAB_R = (IN_W // N_DEV, FFN // N_DEV, FFN // N_DEV, FFN // N_DEV, D // N_DEV, D // N_DEV, D // N_DEV)
SLAB_ROWS = sum(SLAB_R)
SLAB_OFF = tuple(sum(SLAB_R[:i]) for i in range(len(SLAB_R)))
N_W = len(SLAB_R)
GRP_OFF = (0, D // WT, (D + 256) // WT, (5 * D + 256) // WT)
GRP_N = (D // WT, 256 // WT, 4 * D // WT, 2 * D // WT)
SMALL_ROWS = 16


_NN = (((1,), (0,)), ((), ()))
_NT = (((1,), (1,)), ((), ()))
_TN = (((0,), (0,)), ((), ()))


def _pcall(body, **kw):
    return pl.pallas_call(body, **kw)


def _cp(sem=None, **kw):
    return pltpu.CompilerParams(dimension_semantics=sem, vmem_limit_bytes=VMEM_LIMIT, **kw)


def _sig(v):
    return 0.5 * jnp.tanh(0.5 * v) + 0.5


def _accum(ref, val, first):
    @pl.when(first)
    def _():
        ref[...] = val

    @pl.when(jnp.logical_not(first))
    def _():
        ref[...] += val


class _Comm:
    def __init__(self, ins, out_shapes, sem_shapes, phases):
        self.ins, self.out_shapes, self.sem_shapes, self.phases = list(ins), list(out_shapes), list(sem_shapes), phases


def _both(a, b):
    ni, no, ns = len(a.ins), len(a.out_shapes), len(a.sem_shapes)

    def of_a(fn):
        return lambda ins, outs, sems: fn(ins[:ni], outs[:no], sems[:ns])

    def of_b(fn):
        return lambda ins, outs, sems: fn(ins[ni:], outs[no:], sems[ns:])

    return _Comm(a.ins + b.ins, a.out_shapes + b.out_shapes, a.sem_shapes + b.sem_shapes,
                 [(f, of_a(fn)) for f, fn in a.phases] + [(f, of_b(fn)) for f, fn in b.phases])


def _host(body, comm, n_in, n_out, n_scr, nsteps, step_fn):
    if comm is None:
        return body
    ci, co = len(comm.ins), len(comm.out_shapes)

    def wrapped(*refs):
        p = 0
        ins, p = refs[p:p + n_in], p + n_in
        cins, p = refs[p:p + ci], p + ci
        outs, p = refs[p:p + n_out], p + n_out
        couts, p = refs[p:p + co], p + co
        scr, p = refs[p:p + n_scr], p + n_scr
        csems = refs[p:]
        step = step_fn()
        for frac, fn in comm.phases:
            if frac < 1.0:
                @pl.when(step == int(round(frac * (nsteps - 1))))
                def _(fn=fn):
                    fn(cins, couts, csems)
        body(*ins, *outs, *scr)
        for frac, fn in comm.phases:
            if frac >= 1.0:
                @pl.when(step == nsteps - 1)
                def _(fn=fn):
                    fn(cins, couts, csems)

    return wrapped


def _hosted_call(body, comm, args, *, name, grid, in_specs, out_specs, out_shape, scratch_shapes, sem,
                 nsteps, step_fn, aliases=None):
    n_in, n_out, n_scr = len(in_specs), len(out_specs), len(scratch_shapes)
    args = list(args)
    extra = {}
    if comm is not None:
        in_specs = list(in_specs) + [_hbm_spec()] * len(comm.ins)
        out_specs = list(out_specs) + [_hbm_spec()] * len(comm.out_shapes)
        out_shape = list(out_shape) + comm.out_shapes
        scratch_shapes = list(scratch_shapes) + comm.sem_shapes
        args += comm.ins
        extra = dict(has_side_effects=True)
    outs = _pcall(_host(body, comm, n_in, n_out, n_scr, nsteps, step_fn), name=name, grid=grid,
                  in_specs=in_specs, out_specs=out_specs, out_shape=out_shape, scratch_shapes=scratch_shapes,
                  input_output_aliases=aliases or {}, compiler_params=_cp(sem, **extra))(*args)
    return list(outs[:n_out]), list(outs[n_out:])


def _hbm_spec():
    return pl.BlockSpec(memory_space=pl.ANY)


def _mm(a, b, *, m, n, k, tm, tn, tk, ta=False, tb=False, out_dtype=F32, resid=None, name):
    tm, tn, tk = min(tm, m), min(tn, n), min(tk, k)
    gm, gn, gk = m // tm, n // tn, k // tk
    assert gm * tm == m and gn * tn == n and gk * tk == k, (name, m, n, k, tm, tn, tk)
    a_spec = (pl.BlockSpec((tk, tm), lambda i, j, l: (l, i)) if ta
              else pl.BlockSpec((tm, tk), lambda i, j, l: (i, l)))
    b_spec = (pl.BlockSpec((tn, tk), lambda i, j, l: (j, l)) if tb
              else pl.BlockSpec((tk, tn), lambda i, j, l: (l, j)))
    dims = (((0 if ta else 1,), (1 if tb else 0,)), ((), ()))
    ins, in_specs = [a, b], [a_spec, b_spec]
    if resid is not None:
        ins.append(resid)
        in_specs.append(pl.BlockSpec((tm, tn), lambda i, j, l: (i, j)))
    scratch = [pltpu.VMEM((tm, tn), F32)] if gk > 1 else []

    def body(*refs):
        it = iter(refs)
        a_ref, b_ref = next(it), next(it)
        resid_ref = next(it) if resid is not None else None
        o_ref = next(it)
        acc_ref = next(it) if gk > 1 else None
        l = pl.program_id(2)
        part = lax.dot_general(a_ref[...].astype(BF), b_ref[...].astype(BF), dims,
                               preferred_element_type=F32)

        def finish(acc):
            if resid_ref is not None:
                acc = acc + resid_ref[...].astype(F32)
            o_ref[...] = acc.astype(out_dtype)

        if gk == 1:
            finish(part)
        else:
            _accum(acc_ref, part, l == 0)

            @pl.when(l == gk - 1)
            def _():
                finish(acc_ref[...])

    return _pcall(body, name=name, grid=(gm, gn, gk), in_specs=in_specs,
                  out_specs=pl.BlockSpec((tm, tn), lambda i, j, l: (i, j)),
                  out_shape=jax.ShapeDtypeStruct((m, n), out_dtype), scratch_shapes=scratch,
                  compiler_params=_cp(("parallel", "parallel", "arbitrary")))(*ins)


def _fmm(lhs, rhs, extras, epilogue, outs, *, m, n, tm, tn, name, comm=None, vecs=(), sums=()):
    tm, tn = min(tm, m), min(tn, n)
    assert m % tm == 0 and n % tn == 0 and (not sums or tn == n), (name, m, n, tm, tn)
    in_specs, args = [], []
    for a in lhs:
        in_specs.append(pl.BlockSpec((tm, a.shape[1]), lambda i, j: (i, 0)))
        args.append(a)
    for li, b, tb in rhs:
        k = lhs[li].shape[1]
        in_specs.append(pl.BlockSpec((tn, k), lambda i, j: (j, 0)) if tb
                        else pl.BlockSpec((k, tn), lambda i, j: (0, j)))
        args.append(b)
    for arr, w, col in extras:
        in_specs.append(pl.BlockSpec((tm, w), lambda i, j, col=col: (i, col(j))))
        args.append(arr)
    for vec in vecs:
        in_specs.append(pl.BlockSpec((1, tn), lambda i, j: (0, j)))
        args.append(vec)
    out_specs = [pl.BlockSpec((tm, w), lambda i, j, col=col: (i, col(j))) for _, _, w, col in outs]
    out_shape = [jax.ShapeDtypeStruct((m, total), dt) for dt, total, _, _ in outs]
    for w in sums:
        out_specs.append(pl.BlockSpec((1, w), lambda i, j: (0, 0)))
        out_shape.append(jax.ShapeDtypeStruct((1, w), F32))
    nl, nr, ne, no = len(lhs), len(rhs), len(extras) + len(vecs), len(outs)

    def body(*refs):
        prods = []
        for r, (li, _, tb) in enumerate(rhs):
            prods.append(lax.dot_general(refs[li][...], refs[nl + r][...], _NT if tb else _NN,
                                         preferred_element_type=F32))
        vals = epilogue(prods, [ref[...] for ref in refs[nl + nr:nl + nr + ne]])
        o_refs = refs[nl + nr + ne:]
        for o_ref, v in zip(o_refs[:no], vals[:no]):
            o_ref[...] = v.astype(o_ref.dtype)
        for s_ref, v in zip(o_refs[no:], vals[no:]):
            _accum(s_ref, v, pl.program_id(0) == 0)

    gm, gn = m // tm, n // tn
    res, comm_res = _hosted_call(
        body, comm, args, name=name, grid=(gm, gn), in_specs=in_specs, out_specs=out_specs,
        out_shape=out_shape, scratch_shapes=[], sem=("arbitrary", "arbitrary"), nsteps=gm * gn,
        step_fn=lambda: pl.program_id(0) * gn + pl.program_id(1))
    return res if comm is None else (res, comm_res)


def _grp_of(i):
    return [jnp.logical_and(i >= GRP_OFF[g], i < GRP_OFF[g] + GRP_N[g]) for g in range(4)]


def _grp_idx(i, g):
    return jnp.clip(i - GRP_OFF[g], 0, GRP_N[g] - 1)


def _inproj_fwd(u, win_t, b_in, *, t, comm=None):
    n_tiles = IN_W // WT
    dims = (((1,), (1,)), ((), ()))
    dtypes = (BF, BF, F32, F32)

    def body(u_ref, w_ref, b_ref, *o_refs):
        i = pl.program_id(0)
        p = lax.dot_general(u_ref[...], w_ref[...], dims, preferred_element_type=F32) + b_ref[...]
        for g, pred in enumerate(_grp_of(i)):
            @pl.when(pred)
            def _(g=g):
                o_refs[g][...] = p.astype(dtypes[g])

    return _hosted_call(
        body, comm, (u, win_t, b_in), name="inproj_fwd", grid=(n_tiles,),
        in_specs=[pl.BlockSpec((t, D), lambda i: (0, 0)),
                  pl.BlockSpec((WT, D), lambda i: (i, 0)),
                  pl.BlockSpec((1, WT), lambda i: (0, i))],
        out_specs=[pl.BlockSpec((t, WT), lambda i, g=g: (0, _grp_idx(i, g))) for g in range(4)],
        out_shape=[jax.ShapeDtypeStruct((t, GRP_N[g] * WT), dtypes[g]) for g in range(4)],
        scratch_shapes=[], sem=("arbitrary",), nsteps=n_tiles, step_fn=lambda: pl.program_id(0))


def _inproj_bwd_x(dps, win_t, x, g, resid, *, t, part, prev=None, comm=None):
    n_tiles = IN_W // WT
    n_row = 4 if t >= 2048 else 2
    tm = t // n_row
    per = 1 if part == 0 else n_row - 1
    row = lambda i: part + i

    n_chunks = 8
    h_first, g_first = 2, 6
    sub = D // WT

    def w_block(l):
        return jnp.where(l == 0, GRP_OFF[0], jnp.where(l == 1, GRP_OFF[1], GRP_OFF[2] + sub * (l - h_first)))

    def body(d0, d1, d2, d3, w0, w1, w2, w3, x_ref, g_ref, r_ref, *rest):
        dg_prev = rest[0] if prev is not None else None
        o_ref, dg_ref, acc_ref = rest[-3], rest[-2], rest[-1]
        i, l = pl.program_id(0), pl.program_id(1)

        @pl.when(l == 1)
        def _():
            acc_ref[...] += jnp.dot(d1[...], w0[...], preferred_element_type=F32)

        w = jnp.concatenate([w0[...], w1[...], w2[...], w3[...]], axis=0)
        for pred, d_ref in ((l == 0, d0), (jnp.logical_and(l >= h_first, l < g_first), d2), (l >= g_first, d3)):
            @pl.when(pred)
            def _(d_ref=d_ref):
                _accum(acc_ref, jnp.dot(d_ref[...], w, preferred_element_type=F32), l == 0)

        @pl.when(l == n_chunks - 1)
        def _():
            xv = x_ref[...]
            r = lax.rsqrt(jnp.mean(xv * xv, axis=-1, keepdims=True) + EPS)
            xh = xv * r
            du = acc_ref[...]
            dxh = du * g_ref[...]
            o_ref[...] = r_ref[...] + r * (dxh - xh * jnp.mean(dxh * xh, axis=-1, keepdims=True))
            dg = jnp.sum(du * xh, axis=0, keepdims=True)
            if dg_prev is not None:
                dg = dg + jnp.where(i == 0, 1.0, 0.0) * dg_prev[...]
            _accum(dg_ref, dg, i == 0)

    rows = lambda w: pl.BlockSpec((tm, w), lambda i, l: (row(i), 0))
    in_specs = ([rows(D), rows(256),
                 pl.BlockSpec((tm, D), lambda i, l: (row(i), jnp.clip(l - h_first, 0, 3))),
                 pl.BlockSpec((tm, D), lambda i, l: (row(i), jnp.clip(l - g_first, 0, 1)))]
                + [pl.BlockSpec((WT, D), lambda i, l, o=o: (w_block(l) + o, 0)) for o in range(sub)]
                + [rows(D), pl.BlockSpec((1, D), lambda i, l: (0, 0)), rows(D)])
    args = list(dps) + [win_t] * sub + [x, g, resid]
    aliases = None
    if prev is not None:
        in_specs += [pl.BlockSpec((1, D), lambda i, l: (0, 0)), _hbm_spec()]
        args += [prev[1], prev[0]]
        aliases = {len(args) - 1: 0}
    return _hosted_call(
        body, comm, args, name="inproj_bwd_x%d" % part, grid=(per, n_chunks), in_specs=in_specs,
        out_specs=[rows(D), pl.BlockSpec((1, D), lambda i, l: (0, 0))],
        out_shape=[jax.ShapeDtypeStruct((t, D), F32), jax.ShapeDtypeStruct((1, D), F32)],
        scratch_shapes=[pltpu.VMEM((tm, D), F32)], sem=("arbitrary", "arbitrary"), nsteps=per * n_chunks,
        step_fn=lambda: pl.program_id(0) * n_chunks + pl.program_id(1), aliases=aliases)


def _inproj_bwd_w(dps, u, *, t):
    n_tiles = IN_W // WT
    dims = (((0,), (0,)), ((), ()))

    def body(d0, d1, d2, d3, u_ref, o_ref, db_ref):
        i = pl.program_id(0)
        uv = u_ref[...]
        for g, (pred, d_ref) in enumerate(zip(_grp_of(i), (d0, d1, d2, d3))):
            @pl.when(pred)
            def _(d_ref=d_ref):
                dv = d_ref[...]
                o_ref[...] = lax.dot_general(dv, uv, dims, preferred_element_type=F32).astype(BF)
                db_ref[...] = jnp.sum(dv.astype(F32), axis=0, keepdims=True)

    return _pcall(body, name="inproj_bwd_w", grid=(n_tiles,),
                  in_specs=[pl.BlockSpec((t, WT), lambda i, g=g: (0, _grp_idx(i, g))) for g in range(4)]
                  + [pl.BlockSpec((t, D), lambda i: (0, 0))],
                  out_specs=[pl.BlockSpec((WT, D), lambda i: (i, 0)),
                             pl.BlockSpec((1, WT), lambda i: (0, i))],
                  out_shape=[jax.ShapeDtypeStruct((IN_W, D), BF), jax.ShapeDtypeStruct((1, IN_W), F32)],
                  compiler_params=_cp(("arbitrary",)))(*dps, u)


def _row_spec(tm, width, col=0):
    return pl.BlockSpec((tm, width), lambda i: (i, col))


def _vec_spec(width):
    return pl.BlockSpec((1, width), lambda i: (0, 0))


def _rms_fwd(x, g, *, tm, name, comm=None):
    t = x.shape[0]
    tm = min(tm, t)

    def body(x_ref, g_ref, u_ref):
        xv = x_ref[...]
        r = lax.rsqrt(jnp.mean(xv * xv, axis=-1, keepdims=True) + EPS)
        u_ref[...] = (xv * r * g_ref[...]).astype(BF)

    (u,), comm_res = _hosted_call(
        body, comm, (x, g), name=name, grid=(t // tm,), in_specs=[_row_spec(tm, D), _vec_spec(D)],
        out_specs=[_row_spec(tm, D)], out_shape=[jax.ShapeDtypeStruct((t, D), BF)], scratch_shapes=[],
        sem=("arbitrary",), nsteps=t // tm, step_fn=lambda: pl.program_id(0))
    return u if comm is None else (u, comm_res)


def _rms_bwd(du, x, g, resid, *, tm, name):
    t = x.shape[0]
    tm = min(tm, t)

    def body(du_ref, x_ref, g_ref, r_ref, dx_ref, dxb_ref, dg_ref):
        xv = x_ref[...]
        r = lax.rsqrt(jnp.mean(xv * xv, axis=-1, keepdims=True) + EPS)
        xh = xv * r
        duv = du_ref[...]
        dxh = duv * g_ref[...]
        dx = r_ref[...] + r * (dxh - xh * jnp.mean(dxh * xh, axis=-1, keepdims=True))
        dx_ref[...] = dx
        dxb_ref[...] = dx.astype(BF)
        _accum(dg_ref, jnp.sum(duv * xh, axis=0, keepdims=True), pl.program_id(0) == 0)

    return _pcall(body, name=name, grid=(t // tm,),
                  in_specs=[_row_spec(tm, D), _row_spec(tm, D), _vec_spec(D), _row_spec(tm, D)],
                  out_specs=[_row_spec(tm, D), _row_spec(tm, D), _vec_spec(D)],
                  out_shape=[jax.ShapeDtypeStruct((t, D), F32), jax.ShapeDtypeStruct((t, D), BF),
                             jax.ShapeDtypeStruct((1, D), F32)],
                  compiler_params=_cp(("arbitrary",)))(du, x, g, resid)


def _attn_kv_tiles(kprev, kcur):
    kv = jnp.concatenate([kprev, kcur], axis=0).astype(F32)
    lo = lax.broadcasted_iota(jnp.int32, (2 * BLK, 128), 1) < HEAD
    tiles = []
    for part in (kv[:, 0:128], kv[:, 128:256]):
        rolled = pltpu.roll(part, HEAD, 1)
        z = jnp.zeros_like(part)
        tiles.append(((jnp.where(lo, part, z).astype(BF), jnp.where(lo, z, rolled).astype(BF)),
                      (jnp.where(lo, rolled, z).astype(BF), jnp.where(lo, z, part).astype(BF))))
    k_t, v_t = tiles
    return [(jnp.concatenate(k_t[h], axis=0), jnp.concatenate(v_t[h], axis=0)) for h in range(2)]


def _attn_mask(i):
    qi = lax.broadcasted_iota(jnp.int32, (BLK, 2 * BLK), 0)
    kj = lax.broadcasted_iota(jnp.int32, (BLK, 2 * BLK), 1)
    first_key = jnp.where(i == 0, BLK, 0)
    in_prev = jnp.logical_and(jnp.logical_and(kj < BLK, kj > qi), kj >= first_key)
    in_cur = jnp.logical_and(kj >= BLK, kj - BLK <= qi)
    return jnp.logical_or(in_prev, in_cur)


def _attn_probs(s, sink, valid):
    s = jnp.where(valid, s * SCALE, NEG)
    mx = jnp.maximum(jnp.max(s, axis=-1, keepdims=True), sink)
    e = jnp.exp(s - mx)
    es = jnp.exp(sink - mx)
    inv = 1.0 / (jnp.sum(e, axis=-1, keepdims=True) + es)
    return e * inv, es * inv


_KEYS = 2 * BLK


def _pair(ref, j):
    return ref[:, j * 128:(j + 1) * 128]


def _attn_fwd(q, kv, sinks, *, t, comm=None):
    nb = t // BLK

    def body(sink_ref, q_ref, kp_ref, kc_ref, o_ref):
        valid = _attn_mask(pl.program_id(0))
        tiles = _attn_kv_tiles(kp_ref[...], kc_ref[...])
        s = [lax.dot_general(_pair(q_ref, j), tiles[j // 4][0], _NT, preferred_element_type=F32)
             for j in range(N_PAIR)]
        p = []
        for j in range(N_PAIR):
            pe, _ = _attn_probs(s[j][:, 0:_KEYS], sink_ref[0, 2 * j], valid)
            po, _ = _attn_probs(s[j][:, _KEYS:2 * _KEYS], sink_ref[0, 2 * j + 1], valid)
            p.append(jnp.concatenate([pe.astype(BF), po.astype(BF)], axis=1))
        for j in range(N_PAIR):
            o_ref[:, j * 128:(j + 1) * 128] = jnp.dot(p[j], tiles[j // 4][1],
                                                      preferred_element_type=F32).astype(BF)

    return _hosted_call(
        body, comm, (sinks, q, kv, kv), name="attn_fwd", grid=(nb,),
        in_specs=[pl.BlockSpec(memory_space=pltpu.SMEM),
                  pl.BlockSpec((BLK, D), lambda i: (i, 0)),
                  pl.BlockSpec((BLK, 256), lambda i: (jnp.maximum(i - 1, 0), 0)),
                  pl.BlockSpec((BLK, 256), lambda i: (i, 0))],
        out_specs=[pl.BlockSpec((BLK, D), lambda i: (i, 0))],
        out_shape=[jax.ShapeDtypeStruct((t, D), BF)],
        scratch_shapes=[], sem=("arbitrary",), nsteps=nb, step_fn=lambda: pl.program_id(0))


def _attn_bwd(q, kv, sinks, do, *, t, comm=None):
    nb = t // BLK
    last = nb - 1

    def body(sink_ref, q_ref, kp_ref, kc_ref, do_ref, dq_ref, dkv_ref, ds_ref, carry_ref):
        i = pl.program_id(0)

        @pl.when(i == 0)
        def _():
            ds_ref[...] = jnp.zeros_like(ds_ref)
            carry_ref[...] = jnp.zeros_like(carry_ref)

        @pl.when(i < nb)
        def _():
            valid = _attn_mask(i)
            tiles = _attn_kv_tiles(kp_ref[...], kc_ref[...])
            lane1 = lax.broadcasted_iota(jnp.int32, (1, 128), 1)
            dsink = jnp.zeros((1, 128), F32)
            s = [lax.dot_general(_pair(q_ref, j), tiles[j // 4][0], _NT, preferred_element_type=F32)
                 for j in range(N_PAIR)]
            dp = [lax.dot_general(_pair(do_ref, j), tiles[j // 4][1], _NT, preferred_element_type=F32)
                  for j in range(N_PAIR)]
            p_all, ds_all = [], []
            for j in range(N_PAIR):
                halves = []
                for par in range(2):
                    cols = slice(par * _KEYS, (par + 1) * _KEYS)
                    p, ps = _attn_probs(s[j][:, cols], sink_ref[0, 2 * j + par], valid)
                    dpj = dp[j][:, cols]
                    dd = jnp.sum(p * dpj, axis=-1, keepdims=True)
                    dsink = dsink + jnp.where(lane1 == 2 * j + par,
                                              -jnp.sum(ps * dd, axis=0, keepdims=True), 0.0)
                    halves.append((p.astype(BF), (p * (dpj - dd)).astype(BF)))
                p_all.append(jnp.concatenate([halves[0][0], halves[1][0]], axis=1))
                ds_all.append(jnp.concatenate([halves[0][1], halves[1][1]], axis=1))
            for j in range(N_PAIR):
                dq_ref[:, j * 128:(j + 1) * 128] = (
                    jnp.dot(ds_all[j], tiles[j // 4][0], preferred_element_type=F32) * SCALE).astype(BF)
            ds_ref[...] += dsink
            gk, gv = [], []
            for h in range(2):
                grp = range(4 * h, 4 * h + 4)
                q_rows = jnp.concatenate([_pair(q_ref, j) for j in grp], axis=0)
                do_rows = jnp.concatenate([_pair(do_ref, j) for j in grp], axis=0)
                g_k = lax.dot_general(jnp.concatenate([ds_all[j] for j in grp], axis=0), q_rows, _TN,
                                      preferred_element_type=F32)
                g_v = lax.dot_general(jnp.concatenate([p_all[j] for j in grp], axis=0), do_rows, _TN,
                                      preferred_element_type=F32)
                gk.append((g_k[0:_KEYS], g_k[_KEYS:2 * _KEYS]))
                gv.append((g_v[0:_KEYS], g_v[_KEYS:2 * _KEYS]))
            lo = lax.broadcasted_iota(jnp.int32, (2 * BLK, 128), 1) < HEAD
            zero = jnp.zeros((2 * BLK, 128), F32)

            def unpad(g):
                return (jnp.where(lo, g[0][0] + pltpu.roll(g[0][1], HEAD, 1), zero)
                        + jnp.where(lo, zero, pltpu.roll(g[1][0], HEAD, 1) + g[1][1]))

            dk = unpad(gk) * SCALE
            dv = unpad(gv)
            dkv_ref[:, 0:128] = (carry_ref[:, 0:128] + dk[0:BLK]).astype(BF)
            dkv_ref[:, 128:256] = (carry_ref[:, 128:256] + dv[0:BLK]).astype(BF)
            carry_ref[:, 0:128] = dk[BLK:2 * BLK]
            carry_ref[:, 128:256] = dv[BLK:2 * BLK]

        @pl.when(i == nb)
        def _():
            dkv_ref[...] = carry_ref[...].astype(BF)

    return _hosted_call(
        body, comm, (sinks, q, kv, kv, do), name="attn_bwd", grid=(nb + 1,),
        in_specs=[pl.BlockSpec(memory_space=pltpu.SMEM),
                  pl.BlockSpec((BLK, D), lambda i: (jnp.minimum(i, last), 0)),
                  pl.BlockSpec((BLK, 256), lambda i: (jnp.clip(i - 1, 0, last), 0)),
                  pl.BlockSpec((BLK, 256), lambda i: (jnp.minimum(i, last), 0)),
                  pl.BlockSpec((BLK, D), lambda i: (jnp.minimum(i, last), 0))],
        out_specs=[pl.BlockSpec((BLK, D), lambda i: (jnp.minimum(i, last), 0)),
                   pl.BlockSpec((BLK, 256), lambda i: (jnp.maximum(i - 1, 0), 0)),
                   pl.BlockSpec((1, 128), lambda i: (0, 0))],
        out_shape=[jax.ShapeDtypeStruct((t, D), BF), jax.ShapeDtypeStruct((t, 256), BF),
                   jax.ShapeDtypeStruct((1, 128), F32)],
        scratch_shapes=[pltpu.VMEM((BLK, 256), F32)], sem=("arbitrary",), nsteps=nb + 1,
        step_fn=lambda: pl.program_id(0))


def _split3(v):
    h = v.astype(BF)
    r = v - h.astype(F32)
    m = r.astype(BF)
    lo = (r - m.astype(F32)).astype(BF)
    return jnp.concatenate([h, m, lo], axis=1)


def _apply01(mat, v):
    n = v.shape[1]
    r = jnp.dot(mat, _split3(v), preferred_element_type=F32)
    return r[:, 0:n] + r[:, n:2 * n] + r[:, 2 * n:3 * n]


def _hgrn_gates(hq, hf, lb):
    sq = _sig(hq)
    sg = _sig(hf)
    f = lb + (1.0 - lb) * sg
    return hq * sq, (1.0 - lb) * (1.0 - sg), jnp.log(f), sq, sg, f


def _tri(upper):
    r = lax.broadcasted_iota(jnp.int32, (CH, CH), 0)
    c = lax.broadcasted_iota(jnp.int32, (CH, CH), 1)
    return (c >= r) if upper else (c <= r)


def _lb_from_logits(lg_ref):
    return 1.0 / (1.0 + jnp.exp(lg_ref[1:2, :] - lg_ref[0:1, :]))


def _hgrn_fwd(h4, logits, norm_g, *, t, comm=None):
    nc = t // CH
    nt_dims = (((1,), (1,)), ((), ()))
    tn_dims = (((0,), (0,)), ((), ()))

    def body(h_ref, lg_ref, ng_ref, y_ref, o_ref, st_ref, s_scr, b_scr, qa_s, ka_s, qb_s, kb_s, v_s):
        @pl.when(pl.program_id(0) == 0)
        def _():
            s_scr[...] = jnp.zeros_like(s_scr)

        heads = [slice(h * HG_K, (h + 1) * HG_K) for h in range(HG_HEADS)]
        causal = _tri(False)
        lb = _lb_from_logits(lg_ref)
        for c in range(HG_SUB):
            rows = slice(c * CH, (c + 1) * CH)
            q, k, g, _, _, _ = _hgrn_gates(h_ref[rows, 0:D], h_ref[rows, D:2 * D], lb)
            b_scr[...] = _apply01(jnp.where(causal, 1.0, 0.0).astype(BF), g)
            b = b_scr[...]
            b_mid = b_scr[CH // 2 - 1:CH // 2, :]
            b_last = b_scr[CH - 1:CH, :]
            qa_s[...] = (q * jnp.exp(b - b_mid)).astype(BF)
            ka_s[...] = (k * jnp.exp(b_mid - b)).astype(BF)
            qb_s[...] = (q * jnp.exp(b)).astype(BF)
            kb_s[...] = (k * jnp.exp(b_last - b)).astype(BF)
            v_s[...] = h_ref[rows, 2 * D:3 * D].astype(BF)
            dec = jnp.exp(b_last)
            st_ref[c] = s_scr[...].astype(BF)
            a = [jnp.where(causal, lax.dot_general(qa_s[:, sl], ka_s[:, sl], nt_dims, preferred_element_type=F32),
                           0.0).astype(BF) for sl in heads]
            for h, sl in enumerate(heads):
                o_ref[rows, sl] = (jnp.dot(a[h], v_s[:, sl], preferred_element_type=F32)
                                   + lax.dot_general(qb_s[:, sl], s_scr[h].astype(BF), nt_dims,
                                                     preferred_element_type=F32))
            for h, sl in enumerate(heads):
                s_scr[h] = dec[:, sl] * s_scr[h] + lax.dot_general(v_s[:, sl], kb_s[:, sl], tn_dims,
                                                                   preferred_element_type=F32)
            for h, sl in enumerate(heads):
                o = o_ref[rows, sl]
                on = o * lax.rsqrt(jnp.mean(o * o, axis=-1, keepdims=True) + EPS)
                gate = _sig(h_ref[rows, 3 * D + h * HG_K:3 * D + (h + 1) * HG_K])
                y_ref[rows, sl] = (on * ng_ref[:, sl] * gate).astype(BF)

    half = lambda: pltpu.VMEM((CH, D), BF)
    blk = HG_SUB * CH
    return _hosted_call(
        body, comm, (h4, logits, norm_g), name="hgrn_fwd", grid=(nc // HG_SUB,),
        in_specs=[pl.BlockSpec((blk, 4 * D), lambda n: (n, 0)),
                  pl.BlockSpec((2, D), lambda n: (0, 0)),
                  pl.BlockSpec((1, D), lambda n: (0, 0))],
        out_specs=[pl.BlockSpec((blk, D), lambda n: (n, 0)),
                   pl.BlockSpec((blk, D), lambda n: (n, 0)),
                   pl.BlockSpec((HG_SUB, HG_HEADS, HG_K, HG_K), lambda n: (n, 0, 0, 0))],
        out_shape=[jax.ShapeDtypeStruct((t, D), BF), jax.ShapeDtypeStruct((t, D), F32),
                   jax.ShapeDtypeStruct((nc, HG_HEADS, HG_K, HG_K), BF)],
        scratch_shapes=[pltpu.VMEM((HG_HEADS, HG_K, HG_K), F32), pltpu.VMEM((CH, D), F32),
                        half(), half(), half(), half(), half()],
        sem=("arbitrary",), nsteps=nc // HG_SUB, step_fn=lambda: pl.program_id(0))


def _hgrn_bwd(h4, logits, norm_g, o_pre, states, dy, *, t, comm=None):
    nc = t // CH
    nt_dims = (((1,), (1,)), ((), ()))
    tn_dims = (((0,), (0,)), ((), ()))

    def body(h_ref, lg_ref, ng_ref, o_ref, st_ref, dy_ref, dh_ref, dlg_ref, dng_ref, ds_scr, dlb_scr,
             b_scr, tail_s, e_qa, e_ka, e_qb, e_kb, q_s, k_s, dqa_s, dka_s, dqb_s, dkb_s,
             qa_s, ka_s, qb_s, kb_s, v_s, do_s):
        n = pl.program_id(0)

        @pl.when(n == 0)
        def _():
            ds_scr[...] = jnp.zeros_like(ds_scr)
            dlb_scr[...] = jnp.zeros_like(dlb_scr)
            dng_ref[...] = jnp.zeros_like(dng_ref)

        heads = [slice(h * HG_K, (h + 1) * HG_K) for h in range(HG_HEADS)]
        lb = _lb_from_logits(lg_ref)
        causal = _tri(False)

        def chunk(c):
            rows = slice(c * CH, (c + 1) * CH)
            hq = h_ref[rows, 0:D]
            q, k, g, sq, sg, f = _hgrn_gates(hq, h_ref[rows, D:2 * D], lb)
            b_scr[...] = _apply01(jnp.where(causal, 1.0, 0.0).astype(BF), g)
            b = b_scr[...]
            b_mid = b_scr[CH // 2 - 1:CH // 2, :]
            b_last = b_scr[CH - 1:CH, :]
            q_s[...] = q
            k_s[...] = k
            for e_ref, s_ref, base, expo in ((e_qa, qa_s, q, b - b_mid), (e_ka, ka_s, k, b_mid - b),
                                             (e_qb, qb_s, q, b), (e_kb, kb_s, k, b_last - b)):
                e = jnp.exp(expo)
                e_ref[...] = e
                s_ref[...] = (base * e).astype(BF)
            v_s[...] = h_ref[rows, 2 * D:3 * D].astype(BF)
            dec = jnp.exp(b_last)
            for h, sl in enumerate(heads):
                gcol = slice(3 * D + h * HG_K, 3 * D + (h + 1) * HG_K)
                ngh = ng_ref[:, sl]
                sgate = _sig(h_ref[rows, gcol])
                o = o_ref[rows, sl]
                r = lax.rsqrt(jnp.mean(o * o, axis=-1, keepdims=True) + EPS)
                on = o * r
                dyh = dy_ref[rows, sl]
                dh_ref[rows, gcol] = (dyh * on * ngh * sgate * (1.0 - sgate)).astype(BF)
                dng_ref[:, sl] += jnp.sum(dyh * on * sgate, axis=0, keepdims=True)
                don = dyh * ngh * sgate
                do_s[:, sl] = (r * (don - on * jnp.mean(don * on, axis=-1, keepdims=True))).astype(BF)
            a = [jnp.where(causal, lax.dot_general(qa_s[:, sl], ka_s[:, sl], nt_dims, preferred_element_type=F32),
                           0.0).astype(BF) for sl in heads]
            da = [jnp.where(causal, lax.dot_general(do_s[:, sl], v_s[:, sl], nt_dims, preferred_element_type=F32),
                            0.0).astype(BF) for sl in heads]
            for h, sl in enumerate(heads):
                dh_ref[rows, 2 * D + h * HG_K:2 * D + (h + 1) * HG_K] = (
                    lax.dot_general(a[h], do_s[:, sl], tn_dims, preferred_element_type=F32)
                    + lax.dot_general(kb_s[:, sl], ds_scr[h].astype(BF), nt_dims, preferred_element_type=F32)
                ).astype(BF)
            for h, sl in enumerate(heads):
                dqa_s[:, sl] = jnp.dot(da[h], ka_s[:, sl], preferred_element_type=F32)
            for h, sl in enumerate(heads):
                dka_s[:, sl] = lax.dot_general(da[h], qa_s[:, sl], tn_dims, preferred_element_type=F32)
            for h, sl in enumerate(heads):
                dqb_s[:, sl] = jnp.dot(do_s[:, sl], st_ref[c, h], preferred_element_type=F32)
            for h, sl in enumerate(heads):
                dkb_s[:, sl] = jnp.dot(v_s[:, sl], ds_scr[h].astype(BF), preferred_element_type=F32)
            for h, sl in enumerate(heads):
                tail_s[:, sl] = jnp.sum(dec[:, sl] * st_ref[c, h].astype(F32) * ds_scr[h], axis=0, keepdims=True)
            for h, sl in enumerate(heads):
                ds_scr[h] = (lax.dot_general(do_s[:, sl], qb_s[:, sl], tn_dims, preferred_element_type=F32)
                             + dec[:, sl] * ds_scr[h])
            qv, kv = q_s[...], k_s[...]
            dqa, dka, dqb, dkb = dqa_s[...], dka_s[...], dqb_s[...], dkb_s[...]
            eqa, eka, eqb, ekb = e_qa[...], e_ka[...], e_qb[...], e_kb[...]
            dkb_kb = dkb * (kv * ekb)
            db_last = jnp.sum(dkb_kb, axis=0, keepdims=True) + tail_s[...]
            last_row = lax.broadcasted_iota(jnp.int32, (CH, D), 0) == CH - 1
            db = (dqa * (qv * eqa) - dka * (kv * eka) + dqb * (qv * eqb) - dkb_kb
                  + jnp.where(last_row, db_last, 0.0))
            dg = _apply01(jnp.where(_tri(True), 1.0, 0.0).astype(BF), db)
            dq = dqa * eqa + dqb * eqb
            dk = dka * eka + dkb * ekb
            dh_ref[rows, 0:D] = (dq * sq * (1.0 + hq * (1.0 - sq))).astype(BF)
            dfk = dg / f - dk
            dh_ref[rows, D:2 * D] = ((1.0 - lb) * dfk * sg * (1.0 - sg)).astype(BF)
            dlb_scr[...] += jnp.sum((1.0 - sg) * dfk, axis=0, keepdims=True)

        for c in reversed(range(HG_SUB)):
            chunk(c)

        @pl.when(n == nc // HG_SUB - 1)
        def _():
            dl0 = dlb_scr[...] * lb * (1.0 - lb)
            dlg_ref[0:1, :] = dl0
            dlg_ref[1:2, :] = -dl0

    steps = nc // HG_SUB
    blk = HG_SUB * CH
    rev = lambda n: (steps - 1 - n, 0)
    return _hosted_call(
        body, comm, (h4, logits, norm_g, o_pre, states, dy), name="hgrn_bwd", grid=(steps,),
        in_specs=[pl.BlockSpec((blk, 4 * D), rev),
                  pl.BlockSpec((2, D), lambda n: (0, 0)),
                  pl.BlockSpec((1, D), lambda n: (0, 0)),
                  pl.BlockSpec((blk, D), rev),
                  pl.BlockSpec((HG_SUB, HG_HEADS, HG_K, HG_K), lambda n: (steps - 1 - n, 0, 0, 0)),
                  pl.BlockSpec((blk, D), rev)],
        out_specs=[pl.BlockSpec((blk, 4 * D), rev),
                   pl.BlockSpec((2, D), lambda n: (0, 0)),
                   pl.BlockSpec((1, D), lambda n: (0, 0))],
        out_shape=[jax.ShapeDtypeStruct((t, 4 * D), BF), jax.ShapeDtypeStruct((2, D), F32),
                   jax.ShapeDtypeStruct((1, D), F32)],
        scratch_shapes=([pltpu.VMEM((HG_HEADS, HG_K, HG_K), F32), pltpu.VMEM((1, D), F32),
                         pltpu.VMEM((CH, D), F32), pltpu.VMEM((1, D), F32)]
                        + [pltpu.VMEM((CH, D), F32)] * 10 + [pltpu.VMEM((CH, D), BF)] * 6),
        sem=("arbitrary",), nsteps=steps, step_fn=lambda: pl.program_id(0))


def _place():
    x, y, c = lax.axis_index("x"), lax.axis_index("y"), lax.axis_index("c")
    return x, y, c, [(1 - x, y), (x, 1 - y), (1 - x, 1 - y)]


def _gather_comm(shards, mid):
    n = len(shards)
    r = [s.shape[0] for s in shards]

    def tools(ins, outs, sems):
        send_sems, recv_sems, local_sems = sems
        x, y, c, chips = _place()
        me, sib = (x, y, c), (x, y, 1 - c)

        def rows(w, dev):
            return outs[w].at[pl.ds((4 * dev[0] + 2 * dev[1] + dev[2]) * r[w], r[w]), :]

        def copy(kind, w, block, to, src=None):
            return pltpu.make_async_remote_copy(
                src_ref=rows(w, block) if src is None else src, dst_ref=rows(w, block),
                send_sem=send_sems.at[kind], recv_sem=recv_sems.at[kind], device_id=to, device_id_type=MESH)

        def all_of(kind):
            whole = outs[0].at[pl.ds(0, sum(r)), :]
            return pltpu.make_async_remote_copy(
                src_ref=whole, dst_ref=whole, send_sem=send_sems.at[kind], recv_sem=recv_sems.at[kind],
                device_id=me, device_id_type=MESH)

        mine = [pltpu.make_async_copy(ins[w], rows(w, me), local_sems.at[w]) for w in range(n)]
        return c, chips, me, sib, copy, all_of, mine

    def start(ins, outs, sems):
        c, chips, me, sib, copy, _, mine = tools(ins, outs, sems)
        for cp in mine:
            cp.start()
        for w in range(n):
            copy(0, w, me, sib, src=ins[w]).start()
            for j, chip in enumerate(chips):
                copy(1 + j, w, me, (*chip, c), src=ins[w]).start()

    def pass_on(ins, outs, sems):
        c, chips, _, sib, copy, all_of, _ = tools(ins, outs, sems)
        for j, chip in enumerate(chips):
            all_of(1 + j).wait_recv()
            for w in range(n):
                copy(4 + j, w, (*chip, c), sib).start()

    def finish(ins, outs, sems):
        _, _, _, _, _, all_of, mine = tools(ins, outs, sems)
        all_of(0).wait_recv()
        for j in range(3):
            all_of(4 + j).wait_recv()
        for kind in range(7):
            all_of(kind).wait_send()
        for cp in mine:
            cp.wait()

    return _Comm(shards, [jax.ShapeDtypeStruct((N_DEV * rw, D), BF) for rw in r],
                 [pltpu.SemaphoreType.DMA((7,)), pltpu.SemaphoreType.DMA((7,)), pltpu.SemaphoreType.DMA((n,))],
                 [(0.0, start), (mid, pass_on), (1.0, finish)])


def _pair_comm(grads):
    n = len(grads)
    r = [g.shape[0] // N_DEV for g in grads]

    def start(ins, outs, sems):
        send_sems, recv_sems = sems
        x, y, c, _ = _place()
        for w in range(n):
            for a in range(N_CHIP):
                pltpu.make_async_remote_copy(
                    src_ref=ins[w].at[pl.ds((2 * a + 1 - c) * r[w], r[w]), :], dst_ref=outs[w].at[a],
                    send_sem=send_sems.at[w], recv_sem=recv_sems.at[w],
                    device_id=(x, y, 1 - c), device_id_type=MESH).start()

    def finish(ins, outs, sems):
        send_sems, recv_sems = sems
        x, y, c, _ = _place()
        for w in range(n):
            pltpu.make_async_remote_copy(
                src_ref=outs[w], dst_ref=outs[w], send_sem=send_sems.at[w], recv_sem=recv_sems.at[w],
                device_id=(x, y, c), device_id_type=MESH).wait()

    return _Comm(grads, [jax.ShapeDtypeStruct((N_CHIP, rw, D), BF) for rw in r],
                 [pltpu.SemaphoreType.DMA((n,)), pltpu.SemaphoreType.DMA((n,))],
                 [(0.0, start), (1.0, finish)])


def _pair_add(grad, got, core, *, name):
    r = got.shape[1]

    def body(c_ref, g_ref, got_ref, o_ref):
        o_ref[0] = (g_ref[...].astype(F32) + got_ref[0].astype(F32)).astype(BF)

    grid_spec = pltpu.PrefetchScalarGridSpec(
        num_scalar_prefetch=1, grid=(N_CHIP,),
        in_specs=[pl.BlockSpec((r, D), lambda a, c_ref: (2 * a + c_ref[0], 0)),
                  pl.BlockSpec((1, r, D), lambda a, c_ref: (a, 0, 0))],
        out_specs=pl.BlockSpec((1, r, D), lambda a, c_ref: (a, 0, 0)))
    return _pcall(body, name=name, grid_spec=grid_spec,
                  out_shape=jax.ShapeDtypeStruct((N_CHIP, r, D), BF),
                  compiler_params=_cp(("parallel",)))(core, grad, got)


def _chip_comm(pair_sums):
    n = len(pair_sums)
    r = [p.shape[1] for p in pair_sums]
    off = [sum(r[:w]) for w in range(n)]

    def tools(ins, outs, sems):
        send_sems, recv_sems, local_sems = sems
        x, y, c, chips = _place()
        my_chip = 2 * x + y

        def slot(w):
            return outs[0].at[my_chip, pl.ds(off[w], r[w]), :]

        own = [pltpu.make_async_copy(ins[w].at[my_chip], slot(w), local_sems.at[w]) for w in range(n)]
        return x, y, c, chips, my_chip, slot, own, send_sems, recv_sems

    def start(ins, outs, sems):
        x, y, c, chips, my_chip, slot, own, send_sems, recv_sems = tools(ins, outs, sems)
        for cp in own:
            cp.start()
        for j, chip in enumerate(chips):
            for w in range(n):
                pltpu.make_async_remote_copy(
                    src_ref=ins[w].at[2 * chip[0] + chip[1]], dst_ref=slot(w), send_sem=send_sems.at[j],
                    recv_sem=recv_sems.at[j], device_id=(*chip, c), device_id_type=MESH).start()

    def finish(ins, outs, sems):
        x, y, c, chips, my_chip, slot, own, send_sems, recv_sems = tools(ins, outs, sems)
        whole = outs[0].at[my_chip]
        for j in range(3):
            pltpu.make_async_remote_copy(
                src_ref=whole, dst_ref=whole, send_sem=send_sems.at[j], recv_sem=recv_sems.at[j],
                device_id=(x, y, c), device_id_type=MESH).wait()
        for cp in own:
            cp.wait()

    return _Comm(pair_sums, [jax.ShapeDtypeStruct((N_CHIP, sum(r), D), BF)],
                 [pltpu.SemaphoreType.DMA((3,)), pltpu.SemaphoreType.DMA((3,)), pltpu.SemaphoreType.DMA((n,))],
                 [(0.0, start), (1.0, finish)])


def _adam_math(w, g, m, v):
    m = ADAM_B1 * m + (1.0 - ADAM_B1) * g
    v = ADAM_B2 * v + (1.0 - ADAM_B2) * (g * g)
    m_hat = m / (1.0 - ADAM_B1 ** ADAM_STEP)
    v_hat = v / (1.0 - ADAM_B2 ** ADAM_STEP)
    delta = -ADAM_LR * (m_hat / (jnp.sqrt(v_hat) + ADAM_EPS) + ADAM_WD * w)
    return delta, m, v


def _small_allreduce_adam(gpart, w, m, v):
    def body(g_ref, w_ref, m_ref, v_ref, gs_ref, d_ref, mo_ref, vo_ref, gath, send_sems, recv_sems):
        x, y, c, _ = _place()
        me = 4 * x + 2 * y + c
        gath[me] = g_ref[...]
        cps = []
        for d in range(1, N_DEV):
            peer = (x ^ (d >> 2), y ^ ((d >> 1) & 1), c ^ (d & 1))
            cps.append(pltpu.make_async_remote_copy(
                src_ref=g_ref, dst_ref=gath.at[me], send_sem=send_sems.at[d - 1],
                recv_sem=recv_sems.at[d - 1], device_id=peer, device_id_type=MESH))
        for cp in cps:
            cp.start()
        for cp in cps:
            cp.wait()
        g = gath[0]
        for k in range(1, N_DEV):
            g = g + gath[k]
        gs_ref[...] = g
        d_ref[...], mo_ref[...], vo_ref[...] = _adam_math(w_ref[...], g, m_ref[...], v_ref[...])

    shape = jax.ShapeDtypeStruct((SMALL_ROWS, D), F32)
    vm = pl.BlockSpec(memory_space=pltpu.VMEM)
    return _pcall(body, name="small_allreduce_adam", in_specs=[vm] * 4, out_specs=[vm] * 4,
                  out_shape=[shape] * 4,
                  scratch_shapes=[pltpu.VMEM((N_DEV, SMALL_ROWS, D), F32),
                                  pltpu.SemaphoreType.DMA((N_DEV - 1,)), pltpu.SemaphoreType.DMA((N_DEV - 1,))],
                  compiler_params=pltpu.CompilerParams(has_side_effects=True))(gpart, w, m, v)


def _adam(w, parts, index, m, v, *, name):
    rows = w.shape[0]
    tr = rows if rows <= 512 else rows // 2
    steps = rows // tr

    def body(w_ref, p_ref, m_ref, v_ref, g_ref, d_ref, mo_ref, vo_ref):
        g = p_ref[0].astype(F32)
        for a in range(1, N_CHIP):
            g = g + p_ref[a].astype(F32)
        g_ref[...] = g
        d_ref[...], mo_ref[...], vo_ref[...] = _adam_math(w_ref[...], g, m_ref[...], v_ref[...])

    spec = pl.BlockSpec((tr, D), lambda i: (i, 0))
    return _pcall(body, name=name, grid=(steps,),
                  in_specs=[spec, pl.BlockSpec((N_CHIP, tr, D), lambda i: (0, index * steps + i, 0)), spec, spec],
                  out_specs=[spec] * 4, out_shape=[jax.ShapeDtypeStruct((rows, D), F32)] * 4,
                  compiler_params=_cp(("parallel",)))(w, parts, m, v)


def _step(x, tgt, shards, norm_mix_g, b_in, sinks, logits, hgrn_norm_g, norm_ffn_g, norm_final_g):
    t = x.shape[0]
    big = dict(tm=1024, tn=1024, tk=4096)
    core = lax.axis_index("c").astype(jnp.int32).reshape(1)

    u1, (win_t,) = _rms_fwd(x, norm_mix_g, tm=512, name="rms_mix", comm=_gather_comm(shards[0:1], 0.5))
    (q, kv, h4, gates), (wg_t, wba, wbh, wout) = _inproj_fwd(
        u1, win_t, b_in, t=t, comm=_gather_comm([shards[1]] + shards[4:7], 0.8))
    (y_attn,), _ = _attn_fwd(q, kv, sinks, t=t)
    (y_hgrn, o_pre, states), (wu_t, wd) = _hgrn_fwd(h4, logits, hgrn_norm_g, t=t,
                                                    comm=_gather_comm(shards[2:4], 0.8))
    col = lambda j: j
    first, second = (lambda j: 0), (lambda j: 1)
    gate_tiles = [(gates, D, first), (gates, D, second)]

    def merge(prods, ex):
        (ya_, yb_), (ga, gb) = prods, ex
        sa, sb = _sig(ga), _sig(gb)
        return sa, sb, ya_ * sa * (1.0 - sa), yb_ * sb * (1.0 - sb), sa * ya_ + sb * yb_

    sig_a, sig_b, dgate_a, dgate_b, merged = _fmm(
        [y_attn, y_hgrn], [(0, wba, False), (1, wbh, False)], gate_tiles, merge,
        [(BF, D, D, first)] * 5, m=t, n=D, tm=512, tn=D, name="branch_merge")
    def resid_norm(prods, ex):
        (p,), (xv, gv) = prods, ex
        hv = xv + p
        return hv, hv * lax.rsqrt(jnp.mean(hv * hv, axis=-1, keepdims=True) + EPS) * gv

    h1, u2 = _fmm([merged], [(0, wout, False)], [(x, D, first)], resid_norm, [(F32, D, D, first), (BF, D, D, first)],
                  m=t, n=D, tm=1024, tn=D, name="out_proj", vecs=[norm_ffn_g])

    def swiglu(prods, ex):
        g_, u_ = prods
        s = _sig(g_)
        silu = g_ * s
        return u_ * s * (1.0 + g_ * (1.0 - s)), silu, silu * u_

    dz_dgate, dz_dup, z = _fmm([u2], [(0, wg_t, True), (0, wu_t, True)], [], swiglu,
                               [(BF, FFN, FFN // 2, col)] * 3, m=t, n=FFN, tm=1024, tn=FFN // 2,
                               name="ffn_gate_up")
    def loss_head(prods, ex):
        (p,), (hv, tv, gv) = prods, ex
        hv = hv + p
        r = lax.rsqrt(jnp.mean(hv * hv, axis=-1, keepdims=True) + EPS)
        xh = hv * r
        err = xh * gv - tv
        lp = jnp.sum(jnp.sum(err * err, axis=1, keepdims=True), axis=0, keepdims=True) * (0.5 / D)
        dy = err * (1.0 / D)
        dxh = dy * gv
        dh = r * (dxh - xh * jnp.mean(dxh * xh, axis=-1, keepdims=True))
        return dh, dh, jnp.sum(dy * xh, axis=0, keepdims=True), jnp.broadcast_to(lp, (1, 128))

    dh2, dh2_b, d_norm_final, loss_row = _fmm(
        [z], [(0, wd, False)], [(h1, D, first), (tgt, D, first)], loss_head, [(F32, D, D, first), (BF, D, D, first)],
        m=t, n=D, tm=512, tn=D, name="ffn_down_loss", vecs=[norm_final_g], sums=[D, 128])

    def swiglu_bwd(prods, ex):
        (dz,), (da_, db_) = prods, ex
        return dz * da_.astype(F32), dz * db_.astype(F32)

    ffn_tiles = [(dz_dgate, FFN // 2, col), (dz_dup, FFN // 2, col)]
    dgt, dup = _fmm([dh2_b], [(0, wd, True)], ffn_tiles, swiglu_bwd, [(BF, FFN, FFN // 2, col)] * 2,
                    m=t, n=FFN, tm=1024, tn=FFN // 2, name="d_gate_up")
    d_wd = _mm(z, dh2_b, m=FFN, n=D, k=t, ta=True, tm=256, tn=D, tk=4096, out_dtype=BF, name="d_w_down")
    (du2,) = _fmm([dgt, dup], [(0, wg_t, False), (1, wu_t, False)], [], lambda prods, ex: (prods[0] + prods[1],),
                  [(F32, D, 512, col)], m=t, n=D, tm=1024, tn=512, name="d_u2")
    d_wg = _mm(dgt, u2, m=FFN, n=D, k=t, ta=True, tm=256, tn=D, tk=4096, out_dtype=BF, name="d_w_gate")
    d_wu = _mm(dup, u2, m=FFN, n=D, k=t, ta=True, tm=256, tn=D, tk=4096, out_dtype=BF, name="d_w_up")
    dh1, dh1_b, d_norm_ffn = _rms_bwd(du2, h1, norm_ffn_g, dh2, tm=512, name="rms_ffn_bwd")
    d_wout = _mm(merged, dh1_b, m=D, n=D, k=t, ta=True, tm=256, tn=D, tk=4096, out_dtype=BF, name="d_w_out")

    def merge_bwd(prods, ex):
        (dm,), (sa, sb, ca, cb) = prods, ex
        dgate = jnp.concatenate([dm * ca.astype(F32), dm * cb.astype(F32)], axis=1)
        return dm * sa.astype(F32), dm * sb.astype(F32), dgate

    ffn_grads = (d_wg, d_wu, d_wd)
    (dya, dyb, dgates), got = _fmm(
        [dh1_b], [(0, wout, True)], [(a, D, first) for a in (sig_a, sig_b, dgate_a, dgate_b)], merge_bwd,
        [(BF, D, D, first), (BF, D, D, first), (BF, 2 * D, 2 * D, first)],
        m=t, n=D, tm=512, tn=D, name="d_merge", comm=_pair_comm(ffn_grads))
    pair_ffn = [_pair_add(g, r, core, name="pair_add_ffn%d" % i) for i, (g, r) in enumerate(zip(ffn_grads, got))]
    dy_attn = _mm(dya, wba, m=t, n=D, k=D, tb=True, out_dtype=BF, name="d_y_attn", **big)
    dy_hgrn = _mm(dyb, wbh, m=t, n=D, k=D, tb=True, name="d_y_hgrn", **big)
    d_wba = _mm(y_attn, dya, m=D, n=D, k=t, ta=True, tm=256, tn=D, tk=4096, out_dtype=BF, name="d_w_ba")
    d_wbh = _mm(y_hgrn, dyb, m=D, n=D, k=t, ta=True, tm=256, tn=D, tk=4096, out_dtype=BF, name="d_w_bh")
    sq_grads = (d_wba, d_wbh, d_wout)
    (dq, dkv, d_sinks), (parts_ffn, *got) = _attn_bwd(
        q, kv, sinks, dy_attn, t=t, comm=_both(_chip_comm(pair_ffn), _pair_comm(sq_grads)))
    pair_sq = [_pair_add(g, r, core, name="pair_add_sq%d" % i) for i, (g, r) in enumerate(zip(sq_grads, got))]
    (dh4, d_logits, d_hgrn_norm), (parts_sq,) = _hgrn_bwd(h4, logits, hgrn_norm_g, o_pre, states, dy_hgrn,
                                                           t=t, comm=_chip_comm(pair_sq))
    dps = (dq, dkv, dh4, dgates)
    d_win_t, d_b_in = _inproj_bwd_w(dps, u1, t=t)
    half0, got_in = _inproj_bwd_x(dps, win_t, x, norm_mix_g, dh1, t=t, part=0, comm=_pair_comm([d_win_t]))
    pair_in = _pair_add(d_win_t, got_in[0], core, name="pair_add_w_in")
    (grad_x, d_norm_mix), (parts_in,) = _inproj_bwd_x(dps, win_t, x, norm_mix_g, dh1, t=t, part=1, prev=half0,
                                                      comm=_chip_comm([pair_in]))

    small_grads = (d_norm_mix, d_b_in, d_sinks, d_logits, d_hgrn_norm, d_norm_ffn, d_norm_final)
    return loss_row, grad_x, (parts_in, parts_ffn, parts_sq), small_grads


def _pack_small(norm_mix, b_in, sinks, logits, hgrn_norm, norm_ffn, norm_final, extra=None):
    pad = lambda a, n: jnp.pad(a.reshape(1, -1), ((0, 0), (0, n - a.size)))
    rows = [norm_mix.reshape(1, D), hgrn_norm.reshape(1, D), norm_ffn.reshape(1, D), norm_final.reshape(1, D),
            logits.reshape(2, D), pad(sinks.reshape(-1)[:16], D),
            jnp.zeros((1, D), F32) if extra is None else pad(extra, D),
            pad(b_in, 8 * D).reshape(8, D)]
    return jnp.concatenate(rows, axis=0).astype(F32)


def _unpack_small(p):
    return dict(norm_mix_g=p[0:1], hgrn_norm_g=p[1:2], norm_ffn_g=p[2:3], norm_final_g=p[3],
                hgrn_lb_logits=p[4:6], attn_sinks=p[6:7, 0:16], extra=p[7],
                b_in=p[8:16].reshape(1, 8 * D)[:, :IN_W])


def kernel(x, norm_mix_g, w_in, b_in, attn_sinks, hgrn_lb_logits, hgrn_norm_g, w_branch_attn, w_branch_hgrn, w_out, norm_ffn_g, w_ffn_gate, w_ffn_up, w_ffn_down, norm_final_g, loss_target, m_norm_mix_g, m_w_in, m_b_in, m_attn_sinks, m_hgrn_lb_logits, m_hgrn_norm_g, m_w_branch_attn, m_w_branch_hgrn, m_w_out, m_norm_ffn_g, m_w_ffn_gate, m_w_ffn_up, m_w_ffn_down, m_norm_final_g, v_norm_mix_g, v_w_in, v_b_in, v_attn_sinks, v_hgrn_lb_logits, v_hgrn_norm_g, v_w_branch_attn, v_w_branch_hgrn, v_w_out, v_norm_ffn_g, v_w_ffn_gate, v_w_ffn_up, v_w_ffn_down, v_norm_final_g):
    shards = [w_in[0].T.astype(BF), w_ffn_gate[0].T.astype(BF), w_ffn_up[0].T.astype(BF),
              w_ffn_down[0].astype(BF), w_branch_attn[0].astype(BF), w_branch_hgrn[0].astype(BF),
              w_out[0].astype(BF)]
    loss_row, grad_x, grad_parts, small_grads = _step(
        x[0], loss_target[0], shards, norm_mix_g, b_in, attn_sinks, hgrn_lb_logits, hgrn_norm_g,
        norm_ffn_g, norm_final_g.reshape(1, D))

    d_norm_mix, d_b_in, d_sinks, d_logits, d_hgrn_norm, d_norm_ffn, d_norm_final = small_grads
    g_small = _pack_small(d_norm_mix, d_b_in, d_sinks[:, :16], d_logits, d_hgrn_norm, d_norm_ffn,
                          d_norm_final, extra=loss_row[0, 0:1])
    w_small = _pack_small(norm_mix_g, b_in, attn_sinks, hgrn_lb_logits, hgrn_norm_g, norm_ffn_g, norm_final_g)
    m_small = _pack_small(m_norm_mix_g, m_b_in, m_attn_sinks, m_hgrn_lb_logits, m_hgrn_norm_g, m_norm_ffn_g,
                          m_norm_final_g)
    v_small = _pack_small(v_norm_mix_g, v_b_in, v_attn_sinks, v_hgrn_lb_logits, v_hgrn_norm_g, v_norm_ffn_g,
                          v_norm_final_g)
    small = [_unpack_small(p) for p in _small_allreduce_adam(g_small, w_small, m_small, v_small)]
    loss = small[0]["extra"][0]

    names = ["w_in", "w_ffn_gate", "w_ffn_up", "w_ffn_down", "w_branch_attn", "w_branch_hgrn", "w_out"]
    w_full = dict(w_in=(w_in, m_w_in, v_w_in), w_ffn_gate=(w_ffn_gate, m_w_ffn_gate, v_w_ffn_gate),
                  w_ffn_up=(w_ffn_up, m_w_ffn_up, v_w_ffn_up), w_ffn_down=(w_ffn_down, m_w_ffn_down, v_w_ffn_down),
                  w_branch_attn=(w_branch_attn, m_w_branch_attn, v_w_branch_attn),
                  w_branch_hgrn=(w_branch_hgrn, m_w_branch_hgrn, v_w_branch_hgrn),
                  w_out=(w_out, m_w_out, v_w_out))
    parts_in, parts_ffn, parts_sq = grad_parts
    where = [(parts_in, 0), (parts_ffn, 0), (parts_ffn, 1), (parts_ffn, 2), (parts_sq, 0), (parts_sq, 1), (parts_sq, 2)]
    big = {}
    for i, name in enumerate(names):
        view = (lambda a: a[0].T) if i < 3 else (lambda a: a[0])
        back = (lambda a: a.T[None]) if i < 3 else (lambda a: a[None])
        wv, mv, vv = w_full[name]
        res = _adam(view(wv), where[i][0], where[i][1], view(mv), view(vv), name="adam_" + name)
        big[name] = [back(a) for a in res]

    order = ["norm_mix_g", "w_in", "b_in", "attn_sinks", "hgrn_lb_logits", "hgrn_norm_g", "w_branch_attn",
             "w_branch_hgrn", "w_out", "norm_ffn_g", "w_ffn_gate", "w_ffn_up", "w_ffn_down", "norm_final_g"]
    outs = [loss, grad_x[None]]
    for kind in range(4):
        for name in order:
            outs.append(big[name][kind] if name in big else small[kind][name])
    return tuple(outs)
```

```python
import math

import jax
import jax.numpy as jnp
from jax import lax
from jax.experimental import pallas as pl
from jax.experimental.pallas import tpu as pltpu

F32 = jnp.float32
BF = jnp.bfloat16
MESH = pl.DeviceIdType.MESH

D = 1024
HEAD = 64
N_PAIR = 8
BLK = 128
CH = 64
HG_SUB = 2
HG_HEADS = 8
HG_K = 128
FFN = 2816
IN_W = 7424
N_DEV = 8
N_CHIP = 4
EPS = 1e-6
NEG = -1e30
SCALE = 1.0 / math.sqrt(HEAD)
VMEM_LIMIT = 56 * 1024 * 1024
WT = 256

ADAM_LR, ADAM_B1, ADAM_B2, ADAM_EPS, ADAM_WD, ADAM_STEP = 0.001, 0.9, 0.999, 1e-08, 0.01, 10

SLAB_R = (IN_W // N_DEV, FFN // N_DEV, FFN // N_DEV, FFN // N_DEV, D // N_DEV, D // N_DEV, D // N_DEV)
SLAB_ROWS = sum(SLAB_R)
SLAB_OFF = tuple(sum(SLAB_R[:i]) for i in range(len(SLAB_R)))
N_W = len(SLAB_R)
GRP_OFF = (0, D // WT, (D + 256) // WT, (5 * D + 256) // WT)
GRP_N = (D // WT, 256 // WT, 4 * D // WT, 2 * D // WT)
SMALL_ROWS = 16


_NN = (((1,), (0,)), ((), ()))
_NT = (((1,), (1,)), ((), ()))
_TN = (((0,), (0,)), ((), ()))


def _pcall(body, **kw):
    return pl.pallas_call(body, **kw)


def _cp(sem=None, **kw):
    return pltpu.CompilerParams(dimension_semantics=sem, vmem_limit_bytes=VMEM_LIMIT, **kw)


def _sig(v):
    return 0.5 * jnp.tanh(0.5 * v) + 0.5


def _accum(ref, val, first):
    @pl.when(first)
    def _():
        ref[...] = val

    @pl.when(jnp.logical_not(first))
    def _():
        ref[...] += val


class _Comm:
    def __init__(self, ins, out_shapes, sem_shapes, phases):
        self.ins, self.out_shapes, self.sem_shapes, self.phases = list(ins), list(out_shapes), list(sem_shapes), phases


def _both(a, b):
    ni, no, ns = len(a.ins), len(a.out_shapes), len(a.sem_shapes)

    def of_a(fn):
        return lambda ins, outs, sems: fn(ins[:ni], outs[:no], sems[:ns])

    def of_b(fn):
        return lambda ins, outs, sems: fn(ins[ni:], outs[no:], sems[ns:])

    return _Comm(a.ins + b.ins, a.out_shapes + b.out_shapes, a.sem_shapes + b.sem_shapes,
                 [(f, of_a(fn)) for f, fn in a.phases] + [(f, of_b(fn)) for f, fn in b.phases])


def _host(body, comm, n_in, n_out, n_scr, nsteps, step_fn):
    if comm is None:
        return body
    ci, co = len(comm.ins), len(comm.out_shapes)

    def wrapped(*refs):
        p = 0
        ins, p = refs[p:p + n_in], p + n_in
        cins, p = refs[p:p + ci], p + ci
        outs, p = refs[p:p + n_out], p + n_out
        couts, p = refs[p:p + co], p + co
        scr, p = refs[p:p + n_scr], p + n_scr
        csems = refs[p:]
        step = step_fn()
        for frac, fn in comm.phases:
            if frac < 1.0:
                @pl.when(step == int(round(frac * (nsteps - 1))))
                def _(fn=fn):
                    fn(cins, couts, csems)
        body(*ins, *outs, *scr)
        for frac, fn in comm.phases:
            if frac >= 1.0:
                @pl.when(step == nsteps - 1)
                def _(fn=fn):
                    fn(cins, couts, csems)

    return wrapped


def _hosted_call(body, comm, args, *, name, grid, in_specs, out_specs, out_shape, scratch_shapes, sem,
                 nsteps, step_fn, aliases=None):
    n_in, n_out, n_scr = len(in_specs), len(out_specs), len(scratch_shapes)
    args = list(args)
    extra = {}
    if comm is not None:
        in_specs = list(in_specs) + [_hbm_spec()] * len(comm.ins)
        out_specs = list(out_specs) + [_hbm_spec()] * len(comm.out_shapes)
        out_shape = list(out_shape) + comm.out_shapes
        scratch_shapes = list(scratch_shapes) + comm.sem_shapes
        args += comm.ins
        extra = dict(has_side_effects=True)
    outs = _pcall(_host(body, comm, n_in, n_out, n_scr, nsteps, step_fn), name=name, grid=grid,
                  in_specs=in_specs, out_specs=out_specs, out_shape=out_shape, scratch_shapes=scratch_shapes,
                  input_output_aliases=aliases or {}, compiler_params=_cp(sem, **extra))(*args)
    return list(outs[:n_out]), list(outs[n_out:])


def _hbm_spec():
    return pl.BlockSpec(memory_space=pl.ANY)


def _wgrad(a, b, *, name):
    (t, m), n = a.shape, b.shape[1]

    def body(a_ref, b_ref, o_ref):
        o_ref[...] = lax.dot_general(a_ref[...], b_ref[...], _TN, preferred_element_type=F32).astype(BF)

    return _pcall(body, name=name, grid=(m // WT,),
                  in_specs=[pl.BlockSpec((t, WT), lambda i: (0, i)), pl.BlockSpec((t, n), lambda i: (0, 0))],
                  out_specs=pl.BlockSpec((WT, n), lambda i: (i, 0)),
                  out_shape=jax.ShapeDtypeStruct((m, n), BF), compiler_params=_cp(("parallel",)))(a, b)


def _fmm(lhs, rhs, extras, epilogue, outs, *, m, n, tm, tn, name, comm=None, vecs=(), consts=(), sums=()):
    tm, tn = min(tm, m), min(tn, n)
    assert m % tm == 0 and n % tn == 0 and (not sums or tn == n), (name, m, n, tm, tn)
    in_specs, args = [], []
    for a in lhs:
        in_specs.append(pl.BlockSpec((tm, a.shape[1]), lambda i, j: (i, 0)))
        args.append(a)
    for li, b, tb in rhs:
        k = lhs[li].shape[1]
        in_specs.append(pl.BlockSpec((tn, k), lambda i, j: (j, 0)) if tb
                        else pl.BlockSpec((k, tn), lambda i, j: (0, j)))
        args.append(b)
    for arr, w, col in extras:
        in_specs.append(pl.BlockSpec((tm, w), lambda i, j, col=col: (i, col(j))))
        args.append(arr)
    for vec in vecs:
        in_specs.append(pl.BlockSpec((1, tn), lambda i, j: (0, j)))
        args.append(vec)
    for whole in consts:
        in_specs.append(pl.BlockSpec(whole.shape, lambda i, j: (0, 0)))
        args.append(whole)
    out_specs = [pl.BlockSpec((tm, w), lambda i, j, col=col: (i, col(j))) for _, _, w, col in outs]
    out_shape = [jax.ShapeDtypeStruct((m, total), dt) for dt, total, _, _ in outs]
    for w in sums:
        out_specs.append(pl.BlockSpec((1, w), lambda i, j: (0, 0)))
        out_shape.append(jax.ShapeDtypeStruct((1, w), F32))
    nl, nr, ne, no = len(lhs), len(rhs), len(extras) + len(vecs) + len(consts), len(outs)

    def body(*refs):
        prods = []
        for r, (li, _, tb) in enumerate(rhs):
            prods.append(lax.dot_general(refs[li][...], refs[nl + r][...], _NT if tb else _NN,
                                         preferred_element_type=F32))
        vals = epilogue(prods, [ref[...] for ref in refs[nl + nr:nl + nr + ne]])
        o_refs = refs[nl + nr + ne:]
        for o_ref, v in zip(o_refs[:no], vals[:no]):
            o_ref[...] = v.astype(o_ref.dtype)
        for s_ref, v in zip(o_refs[no:], vals[no:]):
            _accum(s_ref, v, pl.program_id(0) == 0)

    gm, gn = m // tm, n // tn
    res, comm_res = _hosted_call(
        body, comm, args, name=name, grid=(gm, gn), in_specs=in_specs, out_specs=out_specs,
        out_shape=out_shape, scratch_shapes=[], sem=("arbitrary", "arbitrary"), nsteps=gm * gn,
        step_fn=lambda: pl.program_id(0) * gn + pl.program_id(1))
    return res if comm is None else (res, comm_res)


def _grp_of(i):
    return [jnp.logical_and(i >= GRP_OFF[g], i < GRP_OFF[g] + GRP_N[g]) for g in range(4)]


def _grp_idx(i, g):
    return jnp.clip(i - GRP_OFF[g], 0, GRP_N[g] - 1)


def _inproj_fwd(u, win_t, b_in, *, t, comm=None):
    n_tiles = IN_W // WT
    dims = (((1,), (1,)), ((), ()))
    dtypes = (BF, BF, F32, F32)

    def body(u_ref, w_ref, b_ref, *o_refs):
        i = pl.program_id(0)
        p = lax.dot_general(u_ref[...], w_ref[...], dims, preferred_element_type=F32) + b_ref[...]
        for g, pred in enumerate(_grp_of(i)):
            @pl.when(pred)
            def _(g=g):
                o_refs[g][...] = p.astype(dtypes[g])

    return _hosted_call(
        body, comm, (u, win_t, b_in), name="inproj_fwd", grid=(n_tiles,),
        in_specs=[pl.BlockSpec((t, D), lambda i: (0, 0)),
                  pl.BlockSpec((WT, D), lambda i: (i, 0)),
                  pl.BlockSpec((1, WT), lambda i: (0, i))],
        out_specs=[pl.BlockSpec((t, WT), lambda i, g=g: (0, _grp_idx(i, g))) for g in range(4)],
        out_shape=[jax.ShapeDtypeStruct((t, GRP_N[g] * WT), dtypes[g]) for g in range(4)],
        scratch_shapes=[], sem=("arbitrary",), nsteps=n_tiles, step_fn=lambda: pl.program_id(0))


def _inproj_bwd_x(dps, win_t, x, g, resid, *, t, part, prev=None, comm=None):
    n_tiles = IN_W // WT
    n_row = 4 if t >= 2048 else 2
    tm = t // n_row
    per = 1 if part == 0 else n_row - 1
    row = lambda i: part + i

    n_chunks = 8
    h_first, g_first = 2, 6
    sub = D // WT

    def w_block(l):
        return jnp.where(l == 0, GRP_OFF[0], jnp.where(l == 1, GRP_OFF[1], GRP_OFF[2] + sub * (l - h_first)))

    def body(d0, d1, d2, d3, w0, w1, w2, w3, x_ref, g_ref, r_ref, *rest):
        dg_prev = rest[0] if prev is not None else None
        o_ref, dg_ref, acc_ref = rest[-3], rest[-2], rest[-1]
        i, l = pl.program_id(0), pl.program_id(1)

        @pl.when(l == 1)
        def _():
            acc_ref[...] += jnp.dot(d1[...], w0[...], preferred_element_type=F32)

        w = jnp.concatenate([w0[...], w1[...], w2[...], w3[...]], axis=0)
        for pred, d_ref in ((l == 0, d0), (jnp.logical_and(l >= h_first, l < g_first), d2), (l >= g_first, d3)):
            @pl.when(pred)
            def _(d_ref=d_ref):
                _accum(acc_ref, jnp.dot(d_ref[...], w, preferred_element_type=F32), l == 0)

        @pl.when(l == n_chunks - 1)
        def _():
            xv = x_ref[...]
            r = lax.rsqrt(jnp.mean(xv * xv, axis=-1, keepdims=True) + EPS)
            xh = xv * r
            du = acc_ref[...]
            dxh = du * g_ref[...]
            o_ref[...] = r_ref[...] + r * (dxh - xh * jnp.mean(dxh * xh, axis=-1, keepdims=True))
            dg = jnp.sum(du * xh, axis=0, keepdims=True)
            if dg_prev is not None:
                dg = dg + jnp.where(i == 0, 1.0, 0.0) * dg_prev[...]
            _accum(dg_ref, dg, i == 0)

    rows = lambda w: pl.BlockSpec((tm, w), lambda i, l: (row(i), 0))
    in_specs = ([rows(D), rows(256),
                 pl.BlockSpec((tm, D), lambda i, l: (row(i), jnp.clip(l - h_first, 0, 3))),
                 pl.BlockSpec((tm, D), lambda i, l: (row(i), jnp.clip(l - g_first, 0, 1)))]
                + [pl.BlockSpec((WT, D), lambda i, l, o=o: (w_block(l) + o, 0)) for o in range(sub)]
                + [rows(D), pl.BlockSpec((1, D), lambda i, l: (0, 0)), rows(D)])
    args = list(dps) + [win_t] * sub + [x, g, resid]
    aliases = None
    if prev is not None:
        in_specs += [pl.BlockSpec((1, D), lambda i, l: (0, 0)), _hbm_spec()]
        args += [prev[1], prev[0]]
        aliases = {len(args) - 1: 0}
    return _hosted_call(
        body, comm, args, name="inproj_bwd_x%d" % part, grid=(per, n_chunks), in_specs=in_specs,
        out_specs=[rows(D), pl.BlockSpec((1, D), lambda i, l: (0, 0))],
        out_shape=[jax.ShapeDtypeStruct((t, D), F32), jax.ShapeDtypeStruct((1, D), F32)],
        scratch_shapes=[pltpu.VMEM((tm, D), F32)], sem=("arbitrary", "arbitrary"), nsteps=per * n_chunks,
        step_fn=lambda: pl.program_id(0) * n_chunks + pl.program_id(1), aliases=aliases)


def _inproj_bwd_w(dps, u, *, t):
    n_tiles = IN_W // WT
    dims = (((0,), (0,)), ((), ()))

    def body(d0, d1, d2, d3, u_ref, o_ref, db_ref):
        i = pl.program_id(0)
        uv = u_ref[...]
        for g, (pred, d_ref) in enumerate(zip(_grp_of(i), (d0, d1, d2, d3))):
            @pl.when(pred)
            def _(d_ref=d_ref):
                dv = d_ref[...]
                o_ref[...] = lax.dot_general(dv, uv, dims, preferred_element_type=F32).astype(BF)
                db_ref[...] = jnp.sum(dv.astype(F32), axis=0, keepdims=True)

    return _pcall(body, name="inproj_bwd_w", grid=(n_tiles,),
                  in_specs=[pl.BlockSpec((t, WT), lambda i, g=g: (0, _grp_idx(i, g))) for g in range(4)]
                  + [pl.BlockSpec((t, D), lambda i: (0, 0))],
                  out_specs=[pl.BlockSpec((WT, D), lambda i: (i, 0)),
                             pl.BlockSpec((1, WT), lambda i: (0, i))],
                  out_shape=[jax.ShapeDtypeStruct((IN_W, D), BF), jax.ShapeDtypeStruct((1, IN_W), F32)],
                  compiler_params=_cp(("arbitrary",)))(*dps, u)


def _row_spec(tm, width, col=0):
    return pl.BlockSpec((tm, width), lambda i: (i, col))


def _vec_spec(width):
    return pl.BlockSpec((1, width), lambda i: (0, 0))


def _rms_fwd(x, g, *, tm, name, comm=None):
    t = x.shape[0]
    tm = min(tm, t)

    def body(x_ref, g_ref, u_ref):
        xv = x_ref[...]
        r = lax.rsqrt(jnp.mean(xv * xv, axis=-1, keepdims=True) + EPS)
        u_ref[...] = (xv * r * g_ref[...]).astype(BF)

    (u,), comm_res = _hosted_call(
        body, comm, (x, g), name=name, grid=(t // tm,), in_specs=[_row_spec(tm, D), _vec_spec(D)],
        out_specs=[_row_spec(tm, D)], out_shape=[jax.ShapeDtypeStruct((t, D), BF)], scratch_shapes=[],
        sem=("arbitrary",), nsteps=t // tm, step_fn=lambda: pl.program_id(0))
    return u if comm is None else (u, comm_res)


def _rms_bwd(du, x, g, resid, *, tm, name):
    t = x.shape[0]
    tm = min(tm, t)

    def body(du_ref, x_ref, g_ref, r_ref, dx_ref, dxb_ref, dg_ref):
        xv = x_ref[...]
        r = lax.rsqrt(jnp.mean(xv * xv, axis=-1, keepdims=True) + EPS)
        xh = xv * r
        duv = du_ref[...]
        dxh = duv * g_ref[...]
        dx = r_ref[...] + r * (dxh - xh * jnp.mean(dxh * xh, axis=-1, keepdims=True))
        dx_ref[...] = dx
        dxb_ref[...] = dx.astype(BF)
        _accum(dg_ref, jnp.sum(duv * xh, axis=0, keepdims=True), pl.program_id(0) == 0)

    return _pcall(body, name=name, grid=(t // tm,),
                  in_specs=[_row_spec(tm, D), _row_spec(tm, D), _vec_spec(D), _row_spec(tm, D)],
                  out_specs=[_row_spec(tm, D), _row_spec(tm, D), _vec_spec(D)],
                  out_shape=[jax.ShapeDtypeStruct((t, D), F32), jax.ShapeDtypeStruct((t, D), BF),
                             jax.ShapeDtypeStruct((1, D), F32)],
                  compiler_params=_cp(("arbitrary",)))(du, x, g, resid)


def _attn_kv_tiles(kprev, kcur):
    kv = jnp.concatenate([kprev, kcur], axis=0).astype(F32)
    lo = lax.broadcasted_iota(jnp.int32, (2 * BLK, 128), 1) < HEAD
    tiles = []
    for part in (kv[:, 0:128], kv[:, 128:256]):
        rolled = pltpu.roll(part, HEAD, 1)
        z = jnp.zeros_like(part)
        tiles.append(((jnp.where(lo, part, z).astype(BF), jnp.where(lo, z, rolled).astype(BF)),
                      (jnp.where(lo, rolled, z).astype(BF), jnp.where(lo, z, part).astype(BF))))
    k_t, v_t = tiles
    return [(jnp.concatenate(k_t[h], axis=0), jnp.concatenate(v_t[h], axis=0)) for h in range(2)]


def _attn_mask(i):
    qi = lax.broadcasted_iota(jnp.int32, (BLK, 2 * BLK), 0)
    kj = lax.broadcasted_iota(jnp.int32, (BLK, 2 * BLK), 1)
    first_key = jnp.where(i == 0, BLK, 0)
    in_prev = jnp.logical_and(jnp.logical_and(kj < BLK, kj > qi), kj >= first_key)
    in_cur = jnp.logical_and(kj >= BLK, kj - BLK <= qi)
    return jnp.logical_or(in_prev, in_cur)


def _attn_probs(s, sink, valid):
    s = jnp.where(valid, s * SCALE, NEG)
    mx = jnp.maximum(jnp.max(s, axis=-1, keepdims=True), sink)
    e = jnp.exp(s - mx)
    es = jnp.exp(sink - mx)
    inv = 1.0 / (jnp.sum(e, axis=-1, keepdims=True) + es)
    return e * inv, es * inv


_KEYS = 2 * BLK


def _pair(ref, j):
    return ref[:, j * 128:(j + 1) * 128]


def _attn_fwd(q, kv, sinks, *, t, comm=None):
    nb = t // BLK

    def body(sink_ref, q_ref, kp_ref, kc_ref, o_ref):
        valid = _attn_mask(pl.program_id(0))
        tiles = _attn_kv_tiles(kp_ref[...], kc_ref[...])
        s = [lax.dot_general(_pair(q_ref, j), tiles[j // 4][0], _NT, preferred_element_type=F32)
             for j in range(N_PAIR)]
        p = []
        for j in range(N_PAIR):
            pe, _ = _attn_probs(s[j][:, 0:_KEYS], sink_ref[0, 2 * j], valid)
            po, _ = _attn_probs(s[j][:, _KEYS:2 * _KEYS], sink_ref[0, 2 * j + 1], valid)
            p.append(jnp.concatenate([pe.astype(BF), po.astype(BF)], axis=1))
        for j in range(N_PAIR):
            o_ref[:, j * 128:(j + 1) * 128] = jnp.dot(p[j], tiles[j // 4][1],
                                                      preferred_element_type=F32).astype(BF)

    return _hosted_call(
        body, comm, (sinks, q, kv, kv), name="attn_fwd", grid=(nb,),
        in_specs=[pl.BlockSpec(memory_space=pltpu.SMEM),
                  pl.BlockSpec((BLK, D), lambda i: (i, 0)),
                  pl.BlockSpec((BLK, 256), lambda i: (jnp.maximum(i - 1, 0), 0)),
                  pl.BlockSpec((BLK, 256), lambda i: (i, 0))],
        out_specs=[pl.BlockSpec((BLK, D), lambda i: (i, 0))],
        out_shape=[jax.ShapeDtypeStruct((t, D), BF)],
        scratch_shapes=[], sem=("arbitrary",), nsteps=nb, step_fn=lambda: pl.program_id(0))


def _attn_bwd(q, kv, sinks, do, *, t, comm=None):
    nb = t // BLK
    last = nb - 1

    def body(sink_ref, q_ref, kp_ref, kc_ref, do_ref, dq_ref, dkv_ref, ds_ref, carry_ref):
        i = pl.program_id(0)

        @pl.when(i == 0)
        def _():
            ds_ref[...] = jnp.zeros_like(ds_ref)
            carry_ref[...] = jnp.zeros_like(carry_ref)

        @pl.when(i < nb)
        def _():
            valid = _attn_mask(i)
            tiles = _attn_kv_tiles(kp_ref[...], kc_ref[...])
            lane1 = lax.broadcasted_iota(jnp.int32, (1, 128), 1)
            dsink = jnp.zeros((1, 128), F32)
            s = [lax.dot_general(_pair(q_ref, j), tiles[j // 4][0], _NT, preferred_element_type=F32)
                 for j in range(N_PAIR)]
            dp = [lax.dot_general(_pair(do_ref, j), tiles[j // 4][1], _NT, preferred_element_type=F32)
                  for j in range(N_PAIR)]
            p_all, ds_all = [], []
            for j in range(N_PAIR):
                halves = []
                for par in range(2):
                    cols = slice(par * _KEYS, (par + 1) * _KEYS)
                    p, ps = _attn_probs(s[j][:, cols], sink_ref[0, 2 * j + par], valid)
                    dpj = dp[j][:, cols]
                    dd = jnp.sum(p * dpj, axis=-1, keepdims=True)
                    dsink = dsink + jnp.where(lane1 == 2 * j + par,
                                              -jnp.sum(ps * dd, axis=0, keepdims=True), 0.0)
                    halves.append((p.astype(BF), (p * (dpj - dd)).astype(BF)))
                p_all.append(jnp.concatenate([halves[0][0], halves[1][0]], axis=1))
                ds_all.append(jnp.concatenate([halves[0][1], halves[1][1]], axis=1))
            for j in range(N_PAIR):
                dq_ref[:, j * 128:(j + 1) * 128] = (
                    jnp.dot(ds_all[j], tiles[j // 4][0], preferred_element_type=F32) * SCALE).astype(BF)
            ds_ref[...] += dsink
            gk, gv = [], []
            for h in range(2):
                grp = range(4 * h, 4 * h + 4)
                q_rows = jnp.concatenate([_pair(q_ref, j) for j in grp], axis=0)
                do_rows = jnp.concatenate([_pair(do_ref, j) for j in grp], axis=0)
                g_k = lax.dot_general(jnp.concatenate([ds_all[j] for j in grp], axis=0), q_rows, _TN,
                                      preferred_element_type=F32)
                g_v = lax.dot_general(jnp.concatenate([p_all[j] for j in grp], axis=0), do_rows, _TN,
                                      preferred_element_type=F32)
                gk.append((g_k[0:_KEYS], g_k[_KEYS:2 * _KEYS]))
                gv.append((g_v[0:_KEYS], g_v[_KEYS:2 * _KEYS]))
            lo = lax.broadcasted_iota(jnp.int32, (2 * BLK, 128), 1) < HEAD
            zero = jnp.zeros((2 * BLK, 128), F32)

            def unpad(g):
                return (jnp.where(lo, g[0][0] + pltpu.roll(g[0][1], HEAD, 1), zero)
                        + jnp.where(lo, zero, pltpu.roll(g[1][0], HEAD, 1) + g[1][1]))

            dk = unpad(gk) * SCALE
            dv = unpad(gv)
            dkv_ref[:, 0:128] = (carry_ref[:, 0:128] + dk[0:BLK]).astype(BF)
            dkv_ref[:, 128:256] = (carry_ref[:, 128:256] + dv[0:BLK]).astype(BF)
            carry_ref[:, 0:128] = dk[BLK:2 * BLK]
            carry_ref[:, 128:256] = dv[BLK:2 * BLK]

        @pl.when(i == nb)
        def _():
            dkv_ref[...] = carry_ref[...].astype(BF)

    return _hosted_call(
        body, comm, (sinks, q, kv, kv, do), name="attn_bwd", grid=(nb + 1,),
        in_specs=[pl.BlockSpec(memory_space=pltpu.SMEM),
                  pl.BlockSpec((BLK, D), lambda i: (jnp.minimum(i, last), 0)),
                  pl.BlockSpec((BLK, 256), lambda i: (jnp.clip(i - 1, 0, last), 0)),
                  pl.BlockSpec((BLK, 256), lambda i: (jnp.minimum(i, last), 0)),
                  pl.BlockSpec((BLK, D), lambda i: (jnp.minimum(i, last), 0))],
        out_specs=[pl.BlockSpec((BLK, D), lambda i: (jnp.minimum(i, last), 0)),
                   pl.BlockSpec((BLK, 256), lambda i: (jnp.maximum(i - 1, 0), 0)),
                   pl.BlockSpec((1, 128), lambda i: (0, 0))],
        out_shape=[jax.ShapeDtypeStruct((t, D), BF), jax.ShapeDtypeStruct((t, 256), BF),
                   jax.ShapeDtypeStruct((1, 128), F32)],
        scratch_shapes=[pltpu.VMEM((BLK, 256), F32)], sem=("arbitrary",), nsteps=nb + 1,
        step_fn=lambda: pl.program_id(0))


def _split3(v):
    h = v.astype(BF)
    r = v - h.astype(F32)
    m = r.astype(BF)
    lo = (r - m.astype(F32)).astype(BF)
    return jnp.concatenate([h, m, lo], axis=1)


def _apply01(mat, v):
    n = v.shape[1]
    r = jnp.dot(mat, _split3(v), preferred_element_type=F32)
    return r[:, 0:n] + r[:, n:2 * n] + r[:, 2 * n:3 * n]


def _hgrn_gates(hq, hf, lb):
    sq = _sig(hq)
    sg = _sig(hf)
    f = lb + (1.0 - lb) * sg
    return hq * sq, (1.0 - lb) * (1.0 - sg), jnp.log(f), sq, sg, f


def _tri(upper):
    r = lax.broadcasted_iota(jnp.int32, (CH, CH), 0)
    c = lax.broadcasted_iota(jnp.int32, (CH, CH), 1)
    return (c >= r) if upper else (c <= r)


def _lb_from_logits(lg_ref):
    return 1.0 / (1.0 + jnp.exp(lg_ref[1:2, :] - lg_ref[0:1, :]))


def _hgrn_fwd(h4, logits, norm_g, *, t, comm=None):
    nc = t // CH
    nt_dims = (((1,), (1,)), ((), ()))
    tn_dims = (((0,), (0,)), ((), ()))

    def body(h_ref, lg_ref, ng_ref, y_ref, o_ref, st_ref, s_scr, b_scr, qa_s, ka_s, qb_s, kb_s, v_s):
        @pl.when(pl.program_id(0) == 0)
        def _():
            s_scr[...] = jnp.zeros_like(s_scr)

        heads = [slice(h * HG_K, (h + 1) * HG_K) for h in range(HG_HEADS)]
        causal = _tri(False)
        lb = _lb_from_logits(lg_ref)
        for c in range(HG_SUB):
            rows = slice(c * CH, (c + 1) * CH)
            q, k, g, _, _, _ = _hgrn_gates(h_ref[rows, 0:D], h_ref[rows, D:2 * D], lb)
            b_scr[...] = _apply01(jnp.where(causal, 1.0, 0.0).astype(BF), g)
            b = b_scr[...]
            b_mid = b_scr[CH // 2 - 1:CH // 2, :]
            b_last = b_scr[CH - 1:CH, :]
            qa_s[...] = (q * jnp.exp(b - b_mid)).astype(BF)
            ka_s[...] = (k * jnp.exp(b_mid - b)).astype(BF)
            qb_s[...] = (q * jnp.exp(b)).astype(BF)
            kb_s[...] = (k * jnp.exp(b_last - b)).astype(BF)
            v_s[...] = h_ref[rows, 2 * D:3 * D].astype(BF)
            dec = jnp.exp(b_last)
            st_ref[c] = s_scr[...].astype(BF)
            a = [jnp.where(causal, lax.dot_general(qa_s[:, sl], ka_s[:, sl], nt_dims, preferred_element_type=F32),
                           0.0).astype(BF) for sl in heads]
            for h, sl in enumerate(heads):
                o_ref[rows, sl] = (jnp.dot(a[h], v_s[:, sl], preferred_element_type=F32)
                                   + lax.dot_general(qb_s[:, sl], s_scr[h].astype(BF), nt_dims,
                                                     preferred_element_type=F32))
            for h, sl in enumerate(heads):
                s_scr[h] = dec[:, sl] * s_scr[h] + lax.dot_general(v_s[:, sl], kb_s[:, sl], tn_dims,
                                                                   preferred_element_type=F32)
            for h, sl in enumerate(heads):
                o = o_ref[rows, sl]
                on = o * lax.rsqrt(jnp.mean(o * o, axis=-1, keepdims=True) + EPS)
                gate = _sig(h_ref[rows, 3 * D + h * HG_K:3 * D + (h + 1) * HG_K])
                y_ref[rows, sl] = (on * ng_ref[:, sl] * gate).astype(BF)

    half = lambda: pltpu.VMEM((CH, D), BF)
    blk = HG_SUB * CH
    return _hosted_call(
        body, comm, (h4, logits, norm_g), name="hgrn_fwd", grid=(nc // HG_SUB,),
        in_specs=[pl.BlockSpec((blk, 4 * D), lambda n: (n, 0)),
                  pl.BlockSpec((2, D), lambda n: (0, 0)),
                  pl.BlockSpec((1, D), lambda n: (0, 0))],
        out_specs=[pl.BlockSpec((blk, D), lambda n: (n, 0)),
                   pl.BlockSpec((blk, D), lambda n: (n, 0)),
                   pl.BlockSpec((HG_SUB, HG_HEADS, HG_K, HG_K), lambda n: (n, 0, 0, 0))],
        out_shape=[jax.ShapeDtypeStruct((t, D), BF), jax.ShapeDtypeStruct((t, D), F32),
                   jax.ShapeDtypeStruct((nc, HG_HEADS, HG_K, HG_K), BF)],
        scratch_shapes=[pltpu.VMEM((HG_HEADS, HG_K, HG_K), F32), pltpu.VMEM((CH, D), F32),
                        half(), half(), half(), half(), half()],
        sem=("arbitrary",), nsteps=nc // HG_SUB, step_fn=lambda: pl.program_id(0))


def _hgrn_bwd(h4, logits, norm_g, o_pre, states, dy, *, t, comm=None):
    nc = t // CH
    nt_dims = (((1,), (1,)), ((), ()))
    tn_dims = (((0,), (0,)), ((), ()))

    def body(h_ref, lg_ref, ng_ref, o_ref, st_ref, dy_ref, dh_ref, dlg_ref, dng_ref, ds_scr, dlb_scr,
             b_scr, tail_s, e_qa, e_ka, e_qb, e_kb, q_s, k_s, dqa_s, dka_s, dqb_s, dkb_s,
             qa_s, ka_s, qb_s, kb_s, v_s, do_s):
        n = pl.program_id(0)

        @pl.when(n == 0)
        def _():
            ds_scr[...] = jnp.zeros_like(ds_scr)
            dlb_scr[...] = jnp.zeros_like(dlb_scr)
            dng_ref[...] = jnp.zeros_like(dng_ref)

        heads = [slice(h * HG_K, (h + 1) * HG_K) for h in range(HG_HEADS)]
        lb = _lb_from_logits(lg_ref)
        causal = _tri(False)

        def chunk(c):
            rows = slice(c * CH, (c + 1) * CH)
            hq = h_ref[rows, 0:D]
            q, k, g, sq, sg, f = _hgrn_gates(hq, h_ref[rows, D:2 * D], lb)
            b_scr[...] = _apply01(jnp.where(causal, 1.0, 0.0).astype(BF), g)
            b = b_scr[...]
            b_mid = b_scr[CH // 2 - 1:CH // 2, :]
            b_last = b_scr[CH - 1:CH, :]
            q_s[...] = q
            k_s[...] = k
            for e_ref, s_ref, base, expo in ((e_qa, qa_s, q, b - b_mid), (e_ka, ka_s, k, b_mid - b),
                                             (e_qb, qb_s, q, b), (e_kb, kb_s, k, b_last - b)):
                e = jnp.exp(expo)
                e_ref[...] = e
                s_ref[...] = (base * e).astype(BF)
            v_s[...] = h_ref[rows, 2 * D:3 * D].astype(BF)
            dec = jnp.exp(b_last)
            for h, sl in enumerate(heads):
                gcol = slice(3 * D + h * HG_K, 3 * D + (h + 1) * HG_K)
                ngh = ng_ref[:, sl]
                sgate = _sig(h_ref[rows, gcol])
                o = o_ref[rows, sl]
                r = lax.rsqrt(jnp.mean(o * o, axis=-1, keepdims=True) + EPS)
                on = o * r
                dyh = dy_ref[rows, sl]
                dh_ref[rows, gcol] = (dyh * on * ngh * sgate * (1.0 - sgate)).astype(BF)
                dng_ref[:, sl] += jnp.sum(dyh * on * sgate, axis=0, keepdims=True)
                don = dyh * ngh * sgate
                do_s[:, sl] = (r * (don - on * jnp.mean(don * on, axis=-1, keepdims=True))).astype(BF)
            a = [jnp.where(causal, lax.dot_general(qa_s[:, sl], ka_s[:, sl], nt_dims, preferred_element_type=F32),
                           0.0).astype(BF) for sl in heads]
            da = [jnp.where(causal, lax.dot_general(do_s[:, sl], v_s[:, sl], nt_dims, preferred_element_type=F32),
                            0.0).astype(BF) for sl in heads]
            for h, sl in enumerate(heads):
                dh_ref[rows, 2 * D + h * HG_K:2 * D + (h + 1) * HG_K] = (
                    lax.dot_general(a[h], do_s[:, sl], tn_dims, preferred_element_type=F32)
                    + lax.dot_general(kb_s[:, sl], ds_scr[h].astype(BF), nt_dims, preferred_element_type=F32)
                ).astype(BF)
            for h, sl in enumerate(heads):
                dqa_s[:, sl] = jnp.dot(da[h], ka_s[:, sl], preferred_element_type=F32)
            for h, sl in enumerate(heads):
                dka_s[:, sl] = lax.dot_general(da[h], qa_s[:, sl], tn_dims, preferred_element_type=F32)
            for h, sl in enumerate(heads):
                dqb_s[:, sl] = jnp.dot(do_s[:, sl], st_ref[c, h], preferred_element_type=F32)
            for h, sl in enumerate(heads):
                dkb_s[:, sl] = jnp.dot(v_s[:, sl], ds_scr[h].astype(BF), preferred_element_type=F32)
            for h, sl in enumerate(heads):
                tail_s[:, sl] = jnp.sum(dec[:, sl] * st_ref[c, h].astype(F32) * ds_scr[h], axis=0, keepdims=True)
            for h, sl in enumerate(heads):
                ds_scr[h] = (lax.dot_general(do_s[:, sl], qb_s[:, sl], tn_dims, preferred_element_type=F32)
                             + dec[:, sl] * ds_scr[h])
            qv, kv = q_s[...], k_s[...]
            dqa, dka, dqb, dkb = dqa_s[...], dka_s[...], dqb_s[...], dkb_s[...]
            eqa, eka, eqb, ekb = e_qa[...], e_ka[...], e_qb[...], e_kb[...]
            dkb_kb = dkb * (kv * ekb)
            db_last = jnp.sum(dkb_kb, axis=0, keepdims=True) + tail_s[...]
            last_row = lax.broadcasted_iota(jnp.int32, (CH, D), 0) == CH - 1
            db = (dqa * (qv * eqa) - dka * (kv * eka) + dqb * (qv * eqb) - dkb_kb
                  + jnp.where(last_row, db_last, 0.0))
            dg = _apply01(jnp.where(_tri(True), 1.0, 0.0).astype(BF), db)
            dq = dqa * eqa + dqb * eqb
            dk = dka * eka + dkb * ekb
            dh_ref[rows, 0:D] = (dq * sq * (1.0 + hq * (1.0 - sq))).astype(BF)
            dfk = dg / f - dk
            dh_ref[rows, D:2 * D] = ((1.0 - lb) * dfk * sg * (1.0 - sg)).astype(BF)
            dlb_scr[...] += jnp.sum((1.0 - sg) * dfk, axis=0, keepdims=True)

        for c in reversed(range(HG_SUB)):
            chunk(c)

        @pl.when(n == nc // HG_SUB - 1)
        def _():
            dl0 = dlb_scr[...] * lb * (1.0 - lb)
            dlg_ref[0:1, :] = dl0
            dlg_ref[1:2, :] = -dl0

    steps = nc // HG_SUB
    blk = HG_SUB * CH
    rev = lambda n: (steps - 1 - n, 0)
    return _hosted_call(
        body, comm, (h4, logits, norm_g, o_pre, states, dy), name="hgrn_bwd", grid=(steps,),
        in_specs=[pl.BlockSpec((blk, 4 * D), rev),
                  pl.BlockSpec((2, D), lambda n: (0, 0)),
                  pl.BlockSpec((1, D), lambda n: (0, 0)),
                  pl.BlockSpec((blk, D), rev),
                  pl.BlockSpec((HG_SUB, HG_HEADS, HG_K, HG_K), lambda n: (steps - 1 - n, 0, 0, 0)),
                  pl.BlockSpec((blk, D), rev)],
        out_specs=[pl.BlockSpec((blk, 4 * D), rev),
                   pl.BlockSpec((2, D), lambda n: (0, 0)),
                   pl.BlockSpec((1, D), lambda n: (0, 0))],
        out_shape=[jax.ShapeDtypeStruct((t, 4 * D), BF), jax.ShapeDtypeStruct((2, D), F32),
                   jax.ShapeDtypeStruct((1, D), F32)],
        scratch_shapes=([pltpu.VMEM((HG_HEADS, HG_K, HG_K), F32), pltpu.VMEM((1, D), F32),
                         pltpu.VMEM((CH, D), F32), pltpu.VMEM((1, D), F32)]
                        + [pltpu.VMEM((CH, D), F32)] * 10 + [pltpu.VMEM((CH, D), BF)] * 6),
        sem=("arbitrary",), nsteps=steps, step_fn=lambda: pl.program_id(0))


def _place():
    x, y, c = lax.axis_index("x"), lax.axis_index("y"), lax.axis_index("c")
    return x, y, c, [(1 - x, y), (x, 1 - y), (1 - x, 1 - y)]


def _gather_comm(shards, mid):
    n = len(shards)
    r = [s.shape[0] for s in shards]

    def tools(ins, outs, sems):
        send_sems, recv_sems, local_sems = sems
        x, y, c, chips = _place()
        me, sib = (x, y, c), (x, y, 1 - c)

        def rows(w, dev):
            return outs[w].at[pl.ds((4 * dev[0] + 2 * dev[1] + dev[2]) * r[w], r[w]), :]

        def copy(kind, w, block, to, src=None):
            return pltpu.make_async_remote_copy(
                src_ref=rows(w, block) if src is None else src, dst_ref=rows(w, block),
                send_sem=send_sems.at[kind], recv_sem=recv_sems.at[kind], device_id=to, device_id_type=MESH)

        def all_of(kind):
            whole = outs[0].at[pl.ds(0, sum(r)), :]
            return pltpu.make_async_remote_copy(
                src_ref=whole, dst_ref=whole, send_sem=send_sems.at[kind], recv_sem=recv_sems.at[kind],
                device_id=me, device_id_type=MESH)

        mine = [pltpu.make_async_copy(ins[w], rows(w, me), local_sems.at[w]) for w in range(n)]
        return c, chips, me, sib, copy, all_of, mine

    def start(ins, outs, sems):
        c, chips, me, sib, copy, _, mine = tools(ins, outs, sems)
        for cp in mine:
            cp.start()
        for w in range(n):
            copy(0, w, me, sib, src=ins[w]).start()
            for j, chip in enumerate(chips):
                copy(1 + j, w, me, (*chip, c), src=ins[w]).start()

    def pass_on(ins, outs, sems):
        c, chips, _, sib, copy, all_of, _ = tools(ins, outs, sems)
        for j, chip in enumerate(chips):
            all_of(1 + j).wait_recv()
            for w in range(n):
                copy(4 + j, w, (*chip, c), sib).start()

    def finish(ins, outs, sems):
        _, _, _, _, _, all_of, mine = tools(ins, outs, sems)
        all_of(0).wait_recv()
        for j in range(3):
            all_of(4 + j).wait_recv()
        for kind in range(7):
            all_of(kind).wait_send()
        for cp in mine:
            cp.wait()

    return _Comm(shards, [jax.ShapeDtypeStruct((N_DEV * rw, D), BF) for rw in r],
                 [pltpu.SemaphoreType.DMA((7,)), pltpu.SemaphoreType.DMA((7,)), pltpu.SemaphoreType.DMA((n,))],
                 [(0.0, start), (mid, pass_on), (1.0, finish)])


def _pair_comm(grads):
    n = len(grads)
    r = [g.shape[0] // N_DEV for g in grads]

    def start(ins, outs, sems):
        send_sems, recv_sems = sems
        x, y, c, _ = _place()
        for w in range(n):
            for a in range(N_CHIP):
                pltpu.make_async_remote_copy(
                    src_ref=ins[w].at[pl.ds((2 * a + 1 - c) * r[w], r[w]), :], dst_ref=outs[w].at[a],
                    send_sem=send_sems.at[w], recv_sem=recv_sems.at[w],
                    device_id=(x, y, 1 - c), device_id_type=MESH).start()

    def finish(ins, outs, sems):
        send_sems, recv_sems = sems
        x, y, c, _ = _place()
        for w in range(n):
            pltpu.make_async_remote_copy(
                src_ref=outs[w], dst_ref=outs[w], send_sem=send_sems.at[w], recv_sem=recv_sems.at[w],
                device_id=(x, y, c), device_id_type=MESH).wait()

    return _Comm(grads, [jax.ShapeDtypeStruct((N_CHIP, rw, D), BF) for rw in r],
                 [pltpu.SemaphoreType.DMA((n,)), pltpu.SemaphoreType.DMA((n,))],
                 [(0.0, start), (1.0, finish)])


def _pair_add(grad, got, core, *, name):
    r = got.shape[1]

    def body(c_ref, g_ref, got_ref, o_ref):
        o_ref[0] = (g_ref[...].astype(F32) + got_ref[0].astype(F32)).astype(BF)

    grid_spec = pltpu.PrefetchScalarGridSpec(
        num_scalar_prefetch=1, grid=(N_CHIP,),
        in_specs=[pl.BlockSpec((r, D), lambda a, c_ref: (2 * a + c_ref[0], 0)),
                  pl.BlockSpec((1, r, D), lambda a, c_ref: (a, 0, 0))],
        out_specs=pl.BlockSpec((1, r, D), lambda a, c_ref: (a, 0, 0)))
    return _pcall(body, name=name, grid_spec=grid_spec,
                  out_shape=jax.ShapeDtypeStruct((N_CHIP, r, D), BF),
                  compiler_params=_cp(("parallel",)))(core, grad, got)


def _chip_comm(pair_sums):
    n = len(pair_sums)
    r = [p.shape[1] for p in pair_sums]
    off = [sum(r[:w]) for w in range(n)]

    def tools(ins, outs, sems):
        send_sems, recv_sems, local_sems = sems
        x, y, c, chips = _place()
        my_chip = 2 * x + y

        def slot(w):
            return outs[0].at[my_chip, pl.ds(off[w], r[w]), :]

        own = [pltpu.make_async_copy(ins[w].at[my_chip], slot(w), local_sems.at[w]) for w in range(n)]
        return x, y, c, chips, my_chip, slot, own, send_sems, recv_sems

    def start(ins, outs, sems):
        x, y, c, chips, my_chip, slot, own, send_sems, recv_sems = tools(ins, outs, sems)
        for cp in own:
            cp.start()
        for j, chip in enumerate(chips):
            for w in range(n):
                pltpu.make_async_remote_copy(
                    src_ref=ins[w].at[2 * chip[0] + chip[1]], dst_ref=slot(w), send_sem=send_sems.at[j],
                    recv_sem=recv_sems.at[j], device_id=(*chip, c), device_id_type=MESH).start()

    def finish(ins, outs, sems):
        x, y, c, chips, my_chip, slot, own, send_sems, recv_sems = tools(ins, outs, sems)
        whole = outs[0].at[my_chip]
        for j in range(3):
            pltpu.make_async_remote_copy(
                src_ref=whole, dst_ref=whole, send_sem=send_sems.at[j], recv_sem=recv_sems.at[j],
                device_id=(x, y, c), device_id_type=MESH).wait()
        for cp in own:
            cp.wait()

    return _Comm(pair_sums, [jax.ShapeDtypeStruct((N_CHIP, sum(r), D), BF)],
                 [pltpu.SemaphoreType.DMA((3,)), pltpu.SemaphoreType.DMA((3,)), pltpu.SemaphoreType.DMA((n,))],
                 [(0.0, start), (1.0, finish)])


def _adam_math(w, g, m, v):
    m = ADAM_B1 * m + (1.0 - ADAM_B1) * g
    v = ADAM_B2 * v + (1.0 - ADAM_B2) * (g * g)
    m_hat = m / (1.0 - ADAM_B1 ** ADAM_STEP)
    v_hat = v / (1.0 - ADAM_B2 ** ADAM_STEP)
    delta = -ADAM_LR * (m_hat / (jnp.sqrt(v_hat) + ADAM_EPS) + ADAM_WD * w)
    return delta, m, v


SMALL = (("norm_mix_g", (1, D), 0), ("hgrn_norm_g", (1, D), 1), ("norm_ffn_g", (1, D), 2),
         ("norm_final_g", (1, D), 3), ("hgrn_lb_logits", (2, D), 4), ("attn_sinks", (1, 16), 6),
         ("b_in", (1, IN_W), 8))
LOSS_ROW = 7


def _small_allreduce_adam(grads, loss_row, params):
    n = len(SMALL)

    def rows_of(ref, shape, row):
        r, w = shape
        if w <= D:
            return ref[row:row + r, 0:w]
        pieces = [ref[row + k:row + k + 1, :] for k in range(-(-w // D))]
        return jnp.concatenate(pieces, axis=1)[:, 0:w]

    def body(*refs):
        g_refs, loss_ref = refs[:n], refs[n]
        wmv = refs[n + 1:4 * n + 1]
        loss_out = refs[4 * n + 1]
        outs = refs[4 * n + 2:8 * n + 2]
        mine, total, gath, send_sems, recv_sems = refs[8 * n + 2:]
        x, y, c, _ = _place()
        me = 4 * x + 2 * y + c
        mine[...] = jnp.zeros_like(mine)
        for g_ref, (_, (r, w), row) in zip(g_refs, SMALL):
            for k in range(-(-w // D)):
                wk = min(D, w - k * D)
                mine[row + k:row + k + r, 0:wk] = g_ref[:, k * D:k * D + wk]
        mine[LOSS_ROW:LOSS_ROW + 1, 0:128] = loss_ref[...]
        gath[me] = mine[...]
        cps = []
        for d in range(1, N_DEV):
            peer = (x ^ (d >> 2), y ^ ((d >> 1) & 1), c ^ (d & 1))
            cps.append(pltpu.make_async_remote_copy(
                src_ref=mine, dst_ref=gath.at[me], send_sem=send_sems.at[d - 1],
                recv_sem=recv_sems.at[d - 1], device_id=peer, device_id_type=MESH))
        for cp in cps:
            cp.start()
        for cp in cps:
            cp.wait()
        g = gath[0]
        for k in range(1, N_DEV):
            g = g + gath[k]
        total[...] = g
        loss_out[...] = total[LOSS_ROW:LOSS_ROW + 1, 0:128]
        for i, (_, shape, row) in enumerate(SMALL):
            gi = rows_of(total, shape, row)
            w_ref, m_ref, v_ref = wmv[3 * i:3 * i + 3]
            o = outs[4 * i:4 * i + 4]
            o[0][...] = gi
            o[1][...], o[2][...], o[3][...] = _adam_math(w_ref[...], gi, m_ref[...], v_ref[...])

    vm = pl.BlockSpec(memory_space=pltpu.VMEM)
    ins = [grads[name] for name, _, _ in SMALL] + [loss_row]
    for name, _, _ in SMALL:
        ins += list(params[name])
    out_shape = [jax.ShapeDtypeStruct((1, 128), F32)]
    for _, shape, _ in SMALL:
        out_shape += [jax.ShapeDtypeStruct(shape, F32)] * 4
    res = _pcall(body, name="small_allreduce_adam", in_specs=[vm] * len(ins), out_specs=[vm] * len(out_shape),
                 out_shape=out_shape,
                 scratch_shapes=[pltpu.VMEM((SMALL_ROWS, D), F32), pltpu.VMEM((SMALL_ROWS, D), F32),
                                 pltpu.VMEM((N_DEV, SMALL_ROWS, D), F32),
                                 pltpu.SemaphoreType.DMA((N_DEV - 1,)), pltpu.SemaphoreType.DMA((N_DEV - 1,))],
                 compiler_params=pltpu.CompilerParams(has_side_effects=True))(*ins)
    return res[0], {name: res[1 + 4 * i:5 + 4 * i] for i, (name, _, _) in enumerate(SMALL)}


def _adam(w, parts, index, m, v, *, name):
    rows = w.shape[0]
    tr = rows if rows <= 512 else rows // 2
    steps = rows // tr

    def body(w_ref, p_ref, m_ref, v_ref, g_ref, d_ref, mo_ref, vo_ref):
        g = p_ref[0].astype(F32)
        for a in range(1, N_CHIP):
            g = g + p_ref[a].astype(F32)
        g_ref[...] = g
        d_ref[...], mo_ref[...], vo_ref[...] = _adam_math(w_ref[...], g, m_ref[...], v_ref[...])

    spec = pl.BlockSpec((tr, D), lambda i: (i, 0))
    return _pcall(body, name=name, grid=(steps,),
                  in_specs=[spec, pl.BlockSpec((N_CHIP, tr, D), lambda i: (0, index * steps + i, 0)), spec, spec],
                  out_specs=[spec] * 4, out_shape=[jax.ShapeDtypeStruct((rows, D), F32)] * 4,
                  compiler_params=_cp(("parallel",)))(w, parts, m, v)


def _step(x, tgt, shards, norm_mix_g, b_in, sinks, logits, hgrn_norm_g, norm_ffn_g, norm_final_g):
    t = x.shape[0]
    core = lax.axis_index("c").astype(jnp.int32).reshape(1)

    u1, (win_t,) = _rms_fwd(x, norm_mix_g, tm=512, name="rms_mix", comm=_gather_comm(shards[0:1], 0.5))
    (q, kv, h4, gates), (wg_t, wba, wbh, wout) = _inproj_fwd(
        u1, win_t, b_in, t=t, comm=_gather_comm([shards[1]] + shards[4:7], 0.8))
    (y_attn,), _ = _attn_fwd(q, kv, sinks, t=t)
    (y_hgrn, o_pre, states), (wu_t, wd) = _hgrn_fwd(h4, logits, hgrn_norm_g, t=t,
                                                    comm=_gather_comm(shards[2:4], 0.8))
    col = lambda j: j
    first, second = (lambda j: 0), (lambda j: 1)
    gate_tiles = [(gates, D, first), (gates, D, second)]

    def merge(prods, ex):
        (ya_, yb_), (ga, gb) = prods, ex
        sa, sb = _sig(ga), _sig(gb)
        return sa, sb, ya_ * sa * (1.0 - sa), yb_ * sb * (1.0 - sb), sa * ya_ + sb * yb_

    sig_a, sig_b, dgate_a, dgate_b, merged = _fmm(
        [y_attn, y_hgrn], [(0, wba, False), (1, wbh, False)], gate_tiles, merge,
        [(BF, D, D, first)] * 5, m=t, n=D, tm=512, tn=D, name="branch_merge")
    def resid_norm(prods, ex):
        (p,), (xv, gv) = prods, ex
        hv = xv + p
        return hv, hv * lax.rsqrt(jnp.mean(hv * hv, axis=-1, keepdims=True) + EPS) * gv

    h1, u2 = _fmm([merged], [(0, wout, False)], [(x, D, first)], resid_norm, [(F32, D, D, first), (BF, D, D, first)],
                  m=t, n=D, tm=1024, tn=D, name="out_proj", vecs=[norm_ffn_g])

    def swiglu(prods, ex):
        g_, u_ = prods
        s = _sig(g_)
        silu = g_ * s
        return u_ * s * (1.0 + g_ * (1.0 - s)), silu, silu * u_

    dz_dgate, dz_dup, z = _fmm([u2], [(0, wg_t, True), (0, wu_t, True)], [], swiglu,
                               [(BF, FFN, FFN // 2, col)] * 3, m=t, n=FFN, tm=1024, tn=FFN // 2,
                               name="ffn_gate_up")
    def loss_head(prods, ex):
        (p,), (hv, tv, gv) = prods, ex
        hv = hv + p
        r = lax.rsqrt(jnp.mean(hv * hv, axis=-1, keepdims=True) + EPS)
        xh = hv * r
        err = xh * gv - tv
        lp = jnp.sum(jnp.sum(err * err, axis=1, keepdims=True), axis=0, keepdims=True) * (0.5 / D)
        dy = err * (1.0 / D)
        dxh = dy * gv
        dh = r * (dxh - xh * jnp.mean(dxh * xh, axis=-1, keepdims=True))
        return dh, dh, jnp.sum(dy * xh, axis=0, keepdims=True), jnp.broadcast_to(lp, (1, 128))

    dh2, dh2_b, d_norm_final, loss_row = _fmm(
        [z], [(0, wd, False)], [(h1, D, first), (tgt, D, first)], loss_head, [(F32, D, D, first), (BF, D, D, first)],
        m=t, n=D, tm=512, tn=D, name="ffn_down_loss", vecs=[norm_final_g], sums=[D, 128])

    def swiglu_bwd(prods, ex):
        (dz,), (da_, db_) = prods, ex
        return dz * da_.astype(F32), dz * db_.astype(F32)

    ffn_tiles = [(dz_dgate, FFN // 2, col), (dz_dup, FFN // 2, col)]
    dgt, dup = _fmm([dh2_b], [(0, wd, True)], ffn_tiles, swiglu_bwd, [(BF, FFN, FFN // 2, col)] * 2,
                    m=t, n=FFN, tm=1024, tn=FFN // 2, name="d_gate_up")
    d_wd = _wgrad(z, dh2_b, name="d_w_down")
    (du2,) = _fmm([dgt, dup], [(0, wg_t, False), (1, wu_t, False)], [], lambda prods, ex: (prods[0] + prods[1],),
                  [(F32, D, 512, col)], m=t, n=D, tm=1024, tn=512, name="d_u2")
    d_wg = _wgrad(dgt, u2, name="d_w_gate")
    d_wu = _wgrad(dup, u2, name="d_w_up")
    dh1, dh1_b, d_norm_ffn = _rms_bwd(du2, h1, norm_ffn_g, dh2, tm=512, name="rms_ffn_bwd")
    d_wout = _wgrad(merged, dh1_b, name="d_w_out")

    def merge_bwd(prods, ex):
        (dm,), (sa, sb, ca, cb, wa, wb) = prods, ex
        dgate = jnp.concatenate([dm * ca.astype(F32), dm * cb.astype(F32)], axis=1)
        dya_ = (dm * sa.astype(F32)).astype(BF)
        dyb_ = (dm * sb.astype(F32)).astype(BF)
        return (dya_, dyb_, dgate, lax.dot_general(dya_, wa, _NT, preferred_element_type=F32),
                lax.dot_general(dyb_, wb, _NT, preferred_element_type=F32))

    ffn_grads = (d_wg, d_wu, d_wd)
    (dya, dyb, dgates, dy_attn, dy_hgrn), got = _fmm(
        [dh1_b], [(0, wout, True)], [(a, D, first) for a in (sig_a, sig_b, dgate_a, dgate_b)], merge_bwd,
        [(BF, D, D, first), (BF, D, D, first), (BF, 2 * D, 2 * D, first), (BF, D, D, first), (F32, D, D, first)],
        m=t, n=D, tm=512, tn=D, name="d_merge", consts=[wba, wbh], comm=_pair_comm(ffn_grads))
    pair_ffn = [_pair_add(g, r, core, name="pair_add_ffn%d" % i) for i, (g, r) in enumerate(zip(ffn_grads, got))]
    d_wba = _wgrad(y_attn, dya, name="d_w_ba")
    d_wbh = _wgrad(y_hgrn, dyb, name="d_w_bh")
    sq_grads = (d_wba, d_wbh, d_wout)
    (dq, dkv, d_sinks), (parts_ffn, *got) = _attn_bwd(
        q, kv, sinks, dy_attn, t=t, comm=_both(_chip_comm(pair_ffn), _pair_comm(sq_grads)))
    pair_sq = [_pair_add(g, r, core, name="pair_add_sq%d" % i) for i, (g, r) in enumerate(zip(sq_grads, got))]
    (dh4, d_logits, d_hgrn_norm), (parts_sq,) = _hgrn_bwd(h4, logits, hgrn_norm_g, o_pre, states, dy_hgrn,
                                                           t=t, comm=_chip_comm(pair_sq))
    dps = (dq, dkv, dh4, dgates)
    d_win_t, d_b_in = _inproj_bwd_w(dps, u1, t=t)
    half0, got_in = _inproj_bwd_x(dps, win_t, x, norm_mix_g, dh1, t=t, part=0, comm=_pair_comm([d_win_t]))
    pair_in = _pair_add(d_win_t, got_in[0], core, name="pair_add_w_in")
    (grad_x, d_norm_mix), (parts_in,) = _inproj_bwd_x(dps, win_t, x, norm_mix_g, dh1, t=t, part=1, prev=half0,
                                                      comm=_chip_comm([pair_in]))

    small_grads = (d_norm_mix, d_b_in, d_sinks, d_logits, d_hgrn_norm, d_norm_ffn, d_norm_final)
    return loss_row, grad_x, (parts_in, parts_ffn, parts_sq), small_grads


def kernel(x, norm_mix_g, w_in, b_in, attn_sinks, hgrn_lb_logits, hgrn_norm_g, w_branch_attn, w_branch_hgrn, w_out, norm_ffn_g, w_ffn_gate, w_ffn_up, w_ffn_down, norm_final_g, loss_target, m_norm_mix_g, m_w_in, m_b_in, m_attn_sinks, m_hgrn_lb_logits, m_hgrn_norm_g, m_w_branch_attn, m_w_branch_hgrn, m_w_out, m_norm_ffn_g, m_w_ffn_gate, m_w_ffn_up, m_w_ffn_down, m_norm_final_g, v_norm_mix_g, v_w_in, v_b_in, v_attn_sinks, v_hgrn_lb_logits, v_hgrn_norm_g, v_w_branch_attn, v_w_branch_hgrn, v_w_out, v_norm_ffn_g, v_w_ffn_gate, v_w_ffn_up, v_w_ffn_down, v_norm_final_g):
    shards = [w_in[0].T.astype(BF), w_ffn_gate[0].T.astype(BF), w_ffn_up[0].T.astype(BF),
              w_ffn_down[0].astype(BF), w_branch_attn[0].astype(BF), w_branch_hgrn[0].astype(BF),
              w_out[0].astype(BF)]
    loss_row, grad_x, grad_parts, small_grads = _step(
        x[0], loss_target[0], shards, norm_mix_g, b_in, attn_sinks, hgrn_lb_logits, hgrn_norm_g,
        norm_ffn_g, norm_final_g.reshape(1, D))

    d_norm_mix, d_b_in, d_sinks, d_logits, d_hgrn_norm, d_norm_ffn, d_norm_final = small_grads
    row = lambda a: a.reshape(1, D)
    loss_out, small = _small_allreduce_adam(
        dict(norm_mix_g=d_norm_mix, hgrn_norm_g=d_hgrn_norm, norm_ffn_g=d_norm_ffn, norm_final_g=d_norm_final,
             hgrn_lb_logits=d_logits, attn_sinks=d_sinks, b_in=d_b_in),
        loss_row,
        dict(norm_mix_g=(norm_mix_g, m_norm_mix_g, v_norm_mix_g), hgrn_norm_g=(hgrn_norm_g, m_hgrn_norm_g, v_hgrn_norm_g),
             norm_ffn_g=(norm_ffn_g, m_norm_ffn_g, v_norm_ffn_g),
             norm_final_g=(row(norm_final_g), row(m_norm_final_g), row(v_norm_final_g)),
             hgrn_lb_logits=(hgrn_lb_logits, m_hgrn_lb_logits, v_hgrn_lb_logits),
             attn_sinks=(attn_sinks, m_attn_sinks, v_attn_sinks), b_in=(b_in, m_b_in, v_b_in)))
    small["norm_final_g"] = [a.reshape(D) for a in small["norm_final_g"]]
    loss = loss_out[0, 0]

    names = ["w_in", "w_ffn_gate", "w_ffn_up", "w_ffn_down", "w_branch_attn", "w_branch_hgrn", "w_out"]
    w_full = dict(w_in=(w_in, m_w_in, v_w_in), w_ffn_gate=(w_ffn_gate, m_w_ffn_gate, v_w_ffn_gate),
                  w_ffn_up=(w_ffn_up, m_w_ffn_up, v_w_ffn_up), w_ffn_down=(w_ffn_down, m_w_ffn_down, v_w_ffn_down),
                  w_branch_attn=(w_branch_attn, m_w_branch_attn, v_w_branch_attn),
                  w_branch_hgrn=(w_branch_hgrn, m_w_branch_hgrn, v_w_branch_hgrn),
                  w_out=(w_out, m_w_out, v_w_out))
    parts_in, parts_ffn, parts_sq = grad_parts
    where = [(parts_in, 0), (parts_ffn, 0), (parts_ffn, 1), (parts_ffn, 2), (parts_sq, 0), (parts_sq, 1), (parts_sq, 2)]
    big = {}
    for i, name in enumerate(names):
        view = (lambda a: a[0].T) if i < 3 else (lambda a: a[0])
        back = (lambda a: a.T[None]) if i < 3 else (lambda a: a[None])
        wv, mv, vv = w_full[name]
        res = _adam(view(wv), where[i][0], where[i][1], view(mv), view(vv), name="adam_" + name)
        big[name] = [back(a) for a in res]

    order = ["norm_mix_g", "w_in", "b_in", "attn_sinks", "hgrn_lb_logits", "hgrn_norm_g", "w_branch_attn",
             "w_branch_hgrn", "w_out", "norm_ffn_g", "w_ffn_gate", "w_ffn_up", "w_ffn_down", "norm_final_g"]
    outs = [loss, grad_x[None]]
    for kind in range(4):
        for name in order:
            outs.append(big[name][kind] if name in big else small[name][kind])
    return tuple(outs)
```

```python
import math

import jax
import jax.numpy as jnp
from jax import lax
from jax.experimental import pallas as pl
from jax.experimental.pallas import tpu as pltpu

F32 = jnp.float32
BF = jnp.bfloat16
MESH = pl.DeviceIdType.MESH

D = 1024
HEAD = 64
N_PAIR = 8
BLK = 128
CH = 64
HG_SUB = 2
HG_HEADS = 8
HG_K = 128
FFN = 2816
IN_W = 7424
N_DEV = 8
N_CHIP = 4
EPS = 1e-6
NEG = -1e30
SCALE = 1.0 / math.sqrt(HEAD)
VMEM_LIMIT = 56 * 1024 * 1024
WT = 256

ADAM_LR, ADAM_B1, ADAM_B2, ADAM_EPS, ADAM_WD, ADAM_STEP = 0.001, 0.9, 0.999, 1e-08, 0.01, 10

SLAB_R = (IN_W // N_DEV, FFN // N_DEV, FFN // N_DEV, FFN // N_DEV, D // N_DEV, D // N_DEV, D // N_DEV)
SLAB_ROWS = sum(SLAB_R)
SLAB_OFF = tuple(sum(SLAB_R[:i]) for i in range(len(SLAB_R)))
N_W = len(SLAB_R)
GRP_OFF = (0, D // WT, (D + 256) // WT, (5 * D + 256) // WT)
GRP_N = (D // WT, 256 // WT, 4 * D // WT, 2 * D // WT)
SMALL_ROWS = 16


_NN = (((1,), (0,)), ((), ()))
_NT = (((1,), (1,)), ((), ()))
_TN = (((0,), (0,)), ((), ()))


def _pcall(body, **kw):
    return pl.pallas_call(body, **kw)


def _cp(sem=None, **kw):
    return pltpu.CompilerParams(dimension_semantics=sem, vmem_limit_bytes=VMEM_LIMIT, **kw)


def _sig(v):
    return 0.5 * jnp.tanh(0.5 * v) + 0.5


def _accum(ref, val, first):
    @pl.when(first)
    def _():
        ref[...] = val

    @pl.when(jnp.logical_not(first))
    def _():
        ref[...] += val


class _Comm:
    def __init__(self, ins, out_shapes, sem_shapes, phases):
        self.ins, self.out_shapes, self.sem_shapes, self.phases = list(ins), list(out_shapes), list(sem_shapes), phases


def _both(a, b):
    ni, no, ns = len(a.ins), len(a.out_shapes), len(a.sem_shapes)

    def of_a(fn):
        return lambda ins, outs, sems: fn(ins[:ni], outs[:no], sems[:ns])

    def of_b(fn):
        return lambda ins, outs, sems: fn(ins[ni:], outs[no:], sems[ns:])

    return _Comm(a.ins + b.ins, a.out_shapes + b.out_shapes, a.sem_shapes + b.sem_shapes,
                 [(f, of_a(fn)) for f, fn in a.phases] + [(f, of_b(fn)) for f, fn in b.phases])


def _host(body, comm, n_in, n_out, n_scr, nsteps, step_fn):
    if comm is None:
        return body
    ci, co = len(comm.ins), len(comm.out_shapes)

    def wrapped(*refs):
        p = 0
        ins, p = refs[p:p + n_in], p + n_in
        cins, p = refs[p:p + ci], p + ci
        outs, p = refs[p:p + n_out], p + n_out
        couts, p = refs[p:p + co], p + co
        scr, p = refs[p:p + n_scr], p + n_scr
        csems = refs[p:]
        step = step_fn()
        for frac, fn in comm.phases:
            if frac < 1.0:
                @pl.when(step == int(round(frac * (nsteps - 1))))
                def _(fn=fn):
                    fn(cins, couts, csems)
        body(*ins, *outs, *scr)
        for frac, fn in comm.phases:
            if frac >= 1.0:
                @pl.when(step == nsteps - 1)
                def _(fn=fn):
                    fn(cins, couts, csems)

    return wrapped


def _hosted_call(body, comm, args, *, name, grid, in_specs, out_specs, out_shape, scratch_shapes, sem,
                 nsteps, step_fn, aliases=None):
    n_in, n_out, n_scr = len(in_specs), len(out_specs), len(scratch_shapes)
    args = list(args)
    extra = {}
    if comm is not None:
        in_specs = list(in_specs) + [_hbm_spec()] * len(comm.ins)
        out_specs = list(out_specs) + [_hbm_spec()] * len(comm.out_shapes)
        out_shape = list(out_shape) + comm.out_shapes
        scratch_shapes = list(scratch_shapes) + comm.sem_shapes
        args += comm.ins
        extra = dict(has_side_effects=True)
    outs = _pcall(_host(body, comm, n_in, n_out, n_scr, nsteps, step_fn), name=name, grid=grid,
                  in_specs=in_specs, out_specs=out_specs, out_shape=out_shape, scratch_shapes=scratch_shapes,
                  input_output_aliases=aliases or {}, compiler_params=_cp(sem, **extra))(*args)
    return list(outs[:n_out]), list(outs[n_out:])


def _hbm_spec():
    return pl.BlockSpec(memory_space=pl.ANY)


def _wgrad(a, b, *, name):
    (t, m), n = a.shape, b.shape[1]

    def body(a_ref, b_ref, o_ref):
        o_ref[...] = lax.dot_general(a_ref[...], b_ref[...], _TN, preferred_element_type=F32).astype(BF)

    return _pcall(body, name=name, grid=(m // WT,),
                  in_specs=[pl.BlockSpec((t, WT), lambda i: (0, i)), pl.BlockSpec((t, n), lambda i: (0, 0))],
                  out_specs=pl.BlockSpec((WT, n), lambda i: (i, 0)),
                  out_shape=jax.ShapeDtypeStruct((m, n), BF), compiler_params=_cp(("parallel",)))(a, b)


def _fmm(lhs, rhs, extras, epilogue, outs, *, m, n, tm, tn, name, comm=None, vecs=(), consts=(), sums=()):
    tm, tn = min(tm, m), min(tn, n)
    assert m % tm == 0 and n % tn == 0 and (not sums or tn == n), (name, m, n, tm, tn)
    in_specs, args = [], []
    for a in lhs:
        in_specs.append(pl.BlockSpec((tm, a.shape[1]), lambda i, j: (i, 0)))
        args.append(a)
    for li, b, tb in rhs:
        k = lhs[li].shape[1]
        in_specs.append(pl.BlockSpec((tn, k), lambda i, j: (j, 0)) if tb
                        else pl.BlockSpec((k, tn), lambda i, j: (0, j)))
        args.append(b)
    for arr, w, col in extras:
        in_specs.append(pl.BlockSpec((tm, w), lambda i, j, col=col: (i, col(j))))
        args.append(arr)
    for vec in vecs:
        in_specs.append(pl.BlockSpec((1, tn), lambda i, j: (0, j)))
        args.append(vec)
    for whole in consts:
        in_specs.append(pl.BlockSpec(whole.shape, lambda i, j: (0, 0)))
        args.append(whole)
    out_specs = [pl.BlockSpec((tm, w), lambda i, j, col=col: (i, col(j))) for _, _, w, col in outs]
    out_shape = [jax.ShapeDtypeStruct((m, total), dt) for dt, total, _, _ in outs]
    for w in sums:
        out_specs.append(pl.BlockSpec((1, w), lambda i, j: (0, 0)))
        out_shape.append(jax.ShapeDtypeStruct((1, w), F32))
    nl, nr, ne, no = len(lhs), len(rhs), len(extras) + len(vecs) + len(consts), len(outs)

    def body(*refs):
        prods = []
        for r, (li, _, tb) in enumerate(rhs):
            prods.append(lax.dot_general(refs[li][...], refs[nl + r][...], _NT if tb else _NN,
                                         preferred_element_type=F32))
        vals = epilogue(prods, [ref[...] for ref in refs[nl + nr:nl + nr + ne]])
        o_refs = refs[nl + nr + ne:]
        for o_ref, v in zip(o_refs[:no], vals[:no]):
            o_ref[...] = v.astype(o_ref.dtype)
        for s_ref, v in zip(o_refs[no:], vals[no:]):
            _accum(s_ref, v, pl.program_id(0) == 0)

    gm, gn = m // tm, n // tn
    res, comm_res = _hosted_call(
        body, comm, args, name=name, grid=(gm, gn), in_specs=in_specs, out_specs=out_specs,
        out_shape=out_shape, scratch_shapes=[], sem=("arbitrary", "arbitrary"), nsteps=gm * gn,
        step_fn=lambda: pl.program_id(0) * gn + pl.program_id(1))
    return res if comm is None else (res, comm_res)


def _grp_of(i):
    return [jnp.logical_and(i >= GRP_OFF[g], i < GRP_OFF[g] + GRP_N[g]) for g in range(4)]


def _grp_idx(i, g):
    return jnp.clip(i - GRP_OFF[g], 0, GRP_N[g] - 1)


def _inproj_fwd(u, win_t, b_in, *, t, comm=None):
    dtypes = (BF, BF, F32, F32)
    tm = min(1024, t)
    n_row = t // tm
    n_chunks, h_first, g_first = 8, 2, 6
    sub = D // WT

    def w_block(l):
        return jnp.where(l == 0, GRP_OFF[0], jnp.where(l == 1, GRP_OFF[1], GRP_OFF[2] + sub * (l - h_first)))

    def body(u_ref, *rest):
        w_refs, b_refs, (q_ref, kv_ref, h_ref, g_ref) = rest[:sub], rest[sub:2 * sub], rest[2 * sub:]
        l = pl.program_id(1)

        @pl.when(l == 1)
        def _():
            kv_ref[...] = (lax.dot_general(u_ref[...], w_refs[0][...], _NT, preferred_element_type=F32)
                           + b_refs[0][...]).astype(BF)

        for pred, o_ref in ((l == 0, q_ref), (jnp.logical_and(l >= h_first, l < g_first), h_ref),
                            (l >= g_first, g_ref)):
            @pl.when(pred)
            def _(o_ref=o_ref):
                w = jnp.concatenate([w[...] for w in w_refs], axis=0)
                b = jnp.concatenate([b[...] for b in b_refs], axis=1)
                o_ref[...] = (lax.dot_general(u_ref[...], w, _NT, preferred_element_type=F32) + b).astype(o_ref.dtype)

    return _hosted_call(
        body, comm, [u] + [win_t] * sub + [b_in] * sub, name="inproj_fwd", grid=(n_row, n_chunks),
        in_specs=[pl.BlockSpec((tm, D), lambda i, l: (i, 0))]
        + [pl.BlockSpec((WT, D), lambda i, l, o=o: (w_block(l) + o, 0)) for o in range(sub)]
        + [pl.BlockSpec((1, WT), lambda i, l, o=o: (0, w_block(l) + o)) for o in range(sub)],
        out_specs=[pl.BlockSpec((tm, D), lambda i, l: (i, 0)),
                   pl.BlockSpec((tm, 256), lambda i, l: (i, 0)),
                   pl.BlockSpec((tm, D), lambda i, l: (i, jnp.clip(l - h_first, 0, 3))),
                   pl.BlockSpec((tm, D), lambda i, l: (i, jnp.clip(l - g_first, 0, 1)))],
        out_shape=[jax.ShapeDtypeStruct((t, GRP_N[g] * WT), dtypes[g]) for g in range(4)],
        scratch_shapes=[], sem=("arbitrary", "arbitrary"), nsteps=n_row * n_chunks,
        step_fn=lambda: pl.program_id(0) * n_chunks + pl.program_id(1))


def _inproj_bwd_x(dps, win_t, x, g, resid, *, t, part, prev=None, comm=None):
    n_row = 8 if t >= 4096 else 4
    tm = t // n_row
    first = n_row // 4
    per = first if part == 0 else n_row - first
    row = lambda i: part * first + i

    n_chunks = 4
    sub = 2 * D // WT

    def w_block(l):
        return jnp.where(l == 0, 0, GRP_OFF[2] + sub * (l - 1))

    def body(d0, d1, d2, d3, *rest):
        w_refs, (x_ref, g_ref, r_ref) = rest[:sub], rest[sub:sub + 3]
        dg_prev = rest[sub + 3] if prev is not None else None
        o_ref, dg_ref, acc_ref = rest[-3], rest[-2], rest[-1]
        i, l = pl.program_id(0), pl.program_id(1)

        @pl.when(l == 0)
        def _():
            wq = jnp.concatenate([w[...] for w in w_refs[:GRP_N[0]]], axis=0)
            acc_ref[...] = (jnp.dot(d0[...], wq, preferred_element_type=F32)
                            + jnp.dot(d1[...], w_refs[GRP_N[0]][...], preferred_element_type=F32))

        for pred, d_ref in ((jnp.logical_and(l >= 1, l < 3), d2), (l == 3, d3)):
            @pl.when(pred)
            def _(d_ref=d_ref):
                w = jnp.concatenate([w[...] for w in w_refs], axis=0)
                acc_ref[...] += jnp.dot(d_ref[...], w, preferred_element_type=F32)

        @pl.when(l == n_chunks - 1)
        def _():
            xv = x_ref[...]
            r = lax.rsqrt(jnp.mean(xv * xv, axis=-1, keepdims=True) + EPS)
            xh = xv * r
            du = acc_ref[...]
            dxh = du * g_ref[...]
            o_ref[...] = r_ref[...] + r * (dxh - xh * jnp.mean(dxh * xh, axis=-1, keepdims=True))
            dg = jnp.sum(du * xh, axis=0, keepdims=True)
            if dg_prev is not None:
                dg = dg + jnp.where(i == 0, 1.0, 0.0) * dg_prev[...]
            _accum(dg_ref, dg, i == 0)

    rows = lambda w: pl.BlockSpec((tm, w), lambda i, l: (row(i), 0))
    in_specs = ([rows(D), rows(256),
                 pl.BlockSpec((tm, 2 * D), lambda i, l: (row(i), jnp.clip(l - 1, 0, 1))), rows(2 * D)]
                + [pl.BlockSpec((WT, D), lambda i, l, o=o: (w_block(l) + o, 0)) for o in range(sub)]
                + [rows(D), pl.BlockSpec((1, D), lambda i, l: (0, 0)), rows(D)])
    args = list(dps) + [win_t] * sub + [x, g, resid]
    aliases = None
    if prev is not None:
        in_specs += [pl.BlockSpec((1, D), lambda i, l: (0, 0)), _hbm_spec()]
        args += [prev[1], prev[0]]
        aliases = {len(args) - 1: 0}
    return _hosted_call(
        body, comm, args, name="inproj_bwd_x%d" % part, grid=(per, n_chunks), in_specs=in_specs,
        out_specs=[rows(D), pl.BlockSpec((1, D), lambda i, l: (0, 0))],
        out_shape=[jax.ShapeDtypeStruct((t, D), F32), jax.ShapeDtypeStruct((1, D), F32)],
        scratch_shapes=[pltpu.VMEM((tm, D), F32)], sem=("arbitrary", "arbitrary"), nsteps=per * n_chunks,
        step_fn=lambda: pl.program_id(0) * n_chunks + pl.program_id(1), aliases=aliases)


def _inproj_bwd_w(dps, u, *, t):
    n_tiles = IN_W // WT
    dims = (((0,), (0,)), ((), ()))

    def body(d0, d1, d2, d3, u_ref, o_ref, db_ref):
        i = pl.program_id(0)
        uv = u_ref[...]
        for g, (pred, d_ref) in enumerate(zip(_grp_of(i), (d0, d1, d2, d3))):
            @pl.when(pred)
            def _(d_ref=d_ref):
                dv = d_ref[...]
                o_ref[...] = lax.dot_general(dv, uv, dims, preferred_element_type=F32).astype(BF)
                db_ref[...] = jnp.sum(dv.astype(F32), axis=0, keepdims=True)

    return _pcall(body, name="inproj_bwd_w", grid=(n_tiles,),
                  in_specs=[pl.BlockSpec((t, WT), lambda i, g=g: (0, _grp_idx(i, g))) for g in range(4)]
                  + [pl.BlockSpec((t, D), lambda i: (0, 0))],
                  out_specs=[pl.BlockSpec((WT, D), lambda i: (i, 0)),
                             pl.BlockSpec((1, WT), lambda i: (0, i))],
                  out_shape=[jax.ShapeDtypeStruct((IN_W, D), BF), jax.ShapeDtypeStruct((1, IN_W), F32)],
                  compiler_params=_cp(("arbitrary",)))(*dps, u)


def _row_spec(tm, width, col=0):
    return pl.BlockSpec((tm, width), lambda i: (i, col))


def _vec_spec(width):
    return pl.BlockSpec((1, width), lambda i: (0, 0))


def _rms_fwd(x, g, *, tm, name, comm=None):
    t = x.shape[0]
    tm = min(tm, t)

    def body(x_ref, g_ref, u_ref):
        xv = x_ref[...]
        r = lax.rsqrt(jnp.mean(xv * xv, axis=-1, keepdims=True) + EPS)
        u_ref[...] = (xv * r * g_ref[...]).astype(BF)

    (u,), comm_res = _hosted_call(
        body, comm, (x, g), name=name, grid=(t // tm,), in_specs=[_row_spec(tm, D), _vec_spec(D)],
        out_specs=[_row_spec(tm, D)], out_shape=[jax.ShapeDtypeStruct((t, D), BF)], scratch_shapes=[],
        sem=("arbitrary",), nsteps=t // tm, step_fn=lambda: pl.program_id(0))
    return u if comm is None else (u, comm_res)


def _rms_bwd(du, x, g, resid, *, tm, name):
    t = x.shape[0]
    tm = min(tm, t)

    def body(du_ref, x_ref, g_ref, r_ref, dx_ref, dxb_ref, dg_ref):
        xv = x_ref[...]
        r = lax.rsqrt(jnp.mean(xv * xv, axis=-1, keepdims=True) + EPS)
        xh = xv * r
        duv = du_ref[...]
        dxh = duv * g_ref[...]
        dx = r_ref[...] + r * (dxh - xh * jnp.mean(dxh * xh, axis=-1, keepdims=True))
        dx_ref[...] = dx
        dxb_ref[...] = dx.astype(BF)
        _accum(dg_ref, jnp.sum(duv * xh, axis=0, keepdims=True), pl.program_id(0) == 0)

    return _pcall(body, name=name, grid=(t // tm,),
                  in_specs=[_row_spec(tm, D), _row_spec(tm, D), _vec_spec(D), _row_spec(tm, D)],
                  out_specs=[_row_spec(tm, D), _row_spec(tm, D), _vec_spec(D)],
                  out_shape=[jax.ShapeDtypeStruct((t, D), F32), jax.ShapeDtypeStruct((t, D), BF),
                             jax.ShapeDtypeStruct((1, D), F32)],
                  compiler_params=_cp(("arbitrary",)))(du, x, g, resid)


def _attn_kv_tiles(kprev, kcur):
    kv = jnp.concatenate([kprev, kcur], axis=0).astype(F32)
    lo = lax.broadcasted_iota(jnp.int32, (2 * BLK, 128), 1) < HEAD
    tiles = []
    for part in (kv[:, 0:128], kv[:, 128:256]):
        rolled = pltpu.roll(part, HEAD, 1)
        z = jnp.zeros_like(part)
        tiles.append(((jnp.where(lo, part, z).astype(BF), jnp.where(lo, z, rolled).astype(BF)),
                      (jnp.where(lo, rolled, z).astype(BF), jnp.where(lo, z, part).astype(BF))))
    k_t, v_t = tiles
    return [(jnp.concatenate(k_t[h], axis=0), jnp.concatenate(v_t[h], axis=0)) for h in range(2)]


def _attn_mask(i):
    qi = lax.broadcasted_iota(jnp.int32, (BLK, 2 * BLK), 0)
    kj = lax.broadcasted_iota(jnp.int32, (BLK, 2 * BLK), 1)
    first_key = jnp.where(i == 0, BLK, 0)
    in_prev = jnp.logical_and(jnp.logical_and(kj < BLK, kj > qi), kj >= first_key)
    in_cur = jnp.logical_and(kj >= BLK, kj - BLK <= qi)
    return jnp.logical_or(in_prev, in_cur)


def _attn_probs(s, sink, valid):
    s = jnp.where(valid, s * SCALE, NEG)
    mx = jnp.maximum(jnp.max(s, axis=-1, keepdims=True), sink)
    e = jnp.exp(s - mx)
    es = jnp.exp(sink - mx)
    inv = 1.0 / (jnp.sum(e, axis=-1, keepdims=True) + es)
    return e * inv, es * inv


_KEYS = 2 * BLK


def _pair(ref, j):
    return ref[:, j * 128:(j + 1) * 128]


def _attn_fwd(q, kv, sinks, *, t, comm=None):
    nb = t // BLK

    def body(sink_ref, q_ref, kp_ref, kc_ref, o_ref):
        valid = _attn_mask(pl.program_id(0))
        tiles = _attn_kv_tiles(kp_ref[...], kc_ref[...])
        s = [lax.dot_general(_pair(q_ref, j), tiles[j // 4][0], _NT, preferred_element_type=F32)
             for j in range(N_PAIR)]
        p = []
        for j in range(N_PAIR):
            pe, _ = _attn_probs(s[j][:, 0:_KEYS], sink_ref[0, 2 * j], valid)
            po, _ = _attn_probs(s[j][:, _KEYS:2 * _KEYS], sink_ref[0, 2 * j + 1], valid)
            p.append(jnp.concatenate([pe.astype(BF), po.astype(BF)], axis=1))
        for j in range(N_PAIR):
            o_ref[:, j * 128:(j + 1) * 128] = jnp.dot(p[j], tiles[j // 4][1],
                                                      preferred_element_type=F32).astype(BF)

    return _hosted_call(
        body, comm, (sinks, q, kv, kv), name="attn_fwd", grid=(nb,),
        in_specs=[pl.BlockSpec(memory_space=pltpu.SMEM),
                  pl.BlockSpec((BLK, D), lambda i: (i, 0)),
                  pl.BlockSpec((BLK, 256), lambda i: (jnp.maximum(i - 1, 0), 0)),
                  pl.BlockSpec((BLK, 256), lambda i: (i, 0))],
        out_specs=[pl.BlockSpec((BLK, D), lambda i: (i, 0))],
        out_shape=[jax.ShapeDtypeStruct((t, D), BF)],
        scratch_shapes=[], sem=("arbitrary",), nsteps=nb, step_fn=lambda: pl.program_id(0))


def _attn_bwd(q, kv, sinks, do, *, t, comm=None):
    nb = t // BLK
    last = nb - 1

    def body(sink_ref, q_ref, kp_ref, kc_ref, do_ref, dq_ref, dkv_ref, ds_ref, carry_ref):
        i = pl.program_id(0)

        @pl.when(i == 0)
        def _():
            ds_ref[...] = jnp.zeros_like(ds_ref)
            carry_ref[...] = jnp.zeros_like(carry_ref)

        @pl.when(i < nb)
        def _():
            valid = _attn_mask(i)
            tiles = _attn_kv_tiles(kp_ref[...], kc_ref[...])
            lane1 = lax.broadcasted_iota(jnp.int32, (1, 128), 1)
            dsink = jnp.zeros((1, 128), F32)
            s = [lax.dot_general(_pair(q_ref, j), tiles[j // 4][0], _NT, preferred_element_type=F32)
                 for j in range(N_PAIR)]
            dp = [lax.dot_general(_pair(do_ref, j), tiles[j // 4][1], _NT, preferred_element_type=F32)
                  for j in range(N_PAIR)]
            p_all, ds_all = [], []
            for j in range(N_PAIR):
                halves = []
                for par in range(2):
                    cols = slice(par * _KEYS, (par + 1) * _KEYS)
                    p, ps = _attn_probs(s[j][:, cols], sink_ref[0, 2 * j + par], valid)
                    dpj = dp[j][:, cols]
                    dd = jnp.sum(p * dpj, axis=-1, keepdims=True)
                    dsink = dsink + jnp.where(lane1 == 2 * j + par,
                                              -jnp.sum(ps * dd, axis=0, keepdims=True), 0.0)
                    halves.append((p.astype(BF), (p * (dpj - dd)).astype(BF)))
                p_all.append(jnp.concatenate([halves[0][0], halves[1][0]], axis=1))
                ds_all.append(jnp.concatenate([halves[0][1], halves[1][1]], axis=1))
            for j in range(N_PAIR):
                dq_ref[:, j * 128:(j + 1) * 128] = (
                    jnp.dot(ds_all[j], tiles[j // 4][0], preferred_element_type=F32) * SCALE).astype(BF)
            ds_ref[...] += dsink
            gk, gv = [], []
            for h in range(2):
                grp = range(4 * h, 4 * h + 4)
                q_rows = jnp.concatenate([_pair(q_ref, j) for j in grp], axis=0)
                do_rows = jnp.concatenate([_pair(do_ref, j) for j in grp], axis=0)
                g_k = lax.dot_general(jnp.concatenate([ds_all[j] for j in grp], axis=0), q_rows, _TN,
                                      preferred_element_type=F32)
                g_v = lax.dot_general(jnp.concatenate([p_all[j] for j in grp], axis=0), do_rows, _TN,
                                      preferred_element_type=F32)
                gk.append((g_k[0:_KEYS], g_k[_KEYS:2 * _KEYS]))
                gv.append((g_v[0:_KEYS], g_v[_KEYS:2 * _KEYS]))
            lo = lax.broadcasted_iota(jnp.int32, (2 * BLK, 128), 1) < HEAD
            zero = jnp.zeros((2 * BLK, 128), F32)

            def unpad(g):
                return (jnp.where(lo, g[0][0] + pltpu.roll(g[0][1], HEAD, 1), zero)
                        + jnp.where(lo, zero, pltpu.roll(g[1][0], HEAD, 1) + g[1][1]))

            dk = unpad(gk) * SCALE
            dv = unpad(gv)
            dkv_ref[:, 0:128] = (carry_ref[:, 0:128] + dk[0:BLK]).astype(BF)
            dkv_ref[:, 128:256] = (carry_ref[:, 128:256] + dv[0:BLK]).astype(BF)
            carry_ref[:, 0:128] = dk[BLK:2 * BLK]
            carry_ref[:, 128:256] = dv[BLK:2 * BLK]

        @pl.when(i == nb)
        def _():
            dkv_ref[...] = carry_ref[...].astype(BF)

    return _hosted_call(
        body, comm, (sinks, q, kv, kv, do), name="attn_bwd", grid=(nb + 1,),
        in_specs=[pl.BlockSpec(memory_space=pltpu.SMEM),
                  pl.BlockSpec((BLK, D), lambda i: (jnp.minimum(i, last), 0)),
                  pl.BlockSpec((BLK, 256), lambda i: (jnp.clip(i - 1, 0, last), 0)),
                  pl.BlockSpec((BLK, 256), lambda i: (jnp.minimum(i, last), 0)),
                  pl.BlockSpec((BLK, D), lambda i: (jnp.minimum(i, last), 0))],
        out_specs=[pl.BlockSpec((BLK, D), lambda i: (jnp.minimum(i, last), 0)),
                   pl.BlockSpec((BLK, 256), lambda i: (jnp.maximum(i - 1, 0), 0)),
                   pl.BlockSpec((1, 128), lambda i: (0, 0))],
        out_shape=[jax.ShapeDtypeStruct((t, D), BF), jax.ShapeDtypeStruct((t, 256), BF),
                   jax.ShapeDtypeStruct((1, 128), F32)],
        scratch_shapes=[pltpu.VMEM((BLK, 256), F32)], sem=("arbitrary",), nsteps=nb + 1,
        step_fn=lambda: pl.program_id(0))


def _split3(v):
    h = v.astype(BF)
    r = v - h.astype(F32)
    m = r.astype(BF)
    lo = (r - m.astype(F32)).astype(BF)
    return jnp.concatenate([h, m, lo], axis=1)


def _apply01(mat, v):
    n = v.shape[1]
    r = jnp.dot(mat, _split3(v), preferred_element_type=F32)
    return r[:, 0:n] + r[:, n:2 * n] + r[:, 2 * n:3 * n]


def _hgrn_gates(hq, hf, lb):
    sq = _sig(hq)
    sg = _sig(hf)
    f = lb + (1.0 - lb) * sg
    return hq * sq, (1.0 - lb) * (1.0 - sg), jnp.log(f), sq, sg, f


def _tri(upper):
    r = lax.broadcasted_iota(jnp.int32, (CH, CH), 0)
    c = lax.broadcasted_iota(jnp.int32, (CH, CH), 1)
    return (c >= r) if upper else (c <= r)


def _lb_from_logits(lg_ref):
    return 1.0 / (1.0 + jnp.exp(lg_ref[1:2, :] - lg_ref[0:1, :]))


def _hgrn_fwd(h4, logits, norm_g, *, t, comm=None):
    nc = t // CH
    nt_dims = (((1,), (1,)), ((), ()))
    tn_dims = (((0,), (0,)), ((), ()))

    def body(h_ref, lg_ref, ng_ref, y_ref, o_ref, st_ref, s_scr, b_scr, qa_s, ka_s, qb_s, kb_s, v_s):
        @pl.when(pl.program_id(0) == 0)
        def _():
            s_scr[...] = jnp.zeros_like(s_scr)

        heads = [slice(h * HG_K, (h + 1) * HG_K) for h in range(HG_HEADS)]
        causal = _tri(False)
        lb = _lb_from_logits(lg_ref)
        for c in range(HG_SUB):
            rows = slice(c * CH, (c + 1) * CH)
            q, k, g, _, _, _ = _hgrn_gates(h_ref[rows, 0:D], h_ref[rows, D:2 * D], lb)
            b_scr[...] = _apply01(jnp.where(causal, 1.0, 0.0).astype(BF), g)
            b = b_scr[...]
            b_mid = b_scr[CH // 2 - 1:CH // 2, :]
            b_last = b_scr[CH - 1:CH, :]
            qa_s[...] = (q * jnp.exp(b - b_mid)).astype(BF)
            ka_s[...] = (k * jnp.exp(b_mid - b)).astype(BF)
            qb_s[...] = (q * jnp.exp(b)).astype(BF)
            kb_s[...] = (k * jnp.exp(b_last - b)).astype(BF)
            v_s[...] = h_ref[rows, 2 * D:3 * D].astype(BF)
            dec = jnp.exp(b_last)
            st_ref[c] = s_scr[...].astype(BF)
            a = [jnp.where(causal, lax.dot_general(qa_s[:, sl], ka_s[:, sl], nt_dims, preferred_element_type=F32),
                           0.0).astype(BF) for sl in heads]
            for h, sl in enumerate(heads):
                o_ref[rows, sl] = (jnp.dot(a[h], v_s[:, sl], preferred_element_type=F32)
                                   + lax.dot_general(qb_s[:, sl], s_scr[h].astype(BF), nt_dims,
                                                     preferred_element_type=F32))
            for h, sl in enumerate(heads):
                s_scr[h] = dec[:, sl] * s_scr[h] + lax.dot_general(v_s[:, sl], kb_s[:, sl], tn_dims,
                                                                   preferred_element_type=F32)
            for h, sl in enumerate(heads):
                o = o_ref[rows, sl]
                on = o * lax.rsqrt(jnp.mean(o * o, axis=-1, keepdims=True) + EPS)
                gate = _sig(h_ref[rows, 3 * D + h * HG_K:3 * D + (h + 1) * HG_K])
                y_ref[rows, sl] = (on * ng_ref[:, sl] * gate).astype(BF)

    half = lambda: pltpu.VMEM((CH, D), BF)
    blk = HG_SUB * CH
    return _hosted_call(
        body, comm, (h4, logits, norm_g), name="hgrn_fwd", grid=(nc // HG_SUB,),
        in_specs=[pl.BlockSpec((blk, 4 * D), lambda n: (n, 0)),
                  pl.BlockSpec((2, D), lambda n: (0, 0)),
                  pl.BlockSpec((1, D), lambda n: (0, 0))],
        out_specs=[pl.BlockSpec((blk, D), lambda n: (n, 0)),
                   pl.BlockSpec((blk, D), lambda n: (n, 0)),
                   pl.BlockSpec((HG_SUB, HG_HEADS, HG_K, HG_K), lambda n: (n, 0, 0, 0))],
        out_shape=[jax.ShapeDtypeStruct((t, D), BF), jax.ShapeDtypeStruct((t, D), F32),
                   jax.ShapeDtypeStruct((nc, HG_HEADS, HG_K, HG_K), BF)],
        scratch_shapes=[pltpu.VMEM((HG_HEADS, HG_K, HG_K), F32), pltpu.VMEM((CH, D), F32),
                        half(), half(), half(), half(), half()],
        sem=("arbitrary",), nsteps=nc // HG_SUB, step_fn=lambda: pl.program_id(0))


def _hgrn_bwd(h4, logits, norm_g, o_pre, states, dy, *, t, comm=None):
    nc = t // CH
    nt_dims = (((1,), (1,)), ((), ()))
    tn_dims = (((0,), (0,)), ((), ()))

    def body(h_ref, lg_ref, ng_ref, o_ref, st_ref, dy_ref, dh_ref, dlg_ref, dng_ref, ds_scr, dlb_scr,
             b_scr, tail_s, e_qa, e_ka, e_qb, e_kb, q_s, k_s, dqa_s, dka_s, dqb_s, dkb_s,
             qa_s, ka_s, qb_s, kb_s, v_s, do_s):
        n = pl.program_id(0)

        @pl.when(n == 0)
        def _():
            ds_scr[...] = jnp.zeros_like(ds_scr)
            dlb_scr[...] = jnp.zeros_like(dlb_scr)
            dng_ref[...] = jnp.zeros_like(dng_ref)

        heads = [slice(h * HG_K, (h + 1) * HG_K) for h in range(HG_HEADS)]
        lb = _lb_from_logits(lg_ref)
        causal = _tri(False)

        def chunk(c):
            rows = slice(c * CH, (c + 1) * CH)
            hq = h_ref[rows, 0:D]
            q, k, g, sq, sg, f = _hgrn_gates(hq, h_ref[rows, D:2 * D], lb)
            b_scr[...] = _apply01(jnp.where(causal, 1.0, 0.0).astype(BF), g)
            b = b_scr[...]
            b_mid = b_scr[CH // 2 - 1:CH // 2, :]
            b_last = b_scr[CH - 1:CH, :]
            q_s[...] = q
            k_s[...] = k
            for e_ref, s_ref, base, expo in ((e_qa, qa_s, q, b - b_mid), (e_ka, ka_s, k, b_mid - b),
                                             (e_qb, qb_s, q, b), (e_kb, kb_s, k, b_last - b)):
                e = jnp.exp(expo)
                e_ref[...] = e
                s_ref[...] = (base * e).astype(BF)
            v_s[...] = h_ref[rows, 2 * D:3 * D].astype(BF)
            dec = jnp.exp(b_last)
            for h, sl in enumerate(heads):
                gcol = slice(3 * D + h * HG_K, 3 * D + (h + 1) * HG_K)
                ngh = ng_ref[:, sl]
                sgate = _sig(h_ref[rows, gcol])
                o = o_ref[rows, sl]
                r = lax.rsqrt(jnp.mean(o * o, axis=-1, keepdims=True) + EPS)
                on = o * r
                dyh = dy_ref[rows, sl]
                dh_ref[rows, gcol] = (dyh * on * ngh * sgate * (1.0 - sgate)).astype(BF)
                dng_ref[:, sl] += jnp.sum(dyh * on * sgate, axis=0, keepdims=True)
                don = dyh * ngh * sgate
                do_s[:, sl] = (r * (don - on * jnp.mean(don * on, axis=-1, keepdims=True))).astype(BF)
            a = [jnp.where(causal, lax.dot_general(qa_s[:, sl], ka_s[:, sl], nt_dims, preferred_element_type=F32),
                           0.0).astype(BF) for sl in heads]
            da = [jnp.where(causal, lax.dot_general(do_s[:, sl], v_s[:, sl], nt_dims, preferred_element_type=F32),
                            0.0).astype(BF) for sl in heads]
            for h, sl in enumerate(heads):
                dh_ref[rows, 2 * D + h * HG_K:2 * D + (h + 1) * HG_K] = (
                    lax.dot_general(a[h], do_s[:, sl], tn_dims, preferred_element_type=F32)
                    + lax.dot_general(kb_s[:, sl], ds_scr[h].astype(BF), nt_dims, preferred_element_type=F32)
                ).astype(BF)
            for h, sl in enumerate(heads):
                dqa_s[:, sl] = jnp.dot(da[h], ka_s[:, sl], preferred_element_type=F32)
            for h, sl in enumerate(heads):
                dka_s[:, sl] = lax.dot_general(da[h], qa_s[:, sl], tn_dims, preferred_element_type=F32)
            for h, sl in enumerate(heads):
                dqb_s[:, sl] = jnp.dot(do_s[:, sl], st_ref[c, h], preferred_element_type=F32)
            for h, sl in enumerate(heads):
                dkb_s[:, sl] = jnp.dot(v_s[:, sl], ds_scr[h].astype(BF), preferred_element_type=F32)
            for h, sl in enumerate(heads):
                tail_s[:, sl] = jnp.sum(dec[:, sl] * st_ref[c, h].astype(F32) * ds_scr[h], axis=0, keepdims=True)
            for h, sl in enumerate(heads):
                ds_scr[h] = (lax.dot_general(do_s[:, sl], qb_s[:, sl], tn_dims, preferred_element_type=F32)
                             + dec[:, sl] * ds_scr[h])
            qv, kv = q_s[...], k_s[...]
            dqa, dka, dqb, dkb = dqa_s[...], dka_s[...], dqb_s[...], dkb_s[...]
            eqa, eka, eqb, ekb = e_qa[...], e_ka[...], e_qb[...], e_kb[...]
            dkb_kb = dkb * (kv * ekb)
            db_last = jnp.sum(dkb_kb, axis=0, keepdims=True) + tail_s[...]
            last_row = lax.broadcasted_iota(jnp.int32, (CH, D), 0) == CH - 1
            db = (dqa * (qv * eqa) - dka * (kv * eka) + dqb * (qv * eqb) - dkb_kb
                  + jnp.where(last_row, db_last, 0.0))
            dg = _apply01(jnp.where(_tri(True), 1.0, 0.0).astype(BF), db)
            dq = dqa * eqa + dqb * eqb
            dk = dka * eka + dkb * ekb
            dh_ref[rows, 0:D] = (dq * sq * (1.0 + hq * (1.0 - sq))).astype(BF)
            dfk = dg / f - dk
            dh_ref[rows, D:2 * D] = ((1.0 - lb) * dfk * sg * (1.0 - sg)).astype(BF)
            dlb_scr[...] += jnp.sum((1.0 - sg) * dfk, axis=0, keepdims=True)

        for c in reversed(range(HG_SUB)):
            chunk(c)

        @pl.when(n == nc // HG_SUB - 1)
        def _():
            dl0 = dlb_scr[...] * lb * (1.0 - lb)
            dlg_ref[0:1, :] = dl0
            dlg_ref[1:2, :] = -dl0

    steps = nc // HG_SUB
    blk = HG_SUB * CH
    rev = lambda n: (steps - 1 - n, 0)
    return _hosted_call(
        body, comm, (h4, logits, norm_g, o_pre, states, dy), name="hgrn_bwd", grid=(steps,),
        in_specs=[pl.BlockSpec((blk, 4 * D), rev),
                  pl.BlockSpec((2, D), lambda n: (0, 0)),
                  pl.BlockSpec((1, D), lambda n: (0, 0)),
                  pl.BlockSpec((blk, D), rev),
                  pl.BlockSpec((HG_SUB, HG_HEADS, HG_K, HG_K), lambda n: (steps - 1 - n, 0, 0, 0)),
                  pl.BlockSpec((blk, D), rev)],
        out_specs=[pl.BlockSpec((blk, 4 * D), rev),
                   pl.BlockSpec((2, D), lambda n: (0, 0)),
                   pl.BlockSpec((1, D), lambda n: (0, 0))],
        out_shape=[jax.ShapeDtypeStruct((t, 4 * D), BF), jax.ShapeDtypeStruct((2, D), F32),
                   jax.ShapeDtypeStruct((1, D), F32)],
        scratch_shapes=([pltpu.VMEM((HG_HEADS, HG_K, HG_K), F32), pltpu.VMEM((1, D), F32),
                         pltpu.VMEM((CH, D), F32), pltpu.VMEM((1, D), F32)]
                        + [pltpu.VMEM((CH, D), F32)] * 10 + [pltpu.VMEM((CH, D), BF)] * 6),
        sem=("arbitrary",), nsteps=steps, step_fn=lambda: pl.program_id(0))


def _place():
    x, y, c = lax.axis_index("x"), lax.axis_index("y"), lax.axis_index("c")
    return x, y, c, [(1 - x, y), (x, 1 - y), (1 - x, 1 - y)]


def _gather_comm(shards, mid):
    n = len(shards)
    r = [s.shape[0] for s in shards]

    def tools(ins, outs, sems):
        send_sems, recv_sems, local_sems = sems
        x, y, c, chips = _place()
        me, sib = (x, y, c), (x, y, 1 - c)

        def rows(w, dev):
            return outs[w].at[pl.ds((4 * dev[0] + 2 * dev[1] + dev[2]) * r[w], r[w]), :]

        def copy(kind, w, block, to, src=None):
            return pltpu.make_async_remote_copy(
                src_ref=rows(w, block) if src is None else src, dst_ref=rows(w, block),
                send_sem=send_sems.at[kind], recv_sem=recv_sems.at[kind], device_id=to, device_id_type=MESH)

        def all_of(kind):
            whole = outs[0].at[pl.ds(0, sum(r)), :]
            return pltpu.make_async_remote_copy(
                src_ref=whole, dst_ref=whole, send_sem=send_sems.at[kind], recv_sem=recv_sems.at[kind],
                device_id=me, device_id_type=MESH)

        mine = [pltpu.make_async_copy(ins[w], rows(w, me), local_sems.at[w]) for w in range(n)]
        return c, chips, me, sib, copy, all_of, mine

    def start(ins, outs, sems):
        c, chips, me, sib, copy, _, mine = tools(ins, outs, sems)
        for cp in mine:
            cp.start()
        for w in range(n):
            copy(0, w, me, sib, src=ins[w]).start()
            for j, chip in enumerate(chips):
                copy(1 + j, w, me, (*chip, c), src=ins[w]).start()

    def pass_on(ins, outs, sems):
        c, chips, _, sib, copy, all_of, _ = tools(ins, outs, sems)
        for j, chip in enumerate(chips):
            all_of(1 + j).wait_recv()
            for w in range(n):
                copy(4 + j, w, (*chip, c), sib).start()

    def finish(ins, outs, sems):
        _, _, _, _, _, all_of, mine = tools(ins, outs, sems)
        all_of(0).wait_recv()
        for j in range(3):
            all_of(4 + j).wait_recv()
        for kind in range(7):
            all_of(kind).wait_send()
        for cp in mine:
            cp.wait()

    return _Comm(shards, [jax.ShapeDtypeStruct((N_DEV * rw, D), BF) for rw in r],
                 [pltpu.SemaphoreType.DMA((7,)), pltpu.SemaphoreType.DMA((7,)), pltpu.SemaphoreType.DMA((n,))],
                 [(0.0, start), (mid, pass_on), (1.0, finish)])


def _pair_comm(grads):
    n = len(grads)
    r = [g.shape[0] // N_DEV for g in grads]

    def start(ins, outs, sems):
        send_sems, recv_sems = sems
        x, y, c, _ = _place()
        for w in range(n):
            for a in range(N_CHIP):
                pltpu.make_async_remote_copy(
                    src_ref=ins[w].at[pl.ds((2 * a + 1 - c) * r[w], r[w]), :], dst_ref=outs[w].at[a],
                    send_sem=send_sems.at[w], recv_sem=recv_sems.at[w],
                    device_id=(x, y, 1 - c), device_id_type=MESH).start()

    def finish(ins, outs, sems):
        send_sems, recv_sems = sems
        x, y, c, _ = _place()
        for w in range(n):
            pltpu.make_async_remote_copy(
                src_ref=outs[w], dst_ref=outs[w], send_sem=send_sems.at[w], recv_sem=recv_sems.at[w],
                device_id=(x, y, c), device_id_type=MESH).wait()

    return _Comm(grads, [jax.ShapeDtypeStruct((N_CHIP, rw, D), BF) for rw in r],
                 [pltpu.SemaphoreType.DMA((n,)), pltpu.SemaphoreType.DMA((n,))],
                 [(0.0, start), (1.0, finish)])


def _pair_add(grad, got, core, *, name):
    r = got.shape[1]

    def body(c_ref, g_ref, got_ref, o_ref):
        o_ref[0] = (g_ref[...].astype(F32) + got_ref[0].astype(F32)).astype(BF)

    grid_spec = pltpu.PrefetchScalarGridSpec(
        num_scalar_prefetch=1, grid=(N_CHIP,),
        in_specs=[pl.BlockSpec((r, D), lambda a, c_ref: (2 * a + c_ref[0], 0)),
                  pl.BlockSpec((1, r, D), lambda a, c_ref: (a, 0, 0))],
        out_specs=pl.BlockSpec((1, r, D), lambda a, c_ref: (a, 0, 0)))
    return _pcall(body, name=name, grid_spec=grid_spec,
                  out_shape=jax.ShapeDtypeStruct((N_CHIP, r, D), BF),
                  compiler_params=_cp(("parallel",)))(core, grad, got)


def _chip_comm(pair_sums):
    n = len(pair_sums)
    r = [p.shape[1] for p in pair_sums]
    off = [sum(r[:w]) for w in range(n)]

    def tools(ins, outs, sems):
        send_sems, recv_sems, local_sems = sems
        x, y, c, chips = _place()
        my_chip = 2 * x + y

        def slot(w):
            return outs[0].at[my_chip, pl.ds(off[w], r[w]), :]

        own = [pltpu.make_async_copy(ins[w].at[my_chip], slot(w), local_sems.at[w]) for w in range(n)]
        return x, y, c, chips, my_chip, slot, own, send_sems, recv_sems

    def start(ins, outs, sems):
        x, y, c, chips, my_chip, slot, own, send_sems, recv_sems = tools(ins, outs, sems)
        for cp in own:
            cp.start()
        for j, chip in enumerate(chips):
            for w in range(n):
                pltpu.make_async_remote_copy(
                    src_ref=ins[w].at[2 * chip[0] + chip[1]], dst_ref=slot(w), send_sem=send_sems.at[j],
                    recv_sem=recv_sems.at[j], device_id=(*chip, c), device_id_type=MESH).start()

    def finish(ins, outs, sems):
        x, y, c, chips, my_chip, slot, own, send_sems, recv_sems = tools(ins, outs, sems)
        whole = outs[0].at[my_chip]
        for j in range(3):
            pltpu.make_async_remote_copy(
                src_ref=whole, dst_ref=whole, send_sem=send_sems.at[j], recv_sem=recv_sems.at[j],
                device_id=(x, y, c), device_id_type=MESH).wait()
        for cp in own:
            cp.wait()

    return _Comm(pair_sums, [jax.ShapeDtypeStruct((N_CHIP, sum(r), D), BF)],
                 [pltpu.SemaphoreType.DMA((3,)), pltpu.SemaphoreType.DMA((3,)), pltpu.SemaphoreType.DMA((n,))],
                 [(0.0, start), (1.0, finish)])


def _adam_math(w, g, m, v):
    m = ADAM_B1 * m + (1.0 - ADAM_B1) * g
    v = ADAM_B2 * v + (1.0 - ADAM_B2) * (g * g)
    m_hat = m / (1.0 - ADAM_B1 ** ADAM_STEP)
    v_hat = v / (1.0 - ADAM_B2 ** ADAM_STEP)
    delta = -ADAM_LR * (m_hat / (jnp.sqrt(v_hat) + ADAM_EPS) + ADAM_WD * w)
    return delta, m, v


SMALL = (("norm_mix_g", (1, D), 0), ("hgrn_norm_g", (1, D), 1), ("norm_ffn_g", (1, D), 2),
         ("norm_final_g", (1, D), 3), ("hgrn_lb_logits", (2, D), 4), ("attn_sinks", (1, 16), 6),
         ("b_in", (1, IN_W), 8))
LOSS_ROW = 7


def _small_allreduce_adam(grads, loss_row, params):
    n = len(SMALL)

    def rows_of(ref, shape, row):
        r, w = shape
        if w <= D:
            return ref[row:row + r, 0:w]
        pieces = [ref[row + k:row + k + 1, :] for k in range(-(-w // D))]
        return jnp.concatenate(pieces, axis=1)[:, 0:w]

    def body(*refs):
        g_refs, loss_ref = refs[:n], refs[n]
        wmv = refs[n + 1:4 * n + 1]
        loss_out = refs[4 * n + 1]
        outs = refs[4 * n + 2:8 * n + 2]
        mine, total, gath, send_sems, recv_sems = refs[8 * n + 2:]
        x, y, c, _ = _place()
        me = 4 * x + 2 * y + c
        mine[...] = jnp.zeros_like(mine)
        for g_ref, (_, (r, w), row) in zip(g_refs, SMALL):
            for k in range(-(-w // D)):
                wk = min(D, w - k * D)
                mine[row + k:row + k + r, 0:wk] = g_ref[:, k * D:k * D + wk]
        mine[LOSS_ROW:LOSS_ROW + 1, 0:128] = loss_ref[...]
        gath[me] = mine[...]
        cps = []
        for d in range(1, N_DEV):
            peer = (x ^ (d >> 2), y ^ ((d >> 1) & 1), c ^ (d & 1))
            cps.append(pltpu.make_async_remote_copy(
                src_ref=mine, dst_ref=gath.at[me], send_sem=send_sems.at[d - 1],
                recv_sem=recv_sems.at[d - 1], device_id=peer, device_id_type=MESH))
        for cp in cps:
            cp.start()
        for cp in cps:
            cp.wait()
        g = gath[0]
        for k in range(1, N_DEV):
            g = g + gath[k]
        total[...] = g
        loss_out[...] = total[LOSS_ROW:LOSS_ROW + 1, 0:128]
        for i, (_, shape, row) in enumerate(SMALL):
            gi = rows_of(total, shape, row)
            w_ref, m_ref, v_ref = wmv[3 * i:3 * i + 3]
            o = outs[4 * i:4 * i + 4]
            o[0][...] = gi
            o[1][...], o[2][...], o[3][...] = _adam_math(w_ref[...], gi, m_ref[...], v_ref[...])

    vm = pl.BlockSpec(memory_space=pltpu.VMEM)
    ins = [grads[name] for name, _, _ in SMALL] + [loss_row]
    for name, _, _ in SMALL:
        ins += list(params[name])
    out_shape = [jax.ShapeDtypeStruct((1, 128), F32)]
    for _, shape, _ in SMALL:
        out_shape += [jax.ShapeDtypeStruct(shape, F32)] * 4
    res = _pcall(body, name="small_allreduce_adam", in_specs=[vm] * len(ins), out_specs=[vm] * len(out_shape),
                 out_shape=out_shape,
                 scratch_shapes=[pltpu.VMEM((SMALL_ROWS, D), F32), pltpu.VMEM((SMALL_ROWS, D), F32),
                                 pltpu.VMEM((N_DEV, SMALL_ROWS, D), F32),
                                 pltpu.SemaphoreType.DMA((N_DEV - 1,)), pltpu.SemaphoreType.DMA((N_DEV - 1,))],
                 compiler_params=pltpu.CompilerParams(has_side_effects=True))(*ins)
    return res[0], {name: res[1 + 4 * i:5 + 4 * i] for i, (name, _, _) in enumerate(SMALL)}


def _adam(w, parts, index, m, v, *, name):
    rows = w.shape[0]
    tr = rows if rows <= 512 else rows // 2
    steps = rows // tr

    def body(w_ref, p_ref, m_ref, v_ref, g_ref, d_ref, mo_ref, vo_ref):
        g = p_ref[0].astype(F32)
        for a in range(1, N_CHIP):
            g = g + p_ref[a].astype(F32)
        g_ref[...] = g
        d_ref[...], mo_ref[...], vo_ref[...] = _adam_math(w_ref[...], g, m_ref[...], v_ref[...])

    spec = pl.BlockSpec((tr, D), lambda i: (i, 0))
    return _pcall(body, name=name, grid=(steps,),
                  in_specs=[spec, pl.BlockSpec((N_CHIP, tr, D), lambda i: (0, index * steps + i, 0)), spec, spec],
                  out_specs=[spec] * 4, out_shape=[jax.ShapeDtypeStruct((rows, D), F32)] * 4,
                  compiler_params=_cp(("parallel",)))(w, parts, m, v)


def _step(x, tgt, shards, norm_mix_g, b_in, sinks, logits, hgrn_norm_g, norm_ffn_g, norm_final_g):
    t = x.shape[0]
    core = lax.axis_index("c").astype(jnp.int32).reshape(1)

    u1, (win_t,) = _rms_fwd(x, norm_mix_g, tm=512, name="rms_mix", comm=_gather_comm(shards[0:1], 0.5))
    (q, kv, h4, gates), (wg_t, wba, wbh, wout) = _inproj_fwd(
        u1, win_t, b_in, t=t, comm=_gather_comm([shards[1]] + shards[4:7], 0.8))
    (y_attn,), _ = _attn_fwd(q, kv, sinks, t=t)
    (y_hgrn, o_pre, states), (wu_t, wd) = _hgrn_fwd(h4, logits, hgrn_norm_g, t=t,
                                                    comm=_gather_comm(shards[2:4], 0.8))
    col = lambda j: j
    first, second = (lambda j: 0), (lambda j: 1)
    gate_tiles = [(gates, D, first), (gates, D, second)]

    def merge(prods, ex):
        (ya_, yb_), (ga, gb) = prods, ex
        sa, sb = _sig(ga), _sig(gb)
        return sa, sb, ya_ * sa * (1.0 - sa), yb_ * sb * (1.0 - sb), sa * ya_ + sb * yb_

    sig_a, sig_b, dgate_a, dgate_b, merged = _fmm(
        [y_attn, y_hgrn], [(0, wba, False), (1, wbh, False)], gate_tiles, merge,
        [(BF, D, D, first)] * 5, m=t, n=D, tm=512, tn=D, name="branch_merge")
    def resid_norm(prods, ex):
        (p,), (xv, gv) = prods, ex
        hv = xv + p
        return hv, hv * lax.rsqrt(jnp.mean(hv * hv, axis=-1, keepdims=True) + EPS) * gv

    h1, u2 = _fmm([merged], [(0, wout, False)], [(x, D, first)], resid_norm, [(F32, D, D, first), (BF, D, D, first)],
                  m=t, n=D, tm=1024, tn=D, name="out_proj", vecs=[norm_ffn_g])

    def swiglu(prods, ex):
        g_, u_ = prods
        s = _sig(g_)
        silu = g_ * s
        return u_ * s * (1.0 + g_ * (1.0 - s)), silu, silu * u_

    dz_dgate, dz_dup, z = _fmm([u2], [(0, wg_t, True), (0, wu_t, True)], [], swiglu,
                               [(BF, FFN, FFN // 2, col)] * 3, m=t, n=FFN, tm=1024, tn=FFN // 2,
                               name="ffn_gate_up")
    def loss_head(prods, ex):
        (p,), (hv, tv, gv) = prods, ex
        hv = hv + p
        r = lax.rsqrt(jnp.mean(hv * hv, axis=-1, keepdims=True) + EPS)
        xh = hv * r
        err = xh * gv - tv
        lp = jnp.sum(jnp.sum(err * err, axis=1, keepdims=True), axis=0, keepdims=True) * (0.5 / D)
        dy = err * (1.0 / D)
        dxh = dy * gv
        dh = r * (dxh - xh * jnp.mean(dxh * xh, axis=-1, keepdims=True))
        return dh, dh, jnp.sum(dy * xh, axis=0, keepdims=True), jnp.broadcast_to(lp, (1, 128))

    dh2, dh2_b, d_norm_final, loss_row = _fmm(
        [z], [(0, wd, False)], [(h1, D, first), (tgt, D, first)], loss_head, [(F32, D, D, first), (BF, D, D, first)],
        m=t, n=D, tm=512, tn=D, name="ffn_down_loss", vecs=[norm_final_g], sums=[D, 128])

    def swiglu_bwd(prods, ex):
        (dz,), (da_, db_) = prods, ex
        return dz * da_.astype(F32), dz * db_.astype(F32)

    ffn_tiles = [(dz_dgate, FFN // 2, col), (dz_dup, FFN // 2, col)]
    dgt, dup = _fmm([dh2_b], [(0, wd, True)], ffn_tiles, swiglu_bwd, [(BF, FFN, FFN // 2, col)] * 2,
                    m=t, n=FFN, tm=1024, tn=FFN // 2, name="d_gate_up")
    d_wd = _wgrad(z, dh2_b, name="d_w_down")
    (du2,) = _fmm([dgt, dup], [(0, wg_t, False), (1, wu_t, False)], [], lambda prods, ex: (prods[0] + prods[1],),
                  [(F32, D, 512, col)], m=t, n=D, tm=1024, tn=512, name="d_u2")
    d_wg = _wgrad(dgt, u2, name="d_w_gate")
    d_wu = _wgrad(dup, u2, name="d_w_up")
    dh1, dh1_b, d_norm_ffn = _rms_bwd(du2, h1, norm_ffn_g, dh2, tm=512, name="rms_ffn_bwd")
    d_wout = _wgrad(merged, dh1_b, name="d_w_out")

    def merge_bwd(prods, ex):
        (dm,), (sa, sb, ca, cb, wa, wb) = prods, ex
        dgate = jnp.concatenate([dm * ca.astype(F32), dm * cb.astype(F32)], axis=1)
        dya_ = (dm * sa.astype(F32)).astype(BF)
        dyb_ = (dm * sb.astype(F32)).astype(BF)
        return (dya_, dyb_, dgate, lax.dot_general(dya_, wa, _NT, preferred_element_type=F32),
                lax.dot_general(dyb_, wb, _NT, preferred_element_type=F32))

    ffn_grads = (d_wg, d_wu, d_wd)
    (dya, dyb, dgates, dy_attn, dy_hgrn), got = _fmm(
        [dh1_b], [(0, wout, True)], [(a, D, first) for a in (sig_a, sig_b, dgate_a, dgate_b)], merge_bwd,
        [(BF, D, D, first), (BF, D, D, first), (BF, 2 * D, 2 * D, first), (BF, D, D, first), (F32, D, D, first)],
        m=t, n=D, tm=512, tn=D, name="d_merge", consts=[wba, wbh], comm=_pair_comm(ffn_grads))
    pair_ffn = [_pair_add(g, r, core, name="pair_add_ffn%d" % i) for i, (g, r) in enumerate(zip(ffn_grads, got))]
    d_wba = _wgrad(y_attn, dya, name="d_w_ba")
    d_wbh = _wgrad(y_hgrn, dyb, name="d_w_bh")
    sq_grads = (d_wba, d_wbh, d_wout)
    (dq, dkv, d_sinks), (parts_ffn, *got) = _attn_bwd(
        q, kv, sinks, dy_attn, t=t, comm=_both(_chip_comm(pair_ffn), _pair_comm(sq_grads)))
    pair_sq = [_pair_add(g, r, core, name="pair_add_sq%d" % i) for i, (g, r) in enumerate(zip(sq_grads, got))]
    (dh4, d_logits, d_hgrn_norm), (parts_sq,) = _hgrn_bwd(h4, logits, hgrn_norm_g, o_pre, states, dy_hgrn,
                                                           t=t, comm=_chip_comm(pair_sq))
    dps = (dq, dkv, dh4, dgates)
    d_win_t, d_b_in = _inproj_bwd_w(dps, u1, t=t)
    half0, got_in = _inproj_bwd_x(dps, win_t, x, norm_mix_g, dh1, t=t, part=0, comm=_pair_comm([d_win_t]))
    pair_in = _pair_add(d_win_t, got_in[0], core, name="pair_add_w_in")
    (grad_x, d_norm_mix), (parts_in,) = _inproj_bwd_x(dps, win_t, x, norm_mix_g, dh1, t=t, part=1, prev=half0,
                                                      comm=_chip_comm([pair_in]))

    small_grads = (d_norm_mix, d_b_in, d_sinks, d_logits, d_hgrn_norm, d_norm_ffn, d_norm_final)
    return loss_row, grad_x, (parts_in, parts_ffn, parts_sq), small_grads


def kernel(x, norm_mix_g, w_in, b_in, attn_sinks, hgrn_lb_logits, hgrn_norm_g, w_branch_attn, w_branch_hgrn, w_out, norm_ffn_g, w_ffn_gate, w_ffn_up, w_ffn_down, norm_final_g, loss_target, m_norm_mix_g, m_w_in, m_b_in, m_attn_sinks, m_hgrn_lb_logits, m_hgrn_norm_g, m_w_branch_attn, m_w_branch_hgrn, m_w_out, m_norm_ffn_g, m_w_ffn_gate, m_w_ffn_up, m_w_ffn_down, m_norm_final_g, v_norm_mix_g, v_w_in, v_b_in, v_attn_sinks, v_hgrn_lb_logits, v_hgrn_norm_g, v_w_branch_attn, v_w_branch_hgrn, v_w_out, v_norm_ffn_g, v_w_ffn_gate, v_w_ffn_up, v_w_ffn_down, v_norm_final_g):
    shards = [w_in[0].T.astype(BF), w_ffn_gate[0].T.astype(BF), w_ffn_up[0].T.astype(BF),
              w_ffn_down[0].astype(BF), w_branch_attn[0].astype(BF), w_branch_hgrn[0].astype(BF),
              w_out[0].astype(BF)]
    loss_row, grad_x, grad_parts, small_grads = _step(
        x[0], loss_target[0], shards, norm_mix_g, b_in, attn_sinks, hgrn_lb_logits, hgrn_norm_g,
        norm_ffn_g, norm_final_g.reshape(1, D))

    d_norm_mix, d_b_in, d_sinks, d_logits, d_hgrn_norm, d_norm_ffn, d_norm_final = small_grads
    row = lambda a: a.reshape(1, D)
    loss_out, small = _small_allreduce_adam(
        dict(norm_mix_g=d_norm_mix, hgrn_norm_g=d_hgrn_norm, norm_ffn_g=d_norm_ffn, norm_final_g=d_norm_final,
             hgrn_lb_logits=d_logits, attn_sinks=d_sinks, b_in=d_b_in),
        loss_row,
        dict(norm_mix_g=(norm_mix_g, m_norm_mix_g, v_norm_mix_g), hgrn_norm_g=(hgrn_norm_g, m_hgrn_norm_g, v_hgrn_norm_g),
             norm_ffn_g=(norm_ffn_g, m_norm_ffn_g, v_norm_ffn_g),
             norm_final_g=(row(norm_final_g), row(m_norm_final_g), row(v_norm_final_g)),
             hgrn_lb_logits=(hgrn_lb_logits, m_hgrn_lb_logits, v_hgrn_lb_logits),
             attn_sinks=(attn_sinks, m_attn_sinks, v_attn_sinks), b_in=(b_in, m_b_in, v_b_in)))
    small["norm_final_g"] = [a.reshape(D) for a in small["norm_final_g"]]
    loss = loss_out[0, 0]

    names = ["w_in", "w_ffn_gate", "w_ffn_up", "w_ffn_down", "w_branch_attn", "w_branch_hgrn", "w_out"]
    w_full = dict(w_in=(w_in, m_w_in, v_w_in), w_ffn_gate=(w_ffn_gate, m_w_ffn_gate, v_w_ffn_gate),
                  w_ffn_up=(w_ffn_up, m_w_ffn_up, v_w_ffn_up), w_ffn_down=(w_ffn_down, m_w_ffn_down, v_w_ffn_down),
                  w_branch_attn=(w_branch_attn, m_w_branch_attn, v_w_branch_attn),
                  w_branch_hgrn=(w_branch_hgrn, m_w_branch_hgrn, v_w_branch_hgrn),
                  w_out=(w_out, m_w_out, v_w_out))
    parts_in, parts_ffn, parts_sq = grad_parts
    where = [(parts_in, 0), (parts_ffn, 0), (parts_ffn, 1), (parts_ffn, 2), (parts_sq, 0), (parts_sq, 1), (parts_sq, 2)]
    big = {}
    for i, name in enumerate(names):
        view = (lambda a: a[0].T) if i < 3 else (lambda a: a[0])
        back = (lambda a: a.T[None]) if i < 3 else (lambda a: a[None])
        wv, mv, vv = w_full[name]
        res = _adam(view(wv), where[i][0], where[i][1], view(mv), view(vv), name="adam_" + name)
        big[name] = [back(a) for a in res]

    order = ["norm_mix_g", "w_in", "b_in", "attn_sinks", "hgrn_lb_logits", "hgrn_norm_g", "w_branch_attn",
             "w_branch_hgrn", "w_out", "norm_ffn_g", "w_ffn_gate", "w_ffn_up", "w_ffn_down", "norm_final_g"]
    outs = [loss, grad_x[None]]
    for kind in range(4):
        for name in order:
            outs.append(big[name][kind] if name in big else small[name][kind])
    return tuple(outs)
```

```python
import math

import jax
import jax.numpy as jnp
from jax import lax
from jax.experimental import pallas as pl
from jax.experimental.pallas import tpu as pltpu

F32 = jnp.float32
BF = jnp.bfloat16
MESH = pl.DeviceIdType.MESH

D = 1024
HEAD = 64
N_PAIR = 8
BLK = 128
CH = 64
HG_SUB = 2
HG_HEADS = 8
HG_K = 128
FFN = 2816
IN_W = 7424
N_DEV = 8
N_CHIP = 4
EPS = 1e-6
NEG = -1e30
SCALE = 1.0 / math.sqrt(HEAD)
VMEM_LIMIT = 56 * 1024 * 1024
WT = 256

ADAM_LR, ADAM_B1, ADAM_B2, ADAM_EPS, ADAM_WD, ADAM_STEP = 0.001, 0.9, 0.999, 1e-08, 0.01, 10

SLAB_R = (IN_W // N_DEV, FFN // N_DEV, FFN // N_DEV, FFN // N_DEV, D // N_DEV, D // N_DEV, D // N_DEV)
SLAB_ROWS = sum(SLAB_R)
SLAB_OFF = tuple(sum(SLAB_R[:i]) for i in range(len(SLAB_R)))
N_W = len(SLAB_R)
GRP_OFF = (0, D // WT, (D + 256) // WT, (5 * D + 256) // WT)
GRP_N = (D // WT, 256 // WT, 4 * D // WT, 2 * D // WT)
SMALL_ROWS = 16


_NN = (((1,), (0,)), ((), ()))
_NT = (((1,), (1,)), ((), ()))
_TN = (((0,), (0,)), ((), ()))


def _pcall(body, **kw):
    return pl.pallas_call(body, **kw)


def _cp(sem=None, **kw):
    return pltpu.CompilerParams(dimension_semantics=sem, vmem_limit_bytes=VMEM_LIMIT, **kw)


def _sig(v):
    return 0.5 * jnp.tanh(0.5 * v) + 0.5


def _accum(ref, val, first):
    @pl.when(first)
    def _():
        ref[...] = val

    @pl.when(jnp.logical_not(first))
    def _():
        ref[...] += val


class _Comm:
    def __init__(self, ins, out_shapes, sem_shapes, phases):
        self.ins, self.out_shapes, self.sem_shapes, self.phases = list(ins), list(out_shapes), list(sem_shapes), phases


def _both(a, b):
    ni, no, ns = len(a.ins), len(a.out_shapes), len(a.sem_shapes)

    def of_a(fn):
        return lambda ins, outs, sems: fn(ins[:ni], outs[:no], sems[:ns])

    def of_b(fn):
        return lambda ins, outs, sems: fn(ins[ni:], outs[no:], sems[ns:])

    return _Comm(a.ins + b.ins, a.out_shapes + b.out_shapes, a.sem_shapes + b.sem_shapes,
                 [(f, of_a(fn)) for f, fn in a.phases] + [(f, of_b(fn)) for f, fn in b.phases])


def _host(body, comm, n_in, n_out, n_scr, nsteps, step_fn):
    if comm is None:
        return body
    ci, co = len(comm.ins), len(comm.out_shapes)

    def wrapped(*refs):
        p = 0
        ins, p = refs[p:p + n_in], p + n_in
        cins, p = refs[p:p + ci], p + ci
        outs, p = refs[p:p + n_out], p + n_out
        couts, p = refs[p:p + co], p + co
        scr, p = refs[p:p + n_scr], p + n_scr
        csems = refs[p:]
        step = step_fn()
        for frac, fn in comm.phases:
            if frac < 1.0:
                @pl.when(step == int(round(frac * (nsteps - 1))))
                def _(fn=fn):
                    fn(cins, couts, csems)
        body(*ins, *outs, *scr)
        for frac, fn in comm.phases:
            if frac >= 1.0:
                @pl.when(step == nsteps - 1)
                def _(fn=fn):
                    fn(cins, couts, csems)

    return wrapped


def _hosted_call(body, comm, args, *, name, grid, in_specs, out_specs, out_shape, scratch_shapes, sem,
                 nsteps, step_fn, aliases=None):
    n_in, n_out, n_scr = len(in_specs), len(out_specs), len(scratch_shapes)
    args = list(args)
    extra = {}
    if comm is not None:
        in_specs = list(in_specs) + [_hbm_spec()] * len(comm.ins)
        out_specs = list(out_specs) + [_hbm_spec()] * len(comm.out_shapes)
        out_shape = list(out_shape) + comm.out_shapes
        scratch_shapes = list(scratch_shapes) + comm.sem_shapes
        args += comm.ins
        extra = dict(has_side_effects=True)
    outs = _pcall(_host(body, comm, n_in, n_out, n_scr, nsteps, step_fn), name=name, grid=grid,
                  in_specs=in_specs, out_specs=out_specs, out_shape=out_shape, scratch_shapes=scratch_shapes,
                  input_output_aliases=aliases or {}, compiler_params=_cp(sem, **extra))(*args)
    return list(outs[:n_out]), list(outs[n_out:])


def _hbm_spec():
    return pl.BlockSpec(memory_space=pl.ANY)


def _wgrad(a, b, *, name):
    (t, m), n = a.shape, b.shape[1]

    def body(a_ref, b_ref, o_ref):
        o_ref[...] = lax.dot_general(a_ref[...], b_ref[...], _TN, preferred_element_type=F32).astype(BF)

    return _pcall(body, name=name, grid=(m // WT,),
                  in_specs=[pl.BlockSpec((t, WT), lambda i: (0, i)), pl.BlockSpec((t, n), lambda i: (0, 0))],
                  out_specs=pl.BlockSpec((WT, n), lambda i: (i, 0)),
                  out_shape=jax.ShapeDtypeStruct((m, n), BF), compiler_params=_cp(("parallel",)))(a, b)


def _fmm(lhs, rhs, extras, epilogue, outs, *, m, n, tm, tn, name, comm=None, vecs=(), consts=(), sums=()):
    tm, tn = min(tm, m), min(tn, n)
    assert m % tm == 0 and n % tn == 0 and (not sums or tn == n), (name, m, n, tm, tn)
    in_specs, args = [], []
    for a in lhs:
        in_specs.append(pl.BlockSpec((tm, a.shape[1]), lambda i, j: (i, 0)))
        args.append(a)
    for li, b, tb in rhs:
        k = lhs[li].shape[1]
        in_specs.append(pl.BlockSpec((tn, k), lambda i, j: (j, 0)) if tb
                        else pl.BlockSpec((k, tn), lambda i, j: (0, j)))
        args.append(b)
    for arr, w, col in extras:
        in_specs.append(pl.BlockSpec((tm, w), lambda i, j, col=col: (i, col(j))))
        args.append(arr)
    for vec in vecs:
        in_specs.append(pl.BlockSpec((1, tn), lambda i, j: (0, j)))
        args.append(vec)
    for whole in consts:
        in_specs.append(pl.BlockSpec(whole.shape, lambda i, j: (0, 0)))
        args.append(whole)
    out_specs = [pl.BlockSpec((tm, w), lambda i, j, col=col: (i, col(j))) for _, _, w, col in outs]
    out_shape = [jax.ShapeDtypeStruct((m, total), dt) for dt, total, _, _ in outs]
    for w in sums:
        out_specs.append(pl.BlockSpec((1, w), lambda i, j: (0, 0)))
        out_shape.append(jax.ShapeDtypeStruct((1, w), F32))
    nl, nr, ne, no = len(lhs), len(rhs), len(extras) + len(vecs) + len(consts), len(outs)

    def body(*refs):
        prods = []
        for r, (li, _, tb) in enumerate(rhs):
            prods.append(lax.dot_general(refs[li][...], refs[nl + r][...], _NT if tb else _NN,
                                         preferred_element_type=F32))
        vals = epilogue(prods, [ref[...] for ref in refs[nl + nr:nl + nr + ne]])
        o_refs = refs[nl + nr + ne:]
        for o_ref, v in zip(o_refs[:no], vals[:no]):
            o_ref[...] = v.astype(o_ref.dtype)
        for s_ref, v in zip(o_refs[no:], vals[no:]):
            _accum(s_ref, v, pl.program_id(0) == 0)

    gm, gn = m // tm, n // tn
    res, comm_res = _hosted_call(
        body, comm, args, name=name, grid=(gm, gn), in_specs=in_specs, out_specs=out_specs,
        out_shape=out_shape, scratch_shapes=[], sem=("arbitrary", "arbitrary"), nsteps=gm * gn,
        step_fn=lambda: pl.program_id(0) * gn + pl.program_id(1))
    return res if comm is None else (res, comm_res)


def _grp_of(i):
    return [jnp.logical_and(i >= GRP_OFF[g], i < GRP_OFF[g] + GRP_N[g]) for g in range(4)]


def _grp_idx(i, g):
    return jnp.clip(i - GRP_OFF[g], 0, GRP_N[g] - 1)


def _inproj_fwd(u, win_t, b_in, *, t, comm=None):
    dtypes = (BF, BF, F32, F32)
    tm = min(1024, t)
    n_row = t // tm
    n_chunks, h_first, g_first = 8, 2, 6
    sub = D // WT

    def w_block(l):
        return jnp.where(l == 0, GRP_OFF[0], jnp.where(l == 1, GRP_OFF[1], GRP_OFF[2] + sub * (l - h_first)))

    def body(u_ref, *rest):
        w_refs, b_refs, (q_ref, kv_ref, h_ref, g_ref) = rest[:sub], rest[sub:2 * sub], rest[2 * sub:]
        l = pl.program_id(1)

        @pl.when(l == 1)
        def _():
            kv_ref[...] = (lax.dot_general(u_ref[...], w_refs[0][...], _NT, preferred_element_type=F32)
                           + b_refs[0][...]).astype(BF)

        for pred, o_ref in ((l == 0, q_ref), (jnp.logical_and(l >= h_first, l < g_first), h_ref),
                            (l >= g_first, g_ref)):
            @pl.when(pred)
            def _(o_ref=o_ref):
                w = jnp.concatenate([w[...] for w in w_refs], axis=0)
                b = jnp.concatenate([b[...] for b in b_refs], axis=1)
                o_ref[...] = (lax.dot_general(u_ref[...], w, _NT, preferred_element_type=F32) + b).astype(o_ref.dtype)

    return _hosted_call(
        body, comm, [u] + [win_t] * sub + [b_in] * sub, name="inproj_fwd", grid=(n_row, n_chunks),
        in_specs=[pl.BlockSpec((tm, D), lambda i, l: (i, 0))]
        + [pl.BlockSpec((WT, D), lambda i, l, o=o: (w_block(l) + o, 0)) for o in range(sub)]
        + [pl.BlockSpec((1, WT), lambda i, l, o=o: (0, w_block(l) + o)) for o in range(sub)],
        out_specs=[pl.BlockSpec((tm, D), lambda i, l: (i, 0)),
                   pl.BlockSpec((tm, 256), lambda i, l: (i, 0)),
                   pl.BlockSpec((tm, D), lambda i, l: (i, jnp.clip(l - h_first, 0, 3))),
                   pl.BlockSpec((tm, D), lambda i, l: (i, jnp.clip(l - g_first, 0, 1)))],
        out_shape=[jax.ShapeDtypeStruct((t, GRP_N[g] * WT), dtypes[g]) for g in range(4)],
        scratch_shapes=[], sem=("arbitrary", "arbitrary"), nsteps=n_row * n_chunks,
        step_fn=lambda: pl.program_id(0) * n_chunks + pl.program_id(1))


def _inproj_bwd_x(dps, win_t, x, g, resid, *, t, part, prev=None, comm=None):
    n_row = 8 if t >= 4096 else 4
    tm = t // n_row
    first = n_row // 4
    per = first if part == 0 else n_row - first
    row = lambda i: part * first + i

    n_chunks = 4
    sub = 2 * D // WT

    def w_block(l):
        return jnp.where(l == 0, 0, GRP_OFF[2] + sub * (l - 1))

    def body(d0, d1, d2, d3, *rest):
        w_refs, (x_ref, g_ref, r_ref) = rest[:sub], rest[sub:sub + 3]
        dg_prev = rest[sub + 3] if prev is not None else None
        o_ref, dg_ref, acc_ref = rest[-3], rest[-2], rest[-1]
        i, l = pl.program_id(0), pl.program_id(1)

        @pl.when(l == 0)
        def _():
            wq = jnp.concatenate([w[...] for w in w_refs[:GRP_N[0]]], axis=0)
            acc_ref[...] = (jnp.dot(d0[...], wq, preferred_element_type=F32)
                            + jnp.dot(d1[...], w_refs[GRP_N[0]][...], preferred_element_type=F32))

        for pred, d_ref in ((jnp.logical_and(l >= 1, l < 3), d2), (l == 3, d3)):
            @pl.when(pred)
            def _(d_ref=d_ref):
                w = jnp.concatenate([w[...] for w in w_refs], axis=0)
                acc_ref[...] += jnp.dot(d_ref[...], w, preferred_element_type=F32)

        @pl.when(l == n_chunks - 1)
        def _():
            xv = x_ref[...]
            r = lax.rsqrt(jnp.mean(xv * xv, axis=-1, keepdims=True) + EPS)
            xh = xv * r
            du = acc_ref[...]
            dxh = du * g_ref[...]
            o_ref[...] = r_ref[...] + r * (dxh - xh * jnp.mean(dxh * xh, axis=-1, keepdims=True))
            dg = jnp.sum(du * xh, axis=0, keepdims=True)
            if dg_prev is not None:
                dg = dg + jnp.where(i == 0, 1.0, 0.0) * dg_prev[...]
            _accum(dg_ref, dg, i == 0)

    rows = lambda w: pl.BlockSpec((tm, w), lambda i, l: (row(i), 0))
    in_specs = ([rows(D), rows(256),
                 pl.BlockSpec((tm, 2 * D), lambda i, l: (row(i), jnp.clip(l - 1, 0, 1))), rows(2 * D)]
                + [pl.BlockSpec((WT, D), lambda i, l, o=o: (w_block(l) + o, 0)) for o in range(sub)]
                + [rows(D), pl.BlockSpec((1, D), lambda i, l: (0, 0)), rows(D)])
    args = list(dps) + [win_t] * sub + [x, g, resid]
    aliases = None
    if prev is not None:
        in_specs += [pl.BlockSpec((1, D), lambda i, l: (0, 0)), _hbm_spec()]
        args += [prev[1], prev[0]]
        aliases = {len(args) - 1: 0}
    return _hosted_call(
        body, comm, args, name="inproj_bwd_x%d" % part, grid=(per, n_chunks), in_specs=in_specs,
        out_specs=[rows(D), pl.BlockSpec((1, D), lambda i, l: (0, 0))],
        out_shape=[jax.ShapeDtypeStruct((t, D), F32), jax.ShapeDtypeStruct((1, D), F32)],
        scratch_shapes=[pltpu.VMEM((tm, D), F32)], sem=("arbitrary", "arbitrary"), nsteps=per * n_chunks,
        step_fn=lambda: pl.program_id(0) * n_chunks + pl.program_id(1), aliases=aliases)


def _inproj_bwd_w(dps, u, *, t):
    n_tiles = IN_W // WT
    dims = (((0,), (0,)), ((), ()))

    def body(d0, d1, d2, d3, u_ref, o_ref, db_ref):
        i = pl.program_id(0)
        uv = u_ref[...]
        for g, (pred, d_ref) in enumerate(zip(_grp_of(i), (d0, d1, d2, d3))):
            @pl.when(pred)
            def _(d_ref=d_ref):
                dv = d_ref[...]
                o_ref[...] = lax.dot_general(dv, uv, dims, preferred_element_type=F32).astype(BF)
                db_ref[...] = jnp.sum(dv.astype(F32), axis=0, keepdims=True)

    return _pcall(body, name="inproj_bwd_w", grid=(n_tiles,),
                  in_specs=[pl.BlockSpec((t, WT), lambda i, g=g: (0, _grp_idx(i, g))) for g in range(4)]
                  + [pl.BlockSpec((t, D), lambda i: (0, 0))],
                  out_specs=[pl.BlockSpec((WT, D), lambda i: (i, 0)),
                             pl.BlockSpec((1, WT), lambda i: (0, i))],
                  out_shape=[jax.ShapeDtypeStruct((IN_W, D), BF), jax.ShapeDtypeStruct((1, IN_W), F32)],
                  compiler_params=_cp(("arbitrary",)))(*dps, u)


def _row_spec(tm, width, col=0):
    return pl.BlockSpec((tm, width), lambda i: (i, col))


def _vec_spec(width):
    return pl.BlockSpec((1, width), lambda i: (0, 0))


def _rms_fwd(x, g, *, tm, name, comm=None):
    t = x.shape[0]
    tm = min(tm, t)

    def body(x_ref, g_ref, u_ref):
        xv = x_ref[...]
        r = lax.rsqrt(jnp.mean(xv * xv, axis=-1, keepdims=True) + EPS)
        u_ref[...] = (xv * r * g_ref[...]).astype(BF)

    (u,), comm_res = _hosted_call(
        body, comm, (x, g), name=name, grid=(t // tm,), in_specs=[_row_spec(tm, D), _vec_spec(D)],
        out_specs=[_row_spec(tm, D)], out_shape=[jax.ShapeDtypeStruct((t, D), BF)], scratch_shapes=[],
        sem=("arbitrary",), nsteps=t // tm, step_fn=lambda: pl.program_id(0))
    return u if comm is None else (u, comm_res)


def _rms_bwd(du, x, g, resid, *, tm, name):
    t = x.shape[0]
    tm = min(tm, t)

    def body(du_ref, x_ref, g_ref, r_ref, dx_ref, dxb_ref, dg_ref):
        xv = x_ref[...]
        r = lax.rsqrt(jnp.mean(xv * xv, axis=-1, keepdims=True) + EPS)
        xh = xv * r
        duv = du_ref[...]
        dxh = duv * g_ref[...]
        dx = r_ref[...] + r * (dxh - xh * jnp.mean(dxh * xh, axis=-1, keepdims=True))
        dx_ref[...] = dx
        dxb_ref[...] = dx.astype(BF)
        _accum(dg_ref, jnp.sum(duv * xh, axis=0, keepdims=True), pl.program_id(0) == 0)

    return _pcall(body, name=name, grid=(t // tm,),
                  in_specs=[_row_spec(tm, D), _row_spec(tm, D), _vec_spec(D), _row_spec(tm, D)],
                  out_specs=[_row_spec(tm, D), _row_spec(tm, D), _vec_spec(D)],
                  out_shape=[jax.ShapeDtypeStruct((t, D), F32), jax.ShapeDtypeStruct((t, D), BF),
                             jax.ShapeDtypeStruct((1, D), F32)],
                  compiler_params=_cp(("arbitrary",)))(du, x, g, resid)


def _attn_kv_tiles(kprev, kcur):
    kv = jnp.concatenate([kprev, kcur], axis=0).astype(F32)
    lo = lax.broadcasted_iota(jnp.int32, (2 * BLK, 128), 1) < HEAD
    tiles = []
    for part in (kv[:, 0:128], kv[:, 128:256]):
        rolled = pltpu.roll(part, HEAD, 1)
        z = jnp.zeros_like(part)
        tiles.append(((jnp.where(lo, part, z).astype(BF), jnp.where(lo, z, rolled).astype(BF)),
                      (jnp.where(lo, rolled, z).astype(BF), jnp.where(lo, z, part).astype(BF))))
    k_t, v_t = tiles
    return [(jnp.concatenate(k_t[h], axis=0), jnp.concatenate(v_t[h], axis=0)) for h in range(2)]


def _attn_mask(i):
    qi = lax.broadcasted_iota(jnp.int32, (BLK, 2 * BLK), 0)
    kj = lax.broadcasted_iota(jnp.int32, (BLK, 2 * BLK), 1)
    first_key = jnp.where(i == 0, BLK, 0)
    in_prev = jnp.logical_and(jnp.logical_and(kj < BLK, kj > qi), kj >= first_key)
    in_cur = jnp.logical_and(kj >= BLK, kj - BLK <= qi)
    return jnp.logical_or(in_prev, in_cur)


def _attn_probs(s, sink, valid):
    s = jnp.where(valid, s * SCALE, NEG)
    mx = jnp.maximum(jnp.max(s, axis=-1, keepdims=True), sink)
    e = jnp.exp(s - mx)
    es = jnp.exp(sink - mx)
    inv = 1.0 / (jnp.sum(e, axis=-1, keepdims=True) + es)
    return e * inv, es * inv


_KEYS = 2 * BLK


def _pair(ref, j):
    return ref[:, j * 128:(j + 1) * 128]


def _attn_fwd(q, kv, sinks, *, t, comm=None):
    nb = t // BLK

    def body(sink_ref, q_ref, kp_ref, kc_ref, o_ref):
        valid = _attn_mask(pl.program_id(0))
        tiles = _attn_kv_tiles(kp_ref[...], kc_ref[...])
        s = [lax.dot_general(_pair(q_ref, j), tiles[j // 4][0], _NT, preferred_element_type=F32)
             for j in range(N_PAIR)]
        p = []
        for j in range(N_PAIR):
            pe, _ = _attn_probs(s[j][:, 0:_KEYS], sink_ref[0, 2 * j], valid)
            po, _ = _attn_probs(s[j][:, _KEYS:2 * _KEYS], sink_ref[0, 2 * j + 1], valid)
            p.append(jnp.concatenate([pe.astype(BF), po.astype(BF)], axis=1))
        for j in range(N_PAIR):
            o_ref[:, j * 128:(j + 1) * 128] = jnp.dot(p[j], tiles[j // 4][1],
                                                      preferred_element_type=F32).astype(BF)

    return _hosted_call(
        body, comm, (sinks, q, kv, kv), name="attn_fwd", grid=(nb,),
        in_specs=[pl.BlockSpec(memory_space=pltpu.SMEM),
                  pl.BlockSpec((BLK, D), lambda i: (i, 0)),
                  pl.BlockSpec((BLK, 256), lambda i: (jnp.maximum(i - 1, 0), 0)),
                  pl.BlockSpec((BLK, 256), lambda i: (i, 0))],
        out_specs=[pl.BlockSpec((BLK, D), lambda i: (i, 0))],
        out_shape=[jax.ShapeDtypeStruct((t, D), BF)],
        scratch_shapes=[], sem=("arbitrary",), nsteps=nb, step_fn=lambda: pl.program_id(0))


def _attn_bwd(q, kv, sinks, do, *, t, comm=None):
    nb = t // BLK
    last = nb - 1

    def body(sink_ref, q_ref, kp_ref, kc_ref, do_ref, dq_ref, dkv_ref, ds_ref, carry_ref):
        i = pl.program_id(0)

        @pl.when(i == 0)
        def _():
            ds_ref[...] = jnp.zeros_like(ds_ref)
            carry_ref[...] = jnp.zeros_like(carry_ref)

        @pl.when(i < nb)
        def _():
            valid = _attn_mask(i)
            tiles = _attn_kv_tiles(kp_ref[...], kc_ref[...])
            lane1 = lax.broadcasted_iota(jnp.int32, (1, 128), 1)
            dsink = jnp.zeros((1, 128), F32)
            s = [lax.dot_general(_pair(q_ref, j), tiles[j // 4][0], _NT, preferred_element_type=F32)
                 for j in range(N_PAIR)]
            dp = [lax.dot_general(_pair(do_ref, j), tiles[j // 4][1], _NT, preferred_element_type=F32)
                  for j in range(N_PAIR)]
            p_all, ds_all = [], []
            for j in range(N_PAIR):
                halves = []
                for par in range(2):
                    cols = slice(par * _KEYS, (par + 1) * _KEYS)
                    p, ps = _attn_probs(s[j][:, cols], sink_ref[0, 2 * j + par], valid)
                    dpj = dp[j][:, cols]
                    dd = jnp.sum(p * dpj, axis=-1, keepdims=True)
                    dsink = dsink + jnp.where(lane1 == 2 * j + par,
                                              -jnp.sum(ps * dd, axis=0, keepdims=True), 0.0)
                    halves.append((p.astype(BF), (p * (dpj - dd)).astype(BF)))
                p_all.append(jnp.concatenate([halves[0][0], halves[1][0]], axis=1))
                ds_all.append(jnp.concatenate([halves[0][1], halves[1][1]], axis=1))
            for j in range(N_PAIR):
                dq_ref[:, j * 128:(j + 1) * 128] = (
                    jnp.dot(ds_all[j], tiles[j // 4][0], preferred_element_type=F32) * SCALE).astype(BF)
            ds_ref[...] += dsink
            gk, gv = [], []
            for h in range(2):
                grp = range(4 * h, 4 * h + 4)
                q_rows = jnp.concatenate([_pair(q_ref, j) for j in grp], axis=0)
                do_rows = jnp.concatenate([_pair(do_ref, j) for j in grp], axis=0)
                g_k = lax.dot_general(jnp.concatenate([ds_all[j] for j in grp], axis=0), q_rows, _TN,
                                      preferred_element_type=F32)
                g_v = lax.dot_general(jnp.concatenate([p_all[j] for j in grp], axis=0), do_rows, _TN,
                                      preferred_element_type=F32)
                gk.append((g_k[0:_KEYS], g_k[_KEYS:2 * _KEYS]))
                gv.append((g_v[0:_KEYS], g_v[_KEYS:2 * _KEYS]))
            lo = lax.broadcasted_iota(jnp.int32, (2 * BLK, 128), 1) < HEAD
            zero = jnp.zeros((2 * BLK, 128), F32)

            def unpad(g):
                return (jnp.where(lo, g[0][0] + pltpu.roll(g[0][1], HEAD, 1), zero)
                        + jnp.where(lo, zero, pltpu.roll(g[1][0], HEAD, 1) + g[1][1]))

            dk = unpad(gk) * SCALE
            dv = unpad(gv)
            dkv_ref[:, 0:128] = (carry_ref[:, 0:128] + dk[0:BLK]).astype(BF)
            dkv_ref[:, 128:256] = (carry_ref[:, 128:256] + dv[0:BLK]).astype(BF)
            carry_ref[:, 0:128] = dk[BLK:2 * BLK]
            carry_ref[:, 128:256] = dv[BLK:2 * BLK]

        @pl.when(i == nb)
        def _():
            dkv_ref[...] = carry_ref[...].astype(BF)

    return _hosted_call(
        body, comm, (sinks, q, kv, kv, do), name="attn_bwd", grid=(nb + 1,),
        in_specs=[pl.BlockSpec(memory_space=pltpu.SMEM),
                  pl.BlockSpec((BLK, D), lambda i: (jnp.minimum(i, last), 0)),
                  pl.BlockSpec((BLK, 256), lambda i: (jnp.clip(i - 1, 0, last), 0)),
                  pl.BlockSpec((BLK, 256), lambda i: (jnp.minimum(i, last), 0)),
                  pl.BlockSpec((BLK, D), lambda i: (jnp.minimum(i, last), 0))],
        out_specs=[pl.BlockSpec((BLK, D), lambda i: (jnp.minimum(i, last), 0)),
                   pl.BlockSpec((BLK, 256), lambda i: (jnp.maximum(i - 1, 0), 0)),
                   pl.BlockSpec((1, 128), lambda i: (0, 0))],
        out_shape=[jax.ShapeDtypeStruct((t, D), BF), jax.ShapeDtypeStruct((t, 256), BF),
                   jax.ShapeDtypeStruct((1, 128), F32)],
        scratch_shapes=[pltpu.VMEM((BLK, 256), F32)], sem=("arbitrary",), nsteps=nb + 1,
        step_fn=lambda: pl.program_id(0))


def _split3(v):
    h = v.astype(BF)
    r = v - h.astype(F32)
    m = r.astype(BF)
    lo = (r - m.astype(F32)).astype(BF)
    return jnp.concatenate([h, m, lo], axis=1)


def _apply01(mat, v):
    n = v.shape[1]
    r = jnp.dot(mat, _split3(v), preferred_element_type=F32)
    return r[:, 0:n] + r[:, n:2 * n] + r[:, 2 * n:3 * n]


def _hgrn_gates(hq, hf, lb):
    sq = _sig(hq)
    sg = _sig(hf)
    f = lb + (1.0 - lb) * sg
    return hq * sq, (1.0 - lb) * (1.0 - sg), jnp.log(f), sq, sg, f


def _tri(upper):
    r = lax.broadcasted_iota(jnp.int32, (CH, CH), 0)
    c = lax.broadcasted_iota(jnp.int32, (CH, CH), 1)
    return (c >= r) if upper else (c <= r)


def _lb_from_logits(lg_ref):
    return 1.0 / (1.0 + jnp.exp(lg_ref[1:2, :] - lg_ref[0:1, :]))


def _hgrn_fwd(h4, logits, norm_g, *, t, comm=None):
    nc = t // CH
    nt_dims = (((1,), (1,)), ((), ()))
    tn_dims = (((0,), (0,)), ((), ()))

    def body(h_ref, lg_ref, ng_ref, y_ref, o_ref, st_ref, s_scr, b_scr, qa_s, ka_s, qb_s, kb_s, v_s):
        @pl.when(pl.program_id(0) == 0)
        def _():
            s_scr[...] = jnp.zeros_like(s_scr)

        heads = [slice(h * HG_K, (h + 1) * HG_K) for h in range(HG_HEADS)]
        causal = _tri(False)
        lb = _lb_from_logits(lg_ref)
        for c in range(HG_SUB):
            rows = slice(c * CH, (c + 1) * CH)
            q, k, g, _, _, _ = _hgrn_gates(h_ref[rows, 0:D], h_ref[rows, D:2 * D], lb)
            b_scr[...] = _apply01(jnp.where(causal, 1.0, 0.0).astype(BF), g)
            b = b_scr[...]
            b_mid = b_scr[CH // 2 - 1:CH // 2, :]
            b_last = b_scr[CH - 1:CH, :]
            qa_s[...] = (q * jnp.exp(b - b_mid)).astype(BF)
            ka_s[...] = (k * jnp.exp(b_mid - b)).astype(BF)
            qb_s[...] = (q * jnp.exp(b)).astype(BF)
            kb_s[...] = (k * jnp.exp(b_last - b)).astype(BF)
            v_s[...] = h_ref[rows, 2 * D:3 * D].astype(BF)
            dec = jnp.exp(b_last)
            st_ref[c] = s_scr[...].astype(BF)
            a = [jnp.where(causal, lax.dot_general(qa_s[:, sl], ka_s[:, sl], nt_dims, preferred_element_type=F32),
                           0.0).astype(BF) for sl in heads]
            for h, sl in enumerate(heads):
                o_ref[rows, sl] = (jnp.dot(a[h], v_s[:, sl], preferred_element_type=F32)
                                   + lax.dot_general(qb_s[:, sl], s_scr[h].astype(BF), nt_dims,
                                                     preferred_element_type=F32))
            for h, sl in enumerate(heads):
                s_scr[h] = dec[:, sl] * s_scr[h] + lax.dot_general(v_s[:, sl], kb_s[:, sl], tn_dims,
                                                                   preferred_element_type=F32)
            for h, sl in enumerate(heads):
                o = o_ref[rows, sl]
                on = o * lax.rsqrt(jnp.mean(o * o, axis=-1, keepdims=True) + EPS)
                gate = _sig(h_ref[rows, 3 * D + h * HG_K:3 * D + (h + 1) * HG_K])
                y_ref[rows, sl] = (on * ng_ref[:, sl] * gate).astype(BF)

    half = lambda: pltpu.VMEM((CH, D), BF)
    blk = HG_SUB * CH
    return _hosted_call(
        body, comm, (h4, logits, norm_g), name="hgrn_fwd", grid=(nc // HG_SUB,),
        in_specs=[pl.BlockSpec((blk, 4 * D), lambda n: (n, 0)),
                  pl.BlockSpec((2, D), lambda n: (0, 0)),
                  pl.BlockSpec((1, D), lambda n: (0, 0))],
        out_specs=[pl.BlockSpec((blk, D), lambda n: (n, 0)),
                   pl.BlockSpec((blk, D), lambda n: (n, 0)),
                   pl.BlockSpec((HG_SUB, HG_HEADS, HG_K, HG_K), lambda n: (n, 0, 0, 0))],
        out_shape=[jax.ShapeDtypeStruct((t, D), BF), jax.ShapeDtypeStruct((t, D), F32),
                   jax.ShapeDtypeStruct((nc, HG_HEADS, HG_K, HG_K), BF)],
        scratch_shapes=[pltpu.VMEM((HG_HEADS, HG_K, HG_K), F32), pltpu.VMEM((CH, D), F32),
                        half(), half(), half(), half(), half()],
        sem=("arbitrary",), nsteps=nc // HG_SUB, step_fn=lambda: pl.program_id(0))


def _hgrn_bwd(h4, logits, norm_g, o_pre, states, dy, *, t, comm=None):
    nc = t // CH
    nt_dims = (((1,), (1,)), ((), ()))
    tn_dims = (((0,), (0,)), ((), ()))

    def body(h_ref, lg_ref, ng_ref, o_ref, st_ref, dy_ref, dh_ref, dlg_ref, dng_ref, ds_scr, dlb_scr,
             b_scr, tail_s, e_qa, e_ka, e_qb, e_kb, q_s, k_s, dqa_s, dka_s, dqb_s, dkb_s,
             qa_s, ka_s, qb_s, kb_s, v_s, do_s):
        n = pl.program_id(0)

        @pl.when(n == 0)
        def _():
            ds_scr[...] = jnp.zeros_like(ds_scr)
            dlb_scr[...] = jnp.zeros_like(dlb_scr)
            dng_ref[...] = jnp.zeros_like(dng_ref)

        heads = [slice(h * HG_K, (h + 1) * HG_K) for h in range(HG_HEADS)]
        lb = _lb_from_logits(lg_ref)
        causal = _tri(False)

        def chunk(c):
            rows = slice(c * CH, (c + 1) * CH)
            hq = h_ref[rows, 0:D]
            q, k, g, sq, sg, f = _hgrn_gates(hq, h_ref[rows, D:2 * D], lb)
            b_scr[...] = _apply01(jnp.where(causal, 1.0, 0.0).astype(BF), g)
            b = b_scr[...]
            b_mid = b_scr[CH // 2 - 1:CH // 2, :]
            b_last = b_scr[CH - 1:CH, :]
            q_s[...] = q
            k_s[...] = k
            for e_ref, s_ref, base, expo in ((e_qa, qa_s, q, b - b_mid), (e_ka, ka_s, k, b_mid - b),
                                             (e_qb, qb_s, q, b), (e_kb, kb_s, k, b_last - b)):
                e = jnp.exp(expo)
                e_ref[...] = e
                s_ref[...] = (base * e).astype(BF)
            v_s[...] = h_ref[rows, 2 * D:3 * D].astype(BF)
            dec = jnp.exp(b_last)
            for h, sl in enumerate(heads):
                gcol = slice(3 * D + h * HG_K, 3 * D + (h + 1) * HG_K)
                ngh = ng_ref[:, sl]
                sgate = _sig(h_ref[rows, gcol])
                o = o_ref[rows, sl]
                r = lax.rsqrt(jnp.mean(o * o, axis=-1, keepdims=True) + EPS)
                on = o * r
                dyh = dy_ref[rows, sl]
                dh_ref[rows, gcol] = (dyh * on * ngh * sgate * (1.0 - sgate)).astype(BF)
                dng_ref[:, sl] += jnp.sum(dyh * on * sgate, axis=0, keepdims=True)
                don = dyh * ngh * sgate
                do_s[:, sl] = (r * (don - on * jnp.mean(don * on, axis=-1, keepdims=True))).astype(BF)
            a = [jnp.where(causal, lax.dot_general(qa_s[:, sl], ka_s[:, sl], nt_dims, preferred_element_type=F32),
                           0.0).astype(BF) for sl in heads]
            da = [jnp.where(causal, lax.dot_general(do_s[:, sl], v_s[:, sl], nt_dims, preferred_element_type=F32),
                            0.0).astype(BF) for sl in heads]
            for h, sl in enumerate(heads):
                dh_ref[rows, 2 * D + h * HG_K:2 * D + (h + 1) * HG_K] = (
                    lax.dot_general(a[h], do_s[:, sl], tn_dims, preferred_element_type=F32)
                    + lax.dot_general(kb_s[:, sl], ds_scr[h].astype(BF), nt_dims, preferred_element_type=F32)
                ).astype(BF)
            for h, sl in enumerate(heads):
                dqa_s[:, sl] = jnp.dot(da[h], ka_s[:, sl], preferred_element_type=F32)
            for h, sl in enumerate(heads):
                dka_s[:, sl] = lax.dot_general(da[h], qa_s[:, sl], tn_dims, preferred_element_type=F32)
            for h, sl in enumerate(heads):
                dqb_s[:, sl] = jnp.dot(do_s[:, sl], st_ref[c, h], preferred_element_type=F32)
            for h, sl in enumerate(heads):
                dkb_s[:, sl] = jnp.dot(v_s[:, sl], ds_scr[h].astype(BF), preferred_element_type=F32)
            for h, sl in enumerate(heads):
                tail_s[:, sl] = jnp.sum(dec[:, sl] * st_ref[c, h].astype(F32) * ds_scr[h], axis=0, keepdims=True)
            for h, sl in enumerate(heads):
                ds_scr[h] = (lax.dot_general(do_s[:, sl], qb_s[:, sl], tn_dims, preferred_element_type=F32)
                             + dec[:, sl] * ds_scr[h])
            qv, kv = q_s[...], k_s[...]
            dqa, dka, dqb, dkb = dqa_s[...], dka_s[...], dqb_s[...], dkb_s[...]
            eqa, eka, eqb, ekb = e_qa[...], e_ka[...], e_qb[...], e_kb[...]
            dkb_kb = dkb * (kv * ekb)
            db_last = jnp.sum(dkb_kb, axis=0, keepdims=True) + tail_s[...]
            last_row = lax.broadcasted_iota(jnp.int32, (CH, D), 0) == CH - 1
            db = (dqa * (qv * eqa) - dka * (kv * eka) + dqb * (qv * eqb) - dkb_kb
                  + jnp.where(last_row, db_last, 0.0))
            dg = _apply01(jnp.where(_tri(True), 1.0, 0.0).astype(BF), db)
            dq = dqa * eqa + dqb * eqb
            dk = dka * eka + dkb * ekb
            dh_ref[rows, 0:D] = (dq * sq * (1.0 + hq * (1.0 - sq))).astype(BF)
            dfk = dg / f - dk
            dh_ref[rows, D:2 * D] = ((1.0 - lb) * dfk * sg * (1.0 - sg)).astype(BF)
            dlb_scr[...] += jnp.sum((1.0 - sg) * dfk, axis=0, keepdims=True)

        for c in reversed(range(HG_SUB)):
            chunk(c)

        @pl.when(n == nc // HG_SUB - 1)
        def _():
            dl0 = dlb_scr[...] * lb * (1.0 - lb)
            dlg_ref[0:1, :] = dl0
            dlg_ref[1:2, :] = -dl0

    steps = nc // HG_SUB
    blk = HG_SUB * CH
    rev = lambda n: (steps - 1 - n, 0)
    return _hosted_call(
        body, comm, (h4, logits, norm_g, o_pre, states, dy), name="hgrn_bwd", grid=(steps,),
        in_specs=[pl.BlockSpec((blk, 4 * D), rev),
                  pl.BlockSpec((2, D), lambda n: (0, 0)),
                  pl.BlockSpec((1, D), lambda n: (0, 0)),
                  pl.BlockSpec((blk, D), rev),
                  pl.BlockSpec((HG_SUB, HG_HEADS, HG_K, HG_K), lambda n: (steps - 1 - n, 0, 0, 0)),
                  pl.BlockSpec((blk, D), rev)],
        out_specs=[pl.BlockSpec((blk, 4 * D), rev),
                   pl.BlockSpec((2, D), lambda n: (0, 0)),
                   pl.BlockSpec((1, D), lambda n: (0, 0))],
        out_shape=[jax.ShapeDtypeStruct((t, 4 * D), BF), jax.ShapeDtypeStruct((2, D), F32),
                   jax.ShapeDtypeStruct((1, D), F32)],
        scratch_shapes=([pltpu.VMEM((HG_HEADS, HG_K, HG_K), F32), pltpu.VMEM((1, D), F32),
                         pltpu.VMEM((CH, D), F32), pltpu.VMEM((1, D), F32)]
                        + [pltpu.VMEM((CH, D), F32)] * 10 + [pltpu.VMEM((CH, D), BF)] * 6),
        sem=("arbitrary",), nsteps=steps, step_fn=lambda: pl.program_id(0))


def _place():
    x, y, c = lax.axis_index("x"), lax.axis_index("y"), lax.axis_index("c")
    return x, y, c, [(1 - x, y), (x, 1 - y), (1 - x, 1 - y)]


def _gather_comm(shards, mids):
    n, pieces = len(shards), len(mids)
    r = [s.shape[0] for s in shards]
    tile = 16
    cut = [[(rw // tile * p // pieces) * tile for p in range(pieces + 1)] for rw in r]
    size = [[cut[w][p + 1] - cut[w][p] for p in range(pieces)] for w in range(n)]

    def tools(ins, outs, sems):
        send_sems, recv_sems, local_sems = sems
        x, y, c, chips = _place()
        me, sib = (x, y, c), (x, y, 1 - c)

        def rows(w, p, dev):
            return outs[w].at[pl.ds((4 * dev[0] + 2 * dev[1] + dev[2]) * r[w] + cut[w][p], size[w][p]), :]

        def copy(kind, w, p, block, to, own=False):
            src = ins[w].at[pl.ds(cut[w][p], size[w][p]), :] if own else rows(w, p, block)
            return pltpu.make_async_remote_copy(
                src_ref=src, dst_ref=rows(w, p, block), send_sem=send_sems.at[p, kind],
                recv_sem=recv_sems.at[p, kind], device_id=to, device_id_type=MESH)

        def all_of(kind, p):
            whole = outs[0].at[pl.ds(0, sum(size[w][p] for w in range(n))), :]
            return pltpu.make_async_remote_copy(
                src_ref=whole, dst_ref=whole, send_sem=send_sems.at[p, kind], recv_sem=recv_sems.at[p, kind],
                device_id=me, device_id_type=MESH)

        mine = [pltpu.make_async_copy(ins[w], outs[w].at[pl.ds((4 * x + 2 * y + c) * r[w], r[w]), :],
                                      local_sems.at[w]) for w in range(n)]
        return c, chips, me, sib, copy, all_of, mine

    def start(ins, outs, sems):
        c, chips, me, sib, copy, _, mine = tools(ins, outs, sems)
        for cp in mine:
            cp.start()
        for p in range(pieces):
            for w in range(n):
                copy(0, w, p, me, sib, own=True).start()
                for j, chip in enumerate(chips):
                    copy(1 + j, w, p, me, (*chip, c), own=True).start()

    def pass_on(p):
        def phase(ins, outs, sems):
            c, chips, _, sib, copy, all_of, _ = tools(ins, outs, sems)
            for j, chip in enumerate(chips):
                all_of(1 + j, p).wait_recv()
                for w in range(n):
                    copy(4 + j, w, p, (*chip, c), sib).start()
        return phase

    def finish(ins, outs, sems):
        _, _, _, _, _, all_of, mine = tools(ins, outs, sems)
        for p in range(pieces):
            all_of(0, p).wait_recv()
            for j in range(3):
                all_of(4 + j, p).wait_recv()
            for kind in range(7):
                all_of(kind, p).wait_send()
        for cp in mine:
            cp.wait()

    return _Comm(shards, [jax.ShapeDtypeStruct((N_DEV * rw, D), BF) for rw in r],
                 [pltpu.SemaphoreType.DMA((pieces, 7)), pltpu.SemaphoreType.DMA((pieces, 7)),
                  pltpu.SemaphoreType.DMA((n,))],
                 [(0.0, start)] + [(f, pass_on(p)) for p, f in enumerate(mids)] + [(1.0, finish)])


def _pair_comm(grads):
    n = len(grads)
    r = [g.shape[0] // N_DEV for g in grads]

    def start(ins, outs, sems):
        send_sems, recv_sems = sems
        x, y, c, _ = _place()
        for w in range(n):
            for a in range(N_CHIP):
                pltpu.make_async_remote_copy(
                    src_ref=ins[w].at[pl.ds((2 * a + 1 - c) * r[w], r[w]), :], dst_ref=outs[w].at[a],
                    send_sem=send_sems.at[w], recv_sem=recv_sems.at[w],
                    device_id=(x, y, 1 - c), device_id_type=MESH).start()

    def finish(ins, outs, sems):
        send_sems, recv_sems = sems
        x, y, c, _ = _place()
        for w in range(n):
            pltpu.make_async_remote_copy(
                src_ref=outs[w], dst_ref=outs[w], send_sem=send_sems.at[w], recv_sem=recv_sems.at[w],
                device_id=(x, y, c), device_id_type=MESH).wait()

    return _Comm(grads, [jax.ShapeDtypeStruct((N_CHIP, rw, D), BF) for rw in r],
                 [pltpu.SemaphoreType.DMA((n,)), pltpu.SemaphoreType.DMA((n,))],
                 [(0.0, start), (1.0, finish)])


def _pair_add(grad, got, core, *, name):
    r = got.shape[1]

    def body(c_ref, g_ref, got_ref, o_ref):
        o_ref[0] = (g_ref[...].astype(F32) + got_ref[0].astype(F32)).astype(BF)

    grid_spec = pltpu.PrefetchScalarGridSpec(
        num_scalar_prefetch=1, grid=(N_CHIP,),
        in_specs=[pl.BlockSpec((r, D), lambda a, c_ref: (2 * a + c_ref[0], 0)),
                  pl.BlockSpec((1, r, D), lambda a, c_ref: (a, 0, 0))],
        out_specs=pl.BlockSpec((1, r, D), lambda a, c_ref: (a, 0, 0)))
    return _pcall(body, name=name, grid_spec=grid_spec,
                  out_shape=jax.ShapeDtypeStruct((N_CHIP, r, D), BF),
                  compiler_params=_cp(("parallel",)))(core, grad, got)


def _chip_comm(pair_sums):
    n = len(pair_sums)
    r = [p.shape[1] for p in pair_sums]
    off = [sum(r[:w]) for w in range(n)]

    def tools(ins, outs, sems):
        send_sems, recv_sems, local_sems = sems
        x, y, c, chips = _place()
        my_chip = 2 * x + y

        def slot(w):
            return outs[0].at[my_chip, pl.ds(off[w], r[w]), :]

        own = [pltpu.make_async_copy(ins[w].at[my_chip], slot(w), local_sems.at[w]) for w in range(n)]
        return x, y, c, chips, my_chip, slot, own, send_sems, recv_sems

    def start(ins, outs, sems):
        x, y, c, chips, my_chip, slot, own, send_sems, recv_sems = tools(ins, outs, sems)
        for cp in own:
            cp.start()
        for j, chip in enumerate(chips):
            for w in range(n):
                pltpu.make_async_remote_copy(
                    src_ref=ins[w].at[2 * chip[0] + chip[1]], dst_ref=slot(w), send_sem=send_sems.at[j],
                    recv_sem=recv_sems.at[j], device_id=(*chip, c), device_id_type=MESH).start()

    def finish(ins, outs, sems):
        x, y, c, chips, my_chip, slot, own, send_sems, recv_sems = tools(ins, outs, sems)
        whole = outs[0].at[my_chip]
        for j in range(3):
            pltpu.make_async_remote_copy(
                src_ref=whole, dst_ref=whole, send_sem=send_sems.at[j], recv_sem=recv_sems.at[j],
                device_id=(x, y, c), device_id_type=MESH).wait()
        for cp in own:
            cp.wait()

    return _Comm(pair_sums, [jax.ShapeDtypeStruct((N_CHIP, sum(r), D), BF)],
                 [pltpu.SemaphoreType.DMA((3,)), pltpu.SemaphoreType.DMA((3,)), pltpu.SemaphoreType.DMA((n,))],
                 [(0.0, start), (1.0, finish)])


def _adam_math(w, g, m, v):
    m = ADAM_B1 * m + (1.0 - ADAM_B1) * g
    v = ADAM_B2 * v + (1.0 - ADAM_B2) * (g * g)
    m_hat = m / (1.0 - ADAM_B1 ** ADAM_STEP)
    v_hat = v / (1.0 - ADAM_B2 ** ADAM_STEP)
    delta = -ADAM_LR * (m_hat / (jnp.sqrt(v_hat) + ADAM_EPS) + ADAM_WD * w)
    return delta, m, v


SMALL = (("norm_mix_g", (1, D), 0), ("hgrn_norm_g", (1, D), 1), ("norm_ffn_g", (1, D), 2),
         ("norm_final_g", (1, D), 3), ("hgrn_lb_logits", (2, D), 4), ("attn_sinks", (1, 16), 6),
         ("b_in", (1, IN_W), 8))
LOSS_ROW = 7


def _small_allreduce_adam(grads, loss_row, params):
    n = len(SMALL)

    def rows_of(ref, shape, row):
        r, w = shape
        if w <= D:
            return ref[row:row + r, 0:w]
        pieces = [ref[row + k:row + k + 1, :] for k in range(-(-w // D))]
        return jnp.concatenate(pieces, axis=1)[:, 0:w]

    def body(*refs):
        g_refs, loss_ref = refs[:n], refs[n]
        wmv = refs[n + 1:4 * n + 1]
        loss_out = refs[4 * n + 1]
        outs = refs[4 * n + 2:8 * n + 2]
        mine, total, gath, send_sems, recv_sems = refs[8 * n + 2:]
        x, y, c, _ = _place()
        me = 4 * x + 2 * y + c
        mine[...] = jnp.zeros_like(mine)
        for g_ref, (_, (r, w), row) in zip(g_refs, SMALL):
            for k in range(-(-w // D)):
                wk = min(D, w - k * D)
                mine[row + k:row + k + r, 0:wk] = g_ref[:, k * D:k * D + wk]
        mine[LOSS_ROW:LOSS_ROW + 1, 0:128] = loss_ref[...]
        gath[me] = mine[...]
        cps = []
        for d in range(1, N_DEV):
            peer = (x ^ (d >> 2), y ^ ((d >> 1) & 1), c ^ (d & 1))
            cps.append(pltpu.make_async_remote_copy(
                src_ref=mine, dst_ref=gath.at[me], send_sem=send_sems.at[d - 1],
                recv_sem=recv_sems.at[d - 1], device_id=peer, device_id_type=MESH))
        for cp in cps:
            cp.start()
        for cp in cps:
            cp.wait()
        g = gath[0]
        for k in range(1, N_DEV):
            g = g + gath[k]
        total[...] = g
        loss_out[...] = total[LOSS_ROW:LOSS_ROW + 1, 0:128]
        for i, (_, shape, row) in enumerate(SMALL):
            gi = rows_of(total, shape, row)
            w_ref, m_ref, v_ref = wmv[3 * i:3 * i + 3]
            o = outs[4 * i:4 * i + 4]
            o[0][...] = gi
            o[1][...], o[2][...], o[3][...] = _adam_math(w_ref[...], gi, m_ref[...], v_ref[...])

    vm = pl.BlockSpec(memory_space=pltpu.VMEM)
    ins = [grads[name] for name, _, _ in SMALL] + [loss_row]
    for name, _, _ in SMALL:
        ins += list(params[name])
    out_shape = [jax.ShapeDtypeStruct((1, 128), F32)]
    for _, shape, _ in SMALL:
        out_shape += [jax.ShapeDtypeStruct(shape, F32)] * 4
    res = _pcall(body, name="small_allreduce_adam", in_specs=[vm] * len(ins), out_specs=[vm] * len(out_shape),
                 out_shape=out_shape,
                 scratch_shapes=[pltpu.VMEM((SMALL_ROWS, D), F32), pltpu.VMEM((SMALL_ROWS, D), F32),
                                 pltpu.VMEM((N_DEV, SMALL_ROWS, D), F32),
                                 pltpu.SemaphoreType.DMA((N_DEV - 1,)), pltpu.SemaphoreType.DMA((N_DEV - 1,))],
                 compiler_params=pltpu.CompilerParams(has_side_effects=True))(*ins)
    return res[0], {name: res[1 + 4 * i:5 + 4 * i] for i, (name, _, _) in enumerate(SMALL)}


def _adam(w, parts, index, m, v, *, name):
    rows = w.shape[0]
    tr = rows if rows <= 512 else rows // 2
    steps = rows // tr

    def body(w_ref, p_ref, m_ref, v_ref, g_ref, d_ref, mo_ref, vo_ref):
        g = p_ref[0].astype(F32)
        for a in range(1, N_CHIP):
            g = g + p_ref[a].astype(F32)
        g_ref[...] = g
        d_ref[...], mo_ref[...], vo_ref[...] = _adam_math(w_ref[...], g, m_ref[...], v_ref[...])

    spec = pl.BlockSpec((tr, D), lambda i: (i, 0))
    return _pcall(body, name=name, grid=(steps,),
                  in_specs=[spec, pl.BlockSpec((N_CHIP, tr, D), lambda i: (0, index * steps + i, 0)), spec, spec],
                  out_specs=[spec] * 4, out_shape=[jax.ShapeDtypeStruct((rows, D), F32)] * 4,
                  compiler_params=_cp(("parallel",)))(w, parts, m, v)


def _step(x, tgt, shards, norm_mix_g, b_in, sinks, logits, hgrn_norm_g, norm_ffn_g, norm_final_g):
    t = x.shape[0]
    core = lax.axis_index("c").astype(jnp.int32).reshape(1)

    u1, (win_t,) = _rms_fwd(x, norm_mix_g, tm=512, name="rms_mix", comm=_gather_comm(shards[0:1], (0.2, 0.4, 0.6, 0.8)))
    (q, kv, h4, gates), (wg_t, wba, wbh, wout) = _inproj_fwd(
        u1, win_t, b_in, t=t, comm=_gather_comm([shards[1]] + shards[4:7], (0.25, 0.47, 0.7, 0.92)))
    (y_attn,), _ = _attn_fwd(q, kv, sinks, t=t)
    (y_hgrn, o_pre, states), (wu_t, wd) = _hgrn_fwd(h4, logits, hgrn_norm_g, t=t,
                                                    comm=_gather_comm(shards[2:4], (0.27, 0.52, 0.77, 0.97)))
    col = lambda j: j
    first, second = (lambda j: 0), (lambda j: 1)
    gate_tiles = [(gates, D, first), (gates, D, second)]

    def merge(prods, ex):
        (ya_, yb_), (ga, gb) = prods, ex
        sa, sb = _sig(ga), _sig(gb)
        return sa, sb, ya_ * sa * (1.0 - sa), yb_ * sb * (1.0 - sb), sa * ya_ + sb * yb_

    sig_a, sig_b, dgate_a, dgate_b, merged = _fmm(
        [y_attn, y_hgrn], [(0, wba, False), (1, wbh, False)], gate_tiles, merge,
        [(BF, D, D, first)] * 5, m=t, n=D, tm=512, tn=D, name="branch_merge")
    def resid_norm(prods, ex):
        (p,), (xv, gv) = prods, ex
        hv = xv + p
        return hv, hv * lax.rsqrt(jnp.mean(hv * hv, axis=-1, keepdims=True) + EPS) * gv

    h1, u2 = _fmm([merged], [(0, wout, False)], [(x, D, first)], resid_norm, [(F32, D, D, first), (BF, D, D, first)],
                  m=t, n=D, tm=1024, tn=D, name="out_proj", vecs=[norm_ffn_g])

    def swiglu(prods, ex):
        g_, u_ = prods
        s = _sig(g_)
        silu = g_ * s
        return u_ * s * (1.0 + g_ * (1.0 - s)), silu, silu * u_

    dz_dgate, dz_dup, z = _fmm([u2], [(0, wg_t, True), (0, wu_t, True)], [], swiglu,
                               [(BF, FFN, FFN // 2, col)] * 3, m=t, n=FFN, tm=1024, tn=FFN // 2,
                               name="ffn_gate_up")
    def loss_head(prods, ex):
        (p,), (hv, tv, gv) = prods, ex
        hv = hv + p
        r = lax.rsqrt(jnp.mean(hv * hv, axis=-1, keepdims=True) + EPS)
        xh = hv * r
        err = xh * gv - tv
        lp = jnp.sum(jnp.sum(err * err, axis=1, keepdims=True), axis=0, keepdims=True) * (0.5 / D)
        dy = err * (1.0 / D)
        dxh = dy * gv
        dh = r * (dxh - xh * jnp.mean(dxh * xh, axis=-1, keepdims=True))
        return dh, dh, jnp.sum(dy * xh, axis=0, keepdims=True), jnp.broadcast_to(lp, (1, 128))

    dh2, dh2_b, d_norm_final, loss_row = _fmm(
        [z], [(0, wd, False)], [(h1, D, first), (tgt, D, first)], loss_head, [(F32, D, D, first), (BF, D, D, first)],
        m=t, n=D, tm=512, tn=D, name="ffn_down_loss", vecs=[norm_final_g], sums=[D, 128])

    def swiglu_bwd(prods, ex):
        (dz,), (da_, db_) = prods, ex
        return dz * da_.astype(F32), dz * db_.astype(F32)

    ffn_tiles = [(dz_dgate, FFN // 2, col), (dz_dup, FFN // 2, col)]
    dgt, dup = _fmm([dh2_b], [(0, wd, True)], ffn_tiles, swiglu_bwd, [(BF, FFN, FFN // 2, col)] * 2,
                    m=t, n=FFN, tm=1024, tn=FFN // 2, name="d_gate_up")
    d_wd = _wgrad(z, dh2_b, name="d_w_down")
    (du2,) = _fmm([dgt, dup], [(0, wg_t, False), (1, wu_t, False)], [], lambda prods, ex: (prods[0] + prods[1],),
                  [(F32, D, 512, col)], m=t, n=D, tm=1024, tn=512, name="d_u2")
    d_wg = _wgrad(dgt, u2, name="d_w_gate")
    d_wu = _wgrad(dup, u2, name="d_w_up")
    dh1, dh1_b, d_norm_ffn = _rms_bwd(du2, h1, norm_ffn_g, dh2, tm=512, name="rms_ffn_bwd")
    d_wout = _wgrad(merged, dh1_b, name="d_w_out")

    def merge_bwd(prods, ex):
        (dm,), (sa, sb, ca, cb, wa, wb) = prods, ex
        dgate = jnp.concatenate([dm * ca.astype(F32), dm * cb.astype(F32)], axis=1)
        dya_ = (dm * sa.astype(F32)).astype(BF)
        dyb_ = (dm * sb.astype(F32)).astype(BF)
        return (dya_, dyb_, dgate, lax.dot_general(dya_, wa, _NT, preferred_element_type=F32),
                lax.dot_general(dyb_, wb, _NT, preferred_element_type=F32))

    ffn_grads = (d_wg, d_wu, d_wd)
    (dya, dyb, dgates, dy_attn, dy_hgrn), got = _fmm(
        [dh1_b], [(0, wout, True)], [(a, D, first) for a in (sig_a, sig_b, dgate_a, dgate_b)], merge_bwd,
        [(BF, D, D, first), (BF, D, D, first), (BF, 2 * D, 2 * D, first), (BF, D, D, first), (F32, D, D, first)],
        m=t, n=D, tm=512, tn=D, name="d_merge", consts=[wba, wbh], comm=_pair_comm(ffn_grads))
    pair_ffn = [_pair_add(g, r, core, name="pair_add_ffn%d" % i) for i, (g, r) in enumerate(zip(ffn_grads, got))]
    d_wba = _wgrad(y_attn, dya, name="d_w_ba")
    d_wbh = _wgrad(y_hgrn, dyb, name="d_w_bh")
    sq_grads = (d_wba, d_wbh, d_wout)
    (dq, dkv, d_sinks), (parts_ffn, *got) = _attn_bwd(
        q, kv, sinks, dy_attn, t=t, comm=_both(_chip_comm(pair_ffn), _pair_comm(sq_grads)))
    pair_sq = [_pair_add(g, r, core, name="pair_add_sq%d" % i) for i, (g, r) in enumerate(zip(sq_grads, got))]
    (dh4, d_logits, d_hgrn_norm), (parts_sq,) = _hgrn_bwd(h4, logits, hgrn_norm_g, o_pre, states, dy_hgrn,
                                                           t=t, comm=_chip_comm(pair_sq))
    dps = (dq, dkv, dh4, dgates)
    d_win_t, d_b_in = _inproj_bwd_w(dps, u1, t=t)
    half0, got_in = _inproj_bwd_x(dps, win_t, x, norm_mix_g, dh1, t=t, part=0, comm=_pair_comm([d_win_t]))
    pair_in = _pair_add(d_win_t, got_in[0], core, name="pair_add_w_in")
    (grad_x, d_norm_mix), (parts_in,) = _inproj_bwd_x(dps, win_t, x, norm_mix_g, dh1, t=t, part=1, prev=half0,
                                                      comm=_chip_comm([pair_in]))

    small_grads = (d_norm_mix, d_b_in, d_sinks, d_logits, d_hgrn_norm, d_norm_ffn, d_norm_final)
    return loss_row, grad_x, (parts_in, parts_ffn, parts_sq), small_grads


def kernel(x, norm_mix_g, w_in, b_in, attn_sinks, hgrn_lb_logits, hgrn_norm_g, w_branch_attn, w_branch_hgrn, w_out, norm_ffn_g, w_ffn_gate, w_ffn_up, w_ffn_down, norm_final_g, loss_target, m_norm_mix_g, m_w_in, m_b_in, m_attn_sinks, m_hgrn_lb_logits, m_hgrn_norm_g, m_w_branch_attn, m_w_branch_hgrn, m_w_out, m_norm_ffn_g, m_w_ffn_gate, m_w_ffn_up, m_w_ffn_down, m_norm_final_g, v_norm_mix_g, v_w_in, v_b_in, v_attn_sinks, v_hgrn_lb_logits, v_hgrn_norm_g, v_w_branch_attn, v_w_branch_hgrn, v_w_out, v_norm_ffn_g, v_w_ffn_gate, v_w_ffn_up, v_w_ffn_down, v_norm_final_g):
    shards = [w_in[0].T.astype(BF), w_ffn_gate[0].T.astype(BF), w_ffn_up[0].T.astype(BF),
              w_ffn_down[0].astype(BF), w_branch_attn[0].astype(BF), w_branch_hgrn[0].astype(BF),
              w_out[0].astype(BF)]
    loss_row, grad_x, grad_parts, small_grads = _step(
        x[0], loss_target[0], shards, norm_mix_g, b_in, attn_sinks, hgrn_lb_logits, hgrn_norm_g,
        norm_ffn_g, norm_final_g.reshape(1, D))

    d_norm_mix, d_b_in, d_sinks, d_logits, d_hgrn_norm, d_norm_ffn, d_norm_final = small_grads
    row = lambda a: a.reshape(1, D)
    loss_out, small = _small_allreduce_adam(
        dict(norm_mix_g=d_norm_mix, hgrn_norm_g=d_hgrn_norm, norm_ffn_g=d_norm_ffn, norm_final_g=d_norm_final,
             hgrn_lb_logits=d_logits, attn_sinks=d_sinks, b_in=d_b_in),
        loss_row,
        dict(norm_mix_g=(norm_mix_g, m_norm_mix_g, v_norm_mix_g), hgrn_norm_g=(hgrn_norm_g, m_hgrn_norm_g, v_hgrn_norm_g),
             norm_ffn_g=(norm_ffn_g, m_norm_ffn_g, v_norm_ffn_g),
             norm_final_g=(row(norm_final_g), row(m_norm_final_g), row(v_norm_final_g)),
             hgrn_lb_logits=(hgrn_lb_logits, m_hgrn_lb_logits, v_hgrn_lb_logits),
             attn_sinks=(attn_sinks, m_attn_sinks, v_attn_sinks), b_in=(b_in, m_b_in, v_b_in)))
    small["norm_final_g"] = [a.reshape(D) for a in small["norm_final_g"]]
    loss = loss_out[0, 0]

    names = ["w_in", "w_ffn_gate", "w_ffn_up", "w_ffn_down", "w_branch_attn", "w_branch_hgrn", "w_out"]
    w_full = dict(w_in=(w_in, m_w_in, v_w_in), w_ffn_gate=(w_ffn_gate, m_w_ffn_gate, v_w_ffn_gate),
                  w_ffn_up=(w_ffn_up, m_w_ffn_up, v_w_ffn_up), w_ffn_down=(w_ffn_down, m_w_ffn_down, v_w_ffn_down),
                  w_branch_attn=(w_branch_attn, m_w_branch_attn, v_w_branch_attn),
                  w_branch_hgrn=(w_branch_hgrn, m_w_branch_hgrn, v_w_branch_hgrn),
                  w_out=(w_out, m_w_out, v_w_out))
    parts_in, parts_ffn, parts_sq = grad_parts
    where = [(parts_in, 0), (parts_ffn, 0), (parts_ffn, 1), (parts_ffn, 2), (parts_sq, 0), (parts_sq, 1), (parts_sq, 2)]
    big = {}
    for i, name in enumerate(names):
        view = (lambda a: a[0].T) if i < 3 else (lambda a: a[0])
        back = (lambda a: a.T[None]) if i < 3 else (lambda a: a[None])
        wv, mv, vv = w_full[name]
        res = _adam(view(wv), where[i][0], where[i][1], view(mv), view(vv), name="adam_" + name)
        big[name] = [back(a) for a in res]

    order = ["norm_mix_g", "w_in", "b_in", "attn_sinks", "hgrn_lb_logits", "hgrn_norm_g", "w_branch_attn",
             "w_branch_hgrn", "w_out", "norm_ffn_g", "w_ffn_gate", "w_ffn_up", "w_ffn_down", "norm_final_g"]
    outs = [loss, grad_x[None]]
    for kind in range(4):
        for name in order:
            outs.append(big[name][kind] if name in big else small[name][kind])
    return tuple(outs)
```

```python
import math

import jax
import jax.numpy as jnp
from jax import lax
from jax.experimental import pallas as pl
from jax.experimental.pallas import tpu as pltpu

F32 = jnp.float32
BF = jnp.bfloat16
MESH = pl.DeviceIdType.MESH

D = 1024
HEAD = 64
N_PAIR = 8
BLK = 128
CH = 64
HG_SUB = 2
HG_HEADS = 8
HG_K = 128
FFN = 2816
IN_W = 7424
N_DEV = 8
N_CHIP = 4
EPS = 1e-6
NEG = -1e30
SCALE = 1.0 / math.sqrt(HEAD)
VMEM_LIMIT = 56 * 1024 * 1024
WT = 256

ADAM_LR, ADAM_B1, ADAM_B2, ADAM_EPS, ADAM_WD, ADAM_STEP = 0.001, 0.9, 0.999, 1e-08, 0.01, 10

SLAB_R = (IN_W // N_DEV, FFN // N_DEV, FFN // N_DEV, FFN // N_DEV, D // N_DEV, D // N_DEV, D // N_DEV)
SLAB_ROWS = sum(SLAB_R)
SLAB_OFF = tuple(sum(SLAB_R[:i]) for i in range(len(SLAB_R)))
N_W = len(SLAB_R)
GRP_OFF = (0, D // WT, (D + 256) // WT, (5 * D + 256) // WT)
GRP_N = (D // WT, 256 // WT, 4 * D // WT, 2 * D // WT)
SMALL_ROWS = 16


_NN = (((1,), (0,)), ((), ()))
_NT = (((1,), (1,)), ((), ()))
_TN = (((0,), (0,)), ((), ()))


def _pcall(body, **kw):
    return pl.pallas_call(body, **kw)


def _cp(sem=None, **kw):
    return pltpu.CompilerParams(dimension_semantics=sem, vmem_limit_bytes=VMEM_LIMIT, **kw)


def _sig(v):
    return 0.5 * jnp.tanh(0.5 * v) + 0.5


def _accum(ref, val, first):
    @pl.when(first)
    def _():
        ref[...] = val

    @pl.when(jnp.logical_not(first))
    def _():
        ref[...] += val


class _Comm:
    def __init__(self, ins, out_shapes, sem_shapes, phases):
        self.ins, self.out_shapes, self.sem_shapes, self.phases = list(ins), list(out_shapes), list(sem_shapes), phases


def _both(a, b):
    ni, no, ns = len(a.ins), len(a.out_shapes), len(a.sem_shapes)

    def of_a(fn):
        return lambda ins, outs, sems: fn(ins[:ni], outs[:no], sems[:ns])

    def of_b(fn):
        return lambda ins, outs, sems: fn(ins[ni:], outs[no:], sems[ns:])

    return _Comm(a.ins + b.ins, a.out_shapes + b.out_shapes, a.sem_shapes + b.sem_shapes,
                 [(f, of_a(fn)) for f, fn in a.phases] + [(f, of_b(fn)) for f, fn in b.phases])


def _host(body, comm, n_in, n_out, n_scr, nsteps, step_fn):
    if comm is None:
        return body
    ci, co = len(comm.ins), len(comm.out_shapes)

    def wrapped(*refs):
        p = 0
        ins, p = refs[p:p + n_in], p + n_in
        cins, p = refs[p:p + ci], p + ci
        outs, p = refs[p:p + n_out], p + n_out
        couts, p = refs[p:p + co], p + co
        scr, p = refs[p:p + n_scr], p + n_scr
        csems = refs[p:]
        step = step_fn()
        for frac, fn in comm.phases:
            if frac < 1.0:
                @pl.when(step == int(round(frac * (nsteps - 1))))
                def _(fn=fn):
                    fn(cins, couts, csems)
        body(*ins, *outs, *scr)
        for frac, fn in comm.phases:
            if frac >= 1.0:
                @pl.when(step == nsteps - 1)
                def _(fn=fn):
                    fn(cins, couts, csems)

    return wrapped


def _hosted_call(body, comm, args, *, name, grid, in_specs, out_specs, out_shape, scratch_shapes, sem,
                 nsteps, step_fn, aliases=None):
    n_in, n_out, n_scr = len(in_specs), len(out_specs), len(scratch_shapes)
    args = list(args)
    extra = {}
    if comm is not None:
        in_specs = list(in_specs) + [_hbm_spec()] * len(comm.ins)
        out_specs = list(out_specs) + [_hbm_spec()] * len(comm.out_shapes)
        out_shape = list(out_shape) + comm.out_shapes
        scratch_shapes = list(scratch_shapes) + comm.sem_shapes
        args += comm.ins
        extra = dict(has_side_effects=True)
    outs = _pcall(_host(body, comm, n_in, n_out, n_scr, nsteps, step_fn), name=name, grid=grid,
                  in_specs=in_specs, out_specs=out_specs, out_shape=out_shape, scratch_shapes=scratch_shapes,
                  input_output_aliases=aliases or {}, compiler_params=_cp(sem, **extra))(*args)
    return list(outs[:n_out]), list(outs[n_out:])


def _hbm_spec():
    return pl.BlockSpec(memory_space=pl.ANY)


def _wgrad(a, b, *, name):
    (t, m), n = a.shape, b.shape[1]

    def body(a_ref, b_ref, o_ref):
        o_ref[...] = lax.dot_general(a_ref[...], b_ref[...], _TN, preferred_element_type=F32).astype(BF)

    return _pcall(body, name=name, grid=(m // WT,),
                  in_specs=[pl.BlockSpec((t, WT), lambda i: (0, i)), pl.BlockSpec((t, n), lambda i: (0, 0))],
                  out_specs=pl.BlockSpec((WT, n), lambda i: (i, 0)),
                  out_shape=jax.ShapeDtypeStruct((m, n), BF), compiler_params=_cp(("parallel",)))(a, b)


def _fmm(lhs, rhs, extras, epilogue, outs, *, m, n, tm, tn, name, comm=None, vecs=(), consts=(), sums=()):
    tm, tn = min(tm, m), min(tn, n)
    assert m % tm == 0 and n % tn == 0 and (not sums or tn == n), (name, m, n, tm, tn)
    in_specs, args = [], []
    for a in lhs:
        in_specs.append(pl.BlockSpec((tm, a.shape[1]), lambda i, j: (i, 0)))
        args.append(a)
    for li, b, tb in rhs:
        k = lhs[li].shape[1]
        in_specs.append(pl.BlockSpec((tn, k), lambda i, j: (j, 0)) if tb
                        else pl.BlockSpec((k, tn), lambda i, j: (0, j)))
        args.append(b)
    for arr, w, col in extras:
        in_specs.append(pl.BlockSpec((tm, w), lambda i, j, col=col: (i, col(j))))
        args.append(arr)
    for vec in vecs:
        in_specs.append(pl.BlockSpec((1, tn), lambda i, j: (0, j)))
        args.append(vec)
    for whole in consts:
        in_specs.append(pl.BlockSpec(whole.shape, lambda i, j: (0, 0)))
        args.append(whole)
    out_specs = [pl.BlockSpec((tm, w), lambda i, j, col=col: (i, col(j))) for _, _, w, col in outs]
    out_shape = [jax.ShapeDtypeStruct((m, total), dt) for dt, total, _, _ in outs]
    for w in sums:
        out_specs.append(pl.BlockSpec((1, w), lambda i, j: (0, 0)))
        out_shape.append(jax.ShapeDtypeStruct((1, w), F32))
    nl, nr, ne, no = len(lhs), len(rhs), len(extras) + len(vecs) + len(consts), len(outs)

    def body(*refs):
        prods = []
        for r, (li, _, tb) in enumerate(rhs):
            prods.append(lax.dot_general(refs[li][...], refs[nl + r][...], _NT if tb else _NN,
                                         preferred_element_type=F32))
        vals = epilogue(prods, [ref[...] for ref in refs[nl + nr:nl + nr + ne]])
        o_refs = refs[nl + nr + ne:]
        for o_ref, v in zip(o_refs[:no], vals[:no]):
            o_ref[...] = v.astype(o_ref.dtype)
        for s_ref, v in zip(o_refs[no:], vals[no:]):
            _accum(s_ref, v, pl.program_id(0) == 0)

    gm, gn = m // tm, n // tn
    res, comm_res = _hosted_call(
        body, comm, args, name=name, grid=(gm, gn), in_specs=in_specs, out_specs=out_specs,
        out_shape=out_shape, scratch_shapes=[], sem=("arbitrary", "arbitrary"), nsteps=gm * gn,
        step_fn=lambda: pl.program_id(0) * gn + pl.program_id(1))
    return res if comm is None else (res, comm_res)


def _grp_of(i):
    return [jnp.logical_and(i >= GRP_OFF[g], i < GRP_OFF[g] + GRP_N[g]) for g in range(4)]


def _grp_idx(i, g):
    return jnp.clip(i - GRP_OFF[g], 0, GRP_N[g] - 1)


def _inproj_fwd(u, win_t, b_in, *, t, comm=None):
    dtypes = (BF, BF, F32, F32)
    tm = min(1024, t)
    n_row = t // tm
    n_chunks, h_first, g_first = 8, 2, 6
    sub = D // WT

    def w_block(l):
        return jnp.where(l == 0, GRP_OFF[0], jnp.where(l == 1, GRP_OFF[1], GRP_OFF[2] + sub * (l - h_first)))

    def body(u_ref, *rest):
        w_refs, b_refs, (q_ref, kv_ref, h_ref, g_ref) = rest[:sub], rest[sub:2 * sub], rest[2 * sub:]
        l = pl.program_id(1)

        @pl.when(l == 1)
        def _():
            kv_ref[...] = (lax.dot_general(u_ref[...], w_refs[0][...], _NT, preferred_element_type=F32)
                           + b_refs[0][...]).astype(BF)

        for pred, o_ref in ((l == 0, q_ref), (jnp.logical_and(l >= h_first, l < g_first), h_ref),
                            (l >= g_first, g_ref)):
            @pl.when(pred)
            def _(o_ref=o_ref):
                w = jnp.concatenate([w[...] for w in w_refs], axis=0)
                b = jnp.concatenate([b[...] for b in b_refs], axis=1)
                o_ref[...] = (lax.dot_general(u_ref[...], w, _NT, preferred_element_type=F32) + b).astype(o_ref.dtype)

    return _hosted_call(
        body, comm, [u] + [win_t] * sub + [b_in] * sub, name="inproj_fwd", grid=(n_row, n_chunks),
        in_specs=[pl.BlockSpec((tm, D), lambda i, l: (i, 0))]
        + [pl.BlockSpec((WT, D), lambda i, l, o=o: (w_block(l) + o, 0)) for o in range(sub)]
        + [pl.BlockSpec((1, WT), lambda i, l, o=o: (0, w_block(l) + o)) for o in range(sub)],
        out_specs=[pl.BlockSpec((tm, D), lambda i, l: (i, 0)),
                   pl.BlockSpec((tm, 256), lambda i, l: (i, 0)),
                   pl.BlockSpec((tm, D), lambda i, l: (i, jnp.clip(l - h_first, 0, 3))),
                   pl.BlockSpec((tm, D), lambda i, l: (i, jnp.clip(l - g_first, 0, 1)))],
        out_shape=[jax.ShapeDtypeStruct((t, GRP_N[g] * WT), dtypes[g]) for g in range(4)],
        scratch_shapes=[], sem=("arbitrary", "arbitrary"), nsteps=n_row * n_chunks,
        step_fn=lambda: pl.program_id(0) * n_chunks + pl.program_id(1))


def _inproj_bwd_x(dps, win_t, x, g, resid, *, t, part, prev=None, comm=None):
    n_row = 8 if t >= 4096 else 4
    tm = t // n_row
    first = n_row // 4
    per = first if part == 0 else n_row - first
    row = lambda i: part * first + i

    n_chunks = 4
    sub = 2 * D // WT

    def w_block(l):
        return jnp.where(l == 0, 0, GRP_OFF[2] + sub * (l - 1))

    def body(d0, d1, d2, d3, *rest):
        w_refs, (x_ref, g_ref, r_ref) = rest[:sub], rest[sub:sub + 3]
        dg_prev = rest[sub + 3] if prev is not None else None
        o_ref, dg_ref, acc_ref = rest[-3], rest[-2], rest[-1]
        i, l = pl.program_id(0), pl.program_id(1)

        @pl.when(l == 0)
        def _():
            wq = jnp.concatenate([w[...] for w in w_refs[:GRP_N[0]]], axis=0)
            acc_ref[...] = (jnp.dot(d0[...], wq, preferred_element_type=F32)
                            + jnp.dot(d1[...], w_refs[GRP_N[0]][...], preferred_element_type=F32))

        for pred, d_ref in ((jnp.logical_and(l >= 1, l < 3), d2), (l == 3, d3)):
            @pl.when(pred)
            def _(d_ref=d_ref):
                w = jnp.concatenate([w[...] for w in w_refs], axis=0)
                acc_ref[...] += jnp.dot(d_ref[...], w, preferred_element_type=F32)

        @pl.when(l == n_chunks - 1)
        def _():
            xv = x_ref[...]
            r = lax.rsqrt(jnp.mean(xv * xv, axis=-1, keepdims=True) + EPS)
            xh = xv * r
            du = acc_ref[...]
            dxh = du * g_ref[...]
            o_ref[...] = r_ref[...] + r * (dxh - xh * jnp.mean(dxh * xh, axis=-1, keepdims=True))
            dg = jnp.sum(du * xh, axis=0, keepdims=True)
            if dg_prev is not None:
                dg = dg + jnp.where(i == 0, 1.0, 0.0) * dg_prev[...]
            _accum(dg_ref, dg, i == 0)

    rows = lambda w: pl.BlockSpec((tm, w), lambda i, l: (row(i), 0))
    in_specs = ([rows(D), rows(256),
                 pl.BlockSpec((tm, 2 * D), lambda i, l: (row(i), jnp.clip(l - 1, 0, 1))), rows(2 * D)]
                + [pl.BlockSpec((WT, D), lambda i, l, o=o: (w_block(l) + o, 0)) for o in range(sub)]
                + [rows(D), pl.BlockSpec((1, D), lambda i, l: (0, 0)), rows(D)])
    args = list(dps) + [win_t] * sub + [x, g, resid]
    aliases = None
    if prev is not None:
        in_specs += [pl.BlockSpec((1, D), lambda i, l: (0, 0)), _hbm_spec()]
        args += [prev[1], prev[0]]
        aliases = {len(args) - 1: 0}
    return _hosted_call(
        body, comm, args, name="inproj_bwd_x%d" % part, grid=(per, n_chunks), in_specs=in_specs,
        out_specs=[rows(D), pl.BlockSpec((1, D), lambda i, l: (0, 0))],
        out_shape=[jax.ShapeDtypeStruct((t, D), F32), jax.ShapeDtypeStruct((1, D), F32)],
        scratch_shapes=[pltpu.VMEM((tm, D), F32)], sem=("arbitrary", "arbitrary"), nsteps=per * n_chunks,
        step_fn=lambda: pl.program_id(0) * n_chunks + pl.program_id(1), aliases=aliases)


def _inproj_bwd_w(dps, u, *, t):
    n_tiles = IN_W // WT
    dims = (((0,), (0,)), ((), ()))

    def body(d0, d1, d2, d3, u_ref, o_ref, db_ref):
        i = pl.program_id(0)
        uv = u_ref[...]
        for g, (pred, d_ref) in enumerate(zip(_grp_of(i), (d0, d1, d2, d3))):
            @pl.when(pred)
            def _(d_ref=d_ref):
                dv = d_ref[...]
                o_ref[...] = lax.dot_general(dv, uv, dims, preferred_element_type=F32).astype(BF)
                db_ref[...] = jnp.sum(dv.astype(F32), axis=0, keepdims=True)

    return _pcall(body, name="inproj_bwd_w", grid=(n_tiles,),
                  in_specs=[pl.BlockSpec((t, WT), lambda i, g=g: (0, _grp_idx(i, g))) for g in range(4)]
                  + [pl.BlockSpec((t, D), lambda i: (0, 0))],
                  out_specs=[pl.BlockSpec((WT, D), lambda i: (i, 0)),
                             pl.BlockSpec((1, WT), lambda i: (0, i))],
                  out_shape=[jax.ShapeDtypeStruct((IN_W, D), BF), jax.ShapeDtypeStruct((1, IN_W), F32)],
                  compiler_params=_cp(("arbitrary",)))(*dps, u)


def _row_spec(tm, width, col=0):
    return pl.BlockSpec((tm, width), lambda i: (i, col))


def _vec_spec(width):
    return pl.BlockSpec((1, width), lambda i: (0, 0))


def _rms_fwd(x, g, *, tm, name, comm=None):
    t = x.shape[0]
    tm = min(tm, t)

    def body(x_ref, g_ref, u_ref):
        xv = x_ref[...]
        r = lax.rsqrt(jnp.mean(xv * xv, axis=-1, keepdims=True) + EPS)
        u_ref[...] = (xv * r * g_ref[...]).astype(BF)

    (u,), comm_res = _hosted_call(
        body, comm, (x, g), name=name, grid=(t // tm,), in_specs=[_row_spec(tm, D), _vec_spec(D)],
        out_specs=[_row_spec(tm, D)], out_shape=[jax.ShapeDtypeStruct((t, D), BF)], scratch_shapes=[],
        sem=("arbitrary",), nsteps=t // tm, step_fn=lambda: pl.program_id(0))
    return u if comm is None else (u, comm_res)


def _rms_bwd(du, x, g, resid, *, tm, name):
    t = x.shape[0]
    tm = min(tm, t)

    def body(du_ref, x_ref, g_ref, r_ref, dx_ref, dxb_ref, dg_ref):
        xv = x_ref[...]
        r = lax.rsqrt(jnp.mean(xv * xv, axis=-1, keepdims=True) + EPS)
        xh = xv * r
        duv = du_ref[...]
        dxh = duv * g_ref[...]
        dx = r_ref[...] + r * (dxh - xh * jnp.mean(dxh * xh, axis=-1, keepdims=True))
        dx_ref[...] = dx
        dxb_ref[...] = dx.astype(BF)
        _accum(dg_ref, jnp.sum(duv * xh, axis=0, keepdims=True), pl.program_id(0) == 0)

    return _pcall(body, name=name, grid=(t // tm,),
                  in_specs=[_row_spec(tm, D), _row_spec(tm, D), _vec_spec(D), _row_spec(tm, D)],
                  out_specs=[_row_spec(tm, D), _row_spec(tm, D), _vec_spec(D)],
                  out_shape=[jax.ShapeDtypeStruct((t, D), F32), jax.ShapeDtypeStruct((t, D), BF),
                             jax.ShapeDtypeStruct((1, D), F32)],
                  compiler_params=_cp(("arbitrary",)))(du, x, g, resid)


def _attn_kv_tiles(kprev, kcur):
    kv = jnp.concatenate([kprev, kcur], axis=0).astype(F32)
    lo = lax.broadcasted_iota(jnp.int32, (2 * BLK, 128), 1) < HEAD
    tiles = []
    for part in (kv[:, 0:128], kv[:, 128:256]):
        rolled = pltpu.roll(part, HEAD, 1)
        z = jnp.zeros_like(part)
        tiles.append(((jnp.where(lo, part, z).astype(BF), jnp.where(lo, z, rolled).astype(BF)),
                      (jnp.where(lo, rolled, z).astype(BF), jnp.where(lo, z, part).astype(BF))))
    k_t, v_t = tiles
    return [(jnp.concatenate(k_t[h], axis=0), jnp.concatenate(v_t[h], axis=0)) for h in range(2)]


def _attn_mask(i):
    qi = lax.broadcasted_iota(jnp.int32, (BLK, 2 * BLK), 0)
    kj = lax.broadcasted_iota(jnp.int32, (BLK, 2 * BLK), 1)
    first_key = jnp.where(i == 0, BLK, 0)
    in_prev = jnp.logical_and(jnp.logical_and(kj < BLK, kj > qi), kj >= first_key)
    in_cur = jnp.logical_and(kj >= BLK, kj - BLK <= qi)
    return jnp.logical_or(in_prev, in_cur)


def _attn_probs(s, sink, valid):
    s = jnp.where(valid, s * SCALE, NEG)
    mx = jnp.maximum(jnp.max(s, axis=-1, keepdims=True), sink)
    e = jnp.exp(s - mx)
    es = jnp.exp(sink - mx)
    inv = 1.0 / (jnp.sum(e, axis=-1, keepdims=True) + es)
    return e * inv, es * inv


_KEYS = 2 * BLK


def _pair(ref, j):
    return ref[:, j * 128:(j + 1) * 128]


def _attn_fwd(q, kv, sinks, *, t, comm=None):
    nb = t // BLK

    def body(sink_ref, q_ref, kp_ref, kc_ref, o_ref):
        valid = _attn_mask(pl.program_id(0))
        tiles = _attn_kv_tiles(kp_ref[...], kc_ref[...])
        s = [lax.dot_general(_pair(q_ref, j), tiles[j // 4][0], _NT, preferred_element_type=F32)
             for j in range(N_PAIR)]
        p = []
        for j in range(N_PAIR):
            pe, _ = _attn_probs(s[j][:, 0:_KEYS], sink_ref[0, 2 * j], valid)
            po, _ = _attn_probs(s[j][:, _KEYS:2 * _KEYS], sink_ref[0, 2 * j + 1], valid)
            p.append(jnp.concatenate([pe.astype(BF), po.astype(BF)], axis=1))
        for j in range(N_PAIR):
            o_ref[:, j * 128:(j + 1) * 128] = jnp.dot(p[j], tiles[j // 4][1],
                                                      preferred_element_type=F32).astype(BF)

    return _hosted_call(
        body, comm, (sinks, q, kv, kv), name="attn_fwd", grid=(nb,),
        in_specs=[pl.BlockSpec(memory_space=pltpu.SMEM),
                  pl.BlockSpec((BLK, D), lambda i: (i, 0)),
                  pl.BlockSpec((BLK, 256), lambda i: (jnp.maximum(i - 1, 0), 0)),
                  pl.BlockSpec((BLK, 256), lambda i: (i, 0))],
        out_specs=[pl.BlockSpec((BLK, D), lambda i: (i, 0))],
        out_shape=[jax.ShapeDtypeStruct((t, D), BF)],
        scratch_shapes=[], sem=("arbitrary",), nsteps=nb, step_fn=lambda: pl.program_id(0))


def _attn_bwd(q, kv, sinks, do, *, t, comm=None):
    nb = t // BLK
    last = nb - 1

    def body(sink_ref, q_ref, kp_ref, kc_ref, do_ref, dq_ref, dkv_ref, ds_ref, carry_ref):
        i = pl.program_id(0)

        @pl.when(i == 0)
        def _():
            ds_ref[...] = jnp.zeros_like(ds_ref)
            carry_ref[...] = jnp.zeros_like(carry_ref)

        @pl.when(i < nb)
        def _():
            valid = _attn_mask(i)
            tiles = _attn_kv_tiles(kp_ref[...], kc_ref[...])
            lane1 = lax.broadcasted_iota(jnp.int32, (1, 128), 1)
            dsink = jnp.zeros((1, 128), F32)
            s = [lax.dot_general(_pair(q_ref, j), tiles[j // 4][0], _NT, preferred_element_type=F32)
                 for j in range(N_PAIR)]
            dp = [lax.dot_general(_pair(do_ref, j), tiles[j // 4][1], _NT, preferred_element_type=F32)
                  for j in range(N_PAIR)]
            p_all, ds_all = [], []
            for j in range(N_PAIR):
                halves = []
                for par in range(2):
                    cols = slice(par * _KEYS, (par + 1) * _KEYS)
                    p, ps = _attn_probs(s[j][:, cols], sink_ref[0, 2 * j + par], valid)
                    dpj = dp[j][:, cols]
                    dd = jnp.sum(p * dpj, axis=-1, keepdims=True)
                    dsink = dsink + jnp.where(lane1 == 2 * j + par,
                                              -jnp.sum(ps * dd, axis=0, keepdims=True), 0.0)
                    halves.append((p.astype(BF), (p * (dpj - dd)).astype(BF)))
                p_all.append(jnp.concatenate([halves[0][0], halves[1][0]], axis=1))
                ds_all.append(jnp.concatenate([halves[0][1], halves[1][1]], axis=1))
            for j in range(N_PAIR):
                dq_ref[:, j * 128:(j + 1) * 128] = (
                    jnp.dot(ds_all[j], tiles[j // 4][0], preferred_element_type=F32) * SCALE).astype(BF)
            ds_ref[...] += dsink
            gk, gv = [], []
            for h in range(2):
                grp = range(4 * h, 4 * h + 4)
                q_rows = jnp.concatenate([_pair(q_ref, j) for j in grp], axis=0)
                do_rows = jnp.concatenate([_pair(do_ref, j) for j in grp], axis=0)
                g_k = lax.dot_general(jnp.concatenate([ds_all[j] for j in grp], axis=0), q_rows, _TN,
                                      preferred_element_type=F32)
                g_v = lax.dot_general(jnp.concatenate([p_all[j] for j in grp], axis=0), do_rows, _TN,
                                      preferred_element_type=F32)
                gk.append((g_k[0:_KEYS], g_k[_KEYS:2 * _KEYS]))
                gv.append((g_v[0:_KEYS], g_v[_KEYS:2 * _KEYS]))
            lo = lax.broadcasted_iota(jnp.int32, (2 * BLK, 128), 1) < HEAD
            zero = jnp.zeros((2 * BLK, 128), F32)

            def unpad(g):
                return (jnp.where(lo, g[0][0] + pltpu.roll(g[0][1], HEAD, 1), zero)
                        + jnp.where(lo, zero, pltpu.roll(g[1][0], HEAD, 1) + g[1][1]))

            dk = unpad(gk) * SCALE
            dv = unpad(gv)
            dkv_ref[:, 0:128] = (carry_ref[:, 0:128] + dk[0:BLK]).astype(BF)
            dkv_ref[:, 128:256] = (carry_ref[:, 128:256] + dv[0:BLK]).astype(BF)
            carry_ref[:, 0:128] = dk[BLK:2 * BLK]
            carry_ref[:, 128:256] = dv[BLK:2 * BLK]

        @pl.when(i == nb)
        def _():
            dkv_ref[...] = carry_ref[...].astype(BF)

    return _hosted_call(
        body, comm, (sinks, q, kv, kv, do), name="attn_bwd", grid=(nb + 1,),
        in_specs=[pl.BlockSpec(memory_space=pltpu.SMEM),
                  pl.BlockSpec((BLK, D), lambda i: (jnp.minimum(i, last), 0)),
                  pl.BlockSpec((BLK, 256), lambda i: (jnp.clip(i - 1, 0, last), 0)),
                  pl.BlockSpec((BLK, 256), lambda i: (jnp.minimum(i, last), 0)),
                  pl.BlockSpec((BLK, D), lambda i: (jnp.minimum(i, last), 0))],
        out_specs=[pl.BlockSpec((BLK, D), lambda i: (jnp.minimum(i, last), 0)),
                   pl.BlockSpec((BLK, 256), lambda i: (jnp.maximum(i - 1, 0), 0)),
                   pl.BlockSpec((1, 128), lambda i: (0, 0))],
        out_shape=[jax.ShapeDtypeStruct((t, D), BF), jax.ShapeDtypeStruct((t, 256), BF),
                   jax.ShapeDtypeStruct((1, 128), F32)],
        scratch_shapes=[pltpu.VMEM((BLK, 256), F32)], sem=("arbitrary",), nsteps=nb + 1,
        step_fn=lambda: pl.program_id(0))


def _split3(v):
    h = v.astype(BF)
    r = v - h.astype(F32)
    m = r.astype(BF)
    lo = (r - m.astype(F32)).astype(BF)
    return jnp.concatenate([h, m, lo], axis=1)


def _apply01(mat, v):
    n = v.shape[1]
    r = jnp.dot(mat, _split3(v), preferred_element_type=F32)
    return r[:, 0:n] + r[:, n:2 * n] + r[:, 2 * n:3 * n]


def _hgrn_gates(hq, hf, lb):
    sq = _sig(hq)
    sg = _sig(hf)
    f = lb + (1.0 - lb) * sg
    return hq * sq, (1.0 - lb) * (1.0 - sg), jnp.log(f), sq, sg, f


def _tri(upper):
    r = lax.broadcasted_iota(jnp.int32, (CH, CH), 0)
    c = lax.broadcasted_iota(jnp.int32, (CH, CH), 1)
    return (c >= r) if upper else (c <= r)


def _lb_from_logits(lg_ref):
    return 1.0 / (1.0 + jnp.exp(lg_ref[1:2, :] - lg_ref[0:1, :]))


def _hgrn_fwd(h4, logits, norm_g, *, t, comm=None):
    nc = t // CH
    nt_dims = (((1,), (1,)), ((), ()))
    tn_dims = (((0,), (0,)), ((), ()))

    def body(h_ref, lg_ref, ng_ref, y_ref, o_ref, st_ref, s_scr, b_scr, qa_s, ka_s, qb_s, kb_s, v_s):
        @pl.when(pl.program_id(0) == 0)
        def _():
            s_scr[...] = jnp.zeros_like(s_scr)

        heads = [slice(h * HG_K, (h + 1) * HG_K) for h in range(HG_HEADS)]
        causal = _tri(False)
        lb = _lb_from_logits(lg_ref)
        for c in range(HG_SUB):
            rows = slice(c * CH, (c + 1) * CH)
            q, k, g, _, _, _ = _hgrn_gates(h_ref[rows, 0:D], h_ref[rows, D:2 * D], lb)
            b_scr[...] = _apply01(jnp.where(causal, 1.0, 0.0).astype(BF), g)
            b = b_scr[...]
            b_mid = b_scr[CH // 2 - 1:CH // 2, :]
            b_last = b_scr[CH - 1:CH, :]
            qa_s[...] = (q * jnp.exp(b - b_mid)).astype(BF)
            ka_s[...] = (k * jnp.exp(b_mid - b)).astype(BF)
            qb_s[...] = (q * jnp.exp(b)).astype(BF)
            kb_s[...] = (k * jnp.exp(b_last - b)).astype(BF)
            v_s[...] = h_ref[rows, 2 * D:3 * D].astype(BF)
            dec = jnp.exp(b_last)
            st_ref[c] = s_scr[...].astype(BF)
            a = [jnp.where(causal, lax.dot_general(qa_s[:, sl], ka_s[:, sl], nt_dims, preferred_element_type=F32),
                           0.0).astype(BF) for sl in heads]
            for h, sl in enumerate(heads):
                o_ref[rows, sl] = (jnp.dot(a[h], v_s[:, sl], preferred_element_type=F32)
                                   + lax.dot_general(qb_s[:, sl], s_scr[h].astype(BF), nt_dims,
                                                     preferred_element_type=F32))
            for h, sl in enumerate(heads):
                s_scr[h] = dec[:, sl] * s_scr[h] + lax.dot_general(v_s[:, sl], kb_s[:, sl], tn_dims,
                                                                   preferred_element_type=F32)
            for h, sl in enumerate(heads):
                o = o_ref[rows, sl]
                on = o * lax.rsqrt(jnp.mean(o * o, axis=-1, keepdims=True) + EPS)
                gate = _sig(h_ref[rows, 3 * D + h * HG_K:3 * D + (h + 1) * HG_K])
                y_ref[rows, sl] = (on * ng_ref[:, sl] * gate).astype(BF)

    half = lambda: pltpu.VMEM((CH, D), BF)
    blk = HG_SUB * CH
    return _hosted_call(
        body, comm, (h4, logits, norm_g), name="hgrn_fwd", grid=(nc // HG_SUB,),
        in_specs=[pl.BlockSpec((blk, 4 * D), lambda n: (n, 0)),
                  pl.BlockSpec((2, D), lambda n: (0, 0)),
                  pl.BlockSpec((1, D), lambda n: (0, 0))],
        out_specs=[pl.BlockSpec((blk, D), lambda n: (n, 0)),
                   pl.BlockSpec((blk, D), lambda n: (n, 0)),
                   pl.BlockSpec((HG_SUB, HG_HEADS, HG_K, HG_K), lambda n: (n, 0, 0, 0))],
        out_shape=[jax.ShapeDtypeStruct((t, D), BF), jax.ShapeDtypeStruct((t, D), F32),
                   jax.ShapeDtypeStruct((nc, HG_HEADS, HG_K, HG_K), BF)],
        scratch_shapes=[pltpu.VMEM((HG_HEADS, HG_K, HG_K), F32), pltpu.VMEM((CH, D), F32),
                        half(), half(), half(), half(), half()],
        sem=("arbitrary",), nsteps=nc // HG_SUB, step_fn=lambda: pl.program_id(0))


def _hgrn_bwd(h4, logits, norm_g, o_pre, states, dy, *, t, comm=None):
    nc = t // CH
    nt_dims = (((1,), (1,)), ((), ()))
    tn_dims = (((0,), (0,)), ((), ()))

    def body(h_ref, lg_ref, ng_ref, o_ref, st_ref, dy_ref, dh_ref, dlg_ref, dng_ref, ds_scr, dlb_scr,
             b_scr, tail_s, e_qa, e_ka, e_qb, e_kb, q_s, k_s, dqa_s, dka_s, dqb_s, dkb_s,
             qa_s, ka_s, qb_s, kb_s, v_s, do_s):
        n = pl.program_id(0)

        @pl.when(n == 0)
        def _():
            ds_scr[...] = jnp.zeros_like(ds_scr)
            dlb_scr[...] = jnp.zeros_like(dlb_scr)
            dng_ref[...] = jnp.zeros_like(dng_ref)

        heads = [slice(h * HG_K, (h + 1) * HG_K) for h in range(HG_HEADS)]
        lb = _lb_from_logits(lg_ref)
        causal = _tri(False)

        def chunk(c):
            rows = slice(c * CH, (c + 1) * CH)
            hq = h_ref[rows, 0:D]
            q, k, g, sq, sg, f = _hgrn_gates(hq, h_ref[rows, D:2 * D], lb)
            b_scr[...] = _apply01(jnp.where(causal, 1.0, 0.0).astype(BF), g)
            b = b_scr[...]
            b_mid = b_scr[CH // 2 - 1:CH // 2, :]
            b_last = b_scr[CH - 1:CH, :]
            q_s[...] = q
            k_s[...] = k
            for e_ref, s_ref, base, expo in ((e_qa, qa_s, q, b - b_mid), (e_ka, ka_s, k, b_mid - b),
                                             (e_qb, qb_s, q, b), (e_kb, kb_s, k, b_last - b)):
                e = jnp.exp(expo)
                e_ref[...] = e
                s_ref[...] = (base * e).astype(BF)
            v_s[...] = h_ref[rows, 2 * D:3 * D].astype(BF)
            dec = jnp.exp(b_last)
            for h, sl in enumerate(heads):
                gcol = slice(3 * D + h * HG_K, 3 * D + (h + 1) * HG_K)
                ngh = ng_ref[:, sl]
                sgate = _sig(h_ref[rows, gcol])
                o = o_ref[rows, sl]
                r = lax.rsqrt(jnp.mean(o * o, axis=-1, keepdims=True) + EPS)
                on = o * r
                dyh = dy_ref[rows, sl]
                dh_ref[rows, gcol] = (dyh * on * ngh * sgate * (1.0 - sgate)).astype(BF)
                dng_ref[:, sl] += jnp.sum(dyh * on * sgate, axis=0, keepdims=True)
                don = dyh * ngh * sgate
                do_s[:, sl] = (r * (don - on * jnp.mean(don * on, axis=-1, keepdims=True))).astype(BF)
            a = [jnp.where(causal, lax.dot_general(qa_s[:, sl], ka_s[:, sl], nt_dims, preferred_element_type=F32),
                           0.0).astype(BF) for sl in heads]
            da = [jnp.where(causal, lax.dot_general(do_s[:, sl], v_s[:, sl], nt_dims, preferred_element_type=F32),
                            0.0).astype(BF) for sl in heads]
            for h, sl in enumerate(heads):
                dh_ref[rows, 2 * D + h * HG_K:2 * D + (h + 1) * HG_K] = (
                    lax.dot_general(a[h], do_s[:, sl], tn_dims, preferred_element_type=F32)
                    + lax.dot_general(kb_s[:, sl], ds_scr[h].astype(BF), nt_dims, preferred_element_type=F32)
                ).astype(BF)
            for h, sl in enumerate(heads):
                dqa_s[:, sl] = jnp.dot(da[h], ka_s[:, sl], preferred_element_type=F32)
            for h, sl in enumerate(heads):
                dka_s[:, sl] = lax.dot_general(da[h], qa_s[:, sl], tn_dims, preferred_element_type=F32)
            for h, sl in enumerate(heads):
                dqb_s[:, sl] = jnp.dot(do_s[:, sl], st_ref[c, h], preferred_element_type=F32)
            for h, sl in enumerate(heads):
                dkb_s[:, sl] = jnp.dot(v_s[:, sl], ds_scr[h].astype(BF), preferred_element_type=F32)
            for h, sl in enumerate(heads):
                tail_s[:, sl] = jnp.sum(dec[:, sl] * st_ref[c, h].astype(F32) * ds_scr[h], axis=0, keepdims=True)
            for h, sl in enumerate(heads):
                ds_scr[h] = (lax.dot_general(do_s[:, sl], qb_s[:, sl], tn_dims, preferred_element_type=F32)
                             + dec[:, sl] * ds_scr[h])
            qv, kv = q_s[...], k_s[...]
            dqa, dka, dqb, dkb = dqa_s[...], dka_s[...], dqb_s[...], dkb_s[...]
            eqa, eka, eqb, ekb = e_qa[...], e_ka[...], e_qb[...], e_kb[...]
            dkb_kb = dkb * (kv * ekb)
            db_last = jnp.sum(dkb_kb, axis=0, keepdims=True) + tail_s[...]
            last_row = lax.broadcasted_iota(jnp.int32, (CH, D), 0) == CH - 1
            db = (dqa * (qv * eqa) - dka * (kv * eka) + dqb * (qv * eqb) - dkb_kb
                  + jnp.where(last_row, db_last, 0.0))
            dg = _apply01(jnp.where(_tri(True), 1.0, 0.0).astype(BF), db)
            dq = dqa * eqa + dqb * eqb
            dk = dka * eka + dkb * ekb
            dh_ref[rows, 0:D] = (dq * sq * (1.0 + hq * (1.0 - sq))).astype(BF)
            dfk = dg / f - dk
            dh_ref[rows, D:2 * D] = ((1.0 - lb) * dfk * sg * (1.0 - sg)).astype(BF)
            dlb_scr[...] += jnp.sum((1.0 - sg) * dfk, axis=0, keepdims=True)

        for c in reversed(range(HG_SUB)):
            chunk(c)

        @pl.when(n == nc // HG_SUB - 1)
        def _():
            dl0 = dlb_scr[...] * lb * (1.0 - lb)
            dlg_ref[0:1, :] = dl0
            dlg_ref[1:2, :] = -dl0

    steps = nc // HG_SUB
    blk = HG_SUB * CH
    rev = lambda n: (steps - 1 - n, 0)
    return _hosted_call(
        body, comm, (h4, logits, norm_g, o_pre, states, dy), name="hgrn_bwd", grid=(steps,),
        in_specs=[pl.BlockSpec((blk, 4 * D), rev),
                  pl.BlockSpec((2, D), lambda n: (0, 0)),
                  pl.BlockSpec((1, D), lambda n: (0, 0)),
                  pl.BlockSpec((blk, D), rev),
                  pl.BlockSpec((HG_SUB, HG_HEADS, HG_K, HG_K), lambda n: (steps - 1 - n, 0, 0, 0)),
                  pl.BlockSpec((blk, D), rev)],
        out_specs=[pl.BlockSpec((blk, 4 * D), rev),
                   pl.BlockSpec((2, D), lambda n: (0, 0)),
                   pl.BlockSpec((1, D), lambda n: (0, 0))],
        out_shape=[jax.ShapeDtypeStruct((t, 4 * D), BF), jax.ShapeDtypeStruct((2, D), F32),
                   jax.ShapeDtypeStruct((1, D), F32)],
        scratch_shapes=([pltpu.VMEM((HG_HEADS, HG_K, HG_K), F32), pltpu.VMEM((1, D), F32),
                         pltpu.VMEM((CH, D), F32), pltpu.VMEM((1, D), F32)]
                        + [pltpu.VMEM((CH, D), F32)] * 10 + [pltpu.VMEM((CH, D), BF)] * 6),
        sem=("arbitrary",), nsteps=steps, step_fn=lambda: pl.program_id(0))


def _place():
    x, y, c = lax.axis_index("x"), lax.axis_index("y"), lax.axis_index("c")
    return x, y, c, [(1 - x, y), (x, 1 - y), (1 - x, 1 - y)]


def _gather_comm(shards, mids):
    n, pieces = len(shards), len(mids)
    r = [s.shape[0] for s in shards]
    tile = 16
    cut = [[(rw // tile * p // pieces) * tile for p in range(pieces + 1)] for rw in r]
    size = [[cut[w][p + 1] - cut[w][p] for p in range(pieces)] for w in range(n)]

    def tools(ins, outs, sems):
        send_sems, recv_sems, local_sems = sems
        x, y, c, chips = _place()
        me, sib = (x, y, c), (x, y, 1 - c)

        def rows(w, p, dev):
            return outs[w].at[pl.ds((4 * dev[0] + 2 * dev[1] + dev[2]) * r[w] + cut[w][p], size[w][p]), :]

        def copy(kind, w, p, block, to, own=False):
            src = ins[w].at[pl.ds(cut[w][p], size[w][p]), :] if own else rows(w, p, block)
            return pltpu.make_async_remote_copy(
                src_ref=src, dst_ref=rows(w, p, block), send_sem=send_sems.at[p, kind],
                recv_sem=recv_sems.at[p, kind], device_id=to, device_id_type=MESH)

        def all_of(kind, p):
            whole = outs[0].at[pl.ds(0, sum(size[w][p] for w in range(n))), :]
            return pltpu.make_async_remote_copy(
                src_ref=whole, dst_ref=whole, send_sem=send_sems.at[p, kind], recv_sem=recv_sems.at[p, kind],
                device_id=me, device_id_type=MESH)

        mine = [pltpu.make_async_copy(ins[w], outs[w].at[pl.ds((4 * x + 2 * y + c) * r[w], r[w]), :],
                                      local_sems.at[w]) for w in range(n)]
        return c, chips, me, sib, copy, all_of, mine

    def start(ins, outs, sems):
        c, chips, me, sib, copy, _, mine = tools(ins, outs, sems)
        for cp in mine:
            cp.start()
        for p in range(pieces):
            for w in range(n):
                copy(0, w, p, me, sib, own=True).start()
                for j, chip in enumerate(chips):
                    copy(1 + j, w, p, me, (*chip, c), own=True).start()

    def pass_on(p):
        def phase(ins, outs, sems):
            c, chips, _, sib, copy, all_of, _ = tools(ins, outs, sems)
            for j, chip in enumerate(chips):
                all_of(1 + j, p).wait_recv()
                for w in range(n):
                    copy(4 + j, w, p, (*chip, c), sib).start()
        return phase

    def finish(ins, outs, sems):
        _, _, _, _, _, all_of, mine = tools(ins, outs, sems)
        for p in range(pieces):
            all_of(0, p).wait_recv()
            for j in range(3):
                all_of(4 + j, p).wait_recv()
            for kind in range(7):
                all_of(kind, p).wait_send()
        for cp in mine:
            cp.wait()

    return _Comm(shards, [jax.ShapeDtypeStruct((N_DEV * rw, D), BF) for rw in r],
                 [pltpu.SemaphoreType.DMA((pieces, 7)), pltpu.SemaphoreType.DMA((pieces, 7)),
                  pltpu.SemaphoreType.DMA((n,))],
                 [(0.0, start)] + [(f, pass_on(p)) for p, f in enumerate(mids)] + [(1.0, finish)])


def _pair_comm(grads):
    n = len(grads)
    r = [g.shape[0] // N_DEV for g in grads]

    def start(ins, outs, sems):
        send_sems, recv_sems = sems
        x, y, c, _ = _place()
        for w in range(n):
            for a in range(N_CHIP):
                pltpu.make_async_remote_copy(
                    src_ref=ins[w].at[pl.ds((2 * a + 1 - c) * r[w], r[w]), :], dst_ref=outs[w].at[a],
                    send_sem=send_sems.at[w], recv_sem=recv_sems.at[w],
                    device_id=(x, y, 1 - c), device_id_type=MESH).start()

    def finish(ins, outs, sems):
        send_sems, recv_sems = sems
        x, y, c, _ = _place()
        for w in range(n):
            pltpu.make_async_remote_copy(
                src_ref=outs[w], dst_ref=outs[w], send_sem=send_sems.at[w], recv_sem=recv_sems.at[w],
                device_id=(x, y, c), device_id_type=MESH).wait()

    return _Comm(grads, [jax.ShapeDtypeStruct((N_CHIP, rw, D), BF) for rw in r],
                 [pltpu.SemaphoreType.DMA((n,)), pltpu.SemaphoreType.DMA((n,))],
                 [(0.0, start), (1.0, finish)])


def _pair_add(grad, got, core, *, name):
    r = got.shape[1]

    def body(c_ref, g_ref, got_ref, o_ref):
        o_ref[0] = (g_ref[...].astype(F32) + got_ref[0].astype(F32)).astype(BF)

    grid_spec = pltpu.PrefetchScalarGridSpec(
        num_scalar_prefetch=1, grid=(N_CHIP,),
        in_specs=[pl.BlockSpec((r, D), lambda a, c_ref: (2 * a + c_ref[0], 0)),
                  pl.BlockSpec((1, r, D), lambda a, c_ref: (a, 0, 0))],
        out_specs=pl.BlockSpec((1, r, D), lambda a, c_ref: (a, 0, 0)))
    return _pcall(body, name=name, grid_spec=grid_spec,
                  out_shape=jax.ShapeDtypeStruct((N_CHIP, r, D), BF),
                  compiler_params=_cp(("parallel",)))(core, grad, got)


def _chip_comm(pair_sums):
    n = len(pair_sums)
    r = [p.shape[1] for p in pair_sums]
    off = [sum(r[:w]) for w in range(n)]

    def tools(ins, outs, sems):
        send_sems, recv_sems, local_sems = sems
        x, y, c, chips = _place()
        my_chip = 2 * x + y

        def slot(w):
            return outs[0].at[my_chip, pl.ds(off[w], r[w]), :]

        own = [pltpu.make_async_copy(ins[w].at[my_chip], slot(w), local_sems.at[w]) for w in range(n)]
        return x, y, c, chips, my_chip, slot, own, send_sems, recv_sems

    def start(ins, outs, sems):
        x, y, c, chips, my_chip, slot, own, send_sems, recv_sems = tools(ins, outs, sems)
        for cp in own:
            cp.start()
        for j, chip in enumerate(chips):
            for w in range(n):
                pltpu.make_async_remote_copy(
                    src_ref=ins[w].at[2 * chip[0] + chip[1]], dst_ref=slot(w), send_sem=send_sems.at[j],
                    recv_sem=recv_sems.at[j], device_id=(*chip, c), device_id_type=MESH).start()

    def finish(ins, outs, sems):
        x, y, c, chips, my_chip, slot, own, send_sems, recv_sems = tools(ins, outs, sems)
        whole = outs[0].at[my_chip]
        for j in range(3):
            pltpu.make_async_remote_copy(
                src_ref=whole, dst_ref=whole, send_sem=send_sems.at[j], recv_sem=recv_sems.at[j],
                device_id=(x, y, c), device_id_type=MESH).wait()
        for cp in own:
            cp.wait()

    return _Comm(pair_sums, [jax.ShapeDtypeStruct((N_CHIP, sum(r), D), BF)],
                 [pltpu.SemaphoreType.DMA((3,)), pltpu.SemaphoreType.DMA((3,)), pltpu.SemaphoreType.DMA((n,))],
                 [(0.0, start), (1.0, finish)])


def _adam_math(w, g, m, v):
    m = ADAM_B1 * m + (1.0 - ADAM_B1) * g
    v = ADAM_B2 * v + (1.0 - ADAM_B2) * (g * g)
    m_hat = m / (1.0 - ADAM_B1 ** ADAM_STEP)
    v_hat = v / (1.0 - ADAM_B2 ** ADAM_STEP)
    delta = -ADAM_LR * (m_hat / (jnp.sqrt(v_hat) + ADAM_EPS) + ADAM_WD * w)
    return delta, m, v


SMALL = (("norm_mix_g", (1, D), 0), ("hgrn_norm_g", (1, D), 1), ("norm_ffn_g", (1, D), 2),
         ("norm_final_g", (1, D), 3), ("hgrn_lb_logits", (2, D), 4), ("attn_sinks", (1, 16), 6),
         ("b_in", (1, IN_W), 8))
LOSS_ROW = 7


def _small_allreduce_adam(grads, loss_row, params):
    n = len(SMALL)

    def rows_of(ref, shape, row):
        r, w = shape
        if w <= D:
            return ref[row:row + r, 0:w]
        pieces = [ref[row + k:row + k + 1, :] for k in range(-(-w // D))]
        return jnp.concatenate(pieces, axis=1)[:, 0:w]

    def body(*refs):
        g_refs, loss_ref = refs[:n], refs[n]
        wmv = refs[n + 1:4 * n + 1]
        loss_out = refs[4 * n + 1]
        outs = refs[4 * n + 2:8 * n + 2]
        mine, total, gath, send_sems, recv_sems = refs[8 * n + 2:]
        x, y, c, _ = _place()
        me = 4 * x + 2 * y + c
        mine[...] = jnp.zeros_like(mine)
        for g_ref, (_, (r, w), row) in zip(g_refs, SMALL):
            for k in range(-(-w // D)):
                wk = min(D, w - k * D)
                mine[row + k:row + k + r, 0:wk] = g_ref[:, k * D:k * D + wk]
        mine[LOSS_ROW:LOSS_ROW + 1, 0:128] = loss_ref[...]
        gath[me] = mine[...]
        cps = []
        for d in range(1, N_DEV):
            peer = (x ^ (d >> 2), y ^ ((d >> 1) & 1), c ^ (d & 1))
            cps.append(pltpu.make_async_remote_copy(
                src_ref=mine, dst_ref=gath.at[me], send_sem=send_sems.at[d - 1],
                recv_sem=recv_sems.at[d - 1], device_id=peer, device_id_type=MESH))
        for cp in cps:
            cp.start()
        for cp in cps:
            cp.wait()
        g = gath[0]
        for k in range(1, N_DEV):
            g = g + gath[k]
        total[...] = g
        loss_out[...] = total[LOSS_ROW:LOSS_ROW + 1, 0:128]
        for i, (_, shape, row) in enumerate(SMALL):
            gi = rows_of(total, shape, row)
            w_ref, m_ref, v_ref = wmv[3 * i:3 * i + 3]
            o = outs[4 * i:4 * i + 4]
            o[0][...] = gi
            o[1][...], o[2][...], o[3][...] = _adam_math(w_ref[...], gi, m_ref[...], v_ref[...])

    vm = pl.BlockSpec(memory_space=pltpu.VMEM)
    ins = [grads[name] for name, _, _ in SMALL] + [loss_row]
    for name, _, _ in SMALL:
        ins += list(params[name])
    out_shape = [jax.ShapeDtypeStruct((1, 128), F32)]
    for _, shape, _ in SMALL:
        out_shape += [jax.ShapeDtypeStruct(shape, F32)] * 4
    res = _pcall(body, name="small_allreduce_adam", in_specs=[vm] * len(ins), out_specs=[vm] * len(out_shape),
                 out_shape=out_shape,
                 scratch_shapes=[pltpu.VMEM((SMALL_ROWS, D), F32), pltpu.VMEM((SMALL_ROWS, D), F32),
                                 pltpu.VMEM((N_DEV, SMALL_ROWS, D), F32),
                                 pltpu.SemaphoreType.DMA((N_DEV - 1,)), pltpu.SemaphoreType.DMA((N_DEV - 1,))],
                 compiler_params=pltpu.CompilerParams(has_side_effects=True))(*ins)
    return res[0], {name: res[1 + 4 * i:5 + 4 * i] for i, (name, _, _) in enumerate(SMALL)}


def _adam(w, parts, index, m, v, *, name):
    rows = w.shape[0]
    tr = rows if rows <= 512 else rows // 2
    steps = rows // tr

    def body(w_ref, p_ref, m_ref, v_ref, g_ref, d_ref, mo_ref, vo_ref):
        g = p_ref[0].astype(F32)
        for a in range(1, N_CHIP):
            g = g + p_ref[a].astype(F32)
        g_ref[...] = g
        d_ref[...], mo_ref[...], vo_ref[...] = _adam_math(w_ref[...], g, m_ref[...], v_ref[...])

    spec = pl.BlockSpec((tr, D), lambda i: (i, 0))
    return _pcall(body, name=name, grid=(steps,),
                  in_specs=[spec, pl.BlockSpec((N_CHIP, tr, D), lambda i: (0, index * steps + i, 0)), spec, spec],
                  out_specs=[spec] * 4, out_shape=[jax.ShapeDtypeStruct((rows, D), F32)] * 4,
                  compiler_params=_cp(("parallel",)))(w, parts, m, v)


def _step(x, tgt, shards, norm_mix_g, b_in, sinks, logits, hgrn_norm_g, norm_ffn_g, norm_final_g):
    t = x.shape[0]
    core = lax.axis_index("c").astype(jnp.int32).reshape(1)

    u1, (win_t,) = _rms_fwd(x, norm_mix_g, tm=512, name="rms_mix", comm=_gather_comm(shards[0:1], (0.2, 0.4, 0.6, 0.8)))
    (q, kv, h4, gates), (wg_t,) = _inproj_fwd(u1, win_t, b_in, t=t,
                                              comm=_gather_comm(shards[1:2], (0.2, 0.4, 0.6, 0.8)))
    (y_attn,), (wba, wbh, wout) = _attn_fwd(q, kv, sinks, t=t,
                                            comm=_gather_comm(shards[4:7], (0.2, 0.4, 0.6, 0.8)))
    (y_hgrn, o_pre, states), (wu_t, wd) = _hgrn_fwd(h4, logits, hgrn_norm_g, t=t,
                                                    comm=_gather_comm(shards[2:4], (0.27, 0.52, 0.77, 0.97)))
    col = lambda j: j
    first, second = (lambda j: 0), (lambda j: 1)
    gate_tiles = [(gates, D, first), (gates, D, second)]

    def merge(prods, ex):
        (ya_, yb_), (ga, gb) = prods, ex
        sa, sb = _sig(ga), _sig(gb)
        return sa, sb, ya_ * sa * (1.0 - sa), yb_ * sb * (1.0 - sb), sa * ya_ + sb * yb_

    sig_a, sig_b, dgate_a, dgate_b, merged = _fmm(
        [y_attn, y_hgrn], [(0, wba, False), (1, wbh, False)], gate_tiles, merge,
        [(BF, D, D, first)] * 5, m=t, n=D, tm=512, tn=D, name="branch_merge")
    def resid_norm(prods, ex):
        (p,), (xv, gv) = prods, ex
        hv = xv + p
        return hv, hv * lax.rsqrt(jnp.mean(hv * hv, axis=-1, keepdims=True) + EPS) * gv

    h1, u2 = _fmm([merged], [(0, wout, False)], [(x, D, first)], resid_norm, [(F32, D, D, first), (BF, D, D, first)],
                  m=t, n=D, tm=1024, tn=D, name="out_proj", vecs=[norm_ffn_g])

    def swiglu(prods, ex):
        g_, u_ = prods
        s = _sig(g_)
        silu = g_ * s
        return u_ * s * (1.0 + g_ * (1.0 - s)), silu, silu * u_

    dz_dgate, dz_dup, z = _fmm([u2], [(0, wg_t, True), (0, wu_t, True)], [], swiglu,
                               [(BF, FFN, FFN // 2, col)] * 3, m=t, n=FFN, tm=1024, tn=FFN // 2,
                               name="ffn_gate_up")
    def loss_head(prods, ex):
        (p,), (hv, tv, gv) = prods, ex
        hv = hv + p
        r = lax.rsqrt(jnp.mean(hv * hv, axis=-1, keepdims=True) + EPS)
        xh = hv * r
        err = xh * gv - tv
        lp = jnp.sum(jnp.sum(err * err, axis=1, keepdims=True), axis=0, keepdims=True) * (0.5 / D)
        dy = err * (1.0 / D)
        dxh = dy * gv
        dh = r * (dxh - xh * jnp.mean(dxh * xh, axis=-1, keepdims=True))
        return dh, dh, jnp.sum(dy * xh, axis=0, keepdims=True), jnp.broadcast_to(lp, (1, 128))

    dh2, dh2_b, d_norm_final, loss_row = _fmm(
        [z], [(0, wd, False)], [(h1, D, first), (tgt, D, first)], loss_head, [(F32, D, D, first), (BF, D, D, first)],
        m=t, n=D, tm=512, tn=D, name="ffn_down_loss", vecs=[norm_final_g], sums=[D, 128])

    def swiglu_bwd(prods, ex):
        (dz,), (da_, db_) = prods, ex
        return dz * da_.astype(F32), dz * db_.astype(F32)

    ffn_tiles = [(dz_dgate, FFN // 2, col), (dz_dup, FFN // 2, col)]
    dgt, dup = _fmm([dh2_b], [(0, wd, True)], ffn_tiles, swiglu_bwd, [(BF, FFN, FFN // 2, col)] * 2,
                    m=t, n=FFN, tm=1024, tn=FFN // 2, name="d_gate_up")
    d_wd = _wgrad(z, dh2_b, name="d_w_down")
    (du2,) = _fmm([dgt, dup], [(0, wg_t, False), (1, wu_t, False)], [], lambda prods, ex: (prods[0] + prods[1],),
                  [(F32, D, 512, col)], m=t, n=D, tm=1024, tn=512, name="d_u2")
    d_wg = _wgrad(dgt, u2, name="d_w_gate")
    d_wu = _wgrad(dup, u2, name="d_w_up")
    dh1, dh1_b, d_norm_ffn = _rms_bwd(du2, h1, norm_ffn_g, dh2, tm=512, name="rms_ffn_bwd")
    d_wout = _wgrad(merged, dh1_b, name="d_w_out")

    def merge_bwd(prods, ex):
        (dm,), (sa, sb, ca, cb, wa, wb) = prods, ex
        dgate = jnp.concatenate([dm * ca.astype(F32), dm * cb.astype(F32)], axis=1)
        dya_ = (dm * sa.astype(F32)).astype(BF)
        dyb_ = (dm * sb.astype(F32)).astype(BF)
        return (dya_, dyb_, dgate, lax.dot_general(dya_, wa, _NT, preferred_element_type=F32),
                lax.dot_general(dyb_, wb, _NT, preferred_element_type=F32))

    ffn_grads = (d_wg, d_wu, d_wd)
    (dya, dyb, dgates, dy_attn, dy_hgrn), got = _fmm(
        [dh1_b], [(0, wout, True)], [(a, D, first) for a in (sig_a, sig_b, dgate_a, dgate_b)], merge_bwd,
        [(BF, D, D, first), (BF, D, D, first), (BF, 2 * D, 2 * D, first), (BF, D, D, first), (F32, D, D, first)],
        m=t, n=D, tm=512, tn=D, name="d_merge", consts=[wba, wbh], comm=_pair_comm(ffn_grads))
    pair_ffn = [_pair_add(g, r, core, name="pair_add_ffn%d" % i) for i, (g, r) in enumerate(zip(ffn_grads, got))]
    d_wba = _wgrad(y_attn, dya, name="d_w_ba")
    d_wbh = _wgrad(y_hgrn, dyb, name="d_w_bh")
    sq_grads = (d_wba, d_wbh, d_wout)
    (dh4, d_logits, d_hgrn_norm), (parts_ffn, *got) = _hgrn_bwd(
        h4, logits, hgrn_norm_g, o_pre, states, dy_hgrn, t=t,
        comm=_both(_chip_comm(pair_ffn), _pair_comm(sq_grads)))
    pair_sq = [_pair_add(g, r, core, name="pair_add_sq%d" % i) for i, (g, r) in enumerate(zip(sq_grads, got))]
    (dq, dkv, d_sinks), (parts_sq,) = _attn_bwd(q, kv, sinks, dy_attn, t=t, comm=_chip_comm(pair_sq))
    dps = (dq, dkv, dh4, dgates)
    d_win_t, d_b_in = _inproj_bwd_w(dps, u1, t=t)
    half0, got_in = _inproj_bwd_x(dps, win_t, x, norm_mix_g, dh1, t=t, part=0, comm=_pair_comm([d_win_t]))
    pair_in = _pair_add(d_win_t, got_in[0], core, name="pair_add_w_in")
    (grad_x, d_norm_mix), (parts_in,) = _inproj_bwd_x(dps, win_t, x, norm_mix_g, dh1, t=t, part=1, prev=half0,
                                                      comm=_chip_comm([pair_in]))

    small_grads = (d_norm_mix, d_b_in, d_sinks, d_logits, d_hgrn_norm, d_norm_ffn, d_norm_final)
    return loss_row, grad_x, (parts_in, parts_ffn, parts_sq), small_grads


def kernel(x, norm_mix_g, w_in, b_in, attn_sinks, hgrn_lb_logits, hgrn_norm_g, w_branch_attn, w_branch_hgrn, w_out, norm_ffn_g, w_ffn_gate, w_ffn_up, w_ffn_down, norm_final_g, loss_target, m_norm_mix_g, m_w_in, m_b_in, m_attn_sinks, m_hgrn_lb_logits, m_hgrn_norm_g, m_w_branch_attn, m_w_branch_hgrn, m_w_out, m_norm_ffn_g, m_w_ffn_gate, m_w_ffn_up, m_w_ffn_down, m_norm_final_g, v_norm_mix_g, v_w_in, v_b_in, v_attn_sinks, v_hgrn_lb_logits, v_hgrn_norm_g, v_w_branch_attn, v_w_branch_hgrn, v_w_out, v_norm_ffn_g, v_w_ffn_gate, v_w_ffn_up, v_w_ffn_down, v_norm_final_g):
    shards = [w_in[0].T.astype(BF), w_ffn_gate[0].T.astype(BF), w_ffn_up[0].T.astype(BF),
              w_ffn_down[0].astype(BF), w_branch_attn[0].astype(BF), w_branch_hgrn[0].astype(BF),
              w_out[0].astype(BF)]
    loss_row, grad_x, grad_parts, small_grads = _step(
        x[0], loss_target[0], shards, norm_mix_g, b_in, attn_sinks, hgrn_lb_logits, hgrn_norm_g,
        norm_ffn_g, norm_final_g.reshape(1, D))

    d_norm_mix, d_b_in, d_sinks, d_logits, d_hgrn_norm, d_norm_ffn, d_norm_final = small_grads
    row = lambda a: a.reshape(1, D)
    loss_out, small = _small_allreduce_adam(
        dict(norm_mix_g=d_norm_mix, hgrn_norm_g=d_hgrn_norm, norm_ffn_g=d_norm_ffn, norm_final_g=d_norm_final,
             hgrn_lb_logits=d_logits, attn_sinks=d_sinks, b_in=d_b_in),
        loss_row,
        dict(norm_mix_g=(norm_mix_g, m_norm_mix_g, v_norm_mix_g), hgrn_norm_g=(hgrn_norm_g, m_hgrn_norm_g, v_hgrn_norm_g),
             norm_ffn_g=(norm_ffn_g, m_norm_ffn_g, v_norm_ffn_g),
             norm_final_g=(row(norm_final_g), row(m_norm_final_g), row(v_norm_final_g)),
             hgrn_lb_logits=(hgrn_lb_logits, m_hgrn_lb_logits, v_hgrn_lb_logits),
             attn_sinks=(attn_sinks, m_attn_sinks, v_attn_sinks), b_in=(b_in, m_b_in, v_b_in)))
    small["norm_final_g"] = [a.reshape(D) for a in small["norm_final_g"]]
    loss = loss_out[0, 0]

    names = ["w_in", "w_ffn_gate", "w_ffn_up", "w_ffn_down", "w_branch_attn", "w_branch_hgrn", "w_out"]
    w_full = dict(w_in=(w_in, m_w_in, v_w_in), w_ffn_gate=(w_ffn_gate, m_w_ffn_gate, v_w_ffn_gate),
                  w_ffn_up=(w_ffn_up, m_w_ffn_up, v_w_ffn_up), w_ffn_down=(w_ffn_down, m_w_ffn_down, v_w_ffn_down),
                  w_branch_attn=(w_branch_attn, m_w_branch_attn, v_w_branch_attn),
                  w_branch_hgrn=(w_branch_hgrn, m_w_branch_hgrn, v_w_branch_hgrn),
                  w_out=(w_out, m_w_out, v_w_out))
    parts_in, parts_ffn, parts_sq = grad_parts
    where = [(parts_in, 0), (parts_ffn, 0), (parts_ffn, 1), (parts_ffn, 2), (parts_sq, 0), (parts_sq, 1), (parts_sq, 2)]
    big = {}
    for i, name in enumerate(names):
        view = (lambda a: a[0].T) if i < 3 else (lambda a: a[0])
        back = (lambda a: a.T[None]) if i < 3 else (lambda a: a[None])
        wv, mv, vv = w_full[name]
        res = _adam(view(wv), where[i][0], where[i][1], view(mv), view(vv), name="adam_" + name)
        big[name] = [back(a) for a in res]

    order = ["norm_mix_g", "w_in", "b_in", "attn_sinks", "hgrn_lb_logits", "hgrn_norm_g", "w_branch_attn",
             "w_branch_hgrn", "w_out", "norm_ffn_g", "w_ffn_gate", "w_ffn_up", "w_ffn_down", "norm_final_g"]
    outs = [loss, grad_x[None]]
    for kind in range(4):
        for name in order:
            outs.append(big[name][kind] if name in big else small[name][kind])
    return tuple(outs)
```

```python
import math

import jax
import jax.numpy as jnp
from jax import lax
from jax.experimental import pallas as pl
from jax.experimental.pallas import tpu as pltpu

F32 = jnp.float32
BF = jnp.bfloat16
MESH = pl.DeviceIdType.MESH

D = 1024
HEAD = 64
N_PAIR = 8
BLK = 128
CH = 64
HG_SUB = 2
HG_HEADS = 8
HG_K = 128
FFN = 2816
IN_W = 7424
N_DEV = 8
N_CHIP = 4
EPS = 1e-6
NEG = -1e30
SCALE = 1.0 / math.sqrt(HEAD)
VMEM_LIMIT = 56 * 1024 * 1024
WT = 256

ADAM_LR, ADAM_B1, ADAM_B2, ADAM_EPS, ADAM_WD, ADAM_STEP = 0.001, 0.9, 0.999, 1e-08, 0.01, 10

SLAB_R = (IN_W // N_DEV, FFN // N_DEV, FFN // N_DEV, FFN // N_DEV, D // N_DEV, D // N_DEV, D // N_DEV)
SLAB_ROWS = sum(SLAB_R)
SLAB_OFF = tuple(sum(SLAB_R[:i]) for i in range(len(SLAB_R)))
N_W = len(SLAB_R)
GRP_OFF = (0, D // WT, (D + 256) // WT, (5 * D + 256) // WT)
GRP_N = (D // WT, 256 // WT, 4 * D // WT, 2 * D // WT)
SMALL_ROWS = 16


_NN = (((1,), (0,)), ((), ()))
_NT = (((1,), (1,)), ((), ()))
_TN = (((0,), (0,)), ((), ()))


def _pcall(body, **kw):
    return pl.pallas_call(body, **kw)


def _cp(sem=None, **kw):
    return pltpu.CompilerParams(dimension_semantics=sem, vmem_limit_bytes=VMEM_LIMIT, **kw)


def _sig(v):
    return 0.5 * jnp.tanh(0.5 * v) + 0.5


def _accum(ref, val, first):
    @pl.when(first)
    def _():
        ref[...] = val

    @pl.when(jnp.logical_not(first))
    def _():
        ref[...] += val


class _Comm:
    def __init__(self, ins, out_shapes, sem_shapes, phases):
        self.ins, self.out_shapes, self.sem_shapes, self.phases = list(ins), list(out_shapes), list(sem_shapes), phases


def _both(a, b):
    ni, no, ns = len(a.ins), len(a.out_shapes), len(a.sem_shapes)

    def of_a(fn):
        return lambda ins, outs, sems: fn(ins[:ni], outs[:no], sems[:ns])

    def of_b(fn):
        return lambda ins, outs, sems: fn(ins[ni:], outs[no:], sems[ns:])

    return _Comm(a.ins + b.ins, a.out_shapes + b.out_shapes, a.sem_shapes + b.sem_shapes,
                 [(f, of_a(fn)) for f, fn in a.phases] + [(f, of_b(fn)) for f, fn in b.phases])


def _host(body, comm, n_in, n_out, n_scr, nsteps, step_fn):
    if comm is None:
        return body
    ci, co = len(comm.ins), len(comm.out_shapes)

    def wrapped(*refs):
        p = 0
        ins, p = refs[p:p + n_in], p + n_in
        cins, p = refs[p:p + ci], p + ci
        outs, p = refs[p:p + n_out], p + n_out
        couts, p = refs[p:p + co], p + co
        scr, p = refs[p:p + n_scr], p + n_scr
        csems = refs[p:]
        step = step_fn()
        for frac, fn in comm.phases:
            if frac < 1.0:
                @pl.when(step == int(round(frac * (nsteps - 1))))
                def _(fn=fn):
                    fn(cins, couts, csems)
        body(*ins, *outs, *scr)
        for frac, fn in comm.phases:
            if frac >= 1.0:
                @pl.when(step == nsteps - 1)
                def _(fn=fn):
                    fn(cins, couts, csems)

    return wrapped


def _hosted_call(body, comm, args, *, name, grid, in_specs, out_specs, out_shape, scratch_shapes, sem,
                 nsteps, step_fn, aliases=None):
    n_in, n_out, n_scr = len(in_specs), len(out_specs), len(scratch_shapes)
    args = list(args)
    extra = {}
    if comm is not None:
        in_specs = list(in_specs) + [_hbm_spec()] * len(comm.ins)
        out_specs = list(out_specs) + [_hbm_spec()] * len(comm.out_shapes)
        out_shape = list(out_shape) + comm.out_shapes
        scratch_shapes = list(scratch_shapes) + comm.sem_shapes
        args += comm.ins
        extra = dict(has_side_effects=True)
    outs = _pcall(_host(body, comm, n_in, n_out, n_scr, nsteps, step_fn), name=name, grid=grid,
                  in_specs=in_specs, out_specs=out_specs, out_shape=out_shape, scratch_shapes=scratch_shapes,
                  input_output_aliases=aliases or {}, compiler_params=_cp(sem, **extra))(*args)
    return list(outs[:n_out]), list(outs[n_out:])


def _hbm_spec():
    return pl.BlockSpec(memory_space=pl.ANY)


def _wgrad(a, b, *, name):
    (t, m), n = a.shape, b.shape[1]

    def body(a_ref, b_ref, o_ref):
        o_ref[...] = lax.dot_general(a_ref[...], b_ref[...], _TN, preferred_element_type=F32).astype(BF)

    return _pcall(body, name=name, grid=(m // WT,),
                  in_specs=[pl.BlockSpec((t, WT), lambda i: (0, i)), pl.BlockSpec((t, n), lambda i: (0, 0))],
                  out_specs=pl.BlockSpec((WT, n), lambda i: (i, 0)),
                  out_shape=jax.ShapeDtypeStruct((m, n), BF), compiler_params=_cp(("parallel",)))(a, b)


def _fmm(lhs, rhs, extras, epilogue, outs, *, m, n, tm, tn, name, comm=None, vecs=(), consts=(), sums=()):
    tm, tn = min(tm, m), min(tn, n)
    assert m % tm == 0 and n % tn == 0 and (not sums or tn == n), (name, m, n, tm, tn)
    in_specs, args = [], []
    for a in lhs:
        in_specs.append(pl.BlockSpec((tm, a.shape[1]), lambda i, j: (i, 0)))
        args.append(a)
    for li, b, tb in rhs:
        k = lhs[li].shape[1]
        in_specs.append(pl.BlockSpec((tn, k), lambda i, j: (j, 0)) if tb
                        else pl.BlockSpec((k, tn), lambda i, j: (0, j)))
        args.append(b)
    for arr, w, col in extras:
        in_specs.append(pl.BlockSpec((tm, w), lambda i, j, col=col: (i, col(j))))
        args.append(arr)
    for vec in vecs:
        in_specs.append(pl.BlockSpec((1, tn), lambda i, j: (0, j)))
        args.append(vec)
    for whole in consts:
        in_specs.append(pl.BlockSpec(whole.shape, lambda i, j: (0, 0)))
        args.append(whole)
    out_specs = [pl.BlockSpec((tm, w), lambda i, j, col=col: (i, col(j))) for _, _, w, col in outs]
    out_shape = [jax.ShapeDtypeStruct((m, total), dt) for dt, total, _, _ in outs]
    for w in sums:
        out_specs.append(pl.BlockSpec((1, w), lambda i, j: (0, 0)))
        out_shape.append(jax.ShapeDtypeStruct((1, w), F32))
    nl, nr, ne, no = len(lhs), len(rhs), len(extras) + len(vecs) + len(consts), len(outs)

    def body(*refs):
        prods = []
        for r, (li, _, tb) in enumerate(rhs):
            prods.append(lax.dot_general(refs[li][...], refs[nl + r][...], _NT if tb else _NN,
                                         preferred_element_type=F32))
        vals = epilogue(prods, [ref[...] for ref in refs[nl + nr:nl + nr + ne]])
        o_refs = refs[nl + nr + ne:]
        for o_ref, v in zip(o_refs[:no], vals[:no]):
            o_ref[...] = v.astype(o_ref.dtype)
        for s_ref, v in zip(o_refs[no:], vals[no:]):
            _accum(s_ref, v, pl.program_id(0) == 0)

    gm, gn = m // tm, n // tn
    res, comm_res = _hosted_call(
        body, comm, args, name=name, grid=(gm, gn), in_specs=in_specs, out_specs=out_specs,
        out_shape=out_shape, scratch_shapes=[], sem=("arbitrary", "arbitrary"), nsteps=gm * gn,
        step_fn=lambda: pl.program_id(0) * gn + pl.program_id(1))
    return res if comm is None else (res, comm_res)


def _grp_of(i):
    return [jnp.logical_and(i >= GRP_OFF[g], i < GRP_OFF[g] + GRP_N[g]) for g in range(4)]


def _grp_idx(i, g):
    return jnp.clip(i - GRP_OFF[g], 0, GRP_N[g] - 1)


def _inproj_fwd(u, win_t, b_in, *, t, comm=None):
    tm = min(1024, t)
    n_row = t // tm
    n_chunks, h_first, g_first = 8, 2, 6
    sub = D // WT

    def w_block(l):
        return jnp.where(l == 0, GRP_OFF[0], jnp.where(l == 1, GRP_OFF[1], GRP_OFF[2] + sub * (l - h_first)))

    def body(u_ref, *rest):
        w_refs, b_refs, (q_ref, kv_ref, h3_ref, hf_ref, g_ref) = rest[:sub], rest[sub:2 * sub], rest[2 * sub:]
        l = pl.program_id(1)

        @pl.when(l == 1)
        def _():
            kv_ref[...] = (lax.dot_general(u_ref[...], w_refs[0][...], _NT, preferred_element_type=F32)
                           + b_refs[0][...]).astype(BF)

        is_hf = l == h_first + 1
        in_h3 = jnp.logical_and(jnp.logical_and(l >= h_first, l < g_first), jnp.logical_not(is_hf))
        for pred, o_ref in ((l == 0, q_ref), (in_h3, h3_ref), (is_hf, hf_ref), (l >= g_first, g_ref)):
            @pl.when(pred)
            def _(o_ref=o_ref):
                w = jnp.concatenate([w[...] for w in w_refs], axis=0)
                b = jnp.concatenate([b[...] for b in b_refs], axis=1)
                o_ref[...] = (lax.dot_general(u_ref[...], w, _NT, preferred_element_type=F32) + b).astype(o_ref.dtype)

    return _hosted_call(
        body, comm, [u] + [win_t] * sub + [b_in] * sub, name="inproj_fwd", grid=(n_row, n_chunks),
        in_specs=[pl.BlockSpec((tm, D), lambda i, l: (i, 0))]
        + [pl.BlockSpec((WT, D), lambda i, l, o=o: (w_block(l) + o, 0)) for o in range(sub)]
        + [pl.BlockSpec((1, WT), lambda i, l, o=o: (0, w_block(l) + o)) for o in range(sub)],
        out_specs=[pl.BlockSpec((tm, D), lambda i, l: (i, 0)),
                   pl.BlockSpec((tm, 256), lambda i, l: (i, 0)),
                   pl.BlockSpec((tm, D), lambda i, l: (i, jnp.clip(l - h_first - 1, 0, 2))),
                   pl.BlockSpec((tm, D), lambda i, l: (i, 0)),
                   pl.BlockSpec((tm, D), lambda i, l: (i, jnp.clip(l - g_first, 0, 1)))],
        out_shape=[jax.ShapeDtypeStruct((t, D), BF), jax.ShapeDtypeStruct((t, 256), BF),
                   jax.ShapeDtypeStruct((t, 3 * D), BF), jax.ShapeDtypeStruct((t, D), F32),
                   jax.ShapeDtypeStruct((t, 2 * D), BF)],
        scratch_shapes=[], sem=("arbitrary", "arbitrary"), nsteps=n_row * n_chunks,
        step_fn=lambda: pl.program_id(0) * n_chunks + pl.program_id(1))


def _inproj_bwd_x(dps, win_t, x, g, resid, *, t, part, prev=None, comm=None):
    n_row = 8 if t >= 4096 else 4
    tm = t // n_row
    first = n_row // 4
    per = first if part == 0 else n_row - first
    row = lambda i: part * first + i

    n_chunks = 4
    sub = 2 * D // WT

    def w_block(l):
        return jnp.where(l == 0, 0, GRP_OFF[2] + sub * (l - 1))

    def body(d0, d1, d2, d3, *rest):
        w_refs, (x_ref, g_ref, r_ref) = rest[:sub], rest[sub:sub + 3]
        dg_prev = rest[sub + 3] if prev is not None else None
        o_ref, dg_ref, acc_ref = rest[-3], rest[-2], rest[-1]
        i, l = pl.program_id(0), pl.program_id(1)

        @pl.when(l == 0)
        def _():
            wq = jnp.concatenate([w[...] for w in w_refs[:GRP_N[0]]], axis=0)
            acc_ref[...] = (jnp.dot(d0[...], wq, preferred_element_type=F32)
                            + jnp.dot(d1[...], w_refs[GRP_N[0]][...], preferred_element_type=F32))

        for pred, d_ref in ((jnp.logical_and(l >= 1, l < 3), d2), (l == 3, d3)):
            @pl.when(pred)
            def _(d_ref=d_ref):
                w = jnp.concatenate([w[...] for w in w_refs], axis=0)
                acc_ref[...] += jnp.dot(d_ref[...], w, preferred_element_type=F32)

        @pl.when(l == n_chunks - 1)
        def _():
            xv = x_ref[...]
            r = lax.rsqrt(jnp.mean(xv * xv, axis=-1, keepdims=True) + EPS)
            xh = xv * r
            du = acc_ref[...]
            dxh = du * g_ref[...]
            o_ref[...] = r_ref[...] + r * (dxh - xh * jnp.mean(dxh * xh, axis=-1, keepdims=True))
            dg = jnp.sum(du * xh, axis=0, keepdims=True)
            if dg_prev is not None:
                dg = dg + jnp.where(i == 0, 1.0, 0.0) * dg_prev[...]
            _accum(dg_ref, dg, i == 0)

    rows = lambda w: pl.BlockSpec((tm, w), lambda i, l: (row(i), 0))
    in_specs = ([rows(D), rows(256),
                 pl.BlockSpec((tm, 2 * D), lambda i, l: (row(i), jnp.clip(l - 1, 0, 1))), rows(2 * D)]
                + [pl.BlockSpec((WT, D), lambda i, l, o=o: (w_block(l) + o, 0)) for o in range(sub)]
                + [rows(D), pl.BlockSpec((1, D), lambda i, l: (0, 0)), rows(D)])
    args = list(dps) + [win_t] * sub + [x, g, resid]
    aliases = None
    if prev is not None:
        in_specs += [pl.BlockSpec((1, D), lambda i, l: (0, 0)), _hbm_spec()]
        args += [prev[1], prev[0]]
        aliases = {len(args) - 1: 0}
    return _hosted_call(
        body, comm, args, name="inproj_bwd_x%d" % part, grid=(per, n_chunks), in_specs=in_specs,
        out_specs=[rows(D), pl.BlockSpec((1, D), lambda i, l: (0, 0))],
        out_shape=[jax.ShapeDtypeStruct((t, D), F32), jax.ShapeDtypeStruct((1, D), F32)],
        scratch_shapes=[pltpu.VMEM((tm, D), F32)], sem=("arbitrary", "arbitrary"), nsteps=per * n_chunks,
        step_fn=lambda: pl.program_id(0) * n_chunks + pl.program_id(1), aliases=aliases)


def _inproj_bwd_w(dps, u, *, t):
    n_tiles = IN_W // WT
    dims = (((0,), (0,)), ((), ()))

    def body(d0, d1, d2, d3, u_ref, o_ref, db_ref):
        i = pl.program_id(0)
        uv = u_ref[...]
        for g, (pred, d_ref) in enumerate(zip(_grp_of(i), (d0, d1, d2, d3))):
            @pl.when(pred)
            def _(d_ref=d_ref):
                dv = d_ref[...]
                o_ref[...] = lax.dot_general(dv, uv, dims, preferred_element_type=F32).astype(BF)
                db_ref[...] = jnp.sum(dv.astype(F32), axis=0, keepdims=True)

    return _pcall(body, name="inproj_bwd_w", grid=(n_tiles,),
                  in_specs=[pl.BlockSpec((t, WT), lambda i, g=g: (0, _grp_idx(i, g))) for g in range(4)]
                  + [pl.BlockSpec((t, D), lambda i: (0, 0))],
                  out_specs=[pl.BlockSpec((WT, D), lambda i: (i, 0)),
                             pl.BlockSpec((1, WT), lambda i: (0, i))],
                  out_shape=[jax.ShapeDtypeStruct((IN_W, D), BF), jax.ShapeDtypeStruct((1, IN_W), F32)],
                  compiler_params=_cp(("arbitrary",)))(*dps, u)


def _row_spec(tm, width, col=0):
    return pl.BlockSpec((tm, width), lambda i: (i, col))


def _vec_spec(width):
    return pl.BlockSpec((1, width), lambda i: (0, 0))


def _rms_fwd(x, g, *, tm, name, comm=None):
    t = x.shape[0]
    tm = min(tm, t)

    def body(x_ref, g_ref, u_ref):
        xv = x_ref[...]
        r = lax.rsqrt(jnp.mean(xv * xv, axis=-1, keepdims=True) + EPS)
        u_ref[...] = (xv * r * g_ref[...]).astype(BF)

    (u,), comm_res = _hosted_call(
        body, comm, (x, g), name=name, grid=(t // tm,), in_specs=[_row_spec(tm, D), _vec_spec(D)],
        out_specs=[_row_spec(tm, D)], out_shape=[jax.ShapeDtypeStruct((t, D), BF)], scratch_shapes=[],
        sem=("arbitrary",), nsteps=t // tm, step_fn=lambda: pl.program_id(0))
    return u if comm is None else (u, comm_res)


def _rms_bwd(du, x, g, resid, *, tm, name):
    t = x.shape[0]
    tm = min(tm, t)

    def body(du_ref, x_ref, g_ref, r_ref, dx_ref, dxb_ref, dg_ref):
        xv = x_ref[...]
        r = lax.rsqrt(jnp.mean(xv * xv, axis=-1, keepdims=True) + EPS)
        xh = xv * r
        duv = du_ref[...]
        dxh = duv * g_ref[...]
        dx = r_ref[...] + r * (dxh - xh * jnp.mean(dxh * xh, axis=-1, keepdims=True))
        dx_ref[...] = dx
        dxb_ref[...] = dx.astype(BF)
        _accum(dg_ref, jnp.sum(duv * xh, axis=0, keepdims=True), pl.program_id(0) == 0)

    return _pcall(body, name=name, grid=(t // tm,),
                  in_specs=[_row_spec(tm, D), _row_spec(tm, D), _vec_spec(D), _row_spec(tm, D)],
                  out_specs=[_row_spec(tm, D), _row_spec(tm, D), _vec_spec(D)],
                  out_shape=[jax.ShapeDtypeStruct((t, D), F32), jax.ShapeDtypeStruct((t, D), BF),
                             jax.ShapeDtypeStruct((1, D), F32)],
                  compiler_params=_cp(("arbitrary",)))(du, x, g, resid)


def _attn_kv_tiles(kprev, kcur):
    kv = jnp.concatenate([kprev, kcur], axis=0).astype(F32)
    lo = lax.broadcasted_iota(jnp.int32, (2 * BLK, 128), 1) < HEAD
    tiles = []
    for part in (kv[:, 0:128], kv[:, 128:256]):
        rolled = pltpu.roll(part, HEAD, 1)
        z = jnp.zeros_like(part)
        tiles.append(((jnp.where(lo, part, z).astype(BF), jnp.where(lo, z, rolled).astype(BF)),
                      (jnp.where(lo, rolled, z).astype(BF), jnp.where(lo, z, part).astype(BF))))
    k_t, v_t = tiles
    return [(jnp.concatenate(k_t[h], axis=0), jnp.concatenate(v_t[h], axis=0)) for h in range(2)]


def _attn_mask(i):
    qi = lax.broadcasted_iota(jnp.int32, (BLK, 2 * BLK), 0)
    kj = lax.broadcasted_iota(jnp.int32, (BLK, 2 * BLK), 1)
    first_key = jnp.where(i == 0, BLK, 0)
    in_prev = jnp.logical_and(jnp.logical_and(kj < BLK, kj > qi), kj >= first_key)
    in_cur = jnp.logical_and(kj >= BLK, kj - BLK <= qi)
    return jnp.logical_or(in_prev, in_cur)


def _attn_probs(s, sink, valid):
    s = jnp.where(valid, s * SCALE, NEG)
    mx = jnp.maximum(jnp.max(s, axis=-1, keepdims=True), sink)
    e = jnp.exp(s - mx)
    es = jnp.exp(sink - mx)
    inv = 1.0 / (jnp.sum(e, axis=-1, keepdims=True) + es)
    return e * inv, es * inv


_KEYS = 2 * BLK


def _pair(ref, j):
    return ref[:, j * 128:(j + 1) * 128]


def _attn_fwd(q, kv, sinks, *, t, comm=None):
    nb = t // BLK

    def body(sink_ref, q_ref, kp_ref, kc_ref, o_ref):
        valid = _attn_mask(pl.program_id(0))
        tiles = _attn_kv_tiles(kp_ref[...], kc_ref[...])
        s = [lax.dot_general(_pair(q_ref, j), tiles[j // 4][0], _NT, preferred_element_type=F32)
             for j in range(N_PAIR)]
        p = []
        for j in range(N_PAIR):
            pe, _ = _attn_probs(s[j][:, 0:_KEYS], sink_ref[0, 2 * j], valid)
            po, _ = _attn_probs(s[j][:, _KEYS:2 * _KEYS], sink_ref[0, 2 * j + 1], valid)
            p.append(jnp.concatenate([pe.astype(BF), po.astype(BF)], axis=1))
        for j in range(N_PAIR):
            o_ref[:, j * 128:(j + 1) * 128] = jnp.dot(p[j], tiles[j // 4][1],
                                                      preferred_element_type=F32).astype(BF)

    return _hosted_call(
        body, comm, (sinks, q, kv, kv), name="attn_fwd", grid=(nb,),
        in_specs=[pl.BlockSpec(memory_space=pltpu.SMEM),
                  pl.BlockSpec((BLK, D), lambda i: (i, 0)),
                  pl.BlockSpec((BLK, 256), lambda i: (jnp.maximum(i - 1, 0), 0)),
                  pl.BlockSpec((BLK, 256), lambda i: (i, 0))],
        out_specs=[pl.BlockSpec((BLK, D), lambda i: (i, 0))],
        out_shape=[jax.ShapeDtypeStruct((t, D), BF)],
        scratch_shapes=[], sem=("arbitrary",), nsteps=nb, step_fn=lambda: pl.program_id(0))


def _attn_bwd(q, kv, sinks, do, *, t, comm=None):
    nb = t // BLK
    last = nb - 1

    def body(sink_ref, q_ref, kp_ref, kc_ref, do_ref, dq_ref, dkv_ref, ds_ref, carry_ref):
        i = pl.program_id(0)

        @pl.when(i == 0)
        def _():
            ds_ref[...] = jnp.zeros_like(ds_ref)
            carry_ref[...] = jnp.zeros_like(carry_ref)

        @pl.when(i < nb)
        def _():
            valid = _attn_mask(i)
            tiles = _attn_kv_tiles(kp_ref[...], kc_ref[...])
            lane1 = lax.broadcasted_iota(jnp.int32, (1, 128), 1)
            dsink = jnp.zeros((1, 128), F32)
            s = [lax.dot_general(_pair(q_ref, j), tiles[j // 4][0], _NT, preferred_element_type=F32)
                 for j in range(N_PAIR)]
            dp = [lax.dot_general(_pair(do_ref, j), tiles[j // 4][1], _NT, preferred_element_type=F32)
                  for j in range(N_PAIR)]
            p_all, ds_all = [], []
            for j in range(N_PAIR):
                halves = []
                for par in range(2):
                    cols = slice(par * _KEYS, (par + 1) * _KEYS)
                    p, ps = _attn_probs(s[j][:, cols], sink_ref[0, 2 * j + par], valid)
                    dpj = dp[j][:, cols]
                    dd = jnp.sum(p * dpj, axis=-1, keepdims=True)
                    dsink = dsink + jnp.where(lane1 == 2 * j + par,
                                              -jnp.sum(ps * dd, axis=0, keepdims=True), 0.0)
                    halves.append((p.astype(BF), (p * (dpj - dd)).astype(BF)))
                p_all.append(jnp.concatenate([halves[0][0], halves[1][0]], axis=1))
                ds_all.append(jnp.concatenate([halves[0][1], halves[1][1]], axis=1))
            for j in range(N_PAIR):
                dq_ref[:, j * 128:(j + 1) * 128] = (
                    jnp.dot(ds_all[j], tiles[j // 4][0], preferred_element_type=F32) * SCALE).astype(BF)
            ds_ref[...] += dsink
            gk, gv = [], []
            for h in range(2):
                grp = range(4 * h, 4 * h + 4)
                q_rows = jnp.concatenate([_pair(q_ref, j) for j in grp], axis=0)
                do_rows = jnp.concatenate([_pair(do_ref, j) for j in grp], axis=0)
                g_k = lax.dot_general(jnp.concatenate([ds_all[j] for j in grp], axis=0), q_rows, _TN,
                                      preferred_element_type=F32)
                g_v = lax.dot_general(jnp.concatenate([p_all[j] for j in grp], axis=0), do_rows, _TN,
                                      preferred_element_type=F32)
                gk.append((g_k[0:_KEYS], g_k[_KEYS:2 * _KEYS]))
                gv.append((g_v[0:_KEYS], g_v[_KEYS:2 * _KEYS]))
            lo = lax.broadcasted_iota(jnp.int32, (2 * BLK, 128), 1) < HEAD
            zero = jnp.zeros((2 * BLK, 128), F32)

            def unpad(g):
                return (jnp.where(lo, g[0][0] + pltpu.roll(g[0][1], HEAD, 1), zero)
                        + jnp.where(lo, zero, pltpu.roll(g[1][0], HEAD, 1) + g[1][1]))

            dk = unpad(gk) * SCALE
            dv = unpad(gv)
            dkv_ref[:, 0:128] = (carry_ref[:, 0:128] + dk[0:BLK]).astype(BF)
            dkv_ref[:, 128:256] = (carry_ref[:, 128:256] + dv[0:BLK]).astype(BF)
            carry_ref[:, 0:128] = dk[BLK:2 * BLK]
            carry_ref[:, 128:256] = dv[BLK:2 * BLK]

        @pl.when(i == nb)
        def _():
            dkv_ref[...] = carry_ref[...].astype(BF)

    return _hosted_call(
        body, comm, (sinks, q, kv, kv, do), name="attn_bwd", grid=(nb + 1,),
        in_specs=[pl.BlockSpec(memory_space=pltpu.SMEM),
                  pl.BlockSpec((BLK, D), lambda i: (jnp.minimum(i, last), 0)),
                  pl.BlockSpec((BLK, 256), lambda i: (jnp.clip(i - 1, 0, last), 0)),
                  pl.BlockSpec((BLK, 256), lambda i: (jnp.minimum(i, last), 0)),
                  pl.BlockSpec((BLK, D), lambda i: (jnp.minimum(i, last), 0))],
        out_specs=[pl.BlockSpec((BLK, D), lambda i: (jnp.minimum(i, last), 0)),
                   pl.BlockSpec((BLK, 256), lambda i: (jnp.maximum(i - 1, 0), 0)),
                   pl.BlockSpec((1, 128), lambda i: (0, 0))],
        out_shape=[jax.ShapeDtypeStruct((t, D), BF), jax.ShapeDtypeStruct((t, 256), BF),
                   jax.ShapeDtypeStruct((1, 128), F32)],
        scratch_shapes=[pltpu.VMEM((BLK, 256), F32)], sem=("arbitrary",), nsteps=nb + 1,
        step_fn=lambda: pl.program_id(0))


def _split3(v):
    h = v.astype(BF)
    r = v - h.astype(F32)
    m = r.astype(BF)
    lo = (r - m.astype(F32)).astype(BF)
    return jnp.concatenate([h, m, lo], axis=1)


def _apply01(mat, v):
    n = v.shape[1]
    r = jnp.dot(mat, _split3(v), preferred_element_type=F32)
    return r[:, 0:n] + r[:, n:2 * n] + r[:, 2 * n:3 * n]


def _hgrn_gates(hq, hf, lb):
    sq = _sig(hq)
    sg = _sig(hf)
    f = lb + (1.0 - lb) * sg
    return hq * sq, (1.0 - lb) * (1.0 - sg), jnp.log(f), sq, sg, f


def _tri(upper):
    r = lax.broadcasted_iota(jnp.int32, (CH, CH), 0)
    c = lax.broadcasted_iota(jnp.int32, (CH, CH), 1)
    return (c >= r) if upper else (c <= r)


def _lb_from_logits(lg_ref):
    return 1.0 / (1.0 + jnp.exp(lg_ref[1:2, :] - lg_ref[0:1, :]))


def _hgrn_fwd(h3, hf, logits, norm_g, *, t, comm=None):
    nc = t // CH
    nt_dims = (((1,), (1,)), ((), ()))
    tn_dims = (((0,), (0,)), ((), ()))

    def body(h_ref, hf_ref, lg_ref, ng_ref, y_ref, o_ref, st_ref, s_scr, b_scr, qa_s, ka_s, qb_s, kb_s, v_s):
        @pl.when(pl.program_id(0) == 0)
        def _():
            s_scr[...] = jnp.zeros_like(s_scr)

        heads = [slice(h * HG_K, (h + 1) * HG_K) for h in range(HG_HEADS)]
        causal = _tri(False)
        lb = _lb_from_logits(lg_ref)
        for c in range(HG_SUB):
            rows = slice(c * CH, (c + 1) * CH)
            q, k, g, _, _, _ = _hgrn_gates(h_ref[rows, 0:D].astype(F32), hf_ref[rows, :], lb)
            b_scr[...] = _apply01(jnp.where(causal, 1.0, 0.0).astype(BF), g)
            b = b_scr[...]
            b_mid = b_scr[CH // 2 - 1:CH // 2, :]
            b_last = b_scr[CH - 1:CH, :]
            qa_s[...] = (q * jnp.exp(b - b_mid)).astype(BF)
            ka_s[...] = (k * jnp.exp(b_mid - b)).astype(BF)
            qb_s[...] = (q * jnp.exp(b)).astype(BF)
            kb_s[...] = (k * jnp.exp(b_last - b)).astype(BF)
            v_s[...] = h_ref[rows, D:2 * D]
            dec = jnp.exp(b_last)
            st_ref[c] = s_scr[...].astype(BF)
            a = [jnp.where(causal, lax.dot_general(qa_s[:, sl], ka_s[:, sl], nt_dims, preferred_element_type=F32),
                           0.0).astype(BF) for sl in heads]
            for h, sl in enumerate(heads):
                o_ref[rows, sl] = (jnp.dot(a[h], v_s[:, sl], preferred_element_type=F32)
                                   + lax.dot_general(qb_s[:, sl], s_scr[h].astype(BF), nt_dims,
                                                     preferred_element_type=F32))
            for h, sl in enumerate(heads):
                s_scr[h] = dec[:, sl] * s_scr[h] + lax.dot_general(v_s[:, sl], kb_s[:, sl], tn_dims,
                                                                   preferred_element_type=F32)
            for h, sl in enumerate(heads):
                o = o_ref[rows, sl]
                on = o * lax.rsqrt(jnp.mean(o * o, axis=-1, keepdims=True) + EPS)
                gate = _sig(h_ref[rows, 2 * D + h * HG_K:2 * D + (h + 1) * HG_K].astype(F32))
                y_ref[rows, sl] = (on * ng_ref[:, sl] * gate).astype(BF)

    half = lambda: pltpu.VMEM((CH, D), BF)
    blk = HG_SUB * CH
    return _hosted_call(
        body, comm, (h3, hf, logits, norm_g), name="hgrn_fwd", grid=(nc // HG_SUB,),
        in_specs=[pl.BlockSpec((blk, 3 * D), lambda n: (n, 0)),
                  pl.BlockSpec((blk, D), lambda n: (n, 0)),
                  pl.BlockSpec((2, D), lambda n: (0, 0)),
                  pl.BlockSpec((1, D), lambda n: (0, 0))],
        out_specs=[pl.BlockSpec((blk, D), lambda n: (n, 0)),
                   pl.BlockSpec((blk, D), lambda n: (n, 0)),
                   pl.BlockSpec((HG_SUB, HG_HEADS, HG_K, HG_K), lambda n: (n, 0, 0, 0))],
        out_shape=[jax.ShapeDtypeStruct((t, D), BF), jax.ShapeDtypeStruct((t, D), F32),
                   jax.ShapeDtypeStruct((nc, HG_HEADS, HG_K, HG_K), BF)],
        scratch_shapes=[pltpu.VMEM((HG_HEADS, HG_K, HG_K), F32), pltpu.VMEM((CH, D), F32),
                        half(), half(), half(), half(), half()],
        sem=("arbitrary",), nsteps=nc // HG_SUB, step_fn=lambda: pl.program_id(0))


def _hgrn_bwd(h3, hf, logits, norm_g, o_pre, states, dy, *, t, comm=None):
    nc = t // CH
    nt_dims = (((1,), (1,)), ((), ()))
    tn_dims = (((0,), (0,)), ((), ()))

    def body(h_ref, hf_ref, lg_ref, ng_ref, o_ref, st_ref, dy_ref, dh_ref, dlg_ref, dng_ref, ds_scr, dlb_scr,
             b_scr, tail_s, e_qa, e_ka, e_qb, e_kb, q_s, k_s, dqa_s, dka_s, dqb_s, dkb_s,
             qa_s, ka_s, qb_s, kb_s, v_s, do_s):
        n = pl.program_id(0)

        @pl.when(n == 0)
        def _():
            ds_scr[...] = jnp.zeros_like(ds_scr)
            dlb_scr[...] = jnp.zeros_like(dlb_scr)
            dng_ref[...] = jnp.zeros_like(dng_ref)

        heads = [slice(h * HG_K, (h + 1) * HG_K) for h in range(HG_HEADS)]
        lb = _lb_from_logits(lg_ref)
        causal = _tri(False)

        def chunk(c):
            rows = slice(c * CH, (c + 1) * CH)
            hq = h_ref[rows, 0:D].astype(F32)
            q, k, g, sq, sg, f = _hgrn_gates(hq, hf_ref[rows, :], lb)
            b_scr[...] = _apply01(jnp.where(causal, 1.0, 0.0).astype(BF), g)
            b = b_scr[...]
            b_mid = b_scr[CH // 2 - 1:CH // 2, :]
            b_last = b_scr[CH - 1:CH, :]
            q_s[...] = q
            k_s[...] = k
            for e_ref, s_ref, base, expo in ((e_qa, qa_s, q, b - b_mid), (e_ka, ka_s, k, b_mid - b),
                                             (e_qb, qb_s, q, b), (e_kb, kb_s, k, b_last - b)):
                e = jnp.exp(expo)
                e_ref[...] = e
                s_ref[...] = (base * e).astype(BF)
            v_s[...] = h_ref[rows, D:2 * D]
            dec = jnp.exp(b_last)
            for h, sl in enumerate(heads):
                gcol = slice(3 * D + h * HG_K, 3 * D + (h + 1) * HG_K)
                ngh = ng_ref[:, sl]
                sgate = _sig(h_ref[rows, 2 * D + h * HG_K:2 * D + (h + 1) * HG_K].astype(F32))
                o = o_ref[rows, sl]
                r = lax.rsqrt(jnp.mean(o * o, axis=-1, keepdims=True) + EPS)
                on = o * r
                dyh = dy_ref[rows, sl]
                dh_ref[rows, gcol] = (dyh * on * ngh * sgate * (1.0 - sgate)).astype(BF)
                dng_ref[:, sl] += jnp.sum(dyh * on * sgate, axis=0, keepdims=True)
                don = dyh * ngh * sgate
                do_s[:, sl] = (r * (don - on * jnp.mean(don * on, axis=-1, keepdims=True))).astype(BF)
            a = [jnp.where(causal, lax.dot_general(qa_s[:, sl], ka_s[:, sl], nt_dims, preferred_element_type=F32),
                           0.0).astype(BF) for sl in heads]
            da = [jnp.where(causal, lax.dot_general(do_s[:, sl], v_s[:, sl], nt_dims, preferred_element_type=F32),
                            0.0).astype(BF) for sl in heads]
            for h, sl in enumerate(heads):
                dh_ref[rows, 2 * D + h * HG_K:2 * D + (h + 1) * HG_K] = (
                    lax.dot_general(a[h], do_s[:, sl], tn_dims, preferred_element_type=F32)
                    + lax.dot_general(kb_s[:, sl], ds_scr[h].astype(BF), nt_dims, preferred_element_type=F32)
                ).astype(BF)
            for h, sl in enumerate(heads):
                dqa_s[:, sl] = jnp.dot(da[h], ka_s[:, sl], preferred_element_type=F32)
            for h, sl in enumerate(heads):
                dka_s[:, sl] = lax.dot_general(da[h], qa_s[:, sl], tn_dims, preferred_element_type=F32)
            for h, sl in enumerate(heads):
                dqb_s[:, sl] = jnp.dot(do_s[:, sl], st_ref[c, h], preferred_element_type=F32)
            for h, sl in enumerate(heads):
                dkb_s[:, sl] = jnp.dot(v_s[:, sl], ds_scr[h].astype(BF), preferred_element_type=F32)
            for h, sl in enumerate(heads):
                tail_s[:, sl] = jnp.sum(dec[:, sl] * st_ref[c, h].astype(F32) * ds_scr[h], axis=0, keepdims=True)
            for h, sl in enumerate(heads):
                ds_scr[h] = (lax.dot_general(do_s[:, sl], qb_s[:, sl], tn_dims, preferred_element_type=F32)
                             + dec[:, sl] * ds_scr[h])
            qv, kv = q_s[...], k_s[...]
            dqa, dka, dqb, dkb = dqa_s[...], dka_s[...], dqb_s[...], dkb_s[...]
            eqa, eka, eqb, ekb = e_qa[...], e_ka[...], e_qb[...], e_kb[...]
            dkb_kb = dkb * (kv * ekb)
            db_last = jnp.sum(dkb_kb, axis=0, keepdims=True) + tail_s[...]
            last_row = lax.broadcasted_iota(jnp.int32, (CH, D), 0) == CH - 1
            db = (dqa * (qv * eqa) - dka * (kv * eka) + dqb * (qv * eqb) - dkb_kb
                  + jnp.where(last_row, db_last, 0.0))
            dg = _apply01(jnp.where(_tri(True), 1.0, 0.0).astype(BF), db)
            dq = dqa * eqa + dqb * eqb
            dk = dka * eka + dkb * ekb
            dh_ref[rows, 0:D] = (dq * sq * (1.0 + hq * (1.0 - sq))).astype(BF)
            dfk = dg / f - dk
            dh_ref[rows, D:2 * D] = ((1.0 - lb) * dfk * sg * (1.0 - sg)).astype(BF)
            dlb_scr[...] += jnp.sum((1.0 - sg) * dfk, axis=0, keepdims=True)

        for c in reversed(range(HG_SUB)):
            chunk(c)

        @pl.when(n == nc // HG_SUB - 1)
        def _():
            dl0 = dlb_scr[...] * lb * (1.0 - lb)
            dlg_ref[0:1, :] = dl0
            dlg_ref[1:2, :] = -dl0

    steps = nc // HG_SUB
    blk = HG_SUB * CH
    rev = lambda n: (steps - 1 - n, 0)
    return _hosted_call(
        body, comm, (h3, hf, logits, norm_g, o_pre, states, dy), name="hgrn_bwd", grid=(steps,),
        in_specs=[pl.BlockSpec((blk, 3 * D), rev),
                  pl.BlockSpec((blk, D), rev),
                  pl.BlockSpec((2, D), lambda n: (0, 0)),
                  pl.BlockSpec((1, D), lambda n: (0, 0)),
                  pl.BlockSpec((blk, D), rev),
                  pl.BlockSpec((HG_SUB, HG_HEADS, HG_K, HG_K), lambda n: (steps - 1 - n, 0, 0, 0)),
                  pl.BlockSpec((blk, D), rev)],
        out_specs=[pl.BlockSpec((blk, 4 * D), rev),
                   pl.BlockSpec((2, D), lambda n: (0, 0)),
                   pl.BlockSpec((1, D), lambda n: (0, 0))],
        out_shape=[jax.ShapeDtypeStruct((t, 4 * D), BF), jax.ShapeDtypeStruct((2, D), F32),
                   jax.ShapeDtypeStruct((1, D), F32)],
        scratch_shapes=([pltpu.VMEM((HG_HEADS, HG_K, HG_K), F32), pltpu.VMEM((1, D), F32),
                         pltpu.VMEM((CH, D), F32), pltpu.VMEM((1, D), F32)]
                        + [pltpu.VMEM((CH, D), F32)] * 10 + [pltpu.VMEM((CH, D), BF)] * 6),
        sem=("arbitrary",), nsteps=steps, step_fn=lambda: pl.program_id(0))


def _place():
    x, y, c = lax.axis_index("x"), lax.axis_index("y"), lax.axis_index("c")
    return x, y, c, [(1 - x, y), (x, 1 - y), (1 - x, 1 - y)]


def _gather_comm(shards, mids):
    n, pieces = len(shards), len(mids)
    r = [s.shape[0] for s in shards]
    tile = 16
    cut = [[(rw // tile * p // pieces) * tile for p in range(pieces + 1)] for rw in r]
    size = [[cut[w][p + 1] - cut[w][p] for p in range(pieces)] for w in range(n)]

    def tools(ins, outs, sems):
        send_sems, recv_sems, local_sems = sems
        x, y, c, chips = _place()
        me, sib = (x, y, c), (x, y, 1 - c)

        def rows(w, p, dev):
            return outs[w].at[pl.ds((4 * dev[0] + 2 * dev[1] + dev[2]) * r[w] + cut[w][p], size[w][p]), :]

        def copy(kind, w, p, block, to, own=False):
            src = ins[w].at[pl.ds(cut[w][p], size[w][p]), :] if own else rows(w, p, block)
            return pltpu.make_async_remote_copy(
                src_ref=src, dst_ref=rows(w, p, block), send_sem=send_sems.at[p, kind],
                recv_sem=recv_sems.at[p, kind], device_id=to, device_id_type=MESH)

        def all_of(kind, p):
            whole = outs[0].at[pl.ds(0, sum(size[w][p] for w in range(n))), :]
            return pltpu.make_async_remote_copy(
                src_ref=whole, dst_ref=whole, send_sem=send_sems.at[p, kind], recv_sem=recv_sems.at[p, kind],
                device_id=me, device_id_type=MESH)

        mine = [pltpu.make_async_copy(ins[w], outs[w].at[pl.ds((4 * x + 2 * y + c) * r[w], r[w]), :],
                                      local_sems.at[w]) for w in range(n)]
        return c, chips, me, sib, copy, all_of, mine

    def start(ins, outs, sems):
        c, chips, me, sib, copy, _, mine = tools(ins, outs, sems)
        for cp in mine:
            cp.start()
        for p in range(pieces):
            for w in range(n):
                copy(0, w, p, me, sib, own=True).start()
                for j, chip in enumerate(chips):
                    copy(1 + j, w, p, me, (*chip, c), own=True).start()

    def pass_on(p):
        def phase(ins, outs, sems):
            c, chips, _, sib, copy, all_of, _ = tools(ins, outs, sems)
            for j, chip in enumerate(chips):
                all_of(1 + j, p).wait_recv()
                for w in range(n):
                    copy(4 + j, w, p, (*chip, c), sib).start()
        return phase

    def finish(ins, outs, sems):
        _, _, _, _, _, all_of, mine = tools(ins, outs, sems)
        for p in range(pieces):
            all_of(0, p).wait_recv()
            for j in range(3):
                all_of(4 + j, p).wait_recv()
            for kind in range(7):
                all_of(kind, p).wait_send()
        for cp in mine:
            cp.wait()

    return _Comm(shards, [jax.ShapeDtypeStruct((N_DEV * rw, D), BF) for rw in r],
                 [pltpu.SemaphoreType.DMA((pieces, 7)), pltpu.SemaphoreType.DMA((pieces, 7)),
                  pltpu.SemaphoreType.DMA((n,))],
                 [(0.0, start)] + [(f, pass_on(p)) for p, f in enumerate(mids)] + [(1.0, finish)])


def _pair_comm(grads):
    n = len(grads)
    r = [g.shape[0] // N_DEV for g in grads]

    def start(ins, outs, sems):
        send_sems, recv_sems = sems
        x, y, c, _ = _place()
        for w in range(n):
            for a in range(N_CHIP):
                pltpu.make_async_remote_copy(
                    src_ref=ins[w].at[pl.ds((2 * a + 1 - c) * r[w], r[w]), :], dst_ref=outs[w].at[a],
                    send_sem=send_sems.at[w], recv_sem=recv_sems.at[w],
                    device_id=(x, y, 1 - c), device_id_type=MESH).start()

    def finish(ins, outs, sems):
        send_sems, recv_sems = sems
        x, y, c, _ = _place()
        for w in range(n):
            pltpu.make_async_remote_copy(
                src_ref=outs[w], dst_ref=outs[w], send_sem=send_sems.at[w], recv_sem=recv_sems.at[w],
                device_id=(x, y, c), device_id_type=MESH).wait()

    return _Comm(grads, [jax.ShapeDtypeStruct((N_CHIP, rw, D), BF) for rw in r],
                 [pltpu.SemaphoreType.DMA((n,)), pltpu.SemaphoreType.DMA((n,))],
                 [(0.0, start), (1.0, finish)])


def _pair_add(grad, got, core, *, name):
    r = got.shape[1]

    def body(c_ref, g_ref, got_ref, o_ref):
        o_ref[0] = (g_ref[...].astype(F32) + got_ref[0].astype(F32)).astype(BF)

    grid_spec = pltpu.PrefetchScalarGridSpec(
        num_scalar_prefetch=1, grid=(N_CHIP,),
        in_specs=[pl.BlockSpec((r, D), lambda a, c_ref: (2 * a + c_ref[0], 0)),
                  pl.BlockSpec((1, r, D), lambda a, c_ref: (a, 0, 0))],
        out_specs=pl.BlockSpec((1, r, D), lambda a, c_ref: (a, 0, 0)))
    return _pcall(body, name=name, grid_spec=grid_spec,
                  out_shape=jax.ShapeDtypeStruct((N_CHIP, r, D), BF),
                  compiler_params=_cp(("parallel",)))(core, grad, got)


def _chip_comm(pair_sums):
    n = len(pair_sums)
    r = [p.shape[1] for p in pair_sums]
    off = [sum(r[:w]) for w in range(n)]

    def tools(ins, outs, sems):
        send_sems, recv_sems, local_sems = sems
        x, y, c, chips = _place()
        my_chip = 2 * x + y

        def slot(w):
            return outs[0].at[my_chip, pl.ds(off[w], r[w]), :]

        own = [pltpu.make_async_copy(ins[w].at[my_chip], slot(w), local_sems.at[w]) for w in range(n)]
        return x, y, c, chips, my_chip, slot, own, send_sems, recv_sems

    def start(ins, outs, sems):
        x, y, c, chips, my_chip, slot, own, send_sems, recv_sems = tools(ins, outs, sems)
        for cp in own:
            cp.start()
        for j, chip in enumerate(chips):
            for w in range(n):
                pltpu.make_async_remote_copy(
                    src_ref=ins[w].at[2 * chip[0] + chip[1]], dst_ref=slot(w), send_sem=send_sems.at[j],
                    recv_sem=recv_sems.at[j], device_id=(*chip, c), device_id_type=MESH).start()

    def finish(ins, outs, sems):
        x, y, c, chips, my_chip, slot, own, send_sems, recv_sems = tools(ins, outs, sems)
        whole = outs[0].at[my_chip]
        for j in range(3):
            pltpu.make_async_remote_copy(
                src_ref=whole, dst_ref=whole, send_sem=send_sems.at[j], recv_sem=recv_sems.at[j],
                device_id=(x, y, c), device_id_type=MESH).wait()
        for cp in own:
            cp.wait()

    return _Comm(pair_sums, [jax.ShapeDtypeStruct((N_CHIP, sum(r), D), BF)],
                 [pltpu.SemaphoreType.DMA((3,)), pltpu.SemaphoreType.DMA((3,)), pltpu.SemaphoreType.DMA((n,))],
                 [(0.0, start), (1.0, finish)])


def _adam_math(w, g, m, v):
    m = ADAM_B1 * m + (1.0 - ADAM_B1) * g
    v = ADAM_B2 * v + (1.0 - ADAM_B2) * (g * g)
    m_hat = m / (1.0 - ADAM_B1 ** ADAM_STEP)
    v_hat = v / (1.0 - ADAM_B2 ** ADAM_STEP)
    delta = -ADAM_LR * (m_hat / (jnp.sqrt(v_hat) + ADAM_EPS) + ADAM_WD * w)
    return delta, m, v


SMALL = (("norm_mix_g", (1, D), 0), ("hgrn_norm_g", (1, D), 1), ("norm_ffn_g", (1, D), 2),
         ("norm_final_g", (1, D), 3), ("hgrn_lb_logits", (2, D), 4), ("attn_sinks", (1, 16), 6),
         ("b_in", (1, IN_W), 8))
LOSS_ROW = 7


def _small_allreduce_adam(grads, loss_row, params):
    n = len(SMALL)

    def rows_of(ref, shape, row):
        r, w = shape
        if w <= D:
            return ref[row:row + r, 0:w]
        pieces = [ref[row + k:row + k + 1, :] for k in range(-(-w // D))]
        return jnp.concatenate(pieces, axis=1)[:, 0:w]

    def body(*refs):
        g_refs, loss_ref = refs[:n], refs[n]
        wmv = refs[n + 1:4 * n + 1]
        loss_out = refs[4 * n + 1]
        outs = refs[4 * n + 2:8 * n + 2]
        mine, total, gath, send_sems, recv_sems = refs[8 * n + 2:]
        x, y, c, _ = _place()
        me = 4 * x + 2 * y + c
        mine[...] = jnp.zeros_like(mine)
        for g_ref, (_, (r, w), row) in zip(g_refs, SMALL):
            for k in range(-(-w // D)):
                wk = min(D, w - k * D)
                mine[row + k:row + k + r, 0:wk] = g_ref[:, k * D:k * D + wk]
        mine[LOSS_ROW:LOSS_ROW + 1, 0:128] = loss_ref[...]
        gath[me] = mine[...]
        cps = []
        for d in range(1, N_DEV):
            peer = (x ^ (d >> 2), y ^ ((d >> 1) & 1), c ^ (d & 1))
            cps.append(pltpu.make_async_remote_copy(
                src_ref=mine, dst_ref=gath.at[me], send_sem=send_sems.at[d - 1],
                recv_sem=recv_sems.at[d - 1], device_id=peer, device_id_type=MESH))
        for cp in cps:
            cp.start()
        for cp in cps:
            cp.wait()
        g = gath[0]
        for k in range(1, N_DEV):
            g = g + gath[k]
        total[...] = g
        loss_out[...] = total[LOSS_ROW:LOSS_ROW + 1, 0:128]
        for i, (_, shape, row) in enumerate(SMALL):
            gi = rows_of(total, shape, row)
            w_ref, m_ref, v_ref = wmv[3 * i:3 * i + 3]
            o = outs[4 * i:4 * i + 4]
            o[0][...] = gi
            o[1][...], o[2][...], o[3][...] = _adam_math(w_ref[...], gi, m_ref[...], v_ref[...])

    vm = pl.BlockSpec(memory_space=pltpu.VMEM)
    ins = [grads[name] for name, _, _ in SMALL] + [loss_row]
    for name, _, _ in SMALL:
        ins += list(params[name])
    out_shape = [jax.ShapeDtypeStruct((1, 128), F32)]
    for _, shape, _ in SMALL:
        out_shape += [jax.ShapeDtypeStruct(shape, F32)] * 4
    res = _pcall(body, name="small_allreduce_adam", in_specs=[vm] * len(ins), out_specs=[vm] * len(out_shape),
                 out_shape=out_shape,
                 scratch_shapes=[pltpu.VMEM((SMALL_ROWS, D), F32), pltpu.VMEM((SMALL_ROWS, D), F32),
                                 pltpu.VMEM((N_DEV, SMALL_ROWS, D), F32),
                                 pltpu.SemaphoreType.DMA((N_DEV - 1,)), pltpu.SemaphoreType.DMA((N_DEV - 1,))],
                 compiler_params=pltpu.CompilerParams(has_side_effects=True))(*ins)
    return res[0], {name: res[1 + 4 * i:5 + 4 * i] for i, (name, _, _) in enumerate(SMALL)}


def _adam(w, parts, index, m, v, *, name):
    rows = w.shape[0]
    tr = rows if rows <= 512 else rows // 2
    steps = rows // tr

    def body(w_ref, p_ref, m_ref, v_ref, g_ref, d_ref, mo_ref, vo_ref):
        g = p_ref[0].astype(F32)
        for a in range(1, N_CHIP):
            g = g + p_ref[a].astype(F32)
        g_ref[...] = g
        d_ref[...], mo_ref[...], vo_ref[...] = _adam_math(w_ref[...], g, m_ref[...], v_ref[...])

    spec = pl.BlockSpec((tr, D), lambda i: (i, 0))
    return _pcall(body, name=name, grid=(steps,),
                  in_specs=[spec, pl.BlockSpec((N_CHIP, tr, D), lambda i: (0, index * steps + i, 0)), spec, spec],
                  out_specs=[spec] * 4, out_shape=[jax.ShapeDtypeStruct((rows, D), F32)] * 4,
                  compiler_params=_cp(("parallel",)))(w, parts, m, v)


def _step(x, tgt, shards, norm_mix_g, b_in, sinks, logits, hgrn_norm_g, norm_ffn_g, norm_final_g):
    t = x.shape[0]
    core = lax.axis_index("c").astype(jnp.int32).reshape(1)

    u1, (win_t,) = _rms_fwd(x, norm_mix_g, tm=512, name="rms_mix", comm=_gather_comm(shards[0:1], (0.2, 0.4, 0.6, 0.8)))
    (q, kv, h3, hf, gates), (wg_t, wba, wbh, wout) = _inproj_fwd(
        u1, win_t, b_in, t=t, comm=_gather_comm([shards[1]] + shards[4:7], (0.25, 0.47, 0.7, 0.92)))
    (y_attn,), _ = _attn_fwd(q, kv, sinks, t=t)
    (y_hgrn, o_pre, states), (wu_t, wd) = _hgrn_fwd(h3, hf, logits, hgrn_norm_g, t=t,
                                                    comm=_gather_comm(shards[2:4], (0.27, 0.52, 0.77, 0.97)))
    col = lambda j: j
    first, second = (lambda j: 0), (lambda j: 1)
    gate_tiles = [(gates, D, first), (gates, D, second)]

    def merge(prods, ex):
        (ya_, yb_), (ga, gb) = prods, ex
        sa, sb = _sig(ga.astype(F32)), _sig(gb.astype(F32))
        return sa, sb, ya_ * sa * (1.0 - sa), yb_ * sb * (1.0 - sb), sa * ya_ + sb * yb_

    sig_a, sig_b, dgate_a, dgate_b, merged = _fmm(
        [y_attn, y_hgrn], [(0, wba, False), (1, wbh, False)], gate_tiles, merge,
        [(BF, D, D, first)] * 5, m=t, n=D, tm=512, tn=D, name="branch_merge")
    def resid_norm(prods, ex):
        (p,), (xv, gv) = prods, ex
        hv = xv + p
        return hv, hv * lax.rsqrt(jnp.mean(hv * hv, axis=-1, keepdims=True) + EPS) * gv

    h1, u2 = _fmm([merged], [(0, wout, False)], [(x, D, first)], resid_norm, [(F32, D, D, first), (BF, D, D, first)],
                  m=t, n=D, tm=1024, tn=D, name="out_proj", vecs=[norm_ffn_g])

    def swiglu(prods, ex):
        g_, u_ = prods
        s = _sig(g_)
        silu = g_ * s
        return u_ * s * (1.0 + g_ * (1.0 - s)), silu, silu * u_

    dz_dgate, dz_dup, z = _fmm([u2], [(0, wg_t, True), (0, wu_t, True)], [], swiglu,
                               [(BF, FFN, FFN // 2, col)] * 3, m=t, n=FFN, tm=1024, tn=FFN // 2,
                               name="ffn_gate_up")
    def loss_head(prods, ex):
        (p,), (hv, tv, gv) = prods, ex
        hv = hv + p
        r = lax.rsqrt(jnp.mean(hv * hv, axis=-1, keepdims=True) + EPS)
        xh = hv * r
        err = xh * gv - tv
        lp = jnp.sum(jnp.sum(err * err, axis=1, keepdims=True), axis=0, keepdims=True) * (0.5 / D)
        dy = err * (1.0 / D)
        dxh = dy * gv
        dh = r * (dxh - xh * jnp.mean(dxh * xh, axis=-1, keepdims=True))
        return dh, dh, jnp.sum(dy * xh, axis=0, keepdims=True), jnp.broadcast_to(lp, (1, 128))

    dh2, dh2_b, d_norm_final, loss_row = _fmm(
        [z], [(0, wd, False)], [(h1, D, first), (tgt, D, first)], loss_head, [(F32, D, D, first), (BF, D, D, first)],
        m=t, n=D, tm=512, tn=D, name="ffn_down_loss", vecs=[norm_final_g], sums=[D, 128])

    def swiglu_bwd(prods, ex):
        (dz,), (da_, db_) = prods, ex
        return dz * da_.astype(F32), dz * db_.astype(F32)

    ffn_tiles = [(dz_dgate, FFN // 2, col), (dz_dup, FFN // 2, col)]
    dgt, dup = _fmm([dh2_b], [(0, wd, True)], ffn_tiles, swiglu_bwd, [(BF, FFN, FFN // 2, col)] * 2,
                    m=t, n=FFN, tm=1024, tn=FFN // 2, name="d_gate_up")
    d_wd = _wgrad(z, dh2_b, name="d_w_down")
    (du2,) = _fmm([dgt, dup], [(0, wg_t, False), (1, wu_t, False)], [], lambda prods, ex: (prods[0] + prods[1],),
                  [(F32, D, 512, col)], m=t, n=D, tm=1024, tn=512, name="d_u2")
    d_wg = _wgrad(dgt, u2, name="d_w_gate")
    d_wu = _wgrad(dup, u2, name="d_w_up")
    dh1, dh1_b, d_norm_ffn = _rms_bwd(du2, h1, norm_ffn_g, dh2, tm=512, name="rms_ffn_bwd")
    d_wout = _wgrad(merged, dh1_b, name="d_w_out")

    def merge_bwd(prods, ex):
        (dm,), (sa, sb, ca, cb, wa, wb) = prods, ex
        dgate = jnp.concatenate([dm * ca.astype(F32), dm * cb.astype(F32)], axis=1)
        dya_ = (dm * sa.astype(F32)).astype(BF)
        dyb_ = (dm * sb.astype(F32)).astype(BF)
        return (dya_, dyb_, dgate, lax.dot_general(dya_, wa, _NT, preferred_element_type=F32),
                lax.dot_general(dyb_, wb, _NT, preferred_element_type=F32))

    ffn_grads = (d_wg, d_wu, d_wd)
    (dya, dyb, dgates, dy_attn, dy_hgrn), got = _fmm(
        [dh1_b], [(0, wout, True)], [(a, D, first) for a in (sig_a, sig_b, dgate_a, dgate_b)], merge_bwd,
        [(BF, D, D, first), (BF, D, D, first), (BF, 2 * D, 2 * D, first), (BF, D, D, first), (F32, D, D, first)],
        m=t, n=D, tm=512, tn=D, name="d_merge", consts=[wba, wbh], comm=_pair_comm(ffn_grads))
    pair_ffn = [_pair_add(g, r, core, name="pair_add_ffn%d" % i) for i, (g, r) in enumerate(zip(ffn_grads, got))]
    d_wba = _wgrad(y_attn, dya, name="d_w_ba")
    d_wbh = _wgrad(y_hgrn, dyb, name="d_w_bh")
    sq_grads = (d_wba, d_wbh, d_wout)
    (dh4, d_logits, d_hgrn_norm), (parts_ffn, *got) = _hgrn_bwd(
        h3, hf, logits, hgrn_norm_g, o_pre, states, dy_hgrn, t=t,
        comm=_both(_chip_comm(pair_ffn), _pair_comm(sq_grads)))
    pair_sq = [_pair_add(g, r, core, name="pair_add_sq%d" % i) for i, (g, r) in enumerate(zip(sq_grads, got))]
    (dq, dkv, d_sinks), (parts_sq,) = _attn_bwd(q, kv, sinks, dy_attn, t=t, comm=_chip_comm(pair_sq))
    dps = (dq, dkv, dh4, dgates)
    d_win_t, d_b_in = _inproj_bwd_w(dps, u1, t=t)
    half0, got_in = _inproj_bwd_x(dps, win_t, x, norm_mix_g, dh1, t=t, part=0, comm=_pair_comm([d_win_t]))
    pair_in = _pair_add(d_win_t, got_in[0], core, name="pair_add_w_in")
    (grad_x, d_norm_mix), (parts_in,) = _inproj_bwd_x(dps, win_t, x, norm_mix_g, dh1, t=t, part=1, prev=half0,
                                                      comm=_chip_comm([pair_in]))

    small_grads = (d_norm_mix, d_b_in, d_sinks, d_logits, d_hgrn_norm, d_norm_ffn, d_norm_final)
    return loss_row, grad_x, (parts_in, parts_ffn, parts_sq), small_grads


def kernel(x, norm_mix_g, w_in, b_in, attn_sinks, hgrn_lb_logits, hgrn_norm_g, w_branch_attn, w_branch_hgrn, w_out, norm_ffn_g, w_ffn_gate, w_ffn_up, w_ffn_down, norm_final_g, loss_target, m_norm_mix_g, m_w_in, m_b_in, m_attn_sinks, m_hgrn_lb_logits, m_hgrn_norm_g, m_w_branch_attn, m_w_branch_hgrn, m_w_out, m_norm_ffn_g, m_w_ffn_gate, m_w_ffn_up, m_w_ffn_down, m_norm_final_g, v_norm_mix_g, v_w_in, v_b_in, v_attn_sinks, v_hgrn_lb_logits, v_hgrn_norm_g, v_w_branch_attn, v_w_branch_hgrn, v_w_out, v_norm_ffn_g, v_w_ffn_gate, v_w_ffn_up, v_w_ffn_down, v_norm_final_g):
    shards = [w_in[0].T.astype(BF), w_ffn_gate[0].T.astype(BF), w_ffn_up[0].T.astype(BF),
              w_ffn_down[0].astype(BF), w_branch_attn[0].astype(BF), w_branch_hgrn[0].astype(BF),
              w_out[0].astype(BF)]
    loss_row, grad_x, grad_parts, small_grads = _step(
        x[0], loss_target[0], shards, norm_mix_g, b_in, attn_sinks, hgrn_lb_logits, hgrn_norm_g,
        norm_ffn_g, norm_final_g.reshape(1, D))

    d_norm_mix, d_b_in, d_sinks, d_logits, d_hgrn_norm, d_norm_ffn, d_norm_final = small_grads
    row = lambda a: a.reshape(1, D)
    loss_out, small = _small_allreduce_adam(
        dict(norm_mix_g=d_norm_mix, hgrn_norm_g=d_hgrn_norm, norm_ffn_g=d_norm_ffn, norm_final_g=d_norm_final,
             hgrn_lb_logits=d_logits, attn_sinks=d_sinks, b_in=d_b_in),
        loss_row,
        dict(norm_mix_g=(norm_mix_g, m_norm_mix_g, v_norm_mix_g), hgrn_norm_g=(hgrn_norm_g, m_hgrn_norm_g, v_hgrn_norm_g),
             norm_ffn_g=(norm_ffn_g, m_norm_ffn_g, v_norm_ffn_g),
             norm_final_g=(row(norm_final_g), row(m_norm_final_g), row(v_norm_final_g)),
             hgrn_lb_logits=(hgrn_lb_logits, m_hgrn_lb_logits, v_hgrn_lb_logits),
             attn_sinks=(attn_sinks, m_attn_sinks, v_attn_sinks), b_in=(b_in, m_b_in, v_b_in)))
    small["norm_final_g"] = [a.reshape(D) for a in small["norm_final_g"]]
    loss = loss_out[0, 0]

    names = ["w_in", "w_ffn_gate", "w_ffn_up", "w_ffn_down", "w_branch_attn", "w_branch_hgrn", "w_out"]
    w_full = dict(w_in=(w_in, m_w_in, v_w_in), w_ffn_gate=(w_ffn_gate, m_w_ffn_gate, v_w_ffn_gate),
                  w_ffn_up=(w_ffn_up, m_w_ffn_up, v_w_ffn_up), w_ffn_down=(w_ffn_down, m_w_ffn_down, v_w_ffn_down),
                  w_branch_attn=(w_branch_attn, m_w_branch_attn, v_w_branch_attn),
                  w_branch_hgrn=(w_branch_hgrn, m_w_branch_hgrn, v_w_branch_hgrn),
                  w_out=(w_out, m_w_out, v_w_out))
    parts_in, parts_ffn, parts_sq = grad_parts
    where = [(parts_in, 0), (parts_ffn, 0), (parts_ffn, 1), (parts_ffn, 2), (parts_sq, 0), (parts_sq, 1), (parts_sq, 2)]
    big = {}
    for i, name in enumerate(names):
        view = (lambda a: a[0].T) if i < 3 else (lambda a: a[0])
        back = (lambda a: a.T[None]) if i < 3 else (lambda a: a[None])
        wv, mv, vv = w_full[name]
        res = _adam(view(wv), where[i][0], where[i][1], view(mv), view(vv), name="adam_" + name)
        big[name] = [back(a) for a in res]

    order = ["norm_mix_g", "w_in", "b_in", "attn_sinks", "hgrn_lb_logits", "hgrn_norm_g", "w_branch_attn",
             "w_branch_hgrn", "w_out", "norm_ffn_g", "w_ffn_gate", "w_ffn_up", "w_ffn_down", "norm_final_g"]
    outs = [loss, grad_x[None]]
    for kind in range(4):
        for name in order:
            outs.append(big[name][kind] if name in big else small[name][kind])
    return tuple(outs)
```

```python
import math

import jax
import jax.numpy as jnp
from jax import lax
from jax.experimental import pallas as pl
from jax.experimental.pallas import tpu as pltpu

F32 = jnp.float32
BF = jnp.bfloat16
MESH = pl.DeviceIdType.MESH

D = 1024
HEAD = 64
N_PAIR = 8
BLK = 128
CH = 64
HG_SUB = 2
HG_HEADS = 8
HG_K = 128
FFN = 2816
IN_W = 7424
N_DEV = 8
N_CHIP = 4
EPS = 1e-6
NEG = -1e30
SCALE = 1.0 / math.sqrt(HEAD)
VMEM_LIMIT = 56 * 1024 * 1024
WT = 256

ADAM_LR, ADAM_B1, ADAM_B2, ADAM_EPS, ADAM_WD, ADAM_STEP = 0.001, 0.9, 0.999, 1e-08, 0.01, 10

SLAB_R = (IN_W // N_DEV, FFN // N_DEV, FFN // N_DEV, FFN // N_DEV, D // N_DEV, D // N_DEV, D // N_DEV)
SLAB_ROWS = sum(SLAB_R)
SLAB_OFF = tuple(sum(SLAB_R[:i]) for i in range(len(SLAB_R)))
N_W = len(SLAB_R)
GRP_OFF = (0, D // WT, (D + 256) // WT, (5 * D + 256) // WT)
GRP_N = (D // WT, 256 // WT, 4 * D // WT, 2 * D // WT)
SMALL_ROWS = 16


_NN = (((1,), (0,)), ((), ()))
_NT = (((1,), (1,)), ((), ()))
_TN = (((0,), (0,)), ((), ()))


def _pcall(body, **kw):
    return pl.pallas_call(body, **kw)


def _cp(sem=None, **kw):
    return pltpu.CompilerParams(dimension_semantics=sem, vmem_limit_bytes=VMEM_LIMIT, **kw)


def _sig(v):
    return 0.5 * jnp.tanh(0.5 * v) + 0.5


def _accum(ref, val, first):
    @pl.when(first)
    def _():
        ref[...] = val

    @pl.when(jnp.logical_not(first))
    def _():
        ref[...] += val


class _Comm:
    def __init__(self, ins, out_shapes, sem_shapes, phases):
        self.ins, self.out_shapes, self.sem_shapes, self.phases = list(ins), list(out_shapes), list(sem_shapes), phases


def _both(a, b):
    ni, no, ns = len(a.ins), len(a.out_shapes), len(a.sem_shapes)

    def of_a(fn):
        return lambda ins, outs, sems: fn(ins[:ni], outs[:no], sems[:ns])

    def of_b(fn):
        return lambda ins, outs, sems: fn(ins[ni:], outs[no:], sems[ns:])

    return _Comm(a.ins + b.ins, a.out_shapes + b.out_shapes, a.sem_shapes + b.sem_shapes,
                 [(f, of_a(fn)) for f, fn in a.phases] + [(f, of_b(fn)) for f, fn in b.phases])


def _host(body, comm, n_in, n_out, n_scr, nsteps, step_fn):
    if comm is None:
        return body
    ci, co = len(comm.ins), len(comm.out_shapes)

    def wrapped(*refs):
        p = 0
        ins, p = refs[p:p + n_in], p + n_in
        cins, p = refs[p:p + ci], p + ci
        outs, p = refs[p:p + n_out], p + n_out
        couts, p = refs[p:p + co], p + co
        scr, p = refs[p:p + n_scr], p + n_scr
        csems = refs[p:]
        step = step_fn()
        for frac, fn in comm.phases:
            if frac < 1.0:
                @pl.when(step == int(round(frac * (nsteps - 1))))
                def _(fn=fn):
                    fn(cins, couts, csems)
        body(*ins, *outs, *scr)
        for frac, fn in comm.phases:
            if frac >= 1.0:
                @pl.when(step == nsteps - 1)
                def _(fn=fn):
                    fn(cins, couts, csems)

    return wrapped


def _hosted_call(body, comm, args, *, name, grid, in_specs, out_specs, out_shape, scratch_shapes, sem,
                 nsteps, step_fn, aliases=None):
    n_in, n_out, n_scr = len(in_specs), len(out_specs), len(scratch_shapes)
    args = list(args)
    extra = {}
    if comm is not None:
        in_specs = list(in_specs) + [_hbm_spec()] * len(comm.ins)
        out_specs = list(out_specs) + [_hbm_spec()] * len(comm.out_shapes)
        out_shape = list(out_shape) + comm.out_shapes
        scratch_shapes = list(scratch_shapes) + comm.sem_shapes
        args += comm.ins
        extra = dict(has_side_effects=True)
    outs = _pcall(_host(body, comm, n_in, n_out, n_scr, nsteps, step_fn), name=name, grid=grid,
                  in_specs=in_specs, out_specs=out_specs, out_shape=out_shape, scratch_shapes=scratch_shapes,
                  input_output_aliases=aliases or {}, compiler_params=_cp(sem, **extra))(*args)
    return list(outs[:n_out]), list(outs[n_out:])


def _hbm_spec():
    return pl.BlockSpec(memory_space=pl.ANY)


def _wgrad(a, b, *, name):
    (t, m), n = a.shape, b.shape[1]

    def body(a_ref, b_ref, o_ref):
        o_ref[...] = lax.dot_general(a_ref[...], b_ref[...], _TN, preferred_element_type=F32).astype(BF)

    return _pcall(body, name=name, grid=(m // WT,),
                  in_specs=[pl.BlockSpec((t, WT), lambda i: (0, i)), pl.BlockSpec((t, n), lambda i: (0, 0))],
                  out_specs=pl.BlockSpec((WT, n), lambda i: (i, 0)),
                  out_shape=jax.ShapeDtypeStruct((m, n), BF), compiler_params=_cp(("parallel",)))(a, b)


def _fmm(lhs, rhs, extras, epilogue, outs, *, m, n, tm, tn, name, comm=None, vecs=(), consts=(), sums=()):
    tm, tn = min(tm, m), min(tn, n)
    assert m % tm == 0 and n % tn == 0 and (not sums or tn == n), (name, m, n, tm, tn)
    in_specs, args = [], []
    for a in lhs:
        in_specs.append(pl.BlockSpec((tm, a.shape[1]), lambda i, j: (i, 0)))
        args.append(a)
    for li, b, tb in rhs:
        k = lhs[li].shape[1]
        in_specs.append(pl.BlockSpec((tn, k), lambda i, j: (j, 0)) if tb
                        else pl.BlockSpec((k, tn), lambda i, j: (0, j)))
        args.append(b)
    for arr, w, col in extras:
        in_specs.append(pl.BlockSpec((tm, w), lambda i, j, col=col: (i, col(j))))
        args.append(arr)
    for vec in vecs:
        in_specs.append(pl.BlockSpec((1, tn), lambda i, j: (0, j)))
        args.append(vec)
    for whole in consts:
        in_specs.append(pl.BlockSpec(whole.shape, lambda i, j: (0, 0)))
        args.append(whole)
    out_specs = [pl.BlockSpec((tm, w), lambda i, j, col=col: (i, col(j))) for _, _, w, col in outs]
    out_shape = [jax.ShapeDtypeStruct((m, total), dt) for dt, total, _, _ in outs]
    for w in sums:
        out_specs.append(pl.BlockSpec((1, w), lambda i, j: (0, 0)))
        out_shape.append(jax.ShapeDtypeStruct((1, w), F32))
    nl, nr, ne, no = len(lhs), len(rhs), len(extras) + len(vecs) + len(consts), len(outs)

    def body(*refs):
        prods = []
        for r, (li, _, tb) in enumerate(rhs):
            prods.append(lax.dot_general(refs[li][...], refs[nl + r][...], _NT if tb else _NN,
                                         preferred_element_type=F32))
        vals = epilogue(prods, [ref[...] for ref in refs[nl + nr:nl + nr + ne]])
        o_refs = refs[nl + nr + ne:]
        for o_ref, v in zip(o_refs[:no], vals[:no]):
            o_ref[...] = v.astype(o_ref.dtype)
        for s_ref, v in zip(o_refs[no:], vals[no:]):
            _accum(s_ref, v, pl.program_id(0) == 0)

    gm, gn = m // tm, n // tn
    res, comm_res = _hosted_call(
        body, comm, args, name=name, grid=(gm, gn), in_specs=in_specs, out_specs=out_specs,
        out_shape=out_shape, scratch_shapes=[], sem=("arbitrary", "arbitrary"), nsteps=gm * gn,
        step_fn=lambda: pl.program_id(0) * gn + pl.program_id(1))
    return res if comm is None else (res, comm_res)


def _grp_of(i):
    return [jnp.logical_and(i >= GRP_OFF[g], i < GRP_OFF[g] + GRP_N[g]) for g in range(4)]


def _grp_idx(i, g):
    return jnp.clip(i - GRP_OFF[g], 0, GRP_N[g] - 1)


def _inproj_fwd(u, win_t, b_in, *, t, comm=None):
    tm = min(1024, t)
    n_row = t // tm
    n_chunks, h_first, g_first = 8, 2, 6
    sub = D // WT

    def w_block(l):
        return jnp.where(l == 0, GRP_OFF[0], jnp.where(l == 1, GRP_OFF[1], GRP_OFF[2] + sub * (l - h_first)))

    def body(u_ref, *rest):
        w_refs, b_refs, (q_ref, kv_ref, h3_ref, hf_ref, g_ref) = rest[:sub], rest[sub:2 * sub], rest[2 * sub:]
        l = pl.program_id(1)

        @pl.when(l == 1)
        def _():
            kv_ref[...] = (lax.dot_general(u_ref[...], w_refs[0][...], _NT, preferred_element_type=F32)
                           + b_refs[0][...]).astype(BF)

        is_hf = l == h_first + 1
        in_h3 = jnp.logical_and(jnp.logical_and(l >= h_first, l < g_first), jnp.logical_not(is_hf))
        for pred, o_ref in ((l == 0, q_ref), (in_h3, h3_ref), (is_hf, hf_ref), (l >= g_first, g_ref)):
            @pl.when(pred)
            def _(o_ref=o_ref):
                w = jnp.concatenate([w[...] for w in w_refs], axis=0)
                b = jnp.concatenate([b[...] for b in b_refs], axis=1)
                o_ref[...] = (lax.dot_general(u_ref[...], w, _NT, preferred_element_type=F32) + b).astype(o_ref.dtype)

    return _hosted_call(
        body, comm, [u] + [win_t] * sub + [b_in] * sub, name="inproj_fwd", grid=(n_row, n_chunks),
        in_specs=[pl.BlockSpec((tm, D), lambda i, l: (i, 0))]
        + [pl.BlockSpec((WT, D), lambda i, l, o=o: (w_block(l) + o, 0)) for o in range(sub)]
        + [pl.BlockSpec((1, WT), lambda i, l, o=o: (0, w_block(l) + o)) for o in range(sub)],
        out_specs=[pl.BlockSpec((tm, D), lambda i, l: (i, 0)),
                   pl.BlockSpec((tm, 256), lambda i, l: (i, 0)),
                   pl.BlockSpec((tm, D), lambda i, l: (i, jnp.clip(l - h_first - 1, 0, 2))),
                   pl.BlockSpec((tm, D), lambda i, l: (i, 0)),
                   pl.BlockSpec((tm, D), lambda i, l: (i, jnp.clip(l - g_first, 0, 1)))],
        out_shape=[jax.ShapeDtypeStruct((t, D), BF), jax.ShapeDtypeStruct((t, 256), BF),
                   jax.ShapeDtypeStruct((t, 3 * D), BF), jax.ShapeDtypeStruct((t, D), F32),
                   jax.ShapeDtypeStruct((t, 2 * D), BF)],
        scratch_shapes=[], sem=("arbitrary", "arbitrary"), nsteps=n_row * n_chunks,
        step_fn=lambda: pl.program_id(0) * n_chunks + pl.program_id(1))


def _inproj_bwd_x(dps, win_t, x, g, resid, *, t, part, prev=None, comm=None):
    n_row = 8 if t >= 4096 else 4
    tm = t // n_row
    first = n_row // 4
    per = first if part == 0 else n_row - first
    row = lambda i: part * first + i

    n_chunks = 4
    sub = 2 * D // WT

    def w_block(l):
        return jnp.where(l == 0, 0, GRP_OFF[2] + sub * (l - 1))

    def body(d0, d1, d2, d3, *rest):
        w_refs, (x_ref, g_ref, r_ref) = rest[:sub], rest[sub:sub + 3]
        dg_prev = rest[sub + 3] if prev is not None else None
        o_ref, dg_ref, acc_ref = rest[-3], rest[-2], rest[-1]
        i, l = pl.program_id(0), pl.program_id(1)

        @pl.when(l == 0)
        def _():
            wq = jnp.concatenate([w[...] for w in w_refs[:GRP_N[0]]], axis=0)
            acc_ref[...] = (jnp.dot(d0[...], wq, preferred_element_type=F32)
                            + jnp.dot(d1[...], w_refs[GRP_N[0]][...], preferred_element_type=F32))

        for pred, d_ref in ((jnp.logical_and(l >= 1, l < 3), d2), (l == 3, d3)):
            @pl.when(pred)
            def _(d_ref=d_ref):
                w = jnp.concatenate([w[...] for w in w_refs], axis=0)
                acc_ref[...] += jnp.dot(d_ref[...], w, preferred_element_type=F32)

        @pl.when(l == n_chunks - 1)
        def _():
            xv = x_ref[...]
            r = lax.rsqrt(jnp.mean(xv * xv, axis=-1, keepdims=True) + EPS)
            xh = xv * r
            du = acc_ref[...]
            dxh = du * g_ref[...]
            o_ref[...] = r_ref[...] + r * (dxh - xh * jnp.mean(dxh * xh, axis=-1, keepdims=True))
            dg = jnp.sum(du * xh, axis=0, keepdims=True)
            if dg_prev is not None:
                dg = dg + jnp.where(i == 0, 1.0, 0.0) * dg_prev[...]
            _accum(dg_ref, dg, i == 0)

    rows = lambda w: pl.BlockSpec((tm, w), lambda i, l: (row(i), 0))
    in_specs = ([rows(D), rows(256),
                 pl.BlockSpec((tm, 2 * D), lambda i, l: (row(i), jnp.clip(l - 1, 0, 1))), rows(2 * D)]
                + [pl.BlockSpec((WT, D), lambda i, l, o=o: (w_block(l) + o, 0)) for o in range(sub)]
                + [rows(D), pl.BlockSpec((1, D), lambda i, l: (0, 0)), rows(D)])
    args = list(dps) + [win_t] * sub + [x, g, resid]
    aliases = None
    if prev is not None:
        in_specs += [pl.BlockSpec((1, D), lambda i, l: (0, 0)), _hbm_spec()]
        args += [prev[1], prev[0]]
        aliases = {len(args) - 1: 0}
    return _hosted_call(
        body, comm, args, name="inproj_bwd_x%d" % part, grid=(per, n_chunks), in_specs=in_specs,
        out_specs=[rows(D), pl.BlockSpec((1, D), lambda i, l: (0, 0))],
        out_shape=[jax.ShapeDtypeStruct((t, D), F32), jax.ShapeDtypeStruct((1, D), F32)],
        scratch_shapes=[pltpu.VMEM((tm, D), F32)], sem=("arbitrary", "arbitrary"), nsteps=per * n_chunks,
        step_fn=lambda: pl.program_id(0) * n_chunks + pl.program_id(1), aliases=aliases)


def _inproj_bwd_w(dps, u, *, t):
    n_tiles = IN_W // WT
    dims = (((0,), (0,)), ((), ()))

    def body(d0, d1, d2, d3, u_ref, o_ref, db_ref):
        i = pl.program_id(0)
        uv = u_ref[...]
        for g, (pred, d_ref) in enumerate(zip(_grp_of(i), (d0, d1, d2, d3))):
            @pl.when(pred)
            def _(d_ref=d_ref):
                dv = d_ref[...]
                o_ref[...] = lax.dot_general(dv, uv, dims, preferred_element_type=F32).astype(BF)
                db_ref[...] = jnp.sum(dv.astype(F32), axis=0, keepdims=True)

    return _pcall(body, name="inproj_bwd_w", grid=(n_tiles,),
                  in_specs=[pl.BlockSpec((t, WT), lambda i, g=g: (0, _grp_idx(i, g))) for g in range(4)]
                  + [pl.BlockSpec((t, D), lambda i: (0, 0))],
                  out_specs=[pl.BlockSpec((WT, D), lambda i: (i, 0)),
                             pl.BlockSpec((1, WT), lambda i: (0, i))],
                  out_shape=[jax.ShapeDtypeStruct((IN_W, D), BF), jax.ShapeDtypeStruct((1, IN_W), F32)],
                  compiler_params=_cp(("arbitrary",)))(*dps, u)


def _row_spec(tm, width, col=0):
    return pl.BlockSpec((tm, width), lambda i: (i, col))


def _vec_spec(width):
    return pl.BlockSpec((1, width), lambda i: (0, 0))


def _rms_fwd(x, g, *, tm, name, comm=None):
    t = x.shape[0]
    tm = min(tm, t)

    def body(x_ref, g_ref, u_ref):
        xv = x_ref[...]
        r = lax.rsqrt(jnp.mean(xv * xv, axis=-1, keepdims=True) + EPS)
        u_ref[...] = (xv * r * g_ref[...]).astype(BF)

    (u,), comm_res = _hosted_call(
        body, comm, (x, g), name=name, grid=(t // tm,), in_specs=[_row_spec(tm, D), _vec_spec(D)],
        out_specs=[_row_spec(tm, D)], out_shape=[jax.ShapeDtypeStruct((t, D), BF)], scratch_shapes=[],
        sem=("arbitrary",), nsteps=t // tm, step_fn=lambda: pl.program_id(0))
    return u if comm is None else (u, comm_res)


def _rms_bwd(du, x, g, resid, *, tm, name):
    t = x.shape[0]
    tm = min(tm, t)

    def body(du_ref, x_ref, g_ref, r_ref, dx_ref, dxb_ref, dg_ref):
        xv = x_ref[...]
        r = lax.rsqrt(jnp.mean(xv * xv, axis=-1, keepdims=True) + EPS)
        xh = xv * r
        duv = du_ref[...]
        dxh = duv * g_ref[...]
        dx = r_ref[...] + r * (dxh - xh * jnp.mean(dxh * xh, axis=-1, keepdims=True))
        dx_ref[...] = dx
        dxb_ref[...] = dx.astype(BF)
        _accum(dg_ref, jnp.sum(duv * xh, axis=0, keepdims=True), pl.program_id(0) == 0)

    return _pcall(body, name=name, grid=(t // tm,),
                  in_specs=[_row_spec(tm, D), _row_spec(tm, D), _vec_spec(D), _row_spec(tm, D)],
                  out_specs=[_row_spec(tm, D), _row_spec(tm, D), _vec_spec(D)],
                  out_shape=[jax.ShapeDtypeStruct((t, D), F32), jax.ShapeDtypeStruct((t, D), BF),
                             jax.ShapeDtypeStruct((1, D), F32)],
                  compiler_params=_cp(("arbitrary",)))(du, x, g, resid)


def _attn_kv_tiles(kprev, kcur):
    kv = jnp.concatenate([kprev, kcur], axis=0).astype(F32)
    lo = lax.broadcasted_iota(jnp.int32, (2 * BLK, 128), 1) < HEAD
    tiles = []
    for part in (kv[:, 0:128], kv[:, 128:256]):
        rolled = pltpu.roll(part, HEAD, 1)
        z = jnp.zeros_like(part)
        tiles.append(((jnp.where(lo, part, z).astype(BF), jnp.where(lo, z, rolled).astype(BF)),
                      (jnp.where(lo, rolled, z).astype(BF), jnp.where(lo, z, part).astype(BF))))
    k_t, v_t = tiles
    return [(jnp.concatenate(k_t[h], axis=0), jnp.concatenate(v_t[h], axis=0)) for h in range(2)]


def _attn_mask(i):
    qi = lax.broadcasted_iota(jnp.int32, (BLK, 2 * BLK), 0)
    kj = lax.broadcasted_iota(jnp.int32, (BLK, 2 * BLK), 1)
    first_key = jnp.where(i == 0, BLK, 0)
    in_prev = jnp.logical_and(jnp.logical_and(kj < BLK, kj > qi), kj >= first_key)
    in_cur = jnp.logical_and(kj >= BLK, kj - BLK <= qi)
    return jnp.logical_or(in_prev, in_cur)


def _attn_probs(s, sink, valid):
    s = jnp.where(valid, s * SCALE, NEG)
    mx = jnp.maximum(jnp.max(s, axis=-1, keepdims=True), sink)
    e = jnp.exp(s - mx)
    es = jnp.exp(sink - mx)
    inv = 1.0 / (jnp.sum(e, axis=-1, keepdims=True) + es)
    return e * inv, es * inv


_KEYS = 2 * BLK


def _pair(ref, j):
    return ref[:, j * 128:(j + 1) * 128]


def _attn_fwd(q, kv, sinks, *, t, comm=None):
    nb = t // BLK

    def body(sink_ref, q_ref, kp_ref, kc_ref, o_ref):
        valid = _attn_mask(pl.program_id(0))
        tiles = _attn_kv_tiles(kp_ref[...], kc_ref[...])
        s = [lax.dot_general(_pair(q_ref, j), tiles[j // 4][0], _NT, preferred_element_type=F32)
             for j in range(N_PAIR)]
        p = []
        for j in range(N_PAIR):
            pe, _ = _attn_probs(s[j][:, 0:_KEYS], sink_ref[0, 2 * j], valid)
            po, _ = _attn_probs(s[j][:, _KEYS:2 * _KEYS], sink_ref[0, 2 * j + 1], valid)
            p.append(jnp.concatenate([pe.astype(BF), po.astype(BF)], axis=1))
        for j in range(N_PAIR):
            o_ref[:, j * 128:(j + 1) * 128] = jnp.dot(p[j], tiles[j // 4][1],
                                                      preferred_element_type=F32).astype(BF)

    return _hosted_call(
        body, comm, (sinks, q, kv, kv), name="attn_fwd", grid=(nb,),
        in_specs=[pl.BlockSpec(memory_space=pltpu.SMEM),
                  pl.BlockSpec((BLK, D), lambda i: (i, 0)),
                  pl.BlockSpec((BLK, 256), lambda i: (jnp.maximum(i - 1, 0), 0)),
                  pl.BlockSpec((BLK, 256), lambda i: (i, 0))],
        out_specs=[pl.BlockSpec((BLK, D), lambda i: (i, 0))],
        out_shape=[jax.ShapeDtypeStruct((t, D), BF)],
        scratch_shapes=[], sem=("arbitrary",), nsteps=nb, step_fn=lambda: pl.program_id(0))


def _attn_bwd(q, kv, sinks, do, *, t, comm=None):
    nb = t // BLK
    last = nb - 1

    def body(sink_ref, q_ref, kp_ref, kc_ref, do_ref, dq_ref, dkv_ref, ds_ref, carry_ref):
        i = pl.program_id(0)

        @pl.when(i == 0)
        def _():
            ds_ref[...] = jnp.zeros_like(ds_ref)
            carry_ref[...] = jnp.zeros_like(carry_ref)

        @pl.when(i < nb)
        def _():
            valid = _attn_mask(i)
            tiles = _attn_kv_tiles(kp_ref[...], kc_ref[...])
            lane1 = lax.broadcasted_iota(jnp.int32, (1, 128), 1)
            dsink = jnp.zeros((1, 128), F32)
            s = [lax.dot_general(_pair(q_ref, j), tiles[j // 4][0], _NT, preferred_element_type=F32)
                 for j in range(N_PAIR)]
            dp = [lax.dot_general(_pair(do_ref, j), tiles[j // 4][1], _NT, preferred_element_type=F32)
                  for j in range(N_PAIR)]
            p_all, ds_all = [], []
            for j in range(N_PAIR):
                halves = []
                for par in range(2):
                    cols = slice(par * _KEYS, (par + 1) * _KEYS)
                    p, ps = _attn_probs(s[j][:, cols], sink_ref[0, 2 * j + par], valid)
                    dpj = dp[j][:, cols]
                    dd = jnp.sum(p * dpj, axis=-1, keepdims=True)
                    dsink = dsink + jnp.where(lane1 == 2 * j + par,
                                              -jnp.sum(ps * dd, axis=0, keepdims=True), 0.0)
                    halves.append((p.astype(BF), (p * (dpj - dd)).astype(BF)))
                p_all.append(jnp.concatenate([halves[0][0], halves[1][0]], axis=1))
                ds_all.append(jnp.concatenate([halves[0][1], halves[1][1]], axis=1))
            for j in range(N_PAIR):
                dq_ref[:, j * 128:(j + 1) * 128] = (
                    jnp.dot(ds_all[j], tiles[j // 4][0], preferred_element_type=F32) * SCALE).astype(BF)
            ds_ref[...] += dsink
            gk, gv = [], []
            for h in range(2):
                grp = range(4 * h, 4 * h + 4)
                q_rows = jnp.concatenate([_pair(q_ref, j) for j in grp], axis=0)
                do_rows = jnp.concatenate([_pair(do_ref, j) for j in grp], axis=0)
                g_k = lax.dot_general(jnp.concatenate([ds_all[j] for j in grp], axis=0), q_rows, _TN,
                                      preferred_element_type=F32)
                g_v = lax.dot_general(jnp.concatenate([p_all[j] for j in grp], axis=0), do_rows, _TN,
                                      preferred_element_type=F32)
                gk.append((g_k[0:_KEYS], g_k[_KEYS:2 * _KEYS]))
                gv.append((g_v[0:_KEYS], g_v[_KEYS:2 * _KEYS]))
            lo = lax.broadcasted_iota(jnp.int32, (2 * BLK, 128), 1) < HEAD
            zero = jnp.zeros((2 * BLK, 128), F32)

            def unpad(g):
                return (jnp.where(lo, g[0][0] + pltpu.roll(g[0][1], HEAD, 1), zero)
                        + jnp.where(lo, zero, pltpu.roll(g[1][0], HEAD, 1) + g[1][1]))

            dk = unpad(gk) * SCALE
            dv = unpad(gv)
            dkv_ref[:, 0:128] = (carry_ref[:, 0:128] + dk[0:BLK]).astype(BF)
            dkv_ref[:, 128:256] = (carry_ref[:, 128:256] + dv[0:BLK]).astype(BF)
            carry_ref[:, 0:128] = dk[BLK:2 * BLK]
            carry_ref[:, 128:256] = dv[BLK:2 * BLK]

        @pl.when(i == nb)
        def _():
            dkv_ref[...] = carry_ref[...].astype(BF)

    return _hosted_call(
        body, comm, (sinks, q, kv, kv, do), name="attn_bwd", grid=(nb + 1,),
        in_specs=[pl.BlockSpec(memory_space=pltpu.SMEM),
                  pl.BlockSpec((BLK, D), lambda i: (jnp.minimum(i, last), 0)),
                  pl.BlockSpec((BLK, 256), lambda i: (jnp.clip(i - 1, 0, last), 0)),
                  pl.BlockSpec((BLK, 256), lambda i: (jnp.minimum(i, last), 0)),
                  pl.BlockSpec((BLK, D), lambda i: (jnp.minimum(i, last), 0))],
        out_specs=[pl.BlockSpec((BLK, D), lambda i: (jnp.minimum(i, last), 0)),
                   pl.BlockSpec((BLK, 256), lambda i: (jnp.maximum(i - 1, 0), 0)),
                   pl.BlockSpec((1, 128), lambda i: (0, 0))],
        out_shape=[jax.ShapeDtypeStruct((t, D), BF), jax.ShapeDtypeStruct((t, 256), BF),
                   jax.ShapeDtypeStruct((1, 128), F32)],
        scratch_shapes=[pltpu.VMEM((BLK, 256), F32)], sem=("arbitrary",), nsteps=nb + 1,
        step_fn=lambda: pl.program_id(0))


def _split3(v):
    h = v.astype(BF)
    r = v - h.astype(F32)
    m = r.astype(BF)
    lo = (r - m.astype(F32)).astype(BF)
    return jnp.concatenate([h, m, lo], axis=1)


def _apply01(mat, v):
    n = v.shape[1]
    r = jnp.dot(mat, _split3(v), preferred_element_type=F32)
    return r[:, 0:n] + r[:, n:2 * n] + r[:, 2 * n:3 * n]


def _hgrn_gates(hq, hf, lb):
    sq = _sig(hq)
    sg = _sig(hf)
    f = lb + (1.0 - lb) * sg
    return hq * sq, (1.0 - lb) * (1.0 - sg), jnp.log(f), sq, sg, f


def _tri(upper):
    r = lax.broadcasted_iota(jnp.int32, (CH, CH), 0)
    c = lax.broadcasted_iota(jnp.int32, (CH, CH), 1)
    return (c >= r) if upper else (c <= r)


def _lb_from_logits(lg_ref):
    return 1.0 / (1.0 + jnp.exp(lg_ref[1:2, :] - lg_ref[0:1, :]))


def _hgrn_fwd(h3, hf, logits, norm_g, *, t, comm=None):
    nc = t // CH
    nt_dims = (((1,), (1,)), ((), ()))
    tn_dims = (((0,), (0,)), ((), ()))

    def body(h_ref, hf_ref, lg_ref, ng_ref, y_ref, o_ref, st_ref, s_scr, b_scr, qa_s, ka_s, qb_s, kb_s, v_s):
        @pl.when(pl.program_id(0) == 0)
        def _():
            s_scr[...] = jnp.zeros_like(s_scr)

        heads = [slice(h * HG_K, (h + 1) * HG_K) for h in range(HG_HEADS)]
        causal = _tri(False)
        lb = _lb_from_logits(lg_ref)
        for c in range(HG_SUB):
            rows = slice(c * CH, (c + 1) * CH)
            q, k, g, _, _, _ = _hgrn_gates(h_ref[rows, 0:D].astype(F32), hf_ref[rows, :], lb)
            b_scr[...] = _apply01(jnp.where(causal, 1.0, 0.0).astype(BF), g)
            b = b_scr[...]
            b_mid = b_scr[CH // 2 - 1:CH // 2, :]
            b_last = b_scr[CH - 1:CH, :]
            qa_s[...] = (q * jnp.exp(b - b_mid)).astype(BF)
            ka_s[...] = (k * jnp.exp(b_mid - b)).astype(BF)
            qb_s[...] = (q * jnp.exp(b)).astype(BF)
            kb_s[...] = (k * jnp.exp(b_last - b)).astype(BF)
            v_s[...] = h_ref[rows, D:2 * D]
            dec = jnp.exp(b_last)
            st_ref[c] = s_scr[...].astype(BF)
            a = [jnp.where(causal, lax.dot_general(qa_s[:, sl], ka_s[:, sl], nt_dims, preferred_element_type=F32),
                           0.0).astype(BF) for sl in heads]
            for h, sl in enumerate(heads):
                o_ref[rows, sl] = (jnp.dot(a[h], v_s[:, sl], preferred_element_type=F32)
                                   + lax.dot_general(qb_s[:, sl], s_scr[h].astype(BF), nt_dims,
                                                     preferred_element_type=F32))
            for h, sl in enumerate(heads):
                s_scr[h] = dec[:, sl] * s_scr[h] + lax.dot_general(v_s[:, sl], kb_s[:, sl], tn_dims,
                                                                   preferred_element_type=F32)
            for h, sl in enumerate(heads):
                o = o_ref[rows, sl]
                on = o * lax.rsqrt(jnp.mean(o * o, axis=-1, keepdims=True) + EPS)
                gate = _sig(h_ref[rows, 2 * D + h * HG_K:2 * D + (h + 1) * HG_K].astype(F32))
                y_ref[rows, sl] = (on * ng_ref[:, sl] * gate).astype(BF)

    half = lambda: pltpu.VMEM((CH, D), BF)
    blk = HG_SUB * CH
    return _hosted_call(
        body, comm, (h3, hf, logits, norm_g), name="hgrn_fwd", grid=(nc // HG_SUB,),
        in_specs=[pl.BlockSpec((blk, 3 * D), lambda n: (n, 0)),
                  pl.BlockSpec((blk, D), lambda n: (n, 0)),
                  pl.BlockSpec((2, D), lambda n: (0, 0)),
                  pl.BlockSpec((1, D), lambda n: (0, 0))],
        out_specs=[pl.BlockSpec((blk, D), lambda n: (n, 0)),
                   pl.BlockSpec((blk, D), lambda n: (n, 0)),
                   pl.BlockSpec((HG_SUB, HG_HEADS, HG_K, HG_K), lambda n: (n, 0, 0, 0))],
        out_shape=[jax.ShapeDtypeStruct((t, D), BF), jax.ShapeDtypeStruct((t, D), F32),
                   jax.ShapeDtypeStruct((nc, HG_HEADS, HG_K, HG_K), BF)],
        scratch_shapes=[pltpu.VMEM((HG_HEADS, HG_K, HG_K), F32), pltpu.VMEM((CH, D), F32),
                        half(), half(), half(), half(), half()],
        sem=("arbitrary",), nsteps=nc // HG_SUB, step_fn=lambda: pl.program_id(0))


def _hgrn_bwd(h3, hf, logits, norm_g, o_pre, states, dy, *, t, comm=None):
    nc = t // CH
    nt_dims = (((1,), (1,)), ((), ()))
    tn_dims = (((0,), (0,)), ((), ()))

    def body(h_ref, hf_ref, lg_ref, ng_ref, o_ref, st_ref, dy_ref, dh_ref, dlg_ref, dng_ref, ds_scr, dlb_scr,
             b_scr, tail_s, e_qa, e_ka, e_qb, e_kb, q_s, k_s, dqa_s, dka_s, dqb_s, dkb_s,
             qa_s, ka_s, qb_s, kb_s, v_s, do_s):
        n = pl.program_id(0)

        @pl.when(n == 0)
        def _():
            ds_scr[...] = jnp.zeros_like(ds_scr)
            dlb_scr[...] = jnp.zeros_like(dlb_scr)
            dng_ref[...] = jnp.zeros_like(dng_ref)

        heads = [slice(h * HG_K, (h + 1) * HG_K) for h in range(HG_HEADS)]
        lb = _lb_from_logits(lg_ref)
        causal = _tri(False)

        def chunk(c):
            rows = slice(c * CH, (c + 1) * CH)
            hq = h_ref[rows, 0:D].astype(F32)
            q, k, g, sq, sg, f = _hgrn_gates(hq, hf_ref[rows, :], lb)
            b_scr[...] = _apply01(jnp.where(causal, 1.0, 0.0).astype(BF), g)
            b = b_scr[...]
            b_mid = b_scr[CH // 2 - 1:CH // 2, :]
            b_last = b_scr[CH - 1:CH, :]
            q_s[...] = q
            k_s[...] = k
            for e_ref, s_ref, base, expo in ((e_qa, qa_s, q, b - b_mid), (e_ka, ka_s, k, b_mid - b),
                                             (e_qb, qb_s, q, b), (e_kb, kb_s, k, b_last - b)):
                e = jnp.exp(expo)
                e_ref[...] = e
                s_ref[...] = (base * e).astype(BF)
            v_s[...] = h_ref[rows, D:2 * D]
            dec = jnp.exp(b_last)
            for h, sl in enumerate(heads):
                gcol = slice(3 * D + h * HG_K, 3 * D + (h + 1) * HG_K)
                ngh = ng_ref[:, sl]
                sgate = _sig(h_ref[rows, 2 * D + h * HG_K:2 * D + (h + 1) * HG_K].astype(F32))
                o = o_ref[rows, sl]
                r = lax.rsqrt(jnp.mean(o * o, axis=-1, keepdims=True) + EPS)
                on = o * r
                dyh = dy_ref[rows, sl]
                dh_ref[rows, gcol] = (dyh * on * ngh * sgate * (1.0 - sgate)).astype(BF)
                dng_ref[:, sl] += jnp.sum(dyh * on * sgate, axis=0, keepdims=True)
                don = dyh * ngh * sgate
                do_s[:, sl] = (r * (don - on * jnp.mean(don * on, axis=-1, keepdims=True))).astype(BF)
            a = [jnp.where(causal, lax.dot_general(qa_s[:, sl], ka_s[:, sl], nt_dims, preferred_element_type=F32),
                           0.0).astype(BF) for sl in heads]
            da = [jnp.where(causal, lax.dot_general(do_s[:, sl], v_s[:, sl], nt_dims, preferred_element_type=F32),
                            0.0).astype(BF) for sl in heads]
            for h, sl in enumerate(heads):
                dh_ref[rows, 2 * D + h * HG_K:2 * D + (h + 1) * HG_K] = (
                    lax.dot_general(a[h], do_s[:, sl], tn_dims, preferred_element_type=F32)
                    + lax.dot_general(kb_s[:, sl], ds_scr[h].astype(BF), nt_dims, preferred_element_type=F32)
                ).astype(BF)
            for h, sl in enumerate(heads):
                dqa_s[:, sl] = jnp.dot(da[h], ka_s[:, sl], preferred_element_type=F32)
            for h, sl in enumerate(heads):
                dka_s[:, sl] = lax.dot_general(da[h], qa_s[:, sl], tn_dims, preferred_element_type=F32)
            for h, sl in enumerate(heads):
                dqb_s[:, sl] = jnp.dot(do_s[:, sl], st_ref[c, h], preferred_element_type=F32)
            for h, sl in enumerate(heads):
                dkb_s[:, sl] = jnp.dot(v_s[:, sl], ds_scr[h].astype(BF), preferred_element_type=F32)
            for h, sl in enumerate(heads):
                tail_s[:, sl] = jnp.sum(dec[:, sl] * st_ref[c, h].astype(F32) * ds_scr[h], axis=0, keepdims=True)
            for h, sl in enumerate(heads):
                ds_scr[h] = (lax.dot_general(do_s[:, sl], qb_s[:, sl], tn_dims, preferred_element_type=F32)
                             + dec[:, sl] * ds_scr[h])
            qv, kv = q_s[...], k_s[...]
            dqa, dka, dqb, dkb = dqa_s[...], dka_s[...], dqb_s[...], dkb_s[...]
            eqa, eka, eqb, ekb = e_qa[...], e_ka[...], e_qb[...], e_kb[...]
            dkb_kb = dkb * (kv * ekb)
            db_last = jnp.sum(dkb_kb, axis=0, keepdims=True) + tail_s[...]
            last_row = lax.broadcasted_iota(jnp.int32, (CH, D), 0) == CH - 1
            db = (dqa * (qv * eqa) - dka * (kv * eka) + dqb * (qv * eqb) - dkb_kb
                  + jnp.where(last_row, db_last, 0.0))
            dg = _apply01(jnp.where(_tri(True), 1.0, 0.0).astype(BF), db)
            dq = dqa * eqa + dqb * eqb
            dk = dka * eka + dkb * ekb
            dh_ref[rows, 0:D] = (dq * sq * (1.0 + hq * (1.0 - sq))).astype(BF)
            dfk = dg / f - dk
            dh_ref[rows, D:2 * D] = ((1.0 - lb) * dfk * sg * (1.0 - sg)).astype(BF)
            dlb_scr[...] += jnp.sum((1.0 - sg) * dfk, axis=0, keepdims=True)

        for c in reversed(range(HG_SUB)):
            chunk(c)

        @pl.when(n == nc // HG_SUB - 1)
        def _():
            dl0 = dlb_scr[...] * lb * (1.0 - lb)
            dlg_ref[0:1, :] = dl0
            dlg_ref[1:2, :] = -dl0

    steps = nc // HG_SUB
    blk = HG_SUB * CH
    rev = lambda n: (steps - 1 - n, 0)
    return _hosted_call(
        body, comm, (h3, hf, logits, norm_g, o_pre, states, dy), name="hgrn_bwd", grid=(steps,),
        in_specs=[pl.BlockSpec((blk, 3 * D), rev),
                  pl.BlockSpec((blk, D), rev),
                  pl.BlockSpec((2, D), lambda n: (0, 0)),
                  pl.BlockSpec((1, D), lambda n: (0, 0)),
                  pl.BlockSpec((blk, D), rev),
                  pl.BlockSpec((HG_SUB, HG_HEADS, HG_K, HG_K), lambda n: (steps - 1 - n, 0, 0, 0)),
                  pl.BlockSpec((blk, D), rev)],
        out_specs=[pl.BlockSpec((blk, 4 * D), rev),
                   pl.BlockSpec((2, D), lambda n: (0, 0)),
                   pl.BlockSpec((1, D), lambda n: (0, 0))],
        out_shape=[jax.ShapeDtypeStruct((t, 4 * D), BF), jax.ShapeDtypeStruct((2, D), F32),
                   jax.ShapeDtypeStruct((1, D), F32)],
        scratch_shapes=([pltpu.VMEM((HG_HEADS, HG_K, HG_K), F32), pltpu.VMEM((1, D), F32),
                         pltpu.VMEM((CH, D), F32), pltpu.VMEM((1, D), F32)]
                        + [pltpu.VMEM((CH, D), F32)] * 10 + [pltpu.VMEM((CH, D), BF)] * 6),
        sem=("arbitrary",), nsteps=steps, step_fn=lambda: pl.program_id(0))


def _place():
    x, y, c = lax.axis_index("x"), lax.axis_index("y"), lax.axis_index("c")
    return x, y, c, [(1 - x, y), (x, 1 - y), (1 - x, 1 - y)]


def _gather_comm(shards, mids):
    n, pieces = len(shards), len(mids)
    r = [s.shape[0] for s in shards]
    tile = 16
    cut = [[(rw // tile * p // pieces) * tile for p in range(pieces + 1)] for rw in r]
    size = [[cut[w][p + 1] - cut[w][p] for p in range(pieces)] for w in range(n)]

    def tools(ins, outs, sems):
        send_sems, recv_sems, local_sems = sems
        x, y, c, _ = _place()
        me, sib = (x, y, c), (x, y, 1 - c)
        near = [(x ^ c, y ^ (1 - c), c), (x ^ (1 - c), y ^ c, c), (1 - x, 1 - y, c)]

        def rows(w, p, dev):
            return outs[w].at[pl.ds((4 * dev[0] + 2 * dev[1] + dev[2]) * r[w] + cut[w][p], size[w][p]), :]

        def copy(kind, w, p, block, to, own=False):
            src = ins[w].at[pl.ds(cut[w][p], size[w][p]), :] if own else rows(w, p, block)
            return pltpu.make_async_remote_copy(
                src_ref=src, dst_ref=rows(w, p, block), send_sem=send_sems.at[p, kind],
                recv_sem=recv_sems.at[p, kind], device_id=to, device_id_type=MESH)

        def all_of(kind, p):
            whole = outs[0].at[pl.ds(0, sum(size[w][p] for w in range(n))), :]
            return pltpu.make_async_remote_copy(
                src_ref=whole, dst_ref=whole, send_sem=send_sems.at[p, kind], recv_sem=recv_sems.at[p, kind],
                device_id=me, device_id_type=MESH)

        mine = [pltpu.make_async_copy(ins[w], outs[w].at[pl.ds((4 * x + 2 * y + c) * r[w], r[w]), :],
                                      local_sems.at[w]) for w in range(n)]
        return near, me, sib, copy, all_of, mine

    def start(ins, outs, sems):
        near, me, sib, copy, _, mine = tools(ins, outs, sems)
        for cp in mine:
            cp.start()
        for p in range(pieces):
            for w in range(n):
                copy(0, w, p, me, sib, own=True).start()
                copy(1, w, p, me, near[0], own=True).start()
                copy(2, w, p, me, near[1], own=True).start()

    def pass_diagonal(p, near, sib, copy, all_of):
        all_of(3, p).wait_recv()
        for w in range(n):
            copy(6, w, p, near[2], sib).start()

    def pass_on(p):
        def phase(ins, outs, sems):
            near, _, sib, copy, all_of, _ = tools(ins, outs, sems)
            all_of(1, p).wait_recv()
            for w in range(n):
                copy(3, w, p, near[0], near[1]).start()
                copy(4, w, p, near[0], sib).start()
            all_of(2, p).wait_recv()
            for w in range(n):
                copy(5, w, p, near[1], sib).start()
            if p > 0:
                pass_diagonal(p - 1, near, sib, copy, all_of)
        return phase

    def finish(ins, outs, sems):
        near, _, sib, copy, all_of, mine = tools(ins, outs, sems)
        pass_diagonal(pieces - 1, near, sib, copy, all_of)
        for p in range(pieces):
            all_of(0, p).wait_recv()
            for kind in (4, 5, 6):
                all_of(kind, p).wait_recv()
            for kind in range(7):
                all_of(kind, p).wait_send()
        for cp in mine:
            cp.wait()

    return _Comm(shards, [jax.ShapeDtypeStruct((N_DEV * rw, D), BF) for rw in r],
                 [pltpu.SemaphoreType.DMA((pieces, 7)), pltpu.SemaphoreType.DMA((pieces, 7)),
                  pltpu.SemaphoreType.DMA((n,))],
                 [(0.0, start)] + [(f, pass_on(p)) for p, f in enumerate(mids)] + [(1.0, finish)])


def _pair_comm(grads):
    n = len(grads)
    r = [g.shape[0] // N_DEV for g in grads]

    def start(ins, outs, sems):
        send_sems, recv_sems = sems
        x, y, c, _ = _place()
        for w in range(n):
            for a in range(N_CHIP):
                pltpu.make_async_remote_copy(
                    src_ref=ins[w].at[pl.ds((2 * a + 1 - c) * r[w], r[w]), :], dst_ref=outs[w].at[a],
                    send_sem=send_sems.at[w], recv_sem=recv_sems.at[w],
                    device_id=(x, y, 1 - c), device_id_type=MESH).start()

    def finish(ins, outs, sems):
        send_sems, recv_sems = sems
        x, y, c, _ = _place()
        for w in range(n):
            pltpu.make_async_remote_copy(
                src_ref=outs[w], dst_ref=outs[w], send_sem=send_sems.at[w], recv_sem=recv_sems.at[w],
                device_id=(x, y, c), device_id_type=MESH).wait()

    return _Comm(grads, [jax.ShapeDtypeStruct((N_CHIP, rw, D), BF) for rw in r],
                 [pltpu.SemaphoreType.DMA((n,)), pltpu.SemaphoreType.DMA((n,))],
                 [(0.0, start), (1.0, finish)])


def _pair_add(grad, got, core, *, name):
    r = got.shape[1]

    def body(c_ref, g_ref, got_ref, o_ref):
        o_ref[0] = (g_ref[...].astype(F32) + got_ref[0].astype(F32)).astype(BF)

    grid_spec = pltpu.PrefetchScalarGridSpec(
        num_scalar_prefetch=1, grid=(N_CHIP,),
        in_specs=[pl.BlockSpec((r, D), lambda a, c_ref: (2 * a + c_ref[0], 0)),
                  pl.BlockSpec((1, r, D), lambda a, c_ref: (a, 0, 0))],
        out_specs=pl.BlockSpec((1, r, D), lambda a, c_ref: (a, 0, 0)))
    return _pcall(body, name=name, grid_spec=grid_spec,
                  out_shape=jax.ShapeDtypeStruct((N_CHIP, r, D), BF),
                  compiler_params=_cp(("parallel",)))(core, grad, got)


def _chip_comm(pair_sums):
    n = len(pair_sums)
    r = [p.shape[1] for p in pair_sums]
    off = [sum(r[:w]) for w in range(n)]

    def tools(ins, outs, sems):
        send_sems, recv_sems, local_sems = sems
        x, y, c, chips = _place()
        my_chip = 2 * x + y

        def slot(w):
            return outs[0].at[my_chip, pl.ds(off[w], r[w]), :]

        own = [pltpu.make_async_copy(ins[w].at[my_chip], slot(w), local_sems.at[w]) for w in range(n)]
        return x, y, c, chips, my_chip, slot, own, send_sems, recv_sems

    def start(ins, outs, sems):
        x, y, c, chips, my_chip, slot, own, send_sems, recv_sems = tools(ins, outs, sems)
        for cp in own:
            cp.start()
        for j, chip in enumerate(chips):
            for w in range(n):
                pltpu.make_async_remote_copy(
                    src_ref=ins[w].at[2 * chip[0] + chip[1]], dst_ref=slot(w), send_sem=send_sems.at[j],
                    recv_sem=recv_sems.at[j], device_id=(*chip, c), device_id_type=MESH).start()

    def finish(ins, outs, sems):
        x, y, c, chips, my_chip, slot, own, send_sems, recv_sems = tools(ins, outs, sems)
        whole = outs[0].at[my_chip]
        for j in range(3):
            pltpu.make_async_remote_copy(
                src_ref=whole, dst_ref=whole, send_sem=send_sems.at[j], recv_sem=recv_sems.at[j],
                device_id=(x, y, c), device_id_type=MESH).wait()
        for cp in own:
            cp.wait()

    return _Comm(pair_sums, [jax.ShapeDtypeStruct((N_CHIP, sum(r), D), BF)],
                 [pltpu.SemaphoreType.DMA((3,)), pltpu.SemaphoreType.DMA((3,)), pltpu.SemaphoreType.DMA((n,))],
                 [(0.0, start), (1.0, finish)])


def _adam_math(w, g, m, v):
    m = ADAM_B1 * m + (1.0 - ADAM_B1) * g
    v = ADAM_B2 * v + (1.0 - ADAM_B2) * (g * g)
    m_hat = m / (1.0 - ADAM_B1 ** ADAM_STEP)
    v_hat = v / (1.0 - ADAM_B2 ** ADAM_STEP)
    delta = -ADAM_LR * (m_hat / (jnp.sqrt(v_hat) + ADAM_EPS) + ADAM_WD * w)
    return delta, m, v


SMALL = (("norm_mix_g", (1, D), 0), ("hgrn_norm_g", (1, D), 1), ("norm_ffn_g", (1, D), 2),
         ("norm_final_g", (1, D), 3), ("hgrn_lb_logits", (2, D), 4), ("attn_sinks", (1, 16), 6),
         ("b_in", (1, IN_W), 8))
LOSS_ROW = 7


def _small_allreduce_adam(grads, loss_row, params):
    n = len(SMALL)

    def rows_of(ref, shape, row):
        r, w = shape
        if w <= D:
            return ref[row:row + r, 0:w]
        pieces = [ref[row + k:row + k + 1, :] for k in range(-(-w // D))]
        return jnp.concatenate(pieces, axis=1)[:, 0:w]

    def body(*refs):
        g_refs, loss_ref = refs[:n], refs[n]
        wmv = refs[n + 1:4 * n + 1]
        loss_out = refs[4 * n + 1]
        outs = refs[4 * n + 2:8 * n + 2]
        mine, total, gath, send_sems, recv_sems = refs[8 * n + 2:]
        x, y, c, _ = _place()
        me = 4 * x + 2 * y + c
        mine[...] = jnp.zeros_like(mine)
        for g_ref, (_, (r, w), row) in zip(g_refs, SMALL):
            for k in range(-(-w // D)):
                wk = min(D, w - k * D)
                mine[row + k:row + k + r, 0:wk] = g_ref[:, k * D:k * D + wk]
        mine[LOSS_ROW:LOSS_ROW + 1, 0:128] = loss_ref[...]
        gath[me] = mine[...]
        cps = []
        for d in range(1, N_DEV):
            peer = (x ^ (d >> 2), y ^ ((d >> 1) & 1), c ^ (d & 1))
            cps.append(pltpu.make_async_remote_copy(
                src_ref=mine, dst_ref=gath.at[me], send_sem=send_sems.at[d - 1],
                recv_sem=recv_sems.at[d - 1], device_id=peer, device_id_type=MESH))
        for cp in cps:
            cp.start()
        for cp in cps:
            cp.wait()
        g = gath[0]
        for k in range(1, N_DEV):
            g = g + gath[k]
        total[...] = g
        loss_out[...] = total[LOSS_ROW:LOSS_ROW + 1, 0:128]
        for i, (_, shape, row) in enumerate(SMALL):
            gi = rows_of(total, shape, row)
            w_ref, m_ref, v_ref = wmv[3 * i:3 * i + 3]
            o = outs[4 * i:4 * i + 4]
            o[0][...] = gi
            o[1][...], o[2][...], o[3][...] = _adam_math(w_ref[...], gi, m_ref[...], v_ref[...])

    vm = pl.BlockSpec(memory_space=pltpu.VMEM)
    ins = [grads[name] for name, _, _ in SMALL] + [loss_row]
    for name, _, _ in SMALL:
        ins += list(params[name])
    out_shape = [jax.ShapeDtypeStruct((1, 128), F32)]
    for _, shape, _ in SMALL:
        out_shape += [jax.ShapeDtypeStruct(shape, F32)] * 4
    res = _pcall(body, name="small_allreduce_adam", in_specs=[vm] * len(ins), out_specs=[vm] * len(out_shape),
                 out_shape=out_shape,
                 scratch_shapes=[pltpu.VMEM((SMALL_ROWS, D), F32), pltpu.VMEM((SMALL_ROWS, D), F32),
                                 pltpu.VMEM((N_DEV, SMALL_ROWS, D), F32),
                                 pltpu.SemaphoreType.DMA((N_DEV - 1,)), pltpu.SemaphoreType.DMA((N_DEV - 1,))],
                 compiler_params=pltpu.CompilerParams(has_side_effects=True))(*ins)
    return res[0], {name: res[1 + 4 * i:5 + 4 * i] for i, (name, _, _) in enumerate(SMALL)}


def _adam(w, parts, index, m, v, *, name):
    rows = w.shape[0]
    tr = rows if rows <= 512 else rows // 2
    steps = rows // tr

    def body(w_ref, p_ref, m_ref, v_ref, g_ref, d_ref, mo_ref, vo_ref):
        g = p_ref[0].astype(F32)
        for a in range(1, N_CHIP):
            g = g + p_ref[a].astype(F32)
        g_ref[...] = g
        d_ref[...], mo_ref[...], vo_ref[...] = _adam_math(w_ref[...], g, m_ref[...], v_ref[...])

    spec = pl.BlockSpec((tr, D), lambda i: (i, 0))
    return _pcall(body, name=name, grid=(steps,),
                  in_specs=[spec, pl.BlockSpec((N_CHIP, tr, D), lambda i: (0, index * steps + i, 0)), spec, spec],
                  out_specs=[spec] * 4, out_shape=[jax.ShapeDtypeStruct((rows, D), F32)] * 4,
                  compiler_params=_cp(("parallel",)))(w, parts, m, v)


def _step(x, tgt, shards, norm_mix_g, b_in, sinks, logits, hgrn_norm_g, norm_ffn_g, norm_final_g):
    t = x.shape[0]
    core = lax.axis_index("c").astype(jnp.int32).reshape(1)

    u1, (win_t,) = _rms_fwd(x, norm_mix_g, tm=512, name="rms_mix", comm=_gather_comm(shards[0:1], (0.2, 0.4, 0.6, 0.8)))
    (q, kv, h3, hf, gates), (wg_t, wba, wbh, wout) = _inproj_fwd(
        u1, win_t, b_in, t=t, comm=_gather_comm([shards[1]] + shards[4:7], (0.25, 0.47, 0.7, 0.92)))
    (y_attn,), _ = _attn_fwd(q, kv, sinks, t=t)
    (y_hgrn, o_pre, states), (wu_t, wd) = _hgrn_fwd(h3, hf, logits, hgrn_norm_g, t=t,
                                                    comm=_gather_comm(shards[2:4], (0.27, 0.52, 0.77, 0.97)))
    col = lambda j: j
    first, second = (lambda j: 0), (lambda j: 1)
    gate_tiles = [(gates, D, first), (gates, D, second)]

    def merge(prods, ex):
        (ya_, yb_), (ga, gb) = prods, ex
        sa, sb = _sig(ga.astype(F32)), _sig(gb.astype(F32))
        return sa, sb, ya_ * sa * (1.0 - sa), yb_ * sb * (1.0 - sb), sa * ya_ + sb * yb_

    sig_a, sig_b, dgate_a, dgate_b, merged = _fmm(
        [y_attn, y_hgrn], [(0, wba, False), (1, wbh, False)], gate_tiles, merge,
        [(BF, D, D, first)] * 5, m=t, n=D, tm=512, tn=D, name="branch_merge")
    def resid_norm(prods, ex):
        (p,), (xv, gv) = prods, ex
        hv = xv + p
        return hv, hv * lax.rsqrt(jnp.mean(hv * hv, axis=-1, keepdims=True) + EPS) * gv

    h1, u2 = _fmm([merged], [(0, wout, False)], [(x, D, first)], resid_norm, [(F32, D, D, first), (BF, D, D, first)],
                  m=t, n=D, tm=1024, tn=D, name="out_proj", vecs=[norm_ffn_g])

    def swiglu(prods, ex):
        g_, u_ = prods
        s = _sig(g_)
        silu = g_ * s
        return u_ * s * (1.0 + g_ * (1.0 - s)), silu, silu * u_

    dz_dgate, dz_dup, z = _fmm([u2], [(0, wg_t, True), (0, wu_t, True)], [], swiglu,
                               [(BF, FFN, FFN // 2, col)] * 3, m=t, n=FFN, tm=1024, tn=FFN // 2,
                               name="ffn_gate_up")
    def loss_head(prods, ex):
        (p,), (hv, tv, gv) = prods, ex
        hv = hv + p
        r = lax.rsqrt(jnp.mean(hv * hv, axis=-1, keepdims=True) + EPS)
        xh = hv * r
        err = xh * gv - tv
        lp = jnp.sum(jnp.sum(err * err, axis=1, keepdims=True), axis=0, keepdims=True) * (0.5 / D)
        dy = err * (1.0 / D)
        dxh = dy * gv
        dh = r * (dxh - xh * jnp.mean(dxh * xh, axis=-1, keepdims=True))
        return dh, dh, jnp.sum(dy * xh, axis=0, keepdims=True), jnp.broadcast_to(lp, (1, 128))

    dh2, dh2_b, d_norm_final, loss_row = _fmm(
        [z], [(0, wd, False)], [(h1, D, first), (tgt, D, first)], loss_head, [(F32, D, D, first), (BF, D, D, first)],
        m=t, n=D, tm=512, tn=D, name="ffn_down_loss", vecs=[norm_final_g], sums=[D, 128])

    def swiglu_bwd(prods, ex):
        (dz,), (da_, db_) = prods, ex
        return dz * da_.astype(F32), dz * db_.astype(F32)

    ffn_tiles = [(dz_dgate, FFN // 2, col), (dz_dup, FFN // 2, col)]
    dgt, dup = _fmm([dh2_b], [(0, wd, True)], ffn_tiles, swiglu_bwd, [(BF, FFN, FFN // 2, col)] * 2,
                    m=t, n=FFN, tm=1024, tn=FFN // 2, name="d_gate_up")
    d_wd = _wgrad(z, dh2_b, name="d_w_down")
    (du2,) = _fmm([dgt, dup], [(0, wg_t, False), (1, wu_t, False)], [], lambda prods, ex: (prods[0] + prods[1],),
                  [(F32, D, 512, col)], m=t, n=D, tm=1024, tn=512, name="d_u2")
    d_wg = _wgrad(dgt, u2, name="d_w_gate")
    d_wu = _wgrad(dup, u2, name="d_w_up")
    dh1, dh1_b, d_norm_ffn = _rms_bwd(du2, h1, norm_ffn_g, dh2, tm=512, name="rms_ffn_bwd")
    d_wout = _wgrad(merged, dh1_b, name="d_w_out")

    def merge_bwd(prods, ex):
        (dm,), (sa, sb, ca, cb, wa, wb) = prods, ex
        dgate = jnp.concatenate([dm * ca.astype(F32), dm * cb.astype(F32)], axis=1)
        dya_ = (dm * sa.astype(F32)).astype(BF)
        dyb_ = (dm * sb.astype(F32)).astype(BF)
        return (dya_, dyb_, dgate, lax.dot_general(dya_, wa, _NT, preferred_element_type=F32),
                lax.dot_general(dyb_, wb, _NT, preferred_element_type=F32))

    ffn_grads = (d_wg, d_wu, d_wd)
    (dya, dyb, dgates, dy_attn, dy_hgrn), got = _fmm(
        [dh1_b], [(0, wout, True)], [(a, D, first) for a in (sig_a, sig_b, dgate_a, dgate_b)], merge_bwd,
        [(BF, D, D, first), (BF, D, D, first), (BF, 2 * D, 2 * D, first), (BF, D, D, first), (F32, D, D, first)],
        m=t, n=D, tm=512, tn=D, name="d_merge", consts=[wba, wbh], comm=_pair_comm(ffn_grads))
    pair_ffn = [_pair_add(g, r, core, name="pair_add_ffn%d" % i) for i, (g, r) in enumerate(zip(ffn_grads, got))]
    d_wba = _wgrad(y_attn, dya, name="d_w_ba")
    d_wbh = _wgrad(y_hgrn, dyb, name="d_w_bh")
    sq_grads = (d_wba, d_wbh, d_wout)
    (dh4, d_logits, d_hgrn_norm), (parts_ffn, *got) = _hgrn_bwd(
        h3, hf, logits, hgrn_norm_g, o_pre, states, dy_hgrn, t=t,
        comm=_both(_chip_comm(pair_ffn), _pair_comm(sq_grads)))
    pair_sq = [_pair_add(g, r, core, name="pair_add_sq%d" % i) for i, (g, r) in enumerate(zip(sq_grads, got))]
    (dq, dkv, d_sinks), (parts_sq,) = _attn_bwd(q, kv, sinks, dy_attn, t=t, comm=_chip_comm(pair_sq))
    dps = (dq, dkv, dh4, dgates)
    d_win_t, d_b_in = _inproj_bwd_w(dps, u1, t=t)
    half0, got_in = _inproj_bwd_x(dps, win_t, x, norm_mix_g, dh1, t=t, part=0, comm=_pair_comm([d_win_t]))
    pair_in = _pair_add(d_win_t, got_in[0], core, name="pair_add_w_in")
    (grad_x, d_norm_mix), (parts_in,) = _inproj_bwd_x(dps, win_t, x, norm_mix_g, dh1, t=t, part=1, prev=half0,
                                                      comm=_chip_comm([pair_in]))

    small_grads = (d_norm_mix, d_b_in, d_sinks, d_logits, d_hgrn_norm, d_norm_ffn, d_norm_final)
    return loss_row, grad_x, (parts_in, parts_ffn, parts_sq), small_grads


def kernel(x, norm_mix_g, w_in, b_in, attn_sinks, hgrn_lb_logits, hgrn_norm_g, w_branch_attn, w_branch_hgrn, w_out, norm_ffn_g, w_ffn_gate, w_ffn_up, w_ffn_down, norm_final_g, loss_target, m_norm_mix_g, m_w_in, m_b_in, m_attn_sinks, m_hgrn_lb_logits, m_hgrn_norm_g, m_w_branch_attn, m_w_branch_hgrn, m_w_out, m_norm_ffn_g, m_w_ffn_gate, m_w_ffn_up, m_w_ffn_down, m_norm_final_g, v_norm_mix_g, v_w_in, v_b_in, v_attn_sinks, v_hgrn_lb_logits, v_hgrn_norm_g, v_w_branch_attn, v_w_branch_hgrn, v_w_out, v_norm_ffn_g, v_w_ffn_gate, v_w_ffn_up, v_w_ffn_down, v_norm_final_g):
    shards = [w_in[0].T.astype(BF), w_ffn_gate[0].T.astype(BF), w_ffn_up[0].T.astype(BF),
              w_ffn_down[0].astype(BF), w_branch_attn[0].astype(BF), w_branch_hgrn[0].astype(BF),
              w_out[0].astype(BF)]
    loss_row, grad_x, grad_parts, small_grads = _step(
        x[0], loss_target[0], shards, norm_mix_g, b_in, attn_sinks, hgrn_lb_logits, hgrn_norm_g,
        norm_ffn_g, norm_final_g.reshape(1, D))

    d_norm_mix, d_b_in, d_sinks, d_logits, d_hgrn_norm, d_norm_ffn, d_norm_final = small_grads
    row = lambda a: a.reshape(1, D)
    loss_out, small = _small_allreduce_adam(
        dict(norm_mix_g=d_norm_mix, hgrn_norm_g=d_hgrn_norm, norm_ffn_g=d_norm_ffn, norm_final_g=d_norm_final,
             hgrn_lb_logits=d_logits, attn_sinks=d_sinks, b_in=d_b_in),
        loss_row,
        dict(norm_mix_g=(norm_mix_g, m_norm_mix_g, v_norm_mix_g), hgrn_norm_g=(hgrn_norm_g, m_hgrn_norm_g, v_hgrn_norm_g),
             norm_ffn_g=(norm_ffn_g, m_norm_ffn_g, v_norm_ffn_g),
             norm_final_g=(row(norm_final_g), row(m_norm_final_g), row(v_norm_final_g)),
             hgrn_lb_logits=(hgrn_lb_logits, m_hgrn_lb_logits, v_hgrn_lb_logits),
             attn_sinks=(attn_sinks, m_attn_sinks, v_attn_sinks), b_in=(b_in, m_b_in, v_b_in)))
    small["norm_final_g"] = [a.reshape(D) for a in small["norm_final_g"]]
    loss = loss_out[0, 0]

    names = ["w_in", "w_ffn_gate", "w_ffn_up", "w_ffn_down", "w_branch_attn", "w_branch_hgrn", "w_out"]
    w_full = dict(w_in=(w_in, m_w_in, v_w_in), w_ffn_gate=(w_ffn_gate, m_w_ffn_gate, v_w_ffn_gate),
                  w_ffn_up=(w_ffn_up, m_w_ffn_up, v_w_ffn_up), w_ffn_down=(w_ffn_down, m_w_ffn_down, v_w_ffn_down),
                  w_branch_attn=(w_branch_attn, m_w_branch_attn, v_w_branch_attn),
                  w_branch_hgrn=(w_branch_hgrn, m_w_branch_hgrn, v_w_branch_hgrn),
                  w_out=(w_out, m_w_out, v_w_out))
    parts_in, parts_ffn, parts_sq = grad_parts
    where = [(parts_in, 0), (parts_ffn, 0), (parts_ffn, 1), (parts_ffn, 2), (parts_sq, 0), (parts_sq, 1), (parts_sq, 2)]
    big = {}
    for i, name in enumerate(names):
        view = (lambda a: a[0].T) if i < 3 else (lambda a: a[0])
        back = (lambda a: a.T[None]) if i < 3 else (lambda a: a[None])
        wv, mv, vv = w_full[name]
        res = _adam(view(wv), where[i][0], where[i][1], view(mv), view(vv), name="adam_" + name)
        big[name] = [back(a) for a in res]

    order = ["norm_mix_g", "w_in", "b_in", "attn_sinks", "hgrn_lb_logits", "hgrn_norm_g", "w_branch_attn",
             "w_branch_hgrn", "w_out", "norm_ffn_g", "w_ffn_gate", "w_ffn_up", "w_ffn_down", "norm_final_g"]
    outs = [loss, grad_x[None]]
    for kind in range(4):
        for name in order:
            outs.append(big[name][kind] if name in big else small[name][kind])
    return tuple(outs)
```

```python
import math

import jax
import jax.numpy as jnp
from jax import lax
from jax.experimental import pallas as pl
from jax.experimental.pallas import tpu as pltpu

F32 = jnp.float32
BF = jnp.bfloat16
MESH = pl.DeviceIdType.MESH

D = 1024
HEAD = 64
N_PAIR = 8
BLK = 128
CH = 64
HG_SUB = 4
HG_HEADS = 8
HG_K = 128
FFN = 2816
IN_W = 7424
N_DEV = 8
N_CHIP = 4
EPS = 1e-6
NEG = -1e30
SCALE = 1.0 / math.sqrt(HEAD)
VMEM_LIMIT = 56 * 1024 * 1024
WT = 256

ADAM_LR, ADAM_B1, ADAM_B2, ADAM_EPS, ADAM_WD, ADAM_STEP = 0.001, 0.9, 0.999, 1e-08, 0.01, 10

SLAB_R = (IN_W // N_DEV, FFN // N_DEV, FFN // N_DEV, FFN // N_DEV, D // N_DEV, D // N_DEV, D // N_DEV)
SLAB_ROWS = sum(SLAB_R)
SLAB_OFF = tuple(sum(SLAB_R[:i]) for i in range(len(SLAB_R)))
N_W = len(SLAB_R)
GRP_OFF = (0, D // WT, (D + 256) // WT, (5 * D + 256) // WT)
GRP_N = (D // WT, 256 // WT, 4 * D // WT, 2 * D // WT)
SMALL_ROWS = 16


_NN = (((1,), (0,)), ((), ()))
_NT = (((1,), (1,)), ((), ()))
_TN = (((0,), (0,)), ((), ()))


def _pcall(body, **kw):
    return pl.pallas_call(body, **kw)


def _cp(sem=None, **kw):
    return pltpu.CompilerParams(dimension_semantics=sem, vmem_limit_bytes=VMEM_LIMIT, **kw)


def _sig(v):
    return 0.5 * jnp.tanh(0.5 * v) + 0.5


def _accum(ref, val, first):
    @pl.when(first)
    def _():
        ref[...] = val

    @pl.when(jnp.logical_not(first))
    def _():
        ref[...] += val


class _Comm:
    def __init__(self, ins, out_shapes, sem_shapes, phases):
        self.ins, self.out_shapes, self.sem_shapes, self.phases = list(ins), list(out_shapes), list(sem_shapes), phases


def _both(a, b):
    ni, no, ns = len(a.ins), len(a.out_shapes), len(a.sem_shapes)

    def of_a(fn):
        return lambda ins, outs, sems: fn(ins[:ni], outs[:no], sems[:ns])

    def of_b(fn):
        return lambda ins, outs, sems: fn(ins[ni:], outs[no:], sems[ns:])

    return _Comm(a.ins + b.ins, a.out_shapes + b.out_shapes, a.sem_shapes + b.sem_shapes,
                 [(f, of_a(fn)) for f, fn in a.phases] + [(f, of_b(fn)) for f, fn in b.phases])


def _host(body, comm, n_in, n_out, n_scr, nsteps, step_fn):
    if comm is None:
        return body
    ci, co = len(comm.ins), len(comm.out_shapes)

    def wrapped(*refs):
        p = 0
        ins, p = refs[p:p + n_in], p + n_in
        cins, p = refs[p:p + ci], p + ci
        outs, p = refs[p:p + n_out], p + n_out
        couts, p = refs[p:p + co], p + co
        scr, p = refs[p:p + n_scr], p + n_scr
        csems = refs[p:]
        step = step_fn()
        for frac, fn in comm.phases:
            if frac < 1.0:
                @pl.when(step == int(round(frac * (nsteps - 1))))
                def _(fn=fn):
                    fn(cins, couts, csems)
        body(*ins, *outs, *scr)
        for frac, fn in comm.phases:
            if frac >= 1.0:
                @pl.when(step == nsteps - 1)
                def _(fn=fn):
                    fn(cins, couts, csems)

    return wrapped


def _hosted_call(body, comm, args, *, name, grid, in_specs, out_specs, out_shape, scratch_shapes, sem,
                 nsteps, step_fn, aliases=None):
    n_in, n_out, n_scr = len(in_specs), len(out_specs), len(scratch_shapes)
    args = list(args)
    extra = {}
    if comm is not None:
        in_specs = list(in_specs) + [_hbm_spec()] * len(comm.ins)
        out_specs = list(out_specs) + [_hbm_spec()] * len(comm.out_shapes)
        out_shape = list(out_shape) + comm.out_shapes
        scratch_shapes = list(scratch_shapes) + comm.sem_shapes
        args += comm.ins
        extra = dict(has_side_effects=True)
    outs = _pcall(_host(body, comm, n_in, n_out, n_scr, nsteps, step_fn), name=name, grid=grid,
                  in_specs=in_specs, out_specs=out_specs, out_shape=out_shape, scratch_shapes=scratch_shapes,
                  input_output_aliases=aliases or {}, compiler_params=_cp(sem, **extra))(*args)
    return list(outs[:n_out]), list(outs[n_out:])


def _hbm_spec():
    return pl.BlockSpec(memory_space=pl.ANY)


def _wgrad(a_list, b, *, name):
    (t, m), n, gm = a_list[0].shape, b.shape[1], a_list[0].shape[1] // WT
    n_a = len(a_list)
    tile = lambda k: (lambda s: jnp.clip(s - k * gm, 0, gm - 1))

    def body(*refs):
        a_refs, b_ref, o_refs = refs[:n_a], refs[n_a], refs[n_a + 1:]
        s = pl.program_id(0)
        for k in range(n_a):
            @pl.when(jnp.logical_and(s >= k * gm, s < (k + 1) * gm))
            def _(k=k):
                o_refs[k][...] = lax.dot_general(a_refs[k][...], b_ref[...], _TN,
                                                 preferred_element_type=F32).astype(BF)

    return _pcall(body, name=name, grid=(n_a * gm,),
                  in_specs=[pl.BlockSpec((t, WT), lambda s, k=k: (0, tile(k)(s))) for k in range(n_a)]
                  + [pl.BlockSpec((t, n), lambda s: (0, 0))],
                  out_specs=[pl.BlockSpec((WT, n), lambda s, k=k: (tile(k)(s), 0)) for k in range(n_a)],
                  out_shape=[jax.ShapeDtypeStruct((m, n), BF)] * n_a,
                  compiler_params=_cp(("arbitrary",)))(*a_list, b)


def _fmm(lhs, rhs, extras, epilogue, outs, *, m, n, tm, tn, name, comm=None, vecs=(), consts=(), sums=()):
    tm, tn = min(tm, m), min(tn, n)
    assert m % tm == 0 and n % tn == 0 and (not sums or tn == n), (name, m, n, tm, tn)
    in_specs, args = [], []
    for a in lhs:
        in_specs.append(pl.BlockSpec((tm, a.shape[1]), lambda i, j: (i, 0)))
        args.append(a)
    for li, b, tb in rhs:
        k = lhs[li].shape[1]
        in_specs.append(pl.BlockSpec((tn, k), lambda i, j: (j, 0)) if tb
                        else pl.BlockSpec((k, tn), lambda i, j: (0, j)))
        args.append(b)
    for arr, w, col in extras:
        in_specs.append(pl.BlockSpec((tm, w), lambda i, j, col=col: (i, col(j))))
        args.append(arr)
    for vec in vecs:
        in_specs.append(pl.BlockSpec((1, tn), lambda i, j: (0, j)))
        args.append(vec)
    for whole in consts:
        in_specs.append(pl.BlockSpec(whole.shape, lambda i, j: (0, 0)))
        args.append(whole)
    out_specs = [pl.BlockSpec((tm, w), lambda i, j, col=col: (i, col(j))) for _, _, w, col in outs]
    out_shape = [jax.ShapeDtypeStruct((m, total), dt) for dt, total, _, _ in outs]
    for w in sums:
        out_specs.append(pl.BlockSpec((1, w), lambda i, j: (0, 0)))
        out_shape.append(jax.ShapeDtypeStruct((1, w), F32))
    nl, nr, ne, no = len(lhs), len(rhs), len(extras) + len(vecs) + len(consts), len(outs)

    def body(*refs):
        prods = []
        for r, (li, _, tb) in enumerate(rhs):
            prods.append(lax.dot_general(refs[li][...], refs[nl + r][...], _NT if tb else _NN,
                                         preferred_element_type=F32))
        vals = epilogue(prods, [ref[...] for ref in refs[nl + nr:nl + nr + ne]])
        o_refs = refs[nl + nr + ne:]
        for o_ref, v in zip(o_refs[:no], vals[:no]):
            o_ref[...] = v.astype(o_ref.dtype)
        for s_ref, v in zip(o_refs[no:], vals[no:]):
            _accum(s_ref, v, pl.program_id(0) == 0)

    gm, gn = m // tm, n // tn
    res, comm_res = _hosted_call(
        body, comm, args, name=name, grid=(gm, gn), in_specs=in_specs, out_specs=out_specs,
        out_shape=out_shape, scratch_shapes=[], sem=("arbitrary", "arbitrary"), nsteps=gm * gn,
        step_fn=lambda: pl.program_id(0) * gn + pl.program_id(1))
    return res if comm is None else (res, comm_res)


def _grp_of(i):
    return [jnp.logical_and(i >= GRP_OFF[g], i < GRP_OFF[g] + GRP_N[g]) for g in range(4)]


def _grp_idx(i, g):
    return jnp.clip(i - GRP_OFF[g], 0, GRP_N[g] - 1)


def _inproj_fwd(u, win_t, b_in, *, t, comm=None):
    tm = min(1024, t)
    n_row = t // tm
    n_chunks, h_first, g_first = 8, 2, 6
    sub = D // WT

    def w_block(l):
        return jnp.where(l == 0, GRP_OFF[0], jnp.where(l == 1, GRP_OFF[1], GRP_OFF[2] + sub * (l - h_first)))

    def body(u_ref, *rest):
        w_refs, b_refs, (q_ref, kv_ref, h3_ref, hf_ref, g_ref) = rest[:sub], rest[sub:2 * sub], rest[2 * sub:]
        l = pl.program_id(1)

        @pl.when(l == 1)
        def _():
            kv_ref[...] = (lax.dot_general(u_ref[...], w_refs[0][...], _NT, preferred_element_type=F32)
                           + b_refs[0][...]).astype(BF)

        is_hf = l == h_first + 1
        in_h3 = jnp.logical_and(jnp.logical_and(l >= h_first, l < g_first), jnp.logical_not(is_hf))
        for pred, o_ref in ((l == 0, q_ref), (in_h3, h3_ref), (is_hf, hf_ref), (l >= g_first, g_ref)):
            @pl.when(pred)
            def _(o_ref=o_ref):
                w = jnp.concatenate([w[...] for w in w_refs], axis=0)
                b = jnp.concatenate([b[...] for b in b_refs], axis=1)
                o_ref[...] = (lax.dot_general(u_ref[...], w, _NT, preferred_element_type=F32) + b).astype(o_ref.dtype)

    return _hosted_call(
        body, comm, [u] + [win_t] * sub + [b_in] * sub, name="inproj_fwd", grid=(n_row, n_chunks),
        in_specs=[pl.BlockSpec((tm, D), lambda i, l: (i, 0))]
        + [pl.BlockSpec((WT, D), lambda i, l, o=o: (w_block(l) + o, 0)) for o in range(sub)]
        + [pl.BlockSpec((1, WT), lambda i, l, o=o: (0, w_block(l) + o)) for o in range(sub)],
        out_specs=[pl.BlockSpec((tm, D), lambda i, l: (i, 0)),
                   pl.BlockSpec((tm, 256), lambda i, l: (i, 0)),
                   pl.BlockSpec((tm, D), lambda i, l: (i, jnp.clip(l - h_first - 1, 0, 2))),
                   pl.BlockSpec((tm, D), lambda i, l: (i, 0)),
                   pl.BlockSpec((tm, D), lambda i, l: (i, jnp.clip(l - g_first, 0, 1)))],
        out_shape=[jax.ShapeDtypeStruct((t, D), BF), jax.ShapeDtypeStruct((t, 256), BF),
                   jax.ShapeDtypeStruct((t, 3 * D), BF), jax.ShapeDtypeStruct((t, D), F32),
                   jax.ShapeDtypeStruct((t, 2 * D), BF)],
        scratch_shapes=[], sem=("arbitrary", "arbitrary"), nsteps=n_row * n_chunks,
        step_fn=lambda: pl.program_id(0) * n_chunks + pl.program_id(1))


def _inproj_bwd_x(dps, win_t, x, g, resid, *, t, part, prev=None, comm=None):
    n_row = 8 if t >= 4096 else 4
    tm = t // n_row
    first = n_row // 4
    per = first if part == 0 else n_row - first
    row = lambda i: part * first + i

    n_chunks = 4
    sub = 2 * D // WT

    def w_block(l):
        return jnp.where(l == 0, 0, GRP_OFF[2] + sub * (l - 1))

    def body(d0, d1, d2, d3, *rest):
        w_refs, (x_ref, g_ref, r_ref) = rest[:sub], rest[sub:sub + 3]
        dg_prev = rest[sub + 3] if prev is not None else None
        o_ref, dg_ref, acc_ref = rest[-3], rest[-2], rest[-1]
        i, l = pl.program_id(0), pl.program_id(1)

        @pl.when(l == 0)
        def _():
            wq = jnp.concatenate([w[...] for w in w_refs[:GRP_N[0]]], axis=0)
            acc_ref[...] = (jnp.dot(d0[...], wq, preferred_element_type=F32)
                            + jnp.dot(d1[...], w_refs[GRP_N[0]][...], preferred_element_type=F32))

        for pred, d_ref in ((jnp.logical_and(l >= 1, l < 3), d2), (l == 3, d3)):
            @pl.when(pred)
            def _(d_ref=d_ref):
                w = jnp.concatenate([w[...] for w in w_refs], axis=0)
                acc_ref[...] += jnp.dot(d_ref[...], w, preferred_element_type=F32)

        @pl.when(l == n_chunks - 1)
        def _():
            xv = x_ref[...]
            r = lax.rsqrt(jnp.mean(xv * xv, axis=-1, keepdims=True) + EPS)
            xh = xv * r
            du = acc_ref[...]
            dxh = du * g_ref[...]
            o_ref[...] = r_ref[...] + r * (dxh - xh * jnp.mean(dxh * xh, axis=-1, keepdims=True))
            dg = jnp.sum(du * xh, axis=0, keepdims=True)
            if dg_prev is not None:
                dg = dg + jnp.where(i == 0, 1.0, 0.0) * dg_prev[...]
            _accum(dg_ref, dg, i == 0)

    rows = lambda w: pl.BlockSpec((tm, w), lambda i, l: (row(i), 0))
    in_specs = ([rows(D), rows(256),
                 pl.BlockSpec((tm, 2 * D), lambda i, l: (row(i), jnp.clip(l - 1, 0, 1))), rows(2 * D)]
                + [pl.BlockSpec((WT, D), lambda i, l, o=o: (w_block(l) + o, 0)) for o in range(sub)]
                + [rows(D), pl.BlockSpec((1, D), lambda i, l: (0, 0)), rows(D)])
    args = list(dps) + [win_t] * sub + [x, g, resid]
    aliases = None
    if prev is not None:
        in_specs += [pl.BlockSpec((1, D), lambda i, l: (0, 0)), _hbm_spec()]
        args += [prev[1], prev[0]]
        aliases = {len(args) - 1: 0}
    return _hosted_call(
        body, comm, args, name="inproj_bwd_x%d" % part, grid=(per, n_chunks), in_specs=in_specs,
        out_specs=[rows(D), pl.BlockSpec((1, D), lambda i, l: (0, 0))],
        out_shape=[jax.ShapeDtypeStruct((t, D), F32), jax.ShapeDtypeStruct((1, D), F32)],
        scratch_shapes=[pltpu.VMEM((tm, D), F32)], sem=("arbitrary", "arbitrary"), nsteps=per * n_chunks,
        step_fn=lambda: pl.program_id(0) * n_chunks + pl.program_id(1), aliases=aliases)


def _inproj_bwd_w(dps, u, *, t):
    n_tiles = IN_W // WT
    dims = (((0,), (0,)), ((), ()))

    def body(d0, d1, d2, d3, u_ref, o_ref, db_ref):
        i = pl.program_id(0)
        uv = u_ref[...]
        for g, (pred, d_ref) in enumerate(zip(_grp_of(i), (d0, d1, d2, d3))):
            @pl.when(pred)
            def _(d_ref=d_ref):
                dv = d_ref[...]
                o_ref[...] = lax.dot_general(dv, uv, dims, preferred_element_type=F32).astype(BF)
                db_ref[...] = jnp.sum(dv.astype(F32), axis=0, keepdims=True)

    return _pcall(body, name="inproj_bwd_w", grid=(n_tiles,),
                  in_specs=[pl.BlockSpec((t, WT), lambda i, g=g: (0, _grp_idx(i, g))) for g in range(4)]
                  + [pl.BlockSpec((t, D), lambda i: (0, 0))],
                  out_specs=[pl.BlockSpec((WT, D), lambda i: (i, 0)),
                             pl.BlockSpec((1, WT), lambda i: (0, i))],
                  out_shape=[jax.ShapeDtypeStruct((IN_W, D), BF), jax.ShapeDtypeStruct((1, IN_W), F32)],
                  compiler_params=_cp(("arbitrary",)))(*dps, u)


def _row_spec(tm, width, col=0):
    return pl.BlockSpec((tm, width), lambda i: (i, col))


def _vec_spec(width):
    return pl.BlockSpec((1, width), lambda i: (0, 0))


def _rms_fwd(x, g, *, tm, name, comm=None):
    t = x.shape[0]
    tm = min(tm, t)

    def body(x_ref, g_ref, u_ref):
        xv = x_ref[...]
        r = lax.rsqrt(jnp.mean(xv * xv, axis=-1, keepdims=True) + EPS)
        u_ref[...] = (xv * r * g_ref[...]).astype(BF)

    (u,), comm_res = _hosted_call(
        body, comm, (x, g), name=name, grid=(t // tm,), in_specs=[_row_spec(tm, D), _vec_spec(D)],
        out_specs=[_row_spec(tm, D)], out_shape=[jax.ShapeDtypeStruct((t, D), BF)], scratch_shapes=[],
        sem=("arbitrary",), nsteps=t // tm, step_fn=lambda: pl.program_id(0))
    return u if comm is None else (u, comm_res)


def _rms_bwd(du, x, g, resid, *, tm, name):
    t = x.shape[0]
    tm = min(tm, t)

    def body(du_ref, x_ref, g_ref, r_ref, dx_ref, dxb_ref, dg_ref):
        xv = x_ref[...]
        r = lax.rsqrt(jnp.mean(xv * xv, axis=-1, keepdims=True) + EPS)
        xh = xv * r
        duv = du_ref[...]
        dxh = duv * g_ref[...]
        dx = r_ref[...] + r * (dxh - xh * jnp.mean(dxh * xh, axis=-1, keepdims=True))
        dx_ref[...] = dx
        dxb_ref[...] = dx.astype(BF)
        _accum(dg_ref, jnp.sum(duv * xh, axis=0, keepdims=True), pl.program_id(0) == 0)

    return _pcall(body, name=name, grid=(t // tm,),
                  in_specs=[_row_spec(tm, D), _row_spec(tm, D), _vec_spec(D), _row_spec(tm, D)],
                  out_specs=[_row_spec(tm, D), _row_spec(tm, D), _vec_spec(D)],
                  out_shape=[jax.ShapeDtypeStruct((t, D), F32), jax.ShapeDtypeStruct((t, D), BF),
                             jax.ShapeDtypeStruct((1, D), F32)],
                  compiler_params=_cp(("arbitrary",)))(du, x, g, resid)


def _attn_kv_tiles(kprev, kcur):
    kv = jnp.concatenate([kprev, kcur], axis=0).astype(F32)
    lo = lax.broadcasted_iota(jnp.int32, (2 * BLK, 128), 1) < HEAD
    tiles = []
    for part in (kv[:, 0:128], kv[:, 128:256]):
        rolled = pltpu.roll(part, HEAD, 1)
        z = jnp.zeros_like(part)
        tiles.append(((jnp.where(lo, part, z).astype(BF), jnp.where(lo, z, rolled).astype(BF)),
                      (jnp.where(lo, rolled, z).astype(BF), jnp.where(lo, z, part).astype(BF))))
    k_t, v_t = tiles
    return [(jnp.concatenate(k_t[h], axis=0), jnp.concatenate(v_t[h], axis=0)) for h in range(2)]


def _attn_mask(i):
    qi = lax.broadcasted_iota(jnp.int32, (BLK, 2 * BLK), 0)
    kj = lax.broadcasted_iota(jnp.int32, (BLK, 2 * BLK), 1)
    first_key = jnp.where(i == 0, BLK, 0)
    in_prev = jnp.logical_and(jnp.logical_and(kj < BLK, kj > qi), kj >= first_key)
    in_cur = jnp.logical_and(kj >= BLK, kj - BLK <= qi)
    return jnp.logical_or(in_prev, in_cur)


def _attn_probs(s, sink, valid):
    s = jnp.where(valid, s * SCALE, NEG)
    mx = jnp.maximum(jnp.max(s, axis=-1, keepdims=True), sink)
    e = jnp.exp(s - mx)
    es = jnp.exp(sink - mx)
    inv = 1.0 / (jnp.sum(e, axis=-1, keepdims=True) + es)
    return e * inv, es * inv


_KEYS = 2 * BLK


def _pair(ref, j):
    return ref[:, j * 128:(j + 1) * 128]


def _attn_fwd(q, kv, sinks, *, t, comm=None):
    nb = t // BLK

    def body(sink_ref, q_ref, kp_ref, kc_ref, o_ref):
        i = pl.program_id(0)
        for c in range(2):
            rows = slice(c * BLK, (c + 1) * BLK)
            valid = _attn_mask(2 * i + c)
            tiles = _attn_kv_tiles(kp_ref[...] if c == 0 else kc_ref[0:BLK, :], kc_ref[rows, :])
            s = [lax.dot_general(q_ref[rows, j * 128:(j + 1) * 128], tiles[j // 4][0], _NT,
                                 preferred_element_type=F32) for j in range(N_PAIR)]
            p = []
            for j in range(N_PAIR):
                pe, _ = _attn_probs(s[j][:, 0:_KEYS], sink_ref[0, 2 * j], valid)
                po, _ = _attn_probs(s[j][:, _KEYS:2 * _KEYS], sink_ref[0, 2 * j + 1], valid)
                p.append(jnp.concatenate([pe.astype(BF), po.astype(BF)], axis=1))
            for j in range(N_PAIR):
                o_ref[rows, j * 128:(j + 1) * 128] = jnp.dot(p[j], tiles[j // 4][1],
                                                             preferred_element_type=F32).astype(BF)

    return _hosted_call(
        body, comm, (sinks, q, kv, kv), name="attn_fwd", grid=(nb // 2,),
        in_specs=[pl.BlockSpec(memory_space=pltpu.SMEM),
                  pl.BlockSpec((2 * BLK, D), lambda i: (i, 0)),
                  pl.BlockSpec((BLK, 256), lambda i: (jnp.maximum(2 * i - 1, 0), 0)),
                  pl.BlockSpec((2 * BLK, 256), lambda i: (i, 0))],
        out_specs=[pl.BlockSpec((2 * BLK, D), lambda i: (i, 0))],
        out_shape=[jax.ShapeDtypeStruct((t, D), BF)],
        scratch_shapes=[], sem=("arbitrary",), nsteps=nb // 2, step_fn=lambda: pl.program_id(0))


def _attn_bwd(q, kv, sinks, do, *, t, comm=None):
    nb = t // BLK
    last = nb - 1

    def body(sink_ref, q_ref, kp_ref, kc_ref, do_ref, dq_ref, dkv_ref, ds_ref, carry_ref):
        i = pl.program_id(0)

        @pl.when(i == 0)
        def _():
            ds_ref[...] = jnp.zeros_like(ds_ref)
            carry_ref[...] = jnp.zeros_like(carry_ref)

        @pl.when(i < nb)
        def _():
            valid = _attn_mask(i)
            tiles = _attn_kv_tiles(kp_ref[...], kc_ref[...])
            lane1 = lax.broadcasted_iota(jnp.int32, (1, 128), 1)
            dsink = jnp.zeros((1, 128), F32)
            s = [lax.dot_general(_pair(q_ref, j), tiles[j // 4][0], _NT, preferred_element_type=F32)
                 for j in range(N_PAIR)]
            dp = [lax.dot_general(_pair(do_ref, j), tiles[j // 4][1], _NT, preferred_element_type=F32)
                  for j in range(N_PAIR)]
            p_all, ds_all = [], []
            for j in range(N_PAIR):
                halves = []
                for par in range(2):
                    cols = slice(par * _KEYS, (par + 1) * _KEYS)
                    p, ps = _attn_probs(s[j][:, cols], sink_ref[0, 2 * j + par], valid)
                    dpj = dp[j][:, cols]
                    dd = jnp.sum(p * dpj, axis=-1, keepdims=True)
                    dsink = dsink + jnp.where(lane1 == 2 * j + par,
                                              -jnp.sum(ps * dd, axis=0, keepdims=True), 0.0)
                    halves.append((p.astype(BF), (p * (dpj - dd)).astype(BF)))
                p_all.append(jnp.concatenate([halves[0][0], halves[1][0]], axis=1))
                ds_all.append(jnp.concatenate([halves[0][1], halves[1][1]], axis=1))
            for j in range(N_PAIR):
                dq_ref[:, j * 128:(j + 1) * 128] = (
                    jnp.dot(ds_all[j], tiles[j // 4][0], preferred_element_type=F32) * SCALE).astype(BF)
            ds_ref[...] += dsink
            gk, gv = [], []
            for h in range(2):
                grp = range(4 * h, 4 * h + 4)
                q_rows = jnp.concatenate([_pair(q_ref, j) for j in grp], axis=0)
                do_rows = jnp.concatenate([_pair(do_ref, j) for j in grp], axis=0)
                g_k = lax.dot_general(jnp.concatenate([ds_all[j] for j in grp], axis=0), q_rows, _TN,
                                      preferred_element_type=F32)
                g_v = lax.dot_general(jnp.concatenate([p_all[j] for j in grp], axis=0), do_rows, _TN,
                                      preferred_element_type=F32)
                gk.append((g_k[0:_KEYS], g_k[_KEYS:2 * _KEYS]))
                gv.append((g_v[0:_KEYS], g_v[_KEYS:2 * _KEYS]))
            lo = lax.broadcasted_iota(jnp.int32, (2 * BLK, 128), 1) < HEAD
            zero = jnp.zeros((2 * BLK, 128), F32)

            def unpad(g):
                return (jnp.where(lo, g[0][0] + pltpu.roll(g[0][1], HEAD, 1), zero)
                        + jnp.where(lo, zero, pltpu.roll(g[1][0], HEAD, 1) + g[1][1]))

            dk = unpad(gk) * SCALE
            dv = unpad(gv)
            dkv_ref[:, 0:128] = (carry_ref[:, 0:128] + dk[0:BLK]).astype(BF)
            dkv_ref[:, 128:256] = (carry_ref[:, 128:256] + dv[0:BLK]).astype(BF)
            carry_ref[:, 0:128] = dk[BLK:2 * BLK]
            carry_ref[:, 128:256] = dv[BLK:2 * BLK]

        @pl.when(i == nb)
        def _():
            dkv_ref[...] = carry_ref[...].astype(BF)

    return _hosted_call(
        body, comm, (sinks, q, kv, kv, do), name="attn_bwd", grid=(nb + 1,),
        in_specs=[pl.BlockSpec(memory_space=pltpu.SMEM),
                  pl.BlockSpec((BLK, D), lambda i: (jnp.minimum(i, last), 0)),
                  pl.BlockSpec((BLK, 256), lambda i: (jnp.clip(i - 1, 0, last), 0)),
                  pl.BlockSpec((BLK, 256), lambda i: (jnp.minimum(i, last), 0)),
                  pl.BlockSpec((BLK, D), lambda i: (jnp.minimum(i, last), 0))],
        out_specs=[pl.BlockSpec((BLK, D), lambda i: (jnp.minimum(i, last), 0)),
                   pl.BlockSpec((BLK, 256), lambda i: (jnp.maximum(i - 1, 0), 0)),
                   pl.BlockSpec((1, 128), lambda i: (0, 0))],
        out_shape=[jax.ShapeDtypeStruct((t, D), BF), jax.ShapeDtypeStruct((t, 256), BF),
                   jax.ShapeDtypeStruct((1, 128), F32)],
        scratch_shapes=[pltpu.VMEM((BLK, 256), F32)], sem=("arbitrary",), nsteps=nb + 1,
        step_fn=lambda: pl.program_id(0))


def _split3(v):
    h = v.astype(BF)
    r = v - h.astype(F32)
    m = r.astype(BF)
    lo = (r - m.astype(F32)).astype(BF)
    return jnp.concatenate([h, m, lo], axis=1)


def _apply01(mat, v):
    n = v.shape[1]
    r = jnp.dot(mat, _split3(v), preferred_element_type=F32)
    return r[:, 0:n] + r[:, n:2 * n] + r[:, 2 * n:3 * n]


def _hgrn_gates(hq, hf, lb):
    sq = _sig(hq)
    sg = _sig(hf)
    f = lb + (1.0 - lb) * sg
    return hq * sq, (1.0 - lb) * (1.0 - sg), jnp.log(f), sq, sg, f


def _tri(upper):
    r = lax.broadcasted_iota(jnp.int32, (CH, CH), 0)
    c = lax.broadcasted_iota(jnp.int32, (CH, CH), 1)
    return (c >= r) if upper else (c <= r)


def _lb_from_logits(lg_ref):
    return 1.0 / (1.0 + jnp.exp(lg_ref[1:2, :] - lg_ref[0:1, :]))


def _hgrn_fwd(h3, hf, logits, norm_g, *, t, comm=None):
    nc = t // CH
    nt_dims = (((1,), (1,)), ((), ()))
    tn_dims = (((0,), (0,)), ((), ()))

    def body(h_ref, hf_ref, lg_ref, ng_ref, y_ref, o_ref, st_ref, s_scr, b_scr, qa_s, ka_s, qb_s, kb_s, v_s):
        @pl.when(pl.program_id(0) == 0)
        def _():
            s_scr[...] = jnp.zeros_like(s_scr)

        heads = [slice(h * HG_K, (h + 1) * HG_K) for h in range(HG_HEADS)]
        causal = _tri(False)
        lb = _lb_from_logits(lg_ref)
        for c in range(HG_SUB):
            rows = slice(c * CH, (c + 1) * CH)
            q, k, g, _, _, _ = _hgrn_gates(h_ref[rows, 0:D].astype(F32), hf_ref[rows, :], lb)
            b_scr[...] = _apply01(jnp.where(causal, 1.0, 0.0).astype(BF), g)
            b = b_scr[...]
            b_mid = b_scr[CH // 2 - 1:CH // 2, :]
            b_last = b_scr[CH - 1:CH, :]
            qa_s[...] = (q * jnp.exp(b - b_mid)).astype(BF)
            ka_s[...] = (k * jnp.exp(b_mid - b)).astype(BF)
            qb_s[...] = (q * jnp.exp(b)).astype(BF)
            kb_s[...] = (k * jnp.exp(b_last - b)).astype(BF)
            v_s[...] = h_ref[rows, D:2 * D]
            dec = jnp.exp(b_last)
            st_ref[c] = s_scr[...].astype(BF)
            a = [jnp.where(causal, lax.dot_general(qa_s[:, sl], ka_s[:, sl], nt_dims, preferred_element_type=F32),
                           0.0).astype(BF) for sl in heads]
            for h, sl in enumerate(heads):
                o_ref[rows, sl] = (jnp.dot(a[h], v_s[:, sl], preferred_element_type=F32)
                                   + lax.dot_general(qb_s[:, sl], s_scr[h].astype(BF), nt_dims,
                                                     preferred_element_type=F32))
            for h, sl in enumerate(heads):
                s_scr[h] = dec[:, sl] * s_scr[h] + lax.dot_general(v_s[:, sl], kb_s[:, sl], tn_dims,
                                                                   preferred_element_type=F32)
            for h, sl in enumerate(heads):
                o = o_ref[rows, sl]
                on = o * lax.rsqrt(jnp.mean(o * o, axis=-1, keepdims=True) + EPS)
                gate = _sig(h_ref[rows, 2 * D + h * HG_K:2 * D + (h + 1) * HG_K].astype(F32))
                y_ref[rows, sl] = (on * ng_ref[:, sl] * gate).astype(BF)

    half = lambda: pltpu.VMEM((CH, D), BF)
    blk = HG_SUB * CH
    return _hosted_call(
        body, comm, (h3, hf, logits, norm_g), name="hgrn_fwd", grid=(nc // HG_SUB,),
        in_specs=[pl.BlockSpec((blk, 3 * D), lambda n: (n, 0)),
                  pl.BlockSpec((blk, D), lambda n: (n, 0)),
                  pl.BlockSpec((2, D), lambda n: (0, 0)),
                  pl.BlockSpec((1, D), lambda n: (0, 0))],
        out_specs=[pl.BlockSpec((blk, D), lambda n: (n, 0)),
                   pl.BlockSpec((blk, D), lambda n: (n, 0)),
                   pl.BlockSpec((HG_SUB, HG_HEADS, HG_K, HG_K), lambda n: (n, 0, 0, 0))],
        out_shape=[jax.ShapeDtypeStruct((t, D), BF), jax.ShapeDtypeStruct((t, D), F32),
                   jax.ShapeDtypeStruct((nc, HG_HEADS, HG_K, HG_K), BF)],
        scratch_shapes=[pltpu.VMEM((HG_HEADS, HG_K, HG_K), F32), pltpu.VMEM((CH, D), F32),
                        half(), half(), half(), half(), half()],
        sem=("arbitrary",), nsteps=nc // HG_SUB, step_fn=lambda: pl.program_id(0))


def _hgrn_bwd(h3, hf, logits, norm_g, o_pre, states, dy, *, t, comm=None):
    nc = t // CH
    nt_dims = (((1,), (1,)), ((), ()))
    tn_dims = (((0,), (0,)), ((), ()))

    def body(h_ref, hf_ref, lg_ref, ng_ref, o_ref, st_ref, dy_ref, dh_ref, dlg_ref, dng_ref, ds_scr, dlb_scr,
             b_scr, tail_s, e_qa, e_ka, e_qb, e_kb, q_s, k_s, dqa_s, dka_s, dqb_s, dkb_s,
             qa_s, ka_s, qb_s, kb_s, v_s, do_s):
        n = pl.program_id(0)

        @pl.when(n == 0)
        def _():
            ds_scr[...] = jnp.zeros_like(ds_scr)
            dlb_scr[...] = jnp.zeros_like(dlb_scr)
            dng_ref[...] = jnp.zeros_like(dng_ref)

        heads = [slice(h * HG_K, (h + 1) * HG_K) for h in range(HG_HEADS)]
        lb = _lb_from_logits(lg_ref)
        causal = _tri(False)

        def chunk(c):
            rows = slice(c * CH, (c + 1) * CH)
            hq = h_ref[rows, 0:D].astype(F32)
            q, k, g, sq, sg, f = _hgrn_gates(hq, hf_ref[rows, :], lb)
            b_scr[...] = _apply01(jnp.where(causal, 1.0, 0.0).astype(BF), g)
            b = b_scr[...]
            b_mid = b_scr[CH // 2 - 1:CH // 2, :]
            b_last = b_scr[CH - 1:CH, :]
            q_s[...] = q
            k_s[...] = k
            for e_ref, s_ref, base, expo in ((e_qa, qa_s, q, b - b_mid), (e_ka, ka_s, k, b_mid - b),
                                             (e_qb, qb_s, q, b), (e_kb, kb_s, k, b_last - b)):
                e = jnp.exp(expo)
                e_ref[...] = e
                s_ref[...] = (base * e).astype(BF)
            v_s[...] = h_ref[rows, D:2 * D]
            dec = jnp.exp(b_last)
            for h, sl in enumerate(heads):
                gcol = slice(3 * D + h * HG_K, 3 * D + (h + 1) * HG_K)
                ngh = ng_ref[:, sl]
                sgate = _sig(h_ref[rows, 2 * D + h * HG_K:2 * D + (h + 1) * HG_K].astype(F32))
                o = o_ref[rows, sl]
                r = lax.rsqrt(jnp.mean(o * o, axis=-1, keepdims=True) + EPS)
                on = o * r
                dyh = dy_ref[rows, sl]
                dh_ref[rows, gcol] = (dyh * on * ngh * sgate * (1.0 - sgate)).astype(BF)
                dng_ref[:, sl] += jnp.sum(dyh * on * sgate, axis=0, keepdims=True)
                don = dyh * ngh * sgate
                do_s[:, sl] = (r * (don - on * jnp.mean(don * on, axis=-1, keepdims=True))).astype(BF)
            a = [jnp.where(causal, lax.dot_general(qa_s[:, sl], ka_s[:, sl], nt_dims, preferred_element_type=F32),
                           0.0).astype(BF) for sl in heads]
            da = [jnp.where(causal, lax.dot_general(do_s[:, sl], v_s[:, sl], nt_dims, preferred_element_type=F32),
                            0.0).astype(BF) for sl in heads]
            for h, sl in enumerate(heads):
                dh_ref[rows, 2 * D + h * HG_K:2 * D + (h + 1) * HG_K] = (
                    lax.dot_general(a[h], do_s[:, sl], tn_dims, preferred_element_type=F32)
                    + lax.dot_general(kb_s[:, sl], ds_scr[h].astype(BF), nt_dims, preferred_element_type=F32)
                ).astype(BF)
            for h, sl in enumerate(heads):
                dqa_s[:, sl] = jnp.dot(da[h], ka_s[:, sl], preferred_element_type=F32)
            for h, sl in enumerate(heads):
                dka_s[:, sl] = lax.dot_general(da[h], qa_s[:, sl], tn_dims, preferred_element_type=F32)
            for h, sl in enumerate(heads):
                dqb_s[:, sl] = jnp.dot(do_s[:, sl], st_ref[c, h], preferred_element_type=F32)
            for h, sl in enumerate(heads):
                dkb_s[:, sl] = jnp.dot(v_s[:, sl], ds_scr[h].astype(BF), preferred_element_type=F32)
            for h, sl in enumerate(heads):
                tail_s[:, sl] = jnp.sum(dec[:, sl] * st_ref[c, h].astype(F32) * ds_scr[h], axis=0, keepdims=True)
            for h, sl in enumerate(heads):
                ds_scr[h] = (lax.dot_general(do_s[:, sl], qb_s[:, sl], tn_dims, preferred_element_type=F32)
                             + dec[:, sl] * ds_scr[h])
            qv, kv = q_s[...], k_s[...]
            dqa, dka, dqb, dkb = dqa_s[...], dka_s[...], dqb_s[...], dkb_s[...]
            eqa, eka, eqb, ekb = e_qa[...], e_ka[...], e_qb[...], e_kb[...]
            dkb_kb = dkb * (kv * ekb)
            db_last = jnp.sum(dkb_kb, axis=0, keepdims=True) + tail_s[...]
            last_row = lax.broadcasted_iota(jnp.int32, (CH, D), 0) == CH - 1
            db = (dqa * (qv * eqa) - dka * (kv * eka) + dqb * (qv * eqb) - dkb_kb
                  + jnp.where(last_row, db_last, 0.0))
            dg = _apply01(jnp.where(_tri(True), 1.0, 0.0).astype(BF), db)
            dq = dqa * eqa + dqb * eqb
            dk = dka * eka + dkb * ekb
            dh_ref[rows, 0:D] = (dq * sq * (1.0 + hq * (1.0 - sq))).astype(BF)
            dfk = dg / f - dk
            dh_ref[rows, D:2 * D] = ((1.0 - lb) * dfk * sg * (1.0 - sg)).astype(BF)
            dlb_scr[...] += jnp.sum((1.0 - sg) * dfk, axis=0, keepdims=True)

        for c in reversed(range(HG_SUB)):
            chunk(c)

        @pl.when(n == nc // HG_SUB - 1)
        def _():
            dl0 = dlb_scr[...] * lb * (1.0 - lb)
            dlg_ref[0:1, :] = dl0
            dlg_ref[1:2, :] = -dl0

    steps = nc // HG_SUB
    blk = HG_SUB * CH
    rev = lambda n: (steps - 1 - n, 0)
    return _hosted_call(
        body, comm, (h3, hf, logits, norm_g, o_pre, states, dy), name="hgrn_bwd", grid=(steps,),
        in_specs=[pl.BlockSpec((blk, 3 * D), rev),
                  pl.BlockSpec((blk, D), rev),
                  pl.BlockSpec((2, D), lambda n: (0, 0)),
                  pl.BlockSpec((1, D), lambda n: (0, 0)),
                  pl.BlockSpec((blk, D), rev),
                  pl.BlockSpec((HG_SUB, HG_HEADS, HG_K, HG_K), lambda n: (steps - 1 - n, 0, 0, 0)),
                  pl.BlockSpec((blk, D), rev)],
        out_specs=[pl.BlockSpec((blk, 4 * D), rev),
                   pl.BlockSpec((2, D), lambda n: (0, 0)),
                   pl.BlockSpec((1, D), lambda n: (0, 0))],
        out_shape=[jax.ShapeDtypeStruct((t, 4 * D), BF), jax.ShapeDtypeStruct((2, D), F32),
                   jax.ShapeDtypeStruct((1, D), F32)],
        scratch_shapes=([pltpu.VMEM((HG_HEADS, HG_K, HG_K), F32), pltpu.VMEM((1, D), F32),
                         pltpu.VMEM((CH, D), F32), pltpu.VMEM((1, D), F32)]
                        + [pltpu.VMEM((CH, D), F32)] * 10 + [pltpu.VMEM((CH, D), BF)] * 6),
        sem=("arbitrary",), nsteps=steps, step_fn=lambda: pl.program_id(0))


def _place():
    x, y, c = lax.axis_index("x"), lax.axis_index("y"), lax.axis_index("c")
    return x, y, c, [(1 - x, y), (x, 1 - y), (1 - x, 1 - y)]


def _gather_comm(shards, mids):
    n, pieces = len(shards), len(mids)
    r = [s.shape[0] for s in shards]
    tile = 16
    cut = [[(rw // tile * p // pieces) * tile for p in range(pieces + 1)] for rw in r]
    size = [[cut[w][p + 1] - cut[w][p] for p in range(pieces)] for w in range(n)]

    def tools(ins, outs, sems):
        send_sems, recv_sems, local_sems = sems
        x, y, c, _ = _place()
        me, sib = (x, y, c), (x, y, 1 - c)
        near = [(x ^ c, y ^ (1 - c), c), (x ^ (1 - c), y ^ c, c), (1 - x, 1 - y, c)]

        def rows(w, p, dev):
            return outs[w].at[pl.ds((4 * dev[0] + 2 * dev[1] + dev[2]) * r[w] + cut[w][p], size[w][p]), :]

        def copy(kind, w, p, block, to, own=False):
            src = ins[w].at[pl.ds(cut[w][p], size[w][p]), :] if own else rows(w, p, block)
            return pltpu.make_async_remote_copy(
                src_ref=src, dst_ref=rows(w, p, block), send_sem=send_sems.at[p, kind],
                recv_sem=recv_sems.at[p, kind], device_id=to, device_id_type=MESH)

        def all_of(kind, p):
            whole = outs[0].at[pl.ds(0, sum(size[w][p] for w in range(n))), :]
            return pltpu.make_async_remote_copy(
                src_ref=whole, dst_ref=whole, send_sem=send_sems.at[p, kind], recv_sem=recv_sems.at[p, kind],
                device_id=me, device_id_type=MESH)

        mine = [pltpu.make_async_copy(ins[w], outs[w].at[pl.ds((4 * x + 2 * y + c) * r[w], r[w]), :],
                                      local_sems.at[w]) for w in range(n)]
        return near, me, sib, copy, all_of, mine

    def start(ins, outs, sems):
        near, me, sib, copy, _, mine = tools(ins, outs, sems)
        for cp in mine:
            cp.start()
        for p in range(pieces):
            for w in range(n):
                copy(0, w, p, me, sib, own=True).start()
                copy(1, w, p, me, near[0], own=True).start()
                copy(2, w, p, me, near[1], own=True).start()

    def pass_diagonal(p, near, sib, copy, all_of):
        all_of(3, p).wait_recv()
        for w in range(n):
            copy(6, w, p, near[2], sib).start()

    def pass_on(p):
        def phase(ins, outs, sems):
            near, _, sib, copy, all_of, _ = tools(ins, outs, sems)
            all_of(1, p).wait_recv()
            for w in range(n):
                copy(3, w, p, near[0], near[1]).start()
                copy(4, w, p, near[0], sib).start()
            all_of(2, p).wait_recv()
            for w in range(n):
                copy(5, w, p, near[1], sib).start()
            if p > 0:
                pass_diagonal(p - 1, near, sib, copy, all_of)
        return phase

    def finish(ins, outs, sems):
        near, _, sib, copy, all_of, mine = tools(ins, outs, sems)
        pass_diagonal(pieces - 1, near, sib, copy, all_of)
        for p in range(pieces):
            all_of(0, p).wait_recv()
            for kind in (4, 5, 6):
                all_of(kind, p).wait_recv()
            for kind in range(7):
                all_of(kind, p).wait_send()
        for cp in mine:
            cp.wait()

    return _Comm(shards, [jax.ShapeDtypeStruct((N_DEV * rw, D), BF) for rw in r],
                 [pltpu.SemaphoreType.DMA((pieces, 7)), pltpu.SemaphoreType.DMA((pieces, 7)),
                  pltpu.SemaphoreType.DMA((n,))],
                 [(0.0, start)] + [(f, pass_on(p)) for p, f in enumerate(mids)] + [(1.0, finish)])


def _pair_comm(grads):
    n = len(grads)
    r = [g.shape[0] // N_DEV for g in grads]

    def start(ins, outs, sems):
        send_sems, recv_sems = sems
        x, y, c, _ = _place()
        for w in range(n):
            for a in range(N_CHIP):
                pltpu.make_async_remote_copy(
                    src_ref=ins[w].at[pl.ds((2 * a + 1 - c) * r[w], r[w]), :], dst_ref=outs[w].at[a],
                    send_sem=send_sems.at[w], recv_sem=recv_sems.at[w],
                    device_id=(x, y, 1 - c), device_id_type=MESH).start()

    def finish(ins, outs, sems):
        send_sems, recv_sems = sems
        x, y, c, _ = _place()
        for w in range(n):
            pltpu.make_async_remote_copy(
                src_ref=outs[w], dst_ref=outs[w], send_sem=send_sems.at[w], recv_sem=recv_sems.at[w],
                device_id=(x, y, c), device_id_type=MESH).wait()

    return _Comm(grads, [jax.ShapeDtypeStruct((N_CHIP, rw, D), BF) for rw in r],
                 [pltpu.SemaphoreType.DMA((n,)), pltpu.SemaphoreType.DMA((n,))],
                 [(0.0, start), (1.0, finish)])


def _pair_add(grad, got, core, *, name):
    r = got.shape[1]

    def body(c_ref, g_ref, got_ref, o_ref):
        o_ref[0] = (g_ref[...].astype(F32) + got_ref[0].astype(F32)).astype(BF)

    grid_spec = pltpu.PrefetchScalarGridSpec(
        num_scalar_prefetch=1, grid=(N_CHIP,),
        in_specs=[pl.BlockSpec((r, D), lambda a, c_ref: (2 * a + c_ref[0], 0)),
                  pl.BlockSpec((1, r, D), lambda a, c_ref: (a, 0, 0))],
        out_specs=pl.BlockSpec((1, r, D), lambda a, c_ref: (a, 0, 0)))
    return _pcall(body, name=name, grid_spec=grid_spec,
                  out_shape=jax.ShapeDtypeStruct((N_CHIP, r, D), BF),
                  compiler_params=_cp(("parallel",)))(core, grad, got)


def _chip_comm(pair_sums):
    n = len(pair_sums)
    r = [p.shape[1] for p in pair_sums]
    off = [sum(r[:w]) for w in range(n)]

    def tools(ins, outs, sems):
        send_sems, recv_sems, local_sems = sems
        x, y, c, chips = _place()
        my_chip = 2 * x + y

        def slot(w):
            return outs[0].at[my_chip, pl.ds(off[w], r[w]), :]

        own = [pltpu.make_async_copy(ins[w].at[my_chip], slot(w), local_sems.at[w]) for w in range(n)]
        return x, y, c, chips, my_chip, slot, own, send_sems, recv_sems

    def start(ins, outs, sems):
        x, y, c, chips, my_chip, slot, own, send_sems, recv_sems = tools(ins, outs, sems)
        for cp in own:
            cp.start()
        for j, chip in enumerate(chips):
            for w in range(n):
                pltpu.make_async_remote_copy(
                    src_ref=ins[w].at[2 * chip[0] + chip[1]], dst_ref=slot(w), send_sem=send_sems.at[j],
                    recv_sem=recv_sems.at[j], device_id=(*chip, c), device_id_type=MESH).start()

    def finish(ins, outs, sems):
        x, y, c, chips, my_chip, slot, own, send_sems, recv_sems = tools(ins, outs, sems)
        whole = outs[0].at[my_chip]
        for j in range(3):
            pltpu.make_async_remote_copy(
                src_ref=whole, dst_ref=whole, send_sem=send_sems.at[j], recv_sem=recv_sems.at[j],
                device_id=(x, y, c), device_id_type=MESH).wait()
        for cp in own:
            cp.wait()

    return _Comm(pair_sums, [jax.ShapeDtypeStruct((N_CHIP, sum(r), D), BF)],
                 [pltpu.SemaphoreType.DMA((3,)), pltpu.SemaphoreType.DMA((3,)), pltpu.SemaphoreType.DMA((n,))],
                 [(0.0, start), (1.0, finish)])


def _adam_math(w, g, m, v):
    m = ADAM_B1 * m + (1.0 - ADAM_B1) * g
    v = ADAM_B2 * v + (1.0 - ADAM_B2) * (g * g)
    m_hat = m / (1.0 - ADAM_B1 ** ADAM_STEP)
    v_hat = v / (1.0 - ADAM_B2 ** ADAM_STEP)
    delta = -ADAM_LR * (m_hat / (jnp.sqrt(v_hat) + ADAM_EPS) + ADAM_WD * w)
    return delta, m, v


SMALL = (("norm_mix_g", (1, D), 0), ("hgrn_norm_g", (1, D), 1), ("norm_ffn_g", (1, D), 2),
         ("norm_final_g", (1, D), 3), ("hgrn_lb_logits", (2, D), 4), ("attn_sinks", (1, 16), 6),
         ("b_in", (1, IN_W), 8))
LOSS_ROW = 7


def _small_allreduce_adam(grads, loss_row, params):
    n = len(SMALL)

    def rows_of(ref, shape, row):
        r, w = shape
        if w <= D:
            return ref[row:row + r, 0:w]
        pieces = [ref[row + k:row + k + 1, :] for k in range(-(-w // D))]
        return jnp.concatenate(pieces, axis=1)[:, 0:w]

    def body(*refs):
        g_refs, loss_ref = refs[:n], refs[n]
        wmv = refs[n + 1:4 * n + 1]
        loss_out = refs[4 * n + 1]
        outs = refs[4 * n + 2:8 * n + 2]
        mine, total, gath, send_sems, recv_sems = refs[8 * n + 2:]
        x, y, c, _ = _place()
        me = 4 * x + 2 * y + c
        mine[...] = jnp.zeros_like(mine)
        for g_ref, (_, (r, w), row) in zip(g_refs, SMALL):
            for k in range(-(-w // D)):
                wk = min(D, w - k * D)
                mine[row + k:row + k + r, 0:wk] = g_ref[:, k * D:k * D + wk]
        mine[LOSS_ROW:LOSS_ROW + 1, 0:128] = loss_ref[...]
        gath[me] = mine[...]
        cps = []
        for d in range(1, N_DEV):
            peer = (x ^ (d >> 2), y ^ ((d >> 1) & 1), c ^ (d & 1))
            cps.append(pltpu.make_async_remote_copy(
                src_ref=mine, dst_ref=gath.at[me], send_sem=send_sems.at[d - 1],
                recv_sem=recv_sems.at[d - 1], device_id=peer, device_id_type=MESH))
        for cp in cps:
            cp.start()
        for cp in cps:
            cp.wait()
        g = gath[0]
        for k in range(1, N_DEV):
            g = g + gath[k]
        total[...] = g
        loss_out[...] = total[LOSS_ROW:LOSS_ROW + 1, 0:128]
        for i, (_, shape, row) in enumerate(SMALL):
            gi = rows_of(total, shape, row)
            w_ref, m_ref, v_ref = wmv[3 * i:3 * i + 3]
            o = outs[4 * i:4 * i + 4]
            o[0][...] = gi
            o[1][...], o[2][...], o[3][...] = _adam_math(w_ref[...], gi, m_ref[...], v_ref[...])

    vm = pl.BlockSpec(memory_space=pltpu.VMEM)
    ins = [grads[name] for name, _, _ in SMALL] + [loss_row]
    for name, _, _ in SMALL:
        ins += list(params[name])
    out_shape = [jax.ShapeDtypeStruct((1, 128), F32)]
    for _, shape, _ in SMALL:
        out_shape += [jax.ShapeDtypeStruct(shape, F32)] * 4
    res = _pcall(body, name="small_allreduce_adam", in_specs=[vm] * len(ins), out_specs=[vm] * len(out_shape),
                 out_shape=out_shape,
                 scratch_shapes=[pltpu.VMEM((SMALL_ROWS, D), F32), pltpu.VMEM((SMALL_ROWS, D), F32),
                                 pltpu.VMEM((N_DEV, SMALL_ROWS, D), F32),
                                 pltpu.SemaphoreType.DMA((N_DEV - 1,)), pltpu.SemaphoreType.DMA((N_DEV - 1,))],
                 compiler_params=pltpu.CompilerParams(has_side_effects=True))(*ins)
    return res[0], {name: res[1 + 4 * i:5 + 4 * i] for i, (name, _, _) in enumerate(SMALL)}


def _adam(w, parts, index, m, v, *, name):
    rows = w.shape[0]
    tr = rows if rows <= 512 else rows // 2
    steps = rows // tr

    def body(w_ref, p_ref, m_ref, v_ref, g_ref, d_ref, mo_ref, vo_ref):
        g = p_ref[0].astype(F32)
        for a in range(1, N_CHIP):
            g = g + p_ref[a].astype(F32)
        g_ref[...] = g
        d_ref[...], mo_ref[...], vo_ref[...] = _adam_math(w_ref[...], g, m_ref[...], v_ref[...])

    spec = pl.BlockSpec((tr, D), lambda i: (i, 0))
    return _pcall(body, name=name, grid=(steps,),
                  in_specs=[spec, pl.BlockSpec((N_CHIP, tr, D), lambda i: (0, index * steps + i, 0)), spec, spec],
                  out_specs=[spec] * 4, out_shape=[jax.ShapeDtypeStruct((rows, D), F32)] * 4,
                  compiler_params=_cp(("parallel",)))(w, parts, m, v)


def _step(x, tgt, shards, norm_mix_g, b_in, sinks, logits, hgrn_norm_g, norm_ffn_g, norm_final_g):
    t = x.shape[0]
    core = lax.axis_index("c").astype(jnp.int32).reshape(1)

    u1, (win_t,) = _rms_fwd(x, norm_mix_g, tm=512, name="rms_mix", comm=_gather_comm(shards[0:1], (0.2, 0.4, 0.6, 0.8)))
    (q, kv, h3, hf, gates), (wg_t, wba, wbh, wout) = _inproj_fwd(
        u1, win_t, b_in, t=t, comm=_gather_comm([shards[1]] + shards[4:7], (0.2, 0.35, 0.5, 0.65)))
    (y_attn,), _ = _attn_fwd(q, kv, sinks, t=t)
    (y_hgrn, o_pre, states), (wu_t, wd) = _hgrn_fwd(h3, hf, logits, hgrn_norm_g, t=t,
                                                    comm=_gather_comm(shards[2:4], (0.25, 0.45, 0.65, 0.85)))
    col = lambda j: j
    first, second = (lambda j: 0), (lambda j: 1)
    gate_tiles = [(gates, D, first), (gates, D, second)]

    def merge(prods, ex):
        (ya_, yb_), (ga, gb) = prods, ex
        sa, sb = _sig(ga.astype(F32)), _sig(gb.astype(F32))
        return sa, sb, ya_ * sa * (1.0 - sa), yb_ * sb * (1.0 - sb), sa * ya_ + sb * yb_

    sig_a, sig_b, dgate_a, dgate_b, merged = _fmm(
        [y_attn, y_hgrn], [(0, wba, False), (1, wbh, False)], gate_tiles, merge,
        [(BF, D, D, first)] * 5, m=t, n=D, tm=512, tn=D, name="branch_merge")
    def resid_norm(prods, ex):
        (p,), (xv, gv) = prods, ex
        hv = xv + p
        return hv, hv * lax.rsqrt(jnp.mean(hv * hv, axis=-1, keepdims=True) + EPS) * gv

    h1, u2 = _fmm([merged], [(0, wout, False)], [(x, D, first)], resid_norm, [(F32, D, D, first), (BF, D, D, first)],
                  m=t, n=D, tm=1024, tn=D, name="out_proj", vecs=[norm_ffn_g])

    def swiglu(prods, ex):
        g_, u_ = prods
        s = _sig(g_)
        silu = g_ * s
        return u_ * s * (1.0 + g_ * (1.0 - s)), silu, silu * u_

    dz_dgate, dz_dup, z = _fmm([u2], [(0, wg_t, True), (0, wu_t, True)], [], swiglu,
                               [(BF, FFN, FFN // 2, col)] * 3, m=t, n=FFN, tm=1024, tn=FFN // 2,
                               name="ffn_gate_up")
    def loss_head(prods, ex):
        (p,), (hv, tv, gv) = prods, ex
        hv = hv + p
        r = lax.rsqrt(jnp.mean(hv * hv, axis=-1, keepdims=True) + EPS)
        xh = hv * r
        err = xh * gv - tv
        lp = jnp.sum(jnp.sum(err * err, axis=1, keepdims=True), axis=0, keepdims=True) * (0.5 / D)
        dy = err * (1.0 / D)
        dxh = dy * gv
        dh = r * (dxh - xh * jnp.mean(dxh * xh, axis=-1, keepdims=True))
        return dh, dh, jnp.sum(dy * xh, axis=0, keepdims=True), jnp.broadcast_to(lp, (1, 128))

    dh2, dh2_b, d_norm_final, loss_row = _fmm(
        [z], [(0, wd, False)], [(h1, D, first), (tgt, D, first)], loss_head, [(F32, D, D, first), (BF, D, D, first)],
        m=t, n=D, tm=512, tn=D, name="ffn_down_loss", vecs=[norm_final_g], sums=[D, 128])

    def swiglu_bwd(prods, ex):
        (dz,), (da_, db_) = prods, ex
        return dz * da_.astype(F32), dz * db_.astype(F32)

    ffn_tiles = [(dz_dgate, FFN // 2, col), (dz_dup, FFN // 2, col)]
    dgt, dup = _fmm([dh2_b], [(0, wd, True)], ffn_tiles, swiglu_bwd, [(BF, FFN, FFN // 2, col)] * 2,
                    m=t, n=FFN, tm=1024, tn=FFN // 2, name="d_gate_up")
    (d_wd,) = _wgrad([z], dh2_b, name="d_w_down")
    (du2,) = _fmm([dgt, dup], [(0, wg_t, False), (1, wu_t, False)], [], lambda prods, ex: (prods[0] + prods[1],),
                  [(F32, D, 512, col)], m=t, n=D, tm=1024, tn=512, name="d_u2")
    d_wg, d_wu = _wgrad([dgt, dup], u2, name="d_w_gate_up")
    dh1, dh1_b, d_norm_ffn = _rms_bwd(du2, h1, norm_ffn_g, dh2, tm=512, name="rms_ffn_bwd")
    (d_wout,) = _wgrad([merged], dh1_b, name="d_w_out")

    def merge_bwd(prods, ex):
        (dm,), (sa, sb, ca, cb, wa, wb) = prods, ex
        dgate = jnp.concatenate([dm * ca.astype(F32), dm * cb.astype(F32)], axis=1)
        dya_ = (dm * sa.astype(F32)).astype(BF)
        dyb_ = (dm * sb.astype(F32)).astype(BF)
        return (dya_, dyb_, dgate, lax.dot_general(dya_, wa, _NT, preferred_element_type=F32),
                lax.dot_general(dyb_, wb, _NT, preferred_element_type=F32))

    ffn_grads = (d_wg, d_wu, d_wd)
    (dya, dyb, dgates, dy_attn, dy_hgrn), got = _fmm(
        [dh1_b], [(0, wout, True)], [(a, D, first) for a in (sig_a, sig_b, dgate_a, dgate_b)], merge_bwd,
        [(BF, D, D, first), (BF, D, D, first), (BF, 2 * D, 2 * D, first), (BF, D, D, first), (F32, D, D, first)],
        m=t, n=D, tm=512, tn=D, name="d_merge", consts=[wba, wbh], comm=_pair_comm(ffn_grads))
    pair_ffn = [_pair_add(g, r, core, name="pair_add_ffn%d" % i) for i, (g, r) in enumerate(zip(ffn_grads, got))]
    (d_wba,) = _wgrad([y_attn], dya, name="d_w_ba")
    (d_wbh,) = _wgrad([y_hgrn], dyb, name="d_w_bh")
    sq_grads = (d_wba, d_wbh, d_wout)
    (dh4, d_logits, d_hgrn_norm), (parts_ffn, *got) = _hgrn_bwd(
        h3, hf, logits, hgrn_norm_g, o_pre, states, dy_hgrn, t=t,
        comm=_both(_chip_comm(pair_ffn), _pair_comm(sq_grads)))
    pair_sq = [_pair_add(g, r, core, name="pair_add_sq%d" % i) for i, (g, r) in enumerate(zip(sq_grads, got))]
    (dq, dkv, d_sinks), (parts_sq,) = _attn_bwd(q, kv, sinks, dy_attn, t=t, comm=_chip_comm(pair_sq))
    dps = (dq, dkv, dh4, dgates)
    d_win_t, d_b_in = _inproj_bwd_w(dps, u1, t=t)
    half0, got_in = _inproj_bwd_x(dps, win_t, x, norm_mix_g, dh1, t=t, part=0, comm=_pair_comm([d_win_t]))
    pair_in = _pair_add(d_win_t, got_in[0], core, name="pair_add_w_in")
    (grad_x, d_norm_mix), (parts_in,) = _inproj_bwd_x(dps, win_t, x, norm_mix_g, dh1, t=t, part=1, prev=half0,
                                                      comm=_chip_comm([pair_in]))

    small_grads = (d_norm_mix, d_b_in, d_sinks, d_logits, d_hgrn_norm, d_norm_ffn, d_norm_final)
    return loss_row, grad_x, (parts_in, parts_ffn, parts_sq), small_grads


def kernel(x, norm_mix_g, w_in, b_in, attn_sinks, hgrn_lb_logits, hgrn_norm_g, w_branch_attn, w_branch_hgrn, w_out, norm_ffn_g, w_ffn_gate, w_ffn_up, w_ffn_down, norm_final_g, loss_target, m_norm_mix_g, m_w_in, m_b_in, m_attn_sinks, m_hgrn_lb_logits, m_hgrn_norm_g, m_w_branch_attn, m_w_branch_hgrn, m_w_out, m_norm_ffn_g, m_w_ffn_gate, m_w_ffn_up, m_w_ffn_down, m_norm_final_g, v_norm_mix_g, v_w_in, v_b_in, v_attn_sinks, v_hgrn_lb_logits, v_hgrn_norm_g, v_w_branch_attn, v_w_branch_hgrn, v_w_out, v_norm_ffn_g, v_w_ffn_gate, v_w_ffn_up, v_w_ffn_down, v_norm_final_g):
    shards = [w_in[0].T.astype(BF), w_ffn_gate[0].T.astype(BF), w_ffn_up[0].T.astype(BF),
              w_ffn_down[0].astype(BF), w_branch_attn[0].astype(BF), w_branch_hgrn[0].astype(BF),
              w_out[0].astype(BF)]
    loss_row, grad_x, grad_parts, small_grads = _step(
        x[0], loss_target[0], shards, norm_mix_g, b_in, attn_sinks, hgrn_lb_logits, hgrn_norm_g,
        norm_ffn_g, norm_final_g.reshape(1, D))

    d_norm_mix, d_b_in, d_sinks, d_logits, d_hgrn_norm, d_norm_ffn, d_norm_final = small_grads
    row = lambda a: a.reshape(1, D)
    loss_out, small = _small_allreduce_adam(
        dict(norm_mix_g=d_norm_mix, hgrn_norm_g=d_hgrn_norm, norm_ffn_g=d_norm_ffn, norm_final_g=d_norm_final,
             hgrn_lb_logits=d_logits, attn_sinks=d_sinks, b_in=d_b_in),
        loss_row,
        dict(norm_mix_g=(norm_mix_g, m_norm_mix_g, v_norm_mix_g), hgrn_norm_g=(hgrn_norm_g, m_hgrn_norm_g, v_hgrn_norm_g),
             norm_ffn_g=(norm_ffn_g, m_norm_ffn_g, v_norm_ffn_g),
             norm_final_g=(row(norm_final_g), row(m_norm_final_g), row(v_norm_final_g)),
             hgrn_lb_logits=(hgrn_lb_logits, m_hgrn_lb_logits, v_hgrn_lb_logits),
             attn_sinks=(attn_sinks, m_attn_sinks, v_attn_sinks), b_in=(b_in, m_b_in, v_b_in)))
    small["norm_final_g"] = [a.reshape(D) for a in small["norm_final_g"]]
    loss = loss_out[0, 0]

    names = ["w_in", "w_ffn_gate", "w_ffn_up", "w_ffn_down", "w_branch_attn", "w_branch_hgrn", "w_out"]
    w_full = dict(w_in=(w_in, m_w_in, v_w_in), w_ffn_gate=(w_ffn_gate, m_w_ffn_gate, v_w_ffn_gate),
                  w_ffn_up=(w_ffn_up, m_w_ffn_up, v_w_ffn_up), w_ffn_down=(w_ffn_down, m_w_ffn_down, v_w_ffn_down),
                  w_branch_attn=(w_branch_attn, m_w_branch_attn, v_w_branch_attn),
                  w_branch_hgrn=(w_branch_hgrn, m_w_branch_hgrn, v_w_branch_hgrn),
                  w_out=(w_out, m_w_out, v_w_out))
    parts_in, parts_ffn, parts_sq = grad_parts
    where = [(parts_in, 0), (parts_ffn, 0), (parts_ffn, 1), (parts_ffn, 2), (parts_sq, 0), (parts_sq, 1), (parts_sq, 2)]
    big = {}
    for i, name in enumerate(names):
        view = (lambda a: a[0].T) if i < 3 else (lambda a: a[0])
        back = (lambda a: a.T[None]) if i < 3 else (lambda a: a[None])
        wv, mv, vv = w_full[name]
        res = _adam(view(wv), where[i][0], where[i][1], view(mv), view(vv), name="adam_" + name)
        big[name] = [back(a) for a in res]

    order = ["norm_mix_g", "w_in", "b_in", "attn_sinks", "hgrn_lb_logits", "hgrn_norm_g", "w_branch_attn",
             "w_branch_hgrn", "w_out", "norm_ffn_g", "w_ffn_gate", "w_ffn_up", "w_ffn_down", "norm_final_g"]
    outs = [loss, grad_x[None]]
    for kind in range(4):
        for name in order:
            outs.append(big[name][kind] if name in big else small[name][kind])
    return tuple(outs)
```

```python
import math

import jax
import jax.numpy as jnp
from jax import lax
from jax.experimental import pallas as pl
from jax.experimental.pallas import tpu as pltpu

F32 = jnp.float32
BF = jnp.bfloat16
MESH = pl.DeviceIdType.MESH

D = 1024
HEAD = 64
N_PAIR = 8
BLK = 128
CH = 64
HG_SUB = 4
HG_HEADS = 8
HG_K = 128
FFN = 2816
IN_W = 7424
N_DEV = 8
N_CHIP = 4
EPS = 1e-6
NEG = -1e30
SCALE = 1.0 / math.sqrt(HEAD)
VMEM_LIMIT = 56 * 1024 * 1024
WT = 256

ADAM_LR, ADAM_B1, ADAM_B2, ADAM_EPS, ADAM_WD, ADAM_STEP = 0.001, 0.9, 0.999, 1e-08, 0.01, 10

SLAB_R = (IN_W // N_DEV, FFN // N_DEV, FFN // N_DEV, FFN // N_DEV, D // N_DEV, D // N_DEV, D // N_DEV)
SLAB_ROWS = sum(SLAB_R)
SLAB_OFF = tuple(sum(SLAB_R[:i]) for i in range(len(SLAB_R)))
N_W = len(SLAB_R)
GRP_OFF = (0, D // WT, (D + 256) // WT, (5 * D + 256) // WT)
GRP_N = (D // WT, 256 // WT, 4 * D // WT, 2 * D // WT)
SMALL_ROWS = 16


_NN = (((1,), (0,)), ((), ()))
_NT = (((1,), (1,)), ((), ()))
_TN = (((0,), (0,)), ((), ()))


def _pcall(body, **kw):
    return pl.pallas_call(body, **kw)


def _cp(sem=None, **kw):
    return pltpu.CompilerParams(dimension_semantics=sem, vmem_limit_bytes=VMEM_LIMIT, **kw)


def _sig(v):
    return 0.5 * jnp.tanh(0.5 * v) + 0.5


def _accum(ref, val, first):
    @pl.when(first)
    def _():
        ref[...] = val

    @pl.when(jnp.logical_not(first))
    def _():
        ref[...] += val


class _Comm:
    def __init__(self, ins, out_shapes, sem_shapes, phases):
        self.ins, self.out_shapes, self.sem_shapes, self.phases = list(ins), list(out_shapes), list(sem_shapes), phases


def _both(a, b):
    ni, no, ns = len(a.ins), len(a.out_shapes), len(a.sem_shapes)

    def of_a(fn):
        return lambda ins, outs, sems: fn(ins[:ni], outs[:no], sems[:ns])

    def of_b(fn):
        return lambda ins, outs, sems: fn(ins[ni:], outs[no:], sems[ns:])

    return _Comm(a.ins + b.ins, a.out_shapes + b.out_shapes, a.sem_shapes + b.sem_shapes,
                 [(f, of_a(fn)) for f, fn in a.phases] + [(f, of_b(fn)) for f, fn in b.phases])


def _host(body, comm, n_in, n_out, n_scr, nsteps, step_fn):
    if comm is None:
        return body
    ci, co = len(comm.ins), len(comm.out_shapes)

    def wrapped(*refs):
        p = 0
        ins, p = refs[p:p + n_in], p + n_in
        cins, p = refs[p:p + ci], p + ci
        outs, p = refs[p:p + n_out], p + n_out
        couts, p = refs[p:p + co], p + co
        scr, p = refs[p:p + n_scr], p + n_scr
        csems = refs[p:]
        step = step_fn()
        for frac, fn in comm.phases:
            if frac < 1.0:
                @pl.when(step == int(round(frac * (nsteps - 1))))
                def _(fn=fn):
                    fn(cins, couts, csems)
        body(*ins, *outs, *scr)
        for frac, fn in comm.phases:
            if frac >= 1.0:
                @pl.when(step == nsteps - 1)
                def _(fn=fn):
                    fn(cins, couts, csems)

    return wrapped


def _hosted_call(body, comm, args, *, name, grid, in_specs, out_specs, out_shape, scratch_shapes, sem,
                 nsteps, step_fn, aliases=None):
    n_in, n_out, n_scr = len(in_specs), len(out_specs), len(scratch_shapes)
    args = list(args)
    extra = {}
    if comm is not None:
        in_specs = list(in_specs) + [_hbm_spec()] * len(comm.ins)
        out_specs = list(out_specs) + [_hbm_spec()] * len(comm.out_shapes)
        out_shape = list(out_shape) + comm.out_shapes
        scratch_shapes = list(scratch_shapes) + comm.sem_shapes
        args += comm.ins
        extra = dict(has_side_effects=True)
    outs = _pcall(_host(body, comm, n_in, n_out, n_scr, nsteps, step_fn), name=name, grid=grid,
                  in_specs=in_specs, out_specs=out_specs, out_shape=out_shape, scratch_shapes=scratch_shapes,
                  input_output_aliases=aliases or {}, compiler_params=_cp(sem, **extra))(*args)
    return list(outs[:n_out]), list(outs[n_out:])


def _hbm_spec():
    return pl.BlockSpec(memory_space=pl.ANY)


def _wgrad(a_list, b, *, name):
    (t, m), n, gm = a_list[0].shape, b.shape[1], a_list[0].shape[1] // WT
    n_a = len(a_list)
    tile = lambda k: (lambda s: jnp.clip(s - k * gm, 0, gm - 1))

    def body(*refs):
        a_refs, b_ref, o_refs = refs[:n_a], refs[n_a], refs[n_a + 1:]
        s = pl.program_id(0)
        for k in range(n_a):
            @pl.when(jnp.logical_and(s >= k * gm, s < (k + 1) * gm))
            def _(k=k):
                o_refs[k][...] = lax.dot_general(a_refs[k][...], b_ref[...], _TN,
                                                 preferred_element_type=F32).astype(BF)

    return _pcall(body, name=name, grid=(n_a * gm,),
                  in_specs=[pl.BlockSpec((t, WT), lambda s, k=k: (0, tile(k)(s))) for k in range(n_a)]
                  + [pl.BlockSpec((t, n), lambda s: (0, 0))],
                  out_specs=[pl.BlockSpec((WT, n), lambda s, k=k: (tile(k)(s), 0)) for k in range(n_a)],
                  out_shape=[jax.ShapeDtypeStruct((m, n), BF)] * n_a,
                  compiler_params=_cp(("arbitrary",)))(*a_list, b)


def _fmm(lhs, rhs, extras, epilogue, outs, *, m, n, tm, tn, name, comm=None, vecs=(), consts=(), sums=()):
    tm, tn = min(tm, m), min(tn, n)
    assert m % tm == 0 and n % tn == 0 and (not sums or tn == n), (name, m, n, tm, tn)
    in_specs, args = [], []
    for a in lhs:
        in_specs.append(pl.BlockSpec((tm, a.shape[1]), lambda i, j: (i, 0)))
        args.append(a)
    for li, b, tb in rhs:
        k = lhs[li].shape[1]
        in_specs.append(pl.BlockSpec((tn, k), lambda i, j: (j, 0)) if tb
                        else pl.BlockSpec((k, tn), lambda i, j: (0, j)))
        args.append(b)
    for arr, w, col in extras:
        in_specs.append(pl.BlockSpec((tm, w), lambda i, j, col=col: (i, col(j))))
        args.append(arr)
    for vec in vecs:
        in_specs.append(pl.BlockSpec((1, tn), lambda i, j: (0, j)))
        args.append(vec)
    for whole in consts:
        in_specs.append(pl.BlockSpec(whole.shape, lambda i, j: (0, 0)))
        args.append(whole)
    out_specs = [pl.BlockSpec((tm, w), lambda i, j, col=col: (i, col(j))) for _, _, w, col in outs]
    out_shape = [jax.ShapeDtypeStruct((m, total), dt) for dt, total, _, _ in outs]
    for w in sums:
        out_specs.append(pl.BlockSpec((1, w), lambda i, j: (0, 0)))
        out_shape.append(jax.ShapeDtypeStruct((1, w), F32))
    nl, nr, ne, no = len(lhs), len(rhs), len(extras) + len(vecs) + len(consts), len(outs)

    def body(*refs):
        prods = []
        for r, (li, _, tb) in enumerate(rhs):
            prods.append(lax.dot_general(refs[li][...], refs[nl + r][...], _NT if tb else _NN,
                                         preferred_element_type=F32))
        vals = epilogue(prods, [ref[...] for ref in refs[nl + nr:nl + nr + ne]])
        o_refs = refs[nl + nr + ne:]
        for o_ref, v in zip(o_refs[:no], vals[:no]):
            o_ref[...] = v.astype(o_ref.dtype)
        for s_ref, v in zip(o_refs[no:], vals[no:]):
            _accum(s_ref, v, pl.program_id(0) == 0)

    gm, gn = m // tm, n // tn
    res, comm_res = _hosted_call(
        body, comm, args, name=name, grid=(gm, gn), in_specs=in_specs, out_specs=out_specs,
        out_shape=out_shape, scratch_shapes=[], sem=("arbitrary", "arbitrary"), nsteps=gm * gn,
        step_fn=lambda: pl.program_id(0) * gn + pl.program_id(1))
    return res if comm is None else (res, comm_res)


def _grp_of(i):
    return [jnp.logical_and(i >= GRP_OFF[g], i < GRP_OFF[g] + GRP_N[g]) for g in range(4)]


def _grp_idx(i, g):
    return jnp.clip(i - GRP_OFF[g], 0, GRP_N[g] - 1)


def _inproj_fwd(u, win_t, b_in, *, t, comm=None):
    tm = min(1024, t)
    n_row = t // tm
    n_chunks, h_first, g_first = 8, 2, 6
    sub = D // WT

    def w_block(l):
        return jnp.where(l == 0, GRP_OFF[0], jnp.where(l == 1, GRP_OFF[1], GRP_OFF[2] + sub * (l - h_first)))

    def body(u_ref, *rest):
        w_refs, b_refs, (q_ref, kv_ref, h3_ref, hf_ref, g_ref) = rest[:sub], rest[sub:2 * sub], rest[2 * sub:]
        l = pl.program_id(1)

        @pl.when(l == 1)
        def _():
            kv_ref[...] = (lax.dot_general(u_ref[...], w_refs[0][...], _NT, preferred_element_type=F32)
                           + b_refs[0][...]).astype(BF)

        is_hf = l == h_first + 1
        in_h3 = jnp.logical_and(jnp.logical_and(l >= h_first, l < g_first), jnp.logical_not(is_hf))
        for pred, o_ref in ((l == 0, q_ref), (in_h3, h3_ref), (is_hf, hf_ref), (l >= g_first, g_ref)):
            @pl.when(pred)
            def _(o_ref=o_ref):
                w = jnp.concatenate([w[...] for w in w_refs], axis=0)
                b = jnp.concatenate([b[...] for b in b_refs], axis=1)
                o_ref[...] = (lax.dot_general(u_ref[...], w, _NT, preferred_element_type=F32) + b).astype(o_ref.dtype)

    return _hosted_call(
        body, comm, [u] + [win_t] * sub + [b_in] * sub, name="inproj_fwd", grid=(n_row, n_chunks),
        in_specs=[pl.BlockSpec((tm, D), lambda i, l: (i, 0))]
        + [pl.BlockSpec((WT, D), lambda i, l, o=o: (w_block(l) + o, 0)) for o in range(sub)]
        + [pl.BlockSpec((1, WT), lambda i, l, o=o: (0, w_block(l) + o)) for o in range(sub)],
        out_specs=[pl.BlockSpec((tm, D), lambda i, l: (i, 0)),
                   pl.BlockSpec((tm, 256), lambda i, l: (i, 0)),
                   pl.BlockSpec((tm, D), lambda i, l: (i, jnp.clip(l - h_first - 1, 0, 2))),
                   pl.BlockSpec((tm, D), lambda i, l: (i, 0)),
                   pl.BlockSpec((tm, D), lambda i, l: (i, jnp.clip(l - g_first, 0, 1)))],
        out_shape=[jax.ShapeDtypeStruct((t, D), BF), jax.ShapeDtypeStruct((t, 256), BF),
                   jax.ShapeDtypeStruct((t, 3 * D), BF), jax.ShapeDtypeStruct((t, D), F32),
                   jax.ShapeDtypeStruct((t, 2 * D), BF)],
        scratch_shapes=[], sem=("arbitrary", "arbitrary"), nsteps=n_row * n_chunks,
        step_fn=lambda: pl.program_id(0) * n_chunks + pl.program_id(1))


def _inproj_bwd_x(dps, win_t, x, g, resid, *, t, part, prev=None, comm=None):
    n_row = 8 if t >= 4096 else 4
    tm = t // n_row
    first = n_row // 4
    per = first if part == 0 else n_row - first
    row = lambda i: part * first + i

    n_chunks = 4
    sub = 2 * D // WT

    def w_block(l):
        return jnp.where(l == 0, 0, GRP_OFF[2] + sub * (l - 1))

    def body(d0, d1, d2, d3, *rest):
        w_refs, (x_ref, g_ref, r_ref) = rest[:sub], rest[sub:sub + 3]
        dg_prev = rest[sub + 3] if prev is not None else None
        o_ref, dg_ref, acc_ref = rest[-3], rest[-2], rest[-1]
        i, l = pl.program_id(0), pl.program_id(1)

        @pl.when(l == 0)
        def _():
            wq = jnp.concatenate([w[...] for w in w_refs[:GRP_N[0]]], axis=0)
            acc_ref[...] = (jnp.dot(d0[...], wq, preferred_element_type=F32)
                            + jnp.dot(d1[...], w_refs[GRP_N[0]][...], preferred_element_type=F32))

        for pred, d_ref in ((jnp.logical_and(l >= 1, l < 3), d2), (l == 3, d3)):
            @pl.when(pred)
            def _(d_ref=d_ref):
                w = jnp.concatenate([w[...] for w in w_refs], axis=0)
                acc_ref[...] += jnp.dot(d_ref[...], w, preferred_element_type=F32)

        @pl.when(l == n_chunks - 1)
        def _():
            xv = x_ref[...]
            r = lax.rsqrt(jnp.mean(xv * xv, axis=-1, keepdims=True) + EPS)
            xh = xv * r
            du = acc_ref[...]
            dxh = du * g_ref[...]
            o_ref[...] = r_ref[...] + r * (dxh - xh * jnp.mean(dxh * xh, axis=-1, keepdims=True))
            dg = jnp.sum(du * xh, axis=0, keepdims=True)
            if dg_prev is not None:
                dg = dg + jnp.where(i == 0, 1.0, 0.0) * dg_prev[...]
            _accum(dg_ref, dg, i == 0)

    rows = lambda w: pl.BlockSpec((tm, w), lambda i, l: (row(i), 0))
    in_specs = ([rows(D), rows(256),
                 pl.BlockSpec((tm, 2 * D), lambda i, l: (row(i), jnp.clip(l - 1, 0, 1))), rows(2 * D)]
                + [pl.BlockSpec((WT, D), lambda i, l, o=o: (w_block(l) + o, 0)) for o in range(sub)]
                + [rows(D), pl.BlockSpec((1, D), lambda i, l: (0, 0)), rows(D)])
    args = list(dps) + [win_t] * sub + [x, g, resid]
    aliases = None
    if prev is not None:
        in_specs += [pl.BlockSpec((1, D), lambda i, l: (0, 0)), _hbm_spec()]
        args += [prev[1], prev[0]]
        aliases = {len(args) - 1: 0}
    return _hosted_call(
        body, comm, args, name="inproj_bwd_x%d" % part, grid=(per, n_chunks), in_specs=in_specs,
        out_specs=[rows(D), pl.BlockSpec((1, D), lambda i, l: (0, 0))],
        out_shape=[jax.ShapeDtypeStruct((t, D), F32), jax.ShapeDtypeStruct((1, D), F32)],
        scratch_shapes=[pltpu.VMEM((tm, D), F32)], sem=("arbitrary", "arbitrary"), nsteps=per * n_chunks,
        step_fn=lambda: pl.program_id(0) * n_chunks + pl.program_id(1), aliases=aliases)


def _inproj_bwd_w(dps, u, *, t):
    n_tiles = IN_W // WT
    dims = (((0,), (0,)), ((), ()))

    def body(d0, d1, d2, d3, u_ref, o_ref, db_ref):
        i = pl.program_id(0)
        uv = u_ref[...]
        for g, (pred, d_ref) in enumerate(zip(_grp_of(i), (d0, d1, d2, d3))):
            @pl.when(pred)
            def _(d_ref=d_ref):
                dv = d_ref[...]
                o_ref[...] = lax.dot_general(dv, uv, dims, preferred_element_type=F32).astype(BF)
                db_ref[...] = jnp.sum(dv.astype(F32), axis=0, keepdims=True)

    return _pcall(body, name="inproj_bwd_w", grid=(n_tiles,),
                  in_specs=[pl.BlockSpec((t, WT), lambda i, g=g: (0, _grp_idx(i, g))) for g in range(4)]
                  + [pl.BlockSpec((t, D), lambda i: (0, 0))],
                  out_specs=[pl.BlockSpec((WT, D), lambda i: (i, 0)),
                             pl.BlockSpec((1, WT), lambda i: (0, i))],
                  out_shape=[jax.ShapeDtypeStruct((IN_W, D), BF), jax.ShapeDtypeStruct((1, IN_W), F32)],
                  compiler_params=_cp(("arbitrary",)))(*dps, u)


def _row_spec(tm, width, col=0):
    return pl.BlockSpec((tm, width), lambda i: (i, col))


def _vec_spec(width):
    return pl.BlockSpec((1, width), lambda i: (0, 0))


def _rms_fwd(x, g, *, tm, name, comm=None):
    t = x.shape[0]
    tm = min(tm, t)

    def body(x_ref, g_ref, u_ref):
        xv = x_ref[...]
        r = lax.rsqrt(jnp.mean(xv * xv, axis=-1, keepdims=True) + EPS)
        u_ref[...] = (xv * r * g_ref[...]).astype(BF)

    (u,), comm_res = _hosted_call(
        body, comm, (x, g), name=name, grid=(t // tm,), in_specs=[_row_spec(tm, D), _vec_spec(D)],
        out_specs=[_row_spec(tm, D)], out_shape=[jax.ShapeDtypeStruct((t, D), BF)], scratch_shapes=[],
        sem=("arbitrary",), nsteps=t // tm, step_fn=lambda: pl.program_id(0))
    return u if comm is None else (u, comm_res)


def _rms_bwd(du, x, g, resid, *, tm, name):
    t = x.shape[0]
    tm = min(tm, t)

    def body(du_ref, x_ref, g_ref, r_ref, dx_ref, dxb_ref, dg_ref):
        xv = x_ref[...]
        r = lax.rsqrt(jnp.mean(xv * xv, axis=-1, keepdims=True) + EPS)
        xh = xv * r
        duv = du_ref[...]
        dxh = duv * g_ref[...]
        dx = r_ref[...] + r * (dxh - xh * jnp.mean(dxh * xh, axis=-1, keepdims=True))
        dx_ref[...] = dx
        dxb_ref[...] = dx.astype(BF)
        _accum(dg_ref, jnp.sum(duv * xh, axis=0, keepdims=True), pl.program_id(0) == 0)

    return _pcall(body, name=name, grid=(t // tm,),
                  in_specs=[_row_spec(tm, D), _row_spec(tm, D), _vec_spec(D), _row_spec(tm, D)],
                  out_specs=[_row_spec(tm, D), _row_spec(tm, D), _vec_spec(D)],
                  out_shape=[jax.ShapeDtypeStruct((t, D), F32), jax.ShapeDtypeStruct((t, D), BF),
                             jax.ShapeDtypeStruct((1, D), F32)],
                  compiler_params=_cp(("arbitrary",)))(du, x, g, resid)


def _attn_kv_tiles(kprev, kcur):
    kv = jnp.concatenate([kprev, kcur], axis=0).astype(F32)
    lo = lax.broadcasted_iota(jnp.int32, (2 * BLK, 128), 1) < HEAD
    tiles = []
    for part in (kv[:, 0:128], kv[:, 128:256]):
        rolled = pltpu.roll(part, HEAD, 1)
        z = jnp.zeros_like(part)
        tiles.append(((jnp.where(lo, part, z).astype(BF), jnp.where(lo, z, rolled).astype(BF)),
                      (jnp.where(lo, rolled, z).astype(BF), jnp.where(lo, z, part).astype(BF))))
    k_t, v_t = tiles
    return [(jnp.concatenate(k_t[h], axis=0), jnp.concatenate(v_t[h], axis=0)) for h in range(2)]


def _attn_mask(i):
    qi = lax.broadcasted_iota(jnp.int32, (BLK, 2 * BLK), 0)
    kj = lax.broadcasted_iota(jnp.int32, (BLK, 2 * BLK), 1)
    first_key = jnp.where(i == 0, BLK, 0)
    in_prev = jnp.logical_and(jnp.logical_and(kj < BLK, kj > qi), kj >= first_key)
    in_cur = jnp.logical_and(kj >= BLK, kj - BLK <= qi)
    return jnp.logical_or(in_prev, in_cur)


def _attn_probs(s, sink, valid):
    s = jnp.where(valid, s * SCALE, NEG)
    mx = jnp.maximum(jnp.max(s, axis=-1, keepdims=True), sink)
    e = jnp.exp(s - mx)
    es = jnp.exp(sink - mx)
    inv = 1.0 / (jnp.sum(e, axis=-1, keepdims=True) + es)
    return e * inv, es * inv


_KEYS = 2 * BLK


def _pair(ref, j):
    return ref[:, j * 128:(j + 1) * 128]


def _attn_fwd(q, kv, sinks, *, t, comm=None):
    nb = t // BLK

    def body(sink_ref, q_ref, kp_ref, kc_ref, o_ref):
        i = pl.program_id(0)
        for c in range(2):
            rows = slice(c * BLK, (c + 1) * BLK)
            valid = _attn_mask(2 * i + c)
            tiles = _attn_kv_tiles(kp_ref[...] if c == 0 else kc_ref[0:BLK, :], kc_ref[rows, :])
            s = [lax.dot_general(q_ref[rows, j * 128:(j + 1) * 128], tiles[j // 4][0], _NT,
                                 preferred_element_type=F32) for j in range(N_PAIR)]
            p = []
            for j in range(N_PAIR):
                pe, _ = _attn_probs(s[j][:, 0:_KEYS], sink_ref[0, 2 * j], valid)
                po, _ = _attn_probs(s[j][:, _KEYS:2 * _KEYS], sink_ref[0, 2 * j + 1], valid)
                p.append(jnp.concatenate([pe.astype(BF), po.astype(BF)], axis=1))
            for j in range(N_PAIR):
                o_ref[rows, j * 128:(j + 1) * 128] = jnp.dot(p[j], tiles[j // 4][1],
                                                             preferred_element_type=F32).astype(BF)

    return _hosted_call(
        body, comm, (sinks, q, kv, kv), name="attn_fwd", grid=(nb // 2,),
        in_specs=[pl.BlockSpec(memory_space=pltpu.SMEM),
                  pl.BlockSpec((2 * BLK, D), lambda i: (i, 0)),
                  pl.BlockSpec((BLK, 256), lambda i: (jnp.maximum(2 * i - 1, 0), 0)),
                  pl.BlockSpec((2 * BLK, 256), lambda i: (i, 0))],
        out_specs=[pl.BlockSpec((2 * BLK, D), lambda i: (i, 0))],
        out_shape=[jax.ShapeDtypeStruct((t, D), BF)],
        scratch_shapes=[], sem=("arbitrary",), nsteps=nb // 2, step_fn=lambda: pl.program_id(0))


def _attn_bwd(q, kv, sinks, do, *, t, comm=None):
    nb = t // BLK
    last = nb - 1

    def body(sink_ref, q_ref, kp_ref, kc_ref, do_ref, dq_ref, dkv_ref, ds_ref, carry_ref):
        i = pl.program_id(0)

        @pl.when(i == 0)
        def _():
            ds_ref[...] = jnp.zeros_like(ds_ref)
            carry_ref[...] = jnp.zeros_like(carry_ref)

        @pl.when(i < nb)
        def _():
            valid = _attn_mask(i)
            tiles = _attn_kv_tiles(kp_ref[...], kc_ref[...])
            lane1 = lax.broadcasted_iota(jnp.int32, (1, 128), 1)
            dsink = jnp.zeros((1, 128), F32)
            s = [lax.dot_general(_pair(q_ref, j), tiles[j // 4][0], _NT, preferred_element_type=F32)
                 for j in range(N_PAIR)]
            dp = [lax.dot_general(_pair(do_ref, j), tiles[j // 4][1], _NT, preferred_element_type=F32)
                  for j in range(N_PAIR)]
            p_all, ds_all = [], []
            for j in range(N_PAIR):
                halves = []
                for par in range(2):
                    cols = slice(par * _KEYS, (par + 1) * _KEYS)
                    p, ps = _attn_probs(s[j][:, cols], sink_ref[0, 2 * j + par], valid)
                    dpj = dp[j][:, cols]
                    dd = jnp.sum(p * dpj, axis=-1, keepdims=True)
                    dsink = dsink + jnp.where(lane1 == 2 * j + par,
                                              -jnp.sum(ps * dd, axis=0, keepdims=True), 0.0)
                    halves.append((p.astype(BF), (p * (dpj - dd)).astype(BF)))
                p_all.append(jnp.concatenate([halves[0][0], halves[1][0]], axis=1))
                ds_all.append(jnp.concatenate([halves[0][1], halves[1][1]], axis=1))
            for j in range(N_PAIR):
                dq_ref[:, j * 128:(j + 1) * 128] = (
                    jnp.dot(ds_all[j], tiles[j // 4][0], preferred_element_type=F32) * SCALE).astype(BF)
            ds_ref[...] += dsink
            gk, gv = [], []
            for h in range(2):
                grp = range(4 * h, 4 * h + 4)
                q_rows = jnp.concatenate([_pair(q_ref, j) for j in grp], axis=0)
                do_rows = jnp.concatenate([_pair(do_ref, j) for j in grp], axis=0)
                g_k = lax.dot_general(jnp.concatenate([ds_all[j] for j in grp], axis=0), q_rows, _TN,
                                      preferred_element_type=F32)
                g_v = lax.dot_general(jnp.concatenate([p_all[j] for j in grp], axis=0), do_rows, _TN,
                                      preferred_element_type=F32)
                gk.append((g_k[0:_KEYS], g_k[_KEYS:2 * _KEYS]))
                gv.append((g_v[0:_KEYS], g_v[_KEYS:2 * _KEYS]))
            lo = lax.broadcasted_iota(jnp.int32, (2 * BLK, 128), 1) < HEAD
            zero = jnp.zeros((2 * BLK, 128), F32)

            def unpad(g):
                return (jnp.where(lo, g[0][0] + pltpu.roll(g[0][1], HEAD, 1), zero)
                        + jnp.where(lo, zero, pltpu.roll(g[1][0], HEAD, 1) + g[1][1]))

            dk = unpad(gk) * SCALE
            dv = unpad(gv)
            dkv_ref[:, 0:128] = (carry_ref[:, 0:128] + dk[0:BLK]).astype(BF)
            dkv_ref[:, 128:256] = (carry_ref[:, 128:256] + dv[0:BLK]).astype(BF)
            carry_ref[:, 0:128] = dk[BLK:2 * BLK]
            carry_ref[:, 128:256] = dv[BLK:2 * BLK]

        @pl.when(i == nb)
        def _():
            dkv_ref[...] = carry_ref[...].astype(BF)

    return _hosted_call(
        body, comm, (sinks, q, kv, kv, do), name="attn_bwd", grid=(nb + 1,),
        in_specs=[pl.BlockSpec(memory_space=pltpu.SMEM),
                  pl.BlockSpec((BLK, D), lambda i: (jnp.minimum(i, last), 0)),
                  pl.BlockSpec((BLK, 256), lambda i: (jnp.clip(i - 1, 0, last), 0)),
                  pl.BlockSpec((BLK, 256), lambda i: (jnp.minimum(i, last), 0)),
                  pl.BlockSpec((BLK, D), lambda i: (jnp.minimum(i, last), 0))],
        out_specs=[pl.BlockSpec((BLK, D), lambda i: (jnp.minimum(i, last), 0)),
                   pl.BlockSpec((BLK, 256), lambda i: (jnp.maximum(i - 1, 0), 0)),
                   pl.BlockSpec((1, 128), lambda i: (0, 0))],
        out_shape=[jax.ShapeDtypeStruct((t, D), BF), jax.ShapeDtypeStruct((t, 256), BF),
                   jax.ShapeDtypeStruct((1, 128), F32)],
        scratch_shapes=[pltpu.VMEM((BLK, 256), F32)], sem=("arbitrary",), nsteps=nb + 1,
        step_fn=lambda: pl.program_id(0))


def _split3(v):
    h = v.astype(BF)
    r = v - h.astype(F32)
    m = r.astype(BF)
    lo = (r - m.astype(F32)).astype(BF)
    return jnp.concatenate([h, m, lo], axis=1)


def _apply01(mat, v):
    n = v.shape[1]
    r = jnp.dot(mat, _split3(v), preferred_element_type=F32)
    return r[:, 0:n] + r[:, n:2 * n] + r[:, 2 * n:3 * n]


def _hgrn_gates(hq, hf, lb):
    sq = _sig(hq)
    sg = _sig(hf)
    f = lb + (1.0 - lb) * sg
    return hq * sq, (1.0 - lb) * (1.0 - sg), jnp.log(f), sq, sg, f


def _tri(upper):
    r = lax.broadcasted_iota(jnp.int32, (CH, CH), 0)
    c = lax.broadcasted_iota(jnp.int32, (CH, CH), 1)
    return (c >= r) if upper else (c <= r)


def _lb_from_logits(lg_ref):
    return 1.0 / (1.0 + jnp.exp(lg_ref[1:2, :] - lg_ref[0:1, :]))


def _hgrn_fwd(h3, hf, logits, norm_g, *, t, comm=None):
    nc = t // CH
    nt_dims = (((1,), (1,)), ((), ()))
    tn_dims = (((0,), (0,)), ((), ()))

    def body(h_ref, hf_ref, lg_ref, ng_ref, y_ref, o_ref, st_ref, s_scr, b_scr, qa_s, ka_s, qb_s, kb_s, v_s):
        @pl.when(pl.program_id(0) == 0)
        def _():
            s_scr[...] = jnp.zeros_like(s_scr)

        heads = [slice(h * HG_K, (h + 1) * HG_K) for h in range(HG_HEADS)]
        causal = _tri(False)
        lb = _lb_from_logits(lg_ref)
        for c in range(HG_SUB):
            rows = slice(c * CH, (c + 1) * CH)
            q, k, g, _, _, _ = _hgrn_gates(h_ref[rows, 0:D].astype(F32), hf_ref[rows, :], lb)
            b_scr[...] = _apply01(jnp.where(causal, 1.0, 0.0).astype(BF), g)
            b = b_scr[...]
            b_mid = b_scr[CH // 2 - 1:CH // 2, :]
            b_last = b_scr[CH - 1:CH, :]
            qa_s[...] = (q * jnp.exp(b - b_mid)).astype(BF)
            ka_s[...] = (k * jnp.exp(b_mid - b)).astype(BF)
            qb_s[...] = (q * jnp.exp(b)).astype(BF)
            kb_s[...] = (k * jnp.exp(b_last - b)).astype(BF)
            v_s[...] = h_ref[rows, D:2 * D]
            dec = jnp.exp(b_last)
            st_ref[c] = s_scr[...].astype(BF)
            a = [jnp.where(causal, lax.dot_general(qa_s[:, sl], ka_s[:, sl], nt_dims, preferred_element_type=F32),
                           0.0).astype(BF) for sl in heads]
            for h, sl in enumerate(heads):
                o_ref[rows, sl] = (jnp.dot(a[h], v_s[:, sl], preferred_element_type=F32)
                                   + lax.dot_general(qb_s[:, sl], s_scr[h].astype(BF), nt_dims,
                                                     preferred_element_type=F32))
            for h, sl in enumerate(heads):
                s_scr[h] = dec[:, sl] * s_scr[h] + lax.dot_general(v_s[:, sl], kb_s[:, sl], tn_dims,
                                                                   preferred_element_type=F32)
            for h, sl in enumerate(heads):
                o = o_ref[rows, sl]
                on = o * lax.rsqrt(jnp.mean(o * o, axis=-1, keepdims=True) + EPS)
                gate = _sig(h_ref[rows, 2 * D + h * HG_K:2 * D + (h + 1) * HG_K].astype(F32))
                y_ref[rows, sl] = (on * ng_ref[:, sl] * gate).astype(BF)

    half = lambda: pltpu.VMEM((CH, D), BF)
    blk = HG_SUB * CH
    return _hosted_call(
        body, comm, (h3, hf, logits, norm_g), name="hgrn_fwd", grid=(nc // HG_SUB,),
        in_specs=[pl.BlockSpec((blk, 3 * D), lambda n: (n, 0)),
                  pl.BlockSpec((blk, D), lambda n: (n, 0)),
                  pl.BlockSpec((2, D), lambda n: (0, 0)),
                  pl.BlockSpec((1, D), lambda n: (0, 0))],
        out_specs=[pl.BlockSpec((blk, D), lambda n: (n, 0)),
                   pl.BlockSpec((blk, D), lambda n: (n, 0)),
                   pl.BlockSpec((HG_SUB, HG_HEADS, HG_K, HG_K), lambda n: (n, 0, 0, 0))],
        out_shape=[jax.ShapeDtypeStruct((t, D), BF), jax.ShapeDtypeStruct((t, D), F32),
                   jax.ShapeDtypeStruct((nc, HG_HEADS, HG_K, HG_K), BF)],
        scratch_shapes=[pltpu.VMEM((HG_HEADS, HG_K, HG_K), F32), pltpu.VMEM((CH, D), F32),
                        half(), half(), half(), half(), half()],
        sem=("arbitrary",), nsteps=nc // HG_SUB, step_fn=lambda: pl.program_id(0))


def _hgrn_bwd(h3, hf, logits, norm_g, o_pre, states, dy, *, t, comm=None):
    nc = t // CH
    nt_dims = (((1,), (1,)), ((), ()))
    tn_dims = (((0,), (0,)), ((), ()))

    def body(h_ref, hf_ref, lg_ref, ng_ref, o_ref, st_ref, dy_ref, dh_ref, dlg_ref, dng_ref, ds_scr, dlb_scr,
             b_scr, tail_s, e_qa, e_ka, e_qb, e_kb, q_s, k_s, dqa_s, dka_s, dqb_s, dkb_s,
             qa_s, ka_s, qb_s, kb_s, v_s, do_s):
        n = pl.program_id(0)

        @pl.when(n == 0)
        def _():
            ds_scr[...] = jnp.zeros_like(ds_scr)
            dlb_scr[...] = jnp.zeros_like(dlb_scr)
            dng_ref[...] = jnp.zeros_like(dng_ref)

        heads = [slice(h * HG_K, (h + 1) * HG_K) for h in range(HG_HEADS)]
        lb = _lb_from_logits(lg_ref)
        causal = _tri(False)

        def chunk(c):
            rows = slice(c * CH, (c + 1) * CH)
            hq = h_ref[rows, 0:D].astype(F32)
            q, k, g, sq, sg, f = _hgrn_gates(hq, hf_ref[rows, :], lb)
            b_scr[...] = _apply01(jnp.where(causal, 1.0, 0.0).astype(BF), g)
            b = b_scr[...]
            b_mid = b_scr[CH // 2 - 1:CH // 2, :]
            b_last = b_scr[CH - 1:CH, :]
            q_s[...] = q
            k_s[...] = k
            for e_ref, s_ref, base, expo in ((e_qa, qa_s, q, b - b_mid), (e_ka, ka_s, k, b_mid - b),
                                             (e_qb, qb_s, q, b), (e_kb, kb_s, k, b_last - b)):
                e = jnp.exp(expo)
                e_ref[...] = e
                s_ref[...] = (base * e).astype(BF)
            v_s[...] = h_ref[rows, D:2 * D]
            dec = jnp.exp(b_last)
            for h, sl in enumerate(heads):
                gcol = slice(3 * D + h * HG_K, 3 * D + (h + 1) * HG_K)
                ngh = ng_ref[:, sl]
                sgate = _sig(h_ref[rows, 2 * D + h * HG_K:2 * D + (h + 1) * HG_K].astype(F32))
                o = o_ref[rows, sl]
                r = lax.rsqrt(jnp.mean(o * o, axis=-1, keepdims=True) + EPS)
                on = o * r
                dyh = dy_ref[rows, sl]
                dh_ref[rows, gcol] = (dyh * on * ngh * sgate * (1.0 - sgate)).astype(BF)
                dng_ref[:, sl] += jnp.sum(dyh * on * sgate, axis=0, keepdims=True)
                don = dyh * ngh * sgate
                do_s[:, sl] = (r * (don - on * jnp.mean(don * on, axis=-1, keepdims=True))).astype(BF)
            a = [jnp.where(causal, lax.dot_general(qa_s[:, sl], ka_s[:, sl], nt_dims, preferred_element_type=F32),
                           0.0).astype(BF) for sl in heads]
            da = [jnp.where(causal, lax.dot_general(do_s[:, sl], v_s[:, sl], nt_dims, preferred_element_type=F32),
                            0.0).astype(BF) for sl in heads]
            for h, sl in enumerate(heads):
                dh_ref[rows, 2 * D + h * HG_K:2 * D + (h + 1) * HG_K] = (
                    lax.dot_general(a[h], do_s[:, sl], tn_dims, preferred_element_type=F32)
                    + lax.dot_general(kb_s[:, sl], ds_scr[h].astype(BF), nt_dims, preferred_element_type=F32)
                ).astype(BF)
            for h, sl in enumerate(heads):
                dqa_s[:, sl] = jnp.dot(da[h], ka_s[:, sl], preferred_element_type=F32)
            for h, sl in enumerate(heads):
                dka_s[:, sl] = lax.dot_general(da[h], qa_s[:, sl], tn_dims, preferred_element_type=F32)
            for h, sl in enumerate(heads):
                dqb_s[:, sl] = jnp.dot(do_s[:, sl], st_ref[c, h], preferred_element_type=F32)
            for h, sl in enumerate(heads):
                dkb_s[:, sl] = jnp.dot(v_s[:, sl], ds_scr[h].astype(BF), preferred_element_type=F32)
            for h, sl in enumerate(heads):
                tail_s[:, sl] = jnp.sum(dec[:, sl] * st_ref[c, h].astype(F32) * ds_scr[h], axis=0, keepdims=True)
            for h, sl in enumerate(heads):
                ds_scr[h] = (lax.dot_general(do_s[:, sl], qb_s[:, sl], tn_dims, preferred_element_type=F32)
                             + dec[:, sl] * ds_scr[h])
            qv, kv = q_s[...], k_s[...]
            dqa, dka, dqb, dkb = dqa_s[...], dka_s[...], dqb_s[...], dkb_s[...]
            eqa, eka, eqb, ekb = e_qa[...], e_ka[...], e_qb[...], e_kb[...]
            dkb_kb = dkb * (kv * ekb)
            db_last = jnp.sum(dkb_kb, axis=0, keepdims=True) + tail_s[...]
            last_row = lax.broadcasted_iota(jnp.int32, (CH, D), 0) == CH - 1
            db = (dqa * (qv * eqa) - dka * (kv * eka) + dqb * (qv * eqb) - dkb_kb
                  + jnp.where(last_row, db_last, 0.0))
            dg = _apply01(jnp.where(_tri(True), 1.0, 0.0).astype(BF), db)
            dq = dqa * eqa + dqb * eqb
            dk = dka * eka + dkb * ekb
            dh_ref[rows, 0:D] = (dq * sq * (1.0 + hq * (1.0 - sq))).astype(BF)
            dfk = dg / f - dk
            dh_ref[rows, D:2 * D] = ((1.0 - lb) * dfk * sg * (1.0 - sg)).astype(BF)
            dlb_scr[...] += jnp.sum((1.0 - sg) * dfk, axis=0, keepdims=True)

        for c in reversed(range(HG_SUB)):
            chunk(c)

        @pl.when(n == nc // HG_SUB - 1)
        def _():
            dl0 = dlb_scr[...] * lb * (1.0 - lb)
            dlg_ref[0:1, :] = dl0
            dlg_ref[1:2, :] = -dl0

    steps = nc // HG_SUB
    blk = HG_SUB * CH
    rev = lambda n: (steps - 1 - n, 0)
    return _hosted_call(
        body, comm, (h3, hf, logits, norm_g, o_pre, states, dy), name="hgrn_bwd", grid=(steps,),
        in_specs=[pl.BlockSpec((blk, 3 * D), rev),
                  pl.BlockSpec((blk, D), rev),
                  pl.BlockSpec((2, D), lambda n: (0, 0)),
                  pl.BlockSpec((1, D), lambda n: (0, 0)),
                  pl.BlockSpec((blk, D), rev),
                  pl.BlockSpec((HG_SUB, HG_HEADS, HG_K, HG_K), lambda n: (steps - 1 - n, 0, 0, 0)),
                  pl.BlockSpec((blk, D), rev)],
        out_specs=[pl.BlockSpec((blk, 4 * D), rev),
                   pl.BlockSpec((2, D), lambda n: (0, 0)),
                   pl.BlockSpec((1, D), lambda n: (0, 0))],
        out_shape=[jax.ShapeDtypeStruct((t, 4 * D), BF), jax.ShapeDtypeStruct((2, D), F32),
                   jax.ShapeDtypeStruct((1, D), F32)],
        scratch_shapes=([pltpu.VMEM((HG_HEADS, HG_K, HG_K), F32), pltpu.VMEM((1, D), F32),
                         pltpu.VMEM((CH, D), F32), pltpu.VMEM((1, D), F32)]
                        + [pltpu.VMEM((CH, D), F32)] * 10 + [pltpu.VMEM((CH, D), BF)] * 6),
        sem=("arbitrary",), nsteps=steps, step_fn=lambda: pl.program_id(0))


def _place():
    x, y, c = lax.axis_index("x"), lax.axis_index("y"), lax.axis_index("c")
    return x, y, c, [(1 - x, y), (x, 1 - y), (1 - x, 1 - y)]


def _gather_comm(shards, mids):
    n, pieces = len(shards), len(mids)
    r = [s.shape[0] for s in shards]
    tile = 16
    cut = [[(rw // tile * p // pieces) * tile for p in range(pieces + 1)] for rw in r]
    size = [[cut[w][p + 1] - cut[w][p] for p in range(pieces)] for w in range(n)]

    def tools(ins, outs, sems):
        send_sems, recv_sems, local_sems = sems
        x, y, c, _ = _place()
        me, sib = (x, y, c), (x, y, 1 - c)
        near = [(x ^ c, y ^ (1 - c), c), (x ^ (1 - c), y ^ c, c), (1 - x, 1 - y, c)]

        def rows(w, p, dev):
            return outs[w].at[pl.ds((4 * dev[0] + 2 * dev[1] + dev[2]) * r[w] + cut[w][p], size[w][p]), :]

        def copy(kind, w, p, block, to, own=False):
            src = ins[w].at[pl.ds(cut[w][p], size[w][p]), :] if own else rows(w, p, block)
            return pltpu.make_async_remote_copy(
                src_ref=src, dst_ref=rows(w, p, block), send_sem=send_sems.at[p, kind],
                recv_sem=recv_sems.at[p, kind], device_id=to, device_id_type=MESH)

        def all_of(kind, p):
            whole = outs[0].at[pl.ds(0, sum(size[w][p] for w in range(n))), :]
            return pltpu.make_async_remote_copy(
                src_ref=whole, dst_ref=whole, send_sem=send_sems.at[p, kind], recv_sem=recv_sems.at[p, kind],
                device_id=me, device_id_type=MESH)

        mine = [pltpu.make_async_copy(ins[w], outs[w].at[pl.ds((4 * x + 2 * y + c) * r[w], r[w]), :],
                                      local_sems.at[w]) for w in range(n)]
        return near, me, sib, copy, all_of, mine

    def start(ins, outs, sems):
        near, me, sib, copy, _, mine = tools(ins, outs, sems)
        for cp in mine:
            cp.start()
        for p in range(pieces):
            for w in range(n):
                copy(0, w, p, me, sib, own=True).start()
                copy(1, w, p, me, near[0], own=True).start()
                copy(2, w, p, me, near[1], own=True).start()

    def pass_diagonal(p, near, sib, copy, all_of):
        all_of(3, p).wait_recv()
        for w in range(n):
            copy(6, w, p, near[2], sib).start()

    def pass_on(p):
        def phase(ins, outs, sems):
            near, _, sib, copy, all_of, _ = tools(ins, outs, sems)
            all_of(1, p).wait_recv()
            for w in range(n):
                copy(3, w, p, near[0], near[1]).start()
                copy(4, w, p, near[0], sib).start()
            all_of(2, p).wait_recv()
            for w in range(n):
                copy(5, w, p, near[1], sib).start()
            if p > 0:
                pass_diagonal(p - 1, near, sib, copy, all_of)
        return phase

    def finish(ins, outs, sems):
        near, _, sib, copy, all_of, mine = tools(ins, outs, sems)
        pass_diagonal(pieces - 1, near, sib, copy, all_of)
        for p in range(pieces):
            all_of(0, p).wait_recv()
            for kind in (4, 5, 6):
                all_of(kind, p).wait_recv()
            for kind in range(7):
                all_of(kind, p).wait_send()
        for cp in mine:
            cp.wait()

    return _Comm(shards, [jax.ShapeDtypeStruct((N_DEV * rw, D), BF) for rw in r],
                 [pltpu.SemaphoreType.DMA((pieces, 7)), pltpu.SemaphoreType.DMA((pieces, 7)),
                  pltpu.SemaphoreType.DMA((n,))],
                 [(0.0, start)] + [(f, pass_on(p)) for p, f in enumerate(mids)] + [(1.0, finish)])


def _pair_comm(grads):
    n = len(grads)
    r = [g.shape[0] // N_DEV for g in grads]

    def start(ins, outs, sems):
        send_sems, recv_sems = sems
        x, y, c, _ = _place()
        for w in range(n):
            for a in range(N_CHIP):
                pltpu.make_async_remote_copy(
                    src_ref=ins[w].at[pl.ds((2 * a + 1 - c) * r[w], r[w]), :], dst_ref=outs[w].at[a],
                    send_sem=send_sems.at[w], recv_sem=recv_sems.at[w],
                    device_id=(x, y, 1 - c), device_id_type=MESH).start()

    def finish(ins, outs, sems):
        send_sems, recv_sems = sems
        x, y, c, _ = _place()
        for w in range(n):
            pltpu.make_async_remote_copy(
                src_ref=outs[w], dst_ref=outs[w], send_sem=send_sems.at[w], recv_sem=recv_sems.at[w],
                device_id=(x, y, c), device_id_type=MESH).wait()

    return _Comm(grads, [jax.ShapeDtypeStruct((N_CHIP, rw, D), BF) for rw in r],
                 [pltpu.SemaphoreType.DMA((n,)), pltpu.SemaphoreType.DMA((n,))],
                 [(0.0, start), (1.0, finish)])


def _pair_add(grads, gots, core, *, name):
    n, r = len(grads), gots[0].shape[1]
    chip = lambda k: (lambda s: jnp.clip(s - k * N_CHIP, 0, N_CHIP - 1))

    def body(c_ref, *refs):
        g_refs, got_refs, o_refs = refs[:n], refs[n:2 * n], refs[2 * n:]
        s = pl.program_id(0)
        for k in range(n):
            @pl.when(jnp.logical_and(s >= k * N_CHIP, s < (k + 1) * N_CHIP))
            def _(k=k):
                o_refs[k][0] = (g_refs[k][...].astype(F32) + got_refs[k][0].astype(F32)).astype(BF)

    grid_spec = pltpu.PrefetchScalarGridSpec(
        num_scalar_prefetch=1, grid=(n * N_CHIP,),
        in_specs=[pl.BlockSpec((r, D), lambda s, c_ref, k=k: (2 * chip(k)(s) + c_ref[0], 0)) for k in range(n)]
        + [pl.BlockSpec((1, r, D), lambda s, c_ref, k=k: (chip(k)(s), 0, 0)) for k in range(n)],
        out_specs=[pl.BlockSpec((1, r, D), lambda s, c_ref, k=k: (chip(k)(s), 0, 0)) for k in range(n)])
    return _pcall(body, name=name, grid_spec=grid_spec,
                  out_shape=[jax.ShapeDtypeStruct((N_CHIP, r, D), BF)] * n,
                  compiler_params=_cp(("arbitrary",)))(core, *grads, *gots)


def _chip_comm(pair_sums):
    n = len(pair_sums)
    r = [p.shape[1] for p in pair_sums]
    off = [sum(r[:w]) for w in range(n)]

    def tools(ins, outs, sems):
        send_sems, recv_sems, local_sems = sems
        x, y, c, chips = _place()
        my_chip = 2 * x + y

        def slot(w):
            return outs[0].at[my_chip, pl.ds(off[w], r[w]), :]

        own = [pltpu.make_async_copy(ins[w].at[my_chip], slot(w), local_sems.at[w]) for w in range(n)]
        return x, y, c, chips, my_chip, slot, own, send_sems, recv_sems

    def start(ins, outs, sems):
        x, y, c, chips, my_chip, slot, own, send_sems, recv_sems = tools(ins, outs, sems)
        for cp in own:
            cp.start()
        for j, chip in enumerate(chips):
            for w in range(n):
                pltpu.make_async_remote_copy(
                    src_ref=ins[w].at[2 * chip[0] + chip[1]], dst_ref=slot(w), send_sem=send_sems.at[j],
                    recv_sem=recv_sems.at[j], device_id=(*chip, c), device_id_type=MESH).start()

    def finish(ins, outs, sems):
        x, y, c, chips, my_chip, slot, own, send_sems, recv_sems = tools(ins, outs, sems)
        whole = outs[0].at[my_chip]
        for j in range(3):
            pltpu.make_async_remote_copy(
                src_ref=whole, dst_ref=whole, send_sem=send_sems.at[j], recv_sem=recv_sems.at[j],
                device_id=(x, y, c), device_id_type=MESH).wait()
        for cp in own:
            cp.wait()

    return _Comm(pair_sums, [jax.ShapeDtypeStruct((N_CHIP, sum(r), D), BF)],
                 [pltpu.SemaphoreType.DMA((3,)), pltpu.SemaphoreType.DMA((3,)), pltpu.SemaphoreType.DMA((n,))],
                 [(0.0, start), (1.0, finish)])


def _adam_math(w, g, m, v):
    m = ADAM_B1 * m + (1.0 - ADAM_B1) * g
    v = ADAM_B2 * v + (1.0 - ADAM_B2) * (g * g)
    m_hat = m / (1.0 - ADAM_B1 ** ADAM_STEP)
    v_hat = v / (1.0 - ADAM_B2 ** ADAM_STEP)
    delta = -ADAM_LR * (m_hat / (jnp.sqrt(v_hat) + ADAM_EPS) + ADAM_WD * w)
    return delta, m, v


SMALL = (("norm_mix_g", (1, D), 0), ("hgrn_norm_g", (1, D), 1), ("norm_ffn_g", (1, D), 2),
         ("norm_final_g", (1, D), 3), ("hgrn_lb_logits", (2, D), 4), ("attn_sinks", (1, 16), 6),
         ("b_in", (1, IN_W), 8))
LOSS_ROW = 7


def _small_allreduce_adam(grads, loss_row, params):
    n = len(SMALL)

    def rows_of(ref, shape, row):
        r, w = shape
        if w <= D:
            return ref[row:row + r, 0:w]
        pieces = [ref[row + k:row + k + 1, :] for k in range(-(-w // D))]
        return jnp.concatenate(pieces, axis=1)[:, 0:w]

    def body(*refs):
        g_refs, loss_ref = refs[:n], refs[n]
        wmv = refs[n + 1:4 * n + 1]
        loss_out = refs[4 * n + 1]
        outs = refs[4 * n + 2:8 * n + 2]
        mine, total, gath, send_sems, recv_sems = refs[8 * n + 2:]
        x, y, c, _ = _place()
        me = 4 * x + 2 * y + c
        mine[...] = jnp.zeros_like(mine)
        for g_ref, (_, (r, w), row) in zip(g_refs, SMALL):
            for k in range(-(-w // D)):
                wk = min(D, w - k * D)
                mine[row + k:row + k + r, 0:wk] = g_ref[:, k * D:k * D + wk]
        mine[LOSS_ROW:LOSS_ROW + 1, 0:128] = loss_ref[...]
        gath[me] = mine[...]
        cps = []
        for d in range(1, N_DEV):
            peer = (x ^ (d >> 2), y ^ ((d >> 1) & 1), c ^ (d & 1))
            cps.append(pltpu.make_async_remote_copy(
                src_ref=mine, dst_ref=gath.at[me], send_sem=send_sems.at[d - 1],
                recv_sem=recv_sems.at[d - 1], device_id=peer, device_id_type=MESH))
        for cp in cps:
            cp.start()
        for cp in cps:
            cp.wait()
        g = gath[0]
        for k in range(1, N_DEV):
            g = g + gath[k]
        total[...] = g
        loss_out[...] = total[LOSS_ROW:LOSS_ROW + 1, 0:128]
        for i, (_, shape, row) in enumerate(SMALL):
            gi = rows_of(total, shape, row)
            w_ref, m_ref, v_ref = wmv[3 * i:3 * i + 3]
            o = outs[4 * i:4 * i + 4]
            o[0][...] = gi
            o[1][...], o[2][...], o[3][...] = _adam_math(w_ref[...], gi, m_ref[...], v_ref[...])

    vm = pl.BlockSpec(memory_space=pltpu.VMEM)
    ins = [grads[name] for name, _, _ in SMALL] + [loss_row]
    for name, _, _ in SMALL:
        ins += list(params[name])
    out_shape = [jax.ShapeDtypeStruct((1, 128), F32)]
    for _, shape, _ in SMALL:
        out_shape += [jax.ShapeDtypeStruct(shape, F32)] * 4
    res = _pcall(body, name="small_allreduce_adam", in_specs=[vm] * len(ins), out_specs=[vm] * len(out_shape),
                 out_shape=out_shape,
                 scratch_shapes=[pltpu.VMEM((SMALL_ROWS, D), F32), pltpu.VMEM((SMALL_ROWS, D), F32),
                                 pltpu.VMEM((N_DEV, SMALL_ROWS, D), F32),
                                 pltpu.SemaphoreType.DMA((N_DEV - 1,)), pltpu.SemaphoreType.DMA((N_DEV - 1,))],
                 compiler_params=pltpu.CompilerParams(has_side_effects=True))(*ins)
    return res[0], {name: res[1 + 4 * i:5 + 4 * i] for i, (name, _, _) in enumerate(SMALL)}


def _adam(w, parts, index, m, v, *, name):
    rows = w.shape[0]
    tr = rows if rows <= 512 else rows // 2
    steps = rows // tr

    def body(w_ref, p_ref, m_ref, v_ref, g_ref, d_ref, mo_ref, vo_ref):
        g = p_ref[0].astype(F32)
        for a in range(1, N_CHIP):
            g = g + p_ref[a].astype(F32)
        g_ref[...] = g
        d_ref[...], mo_ref[...], vo_ref[...] = _adam_math(w_ref[...], g, m_ref[...], v_ref[...])

    spec = pl.BlockSpec((tr, D), lambda i: (i, 0))
    return _pcall(body, name=name, grid=(steps,),
                  in_specs=[spec, pl.BlockSpec((N_CHIP, tr, D), lambda i: (0, index * steps + i, 0)), spec, spec],
                  out_specs=[spec] * 4, out_shape=[jax.ShapeDtypeStruct((rows, D), F32)] * 4,
                  compiler_params=_cp(("parallel",)))(w, parts, m, v)


def _step(x, tgt, shards, norm_mix_g, b_in, sinks, logits, hgrn_norm_g, norm_ffn_g, norm_final_g):
    t = x.shape[0]
    core = lax.axis_index("c").astype(jnp.int32).reshape(1)

    u1, (win_t,) = _rms_fwd(x, norm_mix_g, tm=512, name="rms_mix", comm=_gather_comm(shards[0:1], (0.2, 0.4, 0.6, 0.8)))
    (q, kv, h3, hf, gates), (wg_t, wba, wbh, wout) = _inproj_fwd(
        u1, win_t, b_in, t=t, comm=_gather_comm([shards[1]] + shards[4:7], (0.3, 0.5, 0.7, 0.9)))
    (y_attn,), _ = _attn_fwd(q, kv, sinks, t=t)
    (y_hgrn, o_pre, states), (wu_t, wd) = _hgrn_fwd(h3, hf, logits, hgrn_norm_g, t=t,
                                                    comm=_gather_comm(shards[2:4], (0.3, 0.5, 0.7, 0.9)))
    col = lambda j: j
    first, second = (lambda j: 0), (lambda j: 1)
    gate_tiles = [(gates, D, first), (gates, D, second)]

    def merge(prods, ex):
        (ya_, yb_), (ga, gb) = prods, ex
        sa, sb = _sig(ga.astype(F32)), _sig(gb.astype(F32))
        return sa, sb, ya_ * sa * (1.0 - sa), yb_ * sb * (1.0 - sb), sa * ya_ + sb * yb_

    sig_a, sig_b, dgate_a, dgate_b, merged = _fmm(
        [y_attn, y_hgrn], [(0, wba, False), (1, wbh, False)], gate_tiles, merge,
        [(BF, D, D, first)] * 5, m=t, n=D, tm=512, tn=D, name="branch_merge")
    def resid_norm(prods, ex):
        (p,), (xv, gv) = prods, ex
        hv = xv + p
        return hv, hv * lax.rsqrt(jnp.mean(hv * hv, axis=-1, keepdims=True) + EPS) * gv

    h1, u2 = _fmm([merged], [(0, wout, False)], [(x, D, first)], resid_norm, [(F32, D, D, first), (BF, D, D, first)],
                  m=t, n=D, tm=1024, tn=D, name="out_proj", vecs=[norm_ffn_g])

    def swiglu(prods, ex):
        g_, u_ = prods
        s = _sig(g_)
        silu = g_ * s
        return u_ * s * (1.0 + g_ * (1.0 - s)), silu, silu * u_

    dz_dgate, dz_dup, z = _fmm([u2], [(0, wg_t, True), (0, wu_t, True)], [], swiglu,
                               [(BF, FFN, FFN // 2, col)] * 3, m=t, n=FFN, tm=1024, tn=FFN // 2,
                               name="ffn_gate_up")
    def loss_head(prods, ex):
        (p,), (hv, tv, gv) = prods, ex
        hv = hv + p
        r = lax.rsqrt(jnp.mean(hv * hv, axis=-1, keepdims=True) + EPS)
        xh = hv * r
        err = xh * gv - tv
        lp = jnp.sum(jnp.sum(err * err, axis=1, keepdims=True), axis=0, keepdims=True) * (0.5 / D)
        dy = err * (1.0 / D)
        dxh = dy * gv
        dh = r * (dxh - xh * jnp.mean(dxh * xh, axis=-1, keepdims=True))
        return dh, dh, jnp.sum(dy * xh, axis=0, keepdims=True), jnp.broadcast_to(lp, (1, 128))

    dh2, dh2_b, d_norm_final, loss_row = _fmm(
        [z], [(0, wd, False)], [(h1, D, first), (tgt, D, first)], loss_head, [(F32, D, D, first), (BF, D, D, first)],
        m=t, n=D, tm=512, tn=D, name="ffn_down_loss", vecs=[norm_final_g], sums=[D, 128])

    def swiglu_bwd(prods, ex):
        (dz,), (da_, db_) = prods, ex
        return dz * da_.astype(F32), dz * db_.astype(F32)

    ffn_tiles = [(dz_dgate, FFN // 2, col), (dz_dup, FFN // 2, col)]
    dgt, dup = _fmm([dh2_b], [(0, wd, True)], ffn_tiles, swiglu_bwd, [(BF, FFN, FFN // 2, col)] * 2,
                    m=t, n=FFN, tm=1024, tn=FFN // 2, name="d_gate_up")
    (d_wd,) = _wgrad([z], dh2_b, name="d_w_down")
    (du2,) = _fmm([dgt, dup], [(0, wg_t, False), (1, wu_t, False)], [], lambda prods, ex: (prods[0] + prods[1],),
                  [(F32, D, 512, col)], m=t, n=D, tm=1024, tn=512, name="d_u2")
    d_wg, d_wu = _wgrad([dgt, dup], u2, name="d_w_gate_up")
    dh1, dh1_b, d_norm_ffn = _rms_bwd(du2, h1, norm_ffn_g, dh2, tm=512, name="rms_ffn_bwd")
    (d_wout,) = _wgrad([merged], dh1_b, name="d_w_out")

    def merge_bwd(prods, ex):
        (dm,), (sa, sb, ca, cb, wa, wb) = prods, ex
        dgate = jnp.concatenate([dm * ca.astype(F32), dm * cb.astype(F32)], axis=1)
        dya_ = (dm * sa.astype(F32)).astype(BF)
        dyb_ = (dm * sb.astype(F32)).astype(BF)
        return (dya_, dyb_, dgate, lax.dot_general(dya_, wa, _NT, preferred_element_type=F32),
                lax.dot_general(dyb_, wb, _NT, preferred_element_type=F32))

    ffn_grads = (d_wg, d_wu, d_wd)
    (dya, dyb, dgates, dy_attn, dy_hgrn), got = _fmm(
        [dh1_b], [(0, wout, True)], [(a, D, first) for a in (sig_a, sig_b, dgate_a, dgate_b)], merge_bwd,
        [(BF, D, D, first), (BF, D, D, first), (BF, 2 * D, 2 * D, first), (BF, D, D, first), (F32, D, D, first)],
        m=t, n=D, tm=512, tn=D, name="d_merge", consts=[wba, wbh], comm=_pair_comm(ffn_grads))
    pair_ffn = _pair_add(ffn_grads, got, core, name="pair_add_ffn")
    (d_wba,) = _wgrad([y_attn], dya, name="d_w_ba")
    (d_wbh,) = _wgrad([y_hgrn], dyb, name="d_w_bh")
    sq_grads = (d_wba, d_wbh, d_wout)
    (dh4, d_logits, d_hgrn_norm), (parts_ffn, *got) = _hgrn_bwd(
        h3, hf, logits, hgrn_norm_g, o_pre, states, dy_hgrn, t=t,
        comm=_both(_chip_comm(pair_ffn), _pair_comm(sq_grads)))
    pair_sq = _pair_add(sq_grads, got, core, name="pair_add_sq")
    (dq, dkv, d_sinks), (parts_sq,) = _attn_bwd(q, kv, sinks, dy_attn, t=t, comm=_chip_comm(pair_sq))
    dps = (dq, dkv, dh4, dgates)
    d_win_t, d_b_in = _inproj_bwd_w(dps, u1, t=t)
    half0, got_in = _inproj_bwd_x(dps, win_t, x, norm_mix_g, dh1, t=t, part=0, comm=_pair_comm([d_win_t]))
    pair_in = _pair_add([d_win_t], got_in, core, name="pair_add_w_in")
    (grad_x, d_norm_mix), (parts_in,) = _inproj_bwd_x(dps, win_t, x, norm_mix_g, dh1, t=t, part=1, prev=half0,
                                                      comm=_chip_comm(pair_in))

    small_grads = (d_norm_mix, d_b_in, d_sinks, d_logits, d_hgrn_norm, d_norm_ffn, d_norm_final)
    return loss_row, grad_x, (parts_in, parts_ffn, parts_sq), small_grads


def kernel(x, norm_mix_g, w_in, b_in, attn_sinks, hgrn_lb_logits, hgrn_norm_g, w_branch_attn, w_branch_hgrn, w_out, norm_ffn_g, w_ffn_gate, w_ffn_up, w_ffn_down, norm_final_g, loss_target, m_norm_mix_g, m_w_in, m_b_in, m_attn_sinks, m_hgrn_lb_logits, m_hgrn_norm_g, m_w_branch_attn, m_w_branch_hgrn, m_w_out, m_norm_ffn_g, m_w_ffn_gate, m_w_ffn_up, m_w_ffn_down, m_norm_final_g, v_norm_mix_g, v_w_in, v_b_in, v_attn_sinks, v_hgrn_lb_logits, v_hgrn_norm_g, v_w_branch_attn, v_w_branch_hgrn, v_w_out, v_norm_ffn_g, v_w_ffn_gate, v_w_ffn_up, v_w_ffn_down, v_norm_final_g):
    shards = [w_in[0].T.astype(BF), w_ffn_gate[0].T.astype(BF), w_ffn_up[0].T.astype(BF),
              w_ffn_down[0].astype(BF), w_branch_attn[0].astype(BF), w_branch_hgrn[0].astype(BF),
              w_out[0].astype(BF)]
    loss_row, grad_x, grad_parts, small_grads = _step(
        x[0], loss_target[0], shards, norm_mix_g, b_in, attn_sinks, hgrn_lb_logits, hgrn_norm_g,
        norm_ffn_g, norm_final_g.reshape(1, D))

    d_norm_mix, d_b_in, d_sinks, d_logits, d_hgrn_norm, d_norm_ffn, d_norm_final = small_grads
    row = lambda a: a.reshape(1, D)
    loss_out, small = _small_allreduce_adam(
        dict(norm_mix_g=d_norm_mix, hgrn_norm_g=d_hgrn_norm, norm_ffn_g=d_norm_ffn, norm_final_g=d_norm_final,
             hgrn_lb_logits=d_logits, attn_sinks=d_sinks, b_in=d_b_in),
        loss_row,
        dict(norm_mix_g=(norm_mix_g, m_norm_mix_g, v_norm_mix_g), hgrn_norm_g=(hgrn_norm_g, m_hgrn_norm_g, v_hgrn_norm_g),
             norm_ffn_g=(norm_ffn_g, m_norm_ffn_g, v_norm_ffn_g),
             norm_final_g=(row(norm_final_g), row(m_norm_final_g), row(v_norm_final_g)),
             hgrn_lb_logits=(hgrn_lb_logits, m_hgrn_lb_logits, v_hgrn_lb_logits),
             attn_sinks=(attn_sinks, m_attn_sinks, v_attn_sinks), b_in=(b_in, m_b_in, v_b_in)))
    small["norm_final_g"] = [a.reshape(D) for a in small["norm_final_g"]]
    loss = loss_out[0, 0]

    names = ["w_in", "w_ffn_gate", "w_ffn_up", "w_ffn_down", "w_branch_attn", "w_branch_hgrn", "w_out"]
    w_full = dict(w_in=(w_in, m_w_in, v_w_in), w_ffn_gate=(w_ffn_gate, m_w_ffn_gate, v_w_ffn_gate),
                  w_ffn_up=(w_ffn_up, m_w_ffn_up, v_w_ffn_up), w_ffn_down=(w_ffn_down, m_w_ffn_down, v_w_ffn_down),
                  w_branch_attn=(w_branch_attn, m_w_branch_attn, v_w_branch_attn),
                  w_branch_hgrn=(w_branch_hgrn, m_w_branch_hgrn, v_w_branch_hgrn),
                  w_out=(w_out, m_w_out, v_w_out))
    parts_in, parts_ffn, parts_sq = grad_parts
    where = [(parts_in, 0), (parts_ffn, 0), (parts_ffn, 1), (parts_ffn, 2), (parts_sq, 0), (parts_sq, 1), (parts_sq, 2)]
    big = {}
    for i, name in enumerate(names):
        view = (lambda a: a[0].T) if i < 3 else (lambda a: a[0])
        back = (lambda a: a.T[None]) if i < 3 else (lambda a: a[None])
        wv, mv, vv = w_full[name]
        res = _adam(view(wv), where[i][0], where[i][1], view(mv), view(vv), name="adam_" + name)
        big[name] = [back(a) for a in res]

    order = ["norm_mix_g", "w_in", "b_in", "attn_sinks", "hgrn_lb_logits", "hgrn_norm_g", "w_branch_attn",
             "w_branch_hgrn", "w_out", "norm_ffn_g", "w_ffn_gate", "w_ffn_up", "w_ffn_down", "norm_final_g"]
    outs = [loss, grad_x[None]]
    for kind in range(4):
        for name in order:
            outs.append(big[name][kind] if name in big else small[name][kind])
    return tuple(outs)
```

```python
import math

import jax
import jax.numpy as jnp
from jax import lax
from jax.experimental import pallas as pl
from jax.experimental.pallas import tpu as pltpu

F32 = jnp.float32
BF = jnp.bfloat16
MESH = pl.DeviceIdType.MESH

D = 1024
HEAD = 64
N_PAIR = 8
BLK = 128
CH = 64
HG_SUB = 4
HG_HEADS = 8
HG_K = 128
FFN = 2816
IN_W = 7424
N_DEV = 8
N_CHIP = 4
EPS = 1e-6
NEG = -1e30
SCALE = 1.0 / math.sqrt(HEAD)
VMEM_LIMIT = 56 * 1024 * 1024
WT = 256

ADAM_LR, ADAM_B1, ADAM_B2, ADAM_EPS, ADAM_WD, ADAM_STEP = 0.001, 0.9, 0.999, 1e-08, 0.01, 10

GRP_OFF = (0, D // WT, (D + 256) // WT, (5 * D + 256) // WT)
GRP_N = (D // WT, 256 // WT, 4 * D // WT, 2 * D // WT)
SMALL_ROWS = 16


_NN = (((1,), (0,)), ((), ()))
_NT = (((1,), (1,)), ((), ()))
_TN = (((0,), (0,)), ((), ()))


def _pcall(body, **kw):
    return pl.pallas_call(body, **kw)


def _cp(sem=None, **kw):
    return pltpu.CompilerParams(dimension_semantics=sem, vmem_limit_bytes=VMEM_LIMIT, **kw)


def _sig(v):
    return 0.5 * jnp.tanh(0.5 * v) + 0.5


def _accum(ref, val, first):
    @pl.when(first)
    def _():
        ref[...] = val

    @pl.when(jnp.logical_not(first))
    def _():
        ref[...] += val


class _Comm:
    def __init__(self, ins, out_shapes, sem_shapes, phases):
        self.ins, self.out_shapes, self.sem_shapes, self.phases = list(ins), list(out_shapes), list(sem_shapes), phases


def _both(a, b):
    ni, no, ns = len(a.ins), len(a.out_shapes), len(a.sem_shapes)

    def of_a(fn):
        return lambda ins, outs, sems: fn(ins[:ni], outs[:no], sems[:ns])

    def of_b(fn):
        return lambda ins, outs, sems: fn(ins[ni:], outs[no:], sems[ns:])

    return _Comm(a.ins + b.ins, a.out_shapes + b.out_shapes, a.sem_shapes + b.sem_shapes,
                 [(f, of_a(fn)) for f, fn in a.phases] + [(f, of_b(fn)) for f, fn in b.phases])


def _host(body, comm, n_in, n_out, n_scr, nsteps, step_fn):
    if comm is None:
        return body
    ci, co = len(comm.ins), len(comm.out_shapes)

    def wrapped(*refs):
        p = 0
        ins, p = refs[p:p + n_in], p + n_in
        cins, p = refs[p:p + ci], p + ci
        outs, p = refs[p:p + n_out], p + n_out
        couts, p = refs[p:p + co], p + co
        scr, p = refs[p:p + n_scr], p + n_scr
        csems = refs[p:]
        step = step_fn()
        for frac, fn in comm.phases:
            if frac < 1.0:
                @pl.when(step == int(round(frac * (nsteps - 1))))
                def _(fn=fn):
                    fn(cins, couts, csems)
        body(*ins, *outs, *scr)
        for frac, fn in comm.phases:
            if frac >= 1.0:
                @pl.when(step == nsteps - 1)
                def _(fn=fn):
                    fn(cins, couts, csems)

    return wrapped


def _hosted_call(body, comm, args, *, name, grid, in_specs, out_specs, out_shape, scratch_shapes, sem,
                 nsteps, step_fn, aliases=None):
    n_in, n_out, n_scr = len(in_specs), len(out_specs), len(scratch_shapes)
    args = list(args)
    extra = {}
    if comm is not None:
        in_specs = list(in_specs) + [_hbm_spec()] * len(comm.ins)
        out_specs = list(out_specs) + [_hbm_spec()] * len(comm.out_shapes)
        out_shape = list(out_shape) + comm.out_shapes
        scratch_shapes = list(scratch_shapes) + comm.sem_shapes
        args += comm.ins
        extra = dict(has_side_effects=True)
    outs = _pcall(_host(body, comm, n_in, n_out, n_scr, nsteps, step_fn), name=name, grid=grid,
                  in_specs=in_specs, out_specs=out_specs, out_shape=out_shape, scratch_shapes=scratch_shapes,
                  input_output_aliases=aliases or {}, compiler_params=_cp(sem, **extra))(*args)
    return list(outs[:n_out]), list(outs[n_out:])


def _hbm_spec():
    return pl.BlockSpec(memory_space=pl.ANY)


def _wgrad(a_list, b, *, name):
    (t, m), n, gm = a_list[0].shape, b.shape[1], a_list[0].shape[1] // WT
    n_a = len(a_list)
    tile = lambda k: (lambda s: jnp.clip(s - k * gm, 0, gm - 1))

    def body(*refs):
        a_refs, b_ref, o_refs = refs[:n_a], refs[n_a], refs[n_a + 1:]
        s = pl.program_id(0)
        for k in range(n_a):
            @pl.when(jnp.logical_and(s >= k * gm, s < (k + 1) * gm))
            def _(k=k):
                o_refs[k][...] = lax.dot_general(a_refs[k][...], b_ref[...], _TN,
                                                 preferred_element_type=F32).astype(BF)

    return _pcall(body, name=name, grid=(n_a * gm,),
                  in_specs=[pl.BlockSpec((t, WT), lambda s, k=k: (0, tile(k)(s))) for k in range(n_a)]
                  + [pl.BlockSpec((t, n), lambda s: (0, 0))],
                  out_specs=[pl.BlockSpec((WT, n), lambda s, k=k: (tile(k)(s), 0)) for k in range(n_a)],
                  out_shape=[jax.ShapeDtypeStruct((m, n), BF)] * n_a,
                  compiler_params=_cp(("arbitrary",)))(*a_list, b)


def _fmm(lhs, rhs, extras, epilogue, outs, *, m, n, tm, tn, name, comm=None, vecs=(), consts=(), sums=()):
    tm, tn = min(tm, m), min(tn, n)
    assert m % tm == 0 and n % tn == 0 and (not sums or tn == n), (name, m, n, tm, tn)
    in_specs, args = [], []
    for a in lhs:
        in_specs.append(pl.BlockSpec((tm, a.shape[1]), lambda i, j: (i, 0)))
        args.append(a)
    for li, b, tb in rhs:
        k = lhs[li].shape[1]
        in_specs.append(pl.BlockSpec((tn, k), lambda i, j: (j, 0)) if tb
                        else pl.BlockSpec((k, tn), lambda i, j: (0, j)))
        args.append(b)
    for arr, w, col in extras:
        in_specs.append(pl.BlockSpec((tm, w), lambda i, j, col=col: (i, col(j))))
        args.append(arr)
    for vec in vecs:
        in_specs.append(pl.BlockSpec((1, tn), lambda i, j: (0, j)))
        args.append(vec)
    for whole in consts:
        in_specs.append(pl.BlockSpec(whole.shape, lambda i, j: (0, 0)))
        args.append(whole)
    out_specs = [pl.BlockSpec((tm, w), lambda i, j, col=col: (i, col(j))) for _, _, w, col in outs]
    out_shape = [jax.ShapeDtypeStruct((m, total), dt) for dt, total, _, _ in outs]
    for w in sums:
        out_specs.append(pl.BlockSpec((1, w), lambda i, j: (0, 0)))
        out_shape.append(jax.ShapeDtypeStruct((1, w), F32))
    nl, nr, ne, no = len(lhs), len(rhs), len(extras) + len(vecs) + len(consts), len(outs)

    def body(*refs):
        prods = []
        for r, (li, _, tb) in enumerate(rhs):
            prods.append(lax.dot_general(refs[li][...], refs[nl + r][...], _NT if tb else _NN,
                                         preferred_element_type=F32))
        vals = epilogue(prods, [ref[...] for ref in refs[nl + nr:nl + nr + ne]])
        o_refs = refs[nl + nr + ne:]
        for o_ref, v in zip(o_refs[:no], vals[:no]):
            o_ref[...] = v.astype(o_ref.dtype)
        for s_ref, v in zip(o_refs[no:], vals[no:]):
            _accum(s_ref, v, pl.program_id(0) == 0)

    gm, gn = m // tm, n // tn
    res, comm_res = _hosted_call(
        body, comm, args, name=name, grid=(gm, gn), in_specs=in_specs, out_specs=out_specs,
        out_shape=out_shape, scratch_shapes=[], sem=("arbitrary", "arbitrary"), nsteps=gm * gn,
        step_fn=lambda: pl.program_id(0) * gn + pl.program_id(1))
    return res if comm is None else (res, comm_res)


def _grp_of(i):
    return [jnp.logical_and(i >= GRP_OFF[g], i < GRP_OFF[g] + GRP_N[g]) for g in range(4)]


def _grp_idx(i, g):
    return jnp.clip(i - GRP_OFF[g], 0, GRP_N[g] - 1)


def _inproj_fwd(u, win_t, b_in, *, t, comm=None):
    tm = min(1024, t)
    n_row = t // tm
    n_chunks, h_first, g_first = 8, 2, 6
    sub = D // WT

    def w_block(l):
        return jnp.where(l == 0, GRP_OFF[0], jnp.where(l == 1, GRP_OFF[1], GRP_OFF[2] + sub * (l - h_first)))

    def body(u_ref, *rest):
        w_refs, b_refs, (q_ref, kv_ref, h3_ref, hf_ref, g_ref) = rest[:sub], rest[sub:2 * sub], rest[2 * sub:]
        l = pl.program_id(1)

        @pl.when(l == 1)
        def _():
            kv_ref[...] = (lax.dot_general(u_ref[...], w_refs[0][...], _NT, preferred_element_type=F32)
                           + b_refs[0][...]).astype(BF)

        is_hf = l == h_first + 1
        in_h3 = jnp.logical_and(jnp.logical_and(l >= h_first, l < g_first), jnp.logical_not(is_hf))
        for pred, o_ref in ((l == 0, q_ref), (in_h3, h3_ref), (is_hf, hf_ref), (l >= g_first, g_ref)):
            @pl.when(pred)
            def _(o_ref=o_ref):
                w = jnp.concatenate([w[...] for w in w_refs], axis=0)
                b = jnp.concatenate([b[...] for b in b_refs], axis=1)
                o_ref[...] = (lax.dot_general(u_ref[...], w, _NT, preferred_element_type=F32) + b).astype(o_ref.dtype)

    return _hosted_call(
        body, comm, [u] + [win_t] * sub + [b_in] * sub, name="inproj_fwd", grid=(n_row, n_chunks),
        in_specs=[pl.BlockSpec((tm, D), lambda i, l: (i, 0))]
        + [pl.BlockSpec((WT, D), lambda i, l, o=o: (w_block(l) + o, 0)) for o in range(sub)]
        + [pl.BlockSpec((1, WT), lambda i, l, o=o: (0, w_block(l) + o)) for o in range(sub)],
        out_specs=[pl.BlockSpec((tm, D), lambda i, l: (i, 0)),
                   pl.BlockSpec((tm, 256), lambda i, l: (i, 0)),
                   pl.BlockSpec((tm, D), lambda i, l: (i, jnp.clip(l - h_first - 1, 0, 2))),
                   pl.BlockSpec((tm, D), lambda i, l: (i, 0)),
                   pl.BlockSpec((tm, D), lambda i, l: (i, jnp.clip(l - g_first, 0, 1)))],
        out_shape=[jax.ShapeDtypeStruct((t, D), BF), jax.ShapeDtypeStruct((t, 256), BF),
                   jax.ShapeDtypeStruct((t, 3 * D), BF), jax.ShapeDtypeStruct((t, D), F32),
                   jax.ShapeDtypeStruct((t, 2 * D), BF)],
        scratch_shapes=[], sem=("arbitrary", "arbitrary"), nsteps=n_row * n_chunks,
        step_fn=lambda: pl.program_id(0) * n_chunks + pl.program_id(1))


def _inproj_bwd_x(dps, win_t, x, g, resid, *, t, part, prev=None, comm=None):
    n_row = 8 if t >= 4096 else 4
    tm = t // n_row
    first = n_row // 4
    per = first if part == 0 else n_row - first
    row = lambda i: part * first + i

    n_chunks = 4
    sub = 2 * D // WT

    def w_block(l):
        return jnp.where(l == 0, 0, GRP_OFF[2] + sub * (l - 1))

    def body(d0, d1, d2, d3, *rest):
        w_refs, (x_ref, g_ref, r_ref) = rest[:sub], rest[sub:sub + 3]
        dg_prev = rest[sub + 3] if prev is not None else None
        o_ref, dg_ref, acc_ref = rest[-3], rest[-2], rest[-1]
        i, l = pl.program_id(0), pl.program_id(1)

        @pl.when(l == 0)
        def _():
            wq = jnp.concatenate([w[...] for w in w_refs[:GRP_N[0]]], axis=0)
            acc_ref[...] = (jnp.dot(d0[...], wq, preferred_element_type=F32)
                            + jnp.dot(d1[...], w_refs[GRP_N[0]][...], preferred_element_type=F32))

        for pred, d_ref in ((jnp.logical_and(l >= 1, l < 3), d2), (l == 3, d3)):
            @pl.when(pred)
            def _(d_ref=d_ref):
                w = jnp.concatenate([w[...] for w in w_refs], axis=0)
                acc_ref[...] += jnp.dot(d_ref[...], w, preferred_element_type=F32)

        @pl.when(l == n_chunks - 1)
        def _():
            xv = x_ref[...]
            r = lax.rsqrt(jnp.mean(xv * xv, axis=-1, keepdims=True) + EPS)
            xh = xv * r
            du = acc_ref[...]
            dxh = du * g_ref[...]
            o_ref[...] = r_ref[...] + r * (dxh - xh * jnp.mean(dxh * xh, axis=-1, keepdims=True))
            dg = jnp.sum(du * xh, axis=0, keepdims=True)
            if dg_prev is not None:
                dg = dg + jnp.where(i == 0, 1.0, 0.0) * dg_prev[...]
            _accum(dg_ref, dg, i == 0)

    rows = lambda w: pl.BlockSpec((tm, w), lambda i, l: (row(i), 0))
    in_specs = ([rows(D), rows(256),
                 pl.BlockSpec((tm, 2 * D), lambda i, l: (row(i), jnp.clip(l - 1, 0, 1))), rows(2 * D)]
                + [pl.BlockSpec((WT, D), lambda i, l, o=o: (w_block(l) + o, 0)) for o in range(sub)]
                + [rows(D), pl.BlockSpec((1, D), lambda i, l: (0, 0)), rows(D)])
    args = list(dps) + [win_t] * sub + [x, g, resid]
    aliases = None
    if prev is not None:
        in_specs += [pl.BlockSpec((1, D), lambda i, l: (0, 0)), _hbm_spec()]
        args += [prev[1], prev[0]]
        aliases = {len(args) - 1: 0}
    return _hosted_call(
        body, comm, args, name="inproj_bwd_x%d" % part, grid=(per, n_chunks), in_specs=in_specs,
        out_specs=[rows(D), pl.BlockSpec((1, D), lambda i, l: (0, 0))],
        out_shape=[jax.ShapeDtypeStruct((t, D), F32), jax.ShapeDtypeStruct((1, D), F32)],
        scratch_shapes=[pltpu.VMEM((tm, D), F32)], sem=("arbitrary", "arbitrary"), nsteps=per * n_chunks,
        step_fn=lambda: pl.program_id(0) * n_chunks + pl.program_id(1), aliases=aliases)


def _inproj_bwd_w(dps, u, *, t):
    n_tiles = IN_W // WT
    dims = (((0,), (0,)), ((), ()))

    def body(d0, d1, d2, d3, u_ref, o_ref, db_ref):
        i = pl.program_id(0)
        uv = u_ref[...]
        for g, (pred, d_ref) in enumerate(zip(_grp_of(i), (d0, d1, d2, d3))):
            @pl.when(pred)
            def _(d_ref=d_ref):
                dv = d_ref[...]
                o_ref[...] = lax.dot_general(dv, uv, dims, preferred_element_type=F32).astype(BF)
                db_ref[...] = jnp.sum(dv.astype(F32), axis=0, keepdims=True)

    return _pcall(body, name="inproj_bwd_w", grid=(n_tiles,),
                  in_specs=[pl.BlockSpec((t, WT), lambda i, g=g: (0, _grp_idx(i, g))) for g in range(4)]
                  + [pl.BlockSpec((t, D), lambda i: (0, 0))],
                  out_specs=[pl.BlockSpec((WT, D), lambda i: (i, 0)),
                             pl.BlockSpec((1, WT), lambda i: (0, i))],
                  out_shape=[jax.ShapeDtypeStruct((IN_W, D), BF), jax.ShapeDtypeStruct((1, IN_W), F32)],
                  compiler_params=_cp(("arbitrary",)))(*dps, u)


def _row_spec(tm, width, col=0):
    return pl.BlockSpec((tm, width), lambda i: (i, col))


def _vec_spec(width):
    return pl.BlockSpec((1, width), lambda i: (0, 0))


def _rms_fwd(x, g, *, tm, name, comm=None):
    t = x.shape[0]
    tm = min(tm, t)

    def body(x_ref, g_ref, u_ref):
        xv = x_ref[...]
        r = lax.rsqrt(jnp.mean(xv * xv, axis=-1, keepdims=True) + EPS)
        u_ref[...] = (xv * r * g_ref[...]).astype(BF)

    (u,), comm_res = _hosted_call(
        body, comm, (x, g), name=name, grid=(t // tm,), in_specs=[_row_spec(tm, D), _vec_spec(D)],
        out_specs=[_row_spec(tm, D)], out_shape=[jax.ShapeDtypeStruct((t, D), BF)], scratch_shapes=[],
        sem=("arbitrary",), nsteps=t // tm, step_fn=lambda: pl.program_id(0))
    return u if comm is None else (u, comm_res)


def _rms_bwd(du, x, g, resid, *, tm, name):
    t = x.shape[0]
    tm = min(tm, t)

    def body(du_ref, x_ref, g_ref, r_ref, dx_ref, dxb_ref, dg_ref):
        xv = x_ref[...]
        r = lax.rsqrt(jnp.mean(xv * xv, axis=-1, keepdims=True) + EPS)
        xh = xv * r
        duv = du_ref[...]
        dxh = duv * g_ref[...]
        dx = r_ref[...] + r * (dxh - xh * jnp.mean(dxh * xh, axis=-1, keepdims=True))
        dx_ref[...] = dx
        dxb_ref[...] = dx.astype(BF)
        _accum(dg_ref, jnp.sum(duv * xh, axis=0, keepdims=True), pl.program_id(0) == 0)

    return _pcall(body, name=name, grid=(t // tm,),
                  in_specs=[_row_spec(tm, D), _row_spec(tm, D), _vec_spec(D), _row_spec(tm, D)],
                  out_specs=[_row_spec(tm, D), _row_spec(tm, D), _vec_spec(D)],
                  out_shape=[jax.ShapeDtypeStruct((t, D), F32), jax.ShapeDtypeStruct((t, D), BF),
                             jax.ShapeDtypeStruct((1, D), F32)],
                  compiler_params=_cp(("arbitrary",)))(du, x, g, resid)


def _attn_kv_tiles(kprev, kcur):
    kv = jnp.concatenate([kprev, kcur], axis=0).astype(F32)
    lo = lax.broadcasted_iota(jnp.int32, (2 * BLK, 128), 1) < HEAD
    tiles = []
    for part in (kv[:, 0:128], kv[:, 128:256]):
        rolled = pltpu.roll(part, HEAD, 1)
        z = jnp.zeros_like(part)
        tiles.append(((jnp.where(lo, part, z).astype(BF), jnp.where(lo, z, rolled).astype(BF)),
                      (jnp.where(lo, rolled, z).astype(BF), jnp.where(lo, z, part).astype(BF))))
    k_t, v_t = tiles
    return [(jnp.concatenate(k_t[h], axis=0), jnp.concatenate(v_t[h], axis=0)) for h in range(2)]


def _attn_mask(i):
    qi = lax.broadcasted_iota(jnp.int32, (BLK, 2 * BLK), 0)
    kj = lax.broadcasted_iota(jnp.int32, (BLK, 2 * BLK), 1)
    first_key = jnp.where(i == 0, BLK, 0)
    in_prev = jnp.logical_and(jnp.logical_and(kj < BLK, kj > qi), kj >= first_key)
    in_cur = jnp.logical_and(kj >= BLK, kj - BLK <= qi)
    return jnp.logical_or(in_prev, in_cur)


def _attn_probs(s, sink, valid):
    s = jnp.where(valid, s * SCALE, NEG)
    mx = jnp.maximum(jnp.max(s, axis=-1, keepdims=True), sink)
    e = jnp.exp(s - mx)
    es = jnp.exp(sink - mx)
    inv = 1.0 / (jnp.sum(e, axis=-1, keepdims=True) + es)
    return e * inv, es * inv


_KEYS = 2 * BLK


def _pair(ref, j):
    return ref[:, j * 128:(j + 1) * 128]


def _attn_fwd(q, kv, sinks, *, t, comm=None):
    nb = t // BLK

    def body(sink_ref, q_ref, kp_ref, kc_ref, o_ref):
        i = pl.program_id(0)
        for c in range(2):
            rows = slice(c * BLK, (c + 1) * BLK)
            valid = _attn_mask(2 * i + c)
            tiles = _attn_kv_tiles(kp_ref[...] if c == 0 else kc_ref[0:BLK, :], kc_ref[rows, :])
            s = [lax.dot_general(q_ref[rows, j * 128:(j + 1) * 128], tiles[j // 4][0], _NT,
                                 preferred_element_type=F32) for j in range(N_PAIR)]
            p = []
            for j in range(N_PAIR):
                pe, _ = _attn_probs(s[j][:, 0:_KEYS], sink_ref[0, 2 * j], valid)
                po, _ = _attn_probs(s[j][:, _KEYS:2 * _KEYS], sink_ref[0, 2 * j + 1], valid)
                p.append(jnp.concatenate([pe.astype(BF), po.astype(BF)], axis=1))
            for j in range(N_PAIR):
                o_ref[rows, j * 128:(j + 1) * 128] = jnp.dot(p[j], tiles[j // 4][1],
                                                             preferred_element_type=F32).astype(BF)

    return _hosted_call(
        body, comm, (sinks, q, kv, kv), name="attn_fwd", grid=(nb // 2,),
        in_specs=[pl.BlockSpec(memory_space=pltpu.SMEM),
                  pl.BlockSpec((2 * BLK, D), lambda i: (i, 0)),
                  pl.BlockSpec((BLK, 256), lambda i: (jnp.maximum(2 * i - 1, 0), 0)),
                  pl.BlockSpec((2 * BLK, 256), lambda i: (i, 0))],
        out_specs=[pl.BlockSpec((2 * BLK, D), lambda i: (i, 0))],
        out_shape=[jax.ShapeDtypeStruct((t, D), BF)],
        scratch_shapes=[], sem=("arbitrary",), nsteps=nb // 2, step_fn=lambda: pl.program_id(0))


def _attn_bwd(q, kv, sinks, do, *, t, comm=None):
    nb = t // BLK
    last = nb - 1

    def body(sink_ref, q_ref, kp_ref, kc_ref, do_ref, dq_ref, dkv_ref, ds_ref, carry_ref):
        i = pl.program_id(0)

        @pl.when(i == 0)
        def _():
            ds_ref[...] = jnp.zeros_like(ds_ref)
            carry_ref[...] = jnp.zeros_like(carry_ref)

        @pl.when(i < nb)
        def _():
            valid = _attn_mask(i)
            tiles = _attn_kv_tiles(kp_ref[...], kc_ref[...])
            lane1 = lax.broadcasted_iota(jnp.int32, (1, 128), 1)
            dsink = jnp.zeros((1, 128), F32)
            s = [lax.dot_general(_pair(q_ref, j), tiles[j // 4][0], _NT, preferred_element_type=F32)
                 for j in range(N_PAIR)]
            dp = [lax.dot_general(_pair(do_ref, j), tiles[j // 4][1], _NT, preferred_element_type=F32)
                  for j in range(N_PAIR)]
            p_all, ds_all = [], []
            for j in range(N_PAIR):
                halves = []
                for par in range(2):
                    cols = slice(par * _KEYS, (par + 1) * _KEYS)
                    p, ps = _attn_probs(s[j][:, cols], sink_ref[0, 2 * j + par], valid)
                    dpj = dp[j][:, cols]
                    dd = jnp.sum(p * dpj, axis=-1, keepdims=True)
                    dsink = dsink + jnp.where(lane1 == 2 * j + par,
                                              -jnp.sum(ps * dd, axis=0, keepdims=True), 0.0)
                    halves.append((p.astype(BF), (p * (dpj - dd)).astype(BF)))
                p_all.append(jnp.concatenate([halves[0][0], halves[1][0]], axis=1))
                ds_all.append(jnp.concatenate([halves[0][1], halves[1][1]], axis=1))
            for j in range(N_PAIR):
                dq_ref[:, j * 128:(j + 1) * 128] = (
                    jnp.dot(ds_all[j], tiles[j // 4][0], preferred_element_type=F32) * SCALE).astype(BF)
            ds_ref[...] += dsink
            gk, gv = [], []
            for h in range(2):
                grp = range(4 * h, 4 * h + 4)
                q_rows = jnp.concatenate([_pair(q_ref, j) for j in grp], axis=0)
                do_rows = jnp.concatenate([_pair(do_ref, j) for j in grp], axis=0)
                g_k = lax.dot_general(jnp.concatenate([ds_all[j] for j in grp], axis=0), q_rows, _TN,
                                      preferred_element_type=F32)
                g_v = lax.dot_general(jnp.concatenate([p_all[j] for j in grp], axis=0), do_rows, _TN,
                                      preferred_element_type=F32)
                gk.append((g_k[0:_KEYS], g_k[_KEYS:2 * _KEYS]))
                gv.append((g_v[0:_KEYS], g_v[_KEYS:2 * _KEYS]))
            lo = lax.broadcasted_iota(jnp.int32, (2 * BLK, 128), 1) < HEAD
            zero = jnp.zeros((2 * BLK, 128), F32)

            def unpad(g):
                return (jnp.where(lo, g[0][0] + pltpu.roll(g[0][1], HEAD, 1), zero)
                        + jnp.where(lo, zero, pltpu.roll(g[1][0], HEAD, 1) + g[1][1]))

            dk = unpad(gk) * SCALE
            dv = unpad(gv)
            dkv_ref[:, 0:128] = (carry_ref[:, 0:128] + dk[0:BLK]).astype(BF)
            dkv_ref[:, 128:256] = (carry_ref[:, 128:256] + dv[0:BLK]).astype(BF)
            carry_ref[:, 0:128] = dk[BLK:2 * BLK]
            carry_ref[:, 128:256] = dv[BLK:2 * BLK]

        @pl.when(i == nb)
        def _():
            dkv_ref[...] = carry_ref[...].astype(BF)

    return _hosted_call(
        body, comm, (sinks, q, kv, kv, do), name="attn_bwd", grid=(nb + 1,),
        in_specs=[pl.BlockSpec(memory_space=pltpu.SMEM),
                  pl.BlockSpec((BLK, D), lambda i: (jnp.minimum(i, last), 0)),
                  pl.BlockSpec((BLK, 256), lambda i: (jnp.clip(i - 1, 0, last), 0)),
                  pl.BlockSpec((BLK, 256), lambda i: (jnp.minimum(i, last), 0)),
                  pl.BlockSpec((BLK, D), lambda i: (jnp.minimum(i, last), 0))],
        out_specs=[pl.BlockSpec((BLK, D), lambda i: (jnp.minimum(i, last), 0)),
                   pl.BlockSpec((BLK, 256), lambda i: (jnp.maximum(i - 1, 0), 0)),
                   pl.BlockSpec((1, 128), lambda i: (0, 0))],
        out_shape=[jax.ShapeDtypeStruct((t, D), BF), jax.ShapeDtypeStruct((t, 256), BF),
                   jax.ShapeDtypeStruct((1, 128), F32)],
        scratch_shapes=[pltpu.VMEM((BLK, 256), F32)], sem=("arbitrary",), nsteps=nb + 1,
        step_fn=lambda: pl.program_id(0))


def _split3(v):
    h = v.astype(BF)
    r = v - h.astype(F32)
    m = r.astype(BF)
    lo = (r - m.astype(F32)).astype(BF)
    return jnp.concatenate([h, m, lo], axis=1)


def _apply01(mat, v):
    n = v.shape[1]
    r = jnp.dot(mat, _split3(v), preferred_element_type=F32)
    return r[:, 0:n] + r[:, n:2 * n] + r[:, 2 * n:3 * n]


def _hgrn_gates(hq, hf, lb):
    sq = _sig(hq)
    sg = _sig(hf)
    f = lb + (1.0 - lb) * sg
    return hq * sq, (1.0 - lb) * (1.0 - sg), jnp.log(f), sq, sg, f


def _tri(upper):
    r = lax.broadcasted_iota(jnp.int32, (CH, CH), 0)
    c = lax.broadcasted_iota(jnp.int32, (CH, CH), 1)
    return (c >= r) if upper else (c <= r)


def _lb_from_logits(lg_ref):
    return 1.0 / (1.0 + jnp.exp(lg_ref[1:2, :] - lg_ref[0:1, :]))


def _hgrn_fwd(h3, hf, logits, norm_g, *, t, comm=None):
    nc = t // CH
    nt_dims = (((1,), (1,)), ((), ()))
    tn_dims = (((0,), (0,)), ((), ()))

    def body(h_ref, hf_ref, lg_ref, ng_ref, y_ref, o_ref, st_ref, s_scr, b_scr, qa_s, ka_s, qb_s, kb_s, v_s):
        @pl.when(pl.program_id(0) == 0)
        def _():
            s_scr[...] = jnp.zeros_like(s_scr)

        heads = [slice(h * HG_K, (h + 1) * HG_K) for h in range(HG_HEADS)]
        causal = _tri(False)
        lb = _lb_from_logits(lg_ref)
        for c in range(HG_SUB):
            rows = slice(c * CH, (c + 1) * CH)
            q, k, g, _, _, _ = _hgrn_gates(h_ref[rows, 0:D].astype(F32), hf_ref[rows, :], lb)
            b_scr[...] = _apply01(jnp.where(causal, 1.0, 0.0).astype(BF), g)
            b = b_scr[...]
            b_mid = b_scr[CH // 2 - 1:CH // 2, :]
            b_last = b_scr[CH - 1:CH, :]
            qa_s[...] = (q * jnp.exp(b - b_mid)).astype(BF)
            ka_s[...] = (k * jnp.exp(b_mid - b)).astype(BF)
            qb_s[...] = (q * jnp.exp(b)).astype(BF)
            kb_s[...] = (k * jnp.exp(b_last - b)).astype(BF)
            v_s[...] = h_ref[rows, D:2 * D]
            dec = jnp.exp(b_last)
            st_ref[c] = s_scr[...].astype(BF)
            a = [jnp.where(causal, lax.dot_general(qa_s[:, sl], ka_s[:, sl], nt_dims, preferred_element_type=F32),
                           0.0).astype(BF) for sl in heads]
            for h, sl in enumerate(heads):
                o_ref[rows, sl] = (jnp.dot(a[h], v_s[:, sl], preferred_element_type=F32)
                                   + lax.dot_general(qb_s[:, sl], s_scr[h].astype(BF), nt_dims,
                                                     preferred_element_type=F32))
            for h, sl in enumerate(heads):
                s_scr[h] = dec[:, sl] * s_scr[h] + lax.dot_general(v_s[:, sl], kb_s[:, sl], tn_dims,
                                                                   preferred_element_type=F32)
            for h, sl in enumerate(heads):
                o = o_ref[rows, sl]
                on = o * lax.rsqrt(jnp.mean(o * o, axis=-1, keepdims=True) + EPS)
                gate = _sig(h_ref[rows, 2 * D + h * HG_K:2 * D + (h + 1) * HG_K].astype(F32))
                y_ref[rows, sl] = (on * ng_ref[:, sl] * gate).astype(BF)

    half = lambda: pltpu.VMEM((CH, D), BF)
    blk = HG_SUB * CH
    return _hosted_call(
        body, comm, (h3, hf, logits, norm_g), name="hgrn_fwd", grid=(nc // HG_SUB,),
        in_specs=[pl.BlockSpec((blk, 3 * D), lambda n: (n, 0)),
                  pl.BlockSpec((blk, D), lambda n: (n, 0)),
                  pl.BlockSpec((2, D), lambda n: (0, 0)),
                  pl.BlockSpec((1, D), lambda n: (0, 0))],
        out_specs=[pl.BlockSpec((blk, D), lambda n: (n, 0)),
                   pl.BlockSpec((blk, D), lambda n: (n, 0)),
                   pl.BlockSpec((HG_SUB, HG_HEADS, HG_K, HG_K), lambda n: (n, 0, 0, 0))],
        out_shape=[jax.ShapeDtypeStruct((t, D), BF), jax.ShapeDtypeStruct((t, D), F32),
                   jax.ShapeDtypeStruct((nc, HG_HEADS, HG_K, HG_K), BF)],
        scratch_shapes=[pltpu.VMEM((HG_HEADS, HG_K, HG_K), F32), pltpu.VMEM((CH, D), F32),
                        half(), half(), half(), half(), half()],
        sem=("arbitrary",), nsteps=nc // HG_SUB, step_fn=lambda: pl.program_id(0))


def _hgrn_bwd(h3, hf, logits, norm_g, o_pre, states, dy, *, t, comm=None):
    nc = t // CH
    nt_dims = (((1,), (1,)), ((), ()))
    tn_dims = (((0,), (0,)), ((), ()))

    def body(h_ref, hf_ref, lg_ref, ng_ref, o_ref, st_ref, dy_ref, dh_ref, dlg_ref, dng_ref, ds_scr, dlb_scr,
             b_scr, tail_s, e_qa, e_ka, e_qb, e_kb, q_s, k_s, dqa_s, dka_s, dqb_s, dkb_s,
             qa_s, ka_s, qb_s, kb_s, v_s, do_s):
        n = pl.program_id(0)

        @pl.when(n == 0)
        def _():
            ds_scr[...] = jnp.zeros_like(ds_scr)
            dlb_scr[...] = jnp.zeros_like(dlb_scr)
            dng_ref[...] = jnp.zeros_like(dng_ref)

        heads = [slice(h * HG_K, (h + 1) * HG_K) for h in range(HG_HEADS)]
        lb = _lb_from_logits(lg_ref)
        causal = _tri(False)

        def chunk(c):
            rows = slice(c * CH, (c + 1) * CH)
            hq = h_ref[rows, 0:D].astype(F32)
            q, k, g, sq, sg, f = _hgrn_gates(hq, hf_ref[rows, :], lb)
            b_scr[...] = _apply01(jnp.where(causal, 1.0, 0.0).astype(BF), g)
            b = b_scr[...]
            b_mid = b_scr[CH // 2 - 1:CH // 2, :]
            b_last = b_scr[CH - 1:CH, :]
            q_s[...] = q
            k_s[...] = k
            for e_ref, s_ref, base, expo in ((e_qa, qa_s, q, b - b_mid), (e_ka, ka_s, k, b_mid - b),
                                             (e_qb, qb_s, q, b), (e_kb, kb_s, k, b_last - b)):
                e = jnp.exp(expo)
                e_ref[...] = e
                s_ref[...] = (base * e).astype(BF)
            v_s[...] = h_ref[rows, D:2 * D]
            dec = jnp.exp(b_last)
            for h, sl in enumerate(heads):
                gcol = slice(3 * D + h * HG_K, 3 * D + (h + 1) * HG_K)
                ngh = ng_ref[:, sl]
                sgate = _sig(h_ref[rows, 2 * D + h * HG_K:2 * D + (h + 1) * HG_K].astype(F32))
                o = o_ref[rows, sl]
                r = lax.rsqrt(jnp.mean(o * o, axis=-1, keepdims=True) + EPS)
                on = o * r
                dyh = dy_ref[rows, sl]
                dh_ref[rows, gcol] = (dyh * on * ngh * sgate * (1.0 - sgate)).astype(BF)
                dng_ref[:, sl] += jnp.sum(dyh * on * sgate, axis=0, keepdims=True)
                don = dyh * ngh * sgate
                do_s[:, sl] = (r * (don - on * jnp.mean(don * on, axis=-1, keepdims=True))).astype(BF)
            a = [jnp.where(causal, lax.dot_general(qa_s[:, sl], ka_s[:, sl], nt_dims, preferred_element_type=F32),
                           0.0).astype(BF) for sl in heads]
            da = [jnp.where(causal, lax.dot_general(do_s[:, sl], v_s[:, sl], nt_dims, preferred_element_type=F32),
                            0.0).astype(BF) for sl in heads]
            for h, sl in enumerate(heads):
                dh_ref[rows, 2 * D + h * HG_K:2 * D + (h + 1) * HG_K] = (
                    lax.dot_general(a[h], do_s[:, sl], tn_dims, preferred_element_type=F32)
                    + lax.dot_general(kb_s[:, sl], ds_scr[h].astype(BF), nt_dims, preferred_element_type=F32)
                ).astype(BF)
            for h, sl in enumerate(heads):
                dqa_s[:, sl] = jnp.dot(da[h], ka_s[:, sl], preferred_element_type=F32)
            for h, sl in enumerate(heads):
                dka_s[:, sl] = lax.dot_general(da[h], qa_s[:, sl], tn_dims, preferred_element_type=F32)
            for h, sl in enumerate(heads):
                dqb_s[:, sl] = jnp.dot(do_s[:, sl], st_ref[c, h], preferred_element_type=F32)
            for h, sl in enumerate(heads):
                dkb_s[:, sl] = jnp.dot(v_s[:, sl], ds_scr[h].astype(BF), preferred_element_type=F32)
            for h, sl in enumerate(heads):
                tail_s[:, sl] = jnp.sum(dec[:, sl] * st_ref[c, h].astype(F32) * ds_scr[h], axis=0, keepdims=True)
            for h, sl in enumerate(heads):
                ds_scr[h] = (lax.dot_general(do_s[:, sl], qb_s[:, sl], tn_dims, preferred_element_type=F32)
                             + dec[:, sl] * ds_scr[h])
            qv, kv = q_s[...], k_s[...]
            dqa, dka, dqb, dkb = dqa_s[...], dka_s[...], dqb_s[...], dkb_s[...]
            eqa, eka, eqb, ekb = e_qa[...], e_ka[...], e_qb[...], e_kb[...]
            dkb_kb = dkb * (kv * ekb)
            db_last = jnp.sum(dkb_kb, axis=0, keepdims=True) + tail_s[...]
            last_row = lax.broadcasted_iota(jnp.int32, (CH, D), 0) == CH - 1
            db = (dqa * (qv * eqa) - dka * (kv * eka) + dqb * (qv * eqb) - dkb_kb
                  + jnp.where(last_row, db_last, 0.0))
            dg = _apply01(jnp.where(_tri(True), 1.0, 0.0).astype(BF), db)
            dq = dqa * eqa + dqb * eqb
            dk = dka * eka + dkb * ekb
            dh_ref[rows, 0:D] = (dq * sq * (1.0 + hq * (1.0 - sq))).astype(BF)
            dfk = dg / f - dk
            dh_ref[rows, D:2 * D] = ((1.0 - lb) * dfk * sg * (1.0 - sg)).astype(BF)
            dlb_scr[...] += jnp.sum((1.0 - sg) * dfk, axis=0, keepdims=True)

        for c in reversed(range(HG_SUB)):
            chunk(c)

        @pl.when(n == nc // HG_SUB - 1)
        def _():
            dl0 = dlb_scr[...] * lb * (1.0 - lb)
            dlg_ref[0:1, :] = dl0
            dlg_ref[1:2, :] = -dl0

    steps = nc // HG_SUB
    blk = HG_SUB * CH
    rev = lambda n: (steps - 1 - n, 0)
    return _hosted_call(
        body, comm, (h3, hf, logits, norm_g, o_pre, states, dy), name="hgrn_bwd", grid=(steps,),
        in_specs=[pl.BlockSpec((blk, 3 * D), rev),
                  pl.BlockSpec((blk, D), rev),
                  pl.BlockSpec((2, D), lambda n: (0, 0)),
                  pl.BlockSpec((1, D), lambda n: (0, 0)),
                  pl.BlockSpec((blk, D), rev),
                  pl.BlockSpec((HG_SUB, HG_HEADS, HG_K, HG_K), lambda n: (steps - 1 - n, 0, 0, 0)),
                  pl.BlockSpec((blk, D), rev)],
        out_specs=[pl.BlockSpec((blk, 4 * D), rev),
                   pl.BlockSpec((2, D), lambda n: (0, 0)),
                   pl.BlockSpec((1, D), lambda n: (0, 0))],
        out_shape=[jax.ShapeDtypeStruct((t, 4 * D), BF), jax.ShapeDtypeStruct((2, D), F32),
                   jax.ShapeDtypeStruct((1, D), F32)],
        scratch_shapes=([pltpu.VMEM((HG_HEADS, HG_K, HG_K), F32), pltpu.VMEM((1, D), F32),
                         pltpu.VMEM((CH, D), F32), pltpu.VMEM((1, D), F32)]
                        + [pltpu.VMEM((CH, D), F32)] * 10 + [pltpu.VMEM((CH, D), BF)] * 6),
        sem=("arbitrary",), nsteps=steps, step_fn=lambda: pl.program_id(0))


def _place():
    x, y, c = lax.axis_index("x"), lax.axis_index("y"), lax.axis_index("c")
    return x, y, c, [(1 - x, y), (x, 1 - y), (1 - x, 1 - y)]


def _gather_comm(shards, mids):
    n, pieces = len(shards), len(mids)
    r = [s.shape[0] for s in shards]
    tile = 16
    cut = [[(rw // tile * p // pieces) * tile for p in range(pieces + 1)] for rw in r]
    size = [[cut[w][p + 1] - cut[w][p] for p in range(pieces)] for w in range(n)]

    def tools(ins, outs, sems):
        send_sems, recv_sems, local_sems = sems
        x, y, c, _ = _place()
        me, sib = (x, y, c), (x, y, 1 - c)
        near = [(x ^ c, y ^ (1 - c), c), (x ^ (1 - c), y ^ c, c), (1 - x, 1 - y, c)]

        def rows(w, p, dev):
            return outs[w].at[pl.ds((4 * dev[0] + 2 * dev[1] + dev[2]) * r[w] + cut[w][p], size[w][p]), :]

        def copy(kind, w, p, block, to, own=False):
            src = ins[w].at[pl.ds(cut[w][p], size[w][p]), :] if own else rows(w, p, block)
            return pltpu.make_async_remote_copy(
                src_ref=src, dst_ref=rows(w, p, block), send_sem=send_sems.at[p, kind],
                recv_sem=recv_sems.at[p, kind], device_id=to, device_id_type=MESH)

        def all_of(kind, p):
            whole = outs[0].at[pl.ds(0, sum(size[w][p] for w in range(n))), :]
            return pltpu.make_async_remote_copy(
                src_ref=whole, dst_ref=whole, send_sem=send_sems.at[p, kind], recv_sem=recv_sems.at[p, kind],
                device_id=me, device_id_type=MESH)

        mine = [pltpu.make_async_copy(ins[w], outs[w].at[pl.ds((4 * x + 2 * y + c) * r[w], r[w]), :],
                                      local_sems.at[w]) for w in range(n)]
        return near, me, sib, copy, all_of, mine

    def start(ins, outs, sems):
        near, me, sib, copy, _, mine = tools(ins, outs, sems)
        for cp in mine:
            cp.start()
        for p in range(pieces):
            for w in range(n):
                copy(0, w, p, me, sib, own=True).start()
                copy(1, w, p, me, near[0], own=True).start()
                copy(2, w, p, me, near[1], own=True).start()

    def pass_diagonal(p, near, sib, copy, all_of):
        all_of(3, p).wait_recv()
        for w in range(n):
            copy(6, w, p, near[2], sib).start()

    def pass_on(p):
        def phase(ins, outs, sems):
            near, _, sib, copy, all_of, _ = tools(ins, outs, sems)
            all_of(1, p).wait_recv()
            for w in range(n):
                copy(3, w, p, near[0], near[1]).start()
                copy(4, w, p, near[0], sib).start()
            all_of(2, p).wait_recv()
            for w in range(n):
                copy(5, w, p, near[1], sib).start()
            if p > 0:
                pass_diagonal(p - 1, near, sib, copy, all_of)
        return phase

    def finish(ins, outs, sems):
        near, _, sib, copy, all_of, mine = tools(ins, outs, sems)
        pass_diagonal(pieces - 1, near, sib, copy, all_of)
        for p in range(pieces):
            all_of(0, p).wait_recv()
            for kind in (4, 5, 6):
                all_of(kind, p).wait_recv()
            for kind in range(7):
                all_of(kind, p).wait_send()
        for cp in mine:
            cp.wait()

    return _Comm(shards, [jax.ShapeDtypeStruct((N_DEV * rw, D), BF) for rw in r],
                 [pltpu.SemaphoreType.DMA((pieces, 7)), pltpu.SemaphoreType.DMA((pieces, 7)),
                  pltpu.SemaphoreType.DMA((n,))],
                 [(0.0, start)] + [(f, pass_on(p)) for p, f in enumerate(mids)] + [(1.0, finish)])


def _pair_comm(grads):
    n = len(grads)
    r = [g.shape[0] // N_DEV for g in grads]

    def start(ins, outs, sems):
        send_sems, recv_sems = sems
        x, y, c, _ = _place()
        for w in range(n):
            for a in range(N_CHIP):
                pltpu.make_async_remote_copy(
                    src_ref=ins[w].at[pl.ds((2 * a + 1 - c) * r[w], r[w]), :], dst_ref=outs[w].at[a],
                    send_sem=send_sems.at[w], recv_sem=recv_sems.at[w],
                    device_id=(x, y, 1 - c), device_id_type=MESH).start()

    def finish(ins, outs, sems):
        send_sems, recv_sems = sems
        x, y, c, _ = _place()
        for w in range(n):
            pltpu.make_async_remote_copy(
                src_ref=outs[w], dst_ref=outs[w], send_sem=send_sems.at[w], recv_sem=recv_sems.at[w],
                device_id=(x, y, c), device_id_type=MESH).wait()

    return _Comm(grads, [jax.ShapeDtypeStruct((N_CHIP, rw, D), BF) for rw in r],
                 [pltpu.SemaphoreType.DMA((n,)), pltpu.SemaphoreType.DMA((n,))],
                 [(0.0, start), (1.0, finish)])


def _pair_add(grads, gots, core, *, name):
    n, r = len(grads), gots[0].shape[1]
    chip = lambda k: (lambda s: jnp.clip(s - k * N_CHIP, 0, N_CHIP - 1))

    def body(c_ref, *refs):
        g_refs, got_refs, o_refs = refs[:n], refs[n:2 * n], refs[2 * n:]
        s = pl.program_id(0)
        for k in range(n):
            @pl.when(jnp.logical_and(s >= k * N_CHIP, s < (k + 1) * N_CHIP))
            def _(k=k):
                o_refs[k][0] = (g_refs[k][...].astype(F32) + got_refs[k][0].astype(F32)).astype(BF)

    grid_spec = pltpu.PrefetchScalarGridSpec(
        num_scalar_prefetch=1, grid=(n * N_CHIP,),
        in_specs=[pl.BlockSpec((r, D), lambda s, c_ref, k=k: (2 * chip(k)(s) + c_ref[0], 0)) for k in range(n)]
        + [pl.BlockSpec((1, r, D), lambda s, c_ref, k=k: (chip(k)(s), 0, 0)) for k in range(n)],
        out_specs=[pl.BlockSpec((1, r, D), lambda s, c_ref, k=k: (chip(k)(s), 0, 0)) for k in range(n)])
    return _pcall(body, name=name, grid_spec=grid_spec,
                  out_shape=[jax.ShapeDtypeStruct((N_CHIP, r, D), BF)] * n,
                  compiler_params=_cp(("arbitrary",)))(core, *grads, *gots)


def _chip_comm(pair_sums):
    n = len(pair_sums)
    r = [p.shape[1] for p in pair_sums]
    off = [sum(r[:w]) for w in range(n)]

    def tools(ins, outs, sems):
        send_sems, recv_sems, local_sems = sems
        x, y, c, chips = _place()
        my_chip = 2 * x + y

        def slot(w):
            return outs[0].at[my_chip, pl.ds(off[w], r[w]), :]

        own = [pltpu.make_async_copy(ins[w].at[my_chip], slot(w), local_sems.at[w]) for w in range(n)]
        return x, y, c, chips, my_chip, slot, own, send_sems, recv_sems

    def start(ins, outs, sems):
        x, y, c, chips, my_chip, slot, own, send_sems, recv_sems = tools(ins, outs, sems)
        for cp in own:
            cp.start()
        for j, chip in enumerate(chips):
            for w in range(n):
                pltpu.make_async_remote_copy(
                    src_ref=ins[w].at[2 * chip[0] + chip[1]], dst_ref=slot(w), send_sem=send_sems.at[j],
                    recv_sem=recv_sems.at[j], device_id=(*chip, c), device_id_type=MESH).start()

    def finish(ins, outs, sems):
        x, y, c, chips, my_chip, slot, own, send_sems, recv_sems = tools(ins, outs, sems)
        whole = outs[0].at[my_chip]
        for j in range(3):
            pltpu.make_async_remote_copy(
                src_ref=whole, dst_ref=whole, send_sem=send_sems.at[j], recv_sem=recv_sems.at[j],
                device_id=(x, y, c), device_id_type=MESH).wait()
        for cp in own:
            cp.wait()

    return _Comm(pair_sums, [jax.ShapeDtypeStruct((N_CHIP, sum(r), D), BF)],
                 [pltpu.SemaphoreType.DMA((3,)), pltpu.SemaphoreType.DMA((3,)), pltpu.SemaphoreType.DMA((n,))],
                 [(0.0, start), (1.0, finish)])


def _adam_math(w, g, m, v):
    m = ADAM_B1 * m + (1.0 - ADAM_B1) * g
    v = ADAM_B2 * v + (1.0 - ADAM_B2) * (g * g)
    m_hat = m / (1.0 - ADAM_B1 ** ADAM_STEP)
    v_hat = v / (1.0 - ADAM_B2 ** ADAM_STEP)
    delta = -ADAM_LR * (m_hat / (jnp.sqrt(v_hat) + ADAM_EPS) + ADAM_WD * w)
    return delta, m, v


SMALL = (("norm_mix_g", (1, D), 0), ("hgrn_norm_g", (1, D), 1), ("norm_ffn_g", (1, D), 2),
         ("norm_final_g", (1, D), 3), ("hgrn_lb_logits", (2, D), 4), ("attn_sinks", (1, 16), 6),
         ("b_in", (1, IN_W), 8))
LOSS_ROW = 7


def _small_allreduce_adam(grads, loss_row, params):
    n = len(SMALL)

    def rows_of(ref, shape, row):
        r, w = shape
        if w <= D:
            return ref[row:row + r, 0:w]
        pieces = [ref[row + k:row + k + 1, :] for k in range(-(-w // D))]
        return jnp.concatenate(pieces, axis=1)[:, 0:w]

    def body(*refs):
        g_refs, loss_ref = refs[:n], refs[n]
        wmv = refs[n + 1:4 * n + 1]
        loss_out = refs[4 * n + 1]
        outs = refs[4 * n + 2:8 * n + 2]
        mine, total, gath, send_sems, recv_sems = refs[8 * n + 2:]
        x, y, c, _ = _place()
        me = 4 * x + 2 * y + c
        mine[...] = jnp.zeros_like(mine)
        for g_ref, (_, (r, w), row) in zip(g_refs, SMALL):
            for k in range(-(-w // D)):
                wk = min(D, w - k * D)
                mine[row + k:row + k + r, 0:wk] = g_ref[:, k * D:k * D + wk]
        mine[LOSS_ROW:LOSS_ROW + 1, 0:128] = loss_ref[...]
        gath[me] = mine[...]
        cps = []
        for d in range(1, N_DEV):
            peer = (x ^ (d >> 2), y ^ ((d >> 1) & 1), c ^ (d & 1))
            cps.append(pltpu.make_async_remote_copy(
                src_ref=mine, dst_ref=gath.at[me], send_sem=send_sems.at[d - 1],
                recv_sem=recv_sems.at[d - 1], device_id=peer, device_id_type=MESH))
        for cp in cps:
            cp.start()
        for cp in cps:
            cp.wait()
        g = gath[0]
        for k in range(1, N_DEV):
            g = g + gath[k]
        total[...] = g
        loss_out[...] = total[LOSS_ROW:LOSS_ROW + 1, 0:128]
        for i, (_, shape, row) in enumerate(SMALL):
            gi = rows_of(total, shape, row)
            w_ref, m_ref, v_ref = wmv[3 * i:3 * i + 3]
            o = outs[4 * i:4 * i + 4]
            o[0][...] = gi
            o[1][...], o[2][...], o[3][...] = _adam_math(w_ref[...], gi, m_ref[...], v_ref[...])

    vm = pl.BlockSpec(memory_space=pltpu.VMEM)
    ins = [grads[name] for name, _, _ in SMALL] + [loss_row]
    for name, _, _ in SMALL:
        ins += list(params[name])
    out_shape = [jax.ShapeDtypeStruct((1, 128), F32)]
    for _, shape, _ in SMALL:
        out_shape += [jax.ShapeDtypeStruct(shape, F32)] * 4
    res = _pcall(body, name="small_allreduce_adam", in_specs=[vm] * len(ins), out_specs=[vm] * len(out_shape),
                 out_shape=out_shape,
                 scratch_shapes=[pltpu.VMEM((SMALL_ROWS, D), F32), pltpu.VMEM((SMALL_ROWS, D), F32),
                                 pltpu.VMEM((N_DEV, SMALL_ROWS, D), F32),
                                 pltpu.SemaphoreType.DMA((N_DEV - 1,)), pltpu.SemaphoreType.DMA((N_DEV - 1,))],
                 compiler_params=pltpu.CompilerParams(has_side_effects=True))(*ins)
    return res[0], {name: res[1 + 4 * i:5 + 4 * i] for i, (name, _, _) in enumerate(SMALL)}


def _adam(ws, parts, ms, vs, *, name):
    n, rows = len(ws), ws[0].shape[0]
    tr = rows if rows <= 128 else rows // 2
    steps = rows // tr
    tile = lambda k: (lambda s: jnp.clip(s - k * steps, 0, steps - 1))

    def body(*refs):
        w_refs, m_refs, v_refs, p_ref, o_refs = refs[:n], refs[n:2 * n], refs[2 * n:3 * n], refs[3 * n], refs[3 * n + 1:]
        s = pl.program_id(0)
        for k in range(n):
            @pl.when(jnp.logical_and(s >= k * steps, s < (k + 1) * steps))
            def _(k=k):
                g = p_ref[0].astype(F32)
                for a in range(1, N_CHIP):
                    g = g + p_ref[a].astype(F32)
                o = o_refs[4 * k:4 * k + 4]
                o[0][...] = g
                o[1][...], o[2][...], o[3][...] = _adam_math(w_refs[k][...], g, m_refs[k][...], v_refs[k][...])

    spec = lambda k: pl.BlockSpec((tr, D), lambda s, k=k: (tile(k)(s), 0))
    res = _pcall(body, name=name, grid=(n * steps,),
                 in_specs=[spec(k) for k in range(n)] * 3 + [pl.BlockSpec((N_CHIP, tr, D), lambda s: (0, s, 0))],
                 out_specs=[spec(k) for k in range(n) for _ in range(4)],
                 out_shape=[jax.ShapeDtypeStruct((rows, D), F32)] * (4 * n),
                 compiler_params=_cp(("arbitrary",)))(*ws, *ms, *vs, parts)
    return [res[4 * k:4 * k + 4] for k in range(n)]


def _step(x, tgt, shards, norm_mix_g, b_in, sinks, logits, hgrn_norm_g, norm_ffn_g, norm_final_g):
    t = x.shape[0]
    core = lax.axis_index("c").astype(jnp.int32).reshape(1)

    u1, (win_t,) = _rms_fwd(x, norm_mix_g, tm=512, name="rms_mix", comm=_gather_comm(shards[0:1], (0.2, 0.4, 0.6, 0.8)))
    (q, kv, h3, hf, gates), (wg_t, wba, wbh, wout) = _inproj_fwd(
        u1, win_t, b_in, t=t, comm=_gather_comm([shards[1]] + shards[4:7], (0.3, 0.5, 0.7, 0.9)))
    (y_attn,), _ = _attn_fwd(q, kv, sinks, t=t)
    (y_hgrn, o_pre, states), (wu_t, wd) = _hgrn_fwd(h3, hf, logits, hgrn_norm_g, t=t,
                                                    comm=_gather_comm(shards[2:4], (0.3, 0.5, 0.7, 0.9)))
    col = lambda j: j
    first, second = (lambda j: 0), (lambda j: 1)
    gate_tiles = [(gates, D, first), (gates, D, second)]

    def merge(prods, ex):
        (ya_, yb_), (ga, gb) = prods, ex
        sa, sb = _sig(ga.astype(F32)), _sig(gb.astype(F32))
        return sa, sb, ya_ * sa * (1.0 - sa), yb_ * sb * (1.0 - sb), sa * ya_ + sb * yb_

    sig_a, sig_b, dgate_a, dgate_b, merged = _fmm(
        [y_attn, y_hgrn], [(0, wba, False), (1, wbh, False)], gate_tiles, merge,
        [(BF, D, D, first)] * 5, m=t, n=D, tm=512, tn=D, name="branch_merge")
    def resid_norm(prods, ex):
        (p,), (xv, gv) = prods, ex
        hv = xv + p
        return hv, hv * lax.rsqrt(jnp.mean(hv * hv, axis=-1, keepdims=True) + EPS) * gv

    h1, u2 = _fmm([merged], [(0, wout, False)], [(x, D, first)], resid_norm, [(F32, D, D, first), (BF, D, D, first)],
                  m=t, n=D, tm=1024, tn=D, name="out_proj", vecs=[norm_ffn_g])

    def swiglu(prods, ex):
        g_, u_ = prods
        s = _sig(g_)
        silu = g_ * s
        return u_ * s * (1.0 + g_ * (1.0 - s)), silu, silu * u_

    dz_dgate, dz_dup, z = _fmm([u2], [(0, wg_t, True), (0, wu_t, True)], [], swiglu,
                               [(BF, FFN, FFN // 2, col)] * 3, m=t, n=FFN, tm=1024, tn=FFN // 2,
                               name="ffn_gate_up")
    def loss_head(prods, ex):
        (p,), (hv, tv, gv) = prods, ex
        hv = hv + p
        r = lax.rsqrt(jnp.mean(hv * hv, axis=-1, keepdims=True) + EPS)
        xh = hv * r
        err = xh * gv - tv
        lp = jnp.sum(jnp.sum(err * err, axis=1, keepdims=True), axis=0, keepdims=True) * (0.5 / D)
        dy = err * (1.0 / D)
        dxh = dy * gv
        dh = r * (dxh - xh * jnp.mean(dxh * xh, axis=-1, keepdims=True))
        return dh, dh, jnp.sum(dy * xh, axis=0, keepdims=True), jnp.broadcast_to(lp, (1, 128))

    dh2, dh2_b, d_norm_final, loss_row = _fmm(
        [z], [(0, wd, False)], [(h1, D, first), (tgt, D, first)], loss_head, [(F32, D, D, first), (BF, D, D, first)],
        m=t, n=D, tm=512, tn=D, name="ffn_down_loss", vecs=[norm_final_g], sums=[D, 128])

    def swiglu_bwd(prods, ex):
        (dz,), (da_, db_) = prods, ex
        return dz * da_.astype(F32), dz * db_.astype(F32)

    ffn_tiles = [(dz_dgate, FFN // 2, col), (dz_dup, FFN // 2, col)]
    dgt, dup = _fmm([dh2_b], [(0, wd, True)], ffn_tiles, swiglu_bwd, [(BF, FFN, FFN // 2, col)] * 2,
                    m=t, n=FFN, tm=1024, tn=FFN // 2, name="d_gate_up")
    (d_wd,) = _wgrad([z], dh2_b, name="d_w_down")
    (du2,) = _fmm([dgt, dup], [(0, wg_t, False), (1, wu_t, False)], [], lambda prods, ex: (prods[0] + prods[1],),
                  [(F32, D, 512, col)], m=t, n=D, tm=1024, tn=512, name="d_u2")
    d_wg, d_wu = _wgrad([dgt, dup], u2, name="d_w_gate_up")
    dh1, dh1_b, d_norm_ffn = _rms_bwd(du2, h1, norm_ffn_g, dh2, tm=512, name="rms_ffn_bwd")
    (d_wout,) = _wgrad([merged], dh1_b, name="d_w_out")

    def merge_bwd(prods, ex):
        (dm,), (sa, sb, ca, cb, wa, wb) = prods, ex
        dgate = jnp.concatenate([dm * ca.astype(F32), dm * cb.astype(F32)], axis=1)
        dya_ = (dm * sa.astype(F32)).astype(BF)
        dyb_ = (dm * sb.astype(F32)).astype(BF)
        return (dya_, dyb_, dgate, lax.dot_general(dya_, wa, _NT, preferred_element_type=F32),
                lax.dot_general(dyb_, wb, _NT, preferred_element_type=F32))

    ffn_grads = (d_wg, d_wu, d_wd)
    (dya, dyb, dgates, dy_attn, dy_hgrn), got = _fmm(
        [dh1_b], [(0, wout, True)], [(a, D, first) for a in (sig_a, sig_b, dgate_a, dgate_b)], merge_bwd,
        [(BF, D, D, first), (BF, D, D, first), (BF, 2 * D, 2 * D, first), (BF, D, D, first), (F32, D, D, first)],
        m=t, n=D, tm=512, tn=D, name="d_merge", consts=[wba, wbh], comm=_pair_comm(ffn_grads))
    pair_ffn = _pair_add(ffn_grads, got, core, name="pair_add_ffn")
    (d_wba,) = _wgrad([y_attn], dya, name="d_w_ba")
    (d_wbh,) = _wgrad([y_hgrn], dyb, name="d_w_bh")
    sq_grads = (d_wba, d_wbh, d_wout)
    (dh4, d_logits, d_hgrn_norm), (parts_ffn, *got) = _hgrn_bwd(
        h3, hf, logits, hgrn_norm_g, o_pre, states, dy_hgrn, t=t,
        comm=_both(_chip_comm(pair_ffn), _pair_comm(sq_grads)))
    pair_sq = _pair_add(sq_grads, got, core, name="pair_add_sq")
    (dq, dkv, d_sinks), (parts_sq,) = _attn_bwd(q, kv, sinks, dy_attn, t=t, comm=_chip_comm(pair_sq))
    dps = (dq, dkv, dh4, dgates)
    d_win_t, d_b_in = _inproj_bwd_w(dps, u1, t=t)
    half0, got_in = _inproj_bwd_x(dps, win_t, x, norm_mix_g, dh1, t=t, part=0, comm=_pair_comm([d_win_t]))
    pair_in = _pair_add([d_win_t], got_in, core, name="pair_add_w_in")
    (grad_x, d_norm_mix), (parts_in,) = _inproj_bwd_x(dps, win_t, x, norm_mix_g, dh1, t=t, part=1, prev=half0,
                                                      comm=_chip_comm(pair_in))

    small_grads = (d_norm_mix, d_b_in, d_sinks, d_logits, d_hgrn_norm, d_norm_ffn, d_norm_final)
    return loss_row, grad_x, (parts_in, parts_ffn, parts_sq), small_grads


def kernel(x, norm_mix_g, w_in, b_in, attn_sinks, hgrn_lb_logits, hgrn_norm_g, w_branch_attn, w_branch_hgrn, w_out, norm_ffn_g, w_ffn_gate, w_ffn_up, w_ffn_down, norm_final_g, loss_target, m_norm_mix_g, m_w_in, m_b_in, m_attn_sinks, m_hgrn_lb_logits, m_hgrn_norm_g, m_w_branch_attn, m_w_branch_hgrn, m_w_out, m_norm_ffn_g, m_w_ffn_gate, m_w_ffn_up, m_w_ffn_down, m_norm_final_g, v_norm_mix_g, v_w_in, v_b_in, v_attn_sinks, v_hgrn_lb_logits, v_hgrn_norm_g, v_w_branch_attn, v_w_branch_hgrn, v_w_out, v_norm_ffn_g, v_w_ffn_gate, v_w_ffn_up, v_w_ffn_down, v_norm_final_g):
    shards = [w_in[0].T.astype(BF), w_ffn_gate[0].T.astype(BF), w_ffn_up[0].T.astype(BF),
              w_ffn_down[0].astype(BF), w_branch_attn[0].astype(BF), w_branch_hgrn[0].astype(BF),
              w_out[0].astype(BF)]
    loss_row, grad_x, grad_parts, small_grads = _step(
        x[0], loss_target[0], shards, norm_mix_g, b_in, attn_sinks, hgrn_lb_logits, hgrn_norm_g,
        norm_ffn_g, norm_final_g.reshape(1, D))

    d_norm_mix, d_b_in, d_sinks, d_logits, d_hgrn_norm, d_norm_ffn, d_norm_final = small_grads
    row = lambda a: a.reshape(1, D)
    loss_out, small = _small_allreduce_adam(
        dict(norm_mix_g=d_norm_mix, hgrn_norm_g=d_hgrn_norm, norm_ffn_g=d_norm_ffn, norm_final_g=d_norm_final,
             hgrn_lb_logits=d_logits, attn_sinks=d_sinks, b_in=d_b_in),
        loss_row,
        dict(norm_mix_g=(norm_mix_g, m_norm_mix_g, v_norm_mix_g), hgrn_norm_g=(hgrn_norm_g, m_hgrn_norm_g, v_hgrn_norm_g),
             norm_ffn_g=(norm_ffn_g, m_norm_ffn_g, v_norm_ffn_g),
             norm_final_g=(row(norm_final_g), row(m_norm_final_g), row(v_norm_final_g)),
             hgrn_lb_logits=(hgrn_lb_logits, m_hgrn_lb_logits, v_hgrn_lb_logits),
             attn_sinks=(attn_sinks, m_attn_sinks, v_attn_sinks), b_in=(b_in, m_b_in, v_b_in)))
    small["norm_final_g"] = [a.reshape(D) for a in small["norm_final_g"]]
    loss = loss_out[0, 0]

    names = ["w_in", "w_ffn_gate", "w_ffn_up", "w_ffn_down", "w_branch_attn", "w_branch_hgrn", "w_out"]
    w_full = dict(w_in=(w_in, m_w_in, v_w_in), w_ffn_gate=(w_ffn_gate, m_w_ffn_gate, v_w_ffn_gate),
                  w_ffn_up=(w_ffn_up, m_w_ffn_up, v_w_ffn_up), w_ffn_down=(w_ffn_down, m_w_ffn_down, v_w_ffn_down),
                  w_branch_attn=(w_branch_attn, m_w_branch_attn, v_w_branch_attn),
                  w_branch_hgrn=(w_branch_hgrn, m_w_branch_hgrn, v_w_branch_hgrn),
                  w_out=(w_out, m_w_out, v_w_out))
    big = {}
    for group, parts, tag in zip((names[0:1], names[1:4], names[4:7]), grad_parts, ("w_in", "ffn", "square")):
        flip = [name in names[0:3] for name in group]
        view = lambda a, f: a[0].T if f else a[0]
        cols = [[view(w_full[name][j], f) for name, f in zip(group, flip)] for j in range(3)]
        res = _adam(cols[0], parts, cols[1], cols[2], name="adam_" + tag)
        for name, f, r in zip(group, flip, res):
            big[name] = [a.T[None] if f else a[None] for a in r]

    order = ["norm_mix_g", "w_in", "b_in", "attn_sinks", "hgrn_lb_logits", "hgrn_norm_g", "w_branch_attn",
             "w_branch_hgrn", "w_out", "norm_ffn_g", "w_ffn_gate", "w_ffn_up", "w_ffn_down", "norm_final_g"]
    outs = [loss, grad_x[None]]
    for kind in range(4):
        for name in order:
            outs.append(big[name][kind] if name in big else small[name][kind])
    return tuple(outs)
```

```python
import math

import jax
import jax.numpy as jnp
from jax import lax
from jax.experimental import pallas as pl
from jax.experimental.pallas import tpu as pltpu

F32 = jnp.float32
BF = jnp.bfloat16
MESH = pl.DeviceIdType.MESH

D = 1024
HEAD = 64
N_PAIR = 8
BLK = 128
CH = 64
HG_SUB = 4
HG_HEADS = 8
HG_K = 128
FFN = 2816
IN_W = 7424
N_DEV = 8
N_CHIP = 4
EPS = 1e-6
NEG = -1e30
SCALE = 1.0 / math.sqrt(HEAD)
VMEM_LIMIT = 56 * 1024 * 1024
WT = 256

ADAM_LR, ADAM_B1, ADAM_B2, ADAM_EPS, ADAM_WD, ADAM_STEP = 0.001, 0.9, 0.999, 1e-08, 0.01, 10

GRP_OFF = (0, D // WT, (D + 256) // WT, (5 * D + 256) // WT)
GRP_N = (D // WT, 256 // WT, 4 * D // WT, 2 * D // WT)
SMALL_ROWS = 16


_NN = (((1,), (0,)), ((), ()))
_NT = (((1,), (1,)), ((), ()))
_TN = (((0,), (0,)), ((), ()))


def _pcall(body, **kw):
    return pl.pallas_call(body, **kw)


def _cp(sem=None, **kw):
    return pltpu.CompilerParams(dimension_semantics=sem, vmem_limit_bytes=VMEM_LIMIT, **kw)


def _sig(v):
    return 0.5 * jnp.tanh(0.5 * v) + 0.5


def _accum(ref, val, first):
    @pl.when(first)
    def _():
        ref[...] = val

    @pl.when(jnp.logical_not(first))
    def _():
        ref[...] += val


class _Comm:
    def __init__(self, ins, out_shapes, sem_shapes, phases):
        self.ins, self.out_shapes, self.sem_shapes, self.phases = list(ins), list(out_shapes), list(sem_shapes), phases


def _both(a, b):
    ni, no, ns = len(a.ins), len(a.out_shapes), len(a.sem_shapes)

    def of_a(fn):
        return lambda ins, outs, sems: fn(ins[:ni], outs[:no], sems[:ns])

    def of_b(fn):
        return lambda ins, outs, sems: fn(ins[ni:], outs[no:], sems[ns:])

    return _Comm(a.ins + b.ins, a.out_shapes + b.out_shapes, a.sem_shapes + b.sem_shapes,
                 [(f, of_a(fn)) for f, fn in a.phases] + [(f, of_b(fn)) for f, fn in b.phases])


def _host(body, comm, n_in, n_out, n_scr, nsteps, step_fn):
    if comm is None:
        return body
    ci, co = len(comm.ins), len(comm.out_shapes)

    def wrapped(*refs):
        p = 0
        ins, p = refs[p:p + n_in], p + n_in
        cins, p = refs[p:p + ci], p + ci
        outs, p = refs[p:p + n_out], p + n_out
        couts, p = refs[p:p + co], p + co
        scr, p = refs[p:p + n_scr], p + n_scr
        csems = refs[p:]
        step = step_fn()
        for frac, fn in comm.phases:
            if frac < 1.0:
                @pl.when(step == int(round(frac * (nsteps - 1))))
                def _(fn=fn):
                    fn(cins, couts, csems)
        body(*ins, *outs, *scr)
        for frac, fn in comm.phases:
            if frac >= 1.0:
                @pl.when(step == nsteps - 1)
                def _(fn=fn):
                    fn(cins, couts, csems)

    return wrapped


def _hosted_call(body, comm, args, *, name, grid, in_specs, out_specs, out_shape, scratch_shapes, sem,
                 nsteps, step_fn, aliases=None):
    n_in, n_out, n_scr = len(in_specs), len(out_specs), len(scratch_shapes)
    args = list(args)
    extra = {}
    if comm is not None:
        in_specs = list(in_specs) + [_hbm_spec()] * len(comm.ins)
        out_specs = list(out_specs) + [_hbm_spec()] * len(comm.out_shapes)
        out_shape = list(out_shape) + comm.out_shapes
        scratch_shapes = list(scratch_shapes) + comm.sem_shapes
        args += comm.ins
        extra = dict(has_side_effects=True)
    outs = _pcall(_host(body, comm, n_in, n_out, n_scr, nsteps, step_fn), name=name, grid=grid,
                  in_specs=in_specs, out_specs=out_specs, out_shape=out_shape, scratch_shapes=scratch_shapes,
                  input_output_aliases=aliases or {}, compiler_params=_cp(sem, **extra))(*args)
    return list(outs[:n_out]), list(outs[n_out:])


def _hbm_spec():
    return pl.BlockSpec(memory_space=pl.ANY)


def _wgrad(a_list, b, *, name):
    (t, m), n, gm = a_list[0].shape, b.shape[1], a_list[0].shape[1] // WT
    n_a = len(a_list)
    tile = lambda k: (lambda s: jnp.clip(s - k * gm, 0, gm - 1))

    def body(*refs):
        a_refs, b_ref, o_refs = refs[:n_a], refs[n_a], refs[n_a + 1:]
        s = pl.program_id(0)
        for k in range(n_a):
            @pl.when(jnp.logical_and(s >= k * gm, s < (k + 1) * gm))
            def _(k=k):
                o_refs[k][...] = lax.dot_general(a_refs[k][...], b_ref[...], _TN,
                                                 preferred_element_type=F32).astype(BF)

    return _pcall(body, name=name, grid=(n_a * gm,),
                  in_specs=[pl.BlockSpec((t, WT), lambda s, k=k: (0, tile(k)(s))) for k in range(n_a)]
                  + [pl.BlockSpec((t, n), lambda s: (0, 0))],
                  out_specs=[pl.BlockSpec((WT, n), lambda s, k=k: (tile(k)(s), 0)) for k in range(n_a)],
                  out_shape=[jax.ShapeDtypeStruct((m, n), BF)] * n_a,
                  compiler_params=_cp(("arbitrary",)))(*a_list, b)


def _fmm(lhs, rhs, extras, epilogue, outs, *, m, n, tm, tn, name, comm=None, vecs=(), consts=(), sums=()):
    tm, tn = min(tm, m), min(tn, n)
    assert m % tm == 0 and n % tn == 0 and (not sums or tn == n), (name, m, n, tm, tn)
    in_specs, args = [], []
    for a in lhs:
        in_specs.append(pl.BlockSpec((tm, a.shape[1]), lambda i, j: (i, 0)))
        args.append(a)
    for li, b, tb in rhs:
        k = lhs[li].shape[1]
        in_specs.append(pl.BlockSpec((tn, k), lambda i, j: (j, 0)) if tb
                        else pl.BlockSpec((k, tn), lambda i, j: (0, j)))
        args.append(b)
    for arr, w, col in extras:
        in_specs.append(pl.BlockSpec((tm, w), lambda i, j, col=col: (i, col(j))))
        args.append(arr)
    for vec in vecs:
        in_specs.append(pl.BlockSpec((1, tn), lambda i, j: (0, j)))
        args.append(vec)
    for whole in consts:
        in_specs.append(pl.BlockSpec(whole.shape, lambda i, j: (0, 0)))
        args.append(whole)
    out_specs = [pl.BlockSpec((tm, w), lambda i, j, col=col: (i, col(j))) for _, _, w, col in outs]
    out_shape = [jax.ShapeDtypeStruct((m, total), dt) for dt, total, _, _ in outs]
    for w in sums:
        out_specs.append(pl.BlockSpec((1, w), lambda i, j: (0, 0)))
        out_shape.append(jax.ShapeDtypeStruct((1, w), F32))
    nl, nr, ne, no = len(lhs), len(rhs), len(extras) + len(vecs) + len(consts), len(outs)

    def body(*refs):
        prods = []
        for r, (li, _, tb) in enumerate(rhs):
            prods.append(lax.dot_general(refs[li][...], refs[nl + r][...], _NT if tb else _NN,
                                         preferred_element_type=F32))
        vals = epilogue(prods, [ref[...] for ref in refs[nl + nr:nl + nr + ne]])
        o_refs = refs[nl + nr + ne:]
        for o_ref, v in zip(o_refs[:no], vals[:no]):
            o_ref[...] = v.astype(o_ref.dtype)
        for s_ref, v in zip(o_refs[no:], vals[no:]):
            _accum(s_ref, v, pl.program_id(0) == 0)

    gm, gn = m // tm, n // tn
    res, comm_res = _hosted_call(
        body, comm, args, name=name, grid=(gm, gn), in_specs=in_specs, out_specs=out_specs,
        out_shape=out_shape, scratch_shapes=[], sem=("arbitrary", "arbitrary"), nsteps=gm * gn,
        step_fn=lambda: pl.program_id(0) * gn + pl.program_id(1))
    return res if comm is None else (res, comm_res)


def _grp_of(i):
    return [jnp.logical_and(i >= GRP_OFF[g], i < GRP_OFF[g] + GRP_N[g]) for g in range(4)]


def _grp_idx(i, g):
    return jnp.clip(i - GRP_OFF[g], 0, GRP_N[g] - 1)


def _inproj_fwd(u, win_t, b_in, *, t, comm=None):
    tm = min(1024, t)
    n_row = t // tm
    n_chunks, h_first, g_first = 8, 2, 6
    sub = D // WT

    def w_block(l):
        return jnp.where(l == 0, GRP_OFF[0], jnp.where(l == 1, GRP_OFF[1], GRP_OFF[2] + sub * (l - h_first)))

    def body(u_ref, *rest):
        w_refs, b_refs, (q_ref, kv_ref, h3_ref, hf_ref, g_ref) = rest[:sub], rest[sub:2 * sub], rest[2 * sub:]
        l = pl.program_id(1)

        @pl.when(l == 1)
        def _():
            kv_ref[...] = (lax.dot_general(u_ref[...], w_refs[0][...], _NT, preferred_element_type=F32)
                           + b_refs[0][...]).astype(BF)

        is_hf = l == h_first + 1
        in_h3 = jnp.logical_and(jnp.logical_and(l >= h_first, l < g_first), jnp.logical_not(is_hf))
        for pred, o_ref in ((l == 0, q_ref), (in_h3, h3_ref), (is_hf, hf_ref), (l >= g_first, g_ref)):
            @pl.when(pred)
            def _(o_ref=o_ref):
                w = jnp.concatenate([w[...] for w in w_refs], axis=0)
                b = jnp.concatenate([b[...] for b in b_refs], axis=1)
                o_ref[...] = (lax.dot_general(u_ref[...], w, _NT, preferred_element_type=F32) + b).astype(o_ref.dtype)

    return _hosted_call(
        body, comm, [u] + [win_t] * sub + [b_in] * sub, name="inproj_fwd", grid=(n_row, n_chunks),
        in_specs=[pl.BlockSpec((tm, D), lambda i, l: (i, 0))]
        + [pl.BlockSpec((WT, D), lambda i, l, o=o: (w_block(l) + o, 0)) for o in range(sub)]
        + [pl.BlockSpec((1, WT), lambda i, l, o=o: (0, w_block(l) + o)) for o in range(sub)],
        out_specs=[pl.BlockSpec((tm, D), lambda i, l: (i, 0)),
                   pl.BlockSpec((tm, 256), lambda i, l: (i, 0)),
                   pl.BlockSpec((tm, D), lambda i, l: (i, jnp.clip(l - h_first - 1, 0, 2))),
                   pl.BlockSpec((tm, D), lambda i, l: (i, 0)),
                   pl.BlockSpec((tm, D), lambda i, l: (i, jnp.clip(l - g_first, 0, 1)))],
        out_shape=[jax.ShapeDtypeStruct((t, D), BF), jax.ShapeDtypeStruct((t, 256), BF),
                   jax.ShapeDtypeStruct((t, 3 * D), BF), jax.ShapeDtypeStruct((t, D), F32),
                   jax.ShapeDtypeStruct((t, 2 * D), BF)],
        scratch_shapes=[], sem=("arbitrary", "arbitrary"), nsteps=n_row * n_chunks,
        step_fn=lambda: pl.program_id(0) * n_chunks + pl.program_id(1))


def _inproj_bwd_x(dps, win_t, x, g, resid, *, t, part, prev=None, comm=None):
    n_row = 8 if t >= 4096 else 4
    tm = t // n_row
    first = n_row // 4
    per = first if part == 0 else n_row - first
    row = lambda i: part * first + i

    n_chunks = 4
    sub = 2 * D // WT

    def w_block(l):
        return jnp.where(l == 0, 0, GRP_OFF[2] + sub * (l - 1))

    def body(d0, d1, d2, d3, *rest):
        w_refs, (x_ref, g_ref, r_ref) = rest[:sub], rest[sub:sub + 3]
        dg_prev = rest[sub + 3] if prev is not None else None
        o_ref, dg_ref, acc_ref = rest[-3], rest[-2], rest[-1]
        i, l = pl.program_id(0), pl.program_id(1)

        @pl.when(l == 0)
        def _():
            wq = jnp.concatenate([w[...] for w in w_refs[:GRP_N[0]]], axis=0)
            acc_ref[...] = (jnp.dot(d0[...], wq, preferred_element_type=F32)
                            + jnp.dot(d1[...], w_refs[GRP_N[0]][...], preferred_element_type=F32))

        for pred, d_ref in ((jnp.logical_and(l >= 1, l < 3), d2), (l == 3, d3)):
            @pl.when(pred)
            def _(d_ref=d_ref):
                w = jnp.concatenate([w[...] for w in w_refs], axis=0)
                acc_ref[...] += jnp.dot(d_ref[...], w, preferred_element_type=F32)

        @pl.when(l == n_chunks - 1)
        def _():
            xv = x_ref[...]
            r = lax.rsqrt(jnp.mean(xv * xv, axis=-1, keepdims=True) + EPS)
            xh = xv * r
            du = acc_ref[...]
            dxh = du * g_ref[...]
            o_ref[...] = r_ref[...] + r * (dxh - xh * jnp.mean(dxh * xh, axis=-1, keepdims=True))
            dg = jnp.sum(du * xh, axis=0, keepdims=True)
            if dg_prev is not None:
                dg = dg + jnp.where(i == 0, 1.0, 0.0) * dg_prev[...]
            _accum(dg_ref, dg, i == 0)

    rows = lambda w: pl.BlockSpec((tm, w), lambda i, l: (row(i), 0))
    in_specs = ([rows(D), rows(256),
                 pl.BlockSpec((tm, 2 * D), lambda i, l: (row(i), jnp.clip(l - 1, 0, 1))), rows(2 * D)]
                + [pl.BlockSpec((WT, D), lambda i, l, o=o: (w_block(l) + o, 0)) for o in range(sub)]
                + [rows(D), pl.BlockSpec((1, D), lambda i, l: (0, 0)), rows(D)])
    args = list(dps) + [win_t] * sub + [x, g, resid]
    aliases = None
    if prev is not None:
        in_specs += [pl.BlockSpec((1, D), lambda i, l: (0, 0)), _hbm_spec()]
        args += [prev[1], prev[0]]
        aliases = {len(args) - 1: 0}
    return _hosted_call(
        body, comm, args, name="inproj_bwd_x%d" % part, grid=(per, n_chunks), in_specs=in_specs,
        out_specs=[rows(D), pl.BlockSpec((1, D), lambda i, l: (0, 0))],
        out_shape=[jax.ShapeDtypeStruct((t, D), F32), jax.ShapeDtypeStruct((1, D), F32)],
        scratch_shapes=[pltpu.VMEM((tm, D), F32)], sem=("arbitrary", "arbitrary"), nsteps=per * n_chunks,
        step_fn=lambda: pl.program_id(0) * n_chunks + pl.program_id(1), aliases=aliases)


def _inproj_bwd_w(dps, u, *, t):
    n_tiles = IN_W // WT
    dims = (((0,), (0,)), ((), ()))

    def body(d0, d1, d2, d3, u_ref, o_ref, db_ref):
        i = pl.program_id(0)
        uv = u_ref[...]
        for g, (pred, d_ref) in enumerate(zip(_grp_of(i), (d0, d1, d2, d3))):
            @pl.when(pred)
            def _(d_ref=d_ref):
                dv = d_ref[...]
                o_ref[...] = lax.dot_general(dv, uv, dims, preferred_element_type=F32).astype(BF)
                db_ref[...] = jnp.sum(dv.astype(F32), axis=0, keepdims=True)

    return _pcall(body, name="inproj_bwd_w", grid=(n_tiles,),
                  in_specs=[pl.BlockSpec((t, WT), lambda i, g=g: (0, _grp_idx(i, g))) for g in range(4)]
                  + [pl.BlockSpec((t, D), lambda i: (0, 0))],
                  out_specs=[pl.BlockSpec((WT, D), lambda i: (i, 0)),
                             pl.BlockSpec((1, WT), lambda i: (0, i))],
                  out_shape=[jax.ShapeDtypeStruct((IN_W, D), BF), jax.ShapeDtypeStruct((1, IN_W), F32)],
                  compiler_params=_cp(("arbitrary",)))(*dps, u)


def _row_spec(tm, width, col=0):
    return pl.BlockSpec((tm, width), lambda i: (i, col))


def _vec_spec(width):
    return pl.BlockSpec((1, width), lambda i: (0, 0))


def _rms_fwd(x, g, *, tm, name, comm=None):
    t = x.shape[0]
    tm = min(tm, t)

    def body(x_ref, g_ref, u_ref):
        xv = x_ref[...]
        r = lax.rsqrt(jnp.mean(xv * xv, axis=-1, keepdims=True) + EPS)
        u_ref[...] = (xv * r * g_ref[...]).astype(BF)

    (u,), comm_res = _hosted_call(
        body, comm, (x, g), name=name, grid=(t // tm,), in_specs=[_row_spec(tm, D), _vec_spec(D)],
        out_specs=[_row_spec(tm, D)], out_shape=[jax.ShapeDtypeStruct((t, D), BF)], scratch_shapes=[],
        sem=("arbitrary",), nsteps=t // tm, step_fn=lambda: pl.program_id(0))
    return u if comm is None else (u, comm_res)


def _rms_bwd(du, x, g, resid, *, tm, name):
    t = x.shape[0]
    tm = min(tm, t)

    def body(du_ref, x_ref, g_ref, r_ref, dx_ref, dxb_ref, dg_ref):
        xv = x_ref[...]
        r = lax.rsqrt(jnp.mean(xv * xv, axis=-1, keepdims=True) + EPS)
        xh = xv * r
        duv = du_ref[...]
        dxh = duv * g_ref[...]
        dx = r_ref[...] + r * (dxh - xh * jnp.mean(dxh * xh, axis=-1, keepdims=True))
        dx_ref[...] = dx
        dxb_ref[...] = dx.astype(BF)
        _accum(dg_ref, jnp.sum(duv * xh, axis=0, keepdims=True), pl.program_id(0) == 0)

    return _pcall(body, name=name, grid=(t // tm,),
                  in_specs=[_row_spec(tm, D), _row_spec(tm, D), _vec_spec(D), _row_spec(tm, D)],
                  out_specs=[_row_spec(tm, D), _row_spec(tm, D), _vec_spec(D)],
                  out_shape=[jax.ShapeDtypeStruct((t, D), F32), jax.ShapeDtypeStruct((t, D), BF),
                             jax.ShapeDtypeStruct((1, D), F32)],
                  compiler_params=_cp(("arbitrary",)))(du, x, g, resid)


def _attn_kv_tiles(kprev, kcur):
    kv = jnp.concatenate([kprev, kcur], axis=0).astype(F32)
    lo = lax.broadcasted_iota(jnp.int32, (2 * BLK, 128), 1) < HEAD
    tiles = []
    for part in (kv[:, 0:128], kv[:, 128:256]):
        rolled = pltpu.roll(part, HEAD, 1)
        z = jnp.zeros_like(part)
        tiles.append(((jnp.where(lo, part, z).astype(BF), jnp.where(lo, z, rolled).astype(BF)),
                      (jnp.where(lo, rolled, z).astype(BF), jnp.where(lo, z, part).astype(BF))))
    k_t, v_t = tiles
    return [(jnp.concatenate(k_t[h], axis=0), jnp.concatenate(v_t[h], axis=0)) for h in range(2)]


def _attn_mask(i):
    qi = lax.broadcasted_iota(jnp.int32, (BLK, 2 * BLK), 0)
    kj = lax.broadcasted_iota(jnp.int32, (BLK, 2 * BLK), 1)
    first_key = jnp.where(i == 0, BLK, 0)
    in_prev = jnp.logical_and(jnp.logical_and(kj < BLK, kj > qi), kj >= first_key)
    in_cur = jnp.logical_and(kj >= BLK, kj - BLK <= qi)
    return jnp.logical_or(in_prev, in_cur)


def _attn_probs(s, sink, valid):
    s = jnp.where(valid, s * SCALE, NEG)
    mx = jnp.maximum(jnp.max(s, axis=-1, keepdims=True), sink)
    e = jnp.exp(s - mx)
    es = jnp.exp(sink - mx)
    inv = 1.0 / (jnp.sum(e, axis=-1, keepdims=True) + es)
    return e * inv, es * inv


_KEYS = 2 * BLK


def _pair(ref, j):
    return ref[:, j * 128:(j + 1) * 128]


def _attn_fwd(q, kv, sinks, *, t, comm=None):
    nb = t // BLK

    def body(sink_ref, q_ref, kp_ref, kc_ref, o_ref):
        i = pl.program_id(0)
        for c in range(2):
            rows = slice(c * BLK, (c + 1) * BLK)
            valid = _attn_mask(2 * i + c)
            tiles = _attn_kv_tiles(kp_ref[...] if c == 0 else kc_ref[0:BLK, :], kc_ref[rows, :])
            s = [lax.dot_general(q_ref[rows, j * 128:(j + 1) * 128], tiles[j // 4][0], _NT,
                                 preferred_element_type=F32) for j in range(N_PAIR)]
            p = []
            for j in range(N_PAIR):
                pe, _ = _attn_probs(s[j][:, 0:_KEYS], sink_ref[0, 2 * j], valid)
                po, _ = _attn_probs(s[j][:, _KEYS:2 * _KEYS], sink_ref[0, 2 * j + 1], valid)
                p.append(jnp.concatenate([pe.astype(BF), po.astype(BF)], axis=1))
            for j in range(N_PAIR):
                o_ref[rows, j * 128:(j + 1) * 128] = jnp.dot(p[j], tiles[j // 4][1],
                                                             preferred_element_type=F32).astype(BF)

    return _hosted_call(
        body, comm, (sinks, q, kv, kv), name="attn_fwd", grid=(nb // 2,),
        in_specs=[pl.BlockSpec(memory_space=pltpu.SMEM),
                  pl.BlockSpec((2 * BLK, D), lambda i: (i, 0)),
                  pl.BlockSpec((BLK, 256), lambda i: (jnp.maximum(2 * i - 1, 0), 0)),
                  pl.BlockSpec((2 * BLK, 256), lambda i: (i, 0))],
        out_specs=[pl.BlockSpec((2 * BLK, D), lambda i: (i, 0))],
        out_shape=[jax.ShapeDtypeStruct((t, D), BF)],
        scratch_shapes=[], sem=("arbitrary",), nsteps=nb // 2, step_fn=lambda: pl.program_id(0))


def _attn_bwd(q, kv, sinks, do, *, t, comm=None):
    nb = t // BLK
    last = nb - 1

    def body(sink_ref, q_ref, kp_ref, kc_ref, do_ref, dq_ref, dkv_ref, ds_ref, carry_ref):
        i = pl.program_id(0)

        @pl.when(i == 0)
        def _():
            ds_ref[...] = jnp.zeros_like(ds_ref)
            carry_ref[...] = jnp.zeros_like(carry_ref)

        @pl.when(i < nb)
        def _():
            valid = _attn_mask(i)
            tiles = _attn_kv_tiles(kp_ref[...], kc_ref[...])
            lane1 = lax.broadcasted_iota(jnp.int32, (1, 128), 1)
            dsink = jnp.zeros((1, 128), F32)
            s = [lax.dot_general(_pair(q_ref, j), tiles[j // 4][0], _NT, preferred_element_type=F32)
                 for j in range(N_PAIR)]
            dp = [lax.dot_general(_pair(do_ref, j), tiles[j // 4][1], _NT, preferred_element_type=F32)
                  for j in range(N_PAIR)]
            p_all, ds_all = [], []
            for j in range(N_PAIR):
                halves = []
                for par in range(2):
                    cols = slice(par * _KEYS, (par + 1) * _KEYS)
                    p, ps = _attn_probs(s[j][:, cols], sink_ref[0, 2 * j + par], valid)
                    dpj = dp[j][:, cols]
                    dd = jnp.sum(p * dpj, axis=-1, keepdims=True)
                    dsink = dsink + jnp.where(lane1 == 2 * j + par,
                                              -jnp.sum(ps * dd, axis=0, keepdims=True), 0.0)
                    halves.append((p.astype(BF), (p * (dpj - dd)).astype(BF)))
                p_all.append(jnp.concatenate([halves[0][0], halves[1][0]], axis=1))
                ds_all.append(jnp.concatenate([halves[0][1], halves[1][1]], axis=1))
            for j in range(N_PAIR):
                dq_ref[:, j * 128:(j + 1) * 128] = (
                    jnp.dot(ds_all[j], tiles[j // 4][0], preferred_element_type=F32) * SCALE).astype(BF)
            ds_ref[...] += dsink
            gk, gv = [], []
            for h in range(2):
                grp = range(4 * h, 4 * h + 4)
                q_rows = jnp.concatenate([_pair(q_ref, j) for j in grp], axis=0)
                do_rows = jnp.concatenate([_pair(do_ref, j) for j in grp], axis=0)
                g_k = lax.dot_general(jnp.concatenate([ds_all[j] for j in grp], axis=0), q_rows, _TN,
                                      preferred_element_type=F32)
                g_v = lax.dot_general(jnp.concatenate([p_all[j] for j in grp], axis=0), do_rows, _TN,
                                      preferred_element_type=F32)
                gk.append((g_k[0:_KEYS], g_k[_KEYS:2 * _KEYS]))
                gv.append((g_v[0:_KEYS], g_v[_KEYS:2 * _KEYS]))
            lo = lax.broadcasted_iota(jnp.int32, (2 * BLK, 128), 1) < HEAD
            zero = jnp.zeros((2 * BLK, 128), F32)

            def unpad(g):
                return (jnp.where(lo, g[0][0] + pltpu.roll(g[0][1], HEAD, 1), zero)
                        + jnp.where(lo, zero, pltpu.roll(g[1][0], HEAD, 1) + g[1][1]))

            dk = unpad(gk) * SCALE
            dv = unpad(gv)
            dkv_ref[:, 0:128] = (carry_ref[:, 0:128] + dk[0:BLK]).astype(BF)
            dkv_ref[:, 128:256] = (carry_ref[:, 128:256] + dv[0:BLK]).astype(BF)
            carry_ref[:, 0:128] = dk[BLK:2 * BLK]
            carry_ref[:, 128:256] = dv[BLK:2 * BLK]

        @pl.when(i == nb)
        def _():
            dkv_ref[...] = carry_ref[...].astype(BF)

    return _hosted_call(
        body, comm, (sinks, q, kv, kv, do), name="attn_bwd", grid=(nb + 1,),
        in_specs=[pl.BlockSpec(memory_space=pltpu.SMEM),
                  pl.BlockSpec((BLK, D), lambda i: (jnp.minimum(i, last), 0)),
                  pl.BlockSpec((BLK, 256), lambda i: (jnp.clip(i - 1, 0, last), 0)),
                  pl.BlockSpec((BLK, 256), lambda i: (jnp.minimum(i, last), 0)),
                  pl.BlockSpec((BLK, D), lambda i: (jnp.minimum(i, last), 0))],
        out_specs=[pl.BlockSpec((BLK, D), lambda i: (jnp.minimum(i, last), 0)),
                   pl.BlockSpec((BLK, 256), lambda i: (jnp.maximum(i - 1, 0), 0)),
                   pl.BlockSpec((1, 128), lambda i: (0, 0))],
        out_shape=[jax.ShapeDtypeStruct((t, D), BF), jax.ShapeDtypeStruct((t, 256), BF),
                   jax.ShapeDtypeStruct((1, 128), F32)],
        scratch_shapes=[pltpu.VMEM((BLK, 256), F32)], sem=("arbitrary",), nsteps=nb + 1,
        step_fn=lambda: pl.program_id(0))


def _split3(v):
    h = v.astype(BF)
    r = v - h.astype(F32)
    m = r.astype(BF)
    lo = (r - m.astype(F32)).astype(BF)
    return jnp.concatenate([h, m, lo], axis=1)


def _apply01(mat, v):
    n = v.shape[1]
    r = jnp.dot(mat, _split3(v), preferred_element_type=F32)
    return r[:, 0:n] + r[:, n:2 * n] + r[:, 2 * n:3 * n]


def _hgrn_gates(hq, hf, lb):
    sq = _sig(hq)
    sg = _sig(hf)
    f = lb + (1.0 - lb) * sg
    return hq * sq, (1.0 - lb) * (1.0 - sg), jnp.log(f), sq, sg, f


def _tri(upper):
    r = lax.broadcasted_iota(jnp.int32, (CH, CH), 0)
    c = lax.broadcasted_iota(jnp.int32, (CH, CH), 1)
    return (c >= r) if upper else (c <= r)


def _lb_from_logits(lg_ref):
    return 1.0 / (1.0 + jnp.exp(lg_ref[1:2, :] - lg_ref[0:1, :]))


def _hgrn_fwd(h3, hf, logits, norm_g, *, t, comm=None):
    nc = t // CH
    nt_dims = (((1,), (1,)), ((), ()))
    tn_dims = (((0,), (0,)), ((), ()))

    def body(h_ref, hf_ref, lg_ref, ng_ref, y_ref, o_ref, st_ref, s_scr, b_scr, qa_s, ka_s, qb_s, kb_s, v_s):
        @pl.when(pl.program_id(0) == 0)
        def _():
            s_scr[...] = jnp.zeros_like(s_scr)

        heads = [slice(h * HG_K, (h + 1) * HG_K) for h in range(HG_HEADS)]
        causal = _tri(False)
        lb = _lb_from_logits(lg_ref)
        for c in range(HG_SUB):
            rows = slice(c * CH, (c + 1) * CH)
            q, k, g, _, _, _ = _hgrn_gates(h_ref[rows, 0:D].astype(F32), hf_ref[rows, :], lb)
            b_scr[...] = _apply01(jnp.where(causal, 1.0, 0.0).astype(BF), g)
            b = b_scr[...]
            b_mid = b_scr[CH // 2 - 1:CH // 2, :]
            b_last = b_scr[CH - 1:CH, :]
            qa_s[...] = (q * jnp.exp(b - b_mid)).astype(BF)
            ka_s[...] = (k * jnp.exp(b_mid - b)).astype(BF)
            qb_s[...] = (q * jnp.exp(b)).astype(BF)
            kb_s[...] = (k * jnp.exp(b_last - b)).astype(BF)
            v_s[...] = h_ref[rows, D:2 * D]
            dec = jnp.exp(b_last)
            st_ref[c] = s_scr[...].astype(BF)
            a = [jnp.where(causal, lax.dot_general(qa_s[:, sl], ka_s[:, sl], nt_dims, preferred_element_type=F32),
                           0.0).astype(BF) for sl in heads]
            for h, sl in enumerate(heads):
                o_ref[rows, sl] = (jnp.dot(a[h], v_s[:, sl], preferred_element_type=F32)
                                   + lax.dot_general(qb_s[:, sl], s_scr[h].astype(BF), nt_dims,
                                                     preferred_element_type=F32))
            for h, sl in enumerate(heads):
                s_scr[h] = dec[:, sl] * s_scr[h] + lax.dot_general(v_s[:, sl], kb_s[:, sl], tn_dims,
                                                                   preferred_element_type=F32)
            for h, sl in enumerate(heads):
                o = o_ref[rows, sl]
                on = o * lax.rsqrt(jnp.mean(o * o, axis=-1, keepdims=True) + EPS)
                gate = _sig(h_ref[rows, 2 * D + h * HG_K:2 * D + (h + 1) * HG_K].astype(F32))
                y_ref[rows, sl] = (on * ng_ref[:, sl] * gate).astype(BF)

    half = lambda: pltpu.VMEM((CH, D), BF)
    blk = HG_SUB * CH
    return _hosted_call(
        body, comm, (h3, hf, logits, norm_g), name="hgrn_fwd", grid=(nc // HG_SUB,),
        in_specs=[pl.BlockSpec((blk, 3 * D), lambda n: (n, 0)),
                  pl.BlockSpec((blk, D), lambda n: (n, 0)),
                  pl.BlockSpec((2, D), lambda n: (0, 0)),
                  pl.BlockSpec((1, D), lambda n: (0, 0))],
        out_specs=[pl.BlockSpec((blk, D), lambda n: (n, 0)),
                   pl.BlockSpec((blk, D), lambda n: (n, 0)),
                   pl.BlockSpec((HG_SUB, HG_HEADS, HG_K, HG_K), lambda n: (n, 0, 0, 0))],
        out_shape=[jax.ShapeDtypeStruct((t, D), BF), jax.ShapeDtypeStruct((t, D), F32),
                   jax.ShapeDtypeStruct((nc, HG_HEADS, HG_K, HG_K), BF)],
        scratch_shapes=[pltpu.VMEM((HG_HEADS, HG_K, HG_K), F32), pltpu.VMEM((CH, D), F32),
                        half(), half(), half(), half(), half()],
        sem=("arbitrary",), nsteps=nc // HG_SUB, step_fn=lambda: pl.program_id(0))


def _hgrn_bwd(h3, hf, logits, norm_g, o_pre, states, dy, *, t, comm=None):
    nc = t // CH
    nt_dims = (((1,), (1,)), ((), ()))
    tn_dims = (((0,), (0,)), ((), ()))

    def body(h_ref, hf_ref, lg_ref, ng_ref, o_ref, st_ref, dy_ref, dh_ref, dlg_ref, dng_ref, ds_scr, dlb_scr,
             b_scr, tail_s, e_qa, e_ka, e_qb, e_kb, q_s, k_s, dqa_s, dka_s, dqb_s, dkb_s,
             qa_s, ka_s, qb_s, kb_s, v_s, do_s):
        n = pl.program_id(0)

        @pl.when(n == 0)
        def _():
            ds_scr[...] = jnp.zeros_like(ds_scr)
            dlb_scr[...] = jnp.zeros_like(dlb_scr)
            dng_ref[...] = jnp.zeros_like(dng_ref)

        heads = [slice(h * HG_K, (h + 1) * HG_K) for h in range(HG_HEADS)]
        lb = _lb_from_logits(lg_ref)
        causal = _tri(False)

        def chunk(c):
            rows = slice(c * CH, (c + 1) * CH)
            hq = h_ref[rows, 0:D].astype(F32)
            q, k, g, sq, sg, f = _hgrn_gates(hq, hf_ref[rows, :], lb)
            b_scr[...] = _apply01(jnp.where(causal, 1.0, 0.0).astype(BF), g)
            b = b_scr[...]
            b_mid = b_scr[CH // 2 - 1:CH // 2, :]
            b_last = b_scr[CH - 1:CH, :]
            q_s[...] = q
            k_s[...] = k
            for e_ref, s_ref, base, expo in ((e_qa, qa_s, q, b - b_mid), (e_ka, ka_s, k, b_mid - b),
                                             (e_qb, qb_s, q, b), (e_kb, kb_s, k, b_last - b)):
                e = jnp.exp(expo)
                e_ref[...] = e
                s_ref[...] = (base * e).astype(BF)
            v_s[...] = h_ref[rows, D:2 * D]
            dec = jnp.exp(b_last)
            for h, sl in enumerate(heads):
                gcol = slice(3 * D + h * HG_K, 3 * D + (h + 1) * HG_K)
                ngh = ng_ref[:, sl]
                sgate = _sig(h_ref[rows, 2 * D + h * HG_K:2 * D + (h + 1) * HG_K].astype(F32))
                o = o_ref[rows, sl]
                r = lax.rsqrt(jnp.mean(o * o, axis=-1, keepdims=True) + EPS)
                on = o * r
                dyh = dy_ref[rows, sl]
                dh_ref[rows, gcol] = (dyh * on * ngh * sgate * (1.0 - sgate)).astype(BF)
                dng_ref[:, sl] += jnp.sum(dyh * on * sgate, axis=0, keepdims=True)
                don = dyh * ngh * sgate
                do_s[:, sl] = (r * (don - on * jnp.mean(don * on, axis=-1, keepdims=True))).astype(BF)
            a = [jnp.where(causal, lax.dot_general(qa_s[:, sl], ka_s[:, sl], nt_dims, preferred_element_type=F32),
                           0.0).astype(BF) for sl in heads]
            da = [jnp.where(causal, lax.dot_general(do_s[:, sl], v_s[:, sl], nt_dims, preferred_element_type=F32),
                            0.0).astype(BF) for sl in heads]
            for h, sl in enumerate(heads):
                dh_ref[rows, 2 * D + h * HG_K:2 * D + (h + 1) * HG_K] = (
                    lax.dot_general(a[h], do_s[:, sl], tn_dims, preferred_element_type=F32)
                    + lax.dot_general(kb_s[:, sl], ds_scr[h].astype(BF), nt_dims, preferred_element_type=F32)
                ).astype(BF)
            for h, sl in enumerate(heads):
                dqa_s[:, sl] = jnp.dot(da[h], ka_s[:, sl], preferred_element_type=F32)
            for h, sl in enumerate(heads):
                dka_s[:, sl] = lax.dot_general(da[h], qa_s[:, sl], tn_dims, preferred_element_type=F32)
            for h, sl in enumerate(heads):
                dqb_s[:, sl] = jnp.dot(do_s[:, sl], st_ref[c, h], preferred_element_type=F32)
            for h, sl in enumerate(heads):
                dkb_s[:, sl] = jnp.dot(v_s[:, sl], ds_scr[h].astype(BF), preferred_element_type=F32)
            for h, sl in enumerate(heads):
                tail_s[:, sl] = jnp.sum(dec[:, sl] * st_ref[c, h].astype(F32) * ds_scr[h], axis=0, keepdims=True)
            for h, sl in enumerate(heads):
                ds_scr[h] = (lax.dot_general(do_s[:, sl], qb_s[:, sl], tn_dims, preferred_element_type=F32)
                             + dec[:, sl] * ds_scr[h])
            qv, kv = q_s[...], k_s[...]
            dqa, dka, dqb, dkb = dqa_s[...], dka_s[...], dqb_s[...], dkb_s[...]
            eqa, eka, eqb, ekb = e_qa[...], e_ka[...], e_qb[...], e_kb[...]
            dkb_kb = dkb * (kv * ekb)
            db_last = jnp.sum(dkb_kb, axis=0, keepdims=True) + tail_s[...]
            last_row = lax.broadcasted_iota(jnp.int32, (CH, D), 0) == CH - 1
            db = (dqa * (qv * eqa) - dka * (kv * eka) + dqb * (qv * eqb) - dkb_kb
                  + jnp.where(last_row, db_last, 0.0))
            dg = _apply01(jnp.where(_tri(True), 1.0, 0.0).astype(BF), db)
            dq = dqa * eqa + dqb * eqb
            dk = dka * eka + dkb * ekb
            dh_ref[rows, 0:D] = (dq * sq * (1.0 + hq * (1.0 - sq))).astype(BF)
            dfk = dg / f - dk
            dh_ref[rows, D:2 * D] = ((1.0 - lb) * dfk * sg * (1.0 - sg)).astype(BF)
            dlb_scr[...] += jnp.sum((1.0 - sg) * dfk, axis=0, keepdims=True)

        for c in reversed(range(HG_SUB)):
            chunk(c)

        @pl.when(n == nc // HG_SUB - 1)
        def _():
            dl0 = dlb_scr[...] * lb * (1.0 - lb)
            dlg_ref[0:1, :] = dl0
            dlg_ref[1:2, :] = -dl0

    steps = nc // HG_SUB
    blk = HG_SUB * CH
    rev = lambda n: (steps - 1 - n, 0)
    return _hosted_call(
        body, comm, (h3, hf, logits, norm_g, o_pre, states, dy), name="hgrn_bwd", grid=(steps,),
        in_specs=[pl.BlockSpec((blk, 3 * D), rev),
                  pl.BlockSpec((blk, D), rev),
                  pl.BlockSpec((2, D), lambda n: (0, 0)),
                  pl.BlockSpec((1, D), lambda n: (0, 0)),
                  pl.BlockSpec((blk, D), rev),
                  pl.BlockSpec((HG_SUB, HG_HEADS, HG_K, HG_K), lambda n: (steps - 1 - n, 0, 0, 0)),
                  pl.BlockSpec((blk, D), rev)],
        out_specs=[pl.BlockSpec((blk, 4 * D), rev),
                   pl.BlockSpec((2, D), lambda n: (0, 0)),
                   pl.BlockSpec((1, D), lambda n: (0, 0))],
        out_shape=[jax.ShapeDtypeStruct((t, 4 * D), BF), jax.ShapeDtypeStruct((2, D), F32),
                   jax.ShapeDtypeStruct((1, D), F32)],
        scratch_shapes=([pltpu.VMEM((HG_HEADS, HG_K, HG_K), F32), pltpu.VMEM((1, D), F32),
                         pltpu.VMEM((CH, D), F32), pltpu.VMEM((1, D), F32)]
                        + [pltpu.VMEM((CH, D), F32)] * 10 + [pltpu.VMEM((CH, D), BF)] * 6),
        sem=("arbitrary",), nsteps=steps, step_fn=lambda: pl.program_id(0))


def _place():
    x, y, c = lax.axis_index("x"), lax.axis_index("y"), lax.axis_index("c")
    return x, y, c, [(1 - x, y), (x, 1 - y), (1 - x, 1 - y)]


def _gather_comm(shards, mids):
    n, pieces = len(shards), len(mids)
    r = [s.shape[0] for s in shards]
    tile = 16
    cut = [[(rw // tile * p // pieces) * tile for p in range(pieces + 1)] for rw in r]
    size = [[cut[w][p + 1] - cut[w][p] for p in range(pieces)] for w in range(n)]

    def tools(ins, outs, sems):
        send_sems, recv_sems, local_sems = sems
        x, y, c, _ = _place()
        me, sib = (x, y, c), (x, y, 1 - c)
        near = [(x ^ c, y ^ (1 - c), c), (x ^ (1 - c), y ^ c, c), (1 - x, 1 - y, c)]

        def rows(w, p, dev):
            return outs[w].at[pl.ds((4 * dev[0] + 2 * dev[1] + dev[2]) * r[w] + cut[w][p], size[w][p]), :]

        def copy(kind, w, p, block, to, own=False):
            src = ins[w].at[pl.ds(cut[w][p], size[w][p]), :] if own else rows(w, p, block)
            return pltpu.make_async_remote_copy(
                src_ref=src, dst_ref=rows(w, p, block), send_sem=send_sems.at[p, kind],
                recv_sem=recv_sems.at[p, kind], device_id=to, device_id_type=MESH)

        def all_of(kind, p):
            whole = outs[0].at[pl.ds(0, sum(size[w][p] for w in range(n))), :]
            return pltpu.make_async_remote_copy(
                src_ref=whole, dst_ref=whole, send_sem=send_sems.at[p, kind], recv_sem=recv_sems.at[p, kind],
                device_id=me, device_id_type=MESH)

        mine = [pltpu.make_async_copy(ins[w], outs[w].at[pl.ds((4 * x + 2 * y + c) * r[w], r[w]), :],
                                      local_sems.at[w]) for w in range(n)]
        return near, me, sib, copy, all_of, mine

    def start(ins, outs, sems):
        near, me, sib, copy, _, mine = tools(ins, outs, sems)
        for cp in mine:
            cp.start()
        for p in range(pieces):
            for w in range(n):
                copy(0, w, p, me, sib, own=True).start()
                copy(1, w, p, me, near[0], own=True).start()
                copy(2, w, p, me, near[1], own=True).start()

    def pass_diagonal(p, near, sib, copy, all_of):
        all_of(3, p).wait_recv()
        for w in range(n):
            copy(6, w, p, near[2], sib).start()

    def pass_on(p):
        def phase(ins, outs, sems):
            near, _, sib, copy, all_of, _ = tools(ins, outs, sems)
            all_of(1, p).wait_recv()
            for w in range(n):
                copy(3, w, p, near[0], near[1]).start()
                copy(4, w, p, near[0], sib).start()
            all_of(2, p).wait_recv()
            for w in range(n):
                copy(5, w, p, near[1], sib).start()
            if p > 0:
                pass_diagonal(p - 1, near, sib, copy, all_of)
        return phase

    def finish(ins, outs, sems):
        near, _, sib, copy, all_of, mine = tools(ins, outs, sems)
        pass_diagonal(pieces - 1, near, sib, copy, all_of)
        for p in range(pieces):
            all_of(0, p).wait_recv()
            for kind in (4, 5, 6):
                all_of(kind, p).wait_recv()
            for kind in range(7):
                all_of(kind, p).wait_send()
        for cp in mine:
            cp.wait()

    return _Comm(shards, [jax.ShapeDtypeStruct((N_DEV * rw, D), BF) for rw in r],
                 [pltpu.SemaphoreType.DMA((pieces, 7)), pltpu.SemaphoreType.DMA((pieces, 7)),
                  pltpu.SemaphoreType.DMA((n,))],
                 [(0.0, start)] + [(f, pass_on(p)) for p, f in enumerate(mids)] + [(1.0, finish)])


def _pair_comm(grads):
    n = len(grads)
    r = [g.shape[0] // N_DEV for g in grads]

    def start(ins, outs, sems):
        send_sems, recv_sems = sems
        x, y, c, _ = _place()
        for w in range(n):
            for a in range(N_CHIP):
                pltpu.make_async_remote_copy(
                    src_ref=ins[w].at[pl.ds((2 * a + 1 - c) * r[w], r[w]), :], dst_ref=outs[w].at[a],
                    send_sem=send_sems.at[w], recv_sem=recv_sems.at[w],
                    device_id=(x, y, 1 - c), device_id_type=MESH).start()

    def finish(ins, outs, sems):
        send_sems, recv_sems = sems
        x, y, c, _ = _place()
        for w in range(n):
            pltpu.make_async_remote_copy(
                src_ref=outs[w], dst_ref=outs[w], send_sem=send_sems.at[w], recv_sem=recv_sems.at[w],
                device_id=(x, y, c), device_id_type=MESH).wait()

    return _Comm(grads, [jax.ShapeDtypeStruct((N_CHIP, rw, D), BF) for rw in r],
                 [pltpu.SemaphoreType.DMA((n,)), pltpu.SemaphoreType.DMA((n,))],
                 [(0.0, start), (1.0, finish)])


def _pair_add(grads, gots, core, *, name):
    n, r = len(grads), gots[0].shape[1]
    tr = r if r <= 128 else r // 2
    steps = r // tr
    tile = lambda k: (lambda s: jnp.clip(s - k * steps, 0, steps - 1))

    def body(c_ref, *refs):
        g_refs, got_refs, o_refs = refs[:n], refs[n:2 * n], refs[2 * n:]
        s = pl.program_id(0)
        for k in range(n):
            @pl.when(jnp.logical_and(s >= k * steps, s < (k + 1) * steps))
            def _(k=k):
                o_refs[k][...] = (g_refs[k][:, 0].astype(F32) + got_refs[k][...].astype(F32)).astype(BF)

    grid_spec = pltpu.PrefetchScalarGridSpec(
        num_scalar_prefetch=1, grid=(n * steps,),
        in_specs=[pl.BlockSpec((N_CHIP, 1, tr, D), lambda s, c_ref, k=k: (0, c_ref[0], tile(k)(s), 0))
                  for k in range(n)]
        + [pl.BlockSpec((N_CHIP, tr, D), lambda s, c_ref, k=k: (0, tile(k)(s), 0)) for k in range(n)],
        out_specs=[pl.BlockSpec((N_CHIP, tr, D), lambda s, c_ref, k=k: (0, tile(k)(s), 0)) for k in range(n)])
    return _pcall(body, name=name, grid_spec=grid_spec,
                  out_shape=[jax.ShapeDtypeStruct((N_CHIP, r, D), BF)] * n,
                  compiler_params=_cp(("arbitrary",)))(
                      core, *[g.reshape(N_CHIP, 2, r, D) for g in grads], *gots)


def _chip_comm(pair_sums):
    n = len(pair_sums)
    r = [p.shape[1] for p in pair_sums]
    off = [sum(r[:w]) for w in range(n)]

    def tools(ins, outs, sems):
        send_sems, recv_sems, local_sems = sems
        x, y, c, chips = _place()
        my_chip = 2 * x + y

        def slot(w):
            return outs[0].at[my_chip, pl.ds(off[w], r[w]), :]

        own = [pltpu.make_async_copy(ins[w].at[my_chip], slot(w), local_sems.at[w]) for w in range(n)]
        return x, y, c, chips, my_chip, slot, own, send_sems, recv_sems

    def start(ins, outs, sems):
        x, y, c, chips, my_chip, slot, own, send_sems, recv_sems = tools(ins, outs, sems)
        for cp in own:
            cp.start()
        for j, chip in enumerate(chips):
            for w in range(n):
                pltpu.make_async_remote_copy(
                    src_ref=ins[w].at[2 * chip[0] + chip[1]], dst_ref=slot(w), send_sem=send_sems.at[j],
                    recv_sem=recv_sems.at[j], device_id=(*chip, c), device_id_type=MESH).start()

    def finish(ins, outs, sems):
        x, y, c, chips, my_chip, slot, own, send_sems, recv_sems = tools(ins, outs, sems)
        whole = outs[0].at[my_chip]
        for j in range(3):
            pltpu.make_async_remote_copy(
                src_ref=whole, dst_ref=whole, send_sem=send_sems.at[j], recv_sem=recv_sems.at[j],
                device_id=(x, y, c), device_id_type=MESH).wait()
        for cp in own:
            cp.wait()

    return _Comm(pair_sums, [jax.ShapeDtypeStruct((N_CHIP, sum(r), D), BF)],
                 [pltpu.SemaphoreType.DMA((3,)), pltpu.SemaphoreType.DMA((3,)), pltpu.SemaphoreType.DMA((n,))],
                 [(0.0, start), (1.0, finish)])


def _adam_math(w, g, m, v):
    m = ADAM_B1 * m + (1.0 - ADAM_B1) * g
    v = ADAM_B2 * v + (1.0 - ADAM_B2) * (g * g)
    m_hat = m / (1.0 - ADAM_B1 ** ADAM_STEP)
    v_hat = v / (1.0 - ADAM_B2 ** ADAM_STEP)
    delta = -ADAM_LR * (m_hat / (jnp.sqrt(v_hat) + ADAM_EPS) + ADAM_WD * w)
    return delta, m, v


SMALL = (("norm_mix_g", (1, D), 0), ("hgrn_norm_g", (1, D), 1), ("norm_ffn_g", (1, D), 2),
         ("norm_final_g", (1, D), 3), ("hgrn_lb_logits", (2, D), 4), ("attn_sinks", (1, 16), 6),
         ("b_in", (1, IN_W), 8))
LOSS_ROW = 7


def _small_allreduce_adam(grads, loss_row, params):
    n = len(SMALL)

    def rows_of(ref, shape, row):
        r, w = shape
        if w <= D:
            return ref[row:row + r, 0:w]
        pieces = [ref[row + k:row + k + 1, :] for k in range(-(-w // D))]
        return jnp.concatenate(pieces, axis=1)[:, 0:w]

    def body(*refs):
        g_refs, loss_ref = refs[:n], refs[n]
        wmv = refs[n + 1:4 * n + 1]
        loss_out = refs[4 * n + 1]
        outs = refs[4 * n + 2:8 * n + 2]
        mine, total, gath, send_sems, recv_sems = refs[8 * n + 2:]
        x, y, c, _ = _place()
        me = 4 * x + 2 * y + c
        mine[...] = jnp.zeros_like(mine)
        for g_ref, (_, (r, w), row) in zip(g_refs, SMALL):
            for k in range(-(-w // D)):
                wk = min(D, w - k * D)
                mine[row + k:row + k + r, 0:wk] = g_ref[:, k * D:k * D + wk]
        mine[LOSS_ROW:LOSS_ROW + 1, 0:128] = loss_ref[...]
        gath[me] = mine[...]
        cps = []
        for d in range(1, N_DEV):
            peer = (x ^ (d >> 2), y ^ ((d >> 1) & 1), c ^ (d & 1))
            cps.append(pltpu.make_async_remote_copy(
                src_ref=mine, dst_ref=gath.at[me], send_sem=send_sems.at[d - 1],
                recv_sem=recv_sems.at[d - 1], device_id=peer, device_id_type=MESH))
        for cp in cps:
            cp.start()
        for cp in cps:
            cp.wait()
        g = gath[0]
        for k in range(1, N_DEV):
            g = g + gath[k]
        total[...] = g
        loss_out[...] = total[LOSS_ROW:LOSS_ROW + 1, 0:128]
        for i, (_, shape, row) in enumerate(SMALL):
            gi = rows_of(total, shape, row)
            w_ref, m_ref, v_ref = wmv[3 * i:3 * i + 3]
            o = outs[4 * i:4 * i + 4]
            o[0][...] = gi
            o[1][...], o[2][...], o[3][...] = _adam_math(w_ref[...], gi, m_ref[...], v_ref[...])

    vm = pl.BlockSpec(memory_space=pltpu.VMEM)
    ins = [grads[name] for name, _, _ in SMALL] + [loss_row]
    for name, _, _ in SMALL:
        ins += list(params[name])
    out_shape = [jax.ShapeDtypeStruct((1, 128), F32)]
    for _, shape, _ in SMALL:
        out_shape += [jax.ShapeDtypeStruct(shape, F32)] * 4
    res = _pcall(body, name="small_allreduce_adam", in_specs=[vm] * len(ins), out_specs=[vm] * len(out_shape),
                 out_shape=out_shape,
                 scratch_shapes=[pltpu.VMEM((SMALL_ROWS, D), F32), pltpu.VMEM((SMALL_ROWS, D), F32),
                                 pltpu.VMEM((N_DEV, SMALL_ROWS, D), F32),
                                 pltpu.SemaphoreType.DMA((N_DEV - 1,)), pltpu.SemaphoreType.DMA((N_DEV - 1,))],
                 compiler_params=pltpu.CompilerParams(has_side_effects=True))(*ins)
    return res[0], {name: res[1 + 4 * i:5 + 4 * i] for i, (name, _, _) in enumerate(SMALL)}


def _adam(ws, parts, ms, vs, *, name, carry=None):
    n, rows = len(ws), ws[0].shape[0]
    tr = rows if rows <= 128 else rows // 2
    steps = rows // tr
    tile = lambda k: (lambda s: jnp.clip(s - k * steps, 0, steps - 1))
    extra = [] if carry is None else [carry]

    def body(*refs):
        w_refs, m_refs, v_refs, p_ref = refs[:n], refs[n:2 * n], refs[2 * n:3 * n], refs[3 * n]
        o_refs = refs[3 * n + 1 + len(extra):]
        s = pl.program_id(0)
        for k in range(n):
            @pl.when(jnp.logical_and(s >= k * steps, s < (k + 1) * steps))
            def _(k=k):
                g = p_ref[0].astype(F32)
                for a in range(1, N_CHIP):
                    g = g + p_ref[a].astype(F32)
                o = o_refs[4 * k:4 * k + 4]
                o[0][...] = g
                o[1][...], o[2][...], o[3][...] = _adam_math(w_refs[k][...], g, m_refs[k][...], v_refs[k][...])

    spec = lambda k: pl.BlockSpec((tr, D), lambda s, k=k: (tile(k)(s), 0))
    res = _pcall(body, name=name, grid=(n * steps,),
                 in_specs=[spec(k) for k in range(n)] * 3 + [pl.BlockSpec((N_CHIP, tr, D), lambda s: (0, s, 0))]
                 + [_hbm_spec() for _ in extra],
                 out_specs=[spec(k) for k in range(n) for _ in range(4)] + [_hbm_spec() for _ in extra],
                 out_shape=[jax.ShapeDtypeStruct((rows, D), F32)] * (4 * n)
                 + [jax.ShapeDtypeStruct(a.shape, a.dtype) for a in extra],
                 input_output_aliases={3 * n + 1: 4 * n} if extra else {},
                 compiler_params=_cp(("arbitrary",)))(*ws, *ms, *vs, parts, *extra)
    groups = [res[4 * k:4 * k + 4] for k in range(n)]
    return groups if carry is None else (groups, res[4 * n])


def _step(x, tgt, shards, norm_mix_g, b_in, sinks, logits, hgrn_norm_g, norm_ffn_g, norm_final_g):
    t = x.shape[0]
    core = lax.axis_index("c").astype(jnp.int32).reshape(1)

    u1, (win_t,) = _rms_fwd(x, norm_mix_g, tm=512, name="rms_mix", comm=_gather_comm(shards[0:1], (0.2, 0.4, 0.6, 0.8)))
    (q, kv, h3, hf, gates), (wg_t, wba, wbh, wout) = _inproj_fwd(
        u1, win_t, b_in, t=t, comm=_gather_comm([shards[1]] + shards[4:7], (0.3, 0.5, 0.7, 0.9)))
    (y_attn,), _ = _attn_fwd(q, kv, sinks, t=t)
    (y_hgrn, o_pre, states), (wu_t, wd) = _hgrn_fwd(h3, hf, logits, hgrn_norm_g, t=t,
                                                    comm=_gather_comm(shards[2:4], (0.3, 0.5, 0.7, 0.9)))
    col = lambda j: j
    first, second = (lambda j: 0), (lambda j: 1)
    gate_tiles = [(gates, D, first), (gates, D, second)]

    def merge(prods, ex):
        (ya_, yb_), (ga, gb) = prods, ex
        sa, sb = _sig(ga.astype(F32)), _sig(gb.astype(F32))
        return sa, sb, ya_ * sa * (1.0 - sa), yb_ * sb * (1.0 - sb), sa * ya_ + sb * yb_

    sig_a, sig_b, dgate_a, dgate_b, merged = _fmm(
        [y_attn, y_hgrn], [(0, wba, False), (1, wbh, False)], gate_tiles, merge,
        [(BF, D, D, first)] * 5, m=t, n=D, tm=512, tn=D, name="branch_merge")
    def resid_norm(prods, ex):
        (p,), (xv, gv) = prods, ex
        hv = xv + p
        return hv, hv * lax.rsqrt(jnp.mean(hv * hv, axis=-1, keepdims=True) + EPS) * gv

    h1, u2 = _fmm([merged], [(0, wout, False)], [(x, D, first)], resid_norm, [(F32, D, D, first), (BF, D, D, first)],
                  m=t, n=D, tm=1024, tn=D, name="out_proj", vecs=[norm_ffn_g])

    def swiglu(prods, ex):
        g_, u_ = prods
        s = _sig(g_)
        silu = g_ * s
        return u_ * s * (1.0 + g_ * (1.0 - s)), silu, silu * u_

    dz_dgate, dz_dup, z = _fmm([u2], [(0, wg_t, True), (0, wu_t, True)], [], swiglu,
                               [(BF, FFN, FFN // 2, col)] * 3, m=t, n=FFN, tm=1024, tn=FFN // 2,
                               name="ffn_gate_up")
    def loss_head(prods, ex):
        (p,), (hv, tv, gv) = prods, ex
        hv = hv + p
        r = lax.rsqrt(jnp.mean(hv * hv, axis=-1, keepdims=True) + EPS)
        xh = hv * r
        err = xh * gv - tv
        lp = jnp.sum(jnp.sum(err * err, axis=1, keepdims=True), axis=0, keepdims=True) * (0.5 / D)
        dy = err * (1.0 / D)
        dxh = dy * gv
        dh = r * (dxh - xh * jnp.mean(dxh * xh, axis=-1, keepdims=True))
        return dh, dh, jnp.sum(dy * xh, axis=0, keepdims=True), jnp.broadcast_to(lp, (1, 128))

    dh2, dh2_b, d_norm_final, loss_row = _fmm(
        [z], [(0, wd, False)], [(h1, D, first), (tgt, D, first)], loss_head, [(F32, D, D, first), (BF, D, D, first)],
        m=t, n=D, tm=512, tn=D, name="ffn_down_loss", vecs=[norm_final_g], sums=[D, 128])

    def swiglu_bwd(prods, ex):
        (dz,), (da_, db_) = prods, ex
        return dz * da_.astype(F32), dz * db_.astype(F32)

    ffn_tiles = [(dz_dgate, FFN // 2, col), (dz_dup, FFN // 2, col)]
    dgt, dup = _fmm([dh2_b], [(0, wd, True)], ffn_tiles, swiglu_bwd, [(BF, FFN, FFN // 2, col)] * 2,
                    m=t, n=FFN, tm=1024, tn=FFN // 2, name="d_gate_up")
    (d_wd,) = _wgrad([z], dh2_b, name="d_w_down")
    (du2,) = _fmm([dgt, dup], [(0, wg_t, False), (1, wu_t, False)], [], lambda prods, ex: (prods[0] + prods[1],),
                  [(F32, D, 512, col)], m=t, n=D, tm=1024, tn=512, name="d_u2")
    d_wg, d_wu = _wgrad([dgt, dup], u2, name="d_w_gate_up")
    dh1, dh1_b, d_norm_ffn = _rms_bwd(du2, h1, norm_ffn_g, dh2, tm=512, name="rms_ffn_bwd")
    (d_wout,) = _wgrad([merged], dh1_b, name="d_w_out")

    def merge_bwd(prods, ex):
        (dm,), (sa, sb, ca, cb, wa, wb) = prods, ex
        dgate = jnp.concatenate([dm * ca.astype(F32), dm * cb.astype(F32)], axis=1)
        dya_ = (dm * sa.astype(F32)).astype(BF)
        dyb_ = (dm * sb.astype(F32)).astype(BF)
        return (dya_, dyb_, dgate, lax.dot_general(dya_, wa, _NT, preferred_element_type=F32),
                lax.dot_general(dyb_, wb, _NT, preferred_element_type=F32))

    ffn_grads = (d_wg, d_wu, d_wd)
    (dya, dyb, dgates, dy_attn, dy_hgrn), got = _fmm(
        [dh1_b], [(0, wout, True)], [(a, D, first) for a in (sig_a, sig_b, dgate_a, dgate_b)], merge_bwd,
        [(BF, D, D, first), (BF, D, D, first), (BF, 2 * D, 2 * D, first), (BF, D, D, first), (F32, D, D, first)],
        m=t, n=D, tm=512, tn=D, name="d_merge", consts=[wba, wbh], comm=_pair_comm(ffn_grads))
    pair_ffn = _pair_add(ffn_grads, got, core, name="pair_add_ffn")
    (d_wba,) = _wgrad([y_attn], dya, name="d_w_ba")
    (d_wbh,) = _wgrad([y_hgrn], dyb, name="d_w_bh")
    sq_grads = (d_wba, d_wbh, d_wout)
    (dh4, d_logits, d_hgrn_norm), (parts_ffn, *got) = _hgrn_bwd(
        h3, hf, logits, hgrn_norm_g, o_pre, states, dy_hgrn, t=t,
        comm=_both(_chip_comm(pair_ffn), _pair_comm(sq_grads)))
    pair_sq = _pair_add(sq_grads, got, core, name="pair_add_sq")
    (dq, dkv, d_sinks), (parts_sq,) = _attn_bwd(q, kv, sinks, dy_attn, t=t, comm=_chip_comm(pair_sq))
    dps = (dq, dkv, dh4, dgates)
    d_win_t, d_b_in = _inproj_bwd_w(dps, u1, t=t)
    half0, got_in = _inproj_bwd_x(dps, win_t, x, norm_mix_g, dh1, t=t, part=0, comm=_pair_comm([d_win_t]))
    pair_in = _pair_add([d_win_t], got_in, core, name="pair_add_w_in")
    (grad_x, d_norm_mix), (parts_in,) = _inproj_bwd_x(dps, win_t, x, norm_mix_g, dh1, t=t, part=1, prev=half0,
                                                      comm=_chip_comm(pair_in))

    small_grads = (d_norm_mix, d_b_in, d_sinks, d_logits, d_hgrn_norm, d_norm_ffn, d_norm_final)
    return loss_row, grad_x, (parts_in, parts_ffn, parts_sq), small_grads


def kernel(x, norm_mix_g, w_in, b_in, attn_sinks, hgrn_lb_logits, hgrn_norm_g, w_branch_attn, w_branch_hgrn, w_out, norm_ffn_g, w_ffn_gate, w_ffn_up, w_ffn_down, norm_final_g, loss_target, m_norm_mix_g, m_w_in, m_b_in, m_attn_sinks, m_hgrn_lb_logits, m_hgrn_norm_g, m_w_branch_attn, m_w_branch_hgrn, m_w_out, m_norm_ffn_g, m_w_ffn_gate, m_w_ffn_up, m_w_ffn_down, m_norm_final_g, v_norm_mix_g, v_w_in, v_b_in, v_attn_sinks, v_hgrn_lb_logits, v_hgrn_norm_g, v_w_branch_attn, v_w_branch_hgrn, v_w_out, v_norm_ffn_g, v_w_ffn_gate, v_w_ffn_up, v_w_ffn_down, v_norm_final_g):
    shards = [w_in[0].T.astype(BF), w_ffn_gate[0].T.astype(BF), w_ffn_up[0].T.astype(BF),
              w_ffn_down[0].astype(BF), w_branch_attn[0].astype(BF), w_branch_hgrn[0].astype(BF),
              w_out[0].astype(BF)]
    loss_row, grad_x, grad_parts, small_grads = _step(
        x[0], loss_target[0], shards, norm_mix_g, b_in, attn_sinks, hgrn_lb_logits, hgrn_norm_g,
        norm_ffn_g, norm_final_g.reshape(1, D))

    d_norm_mix, d_b_in, d_sinks, d_logits, d_hgrn_norm, d_norm_ffn, d_norm_final = small_grads
    row = lambda a: a.reshape(1, D)
    loss_out, small = _small_allreduce_adam(
        dict(norm_mix_g=d_norm_mix, hgrn_norm_g=d_hgrn_norm, norm_ffn_g=d_norm_ffn, norm_final_g=d_norm_final,
             hgrn_lb_logits=d_logits, attn_sinks=d_sinks, b_in=d_b_in),
        loss_row,
        dict(norm_mix_g=(norm_mix_g, m_norm_mix_g, v_norm_mix_g), hgrn_norm_g=(hgrn_norm_g, m_hgrn_norm_g, v_hgrn_norm_g),
             norm_ffn_g=(norm_ffn_g, m_norm_ffn_g, v_norm_ffn_g),
             norm_final_g=(row(norm_final_g), row(m_norm_final_g), row(v_norm_final_g)),
             hgrn_lb_logits=(hgrn_lb_logits, m_hgrn_lb_logits, v_hgrn_lb_logits),
             attn_sinks=(attn_sinks, m_attn_sinks, v_attn_sinks), b_in=(b_in, m_b_in, v_b_in)))
    small["norm_final_g"] = [a.reshape(D) for a in small["norm_final_g"]]
    loss = loss_out[0, 0]

    names = ["w_in", "w_ffn_gate", "w_ffn_up", "w_ffn_down", "w_branch_attn", "w_branch_hgrn", "w_out"]
    w_full = dict(w_in=(w_in, m_w_in, v_w_in), w_ffn_gate=(w_ffn_gate, m_w_ffn_gate, v_w_ffn_gate),
                  w_ffn_up=(w_ffn_up, m_w_ffn_up, v_w_ffn_up), w_ffn_down=(w_ffn_down, m_w_ffn_down, v_w_ffn_down),
                  w_branch_attn=(w_branch_attn, m_w_branch_attn, v_w_branch_attn),
                  w_branch_hgrn=(w_branch_hgrn, m_w_branch_hgrn, v_w_branch_hgrn),
                  w_out=(w_out, m_w_out, v_w_out))
    big = {}
    for group, parts, tag in zip((names[0:1], names[1:4], names[4:7]), grad_parts, ("w_in", "ffn", "square")):
        flip = [name in names[0:3] for name in group]
        view = lambda a, f: a[0].T if f else a[0]
        cols = [[view(w_full[name][j], f) for name, f in zip(group, flip)] for j in range(3)]
        if tag == "w_in":
            res, grad_x = _adam(cols[0], parts, cols[1], cols[2], name="adam_" + tag, carry=grad_x)
        else:
            res = _adam(cols[0], parts, cols[1], cols[2], name="adam_" + tag)
        for name, f, r in zip(group, flip, res):
            big[name] = [a.T[None] if f else a[None] for a in r]

    order = ["norm_mix_g", "w_in", "b_in", "attn_sinks", "hgrn_lb_logits", "hgrn_norm_g", "w_branch_attn",
             "w_branch_hgrn", "w_out", "norm_ffn_g", "w_ffn_gate", "w_ffn_up", "w_ffn_down", "norm_final_g"]
    outs = [loss, grad_x[None]]
    for kind in range(4):
        for name in order:
            outs.append(big[name][kind] if name in big else small[name][kind])
    return tuple(outs)
```

```python
import math

import jax
import jax.numpy as jnp
from jax import lax
from jax.experimental import pallas as pl
from jax.experimental.pallas import tpu as pltpu

F32 = jnp.float32
BF = jnp.bfloat16
MESH = pl.DeviceIdType.MESH

D = 1024
HEAD = 64
N_PAIR = 8
BLK = 128
CH = 64
HG_SUB = 4
HG_HEADS = 8
HG_K = 128
FFN = 2816
IN_W = 7424
N_DEV = 8
N_CHIP = 4
EPS = 1e-6
NEG = -1e30
SCALE = 1.0 / math.sqrt(HEAD)
VMEM_LIMIT = 56 * 1024 * 1024
WT = 256

ADAM_LR, ADAM_B1, ADAM_B2, ADAM_EPS, ADAM_WD, ADAM_STEP = 0.001, 0.9, 0.999, 1e-08, 0.01, 10

GRP_OFF = (0, D // WT, (D + 256) // WT, (5 * D + 256) // WT)
GRP_N = (D // WT, 256 // WT, 4 * D // WT, 2 * D // WT)
SMALL_ROWS = 16


_NN = (((1,), (0,)), ((), ()))
_NT = (((1,), (1,)), ((), ()))
_TN = (((0,), (0,)), ((), ()))


def _pcall(body, **kw):
    return pl.pallas_call(body, **kw)


def _cp(sem=None, **kw):
    return pltpu.CompilerParams(dimension_semantics=sem, vmem_limit_bytes=VMEM_LIMIT, **kw)


def _sig(v):
    return 0.5 * jnp.tanh(0.5 * v) + 0.5


def _accum(ref, val, first):
    @pl.when(first)
    def _():
        ref[...] = val

    @pl.when(jnp.logical_not(first))
    def _():
        ref[...] += val


class _Comm:
    def __init__(self, ins, out_shapes, sem_shapes, phases):
        self.ins, self.out_shapes, self.sem_shapes, self.phases = list(ins), list(out_shapes), list(sem_shapes), phases


def _both(a, b):
    ni, no, ns = len(a.ins), len(a.out_shapes), len(a.sem_shapes)

    def of_a(fn):
        return lambda ins, outs, sems: fn(ins[:ni], outs[:no], sems[:ns])

    def of_b(fn):
        return lambda ins, outs, sems: fn(ins[ni:], outs[no:], sems[ns:])

    return _Comm(a.ins + b.ins, a.out_shapes + b.out_shapes, a.sem_shapes + b.sem_shapes,
                 [(f, of_a(fn)) for f, fn in a.phases] + [(f, of_b(fn)) for f, fn in b.phases])


def _host(body, comm, n_in, n_out, n_scr, nsteps, step_fn):
    if comm is None:
        return body
    ci, co = len(comm.ins), len(comm.out_shapes)

    def wrapped(*refs):
        p = 0
        ins, p = refs[p:p + n_in], p + n_in
        cins, p = refs[p:p + ci], p + ci
        outs, p = refs[p:p + n_out], p + n_out
        couts, p = refs[p:p + co], p + co
        scr, p = refs[p:p + n_scr], p + n_scr
        csems = refs[p:]
        step = step_fn()
        for frac, fn in comm.phases:
            if frac < 1.0:
                @pl.when(step == int(round(frac * (nsteps - 1))))
                def _(fn=fn):
                    fn(cins, couts, csems)
        body(*ins, *outs, *scr)
        for frac, fn in comm.phases:
            if frac >= 1.0:
                @pl.when(step == nsteps - 1)
                def _(fn=fn):
                    fn(cins, couts, csems)

    return wrapped


def _hosted_call(body, comm, args, *, name, grid, in_specs, out_specs, out_shape, scratch_shapes, sem,
                 nsteps, step_fn, aliases=None):
    n_in, n_out, n_scr = len(in_specs), len(out_specs), len(scratch_shapes)
    args = list(args)
    extra = {}
    if comm is not None:
        in_specs = list(in_specs) + [_hbm_spec()] * len(comm.ins)
        out_specs = list(out_specs) + [_hbm_spec()] * len(comm.out_shapes)
        out_shape = list(out_shape) + comm.out_shapes
        scratch_shapes = list(scratch_shapes) + comm.sem_shapes
        args += comm.ins
        extra = dict(has_side_effects=True)
    outs = _pcall(_host(body, comm, n_in, n_out, n_scr, nsteps, step_fn), name=name, grid=grid,
                  in_specs=in_specs, out_specs=out_specs, out_shape=out_shape, scratch_shapes=scratch_shapes,
                  input_output_aliases=aliases or {}, compiler_params=_cp(sem, **extra))(*args)
    return list(outs[:n_out]), list(outs[n_out:])


def _hbm_spec():
    return pl.BlockSpec(memory_space=pl.ANY)


def _wgrad(a_list, b, *, name):
    (t, m), n, gm = a_list[0].shape, b.shape[1], a_list[0].shape[1] // WT
    n_a = len(a_list)
    tile = lambda k: (lambda s: jnp.clip(s - k * gm, 0, gm - 1))

    def body(*refs):
        a_refs, b_ref, o_refs = refs[:n_a], refs[n_a], refs[n_a + 1:]
        s = pl.program_id(0)
        for k in range(n_a):
            @pl.when(jnp.logical_and(s >= k * gm, s < (k + 1) * gm))
            def _(k=k):
                o_refs[k][...] = lax.dot_general(a_refs[k][...], b_ref[...], _TN,
                                                 preferred_element_type=F32).astype(BF)

    return _pcall(body, name=name, grid=(n_a * gm,),
                  in_specs=[pl.BlockSpec((t, WT), lambda s, k=k: (0, tile(k)(s))) for k in range(n_a)]
                  + [pl.BlockSpec((t, n), lambda s: (0, 0))],
                  out_specs=[pl.BlockSpec((WT, n), lambda s, k=k: (tile(k)(s), 0)) for k in range(n_a)],
                  out_shape=[jax.ShapeDtypeStruct((m, n), BF)] * n_a,
                  compiler_params=_cp(("arbitrary",)))(*a_list, b)


def _fmm(lhs, rhs, extras, epilogue, outs, *, m, n, tm, tn, name, comm=None, vecs=(), consts=(), sums=()):
    tm, tn = min(tm, m), min(tn, n)
    assert m % tm == 0 and n % tn == 0 and (not sums or tn == n), (name, m, n, tm, tn)
    in_specs, args = [], []
    for a in lhs:
        in_specs.append(pl.BlockSpec((tm, a.shape[1]), lambda i, j: (i, 0)))
        args.append(a)
    for li, b, tb in rhs:
        k = lhs[li].shape[1]
        in_specs.append(pl.BlockSpec((tn, k), lambda i, j: (j, 0)) if tb
                        else pl.BlockSpec((k, tn), lambda i, j: (0, j)))
        args.append(b)
    for arr, w, col in extras:
        in_specs.append(pl.BlockSpec((tm, w), lambda i, j, col=col: (i, col(j))))
        args.append(arr)
    for vec in vecs:
        in_specs.append(pl.BlockSpec((1, tn), lambda i, j: (0, j)))
        args.append(vec)
    for whole in consts:
        in_specs.append(pl.BlockSpec(whole.shape, lambda i, j: (0, 0)))
        args.append(whole)
    out_specs = [pl.BlockSpec((tm, w), lambda i, j, col=col: (i, col(j))) for _, _, w, col in outs]
    out_shape = [jax.ShapeDtypeStruct((m, total), dt) for dt, total, _, _ in outs]
    for w in sums:
        out_specs.append(pl.BlockSpec((1, w), lambda i, j: (0, 0)))
        out_shape.append(jax.ShapeDtypeStruct((1, w), F32))
    nl, nr, ne, no = len(lhs), len(rhs), len(extras) + len(vecs) + len(consts), len(outs)

    def body(*refs):
        prods = []
        for r, (li, _, tb) in enumerate(rhs):
            prods.append(lax.dot_general(refs[li][...], refs[nl + r][...], _NT if tb else _NN,
                                         preferred_element_type=F32))
        vals = epilogue(prods, [ref[...] for ref in refs[nl + nr:nl + nr + ne]])
        o_refs = refs[nl + nr + ne:]
        for o_ref, v in zip(o_refs[:no], vals[:no]):
            o_ref[...] = v.astype(o_ref.dtype)
        for s_ref, v in zip(o_refs[no:], vals[no:]):
            _accum(s_ref, v, pl.program_id(0) == 0)

    gm, gn = m // tm, n // tn
    res, comm_res = _hosted_call(
        body, comm, args, name=name, grid=(gm, gn), in_specs=in_specs, out_specs=out_specs,
        out_shape=out_shape, scratch_shapes=[], sem=("arbitrary", "arbitrary"), nsteps=gm * gn,
        step_fn=lambda: pl.program_id(0) * gn + pl.program_id(1))
    return res if comm is None else (res, comm_res)


def _grp_of(i):
    return [jnp.logical_and(i >= GRP_OFF[g], i < GRP_OFF[g] + GRP_N[g]) for g in range(4)]


def _grp_idx(i, g):
    return jnp.clip(i - GRP_OFF[g], 0, GRP_N[g] - 1)


def _inproj_fwd(u, win_t, b_in, *, t, comm=None):
    tm = min(1024, t)
    n_row = t // tm
    n_chunks, h_first, g_first = 8, 2, 6
    sub = D // WT

    def w_block(l):
        return jnp.where(l == 0, GRP_OFF[0], jnp.where(l == 1, GRP_OFF[1], GRP_OFF[2] + sub * (l - h_first)))

    def body(u_ref, *rest):
        w_refs, b_refs, (q_ref, kv_ref, h3_ref, hf_ref, g_ref) = rest[:sub], rest[sub:2 * sub], rest[2 * sub:]
        l = pl.program_id(1)

        @pl.when(l == 1)
        def _():
            kv_ref[...] = (lax.dot_general(u_ref[...], w_refs[0][...], _NT, preferred_element_type=F32)
                           + b_refs[0][...]).astype(BF)

        is_hf = l == h_first + 1
        in_h3 = jnp.logical_and(jnp.logical_and(l >= h_first, l < g_first), jnp.logical_not(is_hf))
        for pred, o_ref in ((l == 0, q_ref), (in_h3, h3_ref), (is_hf, hf_ref), (l >= g_first, g_ref)):
            @pl.when(pred)
            def _(o_ref=o_ref):
                w = jnp.concatenate([w[...] for w in w_refs], axis=0)
                b = jnp.concatenate([b[...] for b in b_refs], axis=1)
                o_ref[...] = (lax.dot_general(u_ref[...], w, _NT, preferred_element_type=F32) + b).astype(o_ref.dtype)

    return _hosted_call(
        body, comm, [u] + [win_t] * sub + [b_in] * sub, name="inproj_fwd", grid=(n_row, n_chunks),
        in_specs=[pl.BlockSpec((tm, D), lambda i, l: (i, 0))]
        + [pl.BlockSpec((WT, D), lambda i, l, o=o: (w_block(l) + o, 0)) for o in range(sub)]
        + [pl.BlockSpec((1, WT), lambda i, l, o=o: (0, w_block(l) + o)) for o in range(sub)],
        out_specs=[pl.BlockSpec((tm, D), lambda i, l: (i, 0)),
                   pl.BlockSpec((tm, 256), lambda i, l: (i, 0)),
                   pl.BlockSpec((tm, D), lambda i, l: (i, jnp.clip(l - h_first - 1, 0, 2))),
                   pl.BlockSpec((tm, D), lambda i, l: (i, 0)),
                   pl.BlockSpec((tm, D), lambda i, l: (i, jnp.clip(l - g_first, 0, 1)))],
        out_shape=[jax.ShapeDtypeStruct((t, D), BF), jax.ShapeDtypeStruct((t, 256), BF),
                   jax.ShapeDtypeStruct((t, 3 * D), BF), jax.ShapeDtypeStruct((t, D), F32),
                   jax.ShapeDtypeStruct((t, 2 * D), BF)],
        scratch_shapes=[], sem=("arbitrary", "arbitrary"), nsteps=n_row * n_chunks,
        step_fn=lambda: pl.program_id(0) * n_chunks + pl.program_id(1))


def _inproj_bwd_x(dps, win_t, x, g, resid, *, t, part, prev=None, comm=None):
    n_row = 8 if t >= 4096 else 4
    tm = t // n_row
    first = 1
    per = first if part == 0 else n_row - first
    row = lambda i: part * first + i

    n_chunks = 4
    sub = 2 * D // WT

    def w_block(l):
        return jnp.where(l == 0, 0, GRP_OFF[2] + sub * (l - 1))

    def body(d0, d1, d2, d3, *rest):
        w_refs, (x_ref, g_ref, r_ref) = rest[:sub], rest[sub:sub + 3]
        dg_prev = rest[sub + 3] if prev is not None else None
        o_ref, dg_ref, acc_ref = rest[-3], rest[-2], rest[-1]
        i, l = pl.program_id(0), pl.program_id(1)

        @pl.when(l == 0)
        def _():
            wq = jnp.concatenate([w[...] for w in w_refs[:GRP_N[0]]], axis=0)
            acc_ref[...] = (jnp.dot(d0[...], wq, preferred_element_type=F32)
                            + jnp.dot(d1[...], w_refs[GRP_N[0]][...], preferred_element_type=F32))

        for pred, d_ref in ((jnp.logical_and(l >= 1, l < 3), d2), (l == 3, d3)):
            @pl.when(pred)
            def _(d_ref=d_ref):
                w = jnp.concatenate([w[...] for w in w_refs], axis=0)
                acc_ref[...] += jnp.dot(d_ref[...], w, preferred_element_type=F32)

        @pl.when(l == n_chunks - 1)
        def _():
            xv = x_ref[...]
            r = lax.rsqrt(jnp.mean(xv * xv, axis=-1, keepdims=True) + EPS)
            xh = xv * r
            du = acc_ref[...]
            dxh = du * g_ref[...]
            o_ref[...] = r_ref[...] + r * (dxh - xh * jnp.mean(dxh * xh, axis=-1, keepdims=True))
            dg = jnp.sum(du * xh, axis=0, keepdims=True)
            if dg_prev is not None:
                dg = dg + jnp.where(i == 0, 1.0, 0.0) * dg_prev[...]
            _accum(dg_ref, dg, i == 0)

    rows = lambda w: pl.BlockSpec((tm, w), lambda i, l: (row(i), 0))
    in_specs = ([rows(D), rows(256),
                 pl.BlockSpec((tm, 2 * D), lambda i, l: (row(i), jnp.clip(l - 1, 0, 1))), rows(2 * D)]
                + [pl.BlockSpec((WT, D), lambda i, l, o=o: (w_block(l) + o, 0)) for o in range(sub)]
                + [rows(D), pl.BlockSpec((1, D), lambda i, l: (0, 0)), rows(D)])
    args = list(dps) + [win_t] * sub + [x, g, resid]
    aliases = None
    if prev is not None:
        in_specs += [pl.BlockSpec((1, D), lambda i, l: (0, 0)), _hbm_spec()]
        args += [prev[1], prev[0]]
        aliases = {len(args) - 1: 0}
    return _hosted_call(
        body, comm, args, name="inproj_bwd_x%d" % part, grid=(per, n_chunks), in_specs=in_specs,
        out_specs=[rows(D), pl.BlockSpec((1, D), lambda i, l: (0, 0))],
        out_shape=[jax.ShapeDtypeStruct((t, D), F32), jax.ShapeDtypeStruct((1, D), F32)],
        scratch_shapes=[pltpu.VMEM((tm, D), F32)], sem=("arbitrary", "arbitrary"), nsteps=per * n_chunks,
        step_fn=lambda: pl.program_id(0) * n_chunks + pl.program_id(1), aliases=aliases)


def _inproj_bwd_w(dps, u, *, t):
    n_tiles = IN_W // WT
    dims = (((0,), (0,)), ((), ()))

    def body(d0, d1, d2, d3, u_ref, o_ref, db_ref):
        i = pl.program_id(0)
        uv = u_ref[...]
        for g, (pred, d_ref) in enumerate(zip(_grp_of(i), (d0, d1, d2, d3))):
            @pl.when(pred)
            def _(d_ref=d_ref):
                dv = d_ref[...]
                o_ref[...] = lax.dot_general(dv, uv, dims, preferred_element_type=F32).astype(BF)
                db_ref[...] = jnp.sum(dv.astype(F32), axis=0, keepdims=True)

    return _pcall(body, name="inproj_bwd_w", grid=(n_tiles,),
                  in_specs=[pl.BlockSpec((t, WT), lambda i, g=g: (0, _grp_idx(i, g))) for g in range(4)]
                  + [pl.BlockSpec((t, D), lambda i: (0, 0))],
                  out_specs=[pl.BlockSpec((WT, D), lambda i: (i, 0)),
                             pl.BlockSpec((1, WT), lambda i: (0, i))],
                  out_shape=[jax.ShapeDtypeStruct((IN_W, D), BF), jax.ShapeDtypeStruct((1, IN_W), F32)],
                  compiler_params=_cp(("arbitrary",)))(*dps, u)


def _row_spec(tm, width, col=0):
    return pl.BlockSpec((tm, width), lambda i: (i, col))


def _vec_spec(width):
    return pl.BlockSpec((1, width), lambda i: (0, 0))


def _rms_fwd(x, g, *, tm, name, comm=None):
    t = x.shape[0]
    tm = min(tm, t)

    def body(x_ref, g_ref, u_ref):
        xv = x_ref[...]
        r = lax.rsqrt(jnp.mean(xv * xv, axis=-1, keepdims=True) + EPS)
        u_ref[...] = (xv * r * g_ref[...]).astype(BF)

    (u,), comm_res = _hosted_call(
        body, comm, (x, g), name=name, grid=(t // tm,), in_specs=[_row_spec(tm, D), _vec_spec(D)],
        out_specs=[_row_spec(tm, D)], out_shape=[jax.ShapeDtypeStruct((t, D), BF)], scratch_shapes=[],
        sem=("arbitrary",), nsteps=t // tm, step_fn=lambda: pl.program_id(0))
    return u if comm is None else (u, comm_res)


def _rms_bwd(du, x, g, resid, *, tm, name):
    t = x.shape[0]
    tm = min(tm, t)

    def body(du_ref, x_ref, g_ref, r_ref, dx_ref, dxb_ref, dg_ref):
        xv = x_ref[...]
        r = lax.rsqrt(jnp.mean(xv * xv, axis=-1, keepdims=True) + EPS)
        xh = xv * r
        duv = du_ref[...]
        dxh = duv * g_ref[...]
        dx = r_ref[...] + r * (dxh - xh * jnp.mean(dxh * xh, axis=-1, keepdims=True))
        dx_ref[...] = dx
        dxb_ref[...] = dx.astype(BF)
        _accum(dg_ref, jnp.sum(duv * xh, axis=0, keepdims=True), pl.program_id(0) == 0)

    return _pcall(body, name=name, grid=(t // tm,),
                  in_specs=[_row_spec(tm, D), _row_spec(tm, D), _vec_spec(D), _row_spec(tm, D)],
                  out_specs=[_row_spec(tm, D), _row_spec(tm, D), _vec_spec(D)],
                  out_shape=[jax.ShapeDtypeStruct((t, D), F32), jax.ShapeDtypeStruct((t, D), BF),
                             jax.ShapeDtypeStruct((1, D), F32)],
                  compiler_params=_cp(("arbitrary",)))(du, x, g, resid)


def _attn_kv_tiles(kprev, kcur):
    kv = jnp.concatenate([kprev, kcur], axis=0).astype(F32)
    lo = lax.broadcasted_iota(jnp.int32, (2 * BLK, 128), 1) < HEAD
    tiles = []
    for part in (kv[:, 0:128], kv[:, 128:256]):
        rolled = pltpu.roll(part, HEAD, 1)
        z = jnp.zeros_like(part)
        tiles.append(((jnp.where(lo, part, z).astype(BF), jnp.where(lo, z, rolled).astype(BF)),
                      (jnp.where(lo, rolled, z).astype(BF), jnp.where(lo, z, part).astype(BF))))
    k_t, v_t = tiles
    return [(jnp.concatenate(k_t[h], axis=0), jnp.concatenate(v_t[h], axis=0)) for h in range(2)]


def _attn_mask(i):
    qi = lax.broadcasted_iota(jnp.int32, (BLK, 2 * BLK), 0)
    kj = lax.broadcasted_iota(jnp.int32, (BLK, 2 * BLK), 1)
    first_key = jnp.where(i == 0, BLK, 0)
    in_prev = jnp.logical_and(jnp.logical_and(kj < BLK, kj > qi), kj >= first_key)
    in_cur = jnp.logical_and(kj >= BLK, kj - BLK <= qi)
    return jnp.logical_or(in_prev, in_cur)


def _attn_probs(s, sink, valid):
    s = jnp.where(valid, s * SCALE, NEG)
    mx = jnp.maximum(jnp.max(s, axis=-1, keepdims=True), sink)
    e = jnp.exp(s - mx)
    es = jnp.exp(sink - mx)
    inv = 1.0 / (jnp.sum(e, axis=-1, keepdims=True) + es)
    return e * inv, es * inv


_KEYS = 2 * BLK


def _pair(ref, j):
    return ref[:, j * 128:(j + 1) * 128]


def _attn_fwd(q, kv, sinks, *, t, comm=None):
    nb = t // BLK

    def body(sink_ref, q_ref, kp_ref, kc_ref, o_ref):
        i = pl.program_id(0)
        for c in range(2):
            rows = slice(c * BLK, (c + 1) * BLK)
            valid = _attn_mask(2 * i + c)
            tiles = _attn_kv_tiles(kp_ref[...] if c == 0 else kc_ref[0:BLK, :], kc_ref[rows, :])
            s = [lax.dot_general(q_ref[rows, j * 128:(j + 1) * 128], tiles[j // 4][0], _NT,
                                 preferred_element_type=F32) for j in range(N_PAIR)]
            p = []
            for j in range(N_PAIR):
                pe, _ = _attn_probs(s[j][:, 0:_KEYS], sink_ref[0, 2 * j], valid)
                po, _ = _attn_probs(s[j][:, _KEYS:2 * _KEYS], sink_ref[0, 2 * j + 1], valid)
                p.append(jnp.concatenate([pe.astype(BF), po.astype(BF)], axis=1))
            for j in range(N_PAIR):
                o_ref[rows, j * 128:(j + 1) * 128] = jnp.dot(p[j], tiles[j // 4][1],
                                                             preferred_element_type=F32).astype(BF)

    return _hosted_call(
        body, comm, (sinks, q, kv, kv), name="attn_fwd", grid=(nb // 2,),
        in_specs=[pl.BlockSpec(memory_space=pltpu.SMEM),
                  pl.BlockSpec((2 * BLK, D), lambda i: (i, 0)),
                  pl.BlockSpec((BLK, 256), lambda i: (jnp.maximum(2 * i - 1, 0), 0)),
                  pl.BlockSpec((2 * BLK, 256), lambda i: (i, 0))],
        out_specs=[pl.BlockSpec((2 * BLK, D), lambda i: (i, 0))],
        out_shape=[jax.ShapeDtypeStruct((t, D), BF)],
        scratch_shapes=[], sem=("arbitrary",), nsteps=nb // 2, step_fn=lambda: pl.program_id(0))


def _attn_bwd(q, kv, sinks, do, *, t, comm=None):
    nb = t // BLK
    last = nb - 1

    def body(sink_ref, q_ref, kp_ref, kc_ref, do_ref, dq_ref, dkv_ref, ds_ref, carry_ref):
        i = pl.program_id(0)

        @pl.when(i == 0)
        def _():
            ds_ref[...] = jnp.zeros_like(ds_ref)
            carry_ref[...] = jnp.zeros_like(carry_ref)

        @pl.when(i < nb)
        def _():
            valid = _attn_mask(i)
            tiles = _attn_kv_tiles(kp_ref[...], kc_ref[...])
            lane1 = lax.broadcasted_iota(jnp.int32, (1, 128), 1)
            dsink = jnp.zeros((1, 128), F32)
            s = [lax.dot_general(_pair(q_ref, j), tiles[j // 4][0], _NT, preferred_element_type=F32)
                 for j in range(N_PAIR)]
            dp = [lax.dot_general(_pair(do_ref, j), tiles[j // 4][1], _NT, preferred_element_type=F32)
                  for j in range(N_PAIR)]
            p_all, ds_all = [], []
            for j in range(N_PAIR):
                halves = []
                for par in range(2):
                    cols = slice(par * _KEYS, (par + 1) * _KEYS)
                    p, ps = _attn_probs(s[j][:, cols], sink_ref[0, 2 * j + par], valid)
                    dpj = dp[j][:, cols]
                    dd = jnp.sum(p * dpj, axis=-1, keepdims=True)
                    dsink = dsink + jnp.where(lane1 == 2 * j + par,
                                              -jnp.sum(ps * dd, axis=0, keepdims=True), 0.0)
                    halves.append((p.astype(BF), (p * (dpj - dd)).astype(BF)))
                p_all.append(jnp.concatenate([halves[0][0], halves[1][0]], axis=1))
                ds_all.append(jnp.concatenate([halves[0][1], halves[1][1]], axis=1))
            for j in range(N_PAIR):
                dq_ref[:, j * 128:(j + 1) * 128] = (
                    jnp.dot(ds_all[j], tiles[j // 4][0], preferred_element_type=F32) * SCALE).astype(BF)
            ds_ref[...] += dsink
            gk, gv = [], []
            for h in range(2):
                grp = range(4 * h, 4 * h + 4)
                q_rows = jnp.concatenate([_pair(q_ref, j) for j in grp], axis=0)
                do_rows = jnp.concatenate([_pair(do_ref, j) for j in grp], axis=0)
                g_k = lax.dot_general(jnp.concatenate([ds_all[j] for j in grp], axis=0), q_rows, _TN,
                                      preferred_element_type=F32)
                g_v = lax.dot_general(jnp.concatenate([p_all[j] for j in grp], axis=0), do_rows, _TN,
                                      preferred_element_type=F32)
                gk.append((g_k[0:_KEYS], g_k[_KEYS:2 * _KEYS]))
                gv.append((g_v[0:_KEYS], g_v[_KEYS:2 * _KEYS]))
            lo = lax.broadcasted_iota(jnp.int32, (2 * BLK, 128), 1) < HEAD
            zero = jnp.zeros((2 * BLK, 128), F32)

            def unpad(g):
                return (jnp.where(lo, g[0][0] + pltpu.roll(g[0][1], HEAD, 1), zero)
                        + jnp.where(lo, zero, pltpu.roll(g[1][0], HEAD, 1) + g[1][1]))

            dk = unpad(gk) * SCALE
            dv = unpad(gv)
            dkv_ref[:, 0:128] = (carry_ref[:, 0:128] + dk[0:BLK]).astype(BF)
            dkv_ref[:, 128:256] = (carry_ref[:, 128:256] + dv[0:BLK]).astype(BF)
            carry_ref[:, 0:128] = dk[BLK:2 * BLK]
            carry_ref[:, 128:256] = dv[BLK:2 * BLK]

        @pl.when(i == nb)
        def _():
            dkv_ref[...] = carry_ref[...].astype(BF)

    return _hosted_call(
        body, comm, (sinks, q, kv, kv, do), name="attn_bwd", grid=(nb + 1,),
        in_specs=[pl.BlockSpec(memory_space=pltpu.SMEM),
                  pl.BlockSpec((BLK, D), lambda i: (jnp.minimum(i, last), 0)),
                  pl.BlockSpec((BLK, 256), lambda i: (jnp.clip(i - 1, 0, last), 0)),
                  pl.BlockSpec((BLK, 256), lambda i: (jnp.minimum(i, last), 0)),
                  pl.BlockSpec((BLK, D), lambda i: (jnp.minimum(i, last), 0))],
        out_specs=[pl.BlockSpec((BLK, D), lambda i: (jnp.minimum(i, last), 0)),
                   pl.BlockSpec((BLK, 256), lambda i: (jnp.maximum(i - 1, 0), 0)),
                   pl.BlockSpec((1, 128), lambda i: (0, 0))],
        out_shape=[jax.ShapeDtypeStruct((t, D), BF), jax.ShapeDtypeStruct((t, 256), BF),
                   jax.ShapeDtypeStruct((1, 128), F32)],
        scratch_shapes=[pltpu.VMEM((BLK, 256), F32)], sem=("arbitrary",), nsteps=nb + 1,
        step_fn=lambda: pl.program_id(0))


def _split3(v):
    h = v.astype(BF)
    r = v - h.astype(F32)
    m = r.astype(BF)
    lo = (r - m.astype(F32)).astype(BF)
    return jnp.concatenate([h, m, lo], axis=1)


def _apply01(mat, v):
    n = v.shape[1]
    r = jnp.dot(mat, _split3(v), preferred_element_type=F32)
    return r[:, 0:n] + r[:, n:2 * n] + r[:, 2 * n:3 * n]


def _hgrn_gates(hq, hf, lb):
    sq = _sig(hq)
    sg = _sig(hf)
    f = lb + (1.0 - lb) * sg
    return hq * sq, (1.0 - lb) * (1.0 - sg), jnp.log(f), sq, sg, f


def _tri(upper):
    r = lax.broadcasted_iota(jnp.int32, (CH, CH), 0)
    c = lax.broadcasted_iota(jnp.int32, (CH, CH), 1)
    return (c >= r) if upper else (c <= r)


def _lb_from_logits(lg_ref):
    return 1.0 / (1.0 + jnp.exp(lg_ref[1:2, :] - lg_ref[0:1, :]))


def _hgrn_fwd(h3, hf, logits, norm_g, *, t, comm=None):
    nc = t // CH
    nt_dims = (((1,), (1,)), ((), ()))
    tn_dims = (((0,), (0,)), ((), ()))

    def body(h_ref, hf_ref, lg_ref, ng_ref, y_ref, o_ref, st_ref, s_scr, b_scr, qa_s, ka_s, qb_s, kb_s, v_s):
        @pl.when(pl.program_id(0) == 0)
        def _():
            s_scr[...] = jnp.zeros_like(s_scr)

        heads = [slice(h * HG_K, (h + 1) * HG_K) for h in range(HG_HEADS)]
        causal = _tri(False)
        lb = _lb_from_logits(lg_ref)
        for c in range(HG_SUB):
            rows = slice(c * CH, (c + 1) * CH)
            q, k, g, _, _, _ = _hgrn_gates(h_ref[rows, 0:D].astype(F32), hf_ref[rows, :], lb)
            b_scr[...] = _apply01(jnp.where(causal, 1.0, 0.0).astype(BF), g)
            b = b_scr[...]
            b_mid = b_scr[CH // 2 - 1:CH // 2, :]
            b_last = b_scr[CH - 1:CH, :]
            qa_s[...] = (q * jnp.exp(b - b_mid)).astype(BF)
            ka_s[...] = (k * jnp.exp(b_mid - b)).astype(BF)
            qb_s[...] = (q * jnp.exp(b)).astype(BF)
            kb_s[...] = (k * jnp.exp(b_last - b)).astype(BF)
            v_s[...] = h_ref[rows, D:2 * D]
            dec = jnp.exp(b_last)
            st_ref[c] = s_scr[...].astype(BF)
            a = [jnp.where(causal, lax.dot_general(qa_s[:, sl], ka_s[:, sl], nt_dims, preferred_element_type=F32),
                           0.0).astype(BF) for sl in heads]
            for h, sl in enumerate(heads):
                o_ref[rows, sl] = (jnp.dot(a[h], v_s[:, sl], preferred_element_type=F32)
                                   + lax.dot_general(qb_s[:, sl], s_scr[h].astype(BF), nt_dims,
                                                     preferred_element_type=F32))
            for h, sl in enumerate(heads):
                s_scr[h] = dec[:, sl] * s_scr[h] + lax.dot_general(v_s[:, sl], kb_s[:, sl], tn_dims,
                                                                   preferred_element_type=F32)
            for h, sl in enumerate(heads):
                o = o_ref[rows, sl]
                on = o * lax.rsqrt(jnp.mean(o * o, axis=-1, keepdims=True) + EPS)
                gate = _sig(h_ref[rows, 2 * D + h * HG_K:2 * D + (h + 1) * HG_K].astype(F32))
                y_ref[rows, sl] = (on * ng_ref[:, sl] * gate).astype(BF)

    half = lambda: pltpu.VMEM((CH, D), BF)
    blk = HG_SUB * CH
    return _hosted_call(
        body, comm, (h3, hf, logits, norm_g), name="hgrn_fwd", grid=(nc // HG_SUB,),
        in_specs=[pl.BlockSpec((blk, 3 * D), lambda n: (n, 0)),
                  pl.BlockSpec((blk, D), lambda n: (n, 0)),
                  pl.BlockSpec((2, D), lambda n: (0, 0)),
                  pl.BlockSpec((1, D), lambda n: (0, 0))],
        out_specs=[pl.BlockSpec((blk, D), lambda n: (n, 0)),
                   pl.BlockSpec((blk, D), lambda n: (n, 0)),
                   pl.BlockSpec((HG_SUB, HG_HEADS, HG_K, HG_K), lambda n: (n, 0, 0, 0))],
        out_shape=[jax.ShapeDtypeStruct((t, D), BF), jax.ShapeDtypeStruct((t, D), F32),
                   jax.ShapeDtypeStruct((nc, HG_HEADS, HG_K, HG_K), BF)],
        scratch_shapes=[pltpu.VMEM((HG_HEADS, HG_K, HG_K), F32), pltpu.VMEM((CH, D), F32),
                        half(), half(), half(), half(), half()],
        sem=("arbitrary",), nsteps=nc // HG_SUB, step_fn=lambda: pl.program_id(0))


def _hgrn_bwd(h3, hf, logits, norm_g, o_pre, states, dy, *, t, comm=None):
    nc = t // CH
    nt_dims = (((1,), (1,)), ((), ()))
    tn_dims = (((0,), (0,)), ((), ()))

    def body(h_ref, hf_ref, lg_ref, ng_ref, o_ref, st_ref, dy_ref, dh_ref, dlg_ref, dng_ref, ds_scr, dlb_scr,
             b_scr, tail_s, e_qa, e_ka, e_qb, e_kb, q_s, k_s, dqa_s, dka_s, dqb_s, dkb_s,
             qa_s, ka_s, qb_s, kb_s, v_s, do_s):
        n = pl.program_id(0)

        @pl.when(n == 0)
        def _():
            ds_scr[...] = jnp.zeros_like(ds_scr)
            dlb_scr[...] = jnp.zeros_like(dlb_scr)
            dng_ref[...] = jnp.zeros_like(dng_ref)

        heads = [slice(h * HG_K, (h + 1) * HG_K) for h in range(HG_HEADS)]
        lb = _lb_from_logits(lg_ref)
        causal = _tri(False)

        def chunk(c):
            rows = slice(c * CH, (c + 1) * CH)
            hq = h_ref[rows, 0:D].astype(F32)
            q, k, g, sq, sg, f = _hgrn_gates(hq, hf_ref[rows, :], lb)
            b_scr[...] = _apply01(jnp.where(causal, 1.0, 0.0).astype(BF), g)
            b = b_scr[...]
            b_mid = b_scr[CH // 2 - 1:CH // 2, :]
            b_last = b_scr[CH - 1:CH, :]
            q_s[...] = q
            k_s[...] = k
            for e_ref, s_ref, base, expo in ((e_qa, qa_s, q, b - b_mid), (e_ka, ka_s, k, b_mid - b),
                                             (e_qb, qb_s, q, b), (e_kb, kb_s, k, b_last - b)):
                e = jnp.exp(expo)
                e_ref[...] = e
                s_ref[...] = (base * e).astype(BF)
            v_s[...] = h_ref[rows, D:2 * D]
            dec = jnp.exp(b_last)
            for h, sl in enumerate(heads):
                gcol = slice(3 * D + h * HG_K, 3 * D + (h + 1) * HG_K)
                ngh = ng_ref[:, sl]
                sgate = _sig(h_ref[rows, 2 * D + h * HG_K:2 * D + (h + 1) * HG_K].astype(F32))
                o = o_ref[rows, sl]
                r = lax.rsqrt(jnp.mean(o * o, axis=-1, keepdims=True) + EPS)
                on = o * r
                dyh = dy_ref[rows, sl]
                dh_ref[rows, gcol] = (dyh * on * ngh * sgate * (1.0 - sgate)).astype(BF)
                dng_ref[:, sl] += jnp.sum(dyh * on * sgate, axis=0, keepdims=True)
                don = dyh * ngh * sgate
                do_s[:, sl] = (r * (don - on * jnp.mean(don * on, axis=-1, keepdims=True))).astype(BF)
            a = [jnp.where(causal, lax.dot_general(qa_s[:, sl], ka_s[:, sl], nt_dims, preferred_element_type=F32),
                           0.0).astype(BF) for sl in heads]
            da = [jnp.where(causal, lax.dot_general(do_s[:, sl], v_s[:, sl], nt_dims, preferred_element_type=F32),
                            0.0).astype(BF) for sl in heads]
            for h, sl in enumerate(heads):
                dh_ref[rows, 2 * D + h * HG_K:2 * D + (h + 1) * HG_K] = (
                    lax.dot_general(a[h], do_s[:, sl], tn_dims, preferred_element_type=F32)
                    + lax.dot_general(kb_s[:, sl], ds_scr[h].astype(BF), nt_dims, preferred_element_type=F32)
                ).astype(BF)
            for h, sl in enumerate(heads):
                dqa_s[:, sl] = jnp.dot(da[h], ka_s[:, sl], preferred_element_type=F32)
            for h, sl in enumerate(heads):
                dka_s[:, sl] = lax.dot_general(da[h], qa_s[:, sl], tn_dims, preferred_element_type=F32)
            for h, sl in enumerate(heads):
                dqb_s[:, sl] = jnp.dot(do_s[:, sl], st_ref[c, h], preferred_element_type=F32)
            for h, sl in enumerate(heads):
                dkb_s[:, sl] = jnp.dot(v_s[:, sl], ds_scr[h].astype(BF), preferred_element_type=F32)
            for h, sl in enumerate(heads):
                tail_s[:, sl] = jnp.sum(dec[:, sl] * st_ref[c, h].astype(F32) * ds_scr[h], axis=0, keepdims=True)
            for h, sl in enumerate(heads):
                ds_scr[h] = (lax.dot_general(do_s[:, sl], qb_s[:, sl], tn_dims, preferred_element_type=F32)
                             + dec[:, sl] * ds_scr[h])
            qv, kv = q_s[...], k_s[...]
            dqa, dka, dqb, dkb = dqa_s[...], dka_s[...], dqb_s[...], dkb_s[...]
            eqa, eka, eqb, ekb = e_qa[...], e_ka[...], e_qb[...], e_kb[...]
            dkb_kb = dkb * (kv * ekb)
            db_last = jnp.sum(dkb_kb, axis=0, keepdims=True) + tail_s[...]
            last_row = lax.broadcasted_iota(jnp.int32, (CH, D), 0) == CH - 1
            db = (dqa * (qv * eqa) - dka * (kv * eka) + dqb * (qv * eqb) - dkb_kb
                  + jnp.where(last_row, db_last, 0.0))
            dg = _apply01(jnp.where(_tri(True), 1.0, 0.0).astype(BF), db)
            dq = dqa * eqa + dqb * eqb
            dk = dka * eka + dkb * ekb
            dh_ref[rows, 0:D] = (dq * sq * (1.0 + hq * (1.0 - sq))).astype(BF)
            dfk = dg / f - dk
            dh_ref[rows, D:2 * D] = ((1.0 - lb) * dfk * sg * (1.0 - sg)).astype(BF)
            dlb_scr[...] += jnp.sum((1.0 - sg) * dfk, axis=0, keepdims=True)

        for c in reversed(range(HG_SUB)):
            chunk(c)

        @pl.when(n == nc // HG_SUB - 1)
        def _():
            dl0 = dlb_scr[...] * lb * (1.0 - lb)
            dlg_ref[0:1, :] = dl0
            dlg_ref[1:2, :] = -dl0

    steps = nc // HG_SUB
    blk = HG_SUB * CH
    rev = lambda n: (steps - 1 - n, 0)
    return _hosted_call(
        body, comm, (h3, hf, logits, norm_g, o_pre, states, dy), name="hgrn_bwd", grid=(steps,),
        in_specs=[pl.BlockSpec((blk, 3 * D), rev),
                  pl.BlockSpec((blk, D), rev),
                  pl.BlockSpec((2, D), lambda n: (0, 0)),
                  pl.BlockSpec((1, D), lambda n: (0, 0)),
                  pl.BlockSpec((blk, D), rev),
                  pl.BlockSpec((HG_SUB, HG_HEADS, HG_K, HG_K), lambda n: (steps - 1 - n, 0, 0, 0)),
                  pl.BlockSpec((blk, D), rev)],
        out_specs=[pl.BlockSpec((blk, 4 * D), rev),
                   pl.BlockSpec((2, D), lambda n: (0, 0)),
                   pl.BlockSpec((1, D), lambda n: (0, 0))],
        out_shape=[jax.ShapeDtypeStruct((t, 4 * D), BF), jax.ShapeDtypeStruct((2, D), F32),
                   jax.ShapeDtypeStruct((1, D), F32)],
        scratch_shapes=([pltpu.VMEM((HG_HEADS, HG_K, HG_K), F32), pltpu.VMEM((1, D), F32),
                         pltpu.VMEM((CH, D), F32), pltpu.VMEM((1, D), F32)]
                        + [pltpu.VMEM((CH, D), F32)] * 10 + [pltpu.VMEM((CH, D), BF)] * 6),
        sem=("arbitrary",), nsteps=steps, step_fn=lambda: pl.program_id(0))


def _place():
    x, y, c = lax.axis_index("x"), lax.axis_index("y"), lax.axis_index("c")
    return x, y, c, [(1 - x, y), (x, 1 - y), (1 - x, 1 - y)]


def _gather_comm(shards, mids):
    n, pieces = len(shards), len(mids)
    r = [s.shape[0] for s in shards]
    tile = 16
    cut = [[(rw // tile * p // pieces) * tile for p in range(pieces + 1)] for rw in r]
    size = [[cut[w][p + 1] - cut[w][p] for p in range(pieces)] for w in range(n)]

    def tools(ins, outs, sems):
        send_sems, recv_sems, local_sems = sems
        x, y, c, _ = _place()
        me, sib = (x, y, c), (x, y, 1 - c)
        near = [(x ^ c, y ^ (1 - c), c), (x ^ (1 - c), y ^ c, c), (1 - x, 1 - y, c)]

        def rows(w, p, dev):
            return outs[w].at[pl.ds((4 * dev[0] + 2 * dev[1] + dev[2]) * r[w] + cut[w][p], size[w][p]), :]

        def copy(kind, w, p, block, to, own=False):
            src = ins[w].at[pl.ds(cut[w][p], size[w][p]), :] if own else rows(w, p, block)
            return pltpu.make_async_remote_copy(
                src_ref=src, dst_ref=rows(w, p, block), send_sem=send_sems.at[p, kind],
                recv_sem=recv_sems.at[p, kind], device_id=to, device_id_type=MESH)

        def all_of(kind, p):
            whole = outs[0].at[pl.ds(0, sum(size[w][p] for w in range(n))), :]
            return pltpu.make_async_remote_copy(
                src_ref=whole, dst_ref=whole, send_sem=send_sems.at[p, kind], recv_sem=recv_sems.at[p, kind],
                device_id=me, device_id_type=MESH)

        mine = [pltpu.make_async_copy(ins[w], outs[w].at[pl.ds((4 * x + 2 * y + c) * r[w], r[w]), :],
                                      local_sems.at[w]) for w in range(n)]
        return near, me, sib, copy, all_of, mine

    def start(ins, outs, sems):
        near, me, sib, copy, _, mine = tools(ins, outs, sems)
        for cp in mine:
            cp.start()
        for p in range(pieces):
            for w in range(n):
                copy(0, w, p, me, sib, own=True).start()
                copy(1, w, p, me, near[0], own=True).start()
                copy(2, w, p, me, near[1], own=True).start()

    def pass_diagonal(p, near, sib, copy, all_of):
        all_of(3, p).wait_recv()
        for w in range(n):
            copy(6, w, p, near[2], sib).start()

    def pass_on(p):
        def phase(ins, outs, sems):
            near, _, sib, copy, all_of, _ = tools(ins, outs, sems)
            all_of(1, p).wait_recv()
            for w in range(n):
                copy(3, w, p, near[0], near[1]).start()
                copy(4, w, p, near[0], sib).start()
            all_of(2, p).wait_recv()
            for w in range(n):
                copy(5, w, p, near[1], sib).start()
            if p > 0:
                pass_diagonal(p - 1, near, sib, copy, all_of)
        return phase

    def finish(ins, outs, sems):
        near, _, sib, copy, all_of, mine = tools(ins, outs, sems)
        pass_diagonal(pieces - 1, near, sib, copy, all_of)
        for p in range(pieces):
            all_of(0, p).wait_recv()
            for kind in (4, 5, 6):
                all_of(kind, p).wait_recv()
            for kind in range(7):
                all_of(kind, p).wait_send()
        for cp in mine:
            cp.wait()

    return _Comm(shards, [jax.ShapeDtypeStruct((N_DEV * rw, D), BF) for rw in r],
                 [pltpu.SemaphoreType.DMA((pieces, 7)), pltpu.SemaphoreType.DMA((pieces, 7)),
                  pltpu.SemaphoreType.DMA((n,))],
                 [(0.0, start)] + [(f, pass_on(p)) for p, f in enumerate(mids)] + [(1.0, finish)])


def _pair_comm(grads):
    n = len(grads)
    r = [g.shape[0] // N_DEV for g in grads]

    def start(ins, outs, sems):
        send_sems, recv_sems = sems
        x, y, c, _ = _place()
        for w in range(n):
            for a in range(N_CHIP):
                pltpu.make_async_remote_copy(
                    src_ref=ins[w].at[pl.ds((2 * a + 1 - c) * r[w], r[w]), :], dst_ref=outs[w].at[a],
                    send_sem=send_sems.at[w], recv_sem=recv_sems.at[w],
                    device_id=(x, y, 1 - c), device_id_type=MESH).start()

    def finish(ins, outs, sems):
        send_sems, recv_sems = sems
        x, y, c, _ = _place()
        for w in range(n):
            pltpu.make_async_remote_copy(
                src_ref=outs[w], dst_ref=outs[w], send_sem=send_sems.at[w], recv_sem=recv_sems.at[w],
                device_id=(x, y, c), device_id_type=MESH).wait()

    return _Comm(grads, [jax.ShapeDtypeStruct((N_CHIP, rw, D), BF) for rw in r],
                 [pltpu.SemaphoreType.DMA((n,)), pltpu.SemaphoreType.DMA((n,))],
                 [(0.0, start), (1.0, finish)])


def _pair_add(grads, gots, core, *, name):
    n, r = len(grads), gots[0].shape[1]
    tr = r if r <= 128 else r // 2
    steps = r // tr
    tile = lambda k: (lambda s: jnp.clip(s - k * steps, 0, steps - 1))

    def body(c_ref, *refs):
        g_refs, got_refs, o_refs = refs[:n], refs[n:2 * n], refs[2 * n:]
        s = pl.program_id(0)
        for k in range(n):
            @pl.when(jnp.logical_and(s >= k * steps, s < (k + 1) * steps))
            def _(k=k):
                o_refs[k][...] = (g_refs[k][:, 0].astype(F32) + got_refs[k][...].astype(F32)).astype(BF)

    grid_spec = pltpu.PrefetchScalarGridSpec(
        num_scalar_prefetch=1, grid=(n * steps,),
        in_specs=[pl.BlockSpec((N_CHIP, 1, tr, D), lambda s, c_ref, k=k: (0, c_ref[0], tile(k)(s), 0))
                  for k in range(n)]
        + [pl.BlockSpec((N_CHIP, tr, D), lambda s, c_ref, k=k: (0, tile(k)(s), 0)) for k in range(n)],
        out_specs=[pl.BlockSpec((N_CHIP, tr, D), lambda s, c_ref, k=k: (0, tile(k)(s), 0)) for k in range(n)])
    return _pcall(body, name=name, grid_spec=grid_spec,
                  out_shape=[jax.ShapeDtypeStruct((N_CHIP, r, D), BF)] * n,
                  compiler_params=_cp(("arbitrary",)))(
                      core, *[g.reshape(N_CHIP, 2, r, D) for g in grads], *gots)


def _chip_comm(pair_sums):
    n = len(pair_sums)
    r = [p.shape[1] for p in pair_sums]
    off = [sum(r[:w]) for w in range(n)]

    def tools(ins, outs, sems):
        send_sems, recv_sems, local_sems = sems
        x, y, c, chips = _place()
        my_chip = 2 * x + y

        def slot(w):
            return outs[0].at[my_chip, pl.ds(off[w], r[w]), :]

        own = [pltpu.make_async_copy(ins[w].at[my_chip], slot(w), local_sems.at[w]) for w in range(n)]
        return x, y, c, chips, my_chip, slot, own, send_sems, recv_sems

    def start(ins, outs, sems):
        x, y, c, chips, my_chip, slot, own, send_sems, recv_sems = tools(ins, outs, sems)
        for cp in own:
            cp.start()
        for j, chip in enumerate(chips):
            for w in range(n):
                pltpu.make_async_remote_copy(
                    src_ref=ins[w].at[2 * chip[0] + chip[1]], dst_ref=slot(w), send_sem=send_sems.at[j],
                    recv_sem=recv_sems.at[j], device_id=(*chip, c), device_id_type=MESH).start()

    def finish(ins, outs, sems):
        x, y, c, chips, my_chip, slot, own, send_sems, recv_sems = tools(ins, outs, sems)
        whole = outs[0].at[my_chip]
        for j in range(3):
            pltpu.make_async_remote_copy(
                src_ref=whole, dst_ref=whole, send_sem=send_sems.at[j], recv_sem=recv_sems.at[j],
                device_id=(x, y, c), device_id_type=MESH).wait()
        for cp in own:
            cp.wait()

    return _Comm(pair_sums, [jax.ShapeDtypeStruct((N_CHIP, sum(r), D), BF)],
                 [pltpu.SemaphoreType.DMA((3,)), pltpu.SemaphoreType.DMA((3,)), pltpu.SemaphoreType.DMA((n,))],
                 [(0.0, start), (1.0, finish)])


def _adam_math(w, g, m, v):
    m = ADAM_B1 * m + (1.0 - ADAM_B1) * g
    v = ADAM_B2 * v + (1.0 - ADAM_B2) * (g * g)
    m_hat = m / (1.0 - ADAM_B1 ** ADAM_STEP)
    v_hat = v / (1.0 - ADAM_B2 ** ADAM_STEP)
    delta = -ADAM_LR * (m_hat / (jnp.sqrt(v_hat) + ADAM_EPS) + ADAM_WD * w)
    return delta, m, v


SMALL = (("norm_mix_g", (1, D), 0), ("hgrn_norm_g", (1, D), 1), ("norm_ffn_g", (1, D), 2),
         ("norm_final_g", (1, D), 3), ("hgrn_lb_logits", (2, D), 4), ("attn_sinks", (1, 16), 6),
         ("b_in", (1, IN_W), 8))
LOSS_ROW = 7


def _small_allreduce_adam(grads, loss_row, params):
    n = len(SMALL)

    def rows_of(ref, shape, row):
        r, w = shape
        if w <= D:
            return ref[row:row + r, 0:w]
        pieces = [ref[row + k:row + k + 1, :] for k in range(-(-w // D))]
        return jnp.concatenate(pieces, axis=1)[:, 0:w]

    def body(*refs):
        g_refs, loss_ref = refs[:n], refs[n]
        wmv = refs[n + 1:4 * n + 1]
        loss_out = refs[4 * n + 1]
        outs = refs[4 * n + 2:8 * n + 2]
        mine, total, gath, send_sems, recv_sems = refs[8 * n + 2:]
        x, y, c, _ = _place()
        me = 4 * x + 2 * y + c
        mine[...] = jnp.zeros_like(mine)
        for g_ref, (_, (r, w), row) in zip(g_refs, SMALL):
            for k in range(-(-w // D)):
                wk = min(D, w - k * D)
                mine[row + k:row + k + r, 0:wk] = g_ref[:, k * D:k * D + wk]
        mine[LOSS_ROW:LOSS_ROW + 1, 0:128] = loss_ref[...]
        gath[me] = mine[...]
        cps = []
        for d in range(1, N_DEV):
            peer = (x ^ (d >> 2), y ^ ((d >> 1) & 1), c ^ (d & 1))
            cps.append(pltpu.make_async_remote_copy(
                src_ref=mine, dst_ref=gath.at[me], send_sem=send_sems.at[d - 1],
                recv_sem=recv_sems.at[d - 1], device_id=peer, device_id_type=MESH))
        for cp in cps:
            cp.start()
        for cp in cps:
            cp.wait()
        g = gath[0]
        for k in range(1, N_DEV):
            g = g + gath[k]
        total[...] = g
        loss_out[...] = total[LOSS_ROW:LOSS_ROW + 1, 0:128]
        for i, (_, shape, row) in enumerate(SMALL):
            gi = rows_of(total, shape, row)
            w_ref, m_ref, v_ref = wmv[3 * i:3 * i + 3]
            o = outs[4 * i:4 * i + 4]
            o[0][...] = gi
            o[1][...], o[2][...], o[3][...] = _adam_math(w_ref[...], gi, m_ref[...], v_ref[...])

    vm = pl.BlockSpec(memory_space=pltpu.VMEM)
    ins = [grads[name] for name, _, _ in SMALL] + [loss_row]
    for name, _, _ in SMALL:
        ins += list(params[name])
    out_shape = [jax.ShapeDtypeStruct((1, 128), F32)]
    for _, shape, _ in SMALL:
        out_shape += [jax.ShapeDtypeStruct(shape, F32)] * 4
    res = _pcall(body, name="small_allreduce_adam", in_specs=[vm] * len(ins), out_specs=[vm] * len(out_shape),
                 out_shape=out_shape,
                 scratch_shapes=[pltpu.VMEM((SMALL_ROWS, D), F32), pltpu.VMEM((SMALL_ROWS, D), F32),
                                 pltpu.VMEM((N_DEV, SMALL_ROWS, D), F32),
                                 pltpu.SemaphoreType.DMA((N_DEV - 1,)), pltpu.SemaphoreType.DMA((N_DEV - 1,))],
                 compiler_params=pltpu.CompilerParams(has_side_effects=True))(*ins)
    return res[0], {name: res[1 + 4 * i:5 + 4 * i] for i, (name, _, _) in enumerate(SMALL)}


def _adam(ws, parts, ms, vs, *, name):
    n, rows = len(ws), ws[0].shape[0]
    tr = rows if rows <= 128 else rows // 2
    steps = rows // tr
    tile = lambda k: (lambda s: jnp.clip(s - k * steps, 0, steps - 1))

    def body(*refs):
        w_refs, m_refs, v_refs, p_ref = refs[:n], refs[n:2 * n], refs[2 * n:3 * n], refs[3 * n]
        o_refs = refs[3 * n + 1:]
        s = pl.program_id(0)
        for k in range(n):
            @pl.when(jnp.logical_and(s >= k * steps, s < (k + 1) * steps))
            def _(k=k):
                g = p_ref[0].astype(F32)
                for a in range(1, N_CHIP):
                    g = g + p_ref[a].astype(F32)
                o = o_refs[4 * k:4 * k + 4]
                o[0][...] = g
                o[1][...], o[2][...], o[3][...] = _adam_math(w_refs[k][...], g, m_refs[k][...], v_refs[k][...])

    spec = lambda k: pl.BlockSpec((tr, D), lambda s, k=k: (tile(k)(s), 0))
    res = _pcall(body, name=name, grid=(n * steps,),
                 in_specs=[spec(k) for k in range(n)] * 3 + [pl.BlockSpec((N_CHIP, tr, D), lambda s: (0, s, 0))],
                 out_specs=[spec(k) for k in range(n) for _ in range(4)],
                 out_shape=[jax.ShapeDtypeStruct((rows, D), F32)] * (4 * n),
                 compiler_params=_cp(("arbitrary",)))(*ws, *ms, *vs, parts)
    return [res[4 * k:4 * k + 4] for k in range(n)]


def _step(x, tgt, shards, norm_mix_g, b_in, sinks, logits, hgrn_norm_g, norm_ffn_g, norm_final_g):
    t = x.shape[0]
    core = lax.axis_index("c").astype(jnp.int32).reshape(1)

    u1, (win_t,) = _rms_fwd(x, norm_mix_g, tm=512, name="rms_mix", comm=_gather_comm(shards[0:1], (0.2, 0.4, 0.6, 0.8)))
    (q, kv, h3, hf, gates), (wg_t, wba, wbh, wout) = _inproj_fwd(
        u1, win_t, b_in, t=t, comm=_gather_comm([shards[1]] + shards[4:7], (0.3, 0.5, 0.7, 0.9)))
    (y_attn,), _ = _attn_fwd(q, kv, sinks, t=t)
    (y_hgrn, o_pre, states), (wu_t, wd) = _hgrn_fwd(h3, hf, logits, hgrn_norm_g, t=t,
                                                    comm=_gather_comm(shards[2:4], (0.3, 0.5, 0.7, 0.9)))
    col = lambda j: j
    first, second = (lambda j: 0), (lambda j: 1)
    gate_tiles = [(gates, D, first), (gates, D, second)]

    def merge(prods, ex):
        (ya_, yb_), (ga, gb) = prods, ex
        sa, sb = _sig(ga.astype(F32)), _sig(gb.astype(F32))
        return sa, sb, ya_ * sa * (1.0 - sa), yb_ * sb * (1.0 - sb), sa * ya_ + sb * yb_

    sig_a, sig_b, dgate_a, dgate_b, merged = _fmm(
        [y_attn, y_hgrn], [(0, wba, False), (1, wbh, False)], gate_tiles, merge,
        [(BF, D, D, first)] * 5, m=t, n=D, tm=512, tn=D, name="branch_merge")
    def resid_norm(prods, ex):
        (p,), (xv, gv) = prods, ex
        hv = xv + p
        return hv, hv * lax.rsqrt(jnp.mean(hv * hv, axis=-1, keepdims=True) + EPS) * gv

    h1, u2 = _fmm([merged], [(0, wout, False)], [(x, D, first)], resid_norm, [(F32, D, D, first), (BF, D, D, first)],
                  m=t, n=D, tm=1024, tn=D, name="out_proj", vecs=[norm_ffn_g])

    def swiglu(prods, ex):
        g_, u_ = prods
        s = _sig(g_)
        silu = g_ * s
        return u_ * s * (1.0 + g_ * (1.0 - s)), silu, silu * u_

    dz_dgate, dz_dup, z = _fmm([u2], [(0, wg_t, True), (0, wu_t, True)], [], swiglu,
                               [(BF, FFN, FFN // 2, col)] * 3, m=t, n=FFN, tm=1024, tn=FFN // 2,
                               name="ffn_gate_up")
    def loss_head(prods, ex):
        (p,), (hv, tv, gv) = prods, ex
        hv = hv + p
        r = lax.rsqrt(jnp.mean(hv * hv, axis=-1, keepdims=True) + EPS)
        xh = hv * r
        err = xh * gv - tv
        lp = jnp.sum(jnp.sum(err * err, axis=1, keepdims=True), axis=0, keepdims=True) * (0.5 / D)
        dy = err * (1.0 / D)
        dxh = dy * gv
        dh = r * (dxh - xh * jnp.mean(dxh * xh, axis=-1, keepdims=True))
        return dh, dh, jnp.sum(dy * xh, axis=0, keepdims=True), jnp.broadcast_to(lp, (1, 128))

    dh2, dh2_b, d_norm_final, loss_row = _fmm(
        [z], [(0, wd, False)], [(h1, D, first), (tgt, D, first)], loss_head, [(F32, D, D, first), (BF, D, D, first)],
        m=t, n=D, tm=512, tn=D, name="ffn_down_loss", vecs=[norm_final_g], sums=[D, 128])

    def swiglu_bwd(prods, ex):
        (dz,), (da_, db_) = prods, ex
        return dz * da_.astype(F32), dz * db_.astype(F32)

    ffn_tiles = [(dz_dgate, FFN // 2, col), (dz_dup, FFN // 2, col)]
    dgt, dup = _fmm([dh2_b], [(0, wd, True)], ffn_tiles, swiglu_bwd, [(BF, FFN, FFN // 2, col)] * 2,
                    m=t, n=FFN, tm=1024, tn=FFN // 2, name="d_gate_up")
    (d_wd,) = _wgrad([z], dh2_b, name="d_w_down")
    (du2,) = _fmm([dgt, dup], [(0, wg_t, False), (1, wu_t, False)], [], lambda prods, ex: (prods[0] + prods[1],),
                  [(F32, D, 512, col)], m=t, n=D, tm=1024, tn=512, name="d_u2")
    d_wg, d_wu = _wgrad([dgt, dup], u2, name="d_w_gate_up")
    dh1, dh1_b, d_norm_ffn = _rms_bwd(du2, h1, norm_ffn_g, dh2, tm=512, name="rms_ffn_bwd")
    (d_wout,) = _wgrad([merged], dh1_b, name="d_w_out")

    def merge_bwd(prods, ex):
        (dm,), (sa, sb, ca, cb, wa, wb) = prods, ex
        dgate = jnp.concatenate([dm * ca.astype(F32), dm * cb.astype(F32)], axis=1)
        dya_ = (dm * sa.astype(F32)).astype(BF)
        dyb_ = (dm * sb.astype(F32)).astype(BF)
        return (dya_, dyb_, dgate, lax.dot_general(dya_, wa, _NT, preferred_element_type=F32),
                lax.dot_general(dyb_, wb, _NT, preferred_element_type=F32))

    ffn_grads = (d_wg, d_wu, d_wd)
    (dya, dyb, dgates, dy_attn, dy_hgrn), got = _fmm(
        [dh1_b], [(0, wout, True)], [(a, D, first) for a in (sig_a, sig_b, dgate_a, dgate_b)], merge_bwd,
        [(BF, D, D, first), (BF, D, D, first), (BF, 2 * D, 2 * D, first), (BF, D, D, first), (F32, D, D, first)],
        m=t, n=D, tm=512, tn=D, name="d_merge", consts=[wba, wbh], comm=_pair_comm(ffn_grads))
    pair_ffn = _pair_add(ffn_grads, got, core, name="pair_add_ffn")
    (d_wba,) = _wgrad([y_attn], dya, name="d_w_ba")
    (d_wbh,) = _wgrad([y_hgrn], dyb, name="d_w_bh")
    sq_grads = (d_wba, d_wbh, d_wout)
    (dh4, d_logits, d_hgrn_norm), (parts_ffn, *got) = _hgrn_bwd(
        h3, hf, logits, hgrn_norm_g, o_pre, states, dy_hgrn, t=t,
        comm=_both(_chip_comm(pair_ffn), _pair_comm(sq_grads)))
    pair_sq = _pair_add(sq_grads, got, core, name="pair_add_sq")
    (dq, dkv, d_sinks), (parts_sq,) = _attn_bwd(q, kv, sinks, dy_attn, t=t, comm=_chip_comm(pair_sq))
    dps = (dq, dkv, dh4, dgates)
    d_win_t, d_b_in = _inproj_bwd_w(dps, u1, t=t)
    half0, got_in = _inproj_bwd_x(dps, win_t, x, norm_mix_g, dh1, t=t, part=0, comm=_pair_comm([d_win_t]))
    pair_in = _pair_add([d_win_t], got_in, core, name="pair_add_w_in")
    (grad_x, d_norm_mix), (parts_in,) = _inproj_bwd_x(dps, win_t, x, norm_mix_g, dh1, t=t, part=1, prev=half0,
                                                      comm=_chip_comm(pair_in))

    small_grads = (d_norm_mix, d_b_in, d_sinks, d_logits, d_hgrn_norm, d_norm_ffn, d_norm_final)
    return loss_row, grad_x, (parts_in, parts_ffn, parts_sq), small_grads


def kernel(x, norm_mix_g, w_in, b_in, attn_sinks, hgrn_lb_logits, hgrn_norm_g, w_branch_attn, w_branch_hgrn, w_out, norm_ffn_g, w_ffn_gate, w_ffn_up, w_ffn_down, norm_final_g, loss_target, m_norm_mix_g, m_w_in, m_b_in, m_attn_sinks, m_hgrn_lb_logits, m_hgrn_norm_g, m_w_branch_attn, m_w_branch_hgrn, m_w_out, m_norm_ffn_g, m_w_ffn_gate, m_w_ffn_up, m_w_ffn_down, m_norm_final_g, v_norm_mix_g, v_w_in, v_b_in, v_attn_sinks, v_hgrn_lb_logits, v_hgrn_norm_g, v_w_branch_attn, v_w_branch_hgrn, v_w_out, v_norm_ffn_g, v_w_ffn_gate, v_w_ffn_up, v_w_ffn_down, v_norm_final_g):
    shards = [w_in[0].T.astype(BF), w_ffn_gate[0].T.astype(BF), w_ffn_up[0].T.astype(BF),
              w_ffn_down[0].astype(BF), w_branch_attn[0].astype(BF), w_branch_hgrn[0].astype(BF),
              w_out[0].astype(BF)]
    loss_row, grad_x, grad_parts, small_grads = _step(
        x[0], loss_target[0], shards, norm_mix_g, b_in, attn_sinks, hgrn_lb_logits, hgrn_norm_g,
        norm_ffn_g, norm_final_g.reshape(1, D))

    d_norm_mix, d_b_in, d_sinks, d_logits, d_hgrn_norm, d_norm_ffn, d_norm_final = small_grads
    row = lambda a: a.reshape(1, D)
    loss_out, small = _small_allreduce_adam(
        dict(norm_mix_g=d_norm_mix, hgrn_norm_g=d_hgrn_norm, norm_ffn_g=d_norm_ffn, norm_final_g=d_norm_final,
             hgrn_lb_logits=d_logits, attn_sinks=d_sinks, b_in=d_b_in),
        loss_row,
        dict(norm_mix_g=(norm_mix_g, m_norm_mix_g, v_norm_mix_g), hgrn_norm_g=(hgrn_norm_g, m_hgrn_norm_g, v_hgrn_norm_g),
             norm_ffn_g=(norm_ffn_g, m_norm_ffn_g, v_norm_ffn_g),
             norm_final_g=(row(norm_final_g), row(m_norm_final_g), row(v_norm_final_g)),
             hgrn_lb_logits=(hgrn_lb_logits, m_hgrn_lb_logits, v_hgrn_lb_logits),
             attn_sinks=(attn_sinks, m_attn_sinks, v_attn_sinks), b_in=(b_in, m_b_in, v_b_in)))
    small["norm_final_g"] = [a.reshape(D) for a in small["norm_final_g"]]
    loss = loss_out[0, 0]

    names = ["w_in", "w_ffn_gate", "w_ffn_up", "w_ffn_down", "w_branch_attn", "w_branch_hgrn", "w_out"]
    w_full = dict(w_in=(w_in, m_w_in, v_w_in), w_ffn_gate=(w_ffn_gate, m_w_ffn_gate, v_w_ffn_gate),
                  w_ffn_up=(w_ffn_up, m_w_ffn_up, v_w_ffn_up), w_ffn_down=(w_ffn_down, m_w_ffn_down, v_w_ffn_down),
                  w_branch_attn=(w_branch_attn, m_w_branch_attn, v_w_branch_attn),
                  w_branch_hgrn=(w_branch_hgrn, m_w_branch_hgrn, v_w_branch_hgrn),
                  w_out=(w_out, m_w_out, v_w_out))
    big = {}
    for group, parts, tag in zip((names[0:1], names[1:4], names[4:7]), grad_parts, ("w_in", "ffn", "square")):
        flip = [name in names[0:3] for name in group]
        view = lambda a, f: a[0].T if f else a[0]
        cols = [[view(w_full[name][j], f) for name, f in zip(group, flip)] for j in range(3)]
        res = _adam(cols[0], parts, cols[1], cols[2], name="adam_" + tag)
        for name, f, r in zip(group, flip, res):
            big[name] = [a.T[None] if f else a[None] for a in r]

    order = ["norm_mix_g", "w_in", "b_in", "attn_sinks", "hgrn_lb_logits", "hgrn_norm_g", "w_branch_attn",
             "w_branch_hgrn", "w_out", "norm_ffn_g", "w_ffn_gate", "w_ffn_up", "w_ffn_down", "norm_final_g"]
    outs = [loss, grad_x[None]]
    for kind in range(4):
        for name in order:
            outs.append(big[name][kind] if name in big else small[name][kind])
    return tuple(outs)
```

```python
import math

import jax
import jax.numpy as jnp
from jax import lax
from jax.experimental import pallas as pl
from jax.experimental.pallas import tpu as pltpu

F32 = jnp.float32
BF = jnp.bfloat16
MESH = pl.DeviceIdType.MESH

D = 1024
HEAD = 64
N_PAIR = 8
BLK = 128
CH = 64
HG_SUB = 4
HG_HEADS = 8
HG_K = 128
FFN = 2816
IN_W = 7424
N_DEV = 8
N_CHIP = 4
EPS = 1e-6
NEG = -1e30
SCALE = 1.0 / math.sqrt(HEAD)
VMEM_LIMIT = 56 * 1024 * 1024
WT = 256

ADAM_LR, ADAM_B1, ADAM_B2, ADAM_EPS, ADAM_WD, ADAM_STEP = 0.001, 0.9, 0.999, 1e-08, 0.01, 10

GRP_OFF = (0, D // WT, (D + 256) // WT, (5 * D + 256) // WT)
GRP_N = (D // WT, 256 // WT, 4 * D // WT, 2 * D // WT)
SMALL_ROWS = 16


_NN = (((1,), (0,)), ((), ()))
_NT = (((1,), (1,)), ((), ()))
_TN = (((0,), (0,)), ((), ()))


def _pcall(body, **kw):
    return pl.pallas_call(body, **kw)


def _cp(sem=None, **kw):
    return pltpu.CompilerParams(dimension_semantics=sem, vmem_limit_bytes=VMEM_LIMIT, **kw)


def _sig(v):
    return 0.5 * jnp.tanh(0.5 * v) + 0.5


def _accum(ref, val, first):
    @pl.when(first)
    def _():
        ref[...] = val

    @pl.when(jnp.logical_not(first))
    def _():
        ref[...] += val


class _Comm:
    def __init__(self, ins, out_shapes, sem_shapes, phases):
        self.ins, self.out_shapes, self.sem_shapes, self.phases = list(ins), list(out_shapes), list(sem_shapes), phases


def _both(a, b):
    ni, no, ns = len(a.ins), len(a.out_shapes), len(a.sem_shapes)

    def of_a(fn):
        return lambda ins, outs, sems: fn(ins[:ni], outs[:no], sems[:ns])

    def of_b(fn):
        return lambda ins, outs, sems: fn(ins[ni:], outs[no:], sems[ns:])

    return _Comm(a.ins + b.ins, a.out_shapes + b.out_shapes, a.sem_shapes + b.sem_shapes,
                 [(f, of_a(fn)) for f, fn in a.phases] + [(f, of_b(fn)) for f, fn in b.phases])


def _host(body, comm, n_in, n_out, n_scr, nsteps, step_fn):
    if comm is None:
        return body
    ci, co = len(comm.ins), len(comm.out_shapes)

    def wrapped(*refs):
        p = 0
        ins, p = refs[p:p + n_in], p + n_in
        cins, p = refs[p:p + ci], p + ci
        outs, p = refs[p:p + n_out], p + n_out
        couts, p = refs[p:p + co], p + co
        scr, p = refs[p:p + n_scr], p + n_scr
        csems = refs[p:]
        step = step_fn()
        for frac, fn in comm.phases:
            if frac < 1.0:
                @pl.when(step == int(round(frac * (nsteps - 1))))
                def _(fn=fn):
                    fn(cins, couts, csems)
        body(*ins, *outs, *scr)
        for frac, fn in comm.phases:
            if frac >= 1.0:
                @pl.when(step == nsteps - 1)
                def _(fn=fn):
                    fn(cins, couts, csems)

    return wrapped


def _hosted_call(body, comm, args, *, name, grid, in_specs, out_specs, out_shape, scratch_shapes, sem,
                 nsteps, step_fn, aliases=None):
    n_in, n_out, n_scr = len(in_specs), len(out_specs), len(scratch_shapes)
    args = list(args)
    extra = {}
    if comm is not None:
        in_specs = list(in_specs) + [_hbm_spec()] * len(comm.ins)
        out_specs = list(out_specs) + [_hbm_spec()] * len(comm.out_shapes)
        out_shape = list(out_shape) + comm.out_shapes
        scratch_shapes = list(scratch_shapes) + comm.sem_shapes
        args += comm.ins
        extra = dict(has_side_effects=True)
    outs = _pcall(_host(body, comm, n_in, n_out, n_scr, nsteps, step_fn), name=name, grid=grid,
                  in_specs=in_specs, out_specs=out_specs, out_shape=out_shape, scratch_shapes=scratch_shapes,
                  input_output_aliases=aliases or {}, compiler_params=_cp(sem, **extra))(*args)
    return list(outs[:n_out]), list(outs[n_out:])


def _hbm_spec():
    return pl.BlockSpec(memory_space=pl.ANY)


def _wgrad(a_list, b, *, name):
    (t, m), n, gm = a_list[0].shape, b.shape[1], a_list[0].shape[1] // WT
    n_a = len(a_list)
    tile = lambda k: (lambda s: jnp.clip(s - k * gm, 0, gm - 1))

    def body(*refs):
        a_refs, b_ref, o_refs = refs[:n_a], refs[n_a], refs[n_a + 1:]
        s = pl.program_id(0)
        for k in range(n_a):
            @pl.when(jnp.logical_and(s >= k * gm, s < (k + 1) * gm))
            def _(k=k):
                o_refs[k][...] = lax.dot_general(a_refs[k][...], b_ref[...], _TN,
                                                 preferred_element_type=F32).astype(BF)

    return _pcall(body, name=name, grid=(n_a * gm,),
                  in_specs=[pl.BlockSpec((t, WT), lambda s, k=k: (0, tile(k)(s))) for k in range(n_a)]
                  + [pl.BlockSpec((t, n), lambda s: (0, 0))],
                  out_specs=[pl.BlockSpec((WT, n), lambda s, k=k: (tile(k)(s), 0)) for k in range(n_a)],
                  out_shape=[jax.ShapeDtypeStruct((m, n), BF)] * n_a,
                  compiler_params=_cp(("arbitrary",)))(*a_list, b)


def _fmm(lhs, rhs, extras, epilogue, outs, *, m, n, tm, tn, name, comm=None, vecs=(), consts=(), sums=(),
         cols_outer=False):
    tm, tn = min(tm, m), min(tn, n)
    assert m % tm == 0 and n % tn == 0 and (not sums or (tn == n and not cols_outer)), (name, m, n, tm, tn)
    in_specs, args = [], []
    for a in lhs:
        in_specs.append(pl.BlockSpec((tm, a.shape[1]), lambda i, j: (i, 0)))
        args.append(a)
    for li, b, tb in rhs:
        k = lhs[li].shape[1]
        in_specs.append(pl.BlockSpec((tn, k), lambda i, j: (j, 0)) if tb
                        else pl.BlockSpec((k, tn), lambda i, j: (0, j)))
        args.append(b)
    for arr, w, col in extras:
        in_specs.append(pl.BlockSpec((tm, w), lambda i, j, col=col: (i, col(j))))
        args.append(arr)
    for vec in vecs:
        in_specs.append(pl.BlockSpec((1, tn), lambda i, j: (0, j)))
        args.append(vec)
    for whole in consts:
        in_specs.append(pl.BlockSpec(whole.shape, lambda i, j: (0, 0)))
        args.append(whole)
    out_specs = [pl.BlockSpec((tm, w), lambda i, j, col=col: (i, col(j))) for _, _, w, col in outs]
    out_shape = [jax.ShapeDtypeStruct((m, total), dt) for dt, total, _, _ in outs]
    for w in sums:
        out_specs.append(pl.BlockSpec((1, w), lambda i, j: (0, 0)))
        out_shape.append(jax.ShapeDtypeStruct((1, w), F32))
    nl, nr, ne, no = len(lhs), len(rhs), len(extras) + len(vecs) + len(consts), len(outs)

    def body(*refs):
        prods = []
        for r, (li, _, tb) in enumerate(rhs):
            prods.append(lax.dot_general(refs[li][...], refs[nl + r][...], _NT if tb else _NN,
                                         preferred_element_type=F32))
        vals = epilogue(prods, [ref[...] for ref in refs[nl + nr:nl + nr + ne]])
        o_refs = refs[nl + nr + ne:]
        for o_ref, v in zip(o_refs[:no], vals[:no]):
            o_ref[...] = v.astype(o_ref.dtype)
        for s_ref, v in zip(o_refs[no:], vals[no:]):
            _accum(s_ref, v, pl.program_id(0) == 0)

    grid = (m // tm, n // tn)
    if cols_outer:
        flip = lambda spec: pl.BlockSpec(spec.block_shape, lambda j, i, f=spec.index_map: f(i, j))
        in_specs, out_specs, grid = [flip(s) for s in in_specs], [flip(s) for s in out_specs], grid[::-1]
    res, comm_res = _hosted_call(
        body, comm, args, name=name, grid=grid, in_specs=in_specs, out_specs=out_specs,
        out_shape=out_shape, scratch_shapes=[], sem=("arbitrary", "arbitrary"), nsteps=grid[0] * grid[1],
        step_fn=lambda: pl.program_id(0) * grid[1] + pl.program_id(1))
    return res if comm is None else (res, comm_res)


def _grp_of(i):
    return [jnp.logical_and(i >= GRP_OFF[g], i < GRP_OFF[g] + GRP_N[g]) for g in range(4)]


def _grp_idx(i, g):
    return jnp.clip(i - GRP_OFF[g], 0, GRP_N[g] - 1)


def _inproj_fwd(u, win_t, b_in, *, t, comm=None):
    tm = min(1024, t)
    n_row = t // tm
    n_chunks, h_first, g_first = 8, 2, 6
    sub = D // WT

    def w_block(l):
        return jnp.where(l == 0, GRP_OFF[0], jnp.where(l == 1, GRP_OFF[1], GRP_OFF[2] + sub * (l - h_first)))

    def body(u_ref, *rest):
        w_refs, b_refs, (q_ref, kv_ref, h3_ref, hf_ref, g_ref) = rest[:sub], rest[sub:2 * sub], rest[2 * sub:]
        l = pl.program_id(1)

        @pl.when(l == 1)
        def _():
            kv_ref[...] = (lax.dot_general(u_ref[...], w_refs[0][...], _NT, preferred_element_type=F32)
                           + b_refs[0][...]).astype(BF)

        is_hf = l == h_first + 1
        in_h3 = jnp.logical_and(jnp.logical_and(l >= h_first, l < g_first), jnp.logical_not(is_hf))
        for pred, o_ref in ((l == 0, q_ref), (in_h3, h3_ref), (is_hf, hf_ref), (l >= g_first, g_ref)):
            @pl.when(pred)
            def _(o_ref=o_ref):
                w = jnp.concatenate([w[...] for w in w_refs], axis=0)
                b = jnp.concatenate([b[...] for b in b_refs], axis=1)
                o_ref[...] = (lax.dot_general(u_ref[...], w, _NT, preferred_element_type=F32) + b).astype(o_ref.dtype)

    return _hosted_call(
        body, comm, [u] + [win_t] * sub + [b_in] * sub, name="inproj_fwd", grid=(n_row, n_chunks),
        in_specs=[pl.BlockSpec((tm, D), lambda i, l: (i, 0))]
        + [pl.BlockSpec((WT, D), lambda i, l, o=o: (w_block(l) + o, 0)) for o in range(sub)]
        + [pl.BlockSpec((1, WT), lambda i, l, o=o: (0, w_block(l) + o)) for o in range(sub)],
        out_specs=[pl.BlockSpec((tm, D), lambda i, l: (i, 0)),
                   pl.BlockSpec((tm, 256), lambda i, l: (i, 0)),
                   pl.BlockSpec((tm, D), lambda i, l: (i, jnp.clip(l - h_first - 1, 0, 2))),
                   pl.BlockSpec((tm, D), lambda i, l: (i, 0)),
                   pl.BlockSpec((tm, D), lambda i, l: (i, jnp.clip(l - g_first, 0, 1)))],
        out_shape=[jax.ShapeDtypeStruct((t, D), BF), jax.ShapeDtypeStruct((t, 256), BF),
                   jax.ShapeDtypeStruct((t, 3 * D), BF), jax.ShapeDtypeStruct((t, D), F32),
                   jax.ShapeDtypeStruct((t, 2 * D), BF)],
        scratch_shapes=[], sem=("arbitrary", "arbitrary"), nsteps=n_row * n_chunks,
        step_fn=lambda: pl.program_id(0) * n_chunks + pl.program_id(1))


def _inproj_bwd_x(dps, win_t, x, g, resid, *, t, part, prev=None, comm=None):
    n_row = 8 if t >= 4096 else 4
    tm = t // n_row
    first = 1
    per = first if part == 0 else n_row - first
    row = lambda i: part * first + i

    n_chunks = 4
    sub = 2 * D // WT

    def w_block(l):
        return jnp.where(l == 0, 0, GRP_OFF[2] + sub * (l - 1))

    def body(d0, d1, d2, d3, *rest):
        w_refs, (x_ref, g_ref, r_ref) = rest[:sub], rest[sub:sub + 3]
        dg_prev = rest[sub + 3] if prev is not None else None
        o_ref, dg_ref, acc_ref = rest[-3], rest[-2], rest[-1]
        i, l = pl.program_id(0), pl.program_id(1)

        @pl.when(l == 0)
        def _():
            wq = jnp.concatenate([w[...] for w in w_refs[:GRP_N[0]]], axis=0)
            acc_ref[...] = (jnp.dot(d0[...], wq, preferred_element_type=F32)
                            + jnp.dot(d1[...], w_refs[GRP_N[0]][...], preferred_element_type=F32))

        for pred, d_ref in ((jnp.logical_and(l >= 1, l < 3), d2), (l == 3, d3)):
            @pl.when(pred)
            def _(d_ref=d_ref):
                w = jnp.concatenate([w[...] for w in w_refs], axis=0)
                acc_ref[...] += jnp.dot(d_ref[...], w, preferred_element_type=F32)

        @pl.when(l == n_chunks - 1)
        def _():
            xv = x_ref[...]
            r = lax.rsqrt(jnp.mean(xv * xv, axis=-1, keepdims=True) + EPS)
            xh = xv * r
            du = acc_ref[...]
            dxh = du * g_ref[...]
            o_ref[...] = r_ref[...] + r * (dxh - xh * jnp.mean(dxh * xh, axis=-1, keepdims=True))
            dg = jnp.sum(du * xh, axis=0, keepdims=True)
            if dg_prev is not None:
                dg = dg + jnp.where(i == 0, 1.0, 0.0) * dg_prev[...]
            _accum(dg_ref, dg, i == 0)

    rows = lambda w: pl.BlockSpec((tm, w), lambda i, l: (row(i), 0))
    in_specs = ([rows(D), rows(256),
                 pl.BlockSpec((tm, 2 * D), lambda i, l: (row(i), jnp.clip(l - 1, 0, 1))), rows(2 * D)]
                + [pl.BlockSpec((WT, D), lambda i, l, o=o: (w_block(l) + o, 0)) for o in range(sub)]
                + [rows(D), pl.BlockSpec((1, D), lambda i, l: (0, 0)), rows(D)])
    args = list(dps) + [win_t] * sub + [x, g, resid]
    aliases = None
    if prev is not None:
        in_specs += [pl.BlockSpec((1, D), lambda i, l: (0, 0)), _hbm_spec()]
        args += [prev[1], prev[0]]
        aliases = {len(args) - 1: 0}
    return _hosted_call(
        body, comm, args, name="inproj_bwd_x%d" % part, grid=(per, n_chunks), in_specs=in_specs,
        out_specs=[rows(D), pl.BlockSpec((1, D), lambda i, l: (0, 0))],
        out_shape=[jax.ShapeDtypeStruct((t, D), F32), jax.ShapeDtypeStruct((1, D), F32)],
        scratch_shapes=[pltpu.VMEM((tm, D), F32)], sem=("arbitrary", "arbitrary"), nsteps=per * n_chunks,
        step_fn=lambda: pl.program_id(0) * n_chunks + pl.program_id(1), aliases=aliases)


def _inproj_bwd_w(dps, u, *, t):
    n_tiles = IN_W // WT
    dims = (((0,), (0,)), ((), ()))

    def body(d0, d1, d2, d3, u_ref, o_ref, db_ref):
        i = pl.program_id(0)
        uv = u_ref[...]
        for g, (pred, d_ref) in enumerate(zip(_grp_of(i), (d0, d1, d2, d3))):
            @pl.when(pred)
            def _(d_ref=d_ref):
                dv = d_ref[...]
                o_ref[...] = lax.dot_general(dv, uv, dims, preferred_element_type=F32).astype(BF)
                db_ref[...] = jnp.sum(dv.astype(F32), axis=0, keepdims=True)

    return _pcall(body, name="inproj_bwd_w", grid=(n_tiles,),
                  in_specs=[pl.BlockSpec((t, WT), lambda i, g=g: (0, _grp_idx(i, g))) for g in range(4)]
                  + [pl.BlockSpec((t, D), lambda i: (0, 0))],
                  out_specs=[pl.BlockSpec((WT, D), lambda i: (i, 0)),
                             pl.BlockSpec((1, WT), lambda i: (0, i))],
                  out_shape=[jax.ShapeDtypeStruct((IN_W, D), BF), jax.ShapeDtypeStruct((1, IN_W), F32)],
                  compiler_params=_cp(("arbitrary",)))(*dps, u)


def _row_spec(tm, width, col=0):
    return pl.BlockSpec((tm, width), lambda i: (i, col))


def _vec_spec(width):
    return pl.BlockSpec((1, width), lambda i: (0, 0))


def _rms_fwd(x, g, *, tm, name, comm=None):
    t = x.shape[0]
    tm = min(tm, t)

    def body(x_ref, g_ref, u_ref):
        xv = x_ref[...]
        r = lax.rsqrt(jnp.mean(xv * xv, axis=-1, keepdims=True) + EPS)
        u_ref[...] = (xv * r * g_ref[...]).astype(BF)

    (u,), comm_res = _hosted_call(
        body, comm, (x, g), name=name, grid=(t // tm,), in_specs=[_row_spec(tm, D), _vec_spec(D)],
        out_specs=[_row_spec(tm, D)], out_shape=[jax.ShapeDtypeStruct((t, D), BF)], scratch_shapes=[],
        sem=("arbitrary",), nsteps=t // tm, step_fn=lambda: pl.program_id(0))
    return u if comm is None else (u, comm_res)


def _rms_bwd(du, x, g, resid, *, tm, name):
    t = x.shape[0]
    tm = min(tm, t)

    def body(du_ref, x_ref, g_ref, r_ref, dx_ref, dxb_ref, dg_ref):
        xv = x_ref[...]
        r = lax.rsqrt(jnp.mean(xv * xv, axis=-1, keepdims=True) + EPS)
        xh = xv * r
        duv = du_ref[...]
        dxh = duv * g_ref[...]
        dx = r_ref[...] + r * (dxh - xh * jnp.mean(dxh * xh, axis=-1, keepdims=True))
        dx_ref[...] = dx
        dxb_ref[...] = dx.astype(BF)
        _accum(dg_ref, jnp.sum(duv * xh, axis=0, keepdims=True), pl.program_id(0) == 0)

    return _pcall(body, name=name, grid=(t // tm,),
                  in_specs=[_row_spec(tm, D), _row_spec(tm, D), _vec_spec(D), _row_spec(tm, D)],
                  out_specs=[_row_spec(tm, D), _row_spec(tm, D), _vec_spec(D)],
                  out_shape=[jax.ShapeDtypeStruct((t, D), F32), jax.ShapeDtypeStruct((t, D), BF),
                             jax.ShapeDtypeStruct((1, D), F32)],
                  compiler_params=_cp(("arbitrary",)))(du, x, g, resid)


def _attn_kv_tiles(kprev, kcur):
    kv = jnp.concatenate([kprev, kcur], axis=0).astype(F32)
    lo = lax.broadcasted_iota(jnp.int32, (2 * BLK, 128), 1) < HEAD
    tiles = []
    for part in (kv[:, 0:128], kv[:, 128:256]):
        rolled = pltpu.roll(part, HEAD, 1)
        z = jnp.zeros_like(part)
        tiles.append(((jnp.where(lo, part, z).astype(BF), jnp.where(lo, z, rolled).astype(BF)),
                      (jnp.where(lo, rolled, z).astype(BF), jnp.where(lo, z, part).astype(BF))))
    k_t, v_t = tiles
    return [(jnp.concatenate(k_t[h], axis=0), jnp.concatenate(v_t[h], axis=0)) for h in range(2)]


def _attn_mask(i):
    qi = lax.broadcasted_iota(jnp.int32, (BLK, 2 * BLK), 0)
    kj = lax.broadcasted_iota(jnp.int32, (BLK, 2 * BLK), 1)
    first_key = jnp.where(i == 0, BLK, 0)
    in_prev = jnp.logical_and(jnp.logical_and(kj < BLK, kj > qi), kj >= first_key)
    in_cur = jnp.logical_and(kj >= BLK, kj - BLK <= qi)
    return jnp.logical_or(in_prev, in_cur)


def _attn_probs(s, sink, valid):
    s = jnp.where(valid, s * SCALE, NEG)
    mx = jnp.maximum(jnp.max(s, axis=-1, keepdims=True), sink)
    e = jnp.exp(s - mx)
    es = jnp.exp(sink - mx)
    inv = 1.0 / (jnp.sum(e, axis=-1, keepdims=True) + es)
    return e * inv, es * inv


_KEYS = 2 * BLK


def _pair(ref, j):
    return ref[:, j * 128:(j + 1) * 128]


def _attn_fwd(q, kv, sinks, *, t, comm=None):
    nb = t // BLK

    def body(sink_ref, q_ref, kp_ref, kc_ref, o_ref):
        i = pl.program_id(0)
        for c in range(2):
            rows = slice(c * BLK, (c + 1) * BLK)
            valid = _attn_mask(2 * i + c)
            tiles = _attn_kv_tiles(kp_ref[...] if c == 0 else kc_ref[0:BLK, :], kc_ref[rows, :])
            s = [lax.dot_general(q_ref[rows, j * 128:(j + 1) * 128], tiles[j // 4][0], _NT,
                                 preferred_element_type=F32) for j in range(N_PAIR)]
            p = []
            for j in range(N_PAIR):
                pe, _ = _attn_probs(s[j][:, 0:_KEYS], sink_ref[0, 2 * j], valid)
                po, _ = _attn_probs(s[j][:, _KEYS:2 * _KEYS], sink_ref[0, 2 * j + 1], valid)
                p.append(jnp.concatenate([pe.astype(BF), po.astype(BF)], axis=1))
            for j in range(N_PAIR):
                o_ref[rows, j * 128:(j + 1) * 128] = jnp.dot(p[j], tiles[j // 4][1],
                                                             preferred_element_type=F32).astype(BF)

    return _hosted_call(
        body, comm, (sinks, q, kv, kv), name="attn_fwd", grid=(nb // 2,),
        in_specs=[pl.BlockSpec(memory_space=pltpu.SMEM),
                  pl.BlockSpec((2 * BLK, D), lambda i: (i, 0)),
                  pl.BlockSpec((BLK, 256), lambda i: (jnp.maximum(2 * i - 1, 0), 0)),
                  pl.BlockSpec((2 * BLK, 256), lambda i: (i, 0))],
        out_specs=[pl.BlockSpec((2 * BLK, D), lambda i: (i, 0))],
        out_shape=[jax.ShapeDtypeStruct((t, D), BF)],
        scratch_shapes=[], sem=("arbitrary",), nsteps=nb // 2, step_fn=lambda: pl.program_id(0))


def _attn_bwd(q, kv, sinks, do, *, t, comm=None):
    nb = t // BLK
    last = nb - 1

    def body(sink_ref, q_ref, kp_ref, kc_ref, do_ref, dq_ref, dkv_ref, ds_ref, carry_ref):
        i = pl.program_id(0)

        @pl.when(i == 0)
        def _():
            ds_ref[...] = jnp.zeros_like(ds_ref)
            carry_ref[...] = jnp.zeros_like(carry_ref)

        @pl.when(i < nb)
        def _():
            valid = _attn_mask(i)
            tiles = _attn_kv_tiles(kp_ref[...], kc_ref[...])
            lane1 = lax.broadcasted_iota(jnp.int32, (1, 128), 1)
            dsink = jnp.zeros((1, 128), F32)
            s = [lax.dot_general(_pair(q_ref, j), tiles[j // 4][0], _NT, preferred_element_type=F32)
                 for j in range(N_PAIR)]
            dp = [lax.dot_general(_pair(do_ref, j), tiles[j // 4][1], _NT, preferred_element_type=F32)
                  for j in range(N_PAIR)]
            p_all, ds_all = [], []
            for j in range(N_PAIR):
                halves = []
                for par in range(2):
                    cols = slice(par * _KEYS, (par + 1) * _KEYS)
                    p, ps = _attn_probs(s[j][:, cols], sink_ref[0, 2 * j + par], valid)
                    dpj = dp[j][:, cols]
                    dd = jnp.sum(p * dpj, axis=-1, keepdims=True)
                    dsink = dsink + jnp.where(lane1 == 2 * j + par,
                                              -jnp.sum(ps * dd, axis=0, keepdims=True), 0.0)
                    halves.append((p.astype(BF), (p * (dpj - dd)).astype(BF)))
                p_all.append(jnp.concatenate([halves[0][0], halves[1][0]], axis=1))
                ds_all.append(jnp.concatenate([halves[0][1], halves[1][1]], axis=1))
            for j in range(N_PAIR):
                dq_ref[:, j * 128:(j + 1) * 128] = (
                    jnp.dot(ds_all[j], tiles[j // 4][0], preferred_element_type=F32) * SCALE).astype(BF)
            ds_ref[...] += dsink
            gk, gv = [], []
            for h in range(2):
                grp = range(4 * h, 4 * h + 4)
                q_rows = jnp.concatenate([_pair(q_ref, j) for j in grp], axis=0)
                do_rows = jnp.concatenate([_pair(do_ref, j) for j in grp], axis=0)
                g_k = lax.dot_general(jnp.concatenate([ds_all[j] for j in grp], axis=0), q_rows, _TN,
                                      preferred_element_type=F32)
                g_v = lax.dot_general(jnp.concatenate([p_all[j] for j in grp], axis=0), do_rows, _TN,
                                      preferred_element_type=F32)
                gk.append((g_k[0:_KEYS], g_k[_KEYS:2 * _KEYS]))
                gv.append((g_v[0:_KEYS], g_v[_KEYS:2 * _KEYS]))
            lo = lax.broadcasted_iota(jnp.int32, (2 * BLK, 128), 1) < HEAD
            zero = jnp.zeros((2 * BLK, 128), F32)

            def unpad(g):
                return (jnp.where(lo, g[0][0] + pltpu.roll(g[0][1], HEAD, 1), zero)
                        + jnp.where(lo, zero, pltpu.roll(g[1][0], HEAD, 1) + g[1][1]))

            dk = unpad(gk) * SCALE
            dv = unpad(gv)
            dkv_ref[:, 0:128] = (carry_ref[:, 0:128] + dk[0:BLK]).astype(BF)
            dkv_ref[:, 128:256] = (carry_ref[:, 128:256] + dv[0:BLK]).astype(BF)
            carry_ref[:, 0:128] = dk[BLK:2 * BLK]
            carry_ref[:, 128:256] = dv[BLK:2 * BLK]

        @pl.when(i == nb)
        def _():
            dkv_ref[...] = carry_ref[...].astype(BF)

    return _hosted_call(
        body, comm, (sinks, q, kv, kv, do), name="attn_bwd", grid=(nb + 1,),
        in_specs=[pl.BlockSpec(memory_space=pltpu.SMEM),
                  pl.BlockSpec((BLK, D), lambda i: (jnp.minimum(i, last), 0)),
                  pl.BlockSpec((BLK, 256), lambda i: (jnp.clip(i - 1, 0, last), 0)),
                  pl.BlockSpec((BLK, 256), lambda i: (jnp.minimum(i, last), 0)),
                  pl.BlockSpec((BLK, D), lambda i: (jnp.minimum(i, last), 0))],
        out_specs=[pl.BlockSpec((BLK, D), lambda i: (jnp.minimum(i, last), 0)),
                   pl.BlockSpec((BLK, 256), lambda i: (jnp.maximum(i - 1, 0), 0)),
                   pl.BlockSpec((1, 128), lambda i: (0, 0))],
        out_shape=[jax.ShapeDtypeStruct((t, D), BF), jax.ShapeDtypeStruct((t, 256), BF),
                   jax.ShapeDtypeStruct((1, 128), F32)],
        scratch_shapes=[pltpu.VMEM((BLK, 256), F32)], sem=("arbitrary",), nsteps=nb + 1,
        step_fn=lambda: pl.program_id(0))


def _split3(v):
    h = v.astype(BF)
    r = v - h.astype(F32)
    m = r.astype(BF)
    lo = (r - m.astype(F32)).astype(BF)
    return jnp.concatenate([h, m, lo], axis=1)


def _apply01(mat, v):
    n = v.shape[1]
    r = jnp.dot(mat, _split3(v), preferred_element_type=F32)
    return r[:, 0:n] + r[:, n:2 * n] + r[:, 2 * n:3 * n]


def _hgrn_gates(hq, hf, lb):
    sq = _sig(hq)
    sg = _sig(hf)
    f = lb + (1.0 - lb) * sg
    return hq * sq, (1.0 - lb) * (1.0 - sg), jnp.log(f), sq, sg, f


def _tri(upper):
    r = lax.broadcasted_iota(jnp.int32, (CH, CH), 0)
    c = lax.broadcasted_iota(jnp.int32, (CH, CH), 1)
    return (c >= r) if upper else (c <= r)


def _lb_from_logits(lg_ref):
    return 1.0 / (1.0 + jnp.exp(lg_ref[1:2, :] - lg_ref[0:1, :]))


def _hgrn_fwd(h3, hf, logits, norm_g, *, t, comm=None):
    nc = t // CH
    nt_dims = (((1,), (1,)), ((), ()))
    tn_dims = (((0,), (0,)), ((), ()))

    def body(h_ref, hf_ref, lg_ref, ng_ref, y_ref, o_ref, st_ref, s_scr, b_scr, qa_s, ka_s, qb_s, kb_s, v_s):
        @pl.when(pl.program_id(0) == 0)
        def _():
            s_scr[...] = jnp.zeros_like(s_scr)

        heads = [slice(h * HG_K, (h + 1) * HG_K) for h in range(HG_HEADS)]
        causal = _tri(False)
        lb = _lb_from_logits(lg_ref)
        for c in range(HG_SUB):
            rows = slice(c * CH, (c + 1) * CH)
            q, k, g, _, _, _ = _hgrn_gates(h_ref[rows, 0:D].astype(F32), hf_ref[rows, :], lb)
            b_scr[...] = _apply01(jnp.where(causal, 1.0, 0.0).astype(BF), g)
            b = b_scr[...]
            b_mid = b_scr[CH // 2 - 1:CH // 2, :]
            b_last = b_scr[CH - 1:CH, :]
            qa_s[...] = (q * jnp.exp(b - b_mid)).astype(BF)
            ka_s[...] = (k * jnp.exp(b_mid - b)).astype(BF)
            qb_s[...] = (q * jnp.exp(b)).astype(BF)
            kb_s[...] = (k * jnp.exp(b_last - b)).astype(BF)
            v_s[...] = h_ref[rows, D:2 * D]
            dec = jnp.exp(b_last)
            st_ref[c] = s_scr[...].astype(BF)
            a = [jnp.where(causal, lax.dot_general(qa_s[:, sl], ka_s[:, sl], nt_dims, preferred_element_type=F32),
                           0.0).astype(BF) for sl in heads]
            for h, sl in enumerate(heads):
                o_ref[rows, sl] = (jnp.dot(a[h], v_s[:, sl], preferred_element_type=F32)
                                   + lax.dot_general(qb_s[:, sl], s_scr[h].astype(BF), nt_dims,
                                                     preferred_element_type=F32))
            for h, sl in enumerate(heads):
                s_scr[h] = dec[:, sl] * s_scr[h] + lax.dot_general(v_s[:, sl], kb_s[:, sl], tn_dims,
                                                                   preferred_element_type=F32)
            for h, sl in enumerate(heads):
                o = o_ref[rows, sl]
                on = o * lax.rsqrt(jnp.mean(o * o, axis=-1, keepdims=True) + EPS)
                gate = _sig(h_ref[rows, 2 * D + h * HG_K:2 * D + (h + 1) * HG_K].astype(F32))
                y_ref[rows, sl] = (on * ng_ref[:, sl] * gate).astype(BF)

    half = lambda: pltpu.VMEM((CH, D), BF)
    blk = HG_SUB * CH
    return _hosted_call(
        body, comm, (h3, hf, logits, norm_g), name="hgrn_fwd", grid=(nc // HG_SUB,),
        in_specs=[pl.BlockSpec((blk, 3 * D), lambda n: (n, 0)),
                  pl.BlockSpec((blk, D), lambda n: (n, 0)),
                  pl.BlockSpec((2, D), lambda n: (0, 0)),
                  pl.BlockSpec((1, D), lambda n: (0, 0))],
        out_specs=[pl.BlockSpec((blk, D), lambda n: (n, 0)),
                   pl.BlockSpec((blk, D), lambda n: (n, 0)),
                   pl.BlockSpec((HG_SUB, HG_HEADS, HG_K, HG_K), lambda n: (n, 0, 0, 0))],
        out_shape=[jax.ShapeDtypeStruct((t, D), BF), jax.ShapeDtypeStruct((t, D), F32),
                   jax.ShapeDtypeStruct((nc, HG_HEADS, HG_K, HG_K), BF)],
        scratch_shapes=[pltpu.VMEM((HG_HEADS, HG_K, HG_K), F32), pltpu.VMEM((CH, D), F32),
                        half(), half(), half(), half(), half()],
        sem=("arbitrary",), nsteps=nc // HG_SUB, step_fn=lambda: pl.program_id(0))


def _hgrn_bwd(h3, hf, logits, norm_g, o_pre, states, dy, *, t, comm=None):
    nc = t // CH
    nt_dims = (((1,), (1,)), ((), ()))
    tn_dims = (((0,), (0,)), ((), ()))

    def body(h_ref, hf_ref, lg_ref, ng_ref, o_ref, st_ref, dy_ref, dh_ref, dlg_ref, dng_ref, ds_scr, dlb_scr,
             b_scr, tail_s, e_qa, e_ka, e_qb, e_kb, q_s, k_s, dqa_s, dka_s, dqb_s, dkb_s,
             qa_s, ka_s, qb_s, kb_s, v_s, do_s):
        n = pl.program_id(0)

        @pl.when(n == 0)
        def _():
            ds_scr[...] = jnp.zeros_like(ds_scr)
            dlb_scr[...] = jnp.zeros_like(dlb_scr)
            dng_ref[...] = jnp.zeros_like(dng_ref)

        heads = [slice(h * HG_K, (h + 1) * HG_K) for h in range(HG_HEADS)]
        lb = _lb_from_logits(lg_ref)
        causal = _tri(False)

        def chunk(c):
            rows = slice(c * CH, (c + 1) * CH)
            hq = h_ref[rows, 0:D].astype(F32)
            q, k, g, sq, sg, f = _hgrn_gates(hq, hf_ref[rows, :], lb)
            b_scr[...] = _apply01(jnp.where(causal, 1.0, 0.0).astype(BF), g)
            b = b_scr[...]
            b_mid = b_scr[CH // 2 - 1:CH // 2, :]
            b_last = b_scr[CH - 1:CH, :]
            q_s[...] = q
            k_s[...] = k
            for e_ref, s_ref, base, expo in ((e_qa, qa_s, q, b - b_mid), (e_ka, ka_s, k, b_mid - b),
                                             (e_qb, qb_s, q, b), (e_kb, kb_s, k, b_last - b)):
                e = jnp.exp(expo)
                e_ref[...] = e
                s_ref[...] = (base * e).astype(BF)
            v_s[...] = h_ref[rows, D:2 * D]
            dec = jnp.exp(b_last)
            for h, sl in enumerate(heads):
                gcol = slice(3 * D + h * HG_K, 3 * D + (h + 1) * HG_K)
                ngh = ng_ref[:, sl]
                sgate = _sig(h_ref[rows, 2 * D + h * HG_K:2 * D + (h + 1) * HG_K].astype(F32))
                o = o_ref[rows, sl]
                r = lax.rsqrt(jnp.mean(o * o, axis=-1, keepdims=True) + EPS)
                on = o * r
                dyh = dy_ref[rows, sl]
                dh_ref[rows, gcol] = (dyh * on * ngh * sgate * (1.0 - sgate)).astype(BF)
                dng_ref[:, sl] += jnp.sum(dyh * on * sgate, axis=0, keepdims=True)
                don = dyh * ngh * sgate
                do_s[:, sl] = (r * (don - on * jnp.mean(don * on, axis=-1, keepdims=True))).astype(BF)
            a = [jnp.where(causal, lax.dot_general(qa_s[:, sl], ka_s[:, sl], nt_dims, preferred_element_type=F32),
                           0.0).astype(BF) for sl in heads]
            da = [jnp.where(causal, lax.dot_general(do_s[:, sl], v_s[:, sl], nt_dims, preferred_element_type=F32),
                            0.0).astype(BF) for sl in heads]
            for h, sl in enumerate(heads):
                dh_ref[rows, 2 * D + h * HG_K:2 * D + (h + 1) * HG_K] = (
                    lax.dot_general(a[h], do_s[:, sl], tn_dims, preferred_element_type=F32)
                    + lax.dot_general(kb_s[:, sl], ds_scr[h].astype(BF), nt_dims, preferred_element_type=F32)
                ).astype(BF)
            for h, sl in enumerate(heads):
                dqa_s[:, sl] = jnp.dot(da[h], ka_s[:, sl], preferred_element_type=F32)
            for h, sl in enumerate(heads):
                dka_s[:, sl] = lax.dot_general(da[h], qa_s[:, sl], tn_dims, preferred_element_type=F32)
            for h, sl in enumerate(heads):
                dqb_s[:, sl] = jnp.dot(do_s[:, sl], st_ref[c, h], preferred_element_type=F32)
            for h, sl in enumerate(heads):
                dkb_s[:, sl] = jnp.dot(v_s[:, sl], ds_scr[h].astype(BF), preferred_element_type=F32)
            for h, sl in enumerate(heads):
                tail_s[:, sl] = jnp.sum(dec[:, sl] * st_ref[c, h].astype(F32) * ds_scr[h], axis=0, keepdims=True)
            for h, sl in enumerate(heads):
                ds_scr[h] = (lax.dot_general(do_s[:, sl], qb_s[:, sl], tn_dims, preferred_element_type=F32)
                             + dec[:, sl] * ds_scr[h])
            qv, kv = q_s[...], k_s[...]
            dqa, dka, dqb, dkb = dqa_s[...], dka_s[...], dqb_s[...], dkb_s[...]
            eqa, eka, eqb, ekb = e_qa[...], e_ka[...], e_qb[...], e_kb[...]
            dkb_kb = dkb * (kv * ekb)
            db_last = jnp.sum(dkb_kb, axis=0, keepdims=True) + tail_s[...]
            last_row = lax.broadcasted_iota(jnp.int32, (CH, D), 0) == CH - 1
            db = (dqa * (qv * eqa) - dka * (kv * eka) + dqb * (qv * eqb) - dkb_kb
                  + jnp.where(last_row, db_last, 0.0))
            dg = _apply01(jnp.where(_tri(True), 1.0, 0.0).astype(BF), db)
            dq = dqa * eqa + dqb * eqb
            dk = dka * eka + dkb * ekb
            dh_ref[rows, 0:D] = (dq * sq * (1.0 + hq * (1.0 - sq))).astype(BF)
            dfk = dg / f - dk
            dh_ref[rows, D:2 * D] = ((1.0 - lb) * dfk * sg * (1.0 - sg)).astype(BF)
            dlb_scr[...] += jnp.sum((1.0 - sg) * dfk, axis=0, keepdims=True)

        for c in reversed(range(HG_SUB)):
            chunk(c)

        @pl.when(n == nc // HG_SUB - 1)
        def _():
            dl0 = dlb_scr[...] * lb * (1.0 - lb)
            dlg_ref[0:1, :] = dl0
            dlg_ref[1:2, :] = -dl0

    steps = nc // HG_SUB
    blk = HG_SUB * CH
    rev = lambda n: (steps - 1 - n, 0)
    return _hosted_call(
        body, comm, (h3, hf, logits, norm_g, o_pre, states, dy), name="hgrn_bwd", grid=(steps,),
        in_specs=[pl.BlockSpec((blk, 3 * D), rev),
                  pl.BlockSpec((blk, D), rev),
                  pl.BlockSpec((2, D), lambda n: (0, 0)),
                  pl.BlockSpec((1, D), lambda n: (0, 0)),
                  pl.BlockSpec((blk, D), rev),
                  pl.BlockSpec((HG_SUB, HG_HEADS, HG_K, HG_K), lambda n: (steps - 1 - n, 0, 0, 0)),
                  pl.BlockSpec((blk, D), rev)],
        out_specs=[pl.BlockSpec((blk, 4 * D), rev),
                   pl.BlockSpec((2, D), lambda n: (0, 0)),
                   pl.BlockSpec((1, D), lambda n: (0, 0))],
        out_shape=[jax.ShapeDtypeStruct((t, 4 * D), BF), jax.ShapeDtypeStruct((2, D), F32),
                   jax.ShapeDtypeStruct((1, D), F32)],
        scratch_shapes=([pltpu.VMEM((HG_HEADS, HG_K, HG_K), F32), pltpu.VMEM((1, D), F32),
                         pltpu.VMEM((CH, D), F32), pltpu.VMEM((1, D), F32)]
                        + [pltpu.VMEM((CH, D), F32)] * 10 + [pltpu.VMEM((CH, D), BF)] * 6),
        sem=("arbitrary",), nsteps=steps, step_fn=lambda: pl.program_id(0))


def _place():
    x, y, c = lax.axis_index("x"), lax.axis_index("y"), lax.axis_index("c")
    return x, y, c, [(1 - x, y), (x, 1 - y), (1 - x, 1 - y)]


def _gather_comm(shards, mids):
    n, pieces = len(shards), len(mids)
    r = [s.shape[0] for s in shards]
    tile = 16
    cut = [[(rw // tile * p // pieces) * tile for p in range(pieces + 1)] for rw in r]
    size = [[cut[w][p + 1] - cut[w][p] for p in range(pieces)] for w in range(n)]

    def tools(ins, outs, sems):
        send_sems, recv_sems, local_sems = sems
        x, y, c, _ = _place()
        me, sib = (x, y, c), (x, y, 1 - c)
        near = [(x ^ c, y ^ (1 - c), c), (x ^ (1 - c), y ^ c, c), (1 - x, 1 - y, c)]

        def rows(w, p, dev):
            return outs[w].at[pl.ds((4 * dev[0] + 2 * dev[1] + dev[2]) * r[w] + cut[w][p], size[w][p]), :]

        def copy(kind, w, p, block, to, own=False):
            src = ins[w].at[pl.ds(cut[w][p], size[w][p]), :] if own else rows(w, p, block)
            return pltpu.make_async_remote_copy(
                src_ref=src, dst_ref=rows(w, p, block), send_sem=send_sems.at[p, kind],
                recv_sem=recv_sems.at[p, kind], device_id=to, device_id_type=MESH)

        def all_of(kind, p):
            whole = outs[0].at[pl.ds(0, sum(size[w][p] for w in range(n))), :]
            return pltpu.make_async_remote_copy(
                src_ref=whole, dst_ref=whole, send_sem=send_sems.at[p, kind], recv_sem=recv_sems.at[p, kind],
                device_id=me, device_id_type=MESH)

        mine = [pltpu.make_async_copy(ins[w], outs[w].at[pl.ds((4 * x + 2 * y + c) * r[w], r[w]), :],
                                      local_sems.at[w]) for w in range(n)]
        return near, me, sib, copy, all_of, mine

    def start(ins, outs, sems):
        near, me, sib, copy, _, mine = tools(ins, outs, sems)
        for cp in mine:
            cp.start()
        for p in range(pieces):
            for w in range(n):
                copy(0, w, p, me, sib, own=True).start()
                copy(1, w, p, me, near[0], own=True).start()
                copy(2, w, p, me, near[1], own=True).start()

    def pass_diagonal(p, near, sib, copy, all_of):
        all_of(3, p).wait_recv()
        for w in range(n):
            copy(6, w, p, near[2], sib).start()

    def pass_on(p):
        def phase(ins, outs, sems):
            near, _, sib, copy, all_of, _ = tools(ins, outs, sems)
            all_of(1, p).wait_recv()
            for w in range(n):
                copy(3, w, p, near[0], near[1]).start()
                copy(4, w, p, near[0], sib).start()
            all_of(2, p).wait_recv()
            for w in range(n):
                copy(5, w, p, near[1], sib).start()
            if p > 0:
                pass_diagonal(p - 1, near, sib, copy, all_of)
        return phase

    def finish(ins, outs, sems):
        near, _, sib, copy, all_of, mine = tools(ins, outs, sems)
        pass_diagonal(pieces - 1, near, sib, copy, all_of)
        for p in range(pieces):
            all_of(0, p).wait_recv()
            for kind in (4, 5, 6):
                all_of(kind, p).wait_recv()
            for kind in range(7):
                all_of(kind, p).wait_send()
        for cp in mine:
            cp.wait()

    return _Comm(shards, [jax.ShapeDtypeStruct((N_DEV * rw, D), BF) for rw in r],
                 [pltpu.SemaphoreType.DMA((pieces, 7)), pltpu.SemaphoreType.DMA((pieces, 7)),
                  pltpu.SemaphoreType.DMA((n,))],
                 [(0.0, start)] + [(f, pass_on(p)) for p, f in enumerate(mids)] + [(1.0, finish)])


def _pair_comm(grads):
    n = len(grads)
    r = [g.shape[0] // N_DEV for g in grads]

    def start(ins, outs, sems):
        send_sems, recv_sems = sems
        x, y, c, _ = _place()
        for w in range(n):
            for a in range(N_CHIP):
                pltpu.make_async_remote_copy(
                    src_ref=ins[w].at[pl.ds((2 * a + 1 - c) * r[w], r[w]), :], dst_ref=outs[w].at[a],
                    send_sem=send_sems.at[w], recv_sem=recv_sems.at[w],
                    device_id=(x, y, 1 - c), device_id_type=MESH).start()

    def finish(ins, outs, sems):
        send_sems, recv_sems = sems
        x, y, c, _ = _place()
        for w in range(n):
            pltpu.make_async_remote_copy(
                src_ref=outs[w], dst_ref=outs[w], send_sem=send_sems.at[w], recv_sem=recv_sems.at[w],
                device_id=(x, y, c), device_id_type=MESH).wait()

    return _Comm(grads, [jax.ShapeDtypeStruct((N_CHIP, rw, D), BF) for rw in r],
                 [pltpu.SemaphoreType.DMA((n,)), pltpu.SemaphoreType.DMA((n,))],
                 [(0.0, start), (1.0, finish)])


def _pair_add(grads, gots, core, *, name):
    n, r = len(grads), gots[0].shape[1]
    tr = r if r <= 128 else r // 2
    steps = r // tr
    tile = lambda k: (lambda s: jnp.clip(s - k * steps, 0, steps - 1))

    def body(c_ref, *refs):
        g_refs, got_refs, o_refs = refs[:n], refs[n:2 * n], refs[2 * n:]
        s = pl.program_id(0)
        for k in range(n):
            @pl.when(jnp.logical_and(s >= k * steps, s < (k + 1) * steps))
            def _(k=k):
                o_refs[k][...] = (g_refs[k][:, 0].astype(F32) + got_refs[k][...].astype(F32)).astype(BF)

    grid_spec = pltpu.PrefetchScalarGridSpec(
        num_scalar_prefetch=1, grid=(n * steps,),
        in_specs=[pl.BlockSpec((N_CHIP, 1, tr, D), lambda s, c_ref, k=k: (0, c_ref[0], tile(k)(s), 0))
                  for k in range(n)]
        + [pl.BlockSpec((N_CHIP, tr, D), lambda s, c_ref, k=k: (0, tile(k)(s), 0)) for k in range(n)],
        out_specs=[pl.BlockSpec((N_CHIP, tr, D), lambda s, c_ref, k=k: (0, tile(k)(s), 0)) for k in range(n)])
    return _pcall(body, name=name, grid_spec=grid_spec,
                  out_shape=[jax.ShapeDtypeStruct((N_CHIP, r, D), BF)] * n,
                  compiler_params=_cp(("arbitrary",)))(
                      core, *[g.reshape(N_CHIP, 2, r, D) for g in grads], *gots)


def _chip_comm(pair_sums):
    n = len(pair_sums)
    r = [p.shape[1] for p in pair_sums]
    off = [sum(r[:w]) for w in range(n)]

    def tools(ins, outs, sems):
        send_sems, recv_sems, local_sems = sems
        x, y, c, chips = _place()
        my_chip = 2 * x + y

        def slot(w):
            return outs[0].at[my_chip, pl.ds(off[w], r[w]), :]

        own = [pltpu.make_async_copy(ins[w].at[my_chip], slot(w), local_sems.at[w]) for w in range(n)]
        return x, y, c, chips, my_chip, slot, own, send_sems, recv_sems

    def start(ins, outs, sems):
        x, y, c, chips, my_chip, slot, own, send_sems, recv_sems = tools(ins, outs, sems)
        for cp in own:
            cp.start()
        for j, chip in enumerate(chips):
            for w in range(n):
                pltpu.make_async_remote_copy(
                    src_ref=ins[w].at[2 * chip[0] + chip[1]], dst_ref=slot(w), send_sem=send_sems.at[j],
                    recv_sem=recv_sems.at[j], device_id=(*chip, c), device_id_type=MESH).start()

    def finish(ins, outs, sems):
        x, y, c, chips, my_chip, slot, own, send_sems, recv_sems = tools(ins, outs, sems)
        whole = outs[0].at[my_chip]
        for j in range(3):
            pltpu.make_async_remote_copy(
                src_ref=whole, dst_ref=whole, send_sem=send_sems.at[j], recv_sem=recv_sems.at[j],
                device_id=(x, y, c), device_id_type=MESH).wait()
        for cp in own:
            cp.wait()

    return _Comm(pair_sums, [jax.ShapeDtypeStruct((N_CHIP, sum(r), D), BF)],
                 [pltpu.SemaphoreType.DMA((3,)), pltpu.SemaphoreType.DMA((3,)), pltpu.SemaphoreType.DMA((n,))],
                 [(0.0, start), (1.0, finish)])


def _adam_math(w, g, m, v):
    m = ADAM_B1 * m + (1.0 - ADAM_B1) * g
    v = ADAM_B2 * v + (1.0 - ADAM_B2) * (g * g)
    m_hat = m / (1.0 - ADAM_B1 ** ADAM_STEP)
    v_hat = v / (1.0 - ADAM_B2 ** ADAM_STEP)
    delta = -ADAM_LR * (m_hat / (jnp.sqrt(v_hat) + ADAM_EPS) + ADAM_WD * w)
    return delta, m, v


SMALL = (("norm_mix_g", (1, D), 0), ("hgrn_norm_g", (1, D), 1), ("norm_ffn_g", (1, D), 2),
         ("norm_final_g", (1, D), 3), ("hgrn_lb_logits", (2, D), 4), ("attn_sinks", (1, 16), 6),
         ("b_in", (1, IN_W), 8))
LOSS_ROW = 7


def _small_allreduce_adam(grads, loss_row, params):
    n = len(SMALL)

    def rows_of(ref, shape, row):
        r, w = shape
        if w <= D:
            return ref[row:row + r, 0:w]
        pieces = [ref[row + k:row + k + 1, :] for k in range(-(-w // D))]
        return jnp.concatenate(pieces, axis=1)[:, 0:w]

    def body(*refs):
        g_refs, loss_ref = refs[:n], refs[n]
        wmv = refs[n + 1:4 * n + 1]
        loss_out = refs[4 * n + 1]
        outs = refs[4 * n + 2:8 * n + 2]
        mine, total, gath, send_sems, recv_sems = refs[8 * n + 2:]
        x, y, c, _ = _place()
        me = 4 * x + 2 * y + c
        mine[...] = jnp.zeros_like(mine)
        for g_ref, (_, (r, w), row) in zip(g_refs, SMALL):
            for k in range(-(-w // D)):
                wk = min(D, w - k * D)
                mine[row + k:row + k + r, 0:wk] = g_ref[:, k * D:k * D + wk]
        mine[LOSS_ROW:LOSS_ROW + 1, 0:128] = loss_ref[...]
        gath[me] = mine[...]
        cps = []
        for d in range(1, N_DEV):
            peer = (x ^ (d >> 2), y ^ ((d >> 1) & 1), c ^ (d & 1))
            cps.append(pltpu.make_async_remote_copy(
                src_ref=mine, dst_ref=gath.at[me], send_sem=send_sems.at[d - 1],
                recv_sem=recv_sems.at[d - 1], device_id=peer, device_id_type=MESH))
        for cp in cps:
            cp.start()
        for cp in cps:
            cp.wait()
        g = gath[0]
        for k in range(1, N_DEV):
            g = g + gath[k]
        total[...] = g
        loss_out[...] = total[LOSS_ROW:LOSS_ROW + 1, 0:128]
        for i, (_, shape, row) in enumerate(SMALL):
            gi = rows_of(total, shape, row)
            w_ref, m_ref, v_ref = wmv[3 * i:3 * i + 3]
            o = outs[4 * i:4 * i + 4]
            o[0][...] = gi
            o[1][...], o[2][...], o[3][...] = _adam_math(w_ref[...], gi, m_ref[...], v_ref[...])

    vm = pl.BlockSpec(memory_space=pltpu.VMEM)
    ins = [grads[name] for name, _, _ in SMALL] + [loss_row]
    for name, _, _ in SMALL:
        ins += list(params[name])
    out_shape = [jax.ShapeDtypeStruct((1, 128), F32)]
    for _, shape, _ in SMALL:
        out_shape += [jax.ShapeDtypeStruct(shape, F32)] * 4
    res = _pcall(body, name="small_allreduce_adam", in_specs=[vm] * len(ins), out_specs=[vm] * len(out_shape),
                 out_shape=out_shape,
                 scratch_shapes=[pltpu.VMEM((SMALL_ROWS, D), F32), pltpu.VMEM((SMALL_ROWS, D), F32),
                                 pltpu.VMEM((N_DEV, SMALL_ROWS, D), F32),
                                 pltpu.SemaphoreType.DMA((N_DEV - 1,)), pltpu.SemaphoreType.DMA((N_DEV - 1,))],
                 compiler_params=pltpu.CompilerParams(has_side_effects=True))(*ins)
    return res[0], {name: res[1 + 4 * i:5 + 4 * i] for i, (name, _, _) in enumerate(SMALL)}


def _adam(ws, parts, ms, vs, *, name):
    n, rows = len(ws), ws[0].shape[0]
    tr = rows if rows <= 128 else rows // 2
    steps = rows // tr
    tile = lambda k: (lambda s: jnp.clip(s - k * steps, 0, steps - 1))

    def body(*refs):
        w_refs, m_refs, v_refs, p_ref = refs[:n], refs[n:2 * n], refs[2 * n:3 * n], refs[3 * n]
        o_refs = refs[3 * n + 1:]
        s = pl.program_id(0)
        for k in range(n):
            @pl.when(jnp.logical_and(s >= k * steps, s < (k + 1) * steps))
            def _(k=k):
                g = p_ref[0].astype(F32)
                for a in range(1, N_CHIP):
                    g = g + p_ref[a].astype(F32)
                o = o_refs[4 * k:4 * k + 4]
                o[0][...] = g
                o[1][...], o[2][...], o[3][...] = _adam_math(w_refs[k][...], g, m_refs[k][...], v_refs[k][...])

    spec = lambda k: pl.BlockSpec((tr, D), lambda s, k=k: (tile(k)(s), 0))
    res = _pcall(body, name=name, grid=(n * steps,),
                 in_specs=[spec(k) for k in range(n)] * 3 + [pl.BlockSpec((N_CHIP, tr, D), lambda s: (0, s, 0))],
                 out_specs=[spec(k) for k in range(n) for _ in range(4)],
                 out_shape=[jax.ShapeDtypeStruct((rows, D), F32)] * (4 * n),
                 compiler_params=_cp(("arbitrary",)))(*ws, *ms, *vs, parts)
    return [res[4 * k:4 * k + 4] for k in range(n)]


def _step(x, tgt, shards, norm_mix_g, b_in, sinks, logits, hgrn_norm_g, norm_ffn_g, norm_final_g):
    t = x.shape[0]
    core = lax.axis_index("c").astype(jnp.int32).reshape(1)

    u1, (win_t,) = _rms_fwd(x, norm_mix_g, tm=512, name="rms_mix", comm=_gather_comm(shards[0:1], (0.2, 0.4, 0.6, 0.8)))
    (q, kv, h3, hf, gates), (wg_t, wba, wbh, wout) = _inproj_fwd(
        u1, win_t, b_in, t=t, comm=_gather_comm([shards[1]] + shards[4:7], (0.3, 0.5, 0.7, 0.9)))
    (y_attn,), _ = _attn_fwd(q, kv, sinks, t=t)
    (y_hgrn, o_pre, states), (wu_t, wd) = _hgrn_fwd(h3, hf, logits, hgrn_norm_g, t=t,
                                                    comm=_gather_comm(shards[2:4], (0.3, 0.5, 0.7, 0.9)))
    col = lambda j: j
    first, second = (lambda j: 0), (lambda j: 1)
    gate_tiles = [(gates, D, first), (gates, D, second)]

    def merge(prods, ex):
        (ya_, yb_), (ga, gb) = prods, ex
        sa, sb = _sig(ga.astype(F32)), _sig(gb.astype(F32))
        return sa, sb, ya_ * sa * (1.0 - sa), yb_ * sb * (1.0 - sb), sa * ya_ + sb * yb_

    sig_a, sig_b, dgate_a, dgate_b, merged = _fmm(
        [y_attn, y_hgrn], [(0, wba, False), (1, wbh, False)], gate_tiles, merge,
        [(BF, D, D, first)] * 5, m=t, n=D, tm=512, tn=D, name="branch_merge")
    def resid_norm(prods, ex):
        (p,), (xv, gv) = prods, ex
        hv = xv + p
        return hv, hv * lax.rsqrt(jnp.mean(hv * hv, axis=-1, keepdims=True) + EPS) * gv

    h1, u2 = _fmm([merged], [(0, wout, False)], [(x, D, first)], resid_norm, [(F32, D, D, first), (BF, D, D, first)],
                  m=t, n=D, tm=1024, tn=D, name="out_proj", vecs=[norm_ffn_g])

    def swiglu(prods, ex):
        g_, u_ = prods
        s = _sig(g_)
        silu = g_ * s
        return u_ * s * (1.0 + g_ * (1.0 - s)), silu, silu * u_

    dz_dgate, dz_dup, z = _fmm([u2], [(0, wg_t, True), (0, wu_t, True)], [], swiglu,
                               [(BF, FFN, FFN // 2, col)] * 3, m=t, n=FFN, tm=1024, tn=FFN // 2,
                               name="ffn_gate_up", cols_outer=True)
    def loss_head(prods, ex):
        (p,), (hv, tv, gv) = prods, ex
        hv = hv + p
        r = lax.rsqrt(jnp.mean(hv * hv, axis=-1, keepdims=True) + EPS)
        xh = hv * r
        err = xh * gv - tv
        lp = jnp.sum(jnp.sum(err * err, axis=1, keepdims=True), axis=0, keepdims=True) * (0.5 / D)
        dy = err * (1.0 / D)
        dxh = dy * gv
        dh = r * (dxh - xh * jnp.mean(dxh * xh, axis=-1, keepdims=True))
        return dh, dh, jnp.sum(dy * xh, axis=0, keepdims=True), jnp.broadcast_to(lp, (1, 128))

    dh2, dh2_b, d_norm_final, loss_row = _fmm(
        [z], [(0, wd, False)], [(h1, D, first), (tgt, D, first)], loss_head, [(F32, D, D, first), (BF, D, D, first)],
        m=t, n=D, tm=512, tn=D, name="ffn_down_loss", vecs=[norm_final_g], sums=[D, 128])

    def swiglu_bwd(prods, ex):
        (dz,), (da_, db_) = prods, ex
        return dz * da_.astype(F32), dz * db_.astype(F32)

    ffn_tiles = [(dz_dgate, FFN // 2, col), (dz_dup, FFN // 2, col)]
    dgt, dup = _fmm([dh2_b], [(0, wd, True)], ffn_tiles, swiglu_bwd, [(BF, FFN, FFN // 2, col)] * 2,
                    m=t, n=FFN, tm=1024, tn=FFN // 2, name="d_gate_up", cols_outer=True)
    (d_wd,) = _wgrad([z], dh2_b, name="d_w_down")
    (du2,) = _fmm([dgt, dup], [(0, wg_t, False), (1, wu_t, False)], [], lambda prods, ex: (prods[0] + prods[1],),
                  [(F32, D, 512, col)], m=t, n=D, tm=1024, tn=512, name="d_u2")
    d_wg, d_wu = _wgrad([dgt, dup], u2, name="d_w_gate_up")
    dh1, dh1_b, d_norm_ffn = _rms_bwd(du2, h1, norm_ffn_g, dh2, tm=512, name="rms_ffn_bwd")
    (d_wout,) = _wgrad([merged], dh1_b, name="d_w_out")

    def merge_bwd(prods, ex):
        (dm,), (sa, sb, ca, cb, wa, wb) = prods, ex
        dgate = jnp.concatenate([dm * ca.astype(F32), dm * cb.astype(F32)], axis=1)
        dya_ = (dm * sa.astype(F32)).astype(BF)
        dyb_ = (dm * sb.astype(F32)).astype(BF)
        return (dya_, dyb_, dgate, lax.dot_general(dya_, wa, _NT, preferred_element_type=F32),
                lax.dot_general(dyb_, wb, _NT, preferred_element_type=F32))

    ffn_grads = (d_wg, d_wu, d_wd)
    (dya, dyb, dgates, dy_attn, dy_hgrn), got = _fmm(
        [dh1_b], [(0, wout, True)], [(a, D, first) for a in (sig_a, sig_b, dgate_a, dgate_b)], merge_bwd,
        [(BF, D, D, first), (BF, D, D, first), (BF, 2 * D, 2 * D, first), (BF, D, D, first), (F32, D, D, first)],
        m=t, n=D, tm=512, tn=D, name="d_merge", consts=[wba, wbh], comm=_pair_comm(ffn_grads))
    pair_ffn = _pair_add(ffn_grads, got, core, name="pair_add_ffn")
    (d_wba,) = _wgrad([y_attn], dya, name="d_w_ba")
    (d_wbh,) = _wgrad([y_hgrn], dyb, name="d_w_bh")
    sq_grads = (d_wba, d_wbh, d_wout)
    (dh4, d_logits, d_hgrn_norm), (parts_ffn, *got) = _hgrn_bwd(
        h3, hf, logits, hgrn_norm_g, o_pre, states, dy_hgrn, t=t,
        comm=_both(_chip_comm(pair_ffn), _pair_comm(sq_grads)))
    pair_sq = _pair_add(sq_grads, got, core, name="pair_add_sq")
    (dq, dkv, d_sinks), (parts_sq,) = _attn_bwd(q, kv, sinks, dy_attn, t=t, comm=_chip_comm(pair_sq))
    dps = (dq, dkv, dh4, dgates)
    d_win_t, d_b_in = _inproj_bwd_w(dps, u1, t=t)
    half0, got_in = _inproj_bwd_x(dps, win_t, x, norm_mix_g, dh1, t=t, part=0, comm=_pair_comm([d_win_t]))
    pair_in = _pair_add([d_win_t], got_in, core, name="pair_add_w_in")
    (grad_x, d_norm_mix), (parts_in,) = _inproj_bwd_x(dps, win_t, x, norm_mix_g, dh1, t=t, part=1, prev=half0,
                                                      comm=_chip_comm(pair_in))

    small_grads = (d_norm_mix, d_b_in, d_sinks, d_logits, d_hgrn_norm, d_norm_ffn, d_norm_final)
    return loss_row, grad_x, (parts_in, parts_ffn, parts_sq), small_grads


def kernel(x, norm_mix_g, w_in, b_in, attn_sinks, hgrn_lb_logits, hgrn_norm_g, w_branch_attn, w_branch_hgrn, w_out, norm_ffn_g, w_ffn_gate, w_ffn_up, w_ffn_down, norm_final_g, loss_target, m_norm_mix_g, m_w_in, m_b_in, m_attn_sinks, m_hgrn_lb_logits, m_hgrn_norm_g, m_w_branch_attn, m_w_branch_hgrn, m_w_out, m_norm_ffn_g, m_w_ffn_gate, m_w_ffn_up, m_w_ffn_down, m_norm_final_g, v_norm_mix_g, v_w_in, v_b_in, v_attn_sinks, v_hgrn_lb_logits, v_hgrn_norm_g, v_w_branch_attn, v_w_branch_hgrn, v_w_out, v_norm_ffn_g, v_w_ffn_gate, v_w_ffn_up, v_w_ffn_down, v_norm_final_g):
    shards = [w_in[0].T.astype(BF), w_ffn_gate[0].T.astype(BF), w_ffn_up[0].T.astype(BF),
              w_ffn_down[0].astype(BF), w_branch_attn[0].astype(BF), w_branch_hgrn[0].astype(BF),
              w_out[0].astype(BF)]
    loss_row, grad_x, grad_parts, small_grads = _step(
        x[0], loss_target[0], shards, norm_mix_g, b_in, attn_sinks, hgrn_lb_logits, hgrn_norm_g,
        norm_ffn_g, norm_final_g.reshape(1, D))

    d_norm_mix, d_b_in, d_sinks, d_logits, d_hgrn_norm, d_norm_ffn, d_norm_final = small_grads
    row = lambda a: a.reshape(1, D)
    loss_out, small = _small_allreduce_adam(
        dict(norm_mix_g=d_norm_mix, hgrn_norm_g=d_hgrn_norm, norm_ffn_g=d_norm_ffn, norm_final_g=d_norm_final,
             hgrn_lb_logits=d_logits, attn_sinks=d_sinks, b_in=d_b_in),
        loss_row,
        dict(norm_mix_g=(norm_mix_g, m_norm_mix_g, v_norm_mix_g), hgrn_norm_g=(hgrn_norm_g, m_hgrn_norm_g, v_hgrn_norm_g),
             norm_ffn_g=(norm_ffn_g, m_norm_ffn_g, v_norm_ffn_g),
             norm_final_g=(row(norm_final_g), row(m_norm_final_g), row(v_norm_final_g)),
             hgrn_lb_logits=(hgrn_lb_logits, m_hgrn_lb_logits, v_hgrn_lb_logits),
             attn_sinks=(attn_sinks, m_attn_sinks, v_attn_sinks), b_in=(b_in, m_b_in, v_b_in)))
    small["norm_final_g"] = [a.reshape(D) for a in small["norm_final_g"]]
    loss = loss_out[0, 0]

    names = ["w_in", "w_ffn_gate", "w_ffn_up", "w_ffn_down", "w_branch_attn", "w_branch_hgrn", "w_out"]
    w_full = dict(w_in=(w_in, m_w_in, v_w_in), w_ffn_gate=(w_ffn_gate, m_w_ffn_gate, v_w_ffn_gate),
                  w_ffn_up=(w_ffn_up, m_w_ffn_up, v_w_ffn_up), w_ffn_down=(w_ffn_down, m_w_ffn_down, v_w_ffn_down),
                  w_branch_attn=(w_branch_attn, m_w_branch_attn, v_w_branch_attn),
                  w_branch_hgrn=(w_branch_hgrn, m_w_branch_hgrn, v_w_branch_hgrn),
                  w_out=(w_out, m_w_out, v_w_out))
    big = {}
    for group, parts, tag in zip((names[0:1], names[1:4], names[4:7]), grad_parts, ("w_in", "ffn", "square")):
        flip = [name in names[0:3] for name in group]
        view = lambda a, f: a[0].T if f else a[0]
        cols = [[view(w_full[name][j], f) for name, f in zip(group, flip)] for j in range(3)]
        res = _adam(cols[0], parts, cols[1], cols[2], name="adam_" + tag)
        for name, f, r in zip(group, flip, res):
            big[name] = [a.T[None] if f else a[None] for a in r]

    order = ["norm_mix_g", "w_in", "b_in", "attn_sinks", "hgrn_lb_logits", "hgrn_norm_g", "w_branch_attn",
             "w_branch_hgrn", "w_out", "norm_ffn_g", "w_ffn_gate", "w_ffn_up", "w_ffn_down", "norm_final_g"]
    outs = [loss, grad_x[None]]
    for kind in range(4):
        for name in order:
            outs.append(big[name][kind] if name in big else small[name][kind])
    return tuple(outs)
```

```python
import math

import jax
import jax.numpy as jnp
from jax import lax
from jax.experimental import pallas as pl
from jax.experimental.pallas import tpu as pltpu

F32 = jnp.float32
BF = jnp.bfloat16
MESH = pl.DeviceIdType.MESH

D = 1024
HEAD = 64
N_PAIR = 8
BLK = 128
CH = 64
HG_SUB = 4
HG_SUB_BWD = 2
HG_HEADS = 8
HG_K = 128
FFN = 2816
IN_W = 7424
N_DEV = 8
N_CHIP = 4
EPS = 1e-6
NEG = -1e30
SCALE = 1.0 / math.sqrt(HEAD)
VMEM_LIMIT = 56 * 1024 * 1024
WT = 256

ADAM_LR, ADAM_B1, ADAM_B2, ADAM_EPS, ADAM_WD, ADAM_STEP = 0.001, 0.9, 0.999, 1e-08, 0.01, 10

GRP_OFF = (0, D // WT, (D + 256) // WT, (5 * D + 256) // WT)
GRP_N = (D // WT, 256 // WT, 4 * D // WT, 2 * D // WT)
SMALL_ROWS = 16


_NN = (((1,), (0,)), ((), ()))
_NT = (((1,), (1,)), ((), ()))
_TN = (((0,), (0,)), ((), ()))


def _pcall(body, **kw):
    return pl.pallas_call(body, **kw)


def _cp(sem=None, **kw):
    return pltpu.CompilerParams(dimension_semantics=sem, vmem_limit_bytes=VMEM_LIMIT, **kw)


def _sig(v):
    return 0.5 * jnp.tanh(0.5 * v) + 0.5


def _accum(ref, val, first):
    @pl.when(first)
    def _():
        ref[...] = val

    @pl.when(jnp.logical_not(first))
    def _():
        ref[...] += val


class _Comm:
    def __init__(self, ins, out_shapes, sem_shapes, phases):
        self.ins, self.out_shapes, self.sem_shapes, self.phases = list(ins), list(out_shapes), list(sem_shapes), phases


def _both(a, b):
    ni, no, ns = len(a.ins), len(a.out_shapes), len(a.sem_shapes)

    def of_a(fn):
        return lambda ins, outs, sems: fn(ins[:ni], outs[:no], sems[:ns])

    def of_b(fn):
        return lambda ins, outs, sems: fn(ins[ni:], outs[no:], sems[ns:])

    return _Comm(a.ins + b.ins, a.out_shapes + b.out_shapes, a.sem_shapes + b.sem_shapes,
                 [(f, of_a(fn)) for f, fn in a.phases] + [(f, of_b(fn)) for f, fn in b.phases])


def _host(body, comm, n_in, n_out, n_scr, nsteps, step_fn):
    if comm is None:
        return body
    ci, co = len(comm.ins), len(comm.out_shapes)

    def wrapped(*refs):
        p = 0
        ins, p = refs[p:p + n_in], p + n_in
        cins, p = refs[p:p + ci], p + ci
        outs, p = refs[p:p + n_out], p + n_out
        couts, p = refs[p:p + co], p + co
        scr, p = refs[p:p + n_scr], p + n_scr
        csems = refs[p:]
        step = step_fn()
        for frac, fn in comm.phases:
            if frac < 1.0:
                @pl.when(step == int(round(frac * (nsteps - 1))))
                def _(fn=fn):
                    fn(cins, couts, csems)
        body(*ins, *outs, *scr)
        for frac, fn in comm.phases:
            if frac >= 1.0:
                @pl.when(step == nsteps - 1)
                def _(fn=fn):
                    fn(cins, couts, csems)

    return wrapped


def _hosted_call(body, comm, args, *, name, grid, in_specs, out_specs, out_shape, scratch_shapes, sem,
                 nsteps, step_fn, aliases=None):
    n_in, n_out, n_scr = len(in_specs), len(out_specs), len(scratch_shapes)
    args = list(args)
    extra = {}
    if comm is not None:
        in_specs = list(in_specs) + [_hbm_spec()] * len(comm.ins)
        out_specs = list(out_specs) + [_hbm_spec()] * len(comm.out_shapes)
        out_shape = list(out_shape) + comm.out_shapes
        scratch_shapes = list(scratch_shapes) + comm.sem_shapes
        args += comm.ins
        extra = dict(has_side_effects=True)
    outs = _pcall(_host(body, comm, n_in, n_out, n_scr, nsteps, step_fn), name=name, grid=grid,
                  in_specs=in_specs, out_specs=out_specs, out_shape=out_shape, scratch_shapes=scratch_shapes,
                  input_output_aliases=aliases or {}, compiler_params=_cp(sem, **extra))(*args)
    return list(outs[:n_out]), list(outs[n_out:])


def _hbm_spec():
    return pl.BlockSpec(memory_space=pl.ANY)


def _wgrad(a_list, b, *, name):
    (t, m), n, gm = a_list[0].shape, b.shape[1], a_list[0].shape[1] // WT
    n_a = len(a_list)
    tile = lambda k: (lambda s: jnp.clip(s - k * gm, 0, gm - 1))

    def body(*refs):
        a_refs, b_ref, o_refs = refs[:n_a], refs[n_a], refs[n_a + 1:]
        s = pl.program_id(0)
        for k in range(n_a):
            @pl.when(jnp.logical_and(s >= k * gm, s < (k + 1) * gm))
            def _(k=k):
                o_refs[k][...] = lax.dot_general(a_refs[k][...], b_ref[...], _TN,
                                                 preferred_element_type=F32).astype(BF)

    return _pcall(body, name=name, grid=(n_a * gm,),
                  in_specs=[pl.BlockSpec((t, WT), lambda s, k=k: (0, tile(k)(s))) for k in range(n_a)]
                  + [pl.BlockSpec((t, n), lambda s: (0, 0))],
                  out_specs=[pl.BlockSpec((WT, n), lambda s, k=k: (tile(k)(s), 0)) for k in range(n_a)],
                  out_shape=[jax.ShapeDtypeStruct((m, n), BF)] * n_a,
                  compiler_params=_cp(("arbitrary",)))(*a_list, b)


def _fmm(lhs, rhs, extras, epilogue, outs, *, m, n, tm, tn, name, comm=None, vecs=(), consts=(), sums=(),
         cols_outer=False):
    tm, tn = min(tm, m), min(tn, n)
    assert m % tm == 0 and n % tn == 0 and (not sums or (tn == n and not cols_outer)), (name, m, n, tm, tn)
    in_specs, args = [], []
    for a in lhs:
        in_specs.append(pl.BlockSpec((tm, a.shape[1]), lambda i, j: (i, 0)))
        args.append(a)
    for li, b, tb in rhs:
        k = lhs[li].shape[1]
        in_specs.append(pl.BlockSpec((tn, k), lambda i, j: (j, 0)) if tb
                        else pl.BlockSpec((k, tn), lambda i, j: (0, j)))
        args.append(b)
    for arr, w, col in extras:
        in_specs.append(pl.BlockSpec((tm, w), lambda i, j, col=col: (i, col(j))))
        args.append(arr)
    for vec in vecs:
        in_specs.append(pl.BlockSpec((1, tn), lambda i, j: (0, j)))
        args.append(vec)
    for whole in consts:
        in_specs.append(pl.BlockSpec(whole.shape, lambda i, j: (0, 0)))
        args.append(whole)
    out_specs = [pl.BlockSpec((tm, w), lambda i, j, col=col: (i, col(j))) for _, _, w, col in outs]
    out_shape = [jax.ShapeDtypeStruct((m, total), dt) for dt, total, _, _ in outs]
    for w in sums:
        out_specs.append(pl.BlockSpec((1, w), lambda i, j: (0, 0)))
        out_shape.append(jax.ShapeDtypeStruct((1, w), F32))
    nl, nr, ne, no = len(lhs), len(rhs), len(extras) + len(vecs) + len(consts), len(outs)

    def body(*refs):
        prods = []
        for r, (li, _, tb) in enumerate(rhs):
            prods.append(lax.dot_general(refs[li][...], refs[nl + r][...], _NT if tb else _NN,
                                         preferred_element_type=F32))
        vals = epilogue(prods, [ref[...] for ref in refs[nl + nr:nl + nr + ne]])
        o_refs = refs[nl + nr + ne:]
        for o_ref, v in zip(o_refs[:no], vals[:no]):
            o_ref[...] = v.astype(o_ref.dtype)
        for s_ref, v in zip(o_refs[no:], vals[no:]):
            _accum(s_ref, v, pl.program_id(0) == 0)

    grid = (m // tm, n // tn)
    if cols_outer:
        flip = lambda spec: pl.BlockSpec(spec.block_shape, lambda j, i, f=spec.index_map: f(i, j))
        in_specs, out_specs, grid = [flip(s) for s in in_specs], [flip(s) for s in out_specs], grid[::-1]
    res, comm_res = _hosted_call(
        body, comm, args, name=name, grid=grid, in_specs=in_specs, out_specs=out_specs,
        out_shape=out_shape, scratch_shapes=[], sem=("arbitrary", "arbitrary"), nsteps=grid[0] * grid[1],
        step_fn=lambda: pl.program_id(0) * grid[1] + pl.program_id(1))
    return res if comm is None else (res, comm_res)


def _grp_of(i):
    return [jnp.logical_and(i >= GRP_OFF[g], i < GRP_OFF[g] + GRP_N[g]) for g in range(4)]


def _grp_idx(i, g):
    return jnp.clip(i - GRP_OFF[g], 0, GRP_N[g] - 1)


def _inproj_fwd(u, win_t, b_in, *, t, comm=None):
    tm = min(1024, t)
    n_row = t // tm
    n_chunks, h_first, g_first = 8, 2, 6
    sub = D // WT

    def w_block(l):
        return jnp.where(l == 0, GRP_OFF[0], jnp.where(l == 1, GRP_OFF[1], GRP_OFF[2] + sub * (l - h_first)))

    def body(u_ref, *rest):
        w_refs, b_refs, (q_ref, kv_ref, h3_ref, hf_ref, g_ref) = rest[:sub], rest[sub:2 * sub], rest[2 * sub:]
        l = pl.program_id(1)

        @pl.when(l == 1)
        def _():
            kv_ref[...] = (lax.dot_general(u_ref[...], w_refs[0][...], _NT, preferred_element_type=F32)
                           + b_refs[0][...]).astype(BF)

        is_hf = l == h_first + 1
        in_h3 = jnp.logical_and(jnp.logical_and(l >= h_first, l < g_first), jnp.logical_not(is_hf))
        for pred, o_ref in ((l == 0, q_ref), (in_h3, h3_ref), (is_hf, hf_ref), (l >= g_first, g_ref)):
            @pl.when(pred)
            def _(o_ref=o_ref):
                w = jnp.concatenate([w[...] for w in w_refs], axis=0)
                b = jnp.concatenate([b[...] for b in b_refs], axis=1)
                o_ref[...] = (lax.dot_general(u_ref[...], w, _NT, preferred_element_type=F32) + b).astype(o_ref.dtype)

    return _hosted_call(
        body, comm, [u] + [win_t] * sub + [b_in] * sub, name="inproj_fwd", grid=(n_row, n_chunks),
        in_specs=[pl.BlockSpec((tm, D), lambda i, l: (i, 0))]
        + [pl.BlockSpec((WT, D), lambda i, l, o=o: (w_block(l) + o, 0)) for o in range(sub)]
        + [pl.BlockSpec((1, WT), lambda i, l, o=o: (0, w_block(l) + o)) for o in range(sub)],
        out_specs=[pl.BlockSpec((tm, D), lambda i, l: (i, 0)),
                   pl.BlockSpec((tm, 256), lambda i, l: (i, 0)),
                   pl.BlockSpec((tm, D), lambda i, l: (i, jnp.clip(l - h_first - 1, 0, 2))),
                   pl.BlockSpec((tm, D), lambda i, l: (i, 0)),
                   pl.BlockSpec((tm, D), lambda i, l: (i, jnp.clip(l - g_first, 0, 1)))],
        out_shape=[jax.ShapeDtypeStruct((t, D), BF), jax.ShapeDtypeStruct((t, 256), BF),
                   jax.ShapeDtypeStruct((t, 3 * D), BF), jax.ShapeDtypeStruct((t, D), F32),
                   jax.ShapeDtypeStruct((t, 2 * D), BF)],
        scratch_shapes=[], sem=("arbitrary", "arbitrary"), nsteps=n_row * n_chunks,
        step_fn=lambda: pl.program_id(0) * n_chunks + pl.program_id(1))


def _inproj_bwd_x(dps, win_t, x, g, resid, *, t, part, prev=None, comm=None):
    n_row = 8 if t >= 4096 else 4
    tm = t // n_row
    first = 1
    per = first if part == 0 else n_row - first
    row = lambda i: part * first + i

    n_chunks = 4
    sub = 2 * D // WT

    def w_block(l):
        return jnp.where(l == 0, 0, GRP_OFF[2] + sub * (l - 1))

    def body(d0, d1, d2, d3, *rest):
        w_refs, (x_ref, g_ref, r_ref) = rest[:sub], rest[sub:sub + 3]
        dg_prev = rest[sub + 3] if prev is not None else None
        o_ref, dg_ref, acc_ref = rest[-3], rest[-2], rest[-1]
        i, l = pl.program_id(0), pl.program_id(1)

        @pl.when(l == 0)
        def _():
            wq = jnp.concatenate([w[...] for w in w_refs[:GRP_N[0]]], axis=0)
            acc_ref[...] = (jnp.dot(d0[...], wq, preferred_element_type=F32)
                            + jnp.dot(d1[...], w_refs[GRP_N[0]][...], preferred_element_type=F32))

        for pred, d_ref in ((jnp.logical_and(l >= 1, l < 3), d2), (l == 3, d3)):
            @pl.when(pred)
            def _(d_ref=d_ref):
                w = jnp.concatenate([w[...] for w in w_refs], axis=0)
                acc_ref[...] += jnp.dot(d_ref[...], w, preferred_element_type=F32)

        @pl.when(l == n_chunks - 1)
        def _():
            xv = x_ref[...]
            r = lax.rsqrt(jnp.mean(xv * xv, axis=-1, keepdims=True) + EPS)
            xh = xv * r
            du = acc_ref[...]
            dxh = du * g_ref[...]
            o_ref[...] = r_ref[...] + r * (dxh - xh * jnp.mean(dxh * xh, axis=-1, keepdims=True))
            dg = jnp.sum(du * xh, axis=0, keepdims=True)
            if dg_prev is not None:
                dg = dg + jnp.where(i == 0, 1.0, 0.0) * dg_prev[...]
            _accum(dg_ref, dg, i == 0)

    rows = lambda w: pl.BlockSpec((tm, w), lambda i, l: (row(i), 0))
    in_specs = ([rows(D), rows(256),
                 pl.BlockSpec((tm, 2 * D), lambda i, l: (row(i), jnp.clip(l - 1, 0, 1))), rows(2 * D)]
                + [pl.BlockSpec((WT, D), lambda i, l, o=o: (w_block(l) + o, 0)) for o in range(sub)]
                + [rows(D), pl.BlockSpec((1, D), lambda i, l: (0, 0)), rows(D)])
    args = list(dps) + [win_t] * sub + [x, g, resid]
    aliases = None
    if prev is not None:
        in_specs += [pl.BlockSpec((1, D), lambda i, l: (0, 0)), _hbm_spec()]
        args += [prev[1], prev[0]]
        aliases = {len(args) - 1: 0}
    return _hosted_call(
        body, comm, args, name="inproj_bwd_x%d" % part, grid=(per, n_chunks), in_specs=in_specs,
        out_specs=[rows(D), pl.BlockSpec((1, D), lambda i, l: (0, 0))],
        out_shape=[jax.ShapeDtypeStruct((t, D), F32), jax.ShapeDtypeStruct((1, D), F32)],
        scratch_shapes=[pltpu.VMEM((tm, D), F32)], sem=("arbitrary", "arbitrary"), nsteps=per * n_chunks,
        step_fn=lambda: pl.program_id(0) * n_chunks + pl.program_id(1), aliases=aliases)


def _inproj_bwd_w(dps, u, *, t):
    n_tiles = IN_W // WT
    dims = (((0,), (0,)), ((), ()))

    def body(d0, d1, d2, d3, u_ref, o_ref, db_ref):
        i = pl.program_id(0)
        uv = u_ref[...]
        for g, (pred, d_ref) in enumerate(zip(_grp_of(i), (d0, d1, d2, d3))):
            @pl.when(pred)
            def _(d_ref=d_ref):
                dv = d_ref[...]
                o_ref[...] = lax.dot_general(dv, uv, dims, preferred_element_type=F32).astype(BF)
                db_ref[...] = jnp.sum(dv.astype(F32), axis=0, keepdims=True)

    return _pcall(body, name="inproj_bwd_w", grid=(n_tiles,),
                  in_specs=[pl.BlockSpec((t, WT), lambda i, g=g: (0, _grp_idx(i, g))) for g in range(4)]
                  + [pl.BlockSpec((t, D), lambda i: (0, 0))],
                  out_specs=[pl.BlockSpec((WT, D), lambda i: (i, 0)),
                             pl.BlockSpec((1, WT), lambda i: (0, i))],
                  out_shape=[jax.ShapeDtypeStruct((IN_W, D), BF), jax.ShapeDtypeStruct((1, IN_W), F32)],
                  compiler_params=_cp(("arbitrary",)))(*dps, u)


def _row_spec(tm, width, col=0):
    return pl.BlockSpec((tm, width), lambda i: (i, col))


def _vec_spec(width):
    return pl.BlockSpec((1, width), lambda i: (0, 0))


def _rms_fwd(x, g, *, tm, name, comm=None):
    t = x.shape[0]
    tm = min(tm, t)

    def body(x_ref, g_ref, u_ref):
        xv = x_ref[...]
        r = lax.rsqrt(jnp.mean(xv * xv, axis=-1, keepdims=True) + EPS)
        u_ref[...] = (xv * r * g_ref[...]).astype(BF)

    (u,), comm_res = _hosted_call(
        body, comm, (x, g), name=name, grid=(t // tm,), in_specs=[_row_spec(tm, D), _vec_spec(D)],
        out_specs=[_row_spec(tm, D)], out_shape=[jax.ShapeDtypeStruct((t, D), BF)], scratch_shapes=[],
        sem=("arbitrary",), nsteps=t // tm, step_fn=lambda: pl.program_id(0))
    return u if comm is None else (u, comm_res)


def _rms_bwd(du, x, g, resid, *, tm, name):
    t = x.shape[0]
    tm = min(tm, t)

    def body(du_ref, x_ref, g_ref, r_ref, dx_ref, dxb_ref, dg_ref):
        xv = x_ref[...]
        r = lax.rsqrt(jnp.mean(xv * xv, axis=-1, keepdims=True) + EPS)
        xh = xv * r
        duv = du_ref[...]
        dxh = duv * g_ref[...]
        dx = r_ref[...] + r * (dxh - xh * jnp.mean(dxh * xh, axis=-1, keepdims=True))
        dx_ref[...] = dx
        dxb_ref[...] = dx.astype(BF)
        _accum(dg_ref, jnp.sum(duv * xh, axis=0, keepdims=True), pl.program_id(0) == 0)

    return _pcall(body, name=name, grid=(t // tm,),
                  in_specs=[_row_spec(tm, D), _row_spec(tm, D), _vec_spec(D), _row_spec(tm, D)],
                  out_specs=[_row_spec(tm, D), _row_spec(tm, D), _vec_spec(D)],
                  out_shape=[jax.ShapeDtypeStruct((t, D), F32), jax.ShapeDtypeStruct((t, D), BF),
                             jax.ShapeDtypeStruct((1, D), F32)],
                  compiler_params=_cp(("arbitrary",)))(du, x, g, resid)


def _attn_kv_tiles(kprev, kcur):
    kv = jnp.concatenate([kprev, kcur], axis=0).astype(F32)
    lo = lax.broadcasted_iota(jnp.int32, (2 * BLK, 128), 1) < HEAD
    tiles = []
    for part in (kv[:, 0:128], kv[:, 128:256]):
        rolled = pltpu.roll(part, HEAD, 1)
        z = jnp.zeros_like(part)
        tiles.append(((jnp.where(lo, part, z).astype(BF), jnp.where(lo, z, rolled).astype(BF)),
                      (jnp.where(lo, rolled, z).astype(BF), jnp.where(lo, z, part).astype(BF))))
    k_t, v_t = tiles
    return [(jnp.concatenate(k_t[h], axis=0), jnp.concatenate(v_t[h], axis=0)) for h in range(2)]


def _attn_mask(i):
    qi = lax.broadcasted_iota(jnp.int32, (BLK, 2 * BLK), 0)
    kj = lax.broadcasted_iota(jnp.int32, (BLK, 2 * BLK), 1)
    first_key = jnp.where(i == 0, BLK, 0)
    in_prev = jnp.logical_and(jnp.logical_and(kj < BLK, kj > qi), kj >= first_key)
    in_cur = jnp.logical_and(kj >= BLK, kj - BLK <= qi)
    return jnp.logical_or(in_prev, in_cur)


def _attn_probs(s, sink, valid):
    s = jnp.where(valid, s * SCALE, NEG)
    mx = jnp.maximum(jnp.max(s, axis=-1, keepdims=True), sink)
    e = jnp.exp(s - mx)
    es = jnp.exp(sink - mx)
    inv = 1.0 / (jnp.sum(e, axis=-1, keepdims=True) + es)
    return e * inv, es * inv


_KEYS = 2 * BLK


def _pair(ref, j):
    return ref[:, j * 128:(j + 1) * 128]


def _attn_fwd(q, kv, sinks, *, t, comm=None):
    nb = t // BLK
    sub = 4 if nb % 4 == 0 else 2

    def body(sink_ref, q_ref, kp_ref, kc_ref, o_ref):
        i = pl.program_id(0)
        for c in range(sub):
            rows = slice(c * BLK, (c + 1) * BLK)
            valid = _attn_mask(sub * i + c)
            tiles = _attn_kv_tiles(kp_ref[...] if c == 0 else kc_ref[(c - 1) * BLK:c * BLK, :], kc_ref[rows, :])
            s = [lax.dot_general(q_ref[rows, j * 128:(j + 1) * 128], tiles[j // 4][0], _NT,
                                 preferred_element_type=F32) for j in range(N_PAIR)]
            p = []
            for j in range(N_PAIR):
                pe, _ = _attn_probs(s[j][:, 0:_KEYS], sink_ref[0, 2 * j], valid)
                po, _ = _attn_probs(s[j][:, _KEYS:2 * _KEYS], sink_ref[0, 2 * j + 1], valid)
                p.append(jnp.concatenate([pe.astype(BF), po.astype(BF)], axis=1))
            for j in range(N_PAIR):
                o_ref[rows, j * 128:(j + 1) * 128] = jnp.dot(p[j], tiles[j // 4][1],
                                                             preferred_element_type=F32).astype(BF)

    return _hosted_call(
        body, comm, (sinks, q, kv, kv), name="attn_fwd", grid=(nb // sub,),
        in_specs=[pl.BlockSpec(memory_space=pltpu.SMEM),
                  pl.BlockSpec((sub * BLK, D), lambda i: (i, 0)),
                  pl.BlockSpec((BLK, 256), lambda i: (jnp.maximum(sub * i - 1, 0), 0)),
                  pl.BlockSpec((sub * BLK, 256), lambda i: (i, 0))],
        out_specs=[pl.BlockSpec((sub * BLK, D), lambda i: (i, 0))],
        out_shape=[jax.ShapeDtypeStruct((t, D), BF)],
        scratch_shapes=[], sem=("arbitrary",), nsteps=nb // sub, step_fn=lambda: pl.program_id(0))


def _attn_bwd(q, kv, sinks, do, *, t, comm=None):
    nb = t // BLK
    last = nb - 1

    def body(sink_ref, q_ref, kp_ref, kc_ref, do_ref, dq_ref, dkv_ref, ds_ref, carry_ref):
        i = pl.program_id(0)

        @pl.when(i == 0)
        def _():
            ds_ref[...] = jnp.zeros_like(ds_ref)
            carry_ref[...] = jnp.zeros_like(carry_ref)

        @pl.when(i < nb)
        def _():
            valid = _attn_mask(i)
            tiles = _attn_kv_tiles(kp_ref[...], kc_ref[...])
            lane1 = lax.broadcasted_iota(jnp.int32, (1, 128), 1)
            dsink = jnp.zeros((1, 128), F32)
            s = [lax.dot_general(_pair(q_ref, j), tiles[j // 4][0], _NT, preferred_element_type=F32)
                 for j in range(N_PAIR)]
            dp = [lax.dot_general(_pair(do_ref, j), tiles[j // 4][1], _NT, preferred_element_type=F32)
                  for j in range(N_PAIR)]
            p_all, ds_all = [], []
            for j in range(N_PAIR):
                halves = []
                for par in range(2):
                    cols = slice(par * _KEYS, (par + 1) * _KEYS)
                    p, ps = _attn_probs(s[j][:, cols], sink_ref[0, 2 * j + par], valid)
                    dpj = dp[j][:, cols]
                    dd = jnp.sum(p * dpj, axis=-1, keepdims=True)
                    dsink = dsink + jnp.where(lane1 == 2 * j + par,
                                              -jnp.sum(ps * dd, axis=0, keepdims=True), 0.0)
                    halves.append((p.astype(BF), (p * (dpj - dd)).astype(BF)))
                p_all.append(jnp.concatenate([halves[0][0], halves[1][0]], axis=1))
                ds_all.append(jnp.concatenate([halves[0][1], halves[1][1]], axis=1))
            for j in range(N_PAIR):
                dq_ref[:, j * 128:(j + 1) * 128] = (
                    jnp.dot(ds_all[j], tiles[j // 4][0], preferred_element_type=F32) * SCALE).astype(BF)
            ds_ref[...] += dsink
            gk, gv = [], []
            for h in range(2):
                grp = range(4 * h, 4 * h + 4)
                q_rows = jnp.concatenate([_pair(q_ref, j) for j in grp], axis=0)
                do_rows = jnp.concatenate([_pair(do_ref, j) for j in grp], axis=0)
                g_k = lax.dot_general(jnp.concatenate([ds_all[j] for j in grp], axis=0), q_rows, _TN,
                                      preferred_element_type=F32)
                g_v = lax.dot_general(jnp.concatenate([p_all[j] for j in grp], axis=0), do_rows, _TN,
                                      preferred_element_type=F32)
                gk.append((g_k[0:_KEYS], g_k[_KEYS:2 * _KEYS]))
                gv.append((g_v[0:_KEYS], g_v[_KEYS:2 * _KEYS]))
            lo = lax.broadcasted_iota(jnp.int32, (2 * BLK, 128), 1) < HEAD
            zero = jnp.zeros((2 * BLK, 128), F32)

            def unpad(g):
                return (jnp.where(lo, g[0][0] + pltpu.roll(g[0][1], HEAD, 1), zero)
                        + jnp.where(lo, zero, pltpu.roll(g[1][0], HEAD, 1) + g[1][1]))

            dk = unpad(gk) * SCALE
            dv = unpad(gv)
            dkv_ref[:, 0:128] = (carry_ref[:, 0:128] + dk[0:BLK]).astype(BF)
            dkv_ref[:, 128:256] = (carry_ref[:, 128:256] + dv[0:BLK]).astype(BF)
            carry_ref[:, 0:128] = dk[BLK:2 * BLK]
            carry_ref[:, 128:256] = dv[BLK:2 * BLK]

        @pl.when(i == nb)
        def _():
            dkv_ref[...] = carry_ref[...].astype(BF)

    return _hosted_call(
        body, comm, (sinks, q, kv, kv, do), name="attn_bwd", grid=(nb + 1,),
        in_specs=[pl.BlockSpec(memory_space=pltpu.SMEM),
                  pl.BlockSpec((BLK, D), lambda i: (jnp.minimum(i, last), 0)),
                  pl.BlockSpec((BLK, 256), lambda i: (jnp.clip(i - 1, 0, last), 0)),
                  pl.BlockSpec((BLK, 256), lambda i: (jnp.minimum(i, last), 0)),
                  pl.BlockSpec((BLK, D), lambda i: (jnp.minimum(i, last), 0))],
        out_specs=[pl.BlockSpec((BLK, D), lambda i: (jnp.minimum(i, last), 0)),
                   pl.BlockSpec((BLK, 256), lambda i: (jnp.maximum(i - 1, 0), 0)),
                   pl.BlockSpec((1, 128), lambda i: (0, 0))],
        out_shape=[jax.ShapeDtypeStruct((t, D), BF), jax.ShapeDtypeStruct((t, 256), BF),
                   jax.ShapeDtypeStruct((1, 128), F32)],
        scratch_shapes=[pltpu.VMEM((BLK, 256), F32)], sem=("arbitrary",), nsteps=nb + 1,
        step_fn=lambda: pl.program_id(0))


def _split3(v):
    h = v.astype(BF)
    r = v - h.astype(F32)
    m = r.astype(BF)
    lo = (r - m.astype(F32)).astype(BF)
    return jnp.concatenate([h, m, lo], axis=1)


def _apply01(mat, v):
    n = v.shape[1]
    r = jnp.dot(mat, _split3(v), preferred_element_type=F32)
    return r[:, 0:n] + r[:, n:2 * n] + r[:, 2 * n:3 * n]


def _hgrn_gates(hq, hf, lb):
    sq = _sig(hq)
    sg = _sig(hf)
    f = lb + (1.0 - lb) * sg
    return hq * sq, (1.0 - lb) * (1.0 - sg), jnp.log(f), sq, sg, f


def _tri(upper):
    r = lax.broadcasted_iota(jnp.int32, (CH, CH), 0)
    c = lax.broadcasted_iota(jnp.int32, (CH, CH), 1)
    return (c >= r) if upper else (c <= r)


def _lb_from_logits(lg_ref):
    return 1.0 / (1.0 + jnp.exp(lg_ref[1:2, :] - lg_ref[0:1, :]))


def _hgrn_fwd(h3, hf, logits, norm_g, *, t, comm=None):
    nc = t // CH
    nt_dims = (((1,), (1,)), ((), ()))
    tn_dims = (((0,), (0,)), ((), ()))

    def body(h_ref, hf_ref, lg_ref, ng_ref, y_ref, o_ref, st_ref, s_scr, b_scr, qa_s, ka_s, qb_s, kb_s, v_s):
        @pl.when(pl.program_id(0) == 0)
        def _():
            s_scr[...] = jnp.zeros_like(s_scr)

        heads = [slice(h * HG_K, (h + 1) * HG_K) for h in range(HG_HEADS)]
        causal = _tri(False)
        lb = _lb_from_logits(lg_ref)
        for c in range(HG_SUB):
            rows = slice(c * CH, (c + 1) * CH)
            q, k, g, _, _, _ = _hgrn_gates(h_ref[rows, 0:D].astype(F32), hf_ref[rows, :], lb)
            b_scr[...] = _apply01(jnp.where(causal, 1.0, 0.0).astype(BF), g)
            b = b_scr[...]
            b_mid = b_scr[CH // 2 - 1:CH // 2, :]
            b_last = b_scr[CH - 1:CH, :]
            qa_s[...] = (q * jnp.exp(b - b_mid)).astype(BF)
            ka_s[...] = (k * jnp.exp(b_mid - b)).astype(BF)
            qb_s[...] = (q * jnp.exp(b)).astype(BF)
            kb_s[...] = (k * jnp.exp(b_last - b)).astype(BF)
            v_s[...] = h_ref[rows, D:2 * D]
            dec = jnp.exp(b_last)
            st_ref[c] = s_scr[...].astype(BF)
            a = [jnp.where(causal, lax.dot_general(qa_s[:, sl], ka_s[:, sl], nt_dims, preferred_element_type=F32),
                           0.0).astype(BF) for sl in heads]
            for h, sl in enumerate(heads):
                o_ref[rows, sl] = (jnp.dot(a[h], v_s[:, sl], preferred_element_type=F32)
                                   + lax.dot_general(qb_s[:, sl], s_scr[h].astype(BF), nt_dims,
                                                     preferred_element_type=F32))
            for h, sl in enumerate(heads):
                s_scr[h] = dec[:, sl] * s_scr[h] + lax.dot_general(v_s[:, sl], kb_s[:, sl], tn_dims,
                                                                   preferred_element_type=F32)
            for h, sl in enumerate(heads):
                o = o_ref[rows, sl]
                on = o * lax.rsqrt(jnp.mean(o * o, axis=-1, keepdims=True) + EPS)
                gate = _sig(h_ref[rows, 2 * D + h * HG_K:2 * D + (h + 1) * HG_K].astype(F32))
                y_ref[rows, sl] = (on * ng_ref[:, sl] * gate).astype(BF)

    half = lambda: pltpu.VMEM((CH, D), BF)
    blk = HG_SUB * CH
    return _hosted_call(
        body, comm, (h3, hf, logits, norm_g), name="hgrn_fwd", grid=(nc // HG_SUB,),
        in_specs=[pl.BlockSpec((blk, 3 * D), lambda n: (n, 0)),
                  pl.BlockSpec((blk, D), lambda n: (n, 0)),
                  pl.BlockSpec((2, D), lambda n: (0, 0)),
                  pl.BlockSpec((1, D), lambda n: (0, 0))],
        out_specs=[pl.BlockSpec((blk, D), lambda n: (n, 0)),
                   pl.BlockSpec((blk, D), lambda n: (n, 0)),
                   pl.BlockSpec((HG_SUB, HG_HEADS, HG_K, HG_K), lambda n: (n, 0, 0, 0))],
        out_shape=[jax.ShapeDtypeStruct((t, D), BF), jax.ShapeDtypeStruct((t, D), F32),
                   jax.ShapeDtypeStruct((nc, HG_HEADS, HG_K, HG_K), BF)],
        scratch_shapes=[pltpu.VMEM((HG_HEADS, HG_K, HG_K), F32), pltpu.VMEM((CH, D), F32),
                        half(), half(), half(), half(), half()],
        sem=("arbitrary",), nsteps=nc // HG_SUB, step_fn=lambda: pl.program_id(0))


def _hgrn_bwd(h3, hf, logits, norm_g, o_pre, states, dy, *, t, comm=None):
    nc = t // CH
    nt_dims = (((1,), (1,)), ((), ()))
    tn_dims = (((0,), (0,)), ((), ()))

    def body(h_ref, hf_ref, lg_ref, ng_ref, o_ref, st_ref, dy_ref, dh_ref, dlg_ref, dng_ref, ds_scr, dlb_scr,
             b_scr, tail_s, e_qa, e_ka, e_qb, e_kb, q_s, k_s, dqa_s, dka_s, dqb_s, dkb_s,
             qa_s, ka_s, qb_s, kb_s, v_s, do_s):
        n = pl.program_id(0)

        @pl.when(n == 0)
        def _():
            ds_scr[...] = jnp.zeros_like(ds_scr)
            dlb_scr[...] = jnp.zeros_like(dlb_scr)
            dng_ref[...] = jnp.zeros_like(dng_ref)

        heads = [slice(h * HG_K, (h + 1) * HG_K) for h in range(HG_HEADS)]
        lb = _lb_from_logits(lg_ref)
        causal = _tri(False)

        def chunk(c):
            rows = slice(c * CH, (c + 1) * CH)
            hq = h_ref[rows, 0:D].astype(F32)
            q, k, g, sq, sg, f = _hgrn_gates(hq, hf_ref[rows, :], lb)
            b_scr[...] = _apply01(jnp.where(causal, 1.0, 0.0).astype(BF), g)
            b = b_scr[...]
            b_mid = b_scr[CH // 2 - 1:CH // 2, :]
            b_last = b_scr[CH - 1:CH, :]
            q_s[...] = q
            k_s[...] = k
            for e_ref, s_ref, base, expo in ((e_qa, qa_s, q, b - b_mid), (e_ka, ka_s, k, b_mid - b),
                                             (e_qb, qb_s, q, b), (e_kb, kb_s, k, b_last - b)):
                e = jnp.exp(expo)
                e_ref[...] = e
                s_ref[...] = (base * e).astype(BF)
            v_s[...] = h_ref[rows, D:2 * D]
            dec = jnp.exp(b_last)
            for h, sl in enumerate(heads):
                gcol = slice(3 * D + h * HG_K, 3 * D + (h + 1) * HG_K)
                ngh = ng_ref[:, sl]
                sgate = _sig(h_ref[rows, 2 * D + h * HG_K:2 * D + (h + 1) * HG_K].astype(F32))
                o = o_ref[rows, sl]
                r = lax.rsqrt(jnp.mean(o * o, axis=-1, keepdims=True) + EPS)
                on = o * r
                dyh = dy_ref[rows, sl]
                dh_ref[rows, gcol] = (dyh * on * ngh * sgate * (1.0 - sgate)).astype(BF)
                dng_ref[:, sl] += jnp.sum(dyh * on * sgate, axis=0, keepdims=True)
                don = dyh * ngh * sgate
                do_s[:, sl] = (r * (don - on * jnp.mean(don * on, axis=-1, keepdims=True))).astype(BF)
            a = [jnp.where(causal, lax.dot_general(qa_s[:, sl], ka_s[:, sl], nt_dims, preferred_element_type=F32),
                           0.0).astype(BF) for sl in heads]
            da = [jnp.where(causal, lax.dot_general(do_s[:, sl], v_s[:, sl], nt_dims, preferred_element_type=F32),
                            0.0).astype(BF) for sl in heads]
            for h, sl in enumerate(heads):
                dh_ref[rows, 2 * D + h * HG_K:2 * D + (h + 1) * HG_K] = (
                    lax.dot_general(a[h], do_s[:, sl], tn_dims, preferred_element_type=F32)
                    + lax.dot_general(kb_s[:, sl], ds_scr[h].astype(BF), nt_dims, preferred_element_type=F32)
                ).astype(BF)
            for h, sl in enumerate(heads):
                dqa_s[:, sl] = jnp.dot(da[h], ka_s[:, sl], preferred_element_type=F32)
            for h, sl in enumerate(heads):
                dka_s[:, sl] = lax.dot_general(da[h], qa_s[:, sl], tn_dims, preferred_element_type=F32)
            for h, sl in enumerate(heads):
                dqb_s[:, sl] = jnp.dot(do_s[:, sl], st_ref[c, h], preferred_element_type=F32)
            for h, sl in enumerate(heads):
                dkb_s[:, sl] = jnp.dot(v_s[:, sl], ds_scr[h].astype(BF), preferred_element_type=F32)
            for h, sl in enumerate(heads):
                tail_s[:, sl] = jnp.sum(dec[:, sl] * st_ref[c, h].astype(F32) * ds_scr[h], axis=0, keepdims=True)
            for h, sl in enumerate(heads):
                ds_scr[h] = (lax.dot_general(do_s[:, sl], qb_s[:, sl], tn_dims, preferred_element_type=F32)
                             + dec[:, sl] * ds_scr[h])
            qv, kv = q_s[...], k_s[...]
            dqa, dka, dqb, dkb = dqa_s[...], dka_s[...], dqb_s[...], dkb_s[...]
            eqa, eka, eqb, ekb = e_qa[...], e_ka[...], e_qb[...], e_kb[...]
            dkb_kb = dkb * (kv * ekb)
            db_last = jnp.sum(dkb_kb, axis=0, keepdims=True) + tail_s[...]
            last_row = lax.broadcasted_iota(jnp.int32, (CH, D), 0) == CH - 1
            db = (dqa * (qv * eqa) - dka * (kv * eka) + dqb * (qv * eqb) - dkb_kb
                  + jnp.where(last_row, db_last, 0.0))
            dg = _apply01(jnp.where(_tri(True), 1.0, 0.0).astype(BF), db)
            dq = dqa * eqa + dqb * eqb
            dk = dka * eka + dkb * ekb
            dh_ref[rows, 0:D] = (dq * sq * (1.0 + hq * (1.0 - sq))).astype(BF)
            dfk = dg / f - dk
            dh_ref[rows, D:2 * D] = ((1.0 - lb) * dfk * sg * (1.0 - sg)).astype(BF)
            dlb_scr[...] += jnp.sum((1.0 - sg) * dfk, axis=0, keepdims=True)

        for c in reversed(range(HG_SUB_BWD)):
            chunk(c)

        @pl.when(n == nc // HG_SUB_BWD - 1)
        def _():
            dl0 = dlb_scr[...] * lb * (1.0 - lb)
            dlg_ref[0:1, :] = dl0
            dlg_ref[1:2, :] = -dl0

    steps = nc // HG_SUB_BWD
    blk = HG_SUB_BWD * CH
    rev = lambda n: (steps - 1 - n, 0)
    return _hosted_call(
        body, comm, (h3, hf, logits, norm_g, o_pre, states, dy), name="hgrn_bwd", grid=(steps,),
        in_specs=[pl.BlockSpec((blk, 3 * D), rev),
                  pl.BlockSpec((blk, D), rev),
                  pl.BlockSpec((2, D), lambda n: (0, 0)),
                  pl.BlockSpec((1, D), lambda n: (0, 0)),
                  pl.BlockSpec((blk, D), rev),
                  pl.BlockSpec((HG_SUB_BWD, HG_HEADS, HG_K, HG_K), lambda n: (steps - 1 - n, 0, 0, 0)),
                  pl.BlockSpec((blk, D), rev)],
        out_specs=[pl.BlockSpec((blk, 4 * D), rev),
                   pl.BlockSpec((2, D), lambda n: (0, 0)),
                   pl.BlockSpec((1, D), lambda n: (0, 0))],
        out_shape=[jax.ShapeDtypeStruct((t, 4 * D), BF), jax.ShapeDtypeStruct((2, D), F32),
                   jax.ShapeDtypeStruct((1, D), F32)],
        scratch_shapes=([pltpu.VMEM((HG_HEADS, HG_K, HG_K), F32), pltpu.VMEM((1, D), F32),
                         pltpu.VMEM((CH, D), F32), pltpu.VMEM((1, D), F32)]
                        + [pltpu.VMEM((CH, D), F32)] * 10 + [pltpu.VMEM((CH, D), BF)] * 6),
        sem=("arbitrary",), nsteps=steps, step_fn=lambda: pl.program_id(0))


def _place():
    x, y, c = lax.axis_index("x"), lax.axis_index("y"), lax.axis_index("c")
    return x, y, c, [(1 - x, y), (x, 1 - y), (1 - x, 1 - y)]


def _gather_comm(shards, mids):
    n, pieces = len(shards), len(mids)
    r = [s.shape[0] for s in shards]
    tile = 16
    cut = [[(rw // tile * p // pieces) * tile for p in range(pieces + 1)] for rw in r]
    size = [[cut[w][p + 1] - cut[w][p] for p in range(pieces)] for w in range(n)]

    def tools(ins, outs, sems):
        send_sems, recv_sems, local_sems = sems
        x, y, c, _ = _place()
        me, sib = (x, y, c), (x, y, 1 - c)
        near = [(x ^ c, y ^ (1 - c), c), (x ^ (1 - c), y ^ c, c), (1 - x, 1 - y, c)]

        def rows(w, p, dev):
            return outs[w].at[pl.ds((4 * dev[0] + 2 * dev[1] + dev[2]) * r[w] + cut[w][p], size[w][p]), :]

        def copy(kind, w, p, block, to, own=False):
            src = ins[w].at[pl.ds(cut[w][p], size[w][p]), :] if own else rows(w, p, block)
            return pltpu.make_async_remote_copy(
                src_ref=src, dst_ref=rows(w, p, block), send_sem=send_sems.at[p, kind],
                recv_sem=recv_sems.at[p, kind], device_id=to, device_id_type=MESH)

        def all_of(kind, p):
            whole = outs[0].at[pl.ds(0, sum(size[w][p] for w in range(n))), :]
            return pltpu.make_async_remote_copy(
                src_ref=whole, dst_ref=whole, send_sem=send_sems.at[p, kind], recv_sem=recv_sems.at[p, kind],
                device_id=me, device_id_type=MESH)

        mine = [pltpu.make_async_copy(ins[w], outs[w].at[pl.ds((4 * x + 2 * y + c) * r[w], r[w]), :],
                                      local_sems.at[w]) for w in range(n)]
        return near, me, sib, copy, all_of, mine

    def start(ins, outs, sems):
        near, me, sib, copy, _, mine = tools(ins, outs, sems)
        for cp in mine:
            cp.start()
        for p in range(pieces):
            for w in range(n):
                copy(0, w, p, me, sib, own=True).start()
                copy(1, w, p, me, near[0], own=True).start()
                copy(2, w, p, me, near[1], own=True).start()

    def pass_diagonal(p, near, sib, copy, all_of):
        all_of(3, p).wait_recv()
        for w in range(n):
            copy(6, w, p, near[2], sib).start()

    def pass_on(p):
        def phase(ins, outs, sems):
            near, _, sib, copy, all_of, _ = tools(ins, outs, sems)
            all_of(1, p).wait_recv()
            for w in range(n):
                copy(3, w, p, near[0], near[1]).start()
                copy(4, w, p, near[0], sib).start()
            all_of(2, p).wait_recv()
            for w in range(n):
                copy(5, w, p, near[1], sib).start()
            if p > 0:
                pass_diagonal(p - 1, near, sib, copy, all_of)
        return phase

    def finish(ins, outs, sems):
        near, _, sib, copy, all_of, mine = tools(ins, outs, sems)
        pass_diagonal(pieces - 1, near, sib, copy, all_of)
        for p in range(pieces):
            all_of(0, p).wait_recv()
            for kind in (4, 5, 6):
                all_of(kind, p).wait_recv()
            for kind in range(7):
                all_of(kind, p).wait_send()
        for cp in mine:
            cp.wait()

    return _Comm(shards, [jax.ShapeDtypeStruct((N_DEV * rw, D), BF) for rw in r],
                 [pltpu.SemaphoreType.DMA((pieces, 7)), pltpu.SemaphoreType.DMA((pieces, 7)),
                  pltpu.SemaphoreType.DMA((n,))],
                 [(0.0, start)] + [(f, pass_on(p)) for p, f in enumerate(mids)] + [(1.0, finish)])


def _pair_comm(grads):
    n = len(grads)
    r = [g.shape[0] // N_DEV for g in grads]

    def start(ins, outs, sems):
        send_sems, recv_sems = sems
        x, y, c, _ = _place()
        for w in range(n):
            for a in range(N_CHIP):
                pltpu.make_async_remote_copy(
                    src_ref=ins[w].at[pl.ds((2 * a + 1 - c) * r[w], r[w]), :], dst_ref=outs[w].at[a],
                    send_sem=send_sems.at[w], recv_sem=recv_sems.at[w],
                    device_id=(x, y, 1 - c), device_id_type=MESH).start()

    def finish(ins, outs, sems):
        send_sems, recv_sems = sems
        x, y, c, _ = _place()
        for w in range(n):
            pltpu.make_async_remote_copy(
                src_ref=outs[w], dst_ref=outs[w], send_sem=send_sems.at[w], recv_sem=recv_sems.at[w],
                device_id=(x, y, c), device_id_type=MESH).wait()

    return _Comm(grads, [jax.ShapeDtypeStruct((N_CHIP, rw, D), BF) for rw in r],
                 [pltpu.SemaphoreType.DMA((n,)), pltpu.SemaphoreType.DMA((n,))],
                 [(0.0, start), (1.0, finish)])


def _pair_add(grads, gots, core, *, name):
    n, r = len(grads), gots[0].shape[1]
    tr = r if r <= 128 else r // 2
    steps = r // tr
    tile = lambda k: (lambda s: jnp.clip(s - k * steps, 0, steps - 1))

    def body(c_ref, *refs):
        g_refs, got_refs, o_refs = refs[:n], refs[n:2 * n], refs[2 * n:]
        s = pl.program_id(0)
        for k in range(n):
            @pl.when(jnp.logical_and(s >= k * steps, s < (k + 1) * steps))
            def _(k=k):
                o_refs[k][...] = (g_refs[k][:, 0].astype(F32) + got_refs[k][...].astype(F32)).astype(BF)

    grid_spec = pltpu.PrefetchScalarGridSpec(
        num_scalar_prefetch=1, grid=(n * steps,),
        in_specs=[pl.BlockSpec((N_CHIP, 1, tr, D), lambda s, c_ref, k=k: (0, c_ref[0], tile(k)(s), 0))
                  for k in range(n)]
        + [pl.BlockSpec((N_CHIP, tr, D), lambda s, c_ref, k=k: (0, tile(k)(s), 0)) for k in range(n)],
        out_specs=[pl.BlockSpec((N_CHIP, tr, D), lambda s, c_ref, k=k: (0, tile(k)(s), 0)) for k in range(n)])
    return _pcall(body, name=name, grid_spec=grid_spec,
                  out_shape=[jax.ShapeDtypeStruct((N_CHIP, r, D), BF)] * n,
                  compiler_params=_cp(("arbitrary",)))(
                      core, *[g.reshape(N_CHIP, 2, r, D) for g in grads], *gots)


def _chip_comm(pair_sums):
    n = len(pair_sums)
    r = [p.shape[1] for p in pair_sums]
    off = [sum(r[:w]) for w in range(n)]

    def tools(ins, outs, sems):
        send_sems, recv_sems, local_sems = sems
        x, y, c, chips = _place()
        my_chip = 2 * x + y

        def slot(w):
            return outs[0].at[my_chip, pl.ds(off[w], r[w]), :]

        own = [pltpu.make_async_copy(ins[w].at[my_chip], slot(w), local_sems.at[w]) for w in range(n)]
        return x, y, c, chips, my_chip, slot, own, send_sems, recv_sems

    def start(ins, outs, sems):
        x, y, c, chips, my_chip, slot, own, send_sems, recv_sems = tools(ins, outs, sems)
        for cp in own:
            cp.start()
        for j, chip in enumerate(chips):
            for w in range(n):
                pltpu.make_async_remote_copy(
                    src_ref=ins[w].at[2 * chip[0] + chip[1]], dst_ref=slot(w), send_sem=send_sems.at[j],
                    recv_sem=recv_sems.at[j], device_id=(*chip, c), device_id_type=MESH).start()

    def finish(ins, outs, sems):
        x, y, c, chips, my_chip, slot, own, send_sems, recv_sems = tools(ins, outs, sems)
        whole = outs[0].at[my_chip]
        for j in range(3):
            pltpu.make_async_remote_copy(
                src_ref=whole, dst_ref=whole, send_sem=send_sems.at[j], recv_sem=recv_sems.at[j],
                device_id=(x, y, c), device_id_type=MESH).wait()
        for cp in own:
            cp.wait()

    return _Comm(pair_sums, [jax.ShapeDtypeStruct((N_CHIP, sum(r), D), BF)],
                 [pltpu.SemaphoreType.DMA((3,)), pltpu.SemaphoreType.DMA((3,)), pltpu.SemaphoreType.DMA((n,))],
                 [(0.0, start), (1.0, finish)])


def _adam_math(w, g, m, v):
    m = ADAM_B1 * m + (1.0 - ADAM_B1) * g
    v = ADAM_B2 * v + (1.0 - ADAM_B2) * (g * g)
    m_hat = m / (1.0 - ADAM_B1 ** ADAM_STEP)
    v_hat = v / (1.0 - ADAM_B2 ** ADAM_STEP)
    delta = -ADAM_LR * (m_hat / (jnp.sqrt(v_hat) + ADAM_EPS) + ADAM_WD * w)
    return delta, m, v


SMALL = (("norm_mix_g", (1, D), 0), ("hgrn_norm_g", (1, D), 1), ("norm_ffn_g", (1, D), 2),
         ("norm_final_g", (1, D), 3), ("hgrn_lb_logits", (2, D), 4), ("attn_sinks", (1, 16), 6),
         ("b_in", (1, IN_W), 8))
LOSS_ROW = 7


def _small_allreduce_adam(grads, loss_row, params):
    n = len(SMALL)

    def rows_of(ref, shape, row):
        r, w = shape
        if w <= D:
            return ref[row:row + r, 0:w]
        pieces = [ref[row + k:row + k + 1, :] for k in range(-(-w // D))]
        return jnp.concatenate(pieces, axis=1)[:, 0:w]

    def body(*refs):
        g_refs, loss_ref = refs[:n], refs[n]
        wmv = refs[n + 1:4 * n + 1]
        loss_out = refs[4 * n + 1]
        outs = refs[4 * n + 2:8 * n + 2]
        mine, total, gath, send_sems, recv_sems = refs[8 * n + 2:]
        x, y, c, _ = _place()
        me = 4 * x + 2 * y + c
        mine[...] = jnp.zeros_like(mine)
        for g_ref, (_, (r, w), row) in zip(g_refs, SMALL):
            for k in range(-(-w // D)):
                wk = min(D, w - k * D)
                mine[row + k:row + k + r, 0:wk] = g_ref[:, k * D:k * D + wk]
        mine[LOSS_ROW:LOSS_ROW + 1, 0:128] = loss_ref[...]
        gath[me] = mine[...]
        cps = []
        for d in range(1, N_DEV):
            peer = (x ^ (d >> 2), y ^ ((d >> 1) & 1), c ^ (d & 1))
            cps.append(pltpu.make_async_remote_copy(
                src_ref=mine, dst_ref=gath.at[me], send_sem=send_sems.at[d - 1],
                recv_sem=recv_sems.at[d - 1], device_id=peer, device_id_type=MESH))
        for cp in cps:
            cp.start()
        for cp in cps:
            cp.wait()
        g = gath[0]
        for k in range(1, N_DEV):
            g = g + gath[k]
        total[...] = g
        loss_out[...] = total[LOSS_ROW:LOSS_ROW + 1, 0:128]
        for i, (_, shape, row) in enumerate(SMALL):
            gi = rows_of(total, shape, row)
            w_ref, m_ref, v_ref = wmv[3 * i:3 * i + 3]
            o = outs[4 * i:4 * i + 4]
            o[0][...] = gi
            o[1][...], o[2][...], o[3][...] = _adam_math(w_ref[...], gi, m_ref[...], v_ref[...])

    vm = pl.BlockSpec(memory_space=pltpu.VMEM)
    ins = [grads[name] for name, _, _ in SMALL] + [loss_row]
    for name, _, _ in SMALL:
        ins += list(params[name])
    out_shape = [jax.ShapeDtypeStruct((1, 128), F32)]
    for _, shape, _ in SMALL:
        out_shape += [jax.ShapeDtypeStruct(shape, F32)] * 4
    res = _pcall(body, name="small_allreduce_adam", in_specs=[vm] * len(ins), out_specs=[vm] * len(out_shape),
                 out_shape=out_shape,
                 scratch_shapes=[pltpu.VMEM((SMALL_ROWS, D), F32), pltpu.VMEM((SMALL_ROWS, D), F32),
                                 pltpu.VMEM((N_DEV, SMALL_ROWS, D), F32),
                                 pltpu.SemaphoreType.DMA((N_DEV - 1,)), pltpu.SemaphoreType.DMA((N_DEV - 1,))],
                 compiler_params=pltpu.CompilerParams(has_side_effects=True))(*ins)
    return res[0], {name: res[1 + 4 * i:5 + 4 * i] for i, (name, _, _) in enumerate(SMALL)}


def _adam(ws, parts, ms, vs, *, name):
    n, rows = len(ws), ws[0].shape[0]
    tr = rows if rows <= 128 else rows // 2
    steps = rows // tr
    tile = lambda k: (lambda s: jnp.clip(s - k * steps, 0, steps - 1))

    def body(*refs):
        w_refs, m_refs, v_refs, p_ref = refs[:n], refs[n:2 * n], refs[2 * n:3 * n], refs[3 * n]
        o_refs = refs[3 * n + 1:]
        s = pl.program_id(0)
        for k in range(n):
            @pl.when(jnp.logical_and(s >= k * steps, s < (k + 1) * steps))
            def _(k=k):
                g = p_ref[0].astype(F32)
                for a in range(1, N_CHIP):
                    g = g + p_ref[a].astype(F32)
                o = o_refs[4 * k:4 * k + 4]
                o[0][...] = g
                o[1][...], o[2][...], o[3][...] = _adam_math(w_refs[k][...], g, m_refs[k][...], v_refs[k][...])

    spec = lambda k: pl.BlockSpec((tr, D), lambda s, k=k: (tile(k)(s), 0))
    res = _pcall(body, name=name, grid=(n * steps,),
                 in_specs=[spec(k) for k in range(n)] * 3 + [pl.BlockSpec((N_CHIP, tr, D), lambda s: (0, s, 0))],
                 out_specs=[spec(k) for k in range(n) for _ in range(4)],
                 out_shape=[jax.ShapeDtypeStruct((rows, D), F32)] * (4 * n),
                 compiler_params=_cp(("arbitrary",)))(*ws, *ms, *vs, parts)
    return [res[4 * k:4 * k + 4] for k in range(n)]


def _step(x, tgt, shards, norm_mix_g, b_in, sinks, logits, hgrn_norm_g, norm_ffn_g, norm_final_g):
    t = x.shape[0]
    core = lax.axis_index("c").astype(jnp.int32).reshape(1)

    u1, (win_t,) = _rms_fwd(x, norm_mix_g, tm=512, name="rms_mix", comm=_gather_comm(shards[0:1], (0.2, 0.4, 0.6, 0.8)))
    (q, kv, h3, hf, gates), (wg_t, wba, wbh, wout) = _inproj_fwd(
        u1, win_t, b_in, t=t, comm=_gather_comm([shards[1]] + shards[4:7], (0.3, 0.5, 0.7, 0.9)))
    (y_attn,), _ = _attn_fwd(q, kv, sinks, t=t)
    (y_hgrn, o_pre, states), (wu_t, wd) = _hgrn_fwd(h3, hf, logits, hgrn_norm_g, t=t,
                                                    comm=_gather_comm(shards[2:4], (0.3, 0.5, 0.7, 0.9)))
    col = lambda j: j
    first, second = (lambda j: 0), (lambda j: 1)
    gate_tiles = [(gates, D, first), (gates, D, second)]

    def merge(prods, ex):
        (ya_, yb_), (ga, gb) = prods, ex
        sa, sb = _sig(ga.astype(F32)), _sig(gb.astype(F32))
        return sa, sb, ya_ * sa * (1.0 - sa), yb_ * sb * (1.0 - sb), sa * ya_ + sb * yb_

    sig_a, sig_b, dgate_a, dgate_b, merged = _fmm(
        [y_attn, y_hgrn], [(0, wba, False), (1, wbh, False)], gate_tiles, merge,
        [(BF, D, D, first)] * 5, m=t, n=D, tm=512, tn=D, name="branch_merge")
    def resid_norm(prods, ex):
        (p,), (xv, gv) = prods, ex
        hv = xv + p
        return hv, hv * lax.rsqrt(jnp.mean(hv * hv, axis=-1, keepdims=True) + EPS) * gv

    h1, u2 = _fmm([merged], [(0, wout, False)], [(x, D, first)], resid_norm, [(F32, D, D, first), (BF, D, D, first)],
                  m=t, n=D, tm=1024, tn=D, name="out_proj", vecs=[norm_ffn_g])

    def swiglu(prods, ex):
        g_, u_ = prods
        s = _sig(g_)
        silu = g_ * s
        return u_ * s * (1.0 + g_ * (1.0 - s)), silu, silu * u_

    dz_dgate, dz_dup, z = _fmm([u2], [(0, wg_t, True), (0, wu_t, True)], [], swiglu,
                               [(BF, FFN, FFN // 2, col)] * 3, m=t, n=FFN, tm=1024, tn=FFN // 2,
                               name="ffn_gate_up", cols_outer=True)
    def loss_head(prods, ex):
        (p,), (hv, tv, gv) = prods, ex
        hv = hv + p
        r = lax.rsqrt(jnp.mean(hv * hv, axis=-1, keepdims=True) + EPS)
        xh = hv * r
        err = xh * gv - tv
        lp = jnp.sum(jnp.sum(err * err, axis=1, keepdims=True), axis=0, keepdims=True) * (0.5 / D)
        dy = err * (1.0 / D)
        dxh = dy * gv
        dh = r * (dxh - xh * jnp.mean(dxh * xh, axis=-1, keepdims=True))
        return dh, dh, jnp.sum(dy * xh, axis=0, keepdims=True), jnp.broadcast_to(lp, (1, 128))

    dh2, dh2_b, d_norm_final, loss_row = _fmm(
        [z], [(0, wd, False)], [(h1, D, first), (tgt, D, first)], loss_head, [(F32, D, D, first), (BF, D, D, first)],
        m=t, n=D, tm=512, tn=D, name="ffn_down_loss", vecs=[norm_final_g], sums=[D, 128])

    def swiglu_bwd(prods, ex):
        (dz,), (da_, db_) = prods, ex
        return dz * da_.astype(F32), dz * db_.astype(F32)

    ffn_tiles = [(dz_dgate, FFN // 2, col), (dz_dup, FFN // 2, col)]
    dgt, dup = _fmm([dh2_b], [(0, wd, True)], ffn_tiles, swiglu_bwd, [(BF, FFN, FFN // 2, col)] * 2,
                    m=t, n=FFN, tm=1024, tn=FFN // 2, name="d_gate_up", cols_outer=True)
    (d_wd,) = _wgrad([z], dh2_b, name="d_w_down")
    (du2,) = _fmm([dgt, dup], [(0, wg_t, False), (1, wu_t, False)], [], lambda prods, ex: (prods[0] + prods[1],),
                  [(F32, D, 512, col)], m=t, n=D, tm=1024, tn=512, name="d_u2")
    d_wg, d_wu = _wgrad([dgt, dup], u2, name="d_w_gate_up")
    dh1, dh1_b, d_norm_ffn = _rms_bwd(du2, h1, norm_ffn_g, dh2, tm=512, name="rms_ffn_bwd")
    (d_wout,) = _wgrad([merged], dh1_b, name="d_w_out")

    def merge_bwd(prods, ex):
        (dm,), (sa, sb, ca, cb, wa, wb) = prods, ex
        dgate = jnp.concatenate([dm * ca.astype(F32), dm * cb.astype(F32)], axis=1)
        dya_ = (dm * sa.astype(F32)).astype(BF)
        dyb_ = (dm * sb.astype(F32)).astype(BF)
        return (dya_, dyb_, dgate, lax.dot_general(dya_, wa, _NT, preferred_element_type=F32),
                lax.dot_general(dyb_, wb, _NT, preferred_element_type=F32))

    ffn_grads = (d_wg, d_wu, d_wd)
    (dya, dyb, dgates, dy_attn, dy_hgrn), got = _fmm(
        [dh1_b], [(0, wout, True)], [(a, D, first) for a in (sig_a, sig_b, dgate_a, dgate_b)], merge_bwd,
        [(BF, D, D, first), (BF, D, D, first), (BF, 2 * D, 2 * D, first), (BF, D, D, first), (F32, D, D, first)],
        m=t, n=D, tm=512, tn=D, name="d_merge", consts=[wba, wbh], comm=_pair_comm(ffn_grads))
    pair_ffn = _pair_add(ffn_grads, got, core, name="pair_add_ffn")
    (d_wba,) = _wgrad([y_attn], dya, name="d_w_ba")
    (d_wbh,) = _wgrad([y_hgrn], dyb, name="d_w_bh")
    sq_grads = (d_wba, d_wbh, d_wout)
    (dh4, d_logits, d_hgrn_norm), (parts_ffn, *got) = _hgrn_bwd(
        h3, hf, logits, hgrn_norm_g, o_pre, states, dy_hgrn, t=t,
        comm=_both(_chip_comm(pair_ffn), _pair_comm(sq_grads)))
    pair_sq = _pair_add(sq_grads, got, core, name="pair_add_sq")
    (dq, dkv, d_sinks), (parts_sq,) = _attn_bwd(q, kv, sinks, dy_attn, t=t, comm=_chip_comm(pair_sq))
    dps = (dq, dkv, dh4, dgates)
    d_win_t, d_b_in = _inproj_bwd_w(dps, u1, t=t)
    half0, got_in = _inproj_bwd_x(dps, win_t, x, norm_mix_g, dh1, t=t, part=0, comm=_pair_comm([d_win_t]))
    pair_in = _pair_add([d_win_t], got_in, core, name="pair_add_w_in")
    (grad_x, d_norm_mix), (parts_in,) = _inproj_bwd_x(dps, win_t, x, norm_mix_g, dh1, t=t, part=1, prev=half0,
                                                      comm=_chip_comm(pair_in))

    small_grads = (d_norm_mix, d_b_in, d_sinks, d_logits, d_hgrn_norm, d_norm_ffn, d_norm_final)
    return loss_row, grad_x, (parts_in, parts_ffn, parts_sq), small_grads


def kernel(x, norm_mix_g, w_in, b_in, attn_sinks, hgrn_lb_logits, hgrn_norm_g, w_branch_attn, w_branch_hgrn, w_out, norm_ffn_g, w_ffn_gate, w_ffn_up, w_ffn_down, norm_final_g, loss_target, m_norm_mix_g, m_w_in, m_b_in, m_attn_sinks, m_hgrn_lb_logits, m_hgrn_norm_g, m_w_branch_attn, m_w_branch_hgrn, m_w_out, m_norm_ffn_g, m_w_ffn_gate, m_w_ffn_up, m_w_ffn_down, m_norm_final_g, v_norm_mix_g, v_w_in, v_b_in, v_attn_sinks, v_hgrn_lb_logits, v_hgrn_norm_g, v_w_branch_attn, v_w_branch_hgrn, v_w_out, v_norm_ffn_g, v_w_ffn_gate, v_w_ffn_up, v_w_ffn_down, v_norm_final_g):
    shards = [w_in[0].T.astype(BF), w_ffn_gate[0].T.astype(BF), w_ffn_up[0].T.astype(BF),
              w_ffn_down[0].astype(BF), w_branch_attn[0].astype(BF), w_branch_hgrn[0].astype(BF),
              w_out[0].astype(BF)]
    loss_row, grad_x, grad_parts, small_grads = _step(
        x[0], loss_target[0], shards, norm_mix_g, b_in, attn_sinks, hgrn_lb_logits, hgrn_norm_g,
        norm_ffn_g, norm_final_g.reshape(1, D))

    d_norm_mix, d_b_in, d_sinks, d_logits, d_hgrn_norm, d_norm_ffn, d_norm_final = small_grads
    row = lambda a: a.reshape(1, D)
    loss_out, small = _small_allreduce_adam(
        dict(norm_mix_g=d_norm_mix, hgrn_norm_g=d_hgrn_norm, norm_ffn_g=d_norm_ffn, norm_final_g=d_norm_final,
             hgrn_lb_logits=d_logits, attn_sinks=d_sinks, b_in=d_b_in),
        loss_row,
        dict(norm_mix_g=(norm_mix_g, m_norm_mix_g, v_norm_mix_g), hgrn_norm_g=(hgrn_norm_g, m_hgrn_norm_g, v_hgrn_norm_g),
             norm_ffn_g=(norm_ffn_g, m_norm_ffn_g, v_norm_ffn_g),
             norm_final_g=(row(norm_final_g), row(m_norm_final_g), row(v_norm_final_g)),
             hgrn_lb_logits=(hgrn_lb_logits, m_hgrn_lb_logits, v_hgrn_lb_logits),
             attn_sinks=(attn_sinks, m_attn_sinks, v_attn_sinks), b_in=(b_in, m_b_in, v_b_in)))
    small["norm_final_g"] = [a.reshape(D) for a in small["norm_final_g"]]
    loss = loss_out[0, 0]

    names = ["w_in", "w_ffn_gate", "w_ffn_up", "w_ffn_down", "w_branch_attn", "w_branch_hgrn", "w_out"]
    w_full = dict(w_in=(w_in, m_w_in, v_w_in), w_ffn_gate=(w_ffn_gate, m_w_ffn_gate, v_w_ffn_gate),
                  w_ffn_up=(w_ffn_up, m_w_ffn_up, v_w_ffn_up), w_ffn_down=(w_ffn_down, m_w_ffn_down, v_w_ffn_down),
                  w_branch_attn=(w_branch_attn, m_w_branch_attn, v_w_branch_attn),
                  w_branch_hgrn=(w_branch_hgrn, m_w_branch_hgrn, v_w_branch_hgrn),
                  w_out=(w_out, m_w_out, v_w_out))
    big = {}
    for group, parts, tag in zip((names[0:1], names[1:4], names[4:7]), grad_parts, ("w_in", "ffn", "square")):
        flip = [name in names[0:3] for name in group]
        view = lambda a, f: a[0].T if f else a[0]
        cols = [[view(w_full[name][j], f) for name, f in zip(group, flip)] for j in range(3)]
        res = _adam(cols[0], parts, cols[1], cols[2], name="adam_" + tag)
        for name, f, r in zip(group, flip, res):
            big[name] = [a.T[None] if f else a[None] for a in r]

    order = ["norm_mix_g", "w_in", "b_in", "attn_sinks", "hgrn_lb_logits", "hgrn_norm_g", "w_branch_attn",
             "w_branch_hgrn", "w_out", "norm_ffn_g", "w_ffn_gate", "w_ffn_up", "w_ffn_down", "norm_final_g"]
    outs = [loss, grad_x[None]]
    for kind in range(4):
        for name in order:
            outs.append(big[name][kind] if name in big else small[name][kind])
    return tuple(outs)
```

```python
import math

import jax
import jax.numpy as jnp
from jax import lax
from jax.experimental import pallas as pl
from jax.experimental.pallas import tpu as pltpu

F32 = jnp.float32
BF = jnp.bfloat16
MESH = pl.DeviceIdType.MESH

D = 1024
HEAD = 64
N_PAIR = 8
BLK = 128
CH = 64
HG_SUB = 4
HG_SUB_BWD = 2
HG_HEADS = 8
HG_K = 128
FFN = 2816
IN_W = 7424
N_DEV = 8
N_CHIP = 4
EPS = 1e-6
NEG = -1e30
SCALE = 1.0 / math.sqrt(HEAD)
VMEM_LIMIT = 56 * 1024 * 1024
WT = 256

ADAM_LR, ADAM_B1, ADAM_B2, ADAM_EPS, ADAM_WD, ADAM_STEP = 0.001, 0.9, 0.999, 1e-08, 0.01, 10

GRP_OFF = (0, D // WT, (D + 256) // WT, (5 * D + 256) // WT)
GRP_N = (D // WT, 256 // WT, 4 * D // WT, 2 * D // WT)
SMALL_ROWS = 16


_NN = (((1,), (0,)), ((), ()))
_NT = (((1,), (1,)), ((), ()))
_TN = (((0,), (0,)), ((), ()))


def _pcall(body, **kw):
    return pl.pallas_call(body, **kw)


def _cp(sem=None, **kw):
    return pltpu.CompilerParams(dimension_semantics=sem, vmem_limit_bytes=VMEM_LIMIT, **kw)


def _sig(v):
    return 0.5 * jnp.tanh(0.5 * v) + 0.5


def _accum(ref, val, first):
    @pl.when(first)
    def _():
        ref[...] = val

    @pl.when(jnp.logical_not(first))
    def _():
        ref[...] += val


class _Comm:
    def __init__(self, ins, out_shapes, sem_shapes, phases):
        self.ins, self.out_shapes, self.sem_shapes, self.phases = list(ins), list(out_shapes), list(sem_shapes), phases


def _both(a, b):
    ni, no, ns = len(a.ins), len(a.out_shapes), len(a.sem_shapes)

    def of_a(fn):
        return lambda ins, outs, sems: fn(ins[:ni], outs[:no], sems[:ns])

    def of_b(fn):
        return lambda ins, outs, sems: fn(ins[ni:], outs[no:], sems[ns:])

    return _Comm(a.ins + b.ins, a.out_shapes + b.out_shapes, a.sem_shapes + b.sem_shapes,
                 [(f, of_a(fn)) for f, fn in a.phases] + [(f, of_b(fn)) for f, fn in b.phases])


def _host(body, comm, n_in, n_out, n_scr, nsteps, step_fn):
    if comm is None:
        return body
    ci, co = len(comm.ins), len(comm.out_shapes)

    def wrapped(*refs):
        p = 0
        ins, p = refs[p:p + n_in], p + n_in
        cins, p = refs[p:p + ci], p + ci
        outs, p = refs[p:p + n_out], p + n_out
        couts, p = refs[p:p + co], p + co
        scr, p = refs[p:p + n_scr], p + n_scr
        csems = refs[p:]
        step = step_fn()
        for frac, fn in comm.phases:
            if frac < 1.0:
                @pl.when(step == int(round(frac * (nsteps - 1))))
                def _(fn=fn):
                    fn(cins, couts, csems)
        body(*ins, *outs, *scr)
        for frac, fn in comm.phases:
            if frac >= 1.0:
                @pl.when(step == nsteps - 1)
                def _(fn=fn):
                    fn(cins, couts, csems)

    return wrapped


def _hosted_call(body, comm, args, *, name, grid, in_specs, out_specs, out_shape, scratch_shapes, sem,
                 nsteps, step_fn, aliases=None):
    n_in, n_out, n_scr = len(in_specs), len(out_specs), len(scratch_shapes)
    args = list(args)
    extra = {}
    if comm is not None:
        in_specs = list(in_specs) + [_hbm_spec()] * len(comm.ins)
        out_specs = list(out_specs) + [_hbm_spec()] * len(comm.out_shapes)
        out_shape = list(out_shape) + comm.out_shapes
        scratch_shapes = list(scratch_shapes) + comm.sem_shapes
        args += comm.ins
        extra = dict(has_side_effects=True)
    outs = _pcall(_host(body, comm, n_in, n_out, n_scr, nsteps, step_fn), name=name, grid=grid,
                  in_specs=in_specs, out_specs=out_specs, out_shape=out_shape, scratch_shapes=scratch_shapes,
                  input_output_aliases=aliases or {}, compiler_params=_cp(sem, **extra))(*args)
    return list(outs[:n_out]), list(outs[n_out:])


def _hbm_spec():
    return pl.BlockSpec(memory_space=pl.ANY)


def _wgrad(a_list, b, *, name):
    (t, m), n, gm = a_list[0].shape, b.shape[1], a_list[0].shape[1] // WT
    n_a = len(a_list)
    tile = lambda k: (lambda s: jnp.clip(s - k * gm, 0, gm - 1))

    def body(*refs):
        a_refs, b_ref, o_refs = refs[:n_a], refs[n_a], refs[n_a + 1:]
        s = pl.program_id(0)
        for k in range(n_a):
            @pl.when(jnp.logical_and(s >= k * gm, s < (k + 1) * gm))
            def _(k=k):
                o_refs[k][...] = lax.dot_general(a_refs[k][...], b_ref[...], _TN,
                                                 preferred_element_type=F32).astype(BF)

    return _pcall(body, name=name, grid=(n_a * gm,),
                  in_specs=[pl.BlockSpec((t, WT), lambda s, k=k: (0, tile(k)(s))) for k in range(n_a)]
                  + [pl.BlockSpec((t, n), lambda s: (0, 0))],
                  out_specs=[pl.BlockSpec((WT, n), lambda s, k=k: (tile(k)(s), 0)) for k in range(n_a)],
                  out_shape=[jax.ShapeDtypeStruct((m, n), BF)] * n_a,
                  compiler_params=_cp(("arbitrary",)))(*a_list, b)


def _fmm(lhs, rhs, extras, epilogue, outs, *, m, n, tm, tn, name, comm=None, vecs=(), consts=(), sums=(),
         cols_outer=False):
    tm, tn = min(tm, m), min(tn, n)
    assert m % tm == 0 and n % tn == 0 and (not sums or (tn == n and not cols_outer)), (name, m, n, tm, tn)
    in_specs, args = [], []
    for a in lhs:
        in_specs.append(pl.BlockSpec((tm, a.shape[1]), lambda i, j: (i, 0)))
        args.append(a)
    for li, b, tb in rhs:
        k = lhs[li].shape[1]
        in_specs.append(pl.BlockSpec((tn, k), lambda i, j: (j, 0)) if tb
                        else pl.BlockSpec((k, tn), lambda i, j: (0, j)))
        args.append(b)
    for arr, w, col in extras:
        in_specs.append(pl.BlockSpec((tm, w), lambda i, j, col=col: (i, col(j))))
        args.append(arr)
    for vec in vecs:
        in_specs.append(pl.BlockSpec((1, tn), lambda i, j: (0, j)))
        args.append(vec)
    for whole in consts:
        in_specs.append(pl.BlockSpec(whole.shape, lambda i, j: (0, 0)))
        args.append(whole)
    out_specs = [pl.BlockSpec((tm, w), lambda i, j, col=col: (i, col(j))) for _, _, w, col in outs]
    out_shape = [jax.ShapeDtypeStruct((m, total), dt) for dt, total, _, _ in outs]
    for w in sums:
        out_specs.append(pl.BlockSpec((1, w), lambda i, j: (0, 0)))
        out_shape.append(jax.ShapeDtypeStruct((1, w), F32))
    nl, nr, ne, no = len(lhs), len(rhs), len(extras) + len(vecs) + len(consts), len(outs)

    def body(*refs):
        prods = []
        for r, (li, _, tb) in enumerate(rhs):
            prods.append(lax.dot_general(refs[li][...], refs[nl + r][...], _NT if tb else _NN,
                                         preferred_element_type=F32))
        vals = epilogue(prods, [ref[...] for ref in refs[nl + nr:nl + nr + ne]])
        o_refs = refs[nl + nr + ne:]
        for o_ref, v in zip(o_refs[:no], vals[:no]):
            o_ref[...] = v.astype(o_ref.dtype)
        for s_ref, v in zip(o_refs[no:], vals[no:]):
            _accum(s_ref, v, pl.program_id(0) == 0)

    grid = (m // tm, n // tn)
    if cols_outer:
        flip = lambda spec: pl.BlockSpec(spec.block_shape, lambda j, i, f=spec.index_map: f(i, j))
        in_specs, out_specs, grid = [flip(s) for s in in_specs], [flip(s) for s in out_specs], grid[::-1]
    res, comm_res = _hosted_call(
        body, comm, args, name=name, grid=grid, in_specs=in_specs, out_specs=out_specs,
        out_shape=out_shape, scratch_shapes=[], sem=("arbitrary", "arbitrary"), nsteps=grid[0] * grid[1],
        step_fn=lambda: pl.program_id(0) * grid[1] + pl.program_id(1))
    return res if comm is None else (res, comm_res)


def _grp_of(i):
    return [jnp.logical_and(i >= GRP_OFF[g], i < GRP_OFF[g] + GRP_N[g]) for g in range(4)]


def _grp_idx(i, g):
    return jnp.clip(i - GRP_OFF[g], 0, GRP_N[g] - 1)


def _inproj_fwd(u, win_t, b_in, *, t, comm=None):
    tm = min(1024, t)
    n_row = t // tm
    n_chunks, h_first, g_first = 8, 2, 6
    sub = D // WT

    def w_block(l):
        return jnp.where(l == 0, GRP_OFF[0], jnp.where(l == 1, GRP_OFF[1], GRP_OFF[2] + sub * (l - h_first)))

    def body(u_ref, *rest):
        w_refs, b_refs, (q_ref, kv_ref, h3_ref, hf_ref, g_ref) = rest[:sub], rest[sub:2 * sub], rest[2 * sub:]
        l = pl.program_id(1)

        @pl.when(l == 1)
        def _():
            kv_ref[...] = (lax.dot_general(u_ref[...], w_refs[0][...], _NT, preferred_element_type=F32)
                           + b_refs[0][...]).astype(BF)

        is_hf = l == h_first + 1
        in_h3 = jnp.logical_and(jnp.logical_and(l >= h_first, l < g_first), jnp.logical_not(is_hf))
        for pred, o_ref in ((l == 0, q_ref), (in_h3, h3_ref), (is_hf, hf_ref), (l >= g_first, g_ref)):
            @pl.when(pred)
            def _(o_ref=o_ref):
                w = jnp.concatenate([w[...] for w in w_refs], axis=0)
                b = jnp.concatenate([b[...] for b in b_refs], axis=1)
                o_ref[...] = (lax.dot_general(u_ref[...], w, _NT, preferred_element_type=F32) + b).astype(o_ref.dtype)

    return _hosted_call(
        body, comm, [u] + [win_t] * sub + [b_in] * sub, name="inproj_fwd", grid=(n_row, n_chunks),
        in_specs=[pl.BlockSpec((tm, D), lambda i, l: (i, 0))]
        + [pl.BlockSpec((WT, D), lambda i, l, o=o: (w_block(l) + o, 0)) for o in range(sub)]
        + [pl.BlockSpec((1, WT), lambda i, l, o=o: (0, w_block(l) + o)) for o in range(sub)],
        out_specs=[pl.BlockSpec((tm, D), lambda i, l: (i, 0)),
                   pl.BlockSpec((tm, 256), lambda i, l: (i, 0)),
                   pl.BlockSpec((tm, D), lambda i, l: (i, jnp.clip(l - h_first - 1, 0, 2))),
                   pl.BlockSpec((tm, D), lambda i, l: (i, 0)),
                   pl.BlockSpec((tm, D), lambda i, l: (i, jnp.clip(l - g_first, 0, 1)))],
        out_shape=[jax.ShapeDtypeStruct((t, D), BF), jax.ShapeDtypeStruct((t, 256), BF),
                   jax.ShapeDtypeStruct((t, 3 * D), BF), jax.ShapeDtypeStruct((t, D), F32),
                   jax.ShapeDtypeStruct((t, 2 * D), BF)],
        scratch_shapes=[], sem=("arbitrary", "arbitrary"), nsteps=n_row * n_chunks,
        step_fn=lambda: pl.program_id(0) * n_chunks + pl.program_id(1))


def _inproj_bwd_x(dps, win_t, x, g, resid, *, t, part, prev=None, comm=None):
    n_row = 8 if t >= 4096 else 4
    tm = t // n_row
    first = 1
    per = first if part == 0 else n_row - first
    row = lambda i: part * first + i

    n_chunks = 4
    sub = 2 * D // WT

    def w_block(l):
        return jnp.where(l == 0, 0, GRP_OFF[2] + sub * (l - 1))

    def body(d0, d1, d2, d3, *rest):
        w_refs, (x_ref, g_ref, r_ref) = rest[:sub], rest[sub:sub + 3]
        dg_prev = rest[sub + 3] if prev is not None else None
        o_ref, dg_ref, acc_ref = rest[-3], rest[-2], rest[-1]
        i, l = pl.program_id(0), pl.program_id(1)

        @pl.when(l == 0)
        def _():
            wq = jnp.concatenate([w[...] for w in w_refs[:GRP_N[0]]], axis=0)
            acc_ref[...] = (jnp.dot(d0[...], wq, preferred_element_type=F32)
                            + jnp.dot(d1[...], w_refs[GRP_N[0]][...], preferred_element_type=F32))

        for pred, d_ref in ((jnp.logical_and(l >= 1, l < 3), d2), (l == 3, d3)):
            @pl.when(pred)
            def _(d_ref=d_ref):
                w = jnp.concatenate([w[...] for w in w_refs], axis=0)
                acc_ref[...] += jnp.dot(d_ref[...], w, preferred_element_type=F32)

        @pl.when(l == n_chunks - 1)
        def _():
            xv = x_ref[...]
            r = lax.rsqrt(jnp.mean(xv * xv, axis=-1, keepdims=True) + EPS)
            xh = xv * r
            du = acc_ref[...]
            dxh = du * g_ref[...]
            o_ref[...] = r_ref[...] + r * (dxh - xh * jnp.mean(dxh * xh, axis=-1, keepdims=True))
            dg = jnp.sum(du * xh, axis=0, keepdims=True)
            if dg_prev is not None:
                dg = dg + jnp.where(i == 0, 1.0, 0.0) * dg_prev[...]
            _accum(dg_ref, dg, i == 0)

    rows = lambda w: pl.BlockSpec((tm, w), lambda i, l: (row(i), 0))
    in_specs = ([rows(D), rows(256),
                 pl.BlockSpec((tm, 2 * D), lambda i, l: (row(i), jnp.clip(l - 1, 0, 1))), rows(2 * D)]
                + [pl.BlockSpec((WT, D), lambda i, l, o=o: (w_block(l) + o, 0)) for o in range(sub)]
                + [rows(D), pl.BlockSpec((1, D), lambda i, l: (0, 0)), rows(D)])
    args = list(dps) + [win_t] * sub + [x, g, resid]
    aliases = None
    if prev is not None:
        in_specs += [pl.BlockSpec((1, D), lambda i, l: (0, 0)), _hbm_spec()]
        args += [prev[1], prev[0]]
        aliases = {len(args) - 1: 0}
    return _hosted_call(
        body, comm, args, name="inproj_bwd_x%d" % part, grid=(per, n_chunks), in_specs=in_specs,
        out_specs=[rows(D), pl.BlockSpec((1, D), lambda i, l: (0, 0))],
        out_shape=[jax.ShapeDtypeStruct((t, D), F32), jax.ShapeDtypeStruct((1, D), F32)],
        scratch_shapes=[pltpu.VMEM((tm, D), F32)], sem=("arbitrary", "arbitrary"), nsteps=per * n_chunks,
        step_fn=lambda: pl.program_id(0) * n_chunks + pl.program_id(1), aliases=aliases)


def _inproj_bwd_w(dps, u, *, t):
    n_tiles = IN_W // WT
    dims = (((0,), (0,)), ((), ()))

    def body(d0, d1, d2, d3, u_ref, o_ref, db_ref):
        i = pl.program_id(0)
        uv = u_ref[...]
        for g, (pred, d_ref) in enumerate(zip(_grp_of(i), (d0, d1, d2, d3))):
            @pl.when(pred)
            def _(d_ref=d_ref):
                dv = d_ref[...]
                o_ref[...] = lax.dot_general(dv, uv, dims, preferred_element_type=F32).astype(BF)
                db_ref[...] = jnp.sum(dv.astype(F32), axis=0, keepdims=True)

    return _pcall(body, name="inproj_bwd_w", grid=(n_tiles,),
                  in_specs=[pl.BlockSpec((t, WT), lambda i, g=g: (0, _grp_idx(i, g))) for g in range(4)]
                  + [pl.BlockSpec((t, D), lambda i: (0, 0))],
                  out_specs=[pl.BlockSpec((WT, D), lambda i: (i, 0)),
                             pl.BlockSpec((1, WT), lambda i: (0, i))],
                  out_shape=[jax.ShapeDtypeStruct((IN_W, D), BF), jax.ShapeDtypeStruct((1, IN_W), F32)],
                  compiler_params=_cp(("arbitrary",)))(*dps, u)


def _row_spec(tm, width, col=0):
    return pl.BlockSpec((tm, width), lambda i: (i, col))


def _vec_spec(width):
    return pl.BlockSpec((1, width), lambda i: (0, 0))


def _rms_fwd(x, g, *, tm, name, comm=None):
    t = x.shape[0]
    tm = min(tm, t)

    def body(x_ref, g_ref, u_ref):
        xv = x_ref[...]
        r = lax.rsqrt(jnp.mean(xv * xv, axis=-1, keepdims=True) + EPS)
        u_ref[...] = (xv * r * g_ref[...]).astype(BF)

    (u,), comm_res = _hosted_call(
        body, comm, (x, g), name=name, grid=(t // tm,), in_specs=[_row_spec(tm, D), _vec_spec(D)],
        out_specs=[_row_spec(tm, D)], out_shape=[jax.ShapeDtypeStruct((t, D), BF)], scratch_shapes=[],
        sem=("arbitrary",), nsteps=t // tm, step_fn=lambda: pl.program_id(0))
    return u if comm is None else (u, comm_res)


def _rms_bwd(du, x, g, resid, *, tm, name):
    t = x.shape[0]
    tm = min(tm, t)

    def body(du_ref, x_ref, g_ref, r_ref, dx_ref, dxb_ref, dg_ref):
        xv = x_ref[...]
        r = lax.rsqrt(jnp.mean(xv * xv, axis=-1, keepdims=True) + EPS)
        xh = xv * r
        duv = du_ref[...]
        dxh = duv * g_ref[...]
        dx = r_ref[...] + r * (dxh - xh * jnp.mean(dxh * xh, axis=-1, keepdims=True))
        dx_ref[...] = dx
        dxb_ref[...] = dx.astype(BF)
        _accum(dg_ref, jnp.sum(duv * xh, axis=0, keepdims=True), pl.program_id(0) == 0)

    return _pcall(body, name=name, grid=(t // tm,),
                  in_specs=[_row_spec(tm, D), _row_spec(tm, D), _vec_spec(D), _row_spec(tm, D)],
                  out_specs=[_row_spec(tm, D), _row_spec(tm, D), _vec_spec(D)],
                  out_shape=[jax.ShapeDtypeStruct((t, D), F32), jax.ShapeDtypeStruct((t, D), BF),
                             jax.ShapeDtypeStruct((1, D), F32)],
                  compiler_params=_cp(("arbitrary",)))(du, x, g, resid)


def _attn_kv_tiles(kprev, kcur):
    kv = jnp.concatenate([kprev, kcur], axis=0).astype(F32)
    lo = lax.broadcasted_iota(jnp.int32, (2 * BLK, 128), 1) < HEAD
    tiles = []
    for part in (kv[:, 0:128], kv[:, 128:256]):
        rolled = pltpu.roll(part, HEAD, 1)
        z = jnp.zeros_like(part)
        tiles.append(((jnp.where(lo, part, z).astype(BF), jnp.where(lo, z, rolled).astype(BF)),
                      (jnp.where(lo, rolled, z).astype(BF), jnp.where(lo, z, part).astype(BF))))
    k_t, v_t = tiles
    return [(jnp.concatenate(k_t[h], axis=0), jnp.concatenate(v_t[h], axis=0)) for h in range(2)]


def _attn_mask(i):
    qi = lax.broadcasted_iota(jnp.int32, (BLK, 2 * BLK), 0)
    kj = lax.broadcasted_iota(jnp.int32, (BLK, 2 * BLK), 1)
    first_key = jnp.where(i == 0, BLK, 0)
    in_prev = jnp.logical_and(jnp.logical_and(kj < BLK, kj > qi), kj >= first_key)
    in_cur = jnp.logical_and(kj >= BLK, kj - BLK <= qi)
    return jnp.logical_or(in_prev, in_cur)


def _attn_probs(s, sink, valid):
    s = jnp.where(valid, s * SCALE, NEG)
    mx = jnp.maximum(jnp.max(s, axis=-1, keepdims=True), sink)
    e = jnp.exp(s - mx)
    es = jnp.exp(sink - mx)
    inv = 1.0 / (jnp.sum(e, axis=-1, keepdims=True) + es)
    return e * inv, es * inv


_KEYS = 2 * BLK


def _pair(ref, j):
    return ref[:, j * 128:(j + 1) * 128]


def _attn_fwd(q, kv, sinks, *, t, comm=None):
    nb = t // BLK
    sub = 2

    def body(sink_ref, q_ref, kp_ref, kc_ref, o_ref):
        i = pl.program_id(0)
        for c in range(sub):
            rows = slice(c * BLK, (c + 1) * BLK)
            valid = _attn_mask(sub * i + c)
            tiles = _attn_kv_tiles(kp_ref[...] if c == 0 else kc_ref[(c - 1) * BLK:c * BLK, :], kc_ref[rows, :])
            s = [lax.dot_general(q_ref[rows, j * 128:(j + 1) * 128], tiles[j // 4][0], _NT,
                                 preferred_element_type=F32) for j in range(N_PAIR)]
            p = []
            for j in range(N_PAIR):
                pe, _ = _attn_probs(s[j][:, 0:_KEYS], sink_ref[0, 2 * j], valid)
                po, _ = _attn_probs(s[j][:, _KEYS:2 * _KEYS], sink_ref[0, 2 * j + 1], valid)
                p.append(jnp.concatenate([pe.astype(BF), po.astype(BF)], axis=1))
            for j in range(N_PAIR):
                o_ref[rows, j * 128:(j + 1) * 128] = jnp.dot(p[j], tiles[j // 4][1],
                                                             preferred_element_type=F32).astype(BF)

    return _hosted_call(
        body, comm, (sinks, q, kv, kv), name="attn_fwd", grid=(nb // sub,),
        in_specs=[pl.BlockSpec(memory_space=pltpu.SMEM),
                  pl.BlockSpec((sub * BLK, D), lambda i: (i, 0)),
                  pl.BlockSpec((BLK, 256), lambda i: (jnp.maximum(sub * i - 1, 0), 0)),
                  pl.BlockSpec((sub * BLK, 256), lambda i: (i, 0))],
        out_specs=[pl.BlockSpec((sub * BLK, D), lambda i: (i, 0))],
        out_shape=[jax.ShapeDtypeStruct((t, D), BF)],
        scratch_shapes=[], sem=("arbitrary",), nsteps=nb // sub, step_fn=lambda: pl.program_id(0))


def _attn_bwd(q, kv, sinks, do, *, t, comm=None):
    nb = t // BLK
    last = nb - 1

    def body(sink_ref, q_ref, kp_ref, kc_ref, do_ref, dq_ref, dkv_ref, ds_ref, carry_ref):
        i = pl.program_id(0)

        @pl.when(i == 0)
        def _():
            ds_ref[...] = jnp.zeros_like(ds_ref)
            carry_ref[...] = jnp.zeros_like(carry_ref)

        @pl.when(i < nb)
        def _():
            valid = _attn_mask(i)
            tiles = _attn_kv_tiles(kp_ref[...], kc_ref[...])
            lane1 = lax.broadcasted_iota(jnp.int32, (1, 128), 1)
            dsink = jnp.zeros((1, 128), F32)
            s = [lax.dot_general(_pair(q_ref, j), tiles[j // 4][0], _NT, preferred_element_type=F32)
                 for j in range(N_PAIR)]
            dp = [lax.dot_general(_pair(do_ref, j), tiles[j // 4][1], _NT, preferred_element_type=F32)
                  for j in range(N_PAIR)]
            p_all, ds_all = [], []
            for j in range(N_PAIR):
                halves = []
                for par in range(2):
                    cols = slice(par * _KEYS, (par + 1) * _KEYS)
                    p, ps = _attn_probs(s[j][:, cols], sink_ref[0, 2 * j + par], valid)
                    dpj = dp[j][:, cols]
                    dd = jnp.sum(p * dpj, axis=-1, keepdims=True)
                    dsink = dsink + jnp.where(lane1 == 2 * j + par,
                                              -jnp.sum(ps * dd, axis=0, keepdims=True), 0.0)
                    halves.append((p.astype(BF), (p * (dpj - dd)).astype(BF)))
                p_all.append(jnp.concatenate([halves[0][0], halves[1][0]], axis=1))
                ds_all.append(jnp.concatenate([halves[0][1], halves[1][1]], axis=1))
            for j in range(N_PAIR):
                dq_ref[:, j * 128:(j + 1) * 128] = (
                    jnp.dot(ds_all[j], tiles[j // 4][0], preferred_element_type=F32) * SCALE).astype(BF)
            ds_ref[...] += dsink
            gk, gv = [], []
            for h in range(2):
                grp = range(4 * h, 4 * h + 4)
                q_rows = jnp.concatenate([_pair(q_ref, j) for j in grp], axis=0)
                do_rows = jnp.concatenate([_pair(do_ref, j) for j in grp], axis=0)
                g_k = lax.dot_general(jnp.concatenate([ds_all[j] for j in grp], axis=0), q_rows, _TN,
                                      preferred_element_type=F32)
                g_v = lax.dot_general(jnp.concatenate([p_all[j] for j in grp], axis=0), do_rows, _TN,
                                      preferred_element_type=F32)
                gk.append((g_k[0:_KEYS], g_k[_KEYS:2 * _KEYS]))
                gv.append((g_v[0:_KEYS], g_v[_KEYS:2 * _KEYS]))
            lo = lax.broadcasted_iota(jnp.int32, (2 * BLK, 128), 1) < HEAD
            zero = jnp.zeros((2 * BLK, 128), F32)

            def unpad(g):
                return (jnp.where(lo, g[0][0] + pltpu.roll(g[0][1], HEAD, 1), zero)
                        + jnp.where(lo, zero, pltpu.roll(g[1][0], HEAD, 1) + g[1][1]))

            dk = unpad(gk) * SCALE
            dv = unpad(gv)
            dkv_ref[:, 0:128] = (carry_ref[:, 0:128] + dk[0:BLK]).astype(BF)
            dkv_ref[:, 128:256] = (carry_ref[:, 128:256] + dv[0:BLK]).astype(BF)
            carry_ref[:, 0:128] = dk[BLK:2 * BLK]
            carry_ref[:, 128:256] = dv[BLK:2 * BLK]

        @pl.when(i == nb)
        def _():
            dkv_ref[...] = carry_ref[...].astype(BF)

    return _hosted_call(
        body, comm, (sinks, q, kv, kv, do), name="attn_bwd", grid=(nb + 1,),
        in_specs=[pl.BlockSpec(memory_space=pltpu.SMEM),
                  pl.BlockSpec((BLK, D), lambda i: (jnp.minimum(i, last), 0)),
                  pl.BlockSpec((BLK, 256), lambda i: (jnp.clip(i - 1, 0, last), 0)),
                  pl.BlockSpec((BLK, 256), lambda i: (jnp.minimum(i, last), 0)),
                  pl.BlockSpec((BLK, D), lambda i: (jnp.minimum(i, last), 0))],
        out_specs=[pl.BlockSpec((BLK, D), lambda i: (jnp.minimum(i, last), 0)),
                   pl.BlockSpec((BLK, 256), lambda i: (jnp.maximum(i - 1, 0), 0)),
                   pl.BlockSpec((1, 128), lambda i: (0, 0))],
        out_shape=[jax.ShapeDtypeStruct((t, D), BF), jax.ShapeDtypeStruct((t, 256), BF),
                   jax.ShapeDtypeStruct((1, 128), F32)],
        scratch_shapes=[pltpu.VMEM((BLK, 256), F32)], sem=("arbitrary",), nsteps=nb + 1,
        step_fn=lambda: pl.program_id(0))


def _split3(v):
    h = v.astype(BF)
    r = v - h.astype(F32)
    m = r.astype(BF)
    lo = (r - m.astype(F32)).astype(BF)
    return jnp.concatenate([h, m, lo], axis=1)


def _apply01(mat, v):
    n = v.shape[1]
    r = jnp.dot(mat, _split3(v), preferred_element_type=F32)
    return r[:, 0:n] + r[:, n:2 * n] + r[:, 2 * n:3 * n]


def _hgrn_gates(hq, hf, lb):
    sq = _sig(hq)
    sg = _sig(hf)
    f = lb + (1.0 - lb) * sg
    return hq * sq, (1.0 - lb) * (1.0 - sg), jnp.log(f), sq, sg, f


def _tri(upper):
    r = lax.broadcasted_iota(jnp.int32, (CH, CH), 0)
    c = lax.broadcasted_iota(jnp.int32, (CH, CH), 1)
    return (c >= r) if upper else (c <= r)


def _lb_from_logits(lg_ref):
    return 1.0 / (1.0 + jnp.exp(lg_ref[1:2, :] - lg_ref[0:1, :]))


def _hgrn_fwd(h3, hf, logits, norm_g, *, t, comm=None):
    nc = t // CH
    nt_dims = (((1,), (1,)), ((), ()))
    tn_dims = (((0,), (0,)), ((), ()))

    def body(h_ref, hf_ref, lg_ref, ng_ref, y_ref, o_ref, st_ref, s_scr, b_scr, qa_s, ka_s, qb_s, kb_s, v_s):
        @pl.when(pl.program_id(0) == 0)
        def _():
            s_scr[...] = jnp.zeros_like(s_scr)

        heads = [slice(h * HG_K, (h + 1) * HG_K) for h in range(HG_HEADS)]
        causal = _tri(False)
        lb = _lb_from_logits(lg_ref)
        for c in range(HG_SUB):
            rows = slice(c * CH, (c + 1) * CH)
            q, k, g, _, _, _ = _hgrn_gates(h_ref[rows, 0:D].astype(F32), hf_ref[rows, :], lb)
            b_scr[...] = _apply01(jnp.where(causal, 1.0, 0.0).astype(BF), g)
            b = b_scr[...]
            b_mid = b_scr[CH // 2 - 1:CH // 2, :]
            b_last = b_scr[CH - 1:CH, :]
            qa_s[...] = (q * jnp.exp(b - b_mid)).astype(BF)
            ka_s[...] = (k * jnp.exp(b_mid - b)).astype(BF)
            qb_s[...] = (q * jnp.exp(b)).astype(BF)
            kb_s[...] = (k * jnp.exp(b_last - b)).astype(BF)
            v_s[...] = h_ref[rows, D:2 * D]
            dec = jnp.exp(b_last)
            st_ref[c] = s_scr[...].astype(BF)
            a = [jnp.where(causal, lax.dot_general(qa_s[:, sl], ka_s[:, sl], nt_dims, preferred_element_type=F32),
                           0.0).astype(BF) for sl in heads]
            for h, sl in enumerate(heads):
                o_ref[rows, sl] = (jnp.dot(a[h], v_s[:, sl], preferred_element_type=F32)
                                   + lax.dot_general(qb_s[:, sl], s_scr[h].astype(BF), nt_dims,
                                                     preferred_element_type=F32))
            for h, sl in enumerate(heads):
                s_scr[h] = dec[:, sl] * s_scr[h] + lax.dot_general(v_s[:, sl], kb_s[:, sl], tn_dims,
                                                                   preferred_element_type=F32)
            for h, sl in enumerate(heads):
                o = o_ref[rows, sl]
                on = o * lax.rsqrt(jnp.mean(o * o, axis=-1, keepdims=True) + EPS)
                gate = _sig(h_ref[rows, 2 * D + h * HG_K:2 * D + (h + 1) * HG_K].astype(F32))
                y_ref[rows, sl] = (on * ng_ref[:, sl] * gate).astype(BF)

    half = lambda: pltpu.VMEM((CH, D), BF)
    blk = HG_SUB * CH
    return _hosted_call(
        body, comm, (h3, hf, logits, norm_g), name="hgrn_fwd", grid=(nc // HG_SUB,),
        in_specs=[pl.BlockSpec((blk, 3 * D), lambda n: (n, 0)),
                  pl.BlockSpec((blk, D), lambda n: (n, 0)),
                  pl.BlockSpec((2, D), lambda n: (0, 0)),
                  pl.BlockSpec((1, D), lambda n: (0, 0))],
        out_specs=[pl.BlockSpec((blk, D), lambda n: (n, 0)),
                   pl.BlockSpec((blk, D), lambda n: (n, 0)),
                   pl.BlockSpec((HG_SUB, HG_HEADS, HG_K, HG_K), lambda n: (n, 0, 0, 0))],
        out_shape=[jax.ShapeDtypeStruct((t, D), BF), jax.ShapeDtypeStruct((t, D), F32),
                   jax.ShapeDtypeStruct((nc, HG_HEADS, HG_K, HG_K), BF)],
        scratch_shapes=[pltpu.VMEM((HG_HEADS, HG_K, HG_K), F32), pltpu.VMEM((CH, D), F32),
                        half(), half(), half(), half(), half()],
        sem=("arbitrary",), nsteps=nc // HG_SUB, step_fn=lambda: pl.program_id(0))


def _hgrn_bwd(h3, hf, logits, norm_g, o_pre, states, dy, *, t, comm=None):
    nc = t // CH
    nt_dims = (((1,), (1,)), ((), ()))
    tn_dims = (((0,), (0,)), ((), ()))

    def body(h_ref, hf_ref, lg_ref, ng_ref, o_ref, st_ref, dy_ref, dh_ref, dlg_ref, dng_ref, ds_scr, dlb_scr,
             b_scr, tail_s, e_qa, e_ka, e_qb, e_kb, q_s, k_s, dqa_s, dka_s, dqb_s, dkb_s,
             qa_s, ka_s, qb_s, kb_s, v_s, do_s):
        n = pl.program_id(0)

        @pl.when(n == 0)
        def _():
            ds_scr[...] = jnp.zeros_like(ds_scr)
            dlb_scr[...] = jnp.zeros_like(dlb_scr)
            dng_ref[...] = jnp.zeros_like(dng_ref)

        heads = [slice(h * HG_K, (h + 1) * HG_K) for h in range(HG_HEADS)]
        lb = _lb_from_logits(lg_ref)
        causal = _tri(False)

        def chunk(c):
            rows = slice(c * CH, (c + 1) * CH)
            hq = h_ref[rows, 0:D].astype(F32)
            q, k, g, sq, sg, f = _hgrn_gates(hq, hf_ref[rows, :], lb)
            b_scr[...] = _apply01(jnp.where(causal, 1.0, 0.0).astype(BF), g)
            b = b_scr[...]
            b_mid = b_scr[CH // 2 - 1:CH // 2, :]
            b_last = b_scr[CH - 1:CH, :]
            q_s[...] = q
            k_s[...] = k
            for e_ref, s_ref, base, expo in ((e_qa, qa_s, q, b - b_mid), (e_ka, ka_s, k, b_mid - b),
                                             (e_qb, qb_s, q, b), (e_kb, kb_s, k, b_last - b)):
                e = jnp.exp(expo)
                e_ref[...] = e
                s_ref[...] = (base * e).astype(BF)
            v_s[...] = h_ref[rows, D:2 * D]
            dec = jnp.exp(b_last)
            for h, sl in enumerate(heads):
                gcol = slice(3 * D + h * HG_K, 3 * D + (h + 1) * HG_K)
                ngh = ng_ref[:, sl]
                sgate = _sig(h_ref[rows, 2 * D + h * HG_K:2 * D + (h + 1) * HG_K].astype(F32))
                o = o_ref[rows, sl]
                r = lax.rsqrt(jnp.mean(o * o, axis=-1, keepdims=True) + EPS)
                on = o * r
                dyh = dy_ref[rows, sl]
                dh_ref[rows, gcol] = (dyh * on * ngh * sgate * (1.0 - sgate)).astype(BF)
                dng_ref[:, sl] += jnp.sum(dyh * on * sgate, axis=0, keepdims=True)
                don = dyh * ngh * sgate
                do_s[:, sl] = (r * (don - on * jnp.mean(don * on, axis=-1, keepdims=True))).astype(BF)
            a = [jnp.where(causal, lax.dot_general(qa_s[:, sl], ka_s[:, sl], nt_dims, preferred_element_type=F32),
                           0.0).astype(BF) for sl in heads]
            da = [jnp.where(causal, lax.dot_general(do_s[:, sl], v_s[:, sl], nt_dims, preferred_element_type=F32),
                            0.0).astype(BF) for sl in heads]
            for h, sl in enumerate(heads):
                dh_ref[rows, 2 * D + h * HG_K:2 * D + (h + 1) * HG_K] = (
                    lax.dot_general(a[h], do_s[:, sl], tn_dims, preferred_element_type=F32)
                    + lax.dot_general(kb_s[:, sl], ds_scr[h].astype(BF), nt_dims, preferred_element_type=F32)
                ).astype(BF)
            for h, sl in enumerate(heads):
                dqa_s[:, sl] = jnp.dot(da[h], ka_s[:, sl], preferred_element_type=F32)
            for h, sl in enumerate(heads):
                dka_s[:, sl] = lax.dot_general(da[h], qa_s[:, sl], tn_dims, preferred_element_type=F32)
            for h, sl in enumerate(heads):
                dqb_s[:, sl] = jnp.dot(do_s[:, sl], st_ref[c, h], preferred_element_type=F32)
            for h, sl in enumerate(heads):
                dkb_s[:, sl] = jnp.dot(v_s[:, sl], ds_scr[h].astype(BF), preferred_element_type=F32)
            for h, sl in enumerate(heads):
                tail_s[:, sl] = jnp.sum(dec[:, sl] * st_ref[c, h].astype(F32) * ds_scr[h], axis=0, keepdims=True)
            for h, sl in enumerate(heads):
                ds_scr[h] = (lax.dot_general(do_s[:, sl], qb_s[:, sl], tn_dims, preferred_element_type=F32)
                             + dec[:, sl] * ds_scr[h])
            qv, kv = q_s[...], k_s[...]
            dqa, dka, dqb, dkb = dqa_s[...], dka_s[...], dqb_s[...], dkb_s[...]
            eqa, eka, eqb, ekb = e_qa[...], e_ka[...], e_qb[...], e_kb[...]
            dkb_kb = dkb * (kv * ekb)
            db_last = jnp.sum(dkb_kb, axis=0, keepdims=True) + tail_s[...]
            last_row = lax.broadcasted_iota(jnp.int32, (CH, D), 0) == CH - 1
            db = (dqa * (qv * eqa) - dka * (kv * eka) + dqb * (qv * eqb) - dkb_kb
                  + jnp.where(last_row, db_last, 0.0))
            dg = _apply01(jnp.where(_tri(True), 1.0, 0.0).astype(BF), db)
            dq = dqa * eqa + dqb * eqb
            dk = dka * eka + dkb * ekb
            dh_ref[rows, 0:D] = (dq * sq * (1.0 + hq * (1.0 - sq))).astype(BF)
            dfk = dg / f - dk
            dh_ref[rows, D:2 * D] = ((1.0 - lb) * dfk * sg * (1.0 - sg)).astype(BF)
            dlb_scr[...] += jnp.sum((1.0 - sg) * dfk, axis=0, keepdims=True)

        for c in reversed(range(HG_SUB_BWD)):
            chunk(c)

        @pl.when(n == nc // HG_SUB_BWD - 1)
        def _():
            dl0 = dlb_scr[...] * lb * (1.0 - lb)
            dlg_ref[0:1, :] = dl0
            dlg_ref[1:2, :] = -dl0

    steps = nc // HG_SUB_BWD
    blk = HG_SUB_BWD * CH
    rev = lambda n: (steps - 1 - n, 0)
    return _hosted_call(
        body, comm, (h3, hf, logits, norm_g, o_pre, states, dy), name="hgrn_bwd", grid=(steps,),
        in_specs=[pl.BlockSpec((blk, 3 * D), rev),
                  pl.BlockSpec((blk, D), rev),
                  pl.BlockSpec((2, D), lambda n: (0, 0)),
                  pl.BlockSpec((1, D), lambda n: (0, 0)),
                  pl.BlockSpec((blk, D), rev),
                  pl.BlockSpec((HG_SUB_BWD, HG_HEADS, HG_K, HG_K), lambda n: (steps - 1 - n, 0, 0, 0)),
                  pl.BlockSpec((blk, D), rev)],
        out_specs=[pl.BlockSpec((blk, 4 * D), rev),
                   pl.BlockSpec((2, D), lambda n: (0, 0)),
                   pl.BlockSpec((1, D), lambda n: (0, 0))],
        out_shape=[jax.ShapeDtypeStruct((t, 4 * D), BF), jax.ShapeDtypeStruct((2, D), F32),
                   jax.ShapeDtypeStruct((1, D), F32)],
        scratch_shapes=([pltpu.VMEM((HG_HEADS, HG_K, HG_K), F32), pltpu.VMEM((1, D), F32),
                         pltpu.VMEM((CH, D), F32), pltpu.VMEM((1, D), F32)]
                        + [pltpu.VMEM((CH, D), F32)] * 10 + [pltpu.VMEM((CH, D), BF)] * 6),
        sem=("arbitrary",), nsteps=steps, step_fn=lambda: pl.program_id(0))


def _place():
    x, y, c = lax.axis_index("x"), lax.axis_index("y"), lax.axis_index("c")
    return x, y, c, [(1 - x, y), (x, 1 - y), (1 - x, 1 - y)]


def _gather_comm(shards, mids):
    n, pieces = len(shards), len(mids)
    r = [s.shape[0] for s in shards]
    tile = 16
    cut = [[(rw // tile * p // pieces) * tile for p in range(pieces + 1)] for rw in r]
    size = [[cut[w][p + 1] - cut[w][p] for p in range(pieces)] for w in range(n)]

    def tools(ins, outs, sems):
        send_sems, recv_sems, local_sems = sems
        x, y, c, _ = _place()
        me, sib = (x, y, c), (x, y, 1 - c)
        near = [(x ^ c, y ^ (1 - c), c), (x ^ (1 - c), y ^ c, c), (1 - x, 1 - y, c)]

        def rows(w, p, dev):
            return outs[w].at[pl.ds((4 * dev[0] + 2 * dev[1] + dev[2]) * r[w] + cut[w][p], size[w][p]), :]

        def copy(kind, w, p, block, to, own=False):
            src = ins[w].at[pl.ds(cut[w][p], size[w][p]), :] if own else rows(w, p, block)
            return pltpu.make_async_remote_copy(
                src_ref=src, dst_ref=rows(w, p, block), send_sem=send_sems.at[p, kind],
                recv_sem=recv_sems.at[p, kind], device_id=to, device_id_type=MESH)

        def all_of(kind, p):
            whole = outs[0].at[pl.ds(0, sum(size[w][p] for w in range(n))), :]
            return pltpu.make_async_remote_copy(
                src_ref=whole, dst_ref=whole, send_sem=send_sems.at[p, kind], recv_sem=recv_sems.at[p, kind],
                device_id=me, device_id_type=MESH)

        mine = [pltpu.make_async_copy(ins[w], outs[w].at[pl.ds((4 * x + 2 * y + c) * r[w], r[w]), :],
                                      local_sems.at[w]) for w in range(n)]
        return near, me, sib, copy, all_of, mine

    def start(ins, outs, sems):
        near, me, sib, copy, _, mine = tools(ins, outs, sems)
        for cp in mine:
            cp.start()
        for p in range(pieces):
            for w in range(n):
                copy(0, w, p, me, sib, own=True).start()
                copy(1, w, p, me, near[0], own=True).start()
                copy(2, w, p, me, near[1], own=True).start()

    def pass_diagonal(p, near, sib, copy, all_of):
        all_of(3, p).wait_recv()
        for w in range(n):
            copy(6, w, p, near[2], sib).start()

    def pass_on(p):
        def phase(ins, outs, sems):
            near, _, sib, copy, all_of, _ = tools(ins, outs, sems)
            all_of(1, p).wait_recv()
            for w in range(n):
                copy(3, w, p, near[0], near[1]).start()
                copy(4, w, p, near[0], sib).start()
            all_of(2, p).wait_recv()
            for w in range(n):
                copy(5, w, p, near[1], sib).start()
            if p > 0:
                pass_diagonal(p - 1, near, sib, copy, all_of)
        return phase

    def finish(ins, outs, sems):
        near, _, sib, copy, all_of, mine = tools(ins, outs, sems)
        pass_diagonal(pieces - 1, near, sib, copy, all_of)
        for p in range(pieces):
            all_of(0, p).wait_recv()
            for kind in (4, 5, 6):
                all_of(kind, p).wait_recv()
            for kind in range(7):
                all_of(kind, p).wait_send()
        for cp in mine:
            cp.wait()

    return _Comm(shards, [jax.ShapeDtypeStruct((N_DEV * rw, D), BF) for rw in r],
                 [pltpu.SemaphoreType.DMA((pieces, 7)), pltpu.SemaphoreType.DMA((pieces, 7)),
                  pltpu.SemaphoreType.DMA((n,))],
                 [(0.0, start)] + [(f, pass_on(p)) for p, f in enumerate(mids)] + [(1.0, finish)])


def _pair_comm(grads):
    n = len(grads)
    r = [g.shape[0] // N_DEV for g in grads]

    def start(ins, outs, sems):
        send_sems, recv_sems = sems
        x, y, c, _ = _place()
        for w in range(n):
            for a in range(N_CHIP):
                pltpu.make_async_remote_copy(
                    src_ref=ins[w].at[pl.ds((2 * a + 1 - c) * r[w], r[w]), :], dst_ref=outs[w].at[a],
                    send_sem=send_sems.at[w], recv_sem=recv_sems.at[w],
                    device_id=(x, y, 1 - c), device_id_type=MESH).start()

    def finish(ins, outs, sems):
        send_sems, recv_sems = sems
        x, y, c, _ = _place()
        for w in range(n):
            pltpu.make_async_remote_copy(
                src_ref=outs[w], dst_ref=outs[w], send_sem=send_sems.at[w], recv_sem=recv_sems.at[w],
                device_id=(x, y, c), device_id_type=MESH).wait()

    return _Comm(grads, [jax.ShapeDtypeStruct((N_CHIP, rw, D), BF) for rw in r],
                 [pltpu.SemaphoreType.DMA((n,)), pltpu.SemaphoreType.DMA((n,))],
                 [(0.0, start), (1.0, finish)])


def _pair_add(grads, gots, core, *, name):
    n, r = len(grads), gots[0].shape[1]
    tr = r if r <= 128 else r // 2
    steps = r // tr
    tile = lambda k: (lambda s: jnp.clip(s - k * steps, 0, steps - 1))

    def body(c_ref, *refs):
        g_refs, got_refs, o_refs = refs[:n], refs[n:2 * n], refs[2 * n:]
        s = pl.program_id(0)
        for k in range(n):
            @pl.when(jnp.logical_and(s >= k * steps, s < (k + 1) * steps))
            def _(k=k):
                o_refs[k][...] = (g_refs[k][:, 0].astype(F32) + got_refs[k][...].astype(F32)).astype(BF)

    grid_spec = pltpu.PrefetchScalarGridSpec(
        num_scalar_prefetch=1, grid=(n * steps,),
        in_specs=[pl.BlockSpec((N_CHIP, 1, tr, D), lambda s, c_ref, k=k: (0, c_ref[0], tile(k)(s), 0))
                  for k in range(n)]
        + [pl.BlockSpec((N_CHIP, tr, D), lambda s, c_ref, k=k: (0, tile(k)(s), 0)) for k in range(n)],
        out_specs=[pl.BlockSpec((N_CHIP, tr, D), lambda s, c_ref, k=k: (0, tile(k)(s), 0)) for k in range(n)])
    return _pcall(body, name=name, grid_spec=grid_spec,
                  out_shape=[jax.ShapeDtypeStruct((N_CHIP, r, D), BF)] * n,
                  compiler_params=_cp(("arbitrary",)))(
                      core, *[g.reshape(N_CHIP, 2, r, D) for g in grads], *gots)


def _chip_comm(pair_sums):
    n = len(pair_sums)
    r = [p.shape[1] for p in pair_sums]
    off = [sum(r[:w]) for w in range(n)]

    def tools(ins, outs, sems):
        send_sems, recv_sems, local_sems = sems
        x, y, c, chips = _place()
        my_chip = 2 * x + y

        def slot(w):
            return outs[0].at[my_chip, pl.ds(off[w], r[w]), :]

        own = [pltpu.make_async_copy(ins[w].at[my_chip], slot(w), local_sems.at[w]) for w in range(n)]
        return x, y, c, chips, my_chip, slot, own, send_sems, recv_sems

    def start(ins, outs, sems):
        x, y, c, chips, my_chip, slot, own, send_sems, recv_sems = tools(ins, outs, sems)
        for cp in own:
            cp.start()
        for j, chip in enumerate(chips):
            for w in range(n):
                pltpu.make_async_remote_copy(
                    src_ref=ins[w].at[2 * chip[0] + chip[1]], dst_ref=slot(w), send_sem=send_sems.at[j],
                    recv_sem=recv_sems.at[j], device_id=(*chip, c), device_id_type=MESH).start()

    def finish(ins, outs, sems):
        x, y, c, chips, my_chip, slot, own, send_sems, recv_sems = tools(ins, outs, sems)
        whole = outs[0].at[my_chip]
        for j in range(3):
            pltpu.make_async_remote_copy(
                src_ref=whole, dst_ref=whole, send_sem=send_sems.at[j], recv_sem=recv_sems.at[j],
                device_id=(x, y, c), device_id_type=MESH).wait()
        for cp in own:
            cp.wait()

    return _Comm(pair_sums, [jax.ShapeDtypeStruct((N_CHIP, sum(r), D), BF)],
                 [pltpu.SemaphoreType.DMA((3,)), pltpu.SemaphoreType.DMA((3,)), pltpu.SemaphoreType.DMA((n,))],
                 [(0.0, start), (1.0, finish)])


def _adam_math(w, g, m, v):
    m = ADAM_B1 * m + (1.0 - ADAM_B1) * g
    v = ADAM_B2 * v + (1.0 - ADAM_B2) * (g * g)
    m_hat = m / (1.0 - ADAM_B1 ** ADAM_STEP)
    v_hat = v / (1.0 - ADAM_B2 ** ADAM_STEP)
    delta = -ADAM_LR * (m_hat / (jnp.sqrt(v_hat) + ADAM_EPS) + ADAM_WD * w)
    return delta, m, v


SMALL = (("norm_mix_g", (1, D), 0), ("hgrn_norm_g", (1, D), 1), ("norm_ffn_g", (1, D), 2),
         ("norm_final_g", (1, D), 3), ("hgrn_lb_logits", (2, D), 4), ("attn_sinks", (1, 16), 6),
         ("b_in", (1, IN_W), 8))
LOSS_ROW = 7


def _small_allreduce_adam(grads, loss_row, params):
    n = len(SMALL)

    def rows_of(ref, shape, row):
        r, w = shape
        if w <= D:
            return ref[row:row + r, 0:w]
        pieces = [ref[row + k:row + k + 1, :] for k in range(-(-w // D))]
        return jnp.concatenate(pieces, axis=1)[:, 0:w]

    def body(*refs):
        g_refs, loss_ref = refs[:n], refs[n]
        wmv = refs[n + 1:4 * n + 1]
        loss_out = refs[4 * n + 1]
        outs = refs[4 * n + 2:8 * n + 2]
        mine, total, gath, send_sems, recv_sems = refs[8 * n + 2:]
        x, y, c, _ = _place()
        me = 4 * x + 2 * y + c
        mine[...] = jnp.zeros_like(mine)
        for g_ref, (_, (r, w), row) in zip(g_refs, SMALL):
            for k in range(-(-w // D)):
                wk = min(D, w - k * D)
                mine[row + k:row + k + r, 0:wk] = g_ref[:, k * D:k * D + wk]
        mine[LOSS_ROW:LOSS_ROW + 1, 0:128] = loss_ref[...]
        gath[me] = mine[...]
        cps = []
        for d in range(1, N_DEV):
            peer = (x ^ (d >> 2), y ^ ((d >> 1) & 1), c ^ (d & 1))
            cps.append(pltpu.make_async_remote_copy(
                src_ref=mine, dst_ref=gath.at[me], send_sem=send_sems.at[d - 1],
                recv_sem=recv_sems.at[d - 1], device_id=peer, device_id_type=MESH))
        for cp in cps:
            cp.start()
        for cp in cps:
            cp.wait()
        g = gath[0]
        for k in range(1, N_DEV):
            g = g + gath[k]
        total[...] = g
        loss_out[...] = total[LOSS_ROW:LOSS_ROW + 1, 0:128]
        for i, (_, shape, row) in enumerate(SMALL):
            gi = rows_of(total, shape, row)
            w_ref, m_ref, v_ref = wmv[3 * i:3 * i + 3]
            o = outs[4 * i:4 * i + 4]
            o[0][...] = gi
            o[1][...], o[2][...], o[3][...] = _adam_math(w_ref[...], gi, m_ref[...], v_ref[...])

    vm = pl.BlockSpec(memory_space=pltpu.VMEM)
    ins = [grads[name] for name, _, _ in SMALL] + [loss_row]
    for name, _, _ in SMALL:
        ins += list(params[name])
    out_shape = [jax.ShapeDtypeStruct((1, 128), F32)]
    for _, shape, _ in SMALL:
        out_shape += [jax.ShapeDtypeStruct(shape, F32)] * 4
    res = _pcall(body, name="small_allreduce_adam", in_specs=[vm] * len(ins), out_specs=[vm] * len(out_shape),
                 out_shape=out_shape,
                 scratch_shapes=[pltpu.VMEM((SMALL_ROWS, D), F32), pltpu.VMEM((SMALL_ROWS, D), F32),
                                 pltpu.VMEM((N_DEV, SMALL_ROWS, D), F32),
                                 pltpu.SemaphoreType.DMA((N_DEV - 1,)), pltpu.SemaphoreType.DMA((N_DEV - 1,))],
                 compiler_params=pltpu.CompilerParams(has_side_effects=True))(*ins)
    return res[0], {name: res[1 + 4 * i:5 + 4 * i] for i, (name, _, _) in enumerate(SMALL)}


def _adam(ws, parts, ms, vs, *, name):
    n, rows = len(ws), ws[0].shape[0]
    tr = rows if rows <= 128 else rows // 2
    steps = rows // tr
    tile = lambda k: (lambda s: jnp.clip(s - k * steps, 0, steps - 1))

    def body(*refs):
        w_refs, m_refs, v_refs, p_ref = refs[:n], refs[n:2 * n], refs[2 * n:3 * n], refs[3 * n]
        o_refs = refs[3 * n + 1:]
        s = pl.program_id(0)
        for k in range(n):
            @pl.when(jnp.logical_and(s >= k * steps, s < (k + 1) * steps))
            def _(k=k):
                g = p_ref[0].astype(F32)
                for a in range(1, N_CHIP):
                    g = g + p_ref[a].astype(F32)
                o = o_refs[4 * k:4 * k + 4]
                o[0][...] = g
                o[1][...], o[2][...], o[3][...] = _adam_math(w_refs[k][...], g, m_refs[k][...], v_refs[k][...])

    spec = lambda k: pl.BlockSpec((tr, D), lambda s, k=k: (tile(k)(s), 0))
    res = _pcall(body, name=name, grid=(n * steps,),
                 in_specs=[spec(k) for k in range(n)] * 3 + [pl.BlockSpec((N_CHIP, tr, D), lambda s: (0, s, 0))],
                 out_specs=[spec(k) for k in range(n) for _ in range(4)],
                 out_shape=[jax.ShapeDtypeStruct((rows, D), F32)] * (4 * n),
                 compiler_params=_cp(("arbitrary",)))(*ws, *ms, *vs, parts)
    return [res[4 * k:4 * k + 4] for k in range(n)]


def _step(x, tgt, shards, norm_mix_g, b_in, sinks, logits, hgrn_norm_g, norm_ffn_g, norm_final_g):
    t = x.shape[0]
    core = lax.axis_index("c").astype(jnp.int32).reshape(1)

    u1, (win_t,) = _rms_fwd(x, norm_mix_g, tm=512, name="rms_mix", comm=_gather_comm(shards[0:1], (0.2, 0.4, 0.6, 0.8)))
    (q, kv, h3, hf, gates), (wg_t, wba, wbh, wout) = _inproj_fwd(
        u1, win_t, b_in, t=t, comm=_gather_comm([shards[1]] + shards[4:7], (0.3, 0.5, 0.7, 0.9)))
    (y_attn,), _ = _attn_fwd(q, kv, sinks, t=t)
    (y_hgrn, o_pre, states), (wu_t, wd) = _hgrn_fwd(h3, hf, logits, hgrn_norm_g, t=t,
                                                    comm=_gather_comm(shards[2:4], (0.3, 0.5, 0.7, 0.9)))
    col = lambda j: j
    first, second = (lambda j: 0), (lambda j: 1)
    gate_tiles = [(gates, D, first), (gates, D, second)]

    def merge(prods, ex):
        (ya_, yb_), (ga, gb) = prods, ex
        sa, sb = _sig(ga.astype(F32)), _sig(gb.astype(F32))
        return sa, sb, ya_ * sa * (1.0 - sa), yb_ * sb * (1.0 - sb), sa * ya_ + sb * yb_

    sig_a, sig_b, dgate_a, dgate_b, merged = _fmm(
        [y_attn, y_hgrn], [(0, wba, False), (1, wbh, False)], gate_tiles, merge,
        [(BF, D, D, first)] * 5, m=t, n=D, tm=512, tn=D, name="branch_merge")
    def resid_norm(prods, ex):
        (p,), (xv, gv) = prods, ex
        hv = xv + p
        return hv, hv * lax.rsqrt(jnp.mean(hv * hv, axis=-1, keepdims=True) + EPS) * gv

    h1, u2 = _fmm([merged], [(0, wout, False)], [(x, D, first)], resid_norm, [(F32, D, D, first), (BF, D, D, first)],
                  m=t, n=D, tm=1024, tn=D, name="out_proj", vecs=[norm_ffn_g])

    def swiglu(prods, ex):
        g_, u_ = prods
        s = _sig(g_)
        silu = g_ * s
        return u_ * s * (1.0 + g_ * (1.0 - s)), silu, silu * u_

    dz_dgate, dz_dup, z = _fmm([u2], [(0, wg_t, True), (0, wu_t, True)], [], swiglu,
                               [(BF, FFN, FFN // 2, col)] * 3, m=t, n=FFN, tm=1024, tn=FFN // 2,
                               name="ffn_gate_up", cols_outer=True)
    def loss_head(prods, ex):
        (p,), (hv, tv, gv) = prods, ex
        hv = hv + p
        r = lax.rsqrt(jnp.mean(hv * hv, axis=-1, keepdims=True) + EPS)
        xh = hv * r
        err = xh * gv - tv
        lp = jnp.sum(jnp.sum(err * err, axis=1, keepdims=True), axis=0, keepdims=True) * (0.5 / D)
        dy = err * (1.0 / D)
        dxh = dy * gv
        dh = r * (dxh - xh * jnp.mean(dxh * xh, axis=-1, keepdims=True))
        return dh, dh, jnp.sum(dy * xh, axis=0, keepdims=True), jnp.broadcast_to(lp, (1, 128))

    dh2, dh2_b, d_norm_final, loss_row = _fmm(
        [z], [(0, wd, False)], [(h1, D, first), (tgt, D, first)], loss_head, [(F32, D, D, first), (BF, D, D, first)],
        m=t, n=D, tm=512, tn=D, name="ffn_down_loss", vecs=[norm_final_g], sums=[D, 128])

    def swiglu_bwd(prods, ex):
        (dz,), (da_, db_) = prods, ex
        return dz * da_.astype(F32), dz * db_.astype(F32)

    ffn_tiles = [(dz_dgate, FFN // 2, col), (dz_dup, FFN // 2, col)]
    dgt, dup = _fmm([dh2_b], [(0, wd, True)], ffn_tiles, swiglu_bwd, [(BF, FFN, FFN // 2, col)] * 2,
                    m=t, n=FFN, tm=1024, tn=FFN // 2, name="d_gate_up", cols_outer=True)
    (d_wd,) = _wgrad([z], dh2_b, name="d_w_down")
    (du2,) = _fmm([dgt, dup], [(0, wg_t, False), (1, wu_t, False)], [], lambda prods, ex: (prods[0] + prods[1],),
                  [(F32, D, 512, col)], m=t, n=D, tm=1024, tn=512, name="d_u2")
    d_wg, d_wu = _wgrad([dgt, dup], u2, name="d_w_gate_up")
    dh1, dh1_b, d_norm_ffn = _rms_bwd(du2, h1, norm_ffn_g, dh2, tm=512, name="rms_ffn_bwd")
    (d_wout,) = _wgrad([merged], dh1_b, name="d_w_out")

    def merge_bwd(prods, ex):
        (dm,), (sa, sb, ca, cb, wa, wb) = prods, ex
        dgate = jnp.concatenate([dm * ca.astype(F32), dm * cb.astype(F32)], axis=1)
        dya_ = (dm * sa.astype(F32)).astype(BF)
        dyb_ = (dm * sb.astype(F32)).astype(BF)
        return (dya_, dyb_, dgate, lax.dot_general(dya_, wa, _NT, preferred_element_type=F32),
                lax.dot_general(dyb_, wb, _NT, preferred_element_type=F32))

    ffn_grads = (d_wg, d_wu, d_wd)
    (dya, dyb, dgates, dy_attn, dy_hgrn), got = _fmm(
        [dh1_b], [(0, wout, True)], [(a, D, first) for a in (sig_a, sig_b, dgate_a, dgate_b)], merge_bwd,
        [(BF, D, D, first), (BF, D, D, first), (BF, 2 * D, 2 * D, first), (BF, D, D, first), (F32, D, D, first)],
        m=t, n=D, tm=512, tn=D, name="d_merge", consts=[wba, wbh], comm=_pair_comm(ffn_grads))
    pair_ffn = _pair_add(ffn_grads, got, core, name="pair_add_ffn")
    (d_wba,) = _wgrad([y_attn], dya, name="d_w_ba")
    (d_wbh,) = _wgrad([y_hgrn], dyb, name="d_w_bh")
    sq_grads = (d_wba, d_wbh, d_wout)
    (dh4, d_logits, d_hgrn_norm), (parts_ffn, *got) = _hgrn_bwd(
        h3, hf, logits, hgrn_norm_g, o_pre, states, dy_hgrn, t=t,
        comm=_both(_chip_comm(pair_ffn), _pair_comm(sq_grads)))
    pair_sq = _pair_add(sq_grads, got, core, name="pair_add_sq")
    (dq, dkv, d_sinks), (parts_sq,) = _attn_bwd(q, kv, sinks, dy_attn, t=t, comm=_chip_comm(pair_sq))
    dps = (dq, dkv, dh4, dgates)
    d_win_t, d_b_in = _inproj_bwd_w(dps, u1, t=t)
    half0, got_in = _inproj_bwd_x(dps, win_t, x, norm_mix_g, dh1, t=t, part=0, comm=_pair_comm([d_win_t]))
    pair_in = _pair_add([d_win_t], got_in, core, name="pair_add_w_in")
    (grad_x, d_norm_mix), (parts_in,) = _inproj_bwd_x(dps, win_t, x, norm_mix_g, dh1, t=t, part=1, prev=half0,
                                                      comm=_chip_comm(pair_in))

    small_grads = (d_norm_mix, d_b_in, d_sinks, d_logits, d_hgrn_norm, d_norm_ffn, d_norm_final)
    return loss_row, grad_x, (parts_in, parts_ffn, parts_sq), small_grads


def kernel(x, norm_mix_g, w_in, b_in, attn_sinks, hgrn_lb_logits, hgrn_norm_g, w_branch_attn, w_branch_hgrn, w_out, norm_ffn_g, w_ffn_gate, w_ffn_up, w_ffn_down, norm_final_g, loss_target, m_norm_mix_g, m_w_in, m_b_in, m_attn_sinks, m_hgrn_lb_logits, m_hgrn_norm_g, m_w_branch_attn, m_w_branch_hgrn, m_w_out, m_norm_ffn_g, m_w_ffn_gate, m_w_ffn_up, m_w_ffn_down, m_norm_final_g, v_norm_mix_g, v_w_in, v_b_in, v_attn_sinks, v_hgrn_lb_logits, v_hgrn_norm_g, v_w_branch_attn, v_w_branch_hgrn, v_w_out, v_norm_ffn_g, v_w_ffn_gate, v_w_ffn_up, v_w_ffn_down, v_norm_final_g):
    shards = [w_in[0].T.astype(BF), w_ffn_gate[0].T.astype(BF), w_ffn_up[0].T.astype(BF),
              w_ffn_down[0].astype(BF), w_branch_attn[0].astype(BF), w_branch_hgrn[0].astype(BF),
              w_out[0].astype(BF)]
    loss_row, grad_x, grad_parts, small_grads = _step(
        x[0], loss_target[0], shards, norm_mix_g, b_in, attn_sinks, hgrn_lb_logits, hgrn_norm_g,
        norm_ffn_g, norm_final_g.reshape(1, D))

    d_norm_mix, d_b_in, d_sinks, d_logits, d_hgrn_norm, d_norm_ffn, d_norm_final = small_grads
    row = lambda a: a.reshape(1, D)
    loss_out, small = _small_allreduce_adam(
        dict(norm_mix_g=d_norm_mix, hgrn_norm_g=d_hgrn_norm, norm_ffn_g=d_norm_ffn, norm_final_g=d_norm_final,
             hgrn_lb_logits=d_logits, attn_sinks=d_sinks, b_in=d_b_in),
        loss_row,
        dict(norm_mix_g=(norm_mix_g, m_norm_mix_g, v_norm_mix_g), hgrn_norm_g=(hgrn_norm_g, m_hgrn_norm_g, v_hgrn_norm_g),
             norm_ffn_g=(norm_ffn_g, m_norm_ffn_g, v_norm_ffn_g),
             norm_final_g=(row(norm_final_g), row(m_norm_final_g), row(v_norm_final_g)),
             hgrn_lb_logits=(hgrn_lb_logits, m_hgrn_lb_logits, v_hgrn_lb_logits),
             attn_sinks=(attn_sinks, m_attn_sinks, v_attn_sinks), b_in=(b_in, m_b_in, v_b_in)))
    small["norm_final_g"] = [a.reshape(D) for a in small["norm_final_g"]]
    loss = loss_out[0, 0]

    names = ["w_in", "w_ffn_gate", "w_ffn_up", "w_ffn_down", "w_branch_attn", "w_branch_hgrn", "w_out"]
    w_full = dict(w_in=(w_in, m_w_in, v_w_in), w_ffn_gate=(w_ffn_gate, m_w_ffn_gate, v_w_ffn_gate),
                  w_ffn_up=(w_ffn_up, m_w_ffn_up, v_w_ffn_up), w_ffn_down=(w_ffn_down, m_w_ffn_down, v_w_ffn_down),
                  w_branch_attn=(w_branch_attn, m_w_branch_attn, v_w_branch_attn),
                  w_branch_hgrn=(w_branch_hgrn, m_w_branch_hgrn, v_w_branch_hgrn),
                  w_out=(w_out, m_w_out, v_w_out))
    big = {}
    for group, parts, tag in zip((names[0:1], names[1:4], names[4:7]), grad_parts, ("w_in", "ffn", "square")):
        flip = [name in names[0:3] for name in group]
        view = lambda a, f: a[0].T if f else a[0]
        cols = [[view(w_full[name][j], f) for name, f in zip(group, flip)] for j in range(3)]
        res = _adam(cols[0], parts, cols[1], cols[2], name="adam_" + tag)
        for name, f, r in zip(group, flip, res):
            big[name] = [a.T[None] if f else a[None] for a in r]

    order = ["norm_mix_g", "w_in", "b_in", "attn_sinks", "hgrn_lb_logits", "hgrn_norm_g", "w_branch_attn",
             "w_branch_hgrn", "w_out", "norm_ffn_g", "w_ffn_gate", "w_ffn_up", "w_ffn_down", "norm_final_g"]
    outs = [loss, grad_x[None]]
    for kind in range(4):
        for name in order:
            outs.append(big[name][kind] if name in big else small[name][kind])
    return tuple(outs)
```

```python
import math

import jax
import jax.numpy as jnp
from jax import lax
from jax.experimental import pallas as pl
from jax.experimental.pallas import tpu as pltpu

F32 = jnp.float32
BF = jnp.bfloat16
MESH = pl.DeviceIdType.MESH

D = 1024
HEAD = 64
N_PAIR = 8
BLK = 128
CH = 64
HG_SUB = 4
HG_SUB_BWD = 4
HG_HEADS = 8
HG_K = 128
FFN = 2816
IN_W = 7424
N_DEV = 8
N_CHIP = 4
EPS = 1e-6
NEG = -1e30
SCALE = 1.0 / math.sqrt(HEAD)
VMEM_LIMIT = 56 * 1024 * 1024
WT = 256

ADAM_LR, ADAM_B1, ADAM_B2, ADAM_EPS, ADAM_WD, ADAM_STEP = 0.001, 0.9, 0.999, 1e-08, 0.01, 10

GRP_OFF = (0, D // WT, (D + 256) // WT, (5 * D + 256) // WT)
GRP_N = (D // WT, 256 // WT, 4 * D // WT, 2 * D // WT)
SMALL_ROWS = 16


_NN = (((1,), (0,)), ((), ()))
_NT = (((1,), (1,)), ((), ()))
_TN = (((0,), (0,)), ((), ()))


def _pcall(body, **kw):
    return pl.pallas_call(body, **kw)


def _cp(sem=None, **kw):
    return pltpu.CompilerParams(dimension_semantics=sem, vmem_limit_bytes=VMEM_LIMIT, **kw)


def _sig(v):
    return 0.5 * jnp.tanh(0.5 * v) + 0.5


def _accum(ref, val, first):
    @pl.when(first)
    def _():
        ref[...] = val

    @pl.when(jnp.logical_not(first))
    def _():
        ref[...] += val


class _Comm:
    def __init__(self, ins, out_shapes, sem_shapes, phases):
        self.ins, self.out_shapes, self.sem_shapes, self.phases = list(ins), list(out_shapes), list(sem_shapes), phases


def _both(a, b):
    ni, no, ns = len(a.ins), len(a.out_shapes), len(a.sem_shapes)

    def of_a(fn):
        return lambda ins, outs, sems: fn(ins[:ni], outs[:no], sems[:ns])

    def of_b(fn):
        return lambda ins, outs, sems: fn(ins[ni:], outs[no:], sems[ns:])

    return _Comm(a.ins + b.ins, a.out_shapes + b.out_shapes, a.sem_shapes + b.sem_shapes,
                 [(f, of_a(fn)) for f, fn in a.phases] + [(f, of_b(fn)) for f, fn in b.phases])


def _host(body, comm, n_in, n_out, n_scr, nsteps, step_fn):
    if comm is None:
        return body
    ci, co = len(comm.ins), len(comm.out_shapes)

    def wrapped(*refs):
        p = 0
        ins, p = refs[p:p + n_in], p + n_in
        cins, p = refs[p:p + ci], p + ci
        outs, p = refs[p:p + n_out], p + n_out
        couts, p = refs[p:p + co], p + co
        scr, p = refs[p:p + n_scr], p + n_scr
        csems = refs[p:]
        step = step_fn()
        for frac, fn in comm.phases:
            if frac < 1.0:
                @pl.when(step == int(round(frac * (nsteps - 1))))
                def _(fn=fn):
                    fn(cins, couts, csems)
        body(*ins, *outs, *scr)
        for frac, fn in comm.phases:
            if frac >= 1.0:
                @pl.when(step == nsteps - 1)
                def _(fn=fn):
                    fn(cins, couts, csems)

    return wrapped


def _hosted_call(body, comm, args, *, name, grid, in_specs, out_specs, out_shape, scratch_shapes, sem,
                 nsteps, step_fn, aliases=None):
    n_in, n_out, n_scr = len(in_specs), len(out_specs), len(scratch_shapes)
    args = list(args)
    extra = {}
    if comm is not None:
        in_specs = list(in_specs) + [_hbm_spec()] * len(comm.ins)
        out_specs = list(out_specs) + [_hbm_spec()] * len(comm.out_shapes)
        out_shape = list(out_shape) + comm.out_shapes
        scratch_shapes = list(scratch_shapes) + comm.sem_shapes
        args += comm.ins
        extra = dict(has_side_effects=True)
    outs = _pcall(_host(body, comm, n_in, n_out, n_scr, nsteps, step_fn), name=name, grid=grid,
                  in_specs=in_specs, out_specs=out_specs, out_shape=out_shape, scratch_shapes=scratch_shapes,
                  input_output_aliases=aliases or {}, compiler_params=_cp(sem, **extra))(*args)
    return list(outs[:n_out]), list(outs[n_out:])


def _hbm_spec():
    return pl.BlockSpec(memory_space=pl.ANY)


def _wgrad(a_list, b, *, name):
    (t, m), n, gm = a_list[0].shape, b.shape[1], a_list[0].shape[1] // WT
    n_a = len(a_list)
    tile = lambda k: (lambda s: jnp.clip(s - k * gm, 0, gm - 1))

    def body(*refs):
        a_refs, b_ref, o_refs = refs[:n_a], refs[n_a], refs[n_a + 1:]
        s = pl.program_id(0)
        for k in range(n_a):
            @pl.when(jnp.logical_and(s >= k * gm, s < (k + 1) * gm))
            def _(k=k):
                o_refs[k][...] = lax.dot_general(a_refs[k][...], b_ref[...], _TN,
                                                 preferred_element_type=F32).astype(BF)

    return _pcall(body, name=name, grid=(n_a * gm,),
                  in_specs=[pl.BlockSpec((t, WT), lambda s, k=k: (0, tile(k)(s))) for k in range(n_a)]
                  + [pl.BlockSpec((t, n), lambda s: (0, 0))],
                  out_specs=[pl.BlockSpec((WT, n), lambda s, k=k: (tile(k)(s), 0)) for k in range(n_a)],
                  out_shape=[jax.ShapeDtypeStruct((m, n), BF)] * n_a,
                  compiler_params=_cp(("arbitrary",)))(*a_list, b)


def _fmm(lhs, rhs, extras, epilogue, outs, *, m, n, tm, tn, name, comm=None, vecs=(), consts=(), sums=(),
         cols_outer=False):
    tm, tn = min(tm, m), min(tn, n)
    assert m % tm == 0 and n % tn == 0 and (not sums or (tn == n and not cols_outer)), (name, m, n, tm, tn)
    in_specs, args = [], []
    for a in lhs:
        in_specs.append(pl.BlockSpec((tm, a.shape[1]), lambda i, j: (i, 0)))
        args.append(a)
    for li, b, tb in rhs:
        k = lhs[li].shape[1]
        in_specs.append(pl.BlockSpec((tn, k), lambda i, j: (j, 0)) if tb
                        else pl.BlockSpec((k, tn), lambda i, j: (0, j)))
        args.append(b)
    for arr, w, col in extras:
        in_specs.append(pl.BlockSpec((tm, w), lambda i, j, col=col: (i, col(j))))
        args.append(arr)
    for vec in vecs:
        in_specs.append(pl.BlockSpec((1, tn), lambda i, j: (0, j)))
        args.append(vec)
    for whole in consts:
        in_specs.append(pl.BlockSpec(whole.shape, lambda i, j: (0, 0)))
        args.append(whole)
    out_specs = [pl.BlockSpec((tm, w), lambda i, j, col=col: (i, col(j))) for _, _, w, col in outs]
    out_shape = [jax.ShapeDtypeStruct((m, total), dt) for dt, total, _, _ in outs]
    for w in sums:
        out_specs.append(pl.BlockSpec((1, w), lambda i, j: (0, 0)))
        out_shape.append(jax.ShapeDtypeStruct((1, w), F32))
    nl, nr, ne, no = len(lhs), len(rhs), len(extras) + len(vecs) + len(consts), len(outs)

    def body(*refs):
        prods = []
        for r, (li, _, tb) in enumerate(rhs):
            prods.append(lax.dot_general(refs[li][...], refs[nl + r][...], _NT if tb else _NN,
                                         preferred_element_type=F32))
        vals = epilogue(prods, [ref[...] for ref in refs[nl + nr:nl + nr + ne]])
        o_refs = refs[nl + nr + ne:]
        for o_ref, v in zip(o_refs[:no], vals[:no]):
            o_ref[...] = v.astype(o_ref.dtype)
        for s_ref, v in zip(o_refs[no:], vals[no:]):
            _accum(s_ref, v, pl.program_id(0) == 0)

    grid = (m // tm, n // tn)
    if cols_outer:
        flip = lambda spec: pl.BlockSpec(spec.block_shape, lambda j, i, f=spec.index_map: f(i, j))
        in_specs, out_specs, grid = [flip(s) for s in in_specs], [flip(s) for s in out_specs], grid[::-1]
    res, comm_res = _hosted_call(
        body, comm, args, name=name, grid=grid, in_specs=in_specs, out_specs=out_specs,
        out_shape=out_shape, scratch_shapes=[], sem=("arbitrary", "arbitrary"), nsteps=grid[0] * grid[1],
        step_fn=lambda: pl.program_id(0) * grid[1] + pl.program_id(1))
    return res if comm is None else (res, comm_res)


def _grp_of(i):
    return [jnp.logical_and(i >= GRP_OFF[g], i < GRP_OFF[g] + GRP_N[g]) for g in range(4)]


def _grp_idx(i, g):
    return jnp.clip(i - GRP_OFF[g], 0, GRP_N[g] - 1)


def _inproj_fwd(u, win_t, b_in, *, t, comm=None):
    tm = min(1024, t)
    n_row = t // tm
    n_chunks, h_first, g_first = 8, 2, 6
    sub = D // WT

    def w_block(l):
        return jnp.where(l == 0, GRP_OFF[0], jnp.where(l == 1, GRP_OFF[1], GRP_OFF[2] + sub * (l - h_first)))

    def body(u_ref, *rest):
        w_refs, b_refs, (q_ref, kv_ref, h3_ref, hf_ref, g_ref) = rest[:sub], rest[sub:2 * sub], rest[2 * sub:]
        l = pl.program_id(1)

        @pl.when(l == 1)
        def _():
            kv_ref[...] = (lax.dot_general(u_ref[...], w_refs[0][...], _NT, preferred_element_type=F32)
                           + b_refs[0][...]).astype(BF)

        is_hf = l == h_first + 1
        in_h3 = jnp.logical_and(jnp.logical_and(l >= h_first, l < g_first), jnp.logical_not(is_hf))
        for pred, o_ref in ((l == 0, q_ref), (in_h3, h3_ref), (is_hf, hf_ref), (l >= g_first, g_ref)):
            @pl.when(pred)
            def _(o_ref=o_ref):
                w = jnp.concatenate([w[...] for w in w_refs], axis=0)
                b = jnp.concatenate([b[...] for b in b_refs], axis=1)
                o_ref[...] = (lax.dot_general(u_ref[...], w, _NT, preferred_element_type=F32) + b).astype(o_ref.dtype)

    return _hosted_call(
        body, comm, [u] + [win_t] * sub + [b_in] * sub, name="inproj_fwd", grid=(n_row, n_chunks),
        in_specs=[pl.BlockSpec((tm, D), lambda i, l: (i, 0))]
        + [pl.BlockSpec((WT, D), lambda i, l, o=o: (w_block(l) + o, 0)) for o in range(sub)]
        + [pl.BlockSpec((1, WT), lambda i, l, o=o: (0, w_block(l) + o)) for o in range(sub)],
        out_specs=[pl.BlockSpec((tm, D), lambda i, l: (i, 0)),
                   pl.BlockSpec((tm, 256), lambda i, l: (i, 0)),
                   pl.BlockSpec((tm, D), lambda i, l: (i, jnp.clip(l - h_first - 1, 0, 2))),
                   pl.BlockSpec((tm, D), lambda i, l: (i, 0)),
                   pl.BlockSpec((tm, D), lambda i, l: (i, jnp.clip(l - g_first, 0, 1)))],
        out_shape=[jax.ShapeDtypeStruct((t, D), BF), jax.ShapeDtypeStruct((t, 256), BF),
                   jax.ShapeDtypeStruct((t, 3 * D), BF), jax.ShapeDtypeStruct((t, D), F32),
                   jax.ShapeDtypeStruct((t, 2 * D), BF)],
        scratch_shapes=[], sem=("arbitrary", "arbitrary"), nsteps=n_row * n_chunks,
        step_fn=lambda: pl.program_id(0) * n_chunks + pl.program_id(1))


def _inproj_bwd_x(dps, win_t, x, g, resid, *, t, part, prev=None, comm=None):
    n_row = 8 if t >= 4096 else 4
    tm = t // n_row
    first = 1
    per = first if part == 0 else n_row - first
    row = lambda i: part * first + i

    n_chunks = 4
    sub = 2 * D // WT

    def w_block(l):
        return jnp.where(l == 0, 0, GRP_OFF[2] + sub * (l - 1))

    def body(d0, d1, d2, d3, *rest):
        w_refs, (x_ref, g_ref, r_ref) = rest[:sub], rest[sub:sub + 3]
        dg_prev = rest[sub + 3] if prev is not None else None
        o_ref, dg_ref, acc_ref = rest[-3], rest[-2], rest[-1]
        i, l = pl.program_id(0), pl.program_id(1)

        @pl.when(l == 0)
        def _():
            wq = jnp.concatenate([w[...] for w in w_refs[:GRP_N[0]]], axis=0)
            acc_ref[...] = (jnp.dot(d0[...], wq, preferred_element_type=F32)
                            + jnp.dot(d1[...], w_refs[GRP_N[0]][...], preferred_element_type=F32))

        for pred, d_ref in ((jnp.logical_and(l >= 1, l < 3), d2), (l == 3, d3)):
            @pl.when(pred)
            def _(d_ref=d_ref):
                w = jnp.concatenate([w[...] for w in w_refs], axis=0)
                acc_ref[...] += jnp.dot(d_ref[...], w, preferred_element_type=F32)

        @pl.when(l == n_chunks - 1)
        def _():
            xv = x_ref[...]
            r = lax.rsqrt(jnp.mean(xv * xv, axis=-1, keepdims=True) + EPS)
            xh = xv * r
            du = acc_ref[...]
            dxh = du * g_ref[...]
            o_ref[...] = r_ref[...] + r * (dxh - xh * jnp.mean(dxh * xh, axis=-1, keepdims=True))
            dg = jnp.sum(du * xh, axis=0, keepdims=True)
            if dg_prev is not None:
                dg = dg + jnp.where(i == 0, 1.0, 0.0) * dg_prev[...]
            _accum(dg_ref, dg, i == 0)

    rows = lambda w: pl.BlockSpec((tm, w), lambda i, l: (row(i), 0))
    in_specs = ([rows(D), rows(256),
                 pl.BlockSpec((tm, 2 * D), lambda i, l: (row(i), jnp.clip(l - 1, 0, 1))), rows(2 * D)]
                + [pl.BlockSpec((WT, D), lambda i, l, o=o: (w_block(l) + o, 0)) for o in range(sub)]
                + [rows(D), pl.BlockSpec((1, D), lambda i, l: (0, 0)), rows(D)])
    args = list(dps) + [win_t] * sub + [x, g, resid]
    aliases = None
    if prev is not None:
        in_specs += [pl.BlockSpec((1, D), lambda i, l: (0, 0)), _hbm_spec()]
        args += [prev[1], prev[0]]
        aliases = {len(args) - 1: 0}
    return _hosted_call(
        body, comm, args, name="inproj_bwd_x%d" % part, grid=(per, n_chunks), in_specs=in_specs,
        out_specs=[rows(D), pl.BlockSpec((1, D), lambda i, l: (0, 0))],
        out_shape=[jax.ShapeDtypeStruct((t, D), F32), jax.ShapeDtypeStruct((1, D), F32)],
        scratch_shapes=[pltpu.VMEM((tm, D), F32)], sem=("arbitrary", "arbitrary"), nsteps=per * n_chunks,
        step_fn=lambda: pl.program_id(0) * n_chunks + pl.program_id(1), aliases=aliases)


def _inproj_bwd_w(dps, u, *, t):
    n_tiles = IN_W // WT
    dims = (((0,), (0,)), ((), ()))

    def body(d0, d1, d2, d3, u_ref, o_ref, db_ref):
        i = pl.program_id(0)
        uv = u_ref[...]
        for g, (pred, d_ref) in enumerate(zip(_grp_of(i), (d0, d1, d2, d3))):
            @pl.when(pred)
            def _(d_ref=d_ref):
                dv = d_ref[...]
                o_ref[...] = lax.dot_general(dv, uv, dims, preferred_element_type=F32).astype(BF)
                db_ref[...] = jnp.sum(dv.astype(F32), axis=0, keepdims=True)

    return _pcall(body, name="inproj_bwd_w", grid=(n_tiles,),
                  in_specs=[pl.BlockSpec((t, WT), lambda i, g=g: (0, _grp_idx(i, g))) for g in range(4)]
                  + [pl.BlockSpec((t, D), lambda i: (0, 0))],
                  out_specs=[pl.BlockSpec((WT, D), lambda i: (i, 0)),
                             pl.BlockSpec((1, WT), lambda i: (0, i))],
                  out_shape=[jax.ShapeDtypeStruct((IN_W, D), BF), jax.ShapeDtypeStruct((1, IN_W), F32)],
                  compiler_params=_cp(("arbitrary",)))(*dps, u)


def _row_spec(tm, width, col=0):
    return pl.BlockSpec((tm, width), lambda i: (i, col))


def _vec_spec(width):
    return pl.BlockSpec((1, width), lambda i: (0, 0))


def _rms_fwd(x, g, *, tm, name, comm=None):
    t = x.shape[0]
    tm = min(tm, t)

    def body(x_ref, g_ref, u_ref):
        xv = x_ref[...]
        r = lax.rsqrt(jnp.mean(xv * xv, axis=-1, keepdims=True) + EPS)
        u_ref[...] = (xv * r * g_ref[...]).astype(BF)

    (u,), comm_res = _hosted_call(
        body, comm, (x, g), name=name, grid=(t // tm,), in_specs=[_row_spec(tm, D), _vec_spec(D)],
        out_specs=[_row_spec(tm, D)], out_shape=[jax.ShapeDtypeStruct((t, D), BF)], scratch_shapes=[],
        sem=("arbitrary",), nsteps=t // tm, step_fn=lambda: pl.program_id(0))
    return u if comm is None else (u, comm_res)


def _rms_bwd(du, x, g, resid, *, tm, name):
    t = x.shape[0]
    tm = min(tm, t)

    def body(du_ref, x_ref, g_ref, r_ref, dx_ref, dxb_ref, dg_ref):
        xv = x_ref[...]
        r = lax.rsqrt(jnp.mean(xv * xv, axis=-1, keepdims=True) + EPS)
        xh = xv * r
        duv = du_ref[...]
        dxh = duv * g_ref[...]
        dx = r_ref[...] + r * (dxh - xh * jnp.mean(dxh * xh, axis=-1, keepdims=True))
        dx_ref[...] = dx
        dxb_ref[...] = dx.astype(BF)
        _accum(dg_ref, jnp.sum(duv * xh, axis=0, keepdims=True), pl.program_id(0) == 0)

    return _pcall(body, name=name, grid=(t // tm,),
                  in_specs=[_row_spec(tm, D), _row_spec(tm, D), _vec_spec(D), _row_spec(tm, D)],
                  out_specs=[_row_spec(tm, D), _row_spec(tm, D), _vec_spec(D)],
                  out_shape=[jax.ShapeDtypeStruct((t, D), F32), jax.ShapeDtypeStruct((t, D), BF),
                             jax.ShapeDtypeStruct((1, D), F32)],
                  compiler_params=_cp(("arbitrary",)))(du, x, g, resid)


def _attn_kv_tiles(kprev, kcur):
    kv = jnp.concatenate([kprev, kcur], axis=0).astype(F32)
    lo = lax.broadcasted_iota(jnp.int32, (2 * BLK, 128), 1) < HEAD
    tiles = []
    for part in (kv[:, 0:128], kv[:, 128:256]):
        rolled = pltpu.roll(part, HEAD, 1)
        z = jnp.zeros_like(part)
        tiles.append(((jnp.where(lo, part, z).astype(BF), jnp.where(lo, z, rolled).astype(BF)),
                      (jnp.where(lo, rolled, z).astype(BF), jnp.where(lo, z, part).astype(BF))))
    k_t, v_t = tiles
    return [(jnp.concatenate(k_t[h], axis=0), jnp.concatenate(v_t[h], axis=0)) for h in range(2)]


def _attn_mask(i):
    qi = lax.broadcasted_iota(jnp.int32, (BLK, 2 * BLK), 0)
    kj = lax.broadcasted_iota(jnp.int32, (BLK, 2 * BLK), 1)
    first_key = jnp.where(i == 0, BLK, 0)
    in_prev = jnp.logical_and(jnp.logical_and(kj < BLK, kj > qi), kj >= first_key)
    in_cur = jnp.logical_and(kj >= BLK, kj - BLK <= qi)
    return jnp.logical_or(in_prev, in_cur)


def _attn_probs(s, sink, valid):
    s = jnp.where(valid, s * SCALE, NEG)
    mx = jnp.maximum(jnp.max(s, axis=-1, keepdims=True), sink)
    e = jnp.exp(s - mx)
    es = jnp.exp(sink - mx)
    inv = 1.0 / (jnp.sum(e, axis=-1, keepdims=True) + es)
    return e * inv, es * inv


_KEYS = 2 * BLK


def _pair(ref, j):
    return ref[:, j * 128:(j + 1) * 128]


def _attn_fwd(q, kv, sinks, *, t, comm=None):
    nb = t // BLK
    sub = 2

    def body(sink_ref, q_ref, kp_ref, kc_ref, o_ref):
        i = pl.program_id(0)
        for c in range(sub):
            rows = slice(c * BLK, (c + 1) * BLK)
            valid = _attn_mask(sub * i + c)
            tiles = _attn_kv_tiles(kp_ref[...] if c == 0 else kc_ref[(c - 1) * BLK:c * BLK, :], kc_ref[rows, :])
            s = [lax.dot_general(q_ref[rows, j * 128:(j + 1) * 128], tiles[j // 4][0], _NT,
                                 preferred_element_type=F32) for j in range(N_PAIR)]
            p = []
            for j in range(N_PAIR):
                pe, _ = _attn_probs(s[j][:, 0:_KEYS], sink_ref[0, 2 * j], valid)
                po, _ = _attn_probs(s[j][:, _KEYS:2 * _KEYS], sink_ref[0, 2 * j + 1], valid)
                p.append(jnp.concatenate([pe.astype(BF), po.astype(BF)], axis=1))
            for j in range(N_PAIR):
                o_ref[rows, j * 128:(j + 1) * 128] = jnp.dot(p[j], tiles[j // 4][1],
                                                             preferred_element_type=F32).astype(BF)

    return _hosted_call(
        body, comm, (sinks, q, kv, kv), name="attn_fwd", grid=(nb // sub,),
        in_specs=[pl.BlockSpec(memory_space=pltpu.SMEM),
                  pl.BlockSpec((sub * BLK, D), lambda i: (i, 0)),
                  pl.BlockSpec((BLK, 256), lambda i: (jnp.maximum(sub * i - 1, 0), 0)),
                  pl.BlockSpec((sub * BLK, 256), lambda i: (i, 0))],
        out_specs=[pl.BlockSpec((sub * BLK, D), lambda i: (i, 0))],
        out_shape=[jax.ShapeDtypeStruct((t, D), BF)],
        scratch_shapes=[], sem=("arbitrary",), nsteps=nb // sub, step_fn=lambda: pl.program_id(0))


def _attn_bwd(q, kv, sinks, do, *, t, comm=None):
    nb = t // BLK
    last = nb - 1

    def body(sink_ref, q_ref, kp_ref, kc_ref, do_ref, dq_ref, dkv_ref, ds_ref, carry_ref):
        i = pl.program_id(0)

        @pl.when(i == 0)
        def _():
            ds_ref[...] = jnp.zeros_like(ds_ref)
            carry_ref[...] = jnp.zeros_like(carry_ref)

        @pl.when(i < nb)
        def _():
            valid = _attn_mask(i)
            tiles = _attn_kv_tiles(kp_ref[...], kc_ref[...])
            lane1 = lax.broadcasted_iota(jnp.int32, (1, 128), 1)
            dsink = jnp.zeros((1, 128), F32)
            s = [lax.dot_general(_pair(q_ref, j), tiles[j // 4][0], _NT, preferred_element_type=F32)
                 for j in range(N_PAIR)]
            dp = [lax.dot_general(_pair(do_ref, j), tiles[j // 4][1], _NT, preferred_element_type=F32)
                  for j in range(N_PAIR)]
            p_all, ds_all = [], []
            for j in range(N_PAIR):
                halves = []
                for par in range(2):
                    cols = slice(par * _KEYS, (par + 1) * _KEYS)
                    p, ps = _attn_probs(s[j][:, cols], sink_ref[0, 2 * j + par], valid)
                    dpj = dp[j][:, cols]
                    dd = jnp.sum(p * dpj, axis=-1, keepdims=True)
                    dsink = dsink + jnp.where(lane1 == 2 * j + par,
                                              -jnp.sum(ps * dd, axis=0, keepdims=True), 0.0)
                    halves.append((p.astype(BF), (p * (dpj - dd)).astype(BF)))
                p_all.append(jnp.concatenate([halves[0][0], halves[1][0]], axis=1))
                ds_all.append(jnp.concatenate([halves[0][1], halves[1][1]], axis=1))
            for j in range(N_PAIR):
                dq_ref[:, j * 128:(j + 1) * 128] = (
                    jnp.dot(ds_all[j], tiles[j // 4][0], preferred_element_type=F32) * SCALE).astype(BF)
            ds_ref[...] += dsink
            gk, gv = [], []
            for h in range(2):
                grp = range(4 * h, 4 * h + 4)
                q_rows = jnp.concatenate([_pair(q_ref, j) for j in grp], axis=0)
                do_rows = jnp.concatenate([_pair(do_ref, j) for j in grp], axis=0)
                g_k = lax.dot_general(jnp.concatenate([ds_all[j] for j in grp], axis=0), q_rows, _TN,
                                      preferred_element_type=F32)
                g_v = lax.dot_general(jnp.concatenate([p_all[j] for j in grp], axis=0), do_rows, _TN,
                                      preferred_element_type=F32)
                gk.append((g_k[0:_KEYS], g_k[_KEYS:2 * _KEYS]))
                gv.append((g_v[0:_KEYS], g_v[_KEYS:2 * _KEYS]))
            lo = lax.broadcasted_iota(jnp.int32, (2 * BLK, 128), 1) < HEAD
            zero = jnp.zeros((2 * BLK, 128), F32)

            def unpad(g):
                return (jnp.where(lo, g[0][0] + pltpu.roll(g[0][1], HEAD, 1), zero)
                        + jnp.where(lo, zero, pltpu.roll(g[1][0], HEAD, 1) + g[1][1]))

            dk = unpad(gk) * SCALE
            dv = unpad(gv)
            dkv_ref[:, 0:128] = (carry_ref[:, 0:128] + dk[0:BLK]).astype(BF)
            dkv_ref[:, 128:256] = (carry_ref[:, 128:256] + dv[0:BLK]).astype(BF)
            carry_ref[:, 0:128] = dk[BLK:2 * BLK]
            carry_ref[:, 128:256] = dv[BLK:2 * BLK]

        @pl.when(i == nb)
        def _():
            dkv_ref[...] = carry_ref[...].astype(BF)

    return _hosted_call(
        body, comm, (sinks, q, kv, kv, do), name="attn_bwd", grid=(nb + 1,),
        in_specs=[pl.BlockSpec(memory_space=pltpu.SMEM),
                  pl.BlockSpec((BLK, D), lambda i: (jnp.minimum(i, last), 0)),
                  pl.BlockSpec((BLK, 256), lambda i: (jnp.clip(i - 1, 0, last), 0)),
                  pl.BlockSpec((BLK, 256), lambda i: (jnp.minimum(i, last), 0)),
                  pl.BlockSpec((BLK, D), lambda i: (jnp.minimum(i, last), 0))],
        out_specs=[pl.BlockSpec((BLK, D), lambda i: (jnp.minimum(i, last), 0)),
                   pl.BlockSpec((BLK, 256), lambda i: (jnp.maximum(i - 1, 0), 0)),
                   pl.BlockSpec((1, 128), lambda i: (0, 0))],
        out_shape=[jax.ShapeDtypeStruct((t, D), BF), jax.ShapeDtypeStruct((t, 256), BF),
                   jax.ShapeDtypeStruct((1, 128), F32)],
        scratch_shapes=[pltpu.VMEM((BLK, 256), F32)], sem=("arbitrary",), nsteps=nb + 1,
        step_fn=lambda: pl.program_id(0))


def _split3(v):
    h = v.astype(BF)
    r = v - h.astype(F32)
    m = r.astype(BF)
    lo = (r - m.astype(F32)).astype(BF)
    return jnp.concatenate([h, m, lo], axis=1)


def _apply01(mat, v):
    n = v.shape[1]
    r = jnp.dot(mat, _split3(v), preferred_element_type=F32)
    return r[:, 0:n] + r[:, n:2 * n] + r[:, 2 * n:3 * n]


def _hgrn_gates(hq, hf, lb):
    sq = _sig(hq)
    sg = _sig(hf)
    f = lb + (1.0 - lb) * sg
    return hq * sq, (1.0 - lb) * (1.0 - sg), jnp.log(f), sq, sg, f


def _tri(upper):
    r = lax.broadcasted_iota(jnp.int32, (CH, CH), 0)
    c = lax.broadcasted_iota(jnp.int32, (CH, CH), 1)
    return (c >= r) if upper else (c <= r)


def _lb_from_logits(lg_ref):
    return 1.0 / (1.0 + jnp.exp(lg_ref[1:2, :] - lg_ref[0:1, :]))


def _hgrn_fwd(h3, hf, logits, norm_g, *, t, comm=None):
    nc = t // CH
    nt_dims = (((1,), (1,)), ((), ()))
    tn_dims = (((0,), (0,)), ((), ()))

    def body(h_ref, hf_ref, lg_ref, ng_ref, y_ref, o_ref, st_ref, s_scr, b_scr, qa_s, ka_s, qb_s, kb_s, v_s):
        @pl.when(pl.program_id(0) == 0)
        def _():
            s_scr[...] = jnp.zeros_like(s_scr)

        heads = [slice(h * HG_K, (h + 1) * HG_K) for h in range(HG_HEADS)]
        causal = _tri(False)
        lb = _lb_from_logits(lg_ref)
        for c in range(HG_SUB):
            rows = slice(c * CH, (c + 1) * CH)
            q, k, g, _, _, _ = _hgrn_gates(h_ref[rows, 0:D].astype(F32), hf_ref[rows, :], lb)
            b_scr[...] = _apply01(jnp.where(causal, 1.0, 0.0).astype(BF), g)
            b = b_scr[...]
            b_mid = b_scr[CH // 2 - 1:CH // 2, :]
            b_last = b_scr[CH - 1:CH, :]
            qa_s[...] = (q * jnp.exp(b - b_mid)).astype(BF)
            ka_s[...] = (k * jnp.exp(b_mid - b)).astype(BF)
            qb_s[...] = (q * jnp.exp(b)).astype(BF)
            kb_s[...] = (k * jnp.exp(b_last - b)).astype(BF)
            v_s[...] = h_ref[rows, D:2 * D]
            dec = jnp.exp(b_last)
            st_ref[c] = s_scr[...].astype(BF)
            a = [jnp.where(causal, lax.dot_general(qa_s[:, sl], ka_s[:, sl], nt_dims, preferred_element_type=F32),
                           0.0).astype(BF) for sl in heads]
            for h, sl in enumerate(heads):
                o_ref[rows, sl] = (jnp.dot(a[h], v_s[:, sl], preferred_element_type=F32)
                                   + lax.dot_general(qb_s[:, sl], s_scr[h].astype(BF), nt_dims,
                                                     preferred_element_type=F32))
            for h, sl in enumerate(heads):
                s_scr[h] = dec[:, sl] * s_scr[h] + lax.dot_general(v_s[:, sl], kb_s[:, sl], tn_dims,
                                                                   preferred_element_type=F32)
            for h, sl in enumerate(heads):
                o = o_ref[rows, sl]
                on = o * lax.rsqrt(jnp.mean(o * o, axis=-1, keepdims=True) + EPS)
                gate = _sig(h_ref[rows, 2 * D + h * HG_K:2 * D + (h + 1) * HG_K].astype(F32))
                y_ref[rows, sl] = (on * ng_ref[:, sl] * gate).astype(BF)

    half = lambda: pltpu.VMEM((CH, D), BF)
    blk = HG_SUB * CH
    return _hosted_call(
        body, comm, (h3, hf, logits, norm_g), name="hgrn_fwd", grid=(nc // HG_SUB,),
        in_specs=[pl.BlockSpec((blk, 3 * D), lambda n: (n, 0)),
                  pl.BlockSpec((blk, D), lambda n: (n, 0)),
                  pl.BlockSpec((2, D), lambda n: (0, 0)),
                  pl.BlockSpec((1, D), lambda n: (0, 0))],
        out_specs=[pl.BlockSpec((blk, D), lambda n: (n, 0)),
                   pl.BlockSpec((blk, D), lambda n: (n, 0)),
                   pl.BlockSpec((HG_SUB, HG_HEADS, HG_K, HG_K), lambda n: (n, 0, 0, 0))],
        out_shape=[jax.ShapeDtypeStruct((t, D), BF), jax.ShapeDtypeStruct((t, D), F32),
                   jax.ShapeDtypeStruct((nc, HG_HEADS, HG_K, HG_K), BF)],
        scratch_shapes=[pltpu.VMEM((HG_HEADS, HG_K, HG_K), F32), pltpu.VMEM((CH, D), F32),
                        half(), half(), half(), half(), half()],
        sem=("arbitrary",), nsteps=nc // HG_SUB, step_fn=lambda: pl.program_id(0))


def _hgrn_bwd(h3, hf, logits, norm_g, o_pre, states, dy, *, t, comm=None):
    nc = t // CH
    nt_dims = (((1,), (1,)), ((), ()))
    tn_dims = (((0,), (0,)), ((), ()))

    def body(h_ref, hf_ref, lg_ref, ng_ref, o_ref, st_ref, dy_ref, dh_ref, dlg_ref, dng_ref, ds_scr, dlb_scr,
             b_scr, tail_s, e_qa, e_ka, e_qb, e_kb, q_s, k_s, dqa_s, dka_s, dqb_s, dkb_s,
             qa_s, ka_s, qb_s, kb_s, v_s, do_s):
        n = pl.program_id(0)

        @pl.when(n == 0)
        def _():
            ds_scr[...] = jnp.zeros_like(ds_scr)
            dlb_scr[...] = jnp.zeros_like(dlb_scr)
            dng_ref[...] = jnp.zeros_like(dng_ref)

        heads = [slice(h * HG_K, (h + 1) * HG_K) for h in range(HG_HEADS)]
        lb = _lb_from_logits(lg_ref)
        causal = _tri(False)

        def chunk(c):
            rows = slice(c * CH, (c + 1) * CH)
            hq = h_ref[rows, 0:D].astype(F32)
            q, k, g, sq, sg, f = _hgrn_gates(hq, hf_ref[rows, :], lb)
            b_scr[...] = _apply01(jnp.where(causal, 1.0, 0.0).astype(BF), g)
            b = b_scr[...]
            b_mid = b_scr[CH // 2 - 1:CH // 2, :]
            b_last = b_scr[CH - 1:CH, :]
            q_s[...] = q
            k_s[...] = k
            for e_ref, s_ref, base, expo in ((e_qa, qa_s, q, b - b_mid), (e_ka, ka_s, k, b_mid - b),
                                             (e_qb, qb_s, q, b), (e_kb, kb_s, k, b_last - b)):
                e = jnp.exp(expo)
                e_ref[...] = e
                s_ref[...] = (base * e).astype(BF)
            v_s[...] = h_ref[rows, D:2 * D]
            dec = jnp.exp(b_last)
            for h, sl in enumerate(heads):
                gcol = slice(3 * D + h * HG_K, 3 * D + (h + 1) * HG_K)
                ngh = ng_ref[:, sl]
                sgate = _sig(h_ref[rows, 2 * D + h * HG_K:2 * D + (h + 1) * HG_K].astype(F32))
                o = o_ref[rows, sl]
                r = lax.rsqrt(jnp.mean(o * o, axis=-1, keepdims=True) + EPS)
                on = o * r
                dyh = dy_ref[rows, sl]
                dh_ref[rows, gcol] = (dyh * on * ngh * sgate * (1.0 - sgate)).astype(BF)
                dng_ref[:, sl] += jnp.sum(dyh * on * sgate, axis=0, keepdims=True)
                don = dyh * ngh * sgate
                do_s[:, sl] = (r * (don - on * jnp.mean(don * on, axis=-1, keepdims=True))).astype(BF)
            a = [jnp.where(causal, lax.dot_general(qa_s[:, sl], ka_s[:, sl], nt_dims, preferred_element_type=F32),
                           0.0).astype(BF) for sl in heads]
            da = [jnp.where(causal, lax.dot_general(do_s[:, sl], v_s[:, sl], nt_dims, preferred_element_type=F32),
                            0.0).astype(BF) for sl in heads]
            for h, sl in enumerate(heads):
                dh_ref[rows, 2 * D + h * HG_K:2 * D + (h + 1) * HG_K] = (
                    lax.dot_general(a[h], do_s[:, sl], tn_dims, preferred_element_type=F32)
                    + lax.dot_general(kb_s[:, sl], ds_scr[h].astype(BF), nt_dims, preferred_element_type=F32)
                ).astype(BF)
            for h, sl in enumerate(heads):
                dqa_s[:, sl] = jnp.dot(da[h], ka_s[:, sl], preferred_element_type=F32)
            for h, sl in enumerate(heads):
                dka_s[:, sl] = lax.dot_general(da[h], qa_s[:, sl], tn_dims, preferred_element_type=F32)
            for h, sl in enumerate(heads):
                dqb_s[:, sl] = jnp.dot(do_s[:, sl], st_ref[c, h], preferred_element_type=F32)
            for h, sl in enumerate(heads):
                dkb_s[:, sl] = jnp.dot(v_s[:, sl], ds_scr[h].astype(BF), preferred_element_type=F32)
            for h, sl in enumerate(heads):
                tail_s[:, sl] = jnp.sum(dec[:, sl] * st_ref[c, h].astype(F32) * ds_scr[h], axis=0, keepdims=True)
            for h, sl in enumerate(heads):
                ds_scr[h] = (lax.dot_general(do_s[:, sl], qb_s[:, sl], tn_dims, preferred_element_type=F32)
                             + dec[:, sl] * ds_scr[h])
            qv, kv = q_s[...], k_s[...]
            dqa, dka, dqb, dkb = dqa_s[...], dka_s[...], dqb_s[...], dkb_s[...]
            eqa, eka, eqb, ekb = e_qa[...], e_ka[...], e_qb[...], e_kb[...]
            dkb_kb = dkb * (kv * ekb)
            db_last = jnp.sum(dkb_kb, axis=0, keepdims=True) + tail_s[...]
            last_row = lax.broadcasted_iota(jnp.int32, (CH, D), 0) == CH - 1
            db = (dqa * (qv * eqa) - dka * (kv * eka) + dqb * (qv * eqb) - dkb_kb
                  + jnp.where(last_row, db_last, 0.0))
            dg = _apply01(jnp.where(_tri(True), 1.0, 0.0).astype(BF), db)
            dq = dqa * eqa + dqb * eqb
            dk = dka * eka + dkb * ekb
            dh_ref[rows, 0:D] = (dq * sq * (1.0 + hq * (1.0 - sq))).astype(BF)
            dfk = dg / f - dk
            dh_ref[rows, D:2 * D] = ((1.0 - lb) * dfk * sg * (1.0 - sg)).astype(BF)
            dlb_scr[...] += jnp.sum((1.0 - sg) * dfk, axis=0, keepdims=True)

        for c in reversed(range(HG_SUB_BWD)):
            chunk(c)

        @pl.when(n == nc // HG_SUB_BWD - 1)
        def _():
            dl0 = dlb_scr[...] * lb * (1.0 - lb)
            dlg_ref[0:1, :] = dl0
            dlg_ref[1:2, :] = -dl0

    steps = nc // HG_SUB_BWD
    blk = HG_SUB_BWD * CH
    rev = lambda n: (steps - 1 - n, 0)
    return _hosted_call(
        body, comm, (h3, hf, logits, norm_g, o_pre, states, dy), name="hgrn_bwd", grid=(steps,),
        in_specs=[pl.BlockSpec((blk, 3 * D), rev),
                  pl.BlockSpec((blk, D), rev),
                  pl.BlockSpec((2, D), lambda n: (0, 0)),
                  pl.BlockSpec((1, D), lambda n: (0, 0)),
                  pl.BlockSpec((blk, D), rev),
                  pl.BlockSpec((HG_SUB_BWD, HG_HEADS, HG_K, HG_K), lambda n: (steps - 1 - n, 0, 0, 0)),
                  pl.BlockSpec((blk, D), rev)],
        out_specs=[pl.BlockSpec((blk, 4 * D), rev),
                   pl.BlockSpec((2, D), lambda n: (0, 0)),
                   pl.BlockSpec((1, D), lambda n: (0, 0))],
        out_shape=[jax.ShapeDtypeStruct((t, 4 * D), BF), jax.ShapeDtypeStruct((2, D), F32),
                   jax.ShapeDtypeStruct((1, D), F32)],
        scratch_shapes=([pltpu.VMEM((HG_HEADS, HG_K, HG_K), F32), pltpu.VMEM((1, D), F32),
                         pltpu.VMEM((CH, D), F32), pltpu.VMEM((1, D), F32)]
                        + [pltpu.VMEM((CH, D), F32)] * 10 + [pltpu.VMEM((CH, D), BF)] * 6),
        sem=("arbitrary",), nsteps=steps, step_fn=lambda: pl.program_id(0))


def _place():
    x, y, c = lax.axis_index("x"), lax.axis_index("y"), lax.axis_index("c")
    return x, y, c, [(1 - x, y), (x, 1 - y), (1 - x, 1 - y)]


def _gather_comm(shards, mids):
    n, pieces = len(shards), len(mids)
    r = [s.shape[0] for s in shards]
    tile = 16
    cut = [[(rw // tile * p // pieces) * tile for p in range(pieces + 1)] for rw in r]
    size = [[cut[w][p + 1] - cut[w][p] for p in range(pieces)] for w in range(n)]

    def tools(ins, outs, sems):
        send_sems, recv_sems, local_sems = sems
        x, y, c, _ = _place()
        me, sib = (x, y, c), (x, y, 1 - c)
        near = [(x ^ c, y ^ (1 - c), c), (x ^ (1 - c), y ^ c, c), (1 - x, 1 - y, c)]

        def rows(w, p, dev):
            return outs[w].at[pl.ds((4 * dev[0] + 2 * dev[1] + dev[2]) * r[w] + cut[w][p], size[w][p]), :]

        def copy(kind, w, p, block, to, own=False):
            src = ins[w].at[pl.ds(cut[w][p], size[w][p]), :] if own else rows(w, p, block)
            return pltpu.make_async_remote_copy(
                src_ref=src, dst_ref=rows(w, p, block), send_sem=send_sems.at[p, kind],
                recv_sem=recv_sems.at[p, kind], device_id=to, device_id_type=MESH)

        def all_of(kind, p):
            whole = outs[0].at[pl.ds(0, sum(size[w][p] for w in range(n))), :]
            return pltpu.make_async_remote_copy(
                src_ref=whole, dst_ref=whole, send_sem=send_sems.at[p, kind], recv_sem=recv_sems.at[p, kind],
                device_id=me, device_id_type=MESH)

        mine = [pltpu.make_async_copy(ins[w], outs[w].at[pl.ds((4 * x + 2 * y + c) * r[w], r[w]), :],
                                      local_sems.at[w]) for w in range(n)]
        return near, me, sib, copy, all_of, mine

    def start(ins, outs, sems):
        near, me, sib, copy, _, mine = tools(ins, outs, sems)
        for cp in mine:
            cp.start()
        for p in range(pieces):
            for w in range(n):
                copy(0, w, p, me, sib, own=True).start()
                copy(1, w, p, me, near[0], own=True).start()
                copy(2, w, p, me, near[1], own=True).start()

    def pass_diagonal(p, near, sib, copy, all_of):
        all_of(3, p).wait_recv()
        for w in range(n):
            copy(6, w, p, near[2], sib).start()

    def pass_on(p):
        def phase(ins, outs, sems):
            near, _, sib, copy, all_of, _ = tools(ins, outs, sems)
            all_of(1, p).wait_recv()
            for w in range(n):
                copy(3, w, p, near[0], near[1]).start()
                copy(4, w, p, near[0], sib).start()
            all_of(2, p).wait_recv()
            for w in range(n):
                copy(5, w, p, near[1], sib).start()
            if p > 0:
                pass_diagonal(p - 1, near, sib, copy, all_of)
        return phase

    def finish(ins, outs, sems):
        near, _, sib, copy, all_of, mine = tools(ins, outs, sems)
        pass_diagonal(pieces - 1, near, sib, copy, all_of)
        for p in range(pieces):
            all_of(0, p).wait_recv()
            for kind in (4, 5, 6):
                all_of(kind, p).wait_recv()
            for kind in range(7):
                all_of(kind, p).wait_send()
        for cp in mine:
            cp.wait()

    return _Comm(shards, [jax.ShapeDtypeStruct((N_DEV * rw, D), BF) for rw in r],
                 [pltpu.SemaphoreType.DMA((pieces, 7)), pltpu.SemaphoreType.DMA((pieces, 7)),
                  pltpu.SemaphoreType.DMA((n,))],
                 [(0.0, start)] + [(f, pass_on(p)) for p, f in enumerate(mids)] + [(1.0, finish)])


def _pair_comm(grads):
    n = len(grads)
    r = [g.shape[0] // N_DEV for g in grads]

    def start(ins, outs, sems):
        send_sems, recv_sems = sems
        x, y, c, _ = _place()
        for w in range(n):
            for a in range(N_CHIP):
                pltpu.make_async_remote_copy(
                    src_ref=ins[w].at[pl.ds((2 * a + 1 - c) * r[w], r[w]), :], dst_ref=outs[w].at[a],
                    send_sem=send_sems.at[w], recv_sem=recv_sems.at[w],
                    device_id=(x, y, 1 - c), device_id_type=MESH).start()

    def finish(ins, outs, sems):
        send_sems, recv_sems = sems
        x, y, c, _ = _place()
        for w in range(n):
            pltpu.make_async_remote_copy(
                src_ref=outs[w], dst_ref=outs[w], send_sem=send_sems.at[w], recv_sem=recv_sems.at[w],
                device_id=(x, y, c), device_id_type=MESH).wait()

    return _Comm(grads, [jax.ShapeDtypeStruct((N_CHIP, rw, D), BF) for rw in r],
                 [pltpu.SemaphoreType.DMA((n,)), pltpu.SemaphoreType.DMA((n,))],
                 [(0.0, start), (1.0, finish)])


def _pair_add(grads, gots, core, *, name):
    n, r = len(grads), gots[0].shape[1]
    tr = r if r <= 128 else r // 2
    steps = r // tr
    tile = lambda k: (lambda s: jnp.clip(s - k * steps, 0, steps - 1))

    def body(c_ref, *refs):
        g_refs, got_refs, o_refs = refs[:n], refs[n:2 * n], refs[2 * n:]
        s = pl.program_id(0)
        for k in range(n):
            @pl.when(jnp.logical_and(s >= k * steps, s < (k + 1) * steps))
            def _(k=k):
                o_refs[k][...] = (g_refs[k][:, 0].astype(F32) + got_refs[k][...].astype(F32)).astype(BF)

    grid_spec = pltpu.PrefetchScalarGridSpec(
        num_scalar_prefetch=1, grid=(n * steps,),
        in_specs=[pl.BlockSpec((N_CHIP, 1, tr, D), lambda s, c_ref, k=k: (0, c_ref[0], tile(k)(s), 0))
                  for k in range(n)]
        + [pl.BlockSpec((N_CHIP, tr, D), lambda s, c_ref, k=k: (0, tile(k)(s), 0)) for k in range(n)],
        out_specs=[pl.BlockSpec((N_CHIP, tr, D), lambda s, c_ref, k=k: (0, tile(k)(s), 0)) for k in range(n)])
    return _pcall(body, name=name, grid_spec=grid_spec,
                  out_shape=[jax.ShapeDtypeStruct((N_CHIP, r, D), BF)] * n,
                  compiler_params=_cp(("arbitrary",)))(
                      core, *[g.reshape(N_CHIP, 2, r, D) for g in grads], *gots)


def _chip_comm(pair_sums):
    n = len(pair_sums)
    r = [p.shape[1] for p in pair_sums]
    off = [sum(r[:w]) for w in range(n)]

    def tools(ins, outs, sems):
        send_sems, recv_sems, local_sems = sems
        x, y, c, chips = _place()
        my_chip = 2 * x + y

        def slot(w):
            return outs[0].at[my_chip, pl.ds(off[w], r[w]), :]

        own = [pltpu.make_async_copy(ins[w].at[my_chip], slot(w), local_sems.at[w]) for w in range(n)]
        return x, y, c, chips, my_chip, slot, own, send_sems, recv_sems

    def start(ins, outs, sems):
        x, y, c, chips, my_chip, slot, own, send_sems, recv_sems = tools(ins, outs, sems)
        for cp in own:
            cp.start()
        for j, chip in enumerate(chips):
            for w in range(n):
                pltpu.make_async_remote_copy(
                    src_ref=ins[w].at[2 * chip[0] + chip[1]], dst_ref=slot(w), send_sem=send_sems.at[j],
                    recv_sem=recv_sems.at[j], device_id=(*chip, c), device_id_type=MESH).start()

    def finish(ins, outs, sems):
        x, y, c, chips, my_chip, slot, own, send_sems, recv_sems = tools(ins, outs, sems)
        whole = outs[0].at[my_chip]
        for j in range(3):
            pltpu.make_async_remote_copy(
                src_ref=whole, dst_ref=whole, send_sem=send_sems.at[j], recv_sem=recv_sems.at[j],
                device_id=(x, y, c), device_id_type=MESH).wait()
        for cp in own:
            cp.wait()

    return _Comm(pair_sums, [jax.ShapeDtypeStruct((N_CHIP, sum(r), D), BF)],
                 [pltpu.SemaphoreType.DMA((3,)), pltpu.SemaphoreType.DMA((3,)), pltpu.SemaphoreType.DMA((n,))],
                 [(0.0, start), (1.0, finish)])


def _adam_math(w, g, m, v):
    m = ADAM_B1 * m + (1.0 - ADAM_B1) * g
    v = ADAM_B2 * v + (1.0 - ADAM_B2) * (g * g)
    m_hat = m / (1.0 - ADAM_B1 ** ADAM_STEP)
    v_hat = v / (1.0 - ADAM_B2 ** ADAM_STEP)
    delta = -ADAM_LR * (m_hat / (jnp.sqrt(v_hat) + ADAM_EPS) + ADAM_WD * w)
    return delta, m, v


SMALL = (("norm_mix_g", (1, D), 0), ("hgrn_norm_g", (1, D), 1), ("norm_ffn_g", (1, D), 2),
         ("norm_final_g", (1, D), 3), ("hgrn_lb_logits", (2, D), 4), ("attn_sinks", (1, 16), 6),
         ("b_in", (1, IN_W), 8))
LOSS_ROW = 7


def _small_allreduce_adam(grads, loss_row, params):
    n = len(SMALL)

    def rows_of(ref, shape, row):
        r, w = shape
        if w <= D:
            return ref[row:row + r, 0:w]
        pieces = [ref[row + k:row + k + 1, :] for k in range(-(-w // D))]
        return jnp.concatenate(pieces, axis=1)[:, 0:w]

    def body(*refs):
        g_refs, loss_ref = refs[:n], refs[n]
        wmv = refs[n + 1:4 * n + 1]
        loss_out = refs[4 * n + 1]
        outs = refs[4 * n + 2:8 * n + 2]
        mine, total, gath, send_sems, recv_sems = refs[8 * n + 2:]
        x, y, c, _ = _place()
        me = 4 * x + 2 * y + c
        mine[...] = jnp.zeros_like(mine)
        for g_ref, (_, (r, w), row) in zip(g_refs, SMALL):
            for k in range(-(-w // D)):
                wk = min(D, w - k * D)
                mine[row + k:row + k + r, 0:wk] = g_ref[:, k * D:k * D + wk]
        mine[LOSS_ROW:LOSS_ROW + 1, 0:128] = loss_ref[...]
        gath[me] = mine[...]
        cps = []
        for d in range(1, N_DEV):
            peer = (x ^ (d >> 2), y ^ ((d >> 1) & 1), c ^ (d & 1))
            cps.append(pltpu.make_async_remote_copy(
                src_ref=mine, dst_ref=gath.at[me], send_sem=send_sems.at[d - 1],
                recv_sem=recv_sems.at[d - 1], device_id=peer, device_id_type=MESH))
        for cp in cps:
            cp.start()
        for cp in cps:
            cp.wait()
        g = gath[0]
        for k in range(1, N_DEV):
            g = g + gath[k]
        total[...] = g
        loss_out[...] = total[LOSS_ROW:LOSS_ROW + 1, 0:128]
        for i, (_, shape, row) in enumerate(SMALL):
            gi = rows_of(total, shape, row)
            w_ref, m_ref, v_ref = wmv[3 * i:3 * i + 3]
            o = outs[4 * i:4 * i + 4]
            o[0][...] = gi
            o[1][...], o[2][...], o[3][...] = _adam_math(w_ref[...], gi, m_ref[...], v_ref[...])

    vm = pl.BlockSpec(memory_space=pltpu.VMEM)
    ins = [grads[name] for name, _, _ in SMALL] + [loss_row]
    for name, _, _ in SMALL:
        ins += list(params[name])
    out_shape = [jax.ShapeDtypeStruct((1, 128), F32)]
    for _, shape, _ in SMALL:
        out_shape += [jax.ShapeDtypeStruct(shape, F32)] * 4
    res = _pcall(body, name="small_allreduce_adam", in_specs=[vm] * len(ins), out_specs=[vm] * len(out_shape),
                 out_shape=out_shape,
                 scratch_shapes=[pltpu.VMEM((SMALL_ROWS, D), F32), pltpu.VMEM((SMALL_ROWS, D), F32),
                                 pltpu.VMEM((N_DEV, SMALL_ROWS, D), F32),
                                 pltpu.SemaphoreType.DMA((N_DEV - 1,)), pltpu.SemaphoreType.DMA((N_DEV - 1,))],
                 compiler_params=pltpu.CompilerParams(has_side_effects=True))(*ins)
    return res[0], {name: res[1 + 4 * i:5 + 4 * i] for i, (name, _, _) in enumerate(SMALL)}


def _adam(ws, parts, ms, vs, *, name):
    n, rows = len(ws), ws[0].shape[0]
    tr = rows if rows <= 128 else rows // 2
    steps = rows // tr
    tile = lambda k: (lambda s: jnp.clip(s - k * steps, 0, steps - 1))

    def body(*refs):
        w_refs, m_refs, v_refs, p_ref = refs[:n], refs[n:2 * n], refs[2 * n:3 * n], refs[3 * n]
        o_refs = refs[3 * n + 1:]
        s = pl.program_id(0)
        for k in range(n):
            @pl.when(jnp.logical_and(s >= k * steps, s < (k + 1) * steps))
            def _(k=k):
                g = p_ref[0].astype(F32)
                for a in range(1, N_CHIP):
                    g = g + p_ref[a].astype(F32)
                o = o_refs[4 * k:4 * k + 4]
                o[0][...] = g
                o[1][...], o[2][...], o[3][...] = _adam_math(w_refs[k][...], g, m_refs[k][...], v_refs[k][...])

    spec = lambda k: pl.BlockSpec((tr, D), lambda s, k=k: (tile(k)(s), 0))
    res = _pcall(body, name=name, grid=(n * steps,),
                 in_specs=[spec(k) for k in range(n)] * 3 + [pl.BlockSpec((N_CHIP, tr, D), lambda s: (0, s, 0))],
                 out_specs=[spec(k) for k in range(n) for _ in range(4)],
                 out_shape=[jax.ShapeDtypeStruct((rows, D), F32)] * (4 * n),
                 compiler_params=_cp(("arbitrary",)))(*ws, *ms, *vs, parts)
    return [res[4 * k:4 * k + 4] for k in range(n)]


def _step(x, tgt, shards, norm_mix_g, b_in, sinks, logits, hgrn_norm_g, norm_ffn_g, norm_final_g):
    t = x.shape[0]
    core = lax.axis_index("c").astype(jnp.int32).reshape(1)

    u1, (win_t,) = _rms_fwd(x, norm_mix_g, tm=512, name="rms_mix", comm=_gather_comm(shards[0:1], (0.1, 0.2, 0.3, 0.45, 0.55, 0.7, 0.8, 0.9)))
    (q, kv, h3, hf, gates), (wg_t, wba, wbh, wout) = _inproj_fwd(
        u1, win_t, b_in, t=t, comm=_gather_comm([shards[1]] + shards[4:7], (0.3, 0.5, 0.7, 0.9)))
    (y_attn,), _ = _attn_fwd(q, kv, sinks, t=t)
    (y_hgrn, o_pre, states), (wu_t, wd) = _hgrn_fwd(h3, hf, logits, hgrn_norm_g, t=t,
                                                    comm=_gather_comm(shards[2:4], (0.3, 0.5, 0.7, 0.9)))
    col = lambda j: j
    first, second = (lambda j: 0), (lambda j: 1)
    gate_tiles = [(gates, D, first), (gates, D, second)]

    def merge(prods, ex):
        (ya_, yb_), (ga, gb) = prods, ex
        sa, sb = _sig(ga.astype(F32)), _sig(gb.astype(F32))
        return sa, sb, ya_ * sa * (1.0 - sa), yb_ * sb * (1.0 - sb), sa * ya_ + sb * yb_

    sig_a, sig_b, dgate_a, dgate_b, merged = _fmm(
        [y_attn, y_hgrn], [(0, wba, False), (1, wbh, False)], gate_tiles, merge,
        [(BF, D, D, first)] * 5, m=t, n=D, tm=512, tn=D, name="branch_merge")
    def resid_norm(prods, ex):
        (p,), (xv, gv) = prods, ex
        hv = xv + p
        return hv, hv * lax.rsqrt(jnp.mean(hv * hv, axis=-1, keepdims=True) + EPS) * gv

    h1, u2 = _fmm([merged], [(0, wout, False)], [(x, D, first)], resid_norm, [(F32, D, D, first), (BF, D, D, first)],
                  m=t, n=D, tm=1024, tn=D, name="out_proj", vecs=[norm_ffn_g])

    def swiglu(prods, ex):
        g_, u_ = prods
        s = _sig(g_)
        silu = g_ * s
        return u_ * s * (1.0 + g_ * (1.0 - s)), silu, silu * u_

    dz_dgate, dz_dup, z = _fmm([u2], [(0, wg_t, True), (0, wu_t, True)], [], swiglu,
                               [(BF, FFN, FFN // 2, col)] * 3, m=t, n=FFN, tm=1024, tn=FFN // 2,
                               name="ffn_gate_up", cols_outer=True)
    def loss_head(prods, ex):
        (p,), (hv, tv, gv) = prods, ex
        hv = hv + p
        r = lax.rsqrt(jnp.mean(hv * hv, axis=-1, keepdims=True) + EPS)
        xh = hv * r
        err = xh * gv - tv
        lp = jnp.sum(jnp.sum(err * err, axis=1, keepdims=True), axis=0, keepdims=True) * (0.5 / D)
        dy = err * (1.0 / D)
        dxh = dy * gv
        dh = r * (dxh - xh * jnp.mean(dxh * xh, axis=-1, keepdims=True))
        return dh, dh, jnp.sum(dy * xh, axis=0, keepdims=True), jnp.broadcast_to(lp, (1, 128))

    dh2, dh2_b, d_norm_final, loss_row = _fmm(
        [z], [(0, wd, False)], [(h1, D, first), (tgt, D, first)], loss_head, [(F32, D, D, first), (BF, D, D, first)],
        m=t, n=D, tm=512, tn=D, name="ffn_down_loss", vecs=[norm_final_g], sums=[D, 128])

    def swiglu_bwd(prods, ex):
        (dz,), (da_, db_) = prods, ex
        return dz * da_.astype(F32), dz * db_.astype(F32)

    ffn_tiles = [(dz_dgate, FFN // 2, col), (dz_dup, FFN // 2, col)]
    dgt, dup = _fmm([dh2_b], [(0, wd, True)], ffn_tiles, swiglu_bwd, [(BF, FFN, FFN // 2, col)] * 2,
                    m=t, n=FFN, tm=1024, tn=FFN // 2, name="d_gate_up", cols_outer=True)
    (d_wd,) = _wgrad([z], dh2_b, name="d_w_down")
    (du2,) = _fmm([dgt, dup], [(0, wg_t, False), (1, wu_t, False)], [], lambda prods, ex: (prods[0] + prods[1],),
                  [(F32, D, 512, col)], m=t, n=D, tm=1024, tn=512, name="d_u2")
    d_wg, d_wu = _wgrad([dgt, dup], u2, name="d_w_gate_up")
    dh1, dh1_b, d_norm_ffn = _rms_bwd(du2, h1, norm_ffn_g, dh2, tm=512, name="rms_ffn_bwd")
    (d_wout,) = _wgrad([merged], dh1_b, name="d_w_out")

    def merge_bwd(prods, ex):
        (dm,), (sa, sb, ca, cb, wa, wb) = prods, ex
        dgate = jnp.concatenate([dm * ca.astype(F32), dm * cb.astype(F32)], axis=1)
        dya_ = (dm * sa.astype(F32)).astype(BF)
        dyb_ = (dm * sb.astype(F32)).astype(BF)
        return (dya_, dyb_, dgate, lax.dot_general(dya_, wa, _NT, preferred_element_type=F32),
                lax.dot_general(dyb_, wb, _NT, preferred_element_type=F32))

    ffn_grads = (d_wg, d_wu, d_wd)
    (dya, dyb, dgates, dy_attn, dy_hgrn), got = _fmm(
        [dh1_b], [(0, wout, True)], [(a, D, first) for a in (sig_a, sig_b, dgate_a, dgate_b)], merge_bwd,
        [(BF, D, D, first), (BF, D, D, first), (BF, 2 * D, 2 * D, first), (BF, D, D, first), (F32, D, D, first)],
        m=t, n=D, tm=512, tn=D, name="d_merge", consts=[wba, wbh], comm=_pair_comm(ffn_grads))
    pair_ffn = _pair_add(ffn_grads, got, core, name="pair_add_ffn")
    (d_wba,) = _wgrad([y_attn], dya, name="d_w_ba")
    (d_wbh,) = _wgrad([y_hgrn], dyb, name="d_w_bh")
    sq_grads = (d_wba, d_wbh, d_wout)
    (dh4, d_logits, d_hgrn_norm), (parts_ffn, *got) = _hgrn_bwd(
        h3, hf, logits, hgrn_norm_g, o_pre, states, dy_hgrn, t=t,
        comm=_both(_chip_comm(pair_ffn), _pair_comm(sq_grads)))
    pair_sq = _pair_add(sq_grads, got, core, name="pair_add_sq")
    (dq, dkv, d_sinks), (parts_sq,) = _attn_bwd(q, kv, sinks, dy_attn, t=t, comm=_chip_comm(pair_sq))
    dps = (dq, dkv, dh4, dgates)
    d_win_t, d_b_in = _inproj_bwd_w(dps, u1, t=t)
    half0, got_in = _inproj_bwd_x(dps, win_t, x, norm_mix_g, dh1, t=t, part=0, comm=_pair_comm([d_win_t]))
    pair_in = _pair_add([d_win_t], got_in, core, name="pair_add_w_in")
    (grad_x, d_norm_mix), (parts_in,) = _inproj_bwd_x(dps, win_t, x, norm_mix_g, dh1, t=t, part=1, prev=half0,
                                                      comm=_chip_comm(pair_in))

    small_grads = (d_norm_mix, d_b_in, d_sinks, d_logits, d_hgrn_norm, d_norm_ffn, d_norm_final)
    return loss_row, grad_x, (parts_in, parts_ffn, parts_sq), small_grads


def kernel(x, norm_mix_g, w_in, b_in, attn_sinks, hgrn_lb_logits, hgrn_norm_g, w_branch_attn, w_branch_hgrn, w_out, norm_ffn_g, w_ffn_gate, w_ffn_up, w_ffn_down, norm_final_g, loss_target, m_norm_mix_g, m_w_in, m_b_in, m_attn_sinks, m_hgrn_lb_logits, m_hgrn_norm_g, m_w_branch_attn, m_w_branch_hgrn, m_w_out, m_norm_ffn_g, m_w_ffn_gate, m_w_ffn_up, m_w_ffn_down, m_norm_final_g, v_norm_mix_g, v_w_in, v_b_in, v_attn_sinks, v_hgrn_lb_logits, v_hgrn_norm_g, v_w_branch_attn, v_w_branch_hgrn, v_w_out, v_norm_ffn_g, v_w_ffn_gate, v_w_ffn_up, v_w_ffn_down, v_norm_final_g):
    shards = [w_in[0].T.astype(BF), w_ffn_gate[0].T.astype(BF), w_ffn_up[0].T.astype(BF),
              w_ffn_down[0].astype(BF), w_branch_attn[0].astype(BF), w_branch_hgrn[0].astype(BF),
              w_out[0].astype(BF)]
    loss_row, grad_x, grad_parts, small_grads = _step(
        x[0], loss_target[0], shards, norm_mix_g, b_in, attn_sinks, hgrn_lb_logits, hgrn_norm_g,
        norm_ffn_g, norm_final_g.reshape(1, D))

    d_norm_mix, d_b_in, d_sinks, d_logits, d_hgrn_norm, d_norm_ffn, d_norm_final = small_grads
    row = lambda a: a.reshape(1, D)
    loss_out, small = _small_allreduce_adam(
        dict(norm_mix_g=d_norm_mix, hgrn_norm_g=d_hgrn_norm, norm_ffn_g=d_norm_ffn, norm_final_g=d_norm_final,
             hgrn_lb_logits=d_logits, attn_sinks=d_sinks, b_in=d_b_in),
        loss_row,
        dict(norm_mix_g=(norm_mix_g, m_norm_mix_g, v_norm_mix_g), hgrn_norm_g=(hgrn_norm_g, m_hgrn_norm_g, v_hgrn_norm_g),
             norm_ffn_g=(norm_ffn_g, m_norm_ffn_g, v_norm_ffn_g),
             norm_final_g=(row(norm_final_g), row(m_norm_final_g), row(v_norm_final_g)),
             hgrn_lb_logits=(hgrn_lb_logits, m_hgrn_lb_logits, v_hgrn_lb_logits),
             attn_sinks=(attn_sinks, m_attn_sinks, v_attn_sinks), b_in=(b_in, m_b_in, v_b_in)))
    small["norm_final_g"] = [a.reshape(D) for a in small["norm_final_g"]]
    loss = loss_out[0, 0]

    names = ["w_in", "w_ffn_gate", "w_ffn_up", "w_ffn_down", "w_branch_attn", "w_branch_hgrn", "w_out"]
    w_full = dict(w_in=(w_in, m_w_in, v_w_in), w_ffn_gate=(w_ffn_gate, m_w_ffn_gate, v_w_ffn_gate),
                  w_ffn_up=(w_ffn_up, m_w_ffn_up, v_w_ffn_up), w_ffn_down=(w_ffn_down, m_w_ffn_down, v_w_ffn_down),
                  w_branch_attn=(w_branch_attn, m_w_branch_attn, v_w_branch_attn),
                  w_branch_hgrn=(w_branch_hgrn, m_w_branch_hgrn, v_w_branch_hgrn),
                  w_out=(w_out, m_w_out, v_w_out))
    big = {}
    for group, parts, tag in zip((names[0:1], names[1:4], names[4:7]), grad_parts, ("w_in", "ffn", "square")):
        flip = [name in names[0:3] for name in group]
        view = lambda a, f: a[0].T if f else a[0]
        cols = [[view(w_full[name][j], f) for name, f in zip(group, flip)] for j in range(3)]
        res = _adam(cols[0], parts, cols[1], cols[2], name="adam_" + tag)
        for name, f, r in zip(group, flip, res):
            big[name] = [a.T[None] if f else a[None] for a in r]

    order = ["norm_mix_g", "w_in", "b_in", "attn_sinks", "hgrn_lb_logits", "hgrn_norm_g", "w_branch_attn",
             "w_branch_hgrn", "w_out", "norm_ffn_g", "w_ffn_gate", "w_ffn_up", "w_ffn_down", "norm_final_g"]
    outs = [loss, grad_x[None]]
    for kind in range(4):
        for name in order:
            outs.append(big[name][kind] if name in big else small[name][kind])
    return tuple(outs)
```

```python
import math

import jax
import jax.numpy as jnp
from jax import lax
from jax.experimental import pallas as pl
from jax.experimental.pallas import tpu as pltpu

F32 = jnp.float32
BF = jnp.bfloat16
MESH = pl.DeviceIdType.MESH

D = 1024
HEAD = 64
N_PAIR = 8
BLK = 128
CH = 64
HG_SUB = 4
HG_SUB_BWD = 4
HG_HEADS = 8
HG_K = 128
FFN = 2816
IN_W = 7424
N_DEV = 8
N_CHIP = 4
EPS = 1e-6
NEG = -1e30
SCALE = 1.0 / math.sqrt(HEAD)
VMEM_LIMIT = 56 * 1024 * 1024
WT = 256

ADAM_LR, ADAM_B1, ADAM_B2, ADAM_EPS, ADAM_WD, ADAM_STEP = 0.001, 0.9, 0.999, 1e-08, 0.01, 10

GRP_OFF = (0, D // WT, (D + 256) // WT, (5 * D + 256) // WT)
GRP_N = (D // WT, 256 // WT, 4 * D // WT, 2 * D // WT)
SMALL_ROWS = 16


_NN = (((1,), (0,)), ((), ()))
_NT = (((1,), (1,)), ((), ()))
_TN = (((0,), (0,)), ((), ()))


def _pcall(body, **kw):
    return pl.pallas_call(body, **kw)


def _cp(sem=None, **kw):
    return pltpu.CompilerParams(dimension_semantics=sem, vmem_limit_bytes=VMEM_LIMIT, **kw)


def _sig(v):
    return 0.5 * jnp.tanh(0.5 * v) + 0.5


def _accum(ref, val, first):
    @pl.when(first)
    def _():
        ref[...] = val

    @pl.when(jnp.logical_not(first))
    def _():
        ref[...] += val


class _Comm:
    def __init__(self, ins, out_shapes, sem_shapes, phases):
        self.ins, self.out_shapes, self.sem_shapes, self.phases = list(ins), list(out_shapes), list(sem_shapes), phases


def _both(a, b):
    ni, no, ns = len(a.ins), len(a.out_shapes), len(a.sem_shapes)

    def of_a(fn):
        return lambda ins, outs, sems: fn(ins[:ni], outs[:no], sems[:ns])

    def of_b(fn):
        return lambda ins, outs, sems: fn(ins[ni:], outs[no:], sems[ns:])

    return _Comm(a.ins + b.ins, a.out_shapes + b.out_shapes, a.sem_shapes + b.sem_shapes,
                 [(f, of_a(fn)) for f, fn in a.phases] + [(f, of_b(fn)) for f, fn in b.phases])


def _host(body, comm, n_in, n_out, n_scr, nsteps, step_fn):
    if comm is None:
        return body
    ci, co = len(comm.ins), len(comm.out_shapes)

    def wrapped(*refs):
        p = 0
        ins, p = refs[p:p + n_in], p + n_in
        cins, p = refs[p:p + ci], p + ci
        outs, p = refs[p:p + n_out], p + n_out
        couts, p = refs[p:p + co], p + co
        scr, p = refs[p:p + n_scr], p + n_scr
        csems = refs[p:]
        step = step_fn()
        for frac, fn in comm.phases:
            if frac < 1.0:
                @pl.when(step == int(round(frac * (nsteps - 1))))
                def _(fn=fn):
                    fn(cins, couts, csems)
        body(*ins, *outs, *scr)
        for frac, fn in comm.phases:
            if frac >= 1.0:
                @pl.when(step == nsteps - 1)
                def _(fn=fn):
                    fn(cins, couts, csems)

    return wrapped


def _hosted_call(body, comm, args, *, name, grid, in_specs, out_specs, out_shape, scratch_shapes, sem,
                 nsteps, step_fn, aliases=None):
    n_in, n_out, n_scr = len(in_specs), len(out_specs), len(scratch_shapes)
    args = list(args)
    extra = {}
    if comm is not None:
        in_specs = list(in_specs) + [_hbm_spec()] * len(comm.ins)
        out_specs = list(out_specs) + [_hbm_spec()] * len(comm.out_shapes)
        out_shape = list(out_shape) + comm.out_shapes
        scratch_shapes = list(scratch_shapes) + comm.sem_shapes
        args += comm.ins
        extra = dict(has_side_effects=True)
    outs = _pcall(_host(body, comm, n_in, n_out, n_scr, nsteps, step_fn), name=name, grid=grid,
                  in_specs=in_specs, out_specs=out_specs, out_shape=out_shape, scratch_shapes=scratch_shapes,
                  input_output_aliases=aliases or {}, compiler_params=_cp(sem, **extra))(*args)
    return list(outs[:n_out]), list(outs[n_out:])


def _hbm_spec():
    return pl.BlockSpec(memory_space=pl.ANY)


def _wgrad(a_list, b, *, name):
    (t, m), n, gm = a_list[0].shape, b.shape[1], a_list[0].shape[1] // WT
    n_a = len(a_list)
    tile = lambda k: (lambda s: jnp.clip(s - k * gm, 0, gm - 1))

    def body(*refs):
        a_refs, b_ref, o_refs = refs[:n_a], refs[n_a], refs[n_a + 1:]
        s = pl.program_id(0)
        for k in range(n_a):
            @pl.when(jnp.logical_and(s >= k * gm, s < (k + 1) * gm))
            def _(k=k):
                o_refs[k][...] = lax.dot_general(a_refs[k][...], b_ref[...], _TN,
                                                 preferred_element_type=F32).astype(BF)

    return _pcall(body, name=name, grid=(n_a * gm,),
                  in_specs=[pl.BlockSpec((t, WT), lambda s, k=k: (0, tile(k)(s))) for k in range(n_a)]
                  + [pl.BlockSpec((t, n), lambda s: (0, 0))],
                  out_specs=[pl.BlockSpec((WT, n), lambda s, k=k: (tile(k)(s), 0)) for k in range(n_a)],
                  out_shape=[jax.ShapeDtypeStruct((m, n), BF)] * n_a,
                  compiler_params=_cp(("arbitrary",)))(*a_list, b)


def _fmm(lhs, rhs, extras, epilogue, outs, *, m, n, tm, tn, name, comm=None, vecs=(), consts=(), sums=(),
         cols_outer=False):
    tm, tn = min(tm, m), min(tn, n)
    assert m % tm == 0 and n % tn == 0 and (not sums or (tn == n and not cols_outer)), (name, m, n, tm, tn)
    in_specs, args = [], []
    for a in lhs:
        in_specs.append(pl.BlockSpec((tm, a.shape[1]), lambda i, j: (i, 0)))
        args.append(a)
    for li, b, tb in rhs:
        k = lhs[li].shape[1]
        in_specs.append(pl.BlockSpec((tn, k), lambda i, j: (j, 0)) if tb
                        else pl.BlockSpec((k, tn), lambda i, j: (0, j)))
        args.append(b)
    for arr, w, col in extras:
        in_specs.append(pl.BlockSpec((tm, w), lambda i, j, col=col: (i, col(j))))
        args.append(arr)
    for vec in vecs:
        in_specs.append(pl.BlockSpec((1, tn), lambda i, j: (0, j)))
        args.append(vec)
    for whole in consts:
        in_specs.append(pl.BlockSpec(whole.shape, lambda i, j: (0, 0)))
        args.append(whole)
    out_specs = [pl.BlockSpec((tm, w), lambda i, j, col=col: (i, col(j))) for _, _, w, col in outs]
    out_shape = [jax.ShapeDtypeStruct((m, total), dt) for dt, total, _, _ in outs]
    for w in sums:
        out_specs.append(pl.BlockSpec((1, w), lambda i, j: (0, 0)))
        out_shape.append(jax.ShapeDtypeStruct((1, w), F32))
    nl, nr, ne, no = len(lhs), len(rhs), len(extras) + len(vecs) + len(consts), len(outs)

    def body(*refs):
        prods = []
        for r, (li, _, tb) in enumerate(rhs):
            prods.append(lax.dot_general(refs[li][...], refs[nl + r][...], _NT if tb else _NN,
                                         preferred_element_type=F32))
        vals = epilogue(prods, [ref[...] for ref in refs[nl + nr:nl + nr + ne]])
        o_refs = refs[nl + nr + ne:]
        for o_ref, v in zip(o_refs[:no], vals[:no]):
            o_ref[...] = v.astype(o_ref.dtype)
        for s_ref, v in zip(o_refs[no:], vals[no:]):
            _accum(s_ref, v, pl.program_id(0) == 0)

    grid = (m // tm, n // tn)
    if cols_outer:
        flip = lambda spec: pl.BlockSpec(spec.block_shape, lambda j, i, f=spec.index_map: f(i, j))
        in_specs, out_specs, grid = [flip(s) for s in in_specs], [flip(s) for s in out_specs], grid[::-1]
    res, comm_res = _hosted_call(
        body, comm, args, name=name, grid=grid, in_specs=in_specs, out_specs=out_specs,
        out_shape=out_shape, scratch_shapes=[], sem=("arbitrary", "arbitrary"), nsteps=grid[0] * grid[1],
        step_fn=lambda: pl.program_id(0) * grid[1] + pl.program_id(1))
    return res if comm is None else (res, comm_res)


def _grp_of(i):
    return [jnp.logical_and(i >= GRP_OFF[g], i < GRP_OFF[g] + GRP_N[g]) for g in range(4)]


def _grp_idx(i, g):
    return jnp.clip(i - GRP_OFF[g], 0, GRP_N[g] - 1)


def _inproj_fwd(u, win_t, b_in, *, t, comm=None):
    tm = min(1024, t)
    n_row = t // tm
    n_chunks, h_first, g_first = 8, 2, 6
    sub = D // WT

    def w_block(l):
        return jnp.where(l == 0, GRP_OFF[0], jnp.where(l == 1, GRP_OFF[1], GRP_OFF[2] + sub * (l - h_first)))

    def body(u_ref, *rest):
        w_refs, b_refs, (q_ref, kv_ref, h3_ref, hf_ref, g_ref) = rest[:sub], rest[sub:2 * sub], rest[2 * sub:]
        l = pl.program_id(1)

        @pl.when(l == 1)
        def _():
            kv_ref[...] = (lax.dot_general(u_ref[...], w_refs[0][...], _NT, preferred_element_type=F32)
                           + b_refs[0][...]).astype(BF)

        is_hf = l == h_first + 1
        in_h3 = jnp.logical_and(jnp.logical_and(l >= h_first, l < g_first), jnp.logical_not(is_hf))
        for pred, o_ref in ((l == 0, q_ref), (in_h3, h3_ref), (is_hf, hf_ref), (l >= g_first, g_ref)):
            @pl.when(pred)
            def _(o_ref=o_ref):
                w = jnp.concatenate([w[...] for w in w_refs], axis=0)
                b = jnp.concatenate([b[...] for b in b_refs], axis=1)
                o_ref[...] = (lax.dot_general(u_ref[...], w, _NT, preferred_element_type=F32) + b).astype(o_ref.dtype)

    return _hosted_call(
        body, comm, [u] + [win_t] * sub + [b_in] * sub, name="inproj_fwd", grid=(n_row, n_chunks),
        in_specs=[pl.BlockSpec((tm, D), lambda i, l: (i, 0))]
        + [pl.BlockSpec((WT, D), lambda i, l, o=o: (w_block(l) + o, 0)) for o in range(sub)]
        + [pl.BlockSpec((1, WT), lambda i, l, o=o: (0, w_block(l) + o)) for o in range(sub)],
        out_specs=[pl.BlockSpec((tm, D), lambda i, l: (i, 0)),
                   pl.BlockSpec((tm, 256), lambda i, l: (i, 0)),
                   pl.BlockSpec((tm, D), lambda i, l: (i, jnp.clip(l - h_first - 1, 0, 2))),
                   pl.BlockSpec((tm, D), lambda i, l: (i, 0)),
                   pl.BlockSpec((tm, D), lambda i, l: (i, jnp.clip(l - g_first, 0, 1)))],
        out_shape=[jax.ShapeDtypeStruct((t, D), BF), jax.ShapeDtypeStruct((t, 256), BF),
                   jax.ShapeDtypeStruct((t, 3 * D), BF), jax.ShapeDtypeStruct((t, D), F32),
                   jax.ShapeDtypeStruct((t, 2 * D), BF)],
        scratch_shapes=[], sem=("arbitrary", "arbitrary"), nsteps=n_row * n_chunks,
        step_fn=lambda: pl.program_id(0) * n_chunks + pl.program_id(1))


def _inproj_bwd_x(dps, win_t, x, g, resid, *, t, part, prev=None, comm=None):
    n_row = 8 if t >= 4096 else 4
    tm = t // n_row
    first = 1
    per = first if part == 0 else n_row - first
    row = lambda i: part * first + i

    n_chunks = 4
    sub = 2 * D // WT

    def w_block(l):
        return jnp.where(l == 0, 0, GRP_OFF[2] + sub * (l - 1))

    def body(d0, d1, d2, d3, *rest):
        w_refs, (x_ref, g_ref, r_ref) = rest[:sub], rest[sub:sub + 3]
        dg_prev = rest[sub + 3] if prev is not None else None
        o_ref, dg_ref, acc_ref = rest[-3], rest[-2], rest[-1]
        i, l = pl.program_id(0), pl.program_id(1)

        @pl.when(l == 0)
        def _():
            wq = jnp.concatenate([w[...] for w in w_refs[:GRP_N[0]]], axis=0)
            acc_ref[...] = (jnp.dot(d0[...], wq, preferred_element_type=F32)
                            + jnp.dot(d1[...], w_refs[GRP_N[0]][...], preferred_element_type=F32))

        for pred, d_ref in ((jnp.logical_and(l >= 1, l < 3), d2), (l == 3, d3)):
            @pl.when(pred)
            def _(d_ref=d_ref):
                w = jnp.concatenate([w[...] for w in w_refs], axis=0)
                acc_ref[...] += jnp.dot(d_ref[...], w, preferred_element_type=F32)

        @pl.when(l == n_chunks - 1)
        def _():
            xv = x_ref[...]
            r = lax.rsqrt(jnp.mean(xv * xv, axis=-1, keepdims=True) + EPS)
            xh = xv * r
            du = acc_ref[...]
            dxh = du * g_ref[...]
            o_ref[...] = r_ref[...] + r * (dxh - xh * jnp.mean(dxh * xh, axis=-1, keepdims=True))
            dg = jnp.sum(du * xh, axis=0, keepdims=True)
            if dg_prev is not None:
                dg = dg + jnp.where(i == 0, 1.0, 0.0) * dg_prev[...]
            _accum(dg_ref, dg, i == 0)

    rows = lambda w: pl.BlockSpec((tm, w), lambda i, l: (row(i), 0))
    in_specs = ([rows(D), rows(256),
                 pl.BlockSpec((tm, 2 * D), lambda i, l: (row(i), jnp.clip(l - 1, 0, 1))), rows(2 * D)]
                + [pl.BlockSpec((WT, D), lambda i, l, o=o: (w_block(l) + o, 0)) for o in range(sub)]
                + [rows(D), pl.BlockSpec((1, D), lambda i, l: (0, 0)), rows(D)])
    args = list(dps) + [win_t] * sub + [x, g, resid]
    aliases = None
    if prev is not None:
        in_specs += [pl.BlockSpec((1, D), lambda i, l: (0, 0)), _hbm_spec()]
        args += [prev[1], prev[0]]
        aliases = {len(args) - 1: 0}
    return _hosted_call(
        body, comm, args, name="inproj_bwd_x%d" % part, grid=(per, n_chunks), in_specs=in_specs,
        out_specs=[rows(D), pl.BlockSpec((1, D), lambda i, l: (0, 0))],
        out_shape=[jax.ShapeDtypeStruct((t, D), F32), jax.ShapeDtypeStruct((1, D), F32)],
        scratch_shapes=[pltpu.VMEM((tm, D), F32)], sem=("arbitrary", "arbitrary"), nsteps=per * n_chunks,
        step_fn=lambda: pl.program_id(0) * n_chunks + pl.program_id(1), aliases=aliases)


def _inproj_bwd_w(dps, u, *, t):
    n_tiles = IN_W // WT
    dims = (((0,), (0,)), ((), ()))

    def body(d0, d1, d2, d3, u_ref, o_ref, db_ref):
        i = pl.program_id(0)
        uv = u_ref[...]
        for g, (pred, d_ref) in enumerate(zip(_grp_of(i), (d0, d1, d2, d3))):
            @pl.when(pred)
            def _(d_ref=d_ref):
                dv = d_ref[...]
                o_ref[...] = lax.dot_general(dv, uv, dims, preferred_element_type=F32).astype(BF)
                db_ref[...] = jnp.sum(dv.astype(F32), axis=0, keepdims=True)

    return _pcall(body, name="inproj_bwd_w", grid=(n_tiles,),
                  in_specs=[pl.BlockSpec((t, WT), lambda i, g=g: (0, _grp_idx(i, g))) for g in range(4)]
                  + [pl.BlockSpec((t, D), lambda i: (0, 0))],
                  out_specs=[pl.BlockSpec((WT, D), lambda i: (i, 0)),
                             pl.BlockSpec((1, WT), lambda i: (0, i))],
                  out_shape=[jax.ShapeDtypeStruct((IN_W, D), BF), jax.ShapeDtypeStruct((1, IN_W), F32)],
                  compiler_params=_cp(("arbitrary",)))(*dps, u)


def _row_spec(tm, width, col=0):
    return pl.BlockSpec((tm, width), lambda i: (i, col))


def _vec_spec(width):
    return pl.BlockSpec((1, width), lambda i: (0, 0))


def _rms_fwd(x, g, *, tm, name, comm=None):
    t = x.shape[0]
    tm = min(tm, t)

    def body(x_ref, g_ref, u_ref):
        xv = x_ref[...]
        r = lax.rsqrt(jnp.mean(xv * xv, axis=-1, keepdims=True) + EPS)
        u_ref[...] = (xv * r * g_ref[...]).astype(BF)

    (u,), comm_res = _hosted_call(
        body, comm, (x, g), name=name, grid=(t // tm,), in_specs=[_row_spec(tm, D), _vec_spec(D)],
        out_specs=[_row_spec(tm, D)], out_shape=[jax.ShapeDtypeStruct((t, D), BF)], scratch_shapes=[],
        sem=("arbitrary",), nsteps=t // tm, step_fn=lambda: pl.program_id(0))
    return u if comm is None else (u, comm_res)


def _rms_bwd(du, x, g, resid, *, tm, name):
    t = x.shape[0]
    tm = min(tm, t)

    def body(du_ref, x_ref, g_ref, r_ref, dx_ref, dxb_ref, dg_ref):
        xv = x_ref[...]
        r = lax.rsqrt(jnp.mean(xv * xv, axis=-1, keepdims=True) + EPS)
        xh = xv * r
        duv = du_ref[...]
        dxh = duv * g_ref[...]
        dx = r_ref[...] + r * (dxh - xh * jnp.mean(dxh * xh, axis=-1, keepdims=True))
        dx_ref[...] = dx
        dxb_ref[...] = dx.astype(BF)
        _accum(dg_ref, jnp.sum(duv * xh, axis=0, keepdims=True), pl.program_id(0) == 0)

    return _pcall(body, name=name, grid=(t // tm,),
                  in_specs=[_row_spec(tm, D), _row_spec(tm, D), _vec_spec(D), _row_spec(tm, D)],
                  out_specs=[_row_spec(tm, D), _row_spec(tm, D), _vec_spec(D)],
                  out_shape=[jax.ShapeDtypeStruct((t, D), F32), jax.ShapeDtypeStruct((t, D), BF),
                             jax.ShapeDtypeStruct((1, D), F32)],
                  compiler_params=_cp(("arbitrary",)))(du, x, g, resid)


def _attn_kv_tiles(kprev, kcur):
    kv = jnp.concatenate([kprev, kcur], axis=0).astype(F32)
    lo = lax.broadcasted_iota(jnp.int32, (2 * BLK, 128), 1) < HEAD
    tiles = []
    for part in (kv[:, 0:128], kv[:, 128:256]):
        rolled = pltpu.roll(part, HEAD, 1)
        z = jnp.zeros_like(part)
        tiles.append(((jnp.where(lo, part, z).astype(BF), jnp.where(lo, z, rolled).astype(BF)),
                      (jnp.where(lo, rolled, z).astype(BF), jnp.where(lo, z, part).astype(BF))))
    k_t, v_t = tiles
    return [(jnp.concatenate(k_t[h], axis=0), jnp.concatenate(v_t[h], axis=0)) for h in range(2)]


def _attn_mask(i):
    qi = lax.broadcasted_iota(jnp.int32, (BLK, 2 * BLK), 0)
    kj = lax.broadcasted_iota(jnp.int32, (BLK, 2 * BLK), 1)
    first_key = jnp.where(i == 0, BLK, 0)
    in_prev = jnp.logical_and(jnp.logical_and(kj < BLK, kj > qi), kj >= first_key)
    in_cur = jnp.logical_and(kj >= BLK, kj - BLK <= qi)
    return jnp.logical_or(in_prev, in_cur)


def _attn_probs(s, sink, valid):
    s = jnp.where(valid, s * SCALE, NEG)
    mx = jnp.maximum(jnp.max(s, axis=-1, keepdims=True), sink)
    e = jnp.exp(s - mx)
    es = jnp.exp(sink - mx)
    inv = 1.0 / (jnp.sum(e, axis=-1, keepdims=True) + es)
    return e * inv, es * inv


_KEYS = 2 * BLK


def _pair(ref, j):
    return ref[:, j * 128:(j + 1) * 128]


def _attn_fwd(q, kv, sinks, *, t, comm=None):
    nb = t // BLK
    sub = 2

    def body(sink_ref, q_ref, kp_ref, kc_ref, o_ref):
        i = pl.program_id(0)
        for c in range(sub):
            rows = slice(c * BLK, (c + 1) * BLK)
            valid = _attn_mask(sub * i + c)
            tiles = _attn_kv_tiles(kp_ref[...] if c == 0 else kc_ref[(c - 1) * BLK:c * BLK, :], kc_ref[rows, :])
            s = [lax.dot_general(q_ref[rows, j * 128:(j + 1) * 128], tiles[j // 4][0], _NT,
                                 preferred_element_type=F32) for j in range(N_PAIR)]
            p = []
            for j in range(N_PAIR):
                pe, _ = _attn_probs(s[j][:, 0:_KEYS], sink_ref[0, 2 * j], valid)
                po, _ = _attn_probs(s[j][:, _KEYS:2 * _KEYS], sink_ref[0, 2 * j + 1], valid)
                p.append(jnp.concatenate([pe.astype(BF), po.astype(BF)], axis=1))
            for j in range(N_PAIR):
                o_ref[rows, j * 128:(j + 1) * 128] = jnp.dot(p[j], tiles[j // 4][1],
                                                             preferred_element_type=F32).astype(BF)

    return _hosted_call(
        body, comm, (sinks, q, kv, kv), name="attn_fwd", grid=(nb // sub,),
        in_specs=[pl.BlockSpec(memory_space=pltpu.SMEM),
                  pl.BlockSpec((sub * BLK, D), lambda i: (i, 0)),
                  pl.BlockSpec((BLK, 256), lambda i: (jnp.maximum(sub * i - 1, 0), 0)),
                  pl.BlockSpec((sub * BLK, 256), lambda i: (i, 0))],
        out_specs=[pl.BlockSpec((sub * BLK, D), lambda i: (i, 0))],
        out_shape=[jax.ShapeDtypeStruct((t, D), BF)],
        scratch_shapes=[], sem=("arbitrary",), nsteps=nb // sub, step_fn=lambda: pl.program_id(0))


def _attn_bwd(q, kv, sinks, do, *, t, comm=None):
    nb = t // BLK
    last = nb - 1

    def body(sink_ref, q_ref, kp_ref, kc_ref, do_ref, dq_ref, dkv_ref, ds_ref, carry_ref):
        i = pl.program_id(0)

        @pl.when(i == 0)
        def _():
            ds_ref[...] = jnp.zeros_like(ds_ref)
            carry_ref[...] = jnp.zeros_like(carry_ref)

        @pl.when(i < nb)
        def _():
            valid = _attn_mask(i)
            tiles = _attn_kv_tiles(kp_ref[...], kc_ref[...])
            lane1 = lax.broadcasted_iota(jnp.int32, (1, 128), 1)
            dsink = jnp.zeros((1, 128), F32)
            s = [lax.dot_general(_pair(q_ref, j), tiles[j // 4][0], _NT, preferred_element_type=F32)
                 for j in range(N_PAIR)]
            dp = [lax.dot_general(_pair(do_ref, j), tiles[j // 4][1], _NT, preferred_element_type=F32)
                  for j in range(N_PAIR)]
            p_all, ds_all = [], []
            for j in range(N_PAIR):
                halves = []
                for par in range(2):
                    cols = slice(par * _KEYS, (par + 1) * _KEYS)
                    p, ps = _attn_probs(s[j][:, cols], sink_ref[0, 2 * j + par], valid)
                    dpj = dp[j][:, cols]
                    dd = jnp.sum(p * dpj, axis=-1, keepdims=True)
                    dsink = dsink + jnp.where(lane1 == 2 * j + par,
                                              -jnp.sum(ps * dd, axis=0, keepdims=True), 0.0)
                    halves.append((p.astype(BF), (p * (dpj - dd)).astype(BF)))
                p_all.append(jnp.concatenate([halves[0][0], halves[1][0]], axis=1))
                ds_all.append(jnp.concatenate([halves[0][1], halves[1][1]], axis=1))
            for j in range(N_PAIR):
                dq_ref[:, j * 128:(j + 1) * 128] = (
                    jnp.dot(ds_all[j], tiles[j // 4][0], preferred_element_type=F32) * SCALE).astype(BF)
            ds_ref[...] += dsink
            gk, gv = [], []
            for h in range(2):
                grp = range(4 * h, 4 * h + 4)
                q_rows = jnp.concatenate([_pair(q_ref, j) for j in grp], axis=0)
                do_rows = jnp.concatenate([_pair(do_ref, j) for j in grp], axis=0)
                g_k = lax.dot_general(jnp.concatenate([ds_all[j] for j in grp], axis=0), q_rows, _TN,
                                      preferred_element_type=F32)
                g_v = lax.dot_general(jnp.concatenate([p_all[j] for j in grp], axis=0), do_rows, _TN,
                                      preferred_element_type=F32)
                gk.append((g_k[0:_KEYS], g_k[_KEYS:2 * _KEYS]))
                gv.append((g_v[0:_KEYS], g_v[_KEYS:2 * _KEYS]))
            lo = lax.broadcasted_iota(jnp.int32, (2 * BLK, 128), 1) < HEAD
            zero = jnp.zeros((2 * BLK, 128), F32)

            def unpad(g):
                return (jnp.where(lo, g[0][0] + pltpu.roll(g[0][1], HEAD, 1), zero)
                        + jnp.where(lo, zero, pltpu.roll(g[1][0], HEAD, 1) + g[1][1]))

            dk = unpad(gk) * SCALE
            dv = unpad(gv)
            dkv_ref[:, 0:128] = (carry_ref[:, 0:128] + dk[0:BLK]).astype(BF)
            dkv_ref[:, 128:256] = (carry_ref[:, 128:256] + dv[0:BLK]).astype(BF)
            carry_ref[:, 0:128] = dk[BLK:2 * BLK]
            carry_ref[:, 128:256] = dv[BLK:2 * BLK]

        @pl.when(i == nb)
        def _():
            dkv_ref[...] = carry_ref[...].astype(BF)

    return _hosted_call(
        body, comm, (sinks, q, kv, kv, do), name="attn_bwd", grid=(nb + 1,),
        in_specs=[pl.BlockSpec(memory_space=pltpu.SMEM),
                  pl.BlockSpec((BLK, D), lambda i: (jnp.minimum(i, last), 0)),
                  pl.BlockSpec((BLK, 256), lambda i: (jnp.clip(i - 1, 0, last), 0)),
                  pl.BlockSpec((BLK, 256), lambda i: (jnp.minimum(i, last), 0)),
                  pl.BlockSpec((BLK, D), lambda i: (jnp.minimum(i, last), 0))],
        out_specs=[pl.BlockSpec((BLK, D), lambda i: (jnp.minimum(i, last), 0)),
                   pl.BlockSpec((BLK, 256), lambda i: (jnp.maximum(i - 1, 0), 0)),
                   pl.BlockSpec((1, 128), lambda i: (0, 0))],
        out_shape=[jax.ShapeDtypeStruct((t, D), BF), jax.ShapeDtypeStruct((t, 256), BF),
                   jax.ShapeDtypeStruct((1, 128), F32)],
        scratch_shapes=[pltpu.VMEM((BLK, 256), F32)], sem=("arbitrary",), nsteps=nb + 1,
        step_fn=lambda: pl.program_id(0))


def _split3(v):
    h = v.astype(BF)
    r = v - h.astype(F32)
    m = r.astype(BF)
    lo = (r - m.astype(F32)).astype(BF)
    return jnp.concatenate([h, m, lo], axis=1)


def _apply01(mat, v):
    n = v.shape[1]
    r = jnp.dot(mat, _split3(v), preferred_element_type=F32)
    return r[:, 0:n] + r[:, n:2 * n] + r[:, 2 * n:3 * n]


def _hgrn_gates(hq, hf, lb):
    sq = _sig(hq)
    sg = _sig(hf)
    f = lb + (1.0 - lb) * sg
    return hq * sq, (1.0 - lb) * (1.0 - sg), jnp.log(f), sq, sg, f


def _tri(upper):
    r = lax.broadcasted_iota(jnp.int32, (CH, CH), 0)
    c = lax.broadcasted_iota(jnp.int32, (CH, CH), 1)
    return (c >= r) if upper else (c <= r)


def _lb_from_logits(lg_ref):
    return 1.0 / (1.0 + jnp.exp(lg_ref[1:2, :] - lg_ref[0:1, :]))


def _hgrn_fwd(h3, hf, logits, norm_g, *, t, comm=None):
    nc = t // CH
    nt_dims = (((1,), (1,)), ((), ()))
    tn_dims = (((0,), (0,)), ((), ()))

    def body(h_ref, hf_ref, lg_ref, ng_ref, y_ref, o_ref, st_ref, s_scr, b_scr, qa_s, ka_s, qb_s, kb_s, v_s):
        @pl.when(pl.program_id(0) == 0)
        def _():
            s_scr[...] = jnp.zeros_like(s_scr)

        heads = [slice(h * HG_K, (h + 1) * HG_K) for h in range(HG_HEADS)]
        causal = _tri(False)
        lb = _lb_from_logits(lg_ref)
        for c in range(HG_SUB):
            rows = slice(c * CH, (c + 1) * CH)
            q, k, g, _, _, _ = _hgrn_gates(h_ref[rows, 0:D].astype(F32), hf_ref[rows, :], lb)
            b_scr[...] = _apply01(jnp.where(causal, 1.0, 0.0).astype(BF), g)
            b = b_scr[...]
            b_mid = b_scr[CH // 2 - 1:CH // 2, :]
            b_last = b_scr[CH - 1:CH, :]
            qa_s[...] = (q * jnp.exp(b - b_mid)).astype(BF)
            ka_s[...] = (k * jnp.exp(b_mid - b)).astype(BF)
            qb_s[...] = (q * jnp.exp(b)).astype(BF)
            kb_s[...] = (k * jnp.exp(b_last - b)).astype(BF)
            v_s[...] = h_ref[rows, D:2 * D]
            dec = jnp.exp(b_last)
            st_ref[c] = s_scr[...].astype(BF)
            a = [jnp.where(causal, lax.dot_general(qa_s[:, sl], ka_s[:, sl], nt_dims, preferred_element_type=F32),
                           0.0).astype(BF) for sl in heads]
            for h, sl in enumerate(heads):
                o_ref[rows, sl] = (jnp.dot(a[h], v_s[:, sl], preferred_element_type=F32)
                                   + lax.dot_general(qb_s[:, sl], s_scr[h].astype(BF), nt_dims,
                                                     preferred_element_type=F32))
            for h, sl in enumerate(heads):
                s_scr[h] = dec[:, sl] * s_scr[h] + lax.dot_general(v_s[:, sl], kb_s[:, sl], tn_dims,
                                                                   preferred_element_type=F32)
            for h, sl in enumerate(heads):
                o = o_ref[rows, sl]
                on = o * lax.rsqrt(jnp.mean(o * o, axis=-1, keepdims=True) + EPS)
                gate = _sig(h_ref[rows, 2 * D + h * HG_K:2 * D + (h + 1) * HG_K].astype(F32))
                y_ref[rows, sl] = (on * ng_ref[:, sl] * gate).astype(BF)

    half = lambda: pltpu.VMEM((CH, D), BF)
    blk = HG_SUB * CH
    return _hosted_call(
        body, comm, (h3, hf, logits, norm_g), name="hgrn_fwd", grid=(nc // HG_SUB,),
        in_specs=[pl.BlockSpec((blk, 3 * D), lambda n: (n, 0)),
                  pl.BlockSpec((blk, D), lambda n: (n, 0)),
                  pl.BlockSpec((2, D), lambda n: (0, 0)),
                  pl.BlockSpec((1, D), lambda n: (0, 0))],
        out_specs=[pl.BlockSpec((blk, D), lambda n: (n, 0)),
                   pl.BlockSpec((blk, D), lambda n: (n, 0)),
                   pl.BlockSpec((HG_SUB, HG_HEADS, HG_K, HG_K), lambda n: (n, 0, 0, 0))],
        out_shape=[jax.ShapeDtypeStruct((t, D), BF), jax.ShapeDtypeStruct((t, D), F32),
                   jax.ShapeDtypeStruct((nc, HG_HEADS, HG_K, HG_K), BF)],
        scratch_shapes=[pltpu.VMEM((HG_HEADS, HG_K, HG_K), F32), pltpu.VMEM((CH, D), F32),
                        half(), half(), half(), half(), half()],
        sem=("arbitrary",), nsteps=nc // HG_SUB, step_fn=lambda: pl.program_id(0))


def _hgrn_bwd(h3, hf, logits, norm_g, o_pre, states, dy, *, t, comm=None):
    nc = t // CH
    nt_dims = (((1,), (1,)), ((), ()))
    tn_dims = (((0,), (0,)), ((), ()))

    def body(h_ref, hf_ref, lg_ref, ng_ref, o_ref, st_ref, dy_ref, dh_ref, dlg_ref, dng_ref, ds_scr, dlb_scr,
             b_scr, tail_s, e_qa, e_ka, e_qb, e_kb, q_s, k_s, dqa_s, dka_s, dqb_s, dkb_s,
             qa_s, ka_s, qb_s, kb_s, v_s, do_s):
        n = pl.program_id(0)

        @pl.when(n == 0)
        def _():
            ds_scr[...] = jnp.zeros_like(ds_scr)
            dlb_scr[...] = jnp.zeros_like(dlb_scr)
            dng_ref[...] = jnp.zeros_like(dng_ref)

        heads = [slice(h * HG_K, (h + 1) * HG_K) for h in range(HG_HEADS)]
        lb = _lb_from_logits(lg_ref)
        causal = _tri(False)

        def chunk(c):
            rows = slice(c * CH, (c + 1) * CH)
            hq = h_ref[rows, 0:D].astype(F32)
            q, k, g, sq, sg, f = _hgrn_gates(hq, hf_ref[rows, :], lb)
            b_scr[...] = _apply01(jnp.where(causal, 1.0, 0.0).astype(BF), g)
            b = b_scr[...]
            b_mid = b_scr[CH // 2 - 1:CH // 2, :]
            b_last = b_scr[CH - 1:CH, :]
            q_s[...] = q
            k_s[...] = k
            for e_ref, s_ref, base, expo in ((e_qa, qa_s, q, b - b_mid), (e_ka, ka_s, k, b_mid - b),
                                             (e_qb, qb_s, q, b), (e_kb, kb_s, k, b_last - b)):
                e = jnp.exp(expo)
                e_ref[...] = e
                s_ref[...] = (base * e).astype(BF)
            v_s[...] = h_ref[rows, D:2 * D]
            dec = jnp.exp(b_last)
            for h, sl in enumerate(heads):
                gcol = slice(3 * D + h * HG_K, 3 * D + (h + 1) * HG_K)
                ngh = ng_ref[:, sl]
                sgate = _sig(h_ref[rows, 2 * D + h * HG_K:2 * D + (h + 1) * HG_K].astype(F32))
                o = o_ref[rows, sl]
                r = lax.rsqrt(jnp.mean(o * o, axis=-1, keepdims=True) + EPS)
                on = o * r
                dyh = dy_ref[rows, sl]
                dh_ref[rows, gcol] = (dyh * on * ngh * sgate * (1.0 - sgate)).astype(BF)
                dng_ref[:, sl] += jnp.sum(dyh * on * sgate, axis=0, keepdims=True)
                don = dyh * ngh * sgate
                do_s[:, sl] = (r * (don - on * jnp.mean(don * on, axis=-1, keepdims=True))).astype(BF)
            a = [jnp.where(causal, lax.dot_general(qa_s[:, sl], ka_s[:, sl], nt_dims, preferred_element_type=F32),
                           0.0).astype(BF) for sl in heads]
            da = [jnp.where(causal, lax.dot_general(do_s[:, sl], v_s[:, sl], nt_dims, preferred_element_type=F32),
                            0.0).astype(BF) for sl in heads]
            for h, sl in enumerate(heads):
                dh_ref[rows, 2 * D + h * HG_K:2 * D + (h + 1) * HG_K] = (
                    lax.dot_general(a[h], do_s[:, sl], tn_dims, preferred_element_type=F32)
                    + lax.dot_general(kb_s[:, sl], ds_scr[h].astype(BF), nt_dims, preferred_element_type=F32)
                ).astype(BF)
            for h, sl in enumerate(heads):
                dqa_s[:, sl] = jnp.dot(da[h], ka_s[:, sl], preferred_element_type=F32)
            for h, sl in enumerate(heads):
                dka_s[:, sl] = lax.dot_general(da[h], qa_s[:, sl], tn_dims, preferred_element_type=F32)
            for h, sl in enumerate(heads):
                dqb_s[:, sl] = jnp.dot(do_s[:, sl], st_ref[c, h], preferred_element_type=F32)
            for h, sl in enumerate(heads):
                dkb_s[:, sl] = jnp.dot(v_s[:, sl], ds_scr[h].astype(BF), preferred_element_type=F32)
            for h, sl in enumerate(heads):
                tail_s[:, sl] = jnp.sum(dec[:, sl] * st_ref[c, h].astype(F32) * ds_scr[h], axis=0, keepdims=True)
            for h, sl in enumerate(heads):
                ds_scr[h] = (lax.dot_general(do_s[:, sl], qb_s[:, sl], tn_dims, preferred_element_type=F32)
                             + dec[:, sl] * ds_scr[h])
            qv, kv = q_s[...], k_s[...]
            dqa, dka, dqb, dkb = dqa_s[...], dka_s[...], dqb_s[...], dkb_s[...]
            eqa, eka, eqb, ekb = e_qa[...], e_ka[...], e_qb[...], e_kb[...]
            dkb_kb = dkb * (kv * ekb)
            db_last = jnp.sum(dkb_kb, axis=0, keepdims=True) + tail_s[...]
            last_row = lax.broadcasted_iota(jnp.int32, (CH, D), 0) == CH - 1
            db = (dqa * (qv * eqa) - dka * (kv * eka) + dqb * (qv * eqb) - dkb_kb
                  + jnp.where(last_row, db_last, 0.0))
            dg = _apply01(jnp.where(_tri(True), 1.0, 0.0).astype(BF), db)
            dq = dqa * eqa + dqb * eqb
            dk = dka * eka + dkb * ekb
            dh_ref[rows, 0:D] = (dq * sq * (1.0 + hq * (1.0 - sq))).astype(BF)
            dfk = dg / f - dk
            dh_ref[rows, D:2 * D] = ((1.0 - lb) * dfk * sg * (1.0 - sg)).astype(BF)
            dlb_scr[...] += jnp.sum((1.0 - sg) * dfk, axis=0, keepdims=True)

        for c in reversed(range(HG_SUB_BWD)):
            chunk(c)

        @pl.when(n == nc // HG_SUB_BWD - 1)
        def _():
            dl0 = dlb_scr[...] * lb * (1.0 - lb)
            dlg_ref[0:1, :] = dl0
            dlg_ref[1:2, :] = -dl0

    steps = nc // HG_SUB_BWD
    blk = HG_SUB_BWD * CH
    rev = lambda n: (steps - 1 - n, 0)
    return _hosted_call(
        body, comm, (h3, hf, logits, norm_g, o_pre, states, dy), name="hgrn_bwd", grid=(steps,),
        in_specs=[pl.BlockSpec((blk, 3 * D), rev),
                  pl.BlockSpec((blk, D), rev),
                  pl.BlockSpec((2, D), lambda n: (0, 0)),
                  pl.BlockSpec((1, D), lambda n: (0, 0)),
                  pl.BlockSpec((blk, D), rev),
                  pl.BlockSpec((HG_SUB_BWD, HG_HEADS, HG_K, HG_K), lambda n: (steps - 1 - n, 0, 0, 0)),
                  pl.BlockSpec((blk, D), rev)],
        out_specs=[pl.BlockSpec((blk, 4 * D), rev),
                   pl.BlockSpec((2, D), lambda n: (0, 0)),
                   pl.BlockSpec((1, D), lambda n: (0, 0))],
        out_shape=[jax.ShapeDtypeStruct((t, 4 * D), BF), jax.ShapeDtypeStruct((2, D), F32),
                   jax.ShapeDtypeStruct((1, D), F32)],
        scratch_shapes=([pltpu.VMEM((HG_HEADS, HG_K, HG_K), F32), pltpu.VMEM((1, D), F32),
                         pltpu.VMEM((CH, D), F32), pltpu.VMEM((1, D), F32)]
                        + [pltpu.VMEM((CH, D), F32)] * 10 + [pltpu.VMEM((CH, D), BF)] * 6),
        sem=("arbitrary",), nsteps=steps, step_fn=lambda: pl.program_id(0))


def _place():
    x, y, c = lax.axis_index("x"), lax.axis_index("y"), lax.axis_index("c")
    return x, y, c, [(1 - x, y), (x, 1 - y), (1 - x, 1 - y)]


def _gather_comm(shards, mids):
    n, pieces = len(shards), len(mids)
    r = [s.shape[0] for s in shards]
    tile = 16
    cut = [[(rw // tile * p // pieces) * tile for p in range(pieces + 1)] for rw in r]
    size = [[cut[w][p + 1] - cut[w][p] for p in range(pieces)] for w in range(n)]

    def tools(ins, outs, sems):
        send_sems, recv_sems, local_sems = sems
        x, y, c, _ = _place()
        me, sib = (x, y, c), (x, y, 1 - c)
        near = [(x ^ c, y ^ (1 - c), c), (x ^ (1 - c), y ^ c, c), (1 - x, 1 - y, c)]

        def rows(w, p, dev):
            return outs[w].at[pl.ds((4 * dev[0] + 2 * dev[1] + dev[2]) * r[w] + cut[w][p], size[w][p]), :]

        def copy(kind, w, p, block, to, own=False):
            src = ins[w].at[pl.ds(cut[w][p], size[w][p]), :] if own else rows(w, p, block)
            return pltpu.make_async_remote_copy(
                src_ref=src, dst_ref=rows(w, p, block), send_sem=send_sems.at[p, kind],
                recv_sem=recv_sems.at[p, kind], device_id=to, device_id_type=MESH)

        def all_of(kind, p):
            whole = outs[0].at[pl.ds(0, sum(size[w][p] for w in range(n))), :]
            return pltpu.make_async_remote_copy(
                src_ref=whole, dst_ref=whole, send_sem=send_sems.at[p, kind], recv_sem=recv_sems.at[p, kind],
                device_id=me, device_id_type=MESH)

        mine = [pltpu.make_async_copy(ins[w], outs[w].at[pl.ds((4 * x + 2 * y + c) * r[w], r[w]), :],
                                      local_sems.at[w]) for w in range(n)]
        return near, me, sib, copy, all_of, mine

    def start(ins, outs, sems):
        near, me, sib, copy, _, mine = tools(ins, outs, sems)
        for p in range(pieces):
            for w in range(n):
                copy(1, w, p, me, near[0], own=True).start()
                copy(2, w, p, me, near[1], own=True).start()
        for p in range(pieces):
            for w in range(n):
                copy(0, w, p, me, sib, own=True).start()
        for cp in mine:
            cp.start(priority=1)

    def pass_diagonal(p, near, sib, copy, all_of):
        all_of(3, p).wait_recv()
        for w in range(n):
            copy(6, w, p, near[2], sib).start()

    def pass_on(p):
        def phase(ins, outs, sems):
            near, _, sib, copy, all_of, _ = tools(ins, outs, sems)
            all_of(1, p).wait_recv()
            for w in range(n):
                copy(3, w, p, near[0], near[1]).start()
            for w in range(n):
                copy(4, w, p, near[0], sib).start()
            all_of(2, p).wait_recv()
            for w in range(n):
                copy(5, w, p, near[1], sib).start()
            if p > 0:
                pass_diagonal(p - 1, near, sib, copy, all_of)
        return phase

    def finish(ins, outs, sems):
        near, _, sib, copy, all_of, mine = tools(ins, outs, sems)
        pass_diagonal(pieces - 1, near, sib, copy, all_of)
        for p in range(pieces):
            all_of(0, p).wait_recv()
            for kind in (4, 5, 6):
                all_of(kind, p).wait_recv()
            for kind in range(7):
                all_of(kind, p).wait_send()
        for cp in mine:
            cp.wait()

    return _Comm(shards, [jax.ShapeDtypeStruct((N_DEV * rw, D), BF) for rw in r],
                 [pltpu.SemaphoreType.DMA((pieces, 7)), pltpu.SemaphoreType.DMA((pieces, 7)),
                  pltpu.SemaphoreType.DMA((n,))],
                 [(0.0, start)] + [(f, pass_on(p)) for p, f in enumerate(mids)] + [(1.0, finish)])


def _pair_comm(grads):
    n = len(grads)
    r = [g.shape[0] // N_DEV for g in grads]

    def start(ins, outs, sems):
        send_sems, recv_sems = sems
        x, y, c, _ = _place()
        for w in range(n):
            for a in range(N_CHIP):
                pltpu.make_async_remote_copy(
                    src_ref=ins[w].at[pl.ds((2 * a + 1 - c) * r[w], r[w]), :], dst_ref=outs[w].at[a],
                    send_sem=send_sems.at[w], recv_sem=recv_sems.at[w],
                    device_id=(x, y, 1 - c), device_id_type=MESH).start()

    def finish(ins, outs, sems):
        send_sems, recv_sems = sems
        x, y, c, _ = _place()
        for w in range(n):
            pltpu.make_async_remote_copy(
                src_ref=outs[w], dst_ref=outs[w], send_sem=send_sems.at[w], recv_sem=recv_sems.at[w],
                device_id=(x, y, c), device_id_type=MESH).wait()

    return _Comm(grads, [jax.ShapeDtypeStruct((N_CHIP, rw, D), BF) for rw in r],
                 [pltpu.SemaphoreType.DMA((n,)), pltpu.SemaphoreType.DMA((n,))],
                 [(0.0, start), (1.0, finish)])


def _pair_add(grads, gots, core, *, name):
    n, r = len(grads), gots[0].shape[1]
    tr = r if r <= 128 else r // 2
    steps = r // tr
    tile = lambda k: (lambda s: jnp.clip(s - k * steps, 0, steps - 1))

    def body(c_ref, *refs):
        g_refs, got_refs, o_refs = refs[:n], refs[n:2 * n], refs[2 * n:]
        s = pl.program_id(0)
        for k in range(n):
            @pl.when(jnp.logical_and(s >= k * steps, s < (k + 1) * steps))
            def _(k=k):
                o_refs[k][...] = (g_refs[k][:, 0].astype(F32) + got_refs[k][...].astype(F32)).astype(BF)

    grid_spec = pltpu.PrefetchScalarGridSpec(
        num_scalar_prefetch=1, grid=(n * steps,),
        in_specs=[pl.BlockSpec((N_CHIP, 1, tr, D), lambda s, c_ref, k=k: (0, c_ref[0], tile(k)(s), 0))
                  for k in range(n)]
        + [pl.BlockSpec((N_CHIP, tr, D), lambda s, c_ref, k=k: (0, tile(k)(s), 0)) for k in range(n)],
        out_specs=[pl.BlockSpec((N_CHIP, tr, D), lambda s, c_ref, k=k: (0, tile(k)(s), 0)) for k in range(n)])
    return _pcall(body, name=name, grid_spec=grid_spec,
                  out_shape=[jax.ShapeDtypeStruct((N_CHIP, r, D), BF)] * n,
                  compiler_params=_cp(("arbitrary",)))(
                      core, *[g.reshape(N_CHIP, 2, r, D) for g in grads], *gots)


def _chip_comm(pair_sums):
    n = len(pair_sums)
    r = [p.shape[1] for p in pair_sums]
    off = [sum(r[:w]) for w in range(n)]

    def tools(ins, outs, sems):
        send_sems, recv_sems, local_sems = sems
        x, y, c, chips = _place()
        my_chip = 2 * x + y

        def slot(w):
            return outs[0].at[my_chip, pl.ds(off[w], r[w]), :]

        own = [pltpu.make_async_copy(ins[w].at[my_chip], slot(w), local_sems.at[w]) for w in range(n)]
        return x, y, c, chips, my_chip, slot, own, send_sems, recv_sems

    def start(ins, outs, sems):
        x, y, c, chips, my_chip, slot, own, send_sems, recv_sems = tools(ins, outs, sems)
        for j, chip in enumerate(chips):
            for w in range(n):
                pltpu.make_async_remote_copy(
                    src_ref=ins[w].at[2 * chip[0] + chip[1]], dst_ref=slot(w), send_sem=send_sems.at[j],
                    recv_sem=recv_sems.at[j], device_id=(*chip, c), device_id_type=MESH).start()
        for cp in own:
            cp.start(priority=1)

    def finish(ins, outs, sems):
        x, y, c, chips, my_chip, slot, own, send_sems, recv_sems = tools(ins, outs, sems)
        whole = outs[0].at[my_chip]
        for j in range(3):
            pltpu.make_async_remote_copy(
                src_ref=whole, dst_ref=whole, send_sem=send_sems.at[j], recv_sem=recv_sems.at[j],
                device_id=(x, y, c), device_id_type=MESH).wait()
        for cp in own:
            cp.wait()

    return _Comm(pair_sums, [jax.ShapeDtypeStruct((N_CHIP, sum(r), D), BF)],
                 [pltpu.SemaphoreType.DMA((3,)), pltpu.SemaphoreType.DMA((3,)), pltpu.SemaphoreType.DMA((n,))],
                 [(0.0, start), (1.0, finish)])


def _adam_math(w, g, m, v):
    m = ADAM_B1 * m + (1.0 - ADAM_B1) * g
    v = ADAM_B2 * v + (1.0 - ADAM_B2) * (g * g)
    m_hat = m / (1.0 - ADAM_B1 ** ADAM_STEP)
    v_hat = v / (1.0 - ADAM_B2 ** ADAM_STEP)
    delta = -ADAM_LR * (m_hat / (jnp.sqrt(v_hat) + ADAM_EPS) + ADAM_WD * w)
    return delta, m, v


SMALL = (("norm_mix_g", (1, D), 0), ("hgrn_norm_g", (1, D), 1), ("norm_ffn_g", (1, D), 2),
         ("norm_final_g", (1, D), 3), ("hgrn_lb_logits", (2, D), 4), ("attn_sinks", (1, 16), 6),
         ("b_in", (1, IN_W), 8))
LOSS_ROW = 7


def _small_allreduce_adam(grads, loss_row, params):
    n = len(SMALL)

    def rows_of(ref, shape, row):
        r, w = shape
        if w <= D:
            return ref[row:row + r, 0:w]
        pieces = [ref[row + k:row + k + 1, :] for k in range(-(-w // D))]
        return jnp.concatenate(pieces, axis=1)[:, 0:w]

    def body(*refs):
        g_refs, loss_ref = refs[:n], refs[n]
        wmv = refs[n + 1:4 * n + 1]
        loss_out = refs[4 * n + 1]
        outs = refs[4 * n + 2:8 * n + 2]
        mine, total, gath, send_sems, recv_sems = refs[8 * n + 2:]
        x, y, c, _ = _place()
        me = 4 * x + 2 * y + c
        mine[...] = jnp.zeros_like(mine)
        for g_ref, (_, (r, w), row) in zip(g_refs, SMALL):
            for k in range(-(-w // D)):
                wk = min(D, w - k * D)
                mine[row + k:row + k + r, 0:wk] = g_ref[:, k * D:k * D + wk]
        mine[LOSS_ROW:LOSS_ROW + 1, 0:128] = loss_ref[...]
        gath[me] = mine[...]
        cps = []
        for d in range(1, N_DEV):
            peer = (x ^ (d >> 2), y ^ ((d >> 1) & 1), c ^ (d & 1))
            cps.append(pltpu.make_async_remote_copy(
                src_ref=mine, dst_ref=gath.at[me], send_sem=send_sems.at[d - 1],
                recv_sem=recv_sems.at[d - 1], device_id=peer, device_id_type=MESH))
        for cp in cps:
            cp.start()
        for cp in cps:
            cp.wait()
        g = gath[0]
        for k in range(1, N_DEV):
            g = g + gath[k]
        total[...] = g
        loss_out[...] = total[LOSS_ROW:LOSS_ROW + 1, 0:128]
        for i, (_, shape, row) in enumerate(SMALL):
            gi = rows_of(total, shape, row)
            w_ref, m_ref, v_ref = wmv[3 * i:3 * i + 3]
            o = outs[4 * i:4 * i + 4]
            o[0][...] = gi
            o[1][...], o[2][...], o[3][...] = _adam_math(w_ref[...], gi, m_ref[...], v_ref[...])

    vm = pl.BlockSpec(memory_space=pltpu.VMEM)
    ins = [grads[name] for name, _, _ in SMALL] + [loss_row]
    for name, _, _ in SMALL:
        ins += list(params[name])
    out_shape = [jax.ShapeDtypeStruct((1, 128), F32)]
    for _, shape, _ in SMALL:
        out_shape += [jax.ShapeDtypeStruct(shape, F32)] * 4
    res = _pcall(body, name="small_allreduce_adam", in_specs=[vm] * len(ins), out_specs=[vm] * len(out_shape),
                 out_shape=out_shape,
                 scratch_shapes=[pltpu.VMEM((SMALL_ROWS, D), F32), pltpu.VMEM((SMALL_ROWS, D), F32),
                                 pltpu.VMEM((N_DEV, SMALL_ROWS, D), F32),
                                 pltpu.SemaphoreType.DMA((N_DEV - 1,)), pltpu.SemaphoreType.DMA((N_DEV - 1,))],
                 compiler_params=pltpu.CompilerParams(has_side_effects=True))(*ins)
    return res[0], {name: res[1 + 4 * i:5 + 4 * i] for i, (name, _, _) in enumerate(SMALL)}


def _adam(ws, parts, ms, vs, *, name):
    n, rows = len(ws), ws[0].shape[0]
    tr = rows if rows <= 128 else rows // 2
    steps = rows // tr
    tile = lambda k: (lambda s: jnp.clip(s - k * steps, 0, steps - 1))

    def body(*refs):
        w_refs, m_refs, v_refs, p_ref = refs[:n], refs[n:2 * n], refs[2 * n:3 * n], refs[3 * n]
        o_refs = refs[3 * n + 1:]
        s = pl.program_id(0)
        for k in range(n):
            @pl.when(jnp.logical_and(s >= k * steps, s < (k + 1) * steps))
            def _(k=k):
                g = p_ref[0].astype(F32)
                for a in range(1, N_CHIP):
                    g = g + p_ref[a].astype(F32)
                o = o_refs[4 * k:4 * k + 4]
                o[0][...] = g
                o[1][...], o[2][...], o[3][...] = _adam_math(w_refs[k][...], g, m_refs[k][...], v_refs[k][...])

    spec = lambda k: pl.BlockSpec((tr, D), lambda s, k=k: (tile(k)(s), 0))
    res = _pcall(body, name=name, grid=(n * steps,),
                 in_specs=[spec(k) for k in range(n)] * 3 + [pl.BlockSpec((N_CHIP, tr, D), lambda s: (0, s, 0))],
                 out_specs=[spec(k) for k in range(n) for _ in range(4)],
                 out_shape=[jax.ShapeDtypeStruct((rows, D), F32)] * (4 * n),
                 compiler_params=_cp(("arbitrary",)))(*ws, *ms, *vs, parts)
    return [res[4 * k:4 * k + 4] for k in range(n)]


def _step(x, tgt, shards, norm_mix_g, b_in, sinks, logits, hgrn_norm_g, norm_ffn_g, norm_final_g):
    t = x.shape[0]
    core = lax.axis_index("c").astype(jnp.int32).reshape(1)

    u1, (win_t,) = _rms_fwd(x, norm_mix_g, tm=512, name="rms_mix", comm=_gather_comm(shards[0:1], (0.2, 0.4, 0.6, 0.8)))
    (q, kv, h3, hf, gates), (wg_t, wba, wbh, wout) = _inproj_fwd(
        u1, win_t, b_in, t=t, comm=_gather_comm([shards[1]] + shards[4:7], (0.3, 0.5, 0.7, 0.9)))
    (y_attn,), _ = _attn_fwd(q, kv, sinks, t=t)
    (y_hgrn, o_pre, states), (wu_t, wd) = _hgrn_fwd(h3, hf, logits, hgrn_norm_g, t=t,
                                                    comm=_gather_comm(shards[2:4], (0.3, 0.5, 0.7, 0.9)))
    col = lambda j: j
    first, second = (lambda j: 0), (lambda j: 1)
    gate_tiles = [(gates, D, first), (gates, D, second)]

    def merge(prods, ex):
        (ya_, yb_), (ga, gb) = prods, ex
        sa, sb = _sig(ga.astype(F32)), _sig(gb.astype(F32))
        return sa, sb, ya_ * sa * (1.0 - sa), yb_ * sb * (1.0 - sb), sa * ya_ + sb * yb_

    sig_a, sig_b, dgate_a, dgate_b, merged = _fmm(
        [y_attn, y_hgrn], [(0, wba, False), (1, wbh, False)], gate_tiles, merge,
        [(BF, D, D, first)] * 5, m=t, n=D, tm=512, tn=D, name="branch_merge")
    def resid_norm(prods, ex):
        (p,), (xv, gv) = prods, ex
        hv = xv + p
        return hv, hv * lax.rsqrt(jnp.mean(hv * hv, axis=-1, keepdims=True) + EPS) * gv

    h1, u2 = _fmm([merged], [(0, wout, False)], [(x, D, first)], resid_norm, [(F32, D, D, first), (BF, D, D, first)],
                  m=t, n=D, tm=1024, tn=D, name="out_proj", vecs=[norm_ffn_g])

    def swiglu(prods, ex):
        g_, u_ = prods
        s = _sig(g_)
        silu = g_ * s
        return u_ * s * (1.0 + g_ * (1.0 - s)), silu, silu * u_

    dz_dgate, dz_dup, z = _fmm([u2], [(0, wg_t, True), (0, wu_t, True)], [], swiglu,
                               [(BF, FFN, FFN // 2, col)] * 3, m=t, n=FFN, tm=1024, tn=FFN // 2,
                               name="ffn_gate_up", cols_outer=True)
    def loss_head(prods, ex):
        (p,), (hv, tv, gv) = prods, ex
        hv = hv + p
        r = lax.rsqrt(jnp.mean(hv * hv, axis=-1, keepdims=True) + EPS)
        xh = hv * r
        err = xh * gv - tv
        lp = jnp.sum(jnp.sum(err * err, axis=1, keepdims=True), axis=0, keepdims=True) * (0.5 / D)
        dy = err * (1.0 / D)
        dxh = dy * gv
        dh = r * (dxh - xh * jnp.mean(dxh * xh, axis=-1, keepdims=True))
        return dh, dh, jnp.sum(dy * xh, axis=0, keepdims=True), jnp.broadcast_to(lp, (1, 128))

    dh2, dh2_b, d_norm_final, loss_row = _fmm(
        [z], [(0, wd, False)], [(h1, D, first), (tgt, D, first)], loss_head, [(F32, D, D, first), (BF, D, D, first)],
        m=t, n=D, tm=512, tn=D, name="ffn_down_loss", vecs=[norm_final_g], sums=[D, 128])

    def swiglu_bwd(prods, ex):
        (dz,), (da_, db_) = prods, ex
        return dz * da_.astype(F32), dz * db_.astype(F32)

    ffn_tiles = [(dz_dgate, FFN // 2, col), (dz_dup, FFN // 2, col)]
    dgt, dup = _fmm([dh2_b], [(0, wd, True)], ffn_tiles, swiglu_bwd, [(BF, FFN, FFN // 2, col)] * 2,
                    m=t, n=FFN, tm=1024, tn=FFN // 2, name="d_gate_up", cols_outer=True)
    (d_wd,) = _wgrad([z], dh2_b, name="d_w_down")
    (du2,) = _fmm([dgt, dup], [(0, wg_t, False), (1, wu_t, False)], [], lambda prods, ex: (prods[0] + prods[1],),
                  [(F32, D, 512, col)], m=t, n=D, tm=1024, tn=512, name="d_u2")
    d_wg, d_wu = _wgrad([dgt, dup], u2, name="d_w_gate_up")
    dh1, dh1_b, d_norm_ffn = _rms_bwd(du2, h1, norm_ffn_g, dh2, tm=512, name="rms_ffn_bwd")
    (d_wout,) = _wgrad([merged], dh1_b, name="d_w_out")

    def merge_bwd(prods, ex):
        (dm,), (sa, sb, ca, cb, wa, wb) = prods, ex
        dgate = jnp.concatenate([dm * ca.astype(F32), dm * cb.astype(F32)], axis=1)
        dya_ = (dm * sa.astype(F32)).astype(BF)
        dyb_ = (dm * sb.astype(F32)).astype(BF)
        return (dya_, dyb_, dgate, lax.dot_general(dya_, wa, _NT, preferred_element_type=F32),
                lax.dot_general(dyb_, wb, _NT, preferred_element_type=F32))

    ffn_grads = (d_wg, d_wu, d_wd)
    (dya, dyb, dgates, dy_attn, dy_hgrn), got = _fmm(
        [dh1_b], [(0, wout, True)], [(a, D, first) for a in (sig_a, sig_b, dgate_a, dgate_b)], merge_bwd,
        [(BF, D, D, first), (BF, D, D, first), (BF, 2 * D, 2 * D, first), (BF, D, D, first), (F32, D, D, first)],
        m=t, n=D, tm=512, tn=D, name="d_merge", consts=[wba, wbh], comm=_pair_comm(ffn_grads))
    pair_ffn = _pair_add(ffn_grads, got, core, name="pair_add_ffn")
    (d_wba,) = _wgrad([y_attn], dya, name="d_w_ba")
    (d_wbh,) = _wgrad([y_hgrn], dyb, name="d_w_bh")
    sq_grads = (d_wba, d_wbh, d_wout)
    (dh4, d_logits, d_hgrn_norm), (parts_ffn, *got) = _hgrn_bwd(
        h3, hf, logits, hgrn_norm_g, o_pre, states, dy_hgrn, t=t,
        comm=_both(_chip_comm(pair_ffn), _pair_comm(sq_grads)))
    pair_sq = _pair_add(sq_grads, got, core, name="pair_add_sq")
    (dq, dkv, d_sinks), (parts_sq,) = _attn_bwd(q, kv, sinks, dy_attn, t=t, comm=_chip_comm(pair_sq))
    dps = (dq, dkv, dh4, dgates)
    d_win_t, d_b_in = _inproj_bwd_w(dps, u1, t=t)
    half0, got_in = _inproj_bwd_x(dps, win_t, x, norm_mix_g, dh1, t=t, part=0, comm=_pair_comm([d_win_t]))
    pair_in = _pair_add([d_win_t], got_in, core, name="pair_add_w_in")
    (grad_x, d_norm_mix), (parts_in,) = _inproj_bwd_x(dps, win_t, x, norm_mix_g, dh1, t=t, part=1, prev=half0,
                                                      comm=_chip_comm(pair_in))

    small_grads = (d_norm_mix, d_b_in, d_sinks, d_logits, d_hgrn_norm, d_norm_ffn, d_norm_final)
    return loss_row, grad_x, (parts_in, parts_ffn, parts_sq), small_grads


def kernel(x, norm_mix_g, w_in, b_in, attn_sinks, hgrn_lb_logits, hgrn_norm_g, w_branch_attn, w_branch_hgrn, w_out, norm_ffn_g, w_ffn_gate, w_ffn_up, w_ffn_down, norm_final_g, loss_target, m_norm_mix_g, m_w_in, m_b_in, m_attn_sinks, m_hgrn_lb_logits, m_hgrn_norm_g, m_w_branch_attn, m_w_branch_hgrn, m_w_out, m_norm_ffn_g, m_w_ffn_gate, m_w_ffn_up, m_w_ffn_down, m_norm_final_g, v_norm_mix_g, v_w_in, v_b_in, v_attn_sinks, v_hgrn_lb_logits, v_hgrn_norm_g, v_w_branch_attn, v_w_branch_hgrn, v_w_out, v_norm_ffn_g, v_w_ffn_gate, v_w_ffn_up, v_w_ffn_down, v_norm_final_g):
    shards = [w_in[0].T.astype(BF), w_ffn_gate[0].T.astype(BF), w_ffn_up[0].T.astype(BF),
              w_ffn_down[0].astype(BF), w_branch_attn[0].astype(BF), w_branch_hgrn[0].astype(BF),
              w_out[0].astype(BF)]
    loss_row, grad_x, grad_parts, small_grads = _step(
        x[0], loss_target[0], shards, norm_mix_g, b_in, attn_sinks, hgrn_lb_logits, hgrn_norm_g,
        norm_ffn_g, norm_final_g.reshape(1, D))

    d_norm_mix, d_b_in, d_sinks, d_logits, d_hgrn_norm, d_norm_ffn, d_norm_final = small_grads
    row = lambda a: a.reshape(1, D)
    loss_out, small = _small_allreduce_adam(
        dict(norm_mix_g=d_norm_mix, hgrn_norm_g=d_hgrn_norm, norm_ffn_g=d_norm_ffn, norm_final_g=d_norm_final,
             hgrn_lb_logits=d_logits, attn_sinks=d_sinks, b_in=d_b_in),
        loss_row,
        dict(norm_mix_g=(norm_mix_g, m_norm_mix_g, v_norm_mix_g), hgrn_norm_g=(hgrn_norm_g, m_hgrn_norm_g, v_hgrn_norm_g),
             norm_ffn_g=(norm_ffn_g, m_norm_ffn_g, v_norm_ffn_g),
             norm_final_g=(row(norm_final_g), row(m_norm_final_g), row(v_norm_final_g)),
             hgrn_lb_logits=(hgrn_lb_logits, m_hgrn_lb_logits, v_hgrn_lb_logits),
             attn_sinks=(attn_sinks, m_attn_sinks, v_attn_sinks), b_in=(b_in, m_b_in, v_b_in)))
    small["norm_final_g"] = [a.reshape(D) for a in small["norm_final_g"]]
    loss = loss_out[0, 0]

    names = ["w_in", "w_ffn_gate", "w_ffn_up", "w_ffn_down", "w_branch_attn", "w_branch_hgrn", "w_out"]
    w_full = dict(w_in=(w_in, m_w_in, v_w_in), w_ffn_gate=(w_ffn_gate, m_w_ffn_gate, v_w_ffn_gate),
                  w_ffn_up=(w_ffn_up, m_w_ffn_up, v_w_ffn_up), w_ffn_down=(w_ffn_down, m_w_ffn_down, v_w_ffn_down),
                  w_branch_attn=(w_branch_attn, m_w_branch_attn, v_w_branch_attn),
                  w_branch_hgrn=(w_branch_hgrn, m_w_branch_hgrn, v_w_branch_hgrn),
                  w_out=(w_out, m_w_out, v_w_out))
    big = {}
    for group, parts, tag in zip((names[0:1], names[1:4], names[4:7]), grad_parts, ("w_in", "ffn", "square")):
        flip = [name in names[0:3] for name in group]
        view = lambda a, f: a[0].T if f else a[0]
        cols = [[view(w_full[name][j], f) for name, f in zip(group, flip)] for j in range(3)]
        res = _adam(cols[0], parts, cols[1], cols[2], name="adam_" + tag)
        for name, f, r in zip(group, flip, res):
            big[name] = [a.T[None] if f else a[None] for a in r]

    order = ["norm_mix_g", "w_in", "b_in", "attn_sinks", "hgrn_lb_logits", "hgrn_norm_g", "w_branch_attn",
             "w_branch_hgrn", "w_out", "norm_ffn_g", "w_ffn_gate", "w_ffn_up", "w_ffn_down", "norm_final_g"]
    outs = [loss, grad_x[None]]
    for kind in range(4):
        for name in order:
            outs.append(big[name][kind] if name in big else small[name][kind])
    return tuple(outs)
```
